```python
import math
import jax
import jax.numpy as jnp
from jax import lax
import numpy as np

D_MODEL = 1024
BATCH = 16
SEQ = 2048
DEPTH = 1

N_META = 16
GDN_HEADS = 4
GDN_HEAD_DIM = 128
GDN_WIDTH = GDN_HEADS * GDN_HEAD_DIM
GDN_CONV = 4
CHUNK = 64
SC_WIDTH = D_MODEL - GDN_WIDTH
SC_GROUPS = 8
SC_CONV = 3
MIX_WIDTH = GDN_WIDTH + SC_WIDTH
D_FF = -(-8 * D_MODEL // (3 * 256)) * 256
IN_SPLITS = (GDN_WIDTH, GDN_WIDTH, GDN_WIDTH, GDN_WIDTH, GDN_HEADS, GDN_HEADS, SC_WIDTH, SC_WIDTH, SC_WIDTH)
IN_WIDTH = sum(IN_SPLITS)
EPS = 1e-6

kernel_name = 'hymba_gdn_shortconv_block'


def rms_norm(x, w):
    xf = x.astype(jnp.float32)
    y = xf * lax.rsqrt(jnp.mean(xf * xf, axis=-1, keepdims=True) + EPS)
    return (y * w.astype(jnp.float32)).astype(x.dtype)


def l2_normalize(x):
    xf = x.astype(jnp.float32)
    return xf * lax.rsqrt(jnp.sum(xf * xf, axis=-1, keepdims=True) + EPS)


def causal_depthwise_conv(x, w):
    k_width = w.shape[0]
    seq_len = x.shape[1]
    xp = jnp.pad(x, ((0, 0), (k_width - 1, 0), (0, 0)))
    return sum(xp[:, i:i + seq_len] * w[i].astype(x.dtype) for i in range(k_width))


def chunked_gated_delta_rule(q, k, v, g, beta):
    b, seq_len, n_heads, dk = q.shape
    dv = v.shape[-1]
    pad = (-seq_len) % CHUNK
    f32 = jnp.float32

    def to_chunks(t):
        t = jnp.pad(t.astype(f32), ((0, 0), (pad, 0)) + ((0, 0),) * (t.ndim - 2))
        n = t.shape[1] // CHUNK
        t = t.reshape((b, n, CHUNK) + t.shape[2:])
        return jnp.moveaxis(t, 3, 1)

    qc, kc, vc, g_raw, bc = (to_chunks(t) for t in (q, k, v, g, beta))
    gc = jnp.cumsum(g_raw, axis=-1)
    idx = jnp.arange(CHUNK)
    incl = idx[:, None] >= idx[None, :]
    strict = idx[:, None] > idx[None, :]
    diff = gc[..., :, None] - gc[..., None, :]
    decay = jnp.where(incl, jnp.exp(jnp.where(incl, diff, 0.0)), 0.0)
    kb = kc * bc[..., None]
    a = jnp.where(strict, jnp.einsum('bhnid,bhnjd->bhnij', kb, kc) * decay, 0.0)
    eye = jnp.eye(CHUNK, dtype=f32)
    t_inv = lax.linalg.triangular_solve(a + eye, jnp.broadcast_to(eye, a.shape), left_side=True, lower=True)
    u = jnp.einsum('bhnij,bhnje->bhnie', t_inv, vc * bc[..., None])
    w = jnp.einsum('bhnij,bhnjd->bhnid', t_inv, kb * jnp.exp(gc)[..., None])
    qk = jnp.where(incl, jnp.einsum('bhnid,bhnjd->bhnij', qc, kc) * decay, 0.0)
    q_dec = qc * jnp.exp(gc)[..., None]
    k_dec = kc * jnp.exp(gc[..., -1:] - gc)[..., None]
    g_last = jnp.exp(gc[..., -1])

    def step(state, xs):
        q_i, k_i, u_i, w_i, qk_i, gl_i = xs
        v_new = u_i - jnp.einsum('bhcd,bhde->bhce', w_i, state)
        o_i = jnp.einsum('bhcd,bhde->bhce', q_i, state) + jnp.einsum('bhij,bhje->bhie', qk_i, v_new)
        state = state * gl_i[..., None, None] + jnp.einsum('bhcd,bhce->bhde', k_i, v_new)
        return state, o_i

    xs = tuple(jnp.moveaxis(t, 2, 0) for t in (q_dec, k_dec, u, w, qk, g_last))
    state0 = jnp.zeros((b, n_heads, dk, dv), f32)
    _, o = lax.scan(step, state0, xs)
    o = jnp.moveaxis(o, 0, 2).reshape(b, n_heads, -1, dv)
    return jnp.transpose(o, (0, 2, 1, 3))[:, pad:]


def token_mixer(u, w_in, conv_qkv, a_log, dt_bias, gdn_norm, conv_sc, w_out):
    b, seq_len, _ = u.shape
    f32 = jnp.float32
    proj = u @ w_in
    cuts = [int(c) for c in np.cumsum(IN_SPLITS)[:-1]]
    q, k, v, z, b_logit, a_logit, sc_x, sc_b, sc_c = jnp.split(proj, cuts, axis=-1)

    qkv = jax.nn.silu(causal_depthwise_conv(jnp.concatenate([q, k, v], axis=-1), conv_qkv))
    q, k, v = (t.reshape(b, seq_len, GDN_HEADS, GDN_HEAD_DIM) for t in jnp.split(qkv, 3, axis=-1))
    q = l2_normalize(q) * (GDN_HEAD_DIM ** -0.5)
    k = l2_normalize(k)
    beta = jax.nn.sigmoid(b_logit.astype(f32))
    g = -jnp.exp(a_log.astype(f32)) * jax.nn.softplus(a_logit.astype(f32) + dt_bias.astype(f32))
    o = chunked_gated_delta_rule(q, k, v, g, beta)
    gate = jax.nn.silu(z.astype(f32)).reshape(b, seq_len, GDN_HEADS, GDN_HEAD_DIM)
    o = (rms_norm(o, gdn_norm) * gate).astype(u.dtype).reshape(b, seq_len, GDN_WIDTH)

    y_sc = sc_b * causal_depthwise_conv(sc_c * sc_x, conv_sc)

    return jnp.concatenate([o, y_sc], axis=-1) @ w_out


def swiglu(u, w_gate, w_up, w_down):
    return (jax.nn.silu(u @ w_gate) * (u @ w_up)) @ w_down


def _fwd_setup_inputs(seed: int = 0) -> dict:
    key = jax.random.key(seed)
    ks = jax.random.split(key, 17)
    f32 = jnp.float32

    def nrm(k, shape, scale):
        return jax.random.normal(k, shape, f32) * scale

    def gain(k, width):
        return 1.0 + 0.02 * jax.random.normal(k, (DEPTH, width), f32)

    dt = jnp.exp(jax.random.uniform(ks[9], (DEPTH, GDN_HEADS), f32, math.log(1e-3), math.log(1e-1)))
    return {
        'x': nrm(ks[0], (BATCH, SEQ, D_MODEL), 1.0),
        'meta_tokens': nrm(ks[1], (N_META, D_MODEL), 1.0),
        'mix_pre_norm': gain(ks[2], D_MODEL),
        'mix_post_norm': gain(ks[3], D_MODEL),
        'ffn_pre_norm': gain(ks[4], D_MODEL),
        'ffn_post_norm': gain(ks[5], D_MODEL),
        'w_in': nrm(ks[6], (DEPTH, D_MODEL, IN_WIDTH), D_MODEL ** -0.5),
        'conv_qkv': nrm(ks[7], (DEPTH, GDN_CONV, 3 * GDN_WIDTH), GDN_CONV ** -0.5),
        'a_log': jnp.log(jax.random.uniform(ks[8], (DEPTH, GDN_HEADS), f32, 1.0, 16.0)),
        'dt_bias': dt + jnp.log(-jnp.expm1(-dt)),
        'gdn_norm': gain(ks[10], GDN_HEAD_DIM),
        'conv_sc': nrm(ks[11], (DEPTH, SC_CONV, SC_WIDTH), SC_CONV ** -0.5),
        'w_out': nrm(ks[12], (DEPTH, MIX_WIDTH, D_MODEL), MIX_WIDTH ** -0.5),
        'w_gate': nrm(ks[13], (DEPTH, D_MODEL, D_FF), D_MODEL ** -0.5),
        'w_up': nrm(ks[14], (DEPTH, D_MODEL, D_FF), D_MODEL ** -0.5),
        'w_down': nrm(ks[15], (DEPTH, D_FF, D_MODEL), D_FF ** -0.5),
    }


def _fwd_reference(x, meta_tokens, mix_pre_norm, mix_post_norm, ffn_pre_norm, ffn_post_norm, w_in, conv_qkv,
              a_log, dt_bias, gdn_norm, conv_sc, w_out, w_gate, w_up, w_down):
    b = x.shape[0]
    meta = jnp.broadcast_to(meta_tokens.astype(x.dtype)[None], (b, N_META, D_MODEL))
    h = jnp.concatenate([meta, x], axis=1)
    for l in range(DEPTH):
        mix = token_mixer(rms_norm(h, mix_pre_norm[l]), w_in[l], conv_qkv[l], a_log[l], dt_bias[l],
                          gdn_norm[l], conv_sc[l], w_out[l])
        h = h + rms_norm(mix, mix_post_norm[l])
        ffn = swiglu(rms_norm(h, ffn_pre_norm[l]), w_gate[l], w_up[l], w_down[l])
        h = h + rms_norm(ffn, ffn_post_norm[l])
    return h[:, N_META:]


import jax as _jax
import jax.numpy as _jnp

TWIN_FORMAT = 'train_step'
FWD_PARAMS = ['x', 'meta_tokens', 'mix_pre_norm', 'mix_post_norm', 'ffn_pre_norm', 'ffn_post_norm', 'w_in', 'conv_qkv', 'a_log', 'dt_bias', 'gdn_norm', 'conv_sc', 'w_out', 'w_gate', 'w_up', 'w_down']
TWIN_WEIGHTS = ['meta_tokens', 'mix_pre_norm', 'mix_post_norm', 'ffn_pre_norm', 'ffn_post_norm', 'w_in', 'conv_qkv', 'a_log', 'dt_bias', 'gdn_norm', 'conv_sc', 'w_out', 'w_gate', 'w_up', 'w_down']
TWIN_DIFF_INPUT = 'x'
TWIN_INPUTS = ['x', 'meta_tokens', 'mix_pre_norm', 'mix_post_norm', 'ffn_pre_norm', 'ffn_post_norm', 'w_in', 'conv_qkv', 'a_log', 'dt_bias', 'gdn_norm', 'conv_sc', 'w_out', 'w_gate', 'w_up', 'w_down', 'loss_target', 'm_meta_tokens', 'm_mix_pre_norm', 'm_mix_post_norm', 'm_ffn_pre_norm', 'm_ffn_post_norm', 'm_w_in', 'm_conv_qkv', 'm_a_log', 'm_dt_bias', 'm_gdn_norm', 'm_conv_sc', 'm_w_out', 'm_w_gate', 'm_w_up', 'm_w_down', 'v_meta_tokens', 'v_mix_pre_norm', 'v_mix_post_norm', 'v_ffn_pre_norm', 'v_ffn_post_norm', 'v_w_in', 'v_conv_qkv', 'v_a_log', 'v_dt_bias', 'v_gdn_norm', 'v_conv_sc', 'v_w_out', 'v_w_gate', 'v_w_up', 'v_w_down']
TWIN_OUTPUTS = ['loss', 'grad_x', 'grad_meta_tokens', 'grad_mix_pre_norm', 'grad_mix_post_norm', 'grad_ffn_pre_norm', 'grad_ffn_post_norm', 'grad_w_in', 'grad_conv_qkv', 'grad_a_log', 'grad_dt_bias', 'grad_gdn_norm', 'grad_conv_sc', 'grad_w_out', 'grad_w_gate', 'grad_w_up', 'grad_w_down', 'delta_meta_tokens', 'delta_mix_pre_norm', 'delta_mix_post_norm', 'delta_ffn_pre_norm', 'delta_ffn_post_norm', 'delta_w_in', 'delta_conv_qkv', 'delta_a_log', 'delta_dt_bias', 'delta_gdn_norm', 'delta_conv_sc', 'delta_w_out', 'delta_w_gate', 'delta_w_up', 'delta_w_down', 'new_m_meta_tokens', 'new_m_mix_pre_norm', 'new_m_mix_post_norm', 'new_m_ffn_pre_norm', 'new_m_ffn_post_norm', 'new_m_w_in', 'new_m_conv_qkv', 'new_m_a_log', 'new_m_dt_bias', 'new_m_gdn_norm', 'new_m_conv_sc', 'new_m_w_out', 'new_m_w_gate', 'new_m_w_up', 'new_m_w_down', 'new_v_meta_tokens', 'new_v_mix_pre_norm', 'new_v_mix_post_norm', 'new_v_ffn_pre_norm', 'new_v_ffn_post_norm', 'new_v_w_in', 'new_v_conv_qkv', 'new_v_a_log', 'new_v_dt_bias', 'new_v_gdn_norm', 'new_v_conv_sc', 'new_v_w_out', 'new_v_w_gate', 'new_v_w_up', 'new_v_w_down']
TWIN_LEAF_KINDS = {'loss': 'loss', 'grad_x': 'grad_x', 'grad_meta_tokens': 'grad_w', 'grad_mix_pre_norm': 'grad_w', 'grad_mix_post_norm': 'grad_w', 'grad_ffn_pre_norm': 'grad_w', 'grad_ffn_post_norm': 'grad_w', 'grad_w_in': 'grad_w', 'grad_conv_qkv': 'grad_w', 'grad_a_log': 'grad_w', 'grad_dt_bias': 'grad_w', 'grad_gdn_norm': 'grad_w', 'grad_conv_sc': 'grad_w', 'grad_w_out': 'grad_w', 'grad_w_gate': 'grad_w', 'grad_w_up': 'grad_w', 'grad_w_down': 'grad_w', 'delta_meta_tokens': 'delta_w', 'delta_mix_pre_norm': 'delta_w', 'delta_mix_post_norm': 'delta_w', 'delta_ffn_pre_norm': 'delta_w', 'delta_ffn_post_norm': 'delta_w', 'delta_w_in': 'delta_w', 'delta_conv_qkv': 'delta_w', 'delta_a_log': 'delta_w', 'delta_dt_bias': 'delta_w', 'delta_gdn_norm': 'delta_w', 'delta_conv_sc': 'delta_w', 'delta_w_out': 'delta_w', 'delta_w_gate': 'delta_w', 'delta_w_up': 'delta_w', 'delta_w_down': 'delta_w', 'new_m_meta_tokens': 'new_m', 'new_m_mix_pre_norm': 'new_m', 'new_m_mix_post_norm': 'new_m', 'new_m_ffn_pre_norm': 'new_m', 'new_m_ffn_post_norm': 'new_m', 'new_m_w_in': 'new_m', 'new_m_conv_qkv': 'new_m', 'new_m_a_log': 'new_m', 'new_m_dt_bias': 'new_m', 'new_m_gdn_norm': 'new_m', 'new_m_conv_sc': 'new_m', 'new_m_w_out': 'new_m', 'new_m_w_gate': 'new_m', 'new_m_w_up': 'new_m', 'new_m_w_down': 'new_m', 'new_v_meta_tokens': 'new_v', 'new_v_mix_pre_norm': 'new_v', 'new_v_mix_post_norm': 'new_v', 'new_v_ffn_pre_norm': 'new_v', 'new_v_ffn_post_norm': 'new_v', 'new_v_w_in': 'new_v', 'new_v_conv_qkv': 'new_v', 'new_v_a_log': 'new_v', 'new_v_dt_bias': 'new_v', 'new_v_gdn_norm': 'new_v', 'new_v_conv_sc': 'new_v', 'new_v_w_out': 'new_v', 'new_v_w_gate': 'new_v', 'new_v_w_up': 'new_v', 'new_v_w_down': 'new_v'}


def _forward(args):
    return _fwd_reference(*[args[k] for k in FWD_PARAMS])


def _output_shape():
    out = _jax.eval_shape(lambda: _forward(_fwd_setup_inputs(0)))
    return out.shape, out.dtype

N_MICROBATCH = 1
ADAM_LR = 0.001
ADAM_B1 = 0.9
ADAM_B2 = 0.999
ADAM_EPS = 1e-08
ADAM_WD = 0.01
ADAM_STEP = 10
PER_EXAMPLE_BATCH_AXIS = {'x': 0, 'loss_target': 0}
SHARED_INPUTS = []
_WEIGHT_DTYPES = {'meta_tokens': _jnp.float32, 'mix_pre_norm': _jnp.float32, 'mix_post_norm': _jnp.float32, 'ffn_pre_norm': _jnp.float32, 'ffn_post_norm': _jnp.float32, 'w_in': _jnp.float32, 'conv_qkv': _jnp.float32, 'a_log': _jnp.float32, 'dt_bias': _jnp.float32, 'gdn_norm': _jnp.float32, 'conv_sc': _jnp.float32, 'w_out': _jnp.float32, 'w_gate': _jnp.float32, 'w_up': _jnp.float32, 'w_down': _jnp.float32}
MOMENT_SCALE = {'meta_tokens': 1.832347e-02, 'mix_pre_norm': 7.931488e-01, 'mix_post_norm': 3.194994e+01, 'ffn_pre_norm': 6.202777e-01, 'ffn_post_norm': 3.203677e+01, 'w_in': 4.128690e-01, 'conv_qkv': 2.913322e-01, 'a_log': 1.889964e+00, 'dt_bias': 1.849209e+00, 'gdn_norm': 9.420842e-01, 'conv_sc': 6.032470e-01, 'w_out': 5.723449e-01, 'w_gate': 2.047657e-01, 'w_up': 2.952198e-01, 'w_down': 4.895483e-01}


def _to_microbatches(a, axis):
    t = _jnp.moveaxis(a, axis, 0)
    t = t.reshape((N_MICROBATCH, t.shape[0] // N_MICROBATCH) + t.shape[1:])
    return _jnp.moveaxis(t, 1, axis + 1)


def setup_inputs(seed: int = 0) -> dict:
    inp = _fwd_setup_inputs(seed)
    key = _jax.random.fold_in(_jax.random.key(seed), 7919)
    shape, _ = _output_shape()
    out = dict(inp)
    out["loss_target"] = _jax.random.normal(_jax.random.fold_in(key, 0), shape, _jnp.float32)
    for i, name in enumerate(TWIN_WEIGHTS):
        w = inp[name].astype(_jnp.float32)
        if MOMENT_SCALE is None:
            s = _jnp.sqrt(_jnp.mean(_jnp.square(w)) + 1e-30)
        else:
            s = MOMENT_SCALE[name]
        km, kv = _jax.random.split(_jax.random.fold_in(key, i + 1))
        out[name] = w
        out["m_" + name] = s * _jax.random.normal(km, w.shape, _jnp.float32)
        out["v_" + name] = (s * s) * _jax.random.uniform(kv, w.shape, _jnp.float32, 0.5, 1.5)
    if N_MICROBATCH > 1:
        for name, axis in PER_EXAMPLE_BATCH_AXIS.items():
            out[name] = _to_microbatches(out[name], axis)
    return {'x': out['x'], 'meta_tokens': out['meta_tokens'], 'mix_pre_norm': out['mix_pre_norm'], 'mix_post_norm': out['mix_post_norm'], 'ffn_pre_norm': out['ffn_pre_norm'], 'ffn_post_norm': out['ffn_post_norm'], 'w_in': out['w_in'], 'conv_qkv': out['conv_qkv'], 'a_log': out['a_log'], 'dt_bias': out['dt_bias'], 'gdn_norm': out['gdn_norm'], 'conv_sc': out['conv_sc'], 'w_out': out['w_out'], 'w_gate': out['w_gate'], 'w_up': out['w_up'], 'w_down': out['w_down'], 'loss_target': out['loss_target'], 'm_meta_tokens': out['m_meta_tokens'], 'm_mix_pre_norm': out['m_mix_pre_norm'], 'm_mix_post_norm': out['m_mix_post_norm'], 'm_ffn_pre_norm': out['m_ffn_pre_norm'], 'm_ffn_post_norm': out['m_ffn_post_norm'], 'm_w_in': out['m_w_in'], 'm_conv_qkv': out['m_conv_qkv'], 'm_a_log': out['m_a_log'], 'm_dt_bias': out['m_dt_bias'], 'm_gdn_norm': out['m_gdn_norm'], 'm_conv_sc': out['m_conv_sc'], 'm_w_out': out['m_w_out'], 'm_w_gate': out['m_w_gate'], 'm_w_up': out['m_w_up'], 'm_w_down': out['m_w_down'], 'v_meta_tokens': out['v_meta_tokens'], 'v_mix_pre_norm': out['v_mix_pre_norm'], 'v_mix_post_norm': out['v_mix_post_norm'], 'v_ffn_pre_norm': out['v_ffn_pre_norm'], 'v_ffn_post_norm': out['v_ffn_post_norm'], 'v_w_in': out['v_w_in'], 'v_conv_qkv': out['v_conv_qkv'], 'v_a_log': out['v_a_log'], 'v_dt_bias': out['v_dt_bias'], 'v_gdn_norm': out['v_gdn_norm'], 'v_conv_sc': out['v_conv_sc'], 'v_w_out': out['v_w_out'], 'v_w_gate': out['v_w_gate'], 'v_w_up': out['v_w_up'], 'v_w_down': out['v_w_down']}


def _loss(weights, diff, rest, loss_target):
    with _jax.named_scope("forward"):
        args = {**rest, TWIN_DIFF_INPUT: diff, **{k: w.astype(_WEIGHT_DTYPES[k]) for k, w in weights.items()}}
        y = _forward(args)
    with _jax.named_scope("loss_head"):
        err = _jnp.square(y.astype(_jnp.float32) - loss_target)
        return 0.5 * _jnp.sum(_jnp.mean(err, axis=-1)) if err.ndim else 0.5 * err


def _adamw(w, g, m, v):
    m = ADAM_B1 * m + (1.0 - ADAM_B1) * g
    v = ADAM_B2 * v + (1.0 - ADAM_B2) * _jnp.square(g)
    m_hat = m / (1.0 - ADAM_B1 ** ADAM_STEP)
    v_hat = v / (1.0 - ADAM_B2 ** ADAM_STEP)
    delta = -ADAM_LR * (m_hat / (_jnp.sqrt(v_hat) + ADAM_EPS) + ADAM_WD * w)
    return delta, m, v


def reference(x, meta_tokens, mix_pre_norm, mix_post_norm, ffn_pre_norm, ffn_post_norm, w_in, conv_qkv, a_log, dt_bias, gdn_norm, conv_sc, w_out, w_gate, w_up, w_down, loss_target, m_meta_tokens, m_mix_pre_norm, m_mix_post_norm, m_ffn_pre_norm, m_ffn_post_norm, m_w_in, m_conv_qkv, m_a_log, m_dt_bias, m_gdn_norm, m_conv_sc, m_w_out, m_w_gate, m_w_up, m_w_down, v_meta_tokens, v_mix_pre_norm, v_mix_post_norm, v_ffn_pre_norm, v_ffn_post_norm, v_w_in, v_conv_qkv, v_a_log, v_dt_bias, v_gdn_norm, v_conv_sc, v_w_out, v_w_gate, v_w_up, v_w_down):
    given = dict(x=x, meta_tokens=meta_tokens, mix_pre_norm=mix_pre_norm, mix_post_norm=mix_post_norm, ffn_pre_norm=ffn_pre_norm, ffn_post_norm=ffn_post_norm, w_in=w_in, conv_qkv=conv_qkv, a_log=a_log, dt_bias=dt_bias, gdn_norm=gdn_norm, conv_sc=conv_sc, w_out=w_out, w_gate=w_gate, w_up=w_up, w_down=w_down, loss_target=loss_target, m_meta_tokens=m_meta_tokens, m_mix_pre_norm=m_mix_pre_norm, m_mix_post_norm=m_mix_post_norm, m_ffn_pre_norm=m_ffn_pre_norm, m_ffn_post_norm=m_ffn_post_norm, m_w_in=m_w_in, m_conv_qkv=m_conv_qkv, m_a_log=m_a_log, m_dt_bias=m_dt_bias, m_gdn_norm=m_gdn_norm, m_conv_sc=m_conv_sc, m_w_out=m_w_out, m_w_gate=m_w_gate, m_w_up=m_w_up, m_w_down=m_w_down, v_meta_tokens=v_meta_tokens, v_mix_pre_norm=v_mix_pre_norm, v_mix_post_norm=v_mix_post_norm, v_ffn_pre_norm=v_ffn_pre_norm, v_ffn_post_norm=v_ffn_post_norm, v_w_in=v_w_in, v_conv_qkv=v_conv_qkv, v_a_log=v_a_log, v_dt_bias=v_dt_bias, v_gdn_norm=v_gdn_norm, v_conv_sc=v_conv_sc, v_w_out=v_w_out, v_w_gate=v_w_gate, v_w_up=v_w_up, v_w_down=v_w_down)
    weights = {n: given[n] for n in TWIN_WEIGHTS}
    shared = {n: given[n] for n in SHARED_INPUTS}
    per_example = {n: given[n] for n in ['x']}
    grad_fn = _jax.value_and_grad(_loss, argnums=(0, 1))

    def one_microbatch(ex, loss_target):
        ex = dict(ex)
        diff = ex.pop(TWIN_DIFF_INPUT)
        return grad_fn(weights, diff, {**shared, **ex}, loss_target)

    if N_MICROBATCH == 1:
        loss, (grad_w, grad_x) = one_microbatch(per_example, given["loss_target"])
    else:
        def body(carry, xs):
            loss_sum, grad_sum = carry
            l_k, (gw_k, gx_k) = one_microbatch(xs[0], xs[1])
            with _jax.named_scope("update"):
                return (loss_sum + l_k, _jax.tree.map(_jnp.add, grad_sum, gw_k)), gx_k

        init = (_jnp.zeros((), _jnp.float32), _jax.tree.map(_jnp.zeros_like, weights))
        (loss, grad_w), grad_x = _jax.lax.scan(body, init, (per_example, given["loss_target"]))
    with _jax.named_scope("update"):
        delta_w, new_m, new_v = {}, {}, {}
        for n in TWIN_WEIGHTS:
            delta_w[n], new_m[n], new_v[n] = _adamw(weights[n], grad_w[n], given["m_" + n], given["v_" + n])
    return (loss, grad_x, *[grad_w[n] for n in TWIN_WEIGHTS], *[delta_w[n] for n in TWIN_WEIGHTS],
            *[new_m[n] for n in TWIN_WEIGHTS], *[new_v[n] for n in TWIN_WEIGHTS])
```

```python
import functools

import jax
import jax.numpy as jnp
from jax import lax
from jax.experimental import pallas as pl
from jax.experimental.pallas import tpu as pltpu

F32 = jnp.float32
BF16 = jnp.bfloat16
HIGHEST = lax.Precision.HIGHEST
MESH = pl.DeviceIdType.MESH

D = 1024
N_META = 16
HEADS = 4
DH = 128
GW = HEADS * DH
SW = D - GW
FF = 2816
CH = 64
KQ = 4
KS = 3
EPS = 1e-6
IN_W = 3 * GW + GW + 2 * HEADS + 3 * SW
Q0, K0, V0, Z0, BA0, SX0, SB0, SC0, PW = 0, 512, 1024, 1536, 2048, 2304, 2816, 3328, 3840
BA_W = SX0 - BA0
N_DEV = 8
VMEM_LIMIT = 56 * 1024 * 1024

ADAM_LR, ADAM_B1, ADAM_B2, ADAM_EPS, ADAM_WD, ADAM_STEP = 0.001, 0.9, 0.999, 1e-08, 0.01, 10


def _cparams(sem):
    return pltpu.CompilerParams(dimension_semantics=sem, vmem_limit_bytes=VMEM_LIMIT)


def _div_tile(n, target, mult=16):
    best = None
    for t in range(mult, min(n, target) + 1, mult):
        if n % t == 0:
            best = t
    assert best is not None, (n, target)
    return best


def _silu(x):
    return x * jax.nn.sigmoid(x)


def _dsilu(x):
    s = jax.nn.sigmoid(x)
    return s * (1.0 + x * (1.0 - s))


def _bdot(a, b, dims):
    return lax.dot_general(a.astype(BF16), b.astype(BF16), (dims, ((), ())), preferred_element_type=F32)


NN = ((1,), (0,))
NT = ((1,), (1,))
TN = ((0,), (0,))


def _fdot(a, b, dims=NN):
    return lax.dot_general(a, b, (dims, ((), ())), precision=HIGHEST, preferred_element_type=F32)


def _mm(name, pairs, mode, tm, tn, tk, out_dtypes=(F32,), epilogue=None, extras=()):
    a0, b0 = pairs[0]
    if mode == "nn":
        (m, k), n = a0.shape, b0.shape[1]
    elif mode == "nt":
        (m, k), n = a0.shape, b0.shape[0]
    else:
        (k, m), n = a0.shape, b0.shape[1]
    assert m % tm == 0 and n % tn == 0 and k % tk == 0, (name, m, n, k, tm, tn, tk)
    nk = k // tk
    npair = len(pairs)
    nex = len(extras)
    nout = len(out_dtypes)
    dims = {"nn": NN, "nt": NT, "tn": TN}[mode]

    def body(*refs):
        ab = refs[:2 * npair]
        ex = refs[2 * npair:2 * npair + nex]
        outs = refs[2 * npair + nex:2 * npair + nex + nout]
        acc_ref = refs[-1]
        kk = pl.program_id(2)

        part = None
        for p in range(npair):
            d = lax.dot_general(ab[2 * p][...], ab[2 * p + 1][...], (dims, ((), ())), preferred_element_type=F32)
            part = d if part is None else part + d

        def finish(acc):
            res = epilogue(acc, *[e[...] for e in ex]) if epilogue is not None else (acc,)
            for o, r in zip(outs, res):
                o[...] = r.astype(o.dtype)

        if nk == 1:
            finish(part)
        else:
            @pl.when(kk == 0)
            def _():
                acc_ref[...] = part

            @pl.when(kk > 0)
            def _():
                acc_ref[...] += part

            @pl.when(kk == nk - 1)
            def _():
                finish(acc_ref[...])

    if mode == "nn":
        a_spec = pl.BlockSpec((tm, tk), lambda i, j, q: (i, q))
        b_spec = pl.BlockSpec((tk, tn), lambda i, j, q: (q, j))
    elif mode == "nt":
        a_spec = pl.BlockSpec((tm, tk), lambda i, j, q: (i, q))
        b_spec = pl.BlockSpec((tn, tk), lambda i, j, q: (j, q))
    else:
        a_spec = pl.BlockSpec((tk, tm), lambda i, j, q: (q, i))
        b_spec = pl.BlockSpec((tk, tn), lambda i, j, q: (q, j))
    o_spec = pl.BlockSpec((tm, tn), lambda i, j, q: (i, j))
    flat = [t for pr in pairs for t in pr]
    res = pl.pallas_call(
        body, name=name,
        grid=(m // tm, n // tn, nk),
        in_specs=[a_spec, b_spec] * npair + [o_spec] * nex,
        out_specs=[o_spec] * nout,
        out_shape=[jax.ShapeDtypeStruct((m, n), dt) for dt in out_dtypes],
        scratch_shapes=[pltpu.VMEM((tm, tn), F32)],
        compiler_params=_cparams(("parallel", "parallel", "arbitrary")),
    )(*flat, *extras)
    return res


def _ffn_up(u2, wg, wu, tm, tn):
    m, k = u2.shape
    n = wg.shape[1]

    def body(a_ref, g_ref, u_ref, gate_ref, up_ref, act_ref):
        a = a_ref[...]
        g = jnp.dot(a, g_ref[...], preferred_element_type=F32)
        u = jnp.dot(a, u_ref[...], preferred_element_type=F32)
        gate_ref[...] = g
        up_ref[...] = u
        act_ref[...] = (_silu(g) * u).astype(BF16)

    o_spec = pl.BlockSpec((tm, tn), lambda i, j: (i, j))
    w_spec = pl.BlockSpec((k, tn), lambda i, j: (0, j))
    return pl.pallas_call(
        body, name="ffn_up", grid=(m // tm, n // tn),
        in_specs=[pl.BlockSpec((tm, k), lambda i, j: (i, 0)), w_spec, w_spec],
        out_specs=[o_spec, o_spec, o_spec],
        out_shape=[jax.ShapeDtypeStruct((m, n), F32), jax.ShapeDtypeStruct((m, n), F32),
                   jax.ShapeDtypeStruct((m, n), BF16)],
        compiler_params=_cparams(("parallel", "parallel")),
    )(u2, wg, wu)


def _rms(x, gain):
    r = lax.rsqrt(jnp.mean(x * x, axis=-1, keepdims=True) + EPS)
    return x * r * gain, r


def _rms_bwd(x, gain, dy):
    r = lax.rsqrt(jnp.mean(x * x, axis=-1, keepdims=True) + EPS)
    dyw = dy * gain
    dx = r * dyw - x * (r * r * r) * jnp.mean(dyw * x, axis=-1, keepdims=True)
    return dx, dy * x * r


def _row_specs(tr, width, nb):
    return pl.BlockSpec((tr, width), lambda b, j: (b * nb + j, 0))


def _vec_spec(width):
    return pl.BlockSpec((1, width), lambda b, j: (0, 0))


def _first_step(b, j):
    return jnp.logical_and(b == 0, j == 0)


def _pre_norm(h0, gain, bsz, lp, tr):
    nb = lp // tr

    def body(h_ref, g_ref, u_ref):
        u_ref[...] = _rms(h_ref[...], g_ref[...])[0].astype(BF16)

    return pl.pallas_call(
        body, name="mix_pre_norm", grid=(bsz, nb),
        in_specs=[_row_specs(tr, D, nb), _vec_spec(D)],
        out_specs=_row_specs(tr, D, nb),
        out_shape=jax.ShapeDtypeStruct(h0.shape, BF16),
        compiler_params=_cparams(("parallel", "parallel")),
    )(h0, gain)


def _mid_norms(h0, mix, g_post, g_pre, bsz, lp, tr):
    nb = lp // tr

    def body(h_ref, m_ref, gp_ref, gq_ref, h1_ref, u2_ref):
        h1 = h_ref[...] + _rms(m_ref[...], gp_ref[...])[0]
        h1_ref[...] = h1
        u2_ref[...] = _rms(h1, gq_ref[...])[0].astype(BF16)

    rs = _row_specs(tr, D, nb)
    return pl.pallas_call(
        body, name="mid_norms", grid=(bsz, nb),
        in_specs=[rs, rs, _vec_spec(D), _vec_spec(D)],
        out_specs=[rs, rs],
        out_shape=[jax.ShapeDtypeStruct(h0.shape, F32), jax.ShapeDtypeStruct(h0.shape, BF16)],
        compiler_params=_cparams(("parallel", "parallel")),
    )(h0, mix, g_post, g_pre)


def _loss_head(h1, ffn, tgt, g_post, bsz, lp, tr, first_real):
    nb = lp // tr

    def body(h_ref, f_ref, t_ref, g_ref, dy_ref, dffn_ref, dg_ref, loss_ref):
        b, j = pl.program_id(0), pl.program_id(1)
        f = f_ref[...]
        g = g_ref[...]
        y = h_ref[...] + _rms(f, g)[0]
        t = j * tr + lax.broadcasted_iota(jnp.int32, (tr, 1), 0)
        err = jnp.where(t >= first_real, y - t_ref[...], 0.0)
        dy = err * (1.0 / D)
        dy_ref[...] = dy
        dx, dgc = _rms_bwd(f, g, dy)
        dffn_ref[...] = dx.astype(BF16)
        lsum = jnp.sum(jnp.sum(err * err, axis=0, keepdims=True), axis=1, keepdims=True) * (0.5 / D)

        @pl.when(_first_step(b, j))
        def _():
            dg_ref[...] = jnp.zeros_like(dg_ref)
            loss_ref[...] = jnp.zeros_like(loss_ref)

        dg_ref[...] += jnp.sum(dgc, axis=0, keepdims=True)
        loss_ref[...] += jnp.broadcast_to(lsum, loss_ref.shape)

    rs = _row_specs(tr, D, nb)
    return pl.pallas_call(
        body, name="loss_head", grid=(bsz, nb),
        in_specs=[rs, rs, rs, _vec_spec(D)],
        out_specs=[rs, rs, _vec_spec(D), _vec_spec(DH)],
        out_shape=[jax.ShapeDtypeStruct(h1.shape, F32), jax.ShapeDtypeStruct(h1.shape, BF16),
                   jax.ShapeDtypeStruct((1, D), F32), jax.ShapeDtypeStruct((1, DH), F32)],
        compiler_params=_cparams(("arbitrary", "arbitrary")),
    )(h1, ffn, tgt, g_post)


def _mid_norms_bwd(h1, mix, du2, dy, g_pre, g_post, bsz, lp, tr):
    nb = lp // tr

    def body(h_ref, m_ref, du_ref, dy_ref, gq_ref, gp_ref, dh1_ref, dmix_ref, dgq_ref, dgp_ref):
        b, j = pl.program_id(0), pl.program_id(1)
        dx, dgq = _rms_bwd(h_ref[...], gq_ref[...], du_ref[...])
        dh1 = dy_ref[...] + dx
        dh1_ref[...] = dh1
        dm, dgp = _rms_bwd(m_ref[...], gp_ref[...], dh1)
        dmix_ref[...] = dm.astype(BF16)

        @pl.when(_first_step(b, j))
        def _():
            dgq_ref[...] = jnp.zeros_like(dgq_ref)
            dgp_ref[...] = jnp.zeros_like(dgp_ref)

        dgq_ref[...] += jnp.sum(dgq, axis=0, keepdims=True)
        dgp_ref[...] += jnp.sum(dgp, axis=0, keepdims=True)

    rs = _row_specs(tr, D, nb)
    return pl.pallas_call(
        body, name="mid_norms_bwd", grid=(bsz, nb),
        in_specs=[rs, rs, rs, rs, _vec_spec(D), _vec_spec(D)],
        out_specs=[rs, rs, _vec_spec(D), _vec_spec(D)],
        out_shape=[jax.ShapeDtypeStruct(h1.shape, F32), jax.ShapeDtypeStruct(h1.shape, BF16),
                   jax.ShapeDtypeStruct((1, D), F32), jax.ShapeDtypeStruct((1, D), F32)],
        compiler_params=_cparams(("arbitrary", "arbitrary")),
    )(h1, mix, du2, dy, g_pre, g_post)


def _pre_norm_bwd(h0, du1, dh1, gain, bsz, lp, tr):
    nb = lp // tr

    def body(h_ref, du_ref, dh1_ref, g_ref, dh0_ref, dg_ref):
        b, j = pl.program_id(0), pl.program_id(1)
        dx, dgc = _rms_bwd(h_ref[...], g_ref[...], du_ref[...])
        dh0_ref[...] = dh1_ref[...] + dx

        @pl.when(_first_step(b, j))
        def _():
            dg_ref[...] = jnp.zeros_like(dg_ref)

        dg_ref[...] += jnp.sum(dgc, axis=0, keepdims=True)

    rs = _row_specs(tr, D, nb)
    return pl.pallas_call(
        body, name="pre_norm_bwd", grid=(bsz, nb),
        in_specs=[rs, rs, rs, _vec_spec(D)],
        out_specs=[rs, _vec_spec(D)],
        out_shape=[jax.ShapeDtypeStruct(h0.shape, F32), jax.ShapeDtypeStruct((1, D), F32)],
        compiler_params=_cparams(("arbitrary", "arbitrary")),
    )(h0, du1, dh1, gain)


def _shift_down(x, s, lp):
    return x if s == 0 else pltpu.roll(x, s, axis=0)


def _shift_up(x, s, lp):
    return x if s == 0 else pltpu.roll(x, lp - s, axis=0)


def _conv_fwd(x, w, taps, lp):
    y = None
    for i in range(taps):
        term = _shift_down(x, taps - 1 - i, lp) * w[i:i + 1, :]
        y = term if y is None else y + term
    return y


def _conv_bwd(x, w, dy, taps, lp):
    dx = None
    dws = []
    for i in range(taps):
        s = taps - 1 - i
        term = _shift_up(dy, s, lp) * w[i:i + 1, :]
        dx = term if dx is None else dx + term
        dws.append(jnp.sum(dy * _shift_down(x, s, lp), axis=0, keepdims=True))
    return dx, jnp.concatenate(dws, axis=0)


def _slab(lp, col0):
    return pl.BlockSpec((lp, DH), lambda b, s: (b, col0 // DH + s))


def _qkv_act(c, s):
    y = _silu(c)
    r = lax.rsqrt(jnp.sum(y * y, axis=-1, keepdims=True) + EPS)
    scale = jnp.where(s < HEADS, DH ** -0.5, 1.0)
    return y, r, scale


def _gdn_prep(proj, conv_qkv, bsz, lp, padf):
    def body(p_ref, w_ref, o_ref):
        s = pl.program_id(1)
        c = _conv_fwd(p_ref[...], w_ref[...], KQ, lp)
        y, r, scale = _qkv_act(c, s)
        out = jnp.where(s < 2 * HEADS, y * (r * scale), y)
        t = lax.broadcasted_iota(jnp.int32, (lp, 1), 0)
        o_ref[...] = jnp.where(t >= padf, out, 0.0)

    return pl.pallas_call(
        body, name="gdn_prep", grid=(bsz, 3 * HEADS),
        in_specs=[_slab(lp, Q0), pl.BlockSpec((KQ, DH), lambda b, s: (0, s))],
        out_specs=_slab(lp, 0),
        out_shape=jax.ShapeDtypeStruct((bsz * lp, 3 * GW), F32),
        compiler_params=_cparams(("parallel", "parallel")),
    )(proj, conv_qkv)


def _gdn_prep_bwd(proj, conv_qkv, dqkv, bsz, lp, padf):
    def body(p_ref, w_ref, d_ref, dp_ref, dw_ref):
        s, b = pl.program_id(0), pl.program_id(1)
        x = p_ref[...]
        w = w_ref[...]
        c = _conv_fwd(x, w, KQ, lp)
        y, r, scale = _qkv_act(c, s)
        dn = d_ref[...] * scale
        n = y * r
        dy_norm = r * (dn - n * jnp.sum(dn * n, axis=-1, keepdims=True))
        dyy = jnp.where(s < 2 * HEADS, dy_norm, dn)
        t = lax.broadcasted_iota(jnp.int32, (lp, 1), 0)
        dc = jnp.where(t >= padf, dyy * _dsilu(c), 0.0)
        dx, dw = _conv_bwd(x, w, dc, KQ, lp)
        dp_ref[...] = jnp.where(t >= padf, dx, 0.0).astype(BF16)

        @pl.when(b == 0)
        def _():
            dw_ref[...] = jnp.zeros_like(dw_ref)

        dw_ref[...] += dw

    sl = lambda col0: pl.BlockSpec((lp, DH), lambda s, b: (b, col0 // DH + s))
    return pl.pallas_call(
        body, name="gdn_prep_bwd", grid=(3 * HEADS, bsz),
        in_specs=[sl(Q0), pl.BlockSpec((KQ, DH), lambda s, b: (0, s)), sl(0)],
        out_specs=[sl(0), pl.BlockSpec((KQ, DH), lambda s, b: (0, s))],
        out_shape=[jax.ShapeDtypeStruct((bsz * lp, 3 * GW), BF16), jax.ShapeDtypeStruct((KQ, 3 * GW), F32)],
        compiler_params=_cparams(("parallel", "arbitrary")),
    )(proj, conv_qkv, dqkv)


def _softplus(x):
    e = jnp.exp(-jnp.abs(x))
    one_e = 1.0 + e
    l1p = jnp.where(one_e == 1.0, e, jnp.log(one_e) * (e / (one_e - 1.0)))
    return jnp.maximum(x, 0.0) + l1p


def _tri(n, kind):
    i = lax.broadcasted_iota(jnp.int32, (n, n), 0)
    j = lax.broadcasted_iota(jnp.int32, (n, n), 1)
    return {"incl": i >= j, "strict": i > j, "upper": i <= j}[kind]


def _gates(proj, alog_v, dtb_v, bsz, lp, padf):
    nchunk = lp // CH

    def body(p_ref, al_ref, dt_ref, gb_ref, cs_ref):
        x = p_ref[...]
        lane = lax.broadcasted_iota(jnp.int32, (lp, DH), 1)
        t = lax.broadcasted_iota(jnp.int32, (lp, DH), 0)
        beta = jax.nn.sigmoid(x)
        g = -jnp.exp(al_ref[...]) * _softplus(x + dt_ref[...])
        gb = jnp.where(lane < HEADS, beta, jnp.where(lane < 2 * HEADS, g, 0.0))
        gb_ref[...] = jnp.where(t >= padf, gb, 0.0)
        ltri = _tri(CH, "incl").astype(F32)

        def step(n, carry):
            r0 = pl.multiple_of(n * CH, CH)
            cs_ref[pl.ds(r0, CH), :] = _fdot(ltri, gb_ref[pl.ds(r0, CH), :])
            return carry

        lax.fori_loop(0, nchunk, step, 0)

    blk = pl.BlockSpec((lp, DH), lambda b: (b, BA0 // DH))
    vec = pl.BlockSpec((1, DH), lambda b: (0, 0))
    out = pl.BlockSpec((lp, DH), lambda b: (b, 0))
    return pl.pallas_call(
        body, name="gates", grid=(bsz,),
        in_specs=[blk, vec, vec], out_specs=[out, out],
        out_shape=[jax.ShapeDtypeStruct((bsz * lp, DH), F32)] * 2,
        compiler_params=_cparams(("parallel",)),
    )(proj, alog_v, dtb_v)


def _gates_bwd(proj, alog_v, dtb_v, dbeta_b, dg_b, gb, bsz, lp, padf):
    def body(p_ref, al_ref, dt_ref, db_ref, dg_ref, gb_ref, dl_ref, dv_ref):
        b = pl.program_id(0)
        x = p_ref[...]
        lane = lax.broadcasted_iota(jnp.int32, (lp, DH), 1)
        t = lax.broadcasted_iota(jnp.int32, (lp, DH), 0)
        dbeta = jnp.zeros((lp, DH), F32)
        dg = jnp.zeros((lp, DH), F32)
        for h in range(HEADS):
            dbeta = jnp.where(lane == h, db_ref[h], dbeta)
            dg = jnp.where(lane == HEADS + h, dg_ref[h], dg)
        real = t >= padf
        dbeta = jnp.where(real, dbeta, 0.0)
        dg = jnp.where(real, dg, 0.0)
        sb = jax.nn.sigmoid(x)
        neg_ea = -jnp.exp(al_ref[...])
        dsp = dg * neg_ea * jax.nn.sigmoid(x + dt_ref[...])
        dl = dbeta * sb * (1.0 - sb) + dsp
        dl_ref[...] = jnp.concatenate([dl, jnp.zeros((lp, BA_W - DH), F32)], axis=1).astype(BF16)
        dalog = jnp.sum(dg * gb_ref[...], axis=0, keepdims=True)
        ddt = jnp.sum(dsp, axis=0, keepdims=True)

        @pl.when(b == 0)
        def _():
            dv_ref[...] = jnp.zeros_like(dv_ref)

        dv_ref[...] += jnp.concatenate([dalog, ddt], axis=0)

    blk = pl.BlockSpec((lp, DH), lambda b: (b, BA0 // DH))
    vec = pl.BlockSpec((1, DH), lambda b: (0, 0))
    hb = pl.BlockSpec((HEADS, lp, DH), lambda b: (b, 0, 0))
    return pl.pallas_call(
        body, name="gates_bwd", grid=(bsz,),
        in_specs=[blk, vec, vec, hb, hb, pl.BlockSpec((lp, DH), lambda b: (b, 0))],
        out_specs=[pl.BlockSpec((lp, BA_W), lambda b: (b, 0)), pl.BlockSpec((2, DH), lambda b: (0, 0))],
        out_shape=[jax.ShapeDtypeStruct((bsz * lp, BA_W), BF16), jax.ShapeDtypeStruct((2, DH), F32)],
        compiler_params=_cparams(("arbitrary",)),
    )(proj, alog_v, dtb_v, dbeta_b, dg_b, gb)


def _mix_heads(proj, o, gdn_norm, conv_sc, bsz, lp, padf):
    def body(o_ref, z_ref, gn_ref, sx_ref, sb_ref, sc_ref, w_ref, og_ref, os_ref):
        og_ref[...] = (_rms(o_ref[...], gn_ref[...])[0] * _silu(z_ref[...])).astype(BF16)
        conv = _conv_fwd(sc_ref[...] * sx_ref[...], w_ref[...], KS, lp)
        t = lax.broadcasted_iota(jnp.int32, (lp, 1), 0)
        os_ref[...] = jnp.where(t >= padf, sb_ref[...] * conv, 0.0).astype(BF16)

    half = _slab(lp, 0)
    return pl.pallas_call(
        body, name="mix_heads", grid=(bsz, HEADS),
        in_specs=[half, _slab(lp, Z0), pl.BlockSpec((1, DH), lambda b, s: (0, 0)),
                  _slab(lp, SX0), _slab(lp, SB0), _slab(lp, SC0), pl.BlockSpec((KS, DH), lambda b, s: (0, s))],
        out_specs=[half, half],
        out_shape=[jax.ShapeDtypeStruct((bsz * lp, GW), BF16)] * 2,
        compiler_params=_cparams(("parallel", "parallel")),
    )(o, proj, gdn_norm, proj, proj, proj, conv_sc)


def _mix_heads_bwd(proj, o, gdn_norm, conv_sc, dmixin, bsz, lp, padf):
    def body(o_ref, z_ref, gn_ref, sx_ref, sb_ref, sc_ref, w_ref, dg_ref, ds_ref,
             do_ref, dz_ref, dsx_ref, dsb_ref, dsc_ref, dgn_ref, dw_ref):
        s, b = pl.program_id(0), pl.program_id(1)
        t = lax.broadcasted_iota(jnp.int32, (lp, 1), 0)
        real = t >= padf
        o, z, gn, d = o_ref[...], z_ref[...], gn_ref[...], dg_ref[...]
        sz = _silu(z)
        on = _rms(o, gn)[0]
        dz_ref[...] = (d * on * _dsilu(z)).astype(BF16)
        dox, dgn = _rms_bwd(o, gn, d * sz)
        do_ref[...] = dox
        sx, sb, sc, w, e = sx_ref[...], sb_ref[...], sc_ref[...], w_ref[...], ds_ref[...]
        e = jnp.where(real, e, 0.0)
        xin = sc * sx
        conv = _conv_fwd(xin, w, KS, lp)
        dsb_ref[...] = (e * conv).astype(BF16)
        dxin, dw = _conv_bwd(xin, w, e * sb, KS, lp)
        dxin = jnp.where(real, dxin, 0.0)
        dsx_ref[...] = (dxin * sc).astype(BF16)
        dsc_ref[...] = (dxin * sx).astype(BF16)

        @pl.when(jnp.logical_and(s == 0, b == 0))
        def _():
            dgn_ref[...] = jnp.zeros_like(dgn_ref)

        @pl.when(b == 0)
        def _():
            dw_ref[...] = jnp.zeros_like(dw_ref)

        dgn_ref[...] += jnp.sum(dgn, axis=0, keepdims=True)
        dw_ref[...] += dw

    sl = lambda col0: pl.BlockSpec((lp, DH), lambda s, b: (b, col0 // DH + s))
    half = sl(0)
    return pl.pallas_call(
        body, name="mix_heads_bwd", grid=(HEADS, bsz),
        in_specs=[half, sl(Z0), pl.BlockSpec((1, DH), lambda s, b: (0, 0)), sl(SX0), sl(SB0), sl(SC0),
                  pl.BlockSpec((KS, DH), lambda s, b: (0, s)), sl(0), sl(GW)],
        out_specs=[half] * 5 + [pl.BlockSpec((1, DH), lambda s, b: (0, 0)), pl.BlockSpec((KS, DH), lambda s, b: (0, s))],
        out_shape=[jax.ShapeDtypeStruct((bsz * lp, GW), F32)] + [jax.ShapeDtypeStruct((bsz * lp, GW), BF16)] * 4
        + [jax.ShapeDtypeStruct((1, DH), F32), jax.ShapeDtypeStruct((KS, SW), F32)],
        compiler_params=_cparams(("arbitrary", "arbitrary")),
    )(o, proj, gdn_norm, proj, proj, proj, conv_sc, dmixin, dmixin)


def _inv_unit_lower(a):
    eye = (lax.broadcasted_iota(jnp.int32, (CH, CH), 0) == lax.broadcasted_iota(jnp.int32, (CH, CH), 1)).astype(F32)
    p = -a
    t = eye + p
    for _ in range(5):
        p = _fdot(p, p)
        t = t + _fdot(t, p)
    return t


def _chunk_common(q_ref, k_ref, v_ref, gc_ref, be_ref, gt_ref, n):
    r0 = pl.multiple_of(n * CH, CH)
    rows = pl.ds(r0, CH)
    q, k, v = q_ref[rows, :], k_ref[rows, :], v_ref[rows, :]
    gc = gc_ref[0, rows, :]
    be = be_ref[0, rows, :]
    gcl = gc_ref[0, pl.ds(r0 + CH - 1, 1), :]
    gt = gt_ref[0, pl.ds(n, 1), :]
    incl, strict = _tri(CH, "incl"), _tri(CH, "strict")
    dec = jnp.where(incl, jnp.exp(jnp.where(incl, gc[:, :CH] - gt, 0.0)), 0.0)
    eg = jnp.exp(gc)
    ekd = jnp.exp(gcl - gc)
    gl = jnp.exp(gcl)
    kb = k * be
    a = jnp.where(strict, _bdot(kb, k, NT) * dec, 0.0)
    tinv = _inv_unit_lower(a)
    vb = v * be
    kbg = kb * eg
    u = _bdot(tinv, vb, NN)
    w = _bdot(tinv, kbg, NN)
    qk = jnp.where(incl, _bdot(q, k, NT) * dec, 0.0)
    return dict(rows=rows, q=q, k=k, v=v, be=be, dec=dec, eg=eg, ekd=ekd, gl=gl, kb=kb, a=a, tinv=tinv,
                vb=vb, kbg=kbg, u=u, w=w, qk=qk, qd=q * eg, kd=k * ekd)


def _gdn_specs(lp, nchunk):
    head = lambda col0: pl.BlockSpec((lp, DH), lambda i: (i // HEADS, col0 // DH + i % HEADS))
    lane_b = pl.BlockSpec((1, lp, DH), lambda i: (i, 0, 0))
    row_t = pl.BlockSpec((1, nchunk, CH), lambda i: (i, 0, 0))
    states = pl.BlockSpec((1, nchunk, DH, DH), lambda i: (i, 0, 0, 0))
    return head, lane_b, row_t, states


def _gdn_fwd(qkv, gcb, bb, gct, bsz, lp):
    nchunk = lp // CH
    head, lane_b, row_t, states = _gdn_specs(lp, nchunk)

    def body(q_ref, k_ref, v_ref, gc_ref, be_ref, gt_ref, o_ref, st_ref):
        def step(n, s):
            c = _chunk_common(q_ref, k_ref, v_ref, gc_ref, be_ref, gt_ref, n)
            st_ref[0, n] = s
            vnew = c["u"] - _bdot(c["w"], s, NN)
            o_ref[c["rows"], :] = _bdot(c["qd"], s, NN) + _bdot(c["qk"], vnew, NN)
            return s * c["gl"] + _bdot(c["kd"], vnew, TN)

        lax.fori_loop(0, nchunk, step, jnp.zeros((DH, DH), F32))

    return pl.pallas_call(
        body, name="gdn_fwd", grid=(bsz * HEADS,),
        in_specs=[head(0), head(GW), head(2 * GW), lane_b, lane_b, row_t],
        out_specs=[head(0), states],
        out_shape=[jax.ShapeDtypeStruct((bsz * lp, GW), F32),
                   jax.ShapeDtypeStruct((bsz * HEADS, nchunk, DH, DH), F32)],
        compiler_params=_cparams(("parallel",)),
    )(qkv, qkv, qkv, gcb, bb, gct)


def _gdn_bwd(qkv, gcb, bb, gct, states_all, d_o, bsz, lp):
    nchunk = lp // CH
    head, lane_b, row_t, states = _gdn_specs(lp, nchunk)

    def body(q_ref, k_ref, v_ref, gc_ref, be_ref, gt_ref, st_ref, do_ref,
             dq_ref, dk_ref, dv_ref, dg_ref, db_ref):
        incl, strict = _tri(CH, "incl"), _tri(CH, "strict")
        upper = _tri(CH, "upper").astype(F32)
        ones = jnp.ones((CH, DH), F32)
        last = lax.broadcasted_iota(jnp.int32, (CH, 1), 0) == CH - 1

        def rsum(x):
            return jnp.sum(x, axis=-1, keepdims=True)

        def step(i, ds):
            n = nchunk - 1 - i
            c = _chunk_common(q_ref, k_ref, v_ref, gc_ref, be_ref, gt_ref, n)
            rows, q, k, v, be, dec, eg, ekd, gl = c["rows"], c["q"], c["k"], c["v"], c["be"], c["dec"], c["eg"], c["ekd"], c["gl"]
            s = st_ref[0, n]
            do = do_ref[rows, :]
            vnew = c["u"] - _bdot(c["w"], s, NN)
            dvn = _bdot(c["qk"], do, TN) + _bdot(c["kd"], ds, NN)
            dqk = jnp.where(incl, _bdot(do, vnew, NT), 0.0)
            dqd = _bdot(do, s, NT)
            dkd = _bdot(vnew, ds, NT)
            dgl = jnp.sum(rsum(s * ds), axis=0, keepdims=True)
            dw = -_bdot(dvn, s, NT)
            ds_new = ds * gl + _bdot(c["qd"], do, TN) - _bdot(c["w"], dvn, TN)
            tinv = c["tinv"]
            dvb = _bdot(tinv, dvn, TN)
            dkbg = _bdot(tinv, dw, TN)
            dt = _bdot(dvn, c["vb"], NT) + _bdot(dw, c["kbg"], NT)
            da = jnp.where(strict, -_fdot(_fdot(tinv, dt, TN), tinv, NT), 0.0)
            dp = da * dec
            dr = dqk * dec
            dkb = _bdot(dp, k, NN) + dkbg * eg
            dk = _bdot(dp, c["kb"], TN) + _bdot(dr, q, TN) + dkd * ekd + dkb * be
            dq = _bdot(dr, k, NN) + dqd * eg
            m = da * c["a"] + dqk * c["qk"]
            m_rows = _fdot(m, ones[:CH, :], NN)
            m_cols = _fdot(m, ones[:CH, :], TN)
            kd_term = rsum(dkd * c["kd"])
            dgc = m_rows - m_cols + rsum(dqd * c["qd"]) - kd_term + rsum(dkbg * c["kbg"])
            tail = jnp.sum(kd_term, axis=0, keepdims=True) + dgl * gl
            dgc = dgc + jnp.where(last, tail, 0.0)
            dq_ref[rows, :] = dq
            dk_ref[rows, :] = dk
            dv_ref[rows, :] = dvb * be
            dg_ref[0, rows, :] = _fdot(upper, dgc, NN)
            db_ref[0, rows, :] = jnp.broadcast_to(rsum(dkb * k) + rsum(dvb * v), (CH, DH))
            return ds_new

        lax.fori_loop(0, nchunk, step, jnp.zeros((DH, DH), F32))

    r = bsz * lp
    return pl.pallas_call(
        body, name="gdn_bwd", grid=(bsz * HEADS,),
        in_specs=[head(0), head(GW), head(2 * GW), lane_b, lane_b, row_t, states, head(0)],
        out_specs=[head(0), head(0), head(0), lane_b, lane_b],
        out_shape=[jax.ShapeDtypeStruct((r, GW), F32)] * 3
        + [jax.ShapeDtypeStruct((bsz * HEADS, lp, DH), F32)] * 2,
        compiler_params=_cparams(("parallel",)),
    )(qkv, qkv, qkv, gcb, bb, gct, states_all, d_o)


def _place():
    return lax.axis_index("x"), lax.axis_index("y"), lax.axis_index("c")


def _any_specs(n):
    return [pl.BlockSpec(memory_space=pl.ANY)] * n


def _all_gather(name, shards):
    n = len(shards)

    def body(*refs):
        ins, outs = refs[:n], refs[n:2 * n]
        send_sems, recv_sems, local_sems = refs[2 * n:]
        x, y, c = _place()
        me, sibling = (x, y, c), (x, y, 1 - c)
        chips = [(1 - x, y), (x, 1 - y), (1 - x, 1 - y)]

        def slot(i, p):
            return outs[i].at[4 * p[0] + 2 * p[1] + p[2]]

        def copy(i, k, block, to, src=None):
            return pltpu.make_async_remote_copy(
                src_ref=slot(i, block) if src is None else src, dst_ref=slot(i, block),
                send_sem=send_sems.at[i, k], recv_sem=recv_sems.at[i, k], device_id=to, device_id_type=MESH)

        mine = [pltpu.make_async_copy(ins[i], slot(i, me), local_sems.at[i]) for i in range(n)]
        for cp in mine:
            cp.start()
        first = []
        for i in range(n):
            first.append(copy(i, 0, me, sibling, src=ins[i]))
            first += [copy(i, 1 + j, me, (*chip, c), src=ins[i]) for j, chip in enumerate(chips)]
        for cp in first:
            cp.start()
        passed = []
        for j, chip in enumerate(chips):
            for i in range(n):
                copy(i, 1 + j, (*chip, c), me).wait_recv()
                fwd = copy(i, 4 + j, (*chip, c), sibling)
                fwd.start()
                passed.append(fwd)
        for i in range(n):
            copy(i, 0, sibling, me).wait_recv()
            for j, chip in enumerate(chips):
                copy(i, 4 + j, (*chip, 1 - c), me).wait_recv()
        for cp in first + passed:
            cp.wait_send()
        for cp in mine:
            cp.wait()

    return pl.pallas_call(
        body, name=name,
        in_specs=_any_specs(n), out_specs=_any_specs(n),
        out_shape=[jax.ShapeDtypeStruct((N_DEV,) + s.shape, s.dtype) for s in shards],
        scratch_shapes=[pltpu.SemaphoreType.DMA((n, 7)), pltpu.SemaphoreType.DMA((n, 7)),
                        pltpu.SemaphoreType.DMA((n,))],
    )(*shards)


def _sibling_exchange(name, grads):
    n = len(grads)

    def body(*refs):
        ins, outs = refs[:n], refs[n:2 * n]
        send_sems, recv_sems = refs[2 * n:]
        x, y, c = _place()
        cps = [pltpu.make_async_remote_copy(
            src_ref=ins[i].at[1 - c], dst_ref=outs[i], send_sem=send_sems.at[i], recv_sem=recv_sems.at[i],
            device_id=(x, y, 1 - c), device_id_type=MESH) for i in range(n)]
        for cp in cps:
            cp.start()
        for cp in cps:
            cp.wait_recv()
        for cp in cps:
            cp.wait_send()

    return pl.pallas_call(
        body, name=name,
        in_specs=_any_specs(n), out_specs=_any_specs(n),
        out_shape=[jax.ShapeDtypeStruct(g.shape[1:], g.dtype) for g in grads],
        scratch_shapes=[pltpu.SemaphoreType.DMA((n,)), pltpu.SemaphoreType.DMA((n,))],
    )(*grads)


def _chip_exchange(name, parts):
    n = len(parts)

    def body(*refs):
        ins, outs = refs[:n], refs[n:2 * n]
        send_sems, recv_sems, local_sems = refs[2 * n:]
        x, y, c = _place()
        chips = [(1 - x, y), (x, 1 - y), (1 - x, 1 - y)]
        mine = [pltpu.make_async_copy(ins[i].at[2 * x + y], outs[i].at[0], local_sems.at[i]) for i in range(n)]
        for cp in mine:
            cp.start()
        cps = []
        for j, chip in enumerate(chips):
            for i in range(n):
                cps.append(pltpu.make_async_remote_copy(
                    src_ref=ins[i].at[2 * chip[0] + chip[1]], dst_ref=outs[i].at[1 + j],
                    send_sem=send_sems.at[i, j], recv_sem=recv_sems.at[i, j],
                    device_id=(*chip, c), device_id_type=MESH))
        for cp in cps:
            cp.start()
        for cp in cps:
            cp.wait_recv()
        for cp in cps:
            cp.wait_send()
        for cp in mine:
            cp.wait()

    return pl.pallas_call(
        body, name=name,
        in_specs=_any_specs(n), out_specs=_any_specs(n),
        out_shape=[jax.ShapeDtypeStruct(p.shape, p.dtype) for p in parts],
        scratch_shapes=[pltpu.SemaphoreType.DMA((n, 3)), pltpu.SemaphoreType.DMA((n, 3)),
                        pltpu.SemaphoreType.DMA((n,))],
    )(*parts)


def _pair_sum(name, grads, recv, core):
    _, _, r, c = grads.shape
    tr = _div_tile(r, 256, 8)

    def body(core_ref, g_ref, r_ref, o_ref):
        o_ref[...] = g_ref[0] + r_ref[...]

    return pl.pallas_call(
        body, name=name,
        grid_spec=pltpu.PrefetchScalarGridSpec(
            num_scalar_prefetch=1, grid=(4, r // tr),
            in_specs=[pl.BlockSpec((1, 1, tr, c), lambda s, i, cr: (cr[0], s, i, 0)),
                      pl.BlockSpec((1, tr, c), lambda s, i, cr: (s, i, 0))],
            out_specs=pl.BlockSpec((1, tr, c), lambda s, i, cr: (s, i, 0))),
        out_shape=jax.ShapeDtypeStruct(recv.shape, F32),
        compiler_params=_cparams(("parallel", "parallel")),
    )(core, grads, recv)


def _adamw_math(w, g, m, v):
    m = ADAM_B1 * m + (1.0 - ADAM_B1) * g
    v = ADAM_B2 * v + (1.0 - ADAM_B2) * (g * g)
    m_hat = m / (1.0 - ADAM_B1 ** ADAM_STEP)
    v_hat = v / (1.0 - ADAM_B2 ** ADAM_STEP)
    delta = -ADAM_LR * (m_hat / (jnp.sqrt(v_hat) + ADAM_EPS) + ADAM_WD * w)
    return delta, m, v


def _adamw(name, parts, w, m, v):
    nslot, r, c = parts.shape
    tr = _div_tile(r, 256, 8) if r % 8 == 0 else r

    def body(p_ref, w_ref, m_ref, v_ref, g_out, d_out, m_out, v_out):
        g = p_ref[0]
        for s in range(1, nslot):
            g = g + p_ref[s]
        d, mn, vn = _adamw_math(w_ref[...], g, m_ref[...], v_ref[...])
        g_out[...] = g
        d_out[...] = d
        m_out[...] = mn
        v_out[...] = vn

    blk = pl.BlockSpec((tr, c), lambda i: (i, 0))
    return pl.pallas_call(
        body, name=name, grid=(r // tr,),
        in_specs=[pl.BlockSpec((nslot, tr, c), lambda i: (0, i, 0)), blk, blk, blk],
        out_specs=[blk] * 4,
        out_shape=[jax.ShapeDtypeStruct((r, c), F32)] * 4,
        compiler_params=_cparams(("parallel",)),
    )(parts, w, m, v)


SMALL_ROWS, SMALL_COLS = 32, 3 * GW


def _pack_small(gains4, meta, conv_qkv, conv_sc, a_log, dt_bias, gdn_norm):
    def padc(a):
        return jnp.pad(a, ((0, 0), (0, SMALL_COLS - a.shape[1])))
    scal = jnp.concatenate([jnp.pad(a_log, ((0, 0), (0, DH - HEADS))), jnp.pad(dt_bias, ((0, 0), (0, DH - HEADS))),
                            gdn_norm], axis=1)
    rows = [padc(g) for g in gains4] + [padc(meta), padc(conv_qkv), padc(conv_sc), padc(scal)]
    slab = jnp.concatenate(rows, axis=0)
    return jnp.pad(slab, ((0, SMALL_ROWS - slab.shape[0]), (0, 0)))


def _unpack_small(slab):
    gains = [slab[i:i + 1, :D] for i in range(4)]
    meta = slab[4:20, :D]
    conv_qkv = slab[20:24, :]
    conv_sc = slab[24:27, :SW]
    a_log = slab[27:28, 0:HEADS]
    dt_bias = slab[27:28, DH:DH + HEADS]
    gdn_norm = slab[27:28, 2 * DH:3 * DH]
    return gains, meta, conv_qkv, conv_sc, a_log, dt_bias, gdn_norm


def kernel(x, meta_tokens, mix_pre_norm, mix_post_norm, ffn_pre_norm, ffn_post_norm, w_in, conv_qkv, a_log, dt_bias, gdn_norm, conv_sc, w_out, w_gate, w_up, w_down, loss_target, m_meta_tokens, m_mix_pre_norm, m_mix_post_norm, m_ffn_pre_norm, m_ffn_post_norm, m_w_in, m_conv_qkv, m_a_log, m_dt_bias, m_gdn_norm, m_conv_sc, m_w_out, m_w_gate, m_w_up, m_w_down, v_meta_tokens, v_mix_pre_norm, v_mix_post_norm, v_ffn_pre_norm, v_ffn_post_norm, v_w_in, v_conv_qkv, v_a_log, v_dt_bias, v_gdn_norm, v_conv_sc, v_w_out, v_w_gate, v_w_up, v_w_down):
    bsz, seq, _ = x.shape
    lp = -(-(N_META + seq) // CH) * CH
    padf = lp - N_META - seq
    first_real = padf + N_META
    rows = bsz * lp
    nchunk = lp // CH
    tr = _div_tile(lp, 528)
    tm = _div_tile(rows, 1056)
    cx, cy, cc = _place()
    dev = 4 * cx + 2 * cy + cc

    small_shard = jnp.concatenate([
        jnp.pad(meta_tokens, ((0, 0), (0, 256 - DH))),
        jnp.pad(conv_qkv[0], ((0, 0), (0, 256 - 3 * GW // N_DEV))),
        jnp.pad(conv_sc[0], ((0, 5), (0, 256 - SW // N_DEV)))], axis=0)
    g_in, g_out, g_gate, g_up, g_down, g_small = _all_gather(
        "gather_weights",
        [w_in[0].astype(BF16), w_out[0].astype(BF16), w_gate[0].astype(BF16), w_up[0].astype(BF16),
         w_down[0].astype(BF16), small_shard])

    def cols_full(g):
        return jnp.transpose(g, (1, 0, 2)).reshape(g.shape[1], -1)

    win = cols_full(g_in)
    win = jnp.concatenate([win[:, :BA0 + 2 * HEADS], jnp.zeros((D, BA_W - 2 * HEADS), BF16),
                           win[:, BA0 + 2 * HEADS:]], axis=1)
    wout = g_out.reshape(D, D)
    wgate, wup = cols_full(g_gate), cols_full(g_up)
    wdown = g_down.reshape(FF, D)
    meta_full = cols_full(g_small[:, 0:16, :DH])
    convq_full = cols_full(g_small[:, 16:20, :3 * GW // N_DEV])
    convs_full = cols_full(g_small[:, 20:23, :SW // N_DEV])
    alog_v = jnp.pad(a_log, ((0, 0), (HEADS, DH - 2 * HEADS)))
    dtb_v = jnp.pad(dt_bias, ((0, 0), (HEADS, DH - 2 * HEADS)))

    h0 = jnp.concatenate([jnp.zeros((bsz, padf, D), F32), jnp.broadcast_to(meta_full[None], (bsz, N_META, D)), x],
                         axis=1).reshape(rows, D)
    tgt = jnp.pad(loss_target, ((0, 0), (first_real, 0), (0, 0))).reshape(rows, D)
    u1 = _pre_norm(h0, mix_pre_norm, bsz, lp, tr)
    (proj,) = _mm("proj_in", [(u1, win)], "nn", tm, 768, D)
    qkv = _gdn_prep(proj, convq_full, bsz, lp, padf)
    gb, gcs = _gates(proj, alog_v, dtb_v, bsz, lp, padf)

    def lane_bcast(cols):
        t = jnp.transpose(cols.reshape(bsz, lp, HEADS), (0, 2, 1)).reshape(bsz * HEADS, lp, 1)
        return jnp.broadcast_to(t, (bsz * HEADS, lp, DH))

    bb = lane_bcast(gb[:, 0:HEADS])
    gcb = lane_bcast(gcs[:, HEADS:2 * HEADS])
    gct = jnp.transpose(gcs[:, HEADS:2 * HEADS].reshape(bsz, nchunk, CH, HEADS), (0, 3, 1, 2)).reshape(
        bsz * HEADS, nchunk, CH)
    o, states = _gdn_fwd(qkv, gcb, bb, gct, bsz, lp)
    mix_g, mix_s = _mix_heads(proj, o, gdn_norm, convs_full, bsz, lp, padf)
    mixin = jnp.concatenate([mix_g, mix_s], axis=1)
    (mix,) = _mm("proj_out", [(mixin, wout)], "nn", tm, 512, D)
    h1, u2 = _mid_norms(h0, mix, mix_post_norm, ffn_pre_norm, bsz, lp, tr)
    gate, up, act = _ffn_up(u2, wgate, wup, _div_tile(rows, 528), 1408)
    (ffn,) = _mm("ffn_down", [(act, wdown)], "nn", tm, 512, 1408)

    dy, dffn, d_g4, loss_part = _loss_head(h1, ffn, tgt, ffn_post_norm, bsz, lp, tr, first_real)
    loss = lax.psum(loss_part[0, 0], ("x", "y", "c"))
    (dwdown,) = _mm("dw_down", [(act, dffn)], "tn", 1408, 512, tm)

    def swiglu_bwd(da, g, u):
        return da * u * _dsilu(g), da * _silu(g)

    dgate, dup = _mm("d_act", [(dffn, wdown)], "nt", tm, 1408, D, out_dtypes=(BF16, BF16),
                     epilogue=swiglu_bwd, extras=(gate, up))
    (dwgate,) = _mm("dw_gate", [(u2, dgate)], "tn", D, 1408, tm)
    (dwup,) = _mm("dw_up", [(u2, dup)], "tn", D, 1408, tm)
    (du2,) = _mm("d_u2", [(dgate, wgate), (dup, wup)], "nt", tm, 512, 1408)
    dh1, dmix, d_g3, d_g2 = _mid_norms_bwd(h1, mix, du2, dy, ffn_pre_norm, mix_post_norm, bsz, lp, tr)
    (dwout,) = _mm("dw_out", [(mixin, dmix)], "tn", D, 512, tm)
    (dmixin,) = _mm("d_mixin", [(dmix, wout)], "nt", tm, 512, D)
    d_o, dz, dsx, dsb, dsc, d_gn, d_convs = _mix_heads_bwd(proj, o, gdn_norm, convs_full, dmixin, bsz, lp, padf)
    dq, dk, dv, dg_b, dbeta_b = _gdn_bwd(qkv, gcb, bb, gct, states, d_o, bsz, lp)
    dqkv = jnp.concatenate([dq, dk, dv], axis=1)
    dpqkv, d_convq = _gdn_prep_bwd(proj, convq_full, dqkv, bsz, lp, padf)
    dba, d_scal = _gates_bwd(proj, alog_v, dtb_v, dbeta_b, dg_b, gb, bsz, lp, padf)
    dproj = jnp.concatenate([dpqkv, dz, dba, dsx, dsb, dsc], axis=1)
    (dwin,) = _mm("dw_in", [(u1, dproj)], "tn", D, 768, tm)
    (du1,) = _mm("d_u1", [(dproj, win)], "nt", tm, 512, 768)
    dh0, d_g1 = _pre_norm_bwd(h0, du1, dh1, mix_pre_norm, bsz, lp, tr)
    dh0 = dh0.reshape(bsz, lp, D)
    grad_x = dh0[:, first_real:]
    d_meta = jnp.sum(dh0[:, padf:first_real], axis=0)

    dwin = jnp.concatenate([dwin[:, :BA0 + 2 * HEADS], dwin[:, SX0:]], axis=1)

    def col_blocks(g):
        r = g.shape[0]
        return jnp.transpose(g.reshape(r, 4, 2, -1), (2, 1, 0, 3))

    def row_blocks(g):
        c = g.shape[1]
        return jnp.transpose(g.reshape(4, 2, -1, c), (1, 0, 2, 3))

    big = [col_blocks(dwin), row_blocks(dwout), col_blocks(dwgate), col_blocks(dwup), row_blocks(dwdown)]
    from_sibling = _sibling_exchange("grads_to_sibling", big)
    core = jnp.reshape(cc, (1,)).astype(jnp.int32)
    names = ["w_in", "w_out", "w_gate", "w_up", "w_down"]
    chip_parts = [_pair_sum("pair_sum_" + nm, g, r, core) for nm, g, r in zip(names, big, from_sibling)]
    reduced = _chip_exchange("grads_across_chips", chip_parts)

    d_alog = d_scal[0:1, HEADS:2 * HEADS]
    d_dtb = d_scal[1:2, HEADS:2 * HEADS]
    slab = _pack_small([d_g1, d_g2, d_g3, d_g4], d_meta, d_convq, d_convs, d_alog, d_dtb, d_gn)
    (slabs,) = _all_gather("gather_small_grads", [slab])
    off_meta = dev * DH
    off_cq = dev * (3 * GW // N_DEV)
    off_cs = dev * (SW // N_DEV)

    def shard_of(s):
        gains, meta, cq, cs, al, db, gn = _unpack_small(s)
        return gains, lax.dynamic_slice_in_dim(meta, off_meta, DH, 1), \
            lax.dynamic_slice_in_dim(cq, off_cq, 3 * GW // N_DEV, 1), \
            lax.dynamic_slice_in_dim(cs, off_cs, SW // N_DEV, 1), al, db, gn

    slabs_local = jax.vmap(lambda s: _pack_small(*shard_of(s)))(slabs)
    w_small = _pack_small([mix_pre_norm, mix_post_norm, ffn_pre_norm, ffn_post_norm], meta_tokens, conv_qkv[0],
                          conv_sc[0], a_log, dt_bias, gdn_norm)
    m_small = _pack_small([m_mix_pre_norm, m_mix_post_norm, m_ffn_pre_norm, m_ffn_post_norm], m_meta_tokens,
                          m_conv_qkv[0], m_conv_sc[0], m_a_log, m_dt_bias, m_gdn_norm)
    v_small = _pack_small([v_mix_pre_norm, v_mix_post_norm, v_ffn_pre_norm, v_ffn_post_norm], v_meta_tokens,
                          v_conv_qkv[0], v_conv_sc[0], v_a_log, v_dt_bias, v_gdn_norm)
    small_out = _adamw("adamw_small", slabs_local, w_small, m_small, v_small)

    big_w = [(w_in, m_w_in, v_w_in), (w_out, m_w_out, v_w_out), (w_gate, m_w_gate, v_w_gate),
             (w_up, m_w_up, v_w_up), (w_down, m_w_down, v_w_down)]
    big_out = [_adamw("adamw_" + nm, parts, w[0], m[0], v[0]) for nm, parts, (w, m, v) in zip(names, reduced, big_w)]

    def small_leaves(slab_out):
        gains, meta, cq, cs, al, db, gn = _unpack_small(slab_out)
        return {"meta_tokens": meta[:, :DH], "mix_pre_norm": gains[0], "mix_post_norm": gains[1],
                "ffn_pre_norm": gains[2], "ffn_post_norm": gains[3], "conv_qkv": cq[None, :, :3 * GW // N_DEV],
                "a_log": al, "dt_bias": db, "gdn_norm": gn, "conv_sc": cs[None, :, :SW // N_DEV]}

    order = ["meta_tokens", "mix_pre_norm", "mix_post_norm", "ffn_pre_norm", "ffn_post_norm", "w_in", "conv_qkv",
             "a_log", "dt_bias", "gdn_norm", "conv_sc", "w_out", "w_gate", "w_up", "w_down"]
    outs = [loss, grad_x]
    for kind in range(4):
        leaves = small_leaves(small_out[kind])
        for nm, res in zip(names, big_out):
            leaves[nm] = res[kind][None]
        outs += [leaves[nm] for nm in order]
    return tuple(outs)
```

```python
import functools

import jax
import jax.numpy as jnp
from jax import lax
from jax.experimental import pallas as pl
from jax.experimental.pallas import tpu as pltpu

F32 = jnp.float32
BF16 = jnp.bfloat16
HIGHEST = lax.Precision.HIGHEST
MESH = pl.DeviceIdType.MESH

D = 1024
N_META = 16
HEADS = 4
DH = 128
GW = HEADS * DH
SW = D - GW
FF = 2816
CH = 64
KQ = 4
KS = 3
EPS = 1e-6
IN_W = 3 * GW + GW + 2 * HEADS + 3 * SW
Q0, K0, V0, Z0, BA0, SX0, SB0, SC0, PW = 0, 512, 1024, 1536, 2048, 2304, 2816, 3328, 3840
BA_W = SX0 - BA0
N_DEV = 8
VMEM_LIMIT = 56 * 1024 * 1024

ADAM_LR, ADAM_B1, ADAM_B2, ADAM_EPS, ADAM_WD, ADAM_STEP = 0.001, 0.9, 0.999, 1e-08, 0.01, 10


def _cparams(sem):
    return pltpu.CompilerParams(dimension_semantics=sem, vmem_limit_bytes=VMEM_LIMIT)


def _div_tile(n, target, mult=16):
    best = None
    for t in range(mult, min(n, target) + 1, mult):
        if n % t == 0:
            best = t
    assert best is not None, (n, target)
    return best


def _silu(x):
    return x * jax.nn.sigmoid(x)


def _dsilu(x):
    s = jax.nn.sigmoid(x)
    return s * (1.0 + x * (1.0 - s))


def _bdot(a, b, dims):
    return lax.dot_general(a.astype(BF16), b.astype(BF16), (dims, ((), ())), preferred_element_type=F32)


NN = ((1,), (0,))
NT = ((1,), (1,))
TN = ((0,), (0,))


def _fdot(a, b, dims=NN):
    return lax.dot_general(a, b, (dims, ((), ())), precision=HIGHEST, preferred_element_type=F32)


def _mm(name, pairs, mode, tm, tn, tk, out_dtypes=(F32,), epilogue=None, extras=()):
    a0, b0 = pairs[0]
    if mode == "nn":
        (m, k), n = a0.shape, b0.shape[1]
    elif mode == "nt":
        (m, k), n = a0.shape, b0.shape[0]
    else:
        (k, m), n = a0.shape, b0.shape[1]
    assert m % tm == 0 and n % tn == 0 and k % tk == 0, (name, m, n, k, tm, tn, tk)
    nk = k // tk
    npair = len(pairs)
    nex = len(extras)
    nout = len(out_dtypes)
    dims = {"nn": NN, "nt": NT, "tn": TN}[mode]

    def body(*refs):
        ab = refs[:2 * npair]
        ex = refs[2 * npair:2 * npair + nex]
        outs = refs[2 * npair + nex:2 * npair + nex + nout]
        acc_ref = refs[-1]
        kk = pl.program_id(2)

        part = None
        for p in range(npair):
            d = lax.dot_general(ab[2 * p][...], ab[2 * p + 1][...], (dims, ((), ())), preferred_element_type=F32)
            part = d if part is None else part + d

        def finish(acc):
            res = epilogue(acc, *[e[...] for e in ex]) if epilogue is not None else (acc,)
            for o, r in zip(outs, res):
                o[...] = r.astype(o.dtype)

        if nk == 1:
            finish(part)
        else:
            @pl.when(kk == 0)
            def _():
                acc_ref[...] = part

            @pl.when(kk > 0)
            def _():
                acc_ref[...] += part

            @pl.when(kk == nk - 1)
            def _():
                finish(acc_ref[...])

    if mode == "nn":
        a_spec = pl.BlockSpec((tm, tk), lambda i, j, q: (i, q))
        b_spec = pl.BlockSpec((tk, tn), lambda i, j, q: (q, j))
    elif mode == "nt":
        a_spec = pl.BlockSpec((tm, tk), lambda i, j, q: (i, q))
        b_spec = pl.BlockSpec((tn, tk), lambda i, j, q: (j, q))
    else:
        a_spec = pl.BlockSpec((tk, tm), lambda i, j, q: (q, i))
        b_spec = pl.BlockSpec((tk, tn), lambda i, j, q: (q, j))
    o_spec = pl.BlockSpec((tm, tn), lambda i, j, q: (i, j))
    flat = [t for pr in pairs for t in pr]
    res = pl.pallas_call(
        body, name=name,
        grid=(m // tm, n // tn, nk),
        in_specs=[a_spec, b_spec] * npair + [o_spec] * nex,
        out_specs=[o_spec] * nout,
        out_shape=[jax.ShapeDtypeStruct((m, n), dt) for dt in out_dtypes],
        scratch_shapes=[pltpu.VMEM((tm, tn), F32)],
        compiler_params=_cparams(("parallel", "parallel", "arbitrary")),
    )(*flat, *extras)
    return res


def _ffn_up(u2, wg, wu, tm, tn):
    m, k = u2.shape
    n = wg.shape[1]

    def body(a_ref, g_ref, u_ref, gate_ref, up_ref, act_ref):
        a = a_ref[...]
        g = jnp.dot(a, g_ref[...], preferred_element_type=F32)
        u = jnp.dot(a, u_ref[...], preferred_element_type=F32)
        gate_ref[...] = g
        up_ref[...] = u
        act_ref[...] = (_silu(g) * u).astype(BF16)

    o_spec = pl.BlockSpec((tm, tn), lambda i, j: (i, j))
    w_spec = pl.BlockSpec((k, tn), lambda i, j: (0, j))
    return pl.pallas_call(
        body, name="ffn_up", grid=(m // tm, n // tn),
        in_specs=[pl.BlockSpec((tm, k), lambda i, j: (i, 0)), w_spec, w_spec],
        out_specs=[o_spec, o_spec, o_spec],
        out_shape=[jax.ShapeDtypeStruct((m, n), F32), jax.ShapeDtypeStruct((m, n), F32),
                   jax.ShapeDtypeStruct((m, n), BF16)],
        compiler_params=_cparams(("parallel", "parallel")),
    )(u2, wg, wu)


def _rms(x, gain):
    r = lax.rsqrt(jnp.mean(x * x, axis=-1, keepdims=True) + EPS)
    return x * r * gain, r


def _rms_bwd(x, gain, dy):
    r = lax.rsqrt(jnp.mean(x * x, axis=-1, keepdims=True) + EPS)
    dyw = dy * gain
    dx = r * dyw - x * (r * r * r) * jnp.mean(dyw * x, axis=-1, keepdims=True)
    return dx, dy * x * r


def _row_specs(tr, width, nb):
    return pl.BlockSpec((tr, width), lambda b, j: (b * nb + j, 0))


def _vec_spec(width):
    return pl.BlockSpec((1, width), lambda b, j: (0, 0))


def _first_step(b, j):
    return jnp.logical_and(b == 0, j == 0)


def _pre_norm(h0, gain, bsz, lp, tr):
    nb = lp // tr

    def body(h_ref, g_ref, u_ref):
        u_ref[...] = _rms(h_ref[...], g_ref[...])[0].astype(BF16)

    return pl.pallas_call(
        body, name="mix_pre_norm", grid=(bsz, nb),
        in_specs=[_row_specs(tr, D, nb), _vec_spec(D)],
        out_specs=_row_specs(tr, D, nb),
        out_shape=jax.ShapeDtypeStruct(h0.shape, BF16),
        compiler_params=_cparams(("parallel", "parallel")),
    )(h0, gain)


def _mid_norms(h0, mix, g_post, g_pre, bsz, lp, tr):
    nb = lp // tr

    def body(h_ref, m_ref, gp_ref, gq_ref, h1_ref, u2_ref):
        h1 = h_ref[...] + _rms(m_ref[...], gp_ref[...])[0]
        h1_ref[...] = h1
        u2_ref[...] = _rms(h1, gq_ref[...])[0].astype(BF16)

    rs = _row_specs(tr, D, nb)
    return pl.pallas_call(
        body, name="mid_norms", grid=(bsz, nb),
        in_specs=[rs, rs, _vec_spec(D), _vec_spec(D)],
        out_specs=[rs, rs],
        out_shape=[jax.ShapeDtypeStruct(h0.shape, F32), jax.ShapeDtypeStruct(h0.shape, BF16)],
        compiler_params=_cparams(("parallel", "parallel")),
    )(h0, mix, g_post, g_pre)


def _loss_head(h1, ffn, tgt, g_post, bsz, lp, tr, first_real):
    nb = lp // tr

    def body(h_ref, f_ref, t_ref, g_ref, dy_ref, dffn_ref, dg_ref, loss_ref):
        b, j = pl.program_id(0), pl.program_id(1)
        f = f_ref[...]
        g = g_ref[...]
        y = h_ref[...] + _rms(f, g)[0]
        t = j * tr + lax.broadcasted_iota(jnp.int32, (tr, 1), 0)
        err = jnp.where(t >= first_real, y - t_ref[...], 0.0)
        dy = err * (1.0 / D)
        dy_ref[...] = dy
        dx, dgc = _rms_bwd(f, g, dy)
        dffn_ref[...] = dx.astype(BF16)
        lsum = jnp.sum(jnp.sum(err * err, axis=0, keepdims=True), axis=1, keepdims=True) * (0.5 / D)

        @pl.when(_first_step(b, j))
        def _():
            dg_ref[...] = jnp.zeros_like(dg_ref)
            loss_ref[...] = jnp.zeros_like(loss_ref)

        dg_ref[...] += jnp.sum(dgc, axis=0, keepdims=True)
        loss_ref[...] += jnp.broadcast_to(lsum, loss_ref.shape)

    rs = _row_specs(tr, D, nb)
    return pl.pallas_call(
        body, name="loss_head", grid=(bsz, nb),
        in_specs=[rs, rs, rs, _vec_spec(D)],
        out_specs=[rs, rs, _vec_spec(D), _vec_spec(DH)],
        out_shape=[jax.ShapeDtypeStruct(h1.shape, F32), jax.ShapeDtypeStruct(h1.shape, BF16),
                   jax.ShapeDtypeStruct((1, D), F32), jax.ShapeDtypeStruct((1, DH), F32)],
        compiler_params=_cparams(("arbitrary", "arbitrary")),
    )(h1, ffn, tgt, g_post)


def _mid_norms_bwd(h1, mix, du2, dy, g_pre, g_post, bsz, lp, tr):
    nb = lp // tr

    def body(h_ref, m_ref, du_ref, dy_ref, gq_ref, gp_ref, dh1_ref, dmix_ref, dgq_ref, dgp_ref):
        b, j = pl.program_id(0), pl.program_id(1)
        dx, dgq = _rms_bwd(h_ref[...], gq_ref[...], du_ref[...])
        dh1 = dy_ref[...] + dx
        dh1_ref[...] = dh1
        dm, dgp = _rms_bwd(m_ref[...], gp_ref[...], dh1)
        dmix_ref[...] = dm.astype(BF16)

        @pl.when(_first_step(b, j))
        def _():
            dgq_ref[...] = jnp.zeros_like(dgq_ref)
            dgp_ref[...] = jnp.zeros_like(dgp_ref)

        dgq_ref[...] += jnp.sum(dgq, axis=0, keepdims=True)
        dgp_ref[...] += jnp.sum(dgp, axis=0, keepdims=True)

    rs = _row_specs(tr, D, nb)
    return pl.pallas_call(
        body, name="mid_norms_bwd", grid=(bsz, nb),
        in_specs=[rs, rs, rs, rs, _vec_spec(D), _vec_spec(D)],
        out_specs=[rs, rs, _vec_spec(D), _vec_spec(D)],
        out_shape=[jax.ShapeDtypeStruct(h1.shape, F32), jax.ShapeDtypeStruct(h1.shape, BF16),
                   jax.ShapeDtypeStruct((1, D), F32), jax.ShapeDtypeStruct((1, D), F32)],
        compiler_params=_cparams(("arbitrary", "arbitrary")),
    )(h1, mix, du2, dy, g_pre, g_post)


def _pre_norm_bwd(h0, du1, dh1, gain, bsz, lp, tr):
    nb = lp // tr

    def body(h_ref, du_ref, dh1_ref, g_ref, dh0_ref, dg_ref):
        b, j = pl.program_id(0), pl.program_id(1)
        dx, dgc = _rms_bwd(h_ref[...], g_ref[...], du_ref[...])
        dh0_ref[...] = dh1_ref[...] + dx

        @pl.when(_first_step(b, j))
        def _():
            dg_ref[...] = jnp.zeros_like(dg_ref)

        dg_ref[...] += jnp.sum(dgc, axis=0, keepdims=True)

    rs = _row_specs(tr, D, nb)
    return pl.pallas_call(
        body, name="pre_norm_bwd", grid=(bsz, nb),
        in_specs=[rs, rs, rs, _vec_spec(D)],
        out_specs=[rs, _vec_spec(D)],
        out_shape=[jax.ShapeDtypeStruct(h0.shape, F32), jax.ShapeDtypeStruct((1, D), F32)],
        compiler_params=_cparams(("arbitrary", "arbitrary")),
    )(h0, du1, dh1, gain)


def _shift_down(x, s, lp):
    return x if s == 0 else pltpu.roll(x, s, axis=0)


def _shift_up(x, s, lp):
    return x if s == 0 else pltpu.roll(x, lp - s, axis=0)


def _conv_fwd(x, w, taps, lp):
    y = None
    for i in range(taps):
        term = _shift_down(x, taps - 1 - i, lp) * w[i:i + 1, :]
        y = term if y is None else y + term
    return y


def _conv_bwd(x, w, dy, taps, lp):
    dx = None
    dws = []
    for i in range(taps):
        s = taps - 1 - i
        term = _shift_up(dy, s, lp) * w[i:i + 1, :]
        dx = term if dx is None else dx + term
        dws.append(jnp.sum(dy * _shift_down(x, s, lp), axis=0, keepdims=True))
    return dx, jnp.concatenate(dws, axis=0)


def _slab(lp, col0):
    return pl.BlockSpec((lp, DH), lambda b, s: (b, col0 // DH + s))


def _qkv_act(c, s):
    y = _silu(c)
    r = lax.rsqrt(jnp.sum(y * y, axis=-1, keepdims=True) + EPS)
    scale = jnp.where(s < HEADS, DH ** -0.5, 1.0)
    return y, r, scale


def _gdn_prep(proj, conv_qkv, bsz, lp, padf):
    def body(p_ref, w_ref, o_ref):
        s = pl.program_id(1)
        c = _conv_fwd(p_ref[...], w_ref[...], KQ, lp)
        y, r, scale = _qkv_act(c, s)
        out = jnp.where(s < 2 * HEADS, y * (r * scale), y)
        t = lax.broadcasted_iota(jnp.int32, (lp, 1), 0)
        o_ref[...] = jnp.where(t >= padf, out, 0.0)

    return pl.pallas_call(
        body, name="gdn_prep", grid=(bsz, 3 * HEADS),
        in_specs=[_slab(lp, Q0), pl.BlockSpec((KQ, DH), lambda b, s: (0, s))],
        out_specs=_slab(lp, 0),
        out_shape=jax.ShapeDtypeStruct((bsz * lp, 3 * GW), F32),
        compiler_params=_cparams(("parallel", "parallel")),
    )(proj, conv_qkv)


def _gdn_prep_bwd(proj, conv_qkv, dqkv, bsz, lp, padf):
    def body(p_ref, w_ref, d_ref, dp_ref, dw_ref):
        s, b = pl.program_id(0), pl.program_id(1)
        x = p_ref[...]
        w = w_ref[...]
        c = _conv_fwd(x, w, KQ, lp)
        y, r, scale = _qkv_act(c, s)
        dn = d_ref[...] * scale
        n = y * r
        dy_norm = r * (dn - n * jnp.sum(dn * n, axis=-1, keepdims=True))
        dyy = jnp.where(s < 2 * HEADS, dy_norm, dn)
        t = lax.broadcasted_iota(jnp.int32, (lp, 1), 0)
        dc = jnp.where(t >= padf, dyy * _dsilu(c), 0.0)
        dx, dw = _conv_bwd(x, w, dc, KQ, lp)
        dp_ref[...] = jnp.where(t >= padf, dx, 0.0).astype(BF16)

        @pl.when(b == 0)
        def _():
            dw_ref[...] = jnp.zeros_like(dw_ref)

        dw_ref[...] += dw

    sl = lambda col0: pl.BlockSpec((lp, DH), lambda s, b: (b, col0 // DH + s))
    return pl.pallas_call(
        body, name="gdn_prep_bwd", grid=(3 * HEADS, bsz),
        in_specs=[sl(Q0), pl.BlockSpec((KQ, DH), lambda s, b: (0, s)), sl(0)],
        out_specs=[sl(0), pl.BlockSpec((KQ, DH), lambda s, b: (0, s))],
        out_shape=[jax.ShapeDtypeStruct((bsz * lp, 3 * GW), BF16), jax.ShapeDtypeStruct((KQ, 3 * GW), F32)],
        compiler_params=_cparams(("parallel", "arbitrary")),
    )(proj, conv_qkv, dqkv)


def _softplus(x):
    e = jnp.exp(-jnp.abs(x))
    one_e = 1.0 + e
    l1p = jnp.where(one_e == 1.0, e, jnp.log(one_e) * (e / (one_e - 1.0)))
    return jnp.maximum(x, 0.0) + l1p


def _tri(n, kind):
    i = lax.broadcasted_iota(jnp.int32, (n, n), 0)
    j = lax.broadcasted_iota(jnp.int32, (n, n), 1)
    return {"incl": i >= j, "strict": i > j, "upper": i <= j}[kind]


def _gates(proj, alog_v, dtb_v, bsz, lp, padf):
    nchunk = lp // CH

    def body(p_ref, al_ref, dt_ref, gb_ref, cs_ref):
        x = p_ref[...]
        lane = lax.broadcasted_iota(jnp.int32, (lp, DH), 1)
        t = lax.broadcasted_iota(jnp.int32, (lp, DH), 0)
        beta = jax.nn.sigmoid(x)
        g = -jnp.exp(al_ref[...]) * _softplus(x + dt_ref[...])
        gb = jnp.where(lane < HEADS, beta, jnp.where(lane < 2 * HEADS, g, 0.0))
        gb_ref[...] = jnp.where(t >= padf, gb, 0.0)
        ltri = _tri(CH, "incl").astype(F32)

        def step(n, carry):
            r0 = pl.multiple_of(n * CH, CH)
            cs_ref[pl.ds(r0, CH), :] = _fdot(ltri, gb_ref[pl.ds(r0, CH), :])
            return carry

        lax.fori_loop(0, nchunk, step, 0)

    blk = pl.BlockSpec((lp, DH), lambda b: (b, BA0 // DH))
    vec = pl.BlockSpec((1, DH), lambda b: (0, 0))
    out = pl.BlockSpec((lp, DH), lambda b: (b, 0))
    return pl.pallas_call(
        body, name="gates", grid=(bsz,),
        in_specs=[blk, vec, vec], out_specs=[out, out],
        out_shape=[jax.ShapeDtypeStruct((bsz * lp, DH), F32)] * 2,
        compiler_params=_cparams(("parallel",)),
    )(proj, alog_v, dtb_v)


def _gates_bwd(proj, alog_v, dtb_v, dbeta_b, dg_b, gb, bsz, lp, padf):
    def body(p_ref, al_ref, dt_ref, db_ref, dg_ref, gb_ref, dl_ref, dv_ref):
        b = pl.program_id(0)
        x = p_ref[...]
        lane = lax.broadcasted_iota(jnp.int32, (lp, DH), 1)
        t = lax.broadcasted_iota(jnp.int32, (lp, DH), 0)
        dbeta = jnp.zeros((lp, DH), F32)
        dg = jnp.zeros((lp, DH), F32)
        for h in range(HEADS):
            dbeta = jnp.where(lane == h, db_ref[h], dbeta)
            dg = jnp.where(lane == HEADS + h, dg_ref[h], dg)
        real = t >= padf
        dbeta = jnp.where(real, dbeta, 0.0)
        dg = jnp.where(real, dg, 0.0)
        sb = jax.nn.sigmoid(x)
        neg_ea = -jnp.exp(al_ref[...])
        dsp = dg * neg_ea * jax.nn.sigmoid(x + dt_ref[...])
        dl = dbeta * sb * (1.0 - sb) + dsp
        dl_ref[...] = jnp.concatenate([dl, jnp.zeros((lp, BA_W - DH), F32)], axis=1).astype(BF16)
        dalog = jnp.sum(dg * gb_ref[...], axis=0, keepdims=True)
        ddt = jnp.sum(dsp, axis=0, keepdims=True)

        @pl.when(b == 0)
        def _():
            dv_ref[...] = jnp.zeros_like(dv_ref)

        dv_ref[...] += jnp.concatenate([dalog, ddt], axis=0)

    blk = pl.BlockSpec((lp, DH), lambda b: (b, BA0 // DH))
    vec = pl.BlockSpec((1, DH), lambda b: (0, 0))
    hb = pl.BlockSpec((HEADS, lp, DH), lambda b: (b, 0, 0))
    return pl.pallas_call(
        body, name="gates_bwd", grid=(bsz,),
        in_specs=[blk, vec, vec, hb, hb, pl.BlockSpec((lp, DH), lambda b: (b, 0))],
        out_specs=[pl.BlockSpec((lp, BA_W), lambda b: (b, 0)), pl.BlockSpec((2, DH), lambda b: (0, 0))],
        out_shape=[jax.ShapeDtypeStruct((bsz * lp, BA_W), BF16), jax.ShapeDtypeStruct((2, DH), F32)],
        compiler_params=_cparams(("arbitrary",)),
    )(proj, alog_v, dtb_v, dbeta_b, dg_b, gb)


def _mix_heads(proj, o, gdn_norm, conv_sc, bsz, lp, padf):
    def body(o_ref, z_ref, gn_ref, sx_ref, sb_ref, sc_ref, w_ref, og_ref, os_ref):
        og_ref[...] = (_rms(o_ref[...], gn_ref[...])[0] * _silu(z_ref[...])).astype(BF16)
        conv = _conv_fwd(sc_ref[...] * sx_ref[...], w_ref[...], KS, lp)
        t = lax.broadcasted_iota(jnp.int32, (lp, 1), 0)
        os_ref[...] = jnp.where(t >= padf, sb_ref[...] * conv, 0.0).astype(BF16)

    half = _slab(lp, 0)
    return pl.pallas_call(
        body, name="mix_heads", grid=(bsz, HEADS),
        in_specs=[half, _slab(lp, Z0), pl.BlockSpec((1, DH), lambda b, s: (0, 0)),
                  _slab(lp, SX0), _slab(lp, SB0), _slab(lp, SC0), pl.BlockSpec((KS, DH), lambda b, s: (0, s))],
        out_specs=[half, half],
        out_shape=[jax.ShapeDtypeStruct((bsz * lp, GW), BF16)] * 2,
        compiler_params=_cparams(("parallel", "parallel")),
    )(o, proj, gdn_norm, proj, proj, proj, conv_sc)


def _mix_heads_bwd(proj, o, gdn_norm, conv_sc, dmixin, bsz, lp, padf):
    def body(o_ref, z_ref, gn_ref, sx_ref, sb_ref, sc_ref, w_ref, dg_ref, ds_ref,
             do_ref, dz_ref, dsx_ref, dsb_ref, dsc_ref, dgn_ref, dw_ref):
        s, b = pl.program_id(0), pl.program_id(1)
        t = lax.broadcasted_iota(jnp.int32, (lp, 1), 0)
        real = t >= padf
        o, z, gn, d = o_ref[...], z_ref[...], gn_ref[...], dg_ref[...]
        sz = _silu(z)
        on = _rms(o, gn)[0]
        dz_ref[...] = (d * on * _dsilu(z)).astype(BF16)
        dox, dgn = _rms_bwd(o, gn, d * sz)
        do_ref[...] = dox
        sx, sb, sc, w, e = sx_ref[...], sb_ref[...], sc_ref[...], w_ref[...], ds_ref[...]
        e = jnp.where(real, e, 0.0)
        xin = sc * sx
        conv = _conv_fwd(xin, w, KS, lp)
        dsb_ref[...] = (e * conv).astype(BF16)
        dxin, dw = _conv_bwd(xin, w, e * sb, KS, lp)
        dxin = jnp.where(real, dxin, 0.0)
        dsx_ref[...] = (dxin * sc).astype(BF16)
        dsc_ref[...] = (dxin * sx).astype(BF16)

        @pl.when(jnp.logical_and(s == 0, b == 0))
        def _():
            dgn_ref[...] = jnp.zeros_like(dgn_ref)

        @pl.when(b == 0)
        def _():
            dw_ref[...] = jnp.zeros_like(dw_ref)

        dgn_ref[...] += jnp.sum(dgn, axis=0, keepdims=True)
        dw_ref[...] += dw

    sl = lambda col0: pl.BlockSpec((lp, DH), lambda s, b: (b, col0 // DH + s))
    half = sl(0)
    return pl.pallas_call(
        body, name="mix_heads_bwd", grid=(HEADS, bsz),
        in_specs=[half, sl(Z0), pl.BlockSpec((1, DH), lambda s, b: (0, 0)), sl(SX0), sl(SB0), sl(SC0),
                  pl.BlockSpec((KS, DH), lambda s, b: (0, s)), sl(0), sl(GW)],
        out_specs=[half] * 5 + [pl.BlockSpec((1, DH), lambda s, b: (0, 0)), pl.BlockSpec((KS, DH), lambda s, b: (0, s))],
        out_shape=[jax.ShapeDtypeStruct((bsz * lp, GW), F32)] + [jax.ShapeDtypeStruct((bsz * lp, GW), BF16)] * 4
        + [jax.ShapeDtypeStruct((1, DH), F32), jax.ShapeDtypeStruct((KS, SW), F32)],
        compiler_params=_cparams(("arbitrary", "arbitrary")),
    )(o, proj, gdn_norm, proj, proj, proj, conv_sc, dmixin, dmixin)


def _split_dot(a, b, dims=NN):
    ah = a.astype(BF16)
    al = (a - ah.astype(F32)).astype(BF16)
    bh = b.astype(BF16)
    bl = (b - bh.astype(F32)).astype(BF16)

    def d(x, y):
        return lax.dot_general(x, y, (dims, ((), ())), preferred_element_type=F32)

    return d(ah, bh) + (d(ah, bl) + d(al, bh))


def _eye(n):
    return (lax.broadcasted_iota(jnp.int32, (n, n), 0) == lax.broadcasted_iota(jnp.int32, (n, n), 1)).astype(F32)


def _inv_unit_lower(a):
    p = -a
    t = _eye(CH) + p
    for _ in range(5):
        p = _split_dot(p, p)
        t = t + _split_dot(t, p)
    return t


def _chunk_gates(gc_ref, be_ref, gt_ref, g):
    rows = pl.ds(g * CH, CH)
    gc = gc_ref[0, rows, :]
    be = be_ref[0, rows, :]
    gcl = gc_ref[0, pl.ds(g * CH + CH - 1, 1), :]
    gt = gt_ref[0, g]
    incl, upper = _tri(CH, "incl"), _tri(CH, "upper")
    dec = jnp.where(incl, jnp.exp(jnp.where(incl, gc[:, :CH] - gt, 0.0)), 0.0)
    dec_t = jnp.where(upper, jnp.exp(jnp.where(upper, gt - gc[:, :CH], 0.0)), 0.0)
    return rows, be, dec, dec_t, jnp.exp(gc), jnp.exp(gcl - gc), jnp.exp(gcl)


def _chunk_group(nchunk):
    return 3 if nchunk % 3 == 0 else 1


def _gdn_local_specs(lp, grp):
    ngrp = lp // (grp * CH)
    head = lambda col0: pl.BlockSpec((grp * CH, DH), lambda i, j: ((i // HEADS) * ngrp + j, col0 // DH + i % HEADS))
    lane_b = pl.BlockSpec((1, grp * CH, DH), lambda i, j: (i, j, 0))
    per_chunk = lambda r, c: pl.BlockSpec((1, grp, r, c), lambda i, j: (i, j, 0, 0))
    return ngrp, head, lane_b, per_chunk


def _gdn_local_fwd(qkv, gcb, bb, gct, bsz, lp):
    nchunk = lp // CH
    grp = _chunk_group(nchunk)
    ngrp, head, lane_b, per_chunk = _gdn_local_specs(lp, grp)
    bh = bsz * HEADS

    def body(q_ref, k_ref, v_ref, gc_ref, be_ref, gt_ref,
             u_ref, l1_ref, l2_ref, l3_ref, wt_ref, kd_ref, ti_ref, gl_ref):
        incl, strict, upper = _tri(CH, "incl"), _tri(CH, "strict"), _tri(CH, "upper")
        eye = _eye(CH)
        for g in range(grp):
            rows, be, dec, dec_t, eg, ekd, gl = _chunk_gates(gc_ref, be_ref, gt_ref, g)
            q, k, v = q_ref[rows, :], k_ref[rows, :], v_ref[rows, :]
            kb = k * be
            a = jnp.where(strict, _bdot(kb, k, NT) * dec, 0.0)
            tinv = _inv_unit_lower(a)
            u = _bdot(tinv, v * be, NN)
            w = _bdot(tinv, kb * eg, NN)
            qk = jnp.where(incl, _bdot(q, k, NT) * dec, 0.0)
            qk_t = jnp.where(upper, _bdot(k, q, NT) * dec_t, 0.0)
            qd = q * eg
            kd = k * ekd
            u_ref[rows, :] = u
            l1_ref[0, g] = jnp.concatenate([w, qd], axis=0).astype(BF16)
            l2_ref[0, g] = jnp.concatenate([qk, _bdot(kd, eye, TN)], axis=0).astype(BF16)
            l3_ref[0, g] = jnp.concatenate([qk_t, _bdot(qd, eye, TN)], axis=0).astype(BF16)
            wt_ref[0, g] = _bdot(w, eye, TN).astype(BF16)
            kd_ref[0, g] = kd.astype(BF16)
            ti_ref[0, g] = tinv
            gl_ref[0, g] = gl

    r = bsz * lp
    shapes = [((r, GW), F32), ((bh, nchunk, 2 * CH, DH), BF16), ((bh, nchunk, 3 * CH, CH), BF16),
              ((bh, nchunk, 3 * CH, CH), BF16), ((bh, nchunk, DH, CH), BF16), ((bh, nchunk, CH, DH), BF16),
              ((bh, nchunk, CH, CH), F32), ((bh, nchunk, 1, DH), F32)]
    return pl.pallas_call(
        body, name="gdn_local_fwd", grid=(bh, ngrp),
        in_specs=[head(0), head(GW), head(2 * GW), lane_b, lane_b, per_chunk(1, CH)],
        out_specs=[head(0), per_chunk(2 * CH, DH), per_chunk(3 * CH, CH), per_chunk(3 * CH, CH), per_chunk(DH, CH),
                   per_chunk(CH, DH), per_chunk(CH, CH), per_chunk(1, DH)],
        out_shape=[jax.ShapeDtypeStruct(s, dt) for s, dt in shapes],
        compiler_params=_cparams(("parallel", "parallel")),
    )(qkv, qkv, qkv, gcb, bb, gct)


def _scan_specs(bsz, lp, grp, order):
    ngrp = lp // (grp * CH)
    bh = bsz * HEADS
    seq = lambda: pl.BlockSpec((bsz, grp * CH, GW), lambda s: (0, order(s, ngrp), 0))
    per_chunk = lambda r, c: pl.BlockSpec((bh, grp, r, c), lambda s: (0, order(s, ngrp), 0, 0))
    return ngrp, bh, seq, per_chunk


def _gdn_scan_fwd(u, l1, l2, gl, bsz, lp):
    nchunk = lp // CH
    grp = _chunk_group(nchunk)
    ngrp, bh, seq, per_chunk = _scan_specs(bsz, lp, grp, lambda s, n: s)

    def body(u_ref, l1_ref, l2_ref, gl_ref, o_ref, vn_ref, st_ref, s_scr):
        @pl.when(pl.program_id(0) == 0)
        def _():
            s_scr[...] = jnp.zeros_like(s_scr)

        for c in range(grp):
            rows = slice(c * CH, (c + 1) * CH)
            for i in range(bh):
                b, h = divmod(i, HEADS)
                cols = slice(h * DH, (h + 1) * DH)
                s = s_scr[i]
                st_ref[i, c] = s
                x = jnp.dot(l1_ref[i, c], s.astype(BF16), preferred_element_type=F32)
                vnew = u_ref[b, rows, cols] - x[:CH]
                y = jnp.dot(l2_ref[i, c], vnew.astype(BF16), preferred_element_type=F32)
                o_ref[b, rows, cols] = x[CH:] + y[:CH]
                vn_ref[b, rows, cols] = vnew.astype(BF16)
                s_scr[i] = s * gl_ref[i, c] + y[CH:]

    return pl.pallas_call(
        body, name="gdn_scan_fwd", grid=(ngrp,),
        in_specs=[seq(), per_chunk(2 * CH, DH), per_chunk(3 * CH, CH), per_chunk(1, DH)],
        out_specs=[seq(), seq(), per_chunk(DH, DH)],
        out_shape=[jax.ShapeDtypeStruct((bsz, lp, GW), F32), jax.ShapeDtypeStruct((bsz, lp, GW), BF16),
                   jax.ShapeDtypeStruct((bh, nchunk, DH, DH), F32)],
        scratch_shapes=[pltpu.VMEM((bh, DH, DH), F32)],
        compiler_params=_cparams(("arbitrary",)),
    )(u.reshape(bsz, lp, GW), l1, l2, gl)


def _gdn_scan_bwd(d_o, l3, wt, kd, gl, bsz, lp):
    nchunk = lp // CH
    grp = _chunk_group(nchunk)
    ngrp, bh, seq, per_chunk = _scan_specs(bsz, lp, grp, lambda s, n: n - 1 - s)

    def body(do_ref, l3_ref, wt_ref, kd_ref, gl_ref, dvn_ref, dst_ref, ds_scr):
        @pl.when(pl.program_id(0) == 0)
        def _():
            ds_scr[...] = jnp.zeros_like(ds_scr)

        for c in reversed(range(grp)):
            rows = slice(c * CH, (c + 1) * CH)
            for i in range(bh):
                b, h = divmod(i, HEADS)
                cols = slice(h * DH, (h + 1) * DH)
                ds = ds_scr[i]
                dst_ref[i, c] = ds
                p = jnp.dot(l3_ref[i, c], do_ref[b, rows, cols].astype(BF16), preferred_element_type=F32)
                dvn = p[:CH] + jnp.dot(kd_ref[i, c], ds.astype(BF16), preferred_element_type=F32)
                dvn_ref[b, rows, cols] = dvn.astype(BF16)
                ds_scr[i] = ds * gl_ref[i, c] + p[CH:] - jnp.dot(wt_ref[i, c], dvn.astype(BF16),
                                                                 preferred_element_type=F32)

    return pl.pallas_call(
        body, name="gdn_scan_bwd", grid=(ngrp,),
        in_specs=[seq(), per_chunk(3 * CH, CH), per_chunk(DH, CH), per_chunk(CH, DH), per_chunk(1, DH)],
        out_specs=[seq(), per_chunk(DH, DH)],
        out_shape=[jax.ShapeDtypeStruct((bsz, lp, GW), BF16), jax.ShapeDtypeStruct((bh, nchunk, DH, DH), F32)],
        scratch_shapes=[pltpu.VMEM((bh, DH, DH), F32)],
        compiler_params=_cparams(("arbitrary",)),
    )(d_o.reshape(bsz, lp, GW), l3, wt, kd, gl)


def _gdn_local_bwd(qkv, gcb, bb, gct, tinv_all, states, dstates, vnew, dvnew, d_o, bsz, lp):
    nchunk = lp // CH
    grp = _chunk_group(nchunk)
    ngrp, head, lane_b, per_chunk = _gdn_local_specs(lp, grp)
    bh = bsz * HEADS

    def body(q_ref, k_ref, v_ref, gc_ref, be_ref, gt_ref, ti_ref, st_ref, dst_ref, vn_ref, dvn_ref, do_ref,
             dq_ref, dk_ref, dv_ref, dg_ref, db_ref):
        incl, strict = _tri(CH, "incl"), _tri(CH, "strict")
        upper = _tri(CH, "upper").astype(F32)
        ones = jnp.ones((CH, DH), F32)
        last = lax.broadcasted_iota(jnp.int32, (CH, 1), 0) == CH - 1

        def rsum(x):
            return jnp.sum(x, axis=-1, keepdims=True)

        for g in range(grp):
            rows, be, dec, _, eg, ekd, gl = _chunk_gates(gc_ref, be_ref, gt_ref, g)
            q, k, v = q_ref[rows, :], k_ref[rows, :], v_ref[rows, :]
            tinv, s, ds = ti_ref[0, g], st_ref[0, g], dst_ref[0, g]
            vnew, dvn, do = vn_ref[rows, :], dvn_ref[rows, :], do_ref[rows, :]
            kb = k * be
            vb = v * be
            kbg = kb * eg
            qd = q * eg
            kd = k * ekd
            a = jnp.where(strict, _bdot(kb, k, NT) * dec, 0.0)
            qk = jnp.where(incl, _bdot(q, k, NT) * dec, 0.0)
            dqk = jnp.where(incl, _bdot(do, vnew, NT), 0.0)
            dqd = _bdot(do, s, NT)
            dkd = _bdot(vnew, ds, NT)
            dgl = jnp.sum(rsum(s * ds), axis=0, keepdims=True)
            dw = -_bdot(dvn, s, NT)
            dvb = _bdot(tinv, dvn, TN)
            dkbg = _bdot(tinv, dw, TN)
            dt = _bdot(dvn, vb, NT) + _bdot(dw, kbg, NT)
            da = jnp.where(strict, -_split_dot(_split_dot(tinv, dt, TN), tinv, NT), 0.0)
            dp = da * dec
            dr = dqk * dec
            dkb = _bdot(dp, k, NN) + dkbg * eg
            dk = _bdot(dp, kb, TN) + _bdot(dr, q, TN) + dkd * ekd + dkb * be
            dq = _bdot(dr, k, NN) + dqd * eg
            m = da * a + dqk * qk
            kd_term = rsum(dkd * kd)
            dgc = rsum(m) - _split_dot(m, ones, TN) + rsum(dqd * qd) - kd_term + rsum(dkbg * kbg)
            tail = jnp.sum(kd_term, axis=0, keepdims=True) + dgl * gl
            dgc = dgc + jnp.where(last, tail, 0.0)
            dq_ref[rows, :] = dq
            dk_ref[rows, :] = dk
            dv_ref[rows, :] = dvb * be
            dg_ref[0, rows, :] = _split_dot(upper, dgc, NN)
            db_ref[0, rows, :] = jnp.broadcast_to(rsum(dkb * k) + rsum(dvb * v), (CH, DH))

    r = bsz * lp
    return pl.pallas_call(
        body, name="gdn_local_bwd", grid=(bh, ngrp),
        in_specs=[head(0), head(GW), head(2 * GW), lane_b, lane_b, per_chunk(1, CH), per_chunk(CH, CH),
                  per_chunk(DH, DH), per_chunk(DH, DH), head(0), head(0), head(0)],
        out_specs=[head(0), head(0), head(0), lane_b, lane_b],
        out_shape=[jax.ShapeDtypeStruct((r, GW), F32)] * 3 + [jax.ShapeDtypeStruct((bh, lp, DH), F32)] * 2,
        compiler_params=_cparams(("parallel", "parallel")),
    )(qkv, qkv, qkv, gcb, bb, gct, tinv_all, states, dstates, vnew, dvnew, d_o)


def _place():
    return lax.axis_index("x"), lax.axis_index("y"), lax.axis_index("c")


def _any_specs(n):
    return [pl.BlockSpec(memory_space=pl.ANY)] * n


def _all_gather(name, shards):
    n = len(shards)

    def body(*refs):
        ins, outs = refs[:n], refs[n:2 * n]
        send_sems, recv_sems, local_sems = refs[2 * n:]
        x, y, c = _place()
        me, sibling = (x, y, c), (x, y, 1 - c)
        chips = [(1 - x, y), (x, 1 - y), (1 - x, 1 - y)]

        def slot(i, p):
            return outs[i].at[4 * p[0] + 2 * p[1] + p[2]]

        def copy(i, k, block, to, src=None):
            return pltpu.make_async_remote_copy(
                src_ref=slot(i, block) if src is None else src, dst_ref=slot(i, block),
                send_sem=send_sems.at[i, k], recv_sem=recv_sems.at[i, k], device_id=to, device_id_type=MESH)

        mine = [pltpu.make_async_copy(ins[i], slot(i, me), local_sems.at[i]) for i in range(n)]
        for cp in mine:
            cp.start()
        first = []
        for i in range(n):
            first.append(copy(i, 0, me, sibling, src=ins[i]))
            first += [copy(i, 1 + j, me, (*chip, c), src=ins[i]) for j, chip in enumerate(chips)]
        for cp in first:
            cp.start()
        passed = []
        for j, chip in enumerate(chips):
            for i in range(n):
                copy(i, 1 + j, (*chip, c), me).wait_recv()
                fwd = copy(i, 4 + j, (*chip, c), sibling)
                fwd.start()
                passed.append(fwd)
        for i in range(n):
            copy(i, 0, sibling, me).wait_recv()
            for j, chip in enumerate(chips):
                copy(i, 4 + j, (*chip, 1 - c), me).wait_recv()
        for cp in first + passed:
            cp.wait_send()
        for cp in mine:
            cp.wait()

    return pl.pallas_call(
        body, name=name,
        in_specs=_any_specs(n), out_specs=_any_specs(n),
        out_shape=[jax.ShapeDtypeStruct((N_DEV,) + s.shape, s.dtype) for s in shards],
        scratch_shapes=[pltpu.SemaphoreType.DMA((n, 7)), pltpu.SemaphoreType.DMA((n, 7)),
                        pltpu.SemaphoreType.DMA((n,))],
    )(*shards)


def _sibling_exchange(name, grads):
    n = len(grads)

    def body(*refs):
        ins, outs = refs[:n], refs[n:2 * n]
        send_sems, recv_sems = refs[2 * n:]
        x, y, c = _place()
        cps = [pltpu.make_async_remote_copy(
            src_ref=ins[i].at[1 - c], dst_ref=outs[i], send_sem=send_sems.at[i], recv_sem=recv_sems.at[i],
            device_id=(x, y, 1 - c), device_id_type=MESH) for i in range(n)]
        for cp in cps:
            cp.start()
        for cp in cps:
            cp.wait_recv()
        for cp in cps:
            cp.wait_send()

    return pl.pallas_call(
        body, name=name,
        in_specs=_any_specs(n), out_specs=_any_specs(n),
        out_shape=[jax.ShapeDtypeStruct(g.shape[1:], g.dtype) for g in grads],
        scratch_shapes=[pltpu.SemaphoreType.DMA((n,)), pltpu.SemaphoreType.DMA((n,))],
    )(*grads)


def _chip_exchange(name, parts):
    n = len(parts)

    def body(*refs):
        ins, outs = refs[:n], refs[n:2 * n]
        send_sems, recv_sems, local_sems = refs[2 * n:]
        x, y, c = _place()
        chips = [(1 - x, y), (x, 1 - y), (1 - x, 1 - y)]
        mine = [pltpu.make_async_copy(ins[i].at[2 * x + y], outs[i].at[0], local_sems.at[i]) for i in range(n)]
        for cp in mine:
            cp.start()
        cps = []
        for j, chip in enumerate(chips):
            for i in range(n):
                cps.append(pltpu.make_async_remote_copy(
                    src_ref=ins[i].at[2 * chip[0] + chip[1]], dst_ref=outs[i].at[1 + j],
                    send_sem=send_sems.at[i, j], recv_sem=recv_sems.at[i, j],
                    device_id=(*chip, c), device_id_type=MESH))
        for cp in cps:
            cp.start()
        for cp in cps:
            cp.wait_recv()
        for cp in cps:
            cp.wait_send()
        for cp in mine:
            cp.wait()

    return pl.pallas_call(
        body, name=name,
        in_specs=_any_specs(n), out_specs=_any_specs(n),
        out_shape=[jax.ShapeDtypeStruct(p.shape, p.dtype) for p in parts],
        scratch_shapes=[pltpu.SemaphoreType.DMA((n, 3)), pltpu.SemaphoreType.DMA((n, 3)),
                        pltpu.SemaphoreType.DMA((n,))],
    )(*parts)


def _pair_sum(name, grads, recv, core):
    _, _, r, c = grads.shape
    tr = _div_tile(r, 256, 8)

    def body(core_ref, g_ref, r_ref, o_ref):
        o_ref[...] = g_ref[0] + r_ref[...]

    return pl.pallas_call(
        body, name=name,
        grid_spec=pltpu.PrefetchScalarGridSpec(
            num_scalar_prefetch=1, grid=(4, r // tr),
            in_specs=[pl.BlockSpec((1, 1, tr, c), lambda s, i, cr: (cr[0], s, i, 0)),
                      pl.BlockSpec((1, tr, c), lambda s, i, cr: (s, i, 0))],
            out_specs=pl.BlockSpec((1, tr, c), lambda s, i, cr: (s, i, 0))),
        out_shape=jax.ShapeDtypeStruct(recv.shape, F32),
        compiler_params=_cparams(("parallel", "parallel")),
    )(core, grads, recv)


def _adamw_math(w, g, m, v):
    m = ADAM_B1 * m + (1.0 - ADAM_B1) * g
    v = ADAM_B2 * v + (1.0 - ADAM_B2) * (g * g)
    m_hat = m / (1.0 - ADAM_B1 ** ADAM_STEP)
    v_hat = v / (1.0 - ADAM_B2 ** ADAM_STEP)
    delta = -ADAM_LR * (m_hat / (jnp.sqrt(v_hat) + ADAM_EPS) + ADAM_WD * w)
    return delta, m, v


def _adamw(name, parts, w, m, v):
    nslot, r, c = parts.shape
    tr = _div_tile(r, 256, 8) if r % 8 == 0 else r

    def body(p_ref, w_ref, m_ref, v_ref, g_out, d_out, m_out, v_out):
        g = p_ref[0]
        for s in range(1, nslot):
            g = g + p_ref[s]
        d, mn, vn = _adamw_math(w_ref[...], g, m_ref[...], v_ref[...])
        g_out[...] = g
        d_out[...] = d
        m_out[...] = mn
        v_out[...] = vn

    blk = pl.BlockSpec((tr, c), lambda i: (i, 0))
    return pl.pallas_call(
        body, name=name, grid=(r // tr,),
        in_specs=[pl.BlockSpec((nslot, tr, c), lambda i: (0, i, 0)), blk, blk, blk],
        out_specs=[blk] * 4,
        out_shape=[jax.ShapeDtypeStruct((r, c), F32)] * 4,
        compiler_params=_cparams(("parallel",)),
    )(parts, w, m, v)


SMALL_ROWS, SMALL_COLS = 32, 3 * GW


def _pack_small(gains4, meta, conv_qkv, conv_sc, a_log, dt_bias, gdn_norm):
    def padc(a):
        return jnp.pad(a, ((0, 0), (0, SMALL_COLS - a.shape[1])))
    scal = jnp.concatenate([jnp.pad(a_log, ((0, 0), (0, DH - HEADS))), jnp.pad(dt_bias, ((0, 0), (0, DH - HEADS))),
                            gdn_norm], axis=1)
    rows = [padc(g) for g in gains4] + [padc(meta), padc(conv_qkv), padc(conv_sc), padc(scal)]
    slab = jnp.concatenate(rows, axis=0)
    return jnp.pad(slab, ((0, SMALL_ROWS - slab.shape[0]), (0, 0)))


def _unpack_small(slab):
    gains = [slab[i:i + 1, :D] for i in range(4)]
    meta = slab[4:20, :D]
    conv_qkv = slab[20:24, :]
    conv_sc = slab[24:27, :SW]
    a_log = slab[27:28, 0:HEADS]
    dt_bias = slab[27:28, DH:DH + HEADS]
    gdn_norm = slab[27:28, 2 * DH:3 * DH]
    return gains, meta, conv_qkv, conv_sc, a_log, dt_bias, gdn_norm


def kernel(x, meta_tokens, mix_pre_norm, mix_post_norm, ffn_pre_norm, ffn_post_norm, w_in, conv_qkv, a_log, dt_bias, gdn_norm, conv_sc, w_out, w_gate, w_up, w_down, loss_target, m_meta_tokens, m_mix_pre_norm, m_mix_post_norm, m_ffn_pre_norm, m_ffn_post_norm, m_w_in, m_conv_qkv, m_a_log, m_dt_bias, m_gdn_norm, m_conv_sc, m_w_out, m_w_gate, m_w_up, m_w_down, v_meta_tokens, v_mix_pre_norm, v_mix_post_norm, v_ffn_pre_norm, v_ffn_post_norm, v_w_in, v_conv_qkv, v_a_log, v_dt_bias, v_gdn_norm, v_conv_sc, v_w_out, v_w_gate, v_w_up, v_w_down):
    bsz, seq, _ = x.shape
    lp = -(-(N_META + seq) // CH) * CH
    padf = lp - N_META - seq
    first_real = padf + N_META
    rows = bsz * lp
    nchunk = lp // CH
    tr = _div_tile(lp, 528)
    tm = _div_tile(rows, 1056)
    cx, cy, cc = _place()
    dev = 4 * cx + 2 * cy + cc

    small_shard = jnp.concatenate([
        jnp.pad(meta_tokens, ((0, 0), (0, 256 - DH))),
        jnp.pad(conv_qkv[0], ((0, 0), (0, 256 - 3 * GW // N_DEV))),
        jnp.pad(conv_sc[0], ((0, 5), (0, 256 - SW // N_DEV)))], axis=0)
    g_in, g_out, g_gate, g_up, g_down, g_small = _all_gather(
        "gather_weights",
        [w_in[0].astype(BF16), w_out[0].astype(BF16), w_gate[0].astype(BF16), w_up[0].astype(BF16),
         w_down[0].astype(BF16), small_shard])

    def cols_full(g):
        return jnp.transpose(g, (1, 0, 2)).reshape(g.shape[1], -1)

    win = cols_full(g_in)
    win = jnp.concatenate([win[:, :BA0 + 2 * HEADS], jnp.zeros((D, BA_W - 2 * HEADS), BF16),
                           win[:, BA0 + 2 * HEADS:]], axis=1)
    wout = g_out.reshape(D, D)
    wgate, wup = cols_full(g_gate), cols_full(g_up)
    wdown = g_down.reshape(FF, D)
    meta_full = cols_full(g_small[:, 0:16, :DH])
    convq_full = cols_full(g_small[:, 16:20, :3 * GW // N_DEV])
    convs_full = cols_full(g_small[:, 20:23, :SW // N_DEV])
    alog_v = jnp.pad(a_log, ((0, 0), (HEADS, DH - 2 * HEADS)))
    dtb_v = jnp.pad(dt_bias, ((0, 0), (HEADS, DH - 2 * HEADS)))

    h0 = jnp.concatenate([jnp.zeros((bsz, padf, D), F32), jnp.broadcast_to(meta_full[None], (bsz, N_META, D)), x],
                         axis=1).reshape(rows, D)
    tgt = jnp.pad(loss_target, ((0, 0), (first_real, 0), (0, 0))).reshape(rows, D)
    u1 = _pre_norm(h0, mix_pre_norm, bsz, lp, tr)
    (proj,) = _mm("proj_in", [(u1, win)], "nn", tm, 768, D)
    qkv = _gdn_prep(proj, convq_full, bsz, lp, padf)
    gb, gcs = _gates(proj, alog_v, dtb_v, bsz, lp, padf)

    def lane_bcast(cols):
        t = jnp.transpose(cols.reshape(bsz, lp, HEADS), (0, 2, 1)).reshape(bsz * HEADS, lp, 1)
        return jnp.broadcast_to(t, (bsz * HEADS, lp, DH))

    bb = lane_bcast(gb[:, 0:HEADS])
    gcb = lane_bcast(gcs[:, HEADS:2 * HEADS])
    gct = jnp.transpose(gcs[:, HEADS:2 * HEADS].reshape(bsz, nchunk, CH, HEADS), (0, 3, 1, 2)).reshape(
        bsz * HEADS, nchunk, 1, CH)
    u_loc, l1, l2, l3, w_t, k_dec, tinv_all, g_last = _gdn_local_fwd(qkv, gcb, bb, gct, bsz, lp)
    o, v_new, states = _gdn_scan_fwd(u_loc, l1, l2, g_last, bsz, lp)
    o, v_new = o.reshape(rows, GW), v_new.reshape(rows, GW)
    mix_g, mix_s = _mix_heads(proj, o, gdn_norm, convs_full, bsz, lp, padf)
    mixin = jnp.concatenate([mix_g, mix_s], axis=1)
    (mix,) = _mm("proj_out", [(mixin, wout)], "nn", tm, 512, D)
    h1, u2 = _mid_norms(h0, mix, mix_post_norm, ffn_pre_norm, bsz, lp, tr)
    gate, up, act = _ffn_up(u2, wgate, wup, _div_tile(rows, 528), 1408)
    (ffn,) = _mm("ffn_down", [(act, wdown)], "nn", tm, 512, 1408)

    dy, dffn, d_g4, loss_part = _loss_head(h1, ffn, tgt, ffn_post_norm, bsz, lp, tr, first_real)
    loss = lax.psum(loss_part[0, 0], ("x", "y", "c"))
    (dwdown,) = _mm("dw_down", [(act, dffn)], "tn", 1408, 512, tm)

    def swiglu_bwd(da, g, u):
        return da * u * _dsilu(g), da * _silu(g)

    dgate, dup = _mm("d_act", [(dffn, wdown)], "nt", tm, 1408, D, out_dtypes=(BF16, BF16),
                     epilogue=swiglu_bwd, extras=(gate, up))
    (dwgate,) = _mm("dw_gate", [(u2, dgate)], "tn", D, 1408, tm)
    (dwup,) = _mm("dw_up", [(u2, dup)], "tn", D, 1408, tm)
    (du2,) = _mm("d_u2", [(dgate, wgate), (dup, wup)], "nt", tm, 512, 1408)
    dh1, dmix, d_g3, d_g2 = _mid_norms_bwd(h1, mix, du2, dy, ffn_pre_norm, mix_post_norm, bsz, lp, tr)
    (dwout,) = _mm("dw_out", [(mixin, dmix)], "tn", D, 512, tm)
    (dmixin,) = _mm("d_mixin", [(dmix, wout)], "nt", tm, 512, D)
    d_o, dz, dsx, dsb, dsc, d_gn, d_convs = _mix_heads_bwd(proj, o, gdn_norm, convs_full, dmixin, bsz, lp, padf)
    dv_new, dstates = _gdn_scan_bwd(d_o, l3, w_t, k_dec, g_last, bsz, lp)
    dq, dk, dv, dg_b, dbeta_b = _gdn_local_bwd(qkv, gcb, bb, gct, tinv_all, states, dstates, v_new,
                                               dv_new.reshape(rows, GW), d_o, bsz, lp)
    dqkv = jnp.concatenate([dq, dk, dv], axis=1)
    dpqkv, d_convq = _gdn_prep_bwd(proj, convq_full, dqkv, bsz, lp, padf)
    dba, d_scal = _gates_bwd(proj, alog_v, dtb_v, dbeta_b, dg_b, gb, bsz, lp, padf)
    dproj = jnp.concatenate([dpqkv, dz, dba, dsx, dsb, dsc], axis=1)
    (dwin,) = _mm("dw_in", [(u1, dproj)], "tn", D, 768, tm)
    (du1,) = _mm("d_u1", [(dproj, win)], "nt", tm, 512, 768)
    dh0, d_g1 = _pre_norm_bwd(h0, du1, dh1, mix_pre_norm, bsz, lp, tr)
    dh0 = dh0.reshape(bsz, lp, D)
    grad_x = dh0[:, first_real:]
    d_meta = jnp.sum(dh0[:, padf:first_real], axis=0)

    dwin = jnp.concatenate([dwin[:, :BA0 + 2 * HEADS], dwin[:, SX0:]], axis=1)

    def col_blocks(g):
        r = g.shape[0]
        return jnp.transpose(g.reshape(r, 4, 2, -1), (2, 1, 0, 3))

    def row_blocks(g):
        c = g.shape[1]
        return jnp.transpose(g.reshape(4, 2, -1, c), (1, 0, 2, 3))

    big = [col_blocks(dwin), row_blocks(dwout), col_blocks(dwgate), col_blocks(dwup), row_blocks(dwdown)]
    from_sibling = _sibling_exchange("grads_to_sibling", big)
    core = jnp.reshape(cc, (1,)).astype(jnp.int32)
    names = ["w_in", "w_out", "w_gate", "w_up", "w_down"]
    chip_parts = [_pair_sum("pair_sum_" + nm, g, r, core) for nm, g, r in zip(names, big, from_sibling)]
    reduced = _chip_exchange("grads_across_chips", chip_parts)

    d_alog = d_scal[0:1, HEADS:2 * HEADS]
    d_dtb = d_scal[1:2, HEADS:2 * HEADS]
    slab = _pack_small([d_g1, d_g2, d_g3, d_g4], d_meta, d_convq, d_convs, d_alog, d_dtb, d_gn)
    (slabs,) = _all_gather("gather_small_grads", [slab])
    off_meta = dev * DH
    off_cq = dev * (3 * GW // N_DEV)
    off_cs = dev * (SW // N_DEV)

    def shard_of(s):
        gains, meta, cq, cs, al, db, gn = _unpack_small(s)
        return gains, lax.dynamic_slice_in_dim(meta, off_meta, DH, 1), \
            lax.dynamic_slice_in_dim(cq, off_cq, 3 * GW // N_DEV, 1), \
            lax.dynamic_slice_in_dim(cs, off_cs, SW // N_DEV, 1), al, db, gn

    slabs_local = jax.vmap(lambda s: _pack_small(*shard_of(s)))(slabs)
    w_small = _pack_small([mix_pre_norm, mix_post_norm, ffn_pre_norm, ffn_post_norm], meta_tokens, conv_qkv[0],
                          conv_sc[0], a_log, dt_bias, gdn_norm)
    m_small = _pack_small([m_mix_pre_norm, m_mix_post_norm, m_ffn_pre_norm, m_ffn_post_norm], m_meta_tokens,
                          m_conv_qkv[0], m_conv_sc[0], m_a_log, m_dt_bias, m_gdn_norm)
    v_small = _pack_small([v_mix_pre_norm, v_mix_post_norm, v_ffn_pre_norm, v_ffn_post_norm], v_meta_tokens,
                          v_conv_qkv[0], v_conv_sc[0], v_a_log, v_dt_bias, v_gdn_norm)
    small_out = _adamw("adamw_small", slabs_local, w_small, m_small, v_small)

    big_w = [(w_in, m_w_in, v_w_in), (w_out, m_w_out, v_w_out), (w_gate, m_w_gate, v_w_gate),
             (w_up, m_w_up, v_w_up), (w_down, m_w_down, v_w_down)]
    big_out = [_adamw("adamw_" + nm, parts, w[0], m[0], v[0]) for nm, parts, (w, m, v) in zip(names, reduced, big_w)]

    def small_leaves(slab_out):
        gains, meta, cq, cs, al, db, gn = _unpack_small(slab_out)
        return {"meta_tokens": meta[:, :DH], "mix_pre_norm": gains[0], "mix_post_norm": gains[1],
                "ffn_pre_norm": gains[2], "ffn_post_norm": gains[3], "conv_qkv": cq[None, :, :3 * GW // N_DEV],
                "a_log": al, "dt_bias": db, "gdn_norm": gn, "conv_sc": cs[None, :, :SW // N_DEV]}

    order = ["meta_tokens", "mix_pre_norm", "mix_post_norm", "ffn_pre_norm", "ffn_post_norm", "w_in", "conv_qkv",
             "a_log", "dt_bias", "gdn_norm", "conv_sc", "w_out", "w_gate", "w_up", "w_down"]
    outs = [loss, grad_x]
    for kind in range(4):
        leaves = small_leaves(small_out[kind])
        for nm, res in zip(names, big_out):
            leaves[nm] = res[kind][None]
        outs += [leaves[nm] for nm in order]
    return tuple(outs)
```

```python
import functools

import jax
import jax.numpy as jnp
from jax import lax
from jax.experimental import pallas as pl
from jax.experimental.pallas import tpu as pltpu

F32 = jnp.float32
BF16 = jnp.bfloat16
HIGHEST = lax.Precision.HIGHEST
MESH = pl.DeviceIdType.MESH

D = 1024
N_META = 16
HEADS = 4
DH = 128
GW = HEADS * DH
SW = D - GW
FF = 2816
CH = 64
KQ = 4
KS = 3
EPS = 1e-6
IN_W = 3 * GW + GW + 2 * HEADS + 3 * SW
Q0, K0, V0, Z0, BA0, SX0, SB0, SC0, PW = 0, 512, 1024, 1536, 2048, 2304, 2816, 3328, 3840
BA_W = SX0 - BA0
N_DEV = 8
VMEM_LIMIT = 56 * 1024 * 1024

ADAM_LR, ADAM_B1, ADAM_B2, ADAM_EPS, ADAM_WD, ADAM_STEP = 0.001, 0.9, 0.999, 1e-08, 0.01, 10


def _cparams(sem):
    return pltpu.CompilerParams(dimension_semantics=sem, vmem_limit_bytes=VMEM_LIMIT)


def _div_tile(n, target, mult=16):
    best = None
    for t in range(mult, min(n, target) + 1, mult):
        if n % t == 0:
            best = t
    assert best is not None, (n, target)
    return best


def _silu(x):
    return x * jax.nn.sigmoid(x)


def _dsilu(x):
    s = jax.nn.sigmoid(x)
    return s * (1.0 + x * (1.0 - s))


def _bdot(a, b, dims):
    return lax.dot_general(a.astype(BF16), b.astype(BF16), (dims, ((), ())), preferred_element_type=F32)


NN = ((1,), (0,))
NT = ((1,), (1,))
TN = ((0,), (0,))


def _fdot(a, b, dims=NN):
    return lax.dot_general(a, b, (dims, ((), ())), precision=HIGHEST, preferred_element_type=F32)


def _mm(name, pairs, mode, tm, tn, tk, out_dtypes=(F32,), epilogue=None, extras=()):
    a0, b0 = pairs[0]
    if mode == "nn":
        (m, k), n = a0.shape, b0.shape[1]
    elif mode == "nt":
        (m, k), n = a0.shape, b0.shape[0]
    else:
        (k, m), n = a0.shape, b0.shape[1]
    assert m % tm == 0 and n % tn == 0 and k % tk == 0, (name, m, n, k, tm, tn, tk)
    nk = k // tk
    npair = len(pairs)
    nex = len(extras)
    nout = len(out_dtypes)
    dims = {"nn": NN, "nt": NT, "tn": TN}[mode]

    def body(*refs):
        ab = refs[:2 * npair]
        ex = refs[2 * npair:2 * npair + nex]
        outs = refs[2 * npair + nex:2 * npair + nex + nout]
        acc_ref = refs[-1]
        kk = pl.program_id(2)

        part = None
        for p in range(npair):
            d = lax.dot_general(ab[2 * p][...], ab[2 * p + 1][...], (dims, ((), ())), preferred_element_type=F32)
            part = d if part is None else part + d

        def finish(acc):
            res = epilogue(acc, *[e[...] for e in ex]) if epilogue is not None else (acc,)
            for o, r in zip(outs, res):
                o[...] = r.astype(o.dtype)

        if nk == 1:
            finish(part)
        else:
            @pl.when(kk == 0)
            def _():
                acc_ref[...] = part

            @pl.when(kk > 0)
            def _():
                acc_ref[...] += part

            @pl.when(kk == nk - 1)
            def _():
                finish(acc_ref[...])

    if mode == "nn":
        a_spec = pl.BlockSpec((tm, tk), lambda i, j, q: (i, q))
        b_spec = pl.BlockSpec((tk, tn), lambda i, j, q: (q, j))
    elif mode == "nt":
        a_spec = pl.BlockSpec((tm, tk), lambda i, j, q: (i, q))
        b_spec = pl.BlockSpec((tn, tk), lambda i, j, q: (j, q))
    else:
        a_spec = pl.BlockSpec((tk, tm), lambda i, j, q: (q, i))
        b_spec = pl.BlockSpec((tk, tn), lambda i, j, q: (q, j))
    o_spec = pl.BlockSpec((tm, tn), lambda i, j, q: (i, j))
    flat = [t for pr in pairs for t in pr]
    res = pl.pallas_call(
        body, name=name,
        grid=(m // tm, n // tn, nk),
        in_specs=[a_spec, b_spec] * npair + [o_spec] * nex,
        out_specs=[o_spec] * nout,
        out_shape=[jax.ShapeDtypeStruct((m, n), dt) for dt in out_dtypes],
        scratch_shapes=[pltpu.VMEM((tm, tn), F32)],
        compiler_params=_cparams(("parallel", "parallel", "arbitrary")),
    )(*flat, *extras)
    return res


def _ffn_up(u2, wg, wu, tm, tn):
    m, k = u2.shape
    n = wg.shape[1]

    def body(a_ref, g_ref, u_ref, gate_ref, up_ref, act_ref):
        a = a_ref[...]
        g = jnp.dot(a, g_ref[...], preferred_element_type=F32)
        u = jnp.dot(a, u_ref[...], preferred_element_type=F32)
        gate_ref[...] = g
        up_ref[...] = u
        act_ref[...] = (_silu(g) * u).astype(BF16)

    o_spec = pl.BlockSpec((tm, tn), lambda i, j: (i, j))
    w_spec = pl.BlockSpec((k, tn), lambda i, j: (0, j))
    return pl.pallas_call(
        body, name="ffn_up", grid=(m // tm, n // tn),
        in_specs=[pl.BlockSpec((tm, k), lambda i, j: (i, 0)), w_spec, w_spec],
        out_specs=[o_spec, o_spec, o_spec],
        out_shape=[jax.ShapeDtypeStruct((m, n), F32), jax.ShapeDtypeStruct((m, n), F32),
                   jax.ShapeDtypeStruct((m, n), BF16)],
        compiler_params=_cparams(("parallel", "parallel")),
    )(u2, wg, wu)


def _rms(x, gain):
    r = lax.rsqrt(jnp.mean(x * x, axis=-1, keepdims=True) + EPS)
    return x * r * gain, r


def _rms_bwd(x, gain, dy):
    r = lax.rsqrt(jnp.mean(x * x, axis=-1, keepdims=True) + EPS)
    dyw = dy * gain
    dx = r * dyw - x * (r * r * r) * jnp.mean(dyw * x, axis=-1, keepdims=True)
    return dx, dy * x * r


def _row_specs(tr, width, nb):
    return pl.BlockSpec((tr, width), lambda b, j: (b * nb + j, 0))


def _vec_spec(width):
    return pl.BlockSpec((1, width), lambda b, j: (0, 0))


def _first_step(b, j):
    return jnp.logical_and(b == 0, j == 0)


def _pre_norm(h0, gain, bsz, lp, tr):
    nb = lp // tr

    def body(h_ref, g_ref, u_ref):
        u_ref[...] = _rms(h_ref[...], g_ref[...])[0].astype(BF16)

    return pl.pallas_call(
        body, name="mix_pre_norm", grid=(bsz, nb),
        in_specs=[_row_specs(tr, D, nb), _vec_spec(D)],
        out_specs=_row_specs(tr, D, nb),
        out_shape=jax.ShapeDtypeStruct(h0.shape, BF16),
        compiler_params=_cparams(("parallel", "parallel")),
    )(h0, gain)


def _mid_norms(h0, mix, g_post, g_pre, bsz, lp, tr):
    nb = lp // tr

    def body(h_ref, m_ref, gp_ref, gq_ref, h1_ref, u2_ref):
        h1 = h_ref[...] + _rms(m_ref[...], gp_ref[...])[0]
        h1_ref[...] = h1
        u2_ref[...] = _rms(h1, gq_ref[...])[0].astype(BF16)

    rs = _row_specs(tr, D, nb)
    return pl.pallas_call(
        body, name="mid_norms", grid=(bsz, nb),
        in_specs=[rs, rs, _vec_spec(D), _vec_spec(D)],
        out_specs=[rs, rs],
        out_shape=[jax.ShapeDtypeStruct(h0.shape, F32), jax.ShapeDtypeStruct(h0.shape, BF16)],
        compiler_params=_cparams(("parallel", "parallel")),
    )(h0, mix, g_post, g_pre)


def _loss_head(h1, ffn, tgt, g_post, bsz, lp, tr, first_real):
    nb = lp // tr

    def body(h_ref, f_ref, t_ref, g_ref, dy_ref, dffn_ref, dg_ref, loss_ref):
        b, j = pl.program_id(0), pl.program_id(1)
        f = f_ref[...]
        g = g_ref[...]
        y = h_ref[...] + _rms(f, g)[0]
        t = j * tr + lax.broadcasted_iota(jnp.int32, (tr, 1), 0)
        err = jnp.where(t >= first_real, y - t_ref[...], 0.0)
        dy = err * (1.0 / D)
        dy_ref[...] = dy
        dx, dgc = _rms_bwd(f, g, dy)
        dffn_ref[...] = dx.astype(BF16)
        lsum = jnp.sum(jnp.sum(err * err, axis=0, keepdims=True), axis=1, keepdims=True) * (0.5 / D)

        @pl.when(_first_step(b, j))
        def _():
            dg_ref[...] = jnp.zeros_like(dg_ref)
            loss_ref[...] = jnp.zeros_like(loss_ref)

        dg_ref[...] += jnp.sum(dgc, axis=0, keepdims=True)
        loss_ref[...] += jnp.broadcast_to(lsum, loss_ref.shape)

    rs = _row_specs(tr, D, nb)
    return pl.pallas_call(
        body, name="loss_head", grid=(bsz, nb),
        in_specs=[rs, rs, rs, _vec_spec(D)],
        out_specs=[rs, rs, _vec_spec(D), _vec_spec(DH)],
        out_shape=[jax.ShapeDtypeStruct(h1.shape, F32), jax.ShapeDtypeStruct(h1.shape, BF16),
                   jax.ShapeDtypeStruct((1, D), F32), jax.ShapeDtypeStruct((1, DH), F32)],
        compiler_params=_cparams(("arbitrary", "arbitrary")),
    )(h1, ffn, tgt, g_post)


def _mid_norms_bwd(h1, mix, du2, dy, g_pre, g_post, bsz, lp, tr):
    nb = lp // tr

    def body(h_ref, m_ref, du_ref, dy_ref, gq_ref, gp_ref, dh1_ref, dmix_ref, dgq_ref, dgp_ref):
        b, j = pl.program_id(0), pl.program_id(1)
        dx, dgq = _rms_bwd(h_ref[...], gq_ref[...], du_ref[...])
        dh1 = dy_ref[...] + dx
        dh1_ref[...] = dh1
        dm, dgp = _rms_bwd(m_ref[...], gp_ref[...], dh1)
        dmix_ref[...] = dm.astype(BF16)

        @pl.when(_first_step(b, j))
        def _():
            dgq_ref[...] = jnp.zeros_like(dgq_ref)
            dgp_ref[...] = jnp.zeros_like(dgp_ref)

        dgq_ref[...] += jnp.sum(dgq, axis=0, keepdims=True)
        dgp_ref[...] += jnp.sum(dgp, axis=0, keepdims=True)

    rs = _row_specs(tr, D, nb)
    return pl.pallas_call(
        body, name="mid_norms_bwd", grid=(bsz, nb),
        in_specs=[rs, rs, rs, rs, _vec_spec(D), _vec_spec(D)],
        out_specs=[rs, rs, _vec_spec(D), _vec_spec(D)],
        out_shape=[jax.ShapeDtypeStruct(h1.shape, F32), jax.ShapeDtypeStruct(h1.shape, BF16),
                   jax.ShapeDtypeStruct((1, D), F32), jax.ShapeDtypeStruct((1, D), F32)],
        compiler_params=_cparams(("arbitrary", "arbitrary")),
    )(h1, mix, du2, dy, g_pre, g_post)


def _pre_norm_bwd(h0, du1, dh1, gain, bsz, lp, tr):
    nb = lp // tr

    def body(h_ref, du_ref, dh1_ref, g_ref, dh0_ref, dg_ref):
        b, j = pl.program_id(0), pl.program_id(1)
        dx, dgc = _rms_bwd(h_ref[...], g_ref[...], du_ref[...])
        dh0_ref[...] = dh1_ref[...] + dx

        @pl.when(_first_step(b, j))
        def _():
            dg_ref[...] = jnp.zeros_like(dg_ref)

        dg_ref[...] += jnp.sum(dgc, axis=0, keepdims=True)

    rs = _row_specs(tr, D, nb)
    return pl.pallas_call(
        body, name="pre_norm_bwd", grid=(bsz, nb),
        in_specs=[rs, rs, rs, _vec_spec(D)],
        out_specs=[rs, _vec_spec(D)],
        out_shape=[jax.ShapeDtypeStruct(h0.shape, F32), jax.ShapeDtypeStruct((1, D), F32)],
        compiler_params=_cparams(("arbitrary", "arbitrary")),
    )(h0, du1, dh1, gain)


def _shift_down(x, s, lp):
    return x if s == 0 else pltpu.roll(x, s, axis=0)


def _shift_up(x, s, lp):
    return x if s == 0 else pltpu.roll(x, lp - s, axis=0)


def _conv_fwd(x, w, taps, lp):
    y = None
    for i in range(taps):
        term = _shift_down(x, taps - 1 - i, lp) * w[i:i + 1, :]
        y = term if y is None else y + term
    return y


def _conv_bwd(x, w, dy, taps, lp):
    dx = None
    dws = []
    for i in range(taps):
        s = taps - 1 - i
        term = _shift_up(dy, s, lp) * w[i:i + 1, :]
        dx = term if dx is None else dx + term
        dws.append(jnp.sum(dy * _shift_down(x, s, lp), axis=0, keepdims=True))
    return dx, jnp.concatenate(dws, axis=0)


def _slab(lp, col0):
    return pl.BlockSpec((lp, DH), lambda b, s: (b, col0 // DH + s))


def _qkv_act(c, s):
    y = _silu(c)
    r = lax.rsqrt(jnp.sum(y * y, axis=-1, keepdims=True) + EPS)
    scale = jnp.where(s < HEADS, DH ** -0.5, 1.0)
    return y, r, scale


def _gdn_prep(proj, conv_qkv, bsz, lp, padf):
    def body(p_ref, w_ref, o_ref):
        s = pl.program_id(1)
        c = _conv_fwd(p_ref[...], w_ref[...], KQ, lp)
        y, r, scale = _qkv_act(c, s)
        out = jnp.where(s < 2 * HEADS, y * (r * scale), y)
        t = lax.broadcasted_iota(jnp.int32, (lp, 1), 0)
        o_ref[...] = jnp.where(t >= padf, out, 0.0)

    return pl.pallas_call(
        body, name="gdn_prep", grid=(bsz, 3 * HEADS),
        in_specs=[_slab(lp, Q0), pl.BlockSpec((KQ, DH), lambda b, s: (0, s))],
        out_specs=_slab(lp, 0),
        out_shape=jax.ShapeDtypeStruct((bsz * lp, 3 * GW), F32),
        compiler_params=_cparams(("parallel", "parallel")),
    )(proj, conv_qkv)


def _gdn_prep_bwd(proj, conv_qkv, dqkv, bsz, lp, padf):
    def body(p_ref, w_ref, d_ref, dp_ref, dw_ref):
        s, b = pl.program_id(0), pl.program_id(1)
        x = p_ref[...]
        w = w_ref[...]
        c = _conv_fwd(x, w, KQ, lp)
        y, r, scale = _qkv_act(c, s)
        dn = d_ref[...] * scale
        n = y * r
        dy_norm = r * (dn - n * jnp.sum(dn * n, axis=-1, keepdims=True))
        dyy = jnp.where(s < 2 * HEADS, dy_norm, dn)
        t = lax.broadcasted_iota(jnp.int32, (lp, 1), 0)
        dc = jnp.where(t >= padf, dyy * _dsilu(c), 0.0)
        dx, dw = _conv_bwd(x, w, dc, KQ, lp)
        dp_ref[...] = jnp.where(t >= padf, dx, 0.0).astype(BF16)

        @pl.when(b == 0)
        def _():
            dw_ref[...] = jnp.zeros_like(dw_ref)

        dw_ref[...] += dw

    sl = lambda col0: pl.BlockSpec((lp, DH), lambda s, b: (b, col0 // DH + s))
    return pl.pallas_call(
        body, name="gdn_prep_bwd", grid=(3 * HEADS, bsz),
        in_specs=[sl(Q0), pl.BlockSpec((KQ, DH), lambda s, b: (0, s)), sl(0)],
        out_specs=[sl(0), pl.BlockSpec((KQ, DH), lambda s, b: (0, s))],
        out_shape=[jax.ShapeDtypeStruct((bsz * lp, 3 * GW), BF16), jax.ShapeDtypeStruct((KQ, 3 * GW), F32)],
        compiler_params=_cparams(("parallel", "arbitrary")),
    )(proj, conv_qkv, dqkv)


def _softplus(x):
    e = jnp.exp(-jnp.abs(x))
    one_e = 1.0 + e
    l1p = jnp.where(one_e == 1.0, e, jnp.log(one_e) * (e / (one_e - 1.0)))
    return jnp.maximum(x, 0.0) + l1p


def _tri(n, kind):
    i = lax.broadcasted_iota(jnp.int32, (n, n), 0)
    j = lax.broadcasted_iota(jnp.int32, (n, n), 1)
    return {"incl": i >= j, "strict": i > j, "upper": i <= j}[kind]


def _gates(proj, alog_v, dtb_v, bsz, lp, padf):
    nchunk = lp // CH

    def body(p_ref, al_ref, dt_ref, gb_ref, cs_ref):
        x = p_ref[...]
        lane = lax.broadcasted_iota(jnp.int32, (lp, DH), 1)
        t = lax.broadcasted_iota(jnp.int32, (lp, DH), 0)
        beta = jax.nn.sigmoid(x)
        g = -jnp.exp(al_ref[...]) * _softplus(x + dt_ref[...])
        gb = jnp.where(lane < HEADS, beta, jnp.where(lane < 2 * HEADS, g, 0.0))
        gb_ref[...] = jnp.where(t >= padf, gb, 0.0)
        ltri = _tri(CH, "incl").astype(F32)

        def step(n, carry):
            r0 = pl.multiple_of(n * CH, CH)
            cs_ref[pl.ds(r0, CH), :] = _fdot(ltri, gb_ref[pl.ds(r0, CH), :])
            return carry

        lax.fori_loop(0, nchunk, step, 0)

    blk = pl.BlockSpec((lp, DH), lambda b: (b, BA0 // DH))
    vec = pl.BlockSpec((1, DH), lambda b: (0, 0))
    out = pl.BlockSpec((lp, DH), lambda b: (b, 0))
    return pl.pallas_call(
        body, name="gates", grid=(bsz,),
        in_specs=[blk, vec, vec], out_specs=[out, out],
        out_shape=[jax.ShapeDtypeStruct((bsz * lp, DH), F32)] * 2,
        compiler_params=_cparams(("parallel",)),
    )(proj, alog_v, dtb_v)


def _gates_bwd(proj, alog_v, dtb_v, dbeta_b, dg_b, gb, bsz, lp, padf):
    def body(p_ref, al_ref, dt_ref, db_ref, dg_ref, gb_ref, dl_ref, dv_ref):
        b = pl.program_id(0)
        x = p_ref[...]
        lane = lax.broadcasted_iota(jnp.int32, (lp, DH), 1)
        t = lax.broadcasted_iota(jnp.int32, (lp, DH), 0)
        dbeta = jnp.zeros((lp, DH), F32)
        dg = jnp.zeros((lp, DH), F32)
        for h in range(HEADS):
            dbeta = jnp.where(lane == h, db_ref[h], dbeta)
            dg = jnp.where(lane == HEADS + h, dg_ref[h], dg)
        real = t >= padf
        dbeta = jnp.where(real, dbeta, 0.0)
        dg = jnp.where(real, dg, 0.0)
        sb = jax.nn.sigmoid(x)
        neg_ea = -jnp.exp(al_ref[...])
        dsp = dg * neg_ea * jax.nn.sigmoid(x + dt_ref[...])
        dl = dbeta * sb * (1.0 - sb) + dsp
        dl_ref[...] = jnp.concatenate([dl, jnp.zeros((lp, BA_W - DH), F32)], axis=1).astype(BF16)
        dalog = jnp.sum(dg * gb_ref[...], axis=0, keepdims=True)
        ddt = jnp.sum(dsp, axis=0, keepdims=True)

        @pl.when(b == 0)
        def _():
            dv_ref[...] = jnp.zeros_like(dv_ref)

        dv_ref[...] += jnp.concatenate([dalog, ddt], axis=0)

    blk = pl.BlockSpec((lp, DH), lambda b: (b, BA0 // DH))
    vec = pl.BlockSpec((1, DH), lambda b: (0, 0))
    hb = pl.BlockSpec((HEADS, lp, DH), lambda b: (b, 0, 0))
    return pl.pallas_call(
        body, name="gates_bwd", grid=(bsz,),
        in_specs=[blk, vec, vec, hb, hb, pl.BlockSpec((lp, DH), lambda b: (b, 0))],
        out_specs=[pl.BlockSpec((lp, BA_W), lambda b: (b, 0)), pl.BlockSpec((2, DH), lambda b: (0, 0))],
        out_shape=[jax.ShapeDtypeStruct((bsz * lp, BA_W), BF16), jax.ShapeDtypeStruct((2, DH), F32)],
        compiler_params=_cparams(("arbitrary",)),
    )(proj, alog_v, dtb_v, dbeta_b, dg_b, gb)


def _mix_heads(proj, o, gdn_norm, conv_sc, bsz, lp, padf):
    def body(o_ref, z_ref, gn_ref, sx_ref, sb_ref, sc_ref, w_ref, og_ref, os_ref):
        og_ref[...] = (_rms(o_ref[...], gn_ref[...])[0] * _silu(z_ref[...])).astype(BF16)
        conv = _conv_fwd(sc_ref[...] * sx_ref[...], w_ref[...], KS, lp)
        t = lax.broadcasted_iota(jnp.int32, (lp, 1), 0)
        os_ref[...] = jnp.where(t >= padf, sb_ref[...] * conv, 0.0).astype(BF16)

    half = _slab(lp, 0)
    return pl.pallas_call(
        body, name="mix_heads", grid=(bsz, HEADS),
        in_specs=[half, _slab(lp, Z0), pl.BlockSpec((1, DH), lambda b, s: (0, 0)),
                  _slab(lp, SX0), _slab(lp, SB0), _slab(lp, SC0), pl.BlockSpec((KS, DH), lambda b, s: (0, s))],
        out_specs=[half, half],
        out_shape=[jax.ShapeDtypeStruct((bsz * lp, GW), BF16)] * 2,
        compiler_params=_cparams(("parallel", "parallel")),
    )(o, proj, gdn_norm, proj, proj, proj, conv_sc)


def _mix_heads_bwd(proj, o, gdn_norm, conv_sc, dmixin, bsz, lp, padf):
    def body(o_ref, z_ref, gn_ref, sx_ref, sb_ref, sc_ref, w_ref, dg_ref, ds_ref,
             do_ref, dz_ref, dsx_ref, dsb_ref, dsc_ref, dgn_ref, dw_ref):
        s, b = pl.program_id(0), pl.program_id(1)
        t = lax.broadcasted_iota(jnp.int32, (lp, 1), 0)
        real = t >= padf
        o, z, gn, d = o_ref[...], z_ref[...], gn_ref[...], dg_ref[...]
        sz = _silu(z)
        on = _rms(o, gn)[0]
        dz_ref[...] = (d * on * _dsilu(z)).astype(BF16)
        dox, dgn = _rms_bwd(o, gn, d * sz)
        do_ref[...] = dox
        sx, sb, sc, w, e = sx_ref[...], sb_ref[...], sc_ref[...], w_ref[...], ds_ref[...]
        e = jnp.where(real, e, 0.0)
        xin = sc * sx
        conv = _conv_fwd(xin, w, KS, lp)
        dsb_ref[...] = (e * conv).astype(BF16)
        dxin, dw = _conv_bwd(xin, w, e * sb, KS, lp)
        dxin = jnp.where(real, dxin, 0.0)
        dsx_ref[...] = (dxin * sc).astype(BF16)
        dsc_ref[...] = (dxin * sx).astype(BF16)

        @pl.when(jnp.logical_and(s == 0, b == 0))
        def _():
            dgn_ref[...] = jnp.zeros_like(dgn_ref)

        @pl.when(b == 0)
        def _():
            dw_ref[...] = jnp.zeros_like(dw_ref)

        dgn_ref[...] += jnp.sum(dgn, axis=0, keepdims=True)
        dw_ref[...] += dw

    sl = lambda col0: pl.BlockSpec((lp, DH), lambda s, b: (b, col0 // DH + s))
    half = sl(0)
    return pl.pallas_call(
        body, name="mix_heads_bwd", grid=(HEADS, bsz),
        in_specs=[half, sl(Z0), pl.BlockSpec((1, DH), lambda s, b: (0, 0)), sl(SX0), sl(SB0), sl(SC0),
                  pl.BlockSpec((KS, DH), lambda s, b: (0, s)), sl(0), sl(GW)],
        out_specs=[half] * 5 + [pl.BlockSpec((1, DH), lambda s, b: (0, 0)), pl.BlockSpec((KS, DH), lambda s, b: (0, s))],
        out_shape=[jax.ShapeDtypeStruct((bsz * lp, GW), F32)] + [jax.ShapeDtypeStruct((bsz * lp, GW), BF16)] * 4
        + [jax.ShapeDtypeStruct((1, DH), F32), jax.ShapeDtypeStruct((KS, SW), F32)],
        compiler_params=_cparams(("arbitrary", "arbitrary")),
    )(o, proj, gdn_norm, proj, proj, proj, conv_sc, dmixin, dmixin)


def _split_dot(a, b, dims=NN):
    ah = a.astype(BF16)
    al = (a - ah.astype(F32)).astype(BF16)
    bh = b.astype(BF16)
    bl = (b - bh.astype(F32)).astype(BF16)

    def d(x, y):
        return lax.dot_general(x, y, (dims, ((), ())), preferred_element_type=F32)

    return d(ah, bh) + (d(ah, bl) + d(al, bh))


def _eye(n):
    return (lax.broadcasted_iota(jnp.int32, (n, n), 0) == lax.broadcasted_iota(jnp.int32, (n, n), 1)).astype(F32)


def _each(f, *lists):
    return [f(*xs) for xs in zip(*lists)]


def _inv_unit_lower(a_list):
    eye = _eye(CH)
    p = _each(lambda a: -a, a_list)
    t = _each(lambda x: eye + x, p)
    p = _each(lambda x: _split_dot(x, x), p)
    for level in range(5):
        if level < 4:
            prod = _each(lambda tt, pp: _split_dot(jnp.concatenate([tt, pp], axis=0), pp), t, p)
            t = _each(lambda tt, pr: tt + pr[:CH], t, prod)
            p = _each(lambda pr: pr[CH:], prod)
        else:
            t = _each(lambda tt, pp: tt + _split_dot(tt, pp), t, p)
    return t


def _chunk_gates(gc_ref, be_ref, gt_ref, g):
    rows = pl.ds(g * CH, CH)
    gc = gc_ref[0, rows, :]
    be = be_ref[0, rows, :]
    gcl = gc_ref[0, pl.ds(g * CH + CH - 1, 1), :]
    gt = gt_ref[0, g]
    incl, upper = _tri(CH, "incl"), _tri(CH, "upper")
    dec = jnp.where(incl, jnp.exp(jnp.where(incl, gc[:, :CH] - gt, 0.0)), 0.0)
    dec_t = jnp.where(upper, jnp.exp(jnp.where(upper, gt - gc[:, :CH], 0.0)), 0.0)
    return rows, be, dec, dec_t, jnp.exp(gc), jnp.exp(gcl - gc), jnp.exp(gcl)


def _chunk_group(nchunk):
    return 3 if nchunk % 3 == 0 else 1


def _local_group(nchunk):
    return 11 if nchunk % 11 == 0 else _chunk_group(nchunk)


def _gdn_local_specs(lp, grp):
    ngrp = lp // (grp * CH)
    head = lambda col0: pl.BlockSpec((grp * CH, DH), lambda i, j: ((i // HEADS) * ngrp + j, col0 // DH + i % HEADS))
    lane_b = pl.BlockSpec((1, grp * CH, DH), lambda i, j: (i, j, 0))
    per_chunk = lambda r, c: pl.BlockSpec((1, grp, r, c), lambda i, j: (i, j, 0, 0))
    return ngrp, head, lane_b, per_chunk


def _gdn_local_fwd(qkv, gcb, bb, gct, bsz, lp):
    nchunk = lp // CH
    grp = _local_group(nchunk)
    ngrp, head, lane_b, per_chunk = _gdn_local_specs(lp, grp)
    bh = bsz * HEADS

    def body(q_ref, k_ref, v_ref, gc_ref, be_ref, gt_ref,
             u_ref, l1_ref, l2_ref, l3_ref, wt_ref, kd_ref, ti_ref, gl_ref):
        incl, strict, upper = _tri(CH, "incl"), _tri(CH, "strict"), _tri(CH, "upper")
        eye = _eye(CH)
        rows, be, dec, dec_t, eg, ekd, gl = zip(*[_chunk_gates(gc_ref, be_ref, gt_ref, g) for g in range(grp)])
        q = [q_ref[r, :] for r in rows]
        k = [k_ref[r, :] for r in rows]
        v = [v_ref[r, :] for r in rows]
        kb = _each(lambda x, y: x * y, k, be)
        a = _each(lambda x, y, d: jnp.where(strict, _bdot(x, y, NT) * d, 0.0), kb, k, dec)
        tinv = _inv_unit_lower(a)
        u = _each(lambda t, x, y: _bdot(t, x * y, NN), tinv, v, be)
        w = _each(lambda t, x, y: _bdot(t, x * y, NN), tinv, kb, eg)
        qk = _each(lambda x, y, d: jnp.where(incl, _bdot(x, y, NT) * d, 0.0), q, k, dec)
        qk_t = _each(lambda x, y, d: jnp.where(upper, _bdot(y, x, NT) * d, 0.0), q, k, dec_t)
        qd = _each(lambda x, y: x * y, q, eg)
        kd = _each(lambda x, y: x * y, k, ekd)
        kd_t = _each(lambda x: _bdot(x, eye, TN), kd)
        qd_t = _each(lambda x: _bdot(x, eye, TN), qd)
        w_t = _each(lambda x: _bdot(x, eye, TN), w)
        for g in range(grp):
            u_ref[rows[g], :] = u[g]
            l1_ref[0, g] = jnp.concatenate([w[g], qd[g]], axis=0).astype(BF16)
            l2_ref[0, g] = jnp.concatenate([qk[g], kd_t[g]], axis=0).astype(BF16)
            l3_ref[0, g] = jnp.concatenate([qk_t[g], qd_t[g]], axis=0).astype(BF16)
            wt_ref[0, g] = w_t[g].astype(BF16)
            kd_ref[0, g] = kd[g].astype(BF16)
            ti_ref[0, g] = tinv[g]
            gl_ref[0, g] = gl[g]

    r = bsz * lp
    shapes = [((r, GW), F32), ((bh, nchunk, 2 * CH, DH), BF16), ((bh, nchunk, 3 * CH, CH), BF16),
              ((bh, nchunk, 3 * CH, CH), BF16), ((bh, nchunk, DH, CH), BF16), ((bh, nchunk, CH, DH), BF16),
              ((bh, nchunk, CH, CH), F32), ((bh, nchunk, 1, DH), F32)]
    return pl.pallas_call(
        body, name="gdn_local_fwd", grid=(bh, ngrp),
        in_specs=[head(0), head(GW), head(2 * GW), lane_b, lane_b, per_chunk(1, CH)],
        out_specs=[head(0), per_chunk(2 * CH, DH), per_chunk(3 * CH, CH), per_chunk(3 * CH, CH), per_chunk(DH, CH),
                   per_chunk(CH, DH), per_chunk(CH, CH), per_chunk(1, DH)],
        out_shape=[jax.ShapeDtypeStruct(s, dt) for s, dt in shapes],
        compiler_params=_cparams(("parallel", "parallel")),
    )(qkv, qkv, qkv, gcb, bb, gct)


def _scan_specs(bsz, lp, grp, order):
    ngrp = lp // (grp * CH)
    bh = bsz * HEADS
    seq = lambda: pl.BlockSpec((bsz, grp * CH, GW), lambda s: (0, order(s, ngrp), 0))
    per_chunk = lambda r, c: pl.BlockSpec((bh, grp, r, c), lambda s: (0, order(s, ngrp), 0, 0))
    return ngrp, bh, seq, per_chunk


def _gdn_scan_fwd(u, l1, l2, gl, bsz, lp):
    nchunk = lp // CH
    grp = _chunk_group(nchunk)
    ngrp, bh, seq, per_chunk = _scan_specs(bsz, lp, grp, lambda s, n: s)

    def body(u_ref, l1_ref, l2_ref, gl_ref, o_ref, vn_ref, st_ref, s_scr):
        @pl.when(pl.program_id(0) == 0)
        def _():
            s_scr[...] = jnp.zeros_like(s_scr)

        for c in range(grp):
            rows = slice(c * CH, (c + 1) * CH)
            for i in range(bh):
                b, h = divmod(i, HEADS)
                cols = slice(h * DH, (h + 1) * DH)
                s = s_scr[i]
                st_ref[i, c] = s
                x = jnp.dot(l1_ref[i, c], s.astype(BF16), preferred_element_type=F32)
                vnew = u_ref[b, rows, cols] - x[:CH]
                y = jnp.dot(l2_ref[i, c], vnew.astype(BF16), preferred_element_type=F32)
                o_ref[b, rows, cols] = x[CH:] + y[:CH]
                vn_ref[b, rows, cols] = vnew.astype(BF16)
                s_scr[i] = s * gl_ref[i, c] + y[CH:]

    return pl.pallas_call(
        body, name="gdn_scan_fwd", grid=(ngrp,),
        in_specs=[seq(), per_chunk(2 * CH, DH), per_chunk(3 * CH, CH), per_chunk(1, DH)],
        out_specs=[seq(), seq(), per_chunk(DH, DH)],
        out_shape=[jax.ShapeDtypeStruct((bsz, lp, GW), F32), jax.ShapeDtypeStruct((bsz, lp, GW), BF16),
                   jax.ShapeDtypeStruct((bh, nchunk, DH, DH), F32)],
        scratch_shapes=[pltpu.VMEM((bh, DH, DH), F32)],
        compiler_params=_cparams(("arbitrary",)),
    )(u.reshape(bsz, lp, GW), l1, l2, gl)


def _gdn_scan_bwd(d_o, l3, wt, kd, gl, bsz, lp):
    nchunk = lp // CH
    grp = _chunk_group(nchunk)
    ngrp, bh, seq, per_chunk = _scan_specs(bsz, lp, grp, lambda s, n: n - 1 - s)

    def body(do_ref, l3_ref, wt_ref, kd_ref, gl_ref, dvn_ref, dst_ref, ds_scr):
        @pl.when(pl.program_id(0) == 0)
        def _():
            ds_scr[...] = jnp.zeros_like(ds_scr)

        for c in reversed(range(grp)):
            rows = slice(c * CH, (c + 1) * CH)
            for i in range(bh):
                b, h = divmod(i, HEADS)
                cols = slice(h * DH, (h + 1) * DH)
                ds = ds_scr[i]
                dst_ref[i, c] = ds
                p = jnp.dot(l3_ref[i, c], do_ref[b, rows, cols].astype(BF16), preferred_element_type=F32)
                dvn = p[:CH] + jnp.dot(kd_ref[i, c], ds.astype(BF16), preferred_element_type=F32)
                dvn_ref[b, rows, cols] = dvn.astype(BF16)
                ds_scr[i] = ds * gl_ref[i, c] + p[CH:] - jnp.dot(wt_ref[i, c], dvn.astype(BF16),
                                                                 preferred_element_type=F32)

    return pl.pallas_call(
        body, name="gdn_scan_bwd", grid=(ngrp,),
        in_specs=[seq(), per_chunk(3 * CH, CH), per_chunk(DH, CH), per_chunk(CH, DH), per_chunk(1, DH)],
        out_specs=[seq(), per_chunk(DH, DH)],
        out_shape=[jax.ShapeDtypeStruct((bsz, lp, GW), BF16), jax.ShapeDtypeStruct((bh, nchunk, DH, DH), F32)],
        scratch_shapes=[pltpu.VMEM((bh, DH, DH), F32)],
        compiler_params=_cparams(("arbitrary",)),
    )(d_o.reshape(bsz, lp, GW), l3, wt, kd, gl)


def _gdn_local_bwd(qkv, gcb, bb, gct, tinv_all, states, dstates, vnew, dvnew, d_o, bsz, lp):
    nchunk = lp // CH
    grp = _local_group(nchunk)
    ngrp, head, lane_b, per_chunk = _gdn_local_specs(lp, grp)
    bh = bsz * HEADS

    def body(q_ref, k_ref, v_ref, gc_ref, be_ref, gt_ref, ti_ref, st_ref, dst_ref, vn_ref, dvn_ref, do_ref,
             dq_ref, dk_ref, dv_ref, dg_ref, db_ref):
        incl, strict = _tri(CH, "incl"), _tri(CH, "strict")
        upper = _tri(CH, "upper").astype(F32)
        ones = jnp.ones((CH, DH), F32)
        last = lax.broadcasted_iota(jnp.int32, (CH, 1), 0) == CH - 1

        def rsum(x):
            return jnp.sum(x, axis=-1, keepdims=True)

        def mul(xs, ys):
            return _each(lambda x, y: x * y, xs, ys)

        rows, be, dec, _, eg, ekd, gl = zip(*[_chunk_gates(gc_ref, be_ref, gt_ref, g) for g in range(grp)])
        q = [q_ref[r, :] for r in rows]
        k = [k_ref[r, :] for r in rows]
        v = [v_ref[r, :] for r in rows]
        vnew = [vn_ref[r, :] for r in rows]
        dvn = [dvn_ref[r, :] for r in rows]
        do = [do_ref[r, :] for r in rows]
        tinv = [ti_ref[0, g] for g in range(grp)]
        s = [st_ref[0, g] for g in range(grp)]
        ds = [dst_ref[0, g] for g in range(grp)]
        kb, vb, qd, kd = mul(k, be), mul(v, be), mul(q, eg), mul(k, ekd)
        kbg = mul(kb, eg)
        a = _each(lambda x, y, d: jnp.where(strict, _bdot(x, y, NT) * d, 0.0), kb, k, dec)
        qk = _each(lambda x, y, d: jnp.where(incl, _bdot(x, y, NT) * d, 0.0), q, k, dec)
        dqk = _each(lambda x, y: jnp.where(incl, _bdot(x, y, NT), 0.0), do, vnew)
        dqd = _each(lambda x, y: _bdot(x, y, NT), do, s)
        dkd = _each(lambda x, y: _bdot(x, y, NT), vnew, ds)
        dgl = _each(lambda x, y: jnp.sum(rsum(x * y), axis=0, keepdims=True), s, ds)
        dw = _each(lambda x, y: -_bdot(x, y, NT), dvn, s)
        dvb = _each(lambda t, x: _bdot(t, x, TN), tinv, dvn)
        dkbg = _each(lambda t, x: _bdot(t, x, TN), tinv, dw)
        dt = _each(lambda x, y, z, w: _bdot(x, y, NT) + _bdot(z, w, NT), dvn, vb, dw, kbg)
        tdt = _each(lambda t, x: _split_dot(t, x, TN), tinv, dt)
        da = _each(lambda x, t: jnp.where(strict, -_split_dot(x, t, NT), 0.0), tdt, tinv)
        dp, dr = mul(da, dec), mul(dqk, dec)
        dkb = _each(lambda x, y, z, e: _bdot(x, y, NN) + z * e, dp, k, dkbg, eg)
        dk = _each(lambda p_, kb_, r_, q_, dkd_, ekd_, dkb_, be_:
                   _bdot(p_, kb_, TN) + _bdot(r_, q_, TN) + dkd_ * ekd_ + dkb_ * be_,
                   dp, kb, dr, q, dkd, ekd, dkb, be)
        dq = _each(lambda r_, k_, dqd_, eg_: _bdot(r_, k_, NN) + dqd_ * eg_, dr, k, dqd, eg)
        m = _each(lambda da_, a_, dqk_, qk_: da_ * a_ + dqk_ * qk_, da, a, dqk, qk)
        m_cols = _each(lambda x: _split_dot(x, ones, TN), m)
        kd_term = _each(lambda x, y: rsum(x * y), dkd, kd)
        dgc = _each(lambda m_, mc, dqd_, qd_, kt, dkbg_, kbg_, dgl_, gl_:
                    rsum(m_) - mc + rsum(dqd_ * qd_) - kt + rsum(dkbg_ * kbg_)
                    + jnp.where(last, jnp.sum(kt, axis=0, keepdims=True) + dgl_ * gl_, 0.0),
                    m, m_cols, dqd, qd, kd_term, dkbg, kbg, dgl, gl)
        dg = _each(lambda x: _split_dot(upper, x, NN), dgc)
        for g in range(grp):
            dq_ref[rows[g], :] = dq[g]
            dk_ref[rows[g], :] = dk[g]
            dv_ref[rows[g], :] = dvb[g] * be[g]
            dg_ref[0, rows[g], :] = dg[g]
            db_ref[0, rows[g], :] = jnp.broadcast_to(rsum(dkb[g] * k[g]) + rsum(dvb[g] * v[g]), (CH, DH))

    r = bsz * lp
    return pl.pallas_call(
        body, name="gdn_local_bwd", grid=(bh, ngrp),
        in_specs=[head(0), head(GW), head(2 * GW), lane_b, lane_b, per_chunk(1, CH), per_chunk(CH, CH),
                  per_chunk(DH, DH), per_chunk(DH, DH), head(0), head(0), head(0)],
        out_specs=[head(0), head(0), head(0), lane_b, lane_b],
        out_shape=[jax.ShapeDtypeStruct((r, GW), F32)] * 3 + [jax.ShapeDtypeStruct((bh, lp, DH), F32)] * 2,
        compiler_params=_cparams(("parallel", "parallel")),
    )(qkv, qkv, qkv, gcb, bb, gct, tinv_all, states, dstates, vnew, dvnew, d_o)


def _place():
    return lax.axis_index("x"), lax.axis_index("y"), lax.axis_index("c")


def _any_specs(n):
    return [pl.BlockSpec(memory_space=pl.ANY)] * n


def _all_gather(name, shards):
    n = len(shards)

    def body(*refs):
        ins, outs = refs[:n], refs[n:2 * n]
        send_sems, recv_sems, local_sems = refs[2 * n:]
        x, y, c = _place()
        me, sibling = (x, y, c), (x, y, 1 - c)
        chips = [(1 - x, y), (x, 1 - y), (1 - x, 1 - y)]

        def slot(i, p):
            return outs[i].at[4 * p[0] + 2 * p[1] + p[2]]

        def copy(i, k, block, to, src=None):
            return pltpu.make_async_remote_copy(
                src_ref=slot(i, block) if src is None else src, dst_ref=slot(i, block),
                send_sem=send_sems.at[i, k], recv_sem=recv_sems.at[i, k], device_id=to, device_id_type=MESH)

        mine = [pltpu.make_async_copy(ins[i], slot(i, me), local_sems.at[i]) for i in range(n)]
        for cp in mine:
            cp.start()
        first = []
        for i in range(n):
            first.append(copy(i, 0, me, sibling, src=ins[i]))
            first += [copy(i, 1 + j, me, (*chip, c), src=ins[i]) for j, chip in enumerate(chips)]
        for cp in first:
            cp.start()
        passed = []
        for j, chip in enumerate(chips):
            for i in range(n):
                copy(i, 1 + j, (*chip, c), me).wait_recv()
                fwd = copy(i, 4 + j, (*chip, c), sibling)
                fwd.start()
                passed.append(fwd)
        for i in range(n):
            copy(i, 0, sibling, me).wait_recv()
            for j, chip in enumerate(chips):
                copy(i, 4 + j, (*chip, 1 - c), me).wait_recv()
        for cp in first + passed:
            cp.wait_send()
        for cp in mine:
            cp.wait()

    return pl.pallas_call(
        body, name=name,
        in_specs=_any_specs(n), out_specs=_any_specs(n),
        out_shape=[jax.ShapeDtypeStruct((N_DEV,) + s.shape, s.dtype) for s in shards],
        scratch_shapes=[pltpu.SemaphoreType.DMA((n, 7)), pltpu.SemaphoreType.DMA((n, 7)),
                        pltpu.SemaphoreType.DMA((n,))],
    )(*shards)


def _sibling_exchange(name, grads):
    n = len(grads)

    def body(*refs):
        ins, outs = refs[:n], refs[n:2 * n]
        send_sems, recv_sems = refs[2 * n:]
        x, y, c = _place()
        cps = [pltpu.make_async_remote_copy(
            src_ref=ins[i].at[1 - c], dst_ref=outs[i], send_sem=send_sems.at[i], recv_sem=recv_sems.at[i],
            device_id=(x, y, 1 - c), device_id_type=MESH) for i in range(n)]
        for cp in cps:
            cp.start()
        for cp in cps:
            cp.wait_recv()
        for cp in cps:
            cp.wait_send()

    return pl.pallas_call(
        body, name=name,
        in_specs=_any_specs(n), out_specs=_any_specs(n),
        out_shape=[jax.ShapeDtypeStruct(g.shape[1:], g.dtype) for g in grads],
        scratch_shapes=[pltpu.SemaphoreType.DMA((n,)), pltpu.SemaphoreType.DMA((n,))],
    )(*grads)


def _chip_exchange(name, parts):
    n = len(parts)

    def body(*refs):
        ins, outs = refs[:n], refs[n:2 * n]
        send_sems, recv_sems, local_sems = refs[2 * n:]
        x, y, c = _place()
        chips = [(1 - x, y), (x, 1 - y), (1 - x, 1 - y)]
        mine = [pltpu.make_async_copy(ins[i].at[2 * x + y], outs[i].at[0], local_sems.at[i]) for i in range(n)]
        for cp in mine:
            cp.start()
        cps = []
        for j, chip in enumerate(chips):
            for i in range(n):
                cps.append(pltpu.make_async_remote_copy(
                    src_ref=ins[i].at[2 * chip[0] + chip[1]], dst_ref=outs[i].at[1 + j],
                    send_sem=send_sems.at[i, j], recv_sem=recv_sems.at[i, j],
                    device_id=(*chip, c), device_id_type=MESH))
        for cp in cps:
            cp.start()
        for cp in cps:
            cp.wait_recv()
        for cp in cps:
            cp.wait_send()
        for cp in mine:
            cp.wait()

    return pl.pallas_call(
        body, name=name,
        in_specs=_any_specs(n), out_specs=_any_specs(n),
        out_shape=[jax.ShapeDtypeStruct(p.shape, p.dtype) for p in parts],
        scratch_shapes=[pltpu.SemaphoreType.DMA((n, 3)), pltpu.SemaphoreType.DMA((n, 3)),
                        pltpu.SemaphoreType.DMA((n,))],
    )(*parts)


def _pair_sum(name, grads, recv, core):
    _, _, r, c = grads.shape
    tr = _div_tile(r, 256, 8)

    def body(core_ref, g_ref, r_ref, o_ref):
        o_ref[...] = g_ref[0] + r_ref[...]

    return pl.pallas_call(
        body, name=name,
        grid_spec=pltpu.PrefetchScalarGridSpec(
            num_scalar_prefetch=1, grid=(4, r // tr),
            in_specs=[pl.BlockSpec((1, 1, tr, c), lambda s, i, cr: (cr[0], s, i, 0)),
                      pl.BlockSpec((1, tr, c), lambda s, i, cr: (s, i, 0))],
            out_specs=pl.BlockSpec((1, tr, c), lambda s, i, cr: (s, i, 0))),
        out_shape=jax.ShapeDtypeStruct(recv.shape, F32),
        compiler_params=_cparams(("parallel", "parallel")),
    )(core, grads, recv)


def _adamw_math(w, g, m, v):
    m = ADAM_B1 * m + (1.0 - ADAM_B1) * g
    v = ADAM_B2 * v + (1.0 - ADAM_B2) * (g * g)
    m_hat = m / (1.0 - ADAM_B1 ** ADAM_STEP)
    v_hat = v / (1.0 - ADAM_B2 ** ADAM_STEP)
    delta = -ADAM_LR * (m_hat / (jnp.sqrt(v_hat) + ADAM_EPS) + ADAM_WD * w)
    return delta, m, v


def _adamw(name, parts, w, m, v):
    nslot, r, c = parts.shape
    tr = _div_tile(r, 256, 8) if r % 8 == 0 else r

    def body(p_ref, w_ref, m_ref, v_ref, g_out, d_out, m_out, v_out):
        g = p_ref[0]
        for s in range(1, nslot):
            g = g + p_ref[s]
        d, mn, vn = _adamw_math(w_ref[...], g, m_ref[...], v_ref[...])
        g_out[...] = g
        d_out[...] = d
        m_out[...] = mn
        v_out[...] = vn

    blk = pl.BlockSpec((tr, c), lambda i: (i, 0))
    return pl.pallas_call(
        body, name=name, grid=(r // tr,),
        in_specs=[pl.BlockSpec((nslot, tr, c), lambda i: (0, i, 0)), blk, blk, blk],
        out_specs=[blk] * 4,
        out_shape=[jax.ShapeDtypeStruct((r, c), F32)] * 4,
        compiler_params=_cparams(("parallel",)),
    )(parts, w, m, v)


SMALL_ROWS, SMALL_COLS = 32, 3 * GW


def _pack_small(gains4, meta, conv_qkv, conv_sc, a_log, dt_bias, gdn_norm):
    def padc(a):
        return jnp.pad(a, ((0, 0), (0, SMALL_COLS - a.shape[1])))
    scal = jnp.concatenate([jnp.pad(a_log, ((0, 0), (0, DH - HEADS))), jnp.pad(dt_bias, ((0, 0), (0, DH - HEADS))),
                            gdn_norm], axis=1)
    rows = [padc(g) for g in gains4] + [padc(meta), padc(conv_qkv), padc(conv_sc), padc(scal)]
    slab = jnp.concatenate(rows, axis=0)
    return jnp.pad(slab, ((0, SMALL_ROWS - slab.shape[0]), (0, 0)))


def _unpack_small(slab):
    gains = [slab[i:i + 1, :D] for i in range(4)]
    meta = slab[4:20, :D]
    conv_qkv = slab[20:24, :]
    conv_sc = slab[24:27, :SW]
    a_log = slab[27:28, 0:HEADS]
    dt_bias = slab[27:28, DH:DH + HEADS]
    gdn_norm = slab[27:28, 2 * DH:3 * DH]
    return gains, meta, conv_qkv, conv_sc, a_log, dt_bias, gdn_norm


def kernel(x, meta_tokens, mix_pre_norm, mix_post_norm, ffn_pre_norm, ffn_post_norm, w_in, conv_qkv, a_log, dt_bias, gdn_norm, conv_sc, w_out, w_gate, w_up, w_down, loss_target, m_meta_tokens, m_mix_pre_norm, m_mix_post_norm, m_ffn_pre_norm, m_ffn_post_norm, m_w_in, m_conv_qkv, m_a_log, m_dt_bias, m_gdn_norm, m_conv_sc, m_w_out, m_w_gate, m_w_up, m_w_down, v_meta_tokens, v_mix_pre_norm, v_mix_post_norm, v_ffn_pre_norm, v_ffn_post_norm, v_w_in, v_conv_qkv, v_a_log, v_dt_bias, v_gdn_norm, v_conv_sc, v_w_out, v_w_gate, v_w_up, v_w_down):
    bsz, seq, _ = x.shape
    lp = -(-(N_META + seq) // CH) * CH
    padf = lp - N_META - seq
    first_real = padf + N_META
    rows = bsz * lp
    nchunk = lp // CH
    tr = _div_tile(lp, 528)
    tm = _div_tile(rows, 1056)
    cx, cy, cc = _place()
    dev = 4 * cx + 2 * cy + cc

    small_shard = jnp.concatenate([
        jnp.pad(meta_tokens, ((0, 0), (0, 256 - DH))),
        jnp.pad(conv_qkv[0], ((0, 0), (0, 256 - 3 * GW // N_DEV))),
        jnp.pad(conv_sc[0], ((0, 5), (0, 256 - SW // N_DEV)))], axis=0)
    g_in, g_out, g_gate, g_up, g_down, g_small = _all_gather(
        "gather_weights",
        [w_in[0].astype(BF16), w_out[0].astype(BF16), w_gate[0].astype(BF16), w_up[0].astype(BF16),
         w_down[0].astype(BF16), small_shard])

    def cols_full(g):
        return jnp.transpose(g, (1, 0, 2)).reshape(g.shape[1], -1)

    win = cols_full(g_in)
    win = jnp.concatenate([win[:, :BA0 + 2 * HEADS], jnp.zeros((D, BA_W - 2 * HEADS), BF16),
                           win[:, BA0 + 2 * HEADS:]], axis=1)
    wout = g_out.reshape(D, D)
    wgate, wup = cols_full(g_gate), cols_full(g_up)
    wdown = g_down.reshape(FF, D)
    meta_full = cols_full(g_small[:, 0:16, :DH])
    convq_full = cols_full(g_small[:, 16:20, :3 * GW // N_DEV])
    convs_full = cols_full(g_small[:, 20:23, :SW // N_DEV])
    alog_v = jnp.pad(a_log, ((0, 0), (HEADS, DH - 2 * HEADS)))
    dtb_v = jnp.pad(dt_bias, ((0, 0), (HEADS, DH - 2 * HEADS)))

    h0 = jnp.concatenate([jnp.zeros((bsz, padf, D), F32), jnp.broadcast_to(meta_full[None], (bsz, N_META, D)), x],
                         axis=1).reshape(rows, D)
    tgt = jnp.pad(loss_target, ((0, 0), (first_real, 0), (0, 0))).reshape(rows, D)
    u1 = _pre_norm(h0, mix_pre_norm, bsz, lp, tr)
    (proj,) = _mm("proj_in", [(u1, win)], "nn", tm, 768, D)
    qkv = _gdn_prep(proj, convq_full, bsz, lp, padf)
    gb, gcs = _gates(proj, alog_v, dtb_v, bsz, lp, padf)

    def lane_bcast(cols):
        t = jnp.transpose(cols.reshape(bsz, lp, HEADS), (0, 2, 1)).reshape(bsz * HEADS, lp, 1)
        return jnp.broadcast_to(t, (bsz * HEADS, lp, DH))

    bb = lane_bcast(gb[:, 0:HEADS])
    gcb = lane_bcast(gcs[:, HEADS:2 * HEADS])
    gct = jnp.transpose(gcs[:, HEADS:2 * HEADS].reshape(bsz, nchunk, CH, HEADS), (0, 3, 1, 2)).reshape(
        bsz * HEADS, nchunk, 1, CH)
    u_loc, l1, l2, l3, w_t, k_dec, tinv_all, g_last = _gdn_local_fwd(qkv, gcb, bb, gct, bsz, lp)
    o, v_new, states = _gdn_scan_fwd(u_loc, l1, l2, g_last, bsz, lp)
    o, v_new = o.reshape(rows, GW), v_new.reshape(rows, GW)
    mix_g, mix_s = _mix_heads(proj, o, gdn_norm, convs_full, bsz, lp, padf)
    mixin = jnp.concatenate([mix_g, mix_s], axis=1)
    (mix,) = _mm("proj_out", [(mixin, wout)], "nn", tm, 512, D)
    h1, u2 = _mid_norms(h0, mix, mix_post_norm, ffn_pre_norm, bsz, lp, tr)
    gate, up, act = _ffn_up(u2, wgate, wup, _div_tile(rows, 528), 1408)
    (ffn,) = _mm("ffn_down", [(act, wdown)], "nn", tm, 512, 1408)

    dy, dffn, d_g4, loss_part = _loss_head(h1, ffn, tgt, ffn_post_norm, bsz, lp, tr, first_real)
    loss = lax.psum(loss_part[0, 0], ("x", "y", "c"))
    (dwdown,) = _mm("dw_down", [(act, dffn)], "tn", 1408, 512, tm)

    def swiglu_bwd(da, g, u):
        return da * u * _dsilu(g), da * _silu(g)

    dgate, dup = _mm("d_act", [(dffn, wdown)], "nt", tm, 1408, D, out_dtypes=(BF16, BF16),
                     epilogue=swiglu_bwd, extras=(gate, up))
    (dwgate,) = _mm("dw_gate", [(u2, dgate)], "tn", D, 1408, tm)
    (dwup,) = _mm("dw_up", [(u2, dup)], "tn", D, 1408, tm)
    (du2,) = _mm("d_u2", [(dgate, wgate), (dup, wup)], "nt", tm, 512, 1408)
    dh1, dmix, d_g3, d_g2 = _mid_norms_bwd(h1, mix, du2, dy, ffn_pre_norm, mix_post_norm, bsz, lp, tr)
    (dwout,) = _mm("dw_out", [(mixin, dmix)], "tn", D, 512, tm)
    (dmixin,) = _mm("d_mixin", [(dmix, wout)], "nt", tm, 512, D)
    d_o, dz, dsx, dsb, dsc, d_gn, d_convs = _mix_heads_bwd(proj, o, gdn_norm, convs_full, dmixin, bsz, lp, padf)
    dv_new, dstates = _gdn_scan_bwd(d_o, l3, w_t, k_dec, g_last, bsz, lp)
    dq, dk, dv, dg_b, dbeta_b = _gdn_local_bwd(qkv, gcb, bb, gct, tinv_all, states, dstates, v_new,
                                               dv_new.reshape(rows, GW), d_o, bsz, lp)
    dqkv = jnp.concatenate([dq, dk, dv], axis=1)
    dpqkv, d_convq = _gdn_prep_bwd(proj, convq_full, dqkv, bsz, lp, padf)
    dba, d_scal = _gates_bwd(proj, alog_v, dtb_v, dbeta_b, dg_b, gb, bsz, lp, padf)
    dproj = jnp.concatenate([dpqkv, dz, dba, dsx, dsb, dsc], axis=1)
    (dwin,) = _mm("dw_in", [(u1, dproj)], "tn", D, 768, tm)
    (du1,) = _mm("d_u1", [(dproj, win)], "nt", tm, 512, 768)
    dh0, d_g1 = _pre_norm_bwd(h0, du1, dh1, mix_pre_norm, bsz, lp, tr)
    dh0 = dh0.reshape(bsz, lp, D)
    grad_x = dh0[:, first_real:]
    d_meta = jnp.sum(dh0[:, padf:first_real], axis=0)

    dwin = jnp.concatenate([dwin[:, :BA0 + 2 * HEADS], dwin[:, SX0:]], axis=1)

    def col_blocks(g):
        r = g.shape[0]
        return jnp.transpose(g.reshape(r, 4, 2, -1), (2, 1, 0, 3))

    def row_blocks(g):
        c = g.shape[1]
        return jnp.transpose(g.reshape(4, 2, -1, c), (1, 0, 2, 3))

    big = [col_blocks(dwin), row_blocks(dwout), col_blocks(dwgate), col_blocks(dwup), row_blocks(dwdown)]
    from_sibling = _sibling_exchange("grads_to_sibling", big)
    core = jnp.reshape(cc, (1,)).astype(jnp.int32)
    names = ["w_in", "w_out", "w_gate", "w_up", "w_down"]
    chip_parts = [_pair_sum("pair_sum_" + nm, g, r, core) for nm, g, r in zip(names, big, from_sibling)]
    reduced = _chip_exchange("grads_across_chips", chip_parts)

    d_alog = d_scal[0:1, HEADS:2 * HEADS]
    d_dtb = d_scal[1:2, HEADS:2 * HEADS]
    slab = _pack_small([d_g1, d_g2, d_g3, d_g4], d_meta, d_convq, d_convs, d_alog, d_dtb, d_gn)
    (slabs,) = _all_gather("gather_small_grads", [slab])
    off_meta = dev * DH
    off_cq = dev * (3 * GW // N_DEV)
    off_cs = dev * (SW // N_DEV)

    def shard_of(s):
        gains, meta, cq, cs, al, db, gn = _unpack_small(s)
        return gains, lax.dynamic_slice_in_dim(meta, off_meta, DH, 1), \
            lax.dynamic_slice_in_dim(cq, off_cq, 3 * GW // N_DEV, 1), \
            lax.dynamic_slice_in_dim(cs, off_cs, SW // N_DEV, 1), al, db, gn

    slabs_local = jax.vmap(lambda s: _pack_small(*shard_of(s)))(slabs)
    w_small = _pack_small([mix_pre_norm, mix_post_norm, ffn_pre_norm, ffn_post_norm], meta_tokens, conv_qkv[0],
                          conv_sc[0], a_log, dt_bias, gdn_norm)
    m_small = _pack_small([m_mix_pre_norm, m_mix_post_norm, m_ffn_pre_norm, m_ffn_post_norm], m_meta_tokens,
                          m_conv_qkv[0], m_conv_sc[0], m_a_log, m_dt_bias, m_gdn_norm)
    v_small = _pack_small([v_mix_pre_norm, v_mix_post_norm, v_ffn_pre_norm, v_ffn_post_norm], v_meta_tokens,
                          v_conv_qkv[0], v_conv_sc[0], v_a_log, v_dt_bias, v_gdn_norm)
    small_out = _adamw("adamw_small", slabs_local, w_small, m_small, v_small)

    big_w = [(w_in, m_w_in, v_w_in), (w_out, m_w_out, v_w_out), (w_gate, m_w_gate, v_w_gate),
             (w_up, m_w_up, v_w_up), (w_down, m_w_down, v_w_down)]
    big_out = [_adamw("adamw_" + nm, parts, w[0], m[0], v[0]) for nm, parts, (w, m, v) in zip(names, reduced, big_w)]

    def small_leaves(slab_out):
        gains, meta, cq, cs, al, db, gn = _unpack_small(slab_out)
        return {"meta_tokens": meta[:, :DH], "mix_pre_norm": gains[0], "mix_post_norm": gains[1],
                "ffn_pre_norm": gains[2], "ffn_post_norm": gains[3], "conv_qkv": cq[None, :, :3 * GW // N_DEV],
                "a_log": al, "dt_bias": db, "gdn_norm": gn, "conv_sc": cs[None, :, :SW // N_DEV]}

    order = ["meta_tokens", "mix_pre_norm", "mix_post_norm", "ffn_pre_norm", "ffn_post_norm", "w_in", "conv_qkv",
             "a_log", "dt_bias", "gdn_norm", "conv_sc", "w_out", "w_gate", "w_up", "w_down"]
    outs = [loss, grad_x]
    for kind in range(4):
        leaves = small_leaves(small_out[kind])
        for nm, res in zip(names, big_out):
            leaves[nm] = res[kind][None]
        outs += [leaves[nm] for nm in order]
    return tuple(outs)
```

```python
import functools

import jax
import jax.numpy as jnp
from jax import lax
from jax.experimental import pallas as pl
from jax.experimental.pallas import tpu as pltpu

F32 = jnp.float32
BF16 = jnp.bfloat16
HIGHEST = lax.Precision.HIGHEST
MESH = pl.DeviceIdType.MESH

D = 1024
N_META = 16
HEADS = 4
DH = 128
GW = HEADS * DH
SW = D - GW
FF = 2816
CH = 64
KQ = 4
KS = 3
EPS = 1e-6
IN_W = 3 * GW + GW + 2 * HEADS + 3 * SW
Q0, K0, V0, Z0, BA0, SX0, SB0, SC0, PW = 0, 512, 1024, 1536, 2048, 2304, 2816, 3328, 3840
BA_W = SX0 - BA0
N_DEV = 8
VMEM_LIMIT = 56 * 1024 * 1024

ADAM_LR, ADAM_B1, ADAM_B2, ADAM_EPS, ADAM_WD, ADAM_STEP = 0.001, 0.9, 0.999, 1e-08, 0.01, 10


def _cparams(sem):
    return pltpu.CompilerParams(dimension_semantics=sem, vmem_limit_bytes=VMEM_LIMIT)


def _div_tile(n, target, mult=16):
    best = None
    for t in range(mult, min(n, target) + 1, mult):
        if n % t == 0:
            best = t
    assert best is not None, (n, target)
    return best


def _silu(x):
    return x * jax.nn.sigmoid(x)


def _dsilu(x):
    s = jax.nn.sigmoid(x)
    return s * (1.0 + x * (1.0 - s))


def _bdot(a, b, dims):
    return lax.dot_general(a.astype(BF16), b.astype(BF16), (dims, ((), ())), preferred_element_type=F32)


NN = ((1,), (0,))
NT = ((1,), (1,))
TN = ((0,), (0,))


def _fdot(a, b, dims=NN):
    return lax.dot_general(a, b, (dims, ((), ())), precision=HIGHEST, preferred_element_type=F32)


def _mm(name, pairs, mode, tm, tn, tk, out_dtypes=(F32,), epilogue=None, extras=()):
    a0, b0 = pairs[0]
    if mode == "nn":
        (m, k), n = a0.shape, b0.shape[1]
    elif mode == "nt":
        (m, k), n = a0.shape, b0.shape[0]
    else:
        (k, m), n = a0.shape, b0.shape[1]
    assert m % tm == 0 and n % tn == 0 and k % tk == 0, (name, m, n, k, tm, tn, tk)
    nk = k // tk
    npair = len(pairs)
    nex = len(extras)
    nout = len(out_dtypes)
    dims = {"nn": NN, "nt": NT, "tn": TN}[mode]

    def body(*refs):
        ab = refs[:2 * npair]
        ex = refs[2 * npair:2 * npair + nex]
        outs = refs[2 * npair + nex:2 * npair + nex + nout]
        acc_ref = refs[-1]
        kk = pl.program_id(2)

        part = None
        for p in range(npair):
            d = lax.dot_general(ab[2 * p][...], ab[2 * p + 1][...], (dims, ((), ())), preferred_element_type=F32)
            part = d if part is None else part + d

        def finish(acc):
            res = epilogue(acc, *[e[...] for e in ex]) if epilogue is not None else (acc,)
            for o, r in zip(outs, res):
                o[...] = r.astype(o.dtype)

        if nk == 1:
            finish(part)
        else:
            @pl.when(kk == 0)
            def _():
                acc_ref[...] = part

            @pl.when(kk > 0)
            def _():
                acc_ref[...] += part

            @pl.when(kk == nk - 1)
            def _():
                finish(acc_ref[...])

    if mode == "nn":
        a_spec = pl.BlockSpec((tm, tk), lambda i, j, q: (i, q))
        b_spec = pl.BlockSpec((tk, tn), lambda i, j, q: (q, j))
    elif mode == "nt":
        a_spec = pl.BlockSpec((tm, tk), lambda i, j, q: (i, q))
        b_spec = pl.BlockSpec((tn, tk), lambda i, j, q: (j, q))
    else:
        a_spec = pl.BlockSpec((tk, tm), lambda i, j, q: (q, i))
        b_spec = pl.BlockSpec((tk, tn), lambda i, j, q: (q, j))
    o_spec = pl.BlockSpec((tm, tn), lambda i, j, q: (i, j))
    flat = [t for pr in pairs for t in pr]
    res = pl.pallas_call(
        body, name=name,
        grid=(m // tm, n // tn, nk),
        in_specs=[a_spec, b_spec] * npair + [o_spec] * nex,
        out_specs=[o_spec] * nout,
        out_shape=[jax.ShapeDtypeStruct((m, n), dt) for dt in out_dtypes],
        scratch_shapes=[pltpu.VMEM((tm, tn), F32)],
        compiler_params=_cparams(("parallel", "parallel", "arbitrary")),
    )(*flat, *extras)
    return res


def _ffn_up(u2, wg, wu, tm, tn):
    m, k = u2.shape
    n = wg.shape[1]

    def body(a_ref, g_ref, u_ref, gate_ref, up_ref, act_ref):
        a = a_ref[...]
        g = jnp.dot(a, g_ref[...], preferred_element_type=F32)
        u = jnp.dot(a, u_ref[...], preferred_element_type=F32)
        gate_ref[...] = g
        up_ref[...] = u
        act_ref[...] = (_silu(g) * u).astype(BF16)

    o_spec = pl.BlockSpec((tm, tn), lambda i, j: (i, j))
    w_spec = pl.BlockSpec((k, tn), lambda i, j: (0, j))
    return pl.pallas_call(
        body, name="ffn_up", grid=(m // tm, n // tn),
        in_specs=[pl.BlockSpec((tm, k), lambda i, j: (i, 0)), w_spec, w_spec],
        out_specs=[o_spec, o_spec, o_spec],
        out_shape=[jax.ShapeDtypeStruct((m, n), F32), jax.ShapeDtypeStruct((m, n), F32),
                   jax.ShapeDtypeStruct((m, n), BF16)],
        compiler_params=_cparams(("parallel", "parallel")),
    )(u2, wg, wu)


def _rms(x, gain):
    r = lax.rsqrt(jnp.mean(x * x, axis=-1, keepdims=True) + EPS)
    return x * r * gain, r


def _rms_bwd(x, gain, dy):
    r = lax.rsqrt(jnp.mean(x * x, axis=-1, keepdims=True) + EPS)
    dyw = dy * gain
    dx = r * dyw - x * (r * r * r) * jnp.mean(dyw * x, axis=-1, keepdims=True)
    return dx, dy * x * r


def _row_specs(tr, width, nb):
    return pl.BlockSpec((tr, width), lambda b, j: (b * nb + j, 0))


def _vec_spec(width):
    return pl.BlockSpec((1, width), lambda b, j: (0, 0))


def _first_step(b, j):
    return jnp.logical_and(b == 0, j == 0)


def _pre_norm(h0, gain, bsz, lp, tr):
    nb = lp // tr

    def body(h_ref, g_ref, u_ref):
        u_ref[...] = _rms(h_ref[...], g_ref[...])[0].astype(BF16)

    return pl.pallas_call(
        body, name="mix_pre_norm", grid=(bsz, nb),
        in_specs=[_row_specs(tr, D, nb), _vec_spec(D)],
        out_specs=_row_specs(tr, D, nb),
        out_shape=jax.ShapeDtypeStruct(h0.shape, BF16),
        compiler_params=_cparams(("parallel", "parallel")),
    )(h0, gain)


def _mid_norms(h0, mix, g_post, g_pre, bsz, lp, tr):
    nb = lp // tr

    def body(h_ref, m_ref, gp_ref, gq_ref, h1_ref, u2_ref):
        h1 = h_ref[...] + _rms(m_ref[...], gp_ref[...])[0]
        h1_ref[...] = h1
        u2_ref[...] = _rms(h1, gq_ref[...])[0].astype(BF16)

    rs = _row_specs(tr, D, nb)
    return pl.pallas_call(
        body, name="mid_norms", grid=(bsz, nb),
        in_specs=[rs, rs, _vec_spec(D), _vec_spec(D)],
        out_specs=[rs, rs],
        out_shape=[jax.ShapeDtypeStruct(h0.shape, F32), jax.ShapeDtypeStruct(h0.shape, BF16)],
        compiler_params=_cparams(("parallel", "parallel")),
    )(h0, mix, g_post, g_pre)


def _loss_head(h1, ffn, tgt, g_post, bsz, lp, tr, first_real):
    nb = lp // tr

    def body(h_ref, f_ref, t_ref, g_ref, dy_ref, dffn_ref, dg_ref, loss_ref):
        b, j = pl.program_id(0), pl.program_id(1)
        f = f_ref[...]
        g = g_ref[...]
        y = h_ref[...] + _rms(f, g)[0]
        t = j * tr + lax.broadcasted_iota(jnp.int32, (tr, 1), 0)
        err = jnp.where(t >= first_real, y - t_ref[...], 0.0)
        dy = err * (1.0 / D)
        dy_ref[...] = dy
        dx, dgc = _rms_bwd(f, g, dy)
        dffn_ref[...] = dx.astype(BF16)
        lsum = jnp.sum(jnp.sum(err * err, axis=0, keepdims=True), axis=1, keepdims=True) * (0.5 / D)

        @pl.when(_first_step(b, j))
        def _():
            dg_ref[...] = jnp.zeros_like(dg_ref)
            loss_ref[...] = jnp.zeros_like(loss_ref)

        dg_ref[...] += jnp.sum(dgc, axis=0, keepdims=True)
        loss_ref[...] += jnp.broadcast_to(lsum, loss_ref.shape)

    rs = _row_specs(tr, D, nb)
    return pl.pallas_call(
        body, name="loss_head", grid=(bsz, nb),
        in_specs=[rs, rs, rs, _vec_spec(D)],
        out_specs=[rs, rs, _vec_spec(D), _vec_spec(DH)],
        out_shape=[jax.ShapeDtypeStruct(h1.shape, F32), jax.ShapeDtypeStruct(h1.shape, BF16),
                   jax.ShapeDtypeStruct((1, D), F32), jax.ShapeDtypeStruct((1, DH), F32)],
        compiler_params=_cparams(("arbitrary", "arbitrary")),
    )(h1, ffn, tgt, g_post)


def _mid_norms_bwd(h1, mix, du2, dy, g_pre, g_post, bsz, lp, tr):
    nb = lp // tr

    def body(h_ref, m_ref, du_ref, dy_ref, gq_ref, gp_ref, dh1_ref, dmix_ref, dgq_ref, dgp_ref):
        b, j = pl.program_id(0), pl.program_id(1)
        dx, dgq = _rms_bwd(h_ref[...], gq_ref[...], du_ref[...])
        dh1 = dy_ref[...] + dx
        dh1_ref[...] = dh1
        dm, dgp = _rms_bwd(m_ref[...], gp_ref[...], dh1)
        dmix_ref[...] = dm.astype(BF16)

        @pl.when(_first_step(b, j))
        def _():
            dgq_ref[...] = jnp.zeros_like(dgq_ref)
            dgp_ref[...] = jnp.zeros_like(dgp_ref)

        dgq_ref[...] += jnp.sum(dgq, axis=0, keepdims=True)
        dgp_ref[...] += jnp.sum(dgp, axis=0, keepdims=True)

    rs = _row_specs(tr, D, nb)
    return pl.pallas_call(
        body, name="mid_norms_bwd", grid=(bsz, nb),
        in_specs=[rs, rs, rs, rs, _vec_spec(D), _vec_spec(D)],
        out_specs=[rs, rs, _vec_spec(D), _vec_spec(D)],
        out_shape=[jax.ShapeDtypeStruct(h1.shape, F32), jax.ShapeDtypeStruct(h1.shape, BF16),
                   jax.ShapeDtypeStruct((1, D), F32), jax.ShapeDtypeStruct((1, D), F32)],
        compiler_params=_cparams(("arbitrary", "arbitrary")),
    )(h1, mix, du2, dy, g_pre, g_post)


def _pre_norm_bwd(h0, du1, dh1, gain, bsz, lp, tr):
    nb = lp // tr

    def body(h_ref, du_ref, dh1_ref, g_ref, dh0_ref, dg_ref):
        b, j = pl.program_id(0), pl.program_id(1)
        dx, dgc = _rms_bwd(h_ref[...], g_ref[...], du_ref[...])
        dh0_ref[...] = dh1_ref[...] + dx

        @pl.when(_first_step(b, j))
        def _():
            dg_ref[...] = jnp.zeros_like(dg_ref)

        dg_ref[...] += jnp.sum(dgc, axis=0, keepdims=True)

    rs = _row_specs(tr, D, nb)
    return pl.pallas_call(
        body, name="pre_norm_bwd", grid=(bsz, nb),
        in_specs=[rs, rs, rs, _vec_spec(D)],
        out_specs=[rs, _vec_spec(D)],
        out_shape=[jax.ShapeDtypeStruct(h0.shape, F32), jax.ShapeDtypeStruct((1, D), F32)],
        compiler_params=_cparams(("arbitrary", "arbitrary")),
    )(h0, du1, dh1, gain)


def _shift_down(x, s, lp):
    return x if s == 0 else pltpu.roll(x, s, axis=0)


def _shift_up(x, s, lp):
    return x if s == 0 else pltpu.roll(x, lp - s, axis=0)


def _conv_fwd(x, w, taps, lp):
    y = None
    for i in range(taps):
        term = _shift_down(x, taps - 1 - i, lp) * w[i:i + 1, :]
        y = term if y is None else y + term
    return y


def _conv_bwd(x, w, dy, taps, lp):
    dx = None
    dws = []
    for i in range(taps):
        s = taps - 1 - i
        term = _shift_up(dy, s, lp) * w[i:i + 1, :]
        dx = term if dx is None else dx + term
        dws.append(jnp.sum(dy * _shift_down(x, s, lp), axis=0, keepdims=True))
    return dx, jnp.concatenate(dws, axis=0)


def _slab(lp, col0):
    return pl.BlockSpec((lp, DH), lambda b, s: (b, col0 // DH + s))


def _qkv_act(c, s):
    y = _silu(c)
    r = lax.rsqrt(jnp.sum(y * y, axis=-1, keepdims=True) + EPS)
    scale = jnp.where(s < HEADS, DH ** -0.5, 1.0)
    return y, r, scale


def _gdn_prep(proj, conv_qkv, bsz, lp, padf):
    def body(p_ref, w_ref, o_ref):
        s = pl.program_id(1)
        c = _conv_fwd(p_ref[...], w_ref[...], KQ, lp)
        y, r, scale = _qkv_act(c, s)
        out = jnp.where(s < 2 * HEADS, y * (r * scale), y)
        t = lax.broadcasted_iota(jnp.int32, (lp, 1), 0)
        o_ref[...] = jnp.where(t >= padf, out, 0.0)

    return pl.pallas_call(
        body, name="gdn_prep", grid=(bsz, 3 * HEADS),
        in_specs=[_slab(lp, Q0), pl.BlockSpec((KQ, DH), lambda b, s: (0, s))],
        out_specs=_slab(lp, 0),
        out_shape=jax.ShapeDtypeStruct((bsz * lp, 3 * GW), F32),
        compiler_params=_cparams(("parallel", "parallel")),
    )(proj, conv_qkv)


def _gdn_prep_bwd(proj, conv_qkv, dqkv, bsz, lp, padf):
    def body(p_ref, w_ref, d_ref, dp_ref, dw_ref):
        s, b = pl.program_id(0), pl.program_id(1)
        x = p_ref[...]
        w = w_ref[...]
        c = _conv_fwd(x, w, KQ, lp)
        y, r, scale = _qkv_act(c, s)
        dn = d_ref[...] * scale
        n = y * r
        dy_norm = r * (dn - n * jnp.sum(dn * n, axis=-1, keepdims=True))
        dyy = jnp.where(s < 2 * HEADS, dy_norm, dn)
        t = lax.broadcasted_iota(jnp.int32, (lp, 1), 0)
        dc = jnp.where(t >= padf, dyy * _dsilu(c), 0.0)
        dx, dw = _conv_bwd(x, w, dc, KQ, lp)
        dp_ref[...] = jnp.where(t >= padf, dx, 0.0).astype(BF16)

        @pl.when(b == 0)
        def _():
            dw_ref[...] = jnp.zeros_like(dw_ref)

        dw_ref[...] += dw

    sl = lambda col0: pl.BlockSpec((lp, DH), lambda s, b: (b, col0 // DH + s))
    return pl.pallas_call(
        body, name="gdn_prep_bwd", grid=(3 * HEADS, bsz),
        in_specs=[sl(Q0), pl.BlockSpec((KQ, DH), lambda s, b: (0, s)), sl(0)],
        out_specs=[sl(0), pl.BlockSpec((KQ, DH), lambda s, b: (0, s))],
        out_shape=[jax.ShapeDtypeStruct((bsz * lp, 3 * GW), BF16), jax.ShapeDtypeStruct((KQ, 3 * GW), F32)],
        compiler_params=_cparams(("parallel", "arbitrary")),
    )(proj, conv_qkv, dqkv)


def _softplus(x):
    e = jnp.exp(-jnp.abs(x))
    one_e = 1.0 + e
    l1p = jnp.where(one_e == 1.0, e, jnp.log(one_e) * (e / (one_e - 1.0)))
    return jnp.maximum(x, 0.0) + l1p


def _tri(n, kind):
    i = lax.broadcasted_iota(jnp.int32, (n, n), 0)
    j = lax.broadcasted_iota(jnp.int32, (n, n), 1)
    return {"incl": i >= j, "strict": i > j, "upper": i <= j}[kind]


def _gates(proj, alog_v, dtb_v, bsz, lp, padf):
    nchunk = lp // CH

    def body(p_ref, al_ref, dt_ref, gb_ref, cs_ref):
        x = p_ref[...]
        lane = lax.broadcasted_iota(jnp.int32, (lp, DH), 1)
        t = lax.broadcasted_iota(jnp.int32, (lp, DH), 0)
        beta = jax.nn.sigmoid(x)
        g = -jnp.exp(al_ref[...]) * _softplus(x + dt_ref[...])
        gb = jnp.where(lane < HEADS, beta, jnp.where(lane < 2 * HEADS, g, 0.0))
        gb_ref[...] = jnp.where(t >= padf, gb, 0.0)
        ltri = _tri(CH, "incl").astype(F32)

        def step(n, carry):
            r0 = pl.multiple_of(n * CH, CH)
            cs_ref[pl.ds(r0, CH), :] = _fdot(ltri, gb_ref[pl.ds(r0, CH), :])
            return carry

        lax.fori_loop(0, nchunk, step, 0)

    blk = pl.BlockSpec((lp, DH), lambda b: (b, BA0 // DH))
    vec = pl.BlockSpec((1, DH), lambda b: (0, 0))
    out = pl.BlockSpec((lp, DH), lambda b: (b, 0))
    return pl.pallas_call(
        body, name="gates", grid=(bsz,),
        in_specs=[blk, vec, vec], out_specs=[out, out],
        out_shape=[jax.ShapeDtypeStruct((bsz * lp, DH), F32)] * 2,
        compiler_params=_cparams(("parallel",)),
    )(proj, alog_v, dtb_v)


def _gates_bwd(proj, alog_v, dtb_v, dbeta_b, dg_b, gb, bsz, lp, padf):
    def body(p_ref, al_ref, dt_ref, db_ref, dg_ref, gb_ref, dl_ref, dv_ref):
        b = pl.program_id(0)
        x = p_ref[...]
        lane = lax.broadcasted_iota(jnp.int32, (lp, DH), 1)
        t = lax.broadcasted_iota(jnp.int32, (lp, DH), 0)
        dbeta = jnp.zeros((lp, DH), F32)
        dg = jnp.zeros((lp, DH), F32)
        for h in range(HEADS):
            dbeta = jnp.where(lane == h, db_ref[h], dbeta)
            dg = jnp.where(lane == HEADS + h, dg_ref[h], dg)
        real = t >= padf
        dbeta = jnp.where(real, dbeta, 0.0)
        dg = jnp.where(real, dg, 0.0)
        sb = jax.nn.sigmoid(x)
        neg_ea = -jnp.exp(al_ref[...])
        dsp = dg * neg_ea * jax.nn.sigmoid(x + dt_ref[...])
        dl = dbeta * sb * (1.0 - sb) + dsp
        dl_ref[...] = jnp.concatenate([dl, jnp.zeros((lp, BA_W - DH), F32)], axis=1).astype(BF16)
        dalog = jnp.sum(dg * gb_ref[...], axis=0, keepdims=True)
        ddt = jnp.sum(dsp, axis=0, keepdims=True)

        @pl.when(b == 0)
        def _():
            dv_ref[...] = jnp.zeros_like(dv_ref)

        dv_ref[...] += jnp.concatenate([dalog, ddt], axis=0)

    blk = pl.BlockSpec((lp, DH), lambda b: (b, BA0 // DH))
    vec = pl.BlockSpec((1, DH), lambda b: (0, 0))
    hb = pl.BlockSpec((HEADS, lp, DH), lambda b: (b, 0, 0))
    return pl.pallas_call(
        body, name="gates_bwd", grid=(bsz,),
        in_specs=[blk, vec, vec, hb, hb, pl.BlockSpec((lp, DH), lambda b: (b, 0))],
        out_specs=[pl.BlockSpec((lp, BA_W), lambda b: (b, 0)), pl.BlockSpec((2, DH), lambda b: (0, 0))],
        out_shape=[jax.ShapeDtypeStruct((bsz * lp, BA_W), BF16), jax.ShapeDtypeStruct((2, DH), F32)],
        compiler_params=_cparams(("arbitrary",)),
    )(proj, alog_v, dtb_v, dbeta_b, dg_b, gb)


def _mix_heads(proj, o, gdn_norm, conv_sc, bsz, lp, padf):
    def body(o_ref, z_ref, gn_ref, sx_ref, sb_ref, sc_ref, w_ref, og_ref, os_ref):
        og_ref[...] = (_rms(o_ref[...], gn_ref[...])[0] * _silu(z_ref[...])).astype(BF16)
        conv = _conv_fwd(sc_ref[...] * sx_ref[...], w_ref[...], KS, lp)
        t = lax.broadcasted_iota(jnp.int32, (lp, 1), 0)
        os_ref[...] = jnp.where(t >= padf, sb_ref[...] * conv, 0.0).astype(BF16)

    half = _slab(lp, 0)
    return pl.pallas_call(
        body, name="mix_heads", grid=(bsz, HEADS),
        in_specs=[half, _slab(lp, Z0), pl.BlockSpec((1, DH), lambda b, s: (0, 0)),
                  _slab(lp, SX0), _slab(lp, SB0), _slab(lp, SC0), pl.BlockSpec((KS, DH), lambda b, s: (0, s))],
        out_specs=[half, half],
        out_shape=[jax.ShapeDtypeStruct((bsz * lp, GW), BF16)] * 2,
        compiler_params=_cparams(("parallel", "parallel")),
    )(o, proj, gdn_norm, proj, proj, proj, conv_sc)


def _mix_heads_bwd(proj, o, gdn_norm, conv_sc, dmixin, bsz, lp, padf):
    def body(o_ref, z_ref, gn_ref, sx_ref, sb_ref, sc_ref, w_ref, dg_ref, ds_ref,
             do_ref, dz_ref, dsx_ref, dsb_ref, dsc_ref, dgn_ref, dw_ref):
        s, b = pl.program_id(0), pl.program_id(1)
        t = lax.broadcasted_iota(jnp.int32, (lp, 1), 0)
        real = t >= padf
        o, z, gn, d = o_ref[...], z_ref[...], gn_ref[...], dg_ref[...]
        sz = _silu(z)
        on = _rms(o, gn)[0]
        dz_ref[...] = (d * on * _dsilu(z)).astype(BF16)
        dox, dgn = _rms_bwd(o, gn, d * sz)
        do_ref[...] = dox
        sx, sb, sc, w, e = sx_ref[...], sb_ref[...], sc_ref[...], w_ref[...], ds_ref[...]
        e = jnp.where(real, e, 0.0)
        xin = sc * sx
        conv = _conv_fwd(xin, w, KS, lp)
        dsb_ref[...] = (e * conv).astype(BF16)
        dxin, dw = _conv_bwd(xin, w, e * sb, KS, lp)
        dxin = jnp.where(real, dxin, 0.0)
        dsx_ref[...] = (dxin * sc).astype(BF16)
        dsc_ref[...] = (dxin * sx).astype(BF16)

        @pl.when(jnp.logical_and(s == 0, b == 0))
        def _():
            dgn_ref[...] = jnp.zeros_like(dgn_ref)

        @pl.when(b == 0)
        def _():
            dw_ref[...] = jnp.zeros_like(dw_ref)

        dgn_ref[...] += jnp.sum(dgn, axis=0, keepdims=True)
        dw_ref[...] += dw

    sl = lambda col0: pl.BlockSpec((lp, DH), lambda s, b: (b, col0 // DH + s))
    half = sl(0)
    return pl.pallas_call(
        body, name="mix_heads_bwd", grid=(HEADS, bsz),
        in_specs=[half, sl(Z0), pl.BlockSpec((1, DH), lambda s, b: (0, 0)), sl(SX0), sl(SB0), sl(SC0),
                  pl.BlockSpec((KS, DH), lambda s, b: (0, s)), sl(0), sl(GW)],
        out_specs=[half] * 5 + [pl.BlockSpec((1, DH), lambda s, b: (0, 0)), pl.BlockSpec((KS, DH), lambda s, b: (0, s))],
        out_shape=[jax.ShapeDtypeStruct((bsz * lp, GW), F32)] + [jax.ShapeDtypeStruct((bsz * lp, GW), BF16)] * 4
        + [jax.ShapeDtypeStruct((1, DH), F32), jax.ShapeDtypeStruct((KS, SW), F32)],
        compiler_params=_cparams(("arbitrary", "arbitrary")),
    )(o, proj, gdn_norm, proj, proj, proj, conv_sc, dmixin, dmixin)


def _split_dot(a, b, dims=NN):
    ah = a.astype(BF16)
    al = (a - ah.astype(F32)).astype(BF16)
    bh = b.astype(BF16)
    bl = (b - bh.astype(F32)).astype(BF16)

    def d(x, y):
        return lax.dot_general(x, y, (dims, ((), ())), preferred_element_type=F32)

    return d(ah, bh) + (d(ah, bl) + d(al, bh))


def _eye(n):
    return (lax.broadcasted_iota(jnp.int32, (n, n), 0) == lax.broadcasted_iota(jnp.int32, (n, n), 1)).astype(F32)


def _each(f, *lists):
    return [f(*xs) for xs in zip(*lists)]


def _inv_unit_lower(a_list):
    eye = _eye(CH)
    p = _each(lambda a: -a, a_list)
    t = _each(lambda x: eye + x, p)
    p = _each(lambda x: _split_dot(x, x), p)
    for level in range(5):
        if level < 4:
            prod = _each(lambda tt, pp: _split_dot(jnp.concatenate([tt, pp], axis=0), pp), t, p)
            t = _each(lambda tt, pr: tt + pr[:CH], t, prod)
            p = _each(lambda pr: pr[CH:], prod)
        else:
            t = _each(lambda tt, pp: tt + _split_dot(tt, pp), t, p)
    return t


def _chunk_gates(gc_ref, be_ref, gt_ref, g):
    rows = pl.ds(g * CH, CH)
    gc = gc_ref[0, rows, :]
    be = be_ref[0, rows, :]
    gcl = gc_ref[0, pl.ds(g * CH + CH - 1, 1), :]
    gt = gt_ref[0, g]
    incl, upper = _tri(CH, "incl"), _tri(CH, "upper")
    dec = jnp.where(incl, jnp.exp(jnp.where(incl, gc[:, :CH] - gt, 0.0)), 0.0)
    dec_t = jnp.where(upper, jnp.exp(jnp.where(upper, gt - gc[:, :CH], 0.0)), 0.0)
    return rows, be, dec, dec_t, jnp.exp(gc), jnp.exp(gcl - gc), jnp.exp(gcl)


def _chunk_group(nchunk):
    return 3 if nchunk % 3 == 0 else 1


def _local_group(nchunk):
    return 11 if nchunk % 11 == 0 else _chunk_group(nchunk)


def _gdn_local_specs(lp, grp):
    ngrp = lp // (grp * CH)
    head = lambda col0: pl.BlockSpec((grp * CH, DH), lambda i, j: ((i // HEADS) * ngrp + j, col0 // DH + i % HEADS))
    lane_b = pl.BlockSpec((1, grp * CH, DH), lambda i, j: (i, j, 0))
    per_chunk = lambda r, c: pl.BlockSpec((1, grp, r, c), lambda i, j: (i, j, 0, 0))
    return ngrp, head, lane_b, per_chunk


def _gdn_local_fwd(qkv, gcb, bb, gct, bsz, lp):
    nchunk = lp // CH
    grp = _local_group(nchunk)
    ngrp, head, lane_b, per_chunk = _gdn_local_specs(lp, grp)
    bh = bsz * HEADS

    def body(q_ref, k_ref, v_ref, gc_ref, be_ref, gt_ref,
             u_ref, l1_ref, l2_ref, l3_ref, wt_ref, kd_ref, ti_ref, gl_ref):
        incl, strict, upper = _tri(CH, "incl"), _tri(CH, "strict"), _tri(CH, "upper")
        eye = _eye(CH)
        rows, be, dec, dec_t, eg, ekd, gl = zip(*[_chunk_gates(gc_ref, be_ref, gt_ref, g) for g in range(grp)])
        q = [q_ref[r, :] for r in rows]
        k = [k_ref[r, :] for r in rows]
        v = [v_ref[r, :] for r in rows]
        kb = _each(lambda x, y: x * y, k, be)
        a = _each(lambda x, y, d: jnp.where(strict, _bdot(x, y, NT) * d, 0.0), kb, k, dec)
        tinv = _inv_unit_lower(a)
        u = _each(lambda t, x, y: _bdot(t, x * y, NN), tinv, v, be)
        w = _each(lambda t, x, y: _bdot(t, x * y, NN), tinv, kb, eg)
        qk = _each(lambda x, y, d: jnp.where(incl, _bdot(x, y, NT) * d, 0.0), q, k, dec)
        qk_t = _each(lambda x, y, d: jnp.where(upper, _bdot(y, x, NT) * d, 0.0), q, k, dec_t)
        qd = _each(lambda x, y: x * y, q, eg)
        kd = _each(lambda x, y: x * y, k, ekd)
        kd_t = _each(lambda x: _bdot(x, eye, TN), kd)
        qd_t = _each(lambda x: _bdot(x, eye, TN), qd)
        w_t = _each(lambda x: _bdot(x, eye, TN), w)
        for g in range(grp):
            u_ref[rows[g], :] = u[g]
            l1_ref[0, g] = jnp.concatenate([w[g], qd[g]], axis=0).astype(BF16)
            l2_ref[0, g] = jnp.concatenate([qk[g], kd_t[g]], axis=0).astype(BF16)
            l3_ref[0, g] = jnp.concatenate([qk_t[g], qd_t[g]], axis=0).astype(BF16)
            wt_ref[0, g] = w_t[g].astype(BF16)
            kd_ref[0, g] = kd[g].astype(BF16)
            ti_ref[0, g] = tinv[g]
            gl_ref[0, g] = gl[g]

    r = bsz * lp
    shapes = [((r, GW), F32), ((bh, nchunk, 2 * CH, DH), BF16), ((bh, nchunk, 3 * CH, CH), BF16),
              ((bh, nchunk, 3 * CH, CH), BF16), ((bh, nchunk, DH, CH), BF16), ((bh, nchunk, CH, DH), BF16),
              ((bh, nchunk, CH, CH), F32), ((bh, nchunk, 1, DH), F32)]
    return pl.pallas_call(
        body, name="gdn_local_fwd", grid=(bh, ngrp),
        in_specs=[head(0), head(GW), head(2 * GW), lane_b, lane_b, per_chunk(1, CH)],
        out_specs=[head(0), per_chunk(2 * CH, DH), per_chunk(3 * CH, CH), per_chunk(3 * CH, CH), per_chunk(DH, CH),
                   per_chunk(CH, DH), per_chunk(CH, CH), per_chunk(1, DH)],
        out_shape=[jax.ShapeDtypeStruct(s, dt) for s, dt in shapes],
        compiler_params=_cparams(("parallel", "parallel")),
    )(qkv, qkv, qkv, gcb, bb, gct)


def _scan_specs(bsz, lp, grp, order):
    ngrp = lp // (grp * CH)
    bh = bsz * HEADS
    seq = lambda: pl.BlockSpec((bsz, grp * CH, GW), lambda s: (0, order(s, ngrp), 0))
    per_chunk = lambda r, c: pl.BlockSpec((bh, grp, r, c), lambda s: (0, order(s, ngrp), 0, 0))
    return ngrp, bh, seq, per_chunk


def _gdn_scan_fwd(u, l1, l2, gl, bsz, lp):
    nchunk = lp // CH
    grp = _chunk_group(nchunk)
    ngrp, bh, seq, per_chunk = _scan_specs(bsz, lp, grp, lambda s, n: s)

    def body(u_ref, l1_ref, l2_ref, gl_ref, o_ref, vn_ref, st_ref, s_scr):
        @pl.when(pl.program_id(0) == 0)
        def _():
            s_scr[...] = jnp.zeros_like(s_scr)

        for c in range(grp):
            rows = slice(c * CH, (c + 1) * CH)
            for i in range(bh):
                b, h = divmod(i, HEADS)
                cols = slice(h * DH, (h + 1) * DH)
                s = s_scr[i]
                st_ref[i, c] = s
                x = jnp.dot(l1_ref[i, c], s.astype(BF16), preferred_element_type=F32)
                vnew = u_ref[b, rows, cols] - x[:CH]
                y = jnp.dot(l2_ref[i, c], vnew.astype(BF16), preferred_element_type=F32)
                o_ref[b, rows, cols] = x[CH:] + y[:CH]
                vn_ref[b, rows, cols] = vnew.astype(BF16)
                s_scr[i] = s * gl_ref[i, c] + y[CH:]

    return pl.pallas_call(
        body, name="gdn_scan_fwd", grid=(ngrp,),
        in_specs=[seq(), per_chunk(2 * CH, DH), per_chunk(3 * CH, CH), per_chunk(1, DH)],
        out_specs=[seq(), seq(), per_chunk(DH, DH)],
        out_shape=[jax.ShapeDtypeStruct((bsz, lp, GW), F32), jax.ShapeDtypeStruct((bsz, lp, GW), BF16),
                   jax.ShapeDtypeStruct((bh, nchunk, DH, DH), F32)],
        scratch_shapes=[pltpu.VMEM((bh, DH, DH), F32)],
        compiler_params=_cparams(("arbitrary",)),
    )(u.reshape(bsz, lp, GW), l1, l2, gl)


def _gdn_scan_bwd(d_o, l3, wt, kd, gl, bsz, lp):
    nchunk = lp // CH
    grp = _chunk_group(nchunk)
    ngrp, bh, seq, per_chunk = _scan_specs(bsz, lp, grp, lambda s, n: n - 1 - s)

    def body(do_ref, l3_ref, wt_ref, kd_ref, gl_ref, dvn_ref, dst_ref, ds_scr):
        @pl.when(pl.program_id(0) == 0)
        def _():
            ds_scr[...] = jnp.zeros_like(ds_scr)

        for c in reversed(range(grp)):
            rows = slice(c * CH, (c + 1) * CH)
            for i in range(bh):
                b, h = divmod(i, HEADS)
                cols = slice(h * DH, (h + 1) * DH)
                ds = ds_scr[i]
                dst_ref[i, c] = ds
                p = jnp.dot(l3_ref[i, c], do_ref[b, rows, cols].astype(BF16), preferred_element_type=F32)
                dvn = p[:CH] + jnp.dot(kd_ref[i, c], ds.astype(BF16), preferred_element_type=F32)
                dvn_ref[b, rows, cols] = dvn.astype(BF16)
                ds_scr[i] = ds * gl_ref[i, c] + p[CH:] - jnp.dot(wt_ref[i, c], dvn.astype(BF16),
                                                                 preferred_element_type=F32)

    return pl.pallas_call(
        body, name="gdn_scan_bwd", grid=(ngrp,),
        in_specs=[seq(), per_chunk(3 * CH, CH), per_chunk(DH, CH), per_chunk(CH, DH), per_chunk(1, DH)],
        out_specs=[seq(), per_chunk(DH, DH)],
        out_shape=[jax.ShapeDtypeStruct((bsz, lp, GW), BF16), jax.ShapeDtypeStruct((bh, nchunk, DH, DH), F32)],
        scratch_shapes=[pltpu.VMEM((bh, DH, DH), F32)],
        compiler_params=_cparams(("arbitrary",)),
    )(d_o.reshape(bsz, lp, GW), l3, wt, kd, gl)


def _gdn_local_bwd(qkv, gcb, bb, gct, tinv_all, states, dstates, vnew, dvnew, d_o, bsz, lp):
    nchunk = lp // CH
    grp = _local_group(nchunk)
    ngrp, head, lane_b, per_chunk = _gdn_local_specs(lp, grp)
    bh = bsz * HEADS

    def body(q_ref, k_ref, v_ref, gc_ref, be_ref, gt_ref, ti_ref, st_ref, dst_ref, vn_ref, dvn_ref, do_ref,
             dq_ref, dk_ref, dv_ref, dg_ref, db_ref):
        incl, strict = _tri(CH, "incl"), _tri(CH, "strict")
        upper = _tri(CH, "upper").astype(F32)
        ones = jnp.ones((CH, DH), F32)
        last = lax.broadcasted_iota(jnp.int32, (CH, 1), 0) == CH - 1

        def rsum(x):
            return jnp.sum(x, axis=-1, keepdims=True)

        def mul(xs, ys):
            return _each(lambda x, y: x * y, xs, ys)

        rows, be, dec, _, eg, ekd, gl = zip(*[_chunk_gates(gc_ref, be_ref, gt_ref, g) for g in range(grp)])
        q = [q_ref[r, :] for r in rows]
        k = [k_ref[r, :] for r in rows]
        v = [v_ref[r, :] for r in rows]
        vnew = [vn_ref[r, :] for r in rows]
        dvn = [dvn_ref[r, :] for r in rows]
        do = [do_ref[r, :] for r in rows]
        tinv = [ti_ref[0, g] for g in range(grp)]
        s = [st_ref[0, g] for g in range(grp)]
        ds = [dst_ref[0, g] for g in range(grp)]
        kb, vb, qd, kd = mul(k, be), mul(v, be), mul(q, eg), mul(k, ekd)
        kbg = mul(kb, eg)
        a = _each(lambda x, y, d: jnp.where(strict, _bdot(x, y, NT) * d, 0.0), kb, k, dec)
        qk = _each(lambda x, y, d: jnp.where(incl, _bdot(x, y, NT) * d, 0.0), q, k, dec)
        dqk = _each(lambda x, y: jnp.where(incl, _bdot(x, y, NT), 0.0), do, vnew)
        dqd = _each(lambda x, y: _bdot(x, y, NT), do, s)
        dkd = _each(lambda x, y: _bdot(x, y, NT), vnew, ds)
        dgl = _each(lambda x, y: jnp.sum(rsum(x * y), axis=0, keepdims=True), s, ds)
        dw = _each(lambda x, y: -_bdot(x, y, NT), dvn, s)
        dvb = _each(lambda t, x: _bdot(t, x, TN), tinv, dvn)
        dkbg = _each(lambda t, x: _bdot(t, x, TN), tinv, dw)
        dt = _each(lambda x, y, z, w: _bdot(x, y, NT) + _bdot(z, w, NT), dvn, vb, dw, kbg)
        tdt = _each(lambda t, x: _split_dot(t, x, TN), tinv, dt)
        da = _each(lambda x, t: jnp.where(strict, -_split_dot(x, t, NT), 0.0), tdt, tinv)
        dp, dr = mul(da, dec), mul(dqk, dec)
        dkb = _each(lambda x, y, z, e: _bdot(x, y, NN) + z * e, dp, k, dkbg, eg)
        dk = _each(lambda p_, kb_, r_, q_, dkd_, ekd_, dkb_, be_:
                   _bdot(p_, kb_, TN) + _bdot(r_, q_, TN) + dkd_ * ekd_ + dkb_ * be_,
                   dp, kb, dr, q, dkd, ekd, dkb, be)
        dq = _each(lambda r_, k_, dqd_, eg_: _bdot(r_, k_, NN) + dqd_ * eg_, dr, k, dqd, eg)
        m = _each(lambda da_, a_, dqk_, qk_: da_ * a_ + dqk_ * qk_, da, a, dqk, qk)
        m_cols = _each(lambda x: _split_dot(x, ones, TN), m)
        kd_term = _each(lambda x, y: rsum(x * y), dkd, kd)
        dgc = _each(lambda m_, mc, dqd_, qd_, kt, dkbg_, kbg_, dgl_, gl_:
                    rsum(m_) - mc + rsum(dqd_ * qd_) - kt + rsum(dkbg_ * kbg_)
                    + jnp.where(last, jnp.sum(kt, axis=0, keepdims=True) + dgl_ * gl_, 0.0),
                    m, m_cols, dqd, qd, kd_term, dkbg, kbg, dgl, gl)
        dg = _each(lambda x: _split_dot(upper, x, NN), dgc)
        for g in range(grp):
            dq_ref[rows[g], :] = dq[g]
            dk_ref[rows[g], :] = dk[g]
            dv_ref[rows[g], :] = dvb[g] * be[g]
            dg_ref[0, rows[g], :] = dg[g]
            db_ref[0, rows[g], :] = jnp.broadcast_to(rsum(dkb[g] * k[g]) + rsum(dvb[g] * v[g]), (CH, DH))

    r = bsz * lp
    return pl.pallas_call(
        body, name="gdn_local_bwd", grid=(bh, ngrp),
        in_specs=[head(0), head(GW), head(2 * GW), lane_b, lane_b, per_chunk(1, CH), per_chunk(CH, CH),
                  per_chunk(DH, DH), per_chunk(DH, DH), head(0), head(0), head(0)],
        out_specs=[head(0), head(0), head(0), lane_b, lane_b],
        out_shape=[jax.ShapeDtypeStruct((r, GW), F32)] * 3 + [jax.ShapeDtypeStruct((bh, lp, DH), F32)] * 2,
        compiler_params=_cparams(("parallel", "parallel")),
    )(qkv, qkv, qkv, gcb, bb, gct, tinv_all, states, dstates, vnew, dvnew, d_o)


def _place():
    return lax.axis_index("x"), lax.axis_index("y"), lax.axis_index("c")


def _any_specs(n):
    return [pl.BlockSpec(memory_space=pl.ANY)] * n


def _all_gather(name, shards):
    n = len(shards)

    def body(*refs):
        ins, outs = refs[:n], refs[n:2 * n]
        send_sems, recv_sems, local_sems = refs[2 * n:]
        x, y, c = _place()
        me, sibling = (x, y, c), (x, y, 1 - c)
        chips = [(1 - x, y), (x, 1 - y), (1 - x, 1 - y)]

        def slot(i, p):
            return outs[i].at[4 * p[0] + 2 * p[1] + p[2]]

        def copy(i, k, block, to, src=None):
            return pltpu.make_async_remote_copy(
                src_ref=slot(i, block) if src is None else src, dst_ref=slot(i, block),
                send_sem=send_sems.at[i, k], recv_sem=recv_sems.at[i, k], device_id=to, device_id_type=MESH)

        mine = [pltpu.make_async_copy(ins[i], slot(i, me), local_sems.at[i]) for i in range(n)]
        for cp in mine:
            cp.start()
        first = []
        for i in range(n):
            first.append(copy(i, 0, me, sibling, src=ins[i]))
            first += [copy(i, 1 + j, me, (*chip, c), src=ins[i]) for j, chip in enumerate(chips)]
        for cp in first:
            cp.start()
        passed = []
        for j, chip in enumerate(chips):
            for i in range(n):
                copy(i, 1 + j, (*chip, c), me).wait_recv()
                fwd = copy(i, 4 + j, (*chip, c), sibling)
                fwd.start()
                passed.append(fwd)
        for i in range(n):
            copy(i, 0, sibling, me).wait_recv()
            for j, chip in enumerate(chips):
                copy(i, 4 + j, (*chip, 1 - c), me).wait_recv()
        for cp in first + passed:
            cp.wait_send()
        for cp in mine:
            cp.wait()

    return pl.pallas_call(
        body, name=name,
        in_specs=_any_specs(n), out_specs=_any_specs(n),
        out_shape=[jax.ShapeDtypeStruct((N_DEV,) + s.shape, s.dtype) for s in shards],
        scratch_shapes=[pltpu.SemaphoreType.DMA((n, 7)), pltpu.SemaphoreType.DMA((n, 7)),
                        pltpu.SemaphoreType.DMA((n,))],
    )(*shards)


HBM_SPEC = pl.BlockSpec(memory_space=pltpu.HBM)
SEM_SPEC = pl.BlockSpec(memory_space=pltpu.SEMAPHORE)
EFFECT = pltpu.SideEffectType.DATAFLOW_SIDE_EFFECTING
N_PEER = N_DEV - 1


def _peer(r):
    x, y, c = _place()
    return ((1 - x) if r & 4 else x, (1 - y) if r & 2 else y, (1 - c) if r & 1 else c)


def _slot_of(p):
    return 4 * p[0] + 2 * p[1] + p[2]


def _hbm(a):
    return pltpu.with_memory_space_constraint(a, pltpu.HBM)


def _gather_copy(src, land, send_sems, recv_sems, i, r, block):
    return pltpu.make_async_remote_copy(
        src_ref=src, dst_ref=land.at[_slot_of(block)], send_sem=send_sems.at[i * N_PEER + r - 1],
        recv_sem=recv_sems.at[i * N_PEER + r - 1], device_id=_peer(r), device_id_type=MESH)


def _gather_start(name, groups):
    sizes = [len(g) for g in groups]
    shards = [s for g in groups for s in g]
    n = len(shards)

    def body(*refs):
        srcs, lands = refs[:n], refs[n:2 * n]
        sems = refs[2 * n:2 * n + 2 * len(groups)]
        token = refs[-1]
        me = _place()
        k = 0
        for gi, size in enumerate(sizes):
            for i in range(size):
                for r in range(1, N_DEV):
                    _gather_copy(srcs[k], lands[k], sems[2 * gi], sems[2 * gi + 1], i, r, me).start()
                k += 1
        token[...] = jnp.zeros_like(token)

    sem_shapes = [pltpu.SemaphoreType.DMA((size * N_PEER,)) for size in sizes for _ in range(2)]
    res = pl.pallas_call(
        body, name=name,
        in_specs=[HBM_SPEC] * (2 * n),
        out_specs=[SEM_SPEC] * len(sem_shapes) + [HBM_SPEC] * (2 * n) + [pl.BlockSpec(memory_space=pltpu.VMEM)],
        out_shape=sem_shapes + [pltpu.HBM(s.shape, s.dtype) for s in shards]
        + [pltpu.HBM((N_DEV,) + s.shape, s.dtype) for s in shards] + [jax.ShapeDtypeStruct((8, DH), F32)],
        input_output_aliases={i: len(sem_shapes) + i for i in range(2 * n)},
        compiler_params=pltpu.CompilerParams(has_side_effects=EFFECT),
    )(*[_hbm(s) for s in shards], *[_hbm(lax.empty((N_DEV,) + s.shape, s.dtype)) for s in shards])
    nsem = len(sem_shapes)
    out, k = [], 0
    for gi, size in enumerate(sizes):
        out.append((res[2 * gi], res[2 * gi + 1], list(res[nsem + k:nsem + k + size]),
                    list(res[nsem + n + k:nsem + n + k + size])))
        k += size
    return out, res[-1]


def _gather_wait(name, group, after):
    send_sems, recv_sems, shards, lands = group
    n = len(shards)

    def body(*refs):
        srcs, lands_ = refs[:n], refs[n:2 * n]
        s_sems, r_sems = refs[2 * n], refs[2 * n + 1]
        for i in range(n):
            for r in range(1, N_DEV):
                cp = _gather_copy(srcs[i], lands_[i], s_sems, r_sems, i, r, _peer(r))
                cp.wait_send()
                cp.wait_recv()

    res = pl.pallas_call(
        body, name=name,
        in_specs=[HBM_SPEC] * (2 * n) + [SEM_SPEC, SEM_SPEC, pl.BlockSpec(memory_space=pl.ANY)],
        out_specs=[HBM_SPEC] * (2 * n),
        out_shape=[pltpu.HBM(s.shape, s.dtype) for s in shards] + [pltpu.HBM(l.shape, l.dtype) for l in lands],
        input_output_aliases={i: i for i in range(2 * n)},
        compiler_params=pltpu.CompilerParams(has_side_effects=EFFECT),
    )(*shards, *lands, send_sems, recv_sems, after)
    return list(res[n:])


def _sibling_exchange(name, grads):
    n = len(grads)

    def body(*refs):
        ins, outs = refs[:n], refs[n:2 * n]
        send_sems, recv_sems = refs[2 * n:]
        x, y, c = _place()
        cps = [pltpu.make_async_remote_copy(
            src_ref=ins[i].at[1 - c], dst_ref=outs[i], send_sem=send_sems.at[i], recv_sem=recv_sems.at[i],
            device_id=(x, y, 1 - c), device_id_type=MESH) for i in range(n)]
        for cp in cps:
            cp.start()
        for cp in cps:
            cp.wait_recv()
        for cp in cps:
            cp.wait_send()

    return pl.pallas_call(
        body, name=name,
        in_specs=_any_specs(n), out_specs=_any_specs(n),
        out_shape=[jax.ShapeDtypeStruct(g.shape[1:], g.dtype) for g in grads],
        scratch_shapes=[pltpu.SemaphoreType.DMA((n,)), pltpu.SemaphoreType.DMA((n,))],
    )(*grads)


def _chip_exchange(name, parts):
    n = len(parts)

    def body(*refs):
        ins, outs = refs[:n], refs[n:2 * n]
        send_sems, recv_sems, local_sems = refs[2 * n:]
        x, y, c = _place()
        chips = [(1 - x, y), (x, 1 - y), (1 - x, 1 - y)]
        mine = [pltpu.make_async_copy(ins[i].at[2 * x + y], outs[i].at[0], local_sems.at[i]) for i in range(n)]
        for cp in mine:
            cp.start()
        cps = []
        for j, chip in enumerate(chips):
            for i in range(n):
                cps.append(pltpu.make_async_remote_copy(
                    src_ref=ins[i].at[2 * chip[0] + chip[1]], dst_ref=outs[i].at[1 + j],
                    send_sem=send_sems.at[i, j], recv_sem=recv_sems.at[i, j],
                    device_id=(*chip, c), device_id_type=MESH))
        for cp in cps:
            cp.start()
        for cp in cps:
            cp.wait_recv()
        for cp in cps:
            cp.wait_send()
        for cp in mine:
            cp.wait()

    return pl.pallas_call(
        body, name=name,
        in_specs=_any_specs(n), out_specs=_any_specs(n),
        out_shape=[jax.ShapeDtypeStruct(p.shape, p.dtype) for p in parts],
        scratch_shapes=[pltpu.SemaphoreType.DMA((n, 3)), pltpu.SemaphoreType.DMA((n, 3)),
                        pltpu.SemaphoreType.DMA((n,))],
    )(*parts)


def _pair_sum(name, grads, recv, core):
    _, _, r, c = grads.shape
    tr = _div_tile(r, 256, 8)

    def body(core_ref, g_ref, r_ref, o_ref):
        o_ref[...] = g_ref[0] + r_ref[...]

    return pl.pallas_call(
        body, name=name,
        grid_spec=pltpu.PrefetchScalarGridSpec(
            num_scalar_prefetch=1, grid=(4, r // tr),
            in_specs=[pl.BlockSpec((1, 1, tr, c), lambda s, i, cr: (cr[0], s, i, 0)),
                      pl.BlockSpec((1, tr, c), lambda s, i, cr: (s, i, 0))],
            out_specs=pl.BlockSpec((1, tr, c), lambda s, i, cr: (s, i, 0))),
        out_shape=jax.ShapeDtypeStruct(recv.shape, F32),
        compiler_params=_cparams(("parallel", "parallel")),
    )(core, grads, recv)


def _adamw_math(w, g, m, v):
    m = ADAM_B1 * m + (1.0 - ADAM_B1) * g
    v = ADAM_B2 * v + (1.0 - ADAM_B2) * (g * g)
    m_hat = m / (1.0 - ADAM_B1 ** ADAM_STEP)
    v_hat = v / (1.0 - ADAM_B2 ** ADAM_STEP)
    delta = -ADAM_LR * (m_hat / (jnp.sqrt(v_hat) + ADAM_EPS) + ADAM_WD * w)
    return delta, m, v


def _adamw(name, parts, w, m, v):
    nslot, r, c = parts.shape
    tr = _div_tile(r, 256, 8) if r % 8 == 0 else r

    def body(p_ref, w_ref, m_ref, v_ref, g_out, d_out, m_out, v_out):
        g = p_ref[0]
        for s in range(1, nslot):
            g = g + p_ref[s]
        d, mn, vn = _adamw_math(w_ref[...], g, m_ref[...], v_ref[...])
        g_out[...] = g
        d_out[...] = d
        m_out[...] = mn
        v_out[...] = vn

    blk = pl.BlockSpec((tr, c), lambda i: (i, 0))
    return pl.pallas_call(
        body, name=name, grid=(r // tr,),
        in_specs=[pl.BlockSpec((nslot, tr, c), lambda i: (0, i, 0)), blk, blk, blk],
        out_specs=[blk] * 4,
        out_shape=[jax.ShapeDtypeStruct((r, c), F32)] * 4,
        compiler_params=_cparams(("parallel",)),
    )(parts, w, m, v)


SMALL_ROWS, SMALL_COLS = 32, 3 * GW


def _pack_small(gains4, meta, conv_qkv, conv_sc, a_log, dt_bias, gdn_norm):
    def padc(a):
        return jnp.pad(a, ((0, 0), (0, SMALL_COLS - a.shape[1])))
    scal = jnp.concatenate([jnp.pad(a_log, ((0, 0), (0, DH - HEADS))), jnp.pad(dt_bias, ((0, 0), (0, DH - HEADS))),
                            gdn_norm], axis=1)
    rows = [padc(g) for g in gains4] + [padc(meta), padc(conv_qkv), padc(conv_sc), padc(scal)]
    slab = jnp.concatenate(rows, axis=0)
    return jnp.pad(slab, ((0, SMALL_ROWS - slab.shape[0]), (0, 0)))


def _unpack_small(slab):
    gains = [slab[i:i + 1, :D] for i in range(4)]
    meta = slab[4:20, :D]
    conv_qkv = slab[20:24, :]
    conv_sc = slab[24:27, :SW]
    a_log = slab[27:28, 0:HEADS]
    dt_bias = slab[27:28, DH:DH + HEADS]
    gdn_norm = slab[27:28, 2 * DH:3 * DH]
    return gains, meta, conv_qkv, conv_sc, a_log, dt_bias, gdn_norm


def kernel(x, meta_tokens, mix_pre_norm, mix_post_norm, ffn_pre_norm, ffn_post_norm, w_in, conv_qkv, a_log, dt_bias, gdn_norm, conv_sc, w_out, w_gate, w_up, w_down, loss_target, m_meta_tokens, m_mix_pre_norm, m_mix_post_norm, m_ffn_pre_norm, m_ffn_post_norm, m_w_in, m_conv_qkv, m_a_log, m_dt_bias, m_gdn_norm, m_conv_sc, m_w_out, m_w_gate, m_w_up, m_w_down, v_meta_tokens, v_mix_pre_norm, v_mix_post_norm, v_ffn_pre_norm, v_ffn_post_norm, v_w_in, v_conv_qkv, v_a_log, v_dt_bias, v_gdn_norm, v_conv_sc, v_w_out, v_w_gate, v_w_up, v_w_down):
    bsz, seq, _ = x.shape
    lp = -(-(N_META + seq) // CH) * CH
    padf = lp - N_META - seq
    first_real = padf + N_META
    rows = bsz * lp
    nchunk = lp // CH
    tr = _div_tile(lp, 528)
    tm = _div_tile(rows, 1056)
    cx, cy, cc = _place()
    dev = 4 * cx + 2 * cy + cc

    small_shard = jnp.concatenate([
        jnp.pad(meta_tokens, ((0, 0), (0, 256 - DH))),
        jnp.pad(conv_qkv[0], ((0, 0), (0, 256 - 3 * GW // N_DEV))),
        jnp.pad(conv_sc[0], ((0, 5), (0, 256 - SW // N_DEV)))], axis=0)
    g_in, g_small = _all_gather("gather_w_in", [w_in[0].astype(BF16), small_shard])
    my_out, my_ffn = [w_out[0].astype(BF16)], [w_gate[0].astype(BF16), w_up[0].astype(BF16), w_down[0].astype(BF16)]
    (grp_out, grp_ffn), token = _gather_start("gather_rest_start", [my_out, my_ffn])
    after_start = token[0:1, 0:1]

    def cols_full(g):
        return jnp.transpose(g, (1, 0, 2)).reshape(g.shape[1], -1)

    def with_mine(land, mine):
        return lax.dynamic_update_index_in_dim(land, mine, dev, 0)

    win = cols_full(g_in)
    win = jnp.concatenate([win[:, :BA0 + 2 * HEADS], jnp.zeros((D, BA_W - 2 * HEADS), BF16),
                           win[:, BA0 + 2 * HEADS:]], axis=1)
    meta_full = cols_full(g_small[:, 0:16, :DH])
    convq_full = cols_full(g_small[:, 16:20, :3 * GW // N_DEV])
    convs_full = cols_full(g_small[:, 20:23, :SW // N_DEV])
    alog_v = jnp.pad(a_log, ((0, 0), (HEADS, DH - 2 * HEADS)))
    dtb_v = jnp.pad(dt_bias, ((0, 0), (HEADS, DH - 2 * HEADS)))

    h0 = jnp.concatenate([jnp.zeros((bsz, padf, D), F32), jnp.broadcast_to(meta_full[None], (bsz, N_META, D)), x],
                         axis=1).reshape(rows, D)
    tgt = jnp.pad(loss_target, ((0, 0), (first_real, 0), (0, 0))).reshape(rows, D)
    u1 = _pre_norm(h0, mix_pre_norm + after_start, bsz, lp, tr)
    (proj,) = _mm("proj_in", [(u1, win)], "nn", tm, 768, D)
    qkv = _gdn_prep(proj, convq_full, bsz, lp, padf)
    gb, gcs = _gates(proj, alog_v, dtb_v, bsz, lp, padf)

    def lane_bcast(cols):
        t = jnp.transpose(cols.reshape(bsz, lp, HEADS), (0, 2, 1)).reshape(bsz * HEADS, lp, 1)
        return jnp.broadcast_to(t, (bsz * HEADS, lp, DH))

    bb = lane_bcast(gb[:, 0:HEADS])
    gcb = lane_bcast(gcs[:, HEADS:2 * HEADS])
    gct = jnp.transpose(gcs[:, HEADS:2 * HEADS].reshape(bsz, nchunk, CH, HEADS), (0, 3, 1, 2)).reshape(
        bsz * HEADS, nchunk, 1, CH)
    u_loc, l1, l2, l3, w_t, k_dec, tinv_all, g_last = _gdn_local_fwd(qkv, gcb, bb, gct, bsz, lp)
    o, v_new, states = _gdn_scan_fwd(u_loc, l1, l2, g_last, bsz, lp)
    o, v_new = o.reshape(rows, GW), v_new.reshape(rows, GW)
    mix_g, mix_s = _mix_heads(proj, o, gdn_norm, convs_full, bsz, lp, padf)
    mixin = jnp.concatenate([mix_g, mix_s], axis=1)
    (l_out,) = _gather_wait("gather_w_out_wait", grp_out, mixin)
    wout = with_mine(l_out, my_out[0]).reshape(D, D)
    (mix,) = _mm("proj_out", [(mixin, wout)], "nn", tm, 512, D)
    l_gate, l_up, l_down = _gather_wait("gather_ffn_wait", grp_ffn, mix)
    wgate, wup = cols_full(with_mine(l_gate, my_ffn[0])), cols_full(with_mine(l_up, my_ffn[1]))
    wdown = with_mine(l_down, my_ffn[2]).reshape(FF, D)
    h1, u2 = _mid_norms(h0, mix, mix_post_norm, ffn_pre_norm, bsz, lp, tr)
    gate, up, act = _ffn_up(u2, wgate, wup, _div_tile(rows, 528), 1408)
    (ffn,) = _mm("ffn_down", [(act, wdown)], "nn", tm, 512, 1408)

    dy, dffn, d_g4, loss_part = _loss_head(h1, ffn, tgt, ffn_post_norm, bsz, lp, tr, first_real)
    loss = lax.psum(loss_part[0, 0], ("x", "y", "c"))
    (dwdown,) = _mm("dw_down", [(act, dffn)], "tn", 1408, 512, tm)

    def swiglu_bwd(da, g, u):
        return da * u * _dsilu(g), da * _silu(g)

    dgate, dup = _mm("d_act", [(dffn, wdown)], "nt", tm, 1408, D, out_dtypes=(BF16, BF16),
                     epilogue=swiglu_bwd, extras=(gate, up))
    (dwgate,) = _mm("dw_gate", [(u2, dgate)], "tn", D, 1408, tm)
    (dwup,) = _mm("dw_up", [(u2, dup)], "tn", D, 1408, tm)
    (du2,) = _mm("d_u2", [(dgate, wgate), (dup, wup)], "nt", tm, 512, 1408)
    dh1, dmix, d_g3, d_g2 = _mid_norms_bwd(h1, mix, du2, dy, ffn_pre_norm, mix_post_norm, bsz, lp, tr)
    (dwout,) = _mm("dw_out", [(mixin, dmix)], "tn", D, 512, tm)
    (dmixin,) = _mm("d_mixin", [(dmix, wout)], "nt", tm, 512, D)
    d_o, dz, dsx, dsb, dsc, d_gn, d_convs = _mix_heads_bwd(proj, o, gdn_norm, convs_full, dmixin, bsz, lp, padf)
    dv_new, dstates = _gdn_scan_bwd(d_o, l3, w_t, k_dec, g_last, bsz, lp)
    dq, dk, dv, dg_b, dbeta_b = _gdn_local_bwd(qkv, gcb, bb, gct, tinv_all, states, dstates, v_new,
                                               dv_new.reshape(rows, GW), d_o, bsz, lp)
    dqkv = jnp.concatenate([dq, dk, dv], axis=1)
    dpqkv, d_convq = _gdn_prep_bwd(proj, convq_full, dqkv, bsz, lp, padf)
    dba, d_scal = _gates_bwd(proj, alog_v, dtb_v, dbeta_b, dg_b, gb, bsz, lp, padf)
    dproj = jnp.concatenate([dpqkv, dz, dba, dsx, dsb, dsc], axis=1)
    (dwin,) = _mm("dw_in", [(u1, dproj)], "tn", D, 768, tm)
    (du1,) = _mm("d_u1", [(dproj, win)], "nt", tm, 512, 768)
    dh0, d_g1 = _pre_norm_bwd(h0, du1, dh1, mix_pre_norm, bsz, lp, tr)
    dh0 = dh0.reshape(bsz, lp, D)
    grad_x = dh0[:, first_real:]
    d_meta = jnp.sum(dh0[:, padf:first_real], axis=0)

    dwin = jnp.concatenate([dwin[:, :BA0 + 2 * HEADS], dwin[:, SX0:]], axis=1)

    def col_blocks(g):
        r = g.shape[0]
        return jnp.transpose(g.reshape(r, 4, 2, -1), (2, 1, 0, 3))

    def row_blocks(g):
        c = g.shape[1]
        return jnp.transpose(g.reshape(4, 2, -1, c), (1, 0, 2, 3))

    big = [col_blocks(dwin), row_blocks(dwout), col_blocks(dwgate), col_blocks(dwup), row_blocks(dwdown)]
    from_sibling = _sibling_exchange("grads_to_sibling", big)
    core = jnp.reshape(cc, (1,)).astype(jnp.int32)
    names = ["w_in", "w_out", "w_gate", "w_up", "w_down"]
    chip_parts = [_pair_sum("pair_sum_" + nm, g, r, core) for nm, g, r in zip(names, big, from_sibling)]
    reduced = _chip_exchange("grads_across_chips", chip_parts)

    d_alog = d_scal[0:1, HEADS:2 * HEADS]
    d_dtb = d_scal[1:2, HEADS:2 * HEADS]
    slab = _pack_small([d_g1, d_g2, d_g3, d_g4], d_meta, d_convq, d_convs, d_alog, d_dtb, d_gn)
    (slabs,) = _all_gather("gather_small_grads", [slab])
    off_meta = dev * DH
    off_cq = dev * (3 * GW // N_DEV)
    off_cs = dev * (SW // N_DEV)

    def shard_of(s):
        gains, meta, cq, cs, al, db, gn = _unpack_small(s)
        return gains, lax.dynamic_slice_in_dim(meta, off_meta, DH, 1), \
            lax.dynamic_slice_in_dim(cq, off_cq, 3 * GW // N_DEV, 1), \
            lax.dynamic_slice_in_dim(cs, off_cs, SW // N_DEV, 1), al, db, gn

    slabs_local = jax.vmap(lambda s: _pack_small(*shard_of(s)))(slabs)
    w_small = _pack_small([mix_pre_norm, mix_post_norm, ffn_pre_norm, ffn_post_norm], meta_tokens, conv_qkv[0],
                          conv_sc[0], a_log, dt_bias, gdn_norm)
    m_small = _pack_small([m_mix_pre_norm, m_mix_post_norm, m_ffn_pre_norm, m_ffn_post_norm], m_meta_tokens,
                          m_conv_qkv[0], m_conv_sc[0], m_a_log, m_dt_bias, m_gdn_norm)
    v_small = _pack_small([v_mix_pre_norm, v_mix_post_norm, v_ffn_pre_norm, v_ffn_post_norm], v_meta_tokens,
                          v_conv_qkv[0], v_conv_sc[0], v_a_log, v_dt_bias, v_gdn_norm)
    small_out = _adamw("adamw_small", slabs_local, w_small, m_small, v_small)

    big_w = [(w_in, m_w_in, v_w_in), (w_out, m_w_out, v_w_out), (w_gate, m_w_gate, v_w_gate),
             (w_up, m_w_up, v_w_up), (w_down, m_w_down, v_w_down)]
    big_out = [_adamw("adamw_" + nm, parts, w[0], m[0], v[0]) for nm, parts, (w, m, v) in zip(names, reduced, big_w)]

    def small_leaves(slab_out):
        gains, meta, cq, cs, al, db, gn = _unpack_small(slab_out)
        return {"meta_tokens": meta[:, :DH], "mix_pre_norm": gains[0], "mix_post_norm": gains[1],
                "ffn_pre_norm": gains[2], "ffn_post_norm": gains[3], "conv_qkv": cq[None, :, :3 * GW // N_DEV],
                "a_log": al, "dt_bias": db, "gdn_norm": gn, "conv_sc": cs[None, :, :SW // N_DEV]}

    order = ["meta_tokens", "mix_pre_norm", "mix_post_norm", "ffn_pre_norm", "ffn_post_norm", "w_in", "conv_qkv",
             "a_log", "dt_bias", "gdn_norm", "conv_sc", "w_out", "w_gate", "w_up", "w_down"]
    outs = [loss, grad_x]
    for kind in range(4):
        leaves = small_leaves(small_out[kind])
        for nm, res in zip(names, big_out):
            leaves[nm] = res[kind][None]
        outs += [leaves[nm] for nm in order]
    return tuple(outs)
```

```python
import functools

import jax
import jax.numpy as jnp
from jax import lax
from jax.experimental import pallas as pl
from jax.experimental.pallas import tpu as pltpu

F32 = jnp.float32
BF16 = jnp.bfloat16
HIGHEST = lax.Precision.HIGHEST
MESH = pl.DeviceIdType.MESH

D = 1024
N_META = 16
HEADS = 4
DH = 128
GW = HEADS * DH
SW = D - GW
FF = 2816
CH = 64
KQ = 4
KS = 3
EPS = 1e-6
IN_W = 3 * GW + GW + 2 * HEADS + 3 * SW
Q0, K0, V0, Z0, BA0, SX0, SB0, SC0, PW = 0, 512, 1024, 1536, 2048, 2304, 2816, 3328, 3840
BA_W = SX0 - BA0
N_DEV = 8
VMEM_LIMIT = 56 * 1024 * 1024

ADAM_LR, ADAM_B1, ADAM_B2, ADAM_EPS, ADAM_WD, ADAM_STEP = 0.001, 0.9, 0.999, 1e-08, 0.01, 10


def _cparams(sem):
    return pltpu.CompilerParams(dimension_semantics=sem, vmem_limit_bytes=VMEM_LIMIT)


def _div_tile(n, target, mult=16):
    best = None
    for t in range(mult, min(n, target) + 1, mult):
        if n % t == 0:
            best = t
    assert best is not None, (n, target)
    return best


def _silu(x):
    return x * jax.nn.sigmoid(x)


def _dsilu(x):
    s = jax.nn.sigmoid(x)
    return s * (1.0 + x * (1.0 - s))


def _bdot(a, b, dims):
    return lax.dot_general(a.astype(BF16), b.astype(BF16), (dims, ((), ())), preferred_element_type=F32)


NN = ((1,), (0,))
NT = ((1,), (1,))
TN = ((0,), (0,))


def _fdot(a, b, dims=NN):
    return lax.dot_general(a, b, (dims, ((), ())), precision=HIGHEST, preferred_element_type=F32)


def _mm(name, pairs, mode, tm, tn, tk, out_dtypes=(F32,), epilogue=None, extras=(), after=None):
    a0, b0 = pairs[0]
    if mode == "nn":
        (m, k), n = a0.shape, b0.shape[1]
    elif mode == "nt":
        (m, k), n = a0.shape, b0.shape[0]
    else:
        (k, m), n = a0.shape, b0.shape[1]
    assert m % tm == 0 and n % tn == 0 and k % tk == 0, (name, m, n, k, tm, tn, tk)
    nk = k // tk
    npair = len(pairs)
    nex = len(extras)
    nout = len(out_dtypes)
    ntok = 0 if after is None else 1
    dims = {"nn": NN, "nt": NT, "tn": TN}[mode]

    def body(*refs):
        ab = refs[:2 * npair]
        ex = refs[2 * npair:2 * npair + nex]
        outs = refs[2 * npair + nex + ntok:2 * npair + nex + ntok + nout]
        acc_ref = refs[-1]
        kk = pl.program_id(2)

        part = None
        for p in range(npair):
            d = lax.dot_general(ab[2 * p][...], ab[2 * p + 1][...], (dims, ((), ())), preferred_element_type=F32)
            part = d if part is None else part + d

        def finish(acc):
            res = epilogue(acc, *[e[...] for e in ex]) if epilogue is not None else (acc,)
            for o, r in zip(outs, res):
                o[...] = r.astype(o.dtype)

        if nk == 1:
            finish(part)
        else:
            @pl.when(kk == 0)
            def _():
                acc_ref[...] = part

            @pl.when(kk > 0)
            def _():
                acc_ref[...] += part

            @pl.when(kk == nk - 1)
            def _():
                finish(acc_ref[...])

    if mode == "nn":
        a_spec = pl.BlockSpec((tm, tk), lambda i, j, q: (i, q))
        b_spec = pl.BlockSpec((tk, tn), lambda i, j, q: (q, j))
    elif mode == "nt":
        a_spec = pl.BlockSpec((tm, tk), lambda i, j, q: (i, q))
        b_spec = pl.BlockSpec((tn, tk), lambda i, j, q: (j, q))
    else:
        a_spec = pl.BlockSpec((tk, tm), lambda i, j, q: (q, i))
        b_spec = pl.BlockSpec((tk, tn), lambda i, j, q: (q, j))
    o_spec = pl.BlockSpec((tm, tn), lambda i, j, q: (i, j))
    flat = [t for pr in pairs for t in pr]
    res = pl.pallas_call(
        body, name=name,
        grid=(m // tm, n // tn, nk),
        in_specs=[a_spec, b_spec] * npair + [o_spec] * nex + [pl.BlockSpec((8, DH), lambda i, j, q: (0, 0))] * ntok,
        out_specs=[o_spec] * nout,
        out_shape=[jax.ShapeDtypeStruct((m, n), dt) for dt in out_dtypes],
        scratch_shapes=[pltpu.VMEM((tm, tn), F32)],
        compiler_params=_cparams(("parallel", "parallel", "arbitrary")),
    )(*flat, *extras, *([] if after is None else [after]))
    return res


def _ffn_up(u2, wg, wu, tm, tn):
    m, k = u2.shape
    n = wg.shape[1]

    def body(a_ref, g_ref, u_ref, gate_ref, up_ref, act_ref):
        a = a_ref[...]
        g = jnp.dot(a, g_ref[...], preferred_element_type=F32)
        u = jnp.dot(a, u_ref[...], preferred_element_type=F32)
        gate_ref[...] = g
        up_ref[...] = u
        act_ref[...] = (_silu(g) * u).astype(BF16)

    o_spec = pl.BlockSpec((tm, tn), lambda i, j: (i, j))
    w_spec = pl.BlockSpec((k, tn), lambda i, j: (0, j))
    return pl.pallas_call(
        body, name="ffn_up", grid=(m // tm, n // tn),
        in_specs=[pl.BlockSpec((tm, k), lambda i, j: (i, 0)), w_spec, w_spec],
        out_specs=[o_spec, o_spec, o_spec],
        out_shape=[jax.ShapeDtypeStruct((m, n), F32), jax.ShapeDtypeStruct((m, n), F32),
                   jax.ShapeDtypeStruct((m, n), BF16)],
        compiler_params=_cparams(("parallel", "parallel")),
    )(u2, wg, wu)


def _rms(x, gain):
    r = lax.rsqrt(jnp.mean(x * x, axis=-1, keepdims=True) + EPS)
    return x * r * gain, r


def _rms_bwd(x, gain, dy):
    r = lax.rsqrt(jnp.mean(x * x, axis=-1, keepdims=True) + EPS)
    dyw = dy * gain
    dx = r * dyw - x * (r * r * r) * jnp.mean(dyw * x, axis=-1, keepdims=True)
    return dx, dy * x * r


def _row_specs(tr, width, nb):
    return pl.BlockSpec((tr, width), lambda b, j: (b * nb + j, 0))


def _vec_spec(width):
    return pl.BlockSpec((1, width), lambda b, j: (0, 0))


def _first_step(b, j):
    return jnp.logical_and(b == 0, j == 0)


def _pre_norm(h0, gain, bsz, lp, tr):
    nb = lp // tr

    def body(h_ref, g_ref, u_ref):
        u_ref[...] = _rms(h_ref[...], g_ref[...])[0].astype(BF16)

    return pl.pallas_call(
        body, name="mix_pre_norm", grid=(bsz, nb),
        in_specs=[_row_specs(tr, D, nb), _vec_spec(D)],
        out_specs=_row_specs(tr, D, nb),
        out_shape=jax.ShapeDtypeStruct(h0.shape, BF16),
        compiler_params=_cparams(("parallel", "parallel")),
    )(h0, gain)


def _mid_norms(h0, mix, g_post, g_pre, bsz, lp, tr):
    nb = lp // tr

    def body(h_ref, m_ref, gp_ref, gq_ref, h1_ref, u2_ref):
        h1 = h_ref[...] + _rms(m_ref[...], gp_ref[...])[0]
        h1_ref[...] = h1
        u2_ref[...] = _rms(h1, gq_ref[...])[0].astype(BF16)

    rs = _row_specs(tr, D, nb)
    return pl.pallas_call(
        body, name="mid_norms", grid=(bsz, nb),
        in_specs=[rs, rs, _vec_spec(D), _vec_spec(D)],
        out_specs=[rs, rs],
        out_shape=[jax.ShapeDtypeStruct(h0.shape, F32), jax.ShapeDtypeStruct(h0.shape, BF16)],
        compiler_params=_cparams(("parallel", "parallel")),
    )(h0, mix, g_post, g_pre)


def _loss_head(h1, ffn, tgt, g_post, bsz, lp, tr, first_real):
    nb = lp // tr

    def body(h_ref, f_ref, t_ref, g_ref, dy_ref, dffn_ref, dg_ref, loss_ref):
        b, j = pl.program_id(0), pl.program_id(1)
        f = f_ref[...]
        g = g_ref[...]
        y = h_ref[...] + _rms(f, g)[0]
        t = j * tr + lax.broadcasted_iota(jnp.int32, (tr, 1), 0)
        err = jnp.where(t >= first_real, y - t_ref[...], 0.0)
        dy = err * (1.0 / D)
        dy_ref[...] = dy
        dx, dgc = _rms_bwd(f, g, dy)
        dffn_ref[...] = dx.astype(BF16)
        lsum = jnp.sum(jnp.sum(err * err, axis=0, keepdims=True), axis=1, keepdims=True) * (0.5 / D)

        @pl.when(_first_step(b, j))
        def _():
            dg_ref[...] = jnp.zeros_like(dg_ref)
            loss_ref[...] = jnp.zeros_like(loss_ref)

        dg_ref[...] += jnp.sum(dgc, axis=0, keepdims=True)
        loss_ref[...] += jnp.broadcast_to(lsum, loss_ref.shape)

    rs = _row_specs(tr, D, nb)
    return pl.pallas_call(
        body, name="loss_head", grid=(bsz, nb),
        in_specs=[rs, rs, rs, _vec_spec(D)],
        out_specs=[rs, rs, _vec_spec(D), _vec_spec(DH)],
        out_shape=[jax.ShapeDtypeStruct(h1.shape, F32), jax.ShapeDtypeStruct(h1.shape, BF16),
                   jax.ShapeDtypeStruct((1, D), F32), jax.ShapeDtypeStruct((1, DH), F32)],
        compiler_params=_cparams(("arbitrary", "arbitrary")),
    )(h1, ffn, tgt, g_post)


def _mid_norms_bwd(h1, mix, du2, dy, g_pre, g_post, bsz, lp, tr):
    nb = lp // tr

    def body(h_ref, m_ref, du_ref, dy_ref, gq_ref, gp_ref, dh1_ref, dmix_ref, dgq_ref, dgp_ref):
        b, j = pl.program_id(0), pl.program_id(1)
        dx, dgq = _rms_bwd(h_ref[...], gq_ref[...], du_ref[...])
        dh1 = dy_ref[...] + dx
        dh1_ref[...] = dh1
        dm, dgp = _rms_bwd(m_ref[...], gp_ref[...], dh1)
        dmix_ref[...] = dm.astype(BF16)

        @pl.when(_first_step(b, j))
        def _():
            dgq_ref[...] = jnp.zeros_like(dgq_ref)
            dgp_ref[...] = jnp.zeros_like(dgp_ref)

        dgq_ref[...] += jnp.sum(dgq, axis=0, keepdims=True)
        dgp_ref[...] += jnp.sum(dgp, axis=0, keepdims=True)

    rs = _row_specs(tr, D, nb)
    return pl.pallas_call(
        body, name="mid_norms_bwd", grid=(bsz, nb),
        in_specs=[rs, rs, rs, rs, _vec_spec(D), _vec_spec(D)],
        out_specs=[rs, rs, _vec_spec(D), _vec_spec(D)],
        out_shape=[jax.ShapeDtypeStruct(h1.shape, F32), jax.ShapeDtypeStruct(h1.shape, BF16),
                   jax.ShapeDtypeStruct((1, D), F32), jax.ShapeDtypeStruct((1, D), F32)],
        compiler_params=_cparams(("arbitrary", "arbitrary")),
    )(h1, mix, du2, dy, g_pre, g_post)


def _pre_norm_bwd(h0, du1, dh1, gain, bsz, lp, tr):
    nb = lp // tr

    def body(h_ref, du_ref, dh1_ref, g_ref, dh0_ref, dg_ref):
        b, j = pl.program_id(0), pl.program_id(1)
        dx, dgc = _rms_bwd(h_ref[...], g_ref[...], du_ref[...])
        dh0_ref[...] = dh1_ref[...] + dx

        @pl.when(_first_step(b, j))
        def _():
            dg_ref[...] = jnp.zeros_like(dg_ref)

        dg_ref[...] += jnp.sum(dgc, axis=0, keepdims=True)

    rs = _row_specs(tr, D, nb)
    return pl.pallas_call(
        body, name="pre_norm_bwd", grid=(bsz, nb),
        in_specs=[rs, rs, rs, _vec_spec(D)],
        out_specs=[rs, _vec_spec(D)],
        out_shape=[jax.ShapeDtypeStruct(h0.shape, F32), jax.ShapeDtypeStruct((1, D), F32)],
        compiler_params=_cparams(("arbitrary", "arbitrary")),
    )(h0, du1, dh1, gain)


def _shift_down(x, s, lp):
    return x if s == 0 else pltpu.roll(x, s, axis=0)


def _shift_up(x, s, lp):
    return x if s == 0 else pltpu.roll(x, lp - s, axis=0)


def _conv_fwd(x, w, taps, lp):
    y = None
    for i in range(taps):
        term = _shift_down(x, taps - 1 - i, lp) * w[i:i + 1, :]
        y = term if y is None else y + term
    return y


def _conv_bwd(x, w, dy, taps, lp):
    dx = None
    dws = []
    for i in range(taps):
        s = taps - 1 - i
        term = _shift_up(dy, s, lp) * w[i:i + 1, :]
        dx = term if dx is None else dx + term
        dws.append(jnp.sum(dy * _shift_down(x, s, lp), axis=0, keepdims=True))
    return dx, jnp.concatenate(dws, axis=0)


def _slab(lp, col0):
    return pl.BlockSpec((lp, DH), lambda b, s: (b, col0 // DH + s))


def _qkv_act(c, s):
    y = _silu(c)
    r = lax.rsqrt(jnp.sum(y * y, axis=-1, keepdims=True) + EPS)
    scale = jnp.where(s < HEADS, DH ** -0.5, 1.0)
    return y, r, scale


def _gdn_prep(proj, conv_qkv, bsz, lp, padf):
    def body(p_ref, w_ref, o_ref):
        s = pl.program_id(1)
        c = _conv_fwd(p_ref[...], w_ref[...], KQ, lp)
        y, r, scale = _qkv_act(c, s)
        out = jnp.where(s < 2 * HEADS, y * (r * scale), y)
        t = lax.broadcasted_iota(jnp.int32, (lp, 1), 0)
        o_ref[...] = jnp.where(t >= padf, out, 0.0)

    return pl.pallas_call(
        body, name="gdn_prep", grid=(bsz, 3 * HEADS),
        in_specs=[_slab(lp, Q0), pl.BlockSpec((KQ, DH), lambda b, s: (0, s))],
        out_specs=_slab(lp, 0),
        out_shape=jax.ShapeDtypeStruct((bsz * lp, 3 * GW), F32),
        compiler_params=_cparams(("parallel", "parallel")),
    )(proj, conv_qkv)


def _gdn_prep_bwd(proj, conv_qkv, dqkv, bsz, lp, padf):
    def body(p_ref, w_ref, d_ref, dp_ref, dw_ref):
        s, b = pl.program_id(0), pl.program_id(1)
        x = p_ref[...]
        w = w_ref[...]
        c = _conv_fwd(x, w, KQ, lp)
        y, r, scale = _qkv_act(c, s)
        dn = d_ref[...] * scale
        n = y * r
        dy_norm = r * (dn - n * jnp.sum(dn * n, axis=-1, keepdims=True))
        dyy = jnp.where(s < 2 * HEADS, dy_norm, dn)
        t = lax.broadcasted_iota(jnp.int32, (lp, 1), 0)
        dc = jnp.where(t >= padf, dyy * _dsilu(c), 0.0)
        dx, dw = _conv_bwd(x, w, dc, KQ, lp)
        dp_ref[...] = jnp.where(t >= padf, dx, 0.0).astype(BF16)

        @pl.when(b == 0)
        def _():
            dw_ref[...] = jnp.zeros_like(dw_ref)

        dw_ref[...] += dw

    sl = lambda col0: pl.BlockSpec((lp, DH), lambda s, b: (b, col0 // DH + s))
    return pl.pallas_call(
        body, name="gdn_prep_bwd", grid=(3 * HEADS, bsz),
        in_specs=[sl(Q0), pl.BlockSpec((KQ, DH), lambda s, b: (0, s)), sl(0)],
        out_specs=[sl(0), pl.BlockSpec((KQ, DH), lambda s, b: (0, s))],
        out_shape=[jax.ShapeDtypeStruct((bsz * lp, 3 * GW), BF16), jax.ShapeDtypeStruct((KQ, 3 * GW), F32)],
        compiler_params=_cparams(("parallel", "arbitrary")),
    )(proj, conv_qkv, dqkv)


def _softplus(x):
    e = jnp.exp(-jnp.abs(x))
    one_e = 1.0 + e
    l1p = jnp.where(one_e == 1.0, e, jnp.log(one_e) * (e / (one_e - 1.0)))
    return jnp.maximum(x, 0.0) + l1p


def _tri(n, kind):
    i = lax.broadcasted_iota(jnp.int32, (n, n), 0)
    j = lax.broadcasted_iota(jnp.int32, (n, n), 1)
    return {"incl": i >= j, "strict": i > j, "upper": i <= j}[kind]


def _gates(proj, alog_v, dtb_v, bsz, lp, padf):
    nchunk = lp // CH

    def body(p_ref, al_ref, dt_ref, gb_ref, cs_ref):
        x = p_ref[...]
        lane = lax.broadcasted_iota(jnp.int32, (lp, DH), 1)
        t = lax.broadcasted_iota(jnp.int32, (lp, DH), 0)
        beta = jax.nn.sigmoid(x)
        g = -jnp.exp(al_ref[...]) * _softplus(x + dt_ref[...])
        gb = jnp.where(lane < HEADS, beta, jnp.where(lane < 2 * HEADS, g, 0.0))
        gb_ref[...] = jnp.where(t >= padf, gb, 0.0)
        ltri = _tri(CH, "incl").astype(F32)

        def step(n, carry):
            r0 = pl.multiple_of(n * CH, CH)
            cs_ref[pl.ds(r0, CH), :] = _fdot(ltri, gb_ref[pl.ds(r0, CH), :])
            return carry

        lax.fori_loop(0, nchunk, step, 0)

    blk = pl.BlockSpec((lp, DH), lambda b: (b, BA0 // DH))
    vec = pl.BlockSpec((1, DH), lambda b: (0, 0))
    out = pl.BlockSpec((lp, DH), lambda b: (b, 0))
    return pl.pallas_call(
        body, name="gates", grid=(bsz,),
        in_specs=[blk, vec, vec], out_specs=[out, out],
        out_shape=[jax.ShapeDtypeStruct((bsz * lp, DH), F32)] * 2,
        compiler_params=_cparams(("parallel",)),
    )(proj, alog_v, dtb_v)


def _gates_bwd(proj, alog_v, dtb_v, dbeta_b, dg_b, gb, bsz, lp, padf):
    def body(p_ref, al_ref, dt_ref, db_ref, dg_ref, gb_ref, dl_ref, dv_ref):
        b = pl.program_id(0)
        x = p_ref[...]
        lane = lax.broadcasted_iota(jnp.int32, (lp, DH), 1)
        t = lax.broadcasted_iota(jnp.int32, (lp, DH), 0)
        dbeta = jnp.zeros((lp, DH), F32)
        dg = jnp.zeros((lp, DH), F32)
        for h in range(HEADS):
            dbeta = jnp.where(lane == h, db_ref[h], dbeta)
            dg = jnp.where(lane == HEADS + h, dg_ref[h], dg)
        real = t >= padf
        dbeta = jnp.where(real, dbeta, 0.0)
        dg = jnp.where(real, dg, 0.0)
        sb = jax.nn.sigmoid(x)
        neg_ea = -jnp.exp(al_ref[...])
        dsp = dg * neg_ea * jax.nn.sigmoid(x + dt_ref[...])
        dl = dbeta * sb * (1.0 - sb) + dsp
        dl_ref[...] = jnp.concatenate([dl, jnp.zeros((lp, BA_W - DH), F32)], axis=1).astype(BF16)
        dalog = jnp.sum(dg * gb_ref[...], axis=0, keepdims=True)
        ddt = jnp.sum(dsp, axis=0, keepdims=True)

        @pl.when(b == 0)
        def _():
            dv_ref[...] = jnp.zeros_like(dv_ref)

        dv_ref[...] += jnp.concatenate([dalog, ddt], axis=0)

    blk = pl.BlockSpec((lp, DH), lambda b: (b, BA0 // DH))
    vec = pl.BlockSpec((1, DH), lambda b: (0, 0))
    hb = pl.BlockSpec((HEADS, lp, DH), lambda b: (b, 0, 0))
    return pl.pallas_call(
        body, name="gates_bwd", grid=(bsz,),
        in_specs=[blk, vec, vec, hb, hb, pl.BlockSpec((lp, DH), lambda b: (b, 0))],
        out_specs=[pl.BlockSpec((lp, BA_W), lambda b: (b, 0)), pl.BlockSpec((2, DH), lambda b: (0, 0))],
        out_shape=[jax.ShapeDtypeStruct((bsz * lp, BA_W), BF16), jax.ShapeDtypeStruct((2, DH), F32)],
        compiler_params=_cparams(("arbitrary",)),
    )(proj, alog_v, dtb_v, dbeta_b, dg_b, gb)


def _mix_heads(proj, o, gdn_norm, conv_sc, bsz, lp, padf):
    def body(o_ref, z_ref, gn_ref, sx_ref, sb_ref, sc_ref, w_ref, og_ref, os_ref):
        og_ref[...] = (_rms(o_ref[...], gn_ref[...])[0] * _silu(z_ref[...])).astype(BF16)
        conv = _conv_fwd(sc_ref[...] * sx_ref[...], w_ref[...], KS, lp)
        t = lax.broadcasted_iota(jnp.int32, (lp, 1), 0)
        os_ref[...] = jnp.where(t >= padf, sb_ref[...] * conv, 0.0).astype(BF16)

    half = _slab(lp, 0)
    return pl.pallas_call(
        body, name="mix_heads", grid=(bsz, HEADS),
        in_specs=[half, _slab(lp, Z0), pl.BlockSpec((1, DH), lambda b, s: (0, 0)),
                  _slab(lp, SX0), _slab(lp, SB0), _slab(lp, SC0), pl.BlockSpec((KS, DH), lambda b, s: (0, s))],
        out_specs=[half, half],
        out_shape=[jax.ShapeDtypeStruct((bsz * lp, GW), BF16)] * 2,
        compiler_params=_cparams(("parallel", "parallel")),
    )(o, proj, gdn_norm, proj, proj, proj, conv_sc)


def _mix_heads_bwd(proj, o, gdn_norm, conv_sc, dmixin, bsz, lp, padf):
    def body(o_ref, z_ref, gn_ref, sx_ref, sb_ref, sc_ref, w_ref, dg_ref, ds_ref,
             do_ref, dz_ref, dsx_ref, dsb_ref, dsc_ref, dgn_ref, dw_ref):
        s, b = pl.program_id(0), pl.program_id(1)
        t = lax.broadcasted_iota(jnp.int32, (lp, 1), 0)
        real = t >= padf
        o, z, gn, d = o_ref[...], z_ref[...], gn_ref[...], dg_ref[...]
        sz = _silu(z)
        on = _rms(o, gn)[0]
        dz_ref[...] = (d * on * _dsilu(z)).astype(BF16)
        dox, dgn = _rms_bwd(o, gn, d * sz)
        do_ref[...] = dox
        sx, sb, sc, w, e = sx_ref[...], sb_ref[...], sc_ref[...], w_ref[...], ds_ref[...]
        e = jnp.where(real, e, 0.0)
        xin = sc * sx
        conv = _conv_fwd(xin, w, KS, lp)
        dsb_ref[...] = (e * conv).astype(BF16)
        dxin, dw = _conv_bwd(xin, w, e * sb, KS, lp)
        dxin = jnp.where(real, dxin, 0.0)
        dsx_ref[...] = (dxin * sc).astype(BF16)
        dsc_ref[...] = (dxin * sx).astype(BF16)

        @pl.when(jnp.logical_and(s == 0, b == 0))
        def _():
            dgn_ref[...] = jnp.zeros_like(dgn_ref)

        @pl.when(b == 0)
        def _():
            dw_ref[...] = jnp.zeros_like(dw_ref)

        dgn_ref[...] += jnp.sum(dgn, axis=0, keepdims=True)
        dw_ref[...] += dw

    sl = lambda col0: pl.BlockSpec((lp, DH), lambda s, b: (b, col0 // DH + s))
    half = sl(0)
    return pl.pallas_call(
        body, name="mix_heads_bwd", grid=(HEADS, bsz),
        in_specs=[half, sl(Z0), pl.BlockSpec((1, DH), lambda s, b: (0, 0)), sl(SX0), sl(SB0), sl(SC0),
                  pl.BlockSpec((KS, DH), lambda s, b: (0, s)), sl(0), sl(GW)],
        out_specs=[half] * 5 + [pl.BlockSpec((1, DH), lambda s, b: (0, 0)), pl.BlockSpec((KS, DH), lambda s, b: (0, s))],
        out_shape=[jax.ShapeDtypeStruct((bsz * lp, GW), F32)] + [jax.ShapeDtypeStruct((bsz * lp, GW), BF16)] * 4
        + [jax.ShapeDtypeStruct((1, DH), F32), jax.ShapeDtypeStruct((KS, SW), F32)],
        compiler_params=_cparams(("arbitrary", "arbitrary")),
    )(o, proj, gdn_norm, proj, proj, proj, conv_sc, dmixin, dmixin)


def _split_dot(a, b, dims=NN):
    ah = a.astype(BF16)
    al = (a - ah.astype(F32)).astype(BF16)
    bh = b.astype(BF16)
    bl = (b - bh.astype(F32)).astype(BF16)

    def d(x, y):
        return lax.dot_general(x, y, (dims, ((), ())), preferred_element_type=F32)

    return d(ah, bh) + (d(ah, bl) + d(al, bh))


def _eye(n):
    return (lax.broadcasted_iota(jnp.int32, (n, n), 0) == lax.broadcasted_iota(jnp.int32, (n, n), 1)).astype(F32)


def _each(f, *lists):
    return [f(*xs) for xs in zip(*lists)]


def _inv_unit_lower(a_list):
    eye = _eye(CH)
    p = _each(lambda a: -a, a_list)
    t = _each(lambda x: eye + x, p)
    p = _each(lambda x: _split_dot(x, x), p)
    for level in range(5):
        if level < 4:
            prod = _each(lambda tt, pp: _split_dot(jnp.concatenate([tt, pp], axis=0), pp), t, p)
            t = _each(lambda tt, pr: tt + pr[:CH], t, prod)
            p = _each(lambda pr: pr[CH:], prod)
        else:
            t = _each(lambda tt, pp: tt + _split_dot(tt, pp), t, p)
    return t


def _chunk_gates(gc_ref, be_ref, gt_ref, g):
    rows = pl.ds(g * CH, CH)
    gc = gc_ref[0, rows, :]
    be = be_ref[0, rows, :]
    gcl = gc_ref[0, pl.ds(g * CH + CH - 1, 1), :]
    gt = gt_ref[0, g]
    incl, upper = _tri(CH, "incl"), _tri(CH, "upper")
    dec = jnp.where(incl, jnp.exp(jnp.where(incl, gc[:, :CH] - gt, 0.0)), 0.0)
    dec_t = jnp.where(upper, jnp.exp(jnp.where(upper, gt - gc[:, :CH], 0.0)), 0.0)
    return rows, be, dec, dec_t, jnp.exp(gc), jnp.exp(gcl - gc), jnp.exp(gcl)


def _chunk_group(nchunk):
    return 3 if nchunk % 3 == 0 else 1


def _local_group(nchunk):
    return 11 if nchunk % 11 == 0 else _chunk_group(nchunk)


def _gdn_local_specs(lp, grp):
    ngrp = lp // (grp * CH)
    head = lambda col0: pl.BlockSpec((grp * CH, DH), lambda i, j: ((i // HEADS) * ngrp + j, col0 // DH + i % HEADS))
    lane_b = pl.BlockSpec((1, grp * CH, DH), lambda i, j: (i, j, 0))
    per_chunk = lambda r, c: pl.BlockSpec((1, grp, r, c), lambda i, j: (i, j, 0, 0))
    return ngrp, head, lane_b, per_chunk


def _gdn_local_fwd(qkv, gcb, bb, gct, bsz, lp):
    nchunk = lp // CH
    grp = _local_group(nchunk)
    ngrp, head, lane_b, per_chunk = _gdn_local_specs(lp, grp)
    bh = bsz * HEADS

    def body(q_ref, k_ref, v_ref, gc_ref, be_ref, gt_ref,
             u_ref, l1_ref, l2_ref, l3_ref, wt_ref, kd_ref, ti_ref, gl_ref):
        incl, strict, upper = _tri(CH, "incl"), _tri(CH, "strict"), _tri(CH, "upper")
        eye = _eye(CH)
        rows, be, dec, dec_t, eg, ekd, gl = zip(*[_chunk_gates(gc_ref, be_ref, gt_ref, g) for g in range(grp)])
        q = [q_ref[r, :] for r in rows]
        k = [k_ref[r, :] for r in rows]
        v = [v_ref[r, :] for r in rows]
        kb = _each(lambda x, y: x * y, k, be)
        a = _each(lambda x, y, d: jnp.where(strict, _bdot(x, y, NT) * d, 0.0), kb, k, dec)
        tinv = _inv_unit_lower(a)
        u = _each(lambda t, x, y: _bdot(t, x * y, NN), tinv, v, be)
        w = _each(lambda t, x, y: _bdot(t, x * y, NN), tinv, kb, eg)
        qk = _each(lambda x, y, d: jnp.where(incl, _bdot(x, y, NT) * d, 0.0), q, k, dec)
        qk_t = _each(lambda x, y, d: jnp.where(upper, _bdot(y, x, NT) * d, 0.0), q, k, dec_t)
        qd = _each(lambda x, y: x * y, q, eg)
        kd = _each(lambda x, y: x * y, k, ekd)
        kd_t = _each(lambda x: _bdot(x, eye, TN), kd)
        qd_t = _each(lambda x: _bdot(x, eye, TN), qd)
        w_t = _each(lambda x: _bdot(x, eye, TN), w)
        for g in range(grp):
            u_ref[rows[g], :] = u[g]
            l1_ref[0, g] = jnp.concatenate([w[g], qd[g]], axis=0).astype(BF16)
            l2_ref[0, g] = jnp.concatenate([qk[g], kd_t[g]], axis=0).astype(BF16)
            l3_ref[0, g] = jnp.concatenate([qk_t[g], qd_t[g]], axis=0).astype(BF16)
            wt_ref[0, g] = w_t[g].astype(BF16)
            kd_ref[0, g] = kd[g].astype(BF16)
            ti_ref[0, g] = tinv[g]
            gl_ref[0, g] = gl[g]

    r = bsz * lp
    shapes = [((r, GW), F32), ((bh, nchunk, 2 * CH, DH), BF16), ((bh, nchunk, 3 * CH, CH), BF16),
              ((bh, nchunk, 3 * CH, CH), BF16), ((bh, nchunk, DH, CH), BF16), ((bh, nchunk, CH, DH), BF16),
              ((bh, nchunk, CH, CH), F32), ((bh, nchunk, 1, DH), F32)]
    return pl.pallas_call(
        body, name="gdn_local_fwd", grid=(bh, ngrp),
        in_specs=[head(0), head(GW), head(2 * GW), lane_b, lane_b, per_chunk(1, CH)],
        out_specs=[head(0), per_chunk(2 * CH, DH), per_chunk(3 * CH, CH), per_chunk(3 * CH, CH), per_chunk(DH, CH),
                   per_chunk(CH, DH), per_chunk(CH, CH), per_chunk(1, DH)],
        out_shape=[jax.ShapeDtypeStruct(s, dt) for s, dt in shapes],
        compiler_params=_cparams(("parallel", "parallel")),
    )(qkv, qkv, qkv, gcb, bb, gct)


def _scan_specs(bsz, lp, grp, order):
    ngrp = lp // (grp * CH)
    bh = bsz * HEADS
    seq = lambda: pl.BlockSpec((bsz, grp * CH, GW), lambda s: (0, order(s, ngrp), 0))
    per_chunk = lambda r, c: pl.BlockSpec((bh, grp, r, c), lambda s: (0, order(s, ngrp), 0, 0))
    return ngrp, bh, seq, per_chunk


def _gdn_scan_fwd(u, l1, l2, gl, bsz, lp):
    nchunk = lp // CH
    grp = _chunk_group(nchunk)
    ngrp, bh, seq, per_chunk = _scan_specs(bsz, lp, grp, lambda s, n: s)

    def body(u_ref, l1_ref, l2_ref, gl_ref, o_ref, vn_ref, st_ref, s_scr):
        @pl.when(pl.program_id(0) == 0)
        def _():
            s_scr[...] = jnp.zeros_like(s_scr)

        for c in range(grp):
            rows = slice(c * CH, (c + 1) * CH)
            for i in range(bh):
                b, h = divmod(i, HEADS)
                cols = slice(h * DH, (h + 1) * DH)
                s = s_scr[i]
                st_ref[i, c] = s
                x = jnp.dot(l1_ref[i, c], s.astype(BF16), preferred_element_type=F32)
                vnew = u_ref[b, rows, cols] - x[:CH]
                y = jnp.dot(l2_ref[i, c], vnew.astype(BF16), preferred_element_type=F32)
                o_ref[b, rows, cols] = x[CH:] + y[:CH]
                vn_ref[b, rows, cols] = vnew.astype(BF16)
                s_scr[i] = s * gl_ref[i, c] + y[CH:]

    return pl.pallas_call(
        body, name="gdn_scan_fwd", grid=(ngrp,),
        in_specs=[seq(), per_chunk(2 * CH, DH), per_chunk(3 * CH, CH), per_chunk(1, DH)],
        out_specs=[seq(), seq(), per_chunk(DH, DH)],
        out_shape=[jax.ShapeDtypeStruct((bsz, lp, GW), F32), jax.ShapeDtypeStruct((bsz, lp, GW), BF16),
                   jax.ShapeDtypeStruct((bh, nchunk, DH, DH), F32)],
        scratch_shapes=[pltpu.VMEM((bh, DH, DH), F32)],
        compiler_params=_cparams(("arbitrary",)),
    )(u.reshape(bsz, lp, GW), l1, l2, gl)


def _gdn_scan_bwd(d_o, l3, wt, kd, gl, bsz, lp):
    nchunk = lp // CH
    grp = _chunk_group(nchunk)
    ngrp, bh, seq, per_chunk = _scan_specs(bsz, lp, grp, lambda s, n: n - 1 - s)

    def body(do_ref, l3_ref, wt_ref, kd_ref, gl_ref, dvn_ref, dst_ref, ds_scr):
        @pl.when(pl.program_id(0) == 0)
        def _():
            ds_scr[...] = jnp.zeros_like(ds_scr)

        for c in reversed(range(grp)):
            rows = slice(c * CH, (c + 1) * CH)
            for i in range(bh):
                b, h = divmod(i, HEADS)
                cols = slice(h * DH, (h + 1) * DH)
                ds = ds_scr[i]
                dst_ref[i, c] = ds
                p = jnp.dot(l3_ref[i, c], do_ref[b, rows, cols].astype(BF16), preferred_element_type=F32)
                dvn = p[:CH] + jnp.dot(kd_ref[i, c], ds.astype(BF16), preferred_element_type=F32)
                dvn_ref[b, rows, cols] = dvn.astype(BF16)
                ds_scr[i] = ds * gl_ref[i, c] + p[CH:] - jnp.dot(wt_ref[i, c], dvn.astype(BF16),
                                                                 preferred_element_type=F32)

    return pl.pallas_call(
        body, name="gdn_scan_bwd", grid=(ngrp,),
        in_specs=[seq(), per_chunk(3 * CH, CH), per_chunk(DH, CH), per_chunk(CH, DH), per_chunk(1, DH)],
        out_specs=[seq(), per_chunk(DH, DH)],
        out_shape=[jax.ShapeDtypeStruct((bsz, lp, GW), BF16), jax.ShapeDtypeStruct((bh, nchunk, DH, DH), F32)],
        scratch_shapes=[pltpu.VMEM((bh, DH, DH), F32)],
        compiler_params=_cparams(("arbitrary",)),
    )(d_o.reshape(bsz, lp, GW), l3, wt, kd, gl)


def _gdn_local_bwd(qkv, gcb, bb, gct, tinv_all, states, dstates, vnew, dvnew, d_o, bsz, lp):
    nchunk = lp // CH
    grp = _local_group(nchunk)
    ngrp, head, lane_b, per_chunk = _gdn_local_specs(lp, grp)
    bh = bsz * HEADS

    def body(q_ref, k_ref, v_ref, gc_ref, be_ref, gt_ref, ti_ref, st_ref, dst_ref, vn_ref, dvn_ref, do_ref,
             dq_ref, dk_ref, dv_ref, dg_ref, db_ref):
        incl, strict = _tri(CH, "incl"), _tri(CH, "strict")
        upper = _tri(CH, "upper").astype(F32)
        ones = jnp.ones((CH, DH), F32)
        last = lax.broadcasted_iota(jnp.int32, (CH, 1), 0) == CH - 1

        def rsum(x):
            return jnp.sum(x, axis=-1, keepdims=True)

        def mul(xs, ys):
            return _each(lambda x, y: x * y, xs, ys)

        rows, be, dec, _, eg, ekd, gl = zip(*[_chunk_gates(gc_ref, be_ref, gt_ref, g) for g in range(grp)])
        q = [q_ref[r, :] for r in rows]
        k = [k_ref[r, :] for r in rows]
        v = [v_ref[r, :] for r in rows]
        vnew = [vn_ref[r, :] for r in rows]
        dvn = [dvn_ref[r, :] for r in rows]
        do = [do_ref[r, :] for r in rows]
        tinv = [ti_ref[0, g] for g in range(grp)]
        s = [st_ref[0, g] for g in range(grp)]
        ds = [dst_ref[0, g] for g in range(grp)]
        kb, vb, qd, kd = mul(k, be), mul(v, be), mul(q, eg), mul(k, ekd)
        kbg = mul(kb, eg)
        a = _each(lambda x, y, d: jnp.where(strict, _bdot(x, y, NT) * d, 0.0), kb, k, dec)
        qk = _each(lambda x, y, d: jnp.where(incl, _bdot(x, y, NT) * d, 0.0), q, k, dec)
        dqk = _each(lambda x, y: jnp.where(incl, _bdot(x, y, NT), 0.0), do, vnew)
        dqd = _each(lambda x, y: _bdot(x, y, NT), do, s)
        dkd = _each(lambda x, y: _bdot(x, y, NT), vnew, ds)
        dgl = _each(lambda x, y: jnp.sum(rsum(x * y), axis=0, keepdims=True), s, ds)
        dw = _each(lambda x, y: -_bdot(x, y, NT), dvn, s)
        dvb = _each(lambda t, x: _bdot(t, x, TN), tinv, dvn)
        dkbg = _each(lambda t, x: _bdot(t, x, TN), tinv, dw)
        dt = _each(lambda x, y, z, w: _bdot(x, y, NT) + _bdot(z, w, NT), dvn, vb, dw, kbg)
        tdt = _each(lambda t, x: _split_dot(t, x, TN), tinv, dt)
        da = _each(lambda x, t: jnp.where(strict, -_split_dot(x, t, NT), 0.0), tdt, tinv)
        dp, dr = mul(da, dec), mul(dqk, dec)
        dkb = _each(lambda x, y, z, e: _bdot(x, y, NN) + z * e, dp, k, dkbg, eg)
        dk = _each(lambda p_, kb_, r_, q_, dkd_, ekd_, dkb_, be_:
                   _bdot(p_, kb_, TN) + _bdot(r_, q_, TN) + dkd_ * ekd_ + dkb_ * be_,
                   dp, kb, dr, q, dkd, ekd, dkb, be)
        dq = _each(lambda r_, k_, dqd_, eg_: _bdot(r_, k_, NN) + dqd_ * eg_, dr, k, dqd, eg)
        m = _each(lambda da_, a_, dqk_, qk_: da_ * a_ + dqk_ * qk_, da, a, dqk, qk)
        m_cols = _each(lambda x: _split_dot(x, ones, TN), m)
        kd_term = _each(lambda x, y: rsum(x * y), dkd, kd)
        dgc = _each(lambda m_, mc, dqd_, qd_, kt, dkbg_, kbg_, dgl_, gl_:
                    rsum(m_) - mc + rsum(dqd_ * qd_) - kt + rsum(dkbg_ * kbg_)
                    + jnp.where(last, jnp.sum(kt, axis=0, keepdims=True) + dgl_ * gl_, 0.0),
                    m, m_cols, dqd, qd, kd_term, dkbg, kbg, dgl, gl)
        dg = _each(lambda x: _split_dot(upper, x, NN), dgc)
        for g in range(grp):
            dq_ref[rows[g], :] = dq[g]
            dk_ref[rows[g], :] = dk[g]
            dv_ref[rows[g], :] = dvb[g] * be[g]
            dg_ref[0, rows[g], :] = dg[g]
            db_ref[0, rows[g], :] = jnp.broadcast_to(rsum(dkb[g] * k[g]) + rsum(dvb[g] * v[g]), (CH, DH))

    r = bsz * lp
    return pl.pallas_call(
        body, name="gdn_local_bwd", grid=(bh, ngrp),
        in_specs=[head(0), head(GW), head(2 * GW), lane_b, lane_b, per_chunk(1, CH), per_chunk(CH, CH),
                  per_chunk(DH, DH), per_chunk(DH, DH), head(0), head(0), head(0)],
        out_specs=[head(0), head(0), head(0), lane_b, lane_b],
        out_shape=[jax.ShapeDtypeStruct((r, GW), F32)] * 3 + [jax.ShapeDtypeStruct((bh, lp, DH), F32)] * 2,
        compiler_params=_cparams(("parallel", "parallel")),
    )(qkv, qkv, qkv, gcb, bb, gct, tinv_all, states, dstates, vnew, dvnew, d_o)


def _place():
    return lax.axis_index("x"), lax.axis_index("y"), lax.axis_index("c")


def _any_specs(n):
    return [pl.BlockSpec(memory_space=pl.ANY)] * n


def _all_gather(name, shards):
    n = len(shards)

    def body(*refs):
        ins, outs = refs[:n], refs[n:2 * n]
        send_sems, recv_sems, local_sems = refs[2 * n:]
        x, y, c = _place()
        me, sibling = (x, y, c), (x, y, 1 - c)
        chips = [(1 - x, y), (x, 1 - y), (1 - x, 1 - y)]

        def slot(i, p):
            return outs[i].at[4 * p[0] + 2 * p[1] + p[2]]

        def copy(i, k, block, to, src=None):
            return pltpu.make_async_remote_copy(
                src_ref=slot(i, block) if src is None else src, dst_ref=slot(i, block),
                send_sem=send_sems.at[i, k], recv_sem=recv_sems.at[i, k], device_id=to, device_id_type=MESH)

        mine = [pltpu.make_async_copy(ins[i], slot(i, me), local_sems.at[i]) for i in range(n)]
        for cp in mine:
            cp.start()
        first = []
        for i in range(n):
            first.append(copy(i, 0, me, sibling, src=ins[i]))
            first += [copy(i, 1 + j, me, (*chip, c), src=ins[i]) for j, chip in enumerate(chips)]
        for cp in first:
            cp.start()
        passed = []
        for j, chip in enumerate(chips):
            for i in range(n):
                copy(i, 1 + j, (*chip, c), me).wait_recv()
                fwd = copy(i, 4 + j, (*chip, c), sibling)
                fwd.start()
                passed.append(fwd)
        for i in range(n):
            copy(i, 0, sibling, me).wait_recv()
            for j, chip in enumerate(chips):
                copy(i, 4 + j, (*chip, 1 - c), me).wait_recv()
        for cp in first + passed:
            cp.wait_send()
        for cp in mine:
            cp.wait()

    return pl.pallas_call(
        body, name=name,
        in_specs=_any_specs(n), out_specs=_any_specs(n),
        out_shape=[jax.ShapeDtypeStruct((N_DEV,) + s.shape, s.dtype) for s in shards],
        scratch_shapes=[pltpu.SemaphoreType.DMA((n, 7)), pltpu.SemaphoreType.DMA((n, 7)),
                        pltpu.SemaphoreType.DMA((n,))],
    )(*shards)


HBM_SPEC = pl.BlockSpec(memory_space=pltpu.HBM)
SEM_SPEC = pl.BlockSpec(memory_space=pltpu.SEMAPHORE)
EFFECT = pltpu.SideEffectType.DATAFLOW_SIDE_EFFECTING
N_PEER = N_DEV - 1


def _peer(r):
    x, y, c = _place()
    return ((1 - x) if r & 4 else x, (1 - y) if r & 2 else y, (1 - c) if r & 1 else c)


def _slot_of(p):
    return 4 * p[0] + 2 * p[1] + p[2]


def _hbm(a):
    return pltpu.with_memory_space_constraint(a, pltpu.HBM)


def _gather_pattern(src, land):
    me = _place()
    return [(src, land.at[_slot_of(me)], _peer(r), land.at[_slot_of(_peer(r))]) for r in range(1, N_DEV)]


def _sibling_pattern(src, land):
    x, y, c = _place()
    return [(src.at[1 - c], land, (x, y, 1 - c), land)]


def _chip_pattern(src, land):
    x, y, c = _place()
    chips = [(1 - x, y), (x, 1 - y), (1 - x, 1 - y)]
    return [(src.at[2 * ch[0] + ch[1]], land.at[j], (*ch, c), land.at[j]) for j, ch in enumerate(chips)]


def _split_start(name, groups, pattern, land_shape):
    sizes = [len(g) for g in groups]
    arrays = [s for g in groups for s in g]
    n = len(arrays)
    land_shapes = [land_shape(s.shape) for s in arrays]

    def body(*refs):
        srcs, lands = refs[:n], refs[n:2 * n]
        sems = refs[2 * n:2 * n + 2 * len(groups)]
        token = refs[-1]
        k = 0
        for gi, size in enumerate(sizes):
            for i in range(size):
                copies = pattern(srcs[k], lands[k])
                for j, (s, d, peer, _) in enumerate(copies):
                    pltpu.make_async_remote_copy(
                        src_ref=s, dst_ref=d, send_sem=sems[2 * gi].at[i * len(copies) + j],
                        recv_sem=sems[2 * gi + 1].at[i * len(copies) + j], device_id=peer, device_id_type=MESH).start()
                k += 1
        token[...] = jnp.zeros_like(token)

    per_array = {_gather_pattern: N_PEER, _sibling_pattern: 1, _chip_pattern: 3}[pattern]
    sem_shapes = [pltpu.SemaphoreType.DMA((size * per_array,)) for size in sizes for _ in range(2)]
    res = pl.pallas_call(
        body, name=name,
        in_specs=[HBM_SPEC] * (2 * n),
        out_specs=[SEM_SPEC] * len(sem_shapes) + [HBM_SPEC] * (2 * n) + [pl.BlockSpec(memory_space=pltpu.VMEM)],
        out_shape=sem_shapes + [pltpu.HBM(s.shape, s.dtype) for s in arrays]
        + [pltpu.HBM(ls, s.dtype) for ls, s in zip(land_shapes, arrays)] + [jax.ShapeDtypeStruct((8, DH), F32)],
        input_output_aliases={i: len(sem_shapes) + i for i in range(2 * n)},
        compiler_params=pltpu.CompilerParams(has_side_effects=EFFECT),
    )(*[_hbm(s) for s in arrays], *[_hbm(lax.empty(ls, s.dtype)) for ls, s in zip(land_shapes, arrays)])
    nsem = len(sem_shapes)
    out, k = [], 0
    for gi, size in enumerate(sizes):
        out.append((res[2 * gi], res[2 * gi + 1], list(res[nsem + k:nsem + k + size]),
                    list(res[nsem + n + k:nsem + n + k + size])))
        k += size
    return out, res[-1]


def _split_wait(name, group, pattern, after):
    send_sems, recv_sems, arrays, lands = group
    n = len(arrays)

    def body(*refs):
        srcs, lands_ = refs[:n], refs[n:2 * n]
        s_sems, r_sems = refs[2 * n], refs[2 * n + 1]
        for i in range(n):
            copies = pattern(srcs[i], lands_[i])
            for j, (s, _, peer, filled) in enumerate(copies):
                cp = pltpu.make_async_remote_copy(
                    src_ref=s, dst_ref=filled, send_sem=s_sems.at[i * len(copies) + j],
                    recv_sem=r_sems.at[i * len(copies) + j], device_id=peer, device_id_type=MESH)
                cp.wait_send()
                cp.wait_recv()

    res = pl.pallas_call(
        body, name=name,
        in_specs=[HBM_SPEC] * (2 * n) + [SEM_SPEC, SEM_SPEC, pl.BlockSpec(memory_space=pl.ANY)],
        out_specs=[HBM_SPEC] * (2 * n),
        out_shape=[pltpu.HBM(s.shape, s.dtype) for s in arrays] + [pltpu.HBM(l.shape, l.dtype) for l in lands],
        input_output_aliases={i: i for i in range(2 * n)},
        compiler_params=pltpu.CompilerParams(has_side_effects=EFFECT),
    )(*arrays, *lands, send_sems, recv_sems, after)
    return list(res[:n]), list(res[n:])


def _pair_sum(name, grads, recv, core):
    _, _, r, c = grads.shape
    tr = _div_tile(r, 256, 8)

    def body(core_ref, g_ref, r_ref, o_ref):
        o_ref[...] = g_ref[0] + r_ref[...]

    return pl.pallas_call(
        body, name=name,
        grid_spec=pltpu.PrefetchScalarGridSpec(
            num_scalar_prefetch=1, grid=(4, r // tr),
            in_specs=[pl.BlockSpec((1, 1, tr, c), lambda s, i, cr: (cr[0], s, i, 0)),
                      pl.BlockSpec((1, tr, c), lambda s, i, cr: (s, i, 0))],
            out_specs=pl.BlockSpec((1, tr, c), lambda s, i, cr: (s, i, 0))),
        out_shape=jax.ShapeDtypeStruct(recv.shape, F32),
        compiler_params=_cparams(("parallel", "parallel")),
    )(core, grads, recv)


def _adamw_math(w, g, m, v):
    m = ADAM_B1 * m + (1.0 - ADAM_B1) * g
    v = ADAM_B2 * v + (1.0 - ADAM_B2) * (g * g)
    m_hat = m / (1.0 - ADAM_B1 ** ADAM_STEP)
    v_hat = v / (1.0 - ADAM_B2 ** ADAM_STEP)
    delta = -ADAM_LR * (m_hat / (jnp.sqrt(v_hat) + ADAM_EPS) + ADAM_WD * w)
    return delta, m, v


def _adamw(name, parts, w, m, v):
    nslot, r, c = parts.shape
    tr = _div_tile(r, 256, 8) if r % 8 == 0 else r

    def body(p_ref, w_ref, m_ref, v_ref, g_out, d_out, m_out, v_out):
        g = p_ref[0]
        for s in range(1, nslot):
            g = g + p_ref[s]
        d, mn, vn = _adamw_math(w_ref[...], g, m_ref[...], v_ref[...])
        g_out[...] = g
        d_out[...] = d
        m_out[...] = mn
        v_out[...] = vn

    blk = pl.BlockSpec((tr, c), lambda i: (i, 0))
    return pl.pallas_call(
        body, name=name, grid=(r // tr,),
        in_specs=[pl.BlockSpec((nslot, tr, c), lambda i: (0, i, 0)), blk, blk, blk],
        out_specs=[blk] * 4,
        out_shape=[jax.ShapeDtypeStruct((r, c), F32)] * 4,
        compiler_params=_cparams(("parallel",)),
    )(parts, w, m, v)


def _adamw_reduced(name, parts, chip, recv, w, m, v):
    _, r, c = parts.shape
    tr = _div_tile(r, 256, 8)

    def body(chip_ref, p_ref, r_ref, w_ref, m_ref, v_ref, g_out, d_out, m_out, v_out):
        g = p_ref[0]
        for s in range(3):
            g = g + r_ref[s]
        d, mn, vn = _adamw_math(w_ref[...], g, m_ref[...], v_ref[...])
        g_out[...] = g
        d_out[...] = d
        m_out[...] = mn
        v_out[...] = vn

    blk = pl.BlockSpec((tr, c), lambda i, ch: (i, 0))
    return pl.pallas_call(
        body, name=name,
        grid_spec=pltpu.PrefetchScalarGridSpec(
            num_scalar_prefetch=1, grid=(r // tr,),
            in_specs=[pl.BlockSpec((1, tr, c), lambda i, ch: (ch[0], i, 0)),
                      pl.BlockSpec((3, tr, c), lambda i, ch: (0, i, 0)), blk, blk, blk],
            out_specs=[blk] * 4),
        out_shape=[jax.ShapeDtypeStruct((r, c), F32)] * 4,
        compiler_params=_cparams(("parallel",)),
    )(chip, parts, recv, w, m, v)


SMALL_ROWS, SMALL_COLS = 32, 3 * GW


def _pack_small(gains4, meta, conv_qkv, conv_sc, a_log, dt_bias, gdn_norm):
    def padc(a):
        return jnp.pad(a, ((0, 0), (0, SMALL_COLS - a.shape[1])))
    scal = jnp.concatenate([jnp.pad(a_log, ((0, 0), (0, DH - HEADS))), jnp.pad(dt_bias, ((0, 0), (0, DH - HEADS))),
                            gdn_norm], axis=1)
    rows = [padc(g) for g in gains4] + [padc(meta), padc(conv_qkv), padc(conv_sc), padc(scal)]
    slab = jnp.concatenate(rows, axis=0)
    return jnp.pad(slab, ((0, SMALL_ROWS - slab.shape[0]), (0, 0)))


def _unpack_small(slab):
    gains = [slab[i:i + 1, :D] for i in range(4)]
    meta = slab[4:20, :D]
    conv_qkv = slab[20:24, :]
    conv_sc = slab[24:27, :SW]
    a_log = slab[27:28, 0:HEADS]
    dt_bias = slab[27:28, DH:DH + HEADS]
    gdn_norm = slab[27:28, 2 * DH:3 * DH]
    return gains, meta, conv_qkv, conv_sc, a_log, dt_bias, gdn_norm


def kernel(x, meta_tokens, mix_pre_norm, mix_post_norm, ffn_pre_norm, ffn_post_norm, w_in, conv_qkv, a_log, dt_bias, gdn_norm, conv_sc, w_out, w_gate, w_up, w_down, loss_target, m_meta_tokens, m_mix_pre_norm, m_mix_post_norm, m_ffn_pre_norm, m_ffn_post_norm, m_w_in, m_conv_qkv, m_a_log, m_dt_bias, m_gdn_norm, m_conv_sc, m_w_out, m_w_gate, m_w_up, m_w_down, v_meta_tokens, v_mix_pre_norm, v_mix_post_norm, v_ffn_pre_norm, v_ffn_post_norm, v_w_in, v_conv_qkv, v_a_log, v_dt_bias, v_gdn_norm, v_conv_sc, v_w_out, v_w_gate, v_w_up, v_w_down):
    bsz, seq, _ = x.shape
    lp = -(-(N_META + seq) // CH) * CH
    padf = lp - N_META - seq
    first_real = padf + N_META
    rows = bsz * lp
    nchunk = lp // CH
    tr = _div_tile(lp, 528)
    tm = _div_tile(rows, 1056)
    cx, cy, cc = _place()
    dev = 4 * cx + 2 * cy + cc

    small_shard = jnp.concatenate([
        jnp.pad(meta_tokens, ((0, 0), (0, 256 - DH))),
        jnp.pad(conv_qkv[0], ((0, 0), (0, 256 - 3 * GW // N_DEV))),
        jnp.pad(conv_sc[0], ((0, 5), (0, 256 - SW // N_DEV)))], axis=0)
    g_in, g_small = _all_gather("gather_w_in", [w_in[0].astype(BF16), small_shard])
    my_out, my_ffn = [w_out[0].astype(BF16)], [w_gate[0].astype(BF16), w_up[0].astype(BF16), w_down[0].astype(BF16)]
    (grp_out, grp_ffn), token = _split_start("gather_rest_start", [my_out, my_ffn], _gather_pattern,
                                             lambda s: (N_DEV,) + s)
    after_start = token[0:1, 0:1]

    def cols_full(g):
        return jnp.transpose(g, (1, 0, 2)).reshape(g.shape[1], -1)

    def with_mine(land, mine):
        return lax.dynamic_update_index_in_dim(land, mine, dev, 0)

    win = cols_full(g_in)
    win = jnp.concatenate([win[:, :BA0 + 2 * HEADS], jnp.zeros((D, BA_W - 2 * HEADS), BF16),
                           win[:, BA0 + 2 * HEADS:]], axis=1)
    meta_full = cols_full(g_small[:, 0:16, :DH])
    convq_full = cols_full(g_small[:, 16:20, :3 * GW // N_DEV])
    convs_full = cols_full(g_small[:, 20:23, :SW // N_DEV])
    alog_v = jnp.pad(a_log, ((0, 0), (HEADS, DH - 2 * HEADS)))
    dtb_v = jnp.pad(dt_bias, ((0, 0), (HEADS, DH - 2 * HEADS)))

    h0 = jnp.concatenate([jnp.zeros((bsz, padf, D), F32), jnp.broadcast_to(meta_full[None], (bsz, N_META, D)), x],
                         axis=1).reshape(rows, D)
    tgt = jnp.pad(loss_target, ((0, 0), (first_real, 0), (0, 0))).reshape(rows, D)
    u1 = _pre_norm(h0, mix_pre_norm + after_start, bsz, lp, tr)
    (proj,) = _mm("proj_in", [(u1, win)], "nn", tm, 768, D)
    qkv = _gdn_prep(proj, convq_full, bsz, lp, padf)
    gb, gcs = _gates(proj, alog_v, dtb_v, bsz, lp, padf)

    def lane_bcast(cols):
        t = jnp.transpose(cols.reshape(bsz, lp, HEADS), (0, 2, 1)).reshape(bsz * HEADS, lp, 1)
        return jnp.broadcast_to(t, (bsz * HEADS, lp, DH))

    bb = lane_bcast(gb[:, 0:HEADS])
    gcb = lane_bcast(gcs[:, HEADS:2 * HEADS])
    gct = jnp.transpose(gcs[:, HEADS:2 * HEADS].reshape(bsz, nchunk, CH, HEADS), (0, 3, 1, 2)).reshape(
        bsz * HEADS, nchunk, 1, CH)
    u_loc, l1, l2, l3, w_t, k_dec, tinv_all, g_last = _gdn_local_fwd(qkv, gcb, bb, gct, bsz, lp)
    o, v_new, states = _gdn_scan_fwd(u_loc, l1, l2, g_last, bsz, lp)
    o, v_new = o.reshape(rows, GW), v_new.reshape(rows, GW)
    mix_g, mix_s = _mix_heads(proj, o, gdn_norm, convs_full, bsz, lp, padf)
    mixin = jnp.concatenate([mix_g, mix_s], axis=1)
    my_out, (l_out,) = _split_wait("gather_w_out_wait", grp_out, _gather_pattern, mixin)
    wout = with_mine(l_out, my_out[0]).reshape(D, D)
    (mix,) = _mm("proj_out", [(mixin, wout)], "nn", tm, 512, D)
    my_ffn, (l_gate, l_up, l_down) = _split_wait("gather_ffn_wait", grp_ffn, _gather_pattern, mix)
    wgate, wup = cols_full(with_mine(l_gate, my_ffn[0])), cols_full(with_mine(l_up, my_ffn[1]))
    wdown = with_mine(l_down, my_ffn[2]).reshape(FF, D)
    h1, u2 = _mid_norms(h0, mix, mix_post_norm, ffn_pre_norm, bsz, lp, tr)
    gate, up, act = _ffn_up(u2, wgate, wup, _div_tile(rows, 528), 1408)
    (ffn,) = _mm("ffn_down", [(act, wdown)], "nn", tm, 512, 1408)

    dy, dffn, d_g4, loss_part = _loss_head(h1, ffn, tgt, ffn_post_norm, bsz, lp, tr, first_real)
    loss = lax.psum(loss_part[0, 0], ("x", "y", "c"))
    (dwdown,) = _mm("dw_down", [(act, dffn)], "tn", 1408, 512, tm)

    def swiglu_bwd(da, g, u):
        return da * u * _dsilu(g), da * _silu(g)

    dgate, dup = _mm("d_act", [(dffn, wdown)], "nt", tm, 1408, D, out_dtypes=(BF16, BF16),
                     epilogue=swiglu_bwd, extras=(gate, up))
    (dwgate,) = _mm("dw_gate", [(u2, dgate)], "tn", D, 1408, tm)
    (dwup,) = _mm("dw_up", [(u2, dup)], "tn", D, 1408, tm)
    (du2,) = _mm("d_u2", [(dgate, wgate), (dup, wup)], "nt", tm, 512, 1408)
    dh1, dmix, d_g3, d_g2 = _mid_norms_bwd(h1, mix, du2, dy, ffn_pre_norm, mix_post_norm, bsz, lp, tr)
    (dwout,) = _mm("dw_out", [(mixin, dmix)], "tn", D, 512, tm)

    def col_blocks(g):
        r = g.shape[0]
        return jnp.transpose(g.reshape(r, 4, 2, -1), (2, 1, 0, 3))

    def row_blocks(g):
        c = g.shape[1]
        return jnp.transpose(g.reshape(4, 2, -1, c), (1, 0, 2, 3))

    core = jnp.reshape(cc, (1,)).astype(jnp.int32)
    chip = jnp.reshape(2 * cx + cy, (1,)).astype(jnp.int32)

    def reduce_start(tag, grads):
        (grp,), tok = _split_start("rs_sibling_start_" + tag, [grads], _sibling_pattern, lambda s: s[1:])
        return grp, tok

    def reduce_mid(tag, grp, nms, after):
        mine, lands = _split_wait("rs_sibling_wait_" + tag, grp, _sibling_pattern, after)
        parts = [_pair_sum("pair_sum_" + nm, g, r, core) for nm, g, r in zip(nms, mine, lands)]
        (grp2,), tok = _split_start("rs_chip_start_" + tag, [parts], _chip_pattern, lambda s: (3,) + s[1:])
        return grp2, tok

    def reduce_end(tag, grp2, after):
        return _split_wait("rs_chip_wait_" + tag, grp2, _chip_pattern, after)

    names_ffn = ["w_down", "w_gate", "w_up", "w_out"]
    rs_ffn, tok = reduce_start("ffn", [row_blocks(dwdown), col_blocks(dwgate), col_blocks(dwup), row_blocks(dwout)])
    (dmixin,) = _mm("d_mixin", [(dmix, wout)], "nt", tm, 512, D, after=tok)
    d_o, dz, dsx, dsb, dsc, d_gn, d_convs = _mix_heads_bwd(proj, o, gdn_norm, convs_full, dmixin, bsz, lp, padf)
    rs_ffn, tok = reduce_mid("ffn", rs_ffn, names_ffn, d_o)
    dv_new, dstates = _gdn_scan_bwd(d_o, l3, w_t, k_dec, g_last + tok[0, 0], bsz, lp)
    dq, dk, dv, dg_b, dbeta_b = _gdn_local_bwd(qkv, gcb, bb, gct, tinv_all, states, dstates, v_new,
                                               dv_new.reshape(rows, GW), d_o, bsz, lp)
    dqkv = jnp.concatenate([dq, dk, dv], axis=1)
    dpqkv, d_convq = _gdn_prep_bwd(proj, convq_full, dqkv, bsz, lp, padf)
    dba, d_scal = _gates_bwd(proj, alog_v, dtb_v, dbeta_b, dg_b, gb, bsz, lp, padf)
    dproj = jnp.concatenate([dpqkv, dz, dba, dsx, dsb, dsc], axis=1)
    (dwin,) = _mm("dw_in", [(u1, dproj)], "tn", D, 768, tm)
    parts_ffn, recv_ffn = reduce_end("ffn", rs_ffn, dwin)
    dwin = jnp.concatenate([dwin[:, :BA0 + 2 * HEADS], dwin[:, SX0:]], axis=1)
    rs_in, tok = reduce_start("in", [col_blocks(dwin)])
    (du1,) = _mm("d_u1", [(dproj, win)], "nt", tm, 512, 768, after=tok)
    rs_in, tok = reduce_mid("in", rs_in, ["w_in"], du1)
    dh0, d_g1 = _pre_norm_bwd(h0, du1, dh1, mix_pre_norm + tok[0:1, 0:1], bsz, lp, tr)
    dh0 = dh0.reshape(bsz, lp, D)
    grad_x = dh0[:, first_real:]
    d_meta = jnp.sum(dh0[:, padf:first_real], axis=0)

    d_alog = d_scal[0:1, HEADS:2 * HEADS]
    d_dtb = d_scal[1:2, HEADS:2 * HEADS]
    slab = _pack_small([d_g1, d_g2, d_g3, d_g4], d_meta, d_convq, d_convs, d_alog, d_dtb, d_gn)
    (slabs,) = _all_gather("gather_small_grads", [slab])
    off_meta = dev * DH
    off_cq = dev * (3 * GW // N_DEV)
    off_cs = dev * (SW // N_DEV)

    def shard_of(s):
        gains, meta, cq, cs, al, db, gn = _unpack_small(s)
        return gains, lax.dynamic_slice_in_dim(meta, off_meta, DH, 1), \
            lax.dynamic_slice_in_dim(cq, off_cq, 3 * GW // N_DEV, 1), \
            lax.dynamic_slice_in_dim(cs, off_cs, SW // N_DEV, 1), al, db, gn

    slabs_local = jax.vmap(lambda s: _pack_small(*shard_of(s)))(slabs)
    w_small = _pack_small([mix_pre_norm, mix_post_norm, ffn_pre_norm, ffn_post_norm], meta_tokens, conv_qkv[0],
                          conv_sc[0], a_log, dt_bias, gdn_norm)
    m_small = _pack_small([m_mix_pre_norm, m_mix_post_norm, m_ffn_pre_norm, m_ffn_post_norm], m_meta_tokens,
                          m_conv_qkv[0], m_conv_sc[0], m_a_log, m_dt_bias, m_gdn_norm)
    v_small = _pack_small([v_mix_pre_norm, v_mix_post_norm, v_ffn_pre_norm, v_ffn_post_norm], v_meta_tokens,
                          v_conv_qkv[0], v_conv_sc[0], v_a_log, v_dt_bias, v_gdn_norm)
    small_out = _adamw("adamw_small", slabs_local, w_small, m_small, v_small)

    big_w = {"w_in": (w_in, m_w_in, v_w_in), "w_out": (w_out, m_w_out, v_w_out), "w_gate": (w_gate, m_w_gate, v_w_gate),
             "w_up": (w_up, m_w_up, v_w_up), "w_down": (w_down, m_w_down, v_w_down)}

    def update(nm, parts, recv):
        w, m, v = big_w[nm]
        return _adamw_reduced("adamw_" + nm, parts, chip, recv, w[0], m[0], v[0])

    big_out = {nm: update(nm, p, r) for nm, p, r in zip(names_ffn, parts_ffn, recv_ffn)}
    parts_in, recv_in = reduce_end("in", rs_in, big_out["w_out"][1])
    big_out["w_in"] = update("w_in", parts_in[0], recv_in[0])
    names = ["w_in", "w_out", "w_gate", "w_up", "w_down"]

    def small_leaves(slab_out):
        gains, meta, cq, cs, al, db, gn = _unpack_small(slab_out)
        return {"meta_tokens": meta[:, :DH], "mix_pre_norm": gains[0], "mix_post_norm": gains[1],
                "ffn_pre_norm": gains[2], "ffn_post_norm": gains[3], "conv_qkv": cq[None, :, :3 * GW // N_DEV],
                "a_log": al, "dt_bias": db, "gdn_norm": gn, "conv_sc": cs[None, :, :SW // N_DEV]}

    order = ["meta_tokens", "mix_pre_norm", "mix_post_norm", "ffn_pre_norm", "ffn_post_norm", "w_in", "conv_qkv",
             "a_log", "dt_bias", "gdn_norm", "conv_sc", "w_out", "w_gate", "w_up", "w_down"]
    outs = [loss, grad_x]
    for kind in range(4):
        leaves = small_leaves(small_out[kind])
        for nm in names:
            leaves[nm] = big_out[nm][kind][None]
        outs += [leaves[nm] for nm in order]
    return tuple(outs)
```

```python
import functools

import jax
import jax.numpy as jnp
from jax import lax
from jax.experimental import pallas as pl
from jax.experimental.pallas import tpu as pltpu

F32 = jnp.float32
BF16 = jnp.bfloat16
HIGHEST = lax.Precision.HIGHEST
MESH = pl.DeviceIdType.MESH

D = 1024
N_META = 16
HEADS = 4
DH = 128
GW = HEADS * DH
SW = D - GW
FF = 2816
CH = 64
KQ = 4
KS = 3
EPS = 1e-6
IN_W = 3 * GW + GW + 2 * HEADS + 3 * SW
Q0, K0, V0, Z0, BA0, SX0, SB0, SC0, PW = 0, 512, 1024, 1536, 2048, 2304, 2816, 3328, 3840
BA_W = SX0 - BA0
N_DEV = 8
VMEM_LIMIT = 56 * 1024 * 1024

ADAM_LR, ADAM_B1, ADAM_B2, ADAM_EPS, ADAM_WD, ADAM_STEP = 0.001, 0.9, 0.999, 1e-08, 0.01, 10


def _cparams(sem):
    return pltpu.CompilerParams(dimension_semantics=sem, vmem_limit_bytes=VMEM_LIMIT)


def _div_tile(n, target, mult=16):
    best = None
    for t in range(mult, min(n, target) + 1, mult):
        if n % t == 0:
            best = t
    assert best is not None, (n, target)
    return best


def _silu(x):
    return x * jax.nn.sigmoid(x)


def _dsilu(x):
    s = jax.nn.sigmoid(x)
    return s * (1.0 + x * (1.0 - s))


def _bdot(a, b, dims):
    return lax.dot_general(a.astype(BF16), b.astype(BF16), (dims, ((), ())), preferred_element_type=F32)


NN = ((1,), (0,))
NT = ((1,), (1,))
TN = ((0,), (0,))


def _fdot(a, b, dims=NN):
    return lax.dot_general(a, b, (dims, ((), ())), precision=HIGHEST, preferred_element_type=F32)


def _mm(name, pairs, mode, tm, tn, tk, out_dtypes=(F32,), epilogue=None, extras=(), after=None):
    a0, b0 = pairs[0]
    if mode == "nn":
        (m, k), n = a0.shape, b0.shape[1]
    elif mode == "nt":
        (m, k), n = a0.shape, b0.shape[0]
    else:
        (k, m), n = a0.shape, b0.shape[1]
    assert m % tm == 0 and n % tn == 0 and k % tk == 0, (name, m, n, k, tm, tn, tk)
    nk = k // tk
    npair = len(pairs)
    nex = len(extras)
    nout = len(out_dtypes)
    ntok = 0 if after is None else 1
    dims = {"nn": NN, "nt": NT, "tn": TN}[mode]

    def body(*refs):
        ab = refs[:2 * npair]
        ex = refs[2 * npair:2 * npair + nex]
        outs = refs[2 * npair + nex + ntok:2 * npair + nex + ntok + nout]
        acc_ref = refs[-1]
        kk = pl.program_id(2)

        part = None
        for p in range(npair):
            d = lax.dot_general(ab[2 * p][...], ab[2 * p + 1][...], (dims, ((), ())), preferred_element_type=F32)
            part = d if part is None else part + d

        def finish(acc):
            res = epilogue(acc, *[e[...] for e in ex]) if epilogue is not None else (acc,)
            for o, r in zip(outs, res):
                o[...] = r.astype(o.dtype)

        if nk == 1:
            finish(part)
        else:
            @pl.when(kk == 0)
            def _():
                acc_ref[...] = part

            @pl.when(kk > 0)
            def _():
                acc_ref[...] += part

            @pl.when(kk == nk - 1)
            def _():
                finish(acc_ref[...])

    if mode == "nn":
        a_spec = pl.BlockSpec((tm, tk), lambda i, j, q: (i, q))
        b_spec = pl.BlockSpec((tk, tn), lambda i, j, q: (q, j))
    elif mode == "nt":
        a_spec = pl.BlockSpec((tm, tk), lambda i, j, q: (i, q))
        b_spec = pl.BlockSpec((tn, tk), lambda i, j, q: (j, q))
    else:
        a_spec = pl.BlockSpec((tk, tm), lambda i, j, q: (q, i))
        b_spec = pl.BlockSpec((tk, tn), lambda i, j, q: (q, j))
    o_spec = pl.BlockSpec((tm, tn), lambda i, j, q: (i, j))
    flat = [t for pr in pairs for t in pr]
    res = pl.pallas_call(
        body, name=name,
        grid=(m // tm, n // tn, nk),
        in_specs=[a_spec, b_spec] * npair + [o_spec] * nex + [pl.BlockSpec((8, DH), lambda i, j, q: (0, 0))] * ntok,
        out_specs=[o_spec] * nout,
        out_shape=[jax.ShapeDtypeStruct((m, n), dt) for dt in out_dtypes],
        scratch_shapes=[pltpu.VMEM((tm, tn), F32)],
        compiler_params=_cparams(("parallel", "parallel", "arbitrary")),
    )(*flat, *extras, *([] if after is None else [after]))
    return res


def _ffn_up(u2, wg, wu, tm, tn):
    m, k = u2.shape
    n = wg.shape[1]

    def body(a_ref, g_ref, u_ref, gate_ref, up_ref, act_ref):
        a = a_ref[...]
        g = jnp.dot(a, g_ref[...], preferred_element_type=F32)
        u = jnp.dot(a, u_ref[...], preferred_element_type=F32)
        gate_ref[...] = g.astype(BF16)
        up_ref[...] = u.astype(BF16)
        act_ref[...] = (_silu(g) * u).astype(BF16)

    o_spec = pl.BlockSpec((tm, tn), lambda i, j: (i, j))
    w_spec = pl.BlockSpec((k, tn), lambda i, j: (0, j))
    return pl.pallas_call(
        body, name="ffn_up", grid=(m // tm, n // tn),
        in_specs=[pl.BlockSpec((tm, k), lambda i, j: (i, 0)), w_spec, w_spec],
        out_specs=[o_spec, o_spec, o_spec],
        out_shape=[jax.ShapeDtypeStruct((m, n), BF16)] * 3,
        compiler_params=_cparams(("parallel", "parallel")),
    )(u2, wg, wu)


def _rms(x, gain):
    r = lax.rsqrt(jnp.mean(x * x, axis=-1, keepdims=True) + EPS)
    return x * r * gain, r


def _rms_bwd(x, gain, dy):
    r = lax.rsqrt(jnp.mean(x * x, axis=-1, keepdims=True) + EPS)
    dyw = dy * gain
    dx = r * dyw - x * (r * r * r) * jnp.mean(dyw * x, axis=-1, keepdims=True)
    return dx, dy * x * r


def _row_specs(tr, width, nb):
    return pl.BlockSpec((tr, width), lambda b, j: (b * nb + j, 0))


def _vec_spec(width):
    return pl.BlockSpec((1, width), lambda b, j: (0, 0))


def _first_step(b, j):
    return jnp.logical_and(b == 0, j == 0)


def _pre_norm(h0, gain, bsz, lp, tr):
    nb = lp // tr

    def body(h_ref, g_ref, u_ref):
        u_ref[...] = _rms(h_ref[...], g_ref[...])[0].astype(BF16)

    return pl.pallas_call(
        body, name="mix_pre_norm", grid=(bsz, nb),
        in_specs=[_row_specs(tr, D, nb), _vec_spec(D)],
        out_specs=_row_specs(tr, D, nb),
        out_shape=jax.ShapeDtypeStruct(h0.shape, BF16),
        compiler_params=_cparams(("parallel", "parallel")),
    )(h0, gain)


def _mid_norms(h0, mix, g_post, g_pre, bsz, lp, tr):
    nb = lp // tr

    def body(h_ref, m_ref, gp_ref, gq_ref, h1_ref, u2_ref):
        h1 = h_ref[...] + _rms(m_ref[...], gp_ref[...])[0]
        h1_ref[...] = h1
        u2_ref[...] = _rms(h1, gq_ref[...])[0].astype(BF16)

    rs = _row_specs(tr, D, nb)
    return pl.pallas_call(
        body, name="mid_norms", grid=(bsz, nb),
        in_specs=[rs, rs, _vec_spec(D), _vec_spec(D)],
        out_specs=[rs, rs],
        out_shape=[jax.ShapeDtypeStruct(h0.shape, F32), jax.ShapeDtypeStruct(h0.shape, BF16)],
        compiler_params=_cparams(("parallel", "parallel")),
    )(h0, mix, g_post, g_pre)


def _loss_head(h1, ffn, tgt, g_post, bsz, lp, tr, first_real):
    nb = lp // tr

    def body(h_ref, f_ref, t_ref, g_ref, dy_ref, dffn_ref, dg_ref, loss_ref):
        b, j = pl.program_id(0), pl.program_id(1)
        f = f_ref[...]
        g = g_ref[...]
        y = h_ref[...] + _rms(f, g)[0]
        t = j * tr + lax.broadcasted_iota(jnp.int32, (tr, 1), 0)
        err = jnp.where(t >= first_real, y - t_ref[...], 0.0)
        dy = err * (1.0 / D)
        dy_ref[...] = dy
        dx, dgc = _rms_bwd(f, g, dy)
        dffn_ref[...] = dx.astype(BF16)
        lsum = jnp.sum(jnp.sum(err * err, axis=0, keepdims=True), axis=1, keepdims=True) * (0.5 / D)

        @pl.when(_first_step(b, j))
        def _():
            dg_ref[...] = jnp.zeros_like(dg_ref)
            loss_ref[...] = jnp.zeros_like(loss_ref)

        dg_ref[...] += jnp.sum(dgc, axis=0, keepdims=True)
        loss_ref[...] += jnp.broadcast_to(lsum, loss_ref.shape)

    rs = _row_specs(tr, D, nb)
    return pl.pallas_call(
        body, name="loss_head", grid=(bsz, nb),
        in_specs=[rs, rs, rs, _vec_spec(D)],
        out_specs=[rs, rs, _vec_spec(D), _vec_spec(DH)],
        out_shape=[jax.ShapeDtypeStruct(h1.shape, F32), jax.ShapeDtypeStruct(h1.shape, BF16),
                   jax.ShapeDtypeStruct((1, D), F32), jax.ShapeDtypeStruct((1, DH), F32)],
        compiler_params=_cparams(("arbitrary", "arbitrary")),
    )(h1, ffn, tgt, g_post)


def _mid_norms_bwd(h1, mix, du2, dy, g_pre, g_post, bsz, lp, tr):
    nb = lp // tr

    def body(h_ref, m_ref, du_ref, dy_ref, gq_ref, gp_ref, dh1_ref, dmix_ref, dgq_ref, dgp_ref):
        b, j = pl.program_id(0), pl.program_id(1)
        dx, dgq = _rms_bwd(h_ref[...], gq_ref[...], du_ref[...])
        dh1 = dy_ref[...] + dx
        dh1_ref[...] = dh1
        dm, dgp = _rms_bwd(m_ref[...], gp_ref[...], dh1)
        dmix_ref[...] = dm.astype(BF16)

        @pl.when(_first_step(b, j))
        def _():
            dgq_ref[...] = jnp.zeros_like(dgq_ref)
            dgp_ref[...] = jnp.zeros_like(dgp_ref)

        dgq_ref[...] += jnp.sum(dgq, axis=0, keepdims=True)
        dgp_ref[...] += jnp.sum(dgp, axis=0, keepdims=True)

    rs = _row_specs(tr, D, nb)
    return pl.pallas_call(
        body, name="mid_norms_bwd", grid=(bsz, nb),
        in_specs=[rs, rs, rs, rs, _vec_spec(D), _vec_spec(D)],
        out_specs=[rs, rs, _vec_spec(D), _vec_spec(D)],
        out_shape=[jax.ShapeDtypeStruct(h1.shape, F32), jax.ShapeDtypeStruct(h1.shape, BF16),
                   jax.ShapeDtypeStruct((1, D), F32), jax.ShapeDtypeStruct((1, D), F32)],
        compiler_params=_cparams(("arbitrary", "arbitrary")),
    )(h1, mix, du2, dy, g_pre, g_post)


def _pre_norm_bwd(h0, du1, dh1, gain, bsz, lp, tr):
    nb = lp // tr

    def body(h_ref, du_ref, dh1_ref, g_ref, dh0_ref, dg_ref):
        b, j = pl.program_id(0), pl.program_id(1)
        dx, dgc = _rms_bwd(h_ref[...], g_ref[...], du_ref[...])
        dh0_ref[...] = dh1_ref[...] + dx

        @pl.when(_first_step(b, j))
        def _():
            dg_ref[...] = jnp.zeros_like(dg_ref)

        dg_ref[...] += jnp.sum(dgc, axis=0, keepdims=True)

    rs = _row_specs(tr, D, nb)
    return pl.pallas_call(
        body, name="pre_norm_bwd", grid=(bsz, nb),
        in_specs=[rs, rs, rs, _vec_spec(D)],
        out_specs=[rs, _vec_spec(D)],
        out_shape=[jax.ShapeDtypeStruct(h0.shape, F32), jax.ShapeDtypeStruct((1, D), F32)],
        compiler_params=_cparams(("arbitrary", "arbitrary")),
    )(h0, du1, dh1, gain)


def _shift_down(x, s, lp):
    return x if s == 0 else pltpu.roll(x, s, axis=0)


def _shift_up(x, s, lp):
    return x if s == 0 else pltpu.roll(x, lp - s, axis=0)


def _conv_fwd(x, w, taps, lp):
    y = None
    for i in range(taps):
        term = _shift_down(x, taps - 1 - i, lp) * w[i:i + 1, :]
        y = term if y is None else y + term
    return y


def _conv_bwd(x, w, dy, taps, lp):
    dx = None
    dws = []
    for i in range(taps):
        s = taps - 1 - i
        term = _shift_up(dy, s, lp) * w[i:i + 1, :]
        dx = term if dx is None else dx + term
        dws.append(jnp.sum(dy * _shift_down(x, s, lp), axis=0, keepdims=True))
    return dx, jnp.concatenate(dws, axis=0)


def _slab(lp, col0):
    return pl.BlockSpec((lp, DH), lambda b, s: (b, col0 // DH + s))


def _qkv_act(c, s):
    y = _silu(c)
    r = lax.rsqrt(jnp.sum(y * y, axis=-1, keepdims=True) + EPS)
    scale = jnp.where(s < HEADS, DH ** -0.5, 1.0)
    return y, r, scale


def _gdn_prep(proj, conv_qkv, bsz, lp, padf):
    def body(p_ref, w_ref, o_ref):
        s = pl.program_id(1)
        c = _conv_fwd(p_ref[...].astype(F32), w_ref[...], KQ, lp)
        y, r, scale = _qkv_act(c, s)
        out = jnp.where(s < 2 * HEADS, y * (r * scale), y)
        t = lax.broadcasted_iota(jnp.int32, (lp, 1), 0)
        o_ref[...] = jnp.where(t >= padf, out, 0.0)

    return pl.pallas_call(
        body, name="gdn_prep", grid=(bsz, 3 * HEADS),
        in_specs=[_slab(lp, Q0), pl.BlockSpec((KQ, DH), lambda b, s: (0, s))],
        out_specs=_slab(lp, 0),
        out_shape=jax.ShapeDtypeStruct((bsz * lp, 3 * GW), F32),
        compiler_params=_cparams(("parallel", "parallel")),
    )(proj, conv_qkv)


def _gdn_prep_bwd(proj, conv_qkv, dqkv, bsz, lp, padf):
    def body(p_ref, w_ref, d_ref, dp_ref, dw_ref):
        s, b = pl.program_id(0), pl.program_id(1)
        x = p_ref[...].astype(F32)
        w = w_ref[...]
        c = _conv_fwd(x, w, KQ, lp)
        y, r, scale = _qkv_act(c, s)
        dn = d_ref[...] * scale
        n = y * r
        dy_norm = r * (dn - n * jnp.sum(dn * n, axis=-1, keepdims=True))
        dyy = jnp.where(s < 2 * HEADS, dy_norm, dn)
        t = lax.broadcasted_iota(jnp.int32, (lp, 1), 0)
        dc = jnp.where(t >= padf, dyy * _dsilu(c), 0.0)
        dx, dw = _conv_bwd(x, w, dc, KQ, lp)
        dp_ref[...] = jnp.where(t >= padf, dx, 0.0).astype(BF16)

        @pl.when(b == 0)
        def _():
            dw_ref[...] = jnp.zeros_like(dw_ref)

        dw_ref[...] += dw

    sl = lambda col0: pl.BlockSpec((lp, DH), lambda s, b: (b, col0 // DH + s))
    return pl.pallas_call(
        body, name="gdn_prep_bwd", grid=(3 * HEADS, bsz),
        in_specs=[sl(Q0), pl.BlockSpec((KQ, DH), lambda s, b: (0, s)), sl(0)],
        out_specs=[sl(0), pl.BlockSpec((KQ, DH), lambda s, b: (0, s))],
        out_shape=[jax.ShapeDtypeStruct((bsz * lp, 3 * GW), BF16), jax.ShapeDtypeStruct((KQ, 3 * GW), F32)],
        compiler_params=_cparams(("parallel", "arbitrary")),
    )(proj, conv_qkv, dqkv)


def _softplus(x):
    e = jnp.exp(-jnp.abs(x))
    one_e = 1.0 + e
    l1p = jnp.where(one_e == 1.0, e, jnp.log(one_e) * (e / (one_e - 1.0)))
    return jnp.maximum(x, 0.0) + l1p


def _tri(n, kind):
    i = lax.broadcasted_iota(jnp.int32, (n, n), 0)
    j = lax.broadcasted_iota(jnp.int32, (n, n), 1)
    return {"incl": i >= j, "strict": i > j, "upper": i <= j}[kind]


def _gates(proj, alog_v, dtb_v, bsz, lp, padf):
    nchunk = lp // CH

    def body(p_ref, al_ref, dt_ref, gb_ref, cs_ref):
        x = p_ref[...].astype(F32)
        lane = lax.broadcasted_iota(jnp.int32, (lp, DH), 1)
        t = lax.broadcasted_iota(jnp.int32, (lp, DH), 0)
        beta = jax.nn.sigmoid(x)
        g = -jnp.exp(al_ref[...]) * _softplus(x + dt_ref[...])
        gb = jnp.where(lane < HEADS, beta, jnp.where(lane < 2 * HEADS, g, 0.0))
        gb_ref[...] = jnp.where(t >= padf, gb, 0.0)
        ltri = _tri(CH, "incl").astype(F32)

        def step(n, carry):
            r0 = pl.multiple_of(n * CH, CH)
            cs_ref[pl.ds(r0, CH), :] = _fdot(ltri, gb_ref[pl.ds(r0, CH), :])
            return carry

        lax.fori_loop(0, nchunk, step, 0)

    blk = pl.BlockSpec((lp, DH), lambda b: (b, BA0 // DH))
    vec = pl.BlockSpec((1, DH), lambda b: (0, 0))
    out = pl.BlockSpec((lp, DH), lambda b: (b, 0))
    return pl.pallas_call(
        body, name="gates", grid=(bsz,),
        in_specs=[blk, vec, vec], out_specs=[out, out],
        out_shape=[jax.ShapeDtypeStruct((bsz * lp, DH), F32)] * 2,
        compiler_params=_cparams(("parallel",)),
    )(proj, alog_v, dtb_v)


def _gates_bwd(proj, alog_v, dtb_v, dbeta_b, dg_b, gb, bsz, lp, padf):
    def body(p_ref, al_ref, dt_ref, db_ref, dg_ref, gb_ref, dl_ref, dv_ref):
        b = pl.program_id(0)
        x = p_ref[...].astype(F32)
        lane = lax.broadcasted_iota(jnp.int32, (lp, DH), 1)
        t = lax.broadcasted_iota(jnp.int32, (lp, DH), 0)
        dbeta = jnp.zeros((lp, DH), F32)
        dg = jnp.zeros((lp, DH), F32)
        for h in range(HEADS):
            dbeta = jnp.where(lane == h, db_ref[h], dbeta)
            dg = jnp.where(lane == HEADS + h, dg_ref[h], dg)
        real = t >= padf
        dbeta = jnp.where(real, dbeta, 0.0)
        dg = jnp.where(real, dg, 0.0)
        sb = jax.nn.sigmoid(x)
        neg_ea = -jnp.exp(al_ref[...])
        dsp = dg * neg_ea * jax.nn.sigmoid(x + dt_ref[...])
        dl = dbeta * sb * (1.0 - sb) + dsp
        dl_ref[...] = jnp.concatenate([dl, jnp.zeros((lp, BA_W - DH), F32)], axis=1).astype(BF16)
        dalog = jnp.sum(dg * gb_ref[...], axis=0, keepdims=True)
        ddt = jnp.sum(dsp, axis=0, keepdims=True)

        @pl.when(b == 0)
        def _():
            dv_ref[...] = jnp.zeros_like(dv_ref)

        dv_ref[...] += jnp.concatenate([dalog, ddt], axis=0)

    blk = pl.BlockSpec((lp, DH), lambda b: (b, BA0 // DH))
    vec = pl.BlockSpec((1, DH), lambda b: (0, 0))
    hb = pl.BlockSpec((HEADS, lp, DH), lambda b: (b, 0, 0))
    return pl.pallas_call(
        body, name="gates_bwd", grid=(bsz,),
        in_specs=[blk, vec, vec, hb, hb, pl.BlockSpec((lp, DH), lambda b: (b, 0))],
        out_specs=[pl.BlockSpec((lp, BA_W), lambda b: (b, 0)), pl.BlockSpec((2, DH), lambda b: (0, 0))],
        out_shape=[jax.ShapeDtypeStruct((bsz * lp, BA_W), BF16), jax.ShapeDtypeStruct((2, DH), F32)],
        compiler_params=_cparams(("arbitrary",)),
    )(proj, alog_v, dtb_v, dbeta_b, dg_b, gb)


def _mix_heads(proj, o, gdn_norm, conv_sc, bsz, lp, padf):
    def body(o_ref, z_ref, gn_ref, sx_ref, sb_ref, sc_ref, w_ref, og_ref, os_ref):
        og_ref[...] = (_rms(o_ref[...], gn_ref[...])[0] * _silu(z_ref[...].astype(F32))).astype(BF16)
        conv = _conv_fwd(sc_ref[...].astype(F32) * sx_ref[...].astype(F32), w_ref[...], KS, lp)
        t = lax.broadcasted_iota(jnp.int32, (lp, 1), 0)
        os_ref[...] = jnp.where(t >= padf, sb_ref[...].astype(F32) * conv, 0.0).astype(BF16)

    half = _slab(lp, 0)
    return pl.pallas_call(
        body, name="mix_heads", grid=(bsz, HEADS),
        in_specs=[half, _slab(lp, Z0), pl.BlockSpec((1, DH), lambda b, s: (0, 0)),
                  _slab(lp, SX0), _slab(lp, SB0), _slab(lp, SC0), pl.BlockSpec((KS, DH), lambda b, s: (0, s))],
        out_specs=[half, half],
        out_shape=[jax.ShapeDtypeStruct((bsz * lp, GW), BF16)] * 2,
        compiler_params=_cparams(("parallel", "parallel")),
    )(o, proj, gdn_norm, proj, proj, proj, conv_sc)


def _mix_heads_bwd(proj, o, gdn_norm, conv_sc, dmixin, bsz, lp, padf):
    def body(o_ref, z_ref, gn_ref, sx_ref, sb_ref, sc_ref, w_ref, dg_ref, ds_ref,
             do_ref, dz_ref, dsx_ref, dsb_ref, dsc_ref, dgn_ref, dw_ref):
        s, b = pl.program_id(0), pl.program_id(1)
        t = lax.broadcasted_iota(jnp.int32, (lp, 1), 0)
        real = t >= padf
        o, z, gn, d = o_ref[...], z_ref[...].astype(F32), gn_ref[...], dg_ref[...]
        sz = _silu(z)
        on = _rms(o, gn)[0]
        dz_ref[...] = (d * on * _dsilu(z)).astype(BF16)
        dox, dgn = _rms_bwd(o, gn, d * sz)
        do_ref[...] = dox
        sx, sb, sc = sx_ref[...].astype(F32), sb_ref[...].astype(F32), sc_ref[...].astype(F32)
        w, e = w_ref[...], ds_ref[...]
        e = jnp.where(real, e, 0.0)
        xin = sc * sx
        conv = _conv_fwd(xin, w, KS, lp)
        dsb_ref[...] = (e * conv).astype(BF16)
        dxin, dw = _conv_bwd(xin, w, e * sb, KS, lp)
        dxin = jnp.where(real, dxin, 0.0)
        dsx_ref[...] = (dxin * sc).astype(BF16)
        dsc_ref[...] = (dxin * sx).astype(BF16)

        @pl.when(jnp.logical_and(s == 0, b == 0))
        def _():
            dgn_ref[...] = jnp.zeros_like(dgn_ref)

        @pl.when(b == 0)
        def _():
            dw_ref[...] = jnp.zeros_like(dw_ref)

        dgn_ref[...] += jnp.sum(dgn, axis=0, keepdims=True)
        dw_ref[...] += dw

    sl = lambda col0: pl.BlockSpec((lp, DH), lambda s, b: (b, col0 // DH + s))
    half = sl(0)
    return pl.pallas_call(
        body, name="mix_heads_bwd", grid=(HEADS, bsz),
        in_specs=[half, sl(Z0), pl.BlockSpec((1, DH), lambda s, b: (0, 0)), sl(SX0), sl(SB0), sl(SC0),
                  pl.BlockSpec((KS, DH), lambda s, b: (0, s)), sl(0), sl(GW)],
        out_specs=[half] * 5 + [pl.BlockSpec((1, DH), lambda s, b: (0, 0)), pl.BlockSpec((KS, DH), lambda s, b: (0, s))],
        out_shape=[jax.ShapeDtypeStruct((bsz * lp, GW), F32)] + [jax.ShapeDtypeStruct((bsz * lp, GW), BF16)] * 4
        + [jax.ShapeDtypeStruct((1, DH), F32), jax.ShapeDtypeStruct((KS, SW), F32)],
        compiler_params=_cparams(("arbitrary", "arbitrary")),
    )(o, proj, gdn_norm, proj, proj, proj, conv_sc, dmixin, dmixin)


def _split_dot(a, b, dims=NN):
    ah = a.astype(BF16)
    al = (a - ah.astype(F32)).astype(BF16)
    bh = b.astype(BF16)
    bl = (b - bh.astype(F32)).astype(BF16)

    def d(x, y):
        return lax.dot_general(x, y, (dims, ((), ())), preferred_element_type=F32)

    return d(ah, bh) + (d(ah, bl) + d(al, bh))


def _eye(n):
    return (lax.broadcasted_iota(jnp.int32, (n, n), 0) == lax.broadcasted_iota(jnp.int32, (n, n), 1)).astype(F32)


def _each(f, *lists):
    return [f(*xs) for xs in zip(*lists)]


def _inv_unit_lower(a_list):
    eye = _eye(CH)
    p = _each(lambda a: -a, a_list)
    t = _each(lambda x: eye + x, p)
    p = _each(lambda x: _split_dot(x, x), p)
    for level in range(5):
        if level < 4:
            prod = _each(lambda tt, pp: _split_dot(jnp.concatenate([tt, pp], axis=0), pp), t, p)
            t = _each(lambda tt, pr: tt + pr[:CH], t, prod)
            p = _each(lambda pr: pr[CH:], prod)
        else:
            t = _each(lambda tt, pp: tt + _split_dot(tt, pp), t, p)
    return t


def _chunk_gates(gc_ref, be_ref, gt_ref, g):
    rows = pl.ds(g * CH, CH)
    gc = gc_ref[0, rows, :]
    be = be_ref[0, rows, :]
    gcl = gc_ref[0, pl.ds(g * CH + CH - 1, 1), :]
    gt = gt_ref[0, g]
    incl, upper = _tri(CH, "incl"), _tri(CH, "upper")
    dec = jnp.where(incl, jnp.exp(jnp.where(incl, gc[:, :CH] - gt, 0.0)), 0.0)
    dec_t = jnp.where(upper, jnp.exp(jnp.where(upper, gt - gc[:, :CH], 0.0)), 0.0)
    return rows, be, dec, dec_t, jnp.exp(gc), jnp.exp(gcl - gc), jnp.exp(gcl)


def _chunk_group(nchunk):
    return 3 if nchunk % 3 == 0 else 1


def _local_group(nchunk):
    return 11 if nchunk % 11 == 0 else _chunk_group(nchunk)


def _gdn_local_specs(lp, grp):
    ngrp = lp // (grp * CH)
    head = lambda col0: pl.BlockSpec((grp * CH, DH), lambda i, j: ((i // HEADS) * ngrp + j, col0 // DH + i % HEADS))
    lane_b = pl.BlockSpec((1, grp * CH, DH), lambda i, j: (i, j, 0))
    per_chunk = lambda r, c: pl.BlockSpec((1, grp, r, c), lambda i, j: (i, j, 0, 0))
    return ngrp, head, lane_b, per_chunk


def _gdn_local_fwd(qkv, gcb, bb, gct, bsz, lp):
    nchunk = lp // CH
    grp = _local_group(nchunk)
    ngrp, head, lane_b, per_chunk = _gdn_local_specs(lp, grp)
    bh = bsz * HEADS

    def body(q_ref, k_ref, v_ref, gc_ref, be_ref, gt_ref,
             u_ref, l1_ref, l2_ref, l3_ref, wt_ref, kd_ref, ti_ref, gl_ref):
        incl, strict, upper = _tri(CH, "incl"), _tri(CH, "strict"), _tri(CH, "upper")
        eye = _eye(CH)
        rows, be, dec, dec_t, eg, ekd, gl = zip(*[_chunk_gates(gc_ref, be_ref, gt_ref, g) for g in range(grp)])
        q = [q_ref[r, :] for r in rows]
        k = [k_ref[r, :] for r in rows]
        v = [v_ref[r, :] for r in rows]
        kb = _each(lambda x, y: x * y, k, be)
        a = _each(lambda x, y, d: jnp.where(strict, _bdot(x, y, NT) * d, 0.0), kb, k, dec)
        tinv = _inv_unit_lower(a)
        u = _each(lambda t, x, y: _bdot(t, x * y, NN), tinv, v, be)
        w = _each(lambda t, x, y: _bdot(t, x * y, NN), tinv, kb, eg)
        qk = _each(lambda x, y, d: jnp.where(incl, _bdot(x, y, NT) * d, 0.0), q, k, dec)
        qk_t = _each(lambda x, y, d: jnp.where(upper, _bdot(y, x, NT) * d, 0.0), q, k, dec_t)
        qd = _each(lambda x, y: x * y, q, eg)
        kd = _each(lambda x, y: x * y, k, ekd)
        kd_t = _each(lambda x: _bdot(x, eye, TN), kd)
        qd_t = _each(lambda x: _bdot(x, eye, TN), qd)
        w_t = _each(lambda x: _bdot(x, eye, TN), w)
        for g in range(grp):
            u_ref[rows[g], :] = u[g]
            l1_ref[0, g] = jnp.concatenate([w[g], qd[g]], axis=0).astype(BF16)
            l2_ref[0, g] = jnp.concatenate([qk[g], kd_t[g]], axis=0).astype(BF16)
            l3_ref[0, g] = jnp.concatenate([qk_t[g], qd_t[g]], axis=0).astype(BF16)
            wt_ref[0, g] = w_t[g].astype(BF16)
            kd_ref[0, g] = kd[g].astype(BF16)
            ti_ref[0, g] = tinv[g]
            gl_ref[0, g] = gl[g]

    r = bsz * lp
    shapes = [((r, GW), F32), ((bh, nchunk, 2 * CH, DH), BF16), ((bh, nchunk, 3 * CH, CH), BF16),
              ((bh, nchunk, 3 * CH, CH), BF16), ((bh, nchunk, DH, CH), BF16), ((bh, nchunk, CH, DH), BF16),
              ((bh, nchunk, CH, CH), F32), ((bh, nchunk, 1, DH), F32)]
    return pl.pallas_call(
        body, name="gdn_local_fwd", grid=(bh, ngrp),
        in_specs=[head(0), head(GW), head(2 * GW), lane_b, lane_b, per_chunk(1, CH)],
        out_specs=[head(0), per_chunk(2 * CH, DH), per_chunk(3 * CH, CH), per_chunk(3 * CH, CH), per_chunk(DH, CH),
                   per_chunk(CH, DH), per_chunk(CH, CH), per_chunk(1, DH)],
        out_shape=[jax.ShapeDtypeStruct(s, dt) for s, dt in shapes],
        compiler_params=_cparams(("parallel", "parallel")),
    )(qkv, qkv, qkv, gcb, bb, gct)


def _scan_specs(bsz, lp, grp, order):
    ngrp = lp // (grp * CH)
    bh = bsz * HEADS
    seq = lambda: pl.BlockSpec((bsz, grp * CH, GW), lambda s: (0, order(s, ngrp), 0))
    per_chunk = lambda r, c: pl.BlockSpec((bh, grp, r, c), lambda s: (0, order(s, ngrp), 0, 0))
    return ngrp, bh, seq, per_chunk


def _gdn_scan_fwd(u, l1, l2, gl, bsz, lp):
    nchunk = lp // CH
    grp = _chunk_group(nchunk)
    ngrp, bh, seq, per_chunk = _scan_specs(bsz, lp, grp, lambda s, n: s)

    def body(u_ref, l1_ref, l2_ref, gl_ref, o_ref, vn_ref, st_ref, s_scr):
        @pl.when(pl.program_id(0) == 0)
        def _():
            s_scr[...] = jnp.zeros_like(s_scr)

        for c in range(grp):
            rows = slice(c * CH, (c + 1) * CH)
            for i in range(bh):
                b, h = divmod(i, HEADS)
                cols = slice(h * DH, (h + 1) * DH)
                s = s_scr[i]
                st_ref[i, c] = s
                x = jnp.dot(l1_ref[i, c], s.astype(BF16), preferred_element_type=F32)
                vnew = u_ref[b, rows, cols] - x[:CH]
                y = jnp.dot(l2_ref[i, c], vnew.astype(BF16), preferred_element_type=F32)
                o_ref[b, rows, cols] = x[CH:] + y[:CH]
                vn_ref[b, rows, cols] = vnew.astype(BF16)
                s_scr[i] = s * gl_ref[i, c] + y[CH:]

    return pl.pallas_call(
        body, name="gdn_scan_fwd", grid=(ngrp,),
        in_specs=[seq(), per_chunk(2 * CH, DH), per_chunk(3 * CH, CH), per_chunk(1, DH)],
        out_specs=[seq(), seq(), per_chunk(DH, DH)],
        out_shape=[jax.ShapeDtypeStruct((bsz, lp, GW), F32), jax.ShapeDtypeStruct((bsz, lp, GW), BF16),
                   jax.ShapeDtypeStruct((bh, nchunk, DH, DH), F32)],
        scratch_shapes=[pltpu.VMEM((bh, DH, DH), F32)],
        compiler_params=_cparams(("arbitrary",)),
    )(u.reshape(bsz, lp, GW), l1, l2, gl)


def _gdn_scan_bwd(d_o, l3, wt, kd, gl, bsz, lp):
    nchunk = lp // CH
    grp = _chunk_group(nchunk)
    ngrp, bh, seq, per_chunk = _scan_specs(bsz, lp, grp, lambda s, n: n - 1 - s)

    def body(do_ref, l3_ref, wt_ref, kd_ref, gl_ref, dvn_ref, dst_ref, ds_scr):
        @pl.when(pl.program_id(0) == 0)
        def _():
            ds_scr[...] = jnp.zeros_like(ds_scr)

        for c in reversed(range(grp)):
            rows = slice(c * CH, (c + 1) * CH)
            for i in range(bh):
                b, h = divmod(i, HEADS)
                cols = slice(h * DH, (h + 1) * DH)
                ds = ds_scr[i]
                dst_ref[i, c] = ds
                p = jnp.dot(l3_ref[i, c], do_ref[b, rows, cols].astype(BF16), preferred_element_type=F32)
                dvn = p[:CH] + jnp.dot(kd_ref[i, c], ds.astype(BF16), preferred_element_type=F32)
                dvn_ref[b, rows, cols] = dvn.astype(BF16)
                ds_scr[i] = ds * gl_ref[i, c] + p[CH:] - jnp.dot(wt_ref[i, c], dvn.astype(BF16),
                                                                 preferred_element_type=F32)

    return pl.pallas_call(
        body, name="gdn_scan_bwd", grid=(ngrp,),
        in_specs=[seq(), per_chunk(3 * CH, CH), per_chunk(DH, CH), per_chunk(CH, DH), per_chunk(1, DH)],
        out_specs=[seq(), per_chunk(DH, DH)],
        out_shape=[jax.ShapeDtypeStruct((bsz, lp, GW), BF16), jax.ShapeDtypeStruct((bh, nchunk, DH, DH), F32)],
        scratch_shapes=[pltpu.VMEM((bh, DH, DH), F32)],
        compiler_params=_cparams(("arbitrary",)),
    )(d_o.reshape(bsz, lp, GW), l3, wt, kd, gl)


def _gdn_local_bwd(qkv, gcb, bb, gct, tinv_all, states, dstates, vnew, dvnew, d_o, bsz, lp):
    nchunk = lp // CH
    grp = _local_group(nchunk)
    ngrp, head, lane_b, per_chunk = _gdn_local_specs(lp, grp)
    bh = bsz * HEADS

    def body(q_ref, k_ref, v_ref, gc_ref, be_ref, gt_ref, ti_ref, st_ref, dst_ref, vn_ref, dvn_ref, do_ref,
             dq_ref, dk_ref, dv_ref, dg_ref, db_ref):
        incl, strict = _tri(CH, "incl"), _tri(CH, "strict")
        upper = _tri(CH, "upper").astype(F32)
        ones = jnp.ones((CH, DH), F32)
        last = lax.broadcasted_iota(jnp.int32, (CH, 1), 0) == CH - 1

        def rsum(x):
            return jnp.sum(x, axis=-1, keepdims=True)

        def mul(xs, ys):
            return _each(lambda x, y: x * y, xs, ys)

        rows, be, dec, _, eg, ekd, gl = zip(*[_chunk_gates(gc_ref, be_ref, gt_ref, g) for g in range(grp)])
        q = [q_ref[r, :] for r in rows]
        k = [k_ref[r, :] for r in rows]
        v = [v_ref[r, :] for r in rows]
        vnew = [vn_ref[r, :] for r in rows]
        dvn = [dvn_ref[r, :] for r in rows]
        do = [do_ref[r, :] for r in rows]
        tinv = [ti_ref[0, g] for g in range(grp)]
        s = [st_ref[0, g] for g in range(grp)]
        ds = [dst_ref[0, g] for g in range(grp)]
        kb, vb, qd, kd = mul(k, be), mul(v, be), mul(q, eg), mul(k, ekd)
        kbg = mul(kb, eg)
        a = _each(lambda x, y, d: jnp.where(strict, _bdot(x, y, NT) * d, 0.0), kb, k, dec)
        qk = _each(lambda x, y, d: jnp.where(incl, _bdot(x, y, NT) * d, 0.0), q, k, dec)
        dqk = _each(lambda x, y: jnp.where(incl, _bdot(x, y, NT), 0.0), do, vnew)
        dqd = _each(lambda x, y: _bdot(x, y, NT), do, s)
        dkd = _each(lambda x, y: _bdot(x, y, NT), vnew, ds)
        dgl = _each(lambda x, y: jnp.sum(rsum(x * y), axis=0, keepdims=True), s, ds)
        dw = _each(lambda x, y: -_bdot(x, y, NT), dvn, s)
        dvb = _each(lambda t, x: _bdot(t, x, TN), tinv, dvn)
        dkbg = _each(lambda t, x: _bdot(t, x, TN), tinv, dw)
        dt = _each(lambda x, y, z, w: _bdot(x, y, NT) + _bdot(z, w, NT), dvn, vb, dw, kbg)
        tdt = _each(lambda t, x: _split_dot(t, x, TN), tinv, dt)
        da = _each(lambda x, t: jnp.where(strict, -_split_dot(x, t, NT), 0.0), tdt, tinv)
        dp, dr = mul(da, dec), mul(dqk, dec)
        dkb = _each(lambda x, y, z, e: _bdot(x, y, NN) + z * e, dp, k, dkbg, eg)
        dk = _each(lambda p_, kb_, r_, q_, dkd_, ekd_, dkb_, be_:
                   _bdot(p_, kb_, TN) + _bdot(r_, q_, TN) + dkd_ * ekd_ + dkb_ * be_,
                   dp, kb, dr, q, dkd, ekd, dkb, be)
        dq = _each(lambda r_, k_, dqd_, eg_: _bdot(r_, k_, NN) + dqd_ * eg_, dr, k, dqd, eg)
        m = _each(lambda da_, a_, dqk_, qk_: da_ * a_ + dqk_ * qk_, da, a, dqk, qk)
        m_cols = _each(lambda x: _split_dot(x, ones, TN), m)
        kd_term = _each(lambda x, y: rsum(x * y), dkd, kd)
        dgc = _each(lambda m_, mc, dqd_, qd_, kt, dkbg_, kbg_, dgl_, gl_:
                    rsum(m_) - mc + rsum(dqd_ * qd_) - kt + rsum(dkbg_ * kbg_)
                    + jnp.where(last, jnp.sum(kt, axis=0, keepdims=True) + dgl_ * gl_, 0.0),
                    m, m_cols, dqd, qd, kd_term, dkbg, kbg, dgl, gl)
        dg = _each(lambda x: _split_dot(upper, x, NN), dgc)
        for g in range(grp):
            dq_ref[rows[g], :] = dq[g]
            dk_ref[rows[g], :] = dk[g]
            dv_ref[rows[g], :] = dvb[g] * be[g]
            dg_ref[0, rows[g], :] = dg[g]
            db_ref[0, rows[g], :] = jnp.broadcast_to(rsum(dkb[g] * k[g]) + rsum(dvb[g] * v[g]), (CH, DH))

    r = bsz * lp
    return pl.pallas_call(
        body, name="gdn_local_bwd", grid=(bh, ngrp),
        in_specs=[head(0), head(GW), head(2 * GW), lane_b, lane_b, per_chunk(1, CH), per_chunk(CH, CH),
                  per_chunk(DH, DH), per_chunk(DH, DH), head(0), head(0), head(0)],
        out_specs=[head(0), head(0), head(0), lane_b, lane_b],
        out_shape=[jax.ShapeDtypeStruct((r, GW), F32)] * 3 + [jax.ShapeDtypeStruct((bh, lp, DH), F32)] * 2,
        compiler_params=_cparams(("parallel", "parallel")),
    )(qkv, qkv, qkv, gcb, bb, gct, tinv_all, states, dstates, vnew, dvnew, d_o)


def _place():
    return lax.axis_index("x"), lax.axis_index("y"), lax.axis_index("c")


def _any_specs(n):
    return [pl.BlockSpec(memory_space=pl.ANY)] * n


def _all_gather(name, shards):
    n = len(shards)

    def body(*refs):
        ins, outs = refs[:n], refs[n:2 * n]
        send_sems, recv_sems, local_sems = refs[2 * n:]
        x, y, c = _place()
        me, sibling = (x, y, c), (x, y, 1 - c)
        chips = [(1 - x, y), (x, 1 - y), (1 - x, 1 - y)]

        def slot(i, p):
            return outs[i].at[4 * p[0] + 2 * p[1] + p[2]]

        def copy(i, k, block, to, src=None):
            return pltpu.make_async_remote_copy(
                src_ref=slot(i, block) if src is None else src, dst_ref=slot(i, block),
                send_sem=send_sems.at[i, k], recv_sem=recv_sems.at[i, k], device_id=to, device_id_type=MESH)

        mine = [pltpu.make_async_copy(ins[i], slot(i, me), local_sems.at[i]) for i in range(n)]
        for cp in mine:
            cp.start()
        first = []
        for i in range(n):
            first.append(copy(i, 0, me, sibling, src=ins[i]))
            first += [copy(i, 1 + j, me, (*chip, c), src=ins[i]) for j, chip in enumerate(chips)]
        for cp in first:
            cp.start()
        passed = []
        for j, chip in enumerate(chips):
            for i in range(n):
                copy(i, 1 + j, (*chip, c), me).wait_recv()
                fwd = copy(i, 4 + j, (*chip, c), sibling)
                fwd.start()
                passed.append(fwd)
        for i in range(n):
            copy(i, 0, sibling, me).wait_recv()
            for j, chip in enumerate(chips):
                copy(i, 4 + j, (*chip, 1 - c), me).wait_recv()
        for cp in first + passed:
            cp.wait_send()
        for cp in mine:
            cp.wait()

    return pl.pallas_call(
        body, name=name,
        in_specs=_any_specs(n), out_specs=_any_specs(n),
        out_shape=[jax.ShapeDtypeStruct((N_DEV,) + s.shape, s.dtype) for s in shards],
        scratch_shapes=[pltpu.SemaphoreType.DMA((n, 7)), pltpu.SemaphoreType.DMA((n, 7)),
                        pltpu.SemaphoreType.DMA((n,))],
    )(*shards)


HBM_SPEC = pl.BlockSpec(memory_space=pltpu.HBM)
SEM_SPEC = pl.BlockSpec(memory_space=pltpu.SEMAPHORE)
EFFECT = pltpu.SideEffectType.DATAFLOW_SIDE_EFFECTING
N_PEER = N_DEV - 1


def _peer(r):
    x, y, c = _place()
    return ((1 - x) if r & 4 else x, (1 - y) if r & 2 else y, (1 - c) if r & 1 else c)


def _slot_of(p):
    return 4 * p[0] + 2 * p[1] + p[2]


def _hbm(a):
    return pltpu.with_memory_space_constraint(a, pltpu.HBM)


def _gather_pattern(src, land):
    me = _place()
    return [(src, land.at[_slot_of(me)], _peer(r), land.at[_slot_of(_peer(r))]) for r in range(1, N_DEV)]


def _sibling_pattern(src, land):
    x, y, c = _place()
    return [(src.at[1 - c], land, (x, y, 1 - c), land)]


def _chip_pattern(src, land):
    x, y, c = _place()
    chips = [(1 - x, y), (x, 1 - y), (1 - x, 1 - y)]
    return [(src.at[2 * ch[0] + ch[1]], land.at[j], (*ch, c), land.at[j]) for j, ch in enumerate(chips)]


def _split_start(name, groups, pattern, land_shape):
    sizes = [len(g) for g in groups]
    arrays = [s for g in groups for s in g]
    n = len(arrays)
    land_shapes = [land_shape(s.shape) for s in arrays]

    def body(*refs):
        srcs, lands = refs[:n], refs[n:2 * n]
        sems = refs[2 * n:2 * n + 2 * len(groups)]
        token = refs[-1]
        k = 0
        for gi, size in enumerate(sizes):
            for i in range(size):
                copies = pattern(srcs[k], lands[k])
                for j, (s, d, peer, _) in enumerate(copies):
                    pltpu.make_async_remote_copy(
                        src_ref=s, dst_ref=d, send_sem=sems[2 * gi].at[i * len(copies) + j],
                        recv_sem=sems[2 * gi + 1].at[i * len(copies) + j], device_id=peer, device_id_type=MESH).start()
                k += 1
        token[...] = jnp.zeros_like(token)

    per_array = {_gather_pattern: N_PEER, _sibling_pattern: 1, _chip_pattern: 3}[pattern]
    sem_shapes = [pltpu.SemaphoreType.DMA((size * per_array,)) for size in sizes for _ in range(2)]
    res = pl.pallas_call(
        body, name=name,
        in_specs=[HBM_SPEC] * (2 * n),
        out_specs=[SEM_SPEC] * len(sem_shapes) + [HBM_SPEC] * (2 * n) + [pl.BlockSpec(memory_space=pltpu.VMEM)],
        out_shape=sem_shapes + [pltpu.HBM(s.shape, s.dtype) for s in arrays]
        + [pltpu.HBM(ls, s.dtype) for ls, s in zip(land_shapes, arrays)] + [jax.ShapeDtypeStruct((8, DH), F32)],
        input_output_aliases={i: len(sem_shapes) + i for i in range(2 * n)},
        compiler_params=pltpu.CompilerParams(has_side_effects=EFFECT),
    )(*[_hbm(s) for s in arrays], *[_hbm(lax.empty(ls, s.dtype)) for ls, s in zip(land_shapes, arrays)])
    nsem = len(sem_shapes)
    out, k = [], 0
    for gi, size in enumerate(sizes):
        out.append((res[2 * gi], res[2 * gi + 1], list(res[nsem + k:nsem + k + size]),
                    list(res[nsem + n + k:nsem + n + k + size])))
        k += size
    return out, res[-1]


def _split_wait(name, group, pattern, after):
    send_sems, recv_sems, arrays, lands = group
    n = len(arrays)

    def body(*refs):
        srcs, lands_ = refs[:n], refs[n:2 * n]
        s_sems, r_sems = refs[2 * n], refs[2 * n + 1]
        for i in range(n):
            copies = pattern(srcs[i], lands_[i])
            for j, (s, _, peer, filled) in enumerate(copies):
                cp = pltpu.make_async_remote_copy(
                    src_ref=s, dst_ref=filled, send_sem=s_sems.at[i * len(copies) + j],
                    recv_sem=r_sems.at[i * len(copies) + j], device_id=peer, device_id_type=MESH)
                cp.wait_send()
                cp.wait_recv()

    res = pl.pallas_call(
        body, name=name,
        in_specs=[HBM_SPEC] * (2 * n) + [SEM_SPEC, SEM_SPEC, pl.BlockSpec(memory_space=pl.ANY)],
        out_specs=[HBM_SPEC] * (2 * n),
        out_shape=[pltpu.HBM(s.shape, s.dtype) for s in arrays] + [pltpu.HBM(l.shape, l.dtype) for l in lands],
        input_output_aliases={i: i for i in range(2 * n)},
        compiler_params=pltpu.CompilerParams(has_side_effects=EFFECT),
    )(*arrays, *lands, send_sems, recv_sems, after)
    return list(res[:n]), list(res[n:])


def _pair_sum(name, grads, recv, core):
    _, _, r, c = grads.shape
    tr = _div_tile(r, 256, 16)

    def body(core_ref, g_ref, r_ref, o_ref):
        o_ref[...] = (g_ref[0].astype(F32) + r_ref[...].astype(F32)).astype(o_ref.dtype)

    return pl.pallas_call(
        body, name=name,
        grid_spec=pltpu.PrefetchScalarGridSpec(
            num_scalar_prefetch=1, grid=(4, r // tr),
            in_specs=[pl.BlockSpec((1, 1, tr, c), lambda s, i, cr: (cr[0], s, i, 0)),
                      pl.BlockSpec((1, tr, c), lambda s, i, cr: (s, i, 0))],
            out_specs=pl.BlockSpec((1, tr, c), lambda s, i, cr: (s, i, 0))),
        out_shape=jax.ShapeDtypeStruct(recv.shape, recv.dtype),
        compiler_params=_cparams(("parallel", "parallel")),
    )(core, grads, recv)


def _adamw_math(w, g, m, v):
    m = ADAM_B1 * m + (1.0 - ADAM_B1) * g
    v = ADAM_B2 * v + (1.0 - ADAM_B2) * (g * g)
    m_hat = m / (1.0 - ADAM_B1 ** ADAM_STEP)
    v_hat = v / (1.0 - ADAM_B2 ** ADAM_STEP)
    delta = -ADAM_LR * (m_hat / (jnp.sqrt(v_hat) + ADAM_EPS) + ADAM_WD * w)
    return delta, m, v


def _adamw(name, parts, w, m, v):
    nslot, r, c = parts.shape
    tr = _div_tile(r, 256, 8) if r % 8 == 0 else r

    def body(p_ref, w_ref, m_ref, v_ref, g_out, d_out, m_out, v_out):
        g = p_ref[0]
        for s in range(1, nslot):
            g = g + p_ref[s]
        d, mn, vn = _adamw_math(w_ref[...], g, m_ref[...], v_ref[...])
        g_out[...] = g
        d_out[...] = d
        m_out[...] = mn
        v_out[...] = vn

    blk = pl.BlockSpec((tr, c), lambda i: (i, 0))
    return pl.pallas_call(
        body, name=name, grid=(r // tr,),
        in_specs=[pl.BlockSpec((nslot, tr, c), lambda i: (0, i, 0)), blk, blk, blk],
        out_specs=[blk] * 4,
        out_shape=[jax.ShapeDtypeStruct((r, c), F32)] * 4,
        compiler_params=_cparams(("parallel",)),
    )(parts, w, m, v)


def _adamw_reduced(name, parts, chip, recv, w, m, v):
    _, r, c = parts.shape
    tr = _div_tile(r, 256, 16)

    def body(chip_ref, p_ref, r_ref, w_ref, m_ref, v_ref, g_out, d_out, m_out, v_out):
        g = p_ref[0].astype(F32)
        for s in range(3):
            g = g + r_ref[s].astype(F32)
        d, mn, vn = _adamw_math(w_ref[...], g, m_ref[...], v_ref[...])
        g_out[...] = g
        d_out[...] = d
        m_out[...] = mn
        v_out[...] = vn

    blk = pl.BlockSpec((tr, c), lambda i, ch: (i, 0))
    return pl.pallas_call(
        body, name=name,
        grid_spec=pltpu.PrefetchScalarGridSpec(
            num_scalar_prefetch=1, grid=(r // tr,),
            in_specs=[pl.BlockSpec((1, tr, c), lambda i, ch: (ch[0], i, 0)),
                      pl.BlockSpec((3, tr, c), lambda i, ch: (0, i, 0)), blk, blk, blk],
            out_specs=[blk] * 4),
        out_shape=[jax.ShapeDtypeStruct((r, c), F32)] * 4,
        compiler_params=_cparams(("parallel",)),
    )(chip, parts, recv, w, m, v)


SMALL_ROWS, SMALL_COLS = 32, 3 * GW


def _pack_small(gains4, meta, conv_qkv, conv_sc, a_log, dt_bias, gdn_norm):
    def padc(a):
        return jnp.pad(a, ((0, 0), (0, SMALL_COLS - a.shape[1])))
    scal = jnp.concatenate([jnp.pad(a_log, ((0, 0), (0, DH - HEADS))), jnp.pad(dt_bias, ((0, 0), (0, DH - HEADS))),
                            gdn_norm], axis=1)
    rows = [padc(g) for g in gains4] + [padc(meta), padc(conv_qkv), padc(conv_sc), padc(scal)]
    slab = jnp.concatenate(rows, axis=0)
    return jnp.pad(slab, ((0, SMALL_ROWS - slab.shape[0]), (0, 0)))


def _unpack_small(slab):
    gains = [slab[i:i + 1, :D] for i in range(4)]
    meta = slab[4:20, :D]
    conv_qkv = slab[20:24, :]
    conv_sc = slab[24:27, :SW]
    a_log = slab[27:28, 0:HEADS]
    dt_bias = slab[27:28, DH:DH + HEADS]
    gdn_norm = slab[27:28, 2 * DH:3 * DH]
    return gains, meta, conv_qkv, conv_sc, a_log, dt_bias, gdn_norm


def kernel(x, meta_tokens, mix_pre_norm, mix_post_norm, ffn_pre_norm, ffn_post_norm, w_in, conv_qkv, a_log, dt_bias, gdn_norm, conv_sc, w_out, w_gate, w_up, w_down, loss_target, m_meta_tokens, m_mix_pre_norm, m_mix_post_norm, m_ffn_pre_norm, m_ffn_post_norm, m_w_in, m_conv_qkv, m_a_log, m_dt_bias, m_gdn_norm, m_conv_sc, m_w_out, m_w_gate, m_w_up, m_w_down, v_meta_tokens, v_mix_pre_norm, v_mix_post_norm, v_ffn_pre_norm, v_ffn_post_norm, v_w_in, v_conv_qkv, v_a_log, v_dt_bias, v_gdn_norm, v_conv_sc, v_w_out, v_w_gate, v_w_up, v_w_down):
    bsz, seq, _ = x.shape
    lp = -(-(N_META + seq) // CH) * CH
    padf = lp - N_META - seq
    first_real = padf + N_META
    rows = bsz * lp
    nchunk = lp // CH
    tr = _div_tile(lp, 528)
    tm = _div_tile(rows, 1056)
    cx, cy, cc = _place()
    dev = 4 * cx + 2 * cy + cc

    small_shard = jnp.concatenate([
        jnp.pad(meta_tokens, ((0, 0), (0, 256 - DH))),
        jnp.pad(conv_qkv[0], ((0, 0), (0, 256 - 3 * GW // N_DEV))),
        jnp.pad(conv_sc[0], ((0, 5), (0, 256 - SW // N_DEV)))], axis=0)
    g_in, g_small = _all_gather("gather_w_in", [w_in[0].astype(BF16), small_shard])
    my_out, my_ffn = [w_out[0].astype(BF16)], [w_gate[0].astype(BF16), w_up[0].astype(BF16), w_down[0].astype(BF16)]
    (grp_out, grp_ffn), token = _split_start("gather_rest_start", [my_out, my_ffn], _gather_pattern,
                                             lambda s: (N_DEV,) + s)
    after_start = token[0:1, 0:1]

    def cols_full(g):
        return jnp.transpose(g, (1, 0, 2)).reshape(g.shape[1], -1)

    def with_mine(land, mine):
        return lax.dynamic_update_index_in_dim(land, mine, dev, 0)

    win = cols_full(g_in)
    win = jnp.concatenate([win[:, :BA0 + 2 * HEADS], jnp.zeros((D, BA_W - 2 * HEADS), BF16),
                           win[:, BA0 + 2 * HEADS:]], axis=1)
    meta_full = cols_full(g_small[:, 0:16, :DH])
    convq_full = cols_full(g_small[:, 16:20, :3 * GW // N_DEV])
    convs_full = cols_full(g_small[:, 20:23, :SW // N_DEV])
    alog_v = jnp.pad(a_log, ((0, 0), (HEADS, DH - 2 * HEADS)))
    dtb_v = jnp.pad(dt_bias, ((0, 0), (HEADS, DH - 2 * HEADS)))

    h0 = jnp.concatenate([jnp.zeros((bsz, padf, D), F32), jnp.broadcast_to(meta_full[None], (bsz, N_META, D)), x],
                         axis=1).reshape(rows, D)
    tgt = jnp.pad(loss_target, ((0, 0), (first_real, 0), (0, 0))).reshape(rows, D)
    u1 = _pre_norm(h0, mix_pre_norm + after_start, bsz, lp, tr)
    (proj,) = _mm("proj_in", [(u1, win)], "nn", tm, 768, D, out_dtypes=(BF16,))
    qkv = _gdn_prep(proj, convq_full, bsz, lp, padf)
    gb, gcs = _gates(proj, alog_v, dtb_v, bsz, lp, padf)

    def lane_bcast(cols):
        t = jnp.transpose(cols.reshape(bsz, lp, HEADS), (0, 2, 1)).reshape(bsz * HEADS, lp, 1)
        return jnp.broadcast_to(t, (bsz * HEADS, lp, DH))

    bb = lane_bcast(gb[:, 0:HEADS])
    gcb = lane_bcast(gcs[:, HEADS:2 * HEADS])
    gct = jnp.transpose(gcs[:, HEADS:2 * HEADS].reshape(bsz, nchunk, CH, HEADS), (0, 3, 1, 2)).reshape(
        bsz * HEADS, nchunk, 1, CH)
    u_loc, l1, l2, l3, w_t, k_dec, tinv_all, g_last = _gdn_local_fwd(qkv, gcb, bb, gct, bsz, lp)
    o, v_new, states = _gdn_scan_fwd(u_loc, l1, l2, g_last, bsz, lp)
    o, v_new = o.reshape(rows, GW), v_new.reshape(rows, GW)
    mix_g, mix_s = _mix_heads(proj, o, gdn_norm, convs_full, bsz, lp, padf)
    mixin = jnp.concatenate([mix_g, mix_s], axis=1)
    my_out, (l_out,) = _split_wait("gather_w_out_wait", grp_out, _gather_pattern, mixin)
    wout = with_mine(l_out, my_out[0]).reshape(D, D)
    (mix,) = _mm("proj_out", [(mixin, wout)], "nn", tm, 512, D)
    my_ffn, (l_gate, l_up, l_down) = _split_wait("gather_ffn_wait", grp_ffn, _gather_pattern, mix)
    wgate, wup = cols_full(with_mine(l_gate, my_ffn[0])), cols_full(with_mine(l_up, my_ffn[1]))
    wdown = with_mine(l_down, my_ffn[2]).reshape(FF, D)
    h1, u2 = _mid_norms(h0, mix, mix_post_norm, ffn_pre_norm, bsz, lp, tr)
    gate, up, act = _ffn_up(u2, wgate, wup, _div_tile(rows, 528), 1408)
    (ffn,) = _mm("ffn_down", [(act, wdown)], "nn", tm, 512, 1408)

    dy, dffn, d_g4, loss_part = _loss_head(h1, ffn, tgt, ffn_post_norm, bsz, lp, tr, first_real)
    loss = lax.psum(loss_part[0, 0], ("x", "y", "c"))
    (dwdown,) = _mm("dw_down", [(act, dffn)], "tn", 1408, 512, tm, out_dtypes=(BF16,))

    def swiglu_bwd(da, g, u):
        g, u = g.astype(F32), u.astype(F32)
        return da * u * _dsilu(g), da * _silu(g)

    dgate, dup = _mm("d_act", [(dffn, wdown)], "nt", tm, 1408, D, out_dtypes=(BF16, BF16),
                     epilogue=swiglu_bwd, extras=(gate, up))
    (dwgate,) = _mm("dw_gate", [(u2, dgate)], "tn", D, 1408, tm, out_dtypes=(BF16,))
    (dwup,) = _mm("dw_up", [(u2, dup)], "tn", D, 1408, tm, out_dtypes=(BF16,))
    (du2,) = _mm("d_u2", [(dgate, wgate), (dup, wup)], "nt", tm, 512, 1408)
    dh1, dmix, d_g3, d_g2 = _mid_norms_bwd(h1, mix, du2, dy, ffn_pre_norm, mix_post_norm, bsz, lp, tr)
    (dwout,) = _mm("dw_out", [(mixin, dmix)], "tn", D, 512, tm, out_dtypes=(BF16,))

    def col_blocks(g):
        r = g.shape[0]
        return jnp.transpose(g.reshape(r, 4, 2, -1), (2, 1, 0, 3))

    def row_blocks(g):
        c = g.shape[1]
        return jnp.transpose(g.reshape(4, 2, -1, c), (1, 0, 2, 3))

    core = jnp.reshape(cc, (1,)).astype(jnp.int32)
    chip = jnp.reshape(2 * cx + cy, (1,)).astype(jnp.int32)

    def reduce_start(tag, grads):
        (grp,), tok = _split_start("rs_sibling_start_" + tag, [grads], _sibling_pattern, lambda s: s[1:])
        return grp, tok

    def reduce_mid(tag, grp, nms, after):
        mine, lands = _split_wait("rs_sibling_wait_" + tag, grp, _sibling_pattern, after)
        parts = [_pair_sum("pair_sum_" + nm, g, r, core) for nm, g, r in zip(nms, mine, lands)]
        (grp2,), tok = _split_start("rs_chip_start_" + tag, [parts], _chip_pattern, lambda s: (3,) + s[1:])
        return grp2, tok

    def reduce_end(tag, grp2, after):
        return _split_wait("rs_chip_wait_" + tag, grp2, _chip_pattern, after)

    names_ffn = ["w_down", "w_gate", "w_up", "w_out"]
    rs_ffn, tok = reduce_start("ffn", [row_blocks(dwdown), col_blocks(dwgate), col_blocks(dwup), row_blocks(dwout)])
    (dmixin,) = _mm("d_mixin", [(dmix, wout)], "nt", tm, 512, D, after=tok)
    d_o, dz, dsx, dsb, dsc, d_gn, d_convs = _mix_heads_bwd(proj, o, gdn_norm, convs_full, dmixin, bsz, lp, padf)
    rs_ffn, tok = reduce_mid("ffn", rs_ffn, names_ffn, d_o)
    dv_new, dstates = _gdn_scan_bwd(d_o, l3, w_t, k_dec, g_last + tok[0, 0], bsz, lp)
    dq, dk, dv, dg_b, dbeta_b = _gdn_local_bwd(qkv, gcb, bb, gct, tinv_all, states, dstates, v_new,
                                               dv_new.reshape(rows, GW), d_o, bsz, lp)
    dqkv = jnp.concatenate([dq, dk, dv], axis=1)
    dpqkv, d_convq = _gdn_prep_bwd(proj, convq_full, dqkv, bsz, lp, padf)
    dba, d_scal = _gates_bwd(proj, alog_v, dtb_v, dbeta_b, dg_b, gb, bsz, lp, padf)
    dproj = jnp.concatenate([dpqkv, dz, dba, dsx, dsb, dsc], axis=1)
    (dwin,) = _mm("dw_in", [(u1, dproj)], "tn", D, 768, tm, out_dtypes=(BF16,))
    parts_ffn, recv_ffn = reduce_end("ffn", rs_ffn, dwin)
    dwin = jnp.concatenate([dwin[:, :BA0 + 2 * HEADS], dwin[:, SX0:]], axis=1)
    rs_in, tok = reduce_start("in", [col_blocks(dwin)])
    (du1,) = _mm("d_u1", [(dproj, win)], "nt", tm, 512, 768, after=tok)
    rs_in, tok = reduce_mid("in", rs_in, ["w_in"], du1)
    dh0, d_g1 = _pre_norm_bwd(h0, du1, dh1, mix_pre_norm + tok[0:1, 0:1], bsz, lp, tr)
    dh0 = dh0.reshape(bsz, lp, D)
    grad_x = dh0[:, first_real:]
    d_meta = jnp.sum(dh0[:, padf:first_real], axis=0)

    d_alog = d_scal[0:1, HEADS:2 * HEADS]
    d_dtb = d_scal[1:2, HEADS:2 * HEADS]
    slab = _pack_small([d_g1, d_g2, d_g3, d_g4], d_meta, d_convq, d_convs, d_alog, d_dtb, d_gn)
    (grp_small,), _ = _split_start("small_grads_start", [[slab]], _gather_pattern, lambda s: (N_DEV,) + s)

    big_w = {"w_in": (w_in, m_w_in, v_w_in), "w_out": (w_out, m_w_out, v_w_out), "w_gate": (w_gate, m_w_gate, v_w_gate),
             "w_up": (w_up, m_w_up, v_w_up), "w_down": (w_down, m_w_down, v_w_down)}

    def update(nm, parts, recv):
        w, m, v = big_w[nm]
        return _adamw_reduced("adamw_" + nm, parts, chip, recv, w[0], m[0], v[0])

    big_out = {nm: update(nm, p, r) for nm, p, r in zip(names_ffn, parts_ffn, recv_ffn)}
    (slab,), (slabs,) = _split_wait("small_grads_wait", grp_small, _gather_pattern, big_out["w_out"][1])
    slabs = with_mine(slabs, slab)
    off_meta = dev * DH
    off_cq = dev * (3 * GW // N_DEV)
    off_cs = dev * (SW // N_DEV)

    def shard_of(s):
        gains, meta, cq, cs, al, db, gn = _unpack_small(s)
        return gains, lax.dynamic_slice_in_dim(meta, off_meta, DH, 1), \
            lax.dynamic_slice_in_dim(cq, off_cq, 3 * GW // N_DEV, 1), \
            lax.dynamic_slice_in_dim(cs, off_cs, SW // N_DEV, 1), al, db, gn

    slabs_local = jax.vmap(lambda s: _pack_small(*shard_of(s)))(slabs)
    w_small = _pack_small([mix_pre_norm, mix_post_norm, ffn_pre_norm, ffn_post_norm], meta_tokens, conv_qkv[0],
                          conv_sc[0], a_log, dt_bias, gdn_norm)
    m_small = _pack_small([m_mix_pre_norm, m_mix_post_norm, m_ffn_pre_norm, m_ffn_post_norm], m_meta_tokens,
                          m_conv_qkv[0], m_conv_sc[0], m_a_log, m_dt_bias, m_gdn_norm)
    v_small = _pack_small([v_mix_pre_norm, v_mix_post_norm, v_ffn_pre_norm, v_ffn_post_norm], v_meta_tokens,
                          v_conv_qkv[0], v_conv_sc[0], v_a_log, v_dt_bias, v_gdn_norm)
    small_out = _adamw("adamw_small", slabs_local, w_small, m_small, v_small)
    parts_in, recv_in = reduce_end("in", rs_in, small_out[1])
    big_out["w_in"] = update("w_in", parts_in[0], recv_in[0])
    names = ["w_in", "w_out", "w_gate", "w_up", "w_down"]

    def small_leaves(slab_out):
        gains, meta, cq, cs, al, db, gn = _unpack_small(slab_out)
        return {"meta_tokens": meta[:, :DH], "mix_pre_norm": gains[0], "mix_post_norm": gains[1],
                "ffn_pre_norm": gains[2], "ffn_post_norm": gains[3], "conv_qkv": cq[None, :, :3 * GW // N_DEV],
                "a_log": al, "dt_bias": db, "gdn_norm": gn, "conv_sc": cs[None, :, :SW // N_DEV]}

    order = ["meta_tokens", "mix_pre_norm", "mix_post_norm", "ffn_pre_norm", "ffn_post_norm", "w_in", "conv_qkv",
             "a_log", "dt_bias", "gdn_norm", "conv_sc", "w_out", "w_gate", "w_up", "w_down"]
    outs = [loss, grad_x]
    for kind in range(4):
        leaves = small_leaves(small_out[kind])
        for nm in names:
            leaves[nm] = big_out[nm][kind][None]
        outs += [leaves[nm] for nm in order]
    return tuple(outs)
```

```python
import functools

import jax
import jax.numpy as jnp
from jax import lax
from jax.experimental import pallas as pl
from jax.experimental.pallas import tpu as pltpu

F32 = jnp.float32
BF16 = jnp.bfloat16
HIGHEST = lax.Precision.HIGHEST
MESH = pl.DeviceIdType.MESH

D = 1024
N_META = 16
HEADS = 4
DH = 128
GW = HEADS * DH
SW = D - GW
FF = 2816
CH = 64
KQ = 4
KS = 3
EPS = 1e-6
IN_W = 3 * GW + GW + 2 * HEADS + 3 * SW
Q0, K0, V0, Z0, BA0, SX0, SB0, SC0, PW = 0, 512, 1024, 1536, 2048, 2304, 2816, 3328, 3840
BA_W = SX0 - BA0
N_DEV = 8
VMEM_LIMIT = 56 * 1024 * 1024

ADAM_LR, ADAM_B1, ADAM_B2, ADAM_EPS, ADAM_WD, ADAM_STEP = 0.001, 0.9, 0.999, 1e-08, 0.01, 10


def _cparams(sem):
    return pltpu.CompilerParams(dimension_semantics=sem, vmem_limit_bytes=VMEM_LIMIT)


def _div_tile(n, target, mult=16):
    best = None
    for t in range(mult, min(n, target) + 1, mult):
        if n % t == 0:
            best = t
    assert best is not None, (n, target)
    return best


def _silu(x):
    return x * jax.nn.sigmoid(x)


def _dsilu(x):
    s = jax.nn.sigmoid(x)
    return s * (1.0 + x * (1.0 - s))


def _bdot(a, b, dims):
    return lax.dot_general(a.astype(BF16), b.astype(BF16), (dims, ((), ())), preferred_element_type=F32)


NN = ((1,), (0,))
NT = ((1,), (1,))
TN = ((0,), (0,))


def _fdot(a, b, dims=NN):
    return lax.dot_general(a, b, (dims, ((), ())), precision=HIGHEST, preferred_element_type=F32)


def _mm(name, pairs, mode, tm, tn, tk, out_dtypes=(F32,), epilogue=None, extras=(), after=None):
    a0, b0 = pairs[0]
    if mode == "nn":
        (m, k), n = a0.shape, b0.shape[1]
    elif mode == "nt":
        (m, k), n = a0.shape, b0.shape[0]
    else:
        (k, m), n = a0.shape, b0.shape[1]
    assert m % tm == 0 and n % tn == 0 and k % tk == 0, (name, m, n, k, tm, tn, tk)
    nk = k // tk
    npair = len(pairs)
    nex = len(extras)
    nout = len(out_dtypes)
    ntok = 0 if after is None else 1
    dims = {"nn": NN, "nt": NT, "tn": TN}[mode]

    def body(*refs):
        ab = refs[:2 * npair]
        ex = refs[2 * npair:2 * npair + nex]
        outs = refs[2 * npair + nex + ntok:2 * npair + nex + ntok + nout]
        acc_ref = refs[-1]
        kk = pl.program_id(2)

        part = None
        for p in range(npair):
            d = lax.dot_general(ab[2 * p][...], ab[2 * p + 1][...], (dims, ((), ())), preferred_element_type=F32)
            part = d if part is None else part + d

        def finish(acc):
            res = epilogue(acc, *[e[...] for e in ex]) if epilogue is not None else (acc,)
            for o, r in zip(outs, res):
                o[...] = r.astype(o.dtype)

        if nk == 1:
            finish(part)
        else:
            @pl.when(kk == 0)
            def _():
                acc_ref[...] = part

            @pl.when(kk > 0)
            def _():
                acc_ref[...] += part

            @pl.when(kk == nk - 1)
            def _():
                finish(acc_ref[...])

    if mode == "nn":
        a_spec = pl.BlockSpec((tm, tk), lambda i, j, q: (i, q))
        b_spec = pl.BlockSpec((tk, tn), lambda i, j, q: (q, j))
    elif mode == "nt":
        a_spec = pl.BlockSpec((tm, tk), lambda i, j, q: (i, q))
        b_spec = pl.BlockSpec((tn, tk), lambda i, j, q: (j, q))
    else:
        a_spec = pl.BlockSpec((tk, tm), lambda i, j, q: (q, i))
        b_spec = pl.BlockSpec((tk, tn), lambda i, j, q: (q, j))
    o_spec = pl.BlockSpec((tm, tn), lambda i, j, q: (i, j))
    flat = [t for pr in pairs for t in pr]
    res = pl.pallas_call(
        body, name=name,
        grid=(m // tm, n // tn, nk),
        in_specs=[a_spec, b_spec] * npair + [o_spec] * nex + [pl.BlockSpec((8, DH), lambda i, j, q: (0, 0))] * ntok,
        out_specs=[o_spec] * nout,
        out_shape=[jax.ShapeDtypeStruct((m, n), dt) for dt in out_dtypes],
        scratch_shapes=[pltpu.VMEM((tm, tn), F32)],
        compiler_params=_cparams(("parallel", "parallel", "arbitrary")),
    )(*flat, *extras, *([] if after is None else [after]))
    return res


def _ffn_up(u2, wg, wu, tm, tn):
    m, k = u2.shape
    n = wg.shape[1]

    def body(a_ref, g_ref, u_ref, gate_ref, up_ref, act_ref):
        a = a_ref[...]
        g = jnp.dot(a, g_ref[...], preferred_element_type=F32)
        u = jnp.dot(a, u_ref[...], preferred_element_type=F32)
        gate_ref[...] = g.astype(BF16)
        up_ref[...] = u.astype(BF16)
        act_ref[...] = (_silu(g) * u).astype(BF16)

    o_spec = pl.BlockSpec((tm, tn), lambda i, j: (i, j))
    w_spec = pl.BlockSpec((k, tn), lambda i, j: (0, j))
    return pl.pallas_call(
        body, name="ffn_up", grid=(m // tm, n // tn),
        in_specs=[pl.BlockSpec((tm, k), lambda i, j: (i, 0)), w_spec, w_spec],
        out_specs=[o_spec, o_spec, o_spec],
        out_shape=[jax.ShapeDtypeStruct((m, n), BF16)] * 3,
        compiler_params=_cparams(("parallel", "parallel")),
    )(u2, wg, wu)


def _rms(x, gain):
    r = lax.rsqrt(jnp.mean(x * x, axis=-1, keepdims=True) + EPS)
    return x * r * gain, r


def _rms_bwd(x, gain, dy):
    r = lax.rsqrt(jnp.mean(x * x, axis=-1, keepdims=True) + EPS)
    dyw = dy * gain
    dx = r * dyw - x * (r * r * r) * jnp.mean(dyw * x, axis=-1, keepdims=True)
    return dx, dy * x * r


def _row_specs(tr, width, nb):
    return pl.BlockSpec((tr, width), lambda b, j: (b * nb + j, 0))


def _vec_spec(width):
    return pl.BlockSpec((1, width), lambda b, j: (0, 0))


def _first_step(b, j):
    return jnp.logical_and(b == 0, j == 0)


def _pre_norm(h0, gain, bsz, lp, tr):
    nb = lp // tr

    def body(h_ref, g_ref, u_ref):
        u_ref[...] = _rms(h_ref[...], g_ref[...])[0].astype(BF16)

    return pl.pallas_call(
        body, name="mix_pre_norm", grid=(bsz, nb),
        in_specs=[_row_specs(tr, D, nb), _vec_spec(D)],
        out_specs=_row_specs(tr, D, nb),
        out_shape=jax.ShapeDtypeStruct(h0.shape, BF16),
        compiler_params=_cparams(("parallel", "parallel")),
    )(h0, gain)


def _mid_norms(h0, mix, g_post, g_pre, bsz, lp, tr):
    nb = lp // tr

    def body(h_ref, m_ref, gp_ref, gq_ref, h1_ref, u2_ref):
        h1 = h_ref[...] + _rms(m_ref[...], gp_ref[...])[0]
        h1_ref[...] = h1
        u2_ref[...] = _rms(h1, gq_ref[...])[0].astype(BF16)

    rs = _row_specs(tr, D, nb)
    return pl.pallas_call(
        body, name="mid_norms", grid=(bsz, nb),
        in_specs=[rs, rs, _vec_spec(D), _vec_spec(D)],
        out_specs=[rs, rs],
        out_shape=[jax.ShapeDtypeStruct(h0.shape, F32), jax.ShapeDtypeStruct(h0.shape, BF16)],
        compiler_params=_cparams(("parallel", "parallel")),
    )(h0, mix, g_post, g_pre)


def _loss_head(h1, ffn, tgt, g_post, bsz, lp, tr, first_real):
    nb = lp // tr

    def body(h_ref, f_ref, t_ref, g_ref, dy_ref, dffn_ref, dg_ref, loss_ref):
        b, j = pl.program_id(0), pl.program_id(1)
        f = f_ref[...]
        g = g_ref[...]
        y = h_ref[...] + _rms(f, g)[0]
        t = j * tr + lax.broadcasted_iota(jnp.int32, (tr, 1), 0)
        err = jnp.where(t >= first_real, y - t_ref[...], 0.0)
        dy = err * (1.0 / D)
        dy_ref[...] = dy
        dx, dgc = _rms_bwd(f, g, dy)
        dffn_ref[...] = dx.astype(BF16)
        lsum = jnp.sum(jnp.sum(err * err, axis=0, keepdims=True), axis=1, keepdims=True) * (0.5 / D)

        @pl.when(_first_step(b, j))
        def _():
            dg_ref[...] = jnp.zeros_like(dg_ref)
            loss_ref[...] = jnp.zeros_like(loss_ref)

        dg_ref[...] += jnp.sum(dgc, axis=0, keepdims=True)
        loss_ref[...] += jnp.broadcast_to(lsum, loss_ref.shape)

    rs = _row_specs(tr, D, nb)
    return pl.pallas_call(
        body, name="loss_head", grid=(bsz, nb),
        in_specs=[rs, rs, rs, _vec_spec(D)],
        out_specs=[rs, rs, _vec_spec(D), _vec_spec(DH)],
        out_shape=[jax.ShapeDtypeStruct(h1.shape, F32), jax.ShapeDtypeStruct(h1.shape, BF16),
                   jax.ShapeDtypeStruct((1, D), F32), jax.ShapeDtypeStruct((1, DH), F32)],
        compiler_params=_cparams(("arbitrary", "arbitrary")),
    )(h1, ffn, tgt, g_post)


def _mid_norms_bwd(h1, mix, du2, dy, g_pre, g_post, bsz, lp, tr):
    nb = lp // tr

    def body(h_ref, m_ref, du_ref, dy_ref, gq_ref, gp_ref, dh1_ref, dmix_ref, dgq_ref, dgp_ref):
        b, j = pl.program_id(0), pl.program_id(1)
        dx, dgq = _rms_bwd(h_ref[...], gq_ref[...], du_ref[...])
        dh1 = dy_ref[...] + dx
        dh1_ref[...] = dh1
        dm, dgp = _rms_bwd(m_ref[...], gp_ref[...], dh1)
        dmix_ref[...] = dm.astype(BF16)

        @pl.when(_first_step(b, j))
        def _():
            dgq_ref[...] = jnp.zeros_like(dgq_ref)
            dgp_ref[...] = jnp.zeros_like(dgp_ref)

        dgq_ref[...] += jnp.sum(dgq, axis=0, keepdims=True)
        dgp_ref[...] += jnp.sum(dgp, axis=0, keepdims=True)

    rs = _row_specs(tr, D, nb)
    return pl.pallas_call(
        body, name="mid_norms_bwd", grid=(bsz, nb),
        in_specs=[rs, rs, rs, rs, _vec_spec(D), _vec_spec(D)],
        out_specs=[rs, rs, _vec_spec(D), _vec_spec(D)],
        out_shape=[jax.ShapeDtypeStruct(h1.shape, F32), jax.ShapeDtypeStruct(h1.shape, BF16),
                   jax.ShapeDtypeStruct((1, D), F32), jax.ShapeDtypeStruct((1, D), F32)],
        compiler_params=_cparams(("arbitrary", "arbitrary")),
    )(h1, mix, du2, dy, g_pre, g_post)


def _pre_norm_bwd(h0, du1, dh1, gain, bsz, lp, tr):
    nb = lp // tr

    def body(h_ref, du_ref, dh1_ref, g_ref, dh0_ref, dg_ref):
        b, j = pl.program_id(0), pl.program_id(1)
        dx, dgc = _rms_bwd(h_ref[...], g_ref[...], du_ref[...])
        dh0_ref[...] = dh1_ref[...] + dx

        @pl.when(_first_step(b, j))
        def _():
            dg_ref[...] = jnp.zeros_like(dg_ref)

        dg_ref[...] += jnp.sum(dgc, axis=0, keepdims=True)

    rs = _row_specs(tr, D, nb)
    return pl.pallas_call(
        body, name="pre_norm_bwd", grid=(bsz, nb),
        in_specs=[rs, rs, rs, _vec_spec(D)],
        out_specs=[rs, _vec_spec(D)],
        out_shape=[jax.ShapeDtypeStruct(h0.shape, F32), jax.ShapeDtypeStruct((1, D), F32)],
        compiler_params=_cparams(("arbitrary", "arbitrary")),
    )(h0, du1, dh1, gain)


def _shift_down(x, s, lp):
    return x if s == 0 else pltpu.roll(x, s, axis=0)


def _shift_up(x, s, lp):
    return x if s == 0 else pltpu.roll(x, lp - s, axis=0)


def _conv_fwd(x, w, taps, lp):
    y = None
    for i in range(taps):
        term = _shift_down(x, taps - 1 - i, lp) * w[i:i + 1, :]
        y = term if y is None else y + term
    return y


def _conv_bwd(x, w, dy, taps, lp):
    dx = None
    dws = []
    for i in range(taps):
        s = taps - 1 - i
        term = _shift_up(dy, s, lp) * w[i:i + 1, :]
        dx = term if dx is None else dx + term
        dws.append(jnp.sum(dy * _shift_down(x, s, lp), axis=0, keepdims=True))
    return dx, jnp.concatenate(dws, axis=0)


def _slab(lp, col0):
    return pl.BlockSpec((lp, DH), lambda b, s: (b, col0 // DH + s))


def _qkv_act(c, s):
    y = _silu(c)
    r = lax.rsqrt(jnp.sum(y * y, axis=-1, keepdims=True) + EPS)
    scale = jnp.where(s < HEADS, DH ** -0.5, 1.0)
    return y, r, scale


def _gdn_prep(proj, conv_qkv, bsz, lp, padf):
    def body(p_ref, w_ref, o_ref):
        s = pl.program_id(1)
        c = _conv_fwd(p_ref[...].astype(F32), w_ref[...], KQ, lp)
        y, r, scale = _qkv_act(c, s)
        out = jnp.where(s < 2 * HEADS, y * (r * scale), y)
        t = lax.broadcasted_iota(jnp.int32, (lp, 1), 0)
        o_ref[...] = jnp.where(t >= padf, out, 0.0)

    return pl.pallas_call(
        body, name="gdn_prep", grid=(bsz, 3 * HEADS),
        in_specs=[_slab(lp, Q0), pl.BlockSpec((KQ, DH), lambda b, s: (0, s))],
        out_specs=_slab(lp, 0),
        out_shape=jax.ShapeDtypeStruct((bsz * lp, 3 * GW), F32),
        compiler_params=_cparams(("parallel", "parallel")),
    )(proj, conv_qkv)


def _gdn_prep_bwd(proj, conv_qkv, dqkv, bsz, lp, padf):
    def body(p_ref, w_ref, d_ref, dp_ref, dw_ref):
        s, b = pl.program_id(0), pl.program_id(1)
        x = p_ref[...].astype(F32)
        w = w_ref[...]
        c = _conv_fwd(x, w, KQ, lp)
        y, r, scale = _qkv_act(c, s)
        dn = d_ref[...] * scale
        n = y * r
        dy_norm = r * (dn - n * jnp.sum(dn * n, axis=-1, keepdims=True))
        dyy = jnp.where(s < 2 * HEADS, dy_norm, dn)
        t = lax.broadcasted_iota(jnp.int32, (lp, 1), 0)
        dc = jnp.where(t >= padf, dyy * _dsilu(c), 0.0)
        dx, dw = _conv_bwd(x, w, dc, KQ, lp)
        dp_ref[...] = jnp.where(t >= padf, dx, 0.0).astype(BF16)

        @pl.when(b == 0)
        def _():
            dw_ref[...] = jnp.zeros_like(dw_ref)

        dw_ref[...] += dw

    sl = lambda col0: pl.BlockSpec((lp, DH), lambda s, b: (b, col0 // DH + s))
    return pl.pallas_call(
        body, name="gdn_prep_bwd", grid=(3 * HEADS, bsz),
        in_specs=[sl(Q0), pl.BlockSpec((KQ, DH), lambda s, b: (0, s)), sl(0)],
        out_specs=[sl(0), pl.BlockSpec((KQ, DH), lambda s, b: (0, s))],
        out_shape=[jax.ShapeDtypeStruct((bsz * lp, 3 * GW), BF16), jax.ShapeDtypeStruct((KQ, 3 * GW), F32)],
        compiler_params=_cparams(("parallel", "arbitrary")),
    )(proj, conv_qkv, dqkv)


def _softplus(x):
    e = jnp.exp(-jnp.abs(x))
    one_e = 1.0 + e
    l1p = jnp.where(one_e == 1.0, e, jnp.log(one_e) * (e / (one_e - 1.0)))
    return jnp.maximum(x, 0.0) + l1p


def _tri(n, kind):
    i = lax.broadcasted_iota(jnp.int32, (n, n), 0)
    j = lax.broadcasted_iota(jnp.int32, (n, n), 1)
    return {"incl": i >= j, "strict": i > j, "upper": i <= j}[kind]


def _gates(proj, alog_v, dtb_v, bsz, lp, padf):
    nchunk = lp // CH

    def body(p_ref, al_ref, dt_ref, gb_ref, cs_ref):
        x = p_ref[...].astype(F32)
        lane = lax.broadcasted_iota(jnp.int32, (lp, DH), 1)
        t = lax.broadcasted_iota(jnp.int32, (lp, DH), 0)
        beta = jax.nn.sigmoid(x)
        g = -jnp.exp(al_ref[...]) * _softplus(x + dt_ref[...])
        gb = jnp.where(lane < HEADS, beta, jnp.where(lane < 2 * HEADS, g, 0.0))
        gb_ref[...] = jnp.where(t >= padf, gb, 0.0)
        ltri = _tri(CH, "incl").astype(F32)

        def step(n, carry):
            r0 = pl.multiple_of(n * CH, CH)
            cs_ref[pl.ds(r0, CH), :] = _fdot(ltri, gb_ref[pl.ds(r0, CH), :])
            return carry

        lax.fori_loop(0, nchunk, step, 0)

    blk = pl.BlockSpec((lp, DH), lambda b: (b, 0))
    vec = pl.BlockSpec((1, DH), lambda b: (0, 0))
    out = pl.BlockSpec((lp, DH), lambda b: (b, 0))
    return pl.pallas_call(
        body, name="gates", grid=(bsz,),
        in_specs=[blk, vec, vec], out_specs=[out, out],
        out_shape=[jax.ShapeDtypeStruct((bsz * lp, DH), F32)] * 2,
        compiler_params=_cparams(("parallel",)),
    )(proj, alog_v, dtb_v)


def _gates_bwd(proj, alog_v, dtb_v, dbeta_b, dg_b, gb, bsz, lp, padf):
    def body(p_ref, al_ref, dt_ref, db_ref, dg_ref, gb_ref, dl_ref, dv_ref):
        b = pl.program_id(0)
        x = p_ref[...].astype(F32)
        lane = lax.broadcasted_iota(jnp.int32, (lp, DH), 1)
        t = lax.broadcasted_iota(jnp.int32, (lp, DH), 0)
        dbeta = jnp.zeros((lp, DH), F32)
        dg = jnp.zeros((lp, DH), F32)
        for h in range(HEADS):
            dbeta = jnp.where(lane == h, db_ref[h], dbeta)
            dg = jnp.where(lane == HEADS + h, dg_ref[h], dg)
        real = t >= padf
        dbeta = jnp.where(real, dbeta, 0.0)
        dg = jnp.where(real, dg, 0.0)
        sb = jax.nn.sigmoid(x)
        neg_ea = -jnp.exp(al_ref[...])
        dsp = dg * neg_ea * jax.nn.sigmoid(x + dt_ref[...])
        dl = dbeta * sb * (1.0 - sb) + dsp
        dl_ref[...] = jnp.concatenate([dl, jnp.zeros((lp, BA_W - DH), F32)], axis=1).astype(BF16)
        dalog = jnp.sum(dg * gb_ref[...], axis=0, keepdims=True)
        ddt = jnp.sum(dsp, axis=0, keepdims=True)

        @pl.when(b == 0)
        def _():
            dv_ref[...] = jnp.zeros_like(dv_ref)

        dv_ref[...] += jnp.concatenate([dalog, ddt], axis=0)

    blk = pl.BlockSpec((lp, DH), lambda b: (b, 0))
    vec = pl.BlockSpec((1, DH), lambda b: (0, 0))
    hb = pl.BlockSpec((HEADS, lp, DH), lambda b: (b, 0, 0))
    return pl.pallas_call(
        body, name="gates_bwd", grid=(bsz,),
        in_specs=[blk, vec, vec, hb, hb, pl.BlockSpec((lp, DH), lambda b: (b, 0))],
        out_specs=[pl.BlockSpec((lp, BA_W), lambda b: (b, 0)), pl.BlockSpec((2, DH), lambda b: (0, 0))],
        out_shape=[jax.ShapeDtypeStruct((bsz * lp, BA_W), BF16), jax.ShapeDtypeStruct((2, DH), F32)],
        compiler_params=_cparams(("arbitrary",)),
    )(proj, alog_v, dtb_v, dbeta_b, dg_b, gb)


def _mix_heads(proj, o, gdn_norm, conv_sc, bsz, lp, padf):
    def body(o_ref, z_ref, gn_ref, sx_ref, sb_ref, sc_ref, w_ref, og_ref, os_ref):
        og_ref[...] = (_rms(o_ref[...], gn_ref[...])[0] * _silu(z_ref[...].astype(F32))).astype(BF16)
        conv = _conv_fwd(sc_ref[...].astype(F32) * sx_ref[...].astype(F32), w_ref[...], KS, lp)
        t = lax.broadcasted_iota(jnp.int32, (lp, 1), 0)
        os_ref[...] = jnp.where(t >= padf, sb_ref[...].astype(F32) * conv, 0.0).astype(BF16)

    half = _slab(lp, 0)
    return pl.pallas_call(
        body, name="mix_heads", grid=(bsz, HEADS),
        in_specs=[half, _slab(lp, Z0), pl.BlockSpec((1, DH), lambda b, s: (0, 0)),
                  _slab(lp, SX0), _slab(lp, SB0), _slab(lp, SC0), pl.BlockSpec((KS, DH), lambda b, s: (0, s))],
        out_specs=[half, half],
        out_shape=[jax.ShapeDtypeStruct((bsz * lp, GW), BF16)] * 2,
        compiler_params=_cparams(("parallel", "parallel")),
    )(o, proj, gdn_norm, proj, proj, proj, conv_sc)


def _mix_heads_bwd(proj, o, gdn_norm, conv_sc, dmixin, bsz, lp, padf):
    def body(o_ref, z_ref, gn_ref, sx_ref, sb_ref, sc_ref, w_ref, dg_ref, ds_ref,
             do_ref, dz_ref, dsx_ref, dsb_ref, dsc_ref, dgn_ref, dw_ref):
        s, b = pl.program_id(0), pl.program_id(1)
        t = lax.broadcasted_iota(jnp.int32, (lp, 1), 0)
        real = t >= padf
        o, z, gn, d = o_ref[...], z_ref[...].astype(F32), gn_ref[...], dg_ref[...]
        sz = _silu(z)
        on = _rms(o, gn)[0]
        dz_ref[...] = (d * on * _dsilu(z)).astype(BF16)
        dox, dgn = _rms_bwd(o, gn, d * sz)
        do_ref[...] = dox
        sx, sb, sc = sx_ref[...].astype(F32), sb_ref[...].astype(F32), sc_ref[...].astype(F32)
        w, e = w_ref[...], ds_ref[...]
        e = jnp.where(real, e, 0.0)
        xin = sc * sx
        conv = _conv_fwd(xin, w, KS, lp)
        dsb_ref[...] = (e * conv).astype(BF16)
        dxin, dw = _conv_bwd(xin, w, e * sb, KS, lp)
        dxin = jnp.where(real, dxin, 0.0)
        dsx_ref[...] = (dxin * sc).astype(BF16)
        dsc_ref[...] = (dxin * sx).astype(BF16)

        @pl.when(jnp.logical_and(s == 0, b == 0))
        def _():
            dgn_ref[...] = jnp.zeros_like(dgn_ref)

        @pl.when(b == 0)
        def _():
            dw_ref[...] = jnp.zeros_like(dw_ref)

        dgn_ref[...] += jnp.sum(dgn, axis=0, keepdims=True)
        dw_ref[...] += dw

    sl = lambda col0: pl.BlockSpec((lp, DH), lambda s, b: (b, col0 // DH + s))
    half = sl(0)
    return pl.pallas_call(
        body, name="mix_heads_bwd", grid=(HEADS, bsz),
        in_specs=[half, sl(Z0), pl.BlockSpec((1, DH), lambda s, b: (0, 0)), sl(SX0), sl(SB0), sl(SC0),
                  pl.BlockSpec((KS, DH), lambda s, b: (0, s)), sl(0), sl(GW)],
        out_specs=[half] * 5 + [pl.BlockSpec((1, DH), lambda s, b: (0, 0)), pl.BlockSpec((KS, DH), lambda s, b: (0, s))],
        out_shape=[jax.ShapeDtypeStruct((bsz * lp, GW), F32)] + [jax.ShapeDtypeStruct((bsz * lp, GW), BF16)] * 4
        + [jax.ShapeDtypeStruct((1, DH), F32), jax.ShapeDtypeStruct((KS, SW), F32)],
        compiler_params=_cparams(("arbitrary", "arbitrary")),
    )(o, proj, gdn_norm, proj, proj, proj, conv_sc, dmixin, dmixin)


def _split_dot(a, b, dims=NN):
    ah = a.astype(BF16)
    al = (a - ah.astype(F32)).astype(BF16)
    bh = b.astype(BF16)
    bl = (b - bh.astype(F32)).astype(BF16)

    def d(x, y):
        return lax.dot_general(x, y, (dims, ((), ())), preferred_element_type=F32)

    return d(ah, bh) + (d(ah, bl) + d(al, bh))


def _eye(n):
    return (lax.broadcasted_iota(jnp.int32, (n, n), 0) == lax.broadcasted_iota(jnp.int32, (n, n), 1)).astype(F32)


def _each(f, *lists):
    return [f(*xs) for xs in zip(*lists)]


def _inv_unit_lower(a_list):
    eye = _eye(CH)
    p = _each(lambda a: -a, a_list)
    t = _each(lambda x: eye + x, p)
    p = _each(lambda x: _split_dot(x, x), p)
    for level in range(5):
        if level < 4:
            prod = _each(lambda tt, pp: _split_dot(jnp.concatenate([tt, pp], axis=0), pp), t, p)
            t = _each(lambda tt, pr: tt + pr[:CH], t, prod)
            p = _each(lambda pr: pr[CH:], prod)
        else:
            t = _each(lambda tt, pp: tt + _split_dot(tt, pp), t, p)
    return t


def _chunk_gates(gc_ref, be_ref, gt_ref, g):
    rows = pl.ds(g * CH, CH)
    gc = gc_ref[0, rows, :]
    be = be_ref[0, rows, :]
    gcl = gc_ref[0, pl.ds(g * CH + CH - 1, 1), :]
    gt = gt_ref[0, g]
    incl, upper = _tri(CH, "incl"), _tri(CH, "upper")
    dec = jnp.where(incl, jnp.exp(jnp.where(incl, gc[:, :CH] - gt, 0.0)), 0.0)
    dec_t = jnp.where(upper, jnp.exp(jnp.where(upper, gt - gc[:, :CH], 0.0)), 0.0)
    return rows, be, dec, dec_t, jnp.exp(gc), jnp.exp(gcl - gc), jnp.exp(gcl)


def _chunk_group(nchunk):
    return 3 if nchunk % 3 == 0 else 1


def _local_group(nchunk):
    return 11 if nchunk % 11 == 0 else _chunk_group(nchunk)


def _gdn_local_specs(lp, grp):
    ngrp = lp // (grp * CH)
    head = lambda col0: pl.BlockSpec((grp * CH, DH), lambda i, j: ((i // HEADS) * ngrp + j, col0 // DH + i % HEADS))
    lane_b = pl.BlockSpec((1, grp * CH, DH), lambda i, j: (i, j, 0))
    per_chunk = lambda r, c: pl.BlockSpec((1, grp, r, c), lambda i, j: (i, j, 0, 0))
    return ngrp, head, lane_b, per_chunk


def _gdn_local_fwd(qkv, gcb, bb, gct, bsz, lp):
    nchunk = lp // CH
    grp = _local_group(nchunk)
    ngrp, head, lane_b, per_chunk = _gdn_local_specs(lp, grp)
    bh = bsz * HEADS

    def body(q_ref, k_ref, v_ref, gc_ref, be_ref, gt_ref,
             u_ref, l1_ref, l2_ref, l3_ref, wt_ref, kd_ref, ti_ref, gl_ref):
        incl, strict, upper = _tri(CH, "incl"), _tri(CH, "strict"), _tri(CH, "upper")
        eye = _eye(CH)
        rows, be, dec, dec_t, eg, ekd, gl = zip(*[_chunk_gates(gc_ref, be_ref, gt_ref, g) for g in range(grp)])
        q = [q_ref[r, :] for r in rows]
        k = [k_ref[r, :] for r in rows]
        v = [v_ref[r, :] for r in rows]
        kb = _each(lambda x, y: x * y, k, be)
        a = _each(lambda x, y, d: jnp.where(strict, _bdot(x, y, NT) * d, 0.0), kb, k, dec)
        tinv = _inv_unit_lower(a)
        u = _each(lambda t, x, y: _bdot(t, x * y, NN), tinv, v, be)
        w = _each(lambda t, x, y: _bdot(t, x * y, NN), tinv, kb, eg)
        qk = _each(lambda x, y, d: jnp.where(incl, _bdot(x, y, NT) * d, 0.0), q, k, dec)
        qk_t = _each(lambda x, y, d: jnp.where(upper, _bdot(y, x, NT) * d, 0.0), q, k, dec_t)
        qd = _each(lambda x, y: x * y, q, eg)
        kd = _each(lambda x, y: x * y, k, ekd)
        kd_t = _each(lambda x: _bdot(x, eye, TN), kd)
        qd_t = _each(lambda x: _bdot(x, eye, TN), qd)
        w_t = _each(lambda x: _bdot(x, eye, TN), w)
        for g in range(grp):
            u_ref[rows[g], :] = u[g]
            l1_ref[0, g] = jnp.concatenate([w[g], qd[g]], axis=0).astype(BF16)
            l2_ref[0, g] = jnp.concatenate([qk[g], kd_t[g]], axis=0).astype(BF16)
            l3_ref[0, g] = jnp.concatenate([qk_t[g], qd_t[g]], axis=0).astype(BF16)
            wt_ref[0, g] = w_t[g].astype(BF16)
            kd_ref[0, g] = kd[g].astype(BF16)
            ti_ref[0, g] = tinv[g]
            gl_ref[0, g] = gl[g]

    r = bsz * lp
    shapes = [((r, GW), F32), ((bh, nchunk, 2 * CH, DH), BF16), ((bh, nchunk, 3 * CH, CH), BF16),
              ((bh, nchunk, 3 * CH, CH), BF16), ((bh, nchunk, DH, CH), BF16), ((bh, nchunk, CH, DH), BF16),
              ((bh, nchunk, CH, CH), F32), ((bh, nchunk, 1, DH), F32)]
    return pl.pallas_call(
        body, name="gdn_local_fwd", grid=(bh, ngrp),
        in_specs=[head(0), head(GW), head(2 * GW), lane_b, lane_b, per_chunk(1, CH)],
        out_specs=[head(0), per_chunk(2 * CH, DH), per_chunk(3 * CH, CH), per_chunk(3 * CH, CH), per_chunk(DH, CH),
                   per_chunk(CH, DH), per_chunk(CH, CH), per_chunk(1, DH)],
        out_shape=[jax.ShapeDtypeStruct(s, dt) for s, dt in shapes],
        compiler_params=_cparams(("parallel", "parallel")),
    )(qkv, qkv, qkv, gcb, bb, gct)


def _scan_specs(bsz, lp, grp, order):
    ngrp = lp // (grp * CH)
    bh = bsz * HEADS
    seq = lambda: pl.BlockSpec((bsz, grp * CH, GW), lambda s: (0, order(s, ngrp), 0))
    per_chunk = lambda r, c: pl.BlockSpec((bh, grp, r, c), lambda s: (0, order(s, ngrp), 0, 0))
    return ngrp, bh, seq, per_chunk


def _gdn_scan_fwd(u, l1, l2, gl, bsz, lp):
    nchunk = lp // CH
    grp = _chunk_group(nchunk)
    ngrp, bh, seq, per_chunk = _scan_specs(bsz, lp, grp, lambda s, n: s)

    def body(u_ref, l1_ref, l2_ref, gl_ref, o_ref, vn_ref, st_ref, s_scr):
        @pl.when(pl.program_id(0) == 0)
        def _():
            s_scr[...] = jnp.zeros_like(s_scr)

        for c in range(grp):
            rows = slice(c * CH, (c + 1) * CH)
            for i in range(bh):
                b, h = divmod(i, HEADS)
                cols = slice(h * DH, (h + 1) * DH)
                s = s_scr[i]
                st_ref[i, c] = s
                x = jnp.dot(l1_ref[i, c], s.astype(BF16), preferred_element_type=F32)
                vnew = u_ref[b, rows, cols] - x[:CH]
                y = jnp.dot(l2_ref[i, c], vnew.astype(BF16), preferred_element_type=F32)
                o_ref[b, rows, cols] = x[CH:] + y[:CH]
                vn_ref[b, rows, cols] = vnew.astype(BF16)
                s_scr[i] = s * gl_ref[i, c] + y[CH:]

    return pl.pallas_call(
        body, name="gdn_scan_fwd", grid=(ngrp,),
        in_specs=[seq(), per_chunk(2 * CH, DH), per_chunk(3 * CH, CH), per_chunk(1, DH)],
        out_specs=[seq(), seq(), per_chunk(DH, DH)],
        out_shape=[jax.ShapeDtypeStruct((bsz, lp, GW), F32), jax.ShapeDtypeStruct((bsz, lp, GW), BF16),
                   jax.ShapeDtypeStruct((bh, nchunk, DH, DH), F32)],
        scratch_shapes=[pltpu.VMEM((bh, DH, DH), F32)],
        compiler_params=_cparams(("arbitrary",)),
    )(u.reshape(bsz, lp, GW), l1, l2, gl)


def _gdn_scan_bwd(d_o, l3, wt, kd, gl, bsz, lp):
    nchunk = lp // CH
    grp = _chunk_group(nchunk)
    ngrp, bh, seq, per_chunk = _scan_specs(bsz, lp, grp, lambda s, n: n - 1 - s)

    def body(do_ref, l3_ref, wt_ref, kd_ref, gl_ref, dvn_ref, dst_ref, ds_scr):
        @pl.when(pl.program_id(0) == 0)
        def _():
            ds_scr[...] = jnp.zeros_like(ds_scr)

        for c in reversed(range(grp)):
            rows = slice(c * CH, (c + 1) * CH)
            for i in range(bh):
                b, h = divmod(i, HEADS)
                cols = slice(h * DH, (h + 1) * DH)
                ds = ds_scr[i]
                dst_ref[i, c] = ds
                p = jnp.dot(l3_ref[i, c], do_ref[b, rows, cols].astype(BF16), preferred_element_type=F32)
                dvn = p[:CH] + jnp.dot(kd_ref[i, c], ds.astype(BF16), preferred_element_type=F32)
                dvn_ref[b, rows, cols] = dvn.astype(BF16)
                ds_scr[i] = ds * gl_ref[i, c] + p[CH:] - jnp.dot(wt_ref[i, c], dvn.astype(BF16),
                                                                 preferred_element_type=F32)

    return pl.pallas_call(
        body, name="gdn_scan_bwd", grid=(ngrp,),
        in_specs=[seq(), per_chunk(3 * CH, CH), per_chunk(DH, CH), per_chunk(CH, DH), per_chunk(1, DH)],
        out_specs=[seq(), per_chunk(DH, DH)],
        out_shape=[jax.ShapeDtypeStruct((bsz, lp, GW), BF16), jax.ShapeDtypeStruct((bh, nchunk, DH, DH), F32)],
        scratch_shapes=[pltpu.VMEM((bh, DH, DH), F32)],
        compiler_params=_cparams(("arbitrary",)),
    )(d_o.reshape(bsz, lp, GW), l3, wt, kd, gl)


def _gdn_local_bwd(qkv, gcb, bb, gct, tinv_all, states, dstates, vnew, dvnew, d_o, bsz, lp):
    nchunk = lp // CH
    grp = _local_group(nchunk)
    ngrp, head, lane_b, per_chunk = _gdn_local_specs(lp, grp)
    bh = bsz * HEADS

    def body(q_ref, k_ref, v_ref, gc_ref, be_ref, gt_ref, ti_ref, st_ref, dst_ref, vn_ref, dvn_ref, do_ref,
             dq_ref, dk_ref, dv_ref, dg_ref, db_ref):
        incl, strict = _tri(CH, "incl"), _tri(CH, "strict")
        upper = _tri(CH, "upper").astype(F32)
        ones = jnp.ones((CH, DH), F32)
        last = lax.broadcasted_iota(jnp.int32, (CH, 1), 0) == CH - 1

        def rsum(x):
            return jnp.sum(x, axis=-1, keepdims=True)

        def mul(xs, ys):
            return _each(lambda x, y: x * y, xs, ys)

        rows, be, dec, _, eg, ekd, gl = zip(*[_chunk_gates(gc_ref, be_ref, gt_ref, g) for g in range(grp)])
        q = [q_ref[r, :] for r in rows]
        k = [k_ref[r, :] for r in rows]
        v = [v_ref[r, :] for r in rows]
        vnew = [vn_ref[r, :] for r in rows]
        dvn = [dvn_ref[r, :] for r in rows]
        do = [do_ref[r, :] for r in rows]
        tinv = [ti_ref[0, g] for g in range(grp)]
        s = [st_ref[0, g] for g in range(grp)]
        ds = [dst_ref[0, g] for g in range(grp)]
        kb, vb, qd, kd = mul(k, be), mul(v, be), mul(q, eg), mul(k, ekd)
        kbg = mul(kb, eg)
        a = _each(lambda x, y, d: jnp.where(strict, _bdot(x, y, NT) * d, 0.0), kb, k, dec)
        qk = _each(lambda x, y, d: jnp.where(incl, _bdot(x, y, NT) * d, 0.0), q, k, dec)
        dqk = _each(lambda x, y: jnp.where(incl, _bdot(x, y, NT), 0.0), do, vnew)
        dqd = _each(lambda x, y: _bdot(x, y, NT), do, s)
        dkd = _each(lambda x, y: _bdot(x, y, NT), vnew, ds)
        dgl = _each(lambda x, y: jnp.sum(rsum(x * y), axis=0, keepdims=True), s, ds)
        dw = _each(lambda x, y: -_bdot(x, y, NT), dvn, s)
        dvb = _each(lambda t, x: _bdot(t, x, TN), tinv, dvn)
        dkbg = _each(lambda t, x: _bdot(t, x, TN), tinv, dw)
        dt = _each(lambda x, y, z, w: _bdot(x, y, NT) + _bdot(z, w, NT), dvn, vb, dw, kbg)
        tdt = _each(lambda t, x: _split_dot(t, x, TN), tinv, dt)
        da = _each(lambda x, t: jnp.where(strict, -_split_dot(x, t, NT), 0.0), tdt, tinv)
        dp, dr = mul(da, dec), mul(dqk, dec)
        dkb = _each(lambda x, y, z, e: _bdot(x, y, NN) + z * e, dp, k, dkbg, eg)
        dk = _each(lambda p_, kb_, r_, q_, dkd_, ekd_, dkb_, be_:
                   _bdot(p_, kb_, TN) + _bdot(r_, q_, TN) + dkd_ * ekd_ + dkb_ * be_,
                   dp, kb, dr, q, dkd, ekd, dkb, be)
        dq = _each(lambda r_, k_, dqd_, eg_: _bdot(r_, k_, NN) + dqd_ * eg_, dr, k, dqd, eg)
        m = _each(lambda da_, a_, dqk_, qk_: da_ * a_ + dqk_ * qk_, da, a, dqk, qk)
        m_cols = _each(lambda x: _split_dot(x, ones, TN), m)
        kd_term = _each(lambda x, y: rsum(x * y), dkd, kd)
        dgc = _each(lambda m_, mc, dqd_, qd_, kt, dkbg_, kbg_, dgl_, gl_:
                    rsum(m_) - mc + rsum(dqd_ * qd_) - kt + rsum(dkbg_ * kbg_)
                    + jnp.where(last, jnp.sum(kt, axis=0, keepdims=True) + dgl_ * gl_, 0.0),
                    m, m_cols, dqd, qd, kd_term, dkbg, kbg, dgl, gl)
        dg = _each(lambda x: _split_dot(upper, x, NN), dgc)
        for g in range(grp):
            dq_ref[rows[g], :] = dq[g]
            dk_ref[rows[g], :] = dk[g]
            dv_ref[rows[g], :] = dvb[g] * be[g]
            dg_ref[0, rows[g], :] = dg[g]
            db_ref[0, rows[g], :] = jnp.broadcast_to(rsum(dkb[g] * k[g]) + rsum(dvb[g] * v[g]), (CH, DH))

    r = bsz * lp
    return pl.pallas_call(
        body, name="gdn_local_bwd", grid=(bh, ngrp),
        in_specs=[head(0), head(GW), head(2 * GW), lane_b, lane_b, per_chunk(1, CH), per_chunk(CH, CH),
                  per_chunk(DH, DH), per_chunk(DH, DH), head(0), head(0), head(0)],
        out_specs=[head(0), head(0), head(0), lane_b, lane_b],
        out_shape=[jax.ShapeDtypeStruct((r, GW), F32)] * 3 + [jax.ShapeDtypeStruct((bh, lp, DH), F32)] * 2,
        compiler_params=_cparams(("parallel", "parallel")),
    )(qkv, qkv, qkv, gcb, bb, gct, tinv_all, states, dstates, vnew, dvnew, d_o)


def _place():
    return lax.axis_index("x"), lax.axis_index("y"), lax.axis_index("c")


def _any_specs(n):
    return [pl.BlockSpec(memory_space=pl.ANY)] * n


def _all_gather(name, shards):
    n = len(shards)

    def body(*refs):
        ins, outs = refs[:n], refs[n:2 * n]
        send_sems, recv_sems, local_sems = refs[2 * n:]
        x, y, c = _place()
        me, sibling = (x, y, c), (x, y, 1 - c)
        chips = [(1 - x, y), (x, 1 - y), (1 - x, 1 - y)]

        def slot(i, p):
            return outs[i].at[4 * p[0] + 2 * p[1] + p[2]]

        def copy(i, k, block, to, src=None):
            return pltpu.make_async_remote_copy(
                src_ref=slot(i, block) if src is None else src, dst_ref=slot(i, block),
                send_sem=send_sems.at[i, k], recv_sem=recv_sems.at[i, k], device_id=to, device_id_type=MESH)

        mine = [pltpu.make_async_copy(ins[i], slot(i, me), local_sems.at[i]) for i in range(n)]
        for cp in mine:
            cp.start()
        first = []
        for i in range(n):
            first.append(copy(i, 0, me, sibling, src=ins[i]))
            first += [copy(i, 1 + j, me, (*chip, c), src=ins[i]) for j, chip in enumerate(chips)]
        for cp in first:
            cp.start()
        passed = []
        for j, chip in enumerate(chips):
            for i in range(n):
                copy(i, 1 + j, (*chip, c), me).wait_recv()
                fwd = copy(i, 4 + j, (*chip, c), sibling)
                fwd.start()
                passed.append(fwd)
        for i in range(n):
            copy(i, 0, sibling, me).wait_recv()
            for j, chip in enumerate(chips):
                copy(i, 4 + j, (*chip, 1 - c), me).wait_recv()
        for cp in first + passed:
            cp.wait_send()
        for cp in mine:
            cp.wait()

    return pl.pallas_call(
        body, name=name,
        in_specs=_any_specs(n), out_specs=_any_specs(n),
        out_shape=[jax.ShapeDtypeStruct((N_DEV,) + s.shape, s.dtype) for s in shards],
        scratch_shapes=[pltpu.SemaphoreType.DMA((n, 7)), pltpu.SemaphoreType.DMA((n, 7)),
                        pltpu.SemaphoreType.DMA((n,))],
    )(*shards)


HBM_SPEC = pl.BlockSpec(memory_space=pltpu.HBM)
SEM_SPEC = pl.BlockSpec(memory_space=pltpu.SEMAPHORE)
EFFECT = pltpu.SideEffectType.DATAFLOW_SIDE_EFFECTING
N_PEER = N_DEV - 1


def _peer(r):
    x, y, c = _place()
    return ((1 - x) if r & 4 else x, (1 - y) if r & 2 else y, (1 - c) if r & 1 else c)


def _slot_of(p):
    return 4 * p[0] + 2 * p[1] + p[2]


def _hbm(a):
    return pltpu.with_memory_space_constraint(a, pltpu.HBM)


def _gather_pattern(src, land):
    me = _place()
    return [(src, land.at[_slot_of(me)], _peer(r), land.at[_slot_of(_peer(r))]) for r in range(1, N_DEV)]


def _sibling_pattern(src, land):
    x, y, c = _place()
    return [(src.at[1 - c], land, (x, y, 1 - c), land)]


def _chip_pattern(src, land):
    x, y, c = _place()
    chips = [(1 - x, y), (x, 1 - y), (1 - x, 1 - y)]
    return [(src.at[2 * ch[0] + ch[1]], land.at[j], (*ch, c), land.at[j]) for j, ch in enumerate(chips)]


def _split_start(name, groups, pattern, land_shape):
    sizes = [len(g) for g in groups]
    arrays = [s for g in groups for s in g]
    n = len(arrays)
    land_shapes = [land_shape(s.shape) for s in arrays]

    def body(*refs):
        srcs, lands = refs[:n], refs[n:2 * n]
        sems = refs[2 * n:2 * n + 2 * len(groups)]
        token = refs[-1]
        k = 0
        for gi, size in enumerate(sizes):
            for i in range(size):
                copies = pattern(srcs[k], lands[k])
                for j, (s, d, peer, _) in enumerate(copies):
                    pltpu.make_async_remote_copy(
                        src_ref=s, dst_ref=d, send_sem=sems[2 * gi].at[i * len(copies) + j],
                        recv_sem=sems[2 * gi + 1].at[i * len(copies) + j], device_id=peer, device_id_type=MESH).start()
                k += 1
        token[...] = jnp.zeros_like(token)

    per_array = {_gather_pattern: N_PEER, _sibling_pattern: 1, _chip_pattern: 3}[pattern]
    sem_shapes = [pltpu.SemaphoreType.DMA((size * per_array,)) for size in sizes for _ in range(2)]
    res = pl.pallas_call(
        body, name=name,
        in_specs=[HBM_SPEC] * (2 * n),
        out_specs=[SEM_SPEC] * len(sem_shapes) + [HBM_SPEC] * (2 * n) + [pl.BlockSpec(memory_space=pltpu.VMEM)],
        out_shape=sem_shapes + [pltpu.HBM(s.shape, s.dtype) for s in arrays]
        + [pltpu.HBM(ls, s.dtype) for ls, s in zip(land_shapes, arrays)] + [jax.ShapeDtypeStruct((8, DH), F32)],
        input_output_aliases={i: len(sem_shapes) + i for i in range(2 * n)},
        compiler_params=pltpu.CompilerParams(has_side_effects=EFFECT),
    )(*[_hbm(s) for s in arrays], *[_hbm(lax.empty(ls, s.dtype)) for ls, s in zip(land_shapes, arrays)])
    nsem = len(sem_shapes)
    out, k = [], 0
    for gi, size in enumerate(sizes):
        out.append((res[2 * gi], res[2 * gi + 1], list(res[nsem + k:nsem + k + size]),
                    list(res[nsem + n + k:nsem + n + k + size])))
        k += size
    return out, res[-1]


def _split_wait(name, group, pattern, after):
    send_sems, recv_sems, arrays, lands = group
    n = len(arrays)

    def body(*refs):
        srcs, lands_ = refs[:n], refs[n:2 * n]
        s_sems, r_sems = refs[2 * n], refs[2 * n + 1]
        for i in range(n):
            copies = pattern(srcs[i], lands_[i])
            for j, (s, _, peer, filled) in enumerate(copies):
                cp = pltpu.make_async_remote_copy(
                    src_ref=s, dst_ref=filled, send_sem=s_sems.at[i * len(copies) + j],
                    recv_sem=r_sems.at[i * len(copies) + j], device_id=peer, device_id_type=MESH)
                cp.wait_send()
                cp.wait_recv()

    res = pl.pallas_call(
        body, name=name,
        in_specs=[HBM_SPEC] * (2 * n) + [SEM_SPEC, SEM_SPEC, pl.BlockSpec(memory_space=pl.ANY)],
        out_specs=[HBM_SPEC] * (2 * n),
        out_shape=[pltpu.HBM(s.shape, s.dtype) for s in arrays] + [pltpu.HBM(l.shape, l.dtype) for l in lands],
        input_output_aliases={i: i for i in range(2 * n)},
        compiler_params=pltpu.CompilerParams(has_side_effects=EFFECT),
    )(*arrays, *lands, send_sems, recv_sems, after)
    return list(res[:n]), list(res[n:])


def _pair_sum(name, grads, recv, core):
    _, _, r, c = grads.shape
    tr = _div_tile(r, 256, 16) if r % 16 == 0 else r

    def body(core_ref, g_ref, r_ref, o_ref):
        o_ref[...] = (g_ref[0].astype(F32) + r_ref[...].astype(F32)).astype(o_ref.dtype)

    return pl.pallas_call(
        body, name=name,
        grid_spec=pltpu.PrefetchScalarGridSpec(
            num_scalar_prefetch=1, grid=(4, r // tr),
            in_specs=[pl.BlockSpec((1, 1, tr, c), lambda s, i, cr: (cr[0], s, i, 0)),
                      pl.BlockSpec((1, tr, c), lambda s, i, cr: (s, i, 0))],
            out_specs=pl.BlockSpec((1, tr, c), lambda s, i, cr: (s, i, 0))),
        out_shape=jax.ShapeDtypeStruct(recv.shape, recv.dtype),
        compiler_params=_cparams(("parallel", "parallel")),
    )(core, grads, recv)


def _adamw_math(w, g, m, v):
    m = ADAM_B1 * m + (1.0 - ADAM_B1) * g
    v = ADAM_B2 * v + (1.0 - ADAM_B2) * (g * g)
    m_hat = m / (1.0 - ADAM_B1 ** ADAM_STEP)
    v_hat = v / (1.0 - ADAM_B2 ** ADAM_STEP)
    delta = -ADAM_LR * (m_hat / (jnp.sqrt(v_hat) + ADAM_EPS) + ADAM_WD * w)
    return delta, m, v


def _adamw(name, parts, w, m, v):
    nslot, r, c = parts.shape
    tr = _div_tile(r, 256, 8) if r % 8 == 0 else r

    def body(p_ref, w_ref, m_ref, v_ref, g_out, d_out, m_out, v_out):
        g = p_ref[0]
        for s in range(1, nslot):
            g = g + p_ref[s]
        d, mn, vn = _adamw_math(w_ref[...], g, m_ref[...], v_ref[...])
        g_out[...] = g
        d_out[...] = d
        m_out[...] = mn
        v_out[...] = vn

    blk = pl.BlockSpec((tr, c), lambda i: (i, 0))
    return pl.pallas_call(
        body, name=name, grid=(r // tr,),
        in_specs=[pl.BlockSpec((nslot, tr, c), lambda i: (0, i, 0)), blk, blk, blk],
        out_specs=[blk] * 4,
        out_shape=[jax.ShapeDtypeStruct((r, c), F32)] * 4,
        compiler_params=_cparams(("parallel",)),
    )(parts, w, m, v)


def _adamw_reduced(name, parts, chip, recv, w, m, v):
    _, r, c = parts.shape
    tr = _div_tile(r, 256, 16) if r % 16 == 0 else r

    def body(chip_ref, p_ref, r_ref, w_ref, m_ref, v_ref, g_out, d_out, m_out, v_out):
        g = p_ref[0].astype(F32)
        for s in range(3):
            g = g + r_ref[s].astype(F32)
        d, mn, vn = _adamw_math(w_ref[...], g, m_ref[...], v_ref[...])
        g_out[...] = g
        d_out[...] = d
        m_out[...] = mn
        v_out[...] = vn

    blk = pl.BlockSpec((tr, c), lambda i, ch: (i, 0))
    return pl.pallas_call(
        body, name=name,
        grid_spec=pltpu.PrefetchScalarGridSpec(
            num_scalar_prefetch=1, grid=(r // tr,),
            in_specs=[pl.BlockSpec((1, tr, c), lambda i, ch: (ch[0], i, 0)),
                      pl.BlockSpec((3, tr, c), lambda i, ch: (0, i, 0)), blk, blk, blk],
            out_specs=[blk] * 4),
        out_shape=[jax.ShapeDtypeStruct((r, c), F32)] * 4,
        compiler_params=_cparams(("parallel",)),
    )(chip, parts, recv, w, m, v)


SMALL_ROWS, SMALL_COLS = 32, 3 * GW


def _pack_small(gains4, meta, conv_qkv, conv_sc, a_log, dt_bias, gdn_norm):
    def padc(a):
        return jnp.pad(a, ((0, 0), (0, SMALL_COLS - a.shape[1])))
    scal = jnp.concatenate([jnp.pad(a_log, ((0, 0), (0, DH - HEADS))), jnp.pad(dt_bias, ((0, 0), (0, DH - HEADS))),
                            gdn_norm], axis=1)
    rows = [padc(g) for g in gains4] + [padc(meta), padc(conv_qkv), padc(conv_sc), padc(scal)]
    slab = jnp.concatenate(rows, axis=0)
    return jnp.pad(slab, ((0, SMALL_ROWS - slab.shape[0]), (0, 0)))


def _unpack_small(slab):
    gains = [slab[i:i + 1, :D] for i in range(4)]
    meta = slab[4:20, :D]
    conv_qkv = slab[20:24, :]
    conv_sc = slab[24:27, :SW]
    a_log = slab[27:28, 0:HEADS]
    dt_bias = slab[27:28, DH:DH + HEADS]
    gdn_norm = slab[27:28, 2 * DH:3 * DH]
    return gains, meta, conv_qkv, conv_sc, a_log, dt_bias, gdn_norm


def kernel(x, meta_tokens, mix_pre_norm, mix_post_norm, ffn_pre_norm, ffn_post_norm, w_in, conv_qkv, a_log, dt_bias, gdn_norm, conv_sc, w_out, w_gate, w_up, w_down, loss_target, m_meta_tokens, m_mix_pre_norm, m_mix_post_norm, m_ffn_pre_norm, m_ffn_post_norm, m_w_in, m_conv_qkv, m_a_log, m_dt_bias, m_gdn_norm, m_conv_sc, m_w_out, m_w_gate, m_w_up, m_w_down, v_meta_tokens, v_mix_pre_norm, v_mix_post_norm, v_ffn_pre_norm, v_ffn_post_norm, v_w_in, v_conv_qkv, v_a_log, v_dt_bias, v_gdn_norm, v_conv_sc, v_w_out, v_w_gate, v_w_up, v_w_down):
    bsz, seq, _ = x.shape
    lp = -(-(N_META + seq) // CH) * CH
    padf = lp - N_META - seq
    first_real = padf + N_META
    rows = bsz * lp
    nchunk = lp // CH
    tr = _div_tile(lp, 528)
    tm = _div_tile(rows, 1056)
    cx, cy, cc = _place()
    dev = 4 * cx + 2 * cy + cc

    small_shard = jnp.concatenate([
        jnp.pad(meta_tokens, ((0, 0), (0, 256 - DH))),
        jnp.pad(conv_qkv[0], ((0, 0), (0, 256 - 3 * GW // N_DEV))),
        jnp.pad(conv_sc[0], ((0, 5), (0, 256 - SW // N_DEV)))], axis=0)
    g_in, g_small = _all_gather("gather_w_in", [w_in[0].T.astype(BF16), small_shard])
    my_out, my_ffn = [w_out[0].astype(BF16)], [w_gate[0].astype(BF16), w_up[0].astype(BF16), w_down[0].astype(BF16)]
    (grp_out, grp_ffn), token = _split_start("gather_rest_start", [my_out, my_ffn], _gather_pattern,
                                             lambda s: (N_DEV,) + s)
    after_start = token[0:1, 0:1]

    def cols_full(g):
        return jnp.transpose(g, (1, 0, 2)).reshape(g.shape[1], -1)

    def with_mine(land, mine):
        return lax.dynamic_update_index_in_dim(land, mine, dev, 0)

    win_t = g_in.reshape(IN_W, D)
    win_t = jnp.concatenate([win_t[:BA0 + 2 * HEADS], jnp.zeros((BA_W - 2 * HEADS, D), BF16),
                             win_t[BA0 + 2 * HEADS:]], axis=0)
    meta_full = cols_full(g_small[:, 0:16, :DH])
    convq_full = cols_full(g_small[:, 16:20, :3 * GW // N_DEV])
    convs_full = cols_full(g_small[:, 20:23, :SW // N_DEV])
    alog_v = jnp.pad(a_log, ((0, 0), (HEADS, DH - 2 * HEADS)))
    dtb_v = jnp.pad(dt_bias, ((0, 0), (HEADS, DH - 2 * HEADS)))

    h0 = jnp.concatenate([jnp.zeros((bsz, padf, D), F32), jnp.broadcast_to(meta_full[None], (bsz, N_META, D)), x],
                         axis=1).reshape(rows, D)
    tgt = jnp.pad(loss_target, ((0, 0), (first_real, 0), (0, 0))).reshape(rows, D)
    u1 = _pre_norm(h0, mix_pre_norm + after_start, bsz, lp, tr)
    (proj,) = _mm("proj_in", [(u1, win_t)], "nt", tm, 768, D, out_dtypes=(BF16,))
    (ba,) = _mm("proj_gate_logits", [(u1, win_t[BA0:SX0])], "nt", tm, BA_W, D)
    qkv = _gdn_prep(proj, convq_full, bsz, lp, padf)
    gb, gcs = _gates(ba, alog_v, dtb_v, bsz, lp, padf)

    def lane_bcast(cols):
        t = jnp.transpose(cols.reshape(bsz, lp, HEADS), (0, 2, 1)).reshape(bsz * HEADS, lp, 1)
        return jnp.broadcast_to(t, (bsz * HEADS, lp, DH))

    bb = lane_bcast(gb[:, 0:HEADS])
    gcb = lane_bcast(gcs[:, HEADS:2 * HEADS])
    gct = jnp.transpose(gcs[:, HEADS:2 * HEADS].reshape(bsz, nchunk, CH, HEADS), (0, 3, 1, 2)).reshape(
        bsz * HEADS, nchunk, 1, CH)
    u_loc, l1, l2, l3, w_t, k_dec, tinv_all, g_last = _gdn_local_fwd(qkv, gcb, bb, gct, bsz, lp)
    o, v_new, states = _gdn_scan_fwd(u_loc, l1, l2, g_last, bsz, lp)
    o, v_new = o.reshape(rows, GW), v_new.reshape(rows, GW)
    mix_g, mix_s = _mix_heads(proj, o, gdn_norm, convs_full, bsz, lp, padf)
    mixin = jnp.concatenate([mix_g, mix_s], axis=1)
    my_out, (l_out,) = _split_wait("gather_w_out_wait", grp_out, _gather_pattern, mixin)
    wout = with_mine(l_out, my_out[0]).reshape(D, D)
    (mix,) = _mm("proj_out", [(mixin, wout)], "nn", tm, 512, D)
    my_ffn, (l_gate, l_up, l_down) = _split_wait("gather_ffn_wait", grp_ffn, _gather_pattern, mix)
    wgate, wup = cols_full(with_mine(l_gate, my_ffn[0])), cols_full(with_mine(l_up, my_ffn[1]))
    wdown = with_mine(l_down, my_ffn[2]).reshape(FF, D)
    h1, u2 = _mid_norms(h0, mix, mix_post_norm, ffn_pre_norm, bsz, lp, tr)
    gate, up, act = _ffn_up(u2, wgate, wup, _div_tile(rows, 528), 1408)
    (ffn,) = _mm("ffn_down", [(act, wdown)], "nn", tm, 512, 1408)

    dy, dffn, d_g4, loss_part = _loss_head(h1, ffn, tgt, ffn_post_norm, bsz, lp, tr, first_real)
    loss = lax.psum(loss_part[0, 0], ("x", "y", "c"))
    (dwdown,) = _mm("dw_down", [(act, dffn)], "tn", 1408, 512, tm, out_dtypes=(BF16,))

    def swiglu_bwd(da, g, u):
        g, u = g.astype(F32), u.astype(F32)
        return da * u * _dsilu(g), da * _silu(g)

    dgate, dup = _mm("d_act", [(dffn, wdown)], "nt", tm, 1408, D, out_dtypes=(BF16, BF16),
                     epilogue=swiglu_bwd, extras=(gate, up))
    (dwgate,) = _mm("dw_gate", [(u2, dgate)], "tn", D, 1408, tm, out_dtypes=(BF16,))
    (dwup,) = _mm("dw_up", [(u2, dup)], "tn", D, 1408, tm, out_dtypes=(BF16,))
    (du2,) = _mm("d_u2", [(dgate, wgate), (dup, wup)], "nt", tm, 512, 1408)
    dh1, dmix, d_g3, d_g2 = _mid_norms_bwd(h1, mix, du2, dy, ffn_pre_norm, mix_post_norm, bsz, lp, tr)
    (dwout,) = _mm("dw_out", [(mixin, dmix)], "tn", D, 512, tm, out_dtypes=(BF16,))

    def col_blocks(g):
        r = g.shape[0]
        return jnp.transpose(g.reshape(r, 4, 2, -1), (2, 1, 0, 3))

    def row_blocks(g):
        c = g.shape[1]
        return jnp.transpose(g.reshape(4, 2, -1, c), (1, 0, 2, 3))

    core = jnp.reshape(cc, (1,)).astype(jnp.int32)
    chip = jnp.reshape(2 * cx + cy, (1,)).astype(jnp.int32)

    def reduce_start(tag, grads):
        (grp,), tok = _split_start("rs_sibling_start_" + tag, [grads], _sibling_pattern, lambda s: s[1:])
        return grp, tok

    def reduce_mid(tag, grp, nms, after):
        mine, lands = _split_wait("rs_sibling_wait_" + tag, grp, _sibling_pattern, after)
        parts = [_pair_sum("pair_sum_" + nm, g, r, core) for nm, g, r in zip(nms, mine, lands)]
        (grp2,), tok = _split_start("rs_chip_start_" + tag, [parts], _chip_pattern, lambda s: (3,) + s[1:])
        return grp2, tok

    def reduce_end(tag, grp2, after):
        return _split_wait("rs_chip_wait_" + tag, grp2, _chip_pattern, after)

    names_ffn = ["w_down", "w_gate", "w_up", "w_out"]
    rs_ffn, tok = reduce_start("ffn", [row_blocks(dwdown), col_blocks(dwgate), col_blocks(dwup), row_blocks(dwout)])
    (dmixin,) = _mm("d_mixin", [(dmix, wout)], "nt", tm, 512, D, after=tok)
    d_o, dz, dsx, dsb, dsc, d_gn, d_convs = _mix_heads_bwd(proj, o, gdn_norm, convs_full, dmixin, bsz, lp, padf)
    rs_ffn, tok = reduce_mid("ffn", rs_ffn, names_ffn, d_o)
    dv_new, dstates = _gdn_scan_bwd(d_o, l3, w_t, k_dec, g_last + tok[0, 0], bsz, lp)
    dq, dk, dv, dg_b, dbeta_b = _gdn_local_bwd(qkv, gcb, bb, gct, tinv_all, states, dstates, v_new,
                                               dv_new.reshape(rows, GW), d_o, bsz, lp)
    dqkv = jnp.concatenate([dq, dk, dv], axis=1)
    dpqkv, d_convq = _gdn_prep_bwd(proj, convq_full, dqkv, bsz, lp, padf)
    dba, d_scal = _gates_bwd(ba, alog_v, dtb_v, dbeta_b, dg_b, gb, bsz, lp, padf)
    dproj = jnp.concatenate([dpqkv, dz, dba, dsx, dsb, dsc], axis=1)
    (dwin_t,) = _mm("dw_in", [(dproj, u1)], "tn", 768, D, tm, out_dtypes=(BF16,))
    parts_ffn, recv_ffn = reduce_end("ffn", rs_ffn, dwin_t)
    dwin_t = jnp.concatenate([dwin_t[:BA0 + 2 * HEADS], dwin_t[SX0:]], axis=0)
    rs_in, tok = reduce_start("in", [row_blocks(dwin_t)])
    (du1,) = _mm("d_u1", [(dproj, win_t)], "nn", tm, D, 768, after=tok)
    rs_in, tok = reduce_mid("in", rs_in, ["w_in"], du1)
    dh0, d_g1 = _pre_norm_bwd(h0, du1, dh1, mix_pre_norm + tok[0:1, 0:1], bsz, lp, tr)
    dh0 = dh0.reshape(bsz, lp, D)
    grad_x = dh0[:, first_real:]
    d_meta = jnp.sum(dh0[:, padf:first_real], axis=0)

    d_alog = d_scal[0:1, HEADS:2 * HEADS]
    d_dtb = d_scal[1:2, HEADS:2 * HEADS]
    slab = _pack_small([d_g1, d_g2, d_g3, d_g4], d_meta, d_convq, d_convs, d_alog, d_dtb, d_gn)
    (grp_small,), _ = _split_start("small_grads_start", [[slab]], _gather_pattern, lambda s: (N_DEV,) + s)

    big_w = {"w_in": (w_in, m_w_in, v_w_in), "w_out": (w_out, m_w_out, v_w_out), "w_gate": (w_gate, m_w_gate, v_w_gate),
             "w_up": (w_up, m_w_up, v_w_up), "w_down": (w_down, m_w_down, v_w_down)}

    def update(nm, parts, recv):
        w, m, v = big_w[nm]
        return _adamw_reduced("adamw_" + nm, parts, chip, recv, w[0], m[0], v[0])

    big_out = {nm: update(nm, p, r) for nm, p, r in zip(names_ffn, parts_ffn, recv_ffn)}
    (slab,), (slabs,) = _split_wait("small_grads_wait", grp_small, _gather_pattern, big_out["w_out"][1])
    slabs = with_mine(slabs, slab)
    off_meta = dev * DH
    off_cq = dev * (3 * GW // N_DEV)
    off_cs = dev * (SW // N_DEV)

    def shard_of(s):
        gains, meta, cq, cs, al, db, gn = _unpack_small(s)
        return gains, lax.dynamic_slice_in_dim(meta, off_meta, DH, 1), \
            lax.dynamic_slice_in_dim(cq, off_cq, 3 * GW // N_DEV, 1), \
            lax.dynamic_slice_in_dim(cs, off_cs, SW // N_DEV, 1), al, db, gn

    slabs_local = jax.vmap(lambda s: _pack_small(*shard_of(s)))(slabs)
    w_small = _pack_small([mix_pre_norm, mix_post_norm, ffn_pre_norm, ffn_post_norm], meta_tokens, conv_qkv[0],
                          conv_sc[0], a_log, dt_bias, gdn_norm)
    m_small = _pack_small([m_mix_pre_norm, m_mix_post_norm, m_ffn_pre_norm, m_ffn_post_norm], m_meta_tokens,
                          m_conv_qkv[0], m_conv_sc[0], m_a_log, m_dt_bias, m_gdn_norm)
    v_small = _pack_small([v_mix_pre_norm, v_mix_post_norm, v_ffn_pre_norm, v_ffn_post_norm], v_meta_tokens,
                          v_conv_qkv[0], v_conv_sc[0], v_a_log, v_dt_bias, v_gdn_norm)
    small_out = _adamw("adamw_small", slabs_local, w_small, m_small, v_small)
    parts_in, recv_in = reduce_end("in", rs_in, small_out[1])
    big_out["w_in"] = [t.T for t in _adamw_reduced("adamw_w_in", parts_in[0], chip, recv_in[0], w_in[0].T,
                                                   m_w_in[0].T, v_w_in[0].T)]
    names = ["w_in", "w_out", "w_gate", "w_up", "w_down"]

    def small_leaves(slab_out):
        gains, meta, cq, cs, al, db, gn = _unpack_small(slab_out)
        return {"meta_tokens": meta[:, :DH], "mix_pre_norm": gains[0], "mix_post_norm": gains[1],
                "ffn_pre_norm": gains[2], "ffn_post_norm": gains[3], "conv_qkv": cq[None, :, :3 * GW // N_DEV],
                "a_log": al, "dt_bias": db, "gdn_norm": gn, "conv_sc": cs[None, :, :SW // N_DEV]}

    order = ["meta_tokens", "mix_pre_norm", "mix_post_norm", "ffn_pre_norm", "ffn_post_norm", "w_in", "conv_qkv",
             "a_log", "dt_bias", "gdn_norm", "conv_sc", "w_out", "w_gate", "w_up", "w_down"]
    outs = [loss, grad_x]
    for kind in range(4):
        leaves = small_leaves(small_out[kind])
        for nm in names:
            leaves[nm] = big_out[nm][kind][None]
        outs += [leaves[nm] for nm in order]
    return tuple(outs)
```

```python
import functools

import jax
import jax.numpy as jnp
from jax import lax
from jax.experimental import pallas as pl
from jax.experimental.pallas import tpu as pltpu

F32 = jnp.float32
BF16 = jnp.bfloat16
HIGHEST = lax.Precision.HIGHEST
MESH = pl.DeviceIdType.MESH

D = 1024
N_META = 16
HEADS = 4
DH = 128
GW = HEADS * DH
SW = D - GW
FF = 2816
CH = 64
KQ = 4
KS = 3
EPS = 1e-6
IN_W = 3 * GW + GW + 2 * HEADS + 3 * SW
Q0, K0, V0, Z0, BA0, SX0, SB0, SC0, PW = 0, 512, 1024, 1536, 2048, 2304, 2816, 3328, 3840
BA_W = SX0 - BA0
N_DEV = 8
VMEM_LIMIT = 56 * 1024 * 1024

ADAM_LR, ADAM_B1, ADAM_B2, ADAM_EPS, ADAM_WD, ADAM_STEP = 0.001, 0.9, 0.999, 1e-08, 0.01, 10


def _cparams(sem):
    return pltpu.CompilerParams(dimension_semantics=sem, vmem_limit_bytes=VMEM_LIMIT)


def _div_tile(n, target, mult=16):
    best = None
    for t in range(mult, min(n, target) + 1, mult):
        if n % t == 0:
            best = t
    assert best is not None, (n, target)
    return best


def _silu(x):
    return x * jax.nn.sigmoid(x)


def _dsilu(x):
    s = jax.nn.sigmoid(x)
    return s * (1.0 + x * (1.0 - s))


def _bdot(a, b, dims):
    return lax.dot_general(a.astype(BF16), b.astype(BF16), (dims, ((), ())), preferred_element_type=F32)


NN = ((1,), (0,))
NT = ((1,), (1,))
TN = ((0,), (0,))


def _fdot(a, b, dims=NN):
    return lax.dot_general(a, b, (dims, ((), ())), precision=HIGHEST, preferred_element_type=F32)


def _mm(name, pairs, mode, tm, tn, tk, out_dtypes=(F32,), epilogue=None, extras=(), after=None):
    a0, b0 = pairs[0]
    if mode == "nn":
        (m, k), n = a0.shape, b0.shape[1]
    elif mode == "nt":
        (m, k), n = a0.shape, b0.shape[0]
    else:
        (k, m), n = a0.shape, b0.shape[1]
    assert m % tm == 0 and n % tn == 0 and k % tk == 0, (name, m, n, k, tm, tn, tk)
    nk = k // tk
    npair = len(pairs)
    nex = len(extras)
    nout = len(out_dtypes)
    ntok = 0 if after is None else 1
    dims = {"nn": NN, "nt": NT, "tn": TN}[mode]

    def body(*refs):
        ab = refs[:2 * npair]
        ex = refs[2 * npair:2 * npair + nex]
        outs = refs[2 * npair + nex + ntok:2 * npair + nex + ntok + nout]
        acc_ref = refs[-1]
        kk = pl.program_id(2)

        part = None
        for p in range(npair):
            d = lax.dot_general(ab[2 * p][...], ab[2 * p + 1][...], (dims, ((), ())), preferred_element_type=F32)
            part = d if part is None else part + d

        def finish(acc):
            res = epilogue(acc, *[e[...] for e in ex]) if epilogue is not None else (acc,)
            for o, r in zip(outs, res):
                o[...] = r.astype(o.dtype)

        if nk == 1:
            finish(part)
        else:
            @pl.when(kk == 0)
            def _():
                acc_ref[...] = part

            @pl.when(kk > 0)
            def _():
                acc_ref[...] += part

            @pl.when(kk == nk - 1)
            def _():
                finish(acc_ref[...])

    if mode == "nn":
        a_spec = pl.BlockSpec((tm, tk), lambda i, j, q: (i, q))
        b_spec = pl.BlockSpec((tk, tn), lambda i, j, q: (q, j))
    elif mode == "nt":
        a_spec = pl.BlockSpec((tm, tk), lambda i, j, q: (i, q))
        b_spec = pl.BlockSpec((tn, tk), lambda i, j, q: (j, q))
    else:
        a_spec = pl.BlockSpec((tk, tm), lambda i, j, q: (q, i))
        b_spec = pl.BlockSpec((tk, tn), lambda i, j, q: (q, j))
    o_spec = pl.BlockSpec((tm, tn), lambda i, j, q: (i, j))
    flat = [t for pr in pairs for t in pr]
    res = pl.pallas_call(
        body, name=name,
        grid=(m // tm, n // tn, nk),
        in_specs=[a_spec, b_spec] * npair + [o_spec] * nex + [pl.BlockSpec((8, DH), lambda i, j, q: (0, 0))] * ntok,
        out_specs=[o_spec] * nout,
        out_shape=[jax.ShapeDtypeStruct((m, n), dt) for dt in out_dtypes],
        scratch_shapes=[pltpu.VMEM((tm, tn), F32)],
        compiler_params=_cparams(("parallel", "parallel", "arbitrary")),
    )(*flat, *extras, *([] if after is None else [after]))
    return res


def _ffn_up(u2, wg, wu, tm, tn):
    m, k = u2.shape
    n = wg.shape[0]

    def body(a_ref, g_ref, u_ref, gate_ref, up_ref, act_ref):
        a = a_ref[...]
        g = lax.dot_general(a, g_ref[...], (NT, ((), ())), preferred_element_type=F32)
        u = lax.dot_general(a, u_ref[...], (NT, ((), ())), preferred_element_type=F32)
        gate_ref[...] = g.astype(BF16)
        up_ref[...] = u.astype(BF16)
        act_ref[...] = (_silu(g) * u).astype(BF16)

    o_spec = pl.BlockSpec((tm, tn), lambda j, i: (i, j))
    w_spec = pl.BlockSpec((tn, k), lambda j, i: (j, 0))
    return pl.pallas_call(
        body, name="ffn_up", grid=(n // tn, m // tm),
        in_specs=[pl.BlockSpec((tm, k), lambda j, i: (i, 0)), w_spec, w_spec],
        out_specs=[o_spec, o_spec, o_spec],
        out_shape=[jax.ShapeDtypeStruct((m, n), BF16)] * 3,
        compiler_params=_cparams(("parallel", "parallel")),
    )(u2, wg, wu)


def _rms(x, gain):
    r = lax.rsqrt(jnp.mean(x * x, axis=-1, keepdims=True) + EPS)
    return x * r * gain, r


def _rms_bwd(x, gain, dy):
    r = lax.rsqrt(jnp.mean(x * x, axis=-1, keepdims=True) + EPS)
    dyw = dy * gain
    dx = r * dyw - x * (r * r * r) * jnp.mean(dyw * x, axis=-1, keepdims=True)
    return dx, dy * x * r


def _row_specs(tr, width, nb):
    return pl.BlockSpec((tr, width), lambda b, j: (b * nb + j, 0))


def _vec_spec(width):
    return pl.BlockSpec((1, width), lambda b, j: (0, 0))


def _first_step(b, j):
    return jnp.logical_and(b == 0, j == 0)


def _pre_norm(h0, gain, bsz, lp, tr):
    nb = lp // tr

    def body(h_ref, g_ref, u_ref):
        u_ref[...] = _rms(h_ref[...], g_ref[...])[0].astype(BF16)

    return pl.pallas_call(
        body, name="mix_pre_norm", grid=(bsz, nb),
        in_specs=[_row_specs(tr, D, nb), _vec_spec(D)],
        out_specs=_row_specs(tr, D, nb),
        out_shape=jax.ShapeDtypeStruct(h0.shape, BF16),
        compiler_params=_cparams(("parallel", "parallel")),
    )(h0, gain)


def _mid_norms(h0, mix, g_post, g_pre, bsz, lp, tr):
    nb = lp // tr

    def body(h_ref, m_ref, gp_ref, gq_ref, h1_ref, u2_ref):
        h1 = h_ref[...] + _rms(m_ref[...], gp_ref[...])[0]
        h1_ref[...] = h1
        u2_ref[...] = _rms(h1, gq_ref[...])[0].astype(BF16)

    rs = _row_specs(tr, D, nb)
    return pl.pallas_call(
        body, name="mid_norms", grid=(bsz, nb),
        in_specs=[rs, rs, _vec_spec(D), _vec_spec(D)],
        out_specs=[rs, rs],
        out_shape=[jax.ShapeDtypeStruct(h0.shape, F32), jax.ShapeDtypeStruct(h0.shape, BF16)],
        compiler_params=_cparams(("parallel", "parallel")),
    )(h0, mix, g_post, g_pre)


def _loss_head(h1, ffn, tgt, g_post, bsz, lp, tr, first_real):
    nb = lp // tr

    def body(h_ref, f_ref, t_ref, g_ref, dy_ref, dffn_ref, dg_ref, loss_ref):
        b, j = pl.program_id(0), pl.program_id(1)
        f = f_ref[...]
        g = g_ref[...]
        y = h_ref[...] + _rms(f, g)[0]
        t = j * tr + lax.broadcasted_iota(jnp.int32, (tr, 1), 0)
        err = jnp.where(t >= first_real, y - t_ref[...], 0.0)
        dy = err * (1.0 / D)
        dy_ref[...] = dy
        dx, dgc = _rms_bwd(f, g, dy)
        dffn_ref[...] = dx.astype(BF16)
        lsum = jnp.sum(jnp.sum(err * err, axis=0, keepdims=True), axis=1, keepdims=True) * (0.5 / D)

        @pl.when(_first_step(b, j))
        def _():
            dg_ref[...] = jnp.zeros_like(dg_ref)
            loss_ref[...] = jnp.zeros_like(loss_ref)

        dg_ref[...] += jnp.sum(dgc, axis=0, keepdims=True)
        loss_ref[...] += jnp.broadcast_to(lsum, loss_ref.shape)

    rs = _row_specs(tr, D, nb)
    return pl.pallas_call(
        body, name="loss_head", grid=(bsz, nb),
        in_specs=[rs, rs, rs, _vec_spec(D)],
        out_specs=[rs, rs, _vec_spec(D), _vec_spec(DH)],
        out_shape=[jax.ShapeDtypeStruct(h1.shape, F32), jax.ShapeDtypeStruct(h1.shape, BF16),
                   jax.ShapeDtypeStruct((1, D), F32), jax.ShapeDtypeStruct((1, DH), F32)],
        compiler_params=_cparams(("arbitrary", "arbitrary")),
    )(h1, ffn, tgt, g_post)


def _mid_norms_bwd(h1, mix, du2, dy, g_pre, g_post, bsz, lp, tr):
    nb = lp // tr

    def body(h_ref, m_ref, du_ref, dy_ref, gq_ref, gp_ref, dh1_ref, dmix_ref, dgq_ref, dgp_ref):
        b, j = pl.program_id(0), pl.program_id(1)
        dx, dgq = _rms_bwd(h_ref[...], gq_ref[...], du_ref[...])
        dh1 = dy_ref[...] + dx
        dh1_ref[...] = dh1
        dm, dgp = _rms_bwd(m_ref[...], gp_ref[...], dh1)
        dmix_ref[...] = dm.astype(BF16)

        @pl.when(_first_step(b, j))
        def _():
            dgq_ref[...] = jnp.zeros_like(dgq_ref)
            dgp_ref[...] = jnp.zeros_like(dgp_ref)

        dgq_ref[...] += jnp.sum(dgq, axis=0, keepdims=True)
        dgp_ref[...] += jnp.sum(dgp, axis=0, keepdims=True)

    rs = _row_specs(tr, D, nb)
    return pl.pallas_call(
        body, name="mid_norms_bwd", grid=(bsz, nb),
        in_specs=[rs, rs, rs, rs, _vec_spec(D), _vec_spec(D)],
        out_specs=[rs, rs, _vec_spec(D), _vec_spec(D)],
        out_shape=[jax.ShapeDtypeStruct(h1.shape, F32), jax.ShapeDtypeStruct(h1.shape, BF16),
                   jax.ShapeDtypeStruct((1, D), F32), jax.ShapeDtypeStruct((1, D), F32)],
        compiler_params=_cparams(("arbitrary", "arbitrary")),
    )(h1, mix, du2, dy, g_pre, g_post)


def _pre_norm_bwd(h0, du1, dh1, gain, bsz, lp, tr):
    nb = lp // tr

    def body(h_ref, du_ref, dh1_ref, g_ref, dh0_ref, dg_ref):
        b, j = pl.program_id(0), pl.program_id(1)
        dx, dgc = _rms_bwd(h_ref[...], g_ref[...], du_ref[...])
        dh0_ref[...] = dh1_ref[...] + dx

        @pl.when(_first_step(b, j))
        def _():
            dg_ref[...] = jnp.zeros_like(dg_ref)

        dg_ref[...] += jnp.sum(dgc, axis=0, keepdims=True)

    rs = _row_specs(tr, D, nb)
    return pl.pallas_call(
        body, name="pre_norm_bwd", grid=(bsz, nb),
        in_specs=[rs, rs, rs, _vec_spec(D)],
        out_specs=[rs, _vec_spec(D)],
        out_shape=[jax.ShapeDtypeStruct(h0.shape, F32), jax.ShapeDtypeStruct((1, D), F32)],
        compiler_params=_cparams(("arbitrary", "arbitrary")),
    )(h0, du1, dh1, gain)


def _shift_down(x, s, lp):
    return x if s == 0 else pltpu.roll(x, s, axis=0)


def _shift_up(x, s, lp):
    return x if s == 0 else pltpu.roll(x, lp - s, axis=0)


def _conv_fwd(x, w, taps, lp):
    y = None
    for i in range(taps):
        term = _shift_down(x, taps - 1 - i, lp) * w[i:i + 1, :]
        y = term if y is None else y + term
    return y


def _conv_bwd(x, w, dy, taps, lp):
    dx = None
    dws = []
    for i in range(taps):
        s = taps - 1 - i
        term = _shift_up(dy, s, lp) * w[i:i + 1, :]
        dx = term if dx is None else dx + term
        dws.append(jnp.sum(dy * _shift_down(x, s, lp), axis=0, keepdims=True))
    return dx, jnp.concatenate(dws, axis=0)


def _slab(lp, col0):
    return pl.BlockSpec((lp, DH), lambda b, s: (b, col0 // DH + s))


def _qkv_act(c, s):
    y = _silu(c)
    r = lax.rsqrt(jnp.sum(y * y, axis=-1, keepdims=True) + EPS)
    scale = jnp.where(s < HEADS, DH ** -0.5, 1.0)
    return y, r, scale


def _gdn_prep(proj, conv_qkv, bsz, lp, padf):
    def body(p_ref, w_ref, o_ref):
        s = pl.program_id(1)
        c = _conv_fwd(p_ref[...].astype(F32), w_ref[...], KQ, lp)
        y, r, scale = _qkv_act(c, s)
        out = jnp.where(s < 2 * HEADS, y * (r * scale), y)
        t = lax.broadcasted_iota(jnp.int32, (lp, 1), 0)
        o_ref[...] = jnp.where(t >= padf, out, 0.0)

    return pl.pallas_call(
        body, name="gdn_prep", grid=(bsz, 3 * HEADS),
        in_specs=[_slab(lp, Q0), pl.BlockSpec((KQ, DH), lambda b, s: (0, s))],
        out_specs=_slab(lp, 0),
        out_shape=jax.ShapeDtypeStruct((bsz * lp, 3 * GW), F32),
        compiler_params=_cparams(("parallel", "parallel")),
    )(proj, conv_qkv)


def _gdn_prep_bwd(proj, conv_qkv, dq, dk, dv, bsz, lp, padf):
    def body(p_ref, w_ref, dq_ref, dk_ref, dv_ref, dp_ref, dw_ref):
        s = pl.program_id(1)
        x = p_ref[...].astype(F32)
        w = w_ref[...]
        c = _conv_fwd(x, w, KQ, lp)
        y, r, scale = _qkv_act(c, s)
        d = jnp.where(s < HEADS, dq_ref[...], jnp.where(s < 2 * HEADS, dk_ref[...], dv_ref[...]))
        dn = d * scale
        n = y * r
        dy_norm = r * (dn - n * jnp.sum(dn * n, axis=-1, keepdims=True))
        dyy = jnp.where(s < 2 * HEADS, dy_norm, dn)
        t = lax.broadcasted_iota(jnp.int32, (lp, 1), 0)
        dc = jnp.where(t >= padf, dyy * _dsilu(c), 0.0)
        dx, dw = _conv_bwd(x, w, dc, KQ, lp)
        dp_ref[...] = jnp.where(t >= padf, dx, 0.0).astype(BF16)
        dw_ref[0] = dw

    sl = lambda col0: pl.BlockSpec((lp, DH), lambda b, s: (b, col0 // DH + s))
    part = lambda k: pl.BlockSpec((lp, DH), lambda b, s: (b, jnp.clip(s - k * HEADS, 0, HEADS - 1)))
    return pl.pallas_call(
        body, name="gdn_prep_bwd", grid=(bsz, 3 * HEADS),
        in_specs=[sl(Q0), pl.BlockSpec((KQ, DH), lambda b, s: (0, s)), part(0), part(1), part(2)],
        out_specs=[sl(0), pl.BlockSpec((1, KQ, DH), lambda b, s: (b, 0, s))],
        out_shape=[jax.ShapeDtypeStruct((bsz * lp, 3 * GW), BF16), jax.ShapeDtypeStruct((bsz, KQ, 3 * GW), F32)],
        compiler_params=_cparams(("parallel", "parallel")),
    )(proj, conv_qkv, dq, dk, dv)


def _softplus(x):
    e = jnp.exp(-jnp.abs(x))
    one_e = 1.0 + e
    l1p = jnp.where(one_e == 1.0, e, jnp.log(one_e) * (e / (one_e - 1.0)))
    return jnp.maximum(x, 0.0) + l1p


def _tri(n, kind):
    i = lax.broadcasted_iota(jnp.int32, (n, n), 0)
    j = lax.broadcasted_iota(jnp.int32, (n, n), 1)
    return {"incl": i >= j, "strict": i > j, "upper": i <= j}[kind]


def _gates(proj, alog_v, dtb_v, bsz, lp, padf):
    nchunk = lp // CH

    def body(p_ref, al_ref, dt_ref, gb_ref, cs_ref):
        x = p_ref[...].astype(F32)
        lane = lax.broadcasted_iota(jnp.int32, (lp, DH), 1)
        t = lax.broadcasted_iota(jnp.int32, (lp, DH), 0)
        beta = jax.nn.sigmoid(x)
        g = -jnp.exp(al_ref[...]) * _softplus(x + dt_ref[...])
        gb = jnp.where(lane < HEADS, beta, jnp.where(lane < 2 * HEADS, g, 0.0))
        gb_ref[...] = jnp.where(t >= padf, gb, 0.0)
        ltri = _tri(CH, "incl").astype(F32)

        def step(n, carry):
            r0 = pl.multiple_of(n * CH, CH)
            cs_ref[pl.ds(r0, CH), :] = _fdot(ltri, gb_ref[pl.ds(r0, CH), :])
            return carry

        lax.fori_loop(0, nchunk, step, 0)

    blk = pl.BlockSpec((lp, DH), lambda b: (b, 0))
    vec = pl.BlockSpec((1, DH), lambda b: (0, 0))
    out = pl.BlockSpec((lp, DH), lambda b: (b, 0))
    return pl.pallas_call(
        body, name="gates", grid=(bsz,),
        in_specs=[blk, vec, vec], out_specs=[out, out],
        out_shape=[jax.ShapeDtypeStruct((bsz * lp, DH), F32)] * 2,
        compiler_params=_cparams(("parallel",)),
    )(proj, alog_v, dtb_v)


def _gates_bwd(proj, alog_v, dtb_v, dbeta_b, dg_b, gb, bsz, lp, padf):
    def body(p_ref, al_ref, dt_ref, db_ref, dg_ref, gb_ref, dl_ref, dv_ref):
        b = pl.program_id(0)
        x = p_ref[...].astype(F32)
        lane = lax.broadcasted_iota(jnp.int32, (lp, DH), 1)
        t = lax.broadcasted_iota(jnp.int32, (lp, DH), 0)
        dbeta = jnp.zeros((lp, DH), F32)
        dg = jnp.zeros((lp, DH), F32)
        for h in range(HEADS):
            dbeta = jnp.where(lane == h, db_ref[h], dbeta)
            dg = jnp.where(lane == HEADS + h, dg_ref[h], dg)
        real = t >= padf
        dbeta = jnp.where(real, dbeta, 0.0)
        dg = jnp.where(real, dg, 0.0)
        sb = jax.nn.sigmoid(x)
        neg_ea = -jnp.exp(al_ref[...])
        dsp = dg * neg_ea * jax.nn.sigmoid(x + dt_ref[...])
        dl = dbeta * sb * (1.0 - sb) + dsp
        dl_ref[...] = jnp.concatenate([dl, jnp.zeros((lp, BA_W - DH), F32)], axis=1).astype(BF16)
        dalog = jnp.sum(dg * gb_ref[...], axis=0, keepdims=True)
        ddt = jnp.sum(dsp, axis=0, keepdims=True)

        @pl.when(b == 0)
        def _():
            dv_ref[...] = jnp.zeros_like(dv_ref)

        dv_ref[...] += jnp.concatenate([dalog, ddt], axis=0)

    blk = pl.BlockSpec((lp, DH), lambda b: (b, 0))
    vec = pl.BlockSpec((1, DH), lambda b: (0, 0))
    hb = pl.BlockSpec((HEADS, lp, DH), lambda b: (b, 0, 0))
    return pl.pallas_call(
        body, name="gates_bwd", grid=(bsz,),
        in_specs=[blk, vec, vec, hb, hb, pl.BlockSpec((lp, DH), lambda b: (b, 0))],
        out_specs=[pl.BlockSpec((lp, BA_W), lambda b: (b, 0)), pl.BlockSpec((2, DH), lambda b: (0, 0))],
        out_shape=[jax.ShapeDtypeStruct((bsz * lp, BA_W), BF16), jax.ShapeDtypeStruct((2, DH), F32)],
        compiler_params=_cparams(("arbitrary",)),
    )(proj, alog_v, dtb_v, dbeta_b, dg_b, gb)


def _mix_heads(proj, o, gdn_norm, conv_sc, bsz, lp, padf):
    def body(o_ref, z_ref, gn_ref, sx_ref, sb_ref, sc_ref, w_ref, og_ref, os_ref):
        og_ref[...] = (_rms(o_ref[...], gn_ref[...])[0] * _silu(z_ref[...].astype(F32))).astype(BF16)
        conv = _conv_fwd(sc_ref[...].astype(F32) * sx_ref[...].astype(F32), w_ref[...], KS, lp)
        t = lax.broadcasted_iota(jnp.int32, (lp, 1), 0)
        os_ref[...] = jnp.where(t >= padf, sb_ref[...].astype(F32) * conv, 0.0).astype(BF16)

    half = _slab(lp, 0)
    return pl.pallas_call(
        body, name="mix_heads", grid=(bsz, HEADS),
        in_specs=[half, _slab(lp, Z0), pl.BlockSpec((1, DH), lambda b, s: (0, 0)),
                  _slab(lp, SX0), _slab(lp, SB0), _slab(lp, SC0), pl.BlockSpec((KS, DH), lambda b, s: (0, s))],
        out_specs=[half, half],
        out_shape=[jax.ShapeDtypeStruct((bsz * lp, GW), BF16)] * 2,
        compiler_params=_cparams(("parallel", "parallel")),
    )(o, proj, gdn_norm, proj, proj, proj, conv_sc)


def _mix_heads_bwd(proj, o, gdn_norm, conv_sc, dmixin, bsz, lp, padf):
    def body(o_ref, z_ref, gn_ref, sx_ref, sb_ref, sc_ref, w_ref, dg_ref, ds_ref,
             do_ref, dz_ref, dsx_ref, dsb_ref, dsc_ref, dgn_ref, dw_ref):
        s, b = pl.program_id(0), pl.program_id(1)
        t = lax.broadcasted_iota(jnp.int32, (lp, 1), 0)
        real = t >= padf
        o, z, gn, d = o_ref[...], z_ref[...].astype(F32), gn_ref[...], dg_ref[...]
        sz = _silu(z)
        on = _rms(o, gn)[0]
        dz_ref[...] = (d * on * _dsilu(z)).astype(BF16)
        dox, dgn = _rms_bwd(o, gn, d * sz)
        do_ref[...] = dox
        sx, sb, sc = sx_ref[...].astype(F32), sb_ref[...].astype(F32), sc_ref[...].astype(F32)
        w, e = w_ref[...], ds_ref[...]
        e = jnp.where(real, e, 0.0)
        xin = sc * sx
        conv = _conv_fwd(xin, w, KS, lp)
        dsb_ref[...] = (e * conv).astype(BF16)
        dxin, dw = _conv_bwd(xin, w, e * sb, KS, lp)
        dxin = jnp.where(real, dxin, 0.0)
        dsx_ref[...] = (dxin * sc).astype(BF16)
        dsc_ref[...] = (dxin * sx).astype(BF16)

        @pl.when(jnp.logical_and(s == 0, b == 0))
        def _():
            dgn_ref[...] = jnp.zeros_like(dgn_ref)

        @pl.when(b == 0)
        def _():
            dw_ref[...] = jnp.zeros_like(dw_ref)

        dgn_ref[...] += jnp.sum(dgn, axis=0, keepdims=True)
        dw_ref[...] += dw

    sl = lambda col0: pl.BlockSpec((lp, DH), lambda s, b: (b, col0 // DH + s))
    half = sl(0)
    return pl.pallas_call(
        body, name="mix_heads_bwd", grid=(HEADS, bsz),
        in_specs=[half, sl(Z0), pl.BlockSpec((1, DH), lambda s, b: (0, 0)), sl(SX0), sl(SB0), sl(SC0),
                  pl.BlockSpec((KS, DH), lambda s, b: (0, s)), sl(0), sl(GW)],
        out_specs=[half] * 5 + [pl.BlockSpec((1, DH), lambda s, b: (0, 0)), pl.BlockSpec((KS, DH), lambda s, b: (0, s))],
        out_shape=[jax.ShapeDtypeStruct((bsz * lp, GW), F32)] + [jax.ShapeDtypeStruct((bsz * lp, GW), BF16)] * 4
        + [jax.ShapeDtypeStruct((1, DH), F32), jax.ShapeDtypeStruct((KS, SW), F32)],
        compiler_params=_cparams(("arbitrary", "arbitrary")),
    )(o, proj, gdn_norm, proj, proj, proj, conv_sc, dmixin, dmixin)


def _split_dot(a, b, dims=NN):
    ah = a.astype(BF16)
    al = (a - ah.astype(F32)).astype(BF16)
    bh = b.astype(BF16)
    bl = (b - bh.astype(F32)).astype(BF16)

    def d(x, y):
        return lax.dot_general(x, y, (dims, ((), ())), preferred_element_type=F32)

    return d(ah, bh) + (d(ah, bl) + d(al, bh))


def _eye(n):
    return (lax.broadcasted_iota(jnp.int32, (n, n), 0) == lax.broadcasted_iota(jnp.int32, (n, n), 1)).astype(F32)


def _each(f, *lists):
    return [f(*xs) for xs in zip(*lists)]


def _inv_unit_lower(a_list):
    eye = _eye(CH)
    p = _each(lambda a: -a, a_list)
    t = _each(lambda x: eye + x, p)
    p = _each(lambda x: _split_dot(x, x), p)
    for level in range(5):
        if level < 4:
            prod = _each(lambda tt, pp: _split_dot(jnp.concatenate([tt, pp], axis=0), pp), t, p)
            t = _each(lambda tt, pr: tt + pr[:CH], t, prod)
            p = _each(lambda pr: pr[CH:], prod)
        else:
            t = _each(lambda tt, pp: tt + _split_dot(tt, pp), t, p)
    return t


def _chunk_gates(gc_ref, be_ref, gt_ref, g):
    rows = pl.ds(g * CH, CH)
    gc = gc_ref[0, rows, :]
    be = be_ref[0, rows, :]
    gcl = gc_ref[0, pl.ds(g * CH + CH - 1, 1), :]
    gt = gt_ref[0, g]
    incl, upper = _tri(CH, "incl"), _tri(CH, "upper")
    dec = jnp.where(incl, jnp.exp(jnp.where(incl, gc[:, :CH] - gt, 0.0)), 0.0)
    dec_t = jnp.where(upper, jnp.exp(jnp.where(upper, gt - gc[:, :CH], 0.0)), 0.0)
    return rows, be, dec, dec_t, jnp.exp(gc), jnp.exp(gcl - gc), jnp.exp(gcl)


def _chunk_group(nchunk):
    return 3 if nchunk % 3 == 0 else 1


def _local_group(nchunk):
    return 11 if nchunk % 11 == 0 else _chunk_group(nchunk)


def _gdn_local_specs(lp, grp):
    ngrp = lp // (grp * CH)
    head = lambda col0: pl.BlockSpec((grp * CH, DH), lambda i, j: ((i // HEADS) * ngrp + j, col0 // DH + i % HEADS))
    lane_b = pl.BlockSpec((1, grp * CH, DH), lambda i, j: (i, j, 0))
    per_chunk = lambda r, c: pl.BlockSpec((1, grp, r, c), lambda i, j: (i, j, 0, 0))
    return ngrp, head, lane_b, per_chunk


def _gdn_local_fwd(qkv, gcb, bb, gct, bsz, lp):
    nchunk = lp // CH
    grp = _local_group(nchunk)
    ngrp, head, lane_b, per_chunk = _gdn_local_specs(lp, grp)
    bh = bsz * HEADS

    def body(q_ref, k_ref, v_ref, gc_ref, be_ref, gt_ref,
             u_ref, l1_ref, l2_ref, l3_ref, wt_ref, kd_ref, ti_ref, gl_ref):
        incl, strict, upper = _tri(CH, "incl"), _tri(CH, "strict"), _tri(CH, "upper")
        eye = _eye(CH)
        rows, be, dec, dec_t, eg, ekd, gl = zip(*[_chunk_gates(gc_ref, be_ref, gt_ref, g) for g in range(grp)])
        q = [q_ref[r, :] for r in rows]
        k = [k_ref[r, :] for r in rows]
        v = [v_ref[r, :] for r in rows]
        kb = _each(lambda x, y: x * y, k, be)
        a = _each(lambda x, y, d: jnp.where(strict, _bdot(x, y, NT) * d, 0.0), kb, k, dec)
        tinv = _inv_unit_lower(a)
        u = _each(lambda t, x, y: _bdot(t, x * y, NN), tinv, v, be)
        w = _each(lambda t, x, y: _bdot(t, x * y, NN), tinv, kb, eg)
        qk = _each(lambda x, y, d: jnp.where(incl, _bdot(x, y, NT) * d, 0.0), q, k, dec)
        qk_t = _each(lambda x, y, d: jnp.where(upper, _bdot(y, x, NT) * d, 0.0), q, k, dec_t)
        qd = _each(lambda x, y: x * y, q, eg)
        kd = _each(lambda x, y: x * y, k, ekd)
        kd_t = _each(lambda x: _bdot(x, eye, TN), kd)
        qd_t = _each(lambda x: _bdot(x, eye, TN), qd)
        w_t = _each(lambda x: _bdot(x, eye, TN), w)
        for g in range(grp):
            u_ref[rows[g], :] = u[g]
            l1_ref[0, g] = jnp.concatenate([w[g], qd[g]], axis=0).astype(BF16)
            l2_ref[0, g] = jnp.concatenate([qk[g], kd_t[g]], axis=0).astype(BF16)
            l3_ref[0, g] = jnp.concatenate([qk_t[g], qd_t[g]], axis=0).astype(BF16)
            wt_ref[0, g] = w_t[g].astype(BF16)
            kd_ref[0, g] = kd[g].astype(BF16)
            ti_ref[0, g] = tinv[g]
            gl_ref[0, g] = gl[g]

    r = bsz * lp
    shapes = [((r, GW), F32), ((bh, nchunk, 2 * CH, DH), BF16), ((bh, nchunk, 3 * CH, CH), BF16),
              ((bh, nchunk, 3 * CH, CH), BF16), ((bh, nchunk, DH, CH), BF16), ((bh, nchunk, CH, DH), BF16),
              ((bh, nchunk, CH, CH), F32), ((bh, nchunk, 1, DH), F32)]
    return pl.pallas_call(
        body, name="gdn_local_fwd", grid=(bh, ngrp),
        in_specs=[head(0), head(GW), head(2 * GW), lane_b, lane_b, per_chunk(1, CH)],
        out_specs=[head(0), per_chunk(2 * CH, DH), per_chunk(3 * CH, CH), per_chunk(3 * CH, CH), per_chunk(DH, CH),
                   per_chunk(CH, DH), per_chunk(CH, CH), per_chunk(1, DH)],
        out_shape=[jax.ShapeDtypeStruct(s, dt) for s, dt in shapes],
        compiler_params=_cparams(("parallel", "parallel")),
    )(qkv, qkv, qkv, gcb, bb, gct)


def _scan_specs(bsz, lp, grp, order):
    ngrp = lp // (grp * CH)
    bh = bsz * HEADS
    seq = lambda: pl.BlockSpec((bsz, grp * CH, GW), lambda s: (0, order(s, ngrp), 0))
    per_chunk = lambda r, c: pl.BlockSpec((bh, grp, r, c), lambda s: (0, order(s, ngrp), 0, 0))
    return ngrp, bh, seq, per_chunk


def _gdn_scan_fwd(u, l1, l2, gl, bsz, lp):
    nchunk = lp // CH
    grp = _chunk_group(nchunk)
    ngrp, bh, seq, per_chunk = _scan_specs(bsz, lp, grp, lambda s, n: s)

    def body(u_ref, l1_ref, l2_ref, gl_ref, o_ref, vn_ref, st_ref, s_scr):
        @pl.when(pl.program_id(0) == 0)
        def _():
            s_scr[...] = jnp.zeros_like(s_scr)

        for c in range(grp):
            rows = slice(c * CH, (c + 1) * CH)
            for i in range(bh):
                b, h = divmod(i, HEADS)
                cols = slice(h * DH, (h + 1) * DH)
                s = s_scr[i]
                st_ref[i, c] = s
                x = jnp.dot(l1_ref[i, c], s.astype(BF16), preferred_element_type=F32)
                vnew = u_ref[b, rows, cols] - x[:CH]
                y = jnp.dot(l2_ref[i, c], vnew.astype(BF16), preferred_element_type=F32)
                o_ref[b, rows, cols] = x[CH:] + y[:CH]
                vn_ref[b, rows, cols] = vnew.astype(BF16)
                s_scr[i] = s * gl_ref[i, c] + y[CH:]

    return pl.pallas_call(
        body, name="gdn_scan_fwd", grid=(ngrp,),
        in_specs=[seq(), per_chunk(2 * CH, DH), per_chunk(3 * CH, CH), per_chunk(1, DH)],
        out_specs=[seq(), seq(), per_chunk(DH, DH)],
        out_shape=[jax.ShapeDtypeStruct((bsz, lp, GW), F32), jax.ShapeDtypeStruct((bsz, lp, GW), BF16),
                   jax.ShapeDtypeStruct((bh, nchunk, DH, DH), F32)],
        scratch_shapes=[pltpu.VMEM((bh, DH, DH), F32)],
        compiler_params=_cparams(("arbitrary",)),
    )(u.reshape(bsz, lp, GW), l1, l2, gl)


def _gdn_scan_bwd(d_o, l3, wt, kd, gl, bsz, lp):
    nchunk = lp // CH
    grp = _chunk_group(nchunk)
    ngrp, bh, seq, per_chunk = _scan_specs(bsz, lp, grp, lambda s, n: n - 1 - s)

    def body(do_ref, l3_ref, wt_ref, kd_ref, gl_ref, dvn_ref, dst_ref, ds_scr):
        @pl.when(pl.program_id(0) == 0)
        def _():
            ds_scr[...] = jnp.zeros_like(ds_scr)

        for c in reversed(range(grp)):
            rows = slice(c * CH, (c + 1) * CH)
            for i in range(bh):
                b, h = divmod(i, HEADS)
                cols = slice(h * DH, (h + 1) * DH)
                ds = ds_scr[i]
                dst_ref[i, c] = ds
                p = jnp.dot(l3_ref[i, c], do_ref[b, rows, cols].astype(BF16), preferred_element_type=F32)
                dvn = p[:CH] + jnp.dot(kd_ref[i, c], ds.astype(BF16), preferred_element_type=F32)
                dvn_ref[b, rows, cols] = dvn.astype(BF16)
                ds_scr[i] = ds * gl_ref[i, c] + p[CH:] - jnp.dot(wt_ref[i, c], dvn.astype(BF16),
                                                                 preferred_element_type=F32)

    return pl.pallas_call(
        body, name="gdn_scan_bwd", grid=(ngrp,),
        in_specs=[seq(), per_chunk(3 * CH, CH), per_chunk(DH, CH), per_chunk(CH, DH), per_chunk(1, DH)],
        out_specs=[seq(), per_chunk(DH, DH)],
        out_shape=[jax.ShapeDtypeStruct((bsz, lp, GW), BF16), jax.ShapeDtypeStruct((bh, nchunk, DH, DH), F32)],
        scratch_shapes=[pltpu.VMEM((bh, DH, DH), F32)],
        compiler_params=_cparams(("arbitrary",)),
    )(d_o.reshape(bsz, lp, GW), l3, wt, kd, gl)


def _gdn_local_bwd(qkv, gcb, bb, gct, tinv_all, states, dstates, vnew, dvnew, d_o, bsz, lp):
    nchunk = lp // CH
    grp = _local_group(nchunk)
    ngrp, head, lane_b, per_chunk = _gdn_local_specs(lp, grp)
    bh = bsz * HEADS

    def body(q_ref, k_ref, v_ref, gc_ref, be_ref, gt_ref, ti_ref, st_ref, dst_ref, vn_ref, dvn_ref, do_ref,
             dq_ref, dk_ref, dv_ref, dg_ref, db_ref):
        incl, strict = _tri(CH, "incl"), _tri(CH, "strict")
        upper = _tri(CH, "upper").astype(F32)
        ones = jnp.ones((CH, DH), F32)
        last = lax.broadcasted_iota(jnp.int32, (CH, 1), 0) == CH - 1

        def rsum(x):
            return jnp.sum(x, axis=-1, keepdims=True)

        def mul(xs, ys):
            return _each(lambda x, y: x * y, xs, ys)

        rows, be, dec, _, eg, ekd, gl = zip(*[_chunk_gates(gc_ref, be_ref, gt_ref, g) for g in range(grp)])
        q = [q_ref[r, :] for r in rows]
        k = [k_ref[r, :] for r in rows]
        v = [v_ref[r, :] for r in rows]
        vnew = [vn_ref[r, :] for r in rows]
        dvn = [dvn_ref[r, :] for r in rows]
        do = [do_ref[r, :] for r in rows]
        tinv = [ti_ref[0, g] for g in range(grp)]
        s = [st_ref[0, g] for g in range(grp)]
        ds = [dst_ref[0, g] for g in range(grp)]
        kb, vb, qd, kd = mul(k, be), mul(v, be), mul(q, eg), mul(k, ekd)
        kbg = mul(kb, eg)
        a = _each(lambda x, y, d: jnp.where(strict, _bdot(x, y, NT) * d, 0.0), kb, k, dec)
        qk = _each(lambda x, y, d: jnp.where(incl, _bdot(x, y, NT) * d, 0.0), q, k, dec)
        dqk = _each(lambda x, y: jnp.where(incl, _bdot(x, y, NT), 0.0), do, vnew)
        dqd = _each(lambda x, y: _bdot(x, y, NT), do, s)
        dkd = _each(lambda x, y: _bdot(x, y, NT), vnew, ds)
        dgl = _each(lambda x, y: jnp.sum(rsum(x * y), axis=0, keepdims=True), s, ds)
        dw = _each(lambda x, y: -_bdot(x, y, NT), dvn, s)
        dvb = _each(lambda t, x: _bdot(t, x, TN), tinv, dvn)
        dkbg = _each(lambda t, x: _bdot(t, x, TN), tinv, dw)
        dt = _each(lambda x, y, z, w: _bdot(x, y, NT) + _bdot(z, w, NT), dvn, vb, dw, kbg)
        tdt = _each(lambda t, x: _split_dot(t, x, TN), tinv, dt)
        da = _each(lambda x, t: jnp.where(strict, -_split_dot(x, t, NT), 0.0), tdt, tinv)
        dp, dr = mul(da, dec), mul(dqk, dec)
        dkb = _each(lambda x, y, z, e: _bdot(x, y, NN) + z * e, dp, k, dkbg, eg)
        dk = _each(lambda p_, kb_, r_, q_, dkd_, ekd_, dkb_, be_:
                   _bdot(p_, kb_, TN) + _bdot(r_, q_, TN) + dkd_ * ekd_ + dkb_ * be_,
                   dp, kb, dr, q, dkd, ekd, dkb, be)
        dq = _each(lambda r_, k_, dqd_, eg_: _bdot(r_, k_, NN) + dqd_ * eg_, dr, k, dqd, eg)
        m = _each(lambda da_, a_, dqk_, qk_: da_ * a_ + dqk_ * qk_, da, a, dqk, qk)
        m_cols = _each(lambda x: _split_dot(x, ones, TN), m)
        kd_term = _each(lambda x, y: rsum(x * y), dkd, kd)
        dgc = _each(lambda m_, mc, dqd_, qd_, kt, dkbg_, kbg_, dgl_, gl_:
                    rsum(m_) - mc + rsum(dqd_ * qd_) - kt + rsum(dkbg_ * kbg_)
                    + jnp.where(last, jnp.sum(kt, axis=0, keepdims=True) + dgl_ * gl_, 0.0),
                    m, m_cols, dqd, qd, kd_term, dkbg, kbg, dgl, gl)
        dg = _each(lambda x: _split_dot(upper, x, NN), dgc)
        for g in range(grp):
            dq_ref[rows[g], :] = dq[g]
            dk_ref[rows[g], :] = dk[g]
            dv_ref[rows[g], :] = dvb[g] * be[g]
            dg_ref[0, rows[g], :] = dg[g]
            db_ref[0, rows[g], :] = jnp.broadcast_to(rsum(dkb[g] * k[g]) + rsum(dvb[g] * v[g]), (CH, DH))

    r = bsz * lp
    return pl.pallas_call(
        body, name="gdn_local_bwd", grid=(bh, ngrp),
        in_specs=[head(0), head(GW), head(2 * GW), lane_b, lane_b, per_chunk(1, CH), per_chunk(CH, CH),
                  per_chunk(DH, DH), per_chunk(DH, DH), head(0), head(0), head(0)],
        out_specs=[head(0), head(0), head(0), lane_b, lane_b],
        out_shape=[jax.ShapeDtypeStruct((r, GW), F32)] * 3 + [jax.ShapeDtypeStruct((bh, lp, DH), F32)] * 2,
        compiler_params=_cparams(("parallel", "parallel")),
    )(qkv, qkv, qkv, gcb, bb, gct, tinv_all, states, dstates, vnew, dvnew, d_o)


def _place():
    return lax.axis_index("x"), lax.axis_index("y"), lax.axis_index("c")


def _any_specs(n):
    return [pl.BlockSpec(memory_space=pl.ANY)] * n


def _all_gather(name, shards):
    n = len(shards)

    def body(*refs):
        ins, outs = refs[:n], refs[n:2 * n]
        send_sems, recv_sems, local_sems = refs[2 * n:]
        x, y, c = _place()
        me, sibling = (x, y, c), (x, y, 1 - c)
        chips = [(1 - x, y), (x, 1 - y), (1 - x, 1 - y)]

        def slot(i, p):
            return outs[i].at[4 * p[0] + 2 * p[1] + p[2]]

        def copy(i, k, block, to, src=None):
            return pltpu.make_async_remote_copy(
                src_ref=slot(i, block) if src is None else src, dst_ref=slot(i, block),
                send_sem=send_sems.at[i, k], recv_sem=recv_sems.at[i, k], device_id=to, device_id_type=MESH)

        mine = [pltpu.make_async_copy(ins[i], slot(i, me), local_sems.at[i]) for i in range(n)]
        for cp in mine:
            cp.start()
        first = []
        for i in range(n):
            first.append(copy(i, 0, me, sibling, src=ins[i]))
            first += [copy(i, 1 + j, me, (*chip, c), src=ins[i]) for j, chip in enumerate(chips)]
        for cp in first:
            cp.start()
        passed = []
        for j, chip in enumerate(chips):
            for i in range(n):
                copy(i, 1 + j, (*chip, c), me).wait_recv()
                fwd = copy(i, 4 + j, (*chip, c), sibling)
                fwd.start()
                passed.append(fwd)
        for i in range(n):
            copy(i, 0, sibling, me).wait_recv()
            for j, chip in enumerate(chips):
                copy(i, 4 + j, (*chip, 1 - c), me).wait_recv()
        for cp in first + passed:
            cp.wait_send()
        for cp in mine:
            cp.wait()

    return pl.pallas_call(
        body, name=name,
        in_specs=_any_specs(n), out_specs=_any_specs(n),
        out_shape=[jax.ShapeDtypeStruct((N_DEV,) + s.shape, s.dtype) for s in shards],
        scratch_shapes=[pltpu.SemaphoreType.DMA((n, 7)), pltpu.SemaphoreType.DMA((n, 7)),
                        pltpu.SemaphoreType.DMA((n,))],
    )(*shards)


HBM_SPEC = pl.BlockSpec(memory_space=pltpu.HBM)
SEM_SPEC = pl.BlockSpec(memory_space=pltpu.SEMAPHORE)
EFFECT = pltpu.SideEffectType.DATAFLOW_SIDE_EFFECTING
N_PEER = N_DEV - 1


def _peer(r):
    x, y, c = _place()
    return ((1 - x) if r & 4 else x, (1 - y) if r & 2 else y, (1 - c) if r & 1 else c)


def _slot_of(p):
    return 4 * p[0] + 2 * p[1] + p[2]


def _hbm(a):
    return pltpu.with_memory_space_constraint(a, pltpu.HBM)


def _gather_pattern(src, land):
    me = _place()
    return [(src, land.at[_slot_of(me)], _peer(r), land.at[_slot_of(_peer(r))]) for r in range(1, N_DEV)]


def _sibling_pattern(src, land):
    x, y, c = _place()
    return [(src.at[ch, 1 - c], land.at[ch], (x, y, 1 - c), land.at[ch]) for ch in range(4)]


def _chip_pattern(src, land):
    x, y, c = _place()
    chips = [(1 - x, y), (x, 1 - y), (1 - x, 1 - y)]
    return [(src.at[2 * ch[0] + ch[1]], land.at[j], (*ch, c), land.at[j]) for j, ch in enumerate(chips)]


def _split_start(name, groups, pattern, land_shape):
    sizes = [len(g) for g in groups]
    arrays = [s for g in groups for s in g]
    n = len(arrays)
    land_shapes = [land_shape(s.shape) for s in arrays]

    def body(*refs):
        srcs, lands = refs[:n], refs[n:2 * n]
        sems = refs[2 * n:2 * n + 2 * len(groups)]
        token = refs[-1]
        k = 0
        for gi, size in enumerate(sizes):
            for i in range(size):
                copies = pattern(srcs[k], lands[k])
                for j, (s, d, peer, _) in enumerate(copies):
                    pltpu.make_async_remote_copy(
                        src_ref=s, dst_ref=d, send_sem=sems[2 * gi].at[i * len(copies) + j],
                        recv_sem=sems[2 * gi + 1].at[i * len(copies) + j], device_id=peer, device_id_type=MESH).start()
                k += 1
        token[...] = jnp.zeros_like(token)

    per_array = {_gather_pattern: N_PEER, _sibling_pattern: 4, _chip_pattern: 3}[pattern]
    sem_shapes = [pltpu.SemaphoreType.DMA((size * per_array,)) for size in sizes for _ in range(2)]
    res = pl.pallas_call(
        body, name=name,
        in_specs=[HBM_SPEC] * (2 * n),
        out_specs=[SEM_SPEC] * len(sem_shapes) + [HBM_SPEC] * (2 * n) + [pl.BlockSpec(memory_space=pltpu.VMEM)],
        out_shape=sem_shapes + [pltpu.HBM(s.shape, s.dtype) for s in arrays]
        + [pltpu.HBM(ls, s.dtype) for ls, s in zip(land_shapes, arrays)] + [jax.ShapeDtypeStruct((8, DH), F32)],
        input_output_aliases={i: len(sem_shapes) + i for i in range(2 * n)},
        compiler_params=pltpu.CompilerParams(has_side_effects=EFFECT),
    )(*[_hbm(s) for s in arrays], *[_hbm(lax.empty(ls, s.dtype)) for ls, s in zip(land_shapes, arrays)])
    nsem = len(sem_shapes)
    out, k = [], 0
    for gi, size in enumerate(sizes):
        out.append((res[2 * gi], res[2 * gi + 1], list(res[nsem + k:nsem + k + size]),
                    list(res[nsem + n + k:nsem + n + k + size])))
        k += size
    return out, res[-1]


def _split_wait(name, group, pattern, after):
    send_sems, recv_sems, arrays, lands = group
    n = len(arrays)
    fill_own = pattern is _gather_pattern

    def body(*refs):
        srcs, lands_ = refs[:n], refs[n:2 * n]
        s_sems, r_sems = refs[2 * n], refs[2 * n + 1]
        own = []
        if fill_own:
            local_sems = refs[-1]
            own = [pltpu.make_async_copy(srcs[i], lands_[i].at[_slot_of(_place())], local_sems.at[i])
                   for i in range(n)]
            for cp in own:
                cp.start()
        for i in range(n):
            copies = pattern(srcs[i], lands_[i])
            for j, (s, _, peer, filled) in enumerate(copies):
                cp = pltpu.make_async_remote_copy(
                    src_ref=s, dst_ref=filled, send_sem=s_sems.at[i * len(copies) + j],
                    recv_sem=r_sems.at[i * len(copies) + j], device_id=peer, device_id_type=MESH)
                cp.wait_send()
                cp.wait_recv()
        for cp in own:
            cp.wait()

    res = pl.pallas_call(
        body, name=name,
        in_specs=[HBM_SPEC] * (2 * n) + [SEM_SPEC, SEM_SPEC, pl.BlockSpec(memory_space=pl.ANY)],
        out_specs=[HBM_SPEC] * (2 * n),
        out_shape=[pltpu.HBM(s.shape, s.dtype) for s in arrays] + [pltpu.HBM(l.shape, l.dtype) for l in lands],
        scratch_shapes=[pltpu.SemaphoreType.DMA((n,))] if fill_own else [],
        input_output_aliases={i: i for i in range(2 * n)},
        compiler_params=pltpu.CompilerParams(has_side_effects=EFFECT),
    )(*arrays, *lands, send_sems, recv_sems, after)
    return list(res[:n]), list(res[n:])


def _pair_sum(name, grads, recv, core):
    _, _, r, c = grads.shape
    tr = _div_tile(r, 256, 16) if r % 16 == 0 else r

    def body(core_ref, g_ref, r_ref, o_ref):
        o_ref[...] = (g_ref[0].astype(F32) + r_ref[...].astype(F32)).astype(o_ref.dtype)

    return pl.pallas_call(
        body, name=name,
        grid_spec=pltpu.PrefetchScalarGridSpec(
            num_scalar_prefetch=1, grid=(4, r // tr),
            in_specs=[pl.BlockSpec((1, 1, tr, c), lambda s, i, cr: (s, cr[0], i, 0)),
                      pl.BlockSpec((1, tr, c), lambda s, i, cr: (s, i, 0))],
            out_specs=pl.BlockSpec((1, tr, c), lambda s, i, cr: (s, i, 0))),
        out_shape=jax.ShapeDtypeStruct(recv.shape, recv.dtype),
        compiler_params=_cparams(("parallel", "parallel")),
    )(core, grads, recv)


def _adamw_math(w, g, m, v):
    m = ADAM_B1 * m + (1.0 - ADAM_B1) * g
    v = ADAM_B2 * v + (1.0 - ADAM_B2) * (g * g)
    m_hat = m / (1.0 - ADAM_B1 ** ADAM_STEP)
    v_hat = v / (1.0 - ADAM_B2 ** ADAM_STEP)
    delta = -ADAM_LR * (m_hat / (jnp.sqrt(v_hat) + ADAM_EPS) + ADAM_WD * w)
    return delta, m, v


def _adamw(name, parts, w, m, v):
    nslot, r, c = parts.shape
    tr = _div_tile(r, 256, 8) if r % 8 == 0 else r

    def body(p_ref, w_ref, m_ref, v_ref, g_out, d_out, m_out, v_out):
        g = p_ref[0]
        for s in range(1, nslot):
            g = g + p_ref[s]
        d, mn, vn = _adamw_math(w_ref[...], g, m_ref[...], v_ref[...])
        g_out[...] = g
        d_out[...] = d
        m_out[...] = mn
        v_out[...] = vn

    blk = pl.BlockSpec((tr, c), lambda i: (i, 0))
    return pl.pallas_call(
        body, name=name, grid=(r // tr,),
        in_specs=[pl.BlockSpec((nslot, tr, c), lambda i: (0, i, 0)), blk, blk, blk],
        out_specs=[blk] * 4,
        out_shape=[jax.ShapeDtypeStruct((r, c), F32)] * 4,
        compiler_params=_cparams(("parallel",)),
    )(parts, w, m, v)


def _grad_sum(name, parts, chip, recv):
    _, r, c = parts.shape
    tr = _div_tile(r, 256, 16) if r % 16 == 0 else r

    def body(chip_ref, p_ref, r_ref, g_out):
        g = p_ref[0].astype(F32)
        for s in range(3):
            g = g + r_ref[s].astype(F32)
        g_out[...] = g

    return pl.pallas_call(
        body, name=name,
        grid_spec=pltpu.PrefetchScalarGridSpec(
            num_scalar_prefetch=1, grid=(r // tr,),
            in_specs=[pl.BlockSpec((1, tr, c), lambda i, ch: (ch[0], i, 0)),
                      pl.BlockSpec((3, tr, c), lambda i, ch: (0, i, 0))],
            out_specs=pl.BlockSpec((tr, c), lambda i, ch: (i, 0))),
        out_shape=jax.ShapeDtypeStruct((r, c), F32),
        compiler_params=_cparams(("parallel",)),
    )(chip, parts, recv)


def _adamw_reduced(name, parts, chip, recv, w, m, v):
    _, r, c = parts.shape
    tr = _div_tile(r, 256, 16) if r % 16 == 0 else r

    def body(chip_ref, p_ref, r_ref, w_ref, m_ref, v_ref, g_out, d_out, m_out, v_out):
        g = p_ref[0].astype(F32)
        for s in range(3):
            g = g + r_ref[s].astype(F32)
        d, mn, vn = _adamw_math(w_ref[...], g, m_ref[...], v_ref[...])
        g_out[...] = g
        d_out[...] = d
        m_out[...] = mn
        v_out[...] = vn

    blk = pl.BlockSpec((tr, c), lambda i, ch: (i, 0))
    return pl.pallas_call(
        body, name=name,
        grid_spec=pltpu.PrefetchScalarGridSpec(
            num_scalar_prefetch=1, grid=(r // tr,),
            in_specs=[pl.BlockSpec((1, tr, c), lambda i, ch: (ch[0], i, 0)),
                      pl.BlockSpec((3, tr, c), lambda i, ch: (0, i, 0)), blk, blk, blk],
            out_specs=[blk] * 4),
        out_shape=[jax.ShapeDtypeStruct((r, c), F32)] * 4,
        compiler_params=_cparams(("parallel",)),
    )(chip, parts, recv, w, m, v)


SMALL_ROWS, SMALL_COLS = 32, 3 * GW


def _pack_small(gains4, meta, conv_qkv, conv_sc, a_log, dt_bias, gdn_norm, extra=None):
    def padc(a):
        return jnp.pad(a, ((0, 0), (0, SMALL_COLS - a.shape[1])))
    scal = jnp.concatenate([jnp.pad(a_log, ((0, 0), (0, DH - HEADS))), jnp.pad(dt_bias, ((0, 0), (0, DH - HEADS))),
                            gdn_norm] + ([] if extra is None else [extra]), axis=1)
    rows = [padc(g) for g in gains4] + [padc(meta), padc(conv_qkv), padc(conv_sc), padc(scal)]
    slab = jnp.concatenate(rows, axis=0)
    return jnp.pad(slab, ((0, SMALL_ROWS - slab.shape[0]), (0, 0)))


def _unpack_small(slab):
    gains = [slab[i:i + 1, :D] for i in range(4)]
    meta = slab[4:20, :D]
    conv_qkv = slab[20:24, :]
    conv_sc = slab[24:27, :SW]
    a_log = slab[27:28, 0:HEADS]
    dt_bias = slab[27:28, DH:DH + HEADS]
    gdn_norm = slab[27:28, 2 * DH:3 * DH]
    return gains, meta, conv_qkv, conv_sc, a_log, dt_bias, gdn_norm


def kernel(x, meta_tokens, mix_pre_norm, mix_post_norm, ffn_pre_norm, ffn_post_norm, w_in, conv_qkv, a_log, dt_bias, gdn_norm, conv_sc, w_out, w_gate, w_up, w_down, loss_target, m_meta_tokens, m_mix_pre_norm, m_mix_post_norm, m_ffn_pre_norm, m_ffn_post_norm, m_w_in, m_conv_qkv, m_a_log, m_dt_bias, m_gdn_norm, m_conv_sc, m_w_out, m_w_gate, m_w_up, m_w_down, v_meta_tokens, v_mix_pre_norm, v_mix_post_norm, v_ffn_pre_norm, v_ffn_post_norm, v_w_in, v_conv_qkv, v_a_log, v_dt_bias, v_gdn_norm, v_conv_sc, v_w_out, v_w_gate, v_w_up, v_w_down):
    bsz, seq, _ = x.shape
    lp = -(-(N_META + seq) // CH) * CH
    padf = lp - N_META - seq
    first_real = padf + N_META
    rows = bsz * lp
    nchunk = lp // CH
    tr = _div_tile(lp, 528)
    tm = _div_tile(rows, 1056)
    cx, cy, cc = _place()
    dev = 4 * cx + 2 * cy + cc

    small_shard = jnp.concatenate([
        jnp.pad(meta_tokens, ((0, 0), (0, 256 - DH))),
        jnp.pad(conv_qkv[0], ((0, 0), (0, 256 - 3 * GW // N_DEV))),
        jnp.pad(conv_sc[0], ((0, 5), (0, 256 - SW // N_DEV)))], axis=0)
    g_in, g_small = _all_gather("gather_w_in", [w_in[0].T.astype(BF16), small_shard])
    my_out = [w_out[0].astype(BF16)]
    my_ffn = [w_gate[0].T.astype(BF16), w_up[0].T.astype(BF16), w_down[0].astype(BF16)]
    (grp_out, grp_ffn), token = _split_start("gather_rest_start", [my_out, my_ffn], _gather_pattern,
                                             lambda s: (N_DEV,) + s)
    after_start = token[0:1, 0:1]

    def cols_full(g):
        return jnp.transpose(g, (1, 0, 2)).reshape(g.shape[1], -1)

    win_t = g_in.reshape(IN_W, D)
    win_t = jnp.concatenate([win_t[:BA0 + 2 * HEADS], jnp.zeros((BA_W - 2 * HEADS, D), BF16),
                             win_t[BA0 + 2 * HEADS:]], axis=0)
    meta_full = cols_full(g_small[:, 0:16, :DH])
    convq_full = cols_full(g_small[:, 16:20, :3 * GW // N_DEV])
    convs_full = cols_full(g_small[:, 20:23, :SW // N_DEV])
    alog_v = jnp.pad(a_log, ((0, 0), (HEADS, DH - 2 * HEADS)))
    dtb_v = jnp.pad(dt_bias, ((0, 0), (HEADS, DH - 2 * HEADS)))

    h0 = jnp.concatenate([jnp.zeros((bsz, padf, D), F32), jnp.broadcast_to(meta_full[None], (bsz, N_META, D)), x],
                         axis=1).reshape(rows, D)
    tgt = jnp.pad(loss_target, ((0, 0), (first_real, 0), (0, 0))).reshape(rows, D)
    u1 = _pre_norm(h0, mix_pre_norm + after_start, bsz, lp, tr)
    (proj,) = _mm("proj_in", [(u1, win_t)], "nt", tm, 768, D, out_dtypes=(BF16,))
    (ba,) = _mm("proj_gate_logits", [(u1, win_t[BA0:SX0])], "nt", tm, BA_W, D)
    qkv = _gdn_prep(proj, convq_full, bsz, lp, padf)
    gb, gcs = _gates(ba, alog_v, dtb_v, bsz, lp, padf)

    def lane_bcast(cols):
        t = jnp.transpose(cols.reshape(bsz, lp, HEADS), (0, 2, 1)).reshape(bsz * HEADS, lp, 1)
        return jnp.broadcast_to(t, (bsz * HEADS, lp, DH))

    bb = lane_bcast(gb[:, 0:HEADS])
    gcb = lane_bcast(gcs[:, HEADS:2 * HEADS])
    gct = jnp.transpose(gcs[:, HEADS:2 * HEADS].reshape(bsz, nchunk, CH, HEADS), (0, 3, 1, 2)).reshape(
        bsz * HEADS, nchunk, 1, CH)
    u_loc, l1, l2, l3, w_t, k_dec, tinv_all, g_last = _gdn_local_fwd(qkv, gcb, bb, gct, bsz, lp)
    o, v_new, states = _gdn_scan_fwd(u_loc, l1, l2, g_last, bsz, lp)
    o, v_new = o.reshape(rows, GW), v_new.reshape(rows, GW)
    mix_g, mix_s = _mix_heads(proj, o, gdn_norm, convs_full, bsz, lp, padf)
    mixin = jnp.concatenate([mix_g, mix_s], axis=1)
    _, (l_out,) = _split_wait("gather_w_out_wait", grp_out, _gather_pattern, mixin)
    wout = l_out.reshape(D, D)
    (mix,) = _mm("proj_out", [(mixin, wout)], "nn", tm, 512, D)
    _, (l_gate, l_up, l_down) = _split_wait("gather_ffn_wait", grp_ffn, _gather_pattern, mix)
    wgate_t, wup_t = l_gate.reshape(FF, D), l_up.reshape(FF, D)
    wdown = l_down.reshape(FF, D)
    h1, u2 = _mid_norms(h0, mix, mix_post_norm, ffn_pre_norm, bsz, lp, tr)
    gate, up, act = _ffn_up(u2, wgate_t, wup_t, _div_tile(rows, 528), 1408)
    (ffn,) = _mm("ffn_down", [(act, wdown)], "nn", tm, 512, 1408)

    dy, dffn, d_g4, loss_part = _loss_head(h1, ffn, tgt, ffn_post_norm, bsz, lp, tr, first_real)
    (dwdown,) = _mm("dw_down", [(act, dffn)], "tn", 1408, 512, tm, out_dtypes=(BF16,))

    def swiglu_bwd(da, g, u):
        g, u = g.astype(F32), u.astype(F32)
        return da * u * _dsilu(g), da * _silu(g)

    dgate, dup = _mm("d_act", [(dffn, wdown)], "nt", tm, 1408, D, out_dtypes=(BF16, BF16),
                     epilogue=swiglu_bwd, extras=(gate, up))
    (dwgate_t,) = _mm("dw_gate", [(dgate, u2)], "tn", 1408, D, tm, out_dtypes=(BF16,))
    (dwup_t,) = _mm("dw_up", [(dup, u2)], "tn", 1408, D, tm, out_dtypes=(BF16,))
    (du2,) = _mm("d_u2", [(dgate, wgate_t), (dup, wup_t)], "nn", tm, 512, 1408)
    dh1, dmix, d_g3, d_g2 = _mid_norms_bwd(h1, mix, du2, dy, ffn_pre_norm, mix_post_norm, bsz, lp, tr)
    (dwout,) = _mm("dw_out", [(mixin, dmix)], "tn", D, 512, tm, out_dtypes=(BF16,))

    def row_blocks(g):
        return g.reshape(4, 2, -1, g.shape[1])

    core = jnp.reshape(cc, (1,)).astype(jnp.int32)
    chip = jnp.reshape(2 * cx + cy, (1,)).astype(jnp.int32)

    def reduce_start(tag, grads):
        (grp,), tok = _split_start("rs_sibling_start_" + tag, [grads], _sibling_pattern, lambda s: s[:1] + s[2:])
        return grp, tok

    def reduce_mid(tag, grp, nms, after):
        mine, lands = _split_wait("rs_sibling_wait_" + tag, grp, _sibling_pattern, after)
        parts = [_pair_sum("pair_sum_" + nm, g, r, core) for nm, g, r in zip(nms, mine, lands)]
        (grp2,), tok = _split_start("rs_chip_start_" + tag, [parts], _chip_pattern, lambda s: (3,) + s[1:])
        return grp2, tok

    def reduce_end(tag, grp2, after):
        return _split_wait("rs_chip_wait_" + tag, grp2, _chip_pattern, after)

    names_ffn = ["w_down", "w_gate", "w_up", "w_out"]
    rs_ffn, tok = reduce_start("ffn", [row_blocks(dwdown), row_blocks(dwgate_t), row_blocks(dwup_t), row_blocks(dwout)])
    (dmixin,) = _mm("d_mixin", [(dmix, wout)], "nt", tm, 512, D, after=tok)
    d_o, dz, dsx, dsb, dsc, d_gn, d_convs = _mix_heads_bwd(proj, o, gdn_norm, convs_full, dmixin, bsz, lp, padf)
    rs_ffn, tok = reduce_mid("ffn", rs_ffn, names_ffn, d_o)
    dv_new, dstates = _gdn_scan_bwd(d_o, l3, w_t, k_dec, g_last + tok[0, 0], bsz, lp)
    dq, dk, dv, dg_b, dbeta_b = _gdn_local_bwd(qkv, gcb, bb, gct, tinv_all, states, dstates, v_new,
                                               dv_new.reshape(rows, GW), d_o, bsz, lp)
    dpqkv, d_convq = _gdn_prep_bwd(proj, convq_full, dq, dk, dv, bsz, lp, padf)
    dba, d_scal = _gates_bwd(ba, alog_v, dtb_v, dbeta_b, dg_b, gb, bsz, lp, padf)
    dproj = jnp.concatenate([dpqkv, dz, dba, dsx, dsb, dsc], axis=1)
    (dwin_t,) = _mm("dw_in", [(dproj, u1)], "tn", 768, D, tm, out_dtypes=(BF16,))
    parts_ffn, recv_ffn = reduce_end("ffn", rs_ffn, dwin_t)
    dwin_t = jnp.concatenate([dwin_t[:BA0 + 2 * HEADS], dwin_t[SX0:]], axis=0)
    rs_in, tok = reduce_start("in", [row_blocks(dwin_t)])
    (du1,) = _mm("d_u1", [(dproj, win_t)], "nn", tm, D, 768, after=tok)
    rs_in, tok = reduce_mid("in", rs_in, ["w_in"], du1)
    dh0, d_g1 = _pre_norm_bwd(h0, du1, dh1, mix_pre_norm + tok[0:1, 0:1], bsz, lp, tr)
    dh0 = dh0.reshape(bsz, lp, D)
    grad_x = dh0[:, first_real:]

    d_alog = d_scal[0:1, HEADS:2 * HEADS]
    d_dtb = d_scal[1:2, HEADS:2 * HEADS]
    zero = jnp.zeros_like
    slab = jnp.stack(
        [_pack_small([d_g1, d_g2, d_g3, d_g4], dh0[0, padf:first_real], d_convq[0], d_convs, d_alog, d_dtb, d_gn,
                     loss_part)]
        + [_pack_small([zero(d_g1)] * 4, dh0[b, padf:first_real], d_convq[b], zero(d_convs), zero(d_alog),
                       zero(d_dtb), zero(d_gn), zero(loss_part)) for b in range(1, bsz)])
    (grp_small,), _ = _split_start("small_grads_start", [[slab]], _gather_pattern, lambda s: (N_DEV,) + s)

    big_w = {"w_in": (w_in, m_w_in, v_w_in), "w_out": (w_out, m_w_out, v_w_out), "w_gate": (w_gate, m_w_gate, v_w_gate),
             "w_up": (w_up, m_w_up, v_w_up), "w_down": (w_down, m_w_down, v_w_down)}

    def update(nm, parts, recv):
        w, m, v = big_w[nm]
        if nm in ("w_out", "w_down"):
            return _adamw_reduced("adamw_" + nm, parts, chip, recv, w[0], m[0], v[0])
        g = _grad_sum("grad_sum_" + nm, parts, chip, recv).T
        return _adamw("adamw_" + nm, g[None], w[0], m[0], v[0])

    big_out = {nm: update(nm, p, r) for nm, p, r in zip(names_ffn, parts_ffn, recv_ffn)}
    _, (slabs,) = _split_wait("small_grads_wait", grp_small, _gather_pattern, big_out["w_out"][1])
    slabs = slabs.reshape((N_DEV * bsz,) + slabs.shape[2:])
    off_meta = dev * DH
    off_cq = dev * (3 * GW // N_DEV)
    off_cs = dev * (SW // N_DEV)

    def shard_of(s):
        gains, meta, cq, cs, al, db, gn = _unpack_small(s)
        return gains, lax.dynamic_slice_in_dim(meta, off_meta, DH, 1), \
            lax.dynamic_slice_in_dim(cq, off_cq, 3 * GW // N_DEV, 1), \
            lax.dynamic_slice_in_dim(cs, off_cs, SW // N_DEV, 1), al, db, gn, s[27:28, 3 * DH:4 * DH]

    slabs_local = jax.vmap(lambda s: _pack_small(*shard_of(s)))(slabs)
    w_small = _pack_small([mix_pre_norm, mix_post_norm, ffn_pre_norm, ffn_post_norm], meta_tokens, conv_qkv[0],
                          conv_sc[0], a_log, dt_bias, gdn_norm)
    m_small = _pack_small([m_mix_pre_norm, m_mix_post_norm, m_ffn_pre_norm, m_ffn_post_norm], m_meta_tokens,
                          m_conv_qkv[0], m_conv_sc[0], m_a_log, m_dt_bias, m_gdn_norm)
    v_small = _pack_small([v_mix_pre_norm, v_mix_post_norm, v_ffn_pre_norm, v_ffn_post_norm], v_meta_tokens,
                          v_conv_qkv[0], v_conv_sc[0], v_a_log, v_dt_bias, v_gdn_norm)
    small_out = _adamw("adamw_small", slabs_local, w_small, m_small, v_small)
    loss = small_out[0][27, 3 * DH]
    parts_in, recv_in = reduce_end("in", rs_in, small_out[1])
    big_out["w_in"] = update("w_in", parts_in[0], recv_in[0])
    names = ["w_in", "w_out", "w_gate", "w_up", "w_down"]

    def small_leaves(slab_out):
        gains, meta, cq, cs, al, db, gn = _unpack_small(slab_out)
        return {"meta_tokens": meta[:, :DH], "mix_pre_norm": gains[0], "mix_post_norm": gains[1],
                "ffn_pre_norm": gains[2], "ffn_post_norm": gains[3], "conv_qkv": cq[None, :, :3 * GW // N_DEV],
                "a_log": al, "dt_bias": db, "gdn_norm": gn, "conv_sc": cs[None, :, :SW // N_DEV]}

    order = ["meta_tokens", "mix_pre_norm", "mix_post_norm", "ffn_pre_norm", "ffn_post_norm", "w_in", "conv_qkv",
             "a_log", "dt_bias", "gdn_norm", "conv_sc", "w_out", "w_gate", "w_up", "w_down"]
    outs = [loss, grad_x]
    for kind in range(4):
        leaves = small_leaves(small_out[kind])
        for nm in names:
            leaves[nm] = big_out[nm][kind][None]
        outs += [leaves[nm] for nm in order]
    return tuple(outs)
```

```python
import functools

import jax
import jax.numpy as jnp
from jax import lax
from jax.experimental import pallas as pl
from jax.experimental.pallas import tpu as pltpu

F32 = jnp.float32
BF16 = jnp.bfloat16
HIGHEST = lax.Precision.HIGHEST
MESH = pl.DeviceIdType.MESH

D = 1024
N_META = 16
HEADS = 4
DH = 128
GW = HEADS * DH
SW = D - GW
FF = 2816
CH = 64
KQ = 4
KS = 3
EPS = 1e-6
IN_W = 3 * GW + GW + 2 * HEADS + 3 * SW
Q0, K0, V0, Z0, BA0, SX0, SB0, SC0, PW = 0, 512, 1024, 1536, 2048, 2304, 2816, 3328, 3840
BA_W = SX0 - BA0
N_DEV = 8
VMEM_LIMIT = 56 * 1024 * 1024

ADAM_LR, ADAM_B1, ADAM_B2, ADAM_EPS, ADAM_WD, ADAM_STEP = 0.001, 0.9, 0.999, 1e-08, 0.01, 10


def _cparams(sem):
    return pltpu.CompilerParams(dimension_semantics=sem, vmem_limit_bytes=VMEM_LIMIT)


def _div_tile(n, target, mult=16):
    best = None
    for t in range(mult, min(n, target) + 1, mult):
        if n % t == 0:
            best = t
    assert best is not None, (n, target)
    return best


def _silu(x):
    return x * jax.nn.sigmoid(x)


def _dsilu(x):
    s = jax.nn.sigmoid(x)
    return s * (1.0 + x * (1.0 - s))


def _bdot(a, b, dims):
    return lax.dot_general(a.astype(BF16), b.astype(BF16), (dims, ((), ())), preferred_element_type=F32)


NN = ((1,), (0,))
NT = ((1,), (1,))
TN = ((0,), (0,))


def _fdot(a, b, dims=NN):
    return lax.dot_general(a, b, (dims, ((), ())), precision=HIGHEST, preferred_element_type=F32)


def _mm(name, pairs, mode, tm, tn, tk, out_dtypes=(F32,), epilogue=None, extras=(), after=None):
    a0, b0 = pairs[0]
    if mode == "nn":
        (m, k), n = a0.shape, b0.shape[1]
    elif mode == "nt":
        (m, k), n = a0.shape, b0.shape[0]
    else:
        (k, m), n = a0.shape, b0.shape[1]
    assert m % tm == 0 and n % tn == 0 and k % tk == 0, (name, m, n, k, tm, tn, tk)
    nk = k // tk
    npair = len(pairs)
    nex = len(extras)
    nout = len(out_dtypes)
    ntok = 0 if after is None else 1
    dims = {"nn": NN, "nt": NT, "tn": TN}[mode]

    def body(*refs):
        ab = refs[:2 * npair]
        ex = refs[2 * npair:2 * npair + nex]
        outs = refs[2 * npair + nex + ntok:2 * npair + nex + ntok + nout]
        acc_ref = refs[-1]
        kk = pl.program_id(2)

        part = None
        for p in range(npair):
            d = lax.dot_general(ab[2 * p][...], ab[2 * p + 1][...], (dims, ((), ())), preferred_element_type=F32)
            part = d if part is None else part + d

        def finish(acc):
            res = epilogue(acc, *[e[...] for e in ex]) if epilogue is not None else (acc,)
            for o, r in zip(outs, res):
                o[...] = r.astype(o.dtype)

        if nk == 1:
            finish(part)
        else:
            @pl.when(kk == 0)
            def _():
                acc_ref[...] = part

            @pl.when(kk > 0)
            def _():
                acc_ref[...] += part

            @pl.when(kk == nk - 1)
            def _():
                finish(acc_ref[...])

    if mode == "nn":
        a_spec = pl.BlockSpec((tm, tk), lambda i, j, q: (i, q))
        b_spec = pl.BlockSpec((tk, tn), lambda i, j, q: (q, j))
    elif mode == "nt":
        a_spec = pl.BlockSpec((tm, tk), lambda i, j, q: (i, q))
        b_spec = pl.BlockSpec((tn, tk), lambda i, j, q: (j, q))
    else:
        a_spec = pl.BlockSpec((tk, tm), lambda i, j, q: (q, i))
        b_spec = pl.BlockSpec((tk, tn), lambda i, j, q: (q, j))
    o_spec = pl.BlockSpec((tm, tn), lambda i, j, q: (i, j))
    flat = [t for pr in pairs for t in pr]
    res = pl.pallas_call(
        body, name=name,
        grid=(m // tm, n // tn, nk),
        in_specs=[a_spec, b_spec] * npair + [o_spec] * nex + [pl.BlockSpec((8, DH), lambda i, j, q: (0, 0))] * ntok,
        out_specs=[o_spec] * nout,
        out_shape=[jax.ShapeDtypeStruct((m, n), dt) for dt in out_dtypes],
        scratch_shapes=[pltpu.VMEM((tm, tn) if nk > 1 else (8, DH), F32)],
        compiler_params=_cparams(("parallel", "parallel", "arbitrary")),
    )(*flat, *extras, *([] if after is None else [after]))
    return res


def _ffn_up(u2, wg, wu, tm, tn):
    m, k = u2.shape
    n = wg.shape[0]

    def body(a_ref, g_ref, u_ref, gate_ref, up_ref, act_ref):
        a = a_ref[...]
        g = lax.dot_general(a, g_ref[...], (NT, ((), ())), preferred_element_type=F32)
        u = lax.dot_general(a, u_ref[...], (NT, ((), ())), preferred_element_type=F32)
        gate_ref[...] = g.astype(BF16)
        up_ref[...] = u.astype(BF16)
        act_ref[...] = (_silu(g) * u).astype(BF16)

    o_spec = pl.BlockSpec((tm, tn), lambda j, i: (i, j))
    w_spec = pl.BlockSpec((tn, k), lambda j, i: (j, 0))
    return pl.pallas_call(
        body, name="ffn_up", grid=(n // tn, m // tm),
        in_specs=[pl.BlockSpec((tm, k), lambda j, i: (i, 0)), w_spec, w_spec],
        out_specs=[o_spec, o_spec, o_spec],
        out_shape=[jax.ShapeDtypeStruct((m, n), BF16)] * 3,
        compiler_params=_cparams(("parallel", "parallel")),
    )(u2, wg, wu)


def _rms(x, gain):
    r = lax.rsqrt(jnp.mean(x * x, axis=-1, keepdims=True) + EPS)
    return x * r * gain, r


def _rms_bwd(x, gain, dy):
    r = lax.rsqrt(jnp.mean(x * x, axis=-1, keepdims=True) + EPS)
    dyw = dy * gain
    dx = r * dyw - x * (r * r * r) * jnp.mean(dyw * x, axis=-1, keepdims=True)
    return dx, dy * x * r


def _row_specs(tr, width, nb):
    return pl.BlockSpec((tr, width), lambda b, j: (b * nb + j, 0))


def _vec_spec(width):
    return pl.BlockSpec((1, width), lambda b, j: (0, 0))


def _first_step(b, j):
    return jnp.logical_and(b == 0, j == 0)


def _pre_norm(h0, gain, bsz, lp, tr):
    nb = lp // tr

    def body(h_ref, g_ref, u_ref):
        u_ref[...] = _rms(h_ref[...], g_ref[...])[0].astype(BF16)

    return pl.pallas_call(
        body, name="mix_pre_norm", grid=(bsz, nb),
        in_specs=[_row_specs(tr, D, nb), _vec_spec(D)],
        out_specs=_row_specs(tr, D, nb),
        out_shape=jax.ShapeDtypeStruct(h0.shape, BF16),
        compiler_params=_cparams(("parallel", "parallel")),
    )(h0, gain)


def _mid_norms(h0, mix, g_post, g_pre, bsz, lp, tr):
    nb = lp // tr

    def body(h_ref, m_ref, gp_ref, gq_ref, h1_ref, u2_ref):
        h1 = h_ref[...] + _rms(m_ref[...], gp_ref[...])[0]
        h1_ref[...] = h1
        u2_ref[...] = _rms(h1, gq_ref[...])[0].astype(BF16)

    rs = _row_specs(tr, D, nb)
    return pl.pallas_call(
        body, name="mid_norms", grid=(bsz, nb),
        in_specs=[rs, rs, _vec_spec(D), _vec_spec(D)],
        out_specs=[rs, rs],
        out_shape=[jax.ShapeDtypeStruct(h0.shape, F32), jax.ShapeDtypeStruct(h0.shape, BF16)],
        compiler_params=_cparams(("parallel", "parallel")),
    )(h0, mix, g_post, g_pre)


def _loss_head(h1, ffn, tgt, g_post, bsz, lp, tr, first_real):
    nb = lp // tr

    def body(h_ref, f_ref, t_ref, g_ref, dy_ref, dffn_ref, dg_ref, loss_ref):
        b, j = pl.program_id(0), pl.program_id(1)
        f = f_ref[...]
        g = g_ref[...]
        y = h_ref[...] + _rms(f, g)[0]
        t = j * tr + lax.broadcasted_iota(jnp.int32, (tr, 1), 0)
        err = jnp.where(t >= first_real, y - t_ref[...], 0.0)
        dy = err * (1.0 / D)
        dy_ref[...] = dy
        dx, dgc = _rms_bwd(f, g, dy)
        dffn_ref[...] = dx.astype(BF16)
        lsum = jnp.sum(jnp.sum(err * err, axis=0, keepdims=True), axis=1, keepdims=True) * (0.5 / D)

        @pl.when(_first_step(b, j))
        def _():
            dg_ref[...] = jnp.zeros_like(dg_ref)
            loss_ref[...] = jnp.zeros_like(loss_ref)

        dg_ref[...] += jnp.sum(dgc, axis=0, keepdims=True)
        loss_ref[...] += jnp.broadcast_to(lsum, loss_ref.shape)

    rs = _row_specs(tr, D, nb)
    return pl.pallas_call(
        body, name="loss_head", grid=(bsz, nb),
        in_specs=[rs, rs, rs, _vec_spec(D)],
        out_specs=[rs, rs, _vec_spec(D), _vec_spec(DH)],
        out_shape=[jax.ShapeDtypeStruct(h1.shape, F32), jax.ShapeDtypeStruct(h1.shape, BF16),
                   jax.ShapeDtypeStruct((1, D), F32), jax.ShapeDtypeStruct((1, DH), F32)],
        compiler_params=_cparams(("arbitrary", "arbitrary")),
    )(h1, ffn, tgt, g_post)


def _mid_norms_bwd(h1, mix, du2, dy, g_pre, g_post, bsz, lp, tr):
    nb = lp // tr

    def body(h_ref, m_ref, du_ref, dy_ref, gq_ref, gp_ref, dh1_ref, dmix_ref, dgq_ref, dgp_ref):
        b, j = pl.program_id(0), pl.program_id(1)
        dx, dgq = _rms_bwd(h_ref[...], gq_ref[...], du_ref[...])
        dh1 = dy_ref[...] + dx
        dh1_ref[...] = dh1
        dm, dgp = _rms_bwd(m_ref[...], gp_ref[...], dh1)
        dmix_ref[...] = dm.astype(BF16)

        @pl.when(_first_step(b, j))
        def _():
            dgq_ref[...] = jnp.zeros_like(dgq_ref)
            dgp_ref[...] = jnp.zeros_like(dgp_ref)

        dgq_ref[...] += jnp.sum(dgq, axis=0, keepdims=True)
        dgp_ref[...] += jnp.sum(dgp, axis=0, keepdims=True)

    rs = _row_specs(tr, D, nb)
    return pl.pallas_call(
        body, name="mid_norms_bwd", grid=(bsz, nb),
        in_specs=[rs, rs, rs, rs, _vec_spec(D), _vec_spec(D)],
        out_specs=[rs, rs, _vec_spec(D), _vec_spec(D)],
        out_shape=[jax.ShapeDtypeStruct(h1.shape, F32), jax.ShapeDtypeStruct(h1.shape, BF16),
                   jax.ShapeDtypeStruct((1, D), F32), jax.ShapeDtypeStruct((1, D), F32)],
        compiler_params=_cparams(("arbitrary", "arbitrary")),
    )(h1, mix, du2, dy, g_pre, g_post)


def _pre_norm_bwd(h0, du1, dh1, gain, bsz, lp, tr, padf):
    nb = lp // tr
    assert padf % 8 == 0 and padf + N_META <= tr

    def body(h_ref, du_ref, dh1_ref, g_ref, dh0_ref, dg_ref, dmeta_ref):
        b, j = pl.program_id(0), pl.program_id(1)
        dx, dgc = _rms_bwd(h_ref[...], g_ref[...], du_ref[...])
        dh0 = dh1_ref[...] + dx
        dh0_ref[...] = dh0

        @pl.when(_first_step(b, j))
        def _():
            dg_ref[...] = jnp.zeros_like(dg_ref)
            dmeta_ref[...] = jnp.zeros_like(dmeta_ref)

        dg_ref[...] += jnp.sum(dgc, axis=0, keepdims=True)

        @pl.when(j == 0)
        def _():
            dmeta_ref[...] += dh0[padf:padf + N_META, :]

    rs = _row_specs(tr, D, nb)
    return pl.pallas_call(
        body, name="pre_norm_bwd", grid=(bsz, nb),
        in_specs=[rs, rs, rs, _vec_spec(D)],
        out_specs=[rs, _vec_spec(D), pl.BlockSpec((N_META, D), lambda b, j: (0, 0))],
        out_shape=[jax.ShapeDtypeStruct(h0.shape, F32), jax.ShapeDtypeStruct((1, D), F32),
                   jax.ShapeDtypeStruct((N_META, D), F32)],
        compiler_params=_cparams(("arbitrary", "arbitrary")),
    )(h0, du1, dh1, gain)


def _shift_down(x, s, lp):
    return x if s == 0 else pltpu.roll(x, s, axis=0)


def _shift_up(x, s, lp):
    return x if s == 0 else pltpu.roll(x, lp - s, axis=0)


def _conv_fwd(x, w, taps, lp):
    y = None
    for i in range(taps):
        term = _shift_down(x, taps - 1 - i, lp) * w[i:i + 1, :]
        y = term if y is None else y + term
    return y


def _conv_bwd(x, w, dy, taps, lp):
    dx = None
    dws = []
    for i in range(taps):
        s = taps - 1 - i
        term = _shift_up(dy, s, lp) * w[i:i + 1, :]
        dx = term if dx is None else dx + term
        dws.append(jnp.sum(dy * _shift_down(x, s, lp), axis=0, keepdims=True))
    return dx, jnp.concatenate(dws, axis=0)


def _slab(lp, col0):
    return pl.BlockSpec((lp, DH), lambda b, s: (b, col0 // DH + s))


def _qkv_act(c, s):
    y = _silu(c)
    r = lax.rsqrt(jnp.sum(y * y, axis=-1, keepdims=True) + EPS)
    scale = jnp.where(s < HEADS, DH ** -0.5, 1.0)
    return y, r, scale


def _gdn_prep(proj, conv_qkv, bsz, lp, padf):
    def body(p_ref, w_ref, o_ref):
        s = pl.program_id(1)
        c = _conv_fwd(p_ref[...].astype(F32), w_ref[...], KQ, lp)
        y, r, scale = _qkv_act(c, s)
        out = jnp.where(s < 2 * HEADS, y * (r * scale), y)
        t = lax.broadcasted_iota(jnp.int32, (lp, 1), 0)
        o_ref[...] = jnp.where(t >= padf, out, 0.0)

    return pl.pallas_call(
        body, name="gdn_prep", grid=(bsz, 3 * HEADS),
        in_specs=[_slab(lp, Q0), pl.BlockSpec((KQ, DH), lambda b, s: (0, s))],
        out_specs=_slab(lp, 0),
        out_shape=jax.ShapeDtypeStruct((bsz * lp, 3 * GW), F32),
        compiler_params=_cparams(("parallel", "parallel")),
    )(proj, conv_qkv)


def _gdn_prep_bwd(proj, conv_qkv, dq, dk, dv, bsz, lp, padf):
    def body(p_ref, w_ref, dq_ref, dk_ref, dv_ref, dp_ref, dw_ref):
        b, s = pl.program_id(0), pl.program_id(1)
        x = p_ref[...].astype(F32)
        w = w_ref[...]
        c = _conv_fwd(x, w, KQ, lp)
        y, r, scale = _qkv_act(c, s)
        d = jnp.where(s < HEADS, dq_ref[...], jnp.where(s < 2 * HEADS, dk_ref[...], dv_ref[...]))
        dn = d * scale
        n = y * r
        dy_norm = r * (dn - n * jnp.sum(dn * n, axis=-1, keepdims=True))
        dyy = jnp.where(s < 2 * HEADS, dy_norm, dn)
        t = lax.broadcasted_iota(jnp.int32, (lp, 1), 0)
        dc = jnp.where(t >= padf, dyy * _dsilu(c), 0.0)
        dx, dw = _conv_bwd(x, w, dc, KQ, lp)
        dp_ref[...] = jnp.where(t >= padf, dx, 0.0).astype(BF16)

        @pl.when(jnp.logical_and(b == 0, s == 0))
        def _():
            dw_ref[...] = jnp.zeros_like(dw_ref)

        col_block = jnp.right_shift(lax.broadcasted_iota(jnp.int32, (KQ, 3 * GW), 1), DH.bit_length() - 1)
        dw_ref[...] += jnp.where(col_block == s, jnp.tile(dw, (1, 3 * HEADS)), 0.0)

    sl = lambda col0: pl.BlockSpec((lp, DH), lambda b, s: (b, col0 // DH + s))
    part = lambda k: pl.BlockSpec((lp, DH), lambda b, s: (b, jnp.clip(s - k * HEADS, 0, HEADS - 1)))
    return pl.pallas_call(
        body, name="gdn_prep_bwd", grid=(bsz, 3 * HEADS),
        in_specs=[sl(Q0), pl.BlockSpec((KQ, DH), lambda b, s: (0, s)), part(0), part(1), part(2)],
        out_specs=[sl(0), pl.BlockSpec((KQ, 3 * GW), lambda b, s: (0, 0))],
        out_shape=[jax.ShapeDtypeStruct((bsz * lp, 3 * GW), BF16), jax.ShapeDtypeStruct((KQ, 3 * GW), F32)],
        compiler_params=_cparams(("arbitrary", "arbitrary")),
    )(proj, conv_qkv, dq, dk, dv)


def _softplus(x):
    e = jnp.exp(-jnp.abs(x))
    one_e = 1.0 + e
    l1p = jnp.where(one_e == 1.0, e, jnp.log(one_e) * (e / (one_e - 1.0)))
    return jnp.maximum(x, 0.0) + l1p


def _tri(n, kind):
    i = lax.broadcasted_iota(jnp.int32, (n, n), 0)
    j = lax.broadcasted_iota(jnp.int32, (n, n), 1)
    return {"incl": i >= j, "strict": i > j, "upper": i <= j}[kind]


def _gates(proj, alog_v, dtb_v, bsz, lp, padf):
    nchunk = lp // CH

    def body(p_ref, al_ref, dt_ref, gb_ref, cs_ref):
        x = p_ref[...].astype(F32)
        lane = lax.broadcasted_iota(jnp.int32, (lp, DH), 1)
        t = lax.broadcasted_iota(jnp.int32, (lp, DH), 0)
        beta = jax.nn.sigmoid(x)
        g = -jnp.exp(al_ref[...]) * _softplus(x + dt_ref[...])
        gb = jnp.where(lane < HEADS, beta, jnp.where(lane < 2 * HEADS, g, 0.0))
        gb_ref[...] = jnp.where(t >= padf, gb, 0.0)
        ltri = _tri(CH, "incl").astype(F32)

        def step(n, carry):
            r0 = pl.multiple_of(n * CH, CH)
            cs_ref[pl.ds(r0, CH), :] = _fdot(ltri, gb_ref[pl.ds(r0, CH), :])
            return carry

        lax.fori_loop(0, nchunk, step, 0)

    blk = pl.BlockSpec((lp, DH), lambda b: (b, 0))
    vec = pl.BlockSpec((1, DH), lambda b: (0, 0))
    out = pl.BlockSpec((lp, DH), lambda b: (b, 0))
    return pl.pallas_call(
        body, name="gates", grid=(bsz,),
        in_specs=[blk, vec, vec], out_specs=[out, out],
        out_shape=[jax.ShapeDtypeStruct((bsz * lp, DH), F32)] * 2,
        compiler_params=_cparams(("parallel",)),
    )(proj, alog_v, dtb_v)


def _gates_bwd(proj, alog_v, dtb_v, dbeta_b, dg_b, gb, bsz, lp, padf):
    def body(p_ref, al_ref, dt_ref, db_ref, dg_ref, gb_ref, dl_ref, dv_ref):
        b = pl.program_id(0)
        x = p_ref[...].astype(F32)
        lane = lax.broadcasted_iota(jnp.int32, (lp, DH), 1)
        t = lax.broadcasted_iota(jnp.int32, (lp, DH), 0)
        dbeta = jnp.zeros((lp, DH), F32)
        dg = jnp.zeros((lp, DH), F32)
        for h in range(HEADS):
            dbeta = jnp.where(lane == h, db_ref[h], dbeta)
            dg = jnp.where(lane == HEADS + h, dg_ref[h], dg)
        real = t >= padf
        dbeta = jnp.where(real, dbeta, 0.0)
        dg = jnp.where(real, dg, 0.0)
        sb = jax.nn.sigmoid(x)
        neg_ea = -jnp.exp(al_ref[...])
        dsp = dg * neg_ea * jax.nn.sigmoid(x + dt_ref[...])
        dl = dbeta * sb * (1.0 - sb) + dsp
        dl_ref[...] = jnp.concatenate([dl, jnp.zeros((lp, BA_W - DH), F32)], axis=1).astype(BF16)
        dalog = jnp.sum(dg * gb_ref[...], axis=0, keepdims=True)
        ddt = jnp.sum(dsp, axis=0, keepdims=True)

        @pl.when(b == 0)
        def _():
            dv_ref[...] = jnp.zeros_like(dv_ref)

        dv_ref[...] += jnp.concatenate([dalog, ddt], axis=0)

    blk = pl.BlockSpec((lp, DH), lambda b: (b, 0))
    vec = pl.BlockSpec((1, DH), lambda b: (0, 0))
    hb = pl.BlockSpec((HEADS, lp, DH), lambda b: (b, 0, 0))
    return pl.pallas_call(
        body, name="gates_bwd", grid=(bsz,),
        in_specs=[blk, vec, vec, hb, hb, pl.BlockSpec((lp, DH), lambda b: (b, 0))],
        out_specs=[pl.BlockSpec((lp, BA_W), lambda b: (b, 0)), pl.BlockSpec((2, DH), lambda b: (0, 0))],
        out_shape=[jax.ShapeDtypeStruct((bsz * lp, BA_W), BF16), jax.ShapeDtypeStruct((2, DH), F32)],
        compiler_params=_cparams(("arbitrary",)),
    )(proj, alog_v, dtb_v, dbeta_b, dg_b, gb)


def _mix_heads(proj, o, gdn_norm, conv_sc, bsz, lp, padf):
    def body(o_ref, z_ref, gn_ref, sx_ref, sb_ref, sc_ref, w_ref, og_ref, os_ref):
        og_ref[...] = (_rms(o_ref[...], gn_ref[...])[0] * _silu(z_ref[...].astype(F32))).astype(BF16)
        conv = _conv_fwd(sc_ref[...].astype(F32) * sx_ref[...].astype(F32), w_ref[...], KS, lp)
        t = lax.broadcasted_iota(jnp.int32, (lp, 1), 0)
        os_ref[...] = jnp.where(t >= padf, sb_ref[...].astype(F32) * conv, 0.0).astype(BF16)

    half = _slab(lp, 0)
    return pl.pallas_call(
        body, name="mix_heads", grid=(bsz, HEADS),
        in_specs=[half, _slab(lp, Z0), pl.BlockSpec((1, DH), lambda b, s: (0, 0)),
                  _slab(lp, SX0), _slab(lp, SB0), _slab(lp, SC0), pl.BlockSpec((KS, DH), lambda b, s: (0, s))],
        out_specs=[half, half],
        out_shape=[jax.ShapeDtypeStruct((bsz * lp, GW), BF16)] * 2,
        compiler_params=_cparams(("parallel", "parallel")),
    )(o, proj, gdn_norm, proj, proj, proj, conv_sc)


def _mix_heads_bwd(proj, o, gdn_norm, conv_sc, dmixin, bsz, lp, padf):
    def body(o_ref, z_ref, gn_ref, sx_ref, sb_ref, sc_ref, w_ref, dg_ref, ds_ref,
             do_ref, dz_ref, dsx_ref, dsb_ref, dsc_ref, dgn_ref, dw_ref):
        s, b = pl.program_id(0), pl.program_id(1)
        t = lax.broadcasted_iota(jnp.int32, (lp, 1), 0)
        real = t >= padf
        o, z, gn, d = o_ref[...], z_ref[...].astype(F32), gn_ref[...], dg_ref[...]
        sz = _silu(z)
        on = _rms(o, gn)[0]
        dz_ref[...] = (d * on * _dsilu(z)).astype(BF16)
        dox, dgn = _rms_bwd(o, gn, d * sz)
        do_ref[...] = dox
        sx, sb, sc = sx_ref[...].astype(F32), sb_ref[...].astype(F32), sc_ref[...].astype(F32)
        w, e = w_ref[...], ds_ref[...]
        e = jnp.where(real, e, 0.0)
        xin = sc * sx
        conv = _conv_fwd(xin, w, KS, lp)
        dsb_ref[...] = (e * conv).astype(BF16)
        dxin, dw = _conv_bwd(xin, w, e * sb, KS, lp)
        dxin = jnp.where(real, dxin, 0.0)
        dsx_ref[...] = (dxin * sc).astype(BF16)
        dsc_ref[...] = (dxin * sx).astype(BF16)

        @pl.when(jnp.logical_and(s == 0, b == 0))
        def _():
            dgn_ref[...] = jnp.zeros_like(dgn_ref)

        @pl.when(b == 0)
        def _():
            dw_ref[...] = jnp.zeros_like(dw_ref)

        dgn_ref[...] += jnp.sum(dgn, axis=0, keepdims=True)
        dw_ref[...] += dw

    sl = lambda col0: pl.BlockSpec((lp, DH), lambda s, b: (b, col0 // DH + s))
    half = sl(0)
    return pl.pallas_call(
        body, name="mix_heads_bwd", grid=(HEADS, bsz),
        in_specs=[half, sl(Z0), pl.BlockSpec((1, DH), lambda s, b: (0, 0)), sl(SX0), sl(SB0), sl(SC0),
                  pl.BlockSpec((KS, DH), lambda s, b: (0, s)), sl(0), sl(GW)],
        out_specs=[half] * 5 + [pl.BlockSpec((1, DH), lambda s, b: (0, 0)), pl.BlockSpec((KS, DH), lambda s, b: (0, s))],
        out_shape=[jax.ShapeDtypeStruct((bsz * lp, GW), F32)] + [jax.ShapeDtypeStruct((bsz * lp, GW), BF16)] * 4
        + [jax.ShapeDtypeStruct((1, DH), F32), jax.ShapeDtypeStruct((KS, SW), F32)],
        compiler_params=_cparams(("arbitrary", "arbitrary")),
    )(o, proj, gdn_norm, proj, proj, proj, conv_sc, dmixin, dmixin)


def _split_dot(a, b, dims=NN):
    ah = a.astype(BF16)
    al = (a - ah.astype(F32)).astype(BF16)
    bh = b.astype(BF16)
    bl = (b - bh.astype(F32)).astype(BF16)

    def d(x, y):
        return lax.dot_general(x, y, (dims, ((), ())), preferred_element_type=F32)

    return d(ah, bh) + (d(ah, bl) + d(al, bh))


def _eye(n):
    return (lax.broadcasted_iota(jnp.int32, (n, n), 0) == lax.broadcasted_iota(jnp.int32, (n, n), 1)).astype(F32)


def _each(f, *lists):
    return [f(*xs) for xs in zip(*lists)]


def _inv_unit_lower(a_list):
    eye = _eye(CH)
    p = _each(lambda a: -a, a_list)
    t = _each(lambda x: eye + x, p)
    p = _each(lambda x: _split_dot(x, x), p)
    for level in range(5):
        if level < 4:
            prod = _each(lambda tt, pp: _split_dot(jnp.concatenate([tt, pp], axis=0), pp), t, p)
            t = _each(lambda tt, pr: tt + pr[:CH], t, prod)
            p = _each(lambda pr: pr[CH:], prod)
        else:
            t = _each(lambda tt, pp: tt + _split_dot(tt, pp), t, p)
    return t


def _chunk_gates(gc_ref, be_ref, gt_ref, g):
    rows = pl.ds(g * CH, CH)
    gc = gc_ref[0, rows, :]
    be = be_ref[0, rows, :]
    gcl = gc_ref[0, pl.ds(g * CH + CH - 1, 1), :]
    gt = gt_ref[0, g]
    incl, upper = _tri(CH, "incl"), _tri(CH, "upper")
    dec = jnp.where(incl, jnp.exp(jnp.where(incl, gc[:, :CH] - gt, 0.0)), 0.0)
    dec_t = jnp.where(upper, jnp.exp(jnp.where(upper, gt - gc[:, :CH], 0.0)), 0.0)
    return rows, be, dec, dec_t, jnp.exp(gc), jnp.exp(gcl - gc), jnp.exp(gcl)


def _chunk_group(nchunk):
    return 3 if nchunk % 3 == 0 else 1


def _local_group(nchunk):
    return 11 if nchunk % 11 == 0 else _chunk_group(nchunk)


def _gdn_local_specs(lp, grp):
    ngrp = lp // (grp * CH)
    head = lambda col0: pl.BlockSpec((grp * CH, DH), lambda i, j: ((i // HEADS) * ngrp + j, col0 // DH + i % HEADS))
    lane_b = pl.BlockSpec((1, grp * CH, DH), lambda i, j: (i, j, 0))
    per_chunk = lambda r, c: pl.BlockSpec((1, grp, r, c), lambda i, j: (i, j, 0, 0))
    return ngrp, head, lane_b, per_chunk


def _gdn_local_fwd(qkv, gcb, bb, gct, bsz, lp):
    nchunk = lp // CH
    grp = _local_group(nchunk)
    ngrp, head, lane_b, per_chunk = _gdn_local_specs(lp, grp)
    bh = bsz * HEADS

    def body(q_ref, k_ref, v_ref, gc_ref, be_ref, gt_ref,
             u_ref, l1_ref, l2_ref, l3_ref, wt_ref, kd_ref, ti_ref, gl_ref):
        incl, strict, upper = _tri(CH, "incl"), _tri(CH, "strict"), _tri(CH, "upper")
        eye = _eye(CH)
        rows, be, dec, dec_t, eg, ekd, gl = zip(*[_chunk_gates(gc_ref, be_ref, gt_ref, g) for g in range(grp)])
        q = [q_ref[r, :] for r in rows]
        k = [k_ref[r, :] for r in rows]
        v = [v_ref[r, :] for r in rows]
        kb = _each(lambda x, y: x * y, k, be)
        a = _each(lambda x, y, d: jnp.where(strict, _bdot(x, y, NT) * d, 0.0), kb, k, dec)
        tinv = _inv_unit_lower(a)
        u = _each(lambda t, x, y: _bdot(t, x * y, NN), tinv, v, be)
        w = _each(lambda t, x, y: _bdot(t, x * y, NN), tinv, kb, eg)
        qk = _each(lambda x, y, d: jnp.where(incl, _bdot(x, y, NT) * d, 0.0), q, k, dec)
        qk_t = _each(lambda x, y, d: jnp.where(upper, _bdot(y, x, NT) * d, 0.0), q, k, dec_t)
        qd = _each(lambda x, y: x * y, q, eg)
        kd = _each(lambda x, y: x * y, k, ekd)
        kd_t = _each(lambda x: _bdot(x, eye, TN), kd)
        qd_t = _each(lambda x: _bdot(x, eye, TN), qd)
        w_t = _each(lambda x: _bdot(x, eye, TN), w)
        for g in range(grp):
            u_ref[rows[g], :] = u[g]
            l1_ref[0, g] = jnp.concatenate([w[g], qd[g]], axis=0).astype(BF16)
            l2_ref[0, g] = jnp.concatenate([qk[g], kd_t[g]], axis=0).astype(BF16)
            l3_ref[0, g] = jnp.concatenate([qk_t[g], qd_t[g]], axis=0).astype(BF16)
            wt_ref[0, g] = w_t[g].astype(BF16)
            kd_ref[0, g] = kd[g].astype(BF16)
            ti_ref[0, g] = tinv[g]
            gl_ref[0, g] = gl[g]

    r = bsz * lp
    shapes = [((r, GW), F32), ((bh, nchunk, 2 * CH, DH), BF16), ((bh, nchunk, 3 * CH, CH), BF16),
              ((bh, nchunk, 3 * CH, CH), BF16), ((bh, nchunk, DH, CH), BF16), ((bh, nchunk, CH, DH), BF16),
              ((bh, nchunk, CH, CH), F32), ((bh, nchunk, 1, DH), F32)]
    return pl.pallas_call(
        body, name="gdn_local_fwd", grid=(bh, ngrp),
        in_specs=[head(0), head(GW), head(2 * GW), lane_b, lane_b, per_chunk(1, CH)],
        out_specs=[head(0), per_chunk(2 * CH, DH), per_chunk(3 * CH, CH), per_chunk(3 * CH, CH), per_chunk(DH, CH),
                   per_chunk(CH, DH), per_chunk(CH, CH), per_chunk(1, DH)],
        out_shape=[jax.ShapeDtypeStruct(s, dt) for s, dt in shapes],
        compiler_params=_cparams(("parallel", "parallel")),
    )(qkv, qkv, qkv, gcb, bb, gct)


def _scan_specs(bsz, lp, grp, order):
    ngrp = lp // (grp * CH)
    bh = bsz * HEADS
    seq = lambda: pl.BlockSpec((bsz, grp * CH, GW), lambda s: (0, order(s, ngrp), 0))
    per_chunk = lambda r, c: pl.BlockSpec((bh, grp, r, c), lambda s: (0, order(s, ngrp), 0, 0))
    return ngrp, bh, seq, per_chunk


def _gdn_scan_fwd(u, l1, l2, gl, bsz, lp):
    nchunk = lp // CH
    grp = _chunk_group(nchunk)
    ngrp, bh, seq, per_chunk = _scan_specs(bsz, lp, grp, lambda s, n: s)

    def body(u_ref, l1_ref, l2_ref, gl_ref, o_ref, vn_ref, st_ref, s_scr):
        @pl.when(pl.program_id(0) == 0)
        def _():
            s_scr[...] = jnp.zeros_like(s_scr)

        for c in range(grp):
            rows = slice(c * CH, (c + 1) * CH)
            for i in range(bh):
                b, h = divmod(i, HEADS)
                cols = slice(h * DH, (h + 1) * DH)
                s = s_scr[i]
                st_ref[i, c] = s
                x = jnp.dot(l1_ref[i, c], s.astype(BF16), preferred_element_type=F32)
                vnew = u_ref[b, rows, cols] - x[:CH]
                y = jnp.dot(l2_ref[i, c], vnew.astype(BF16), preferred_element_type=F32)
                o_ref[b, rows, cols] = x[CH:] + y[:CH]
                vn_ref[b, rows, cols] = vnew.astype(BF16)
                s_scr[i] = s * gl_ref[i, c] + y[CH:]

    return pl.pallas_call(
        body, name="gdn_scan_fwd", grid=(ngrp,),
        in_specs=[seq(), per_chunk(2 * CH, DH), per_chunk(3 * CH, CH), per_chunk(1, DH)],
        out_specs=[seq(), seq(), per_chunk(DH, DH)],
        out_shape=[jax.ShapeDtypeStruct((bsz, lp, GW), F32), jax.ShapeDtypeStruct((bsz, lp, GW), BF16),
                   jax.ShapeDtypeStruct((bh, nchunk, DH, DH), F32)],
        scratch_shapes=[pltpu.VMEM((bh, DH, DH), F32)],
        compiler_params=_cparams(("arbitrary",)),
    )(u.reshape(bsz, lp, GW), l1, l2, gl)


def _gdn_scan_bwd(d_o, l3, wt, kd, gl, bsz, lp):
    nchunk = lp // CH
    grp = _chunk_group(nchunk)
    ngrp, bh, seq, per_chunk = _scan_specs(bsz, lp, grp, lambda s, n: n - 1 - s)

    def body(do_ref, l3_ref, wt_ref, kd_ref, gl_ref, dvn_ref, dst_ref, ds_scr):
        @pl.when(pl.program_id(0) == 0)
        def _():
            ds_scr[...] = jnp.zeros_like(ds_scr)

        for c in reversed(range(grp)):
            rows = slice(c * CH, (c + 1) * CH)
            for i in range(bh):
                b, h = divmod(i, HEADS)
                cols = slice(h * DH, (h + 1) * DH)
                ds = ds_scr[i]
                dst_ref[i, c] = ds
                p = jnp.dot(l3_ref[i, c], do_ref[b, rows, cols].astype(BF16), preferred_element_type=F32)
                dvn = p[:CH] + jnp.dot(kd_ref[i, c], ds.astype(BF16), preferred_element_type=F32)
                dvn_ref[b, rows, cols] = dvn.astype(BF16)
                ds_scr[i] = ds * gl_ref[i, c] + p[CH:] - jnp.dot(wt_ref[i, c], dvn.astype(BF16),
                                                                 preferred_element_type=F32)

    return pl.pallas_call(
        body, name="gdn_scan_bwd", grid=(ngrp,),
        in_specs=[seq(), per_chunk(3 * CH, CH), per_chunk(DH, CH), per_chunk(CH, DH), per_chunk(1, DH)],
        out_specs=[seq(), per_chunk(DH, DH)],
        out_shape=[jax.ShapeDtypeStruct((bsz, lp, GW), BF16), jax.ShapeDtypeStruct((bh, nchunk, DH, DH), F32)],
        scratch_shapes=[pltpu.VMEM((bh, DH, DH), F32)],
        compiler_params=_cparams(("arbitrary",)),
    )(d_o.reshape(bsz, lp, GW), l3, wt, kd, gl)


def _gdn_local_bwd(qkv, gcb, bb, gct, tinv_all, states, dstates, vnew, dvnew, d_o, bsz, lp):
    nchunk = lp // CH
    grp = _local_group(nchunk)
    ngrp, head, lane_b, per_chunk = _gdn_local_specs(lp, grp)
    bh = bsz * HEADS

    def body(q_ref, k_ref, v_ref, gc_ref, be_ref, gt_ref, ti_ref, st_ref, dst_ref, vn_ref, dvn_ref, do_ref,
             dq_ref, dk_ref, dv_ref, dg_ref, db_ref):
        incl, strict = _tri(CH, "incl"), _tri(CH, "strict")
        upper = _tri(CH, "upper").astype(F32)
        ones = jnp.ones((CH, DH), F32)
        last = lax.broadcasted_iota(jnp.int32, (CH, 1), 0) == CH - 1

        def rsum(x):
            return jnp.sum(x, axis=-1, keepdims=True)

        def mul(xs, ys):
            return _each(lambda x, y: x * y, xs, ys)

        rows, be, dec, _, eg, ekd, gl = zip(*[_chunk_gates(gc_ref, be_ref, gt_ref, g) for g in range(grp)])
        q = [q_ref[r, :] for r in rows]
        k = [k_ref[r, :] for r in rows]
        v = [v_ref[r, :] for r in rows]
        vnew = [vn_ref[r, :] for r in rows]
        dvn = [dvn_ref[r, :] for r in rows]
        do = [do_ref[r, :] for r in rows]
        tinv = [ti_ref[0, g] for g in range(grp)]
        s = [st_ref[0, g] for g in range(grp)]
        ds = [dst_ref[0, g] for g in range(grp)]
        kb, vb, qd, kd = mul(k, be), mul(v, be), mul(q, eg), mul(k, ekd)
        kbg = mul(kb, eg)
        a = _each(lambda x, y, d: jnp.where(strict, _bdot(x, y, NT) * d, 0.0), kb, k, dec)
        qk = _each(lambda x, y, d: jnp.where(incl, _bdot(x, y, NT) * d, 0.0), q, k, dec)
        dqk = _each(lambda x, y: jnp.where(incl, _bdot(x, y, NT), 0.0), do, vnew)
        dqd = _each(lambda x, y: _bdot(x, y, NT), do, s)
        dkd = _each(lambda x, y: _bdot(x, y, NT), vnew, ds)
        dgl = _each(lambda x, y: jnp.sum(rsum(x * y), axis=0, keepdims=True), s, ds)
        dw = _each(lambda x, y: -_bdot(x, y, NT), dvn, s)
        dvb = _each(lambda t, x: _bdot(t, x, TN), tinv, dvn)
        dkbg = _each(lambda t, x: _bdot(t, x, TN), tinv, dw)
        dt = _each(lambda x, y, z, w: _bdot(x, y, NT) + _bdot(z, w, NT), dvn, vb, dw, kbg)
        tdt = _each(lambda t, x: _split_dot(t, x, TN), tinv, dt)
        da = _each(lambda x, t: jnp.where(strict, -_split_dot(x, t, NT), 0.0), tdt, tinv)
        dp, dr = mul(da, dec), mul(dqk, dec)
        dkb = _each(lambda x, y, z, e: _bdot(x, y, NN) + z * e, dp, k, dkbg, eg)
        dk = _each(lambda p_, kb_, r_, q_, dkd_, ekd_, dkb_, be_:
                   _bdot(p_, kb_, TN) + _bdot(r_, q_, TN) + dkd_ * ekd_ + dkb_ * be_,
                   dp, kb, dr, q, dkd, ekd, dkb, be)
        dq = _each(lambda r_, k_, dqd_, eg_: _bdot(r_, k_, NN) + dqd_ * eg_, dr, k, dqd, eg)
        m = _each(lambda da_, a_, dqk_, qk_: da_ * a_ + dqk_ * qk_, da, a, dqk, qk)
        m_cols = _each(lambda x: _split_dot(x, ones, TN), m)
        kd_term = _each(lambda x, y: rsum(x * y), dkd, kd)
        dgc = _each(lambda m_, mc, dqd_, qd_, kt, dkbg_, kbg_, dgl_, gl_:
                    rsum(m_) - mc + rsum(dqd_ * qd_) - kt + rsum(dkbg_ * kbg_)
                    + jnp.where(last, jnp.sum(kt, axis=0, keepdims=True) + dgl_ * gl_, 0.0),
                    m, m_cols, dqd, qd, kd_term, dkbg, kbg, dgl, gl)
        dg = _each(lambda x: _split_dot(upper, x, NN), dgc)
        for g in range(grp):
            dq_ref[rows[g], :] = dq[g]
            dk_ref[rows[g], :] = dk[g]
            dv_ref[rows[g], :] = dvb[g] * be[g]
            dg_ref[0, rows[g], :] = dg[g]
            db_ref[0, rows[g], :] = jnp.broadcast_to(rsum(dkb[g] * k[g]) + rsum(dvb[g] * v[g]), (CH, DH))

    r = bsz * lp
    return pl.pallas_call(
        body, name="gdn_local_bwd", grid=(bh, ngrp),
        in_specs=[head(0), head(GW), head(2 * GW), lane_b, lane_b, per_chunk(1, CH), per_chunk(CH, CH),
                  per_chunk(DH, DH), per_chunk(DH, DH), head(0), head(0), head(0)],
        out_specs=[head(0), head(0), head(0), lane_b, lane_b],
        out_shape=[jax.ShapeDtypeStruct((r, GW), F32)] * 3 + [jax.ShapeDtypeStruct((bh, lp, DH), F32)] * 2,
        compiler_params=_cparams(("parallel", "parallel")),
    )(qkv, qkv, qkv, gcb, bb, gct, tinv_all, states, dstates, vnew, dvnew, d_o)


def _place():
    return lax.axis_index("x"), lax.axis_index("y"), lax.axis_index("c")


def _any_specs(n):
    return [pl.BlockSpec(memory_space=pl.ANY)] * n


def _all_gather(name, shards):
    n = len(shards)

    def body(*refs):
        ins, outs = refs[:n], refs[n:2 * n]
        send_sems, recv_sems, local_sems = refs[2 * n:]
        x, y, c = _place()
        me, sibling = (x, y, c), (x, y, 1 - c)
        chips = [(1 - x, y), (x, 1 - y), (1 - x, 1 - y)]

        def slot(i, p):
            return outs[i].at[4 * p[0] + 2 * p[1] + p[2]]

        def copy(i, k, block, to, src=None):
            return pltpu.make_async_remote_copy(
                src_ref=slot(i, block) if src is None else src, dst_ref=slot(i, block),
                send_sem=send_sems.at[i, k], recv_sem=recv_sems.at[i, k], device_id=to, device_id_type=MESH)

        mine = [pltpu.make_async_copy(ins[i], slot(i, me), local_sems.at[i]) for i in range(n)]
        for cp in mine:
            cp.start()
        first = []
        for i in range(n):
            first.append(copy(i, 0, me, sibling, src=ins[i]))
            first += [copy(i, 1 + j, me, (*chip, c), src=ins[i]) for j, chip in enumerate(chips)]
        for cp in first:
            cp.start()
        passed = []
        for j, chip in enumerate(chips):
            for i in range(n):
                copy(i, 1 + j, (*chip, c), me).wait_recv()
                fwd = copy(i, 4 + j, (*chip, c), sibling)
                fwd.start()
                passed.append(fwd)
        for i in range(n):
            copy(i, 0, sibling, me).wait_recv()
            for j, chip in enumerate(chips):
                copy(i, 4 + j, (*chip, 1 - c), me).wait_recv()
        for cp in first + passed:
            cp.wait_send()
        for cp in mine:
            cp.wait()

    return pl.pallas_call(
        body, name=name,
        in_specs=_any_specs(n), out_specs=_any_specs(n),
        out_shape=[jax.ShapeDtypeStruct((N_DEV,) + s.shape, s.dtype) for s in shards],
        scratch_shapes=[pltpu.SemaphoreType.DMA((n, 7)), pltpu.SemaphoreType.DMA((n, 7)),
                        pltpu.SemaphoreType.DMA((n,))],
    )(*shards)


HBM_SPEC = pl.BlockSpec(memory_space=pltpu.HBM)
SEM_SPEC = pl.BlockSpec(memory_space=pltpu.SEMAPHORE)
EFFECT = pltpu.SideEffectType.DATAFLOW_SIDE_EFFECTING
N_PEER = N_DEV - 1


def _peer(r):
    x, y, c = _place()
    return ((1 - x) if r & 4 else x, (1 - y) if r & 2 else y, (1 - c) if r & 1 else c)


def _slot_of(p):
    return 4 * p[0] + 2 * p[1] + p[2]


def _hbm(a):
    return pltpu.with_memory_space_constraint(a, pltpu.HBM)


def _gather_pattern(src, land):
    me = _place()
    return [(src, land.at[_slot_of(me)], _peer(r), land.at[_slot_of(_peer(r))]) for r in range(1, N_DEV)]


def _sibling_pattern(src, land):
    x, y, c = _place()
    return [(src.at[ch, 1 - c], land.at[ch], (x, y, 1 - c), land.at[ch]) for ch in range(4)]


def _chip_pattern(src, land):
    x, y, c = _place()
    chips = [(1 - x, y), (x, 1 - y), (1 - x, 1 - y)]
    return [(src.at[2 * ch[0] + ch[1]], land.at[j], (*ch, c), land.at[j]) for j, ch in enumerate(chips)]


def _split_start(name, groups, pattern, land_shape, after=None):
    sizes = [len(g) for g in groups]
    arrays = [s for g in groups for s in g]
    n = len(arrays)
    land_shapes = [land_shape(s.shape) for s in arrays]
    order = [] if after is None else [after]

    def body(*refs):
        srcs, lands = refs[:n], refs[n:2 * n]
        first_out = 2 * n + len(order)
        sems = refs[first_out:first_out + 2 * len(groups)]
        token = refs[-1]
        k = 0
        for gi, size in enumerate(sizes):
            for i in range(size):
                copies = pattern(srcs[k], lands[k])
                for j, (s, d, peer, _) in enumerate(copies):
                    pltpu.make_async_remote_copy(
                        src_ref=s, dst_ref=d, send_sem=sems[2 * gi].at[i * len(copies) + j],
                        recv_sem=sems[2 * gi + 1].at[i * len(copies) + j], device_id=peer, device_id_type=MESH).start()
                k += 1
        token[...] = jnp.zeros_like(token)

    per_array = {_gather_pattern: N_PEER, _sibling_pattern: 4, _chip_pattern: 3}[pattern]
    sem_shapes = [pltpu.SemaphoreType.DMA((size * per_array,)) for size in sizes for _ in range(2)]
    res = pl.pallas_call(
        body, name=name,
        in_specs=[HBM_SPEC] * (2 * n) + [pl.BlockSpec(memory_space=pl.ANY)] * len(order),
        out_specs=[SEM_SPEC] * len(sem_shapes) + [HBM_SPEC] * (2 * n) + [pl.BlockSpec(memory_space=pltpu.VMEM)],
        out_shape=sem_shapes + [pltpu.HBM(s.shape, s.dtype) for s in arrays]
        + [pltpu.HBM(ls, s.dtype) for ls, s in zip(land_shapes, arrays)] + [jax.ShapeDtypeStruct((8, DH), F32)],
        input_output_aliases={i: len(sem_shapes) + i for i in range(2 * n)},
        compiler_params=pltpu.CompilerParams(has_side_effects=EFFECT),
    )(*[_hbm(s) for s in arrays], *[_hbm(lax.empty(ls, s.dtype)) for ls, s in zip(land_shapes, arrays)], *order)
    nsem = len(sem_shapes)
    out, k = [], 0
    for gi, size in enumerate(sizes):
        out.append((res[2 * gi], res[2 * gi + 1], list(res[nsem + k:nsem + k + size]),
                    list(res[nsem + n + k:nsem + n + k + size])))
        k += size
    return out, res[-1]


def _split_wait(name, group, pattern, after):
    send_sems, recv_sems, arrays, lands = group
    n = len(arrays)
    fill_own = pattern is _gather_pattern

    def body(*refs):
        srcs, lands_ = refs[:n], refs[n:2 * n]
        s_sems, r_sems = refs[2 * n], refs[2 * n + 1]
        own = []
        if fill_own:
            local_sems = refs[-1]
            own = [pltpu.make_async_copy(srcs[i], lands_[i].at[_slot_of(_place())], local_sems.at[i])
                   for i in range(n)]
            for cp in own:
                cp.start()
        for i in range(n):
            copies = pattern(srcs[i], lands_[i])
            for j, (s, _, peer, filled) in enumerate(copies):
                cp = pltpu.make_async_remote_copy(
                    src_ref=s, dst_ref=filled, send_sem=s_sems.at[i * len(copies) + j],
                    recv_sem=r_sems.at[i * len(copies) + j], device_id=peer, device_id_type=MESH)
                cp.wait_send()
                cp.wait_recv()
        for cp in own:
            cp.wait()

    res = pl.pallas_call(
        body, name=name,
        in_specs=[HBM_SPEC] * (2 * n) + [SEM_SPEC, SEM_SPEC, pl.BlockSpec(memory_space=pl.ANY)],
        out_specs=[HBM_SPEC] * (2 * n),
        out_shape=[pltpu.HBM(s.shape, s.dtype) for s in arrays] + [pltpu.HBM(l.shape, l.dtype) for l in lands],
        scratch_shapes=[pltpu.SemaphoreType.DMA((n,))] if fill_own else [],
        input_output_aliases={i: i for i in range(2 * n)},
        compiler_params=pltpu.CompilerParams(has_side_effects=EFFECT),
    )(*arrays, *lands, send_sems, recv_sems, after)
    return list(res[:n]), list(res[n:])


def _pair_sum(name, grads, recv, core):
    _, _, r, c = grads.shape
    tr = _div_tile(r, 256, 16) if r % 16 == 0 else r

    def body(core_ref, g_ref, r_ref, o_ref):
        o_ref[...] = (g_ref[0].astype(F32) + r_ref[...].astype(F32)).astype(o_ref.dtype)

    return pl.pallas_call(
        body, name=name,
        grid_spec=pltpu.PrefetchScalarGridSpec(
            num_scalar_prefetch=1, grid=(4, r // tr),
            in_specs=[pl.BlockSpec((1, 1, tr, c), lambda s, i, cr: (s, cr[0], i, 0)),
                      pl.BlockSpec((1, tr, c), lambda s, i, cr: (s, i, 0))],
            out_specs=pl.BlockSpec((1, tr, c), lambda s, i, cr: (s, i, 0))),
        out_shape=jax.ShapeDtypeStruct(recv.shape, recv.dtype),
        compiler_params=_cparams(("parallel", "parallel")),
    )(core, grads, recv)


def _adamw_math(w, g, m, v):
    m = ADAM_B1 * m + (1.0 - ADAM_B1) * g
    v = ADAM_B2 * v + (1.0 - ADAM_B2) * (g * g)
    m_hat = m / (1.0 - ADAM_B1 ** ADAM_STEP)
    v_hat = v / (1.0 - ADAM_B2 ** ADAM_STEP)
    delta = -ADAM_LR * (m_hat / (jnp.sqrt(v_hat) + ADAM_EPS) + ADAM_WD * w)
    return delta, m, v


def _adamw(name, parts, w, m, v):
    nslot, r, c = parts.shape
    tr = _div_tile(r, 256, 8) if r % 8 == 0 else r

    def body(p_ref, w_ref, m_ref, v_ref, g_out, d_out, m_out, v_out):
        g = p_ref[0]
        for s in range(1, nslot):
            g = g + p_ref[s]
        d, mn, vn = _adamw_math(w_ref[...], g, m_ref[...], v_ref[...])
        g_out[...] = g
        d_out[...] = d
        m_out[...] = mn
        v_out[...] = vn

    blk = pl.BlockSpec((tr, c), lambda i: (i, 0))
    return pl.pallas_call(
        body, name=name, grid=(r // tr,),
        in_specs=[pl.BlockSpec((nslot, tr, c), lambda i: (0, i, 0)), blk, blk, blk],
        out_specs=[blk] * 4,
        out_shape=[jax.ShapeDtypeStruct((r, c), F32)] * 4,
        compiler_params=_cparams(("parallel",)),
    )(parts, w, m, v)


def _grad_sum(name, parts, chip, recv):
    _, r, c = parts.shape
    tr = _div_tile(r, 256, 16) if r % 16 == 0 else r

    def body(chip_ref, p_ref, r_ref, g_out):
        g = p_ref[0].astype(F32)
        for s in range(3):
            g = g + r_ref[s].astype(F32)
        g_out[...] = g

    return pl.pallas_call(
        body, name=name,
        grid_spec=pltpu.PrefetchScalarGridSpec(
            num_scalar_prefetch=1, grid=(r // tr,),
            in_specs=[pl.BlockSpec((1, tr, c), lambda i, ch: (ch[0], i, 0)),
                      pl.BlockSpec((3, tr, c), lambda i, ch: (0, i, 0))],
            out_specs=pl.BlockSpec((tr, c), lambda i, ch: (i, 0))),
        out_shape=jax.ShapeDtypeStruct((r, c), F32),
        compiler_params=_cparams(("parallel",)),
    )(chip, parts, recv)


def _adamw_reduced(name, parts, chip, recv, w, m, v):
    _, r, c = parts.shape
    tr = _div_tile(r, 256, 16) if r % 16 == 0 else r

    def body(chip_ref, p_ref, r_ref, w_ref, m_ref, v_ref, g_out, d_out, m_out, v_out):
        g = p_ref[0].astype(F32)
        for s in range(3):
            g = g + r_ref[s].astype(F32)
        d, mn, vn = _adamw_math(w_ref[...], g, m_ref[...], v_ref[...])
        g_out[...] = g
        d_out[...] = d
        m_out[...] = mn
        v_out[...] = vn

    blk = pl.BlockSpec((tr, c), lambda i, ch: (i, 0))
    return pl.pallas_call(
        body, name=name,
        grid_spec=pltpu.PrefetchScalarGridSpec(
            num_scalar_prefetch=1, grid=(r // tr,),
            in_specs=[pl.BlockSpec((1, tr, c), lambda i, ch: (ch[0], i, 0)),
                      pl.BlockSpec((3, tr, c), lambda i, ch: (0, i, 0)), blk, blk, blk],
            out_specs=[blk] * 4),
        out_shape=[jax.ShapeDtypeStruct((r, c), F32)] * 4,
        compiler_params=_cparams(("parallel",)),
    )(chip, parts, recv, w, m, v)


SMALL_ROWS, SMALL_COLS = 32, 3 * GW


def _pack_small(gains4, meta, conv_qkv, conv_sc, a_log, dt_bias, gdn_norm, extra=None):
    def padc(a):
        return jnp.pad(a, ((0, 0), (0, SMALL_COLS - a.shape[1])))
    scal = jnp.concatenate([jnp.pad(a_log, ((0, 0), (0, DH - HEADS))), jnp.pad(dt_bias, ((0, 0), (0, DH - HEADS))),
                            gdn_norm] + ([] if extra is None else [extra]), axis=1)
    rows = [padc(g) for g in gains4] + [padc(meta), padc(conv_qkv), padc(conv_sc), padc(scal)]
    slab = jnp.concatenate(rows, axis=0)
    return jnp.pad(slab, ((0, SMALL_ROWS - slab.shape[0]), (0, 0)))


def _unpack_small(slab):
    gains = [slab[i:i + 1, :D] for i in range(4)]
    meta = slab[4:20, :D]
    conv_qkv = slab[20:24, :]
    conv_sc = slab[24:27, :SW]
    a_log = slab[27:28, 0:HEADS]
    dt_bias = slab[27:28, DH:DH + HEADS]
    gdn_norm = slab[27:28, 2 * DH:3 * DH]
    return gains, meta, conv_qkv, conv_sc, a_log, dt_bias, gdn_norm


def kernel(x, meta_tokens, mix_pre_norm, mix_post_norm, ffn_pre_norm, ffn_post_norm, w_in, conv_qkv, a_log, dt_bias, gdn_norm, conv_sc, w_out, w_gate, w_up, w_down, loss_target, m_meta_tokens, m_mix_pre_norm, m_mix_post_norm, m_ffn_pre_norm, m_ffn_post_norm, m_w_in, m_conv_qkv, m_a_log, m_dt_bias, m_gdn_norm, m_conv_sc, m_w_out, m_w_gate, m_w_up, m_w_down, v_meta_tokens, v_mix_pre_norm, v_mix_post_norm, v_ffn_pre_norm, v_ffn_post_norm, v_w_in, v_conv_qkv, v_a_log, v_dt_bias, v_gdn_norm, v_conv_sc, v_w_out, v_w_gate, v_w_up, v_w_down):
    bsz, seq, _ = x.shape
    lp = -(-(N_META + seq) // CH) * CH
    padf = lp - N_META - seq
    first_real = padf + N_META
    rows = bsz * lp
    nchunk = lp // CH
    tr = _div_tile(lp, 528)
    tm = _div_tile(rows, 1056)
    tm_big = _div_tile(rows, 2112)
    cx, cy, cc = _place()
    dev = 4 * cx + 2 * cy + cc

    small_shard = jnp.concatenate([
        jnp.pad(meta_tokens, ((0, 0), (0, 256 - DH))),
        jnp.pad(conv_qkv[0], ((0, 0), (0, 256 - 3 * GW // N_DEV))),
        jnp.pad(conv_sc[0], ((0, 5), (0, 256 - SW // N_DEV)))], axis=0)
    g_in, g_small = _all_gather("gather_w_in", [w_in[0].T.astype(BF16), small_shard])
    my_out = [w_out[0].astype(BF16)]
    my_ffn = [w_gate[0].T.astype(BF16), w_up[0].T.astype(BF16), w_down[0].astype(BF16)]
    (grp_out, grp_ffn), token = _split_start("gather_rest_start", [my_out, my_ffn], _gather_pattern,
                                             lambda s: (N_DEV,) + s, after=g_in)
    after_start = token[0:1, 0:1]

    def cols_full(g):
        return jnp.transpose(g, (1, 0, 2)).reshape(g.shape[1], -1)

    win_t = g_in.reshape(IN_W, D)
    win_t = jnp.concatenate([win_t[:BA0 + 2 * HEADS], jnp.zeros((BA_W - 2 * HEADS, D), BF16),
                             win_t[BA0 + 2 * HEADS:]], axis=0)
    meta_full = cols_full(g_small[:, 0:16, :DH])
    convq_full = cols_full(g_small[:, 16:20, :3 * GW // N_DEV])
    convs_full = cols_full(g_small[:, 20:23, :SW // N_DEV])
    alog_v = jnp.pad(a_log, ((0, 0), (HEADS, DH - 2 * HEADS)))
    dtb_v = jnp.pad(dt_bias, ((0, 0), (HEADS, DH - 2 * HEADS)))

    h0 = jnp.concatenate([jnp.zeros((bsz, padf, D), F32), jnp.broadcast_to(meta_full[None], (bsz, N_META, D)), x],
                         axis=1).reshape(rows, D)
    tgt = jnp.pad(loss_target, ((0, 0), (first_real, 0), (0, 0))).reshape(rows, D)
    u1 = _pre_norm(h0, mix_pre_norm + after_start, bsz, lp, tr)
    (proj,) = _mm("proj_in", [(u1, win_t)], "nt", tm_big, 768, D, out_dtypes=(BF16,))
    (ba,) = _mm("proj_gate_logits", [(u1, win_t[BA0:SX0])], "nt", tm, BA_W, D)
    qkv = _gdn_prep(proj, convq_full, bsz, lp, padf)
    gb, gcs = _gates(ba, alog_v, dtb_v, bsz, lp, padf)

    def lane_bcast(cols):
        t = jnp.transpose(cols.reshape(bsz, lp, HEADS), (0, 2, 1)).reshape(bsz * HEADS, lp, 1)
        return jnp.broadcast_to(t, (bsz * HEADS, lp, DH))

    bb = lane_bcast(gb[:, 0:HEADS])
    gcb = lane_bcast(gcs[:, HEADS:2 * HEADS])
    gct = jnp.transpose(gcs[:, HEADS:2 * HEADS].reshape(bsz, nchunk, CH, HEADS), (0, 3, 1, 2)).reshape(
        bsz * HEADS, nchunk, 1, CH)
    u_loc, l1, l2, l3, w_t, k_dec, tinv_all, g_last = _gdn_local_fwd(qkv, gcb, bb, gct, bsz, lp)
    o, v_new, states = _gdn_scan_fwd(u_loc, l1, l2, g_last, bsz, lp)
    o, v_new = o.reshape(rows, GW), v_new.reshape(rows, GW)
    mix_g, mix_s = _mix_heads(proj, o, gdn_norm, convs_full, bsz, lp, padf)
    mixin = jnp.concatenate([mix_g, mix_s], axis=1)
    _, (l_out,) = _split_wait("gather_w_out_wait", grp_out, _gather_pattern, mixin)
    wout = l_out.reshape(D, D)
    (mix,) = _mm("proj_out", [(mixin, wout)], "nn", tm, 512, D)
    _, (l_gate, l_up, l_down) = _split_wait("gather_ffn_wait", grp_ffn, _gather_pattern, mix)
    wgate_t, wup_t = l_gate.reshape(FF, D), l_up.reshape(FF, D)
    wdown = l_down.reshape(FF, D)
    h1, u2 = _mid_norms(h0, mix, mix_post_norm, ffn_pre_norm, bsz, lp, tr)
    gate, up, act = _ffn_up(u2, wgate_t, wup_t, _div_tile(rows, 528), 1408)
    (ffn,) = _mm("ffn_down", [(act, wdown)], "nn", tm, 512, 1408)

    dy, dffn, d_g4, loss_part = _loss_head(h1, ffn, tgt, ffn_post_norm, bsz, lp, tr, first_real)
    (dwdown,) = _mm("dw_down", [(act, dffn)], "tn", 1408, 512, tm, out_dtypes=(BF16,))

    def swiglu_bwd(da, g, u):
        g, u = g.astype(F32), u.astype(F32)
        return da * u * _dsilu(g), da * _silu(g)

    dgate, dup = _mm("d_act", [(dffn, wdown)], "nt", tm, 1408, D, out_dtypes=(BF16, BF16),
                     epilogue=swiglu_bwd, extras=(gate, up))
    (dwgate_t,) = _mm("dw_gate", [(dgate, u2)], "tn", 1408, D, tm, out_dtypes=(BF16,))
    (dwup_t,) = _mm("dw_up", [(dup, u2)], "tn", 1408, D, tm, out_dtypes=(BF16,))
    (du2,) = _mm("d_u2", [(dgate, wgate_t), (dup, wup_t)], "nn", tm, D, 1408)
    dh1, dmix, d_g3, d_g2 = _mid_norms_bwd(h1, mix, du2, dy, ffn_pre_norm, mix_post_norm, bsz, lp, tr)
    (dwout,) = _mm("dw_out", [(mixin, dmix)], "tn", D, 512, tm, out_dtypes=(BF16,))

    def row_blocks(g):
        return g.reshape(4, 2, -1, g.shape[1])

    core = jnp.reshape(cc, (1,)).astype(jnp.int32)
    chip = jnp.reshape(2 * cx + cy, (1,)).astype(jnp.int32)

    def reduce_start(tag, grads):
        (grp,), tok = _split_start("rs_sibling_start_" + tag, [grads], _sibling_pattern, lambda s: s[:1] + s[2:])
        return grp, tok

    def reduce_mid(tag, grp, nms, after):
        mine, lands = _split_wait("rs_sibling_wait_" + tag, grp, _sibling_pattern, after)
        parts = [_pair_sum("pair_sum_" + nm, g, r, core) for nm, g, r in zip(nms, mine, lands)]
        (grp2,), tok = _split_start("rs_chip_start_" + tag, [parts], _chip_pattern, lambda s: (3,) + s[1:])
        return grp2, tok

    def reduce_end(tag, grp2, after):
        return _split_wait("rs_chip_wait_" + tag, grp2, _chip_pattern, after)

    names_ffn = ["w_down", "w_gate", "w_up", "w_out"]
    rs_ffn, tok = reduce_start("ffn", [row_blocks(dwdown), row_blocks(dwgate_t), row_blocks(dwup_t), row_blocks(dwout)])
    (dmixin,) = _mm("d_mixin", [(dmix, wout)], "nt", tm, 512, D, after=tok)
    d_o, dz, dsx, dsb, dsc, d_gn, d_convs = _mix_heads_bwd(proj, o, gdn_norm, convs_full, dmixin, bsz, lp, padf)
    rs_ffn, tok = reduce_mid("ffn", rs_ffn, names_ffn, d_o)
    dv_new, dstates = _gdn_scan_bwd(d_o, l3, w_t, k_dec, g_last + tok[0, 0], bsz, lp)
    dq, dk, dv, dg_b, dbeta_b = _gdn_local_bwd(qkv, gcb, bb, gct, tinv_all, states, dstates, v_new,
                                               dv_new.reshape(rows, GW), d_o, bsz, lp)
    dpqkv, d_convq = _gdn_prep_bwd(proj, convq_full, dq, dk, dv, bsz, lp, padf)
    dba, d_scal = _gates_bwd(ba, alog_v, dtb_v, dbeta_b, dg_b, gb, bsz, lp, padf)
    dproj = jnp.concatenate([dpqkv, dz, dba, dsx, dsb, dsc], axis=1)
    (dwin_t,) = _mm("dw_in", [(dproj, u1)], "tn", 1280, D, tm, out_dtypes=(BF16,))
    parts_ffn, recv_ffn = reduce_end("ffn", rs_ffn, dwin_t)
    dwin_t = jnp.concatenate([dwin_t[:BA0 + 2 * HEADS], dwin_t[SX0:]], axis=0)
    rs_in, tok = reduce_start("in", [row_blocks(dwin_t)])
    (du1,) = _mm("d_u1", [(dproj, win_t)], "nn", tm_big, D, 768, after=tok)
    rs_in, tok = reduce_mid("in", rs_in, ["w_in"], du1)
    dh0, d_g1, d_meta = _pre_norm_bwd(h0, du1, dh1, mix_pre_norm + tok[0:1, 0:1], bsz, lp, tr, padf)
    grad_x = dh0.reshape(bsz, lp, D)[:, first_real:]

    d_alog = d_scal[0:1, HEADS:2 * HEADS]
    d_dtb = d_scal[1:2, HEADS:2 * HEADS]
    slab = _pack_small([d_g1, d_g2, d_g3, d_g4], d_meta, d_convq, d_convs, d_alog, d_dtb, d_gn, loss_part)
    (grp_small,), _ = _split_start("small_grads_start", [[slab]], _gather_pattern, lambda s: (N_DEV,) + s)

    big_w = {"w_in": (w_in, m_w_in, v_w_in), "w_out": (w_out, m_w_out, v_w_out), "w_gate": (w_gate, m_w_gate, v_w_gate),
             "w_up": (w_up, m_w_up, v_w_up), "w_down": (w_down, m_w_down, v_w_down)}

    def update(nm, parts, recv):
        w, m, v = big_w[nm]
        if nm in ("w_out", "w_down"):
            return _adamw_reduced("adamw_" + nm, parts, chip, recv, w[0], m[0], v[0])
        g = _grad_sum("grad_sum_" + nm, parts, chip, recv).T
        return _adamw("adamw_" + nm, g[None], w[0], m[0], v[0])

    big_out = {nm: update(nm, p, r) for nm, p, r in zip(names_ffn, parts_ffn, recv_ffn)}
    _, (slabs,) = _split_wait("small_grads_wait", grp_small, _gather_pattern, big_out["w_out"][1])
    off_meta = dev * DH
    off_cq = dev * (3 * GW // N_DEV)
    off_cs = dev * (SW // N_DEV)

    def shard_of(s):
        gains, meta, cq, cs, al, db, gn = _unpack_small(s)
        return gains, lax.dynamic_slice_in_dim(meta, off_meta, DH, 1), \
            lax.dynamic_slice_in_dim(cq, off_cq, 3 * GW // N_DEV, 1), \
            lax.dynamic_slice_in_dim(cs, off_cs, SW // N_DEV, 1), al, db, gn, s[27:28, 3 * DH:4 * DH]

    slabs_local = jax.vmap(lambda s: _pack_small(*shard_of(s)))(slabs)
    w_small = _pack_small([mix_pre_norm, mix_post_norm, ffn_pre_norm, ffn_post_norm], meta_tokens, conv_qkv[0],
                          conv_sc[0], a_log, dt_bias, gdn_norm)
    m_small = _pack_small([m_mix_pre_norm, m_mix_post_norm, m_ffn_pre_norm, m_ffn_post_norm], m_meta_tokens,
                          m_conv_qkv[0], m_conv_sc[0], m_a_log, m_dt_bias, m_gdn_norm)
    v_small = _pack_small([v_mix_pre_norm, v_mix_post_norm, v_ffn_pre_norm, v_ffn_post_norm], v_meta_tokens,
                          v_conv_qkv[0], v_conv_sc[0], v_a_log, v_dt_bias, v_gdn_norm)
    small_out = _adamw("adamw_small", slabs_local, w_small, m_small, v_small)
    loss = small_out[0][27, 3 * DH]
    parts_in, recv_in = reduce_end("in", rs_in, small_out[1])
    big_out["w_in"] = update("w_in", parts_in[0], recv_in[0])
    names = ["w_in", "w_out", "w_gate", "w_up", "w_down"]

    def small_leaves(slab_out):
        gains, meta, cq, cs, al, db, gn = _unpack_small(slab_out)
        return {"meta_tokens": meta[:, :DH], "mix_pre_norm": gains[0], "mix_post_norm": gains[1],
                "ffn_pre_norm": gains[2], "ffn_post_norm": gains[3], "conv_qkv": cq[None, :, :3 * GW // N_DEV],
                "a_log": al, "dt_bias": db, "gdn_norm": gn, "conv_sc": cs[None, :, :SW // N_DEV]}

    order = ["meta_tokens", "mix_pre_norm", "mix_post_norm", "ffn_pre_norm", "ffn_post_norm", "w_in", "conv_qkv",
             "a_log", "dt_bias", "gdn_norm", "conv_sc", "w_out", "w_gate", "w_up", "w_down"]
    outs = [loss, grad_x]
    for kind in range(4):
        leaves = small_leaves(small_out[kind])
        for nm in names:
            leaves[nm] = big_out[nm][kind][None]
        outs += [leaves[nm] for nm in order]
    return tuple(outs)
```

```python
import functools

import jax
import jax.numpy as jnp
from jax import lax
from jax.experimental import pallas as pl
from jax.experimental.pallas import tpu as pltpu

F32 = jnp.float32
BF16 = jnp.bfloat16
HIGHEST = lax.Precision.HIGHEST
MESH = pl.DeviceIdType.MESH

D = 1024
N_META = 16
HEADS = 4
DH = 128
GW = HEADS * DH
SW = D - GW
FF = 2816
CH = 64
KQ = 4
KS = 3
EPS = 1e-6
IN_W = 3 * GW + GW + 2 * HEADS + 3 * SW
Q0, K0, V0, Z0, BA0, SX0, SB0, SC0, PW = 0, 512, 1024, 1536, 2048, 2304, 2816, 3328, 3840
BA_W = SX0 - BA0
N_DEV = 8
VMEM_LIMIT = 56 * 1024 * 1024

ADAM_LR, ADAM_B1, ADAM_B2, ADAM_EPS, ADAM_WD, ADAM_STEP = 0.001, 0.9, 0.999, 1e-08, 0.01, 10


def _cparams(sem):
    return pltpu.CompilerParams(dimension_semantics=sem, vmem_limit_bytes=VMEM_LIMIT)


def _div_tile(n, target, mult=16):
    best = None
    for t in range(mult, min(n, target) + 1, mult):
        if n % t == 0:
            best = t
    assert best is not None, (n, target)
    return best


def _silu(x):
    return x * jax.nn.sigmoid(x)


def _dsilu(x):
    s = jax.nn.sigmoid(x)
    return s * (1.0 + x * (1.0 - s))


def _bdot(a, b, dims):
    return lax.dot_general(a.astype(BF16), b.astype(BF16), (dims, ((), ())), preferred_element_type=F32)


NN = ((1,), (0,))
NT = ((1,), (1,))
TN = ((0,), (0,))


def _fdot(a, b, dims=NN):
    return lax.dot_general(a, b, (dims, ((), ())), precision=HIGHEST, preferred_element_type=F32)


def _mm(name, pairs, mode, tm, tn, tk, out_dtypes=(F32,), epilogue=None, extras=(), after=None):
    a0, b0 = pairs[0]
    if mode == "nn":
        (m, k), n = a0.shape, b0.shape[1]
    elif mode == "nt":
        (m, k), n = a0.shape, b0.shape[0]
    else:
        (k, m), n = a0.shape, b0.shape[1]
    assert m % tm == 0 and n % tn == 0 and k % tk == 0, (name, m, n, k, tm, tn, tk)
    nk = k // tk
    npair = len(pairs)
    nex = len(extras)
    nout = len(out_dtypes)
    ntok = 0 if after is None else 1
    dims = {"nn": NN, "nt": NT, "tn": TN}[mode]

    def body(*refs):
        ab = refs[:2 * npair]
        ex = refs[2 * npair:2 * npair + nex]
        outs = refs[2 * npair + nex + ntok:2 * npair + nex + ntok + nout]
        acc_ref = refs[-1]
        kk = pl.program_id(2)

        part = None
        for p in range(npair):
            d = lax.dot_general(ab[2 * p][...], ab[2 * p + 1][...], (dims, ((), ())), preferred_element_type=F32)
            part = d if part is None else part + d

        def finish(acc):
            res = epilogue(acc, *[e[...] for e in ex]) if epilogue is not None else (acc,)
            for o, r in zip(outs, res):
                o[...] = r.astype(o.dtype)

        if nk == 1:
            finish(part)
        else:
            @pl.when(kk == 0)
            def _():
                acc_ref[...] = part

            @pl.when(kk > 0)
            def _():
                acc_ref[...] += part

            @pl.when(kk == nk - 1)
            def _():
                finish(acc_ref[...])

    if mode == "nn":
        a_spec = pl.BlockSpec((tm, tk), lambda i, j, q: (i, q))
        b_spec = pl.BlockSpec((tk, tn), lambda i, j, q: (q, j))
    elif mode == "nt":
        a_spec = pl.BlockSpec((tm, tk), lambda i, j, q: (i, q))
        b_spec = pl.BlockSpec((tn, tk), lambda i, j, q: (j, q))
    else:
        a_spec = pl.BlockSpec((tk, tm), lambda i, j, q: (q, i))
        b_spec = pl.BlockSpec((tk, tn), lambda i, j, q: (q, j))
    o_spec = pl.BlockSpec((tm, tn), lambda i, j, q: (i, j))
    flat = [t for pr in pairs for t in pr]
    res = pl.pallas_call(
        body, name=name,
        grid=(m // tm, n // tn, nk),
        in_specs=[a_spec, b_spec] * npair + [o_spec] * nex + [pl.BlockSpec((8, DH), lambda i, j, q: (0, 0))] * ntok,
        out_specs=[o_spec] * nout,
        out_shape=[jax.ShapeDtypeStruct((m, n), dt) for dt in out_dtypes],
        scratch_shapes=[pltpu.VMEM((tm, tn) if nk > 1 else (8, DH), F32)],
        compiler_params=_cparams(("parallel", "parallel", "arbitrary")),
    )(*flat, *extras, *([] if after is None else [after]))
    return res


def _ffn_up(u2, wg, wu, tm, tn):
    m, k = u2.shape
    n = wg.shape[0]

    def body(a_ref, g_ref, u_ref, gate_ref, up_ref, act_ref):
        a = a_ref[...]
        g = lax.dot_general(a, g_ref[...], (NT, ((), ())), preferred_element_type=F32)
        u = lax.dot_general(a, u_ref[...], (NT, ((), ())), preferred_element_type=F32)
        gate_ref[...] = g.astype(BF16)
        up_ref[...] = u.astype(BF16)
        act_ref[...] = (_silu(g) * u).astype(BF16)

    o_spec = pl.BlockSpec((tm, tn), lambda j, i: (i, j))
    w_spec = pl.BlockSpec((tn, k), lambda j, i: (j, 0))
    return pl.pallas_call(
        body, name="ffn_up", grid=(n // tn, m // tm),
        in_specs=[pl.BlockSpec((tm, k), lambda j, i: (i, 0)), w_spec, w_spec],
        out_specs=[o_spec, o_spec, o_spec],
        out_shape=[jax.ShapeDtypeStruct((m, n), BF16)] * 3,
        compiler_params=_cparams(("parallel", "parallel")),
    )(u2, wg, wu)


def _rms(x, gain):
    r = lax.rsqrt(jnp.mean(x * x, axis=-1, keepdims=True) + EPS)
    return x * r * gain, r


def _rms_bwd(x, gain, dy):
    r = lax.rsqrt(jnp.mean(x * x, axis=-1, keepdims=True) + EPS)
    dyw = dy * gain
    dx = r * dyw - x * (r * r * r) * jnp.mean(dyw * x, axis=-1, keepdims=True)
    return dx, dy * x * r


def _row_specs(tr, width, nb):
    return pl.BlockSpec((tr, width), lambda b, j: (b * nb + j, 0))


def _vec_spec(width):
    return pl.BlockSpec((1, width), lambda b, j: (0, 0))


def _first_step(b, j):
    return jnp.logical_and(b == 0, j == 0)


def _pre_norm(x, head_rows, gain, bsz, lp, tr):
    nb = lp // tr
    first_real = head_rows.shape[0]
    assert first_real % 8 == 0 and first_real < tr
    x_spec = pl.BlockSpec((pl.Squeezed(), pl.Element(tr), pl.Element(D)),
                          lambda b, j: (b, pl.multiple_of(jnp.maximum(j * tr - first_real, 0), 8), 0))

    def body(x_ref, hd_ref, g_ref, h_ref, u_ref):
        j = pl.program_id(1)
        xt = x_ref[...]
        h = jnp.where(j == 0, jnp.concatenate([hd_ref[...], xt[:tr - first_real]], axis=0), xt)
        h_ref[...] = h
        u_ref[...] = _rms(h, g_ref[...])[0].astype(BF16)

    rs = _row_specs(tr, D, nb)
    return pl.pallas_call(
        body, name="mix_pre_norm", grid=(bsz, nb),
        in_specs=[x_spec, pl.BlockSpec((first_real, D), lambda b, j: (0, 0)), _vec_spec(D)],
        out_specs=[rs, rs],
        out_shape=[jax.ShapeDtypeStruct((bsz * lp, D), F32), jax.ShapeDtypeStruct((bsz * lp, D), BF16)],
        compiler_params=_cparams(("parallel", "parallel")),
    )(x, head_rows, gain)


def _mid_norms(h0, mix, g_post, g_pre, bsz, lp, tr):
    nb = lp // tr

    def body(h_ref, m_ref, gp_ref, gq_ref, h1_ref, u2_ref):
        h1 = h_ref[...] + _rms(m_ref[...], gp_ref[...])[0]
        h1_ref[...] = h1
        u2_ref[...] = _rms(h1, gq_ref[...])[0].astype(BF16)

    rs = _row_specs(tr, D, nb)
    return pl.pallas_call(
        body, name="mid_norms", grid=(bsz, nb),
        in_specs=[rs, rs, _vec_spec(D), _vec_spec(D)],
        out_specs=[rs, rs],
        out_shape=[jax.ShapeDtypeStruct(h0.shape, F32), jax.ShapeDtypeStruct(h0.shape, BF16)],
        compiler_params=_cparams(("parallel", "parallel")),
    )(h0, mix, g_post, g_pre)


def _loss_head(h1, ffn, tgt, g_post, bsz, lp, tr, first_real):
    nb = lp // tr
    assert first_real % 8 == 0 and first_real < tr
    tgt_spec = pl.BlockSpec((pl.Squeezed(), pl.Element(tr), pl.Element(D)),
                            lambda b, j: (b, pl.multiple_of(jnp.maximum(j * tr - first_real, 0), 8), 0))

    def body(h_ref, f_ref, t_ref, g_ref, dy_ref, dffn_ref, dg_ref, loss_ref):
        b, j = pl.program_id(0), pl.program_id(1)
        f = f_ref[...]
        g = g_ref[...]
        y = h_ref[...] + _rms(f, g)[0]
        t = j * tr + lax.broadcasted_iota(jnp.int32, (tr, 1), 0)
        tg = t_ref[...]
        tg = jnp.where(j == 0, jnp.concatenate([tg[tr - first_real:], tg[:tr - first_real]], axis=0), tg)
        err = jnp.where(t >= first_real, y - tg, 0.0)
        dy = err * (1.0 / D)
        dy_ref[...] = dy
        dx, dgc = _rms_bwd(f, g, dy)
        dffn_ref[...] = dx.astype(BF16)
        lsum = jnp.sum(jnp.sum(err * err, axis=0, keepdims=True), axis=1, keepdims=True) * (0.5 / D)

        @pl.when(_first_step(b, j))
        def _():
            dg_ref[...] = jnp.zeros_like(dg_ref)
            loss_ref[...] = jnp.zeros_like(loss_ref)

        dg_ref[...] += jnp.sum(dgc, axis=0, keepdims=True)
        loss_ref[...] += jnp.broadcast_to(lsum, loss_ref.shape)

    rs = _row_specs(tr, D, nb)
    return pl.pallas_call(
        body, name="loss_head", grid=(bsz, nb),
        in_specs=[rs, rs, tgt_spec, _vec_spec(D)],
        out_specs=[rs, rs, _vec_spec(D), _vec_spec(DH)],
        out_shape=[jax.ShapeDtypeStruct(h1.shape, F32), jax.ShapeDtypeStruct(h1.shape, BF16),
                   jax.ShapeDtypeStruct((1, D), F32), jax.ShapeDtypeStruct((1, DH), F32)],
        compiler_params=_cparams(("arbitrary", "arbitrary")),
    )(h1, ffn, tgt, g_post)


def _mid_norms_bwd(h1, mix, du2, dy, g_pre, g_post, bsz, lp, tr):
    nb = lp // tr

    def body(h_ref, m_ref, du_ref, dy_ref, gq_ref, gp_ref, dh1_ref, dmix_ref, dgq_ref, dgp_ref):
        b, j = pl.program_id(0), pl.program_id(1)
        dx, dgq = _rms_bwd(h_ref[...], gq_ref[...], du_ref[...])
        dh1 = dy_ref[...] + dx
        dh1_ref[...] = dh1
        dm, dgp = _rms_bwd(m_ref[...], gp_ref[...], dh1)
        dmix_ref[...] = dm.astype(BF16)

        @pl.when(_first_step(b, j))
        def _():
            dgq_ref[...] = jnp.zeros_like(dgq_ref)
            dgp_ref[...] = jnp.zeros_like(dgp_ref)

        dgq_ref[...] += jnp.sum(dgq, axis=0, keepdims=True)
        dgp_ref[...] += jnp.sum(dgp, axis=0, keepdims=True)

    rs = _row_specs(tr, D, nb)
    return pl.pallas_call(
        body, name="mid_norms_bwd", grid=(bsz, nb),
        in_specs=[rs, rs, rs, rs, _vec_spec(D), _vec_spec(D)],
        out_specs=[rs, rs, _vec_spec(D), _vec_spec(D)],
        out_shape=[jax.ShapeDtypeStruct(h1.shape, F32), jax.ShapeDtypeStruct(h1.shape, BF16),
                   jax.ShapeDtypeStruct((1, D), F32), jax.ShapeDtypeStruct((1, D), F32)],
        compiler_params=_cparams(("arbitrary", "arbitrary")),
    )(h1, mix, du2, dy, g_pre, g_post)


def _pre_norm_bwd(h0, du1, dh1, gain, bsz, lp, tr, padf):
    nb = lp // tr
    assert padf % 8 == 0 and padf + N_META <= tr

    def body(h_ref, du_ref, dh1_ref, g_ref, dh0_ref, dg_ref, dmeta_ref):
        b, j = pl.program_id(0), pl.program_id(1)
        dx, dgc = _rms_bwd(h_ref[...], g_ref[...], du_ref[...])
        dh0 = dh1_ref[...] + dx
        dh0_ref[...] = dh0

        @pl.when(_first_step(b, j))
        def _():
            dg_ref[...] = jnp.zeros_like(dg_ref)
            dmeta_ref[...] = jnp.zeros_like(dmeta_ref)

        dg_ref[...] += jnp.sum(dgc, axis=0, keepdims=True)

        @pl.when(j == 0)
        def _():
            dmeta_ref[...] += dh0[padf:padf + N_META, :]

    rs = _row_specs(tr, D, nb)
    return pl.pallas_call(
        body, name="pre_norm_bwd", grid=(bsz, nb),
        in_specs=[rs, rs, rs, _vec_spec(D)],
        out_specs=[rs, _vec_spec(D), pl.BlockSpec((N_META, D), lambda b, j: (0, 0))],
        out_shape=[jax.ShapeDtypeStruct(h0.shape, F32), jax.ShapeDtypeStruct((1, D), F32),
                   jax.ShapeDtypeStruct((N_META, D), F32)],
        compiler_params=_cparams(("arbitrary", "arbitrary")),
    )(h0, du1, dh1, gain)


def _shift_down(x, s, lp):
    return x if s == 0 else pltpu.roll(x, s, axis=0)


def _shift_up(x, s, lp):
    return x if s == 0 else pltpu.roll(x, lp - s, axis=0)


def _conv_fwd(x, w, taps, lp):
    y = None
    for i in range(taps):
        term = _shift_down(x, taps - 1 - i, lp) * w[i:i + 1, :]
        y = term if y is None else y + term
    return y


def _conv_bwd(x, w, dy, taps, lp):
    dx = None
    dws = []
    for i in range(taps):
        s = taps - 1 - i
        term = _shift_up(dy, s, lp) * w[i:i + 1, :]
        dx = term if dx is None else dx + term
        dws.append(jnp.sum(dy * _shift_down(x, s, lp), axis=0, keepdims=True))
    return dx, jnp.concatenate(dws, axis=0)


def _slab(lp, col0):
    return pl.BlockSpec((lp, DH), lambda b, s: (b, col0 // DH + s))


def _qkv_act(c, s):
    y = _silu(c)
    r = lax.rsqrt(jnp.sum(y * y, axis=-1, keepdims=True) + EPS)
    scale = jnp.where(s < HEADS, DH ** -0.5, 1.0)
    return y, r, scale


def _gdn_prep(proj, conv_qkv, bsz, lp, padf):
    def body(p_ref, w_ref, o_ref):
        s = pl.program_id(1)
        c = _conv_fwd(p_ref[...].astype(F32), w_ref[...], KQ, lp)
        y, r, scale = _qkv_act(c, s)
        out = jnp.where(s < 2 * HEADS, y * (r * scale), y)
        t = lax.broadcasted_iota(jnp.int32, (lp, 1), 0)
        o_ref[...] = jnp.where(t >= padf, out, 0.0)

    return pl.pallas_call(
        body, name="gdn_prep", grid=(bsz, 3 * HEADS),
        in_specs=[_slab(lp, Q0), pl.BlockSpec((KQ, DH), lambda b, s: (0, s))],
        out_specs=_slab(lp, 0),
        out_shape=jax.ShapeDtypeStruct((bsz * lp, 3 * GW), F32),
        compiler_params=_cparams(("parallel", "parallel")),
    )(proj, conv_qkv)


def _gdn_prep_bwd(proj, conv_qkv, dq, dk, dv, bsz, lp, padf):
    def body(p_ref, w_ref, dq_ref, dk_ref, dv_ref, dp_ref, dw_ref):
        b, s = pl.program_id(0), pl.program_id(1)
        x = p_ref[...].astype(F32)
        w = w_ref[...]
        c = _conv_fwd(x, w, KQ, lp)
        y, r, scale = _qkv_act(c, s)
        d = jnp.where(s < HEADS, dq_ref[...], jnp.where(s < 2 * HEADS, dk_ref[...], dv_ref[...]))
        dn = d * scale
        n = y * r
        dy_norm = r * (dn - n * jnp.sum(dn * n, axis=-1, keepdims=True))
        dyy = jnp.where(s < 2 * HEADS, dy_norm, dn)
        t = lax.broadcasted_iota(jnp.int32, (lp, 1), 0)
        dc = jnp.where(t >= padf, dyy * _dsilu(c), 0.0)
        dx, dw = _conv_bwd(x, w, dc, KQ, lp)
        dp_ref[...] = jnp.where(t >= padf, dx, 0.0).astype(BF16)

        @pl.when(jnp.logical_and(b == 0, s == 0))
        def _():
            dw_ref[...] = jnp.zeros_like(dw_ref)

        col_block = jnp.right_shift(lax.broadcasted_iota(jnp.int32, (KQ, 3 * GW), 1), DH.bit_length() - 1)
        dw_ref[...] += jnp.where(col_block == s, jnp.tile(dw, (1, 3 * HEADS)), 0.0)

    sl = lambda col0: pl.BlockSpec((lp, DH), lambda b, s: (b, col0 // DH + s))
    part = lambda k: pl.BlockSpec((lp, DH), lambda b, s: (b, jnp.clip(s - k * HEADS, 0, HEADS - 1)))
    return pl.pallas_call(
        body, name="gdn_prep_bwd", grid=(bsz, 3 * HEADS),
        in_specs=[sl(Q0), pl.BlockSpec((KQ, DH), lambda b, s: (0, s)), part(0), part(1), part(2)],
        out_specs=[sl(0), pl.BlockSpec((KQ, 3 * GW), lambda b, s: (0, 0))],
        out_shape=[jax.ShapeDtypeStruct((bsz * lp, 3 * GW), BF16), jax.ShapeDtypeStruct((KQ, 3 * GW), F32)],
        compiler_params=_cparams(("arbitrary", "arbitrary")),
    )(proj, conv_qkv, dq, dk, dv)


def _softplus(x):
    e = jnp.exp(-jnp.abs(x))
    one_e = 1.0 + e
    l1p = jnp.where(one_e == 1.0, e, jnp.log(one_e) * (e / (one_e - 1.0)))
    return jnp.maximum(x, 0.0) + l1p


def _tri(n, kind):
    i = lax.broadcasted_iota(jnp.int32, (n, n), 0)
    j = lax.broadcasted_iota(jnp.int32, (n, n), 1)
    return {"incl": i >= j, "strict": i > j, "upper": i <= j}[kind]


def _gates(proj, alog_v, dtb_v, bsz, lp, padf):
    nchunk = lp // CH

    def body(p_ref, al_ref, dt_ref, gb_ref, cs_ref):
        x = p_ref[...].astype(F32)
        lane = lax.broadcasted_iota(jnp.int32, (lp, DH), 1)
        t = lax.broadcasted_iota(jnp.int32, (lp, DH), 0)
        beta = jax.nn.sigmoid(x)
        g = -jnp.exp(al_ref[...]) * _softplus(x + dt_ref[...])
        gb = jnp.where(lane < HEADS, beta, jnp.where(lane < 2 * HEADS, g, 0.0))
        gb_ref[...] = jnp.where(t >= padf, gb, 0.0)
        ltri = _tri(CH, "incl").astype(F32)

        def step(n, carry):
            r0 = pl.multiple_of(n * CH, CH)
            cs_ref[pl.ds(r0, CH), :] = _fdot(ltri, gb_ref[pl.ds(r0, CH), :])
            return carry

        lax.fori_loop(0, nchunk, step, 0)

    blk = pl.BlockSpec((lp, DH), lambda b: (b, 0))
    vec = pl.BlockSpec((1, DH), lambda b: (0, 0))
    out = pl.BlockSpec((lp, DH), lambda b: (b, 0))
    return pl.pallas_call(
        body, name="gates", grid=(bsz,),
        in_specs=[blk, vec, vec], out_specs=[out, out],
        out_shape=[jax.ShapeDtypeStruct((bsz * lp, DH), F32)] * 2,
        compiler_params=_cparams(("parallel",)),
    )(proj, alog_v, dtb_v)


def _gates_bwd(proj, alog_v, dtb_v, dbeta_b, dg_b, gb, bsz, lp, padf):
    def body(p_ref, al_ref, dt_ref, db_ref, dg_ref, gb_ref, dl_ref, dv_ref):
        b = pl.program_id(0)
        x = p_ref[...].astype(F32)
        lane = lax.broadcasted_iota(jnp.int32, (lp, DH), 1)
        t = lax.broadcasted_iota(jnp.int32, (lp, DH), 0)
        dbeta = jnp.zeros((lp, DH), F32)
        dg = jnp.zeros((lp, DH), F32)
        for h in range(HEADS):
            dbeta = jnp.where(lane == h, db_ref[h], dbeta)
            dg = jnp.where(lane == HEADS + h, dg_ref[h], dg)
        real = t >= padf
        dbeta = jnp.where(real, dbeta, 0.0)
        dg = jnp.where(real, dg, 0.0)
        sb = jax.nn.sigmoid(x)
        neg_ea = -jnp.exp(al_ref[...])
        dsp = dg * neg_ea * jax.nn.sigmoid(x + dt_ref[...])
        dl = dbeta * sb * (1.0 - sb) + dsp
        dl_ref[...] = jnp.concatenate([dl, jnp.zeros((lp, BA_W - DH), F32)], axis=1).astype(BF16)
        dalog = jnp.sum(dg * gb_ref[...], axis=0, keepdims=True)
        ddt = jnp.sum(dsp, axis=0, keepdims=True)

        @pl.when(b == 0)
        def _():
            dv_ref[...] = jnp.zeros_like(dv_ref)

        dv_ref[...] += jnp.concatenate([dalog, ddt], axis=0)

    blk = pl.BlockSpec((lp, DH), lambda b: (b, 0))
    vec = pl.BlockSpec((1, DH), lambda b: (0, 0))
    hb = pl.BlockSpec((HEADS, lp, DH), lambda b: (b, 0, 0))
    return pl.pallas_call(
        body, name="gates_bwd", grid=(bsz,),
        in_specs=[blk, vec, vec, hb, hb, pl.BlockSpec((lp, DH), lambda b: (b, 0))],
        out_specs=[pl.BlockSpec((lp, BA_W), lambda b: (b, 0)), pl.BlockSpec((2, DH), lambda b: (0, 0))],
        out_shape=[jax.ShapeDtypeStruct((bsz * lp, BA_W), BF16), jax.ShapeDtypeStruct((2, DH), F32)],
        compiler_params=_cparams(("arbitrary",)),
    )(proj, alog_v, dtb_v, dbeta_b, dg_b, gb)


def _mix_heads(proj, o, gdn_norm, conv_sc, bsz, lp, padf):
    def body(o_ref, z_ref, gn_ref, sx_ref, sb_ref, sc_ref, w_ref, og_ref, os_ref):
        og_ref[...] = (_rms(o_ref[...], gn_ref[...])[0] * _silu(z_ref[...].astype(F32))).astype(BF16)
        conv = _conv_fwd(sc_ref[...].astype(F32) * sx_ref[...].astype(F32), w_ref[...], KS, lp)
        t = lax.broadcasted_iota(jnp.int32, (lp, 1), 0)
        os_ref[...] = jnp.where(t >= padf, sb_ref[...].astype(F32) * conv, 0.0).astype(BF16)

    half = _slab(lp, 0)
    return pl.pallas_call(
        body, name="mix_heads", grid=(bsz, HEADS),
        in_specs=[half, _slab(lp, Z0), pl.BlockSpec((1, DH), lambda b, s: (0, 0)),
                  _slab(lp, SX0), _slab(lp, SB0), _slab(lp, SC0), pl.BlockSpec((KS, DH), lambda b, s: (0, s))],
        out_specs=[half, half],
        out_shape=[jax.ShapeDtypeStruct((bsz * lp, GW), BF16)] * 2,
        compiler_params=_cparams(("parallel", "parallel")),
    )(o, proj, gdn_norm, proj, proj, proj, conv_sc)


def _mix_heads_bwd(proj, o, gdn_norm, conv_sc, dmixin, bsz, lp, padf):
    def body(o_ref, z_ref, gn_ref, sx_ref, sb_ref, sc_ref, w_ref, dg_ref, ds_ref,
             do_ref, dz_ref, dsx_ref, dsb_ref, dsc_ref, dgn_ref, dw_ref):
        s, b = pl.program_id(0), pl.program_id(1)
        t = lax.broadcasted_iota(jnp.int32, (lp, 1), 0)
        real = t >= padf
        o, z, gn, d = o_ref[...], z_ref[...].astype(F32), gn_ref[...], dg_ref[...]
        sz = _silu(z)
        on = _rms(o, gn)[0]
        dz_ref[...] = (d * on * _dsilu(z)).astype(BF16)
        dox, dgn = _rms_bwd(o, gn, d * sz)
        do_ref[...] = dox
        sx, sb, sc = sx_ref[...].astype(F32), sb_ref[...].astype(F32), sc_ref[...].astype(F32)
        w, e = w_ref[...], ds_ref[...]
        e = jnp.where(real, e, 0.0)
        xin = sc * sx
        conv = _conv_fwd(xin, w, KS, lp)
        dsb_ref[...] = (e * conv).astype(BF16)
        dxin, dw = _conv_bwd(xin, w, e * sb, KS, lp)
        dxin = jnp.where(real, dxin, 0.0)
        dsx_ref[...] = (dxin * sc).astype(BF16)
        dsc_ref[...] = (dxin * sx).astype(BF16)

        @pl.when(jnp.logical_and(s == 0, b == 0))
        def _():
            dgn_ref[...] = jnp.zeros_like(dgn_ref)

        @pl.when(b == 0)
        def _():
            dw_ref[...] = jnp.zeros_like(dw_ref)

        dgn_ref[...] += jnp.sum(dgn, axis=0, keepdims=True)
        dw_ref[...] += dw

    sl = lambda col0: pl.BlockSpec((lp, DH), lambda s, b: (b, col0 // DH + s))
    half = sl(0)
    return pl.pallas_call(
        body, name="mix_heads_bwd", grid=(HEADS, bsz),
        in_specs=[half, sl(Z0), pl.BlockSpec((1, DH), lambda s, b: (0, 0)), sl(SX0), sl(SB0), sl(SC0),
                  pl.BlockSpec((KS, DH), lambda s, b: (0, s)), sl(0), sl(GW)],
        out_specs=[half] * 5 + [pl.BlockSpec((1, DH), lambda s, b: (0, 0)), pl.BlockSpec((KS, DH), lambda s, b: (0, s))],
        out_shape=[jax.ShapeDtypeStruct((bsz * lp, GW), F32)] + [jax.ShapeDtypeStruct((bsz * lp, GW), BF16)] * 4
        + [jax.ShapeDtypeStruct((1, DH), F32), jax.ShapeDtypeStruct((KS, SW), F32)],
        compiler_params=_cparams(("arbitrary", "arbitrary")),
    )(o, proj, gdn_norm, proj, proj, proj, conv_sc, dmixin, dmixin)


def _split_dot(a, b, dims=NN):
    ah = a.astype(BF16)
    al = (a - ah.astype(F32)).astype(BF16)
    bh = b.astype(BF16)
    bl = (b - bh.astype(F32)).astype(BF16)

    def d(x, y):
        return lax.dot_general(x, y, (dims, ((), ())), preferred_element_type=F32)

    return d(ah, bh) + (d(ah, bl) + d(al, bh))


def _eye(n):
    return (lax.broadcasted_iota(jnp.int32, (n, n), 0) == lax.broadcasted_iota(jnp.int32, (n, n), 1)).astype(F32)


def _each(f, *lists):
    return [f(*xs) for xs in zip(*lists)]


def _inv_unit_lower(a_list):
    eye = _eye(CH)
    p = _each(lambda a: -a, a_list)
    t = _each(lambda x: eye + x, p)
    p = _each(lambda x: _split_dot(x, x), p)
    for level in range(5):
        if level < 4:
            prod = _each(lambda tt, pp: _split_dot(jnp.concatenate([tt, pp], axis=0), pp), t, p)
            t = _each(lambda tt, pr: tt + pr[:CH], t, prod)
            p = _each(lambda pr: pr[CH:], prod)
        else:
            t = _each(lambda tt, pp: tt + _split_dot(tt, pp), t, p)
    return t


def _chunk_gates(gc_ref, be_ref, gt_ref, g):
    rows = pl.ds(g * CH, CH)
    gc = gc_ref[0, rows, :]
    be = be_ref[0, rows, :]
    gcl = gc_ref[0, pl.ds(g * CH + CH - 1, 1), :]
    gt = gt_ref[0, g]
    incl, upper = _tri(CH, "incl"), _tri(CH, "upper")
    dec = jnp.where(incl, jnp.exp(jnp.where(incl, gc[:, :CH] - gt, 0.0)), 0.0)
    dec_t = jnp.where(upper, jnp.exp(jnp.where(upper, gt - gc[:, :CH], 0.0)), 0.0)
    return rows, be, dec, dec_t, jnp.exp(gc), jnp.exp(gcl - gc), jnp.exp(gcl)


def _chunk_group(nchunk):
    return 3 if nchunk % 3 == 0 else 1


def _local_group(nchunk):
    return 11 if nchunk % 11 == 0 else _chunk_group(nchunk)


def _gdn_local_specs(lp, grp):
    ngrp = lp // (grp * CH)
    head = lambda col0: pl.BlockSpec((grp * CH, DH), lambda i, j: ((i // HEADS) * ngrp + j, col0 // DH + i % HEADS))
    lane_b = pl.BlockSpec((1, grp * CH, DH), lambda i, j: (i, j, 0))
    per_chunk = lambda r, c: pl.BlockSpec((1, grp, r, c), lambda i, j: (i, j, 0, 0))
    return ngrp, head, lane_b, per_chunk


def _gdn_local_fwd(qkv, gcb, bb, gct, bsz, lp):
    nchunk = lp // CH
    grp = _local_group(nchunk)
    ngrp, head, lane_b, per_chunk = _gdn_local_specs(lp, grp)
    bh = bsz * HEADS

    def body(q_ref, k_ref, v_ref, gc_ref, be_ref, gt_ref,
             u_ref, l1_ref, l2_ref, l3_ref, wt_ref, kd_ref, ti_ref, gl_ref):
        incl, strict, upper = _tri(CH, "incl"), _tri(CH, "strict"), _tri(CH, "upper")
        eye = _eye(CH)
        rows, be, dec, dec_t, eg, ekd, gl = zip(*[_chunk_gates(gc_ref, be_ref, gt_ref, g) for g in range(grp)])
        q = [q_ref[r, :] for r in rows]
        k = [k_ref[r, :] for r in rows]
        v = [v_ref[r, :] for r in rows]
        kb = _each(lambda x, y: x * y, k, be)
        a = _each(lambda x, y, d: jnp.where(strict, _bdot(x, y, NT) * d, 0.0), kb, k, dec)
        tinv = _inv_unit_lower(a)
        u = _each(lambda t, x, y: _bdot(t, x * y, NN), tinv, v, be)
        w = _each(lambda t, x, y: _bdot(t, x * y, NN), tinv, kb, eg)
        qk = _each(lambda x, y, d: jnp.where(incl, _bdot(x, y, NT) * d, 0.0), q, k, dec)
        qk_t = _each(lambda x, y, d: jnp.where(upper, _bdot(y, x, NT) * d, 0.0), q, k, dec_t)
        qd = _each(lambda x, y: x * y, q, eg)
        kd = _each(lambda x, y: x * y, k, ekd)
        kd_t = _each(lambda x: _bdot(x, eye, TN), kd)
        qd_t = _each(lambda x: _bdot(x, eye, TN), qd)
        w_t = _each(lambda x: _bdot(x, eye, TN), w)
        for g in range(grp):
            u_ref[rows[g], :] = u[g]
            l1_ref[0, g] = jnp.concatenate([w[g], qd[g]], axis=0).astype(BF16)
            l2_ref[0, g] = jnp.concatenate([qk[g], kd_t[g]], axis=0).astype(BF16)
            l3_ref[0, g] = jnp.concatenate([qk_t[g], qd_t[g]], axis=0).astype(BF16)
            wt_ref[0, g] = w_t[g].astype(BF16)
            kd_ref[0, g] = kd[g].astype(BF16)
            ti_ref[0, g] = tinv[g]
            gl_ref[0, g] = gl[g]

    r = bsz * lp
    shapes = [((r, GW), F32), ((bh, nchunk, 2 * CH, DH), BF16), ((bh, nchunk, 3 * CH, CH), BF16),
              ((bh, nchunk, 3 * CH, CH), BF16), ((bh, nchunk, DH, CH), BF16), ((bh, nchunk, CH, DH), BF16),
              ((bh, nchunk, CH, CH), F32), ((bh, nchunk, 1, DH), F32)]
    return pl.pallas_call(
        body, name="gdn_local_fwd", grid=(bh, ngrp),
        in_specs=[head(0), head(GW), head(2 * GW), lane_b, lane_b, per_chunk(1, CH)],
        out_specs=[head(0), per_chunk(2 * CH, DH), per_chunk(3 * CH, CH), per_chunk(3 * CH, CH), per_chunk(DH, CH),
                   per_chunk(CH, DH), per_chunk(CH, CH), per_chunk(1, DH)],
        out_shape=[jax.ShapeDtypeStruct(s, dt) for s, dt in shapes],
        compiler_params=_cparams(("parallel", "parallel")),
    )(qkv, qkv, qkv, gcb, bb, gct)


def _scan_specs(bsz, lp, grp, order):
    ngrp = lp // (grp * CH)
    bh = bsz * HEADS
    seq = lambda: pl.BlockSpec((bsz, grp * CH, GW), lambda s: (0, order(s, ngrp), 0))
    per_chunk = lambda r, c: pl.BlockSpec((bh, grp, r, c), lambda s: (0, order(s, ngrp), 0, 0))
    return ngrp, bh, seq, per_chunk


def _gdn_scan_fwd(u, l1, l2, gl, bsz, lp):
    nchunk = lp // CH
    grp = _chunk_group(nchunk)
    ngrp, bh, seq, per_chunk = _scan_specs(bsz, lp, grp, lambda s, n: s)

    def body(u_ref, l1_ref, l2_ref, gl_ref, o_ref, vn_ref, st_ref, s_scr):
        @pl.when(pl.program_id(0) == 0)
        def _():
            s_scr[...] = jnp.zeros_like(s_scr)

        for c in range(grp):
            rows = slice(c * CH, (c + 1) * CH)
            for i in range(bh):
                b, h = divmod(i, HEADS)
                cols = slice(h * DH, (h + 1) * DH)
                s = s_scr[i]
                st_ref[i, c] = s
                x = jnp.dot(l1_ref[i, c], s.astype(BF16), preferred_element_type=F32)
                vnew = u_ref[b, rows, cols] - x[:CH]
                y = jnp.dot(l2_ref[i, c], vnew.astype(BF16), preferred_element_type=F32)
                o_ref[b, rows, cols] = x[CH:] + y[:CH]
                vn_ref[b, rows, cols] = vnew.astype(BF16)
                s_scr[i] = s * gl_ref[i, c] + y[CH:]

    return pl.pallas_call(
        body, name="gdn_scan_fwd", grid=(ngrp,),
        in_specs=[seq(), per_chunk(2 * CH, DH), per_chunk(3 * CH, CH), per_chunk(1, DH)],
        out_specs=[seq(), seq(), per_chunk(DH, DH)],
        out_shape=[jax.ShapeDtypeStruct((bsz, lp, GW), F32), jax.ShapeDtypeStruct((bsz, lp, GW), BF16),
                   jax.ShapeDtypeStruct((bh, nchunk, DH, DH), F32)],
        scratch_shapes=[pltpu.VMEM((bh, DH, DH), F32)],
        compiler_params=_cparams(("arbitrary",)),
    )(u.reshape(bsz, lp, GW), l1, l2, gl)


def _gdn_scan_bwd(d_o, l3, wt, kd, gl, bsz, lp):
    nchunk = lp // CH
    grp = _chunk_group(nchunk)
    ngrp, bh, seq, per_chunk = _scan_specs(bsz, lp, grp, lambda s, n: n - 1 - s)

    def body(do_ref, l3_ref, wt_ref, kd_ref, gl_ref, dvn_ref, dst_ref, ds_scr):
        @pl.when(pl.program_id(0) == 0)
        def _():
            ds_scr[...] = jnp.zeros_like(ds_scr)

        for c in reversed(range(grp)):
            rows = slice(c * CH, (c + 1) * CH)
            for i in range(bh):
                b, h = divmod(i, HEADS)
                cols = slice(h * DH, (h + 1) * DH)
                ds = ds_scr[i]
                dst_ref[i, c] = ds
                p = jnp.dot(l3_ref[i, c], do_ref[b, rows, cols].astype(BF16), preferred_element_type=F32)
                dvn = p[:CH] + jnp.dot(kd_ref[i, c], ds.astype(BF16), preferred_element_type=F32)
                dvn_ref[b, rows, cols] = dvn.astype(BF16)
                ds_scr[i] = ds * gl_ref[i, c] + p[CH:] - jnp.dot(wt_ref[i, c], dvn.astype(BF16),
                                                                 preferred_element_type=F32)

    return pl.pallas_call(
        body, name="gdn_scan_bwd", grid=(ngrp,),
        in_specs=[seq(), per_chunk(3 * CH, CH), per_chunk(DH, CH), per_chunk(CH, DH), per_chunk(1, DH)],
        out_specs=[seq(), per_chunk(DH, DH)],
        out_shape=[jax.ShapeDtypeStruct((bsz, lp, GW), BF16), jax.ShapeDtypeStruct((bh, nchunk, DH, DH), F32)],
        scratch_shapes=[pltpu.VMEM((bh, DH, DH), F32)],
        compiler_params=_cparams(("arbitrary",)),
    )(d_o.reshape(bsz, lp, GW), l3, wt, kd, gl)


def _gdn_local_bwd(qkv, gcb, bb, gct, tinv_all, states, dstates, vnew, dvnew, d_o, bsz, lp):
    nchunk = lp // CH
    grp = _local_group(nchunk)
    ngrp, head, lane_b, per_chunk = _gdn_local_specs(lp, grp)
    bh = bsz * HEADS

    def body(q_ref, k_ref, v_ref, gc_ref, be_ref, gt_ref, ti_ref, st_ref, dst_ref, vn_ref, dvn_ref, do_ref,
             dq_ref, dk_ref, dv_ref, dg_ref, db_ref):
        incl, strict = _tri(CH, "incl"), _tri(CH, "strict")
        upper = _tri(CH, "upper").astype(F32)
        ones = jnp.ones((CH, DH), F32)
        last = lax.broadcasted_iota(jnp.int32, (CH, 1), 0) == CH - 1

        def rsum(x):
            return jnp.sum(x, axis=-1, keepdims=True)

        def mul(xs, ys):
            return _each(lambda x, y: x * y, xs, ys)

        rows, be, dec, _, eg, ekd, gl = zip(*[_chunk_gates(gc_ref, be_ref, gt_ref, g) for g in range(grp)])
        q = [q_ref[r, :] for r in rows]
        k = [k_ref[r, :] for r in rows]
        v = [v_ref[r, :] for r in rows]
        vnew = [vn_ref[r, :] for r in rows]
        dvn = [dvn_ref[r, :] for r in rows]
        do = [do_ref[r, :] for r in rows]
        tinv = [ti_ref[0, g] for g in range(grp)]
        s = [st_ref[0, g] for g in range(grp)]
        ds = [dst_ref[0, g] for g in range(grp)]
        kb, vb, qd, kd = mul(k, be), mul(v, be), mul(q, eg), mul(k, ekd)
        kbg = mul(kb, eg)
        a = _each(lambda x, y, d: jnp.where(strict, _bdot(x, y, NT) * d, 0.0), kb, k, dec)
        qk = _each(lambda x, y, d: jnp.where(incl, _bdot(x, y, NT) * d, 0.0), q, k, dec)
        dqk = _each(lambda x, y: jnp.where(incl, _bdot(x, y, NT), 0.0), do, vnew)
        dqd = _each(lambda x, y: _bdot(x, y, NT), do, s)
        dkd = _each(lambda x, y: _bdot(x, y, NT), vnew, ds)
        dgl = _each(lambda x, y: jnp.sum(rsum(x * y), axis=0, keepdims=True), s, ds)
        dw = _each(lambda x, y: -_bdot(x, y, NT), dvn, s)
        dvb = _each(lambda t, x: _bdot(t, x, TN), tinv, dvn)
        dkbg = _each(lambda t, x: _bdot(t, x, TN), tinv, dw)
        dt = _each(lambda x, y, z, w: _bdot(x, y, NT) + _bdot(z, w, NT), dvn, vb, dw, kbg)
        tdt = _each(lambda t, x: _split_dot(t, x, TN), tinv, dt)
        da = _each(lambda x, t: jnp.where(strict, -_split_dot(x, t, NT), 0.0), tdt, tinv)
        dp, dr = mul(da, dec), mul(dqk, dec)
        dkb = _each(lambda x, y, z, e: _bdot(x, y, NN) + z * e, dp, k, dkbg, eg)
        dk = _each(lambda p_, kb_, r_, q_, dkd_, ekd_, dkb_, be_:
                   _bdot(p_, kb_, TN) + _bdot(r_, q_, TN) + dkd_ * ekd_ + dkb_ * be_,
                   dp, kb, dr, q, dkd, ekd, dkb, be)
        dq = _each(lambda r_, k_, dqd_, eg_: _bdot(r_, k_, NN) + dqd_ * eg_, dr, k, dqd, eg)
        m = _each(lambda da_, a_, dqk_, qk_: da_ * a_ + dqk_ * qk_, da, a, dqk, qk)
        m_cols = _each(lambda x: _split_dot(x, ones, TN), m)
        kd_term = _each(lambda x, y: rsum(x * y), dkd, kd)
        dgc = _each(lambda m_, mc, dqd_, qd_, kt, dkbg_, kbg_, dgl_, gl_:
                    rsum(m_) - mc + rsum(dqd_ * qd_) - kt + rsum(dkbg_ * kbg_)
                    + jnp.where(last, jnp.sum(kt, axis=0, keepdims=True) + dgl_ * gl_, 0.0),
                    m, m_cols, dqd, qd, kd_term, dkbg, kbg, dgl, gl)
        dg = _each(lambda x: _split_dot(upper, x, NN), dgc)
        for g in range(grp):
            dq_ref[rows[g], :] = dq[g]
            dk_ref[rows[g], :] = dk[g]
            dv_ref[rows[g], :] = dvb[g] * be[g]
            dg_ref[0, rows[g], :] = dg[g]
            db_ref[0, rows[g], :] = jnp.broadcast_to(rsum(dkb[g] * k[g]) + rsum(dvb[g] * v[g]), (CH, DH))

    r = bsz * lp
    return pl.pallas_call(
        body, name="gdn_local_bwd", grid=(bh, ngrp),
        in_specs=[head(0), head(GW), head(2 * GW), lane_b, lane_b, per_chunk(1, CH), per_chunk(CH, CH),
                  per_chunk(DH, DH), per_chunk(DH, DH), head(0), head(0), head(0)],
        out_specs=[head(0), head(0), head(0), lane_b, lane_b],
        out_shape=[jax.ShapeDtypeStruct((r, GW), F32)] * 3 + [jax.ShapeDtypeStruct((bh, lp, DH), F32)] * 2,
        compiler_params=_cparams(("parallel", "parallel")),
    )(qkv, qkv, qkv, gcb, bb, gct, tinv_all, states, dstates, vnew, dvnew, d_o)


def _place():
    return lax.axis_index("x"), lax.axis_index("y"), lax.axis_index("c")


def _any_specs(n):
    return [pl.BlockSpec(memory_space=pl.ANY)] * n


def _all_gather(name, shards):
    n = len(shards)

    def body(*refs):
        ins, outs = refs[:n], refs[n:2 * n]
        send_sems, recv_sems, local_sems = refs[2 * n:]
        x, y, c = _place()
        me, sibling = (x, y, c), (x, y, 1 - c)
        chips = [(1 - x, y), (x, 1 - y), (1 - x, 1 - y)]

        def slot(i, p):
            return outs[i].at[4 * p[0] + 2 * p[1] + p[2]]

        def copy(i, k, block, to, src=None):
            return pltpu.make_async_remote_copy(
                src_ref=slot(i, block) if src is None else src, dst_ref=slot(i, block),
                send_sem=send_sems.at[i, k], recv_sem=recv_sems.at[i, k], device_id=to, device_id_type=MESH)

        mine = [pltpu.make_async_copy(ins[i], slot(i, me), local_sems.at[i]) for i in range(n)]
        for cp in mine:
            cp.start()
        first = []
        for i in range(n):
            first.append(copy(i, 0, me, sibling, src=ins[i]))
            first += [copy(i, 1 + j, me, (*chip, c), src=ins[i]) for j, chip in enumerate(chips)]
        for cp in first:
            cp.start()
        passed = []
        for j, chip in enumerate(chips):
            for i in range(n):
                copy(i, 1 + j, (*chip, c), me).wait_recv()
                fwd = copy(i, 4 + j, (*chip, c), sibling)
                fwd.start()
                passed.append(fwd)
        for i in range(n):
            copy(i, 0, sibling, me).wait_recv()
            for j, chip in enumerate(chips):
                copy(i, 4 + j, (*chip, 1 - c), me).wait_recv()
        for cp in first + passed:
            cp.wait_send()
        for cp in mine:
            cp.wait()

    return pl.pallas_call(
        body, name=name,
        in_specs=_any_specs(n), out_specs=_any_specs(n),
        out_shape=[jax.ShapeDtypeStruct((N_DEV,) + s.shape, s.dtype) for s in shards],
        scratch_shapes=[pltpu.SemaphoreType.DMA((n, 7)), pltpu.SemaphoreType.DMA((n, 7)),
                        pltpu.SemaphoreType.DMA((n,))],
    )(*shards)


HBM_SPEC = pl.BlockSpec(memory_space=pltpu.HBM)
SEM_SPEC = pl.BlockSpec(memory_space=pltpu.SEMAPHORE)
EFFECT = pltpu.SideEffectType.DATAFLOW_SIDE_EFFECTING
N_PEER = N_DEV - 1


def _peer(r):
    x, y, c = _place()
    return ((1 - x) if r & 4 else x, (1 - y) if r & 2 else y, (1 - c) if r & 1 else c)


def _slot_of(p):
    return 4 * p[0] + 2 * p[1] + p[2]


def _hbm(a):
    return pltpu.with_memory_space_constraint(a, pltpu.HBM)


def _gather_pattern(src, land):
    me = _place()
    return [(src, land.at[_slot_of(me)], _peer(r), land.at[_slot_of(_peer(r))]) for r in range(1, N_DEV)]


def _sibling_pattern(src, land):
    x, y, c = _place()
    return [(src.at[ch, 1 - c], land.at[ch], (x, y, 1 - c), land.at[ch]) for ch in range(4)]


def _chip_pattern(src, land):
    x, y, c = _place()
    chips = [(1 - x, y), (x, 1 - y), (1 - x, 1 - y)]
    return [(src.at[2 * ch[0] + ch[1]], land.at[j], (*ch, c), land.at[j]) for j, ch in enumerate(chips)]


def _split_start(name, groups, pattern, land_shape, after=None):
    sizes = [len(g) for g in groups]
    arrays = [s for g in groups for s in g]
    n = len(arrays)
    land_shapes = [land_shape(s.shape) for s in arrays]
    order = [] if after is None else [after]

    def body(*refs):
        srcs, lands = refs[:n], refs[n:2 * n]
        first_out = 2 * n + len(order)
        sems = refs[first_out:first_out + 2 * len(groups)]
        token = refs[-1]
        k = 0
        for gi, size in enumerate(sizes):
            for i in range(size):
                copies = pattern(srcs[k], lands[k])
                for j, (s, d, peer, _) in enumerate(copies):
                    pltpu.make_async_remote_copy(
                        src_ref=s, dst_ref=d, send_sem=sems[2 * gi].at[i * len(copies) + j],
                        recv_sem=sems[2 * gi + 1].at[i * len(copies) + j], device_id=peer, device_id_type=MESH).start()
                k += 1
        token[...] = jnp.zeros_like(token)

    per_array = {_gather_pattern: N_PEER, _sibling_pattern: 4, _chip_pattern: 3}[pattern]
    sem_shapes = [pltpu.SemaphoreType.DMA((size * per_array,)) for size in sizes for _ in range(2)]
    res = pl.pallas_call(
        body, name=name,
        in_specs=[HBM_SPEC] * (2 * n) + [pl.BlockSpec(memory_space=pl.ANY)] * len(order),
        out_specs=[SEM_SPEC] * len(sem_shapes) + [HBM_SPEC] * (2 * n) + [pl.BlockSpec(memory_space=pltpu.VMEM)],
        out_shape=sem_shapes + [pltpu.HBM(s.shape, s.dtype) for s in arrays]
        + [pltpu.HBM(ls, s.dtype) for ls, s in zip(land_shapes, arrays)] + [jax.ShapeDtypeStruct((8, DH), F32)],
        input_output_aliases={i: len(sem_shapes) + i for i in range(2 * n)},
        compiler_params=pltpu.CompilerParams(has_side_effects=EFFECT),
    )(*[_hbm(s) for s in arrays], *[_hbm(lax.empty(ls, s.dtype)) for ls, s in zip(land_shapes, arrays)], *order)
    nsem = len(sem_shapes)
    out, k = [], 0
    for gi, size in enumerate(sizes):
        out.append((res[2 * gi], res[2 * gi + 1], list(res[nsem + k:nsem + k + size]),
                    list(res[nsem + n + k:nsem + n + k + size])))
        k += size
    return out, res[-1]


def _split_wait(name, group, pattern, after):
    send_sems, recv_sems, arrays, lands = group
    n = len(arrays)
    fill_own = pattern is _gather_pattern

    def body(*refs):
        srcs, lands_ = refs[:n], refs[n:2 * n]
        s_sems, r_sems = refs[2 * n], refs[2 * n + 1]
        own = []
        if fill_own:
            local_sems = refs[-1]
            own = [pltpu.make_async_copy(srcs[i], lands_[i].at[_slot_of(_place())], local_sems.at[i])
                   for i in range(n)]
            for cp in own:
                cp.start()
        for i in range(n):
            copies = pattern(srcs[i], lands_[i])
            for j, (s, _, peer, filled) in enumerate(copies):
                cp = pltpu.make_async_remote_copy(
                    src_ref=s, dst_ref=filled, send_sem=s_sems.at[i * len(copies) + j],
                    recv_sem=r_sems.at[i * len(copies) + j], device_id=peer, device_id_type=MESH)
                cp.wait_send()
                cp.wait_recv()
        for cp in own:
            cp.wait()

    res = pl.pallas_call(
        body, name=name,
        in_specs=[HBM_SPEC] * (2 * n) + [SEM_SPEC, SEM_SPEC, pl.BlockSpec(memory_space=pl.ANY)],
        out_specs=[HBM_SPEC] * (2 * n),
        out_shape=[pltpu.HBM(s.shape, s.dtype) for s in arrays] + [pltpu.HBM(l.shape, l.dtype) for l in lands],
        scratch_shapes=[pltpu.SemaphoreType.DMA((n,))] if fill_own else [],
        input_output_aliases={i: i for i in range(2 * n)},
        compiler_params=pltpu.CompilerParams(has_side_effects=EFFECT),
    )(*arrays, *lands, send_sems, recv_sems, after)
    return list(res[:n]), list(res[n:])


def _pair_sum(name, grads, recv, core):
    _, _, r, c = grads.shape
    tr = _div_tile(r, 256, 16) if r % 16 == 0 else r

    def body(core_ref, g_ref, r_ref, o_ref):
        o_ref[...] = (g_ref[0].astype(F32) + r_ref[...].astype(F32)).astype(o_ref.dtype)

    return pl.pallas_call(
        body, name=name,
        grid_spec=pltpu.PrefetchScalarGridSpec(
            num_scalar_prefetch=1, grid=(4, r // tr),
            in_specs=[pl.BlockSpec((1, 1, tr, c), lambda s, i, cr: (s, cr[0], i, 0)),
                      pl.BlockSpec((1, tr, c), lambda s, i, cr: (s, i, 0))],
            out_specs=pl.BlockSpec((1, tr, c), lambda s, i, cr: (s, i, 0))),
        out_shape=jax.ShapeDtypeStruct(recv.shape, recv.dtype),
        compiler_params=_cparams(("parallel", "parallel")),
    )(core, grads, recv)


def _adamw_math(w, g, m, v):
    m = ADAM_B1 * m + (1.0 - ADAM_B1) * g
    v = ADAM_B2 * v + (1.0 - ADAM_B2) * (g * g)
    m_hat = m / (1.0 - ADAM_B1 ** ADAM_STEP)
    v_hat = v / (1.0 - ADAM_B2 ** ADAM_STEP)
    delta = -ADAM_LR * (m_hat / (jnp.sqrt(v_hat) + ADAM_EPS) + ADAM_WD * w)
    return delta, m, v


def _adamw(name, parts, w, m, v):
    nslot, r, c = parts.shape
    tr = _div_tile(r, 256, 8) if r % 8 == 0 else r

    def body(p_ref, w_ref, m_ref, v_ref, g_out, d_out, m_out, v_out):
        g = p_ref[0]
        for s in range(1, nslot):
            g = g + p_ref[s]
        d, mn, vn = _adamw_math(w_ref[...], g, m_ref[...], v_ref[...])
        g_out[...] = g
        d_out[...] = d
        m_out[...] = mn
        v_out[...] = vn

    blk = pl.BlockSpec((tr, c), lambda i: (i, 0))
    return pl.pallas_call(
        body, name=name, grid=(r // tr,),
        in_specs=[pl.BlockSpec((nslot, tr, c), lambda i: (0, i, 0)), blk, blk, blk],
        out_specs=[blk] * 4,
        out_shape=[jax.ShapeDtypeStruct((r, c), F32)] * 4,
        compiler_params=_cparams(("parallel",)),
    )(parts, w, m, v)


def _adamw_reduced(name, parts, chip, recv, w, m, v):
    _, r, c = parts.shape
    tr = _div_tile(r, 256, 16) if r % 16 == 0 else r

    def body(chip_ref, p_ref, r_ref, w_ref, m_ref, v_ref, g_out, d_out, m_out, v_out):
        g = p_ref[0].astype(F32)
        for s in range(3):
            g = g + r_ref[s].astype(F32)
        d, mn, vn = _adamw_math(w_ref[...], g, m_ref[...], v_ref[...])
        g_out[...] = g
        d_out[...] = d
        m_out[...] = mn
        v_out[...] = vn

    blk = pl.BlockSpec((tr, c), lambda i, ch: (i, 0))
    return pl.pallas_call(
        body, name=name,
        grid_spec=pltpu.PrefetchScalarGridSpec(
            num_scalar_prefetch=1, grid=(r // tr,),
            in_specs=[pl.BlockSpec((1, tr, c), lambda i, ch: (ch[0], i, 0)),
                      pl.BlockSpec((3, tr, c), lambda i, ch: (0, i, 0)), blk, blk, blk],
            out_specs=[blk] * 4),
        out_shape=[jax.ShapeDtypeStruct((r, c), F32)] * 4,
        compiler_params=_cparams(("parallel",)),
    )(chip, parts, recv, w, m, v)


SMALL_ROWS, SMALL_COLS = 32, 3 * GW


def _pack_small(gains4, meta, conv_qkv, conv_sc, a_log, dt_bias, gdn_norm, extra=None):
    def padc(a):
        return jnp.pad(a, ((0, 0), (0, SMALL_COLS - a.shape[1])))
    scal = jnp.concatenate([jnp.pad(a_log, ((0, 0), (0, DH - HEADS))), jnp.pad(dt_bias, ((0, 0), (0, DH - HEADS))),
                            gdn_norm] + ([] if extra is None else [extra]), axis=1)
    rows = [padc(g) for g in gains4] + [padc(meta), padc(conv_qkv), padc(conv_sc), padc(scal)]
    slab = jnp.concatenate(rows, axis=0)
    return jnp.pad(slab, ((0, SMALL_ROWS - slab.shape[0]), (0, 0)))


def _unpack_small(slab):
    gains = [slab[i:i + 1, :D] for i in range(4)]
    meta = slab[4:20, :D]
    conv_qkv = slab[20:24, :]
    conv_sc = slab[24:27, :SW]
    a_log = slab[27:28, 0:HEADS]
    dt_bias = slab[27:28, DH:DH + HEADS]
    gdn_norm = slab[27:28, 2 * DH:3 * DH]
    return gains, meta, conv_qkv, conv_sc, a_log, dt_bias, gdn_norm


def kernel(x, meta_tokens, mix_pre_norm, mix_post_norm, ffn_pre_norm, ffn_post_norm, w_in, conv_qkv, a_log, dt_bias, gdn_norm, conv_sc, w_out, w_gate, w_up, w_down, loss_target, m_meta_tokens, m_mix_pre_norm, m_mix_post_norm, m_ffn_pre_norm, m_ffn_post_norm, m_w_in, m_conv_qkv, m_a_log, m_dt_bias, m_gdn_norm, m_conv_sc, m_w_out, m_w_gate, m_w_up, m_w_down, v_meta_tokens, v_mix_pre_norm, v_mix_post_norm, v_ffn_pre_norm, v_ffn_post_norm, v_w_in, v_conv_qkv, v_a_log, v_dt_bias, v_gdn_norm, v_conv_sc, v_w_out, v_w_gate, v_w_up, v_w_down):
    bsz, seq, _ = x.shape
    lp = -(-(N_META + seq) // CH) * CH
    padf = lp - N_META - seq
    first_real = padf + N_META
    rows = bsz * lp
    nchunk = lp // CH
    tr = _div_tile(lp, 528)
    tm = _div_tile(rows, 1056)
    tm_big = _div_tile(rows, 2112)
    cx, cy, cc = _place()
    dev = 4 * cx + 2 * cy + cc

    small_shard = jnp.concatenate([
        jnp.pad(meta_tokens, ((0, 0), (0, 256 - DH))),
        jnp.pad(conv_qkv[0], ((0, 0), (0, 256 - 3 * GW // N_DEV))),
        jnp.pad(conv_sc[0], ((0, 5), (0, 256 - SW // N_DEV)))], axis=0)
    g_in, g_small = _all_gather("gather_w_in", [w_in[0].T.astype(BF16), small_shard])
    my_out = [w_out[0].astype(BF16)]
    my_ffn = [w_gate[0].T.astype(BF16), w_up[0].T.astype(BF16)]
    my_down = [w_down[0].astype(BF16)]
    (grp_out, grp_ffn, grp_down), token = _split_start(
        "gather_rest_start", [my_out, my_ffn, my_down], _gather_pattern, lambda s: (N_DEV,) + s, after=g_in)
    after_start = token[0:1, 0:1]

    def cols_full(g):
        return jnp.transpose(g, (1, 0, 2)).reshape(g.shape[1], -1)

    win_t = g_in.reshape(IN_W, D)
    win_t = jnp.concatenate([win_t[:BA0 + 2 * HEADS], jnp.zeros((BA_W - 2 * HEADS, D), BF16),
                             win_t[BA0 + 2 * HEADS:]], axis=0)
    meta_full = cols_full(g_small[:, 0:16, :DH])
    convq_full = cols_full(g_small[:, 16:20, :3 * GW // N_DEV])
    convs_full = cols_full(g_small[:, 20:23, :SW // N_DEV])
    alog_v = jnp.pad(a_log, ((0, 0), (HEADS, DH - 2 * HEADS)))
    dtb_v = jnp.pad(dt_bias, ((0, 0), (HEADS, DH - 2 * HEADS)))

    head_rows = jnp.concatenate([jnp.zeros((padf, D), F32), meta_full], axis=0)
    h0, u1 = _pre_norm(x, head_rows, mix_pre_norm + after_start, bsz, lp, tr)
    (proj,) = _mm("proj_in", [(u1, win_t)], "nt", tm_big, 768, D, out_dtypes=(BF16,))
    (ba,) = _mm("proj_gate_logits", [(u1, win_t[BA0:SX0])], "nt", tm, BA_W, D)
    qkv = _gdn_prep(proj, convq_full, bsz, lp, padf)
    gb, gcs = _gates(ba, alog_v, dtb_v, bsz, lp, padf)

    def lane_bcast(cols):
        t = jnp.transpose(cols.reshape(bsz, lp, HEADS), (0, 2, 1)).reshape(bsz * HEADS, lp, 1)
        return jnp.broadcast_to(t, (bsz * HEADS, lp, DH))

    bb = lane_bcast(gb[:, 0:HEADS])
    gcb = lane_bcast(gcs[:, HEADS:2 * HEADS])
    gct = jnp.transpose(gcs[:, HEADS:2 * HEADS].reshape(bsz, nchunk, CH, HEADS), (0, 3, 1, 2)).reshape(
        bsz * HEADS, nchunk, 1, CH)
    u_loc, l1, l2, l3, w_t, k_dec, tinv_all, g_last = _gdn_local_fwd(qkv, gcb, bb, gct, bsz, lp)
    o, v_new, states = _gdn_scan_fwd(u_loc, l1, l2, g_last, bsz, lp)
    o, v_new = o.reshape(rows, GW), v_new.reshape(rows, GW)
    mix_g, mix_s = _mix_heads(proj, o, gdn_norm, convs_full, bsz, lp, padf)
    mixin = jnp.concatenate([mix_g, mix_s], axis=1)
    _, (l_out,) = _split_wait("gather_w_out_wait", grp_out, _gather_pattern, mixin)
    wout = l_out.reshape(D, D)
    (mix,) = _mm("proj_out", [(mixin, wout)], "nn", tm, 512, D)
    h1, u2 = _mid_norms(h0, mix, mix_post_norm, ffn_pre_norm, bsz, lp, tr)
    _, (l_gate, l_up) = _split_wait("gather_ffn_wait", grp_ffn, _gather_pattern, u2)
    wgate_t, wup_t = l_gate.reshape(FF, D), l_up.reshape(FF, D)
    gate, up, act = _ffn_up(u2, wgate_t, wup_t, _div_tile(rows, 528), 1408)
    _, (l_down,) = _split_wait("gather_down_wait", grp_down, _gather_pattern, act)
    wdown = l_down.reshape(FF, D)
    (ffn,) = _mm("ffn_down", [(act, wdown)], "nn", tm, 512, 1408)

    dy, dffn, d_g4, loss_part = _loss_head(h1, ffn, loss_target, ffn_post_norm, bsz, lp, tr, first_real)
    (dwdown,) = _mm("dw_down", [(act, dffn)], "tn", 1408, 512, tm, out_dtypes=(BF16,))

    def swiglu_bwd(da, g, u):
        g, u = g.astype(F32), u.astype(F32)
        return da * u * _dsilu(g), da * _silu(g)

    dgate, dup = _mm("d_act", [(dffn, wdown)], "nt", tm, 1408, D, out_dtypes=(BF16, BF16),
                     epilogue=swiglu_bwd, extras=(gate, up))
    (dwgate_t,) = _mm("dw_gate", [(dgate, u2)], "tn", 1408, D, tm, out_dtypes=(BF16,))
    (dwup_t,) = _mm("dw_up", [(dup, u2)], "tn", 1408, D, tm, out_dtypes=(BF16,))
    (du2,) = _mm("d_u2", [(dgate, wgate_t), (dup, wup_t)], "nn", tm, D, 1408)
    dh1, dmix, d_g3, d_g2 = _mid_norms_bwd(h1, mix, du2, dy, ffn_pre_norm, mix_post_norm, bsz, lp, tr)
    (dwout,) = _mm("dw_out", [(mixin, dmix)], "tn", D, 512, tm, out_dtypes=(BF16,))

    def row_blocks(g):
        return g.reshape(4, 2, -1, g.shape[1])

    core = jnp.reshape(cc, (1,)).astype(jnp.int32)
    chip = jnp.reshape(2 * cx + cy, (1,)).astype(jnp.int32)

    def reduce_start(tag, grads):
        (grp,), tok = _split_start("rs_sibling_start_" + tag, [grads], _sibling_pattern, lambda s: s[:1] + s[2:])
        return grp, tok

    def reduce_mid(tag, grp, nms, after):
        mine, lands = _split_wait("rs_sibling_wait_" + tag, grp, _sibling_pattern, after)
        parts = [_pair_sum("pair_sum_" + nm, g, r, core) for nm, g, r in zip(nms, mine, lands)]
        (grp2,), tok = _split_start("rs_chip_start_" + tag, [parts], _chip_pattern, lambda s: (3,) + s[1:])
        return grp2, tok

    def reduce_end(tag, grp2, after):
        return _split_wait("rs_chip_wait_" + tag, grp2, _chip_pattern, after)

    names_ffn = ["w_down", "w_gate", "w_up", "w_out"]
    rs_ffn, tok = reduce_start("ffn", [row_blocks(dwdown), row_blocks(dwgate_t), row_blocks(dwup_t), row_blocks(dwout)])
    (dmixin,) = _mm("d_mixin", [(dmix, wout)], "nt", tm, 512, D, after=tok)
    d_o, dz, dsx, dsb, dsc, d_gn, d_convs = _mix_heads_bwd(proj, o, gdn_norm, convs_full, dmixin, bsz, lp, padf)
    rs_ffn, tok = reduce_mid("ffn", rs_ffn, names_ffn, d_o)
    dv_new, dstates = _gdn_scan_bwd(d_o, l3, w_t, k_dec, g_last + tok[0, 0], bsz, lp)
    dq, dk, dv, dg_b, dbeta_b = _gdn_local_bwd(qkv, gcb, bb, gct, tinv_all, states, dstates, v_new,
                                               dv_new.reshape(rows, GW), d_o, bsz, lp)
    dpqkv, d_convq = _gdn_prep_bwd(proj, convq_full, dq, dk, dv, bsz, lp, padf)
    dba, d_scal = _gates_bwd(ba, alog_v, dtb_v, dbeta_b, dg_b, gb, bsz, lp, padf)
    dproj = jnp.concatenate([dpqkv, dz, dba, dsx, dsb, dsc], axis=1)
    (dwin_t,) = _mm("dw_in", [(dproj, u1)], "tn", 1280, D, tm, out_dtypes=(BF16,))
    parts_ffn, recv_ffn = reduce_end("ffn", rs_ffn, dwin_t)
    dwin_t = jnp.concatenate([dwin_t[:BA0 + 2 * HEADS], dwin_t[SX0:]], axis=0)
    rs_in, tok = reduce_start("in", [row_blocks(dwin_t)])
    (du1,) = _mm("d_u1", [(dproj, win_t)], "nn", tm_big, D, 768, after=tok)
    rs_in, tok = reduce_mid("in", rs_in, ["w_in"], du1)
    dh0, d_g1, d_meta = _pre_norm_bwd(h0, du1, dh1, mix_pre_norm + tok[0:1, 0:1], bsz, lp, tr, padf)
    grad_x = dh0.reshape(bsz, lp, D)[:, first_real:]

    d_alog = d_scal[0:1, HEADS:2 * HEADS]
    d_dtb = d_scal[1:2, HEADS:2 * HEADS]
    slab = _pack_small([d_g1, d_g2, d_g3, d_g4], d_meta, d_convq, d_convs, d_alog, d_dtb, d_gn, loss_part)
    (grp_small,), _ = _split_start("small_grads_start", [[slab]], _gather_pattern, lambda s: (N_DEV,) + s)

    big_w = {"w_in": (w_in, m_w_in, v_w_in), "w_out": (w_out, m_w_out, v_w_out), "w_gate": (w_gate, m_w_gate, v_w_gate),
             "w_up": (w_up, m_w_up, v_w_up), "w_down": (w_down, m_w_down, v_w_down)}

    def update(nm, parts, recv):
        w, m, v = big_w[nm]
        if nm in ("w_out", "w_down"):
            return _adamw_reduced("adamw_" + nm, parts, chip, recv, w[0], m[0], v[0])
        res = _adamw_reduced("adamw_" + nm, parts, chip, recv, w[0].T, m[0].T, v[0].T)
        return [t.T for t in res]

    big_out = {nm: update(nm, p, r) for nm, p, r in zip(names_ffn, parts_ffn, recv_ffn)}
    ffn_done = sum(big_out[nm][1][:1, :1] for nm in names_ffn)
    _, (slabs,) = _split_wait("small_grads_wait", grp_small, _gather_pattern, ffn_done)
    off_meta = dev * DH
    off_cq = dev * (3 * GW // N_DEV)
    off_cs = dev * (SW // N_DEV)

    def shard_of(s):
        gains, meta, cq, cs, al, db, gn = _unpack_small(s)
        return gains, lax.dynamic_slice_in_dim(meta, off_meta, DH, 1), \
            lax.dynamic_slice_in_dim(cq, off_cq, 3 * GW // N_DEV, 1), \
            lax.dynamic_slice_in_dim(cs, off_cs, SW // N_DEV, 1), al, db, gn, s[27:28, 3 * DH:4 * DH]

    slabs_local = jax.vmap(lambda s: _pack_small(*shard_of(s)))(slabs)
    w_small = _pack_small([mix_pre_norm, mix_post_norm, ffn_pre_norm, ffn_post_norm], meta_tokens, conv_qkv[0],
                          conv_sc[0], a_log, dt_bias, gdn_norm)
    m_small = _pack_small([m_mix_pre_norm, m_mix_post_norm, m_ffn_pre_norm, m_ffn_post_norm], m_meta_tokens,
                          m_conv_qkv[0], m_conv_sc[0], m_a_log, m_dt_bias, m_gdn_norm)
    v_small = _pack_small([v_mix_pre_norm, v_mix_post_norm, v_ffn_pre_norm, v_ffn_post_norm], v_meta_tokens,
                          v_conv_qkv[0], v_conv_sc[0], v_a_log, v_dt_bias, v_gdn_norm)
    small_out = _adamw("adamw_small", slabs_local, w_small, m_small, v_small)
    loss = small_out[0][27, 3 * DH]
    parts_in, recv_in = reduce_end("in", rs_in, small_out[1])
    big_out["w_in"] = update("w_in", parts_in[0], recv_in[0])
    names = ["w_in", "w_out", "w_gate", "w_up", "w_down"]

    def small_leaves(slab_out):
        gains, meta, cq, cs, al, db, gn = _unpack_small(slab_out)
        return {"meta_tokens": meta[:, :DH], "mix_pre_norm": gains[0], "mix_post_norm": gains[1],
                "ffn_pre_norm": gains[2], "ffn_post_norm": gains[3], "conv_qkv": cq[None, :, :3 * GW // N_DEV],
                "a_log": al, "dt_bias": db, "gdn_norm": gn, "conv_sc": cs[None, :, :SW // N_DEV]}

    order = ["meta_tokens", "mix_pre_norm", "mix_post_norm", "ffn_pre_norm", "ffn_post_norm", "w_in", "conv_qkv",
             "a_log", "dt_bias", "gdn_norm", "conv_sc", "w_out", "w_gate", "w_up", "w_down"]
    outs = [loss, grad_x]
    for kind in range(4):
        leaves = small_leaves(small_out[kind])
        for nm in names:
            leaves[nm] = big_out[nm][kind][None]
        outs += [leaves[nm] for nm in order]
    return tuple(outs)
```

```python
import functools

import jax
import jax.numpy as jnp
from jax import lax
from jax.experimental import pallas as pl
from jax.experimental.pallas import tpu as pltpu

F32 = jnp.float32
BF16 = jnp.bfloat16
HIGHEST = lax.Precision.HIGHEST
MESH = pl.DeviceIdType.MESH

D = 1024
N_META = 16
HEADS = 4
DH = 128
GW = HEADS * DH
SW = D - GW
FF = 2816
CH = 64
KQ = 4
KS = 3
EPS = 1e-6
IN_W = 3 * GW + GW + 2 * HEADS + 3 * SW
Q0, K0, V0, Z0, BA0, SX0, SB0, SC0, PW = 0, 512, 1024, 1536, 2048, 2304, 2816, 3328, 3840
BA_W = SX0 - BA0
N_DEV = 8
VMEM_LIMIT = 56 * 1024 * 1024

ADAM_LR, ADAM_B1, ADAM_B2, ADAM_EPS, ADAM_WD, ADAM_STEP = 0.001, 0.9, 0.999, 1e-08, 0.01, 10


def _cparams(sem):
    return pltpu.CompilerParams(dimension_semantics=sem, vmem_limit_bytes=VMEM_LIMIT)


def _div_tile(n, target, mult=16):
    best = None
    for t in range(mult, min(n, target) + 1, mult):
        if n % t == 0:
            best = t
    assert best is not None, (n, target)
    return best


def _silu(x):
    return x * jax.nn.sigmoid(x)


def _dsilu(x):
    s = jax.nn.sigmoid(x)
    return s * (1.0 + x * (1.0 - s))


def _bdot(a, b, dims):
    return lax.dot_general(a.astype(BF16), b.astype(BF16), (dims, ((), ())), preferred_element_type=F32)


NN = ((1,), (0,))
NT = ((1,), (1,))
TN = ((0,), (0,))


def _fdot(a, b, dims=NN):
    return lax.dot_general(a, b, (dims, ((), ())), precision=HIGHEST, preferred_element_type=F32)


def _mm(name, pairs, mode, tm, tn, tk, out_dtypes=(F32,), epilogue=None, extras=(), after=None):
    a0, b0 = pairs[0]
    if mode == "nn":
        (m, k), n = a0.shape, b0.shape[1]
    elif mode == "nt":
        (m, k), n = a0.shape, b0.shape[0]
    else:
        (k, m), n = a0.shape, b0.shape[1]
    assert m % tm == 0 and n % tn == 0 and k % tk == 0, (name, m, n, k, tm, tn, tk)
    nk = k // tk
    npair = len(pairs)
    nex = len(extras)
    nout = len(out_dtypes)
    ntok = 0 if after is None else 1
    dims = {"nn": NN, "nt": NT, "tn": TN}[mode]

    def body(*refs):
        ab = refs[:2 * npair]
        ex = refs[2 * npair:2 * npair + nex]
        outs = refs[2 * npair + nex + ntok:2 * npair + nex + ntok + nout]
        acc_ref = refs[-1]
        kk = pl.program_id(2)

        part = None
        for p in range(npair):
            d = lax.dot_general(ab[2 * p][...], ab[2 * p + 1][...], (dims, ((), ())), preferred_element_type=F32)
            part = d if part is None else part + d

        def finish(acc):
            res = epilogue(acc, *[e[...] for e in ex]) if epilogue is not None else (acc,)
            for o, r in zip(outs, res):
                o[...] = r.astype(o.dtype)

        if nk == 1:
            finish(part)
        else:
            @pl.when(kk == 0)
            def _():
                acc_ref[...] = part

            @pl.when(kk > 0)
            def _():
                acc_ref[...] += part

            @pl.when(kk == nk - 1)
            def _():
                finish(acc_ref[...])

    if mode == "nn":
        a_spec = pl.BlockSpec((tm, tk), lambda i, j, q: (i, q))
        b_spec = pl.BlockSpec((tk, tn), lambda i, j, q: (q, j))
    elif mode == "nt":
        a_spec = pl.BlockSpec((tm, tk), lambda i, j, q: (i, q))
        b_spec = pl.BlockSpec((tn, tk), lambda i, j, q: (j, q))
    else:
        a_spec = pl.BlockSpec((tk, tm), lambda i, j, q: (q, i))
        b_spec = pl.BlockSpec((tk, tn), lambda i, j, q: (q, j))
    o_spec = pl.BlockSpec((tm, tn), lambda i, j, q: (i, j))
    flat = [t for pr in pairs for t in pr]
    res = pl.pallas_call(
        body, name=name,
        grid=(m // tm, n // tn, nk),
        in_specs=[a_spec, b_spec] * npair + [o_spec] * nex + [pl.BlockSpec((8, DH), lambda i, j, q: (0, 0))] * ntok,
        out_specs=[o_spec] * nout,
        out_shape=[jax.ShapeDtypeStruct((m, n), dt) for dt in out_dtypes],
        scratch_shapes=[pltpu.VMEM((tm, tn) if nk > 1 else (8, DH), F32)],
        compiler_params=_cparams(("parallel", "parallel", "arbitrary")),
    )(*flat, *extras, *([] if after is None else [after]))
    return res


def _ffn_up(u2, wg, wu, tm, tn):
    m, k = u2.shape
    n = wg.shape[0]

    def body(a_ref, g_ref, u_ref, gate_ref, up_ref, act_ref):
        a = a_ref[...]
        g = lax.dot_general(a, g_ref[...], (NT, ((), ())), preferred_element_type=F32)
        u = lax.dot_general(a, u_ref[...], (NT, ((), ())), preferred_element_type=F32)
        gate_ref[...] = g.astype(BF16)
        up_ref[...] = u.astype(BF16)
        act_ref[...] = (_silu(g) * u).astype(BF16)

    o_spec = pl.BlockSpec((tm, tn), lambda j, i: (i, j))
    w_spec = pl.BlockSpec((tn, k), lambda j, i: (j, 0))
    return pl.pallas_call(
        body, name="ffn_up", grid=(n // tn, m // tm),
        in_specs=[pl.BlockSpec((tm, k), lambda j, i: (i, 0)), w_spec, w_spec],
        out_specs=[o_spec, o_spec, o_spec],
        out_shape=[jax.ShapeDtypeStruct((m, n), BF16)] * 3,
        compiler_params=_cparams(("parallel", "parallel")),
    )(u2, wg, wu)


def _rms(x, gain):
    r = lax.rsqrt(jnp.mean(x * x, axis=-1, keepdims=True) + EPS)
    return x * r * gain, r


def _rms_bwd(x, gain, dy):
    r = lax.rsqrt(jnp.mean(x * x, axis=-1, keepdims=True) + EPS)
    dyw = dy * gain
    dx = r * dyw - x * (r * r * r) * jnp.mean(dyw * x, axis=-1, keepdims=True)
    return dx, dy * x * r


def _row_specs(tr, width, nb):
    return pl.BlockSpec((tr, width), lambda b, j: (b * nb + j, 0))


def _vec_spec(width):
    return pl.BlockSpec((1, width), lambda b, j: (0, 0))


def _first_step(b, j):
    return jnp.logical_and(b == 0, j == 0)


def _pre_norm(x, head_rows, gain, bsz, lp, tr):
    nb = lp // tr
    first_real = head_rows.shape[0]
    assert first_real % 8 == 0 and first_real < tr
    x_spec = pl.BlockSpec((pl.Squeezed(), pl.Element(tr), pl.Element(D)),
                          lambda b, j: (b, pl.multiple_of(jnp.maximum(j * tr - first_real, 0), 8), 0))

    def body(x_ref, hd_ref, g_ref, h_ref, u_ref):
        j = pl.program_id(1)
        xt = x_ref[...]
        h = jnp.where(j == 0, jnp.concatenate([hd_ref[...], xt[:tr - first_real]], axis=0), xt)
        h_ref[...] = h
        u_ref[...] = _rms(h, g_ref[...])[0].astype(BF16)

    rs = _row_specs(tr, D, nb)
    return pl.pallas_call(
        body, name="mix_pre_norm", grid=(bsz, nb),
        in_specs=[x_spec, pl.BlockSpec((first_real, D), lambda b, j: (0, 0)), _vec_spec(D)],
        out_specs=[rs, rs],
        out_shape=[jax.ShapeDtypeStruct((bsz * lp, D), F32), jax.ShapeDtypeStruct((bsz * lp, D), BF16)],
        compiler_params=_cparams(("parallel", "parallel")),
    )(x, head_rows, gain)


def _mid_norms(h0, mix, g_post, g_pre, bsz, lp, tr):
    nb = lp // tr

    def body(h_ref, m_ref, gp_ref, gq_ref, h1_ref, u2_ref):
        h1 = h_ref[...] + _rms(m_ref[...], gp_ref[...])[0]
        h1_ref[...] = h1
        u2_ref[...] = _rms(h1, gq_ref[...])[0].astype(BF16)

    rs = _row_specs(tr, D, nb)
    return pl.pallas_call(
        body, name="mid_norms", grid=(bsz, nb),
        in_specs=[rs, rs, _vec_spec(D), _vec_spec(D)],
        out_specs=[rs, rs],
        out_shape=[jax.ShapeDtypeStruct(h0.shape, F32), jax.ShapeDtypeStruct(h0.shape, BF16)],
        compiler_params=_cparams(("parallel", "parallel")),
    )(h0, mix, g_post, g_pre)


def _loss_head(h1, ffn, tgt, g_post, bsz, lp, tr, first_real):
    nb = lp // tr
    assert first_real % 8 == 0 and first_real < tr
    tgt_spec = pl.BlockSpec((pl.Squeezed(), pl.Element(tr), pl.Element(D)),
                            lambda b, j: (b, pl.multiple_of(jnp.maximum(j * tr - first_real, 0), 8), 0))

    def body(h_ref, f_ref, t_ref, g_ref, dy_ref, dffn_ref, dg_ref, loss_ref):
        b, j = pl.program_id(0), pl.program_id(1)
        f = f_ref[...]
        g = g_ref[...]
        y = h_ref[...] + _rms(f, g)[0]
        t = j * tr + lax.broadcasted_iota(jnp.int32, (tr, 1), 0)
        tg = t_ref[...]
        tg = jnp.where(j == 0, jnp.concatenate([tg[tr - first_real:], tg[:tr - first_real]], axis=0), tg)
        err = jnp.where(t >= first_real, y - tg, 0.0)
        dy = err * (1.0 / D)
        dy_ref[...] = dy
        dx, dgc = _rms_bwd(f, g, dy)
        dffn_ref[...] = dx.astype(BF16)
        lsum = jnp.sum(jnp.sum(err * err, axis=0, keepdims=True), axis=1, keepdims=True) * (0.5 / D)

        @pl.when(_first_step(b, j))
        def _():
            dg_ref[...] = jnp.zeros_like(dg_ref)
            loss_ref[...] = jnp.zeros_like(loss_ref)

        dg_ref[...] += jnp.sum(dgc, axis=0, keepdims=True)
        loss_ref[...] += jnp.broadcast_to(lsum, loss_ref.shape)

    rs = _row_specs(tr, D, nb)
    return pl.pallas_call(
        body, name="loss_head", grid=(bsz, nb),
        in_specs=[rs, rs, tgt_spec, _vec_spec(D)],
        out_specs=[rs, rs, _vec_spec(D), _vec_spec(DH)],
        out_shape=[jax.ShapeDtypeStruct(h1.shape, F32), jax.ShapeDtypeStruct(h1.shape, BF16),
                   jax.ShapeDtypeStruct((1, D), F32), jax.ShapeDtypeStruct((1, DH), F32)],
        compiler_params=_cparams(("arbitrary", "arbitrary")),
    )(h1, ffn, tgt, g_post)


def _mid_norms_bwd(h1, mix, du2, dy, g_pre, g_post, bsz, lp, tr):
    nb = lp // tr

    def body(h_ref, m_ref, du_ref, dy_ref, gq_ref, gp_ref, dh1_ref, dmix_ref, dgq_ref, dgp_ref):
        b, j = pl.program_id(0), pl.program_id(1)
        dx, dgq = _rms_bwd(h_ref[...], gq_ref[...], du_ref[...])
        dh1 = dy_ref[...] + dx
        dh1_ref[...] = dh1
        dm, dgp = _rms_bwd(m_ref[...], gp_ref[...], dh1)
        dmix_ref[...] = dm.astype(BF16)

        @pl.when(_first_step(b, j))
        def _():
            dgq_ref[...] = jnp.zeros_like(dgq_ref)
            dgp_ref[...] = jnp.zeros_like(dgp_ref)

        dgq_ref[...] += jnp.sum(dgq, axis=0, keepdims=True)
        dgp_ref[...] += jnp.sum(dgp, axis=0, keepdims=True)

    rs = _row_specs(tr, D, nb)
    return pl.pallas_call(
        body, name="mid_norms_bwd", grid=(bsz, nb),
        in_specs=[rs, rs, rs, rs, _vec_spec(D), _vec_spec(D)],
        out_specs=[rs, rs, _vec_spec(D), _vec_spec(D)],
        out_shape=[jax.ShapeDtypeStruct(h1.shape, F32), jax.ShapeDtypeStruct(h1.shape, BF16),
                   jax.ShapeDtypeStruct((1, D), F32), jax.ShapeDtypeStruct((1, D), F32)],
        compiler_params=_cparams(("arbitrary", "arbitrary")),
    )(h1, mix, du2, dy, g_pre, g_post)


def _pre_norm_bwd(h0, du1, dh1, gain, bsz, lp, tr, padf):
    nb = lp // tr
    assert padf % 8 == 0 and padf + N_META <= tr

    def body(h_ref, du_ref, dh1_ref, g_ref, dh0_ref, dg_ref, dmeta_ref):
        b, j = pl.program_id(0), pl.program_id(1)
        dx, dgc = _rms_bwd(h_ref[...], g_ref[...], du_ref[...])
        dh0 = dh1_ref[...] + dx
        dh0_ref[...] = dh0

        @pl.when(_first_step(b, j))
        def _():
            dg_ref[...] = jnp.zeros_like(dg_ref)
            dmeta_ref[...] = jnp.zeros_like(dmeta_ref)

        dg_ref[...] += jnp.sum(dgc, axis=0, keepdims=True)

        @pl.when(j == 0)
        def _():
            dmeta_ref[...] += dh0[padf:padf + N_META, :]

    rs = _row_specs(tr, D, nb)
    return pl.pallas_call(
        body, name="pre_norm_bwd", grid=(bsz, nb),
        in_specs=[rs, rs, rs, _vec_spec(D)],
        out_specs=[rs, _vec_spec(D), pl.BlockSpec((N_META, D), lambda b, j: (0, 0))],
        out_shape=[jax.ShapeDtypeStruct(h0.shape, F32), jax.ShapeDtypeStruct((1, D), F32),
                   jax.ShapeDtypeStruct((N_META, D), F32)],
        compiler_params=_cparams(("arbitrary", "arbitrary")),
    )(h0, du1, dh1, gain)


def _shift_down(x, s, lp):
    return x if s == 0 else pltpu.roll(x, s, axis=0)


def _shift_up(x, s, lp):
    return x if s == 0 else pltpu.roll(x, lp - s, axis=0)


def _conv_fwd(x, w, taps, lp):
    y = None
    for i in range(taps):
        term = _shift_down(x, taps - 1 - i, lp) * w[i:i + 1, :]
        y = term if y is None else y + term
    return y


def _conv_bwd(x, w, dy, taps, lp):
    dx = None
    dws = []
    for i in range(taps):
        s = taps - 1 - i
        term = _shift_up(dy, s, lp) * w[i:i + 1, :]
        dx = term if dx is None else dx + term
        dws.append(jnp.sum(dy * _shift_down(x, s, lp), axis=0, keepdims=True))
    return dx, jnp.concatenate(dws, axis=0)


def _slab(lp, col0):
    return pl.BlockSpec((lp, DH), lambda b, s: (b, col0 // DH + s))


def _qkv_act(c, s):
    y = _silu(c)
    r = lax.rsqrt(jnp.sum(y * y, axis=-1, keepdims=True) + EPS)
    scale = jnp.where(s < HEADS, DH ** -0.5, 1.0)
    return y, r, scale


def _gdn_prep(proj, conv_qkv, bsz, lp, padf):
    def body(p_ref, w_ref, o_ref):
        s = pl.program_id(1)
        c = _conv_fwd(p_ref[...].astype(F32), w_ref[...], KQ, lp)
        y, r, scale = _qkv_act(c, s)
        out = jnp.where(s < 2 * HEADS, y * (r * scale), y)
        t = lax.broadcasted_iota(jnp.int32, (lp, 1), 0)
        o_ref[...] = jnp.where(t >= padf, out, 0.0)

    return pl.pallas_call(
        body, name="gdn_prep", grid=(bsz, 3 * HEADS),
        in_specs=[_slab(lp, Q0), pl.BlockSpec((KQ, DH), lambda b, s: (0, s))],
        out_specs=_slab(lp, 0),
        out_shape=jax.ShapeDtypeStruct((bsz * lp, 3 * GW), F32),
        compiler_params=_cparams(("parallel", "parallel")),
    )(proj, conv_qkv)


def _gdn_prep_bwd(proj, conv_qkv, dq, dk, dv, bsz, lp, padf):
    def body(p_ref, w_ref, dq_ref, dk_ref, dv_ref, dp_ref, dw_ref):
        b, s = pl.program_id(0), pl.program_id(1)
        x = p_ref[...].astype(F32)
        w = w_ref[...]
        c = _conv_fwd(x, w, KQ, lp)
        y, r, scale = _qkv_act(c, s)
        d = jnp.where(s < HEADS, dq_ref[...], jnp.where(s < 2 * HEADS, dk_ref[...], dv_ref[...]))
        dn = d * scale
        n = y * r
        dy_norm = r * (dn - n * jnp.sum(dn * n, axis=-1, keepdims=True))
        dyy = jnp.where(s < 2 * HEADS, dy_norm, dn)
        t = lax.broadcasted_iota(jnp.int32, (lp, 1), 0)
        dc = jnp.where(t >= padf, dyy * _dsilu(c), 0.0)
        dx, dw = _conv_bwd(x, w, dc, KQ, lp)
        dp_ref[...] = jnp.where(t >= padf, dx, 0.0).astype(BF16)

        @pl.when(jnp.logical_and(b == 0, s == 0))
        def _():
            dw_ref[...] = jnp.zeros_like(dw_ref)

        col_block = jnp.right_shift(lax.broadcasted_iota(jnp.int32, (KQ, 3 * GW), 1), DH.bit_length() - 1)
        dw_ref[...] += jnp.where(col_block == s, jnp.tile(dw, (1, 3 * HEADS)), 0.0)

    sl = lambda col0: pl.BlockSpec((lp, DH), lambda b, s: (b, col0 // DH + s))
    part = lambda k: pl.BlockSpec((lp, DH), lambda b, s: (b, jnp.clip(s - k * HEADS, 0, HEADS - 1)))
    return pl.pallas_call(
        body, name="gdn_prep_bwd", grid=(bsz, 3 * HEADS),
        in_specs=[sl(Q0), pl.BlockSpec((KQ, DH), lambda b, s: (0, s)), part(0), part(1), part(2)],
        out_specs=[sl(0), pl.BlockSpec((KQ, 3 * GW), lambda b, s: (0, 0))],
        out_shape=[jax.ShapeDtypeStruct((bsz * lp, 3 * GW), BF16), jax.ShapeDtypeStruct((KQ, 3 * GW), F32)],
        compiler_params=_cparams(("arbitrary", "arbitrary")),
    )(proj, conv_qkv, dq, dk, dv)


def _softplus(x):
    e = jnp.exp(-jnp.abs(x))
    one_e = 1.0 + e
    l1p = jnp.where(one_e == 1.0, e, jnp.log(one_e) * (e / (one_e - 1.0)))
    return jnp.maximum(x, 0.0) + l1p


def _tri(n, kind):
    i = lax.broadcasted_iota(jnp.int32, (n, n), 0)
    j = lax.broadcasted_iota(jnp.int32, (n, n), 1)
    return {"incl": i >= j, "strict": i > j, "upper": i <= j}[kind]


def _gates(proj, alog_v, dtb_v, bsz, lp, padf):
    nchunk = lp // CH

    def body(p_ref, al_ref, dt_ref, gb_ref, cs_ref):
        x = p_ref[...].astype(F32)
        lane = lax.broadcasted_iota(jnp.int32, (lp, DH), 1)
        t = lax.broadcasted_iota(jnp.int32, (lp, DH), 0)
        beta = jax.nn.sigmoid(x)
        g = -jnp.exp(al_ref[...]) * _softplus(x + dt_ref[...])
        gb = jnp.where(lane < HEADS, beta, jnp.where(lane < 2 * HEADS, g, 0.0))
        gb_ref[...] = jnp.where(t >= padf, gb, 0.0)
        ltri = _tri(CH, "incl").astype(F32)

        def step(n, carry):
            r0 = pl.multiple_of(n * CH, CH)
            cs_ref[pl.ds(r0, CH), :] = _fdot(ltri, gb_ref[pl.ds(r0, CH), :])
            return carry

        lax.fori_loop(0, nchunk, step, 0)

    blk = pl.BlockSpec((lp, DH), lambda b: (b, 0))
    vec = pl.BlockSpec((1, DH), lambda b: (0, 0))
    out = pl.BlockSpec((lp, DH), lambda b: (b, 0))
    return pl.pallas_call(
        body, name="gates", grid=(bsz,),
        in_specs=[blk, vec, vec], out_specs=[out, out],
        out_shape=[jax.ShapeDtypeStruct((bsz * lp, DH), F32)] * 2,
        compiler_params=_cparams(("parallel",)),
    )(proj, alog_v, dtb_v)


def _gates_bwd(proj, alog_v, dtb_v, dbeta_b, dg_b, gb, bsz, lp, padf):
    def body(p_ref, al_ref, dt_ref, db_ref, dg_ref, gb_ref, dl_ref, dv_ref):
        b = pl.program_id(0)
        x = p_ref[...].astype(F32)
        lane = lax.broadcasted_iota(jnp.int32, (lp, DH), 1)
        t = lax.broadcasted_iota(jnp.int32, (lp, DH), 0)
        dbeta = jnp.zeros((lp, DH), F32)
        dg = jnp.zeros((lp, DH), F32)
        for h in range(HEADS):
            dbeta = jnp.where(lane == h, db_ref[h], dbeta)
            dg = jnp.where(lane == HEADS + h, dg_ref[h], dg)
        real = t >= padf
        dbeta = jnp.where(real, dbeta, 0.0)
        dg = jnp.where(real, dg, 0.0)
        sb = jax.nn.sigmoid(x)
        neg_ea = -jnp.exp(al_ref[...])
        dsp = dg * neg_ea * jax.nn.sigmoid(x + dt_ref[...])
        dl = dbeta * sb * (1.0 - sb) + dsp
        dl_ref[...] = jnp.concatenate([dl, jnp.zeros((lp, BA_W - DH), F32)], axis=1).astype(BF16)
        dalog = jnp.sum(dg * gb_ref[...], axis=0, keepdims=True)
        ddt = jnp.sum(dsp, axis=0, keepdims=True)

        @pl.when(b == 0)
        def _():
            dv_ref[...] = jnp.zeros_like(dv_ref)

        dv_ref[...] += jnp.concatenate([dalog, ddt], axis=0)

    blk = pl.BlockSpec((lp, DH), lambda b: (b, 0))
    vec = pl.BlockSpec((1, DH), lambda b: (0, 0))
    hb = pl.BlockSpec((HEADS, lp, DH), lambda b: (b, 0, 0))
    return pl.pallas_call(
        body, name="gates_bwd", grid=(bsz,),
        in_specs=[blk, vec, vec, hb, hb, pl.BlockSpec((lp, DH), lambda b: (b, 0))],
        out_specs=[pl.BlockSpec((lp, BA_W), lambda b: (b, 0)), pl.BlockSpec((2, DH), lambda b: (0, 0))],
        out_shape=[jax.ShapeDtypeStruct((bsz * lp, BA_W), BF16), jax.ShapeDtypeStruct((2, DH), F32)],
        compiler_params=_cparams(("arbitrary",)),
    )(proj, alog_v, dtb_v, dbeta_b, dg_b, gb)


def _mix_heads(proj, o, gdn_norm, conv_sc, bsz, lp, padf):
    def body(o_ref, z_ref, gn_ref, sx_ref, sb_ref, sc_ref, w_ref, og_ref, os_ref):
        og_ref[...] = (_rms(o_ref[...], gn_ref[...])[0] * _silu(z_ref[...].astype(F32))).astype(BF16)
        conv = _conv_fwd(sc_ref[...].astype(F32) * sx_ref[...].astype(F32), w_ref[...], KS, lp)
        t = lax.broadcasted_iota(jnp.int32, (lp, 1), 0)
        os_ref[...] = jnp.where(t >= padf, sb_ref[...].astype(F32) * conv, 0.0).astype(BF16)

    half = _slab(lp, 0)
    return pl.pallas_call(
        body, name="mix_heads", grid=(bsz, HEADS),
        in_specs=[half, _slab(lp, Z0), pl.BlockSpec((1, DH), lambda b, s: (0, 0)),
                  _slab(lp, SX0), _slab(lp, SB0), _slab(lp, SC0), pl.BlockSpec((KS, DH), lambda b, s: (0, s))],
        out_specs=[half, half],
        out_shape=[jax.ShapeDtypeStruct((bsz * lp, GW), BF16)] * 2,
        compiler_params=_cparams(("parallel", "parallel")),
    )(o, proj, gdn_norm, proj, proj, proj, conv_sc)


def _mix_heads_bwd(proj, o, gdn_norm, conv_sc, dmixin, bsz, lp, padf):
    def body(o_ref, z_ref, gn_ref, sx_ref, sb_ref, sc_ref, w_ref, dg_ref, ds_ref,
             do_ref, dz_ref, dsx_ref, dsb_ref, dsc_ref, dgn_ref, dw_ref):
        s, b = pl.program_id(0), pl.program_id(1)
        t = lax.broadcasted_iota(jnp.int32, (lp, 1), 0)
        real = t >= padf
        o, z, gn, d = o_ref[...], z_ref[...].astype(F32), gn_ref[...], dg_ref[...]
        sz = _silu(z)
        on = _rms(o, gn)[0]
        dz_ref[...] = (d * on * _dsilu(z)).astype(BF16)
        dox, dgn = _rms_bwd(o, gn, d * sz)
        do_ref[...] = dox
        sx, sb, sc = sx_ref[...].astype(F32), sb_ref[...].astype(F32), sc_ref[...].astype(F32)
        w, e = w_ref[...], ds_ref[...]
        e = jnp.where(real, e, 0.0)
        xin = sc * sx
        conv = _conv_fwd(xin, w, KS, lp)
        dsb_ref[...] = (e * conv).astype(BF16)
        dxin, dw = _conv_bwd(xin, w, e * sb, KS, lp)
        dxin = jnp.where(real, dxin, 0.0)
        dsx_ref[...] = (dxin * sc).astype(BF16)
        dsc_ref[...] = (dxin * sx).astype(BF16)

        @pl.when(jnp.logical_and(s == 0, b == 0))
        def _():
            dgn_ref[...] = jnp.zeros_like(dgn_ref)

        @pl.when(b == 0)
        def _():
            dw_ref[...] = jnp.zeros_like(dw_ref)

        dgn_ref[...] += jnp.sum(dgn, axis=0, keepdims=True)
        dw_ref[...] += dw

    sl = lambda col0: pl.BlockSpec((lp, DH), lambda s, b: (b, col0 // DH + s))
    half = sl(0)
    return pl.pallas_call(
        body, name="mix_heads_bwd", grid=(HEADS, bsz),
        in_specs=[half, sl(Z0), pl.BlockSpec((1, DH), lambda s, b: (0, 0)), sl(SX0), sl(SB0), sl(SC0),
                  pl.BlockSpec((KS, DH), lambda s, b: (0, s)), sl(0), sl(GW)],
        out_specs=[half] * 5 + [pl.BlockSpec((1, DH), lambda s, b: (0, 0)), pl.BlockSpec((KS, DH), lambda s, b: (0, s))],
        out_shape=[jax.ShapeDtypeStruct((bsz * lp, GW), F32)] + [jax.ShapeDtypeStruct((bsz * lp, GW), BF16)] * 4
        + [jax.ShapeDtypeStruct((1, DH), F32), jax.ShapeDtypeStruct((KS, SW), F32)],
        compiler_params=_cparams(("arbitrary", "arbitrary")),
    )(o, proj, gdn_norm, proj, proj, proj, conv_sc, dmixin, dmixin)


def _split_dot(a, b, dims=NN):
    ah = a.astype(BF16)
    al = (a - ah.astype(F32)).astype(BF16)
    bh = b.astype(BF16)
    bl = (b - bh.astype(F32)).astype(BF16)

    def d(x, y):
        return lax.dot_general(x, y, (dims, ((), ())), preferred_element_type=F32)

    return d(ah, bh) + (d(ah, bl) + d(al, bh))


def _eye(n):
    return (lax.broadcasted_iota(jnp.int32, (n, n), 0) == lax.broadcasted_iota(jnp.int32, (n, n), 1)).astype(F32)


def _each(f, *lists):
    return [f(*xs) for xs in zip(*lists)]


def _inv_unit_lower(a_list):
    eye = _eye(CH)
    p = _each(lambda a: -a, a_list)
    t = _each(lambda x: eye + x, p)
    p = _each(lambda x: _split_dot(x, x), p)
    for level in range(5):
        if level < 4:
            prod = _each(lambda tt, pp: _split_dot(jnp.concatenate([tt, pp], axis=0), pp), t, p)
            t = _each(lambda tt, pr: tt + pr[:CH], t, prod)
            p = _each(lambda pr: pr[CH:], prod)
        else:
            t = _each(lambda tt, pp: tt + _split_dot(tt, pp), t, p)
    return t


def _chunk_gates(gc_ref, be_ref, gt_ref, g):
    rows = pl.ds(g * CH, CH)
    gc = gc_ref[0, rows, :]
    be = be_ref[0, rows, :]
    gcl = gc_ref[0, pl.ds(g * CH + CH - 1, 1), :]
    gt = gt_ref[0, g]
    incl, upper = _tri(CH, "incl"), _tri(CH, "upper")
    dec = jnp.where(incl, jnp.exp(jnp.where(incl, gc[:, :CH] - gt, 0.0)), 0.0)
    dec_t = jnp.where(upper, jnp.exp(jnp.where(upper, gt - gc[:, :CH], 0.0)), 0.0)
    return rows, be, dec, dec_t, jnp.exp(gc), jnp.exp(gcl - gc), jnp.exp(gcl)


def _chunk_group(nchunk):
    return 3 if nchunk % 3 == 0 else 1


def _local_group(nchunk):
    return 11 if nchunk % 11 == 0 else _chunk_group(nchunk)


def _gdn_local_specs(lp, grp):
    ngrp = lp // (grp * CH)
    head = lambda col0: pl.BlockSpec((grp * CH, DH), lambda i, j: ((i // HEADS) * ngrp + j, col0 // DH + i % HEADS))
    lane_b = pl.BlockSpec((1, grp * CH, DH), lambda i, j: (i, j, 0))
    per_chunk = lambda r, c: pl.BlockSpec((1, grp, r, c), lambda i, j: (i, j, 0, 0))
    return ngrp, head, lane_b, per_chunk


def _gdn_local_fwd(qkv, gcb, bb, gct, bsz, lp):
    nchunk = lp // CH
    grp = _local_group(nchunk)
    ngrp, head, lane_b, per_chunk = _gdn_local_specs(lp, grp)
    bh = bsz * HEADS

    def body(q_ref, k_ref, v_ref, gc_ref, be_ref, gt_ref,
             u_ref, l1_ref, l2_ref, l3_ref, wt_ref, kd_ref, ti_ref, gl_ref):
        incl, strict, upper = _tri(CH, "incl"), _tri(CH, "strict"), _tri(CH, "upper")
        eye = _eye(CH)
        rows, be, dec, dec_t, eg, ekd, gl = zip(*[_chunk_gates(gc_ref, be_ref, gt_ref, g) for g in range(grp)])
        q = [q_ref[r, :] for r in rows]
        k = [k_ref[r, :] for r in rows]
        v = [v_ref[r, :] for r in rows]
        kb = _each(lambda x, y: x * y, k, be)
        a = _each(lambda x, y, d: jnp.where(strict, _bdot(x, y, NT) * d, 0.0), kb, k, dec)
        tinv = _inv_unit_lower(a)
        u = _each(lambda t, x, y: _bdot(t, x * y, NN), tinv, v, be)
        w = _each(lambda t, x, y: _bdot(t, x * y, NN), tinv, kb, eg)
        qk = _each(lambda x, y, d: jnp.where(incl, _bdot(x, y, NT) * d, 0.0), q, k, dec)
        qk_t = _each(lambda x, y, d: jnp.where(upper, _bdot(y, x, NT) * d, 0.0), q, k, dec_t)
        qd = _each(lambda x, y: x * y, q, eg)
        kd = _each(lambda x, y: x * y, k, ekd)
        kd_t = _each(lambda x: _bdot(x, eye, TN), kd)
        qd_t = _each(lambda x: _bdot(x, eye, TN), qd)
        w_t = _each(lambda x: _bdot(x, eye, TN), w)
        for g in range(grp):
            u_ref[rows[g], :] = u[g]
            l1_ref[0, g] = jnp.concatenate([w[g], qd[g]], axis=0).astype(BF16)
            l2_ref[0, g] = jnp.concatenate([qk[g], kd_t[g]], axis=0).astype(BF16)
            l3_ref[0, g] = jnp.concatenate([qk_t[g], qd_t[g]], axis=0).astype(BF16)
            wt_ref[0, g] = w_t[g].astype(BF16)
            kd_ref[0, g] = kd[g].astype(BF16)
            ti_ref[0, g] = tinv[g]
            gl_ref[0, g] = gl[g]

    r = bsz * lp
    shapes = [((r, GW), F32), ((bh, nchunk, 2 * CH, DH), BF16), ((bh, nchunk, 3 * CH, CH), BF16),
              ((bh, nchunk, 3 * CH, CH), BF16), ((bh, nchunk, DH, CH), BF16), ((bh, nchunk, CH, DH), BF16),
              ((bh, nchunk, CH, CH), F32), ((bh, nchunk, 1, DH), F32)]
    return pl.pallas_call(
        body, name="gdn_local_fwd", grid=(bh, ngrp),
        in_specs=[head(0), head(GW), head(2 * GW), lane_b, lane_b, per_chunk(1, CH)],
        out_specs=[head(0), per_chunk(2 * CH, DH), per_chunk(3 * CH, CH), per_chunk(3 * CH, CH), per_chunk(DH, CH),
                   per_chunk(CH, DH), per_chunk(CH, CH), per_chunk(1, DH)],
        out_shape=[jax.ShapeDtypeStruct(s, dt) for s, dt in shapes],
        compiler_params=_cparams(("parallel", "parallel")),
    )(qkv, qkv, qkv, gcb, bb, gct)


def _scan_specs(bsz, lp, grp, order):
    ngrp = lp // (grp * CH)
    bh = bsz * HEADS
    seq = lambda: pl.BlockSpec((bsz, grp * CH, GW), lambda s: (0, order(s, ngrp), 0))
    per_chunk = lambda r, c: pl.BlockSpec((bh, grp, r, c), lambda s: (0, order(s, ngrp), 0, 0))
    return ngrp, bh, seq, per_chunk


def _gdn_scan_fwd(u, l1, l2, gl, bsz, lp):
    nchunk = lp // CH
    grp = _chunk_group(nchunk)
    ngrp, bh, seq, per_chunk = _scan_specs(bsz, lp, grp, lambda s, n: s)

    def body(u_ref, l1_ref, l2_ref, gl_ref, o_ref, vn_ref, st_ref, s_scr):
        @pl.when(pl.program_id(0) == 0)
        def _():
            s_scr[...] = jnp.zeros_like(s_scr)

        for c in range(grp):
            rows = slice(c * CH, (c + 1) * CH)
            for i in range(bh):
                b, h = divmod(i, HEADS)
                cols = slice(h * DH, (h + 1) * DH)
                s = s_scr[i]
                st_ref[i, c] = s
                x = jnp.dot(l1_ref[i, c], s.astype(BF16), preferred_element_type=F32)
                vnew = u_ref[b, rows, cols] - x[:CH]
                y = jnp.dot(l2_ref[i, c], vnew.astype(BF16), preferred_element_type=F32)
                o_ref[b, rows, cols] = x[CH:] + y[:CH]
                vn_ref[b, rows, cols] = vnew.astype(BF16)
                s_scr[i] = s * gl_ref[i, c] + y[CH:]

    return pl.pallas_call(
        body, name="gdn_scan_fwd", grid=(ngrp,),
        in_specs=[seq(), per_chunk(2 * CH, DH), per_chunk(3 * CH, CH), per_chunk(1, DH)],
        out_specs=[seq(), seq(), per_chunk(DH, DH)],
        out_shape=[jax.ShapeDtypeStruct((bsz, lp, GW), F32), jax.ShapeDtypeStruct((bsz, lp, GW), BF16),
                   jax.ShapeDtypeStruct((bh, nchunk, DH, DH), F32)],
        scratch_shapes=[pltpu.VMEM((bh, DH, DH), F32)],
        compiler_params=_cparams(("arbitrary",)),
    )(u.reshape(bsz, lp, GW), l1, l2, gl)


def _gdn_scan_bwd(d_o, l3, wt, kd, gl, bsz, lp):
    nchunk = lp // CH
    grp = _chunk_group(nchunk)
    ngrp, bh, seq, per_chunk = _scan_specs(bsz, lp, grp, lambda s, n: n - 1 - s)

    def body(do_ref, l3_ref, wt_ref, kd_ref, gl_ref, dvn_ref, dst_ref, ds_scr):
        @pl.when(pl.program_id(0) == 0)
        def _():
            ds_scr[...] = jnp.zeros_like(ds_scr)

        for c in reversed(range(grp)):
            rows = slice(c * CH, (c + 1) * CH)
            for i in range(bh):
                b, h = divmod(i, HEADS)
                cols = slice(h * DH, (h + 1) * DH)
                ds = ds_scr[i]
                dst_ref[i, c] = ds
                p = jnp.dot(l3_ref[i, c], do_ref[b, rows, cols].astype(BF16), preferred_element_type=F32)
                dvn = p[:CH] + jnp.dot(kd_ref[i, c], ds.astype(BF16), preferred_element_type=F32)
                dvn_ref[b, rows, cols] = dvn.astype(BF16)
                ds_scr[i] = ds * gl_ref[i, c] + p[CH:] - jnp.dot(wt_ref[i, c], dvn.astype(BF16),
                                                                 preferred_element_type=F32)

    return pl.pallas_call(
        body, name="gdn_scan_bwd", grid=(ngrp,),
        in_specs=[seq(), per_chunk(3 * CH, CH), per_chunk(DH, CH), per_chunk(CH, DH), per_chunk(1, DH)],
        out_specs=[seq(), per_chunk(DH, DH)],
        out_shape=[jax.ShapeDtypeStruct((bsz, lp, GW), BF16), jax.ShapeDtypeStruct((bh, nchunk, DH, DH), F32)],
        scratch_shapes=[pltpu.VMEM((bh, DH, DH), F32)],
        compiler_params=_cparams(("arbitrary",)),
    )(d_o.reshape(bsz, lp, GW), l3, wt, kd, gl)


def _gdn_local_bwd(qkv, gcb, bb, gct, tinv_all, states, dstates, vnew, dvnew, d_o, bsz, lp):
    nchunk = lp // CH
    grp = _local_group(nchunk)
    ngrp, head, lane_b, per_chunk = _gdn_local_specs(lp, grp)
    bh = bsz * HEADS

    def body(q_ref, k_ref, v_ref, gc_ref, be_ref, gt_ref, ti_ref, st_ref, dst_ref, vn_ref, dvn_ref, do_ref,
             dq_ref, dk_ref, dv_ref, dg_ref, db_ref):
        incl, strict = _tri(CH, "incl"), _tri(CH, "strict")
        upper = _tri(CH, "upper").astype(F32)
        ones = jnp.ones((CH, DH), F32)
        last = lax.broadcasted_iota(jnp.int32, (CH, 1), 0) == CH - 1

        def rsum(x):
            return jnp.sum(x, axis=-1, keepdims=True)

        def mul(xs, ys):
            return _each(lambda x, y: x * y, xs, ys)

        rows, be, dec, _, eg, ekd, gl = zip(*[_chunk_gates(gc_ref, be_ref, gt_ref, g) for g in range(grp)])
        q = [q_ref[r, :] for r in rows]
        k = [k_ref[r, :] for r in rows]
        v = [v_ref[r, :] for r in rows]
        vnew = [vn_ref[r, :] for r in rows]
        dvn = [dvn_ref[r, :] for r in rows]
        do = [do_ref[r, :] for r in rows]
        tinv = [ti_ref[0, g] for g in range(grp)]
        s = [st_ref[0, g] for g in range(grp)]
        ds = [dst_ref[0, g] for g in range(grp)]
        kb, vb, qd, kd = mul(k, be), mul(v, be), mul(q, eg), mul(k, ekd)
        kbg = mul(kb, eg)
        a = _each(lambda x, y, d: jnp.where(strict, _bdot(x, y, NT) * d, 0.0), kb, k, dec)
        qk = _each(lambda x, y, d: jnp.where(incl, _bdot(x, y, NT) * d, 0.0), q, k, dec)
        dqk = _each(lambda x, y: jnp.where(incl, _bdot(x, y, NT), 0.0), do, vnew)
        dqd = _each(lambda x, y: _bdot(x, y, NT), do, s)
        dkd = _each(lambda x, y: _bdot(x, y, NT), vnew, ds)
        dgl = _each(lambda x, y: jnp.sum(rsum(x * y), axis=0, keepdims=True), s, ds)
        dw = _each(lambda x, y: -_bdot(x, y, NT), dvn, s)
        dvb = _each(lambda t, x: _bdot(t, x, TN), tinv, dvn)
        dkbg = _each(lambda t, x: _bdot(t, x, TN), tinv, dw)
        dt = _each(lambda x, y, z, w: _bdot(x, y, NT) + _bdot(z, w, NT), dvn, vb, dw, kbg)
        tdt = _each(lambda t, x: _split_dot(t, x, TN), tinv, dt)
        da = _each(lambda x, t: jnp.where(strict, -_split_dot(x, t, NT), 0.0), tdt, tinv)
        dp, dr = mul(da, dec), mul(dqk, dec)
        dkb = _each(lambda x, y, z, e: _bdot(x, y, NN) + z * e, dp, k, dkbg, eg)
        dk = _each(lambda p_, kb_, r_, q_, dkd_, ekd_, dkb_, be_:
                   _bdot(p_, kb_, TN) + _bdot(r_, q_, TN) + dkd_ * ekd_ + dkb_ * be_,
                   dp, kb, dr, q, dkd, ekd, dkb, be)
        dq = _each(lambda r_, k_, dqd_, eg_: _bdot(r_, k_, NN) + dqd_ * eg_, dr, k, dqd, eg)
        m = _each(lambda da_, a_, dqk_, qk_: da_ * a_ + dqk_ * qk_, da, a, dqk, qk)
        m_cols = _each(lambda x: _split_dot(x, ones, TN), m)
        kd_term = _each(lambda x, y: rsum(x * y), dkd, kd)
        dgc = _each(lambda m_, mc, dqd_, qd_, kt, dkbg_, kbg_, dgl_, gl_:
                    rsum(m_) - mc + rsum(dqd_ * qd_) - kt + rsum(dkbg_ * kbg_)
                    + jnp.where(last, jnp.sum(kt, axis=0, keepdims=True) + dgl_ * gl_, 0.0),
                    m, m_cols, dqd, qd, kd_term, dkbg, kbg, dgl, gl)
        dg = _each(lambda x: _split_dot(upper, x, NN), dgc)
        for g in range(grp):
            dq_ref[rows[g], :] = dq[g]
            dk_ref[rows[g], :] = dk[g]
            dv_ref[rows[g], :] = dvb[g] * be[g]
            dg_ref[0, rows[g], :] = dg[g]
            db_ref[0, rows[g], :] = jnp.broadcast_to(rsum(dkb[g] * k[g]) + rsum(dvb[g] * v[g]), (CH, DH))

    r = bsz * lp
    return pl.pallas_call(
        body, name="gdn_local_bwd", grid=(bh, ngrp),
        in_specs=[head(0), head(GW), head(2 * GW), lane_b, lane_b, per_chunk(1, CH), per_chunk(CH, CH),
                  per_chunk(DH, DH), per_chunk(DH, DH), head(0), head(0), head(0)],
        out_specs=[head(0), head(0), head(0), lane_b, lane_b],
        out_shape=[jax.ShapeDtypeStruct((r, GW), F32)] * 3 + [jax.ShapeDtypeStruct((bh, lp, DH), F32)] * 2,
        compiler_params=_cparams(("parallel", "parallel")),
    )(qkv, qkv, qkv, gcb, bb, gct, tinv_all, states, dstates, vnew, dvnew, d_o)


def _place():
    return lax.axis_index("x"), lax.axis_index("y"), lax.axis_index("c")


def _any_specs(n):
    return [pl.BlockSpec(memory_space=pl.ANY)] * n


def _all_gather(name, shards):
    n = len(shards)

    def body(*refs):
        ins, outs = refs[:n], refs[n:2 * n]
        send_sems, recv_sems, local_sems = refs[2 * n:]
        x, y, c = _place()
        me, sibling = (x, y, c), (x, y, 1 - c)
        chips = [(1 - x, y), (x, 1 - y), (1 - x, 1 - y)]

        def slot(i, p):
            return outs[i].at[4 * p[0] + 2 * p[1] + p[2]]

        def copy(i, k, block, to, src=None):
            return pltpu.make_async_remote_copy(
                src_ref=slot(i, block) if src is None else src, dst_ref=slot(i, block),
                send_sem=send_sems.at[i, k], recv_sem=recv_sems.at[i, k], device_id=to, device_id_type=MESH)

        mine = [pltpu.make_async_copy(ins[i], slot(i, me), local_sems.at[i]) for i in range(n)]
        for cp in mine:
            cp.start()
        first = []
        for i in range(n):
            first.append(copy(i, 0, me, sibling, src=ins[i]))
            first += [copy(i, 1 + j, me, (*chip, c), src=ins[i]) for j, chip in enumerate(chips)]
        for cp in first:
            cp.start()
        passed = []
        for j, chip in enumerate(chips):
            for i in range(n):
                copy(i, 1 + j, (*chip, c), me).wait_recv()
                fwd = copy(i, 4 + j, (*chip, c), sibling)
                fwd.start()
                passed.append(fwd)
        for i in range(n):
            copy(i, 0, sibling, me).wait_recv()
            for j, chip in enumerate(chips):
                copy(i, 4 + j, (*chip, 1 - c), me).wait_recv()
        for cp in first + passed:
            cp.wait_send()
        for cp in mine:
            cp.wait()

    return pl.pallas_call(
        body, name=name,
        in_specs=_any_specs(n), out_specs=_any_specs(n),
        out_shape=[jax.ShapeDtypeStruct((N_DEV,) + s.shape, s.dtype) for s in shards],
        scratch_shapes=[pltpu.SemaphoreType.DMA((n, 7)), pltpu.SemaphoreType.DMA((n, 7)),
                        pltpu.SemaphoreType.DMA((n,))],
    )(*shards)


HBM_SPEC = pl.BlockSpec(memory_space=pltpu.HBM)
SEM_SPEC = pl.BlockSpec(memory_space=pltpu.SEMAPHORE)
EFFECT = pltpu.SideEffectType.DATAFLOW_SIDE_EFFECTING
N_PEER = N_DEV - 1


def _peer(r):
    x, y, c = _place()
    return ((1 - x) if r & 4 else x, (1 - y) if r & 2 else y, (1 - c) if r & 1 else c)


def _slot_of(p):
    return 4 * p[0] + 2 * p[1] + p[2]


def _hbm(a):
    return pltpu.with_memory_space_constraint(a, pltpu.HBM)


def _gather_pattern(src, land):
    me = _place()
    return [(src, land.at[_slot_of(me)], _peer(r), land.at[_slot_of(_peer(r))]) for r in range(1, N_DEV)]


def _chipcast_pattern(src, land):
    x, y, c = _place()
    me = (x, y, c)
    peers = [(x, y, 1 - c), (1 - x, y, c), (x, 1 - y, c), (1 - x, 1 - y, c)]
    return [(src, land.at[_slot_of(me)], p, land.at[_slot_of(p)]) for p in peers]


def _forward_pattern(land, _):
    x, y, c = _place()
    chips = [(1 - x, y), (x, 1 - y), (1 - x, 1 - y)]
    return [(land.at[_slot_of((*ch, c))], land.at[_slot_of((*ch, c))], (x, y, 1 - c), land.at[_slot_of((*ch, 1 - c))])
            for ch in chips]


def _sibling_pattern(src, land):
    x, y, c = _place()
    return [(src.at[ch, 1 - c], land.at[ch], (x, y, 1 - c), land.at[ch]) for ch in range(4)]


def _chip_pattern(src, land):
    x, y, c = _place()
    chips = [(1 - x, y), (x, 1 - y), (1 - x, 1 - y)]
    return [(src.at[2 * ch[0] + ch[1]], land.at[j], (*ch, c), land.at[j]) for j, ch in enumerate(chips)]


def _split_start(name, groups, pattern, land_shape, after=None):
    sizes = [len(g) for g in groups]
    arrays = [s for g in groups for s in g]
    n = len(arrays)
    inplace = land_shape is None
    fresh = [] if inplace else [lax.empty(land_shape(s.shape), s.dtype) for s in arrays]
    bufs = arrays + fresh
    order = [] if after is None else [after]

    def body(*refs):
        srcs, lands = refs[:n], (refs[:n] if inplace else refs[n:2 * n])
        first_out = len(bufs) + len(order)
        sems = refs[first_out:first_out + 2 * len(groups)]
        token = refs[-1]
        k = 0
        for gi, size in enumerate(sizes):
            for i in range(size):
                copies = pattern(srcs[k], lands[k])
                for j, (s, d, peer, _) in enumerate(copies):
                    pltpu.make_async_remote_copy(
                        src_ref=s, dst_ref=d, send_sem=sems[2 * gi].at[i * len(copies) + j],
                        recv_sem=sems[2 * gi + 1].at[i * len(copies) + j], device_id=peer, device_id_type=MESH).start()
                k += 1
        token[...] = jnp.zeros_like(token)

    per_array = {_gather_pattern: N_PEER, _chipcast_pattern: 4, _forward_pattern: 3, _sibling_pattern: 4,
                 _chip_pattern: 3}[pattern]
    sem_shapes = [pltpu.SemaphoreType.DMA((size * per_array,)) for size in sizes for _ in range(2)]
    res = pl.pallas_call(
        body, name=name,
        in_specs=[HBM_SPEC] * len(bufs) + [pl.BlockSpec(memory_space=pl.ANY)] * len(order),
        out_specs=[SEM_SPEC] * len(sem_shapes) + [HBM_SPEC] * len(bufs) + [pl.BlockSpec(memory_space=pltpu.VMEM)],
        out_shape=sem_shapes + [pltpu.HBM(s.shape, s.dtype) for s in bufs] + [jax.ShapeDtypeStruct((8, DH), F32)],
        input_output_aliases={i: len(sem_shapes) + i for i in range(len(bufs))},
        compiler_params=pltpu.CompilerParams(has_side_effects=EFFECT),
    )(*[_hbm(s) for s in bufs], *order)
    nsem = len(sem_shapes)
    thru = list(res[nsem:nsem + len(bufs)])
    out, k = [], 0
    for gi, size in enumerate(sizes):
        srcs_out = [] if inplace else thru[k:k + size]
        lands_out = thru[k:k + size] if inplace else thru[n + k:n + k + size]
        out.append((res[2 * gi], res[2 * gi + 1], srcs_out, lands_out))
        k += size
    return out, res[-1]


def _split_wait(name, group, pattern, after):
    send_sems, recv_sems, arrays, lands = group
    n = len(lands)
    inplace = not arrays
    bufs = list(arrays) + list(lands)
    fill_own = pattern in (_gather_pattern, _chipcast_pattern)

    def body(*refs):
        srcs, lands_ = refs[:n], (refs[:n] if inplace else refs[n:2 * n])
        s_sems, r_sems = refs[len(bufs)], refs[len(bufs) + 1]
        own = []
        if fill_own:
            local_sems = refs[-1]
            own = [pltpu.make_async_copy(srcs[i], lands_[i].at[_slot_of(_place())], local_sems.at[i])
                   for i in range(n)]
            for cp in own:
                cp.start()
        for i in range(n):
            copies = pattern(srcs[i], lands_[i])
            for j, (s, _, peer, filled) in enumerate(copies):
                cp = pltpu.make_async_remote_copy(
                    src_ref=s, dst_ref=filled, send_sem=s_sems.at[i * len(copies) + j],
                    recv_sem=r_sems.at[i * len(copies) + j], device_id=peer, device_id_type=MESH)
                cp.wait_send()
                cp.wait_recv()
        for cp in own:
            cp.wait()

    res = pl.pallas_call(
        body, name=name,
        in_specs=[HBM_SPEC] * len(bufs) + [SEM_SPEC, SEM_SPEC, pl.BlockSpec(memory_space=pl.ANY)],
        out_specs=[HBM_SPEC] * len(bufs),
        out_shape=[pltpu.HBM(s.shape, s.dtype) for s in bufs],
        scratch_shapes=[pltpu.SemaphoreType.DMA((n,))] if fill_own else [],
        input_output_aliases={i: i for i in range(len(bufs))},
        compiler_params=pltpu.CompilerParams(has_side_effects=EFFECT),
    )(*bufs, send_sems, recv_sems, after)
    return ([], list(res)) if inplace else (list(res[:n]), list(res[n:]))


def _pair_sum(name, grads, recv, core):
    _, _, r, c = grads.shape
    tr = _div_tile(r, 256, 16) if r % 16 == 0 else r

    def body(core_ref, g_ref, r_ref, o_ref):
        o_ref[...] = (g_ref[0].astype(F32) + r_ref[...].astype(F32)).astype(o_ref.dtype)

    return pl.pallas_call(
        body, name=name,
        grid_spec=pltpu.PrefetchScalarGridSpec(
            num_scalar_prefetch=1, grid=(4, r // tr),
            in_specs=[pl.BlockSpec((1, 1, tr, c), lambda s, i, cr: (s, cr[0], i, 0)),
                      pl.BlockSpec((1, tr, c), lambda s, i, cr: (s, i, 0))],
            out_specs=pl.BlockSpec((1, tr, c), lambda s, i, cr: (s, i, 0))),
        out_shape=jax.ShapeDtypeStruct(recv.shape, recv.dtype),
        compiler_params=_cparams(("parallel", "parallel")),
    )(core, grads, recv)


def _adamw_math(w, g, m, v):
    m = ADAM_B1 * m + (1.0 - ADAM_B1) * g
    v = ADAM_B2 * v + (1.0 - ADAM_B2) * (g * g)
    m_hat = m / (1.0 - ADAM_B1 ** ADAM_STEP)
    v_hat = v / (1.0 - ADAM_B2 ** ADAM_STEP)
    delta = -ADAM_LR * (m_hat / (jnp.sqrt(v_hat) + ADAM_EPS) + ADAM_WD * w)
    return delta, m, v


SMALL_LEAVES = (
    ("mix_pre_norm", 0, 1, 0, D, (1, D)), ("mix_post_norm", 1, 1, 0, D, (1, D)),
    ("ffn_pre_norm", 2, 1, 0, D, (1, D)), ("ffn_post_norm", 3, 1, 0, D, (1, D)),
    ("meta_tokens", 4, N_META, 0, DH, (N_META, DH)),
    ("conv_qkv", 20, KQ, 0, 3 * GW // N_DEV, (1, KQ, 3 * GW // N_DEV)),
    ("conv_sc", 24, KS, 0, SW // N_DEV, (1, KS, SW // N_DEV)),
    ("a_log", 27, 1, 0, HEADS, (1, HEADS)), ("dt_bias", 27, 1, DH, HEADS, (1, HEADS)),
    ("gdn_norm", 27, 1, 2 * DH, DH, (1, DH)),
)


def _adamw_small(slabs, w, m, v):
    nslot = slabs.shape[0]
    nleaf = len(SMALL_LEAVES)

    def body(p_ref, w_ref, m_ref, v_ref, *refs):
        outs, loss_ref, scr = refs[:4 * nleaf], refs[4 * nleaf], refs[4 * nleaf + 1]
        g = p_ref[0]
        for s in range(1, nslot):
            g = g + p_ref[s]
        d, mn, vn = _adamw_math(w_ref[...], g, m_ref[...], v_ref[...])
        for kind, val in enumerate((g, d, mn, vn)):
            scr[kind] = val
        for kind in range(4):
            for li, (_, r0, nr, c0, nc, shape) in enumerate(SMALL_LEAVES):
                val = scr[kind, pl.ds(r0, nr), pl.ds(c0, nc)]
                o = outs[kind * nleaf + li]
                if len(shape) == 3:
                    o[0] = val
                else:
                    o[...] = val
        loss_ref[...] = scr[0, pl.ds(27, 1), pl.ds(3 * DH, 1)]

    whole = lambda a: pl.BlockSpec(a.shape, lambda: (0,) * a.ndim)
    out_shapes = [jax.ShapeDtypeStruct(shape, F32) for _ in range(4) for (*_, shape) in SMALL_LEAVES]
    out_shapes.append(jax.ShapeDtypeStruct((1, 1), F32))
    res = pl.pallas_call(
        body, name="adamw_small",
        in_specs=[whole(slabs), whole(w), whole(m), whole(v)],
        out_specs=[pl.BlockSpec(s.shape, lambda n=len(s.shape): (0,) * n) for s in out_shapes],
        out_shape=out_shapes,
        scratch_shapes=[pltpu.VMEM((4,) + w.shape, F32)],
        compiler_params=pltpu.CompilerParams(vmem_limit_bytes=VMEM_LIMIT),
    )(slabs, w, m, v)
    kinds = [{nm: res[kind * nleaf + li] for li, (nm, *_) in enumerate(SMALL_LEAVES)} for kind in range(4)]
    return kinds, res[-1]


def _adamw_reduced(name, parts, chip, recv, w, m, v):
    _, r, c = parts.shape
    tr = _div_tile(r, 256, 16) if r % 16 == 0 else r

    def body(chip_ref, p_ref, r_ref, w_ref, m_ref, v_ref, g_out, d_out, m_out, v_out):
        g = p_ref[0].astype(F32)
        for s in range(3):
            g = g + r_ref[s].astype(F32)
        d, mn, vn = _adamw_math(w_ref[...], g, m_ref[...], v_ref[...])
        g_out[...] = g
        d_out[...] = d
        m_out[...] = mn
        v_out[...] = vn

    blk = pl.BlockSpec((tr, c), lambda i, ch: (i, 0))
    return pl.pallas_call(
        body, name=name,
        grid_spec=pltpu.PrefetchScalarGridSpec(
            num_scalar_prefetch=1, grid=(r // tr,),
            in_specs=[pl.BlockSpec((1, tr, c), lambda i, ch: (ch[0], i, 0)),
                      pl.BlockSpec((3, tr, c), lambda i, ch: (0, i, 0)), blk, blk, blk],
            out_specs=[blk] * 4),
        out_shape=[jax.ShapeDtypeStruct((r, c), F32)] * 4,
        compiler_params=_cparams(("parallel",)),
    )(chip, parts, recv, w, m, v)


SMALL_ROWS, SMALL_COLS = 32, 3 * GW


def _pack_small(gains4, meta, conv_qkv, conv_sc, a_log, dt_bias, gdn_norm, extra=None):
    def padc(a):
        return jnp.pad(a, ((0, 0), (0, SMALL_COLS - a.shape[1])))
    scal = jnp.concatenate([jnp.pad(a_log, ((0, 0), (0, DH - HEADS))), jnp.pad(dt_bias, ((0, 0), (0, DH - HEADS))),
                            gdn_norm] + ([] if extra is None else [extra]), axis=1)
    rows = [padc(g) for g in gains4] + [padc(meta), padc(conv_qkv), padc(conv_sc), padc(scal)]
    slab = jnp.concatenate(rows, axis=0)
    return jnp.pad(slab, ((0, SMALL_ROWS - slab.shape[0]), (0, 0)))


def _unpack_small(slab):
    gains = [slab[i:i + 1, :D] for i in range(4)]
    meta = slab[4:20, :D]
    conv_qkv = slab[20:24, :]
    conv_sc = slab[24:27, :SW]
    a_log = slab[27:28, 0:HEADS]
    dt_bias = slab[27:28, DH:DH + HEADS]
    gdn_norm = slab[27:28, 2 * DH:3 * DH]
    return gains, meta, conv_qkv, conv_sc, a_log, dt_bias, gdn_norm


def kernel(x, meta_tokens, mix_pre_norm, mix_post_norm, ffn_pre_norm, ffn_post_norm, w_in, conv_qkv, a_log, dt_bias, gdn_norm, conv_sc, w_out, w_gate, w_up, w_down, loss_target, m_meta_tokens, m_mix_pre_norm, m_mix_post_norm, m_ffn_pre_norm, m_ffn_post_norm, m_w_in, m_conv_qkv, m_a_log, m_dt_bias, m_gdn_norm, m_conv_sc, m_w_out, m_w_gate, m_w_up, m_w_down, v_meta_tokens, v_mix_pre_norm, v_mix_post_norm, v_ffn_pre_norm, v_ffn_post_norm, v_w_in, v_conv_qkv, v_a_log, v_dt_bias, v_gdn_norm, v_conv_sc, v_w_out, v_w_gate, v_w_up, v_w_down):
    bsz, seq, _ = x.shape
    lp = -(-(N_META + seq) // CH) * CH
    padf = lp - N_META - seq
    first_real = padf + N_META
    rows = bsz * lp
    nchunk = lp // CH
    tr = _div_tile(lp, 528)
    tm = _div_tile(rows, 1056)
    tm_big = _div_tile(rows, 2112)
    cx, cy, cc = _place()
    dev = 4 * cx + 2 * cy + cc

    small_shard = jnp.concatenate([
        jnp.pad(meta_tokens, ((0, 0), (0, 256 - DH))),
        jnp.pad(conv_qkv[0], ((0, 0), (0, 256 - 3 * GW // N_DEV))),
        jnp.pad(conv_sc[0], ((0, 5), (0, 256 - SW // N_DEV)))], axis=0)
    g_in, g_small = _all_gather("gather_w_in", [w_in[0].T.astype(BF16), small_shard])
    my_rest = [w_out[0].astype(BF16), w_gate[0].T.astype(BF16), w_up[0].T.astype(BF16), w_down[0].astype(BF16)]
    (grp_rest,), token = _split_start("gather_rest_start", [my_rest], _chipcast_pattern, lambda s: (N_DEV,) + s,
                                      after=g_in)
    after_start = token[0:1, 0:1]

    def cols_full(g):
        return jnp.transpose(g, (1, 0, 2)).reshape(g.shape[1], -1)

    win_t = g_in.reshape(IN_W, D)
    win_t = jnp.concatenate([win_t[:BA0 + 2 * HEADS], jnp.zeros((BA_W - 2 * HEADS, D), BF16),
                             win_t[BA0 + 2 * HEADS:]], axis=0)
    meta_full = cols_full(g_small[:, 0:16, :DH])
    convq_full = cols_full(g_small[:, 16:20, :3 * GW // N_DEV])
    convs_full = cols_full(g_small[:, 20:23, :SW // N_DEV])
    alog_v = jnp.pad(a_log, ((0, 0), (HEADS, DH - 2 * HEADS)))
    dtb_v = jnp.pad(dt_bias, ((0, 0), (HEADS, DH - 2 * HEADS)))

    head_rows = jnp.concatenate([jnp.zeros((padf, D), F32), meta_full], axis=0)
    h0, u1 = _pre_norm(x, head_rows, mix_pre_norm + after_start, bsz, lp, tr)
    (proj,) = _mm("proj_in", [(u1, win_t)], "nt", tm_big, 768, D, out_dtypes=(BF16,))
    (ba,) = _mm("proj_gate_logits", [(u1, win_t[BA0:SX0])], "nt", tm, BA_W, D)
    qkv = _gdn_prep(proj, convq_full, bsz, lp, padf)
    gb, gcs = _gates(ba, alog_v, dtb_v, bsz, lp, padf)

    def lane_bcast(cols):
        t = jnp.transpose(cols.reshape(bsz, lp, HEADS), (0, 2, 1)).reshape(bsz * HEADS, lp, 1)
        return jnp.broadcast_to(t, (bsz * HEADS, lp, DH))

    bb = lane_bcast(gb[:, 0:HEADS])
    gcb = lane_bcast(gcs[:, HEADS:2 * HEADS])
    gct = jnp.transpose(gcs[:, HEADS:2 * HEADS].reshape(bsz, nchunk, CH, HEADS), (0, 3, 1, 2)).reshape(
        bsz * HEADS, nchunk, 1, CH)
    u_loc, l1, l2, l3, w_t, k_dec, tinv_all, g_last = _gdn_local_fwd(qkv, gcb, bb, gct, bsz, lp)
    _, l_rest = _split_wait("gather_rest_arrive", grp_rest, _chipcast_pattern, u_loc)
    (grp_out, grp_ffn, grp_down), token = _split_start(
        "gather_rest_forward", [l_rest[0:1], l_rest[1:3], l_rest[3:4]], _forward_pattern, None)
    o, v_new, states = _gdn_scan_fwd(u_loc, l1, l2, g_last + token[0, 0], bsz, lp)
    o, v_new = o.reshape(rows, GW), v_new.reshape(rows, GW)
    mix_g, mix_s = _mix_heads(proj, o, gdn_norm, convs_full, bsz, lp, padf)
    mixin = jnp.concatenate([mix_g, mix_s], axis=1)
    _, (l_out,) = _split_wait("gather_w_out_wait", grp_out, _forward_pattern, mixin)
    wout = l_out.reshape(D, D)
    (mix,) = _mm("proj_out", [(mixin, wout)], "nn", tm, 512, D)
    h1, u2 = _mid_norms(h0, mix, mix_post_norm, ffn_pre_norm, bsz, lp, tr)
    _, (l_gate, l_up) = _split_wait("gather_ffn_wait", grp_ffn, _forward_pattern, u2)
    wgate_t, wup_t = l_gate.reshape(FF, D), l_up.reshape(FF, D)
    gate, up, act = _ffn_up(u2, wgate_t, wup_t, _div_tile(rows, 528), 1408)
    _, (l_down,) = _split_wait("gather_down_wait", grp_down, _forward_pattern, act)
    wdown = l_down.reshape(FF, D)
    (ffn,) = _mm("ffn_down", [(act, wdown)], "nn", tm, 512, 1408)

    dy, dffn, d_g4, loss_part = _loss_head(h1, ffn, loss_target, ffn_post_norm, bsz, lp, tr, first_real)
    (dwdown,) = _mm("dw_down", [(act, dffn)], "tn", 1408, 512, tm, out_dtypes=(BF16,))

    def swiglu_bwd(da, g, u):
        g, u = g.astype(F32), u.astype(F32)
        return da * u * _dsilu(g), da * _silu(g)

    dgate, dup = _mm("d_act", [(dffn, wdown)], "nt", tm, 1408, D, out_dtypes=(BF16, BF16),
                     epilogue=swiglu_bwd, extras=(gate, up))
    (dwgate_t,) = _mm("dw_gate", [(dgate, u2)], "tn", 1408, D, tm, out_dtypes=(BF16,))
    (dwup_t,) = _mm("dw_up", [(dup, u2)], "tn", 1408, D, tm, out_dtypes=(BF16,))
    (du2,) = _mm("d_u2", [(dgate, wgate_t), (dup, wup_t)], "nn", tm, D, 1408)
    dh1, dmix, d_g3, d_g2 = _mid_norms_bwd(h1, mix, du2, dy, ffn_pre_norm, mix_post_norm, bsz, lp, tr)
    (dwout,) = _mm("dw_out", [(mixin, dmix)], "tn", D, 512, tm, out_dtypes=(BF16,))

    def row_blocks(g):
        return g.reshape(4, 2, -1, g.shape[1])

    core = jnp.reshape(cc, (1,)).astype(jnp.int32)
    chip = jnp.reshape(2 * cx + cy, (1,)).astype(jnp.int32)

    def reduce_start(tag, grads):
        (grp,), tok = _split_start("rs_sibling_start_" + tag, [grads], _sibling_pattern, lambda s: s[:1] + s[2:])
        return grp, tok

    def reduce_mid(tag, grp, nms, after):
        mine, lands = _split_wait("rs_sibling_wait_" + tag, grp, _sibling_pattern, after)
        parts = [_pair_sum("pair_sum_" + nm, g, r, core) for nm, g, r in zip(nms, mine, lands)]
        (grp2,), tok = _split_start("rs_chip_start_" + tag, [parts], _chip_pattern, lambda s: (3,) + s[1:])
        return grp2, tok

    def reduce_end(tag, grp2, after):
        return _split_wait("rs_chip_wait_" + tag, grp2, _chip_pattern, after)

    names_ffn = ["w_down", "w_gate", "w_up", "w_out"]
    rs_ffn, tok = reduce_start("ffn", [row_blocks(dwdown), row_blocks(dwgate_t), row_blocks(dwup_t), row_blocks(dwout)])
    (dmixin,) = _mm("d_mixin", [(dmix, wout)], "nt", tm, 512, D, after=tok)
    d_o, dz, dsx, dsb, dsc, d_gn, d_convs = _mix_heads_bwd(proj, o, gdn_norm, convs_full, dmixin, bsz, lp, padf)
    rs_ffn, tok = reduce_mid("ffn", rs_ffn, names_ffn, d_o)
    dv_new, dstates = _gdn_scan_bwd(d_o, l3, w_t, k_dec, g_last + tok[0, 0], bsz, lp)
    dq, dk, dv, dg_b, dbeta_b = _gdn_local_bwd(qkv, gcb, bb, gct, tinv_all, states, dstates, v_new,
                                               dv_new.reshape(rows, GW), d_o, bsz, lp)
    dpqkv, d_convq = _gdn_prep_bwd(proj, convq_full, dq, dk, dv, bsz, lp, padf)
    dba, d_scal = _gates_bwd(ba, alog_v, dtb_v, dbeta_b, dg_b, gb, bsz, lp, padf)
    dproj = jnp.concatenate([dpqkv, dz, dba, dsx, dsb, dsc], axis=1)
    (dwin_t,) = _mm("dw_in", [(dproj, u1)], "tn", 1280, D, tm, out_dtypes=(BF16,))
    parts_ffn, recv_ffn = reduce_end("ffn", rs_ffn, dwin_t)
    dwin_t = jnp.concatenate([dwin_t[:BA0 + 2 * HEADS], dwin_t[SX0:]], axis=0)
    rs_in, tok = reduce_start("in", [row_blocks(dwin_t)])
    (du1,) = _mm("d_u1", [(dproj, win_t)], "nn", tm_big, D, 768, after=tok)
    rs_in, tok = reduce_mid("in", rs_in, ["w_in"], du1)
    dh0, d_g1, d_meta = _pre_norm_bwd(h0, du1, dh1, mix_pre_norm + tok[0:1, 0:1], bsz, lp, tr, padf)
    grad_x = dh0.reshape(bsz, lp, D)[:, first_real:]

    d_alog = d_scal[0:1, HEADS:2 * HEADS]
    d_dtb = d_scal[1:2, HEADS:2 * HEADS]
    slab = _pack_small([d_g1, d_g2, d_g3, d_g4], d_meta, d_convq, d_convs, d_alog, d_dtb, d_gn, loss_part)
    (grp_small,), _ = _split_start("small_grads_start", [[slab]], _gather_pattern, lambda s: (N_DEV,) + s)

    big_w = {"w_in": (w_in, m_w_in, v_w_in), "w_out": (w_out, m_w_out, v_w_out), "w_gate": (w_gate, m_w_gate, v_w_gate),
             "w_up": (w_up, m_w_up, v_w_up), "w_down": (w_down, m_w_down, v_w_down)}

    def update(nm, parts, recv):
        w, m, v = big_w[nm]
        if nm in ("w_out", "w_down"):
            return _adamw_reduced("adamw_" + nm, parts, chip, recv, w[0], m[0], v[0])
        res = _adamw_reduced("adamw_" + nm, parts, chip, recv, w[0].T, m[0].T, v[0].T)
        return [t.T for t in res]

    big_out = {nm: update(nm, p, r) for nm, p, r in zip(names_ffn, parts_ffn, recv_ffn)}
    ffn_done = sum(big_out[nm][1][:1, :1] for nm in names_ffn)
    _, (slabs,) = _split_wait("small_grads_wait", grp_small, _gather_pattern, ffn_done)
    off_meta = dev * DH
    off_cq = dev * (3 * GW // N_DEV)
    off_cs = dev * (SW // N_DEV)

    def shard_of(s):
        gains, meta, cq, cs, al, db, gn = _unpack_small(s)
        return gains, lax.dynamic_slice_in_dim(meta, off_meta, DH, 1), \
            lax.dynamic_slice_in_dim(cq, off_cq, 3 * GW // N_DEV, 1), \
            lax.dynamic_slice_in_dim(cs, off_cs, SW // N_DEV, 1), al, db, gn, s[27:28, 3 * DH:4 * DH]

    slabs_local = jax.vmap(lambda s: _pack_small(*shard_of(s)))(slabs)
    w_small = _pack_small([mix_pre_norm, mix_post_norm, ffn_pre_norm, ffn_post_norm], meta_tokens, conv_qkv[0],
                          conv_sc[0], a_log, dt_bias, gdn_norm)
    m_small = _pack_small([m_mix_pre_norm, m_mix_post_norm, m_ffn_pre_norm, m_ffn_post_norm], m_meta_tokens,
                          m_conv_qkv[0], m_conv_sc[0], m_a_log, m_dt_bias, m_gdn_norm)
    v_small = _pack_small([v_mix_pre_norm, v_mix_post_norm, v_ffn_pre_norm, v_ffn_post_norm], v_meta_tokens,
                          v_conv_qkv[0], v_conv_sc[0], v_a_log, v_dt_bias, v_gdn_norm)
    small_out, loss = _adamw_small(slabs_local, w_small, m_small, v_small)
    parts_in, recv_in = reduce_end("in", rs_in, small_out[1]["gdn_norm"])
    big_out["w_in"] = update("w_in", parts_in[0], recv_in[0])
    names = ["w_in", "w_out", "w_gate", "w_up", "w_down"]

    order = ["meta_tokens", "mix_pre_norm", "mix_post_norm", "ffn_pre_norm", "ffn_post_norm", "w_in", "conv_qkv",
             "a_log", "dt_bias", "gdn_norm", "conv_sc", "w_out", "w_gate", "w_up", "w_down"]
    outs = [loss[0, 0], grad_x]
    for kind in range(4):
        leaves = dict(small_out[kind])
        for nm in names:
            leaves[nm] = big_out[nm][kind][None]
        outs += [leaves[nm] for nm in order]
    return tuple(outs)
```

```python
import functools

import jax
import jax.numpy as jnp
from jax import lax
from jax.experimental import pallas as pl
from jax.experimental.pallas import tpu as pltpu

F32 = jnp.float32
BF16 = jnp.bfloat16
HIGHEST = lax.Precision.HIGHEST
MESH = pl.DeviceIdType.MESH

D = 1024
N_META = 16
HEADS = 4
DH = 128
GW = HEADS * DH
SW = D - GW
FF = 2816
CH = 64
KQ = 4
KS = 3
EPS = 1e-6
IN_W = 3 * GW + GW + 2 * HEADS + 3 * SW
Q0, K0, V0, Z0, BA0, SX0, SB0, SC0, PW = 0, 512, 1024, 1536, 2048, 2304, 2816, 3328, 3840
BA_W = SX0 - BA0
N_DEV = 8
VMEM_LIMIT = 56 * 1024 * 1024

ADAM_LR, ADAM_B1, ADAM_B2, ADAM_EPS, ADAM_WD, ADAM_STEP = 0.001, 0.9, 0.999, 1e-08, 0.01, 10


def _cparams(sem):
    return pltpu.CompilerParams(dimension_semantics=sem, vmem_limit_bytes=VMEM_LIMIT)


def _div_tile(n, target, mult=16):
    best = None
    for t in range(mult, min(n, target) + 1, mult):
        if n % t == 0:
            best = t
    assert best is not None, (n, target)
    return best


def _silu(x):
    return x * jax.nn.sigmoid(x)


def _dsilu(x):
    s = jax.nn.sigmoid(x)
    return s * (1.0 + x * (1.0 - s))


def _bdot(a, b, dims):
    return lax.dot_general(a.astype(BF16), b.astype(BF16), (dims, ((), ())), preferred_element_type=F32)


NN = ((1,), (0,))
NT = ((1,), (1,))
TN = ((0,), (0,))


def _fdot(a, b, dims=NN):
    return lax.dot_general(a, b, (dims, ((), ())), precision=HIGHEST, preferred_element_type=F32)


def _mm(name, pairs, mode, tm, tn, tk, out_dtypes=(F32,), epilogue=None, extras=(), after=None):
    a0, b0 = pairs[0]
    if mode == "nn":
        (m, k), n = a0.shape, b0.shape[1]
    elif mode == "nt":
        (m, k), n = a0.shape, b0.shape[0]
    else:
        (k, m), n = a0.shape, b0.shape[1]
    assert m % tm == 0 and n % tn == 0 and k % tk == 0, (name, m, n, k, tm, tn, tk)
    nk = k // tk
    npair = len(pairs)
    nex = len(extras)
    nout = len(out_dtypes)
    ntok = 0 if after is None else 1
    dims = {"nn": NN, "nt": NT, "tn": TN}[mode]

    def body(*refs):
        ab = refs[:2 * npair]
        ex = refs[2 * npair:2 * npair + nex]
        outs = refs[2 * npair + nex + ntok:2 * npair + nex + ntok + nout]
        acc_ref = refs[-1]
        kk = pl.program_id(2)

        part = None
        for p in range(npair):
            d = lax.dot_general(ab[2 * p][...], ab[2 * p + 1][...], (dims, ((), ())), preferred_element_type=F32)
            part = d if part is None else part + d

        def finish(acc):
            res = epilogue(acc, *[e[...] for e in ex]) if epilogue is not None else (acc,)
            for o, r in zip(outs, res):
                o[...] = r.astype(o.dtype)

        if nk == 1:
            finish(part)
        else:
            @pl.when(kk == 0)
            def _():
                acc_ref[...] = part

            @pl.when(kk > 0)
            def _():
                acc_ref[...] += part

            @pl.when(kk == nk - 1)
            def _():
                finish(acc_ref[...])

    if mode == "nn":
        a_spec = pl.BlockSpec((tm, tk), lambda i, j, q: (i, q))
        b_spec = pl.BlockSpec((tk, tn), lambda i, j, q: (q, j))
    elif mode == "nt":
        a_spec = pl.BlockSpec((tm, tk), lambda i, j, q: (i, q))
        b_spec = pl.BlockSpec((tn, tk), lambda i, j, q: (j, q))
    else:
        a_spec = pl.BlockSpec((tk, tm), lambda i, j, q: (q, i))
        b_spec = pl.BlockSpec((tk, tn), lambda i, j, q: (q, j))
    o_spec = pl.BlockSpec((tm, tn), lambda i, j, q: (i, j))
    flat = [t for pr in pairs for t in pr]
    res = pl.pallas_call(
        body, name=name,
        grid=(m // tm, n // tn, nk),
        in_specs=[a_spec, b_spec] * npair + [o_spec] * nex + [pl.BlockSpec((8, DH), lambda i, j, q: (0, 0))] * ntok,
        out_specs=[o_spec] * nout,
        out_shape=[jax.ShapeDtypeStruct((m, n), dt) for dt in out_dtypes],
        scratch_shapes=[pltpu.VMEM((tm, tn) if nk > 1 else (8, DH), F32)],
        compiler_params=_cparams(("parallel", "parallel", "arbitrary")),
    )(*flat, *extras, *([] if after is None else [after]))
    return res


def _ffn_up(u2, wg, wu, tm, tn):
    m, k = u2.shape
    n = wg.shape[0]

    def body(a_ref, g_ref, u_ref, gate_ref, up_ref, act_ref):
        a = a_ref[...]
        g = lax.dot_general(a, g_ref[...], (NT, ((), ())), preferred_element_type=F32)
        u = lax.dot_general(a, u_ref[...], (NT, ((), ())), preferred_element_type=F32)
        gate_ref[...] = g.astype(BF16)
        up_ref[...] = u.astype(BF16)
        act_ref[...] = (_silu(g) * u).astype(BF16)

    o_spec = pl.BlockSpec((tm, tn), lambda j, i: (i, j))
    w_spec = pl.BlockSpec((tn, k), lambda j, i: (j, 0))
    return pl.pallas_call(
        body, name="ffn_up", grid=(n // tn, m // tm),
        in_specs=[pl.BlockSpec((tm, k), lambda j, i: (i, 0)), w_spec, w_spec],
        out_specs=[o_spec, o_spec, o_spec],
        out_shape=[jax.ShapeDtypeStruct((m, n), BF16)] * 3,
        compiler_params=_cparams(("parallel", "parallel")),
    )(u2, wg, wu)


def _rms(x, gain):
    r = lax.rsqrt(jnp.mean(x * x, axis=-1, keepdims=True) + EPS)
    return x * r * gain, r


def _rms_bwd(x, gain, dy):
    r = lax.rsqrt(jnp.mean(x * x, axis=-1, keepdims=True) + EPS)
    dyw = dy * gain
    dx = r * dyw - x * (r * r * r) * jnp.mean(dyw * x, axis=-1, keepdims=True)
    return dx, dy * x * r


def _row_specs(tr, width, nb):
    return pl.BlockSpec((tr, width), lambda b, j: (b * nb + j, 0))


def _vec_spec(width):
    return pl.BlockSpec((1, width), lambda b, j: (0, 0))


def _first_step(b, j):
    return jnp.logical_and(b == 0, j == 0)


def _pre_norm(x, head_rows, gain, bsz, lp, tr):
    nb = lp // tr
    first_real = head_rows.shape[0]
    assert first_real % 8 == 0 and first_real < tr
    x_spec = pl.BlockSpec((pl.Squeezed(), pl.Element(tr), pl.Element(D)),
                          lambda b, j: (b, pl.multiple_of(jnp.maximum(j * tr - first_real, 0), 8), 0))

    def body(x_ref, hd_ref, g_ref, h_ref, u_ref):
        j = pl.program_id(1)
        xt = x_ref[...]
        h = jnp.where(j == 0, jnp.concatenate([hd_ref[...], xt[:tr - first_real]], axis=0), xt)
        h_ref[...] = h
        u_ref[...] = _rms(h, g_ref[...])[0].astype(BF16)

    rs = _row_specs(tr, D, nb)
    return pl.pallas_call(
        body, name="mix_pre_norm", grid=(bsz, nb),
        in_specs=[x_spec, pl.BlockSpec((first_real, D), lambda b, j: (0, 0)), _vec_spec(D)],
        out_specs=[rs, rs],
        out_shape=[jax.ShapeDtypeStruct((bsz * lp, D), F32), jax.ShapeDtypeStruct((bsz * lp, D), BF16)],
        compiler_params=_cparams(("parallel", "parallel")),
    )(x, head_rows, gain)


def _mid_norms(h0, mix, g_post, g_pre, bsz, lp, tr):
    nb = lp // tr

    def body(h_ref, m_ref, gp_ref, gq_ref, h1_ref, u2_ref):
        h1 = h_ref[...] + _rms(m_ref[...], gp_ref[...])[0]
        h1_ref[...] = h1
        u2_ref[...] = _rms(h1, gq_ref[...])[0].astype(BF16)

    rs = _row_specs(tr, D, nb)
    return pl.pallas_call(
        body, name="mid_norms", grid=(bsz, nb),
        in_specs=[rs, rs, _vec_spec(D), _vec_spec(D)],
        out_specs=[rs, rs],
        out_shape=[jax.ShapeDtypeStruct(h0.shape, F32), jax.ShapeDtypeStruct(h0.shape, BF16)],
        compiler_params=_cparams(("parallel", "parallel")),
    )(h0, mix, g_post, g_pre)


def _loss_head(h1, ffn, tgt, g_post, bsz, lp, tr, first_real):
    nb = lp // tr
    assert first_real % 8 == 0 and first_real < tr
    tgt_spec = pl.BlockSpec((pl.Squeezed(), pl.Element(tr), pl.Element(D)),
                            lambda b, j: (b, pl.multiple_of(jnp.maximum(j * tr - first_real, 0), 8), 0))

    def body(h_ref, f_ref, t_ref, g_ref, dy_ref, dffn_ref, dg_ref, loss_ref):
        b, j = pl.program_id(0), pl.program_id(1)
        f = f_ref[...]
        g = g_ref[...]
        y = h_ref[...] + _rms(f, g)[0]
        t = j * tr + lax.broadcasted_iota(jnp.int32, (tr, 1), 0)
        tg = t_ref[...]
        tg = jnp.where(j == 0, jnp.concatenate([tg[tr - first_real:], tg[:tr - first_real]], axis=0), tg)
        err = jnp.where(t >= first_real, y - tg, 0.0)
        dy = err * (1.0 / D)
        dy_ref[...] = dy
        dx, dgc = _rms_bwd(f, g, dy)
        dffn_ref[...] = dx.astype(BF16)
        lsum = jnp.sum(jnp.sum(err * err, axis=0, keepdims=True), axis=1, keepdims=True) * (0.5 / D)

        @pl.when(_first_step(b, j))
        def _():
            dg_ref[...] = jnp.zeros_like(dg_ref)
            loss_ref[...] = jnp.zeros_like(loss_ref)

        dg_ref[...] += jnp.sum(dgc, axis=0, keepdims=True)
        loss_ref[...] += jnp.broadcast_to(lsum, loss_ref.shape)

    rs = _row_specs(tr, D, nb)
    return pl.pallas_call(
        body, name="loss_head", grid=(bsz, nb),
        in_specs=[rs, rs, tgt_spec, _vec_spec(D)],
        out_specs=[rs, rs, _vec_spec(D), _vec_spec(DH)],
        out_shape=[jax.ShapeDtypeStruct(h1.shape, F32), jax.ShapeDtypeStruct(h1.shape, BF16),
                   jax.ShapeDtypeStruct((1, D), F32), jax.ShapeDtypeStruct((1, DH), F32)],
        compiler_params=_cparams(("arbitrary", "arbitrary")),
    )(h1, ffn, tgt, g_post)


def _mid_norms_bwd(h1, mix, du2, dy, g_pre, g_post, bsz, lp, tr):
    nb = lp // tr

    def body(h_ref, m_ref, du_ref, dy_ref, gq_ref, gp_ref, dh1_ref, dmix_ref, dgq_ref, dgp_ref):
        b, j = pl.program_id(0), pl.program_id(1)
        dx, dgq = _rms_bwd(h_ref[...], gq_ref[...], du_ref[...])
        dh1 = dy_ref[...] + dx
        dh1_ref[...] = dh1
        dm, dgp = _rms_bwd(m_ref[...], gp_ref[...], dh1)
        dmix_ref[...] = dm.astype(BF16)

        @pl.when(_first_step(b, j))
        def _():
            dgq_ref[...] = jnp.zeros_like(dgq_ref)
            dgp_ref[...] = jnp.zeros_like(dgp_ref)

        dgq_ref[...] += jnp.sum(dgq, axis=0, keepdims=True)
        dgp_ref[...] += jnp.sum(dgp, axis=0, keepdims=True)

    rs = _row_specs(tr, D, nb)
    return pl.pallas_call(
        body, name="mid_norms_bwd", grid=(bsz, nb),
        in_specs=[rs, rs, rs, rs, _vec_spec(D), _vec_spec(D)],
        out_specs=[rs, rs, _vec_spec(D), _vec_spec(D)],
        out_shape=[jax.ShapeDtypeStruct(h1.shape, F32), jax.ShapeDtypeStruct(h1.shape, BF16),
                   jax.ShapeDtypeStruct((1, D), F32), jax.ShapeDtypeStruct((1, D), F32)],
        compiler_params=_cparams(("arbitrary", "arbitrary")),
    )(h1, mix, du2, dy, g_pre, g_post)


def _pre_norm_bwd(h0, du1, dh1, gain, bsz, lp, tr, padf):
    nb = lp // tr
    assert padf % 8 == 0 and padf + N_META <= tr

    def body(h_ref, du_ref, dh1_ref, g_ref, dh0_ref, dg_ref, dmeta_ref):
        b, j = pl.program_id(0), pl.program_id(1)
        dx, dgc = _rms_bwd(h_ref[...], g_ref[...], du_ref[...])
        dh0 = dh1_ref[...] + dx
        dh0_ref[...] = dh0

        @pl.when(_first_step(b, j))
        def _():
            dg_ref[...] = jnp.zeros_like(dg_ref)
            dmeta_ref[...] = jnp.zeros_like(dmeta_ref)

        dg_ref[...] += jnp.sum(dgc, axis=0, keepdims=True)

        @pl.when(j == 0)
        def _():
            dmeta_ref[...] += dh0[padf:padf + N_META, :]

    rs = _row_specs(tr, D, nb)
    return pl.pallas_call(
        body, name="pre_norm_bwd", grid=(bsz, nb),
        in_specs=[rs, rs, rs, _vec_spec(D)],
        out_specs=[rs, _vec_spec(D), pl.BlockSpec((N_META, D), lambda b, j: (0, 0))],
        out_shape=[jax.ShapeDtypeStruct(h0.shape, F32), jax.ShapeDtypeStruct((1, D), F32),
                   jax.ShapeDtypeStruct((N_META, D), F32)],
        compiler_params=_cparams(("arbitrary", "arbitrary")),
    )(h0, du1, dh1, gain)


def _shift_down(x, s, lp):
    return x if s == 0 else pltpu.roll(x, s, axis=0)


def _shift_up(x, s, lp):
    return x if s == 0 else pltpu.roll(x, lp - s, axis=0)


def _conv_fwd(x, w, taps, lp):
    y = None
    for i in range(taps):
        term = _shift_down(x, taps - 1 - i, lp) * w[i:i + 1, :]
        y = term if y is None else y + term
    return y


def _conv_bwd(x, w, dy, taps, lp):
    dx = None
    dws = []
    for i in range(taps):
        s = taps - 1 - i
        term = _shift_up(dy, s, lp) * w[i:i + 1, :]
        dx = term if dx is None else dx + term
        dws.append(jnp.sum(dy * _shift_down(x, s, lp), axis=0, keepdims=True))
    return dx, jnp.concatenate(dws, axis=0)


def _slab(lp, col0):
    return pl.BlockSpec((lp, DH), lambda b, s: (b, col0 // DH + s))


def _qkv_act(c, s):
    y = _silu(c)
    r = lax.rsqrt(jnp.sum(y * y, axis=-1, keepdims=True) + EPS)
    scale = jnp.where(s < HEADS, DH ** -0.5, 1.0)
    return y, r, scale


def _gdn_prep(proj, conv_qkv, bsz, lp, padf):
    def body(p_ref, w_ref, o_ref):
        s = pl.program_id(1)
        c = _conv_fwd(p_ref[...].astype(F32), w_ref[...], KQ, lp)
        y, r, scale = _qkv_act(c, s)
        out = jnp.where(s < 2 * HEADS, y * (r * scale), y)
        t = lax.broadcasted_iota(jnp.int32, (lp, 1), 0)
        o_ref[...] = jnp.where(t >= padf, out, 0.0)

    return pl.pallas_call(
        body, name="gdn_prep", grid=(bsz, 3 * HEADS),
        in_specs=[_slab(lp, Q0), pl.BlockSpec((KQ, DH), lambda b, s: (0, s))],
        out_specs=_slab(lp, 0),
        out_shape=jax.ShapeDtypeStruct((bsz * lp, 3 * GW), F32),
        compiler_params=_cparams(("parallel", "parallel")),
    )(proj, conv_qkv)


def _gdn_prep_bwd(proj, conv_qkv, dq, dk, dv, bsz, lp, padf):
    def body(p_ref, w_ref, dq_ref, dk_ref, dv_ref, dp_ref, dw_ref):
        b, s = pl.program_id(0), pl.program_id(1)
        x = p_ref[...].astype(F32)
        w = w_ref[...]
        c = _conv_fwd(x, w, KQ, lp)
        y, r, scale = _qkv_act(c, s)
        d = jnp.where(s < HEADS, dq_ref[...], jnp.where(s < 2 * HEADS, dk_ref[...], dv_ref[...]))
        dn = d * scale
        n = y * r
        dy_norm = r * (dn - n * jnp.sum(dn * n, axis=-1, keepdims=True))
        dyy = jnp.where(s < 2 * HEADS, dy_norm, dn)
        t = lax.broadcasted_iota(jnp.int32, (lp, 1), 0)
        dc = jnp.where(t >= padf, dyy * _dsilu(c), 0.0)
        dx, dw = _conv_bwd(x, w, dc, KQ, lp)
        dp_ref[...] = jnp.where(t >= padf, dx, 0.0).astype(BF16)

        @pl.when(jnp.logical_and(b == 0, s == 0))
        def _():
            dw_ref[...] = jnp.zeros_like(dw_ref)

        col_block = jnp.right_shift(lax.broadcasted_iota(jnp.int32, (KQ, 3 * GW), 1), DH.bit_length() - 1)
        dw_ref[...] += jnp.where(col_block == s, jnp.tile(dw, (1, 3 * HEADS)), 0.0)

    sl = lambda col0: pl.BlockSpec((lp, DH), lambda b, s: (b, col0 // DH + s))
    part = lambda k: pl.BlockSpec((lp, DH), lambda b, s: (b, jnp.clip(s - k * HEADS, 0, HEADS - 1)))
    return pl.pallas_call(
        body, name="gdn_prep_bwd", grid=(bsz, 3 * HEADS),
        in_specs=[sl(Q0), pl.BlockSpec((KQ, DH), lambda b, s: (0, s)), part(0), part(1), part(2)],
        out_specs=[sl(0), pl.BlockSpec((KQ, 3 * GW), lambda b, s: (0, 0))],
        out_shape=[jax.ShapeDtypeStruct((bsz * lp, 3 * GW), BF16), jax.ShapeDtypeStruct((KQ, 3 * GW), F32)],
        compiler_params=_cparams(("arbitrary", "arbitrary")),
    )(proj, conv_qkv, dq, dk, dv)


def _softplus(x):
    e = jnp.exp(-jnp.abs(x))
    one_e = 1.0 + e
    l1p = jnp.where(one_e == 1.0, e, jnp.log(one_e) * (e / (one_e - 1.0)))
    return jnp.maximum(x, 0.0) + l1p


def _tri(n, kind):
    i = lax.broadcasted_iota(jnp.int32, (n, n), 0)
    j = lax.broadcasted_iota(jnp.int32, (n, n), 1)
    return {"incl": i >= j, "strict": i > j, "upper": i <= j}[kind]


def _gates(proj, alog_v, dtb_v, bsz, lp, padf):
    nchunk = lp // CH

    def body(p_ref, al_ref, dt_ref, gb_ref, cs_ref):
        x = p_ref[...].astype(F32)
        lane = lax.broadcasted_iota(jnp.int32, (lp, DH), 1)
        t = lax.broadcasted_iota(jnp.int32, (lp, DH), 0)
        beta = jax.nn.sigmoid(x)
        g = -jnp.exp(al_ref[...]) * _softplus(x + dt_ref[...])
        gb = jnp.where(lane < HEADS, beta, jnp.where(lane < 2 * HEADS, g, 0.0))
        gb_ref[...] = jnp.where(t >= padf, gb, 0.0)
        ltri = _tri(CH, "incl").astype(F32)

        def step(n, carry):
            r0 = pl.multiple_of(n * CH, CH)
            cs_ref[pl.ds(r0, CH), :] = _fdot(ltri, gb_ref[pl.ds(r0, CH), :])
            return carry

        lax.fori_loop(0, nchunk, step, 0)

    blk = pl.BlockSpec((lp, DH), lambda b: (b, 0))
    vec = pl.BlockSpec((1, DH), lambda b: (0, 0))
    out = pl.BlockSpec((lp, DH), lambda b: (b, 0))
    return pl.pallas_call(
        body, name="gates", grid=(bsz,),
        in_specs=[blk, vec, vec], out_specs=[out, out],
        out_shape=[jax.ShapeDtypeStruct((bsz * lp, DH), F32)] * 2,
        compiler_params=_cparams(("parallel",)),
    )(proj, alog_v, dtb_v)


def _gates_bwd(proj, alog_v, dtb_v, dbeta_b, dg_b, gb, bsz, lp, padf):
    def body(p_ref, al_ref, dt_ref, db_ref, dg_ref, gb_ref, dl_ref, dv_ref):
        b = pl.program_id(0)
        x = p_ref[...].astype(F32)
        lane = lax.broadcasted_iota(jnp.int32, (lp, DH), 1)
        t = lax.broadcasted_iota(jnp.int32, (lp, DH), 0)
        dbeta = jnp.zeros((lp, DH), F32)
        dg = jnp.zeros((lp, DH), F32)
        for h in range(HEADS):
            dbeta = jnp.where(lane == h, db_ref[h], dbeta)
            dg = jnp.where(lane == HEADS + h, dg_ref[h], dg)
        real = t >= padf
        dbeta = jnp.where(real, dbeta, 0.0)
        dg = jnp.where(real, dg, 0.0)
        sb = jax.nn.sigmoid(x)
        neg_ea = -jnp.exp(al_ref[...])
        dsp = dg * neg_ea * jax.nn.sigmoid(x + dt_ref[...])
        dl = dbeta * sb * (1.0 - sb) + dsp
        dl_ref[...] = jnp.concatenate([dl, jnp.zeros((lp, BA_W - DH), F32)], axis=1).astype(BF16)
        dalog = jnp.sum(dg * gb_ref[...], axis=0, keepdims=True)
        ddt = jnp.sum(dsp, axis=0, keepdims=True)

        @pl.when(b == 0)
        def _():
            dv_ref[...] = jnp.zeros_like(dv_ref)

        dv_ref[...] += jnp.concatenate([dalog, ddt], axis=0)

    blk = pl.BlockSpec((lp, DH), lambda b: (b, 0))
    vec = pl.BlockSpec((1, DH), lambda b: (0, 0))
    hb = pl.BlockSpec((HEADS, lp, DH), lambda b: (b, 0, 0))
    return pl.pallas_call(
        body, name="gates_bwd", grid=(bsz,),
        in_specs=[blk, vec, vec, hb, hb, pl.BlockSpec((lp, DH), lambda b: (b, 0))],
        out_specs=[pl.BlockSpec((lp, BA_W), lambda b: (b, 0)), pl.BlockSpec((2, DH), lambda b: (0, 0))],
        out_shape=[jax.ShapeDtypeStruct((bsz * lp, BA_W), BF16), jax.ShapeDtypeStruct((2, DH), F32)],
        compiler_params=_cparams(("arbitrary",)),
    )(proj, alog_v, dtb_v, dbeta_b, dg_b, gb)


def _mix_heads(proj, o, gdn_norm, conv_sc, bsz, lp, padf):
    def body(o_ref, z_ref, gn_ref, sx_ref, sb_ref, sc_ref, w_ref, og_ref, os_ref):
        og_ref[...] = (_rms(o_ref[...], gn_ref[...])[0] * _silu(z_ref[...].astype(F32))).astype(BF16)
        conv = _conv_fwd(sc_ref[...].astype(F32) * sx_ref[...].astype(F32), w_ref[...], KS, lp)
        t = lax.broadcasted_iota(jnp.int32, (lp, 1), 0)
        os_ref[...] = jnp.where(t >= padf, sb_ref[...].astype(F32) * conv, 0.0).astype(BF16)

    half = _slab(lp, 0)
    return pl.pallas_call(
        body, name="mix_heads", grid=(bsz, HEADS),
        in_specs=[half, _slab(lp, Z0), pl.BlockSpec((1, DH), lambda b, s: (0, 0)),
                  _slab(lp, SX0), _slab(lp, SB0), _slab(lp, SC0), pl.BlockSpec((KS, DH), lambda b, s: (0, s))],
        out_specs=[half, half],
        out_shape=[jax.ShapeDtypeStruct((bsz * lp, GW), BF16)] * 2,
        compiler_params=_cparams(("parallel", "parallel")),
    )(o, proj, gdn_norm, proj, proj, proj, conv_sc)


def _mix_heads_bwd(proj, o, gdn_norm, conv_sc, dmixin, bsz, lp, padf):
    def body(o_ref, z_ref, gn_ref, sx_ref, sb_ref, sc_ref, w_ref, dg_ref, ds_ref,
             do_ref, dz_ref, dsx_ref, dsb_ref, dsc_ref, dgn_ref, dw_ref):
        s, b = pl.program_id(0), pl.program_id(1)
        t = lax.broadcasted_iota(jnp.int32, (lp, 1), 0)
        real = t >= padf
        o, z, gn, d = o_ref[...], z_ref[...].astype(F32), gn_ref[...], dg_ref[...]
        sz = _silu(z)
        on = _rms(o, gn)[0]
        dz_ref[...] = (d * on * _dsilu(z)).astype(BF16)
        dox, dgn = _rms_bwd(o, gn, d * sz)
        do_ref[...] = dox
        sx, sb, sc = sx_ref[...].astype(F32), sb_ref[...].astype(F32), sc_ref[...].astype(F32)
        w, e = w_ref[...], ds_ref[...]
        e = jnp.where(real, e, 0.0)
        xin = sc * sx
        conv = _conv_fwd(xin, w, KS, lp)
        dsb_ref[...] = (e * conv).astype(BF16)
        dxin, dw = _conv_bwd(xin, w, e * sb, KS, lp)
        dxin = jnp.where(real, dxin, 0.0)
        dsx_ref[...] = (dxin * sc).astype(BF16)
        dsc_ref[...] = (dxin * sx).astype(BF16)

        @pl.when(jnp.logical_and(s == 0, b == 0))
        def _():
            dgn_ref[...] = jnp.zeros_like(dgn_ref)

        @pl.when(b == 0)
        def _():
            dw_ref[...] = jnp.zeros_like(dw_ref)

        dgn_ref[...] += jnp.sum(dgn, axis=0, keepdims=True)
        dw_ref[...] += dw

    sl = lambda col0: pl.BlockSpec((lp, DH), lambda s, b: (b, col0 // DH + s))
    half = sl(0)
    return pl.pallas_call(
        body, name="mix_heads_bwd", grid=(HEADS, bsz),
        in_specs=[half, sl(Z0), pl.BlockSpec((1, DH), lambda s, b: (0, 0)), sl(SX0), sl(SB0), sl(SC0),
                  pl.BlockSpec((KS, DH), lambda s, b: (0, s)), sl(0), sl(GW)],
        out_specs=[half] * 5 + [pl.BlockSpec((1, DH), lambda s, b: (0, 0)), pl.BlockSpec((KS, DH), lambda s, b: (0, s))],
        out_shape=[jax.ShapeDtypeStruct((bsz * lp, GW), F32)] + [jax.ShapeDtypeStruct((bsz * lp, GW), BF16)] * 4
        + [jax.ShapeDtypeStruct((1, DH), F32), jax.ShapeDtypeStruct((KS, SW), F32)],
        compiler_params=_cparams(("arbitrary", "arbitrary")),
    )(o, proj, gdn_norm, proj, proj, proj, conv_sc, dmixin, dmixin)


def _split_dot(a, b, dims=NN):
    ah = a.astype(BF16)
    al = (a - ah.astype(F32)).astype(BF16)
    bh = b.astype(BF16)
    bl = (b - bh.astype(F32)).astype(BF16)

    def d(x, y):
        return lax.dot_general(x, y, (dims, ((), ())), preferred_element_type=F32)

    return d(ah, bh) + (d(ah, bl) + d(al, bh))


def _eye(n):
    return (lax.broadcasted_iota(jnp.int32, (n, n), 0) == lax.broadcasted_iota(jnp.int32, (n, n), 1)).astype(F32)


def _each(f, *lists):
    return [f(*xs) for xs in zip(*lists)]


def _inv_unit_lower(a_list):
    eye = _eye(CH)
    p = _each(lambda a: -a, a_list)
    t = _each(lambda x: eye + x, p)
    p = _each(lambda x: _split_dot(x, x), p)
    for level in range(5):
        if level < 4:
            prod = _each(lambda tt, pp: _split_dot(jnp.concatenate([tt, pp], axis=0), pp), t, p)
            t = _each(lambda tt, pr: tt + pr[:CH], t, prod)
            p = _each(lambda pr: pr[CH:], prod)
        else:
            t = _each(lambda tt, pp: tt + _split_dot(tt, pp), t, p)
    return t


def _chunk_gates(gc_ref, be_ref, gt_ref, g):
    rows = pl.ds(g * CH, CH)
    gc = gc_ref[0, rows, :]
    be = be_ref[0, rows, :]
    gcl = gc_ref[0, pl.ds(g * CH + CH - 1, 1), :]
    gt = gt_ref[0, g]
    incl, upper = _tri(CH, "incl"), _tri(CH, "upper")
    dec = jnp.where(incl, jnp.exp(jnp.where(incl, gc[:, :CH] - gt, 0.0)), 0.0)
    dec_t = jnp.where(upper, jnp.exp(jnp.where(upper, gt - gc[:, :CH], 0.0)), 0.0)
    return rows, be, dec, dec_t, jnp.exp(gc), jnp.exp(gcl - gc), jnp.exp(gcl)


def _chunk_group(nchunk):
    return 3 if nchunk % 3 == 0 else 1


def _local_group(nchunk):
    return 11 if nchunk % 11 == 0 else _chunk_group(nchunk)


def _gdn_local_specs(lp, grp):
    ngrp = lp // (grp * CH)
    head = lambda col0: pl.BlockSpec((grp * CH, DH), lambda i, j: ((i // HEADS) * ngrp + j, col0 // DH + i % HEADS))
    lane_b = pl.BlockSpec((1, grp * CH, DH), lambda i, j: (i, j, 0))
    per_chunk = lambda r, c: pl.BlockSpec((1, grp, r, c), lambda i, j: (i, j, 0, 0))
    return ngrp, head, lane_b, per_chunk


def _gdn_local_fwd(qkv, gcb, bb, gct, bsz, lp):
    nchunk = lp // CH
    grp = _local_group(nchunk)
    ngrp, head, lane_b, per_chunk = _gdn_local_specs(lp, grp)
    bh = bsz * HEADS

    def body(q_ref, k_ref, v_ref, gc_ref, be_ref, gt_ref,
             u_ref, l1_ref, l2_ref, l3_ref, wt_ref, kd_ref, ti_ref, gl_ref):
        incl, strict, upper = _tri(CH, "incl"), _tri(CH, "strict"), _tri(CH, "upper")
        eye = _eye(CH)
        rows, be, dec, dec_t, eg, ekd, gl = zip(*[_chunk_gates(gc_ref, be_ref, gt_ref, g) for g in range(grp)])
        q = [q_ref[r, :] for r in rows]
        k = [k_ref[r, :] for r in rows]
        v = [v_ref[r, :] for r in rows]
        kb = _each(lambda x, y: x * y, k, be)
        a = _each(lambda x, y, d: jnp.where(strict, _bdot(x, y, NT) * d, 0.0), kb, k, dec)
        tinv = _inv_unit_lower(a)
        u = _each(lambda t, x, y: _bdot(t, x * y, NN), tinv, v, be)
        w = _each(lambda t, x, y: _bdot(t, x * y, NN), tinv, kb, eg)
        qk = _each(lambda x, y, d: jnp.where(incl, _bdot(x, y, NT) * d, 0.0), q, k, dec)
        qk_t = _each(lambda x, y, d: jnp.where(upper, _bdot(y, x, NT) * d, 0.0), q, k, dec_t)
        qd = _each(lambda x, y: x * y, q, eg)
        kd = _each(lambda x, y: x * y, k, ekd)
        kd_t = _each(lambda x: _bdot(x, eye, TN), kd)
        qd_t = _each(lambda x: _bdot(x, eye, TN), qd)
        w_t = _each(lambda x: _bdot(x, eye, TN), w)
        for g in range(grp):
            u_ref[rows[g], :] = u[g]
            l1_ref[0, g] = jnp.concatenate([w[g], qd[g]], axis=0).astype(BF16)
            l2_ref[0, g] = jnp.concatenate([qk[g], kd_t[g]], axis=0).astype(BF16)
            l3_ref[0, g] = jnp.concatenate([qk_t[g], qd_t[g]], axis=0).astype(BF16)
            wt_ref[0, g] = w_t[g].astype(BF16)
            kd_ref[0, g] = kd[g].astype(BF16)
            ti_ref[0, g] = tinv[g]
            gl_ref[0, g] = gl[g]

    r = bsz * lp
    shapes = [((r, GW), F32), ((bh, nchunk, 2 * CH, DH), BF16), ((bh, nchunk, 3 * CH, CH), BF16),
              ((bh, nchunk, 3 * CH, CH), BF16), ((bh, nchunk, DH, CH), BF16), ((bh, nchunk, CH, DH), BF16),
              ((bh, nchunk, CH, CH), F32), ((bh, nchunk, 1, DH), F32)]
    return pl.pallas_call(
        body, name="gdn_local_fwd", grid=(bh, ngrp),
        in_specs=[head(0), head(GW), head(2 * GW), lane_b, lane_b, per_chunk(1, CH)],
        out_specs=[head(0), per_chunk(2 * CH, DH), per_chunk(3 * CH, CH), per_chunk(3 * CH, CH), per_chunk(DH, CH),
                   per_chunk(CH, DH), per_chunk(CH, CH), per_chunk(1, DH)],
        out_shape=[jax.ShapeDtypeStruct(s, dt) for s, dt in shapes],
        compiler_params=_cparams(("parallel", "parallel")),
    )(qkv, qkv, qkv, gcb, bb, gct)


def _scan_specs(bsz, lp, grp, order):
    ngrp = lp // (grp * CH)
    bh = bsz * HEADS
    seq = lambda: pl.BlockSpec((bsz, grp * CH, GW), lambda s: (0, order(s, ngrp), 0))
    per_chunk = lambda r, c: pl.BlockSpec((bh, grp, r, c), lambda s: (0, order(s, ngrp), 0, 0))
    return ngrp, bh, seq, per_chunk


def _gdn_scan_fwd(u, l1, l2, gl, bsz, lp):
    nchunk = lp // CH
    grp = _chunk_group(nchunk)
    ngrp, bh, seq, per_chunk = _scan_specs(bsz, lp, grp, lambda s, n: s)

    def body(u_ref, l1_ref, l2_ref, gl_ref, o_ref, vn_ref, st_ref, s_scr):
        @pl.when(pl.program_id(0) == 0)
        def _():
            s_scr[...] = jnp.zeros_like(s_scr)

        for c in range(grp):
            rows = slice(c * CH, (c + 1) * CH)
            for i in range(bh):
                b, h = divmod(i, HEADS)
                cols = slice(h * DH, (h + 1) * DH)
                s = s_scr[i]
                st_ref[i, c] = s
                x = jnp.dot(l1_ref[i, c], s.astype(BF16), preferred_element_type=F32)
                vnew = u_ref[b, rows, cols] - x[:CH]
                y = jnp.dot(l2_ref[i, c], vnew.astype(BF16), preferred_element_type=F32)
                o_ref[b, rows, cols] = x[CH:] + y[:CH]
                vn_ref[b, rows, cols] = vnew.astype(BF16)
                s_scr[i] = s * gl_ref[i, c] + y[CH:]

    return pl.pallas_call(
        body, name="gdn_scan_fwd", grid=(ngrp,),
        in_specs=[seq(), per_chunk(2 * CH, DH), per_chunk(3 * CH, CH), per_chunk(1, DH)],
        out_specs=[seq(), seq(), per_chunk(DH, DH)],
        out_shape=[jax.ShapeDtypeStruct((bsz, lp, GW), F32), jax.ShapeDtypeStruct((bsz, lp, GW), BF16),
                   jax.ShapeDtypeStruct((bh, nchunk, DH, DH), F32)],
        scratch_shapes=[pltpu.VMEM((bh, DH, DH), F32)],
        compiler_params=_cparams(("arbitrary",)),
    )(u.reshape(bsz, lp, GW), l1, l2, gl)


def _gdn_scan_bwd(d_o, l3, wt, kd, gl, bsz, lp):
    nchunk = lp // CH
    grp = _chunk_group(nchunk)
    ngrp, bh, seq, per_chunk = _scan_specs(bsz, lp, grp, lambda s, n: n - 1 - s)

    def body(do_ref, l3_ref, wt_ref, kd_ref, gl_ref, dvn_ref, dst_ref, ds_scr):
        @pl.when(pl.program_id(0) == 0)
        def _():
            ds_scr[...] = jnp.zeros_like(ds_scr)

        for c in reversed(range(grp)):
            rows = slice(c * CH, (c + 1) * CH)
            for i in range(bh):
                b, h = divmod(i, HEADS)
                cols = slice(h * DH, (h + 1) * DH)
                ds = ds_scr[i]
                dst_ref[i, c] = ds
                p = jnp.dot(l3_ref[i, c], do_ref[b, rows, cols].astype(BF16), preferred_element_type=F32)
                dvn = p[:CH] + jnp.dot(kd_ref[i, c], ds.astype(BF16), preferred_element_type=F32)
                dvn_ref[b, rows, cols] = dvn.astype(BF16)
                ds_scr[i] = ds * gl_ref[i, c] + p[CH:] - jnp.dot(wt_ref[i, c], dvn.astype(BF16),
                                                                 preferred_element_type=F32)

    return pl.pallas_call(
        body, name="gdn_scan_bwd", grid=(ngrp,),
        in_specs=[seq(), per_chunk(3 * CH, CH), per_chunk(DH, CH), per_chunk(CH, DH), per_chunk(1, DH)],
        out_specs=[seq(), per_chunk(DH, DH)],
        out_shape=[jax.ShapeDtypeStruct((bsz, lp, GW), BF16), jax.ShapeDtypeStruct((bh, nchunk, DH, DH), F32)],
        scratch_shapes=[pltpu.VMEM((bh, DH, DH), F32)],
        compiler_params=_cparams(("arbitrary",)),
    )(d_o.reshape(bsz, lp, GW), l3, wt, kd, gl)


def _gdn_local_bwd(qkv, gcb, bb, gct, tinv_all, states, dstates, vnew, dvnew, d_o, bsz, lp):
    nchunk = lp // CH
    grp = _local_group(nchunk)
    ngrp, head, lane_b, per_chunk = _gdn_local_specs(lp, grp)
    bh = bsz * HEADS

    def body(q_ref, k_ref, v_ref, gc_ref, be_ref, gt_ref, ti_ref, st_ref, dst_ref, vn_ref, dvn_ref, do_ref,
             dq_ref, dk_ref, dv_ref, dg_ref, db_ref):
        incl, strict = _tri(CH, "incl"), _tri(CH, "strict")
        upper = _tri(CH, "upper").astype(F32)
        ones = jnp.ones((CH, DH), F32)
        last = lax.broadcasted_iota(jnp.int32, (CH, 1), 0) == CH - 1

        def rsum(x):
            return jnp.sum(x, axis=-1, keepdims=True)

        def mul(xs, ys):
            return _each(lambda x, y: x * y, xs, ys)

        rows, be, dec, _, eg, ekd, gl = zip(*[_chunk_gates(gc_ref, be_ref, gt_ref, g) for g in range(grp)])
        q = [q_ref[r, :] for r in rows]
        k = [k_ref[r, :] for r in rows]
        v = [v_ref[r, :] for r in rows]
        vnew = [vn_ref[r, :] for r in rows]
        dvn = [dvn_ref[r, :] for r in rows]
        do = [do_ref[r, :] for r in rows]
        tinv = [ti_ref[0, g] for g in range(grp)]
        s = [st_ref[0, g] for g in range(grp)]
        ds = [dst_ref[0, g] for g in range(grp)]
        kb, vb, qd, kd = mul(k, be), mul(v, be), mul(q, eg), mul(k, ekd)
        kbg = mul(kb, eg)
        a = _each(lambda x, y, d: jnp.where(strict, _bdot(x, y, NT) * d, 0.0), kb, k, dec)
        qk = _each(lambda x, y, d: jnp.where(incl, _bdot(x, y, NT) * d, 0.0), q, k, dec)
        dqk = _each(lambda x, y: jnp.where(incl, _bdot(x, y, NT), 0.0), do, vnew)
        dqd = _each(lambda x, y: _bdot(x, y, NT), do, s)
        dkd = _each(lambda x, y: _bdot(x, y, NT), vnew, ds)
        dgl = _each(lambda x, y: jnp.sum(rsum(x * y), axis=0, keepdims=True), s, ds)
        dw = _each(lambda x, y: -_bdot(x, y, NT), dvn, s)
        dvb = _each(lambda t, x: _bdot(t, x, TN), tinv, dvn)
        dkbg = _each(lambda t, x: _bdot(t, x, TN), tinv, dw)
        dt = _each(lambda x, y, z, w: _bdot(x, y, NT) + _bdot(z, w, NT), dvn, vb, dw, kbg)
        tdt = _each(lambda t, x: _split_dot(t, x, TN), tinv, dt)
        da = _each(lambda x, t: jnp.where(strict, -_split_dot(x, t, NT), 0.0), tdt, tinv)
        dp, dr = mul(da, dec), mul(dqk, dec)
        dkb = _each(lambda x, y, z, e: _bdot(x, y, NN) + z * e, dp, k, dkbg, eg)
        dk = _each(lambda p_, kb_, r_, q_, dkd_, ekd_, dkb_, be_:
                   _bdot(p_, kb_, TN) + _bdot(r_, q_, TN) + dkd_ * ekd_ + dkb_ * be_,
                   dp, kb, dr, q, dkd, ekd, dkb, be)
        dq = _each(lambda r_, k_, dqd_, eg_: _bdot(r_, k_, NN) + dqd_ * eg_, dr, k, dqd, eg)
        m = _each(lambda da_, a_, dqk_, qk_: da_ * a_ + dqk_ * qk_, da, a, dqk, qk)
        m_cols = _each(lambda x: _split_dot(x, ones, TN), m)
        kd_term = _each(lambda x, y: rsum(x * y), dkd, kd)
        dgc = _each(lambda m_, mc, dqd_, qd_, kt, dkbg_, kbg_, dgl_, gl_:
                    rsum(m_) - mc + rsum(dqd_ * qd_) - kt + rsum(dkbg_ * kbg_)
                    + jnp.where(last, jnp.sum(kt, axis=0, keepdims=True) + dgl_ * gl_, 0.0),
                    m, m_cols, dqd, qd, kd_term, dkbg, kbg, dgl, gl)
        dg = _each(lambda x: _split_dot(upper, x, NN), dgc)
        for g in range(grp):
            dq_ref[rows[g], :] = dq[g]
            dk_ref[rows[g], :] = dk[g]
            dv_ref[rows[g], :] = dvb[g] * be[g]
            dg_ref[0, rows[g], :] = dg[g]
            db_ref[0, rows[g], :] = jnp.broadcast_to(rsum(dkb[g] * k[g]) + rsum(dvb[g] * v[g]), (CH, DH))

    r = bsz * lp
    return pl.pallas_call(
        body, name="gdn_local_bwd", grid=(bh, ngrp),
        in_specs=[head(0), head(GW), head(2 * GW), lane_b, lane_b, per_chunk(1, CH), per_chunk(CH, CH),
                  per_chunk(DH, DH), per_chunk(DH, DH), head(0), head(0), head(0)],
        out_specs=[head(0), head(0), head(0), lane_b, lane_b],
        out_shape=[jax.ShapeDtypeStruct((r, GW), F32)] * 3 + [jax.ShapeDtypeStruct((bh, lp, DH), F32)] * 2,
        compiler_params=_cparams(("parallel", "parallel")),
    )(qkv, qkv, qkv, gcb, bb, gct, tinv_all, states, dstates, vnew, dvnew, d_o)


def _place():
    return lax.axis_index("x"), lax.axis_index("y"), lax.axis_index("c")


def _any_specs(n):
    return [pl.BlockSpec(memory_space=pl.ANY)] * n


def _all_gather(name, shards):
    n = len(shards)

    def body(*refs):
        ins, outs = refs[:n], refs[n:2 * n]
        send_sems, recv_sems, local_sems = refs[2 * n:]
        x, y, c = _place()
        me, sibling = (x, y, c), (x, y, 1 - c)
        chips = [(1 - x, y), (x, 1 - y), (1 - x, 1 - y)]

        def slot(i, p):
            return outs[i].at[4 * p[0] + 2 * p[1] + p[2]]

        def copy(i, k, block, to, src=None):
            return pltpu.make_async_remote_copy(
                src_ref=slot(i, block) if src is None else src, dst_ref=slot(i, block),
                send_sem=send_sems.at[i, k], recv_sem=recv_sems.at[i, k], device_id=to, device_id_type=MESH)

        mine = [pltpu.make_async_copy(ins[i], slot(i, me), local_sems.at[i]) for i in range(n)]
        for cp in mine:
            cp.start()
        first = []
        for i in range(n):
            first.append(copy(i, 0, me, sibling, src=ins[i]))
            first += [copy(i, 1 + j, me, (*chip, c), src=ins[i]) for j, chip in enumerate(chips)]
        for cp in first:
            cp.start()
        passed = []
        for j, chip in enumerate(chips):
            for i in range(n):
                copy(i, 1 + j, (*chip, c), me).wait_recv()
                fwd = copy(i, 4 + j, (*chip, c), sibling)
                fwd.start()
                passed.append(fwd)
        for i in range(n):
            copy(i, 0, sibling, me).wait_recv()
            for j, chip in enumerate(chips):
                copy(i, 4 + j, (*chip, 1 - c), me).wait_recv()
        for cp in first + passed:
            cp.wait_send()
        for cp in mine:
            cp.wait()

    return pl.pallas_call(
        body, name=name,
        in_specs=_any_specs(n), out_specs=_any_specs(n),
        out_shape=[jax.ShapeDtypeStruct((N_DEV,) + s.shape, s.dtype) for s in shards],
        scratch_shapes=[pltpu.SemaphoreType.DMA((n, 7)), pltpu.SemaphoreType.DMA((n, 7)),
                        pltpu.SemaphoreType.DMA((n,))],
    )(*shards)


HBM_SPEC = pl.BlockSpec(memory_space=pltpu.HBM)
SEM_SPEC = pl.BlockSpec(memory_space=pltpu.SEMAPHORE)
EFFECT = pltpu.SideEffectType.DATAFLOW_SIDE_EFFECTING
N_PEER = N_DEV - 1


def _peer(r):
    x, y, c = _place()
    return ((1 - x) if r & 4 else x, (1 - y) if r & 2 else y, (1 - c) if r & 1 else c)


def _slot_of(p):
    return 4 * p[0] + 2 * p[1] + p[2]


def _hbm(a):
    return pltpu.with_memory_space_constraint(a, pltpu.HBM)


def _gather_pattern(src, land):
    me = _place()
    return [(src, land.at[_slot_of(me)], _peer(r), land.at[_slot_of(_peer(r))]) for r in range(1, N_DEV)]


def _chipcast_pattern(src, land):
    x, y, c = _place()
    me = (x, y, c)
    peers = [(x, y, 1 - c), (1 - x, y, c), (x, 1 - y, c), (1 - x, 1 - y, c)]
    return [(src, land.at[_slot_of(me)], p, land.at[_slot_of(p)]) for p in peers]


def _forward_pattern(land, _):
    x, y, c = _place()
    chips = [(1 - x, y), (x, 1 - y), (1 - x, 1 - y)]
    return [(land.at[_slot_of((*ch, c))], land.at[_slot_of((*ch, c))], (x, y, 1 - c), land.at[_slot_of((*ch, 1 - c))])
            for ch in chips]


def _sibling_pattern(src, land):
    x, y, c = _place()
    return [(src.at[ch, 1 - c], land.at[ch], (x, y, 1 - c), land.at[ch]) for ch in range(4)]


def _chip_pattern(src, land):
    x, y, c = _place()
    chips = [(1 - x, y), (x, 1 - y), (1 - x, 1 - y)]
    return [(src.at[2 * ch[0] + ch[1]], land.at[j], (*ch, c), land.at[j]) for j, ch in enumerate(chips)]


def _split_start(name, groups, pattern, land_shape, after=None):
    sizes = [len(g) for g in groups]
    arrays = [s for g in groups for s in g]
    n = len(arrays)
    inplace = land_shape is None
    fresh = [] if inplace else [lax.empty(land_shape(s.shape), s.dtype) for s in arrays]
    bufs = arrays + fresh
    order = [] if after is None else [after]

    def body(*refs):
        srcs, lands = refs[:n], (refs[:n] if inplace else refs[n:2 * n])
        first_out = len(bufs) + len(order)
        sems = refs[first_out:first_out + 2 * len(groups)]
        token = refs[-1]
        k = 0
        for gi, size in enumerate(sizes):
            for i in range(size):
                copies = pattern(srcs[k], lands[k])
                for j, (s, d, peer, _) in enumerate(copies):
                    pltpu.make_async_remote_copy(
                        src_ref=s, dst_ref=d, send_sem=sems[2 * gi].at[i * len(copies) + j],
                        recv_sem=sems[2 * gi + 1].at[i * len(copies) + j], device_id=peer, device_id_type=MESH).start()
                k += 1
        token[...] = jnp.zeros_like(token)

    per_array = {_gather_pattern: N_PEER, _chipcast_pattern: 4, _forward_pattern: 3, _sibling_pattern: 4,
                 _chip_pattern: 3}[pattern]
    sem_shapes = [pltpu.SemaphoreType.DMA((size * per_array,)) for size in sizes for _ in range(2)]
    res = pl.pallas_call(
        body, name=name,
        in_specs=[HBM_SPEC] * len(bufs) + [pl.BlockSpec(memory_space=pl.ANY)] * len(order),
        out_specs=[SEM_SPEC] * len(sem_shapes) + [HBM_SPEC] * len(bufs) + [pl.BlockSpec(memory_space=pltpu.VMEM)],
        out_shape=sem_shapes + [pltpu.HBM(s.shape, s.dtype) for s in bufs] + [jax.ShapeDtypeStruct((8, DH), F32)],
        input_output_aliases={i: len(sem_shapes) + i for i in range(len(bufs))},
        compiler_params=pltpu.CompilerParams(has_side_effects=EFFECT),
    )(*[_hbm(s) for s in bufs], *order)
    nsem = len(sem_shapes)
    thru = list(res[nsem:nsem + len(bufs)])
    out, k = [], 0
    for gi, size in enumerate(sizes):
        srcs_out = [] if inplace else thru[k:k + size]
        lands_out = thru[k:k + size] if inplace else thru[n + k:n + k + size]
        out.append((res[2 * gi], res[2 * gi + 1], srcs_out, lands_out))
        k += size
    return out, res[-1]


def _split_wait(name, group, pattern, after):
    send_sems, recv_sems, arrays, lands = group
    n = len(lands)
    inplace = not arrays
    bufs = list(arrays) + list(lands)
    fill_own = pattern in (_gather_pattern, _chipcast_pattern)

    def body(*refs):
        srcs, lands_ = refs[:n], (refs[:n] if inplace else refs[n:2 * n])
        s_sems, r_sems = refs[len(bufs)], refs[len(bufs) + 1]
        own = []
        if fill_own:
            local_sems = refs[-1]
            own = [pltpu.make_async_copy(srcs[i], lands_[i].at[_slot_of(_place())], local_sems.at[i])
                   for i in range(n)]
            for cp in own:
                cp.start()
        for i in range(n):
            copies = pattern(srcs[i], lands_[i])
            for j, (s, _, peer, filled) in enumerate(copies):
                cp = pltpu.make_async_remote_copy(
                    src_ref=s, dst_ref=filled, send_sem=s_sems.at[i * len(copies) + j],
                    recv_sem=r_sems.at[i * len(copies) + j], device_id=peer, device_id_type=MESH)
                cp.wait_send()
                cp.wait_recv()
        for cp in own:
            cp.wait()

    res = pl.pallas_call(
        body, name=name,
        in_specs=[HBM_SPEC] * len(bufs) + [SEM_SPEC, SEM_SPEC, pl.BlockSpec(memory_space=pl.ANY)],
        out_specs=[HBM_SPEC] * len(bufs),
        out_shape=[pltpu.HBM(s.shape, s.dtype) for s in bufs],
        scratch_shapes=[pltpu.SemaphoreType.DMA((n,))] if fill_own else [],
        input_output_aliases={i: i for i in range(len(bufs))},
        compiler_params=pltpu.CompilerParams(has_side_effects=EFFECT),
    )(*bufs, send_sems, recv_sems, after)
    return ([], list(res)) if inplace else (list(res[:n]), list(res[n:]))


def _pair_sum(name, grads, recv, core):
    _, _, r, c = grads.shape
    tr = _div_tile(r, 256, 16) if r % 16 == 0 else r

    def body(core_ref, g_ref, r_ref, o_ref):
        o_ref[...] = (g_ref[0].astype(F32) + r_ref[...].astype(F32)).astype(o_ref.dtype)

    return pl.pallas_call(
        body, name=name,
        grid_spec=pltpu.PrefetchScalarGridSpec(
            num_scalar_prefetch=1, grid=(4, r // tr),
            in_specs=[pl.BlockSpec((1, 1, tr, c), lambda s, i, cr: (s, cr[0], i, 0)),
                      pl.BlockSpec((1, tr, c), lambda s, i, cr: (s, i, 0))],
            out_specs=pl.BlockSpec((1, tr, c), lambda s, i, cr: (s, i, 0))),
        out_shape=jax.ShapeDtypeStruct(recv.shape, recv.dtype),
        compiler_params=_cparams(("parallel", "parallel")),
    )(core, grads, recv)


def _adamw_math(w, g, m, v):
    m = ADAM_B1 * m + (1.0 - ADAM_B1) * g
    v = ADAM_B2 * v + (1.0 - ADAM_B2) * (g * g)
    m_hat = m / (1.0 - ADAM_B1 ** ADAM_STEP)
    v_hat = v / (1.0 - ADAM_B2 ** ADAM_STEP)
    delta = -ADAM_LR * (m_hat / (jnp.sqrt(v_hat) + ADAM_EPS) + ADAM_WD * w)
    return delta, m, v


SMALL_LEAVES = (
    ("mix_pre_norm", 0, 1, 0, D, (1, D)), ("mix_post_norm", 1, 1, 0, D, (1, D)),
    ("ffn_pre_norm", 2, 1, 0, D, (1, D)), ("ffn_post_norm", 3, 1, 0, D, (1, D)),
    ("meta_tokens", 4, N_META, 0, DH, (N_META, DH)),
    ("conv_qkv", 20, KQ, 0, 3 * GW // N_DEV, (1, KQ, 3 * GW // N_DEV)),
    ("conv_sc", 24, KS, 0, SW // N_DEV, (1, KS, SW // N_DEV)),
    ("a_log", 27, 1, 0, HEADS, (1, HEADS)), ("dt_bias", 27, 1, DH, HEADS, (1, HEADS)),
    ("gdn_norm", 27, 1, 2 * DH, DH, (1, DH)),
)


def _adamw_small(slabs, w, m, v):
    nslot = slabs.shape[0]
    nleaf = len(SMALL_LEAVES)

    def body(p_ref, w_ref, m_ref, v_ref, *refs):
        outs, loss_ref, scr = refs[:4 * nleaf], refs[4 * nleaf], refs[4 * nleaf + 1]
        g = p_ref[0]
        for s in range(1, nslot):
            g = g + p_ref[s]
        d, mn, vn = _adamw_math(w_ref[...], g, m_ref[...], v_ref[...])
        for kind, val in enumerate((g, d, mn, vn)):
            scr[kind] = val
        for kind in range(4):
            for li, (_, r0, nr, c0, nc, shape) in enumerate(SMALL_LEAVES):
                val = scr[kind, pl.ds(r0, nr), pl.ds(c0, nc)]
                o = outs[kind * nleaf + li]
                if len(shape) == 3:
                    o[0] = val
                else:
                    o[...] = val
        loss_ref[...] = scr[0, pl.ds(27, 1), pl.ds(3 * DH, 1)]

    whole = lambda a: pl.BlockSpec(a.shape, lambda: (0,) * a.ndim)
    out_shapes = [jax.ShapeDtypeStruct(shape, F32) for _ in range(4) for (*_, shape) in SMALL_LEAVES]
    out_shapes.append(jax.ShapeDtypeStruct((1, 1), F32))
    res = pl.pallas_call(
        body, name="adamw_small",
        in_specs=[whole(slabs), whole(w), whole(m), whole(v)],
        out_specs=[pl.BlockSpec(s.shape, lambda n=len(s.shape): (0,) * n) for s in out_shapes],
        out_shape=out_shapes,
        scratch_shapes=[pltpu.VMEM((4,) + w.shape, F32)],
        compiler_params=pltpu.CompilerParams(vmem_limit_bytes=VMEM_LIMIT),
    )(slabs, w, m, v)
    kinds = [{nm: res[kind * nleaf + li] for li, (nm, *_) in enumerate(SMALL_LEAVES)} for kind in range(4)]
    return kinds, res[-1]


def _adamw_reduced(name, parts, chip, recv, w, m, v, after=None):
    _, r, c = parts.shape
    tr = _div_tile(r, 256, 16) if r % 16 == 0 else r
    order = [] if after is None else [after]

    def body(chip_ref, p_ref, r_ref, w_ref, m_ref, v_ref, *rest):
        g_out, d_out, m_out, v_out = rest[len(order):]
        g = p_ref[0].astype(F32)
        for s in range(3):
            g = g + r_ref[s].astype(F32)
        d, mn, vn = _adamw_math(w_ref[...], g, m_ref[...], v_ref[...])
        g_out[...] = g
        d_out[...] = d
        m_out[...] = mn
        v_out[...] = vn

    blk = pl.BlockSpec((tr, c), lambda i, ch: (i, 0))
    return pl.pallas_call(
        body, name=name,
        grid_spec=pltpu.PrefetchScalarGridSpec(
            num_scalar_prefetch=1, grid=(r // tr,),
            in_specs=[pl.BlockSpec((1, tr, c), lambda i, ch: (ch[0], i, 0)),
                      pl.BlockSpec((3, tr, c), lambda i, ch: (0, i, 0)), blk, blk, blk]
            + [pl.BlockSpec((8, DH), lambda i, ch: (0, 0))] * len(order),
            out_specs=[blk] * 4),
        out_shape=[jax.ShapeDtypeStruct((r, c), F32)] * 4,
        compiler_params=_cparams(("parallel",)),
    )(chip, parts, recv, w, m, v, *order)


SMALL_ROWS, SMALL_COLS = 32, 3 * GW


def _pack_small(gains4, meta, conv_qkv, conv_sc, a_log, dt_bias, gdn_norm, extra=None):
    def padc(a):
        return jnp.pad(a, ((0, 0), (0, SMALL_COLS - a.shape[1])))
    scal = jnp.concatenate([jnp.pad(a_log, ((0, 0), (0, DH - HEADS))), jnp.pad(dt_bias, ((0, 0), (0, DH - HEADS))),
                            gdn_norm] + ([] if extra is None else [extra]), axis=1)
    rows = [padc(g) for g in gains4] + [padc(meta), padc(conv_qkv), padc(conv_sc), padc(scal)]
    slab = jnp.concatenate(rows, axis=0)
    return jnp.pad(slab, ((0, SMALL_ROWS - slab.shape[0]), (0, 0)))


def _unpack_small(slab):
    gains = [slab[i:i + 1, :D] for i in range(4)]
    meta = slab[4:20, :D]
    conv_qkv = slab[20:24, :]
    conv_sc = slab[24:27, :SW]
    a_log = slab[27:28, 0:HEADS]
    dt_bias = slab[27:28, DH:DH + HEADS]
    gdn_norm = slab[27:28, 2 * DH:3 * DH]
    return gains, meta, conv_qkv, conv_sc, a_log, dt_bias, gdn_norm


def kernel(x, meta_tokens, mix_pre_norm, mix_post_norm, ffn_pre_norm, ffn_post_norm, w_in, conv_qkv, a_log, dt_bias, gdn_norm, conv_sc, w_out, w_gate, w_up, w_down, loss_target, m_meta_tokens, m_mix_pre_norm, m_mix_post_norm, m_ffn_pre_norm, m_ffn_post_norm, m_w_in, m_conv_qkv, m_a_log, m_dt_bias, m_gdn_norm, m_conv_sc, m_w_out, m_w_gate, m_w_up, m_w_down, v_meta_tokens, v_mix_pre_norm, v_mix_post_norm, v_ffn_pre_norm, v_ffn_post_norm, v_w_in, v_conv_qkv, v_a_log, v_dt_bias, v_gdn_norm, v_conv_sc, v_w_out, v_w_gate, v_w_up, v_w_down):
    bsz, seq, _ = x.shape
    lp = -(-(N_META + seq) // CH) * CH
    padf = lp - N_META - seq
    first_real = padf + N_META
    rows = bsz * lp
    nchunk = lp // CH
    tr = _div_tile(lp, 528)
    tm = _div_tile(rows, 1056)
    tm_big = _div_tile(rows, 2112)
    cx, cy, cc = _place()
    dev = 4 * cx + 2 * cy + cc

    small_shard = jnp.concatenate([
        jnp.pad(meta_tokens, ((0, 0), (0, 256 - DH))),
        jnp.pad(conv_qkv[0], ((0, 0), (0, 256 - 3 * GW // N_DEV))),
        jnp.pad(conv_sc[0], ((0, 5), (0, 256 - SW // N_DEV)))], axis=0)
    g_in, g_small = _all_gather("gather_w_in", [w_in[0].T.astype(BF16), small_shard])
    my_rest = [w_out[0].astype(BF16), w_gate[0].T.astype(BF16), w_up[0].T.astype(BF16), w_down[0].astype(BF16)]
    (grp_rest,), token = _split_start("gather_rest_start", [my_rest], _chipcast_pattern, lambda s: (N_DEV,) + s,
                                      after=g_in)
    after_start = token[0:1, 0:1]

    def cols_full(g):
        return jnp.transpose(g, (1, 0, 2)).reshape(g.shape[1], -1)

    win_t = g_in.reshape(IN_W, D)
    win_t = jnp.concatenate([win_t[:BA0 + 2 * HEADS], jnp.zeros((BA_W - 2 * HEADS, D), BF16),
                             win_t[BA0 + 2 * HEADS:]], axis=0)
    meta_full = cols_full(g_small[:, 0:16, :DH])
    convq_full = cols_full(g_small[:, 16:20, :3 * GW // N_DEV])
    convs_full = cols_full(g_small[:, 20:23, :SW // N_DEV])
    alog_v = jnp.pad(a_log, ((0, 0), (HEADS, DH - 2 * HEADS)))
    dtb_v = jnp.pad(dt_bias, ((0, 0), (HEADS, DH - 2 * HEADS)))

    head_rows = jnp.concatenate([jnp.zeros((padf, D), F32), meta_full], axis=0)
    h0, u1 = _pre_norm(x, head_rows, mix_pre_norm + after_start, bsz, lp, tr)
    (proj,) = _mm("proj_in", [(u1, win_t)], "nt", tm_big, 768, D, out_dtypes=(BF16,))
    (ba,) = _mm("proj_gate_logits", [(u1, win_t[BA0:SX0])], "nt", tm, BA_W, D)
    qkv = _gdn_prep(proj, convq_full, bsz, lp, padf)
    gb, gcs = _gates(ba, alog_v, dtb_v, bsz, lp, padf)

    def lane_bcast(cols):
        t = jnp.transpose(cols.reshape(bsz, lp, HEADS), (0, 2, 1)).reshape(bsz * HEADS, lp, 1)
        return jnp.broadcast_to(t, (bsz * HEADS, lp, DH))

    bb = lane_bcast(gb[:, 0:HEADS])
    gcb = lane_bcast(gcs[:, HEADS:2 * HEADS])
    gct = jnp.transpose(gcs[:, HEADS:2 * HEADS].reshape(bsz, nchunk, CH, HEADS), (0, 3, 1, 2)).reshape(
        bsz * HEADS, nchunk, 1, CH)
    u_loc, l1, l2, l3, w_t, k_dec, tinv_all, g_last = _gdn_local_fwd(qkv, gcb, bb, gct, bsz, lp)
    o, v_new, states = _gdn_scan_fwd(u_loc, l1, l2, g_last, bsz, lp)
    o, v_new = o.reshape(rows, GW), v_new.reshape(rows, GW)
    _, l_rest = _split_wait("gather_rest_arrive", grp_rest, _chipcast_pattern, o)
    (grp_out, grp_ffn, grp_down), token = _split_start(
        "gather_rest_forward", [l_rest[0:1], l_rest[1:3], l_rest[3:4]], _forward_pattern, None)
    mix_g, mix_s = _mix_heads(proj, o, gdn_norm + token[0:1, :], convs_full, bsz, lp, padf)
    mixin = jnp.concatenate([mix_g, mix_s], axis=1)
    _, (l_out,) = _split_wait("gather_w_out_wait", grp_out, _forward_pattern, mixin)
    wout = l_out.reshape(D, D)
    (mix,) = _mm("proj_out", [(mixin, wout)], "nn", tm, 512, D)
    h1, u2 = _mid_norms(h0, mix, mix_post_norm, ffn_pre_norm, bsz, lp, tr)
    _, (l_gate, l_up) = _split_wait("gather_ffn_wait", grp_ffn, _forward_pattern, u2)
    wgate_t, wup_t = l_gate.reshape(FF, D), l_up.reshape(FF, D)
    gate, up, act = _ffn_up(u2, wgate_t, wup_t, _div_tile(rows, 528), 1408)
    _, (l_down,) = _split_wait("gather_down_wait", grp_down, _forward_pattern, act)
    wdown = l_down.reshape(FF, D)
    (ffn,) = _mm("ffn_down", [(act, wdown)], "nn", tm, 512, 1408)

    dy, dffn, d_g4, loss_part = _loss_head(h1, ffn, loss_target, ffn_post_norm, bsz, lp, tr, first_real)
    (dwdown,) = _mm("dw_down", [(act, dffn)], "tn", 1408, 512, tm, out_dtypes=(BF16,))

    def swiglu_bwd(da, g, u):
        g, u = g.astype(F32), u.astype(F32)
        return da * u * _dsilu(g), da * _silu(g)

    dgate, dup = _mm("d_act", [(dffn, wdown)], "nt", tm, 1408, D, out_dtypes=(BF16, BF16),
                     epilogue=swiglu_bwd, extras=(gate, up))
    (dwgate_t,) = _mm("dw_gate", [(dgate, u2)], "tn", 1408, D, tm, out_dtypes=(BF16,))
    (dwup_t,) = _mm("dw_up", [(dup, u2)], "tn", 1408, D, tm, out_dtypes=(BF16,))
    (du2,) = _mm("d_u2", [(dgate, wgate_t), (dup, wup_t)], "nn", tm, D, 1408)
    dh1, dmix, d_g3, d_g2 = _mid_norms_bwd(h1, mix, du2, dy, ffn_pre_norm, mix_post_norm, bsz, lp, tr)
    (dwout,) = _mm("dw_out", [(mixin, dmix)], "tn", D, 512, tm, out_dtypes=(BF16,))

    def row_blocks(g):
        return g.reshape(4, 2, -1, g.shape[1])

    core = jnp.reshape(cc, (1,)).astype(jnp.int32)
    chip = jnp.reshape(2 * cx + cy, (1,)).astype(jnp.int32)

    def reduce_start(tag, grads):
        (grp,), tok = _split_start("rs_sibling_start_" + tag, [grads], _sibling_pattern, lambda s: s[:1] + s[2:])
        return grp, tok

    def reduce_mid(tag, grp, nms, after):
        mine, lands = _split_wait("rs_sibling_wait_" + tag, grp, _sibling_pattern, after)
        parts = [_pair_sum("pair_sum_" + nm, g, r, core) for nm, g, r in zip(nms, mine, lands)]
        (grp2,), tok = _split_start("rs_chip_start_" + tag, [parts], _chip_pattern, lambda s: (3,) + s[1:])
        return grp2, tok

    def reduce_end(tag, grp2, after):
        return _split_wait("rs_chip_wait_" + tag, grp2, _chip_pattern, after)

    names_ffn = ["w_down", "w_gate", "w_up", "w_out"]
    rs_ffn, tok = reduce_start("ffn", [row_blocks(dwdown), row_blocks(dwgate_t), row_blocks(dwup_t), row_blocks(dwout)])
    (dmixin,) = _mm("d_mixin", [(dmix, wout)], "nt", tm, 512, D, after=tok)
    d_o, dz, dsx, dsb, dsc, d_gn, d_convs = _mix_heads_bwd(proj, o, gdn_norm, convs_full, dmixin, bsz, lp, padf)
    rs_ffn, tok = reduce_mid("ffn", rs_ffn, names_ffn, d_o)
    dv_new, dstates = _gdn_scan_bwd(d_o, l3, w_t, k_dec, g_last + tok[0, 0], bsz, lp)
    dq, dk, dv, dg_b, dbeta_b = _gdn_local_bwd(qkv, gcb, bb, gct, tinv_all, states, dstates, v_new,
                                               dv_new.reshape(rows, GW), d_o, bsz, lp)
    dpqkv, d_convq = _gdn_prep_bwd(proj, convq_full, dq, dk, dv, bsz, lp, padf)
    dba, d_scal = _gates_bwd(ba, alog_v, dtb_v, dbeta_b, dg_b, gb, bsz, lp, padf)
    dproj = jnp.concatenate([dpqkv, dz, dba, dsx, dsb, dsc], axis=1)
    (dwin_t,) = _mm("dw_in", [(dproj, u1)], "tn", 1280, D, tm, out_dtypes=(BF16,))
    parts_ffn, recv_ffn = reduce_end("ffn", rs_ffn, dwin_t)
    dwin_t = jnp.concatenate([dwin_t[:BA0 + 2 * HEADS], dwin_t[SX0:]], axis=0)
    rs_in, tok = reduce_start("in", [row_blocks(dwin_t)])

    big_w = {"w_in": (w_in, m_w_in, v_w_in), "w_out": (w_out, m_w_out, v_w_out), "w_gate": (w_gate, m_w_gate, v_w_gate),
             "w_up": (w_up, m_w_up, v_w_up), "w_down": (w_down, m_w_down, v_w_down)}

    def update(nm, parts, recv, after=None):
        w, m, v = big_w[nm]
        if nm in ("w_out", "w_down"):
            return _adamw_reduced("adamw_" + nm, parts, chip, recv, w[0], m[0], v[0], after)
        res = _adamw_reduced("adamw_" + nm, parts, chip, recv, w[0].T, m[0].T, v[0].T, after)
        return [t.T for t in res]

    big_out = {nm: update(nm, p, r, tok) for nm, p, r in zip(names_ffn, parts_ffn, recv_ffn)}
    ffn_done = sum(big_out[nm][1][:1, :1] for nm in names_ffn)
    rs_in, tok = reduce_mid("in", rs_in, ["w_in"], ffn_done)
    (du1,) = _mm("d_u1", [(dproj, win_t)], "nn", tm_big, D, 768, after=tok)
    dh0, d_g1, d_meta = _pre_norm_bwd(h0, du1, dh1, mix_pre_norm, bsz, lp, tr, padf)
    grad_x = dh0.reshape(bsz, lp, D)[:, first_real:]

    d_alog = d_scal[0:1, HEADS:2 * HEADS]
    d_dtb = d_scal[1:2, HEADS:2 * HEADS]
    slab = _pack_small([d_g1, d_g2, d_g3, d_g4], d_meta, d_convq, d_convs, d_alog, d_dtb, d_gn, loss_part)
    (grp_small,), _ = _split_start("small_grads_start", [[slab]], _gather_pattern, lambda s: (N_DEV,) + s)
    _, (slabs,) = _split_wait("small_grads_wait", grp_small, _gather_pattern, grad_x)
    off_meta = dev * DH
    off_cq = dev * (3 * GW // N_DEV)
    off_cs = dev * (SW // N_DEV)

    def shard_of(s):
        gains, meta, cq, cs, al, db, gn = _unpack_small(s)
        return gains, lax.dynamic_slice_in_dim(meta, off_meta, DH, 1), \
            lax.dynamic_slice_in_dim(cq, off_cq, 3 * GW // N_DEV, 1), \
            lax.dynamic_slice_in_dim(cs, off_cs, SW // N_DEV, 1), al, db, gn, s[27:28, 3 * DH:4 * DH]

    slabs_local = jax.vmap(lambda s: _pack_small(*shard_of(s)))(slabs)
    w_small = _pack_small([mix_pre_norm, mix_post_norm, ffn_pre_norm, ffn_post_norm], meta_tokens, conv_qkv[0],
                          conv_sc[0], a_log, dt_bias, gdn_norm)
    m_small = _pack_small([m_mix_pre_norm, m_mix_post_norm, m_ffn_pre_norm, m_ffn_post_norm], m_meta_tokens,
                          m_conv_qkv[0], m_conv_sc[0], m_a_log, m_dt_bias, m_gdn_norm)
    v_small = _pack_small([v_mix_pre_norm, v_mix_post_norm, v_ffn_pre_norm, v_ffn_post_norm], v_meta_tokens,
                          v_conv_qkv[0], v_conv_sc[0], v_a_log, v_dt_bias, v_gdn_norm)
    small_out, loss = _adamw_small(slabs_local, w_small, m_small, v_small)
    parts_in, recv_in = reduce_end("in", rs_in, small_out[1]["gdn_norm"])
    big_out["w_in"] = update("w_in", parts_in[0], recv_in[0])
    names = ["w_in", "w_out", "w_gate", "w_up", "w_down"]

    order = ["meta_tokens", "mix_pre_norm", "mix_post_norm", "ffn_pre_norm", "ffn_post_norm", "w_in", "conv_qkv",
             "a_log", "dt_bias", "gdn_norm", "conv_sc", "w_out", "w_gate", "w_up", "w_down"]
    outs = [loss[0, 0], grad_x]
    for kind in range(4):
        leaves = dict(small_out[kind])
        for nm in names:
            leaves[nm] = big_out[nm][kind][None]
        outs += [leaves[nm] for nm in order]
    return tuple(outs)
```

```python
import functools

import jax
import jax.numpy as jnp
from jax import lax
from jax.experimental import pallas as pl
from jax.experimental.pallas import tpu as pltpu

F32 = jnp.float32
BF16 = jnp.bfloat16
HIGHEST = lax.Precision.HIGHEST
MESH = pl.DeviceIdType.MESH

D = 1024
N_META = 16
HEADS = 4
DH = 128
GW = HEADS * DH
SW = D - GW
FF = 2816
CH = 64
KQ = 4
KS = 3
EPS = 1e-6
IN_W = 3 * GW + GW + 2 * HEADS + 3 * SW
Q0, K0, V0, Z0, BA0, SX0, SB0, SC0, PW = 0, 512, 1024, 1536, 2048, 2304, 2816, 3328, 3840
BA_W = SX0 - BA0
N_DEV = 8
VMEM_LIMIT = 56 * 1024 * 1024

ADAM_LR, ADAM_B1, ADAM_B2, ADAM_EPS, ADAM_WD, ADAM_STEP = 0.001, 0.9, 0.999, 1e-08, 0.01, 10


def _cparams(sem):
    return pltpu.CompilerParams(dimension_semantics=sem, vmem_limit_bytes=VMEM_LIMIT)


def _div_tile(n, target, mult=16):
    best = None
    for t in range(mult, min(n, target) + 1, mult):
        if n % t == 0:
            best = t
    assert best is not None, (n, target)
    return best


def _silu(x):
    return x * jax.nn.sigmoid(x)


def _dsilu(x):
    s = jax.nn.sigmoid(x)
    return s * (1.0 + x * (1.0 - s))


def _bdot(a, b, dims):
    return lax.dot_general(a.astype(BF16), b.astype(BF16), (dims, ((), ())), preferred_element_type=F32)


NN = ((1,), (0,))
NT = ((1,), (1,))
TN = ((0,), (0,))


def _fdot(a, b, dims=NN):
    return lax.dot_general(a, b, (dims, ((), ())), precision=HIGHEST, preferred_element_type=F32)


def _mm(name, pairs, mode, tm, tn, tk, out_dtypes=(F32,), epilogue=None, extras=(), after=None):
    a0, b0 = pairs[0]
    if mode == "nn":
        (m, k), n = a0.shape, b0.shape[1]
    elif mode == "nt":
        (m, k), n = a0.shape, b0.shape[0]
    else:
        (k, m), n = a0.shape, b0.shape[1]
    assert m % tm == 0 and n % tn == 0 and k % tk == 0, (name, m, n, k, tm, tn, tk)
    nk = k // tk
    npair = len(pairs)
    nex = len(extras)
    nout = len(out_dtypes)
    ntok = 0 if after is None else 1
    dims = {"nn": NN, "nt": NT, "tn": TN}[mode]

    def body(*refs):
        ab = refs[:2 * npair]
        ex = refs[2 * npair:2 * npair + nex]
        outs = refs[2 * npair + nex + ntok:2 * npair + nex + ntok + nout]
        acc_ref = refs[-1]
        kk = pl.program_id(2)

        part = None
        for p in range(npair):
            d = lax.dot_general(ab[2 * p][...], ab[2 * p + 1][...], (dims, ((), ())), preferred_element_type=F32)
            part = d if part is None else part + d

        def finish(acc):
            res = epilogue(acc, *[e[...] for e in ex]) if epilogue is not None else (acc,)
            for o, r in zip(outs, res):
                o[...] = r.astype(o.dtype)

        if nk == 1:
            finish(part)
        else:
            @pl.when(kk == 0)
            def _():
                acc_ref[...] = part

            @pl.when(kk > 0)
            def _():
                acc_ref[...] += part

            @pl.when(kk == nk - 1)
            def _():
                finish(acc_ref[...])

    if mode == "nn":
        a_spec = pl.BlockSpec((tm, tk), lambda i, j, q: (i, q))
        b_spec = pl.BlockSpec((tk, tn), lambda i, j, q: (q, j))
    elif mode == "nt":
        a_spec = pl.BlockSpec((tm, tk), lambda i, j, q: (i, q))
        b_spec = pl.BlockSpec((tn, tk), lambda i, j, q: (j, q))
    else:
        a_spec = pl.BlockSpec((tk, tm), lambda i, j, q: (q, i))
        b_spec = pl.BlockSpec((tk, tn), lambda i, j, q: (q, j))
    o_spec = pl.BlockSpec((tm, tn), lambda i, j, q: (i, j))
    flat = [t for pr in pairs for t in pr]
    res = pl.pallas_call(
        body, name=name,
        grid=(m // tm, n // tn, nk),
        in_specs=[a_spec, b_spec] * npair + [o_spec] * nex + [pl.BlockSpec((8, DH), lambda i, j, q: (0, 0))] * ntok,
        out_specs=[o_spec] * nout,
        out_shape=[jax.ShapeDtypeStruct((m, n), dt) for dt in out_dtypes],
        scratch_shapes=[pltpu.VMEM((tm, tn) if nk > 1 else (8, DH), F32)],
        compiler_params=_cparams(("parallel", "parallel", "arbitrary")),
    )(*flat, *extras, *([] if after is None else [after]))
    return res


def _ffn_up(u2, wg, wu, tm, tn):
    m, k = u2.shape
    n = wg.shape[0]

    def body(a_ref, g_ref, u_ref, gate_ref, up_ref, act_ref):
        a = a_ref[...]
        g = lax.dot_general(a, g_ref[...], (NT, ((), ())), preferred_element_type=F32)
        u = lax.dot_general(a, u_ref[...], (NT, ((), ())), preferred_element_type=F32)
        gate_ref[...] = g.astype(BF16)
        up_ref[...] = u.astype(BF16)
        act_ref[...] = (_silu(g) * u).astype(BF16)

    o_spec = pl.BlockSpec((tm, tn), lambda j, i: (i, j))
    w_spec = pl.BlockSpec((tn, k), lambda j, i: (j, 0))
    return pl.pallas_call(
        body, name="ffn_up", grid=(n // tn, m // tm),
        in_specs=[pl.BlockSpec((tm, k), lambda j, i: (i, 0)), w_spec, w_spec],
        out_specs=[o_spec, o_spec, o_spec],
        out_shape=[jax.ShapeDtypeStruct((m, n), BF16)] * 3,
        compiler_params=_cparams(("parallel", "parallel")),
    )(u2, wg, wu)


def _rms(x, gain):
    r = lax.rsqrt(jnp.mean(x * x, axis=-1, keepdims=True) + EPS)
    return x * r * gain, r


def _rms_bwd(x, gain, dy):
    r = lax.rsqrt(jnp.mean(x * x, axis=-1, keepdims=True) + EPS)
    dyw = dy * gain
    dx = r * dyw - x * (r * r * r) * jnp.mean(dyw * x, axis=-1, keepdims=True)
    return dx, dy * x * r


def _row_specs(tr, width, nb):
    return pl.BlockSpec((tr, width), lambda b, j: (b * nb + j, 0))


def _vec_spec(width):
    return pl.BlockSpec((1, width), lambda b, j: (0, 0))


def _first_step(b, j):
    return jnp.logical_and(b == 0, j == 0)


def _pre_norm(x, head_rows, gain, bsz, lp, tr):
    nb = lp // tr
    first_real = head_rows.shape[0]
    assert first_real % 8 == 0 and first_real < tr
    x_spec = pl.BlockSpec((pl.Squeezed(), pl.Element(tr), pl.Element(D)),
                          lambda b, j: (b, pl.multiple_of(jnp.maximum(j * tr - first_real, 0), 8), 0))

    def body(x_ref, hd_ref, g_ref, h_ref, u_ref):
        j = pl.program_id(1)
        xt = x_ref[...]
        h = jnp.where(j == 0, jnp.concatenate([hd_ref[...], xt[:tr - first_real]], axis=0), xt)
        h_ref[...] = h
        u_ref[...] = _rms(h, g_ref[...])[0].astype(BF16)

    rs = _row_specs(tr, D, nb)
    return pl.pallas_call(
        body, name="mix_pre_norm", grid=(bsz, nb),
        in_specs=[x_spec, pl.BlockSpec((first_real, D), lambda b, j: (0, 0)), _vec_spec(D)],
        out_specs=[rs, rs],
        out_shape=[jax.ShapeDtypeStruct((bsz * lp, D), F32), jax.ShapeDtypeStruct((bsz * lp, D), BF16)],
        compiler_params=_cparams(("parallel", "parallel")),
    )(x, head_rows, gain)


def _mid_norms(h0, mix, g_post, g_pre, bsz, lp, tr):
    nb = lp // tr

    def body(h_ref, m_ref, gp_ref, gq_ref, h1_ref, u2_ref):
        h1 = h_ref[...] + _rms(m_ref[...], gp_ref[...])[0]
        h1_ref[...] = h1
        u2_ref[...] = _rms(h1, gq_ref[...])[0].astype(BF16)

    rs = _row_specs(tr, D, nb)
    return pl.pallas_call(
        body, name="mid_norms", grid=(bsz, nb),
        in_specs=[rs, rs, _vec_spec(D), _vec_spec(D)],
        out_specs=[rs, rs],
        out_shape=[jax.ShapeDtypeStruct(h0.shape, F32), jax.ShapeDtypeStruct(h0.shape, BF16)],
        compiler_params=_cparams(("parallel", "parallel")),
    )(h0, mix, g_post, g_pre)


def _loss_head(h1, ffn, tgt, g_post, bsz, lp, tr, first_real):
    nb = lp // tr
    assert first_real % 8 == 0 and first_real < tr
    tgt_spec = pl.BlockSpec((pl.Squeezed(), pl.Element(tr), pl.Element(D)),
                            lambda b, j: (b, pl.multiple_of(jnp.maximum(j * tr - first_real, 0), 8), 0))

    def body(h_ref, f_ref, t_ref, g_ref, dy_ref, dffn_ref, dg_ref, loss_ref):
        b, j = pl.program_id(0), pl.program_id(1)
        f = f_ref[...]
        g = g_ref[...]
        y = h_ref[...] + _rms(f, g)[0]
        t = j * tr + lax.broadcasted_iota(jnp.int32, (tr, 1), 0)
        tg = t_ref[...]
        tg = jnp.where(j == 0, jnp.concatenate([tg[tr - first_real:], tg[:tr - first_real]], axis=0), tg)
        err = jnp.where(t >= first_real, y - tg, 0.0)
        dy = err * (1.0 / D)
        dy_ref[...] = dy
        dx, dgc = _rms_bwd(f, g, dy)
        dffn_ref[...] = dx.astype(BF16)
        lsum = jnp.sum(jnp.sum(err * err, axis=0, keepdims=True), axis=1, keepdims=True) * (0.5 / D)

        @pl.when(_first_step(b, j))
        def _():
            dg_ref[...] = jnp.zeros_like(dg_ref)
            loss_ref[...] = jnp.zeros_like(loss_ref)

        dg_ref[...] += jnp.sum(dgc, axis=0, keepdims=True)
        loss_ref[...] += jnp.broadcast_to(lsum, loss_ref.shape)

    rs = _row_specs(tr, D, nb)
    return pl.pallas_call(
        body, name="loss_head", grid=(bsz, nb),
        in_specs=[rs, rs, tgt_spec, _vec_spec(D)],
        out_specs=[rs, rs, _vec_spec(D), _vec_spec(DH)],
        out_shape=[jax.ShapeDtypeStruct(h1.shape, F32), jax.ShapeDtypeStruct(h1.shape, BF16),
                   jax.ShapeDtypeStruct((1, D), F32), jax.ShapeDtypeStruct((1, DH), F32)],
        compiler_params=_cparams(("arbitrary", "arbitrary")),
    )(h1, ffn, tgt, g_post)


def _mid_norms_bwd(h1, mix, du2, dy, g_pre, g_post, bsz, lp, tr):
    nb = lp // tr

    def body(h_ref, m_ref, du_ref, dy_ref, gq_ref, gp_ref, dh1_ref, dmix_ref, dgq_ref, dgp_ref):
        b, j = pl.program_id(0), pl.program_id(1)
        dx, dgq = _rms_bwd(h_ref[...], gq_ref[...], du_ref[...])
        dh1 = dy_ref[...] + dx
        dh1_ref[...] = dh1
        dm, dgp = _rms_bwd(m_ref[...], gp_ref[...], dh1)
        dmix_ref[...] = dm.astype(BF16)

        @pl.when(_first_step(b, j))
        def _():
            dgq_ref[...] = jnp.zeros_like(dgq_ref)
            dgp_ref[...] = jnp.zeros_like(dgp_ref)

        dgq_ref[...] += jnp.sum(dgq, axis=0, keepdims=True)
        dgp_ref[...] += jnp.sum(dgp, axis=0, keepdims=True)

    rs = _row_specs(tr, D, nb)
    return pl.pallas_call(
        body, name="mid_norms_bwd", grid=(bsz, nb),
        in_specs=[rs, rs, rs, rs, _vec_spec(D), _vec_spec(D)],
        out_specs=[rs, rs, _vec_spec(D), _vec_spec(D)],
        out_shape=[jax.ShapeDtypeStruct(h1.shape, F32), jax.ShapeDtypeStruct(h1.shape, BF16),
                   jax.ShapeDtypeStruct((1, D), F32), jax.ShapeDtypeStruct((1, D), F32)],
        compiler_params=_cparams(("arbitrary", "arbitrary")),
    )(h1, mix, du2, dy, g_pre, g_post)


def _pre_norm_bwd(h0, du1, dh1, gain, bsz, lp, tr, padf):
    nb = lp // tr
    assert padf % 8 == 0 and padf + N_META <= tr

    def body(h_ref, du_ref, dh1_ref, g_ref, dh0_ref, dg_ref, dmeta_ref):
        b, j = pl.program_id(0), pl.program_id(1)
        dx, dgc = _rms_bwd(h_ref[...], g_ref[...], du_ref[...])
        dh0 = dh1_ref[...] + dx
        dh0_ref[...] = dh0

        @pl.when(_first_step(b, j))
        def _():
            dg_ref[...] = jnp.zeros_like(dg_ref)
            dmeta_ref[...] = jnp.zeros_like(dmeta_ref)

        dg_ref[...] += jnp.sum(dgc, axis=0, keepdims=True)

        @pl.when(j == 0)
        def _():
            dmeta_ref[...] += dh0[padf:padf + N_META, :]

    rs = _row_specs(tr, D, nb)
    return pl.pallas_call(
        body, name="pre_norm_bwd", grid=(bsz, nb),
        in_specs=[rs, rs, rs, _vec_spec(D)],
        out_specs=[rs, _vec_spec(D), pl.BlockSpec((N_META, D), lambda b, j: (0, 0))],
        out_shape=[jax.ShapeDtypeStruct(h0.shape, F32), jax.ShapeDtypeStruct((1, D), F32),
                   jax.ShapeDtypeStruct((N_META, D), F32)],
        compiler_params=_cparams(("arbitrary", "arbitrary")),
    )(h0, du1, dh1, gain)


def _shift_down(x, s, lp):
    return x if s == 0 else pltpu.roll(x, s, axis=0)


def _shift_up(x, s, lp):
    return x if s == 0 else pltpu.roll(x, lp - s, axis=0)


def _conv_fwd(x, w, taps, lp):
    y = None
    for i in range(taps):
        term = _shift_down(x, taps - 1 - i, lp) * w[i:i + 1, :]
        y = term if y is None else y + term
    return y


def _conv_bwd(x, w, dy, taps, lp):
    dx = None
    dws = []
    for i in range(taps):
        s = taps - 1 - i
        term = _shift_up(dy, s, lp) * w[i:i + 1, :]
        dx = term if dx is None else dx + term
        dws.append(jnp.sum(dy * _shift_down(x, s, lp), axis=0, keepdims=True))
    return dx, jnp.concatenate(dws, axis=0)


def _slab(lp, col0):
    return pl.BlockSpec((lp, DH), lambda b, s: (b, col0 // DH + s))


def _qkv_act(c, s):
    y = _silu(c)
    r = lax.rsqrt(jnp.sum(y * y, axis=-1, keepdims=True) + EPS)
    scale = jnp.where(s < HEADS, DH ** -0.5, 1.0)
    return y, r, scale


def _gdn_prep(proj, conv_qkv, bsz, lp, padf):
    def body(p_ref, w_ref, o_ref):
        s = pl.program_id(1)
        c = _conv_fwd(p_ref[...].astype(F32), w_ref[...], KQ, lp)
        y, r, scale = _qkv_act(c, s)
        out = jnp.where(s < 2 * HEADS, y * (r * scale), y)
        t = lax.broadcasted_iota(jnp.int32, (lp, 1), 0)
        o_ref[...] = jnp.where(t >= padf, out, 0.0)

    return pl.pallas_call(
        body, name="gdn_prep", grid=(bsz, 3 * HEADS),
        in_specs=[_slab(lp, Q0), pl.BlockSpec((KQ, DH), lambda b, s: (0, s))],
        out_specs=_slab(lp, 0),
        out_shape=jax.ShapeDtypeStruct((bsz * lp, 3 * GW), F32),
        compiler_params=_cparams(("parallel", "parallel")),
    )(proj, conv_qkv)


def _gdn_prep_bwd(proj, conv_qkv, dq, dk, dv, bsz, lp, padf):
    def body(p_ref, w_ref, dq_ref, dk_ref, dv_ref, dp_ref, dw_ref):
        b, s = pl.program_id(0), pl.program_id(1)
        x = p_ref[...].astype(F32)
        w = w_ref[...]
        c = _conv_fwd(x, w, KQ, lp)
        y, r, scale = _qkv_act(c, s)
        d = jnp.where(s < HEADS, dq_ref[...], jnp.where(s < 2 * HEADS, dk_ref[...], dv_ref[...]))
        dn = d * scale
        n = y * r
        dy_norm = r * (dn - n * jnp.sum(dn * n, axis=-1, keepdims=True))
        dyy = jnp.where(s < 2 * HEADS, dy_norm, dn)
        t = lax.broadcasted_iota(jnp.int32, (lp, 1), 0)
        dc = jnp.where(t >= padf, dyy * _dsilu(c), 0.0)
        dx, dw = _conv_bwd(x, w, dc, KQ, lp)
        dp_ref[...] = jnp.where(t >= padf, dx, 0.0).astype(BF16)

        @pl.when(jnp.logical_and(b == 0, s == 0))
        def _():
            dw_ref[...] = jnp.zeros_like(dw_ref)

        col_block = jnp.right_shift(lax.broadcasted_iota(jnp.int32, (KQ, 3 * GW), 1), DH.bit_length() - 1)
        dw_ref[...] += jnp.where(col_block == s, jnp.tile(dw, (1, 3 * HEADS)), 0.0)

    sl = lambda col0: pl.BlockSpec((lp, DH), lambda b, s: (b, col0 // DH + s))
    part = lambda k: pl.BlockSpec((lp, DH), lambda b, s: (b, jnp.clip(s - k * HEADS, 0, HEADS - 1)))
    return pl.pallas_call(
        body, name="gdn_prep_bwd", grid=(bsz, 3 * HEADS),
        in_specs=[sl(Q0), pl.BlockSpec((KQ, DH), lambda b, s: (0, s)), part(0), part(1), part(2)],
        out_specs=[sl(0), pl.BlockSpec((KQ, 3 * GW), lambda b, s: (0, 0))],
        out_shape=[jax.ShapeDtypeStruct((bsz * lp, 3 * GW), BF16), jax.ShapeDtypeStruct((KQ, 3 * GW), F32)],
        compiler_params=_cparams(("arbitrary", "arbitrary")),
    )(proj, conv_qkv, dq, dk, dv)


def _softplus(x):
    e = jnp.exp(-jnp.abs(x))
    one_e = 1.0 + e
    l1p = jnp.where(one_e == 1.0, e, jnp.log(one_e) * (e / (one_e - 1.0)))
    return jnp.maximum(x, 0.0) + l1p


def _tri(n, kind):
    i = lax.broadcasted_iota(jnp.int32, (n, n), 0)
    j = lax.broadcasted_iota(jnp.int32, (n, n), 1)
    return {"incl": i >= j, "strict": i > j, "upper": i <= j}[kind]


def _gates(proj, alog_v, dtb_v, bsz, lp, padf):
    nchunk = lp // CH

    def body(p_ref, al_ref, dt_ref, gb_ref, cs_ref):
        x = p_ref[...].astype(F32)
        lane = lax.broadcasted_iota(jnp.int32, (lp, DH), 1)
        t = lax.broadcasted_iota(jnp.int32, (lp, DH), 0)
        beta = jax.nn.sigmoid(x)
        g = -jnp.exp(al_ref[...]) * _softplus(x + dt_ref[...])
        gb = jnp.where(lane < HEADS, beta, jnp.where(lane < 2 * HEADS, g, 0.0))
        gb_ref[...] = jnp.where(t >= padf, gb, 0.0)
        ltri = _tri(CH, "incl").astype(F32)

        def step(n, carry):
            r0 = pl.multiple_of(n * CH, CH)
            cs_ref[pl.ds(r0, CH), :] = _fdot(ltri, gb_ref[pl.ds(r0, CH), :])
            return carry

        lax.fori_loop(0, nchunk, step, 0)

    blk = pl.BlockSpec((lp, DH), lambda b: (b, 0))
    vec = pl.BlockSpec((1, DH), lambda b: (0, 0))
    out = pl.BlockSpec((lp, DH), lambda b: (b, 0))
    return pl.pallas_call(
        body, name="gates", grid=(bsz,),
        in_specs=[blk, vec, vec], out_specs=[out, out],
        out_shape=[jax.ShapeDtypeStruct((bsz * lp, DH), F32)] * 2,
        compiler_params=_cparams(("parallel",)),
    )(proj, alog_v, dtb_v)


def _gates_bwd(proj, alog_v, dtb_v, dbeta_b, dg_b, gb, bsz, lp, padf):
    def body(p_ref, al_ref, dt_ref, db_ref, dg_ref, gb_ref, dl_ref, dv_ref):
        b = pl.program_id(0)
        x = p_ref[...].astype(F32)
        lane = lax.broadcasted_iota(jnp.int32, (lp, DH), 1)
        t = lax.broadcasted_iota(jnp.int32, (lp, DH), 0)
        dbeta = jnp.zeros((lp, DH), F32)
        dg = jnp.zeros((lp, DH), F32)
        for h in range(HEADS):
            dbeta = jnp.where(lane == h, db_ref[h], dbeta)
            dg = jnp.where(lane == HEADS + h, dg_ref[h], dg)
        real = t >= padf
        dbeta = jnp.where(real, dbeta, 0.0)
        dg = jnp.where(real, dg, 0.0)
        sb = jax.nn.sigmoid(x)
        neg_ea = -jnp.exp(al_ref[...])
        dsp = dg * neg_ea * jax.nn.sigmoid(x + dt_ref[...])
        dl = dbeta * sb * (1.0 - sb) + dsp
        dl_ref[...] = jnp.concatenate([dl, jnp.zeros((lp, BA_W - DH), F32)], axis=1).astype(BF16)
        dalog = jnp.sum(dg * gb_ref[...], axis=0, keepdims=True)
        ddt = jnp.sum(dsp, axis=0, keepdims=True)

        @pl.when(b == 0)
        def _():
            dv_ref[...] = jnp.zeros_like(dv_ref)

        dv_ref[...] += jnp.concatenate([dalog, ddt], axis=0)

    blk = pl.BlockSpec((lp, DH), lambda b: (b, 0))
    vec = pl.BlockSpec((1, DH), lambda b: (0, 0))
    hb = pl.BlockSpec((HEADS, lp, DH), lambda b: (b, 0, 0))
    return pl.pallas_call(
        body, name="gates_bwd", grid=(bsz,),
        in_specs=[blk, vec, vec, hb, hb, pl.BlockSpec((lp, DH), lambda b: (b, 0))],
        out_specs=[pl.BlockSpec((lp, BA_W), lambda b: (b, 0)), pl.BlockSpec((2, DH), lambda b: (0, 0))],
        out_shape=[jax.ShapeDtypeStruct((bsz * lp, BA_W), BF16), jax.ShapeDtypeStruct((2, DH), F32)],
        compiler_params=_cparams(("arbitrary",)),
    )(proj, alog_v, dtb_v, dbeta_b, dg_b, gb)


def _mix_heads(proj, o, gdn_norm, conv_sc, bsz, lp, padf):
    def body(o_ref, z_ref, gn_ref, sx_ref, sb_ref, sc_ref, w_ref, og_ref, os_ref):
        og_ref[...] = (_rms(o_ref[...], gn_ref[...])[0] * _silu(z_ref[...].astype(F32))).astype(BF16)
        conv = _conv_fwd(sc_ref[...].astype(F32) * sx_ref[...].astype(F32), w_ref[...], KS, lp)
        t = lax.broadcasted_iota(jnp.int32, (lp, 1), 0)
        os_ref[...] = jnp.where(t >= padf, sb_ref[...].astype(F32) * conv, 0.0).astype(BF16)

    half = _slab(lp, 0)
    return pl.pallas_call(
        body, name="mix_heads", grid=(bsz, HEADS),
        in_specs=[half, _slab(lp, Z0), pl.BlockSpec((1, DH), lambda b, s: (0, 0)),
                  _slab(lp, SX0), _slab(lp, SB0), _slab(lp, SC0), pl.BlockSpec((KS, DH), lambda b, s: (0, s))],
        out_specs=[half, half],
        out_shape=[jax.ShapeDtypeStruct((bsz * lp, GW), BF16)] * 2,
        compiler_params=_cparams(("parallel", "parallel")),
    )(o, proj, gdn_norm, proj, proj, proj, conv_sc)


def _mix_heads_bwd(proj, o, gdn_norm, conv_sc, dmixin, bsz, lp, padf):
    def body(o_ref, z_ref, gn_ref, sx_ref, sb_ref, sc_ref, w_ref, dg_ref, ds_ref,
             do_ref, dz_ref, dsx_ref, dsb_ref, dsc_ref, dgn_ref, dw_ref):
        s, b = pl.program_id(0), pl.program_id(1)
        t = lax.broadcasted_iota(jnp.int32, (lp, 1), 0)
        real = t >= padf
        o, z, gn, d = o_ref[...], z_ref[...].astype(F32), gn_ref[...], dg_ref[...]
        sz = _silu(z)
        on = _rms(o, gn)[0]
        dz_ref[...] = (d * on * _dsilu(z)).astype(BF16)
        dox, dgn = _rms_bwd(o, gn, d * sz)
        do_ref[...] = dox
        sx, sb, sc = sx_ref[...].astype(F32), sb_ref[...].astype(F32), sc_ref[...].astype(F32)
        w, e = w_ref[...], ds_ref[...]
        e = jnp.where(real, e, 0.0)
        xin = sc * sx
        conv = _conv_fwd(xin, w, KS, lp)
        dsb_ref[...] = (e * conv).astype(BF16)
        dxin, dw = _conv_bwd(xin, w, e * sb, KS, lp)
        dxin = jnp.where(real, dxin, 0.0)
        dsx_ref[...] = (dxin * sc).astype(BF16)
        dsc_ref[...] = (dxin * sx).astype(BF16)

        @pl.when(jnp.logical_and(s == 0, b == 0))
        def _():
            dgn_ref[...] = jnp.zeros_like(dgn_ref)

        @pl.when(b == 0)
        def _():
            dw_ref[...] = jnp.zeros_like(dw_ref)

        dgn_ref[...] += jnp.sum(dgn, axis=0, keepdims=True)
        dw_ref[...] += dw

    sl = lambda col0: pl.BlockSpec((lp, DH), lambda s, b: (b, col0 // DH + s))
    half = sl(0)
    return pl.pallas_call(
        body, name="mix_heads_bwd", grid=(HEADS, bsz),
        in_specs=[half, sl(Z0), pl.BlockSpec((1, DH), lambda s, b: (0, 0)), sl(SX0), sl(SB0), sl(SC0),
                  pl.BlockSpec((KS, DH), lambda s, b: (0, s)), sl(0), sl(GW)],
        out_specs=[half] * 5 + [pl.BlockSpec((1, DH), lambda s, b: (0, 0)), pl.BlockSpec((KS, DH), lambda s, b: (0, s))],
        out_shape=[jax.ShapeDtypeStruct((bsz * lp, GW), F32)] + [jax.ShapeDtypeStruct((bsz * lp, GW), BF16)] * 4
        + [jax.ShapeDtypeStruct((1, DH), F32), jax.ShapeDtypeStruct((KS, SW), F32)],
        compiler_params=_cparams(("arbitrary", "arbitrary")),
    )(o, proj, gdn_norm, proj, proj, proj, conv_sc, dmixin, dmixin)


def _split_dot(a, b, dims=NN):
    ah = a.astype(BF16)
    al = (a - ah.astype(F32)).astype(BF16)
    bh = b.astype(BF16)
    bl = (b - bh.astype(F32)).astype(BF16)

    def d(x, y):
        return lax.dot_general(x, y, (dims, ((), ())), preferred_element_type=F32)

    return d(ah, bh) + (d(ah, bl) + d(al, bh))


def _eye(n):
    return (lax.broadcasted_iota(jnp.int32, (n, n), 0) == lax.broadcasted_iota(jnp.int32, (n, n), 1)).astype(F32)


def _each(f, *lists):
    return [f(*xs) for xs in zip(*lists)]


def _inv_unit_lower(a_list):
    eye = _eye(CH)
    p = _each(lambda a: -a, a_list)
    t = _each(lambda x: eye + x, p)
    p = _each(lambda x: _split_dot(x, x), p)
    for level in range(5):
        if level < 4:
            prod = _each(lambda tt, pp: _split_dot(jnp.concatenate([tt, pp], axis=0), pp), t, p)
            t = _each(lambda tt, pr: tt + pr[:CH], t, prod)
            p = _each(lambda pr: pr[CH:], prod)
        else:
            t = _each(lambda tt, pp: tt + _split_dot(tt, pp), t, p)
    return t


def _chunk_gates(gc_ref, be_ref, gt_ref, g):
    rows = pl.ds(g * CH, CH)
    gc = gc_ref[0, rows, :]
    be = be_ref[0, rows, :]
    gcl = gc_ref[0, pl.ds(g * CH + CH - 1, 1), :]
    gt = gt_ref[0, g]
    incl, upper = _tri(CH, "incl"), _tri(CH, "upper")
    dec = jnp.where(incl, jnp.exp(jnp.where(incl, gc[:, :CH] - gt, 0.0)), 0.0)
    dec_t = jnp.where(upper, jnp.exp(jnp.where(upper, gt - gc[:, :CH], 0.0)), 0.0)
    return rows, be, dec, dec_t, jnp.exp(gc), jnp.exp(gcl - gc), jnp.exp(gcl)


def _chunk_group(nchunk):
    return 3 if nchunk % 3 == 0 else 1


def _local_group(nchunk):
    return 11 if nchunk % 11 == 0 else _chunk_group(nchunk)


def _gdn_local_specs(lp, grp):
    ngrp = lp // (grp * CH)
    head = lambda col0: pl.BlockSpec((grp * CH, DH), lambda i, j: ((i // HEADS) * ngrp + j, col0 // DH + i % HEADS))
    lane_b = pl.BlockSpec((1, grp * CH, DH), lambda i, j: (i, j, 0))
    per_chunk = lambda r, c: pl.BlockSpec((1, grp, r, c), lambda i, j: (i, j, 0, 0))
    return ngrp, head, lane_b, per_chunk


def _gdn_local_fwd(qkv, gcb, bb, gct, bsz, lp):
    nchunk = lp // CH
    grp = _local_group(nchunk)
    ngrp, head, lane_b, per_chunk = _gdn_local_specs(lp, grp)
    bh = bsz * HEADS

    def body(q_ref, k_ref, v_ref, gc_ref, be_ref, gt_ref,
             u_ref, l1_ref, l2_ref, l3_ref, wt_ref, kd_ref, ti_ref, gl_ref):
        incl, strict, upper = _tri(CH, "incl"), _tri(CH, "strict"), _tri(CH, "upper")
        eye = _eye(CH)
        rows, be, dec, dec_t, eg, ekd, gl = zip(*[_chunk_gates(gc_ref, be_ref, gt_ref, g) for g in range(grp)])
        q = [q_ref[r, :] for r in rows]
        k = [k_ref[r, :] for r in rows]
        v = [v_ref[r, :] for r in rows]
        kb = _each(lambda x, y: x * y, k, be)
        a = _each(lambda x, y, d: jnp.where(strict, _bdot(x, y, NT) * d, 0.0), kb, k, dec)
        tinv = _inv_unit_lower(a)
        u = _each(lambda t, x, y: _bdot(t, x * y, NN), tinv, v, be)
        w = _each(lambda t, x, y: _bdot(t, x * y, NN), tinv, kb, eg)
        qk = _each(lambda x, y, d: jnp.where(incl, _bdot(x, y, NT) * d, 0.0), q, k, dec)
        qk_t = _each(lambda x, y, d: jnp.where(upper, _bdot(y, x, NT) * d, 0.0), q, k, dec_t)
        qd = _each(lambda x, y: x * y, q, eg)
        kd = _each(lambda x, y: x * y, k, ekd)
        kd_t = _each(lambda x: _bdot(x, eye, TN), kd)
        qd_t = _each(lambda x: _bdot(x, eye, TN), qd)
        w_t = _each(lambda x: _bdot(x, eye, TN), w)
        for g in range(grp):
            u_ref[rows[g], :] = u[g]
            l1_ref[0, g] = jnp.concatenate([w[g], qd[g]], axis=0).astype(BF16)
            l2_ref[0, g] = jnp.concatenate([qk[g], kd_t[g]], axis=0).astype(BF16)
            l3_ref[0, g] = jnp.concatenate([qk_t[g], qd_t[g]], axis=0).astype(BF16)
            wt_ref[0, g] = w_t[g].astype(BF16)
            kd_ref[0, g] = kd[g].astype(BF16)
            ti_ref[0, g] = tinv[g]
            gl_ref[0, g] = gl[g]

    r = bsz * lp
    shapes = [((r, GW), F32), ((bh, nchunk, 2 * CH, DH), BF16), ((bh, nchunk, 3 * CH, CH), BF16),
              ((bh, nchunk, 3 * CH, CH), BF16), ((bh, nchunk, DH, CH), BF16), ((bh, nchunk, CH, DH), BF16),
              ((bh, nchunk, CH, CH), F32), ((bh, nchunk, 1, DH), F32)]
    return pl.pallas_call(
        body, name="gdn_local_fwd", grid=(bh, ngrp),
        in_specs=[head(0), head(GW), head(2 * GW), lane_b, lane_b, per_chunk(1, CH)],
        out_specs=[head(0), per_chunk(2 * CH, DH), per_chunk(3 * CH, CH), per_chunk(3 * CH, CH), per_chunk(DH, CH),
                   per_chunk(CH, DH), per_chunk(CH, CH), per_chunk(1, DH)],
        out_shape=[jax.ShapeDtypeStruct(s, dt) for s, dt in shapes],
        compiler_params=_cparams(("parallel", "parallel")),
    )(qkv, qkv, qkv, gcb, bb, gct)


def _scan_specs(bsz, lp, grp, order):
    ngrp = lp // (grp * CH)
    bh = bsz * HEADS
    seq = lambda: pl.BlockSpec((bsz, grp * CH, GW), lambda s: (0, order(s, ngrp), 0))
    per_chunk = lambda r, c: pl.BlockSpec((bh, grp, r, c), lambda s: (0, order(s, ngrp), 0, 0))
    return ngrp, bh, seq, per_chunk


def _gdn_scan_fwd(u, l1, l2, gl, bsz, lp):
    nchunk = lp // CH
    grp = _chunk_group(nchunk)
    ngrp, bh, seq, per_chunk = _scan_specs(bsz, lp, grp, lambda s, n: s)

    def body(u_ref, l1_ref, l2_ref, gl_ref, o_ref, vn_ref, st_ref, s_scr):
        @pl.when(pl.program_id(0) == 0)
        def _():
            s_scr[...] = jnp.zeros_like(s_scr)

        for c in range(grp):
            rows = slice(c * CH, (c + 1) * CH)
            for i in range(bh):
                b, h = divmod(i, HEADS)
                cols = slice(h * DH, (h + 1) * DH)
                s = s_scr[i]
                st_ref[i, c] = s
                x = jnp.dot(l1_ref[i, c], s.astype(BF16), preferred_element_type=F32)
                vnew = u_ref[b, rows, cols] - x[:CH]
                y = jnp.dot(l2_ref[i, c], vnew.astype(BF16), preferred_element_type=F32)
                o_ref[b, rows, cols] = x[CH:] + y[:CH]
                vn_ref[b, rows, cols] = vnew.astype(BF16)
                s_scr[i] = s * gl_ref[i, c] + y[CH:]

    return pl.pallas_call(
        body, name="gdn_scan_fwd", grid=(ngrp,),
        in_specs=[seq(), per_chunk(2 * CH, DH), per_chunk(3 * CH, CH), per_chunk(1, DH)],
        out_specs=[seq(), seq(), per_chunk(DH, DH)],
        out_shape=[jax.ShapeDtypeStruct((bsz, lp, GW), F32), jax.ShapeDtypeStruct((bsz, lp, GW), BF16),
                   jax.ShapeDtypeStruct((bh, nchunk, DH, DH), F32)],
        scratch_shapes=[pltpu.VMEM((bh, DH, DH), F32)],
        compiler_params=_cparams(("arbitrary",)),
    )(u.reshape(bsz, lp, GW), l1, l2, gl)


def _gdn_scan_bwd(d_o, l3, wt, kd, gl, bsz, lp):
    nchunk = lp // CH
    grp = _chunk_group(nchunk)
    ngrp, bh, seq, per_chunk = _scan_specs(bsz, lp, grp, lambda s, n: n - 1 - s)

    def body(do_ref, l3_ref, wt_ref, kd_ref, gl_ref, dvn_ref, dst_ref, ds_scr):
        @pl.when(pl.program_id(0) == 0)
        def _():
            ds_scr[...] = jnp.zeros_like(ds_scr)

        for c in reversed(range(grp)):
            rows = slice(c * CH, (c + 1) * CH)
            for i in range(bh):
                b, h = divmod(i, HEADS)
                cols = slice(h * DH, (h + 1) * DH)
                ds = ds_scr[i]
                dst_ref[i, c] = ds
                p = jnp.dot(l3_ref[i, c], do_ref[b, rows, cols].astype(BF16), preferred_element_type=F32)
                dvn = p[:CH] + jnp.dot(kd_ref[i, c], ds.astype(BF16), preferred_element_type=F32)
                dvn_ref[b, rows, cols] = dvn.astype(BF16)
                ds_scr[i] = ds * gl_ref[i, c] + p[CH:] - jnp.dot(wt_ref[i, c], dvn.astype(BF16),
                                                                 preferred_element_type=F32)

    return pl.pallas_call(
        body, name="gdn_scan_bwd", grid=(ngrp,),
        in_specs=[seq(), per_chunk(3 * CH, CH), per_chunk(DH, CH), per_chunk(CH, DH), per_chunk(1, DH)],
        out_specs=[seq(), per_chunk(DH, DH)],
        out_shape=[jax.ShapeDtypeStruct((bsz, lp, GW), BF16), jax.ShapeDtypeStruct((bh, nchunk, DH, DH), F32)],
        scratch_shapes=[pltpu.VMEM((bh, DH, DH), F32)],
        compiler_params=_cparams(("arbitrary",)),
    )(d_o.reshape(bsz, lp, GW), l3, wt, kd, gl)


def _gdn_local_bwd(qkv, gcb, bb, gct, tinv_all, states, dstates, vnew, dvnew, d_o, bsz, lp):
    nchunk = lp // CH
    grp = _local_group(nchunk)
    ngrp, head, lane_b, per_chunk = _gdn_local_specs(lp, grp)
    bh = bsz * HEADS

    def body(q_ref, k_ref, v_ref, gc_ref, be_ref, gt_ref, ti_ref, st_ref, dst_ref, vn_ref, dvn_ref, do_ref,
             dq_ref, dk_ref, dv_ref, dg_ref, db_ref):
        incl, strict = _tri(CH, "incl"), _tri(CH, "strict")
        upper = _tri(CH, "upper").astype(F32)
        ones = jnp.ones((CH, DH), F32)
        last = lax.broadcasted_iota(jnp.int32, (CH, 1), 0) == CH - 1

        def rsum(x):
            return jnp.sum(x, axis=-1, keepdims=True)

        def mul(xs, ys):
            return _each(lambda x, y: x * y, xs, ys)

        rows, be, dec, _, eg, ekd, gl = zip(*[_chunk_gates(gc_ref, be_ref, gt_ref, g) for g in range(grp)])
        q = [q_ref[r, :] for r in rows]
        k = [k_ref[r, :] for r in rows]
        v = [v_ref[r, :] for r in rows]
        vnew = [vn_ref[r, :] for r in rows]
        dvn = [dvn_ref[r, :] for r in rows]
        do = [do_ref[r, :] for r in rows]
        tinv = [ti_ref[0, g] for g in range(grp)]
        s = [st_ref[0, g] for g in range(grp)]
        ds = [dst_ref[0, g] for g in range(grp)]
        kb, vb, qd, kd = mul(k, be), mul(v, be), mul(q, eg), mul(k, ekd)
        kbg = mul(kb, eg)
        a = _each(lambda x, y, d: jnp.where(strict, _bdot(x, y, NT) * d, 0.0), kb, k, dec)
        qk = _each(lambda x, y, d: jnp.where(incl, _bdot(x, y, NT) * d, 0.0), q, k, dec)
        dqk = _each(lambda x, y: jnp.where(incl, _bdot(x, y, NT), 0.0), do, vnew)
        dqd = _each(lambda x, y: _bdot(x, y, NT), do, s)
        dkd = _each(lambda x, y: _bdot(x, y, NT), vnew, ds)
        dgl = _each(lambda x, y: jnp.sum(rsum(x * y), axis=0, keepdims=True), s, ds)
        dw = _each(lambda x, y: -_bdot(x, y, NT), dvn, s)
        dvb = _each(lambda t, x: _bdot(t, x, TN), tinv, dvn)
        dkbg = _each(lambda t, x: _bdot(t, x, TN), tinv, dw)
        dt = _each(lambda x, y, z, w: _bdot(x, y, NT) + _bdot(z, w, NT), dvn, vb, dw, kbg)
        tdt = _each(lambda t, x: _split_dot(t, x, TN), tinv, dt)
        da = _each(lambda x, t: jnp.where(strict, -_split_dot(x, t, NT), 0.0), tdt, tinv)
        dp, dr = mul(da, dec), mul(dqk, dec)
        dkb = _each(lambda x, y, z, e: _bdot(x, y, NN) + z * e, dp, k, dkbg, eg)
        dk = _each(lambda p_, kb_, r_, q_, dkd_, ekd_, dkb_, be_:
                   _bdot(p_, kb_, TN) + _bdot(r_, q_, TN) + dkd_ * ekd_ + dkb_ * be_,
                   dp, kb, dr, q, dkd, ekd, dkb, be)
        dq = _each(lambda r_, k_, dqd_, eg_: _bdot(r_, k_, NN) + dqd_ * eg_, dr, k, dqd, eg)
        m = _each(lambda da_, a_, dqk_, qk_: da_ * a_ + dqk_ * qk_, da, a, dqk, qk)
        m_cols = _each(lambda x: _split_dot(x, ones, TN), m)
        kd_term = _each(lambda x, y: rsum(x * y), dkd, kd)
        dgc = _each(lambda m_, mc, dqd_, qd_, kt, dkbg_, kbg_, dgl_, gl_:
                    rsum(m_) - mc + rsum(dqd_ * qd_) - kt + rsum(dkbg_ * kbg_)
                    + jnp.where(last, jnp.sum(kt, axis=0, keepdims=True) + dgl_ * gl_, 0.0),
                    m, m_cols, dqd, qd, kd_term, dkbg, kbg, dgl, gl)
        dg = _each(lambda x: _split_dot(upper, x, NN), dgc)
        for g in range(grp):
            dq_ref[rows[g], :] = dq[g]
            dk_ref[rows[g], :] = dk[g]
            dv_ref[rows[g], :] = dvb[g] * be[g]
            dg_ref[0, rows[g], :] = dg[g]
            db_ref[0, rows[g], :] = jnp.broadcast_to(rsum(dkb[g] * k[g]) + rsum(dvb[g] * v[g]), (CH, DH))

    r = bsz * lp
    return pl.pallas_call(
        body, name="gdn_local_bwd", grid=(bh, ngrp),
        in_specs=[head(0), head(GW), head(2 * GW), lane_b, lane_b, per_chunk(1, CH), per_chunk(CH, CH),
                  per_chunk(DH, DH), per_chunk(DH, DH), head(0), head(0), head(0)],
        out_specs=[head(0), head(0), head(0), lane_b, lane_b],
        out_shape=[jax.ShapeDtypeStruct((r, GW), F32)] * 3 + [jax.ShapeDtypeStruct((bh, lp, DH), F32)] * 2,
        compiler_params=_cparams(("parallel", "parallel")),
    )(qkv, qkv, qkv, gcb, bb, gct, tinv_all, states, dstates, vnew, dvnew, d_o)


def _place():
    return lax.axis_index("x"), lax.axis_index("y"), lax.axis_index("c")


def _any_specs(n):
    return [pl.BlockSpec(memory_space=pl.ANY)] * n


def _all_gather(name, shards):
    n = len(shards)

    def body(*refs):
        ins, outs = refs[:n], refs[n:2 * n]
        send_sems, recv_sems, local_sems = refs[2 * n:2 * n + 3]
        stage = refs[2 * n + 3:]
        x, y, c = _place()
        me, sibling = (x, y, c), (x, y, 1 - c)
        chips = [(1 - x, y), (x, 1 - y), (1 - x, 1 - y)]

        def slot(i, p):
            return outs[i].at[4 * p[0] + 2 * p[1] + p[2]]

        def copy(i, k, block, to, src=None):
            return pltpu.make_async_remote_copy(
                src_ref=slot(i, block) if src is None else src, dst_ref=slot(i, block),
                send_sem=send_sems.at[i, k], recv_sem=recv_sems.at[i, k], device_id=to, device_id_type=MESH)

        load = [pltpu.make_async_copy(ins[i], stage[i], local_sems.at[i]) for i in range(n)]
        for cp in load:
            cp.start()
        first = []
        for i in range(n):
            first.append(copy(i, 0, me, sibling, src=ins[i]))
            first += [copy(i, 1 + j, me, (*chip, c), src=ins[i]) for j, chip in enumerate(chips)]
        for cp in first:
            cp.start()
        for cp in load:
            cp.wait()
        mine = [pltpu.make_async_copy(stage[i], slot(i, me), local_sems.at[i]) for i in range(n)]
        for cp in mine:
            cp.start()
        passed = []
        for j, chip in enumerate(chips):
            for i in range(n):
                copy(i, 1 + j, (*chip, c), me).wait_recv()
                fwd = copy(i, 4 + j, (*chip, c), sibling)
                fwd.start()
                passed.append(fwd)
        for i in range(n):
            copy(i, 0, sibling, me).wait_recv()
            for j, chip in enumerate(chips):
                copy(i, 4 + j, (*chip, 1 - c), me).wait_recv()
        for cp in first + passed:
            cp.wait_send()
        for cp in mine:
            cp.wait()

    return pl.pallas_call(
        body, name=name,
        in_specs=_any_specs(n), out_specs=_any_specs(n),
        out_shape=[jax.ShapeDtypeStruct((N_DEV,) + s.shape, s.dtype) for s in shards],
        scratch_shapes=[pltpu.SemaphoreType.DMA((n, 7)), pltpu.SemaphoreType.DMA((n, 7)),
                        pltpu.SemaphoreType.DMA((n,))] + [pltpu.VMEM(s.shape, s.dtype) for s in shards],
    )(*shards)


HBM_SPEC = pl.BlockSpec(memory_space=pltpu.HBM)
SEM_SPEC = pl.BlockSpec(memory_space=pltpu.SEMAPHORE)
EFFECT = pltpu.SideEffectType.DATAFLOW_SIDE_EFFECTING
N_PEER = N_DEV - 1


def _peer(r):
    x, y, c = _place()
    return ((1 - x) if r & 4 else x, (1 - y) if r & 2 else y, (1 - c) if r & 1 else c)


def _slot_of(p):
    return 4 * p[0] + 2 * p[1] + p[2]


def _hbm(a):
    return pltpu.with_memory_space_constraint(a, pltpu.HBM)


def _gather_pattern(src, land):
    me = _place()
    return [(src, land.at[_slot_of(me)], _peer(r), land.at[_slot_of(_peer(r))]) for r in range(1, N_DEV)]


def _chipcast_pattern(src, land):
    x, y, c = _place()
    me = (x, y, c)
    peers = [(x, y, 1 - c), (1 - x, y, c), (x, 1 - y, c), (1 - x, 1 - y, c)]
    return [(src, land.at[_slot_of(me)], p, land.at[_slot_of(p)]) for p in peers]


def _forward_pattern(land, _):
    x, y, c = _place()
    chips = [(1 - x, y), (x, 1 - y), (1 - x, 1 - y)]
    return [(land.at[_slot_of((*ch, c))], land.at[_slot_of((*ch, c))], (x, y, 1 - c), land.at[_slot_of((*ch, 1 - c))])
            for ch in chips]


def _sibling_pattern(src, land):
    x, y, c = _place()
    return [(src.at[ch, 1 - c], land.at[ch], (x, y, 1 - c), land.at[ch]) for ch in range(4)]


def _chip_pattern(src, land):
    x, y, c = _place()
    chips = [(1 - x, y), (x, 1 - y), (1 - x, 1 - y)]
    return [(src.at[2 * ch[0] + ch[1]], land.at[j], (*ch, c), land.at[j]) for j, ch in enumerate(chips)]


def _split_start(name, groups, pattern, land_shape, after=None):
    sizes = [len(g) for g in groups]
    arrays = [s for g in groups for s in g]
    n = len(arrays)
    inplace = land_shape is None
    fresh = [] if inplace else [lax.empty(land_shape(s.shape), s.dtype) for s in arrays]
    bufs = arrays + fresh
    order = [] if after is None else [after]

    def body(*refs):
        srcs, lands = refs[:n], (refs[:n] if inplace else refs[n:2 * n])
        first_out = len(bufs) + len(order)
        sems = refs[first_out:first_out + 2 * len(groups)]
        token = refs[-1]
        k = 0
        for gi, size in enumerate(sizes):
            for i in range(size):
                copies = pattern(srcs[k], lands[k])
                for j, (s, d, peer, _) in enumerate(copies):
                    pltpu.make_async_remote_copy(
                        src_ref=s, dst_ref=d, send_sem=sems[2 * gi].at[i * len(copies) + j],
                        recv_sem=sems[2 * gi + 1].at[i * len(copies) + j], device_id=peer, device_id_type=MESH).start()
                k += 1
        token[...] = jnp.zeros_like(token)

    per_array = {_gather_pattern: N_PEER, _chipcast_pattern: 4, _forward_pattern: 3, _sibling_pattern: 4,
                 _chip_pattern: 3}[pattern]
    sem_shapes = [pltpu.SemaphoreType.DMA((size * per_array,)) for size in sizes for _ in range(2)]
    res = pl.pallas_call(
        body, name=name,
        in_specs=[HBM_SPEC] * len(bufs) + [pl.BlockSpec(memory_space=pl.ANY)] * len(order),
        out_specs=[SEM_SPEC] * len(sem_shapes) + [HBM_SPEC] * len(bufs) + [pl.BlockSpec(memory_space=pltpu.VMEM)],
        out_shape=sem_shapes + [pltpu.HBM(s.shape, s.dtype) for s in bufs] + [jax.ShapeDtypeStruct((8, DH), F32)],
        input_output_aliases={i: len(sem_shapes) + i for i in range(len(bufs))},
        compiler_params=pltpu.CompilerParams(has_side_effects=EFFECT),
    )(*[_hbm(s) for s in bufs], *order)
    nsem = len(sem_shapes)
    thru = list(res[nsem:nsem + len(bufs)])
    out, k = [], 0
    for gi, size in enumerate(sizes):
        srcs_out = [] if inplace else thru[k:k + size]
        lands_out = thru[k:k + size] if inplace else thru[n + k:n + k + size]
        out.append((res[2 * gi], res[2 * gi + 1], srcs_out, lands_out))
        k += size
    return out, res[-1]


def _split_wait(name, group, pattern, after):
    send_sems, recv_sems, arrays, lands = group
    n = len(lands)
    inplace = not arrays
    bufs = list(arrays) + list(lands)
    fill_own = pattern in (_gather_pattern, _chipcast_pattern)

    def body(*refs):
        srcs, lands_ = refs[:n], (refs[:n] if inplace else refs[n:2 * n])
        s_sems, r_sems = refs[len(bufs)], refs[len(bufs) + 1]
        own = []
        if fill_own:
            stage, local_sems = refs[-n - 1:-1], refs[-1]
            load = [pltpu.make_async_copy(srcs[i], stage[i], local_sems.at[i]) for i in range(n)]
            for cp in load:
                cp.start()
            for cp in load:
                cp.wait()
            own = [pltpu.make_async_copy(stage[i], lands_[i].at[_slot_of(_place())], local_sems.at[i])
                   for i in range(n)]
            for cp in own:
                cp.start()
        for i in range(n):
            copies = pattern(srcs[i], lands_[i])
            for j, (s, _, peer, filled) in enumerate(copies):
                cp = pltpu.make_async_remote_copy(
                    src_ref=s, dst_ref=filled, send_sem=s_sems.at[i * len(copies) + j],
                    recv_sem=r_sems.at[i * len(copies) + j], device_id=peer, device_id_type=MESH)
                cp.wait_send()
                cp.wait_recv()
        for cp in own:
            cp.wait()

    res = pl.pallas_call(
        body, name=name,
        in_specs=[HBM_SPEC] * len(bufs) + [SEM_SPEC, SEM_SPEC, pl.BlockSpec(memory_space=pl.ANY)],
        out_specs=[HBM_SPEC] * len(bufs),
        out_shape=[pltpu.HBM(s.shape, s.dtype) for s in bufs],
        scratch_shapes=([pltpu.VMEM(a.shape, a.dtype) for a in arrays] + [pltpu.SemaphoreType.DMA((n,))])
        if fill_own else [],
        input_output_aliases={i: i for i in range(len(bufs))},
        compiler_params=pltpu.CompilerParams(has_side_effects=EFFECT),
    )(*bufs, send_sems, recv_sems, after)
    return ([], list(res)) if inplace else (list(res[:n]), list(res[n:]))


def _pair_sum(name, grads, recv, core):
    _, _, r, c = grads.shape
    tr = _div_tile(r, 256, 16) if r % 16 == 0 else r

    def body(core_ref, g_ref, r_ref, o_ref):
        o_ref[...] = (g_ref[0].astype(F32) + r_ref[...].astype(F32)).astype(o_ref.dtype)

    return pl.pallas_call(
        body, name=name,
        grid_spec=pltpu.PrefetchScalarGridSpec(
            num_scalar_prefetch=1, grid=(4, r // tr),
            in_specs=[pl.BlockSpec((1, 1, tr, c), lambda s, i, cr: (s, cr[0], i, 0)),
                      pl.BlockSpec((1, tr, c), lambda s, i, cr: (s, i, 0))],
            out_specs=pl.BlockSpec((1, tr, c), lambda s, i, cr: (s, i, 0))),
        out_shape=jax.ShapeDtypeStruct(recv.shape, recv.dtype),
        compiler_params=_cparams(("parallel", "parallel")),
    )(core, grads, recv)


def _adamw_math(w, g, m, v):
    m = ADAM_B1 * m + (1.0 - ADAM_B1) * g
    v = ADAM_B2 * v + (1.0 - ADAM_B2) * (g * g)
    m_hat = m / (1.0 - ADAM_B1 ** ADAM_STEP)
    v_hat = v / (1.0 - ADAM_B2 ** ADAM_STEP)
    delta = -ADAM_LR * (m_hat / (jnp.sqrt(v_hat) + ADAM_EPS) + ADAM_WD * w)
    return delta, m, v


SMALL_LEAVES = (
    ("mix_pre_norm", 0, 1, 0, D, (1, D)), ("mix_post_norm", 1, 1, 0, D, (1, D)),
    ("ffn_pre_norm", 2, 1, 0, D, (1, D)), ("ffn_post_norm", 3, 1, 0, D, (1, D)),
    ("meta_tokens", 4, N_META, 0, DH, (N_META, DH)),
    ("conv_qkv", 20, KQ, 0, 3 * GW // N_DEV, (1, KQ, 3 * GW // N_DEV)),
    ("conv_sc", 24, KS, 0, SW // N_DEV, (1, KS, SW // N_DEV)),
    ("a_log", 27, 1, 0, HEADS, (1, HEADS)), ("dt_bias", 27, 1, DH, HEADS, (1, HEADS)),
    ("gdn_norm", 27, 1, 2 * DH, DH, (1, DH)),
)


def _adamw_small(slabs, w, m, v):
    nslot = slabs.shape[0]
    nleaf = len(SMALL_LEAVES)

    def body(p_ref, w_ref, m_ref, v_ref, *refs):
        outs, loss_ref, scr = refs[:4 * nleaf], refs[4 * nleaf], refs[4 * nleaf + 1]
        g = p_ref[0]
        for s in range(1, nslot):
            g = g + p_ref[s]
        d, mn, vn = _adamw_math(w_ref[...], g, m_ref[...], v_ref[...])
        for kind, val in enumerate((g, d, mn, vn)):
            scr[kind] = val
        for kind in range(4):
            for li, (_, r0, nr, c0, nc, shape) in enumerate(SMALL_LEAVES):
                val = scr[kind, pl.ds(r0, nr), pl.ds(c0, nc)]
                o = outs[kind * nleaf + li]
                if len(shape) == 3:
                    o[0] = val
                else:
                    o[...] = val
        loss_ref[...] = scr[0, pl.ds(27, 1), pl.ds(3 * DH, 1)]

    whole = lambda a: pl.BlockSpec(a.shape, lambda: (0,) * a.ndim)
    out_shapes = [jax.ShapeDtypeStruct(shape, F32) for _ in range(4) for (*_, shape) in SMALL_LEAVES]
    out_shapes.append(jax.ShapeDtypeStruct((1, 1), F32))
    res = pl.pallas_call(
        body, name="adamw_small",
        in_specs=[whole(slabs), whole(w), whole(m), whole(v)],
        out_specs=[pl.BlockSpec(s.shape, lambda n=len(s.shape): (0,) * n) for s in out_shapes],
        out_shape=out_shapes,
        scratch_shapes=[pltpu.VMEM((4,) + w.shape, F32)],
        compiler_params=pltpu.CompilerParams(vmem_limit_bytes=VMEM_LIMIT),
    )(slabs, w, m, v)
    kinds = [{nm: res[kind * nleaf + li] for li, (nm, *_) in enumerate(SMALL_LEAVES)} for kind in range(4)]
    return kinds, res[-1]


def _adamw_reduced(name, parts, chip, recv, w, m, v, after=None):
    _, r, c = parts.shape
    tr = _div_tile(r, 256, 16) if r % 16 == 0 else r
    order = [] if after is None else [after]

    def body(chip_ref, p_ref, r_ref, w_ref, m_ref, v_ref, *rest):
        g_out, d_out, m_out, v_out = rest[len(order):]
        g = p_ref[0].astype(F32)
        for s in range(3):
            g = g + r_ref[s].astype(F32)
        d, mn, vn = _adamw_math(w_ref[...], g, m_ref[...], v_ref[...])
        g_out[...] = g
        d_out[...] = d
        m_out[...] = mn
        v_out[...] = vn

    blk = pl.BlockSpec((tr, c), lambda i, ch: (i, 0))
    return pl.pallas_call(
        body, name=name,
        grid_spec=pltpu.PrefetchScalarGridSpec(
            num_scalar_prefetch=1, grid=(r // tr,),
            in_specs=[pl.BlockSpec((1, tr, c), lambda i, ch: (ch[0], i, 0)),
                      pl.BlockSpec((3, tr, c), lambda i, ch: (0, i, 0)), blk, blk, blk]
            + [pl.BlockSpec((8, DH), lambda i, ch: (0, 0))] * len(order),
            out_specs=[blk] * 4),
        out_shape=[jax.ShapeDtypeStruct((r, c), F32)] * 4,
        compiler_params=_cparams(("parallel",)),
    )(chip, parts, recv, w, m, v, *order)


SMALL_ROWS, SMALL_COLS = 32, 3 * GW


def _pack_small(gains4, meta, conv_qkv, conv_sc, a_log, dt_bias, gdn_norm, extra=None):
    def padc(a):
        return jnp.pad(a, ((0, 0), (0, SMALL_COLS - a.shape[1])))
    scal = jnp.concatenate([jnp.pad(a_log, ((0, 0), (0, DH - HEADS))), jnp.pad(dt_bias, ((0, 0), (0, DH - HEADS))),
                            gdn_norm] + ([] if extra is None else [extra]), axis=1)
    rows = [padc(g) for g in gains4] + [padc(meta), padc(conv_qkv), padc(conv_sc), padc(scal)]
    slab = jnp.concatenate(rows, axis=0)
    return jnp.pad(slab, ((0, SMALL_ROWS - slab.shape[0]), (0, 0)))


def _unpack_small(slab):
    gains = [slab[i:i + 1, :D] for i in range(4)]
    meta = slab[4:20, :D]
    conv_qkv = slab[20:24, :]
    conv_sc = slab[24:27, :SW]
    a_log = slab[27:28, 0:HEADS]
    dt_bias = slab[27:28, DH:DH + HEADS]
    gdn_norm = slab[27:28, 2 * DH:3 * DH]
    return gains, meta, conv_qkv, conv_sc, a_log, dt_bias, gdn_norm


def kernel(x, meta_tokens, mix_pre_norm, mix_post_norm, ffn_pre_norm, ffn_post_norm, w_in, conv_qkv, a_log, dt_bias, gdn_norm, conv_sc, w_out, w_gate, w_up, w_down, loss_target, m_meta_tokens, m_mix_pre_norm, m_mix_post_norm, m_ffn_pre_norm, m_ffn_post_norm, m_w_in, m_conv_qkv, m_a_log, m_dt_bias, m_gdn_norm, m_conv_sc, m_w_out, m_w_gate, m_w_up, m_w_down, v_meta_tokens, v_mix_pre_norm, v_mix_post_norm, v_ffn_pre_norm, v_ffn_post_norm, v_w_in, v_conv_qkv, v_a_log, v_dt_bias, v_gdn_norm, v_conv_sc, v_w_out, v_w_gate, v_w_up, v_w_down):
    bsz, seq, _ = x.shape
    lp = -(-(N_META + seq) // CH) * CH
    padf = lp - N_META - seq
    first_real = padf + N_META
    rows = bsz * lp
    nchunk = lp // CH
    tr = _div_tile(lp, 528)
    tm = _div_tile(rows, 1056)
    tm_big = _div_tile(rows, 2112)
    cx, cy, cc = _place()
    dev = 4 * cx + 2 * cy + cc

    small_shard = jnp.concatenate([
        jnp.pad(meta_tokens, ((0, 0), (0, 256 - DH))),
        jnp.pad(conv_qkv[0], ((0, 0), (0, 256 - 3 * GW // N_DEV))),
        jnp.pad(conv_sc[0], ((0, 5), (0, 256 - SW // N_DEV)))], axis=0)
    g_in, g_small = _all_gather("gather_w_in", [w_in[0].T.astype(BF16), small_shard])
    my_rest = [w_out[0].astype(BF16), w_gate[0].T.astype(BF16), w_up[0].T.astype(BF16), w_down[0].astype(BF16)]
    (grp_rest,), token = _split_start("gather_rest_start", [my_rest], _chipcast_pattern, lambda s: (N_DEV,) + s,
                                      after=g_in)
    after_start = token[0:1, 0:1]

    def cols_full(g):
        return jnp.transpose(g, (1, 0, 2)).reshape(g.shape[1], -1)

    win_t = g_in.reshape(IN_W, D)
    win_t = jnp.concatenate([win_t[:BA0 + 2 * HEADS], jnp.zeros((BA_W - 2 * HEADS, D), BF16),
                             win_t[BA0 + 2 * HEADS:]], axis=0)
    meta_full = cols_full(g_small[:, 0:16, :DH])
    convq_full = cols_full(g_small[:, 16:20, :3 * GW // N_DEV])
    convs_full = cols_full(g_small[:, 20:23, :SW // N_DEV])
    alog_v = jnp.pad(a_log, ((0, 0), (HEADS, DH - 2 * HEADS)))
    dtb_v = jnp.pad(dt_bias, ((0, 0), (HEADS, DH - 2 * HEADS)))

    head_rows = jnp.concatenate([jnp.zeros((padf, D), F32), meta_full], axis=0)
    h0, u1 = _pre_norm(x, head_rows, mix_pre_norm + after_start, bsz, lp, tr)
    (proj,) = _mm("proj_in", [(u1, win_t)], "nt", tm_big, 768, D, out_dtypes=(BF16,))
    (ba,) = _mm("proj_gate_logits", [(u1, win_t[BA0:SX0])], "nt", tm, BA_W, D)
    qkv = _gdn_prep(proj, convq_full, bsz, lp, padf)
    gb, gcs = _gates(ba, alog_v, dtb_v, bsz, lp, padf)

    def lane_bcast(cols):
        t = jnp.transpose(cols.reshape(bsz, lp, HEADS), (0, 2, 1)).reshape(bsz * HEADS, lp, 1)
        return jnp.broadcast_to(t, (bsz * HEADS, lp, DH))

    bb = lane_bcast(gb[:, 0:HEADS])
    gcb = lane_bcast(gcs[:, HEADS:2 * HEADS])
    gct = jnp.transpose(gcs[:, HEADS:2 * HEADS].reshape(bsz, nchunk, CH, HEADS), (0, 3, 1, 2)).reshape(
        bsz * HEADS, nchunk, 1, CH)
    u_loc, l1, l2, l3, w_t, k_dec, tinv_all, g_last = _gdn_local_fwd(qkv, gcb, bb, gct, bsz, lp)
    o, v_new, states = _gdn_scan_fwd(u_loc, l1, l2, g_last, bsz, lp)
    o, v_new = o.reshape(rows, GW), v_new.reshape(rows, GW)
    _, l_rest = _split_wait("gather_rest_arrive", grp_rest, _chipcast_pattern, o)
    (grp_out, grp_ffn, grp_down), token = _split_start(
        "gather_rest_forward", [l_rest[0:1], l_rest[1:3], l_rest[3:4]], _forward_pattern, None)
    mix_g, mix_s = _mix_heads(proj, o, gdn_norm + token[0:1, :], convs_full, bsz, lp, padf)
    mixin = jnp.concatenate([mix_g, mix_s], axis=1)
    _, (l_out,) = _split_wait("gather_w_out_wait", grp_out, _forward_pattern, mixin)
    wout = l_out.reshape(D, D)
    (mix,) = _mm("proj_out", [(mixin, wout)], "nn", tm, 512, D)
    h1, u2 = _mid_norms(h0, mix, mix_post_norm, ffn_pre_norm, bsz, lp, tr)
    _, (l_gate, l_up) = _split_wait("gather_ffn_wait", grp_ffn, _forward_pattern, u2)
    wgate_t, wup_t = l_gate.reshape(FF, D), l_up.reshape(FF, D)
    gate, up, act = _ffn_up(u2, wgate_t, wup_t, _div_tile(rows, 528), 1408)
    _, (l_down,) = _split_wait("gather_down_wait", grp_down, _forward_pattern, act)
    wdown = l_down.reshape(FF, D)
    (ffn,) = _mm("ffn_down", [(act, wdown)], "nn", tm, 512, 1408)

    dy, dffn, d_g4, loss_part = _loss_head(h1, ffn, loss_target, ffn_post_norm, bsz, lp, tr, first_real)
    (dwdown,) = _mm("dw_down", [(act, dffn)], "tn", 1408, 512, tm, out_dtypes=(BF16,))

    def swiglu_bwd(da, g, u):
        g, u = g.astype(F32), u.astype(F32)
        return da * u * _dsilu(g), da * _silu(g)

    dgate, dup = _mm("d_act", [(dffn, wdown)], "nt", tm, 1408, D, out_dtypes=(BF16, BF16),
                     epilogue=swiglu_bwd, extras=(gate, up))
    (dwgate_t,) = _mm("dw_gate", [(dgate, u2)], "tn", 1408, D, tm, out_dtypes=(BF16,))
    (dwup_t,) = _mm("dw_up", [(dup, u2)], "tn", 1408, D, tm, out_dtypes=(BF16,))
    (du2,) = _mm("d_u2", [(dgate, wgate_t), (dup, wup_t)], "nn", tm, D, 1408)
    dh1, dmix, d_g3, d_g2 = _mid_norms_bwd(h1, mix, du2, dy, ffn_pre_norm, mix_post_norm, bsz, lp, tr)
    (dwout,) = _mm("dw_out", [(mixin, dmix)], "tn", D, 512, tm, out_dtypes=(BF16,))

    def row_blocks(g):
        return g.reshape(4, 2, -1, g.shape[1])

    core = jnp.reshape(cc, (1,)).astype(jnp.int32)
    chip = jnp.reshape(2 * cx + cy, (1,)).astype(jnp.int32)

    def reduce_start(tag, grads):
        (grp,), tok = _split_start("rs_sibling_start_" + tag, [grads], _sibling_pattern, lambda s: s[:1] + s[2:])
        return grp, tok

    def reduce_mid(tag, grp, nms, after):
        mine, lands = _split_wait("rs_sibling_wait_" + tag, grp, _sibling_pattern, after)
        parts = [_pair_sum("pair_sum_" + nm, g, r, core) for nm, g, r in zip(nms, mine, lands)]
        (grp2,), tok = _split_start("rs_chip_start_" + tag, [parts], _chip_pattern, lambda s: (3,) + s[1:])
        return grp2, tok

    def reduce_end(tag, grp2, after):
        return _split_wait("rs_chip_wait_" + tag, grp2, _chip_pattern, after)

    names_ffn = ["w_down", "w_gate", "w_up", "w_out"]
    rs_ffn, tok = reduce_start("ffn", [row_blocks(dwdown), row_blocks(dwgate_t), row_blocks(dwup_t), row_blocks(dwout)])
    (dmixin,) = _mm("d_mixin", [(dmix, wout)], "nt", tm, 512, D, after=tok)
    d_o, dz, dsx, dsb, dsc, d_gn, d_convs = _mix_heads_bwd(proj, o, gdn_norm, convs_full, dmixin, bsz, lp, padf)
    rs_ffn, tok = reduce_mid("ffn", rs_ffn, names_ffn, d_o)
    dv_new, dstates = _gdn_scan_bwd(d_o, l3, w_t, k_dec, g_last + tok[0, 0], bsz, lp)
    dq, dk, dv, dg_b, dbeta_b = _gdn_local_bwd(qkv, gcb, bb, gct, tinv_all, states, dstates, v_new,
                                               dv_new.reshape(rows, GW), d_o, bsz, lp)
    dpqkv, d_convq = _gdn_prep_bwd(proj, convq_full, dq, dk, dv, bsz, lp, padf)
    dba, d_scal = _gates_bwd(ba, alog_v, dtb_v, dbeta_b, dg_b, gb, bsz, lp, padf)
    dproj = jnp.concatenate([dpqkv, dz, dba, dsx, dsb, dsc], axis=1)
    (dwin_t,) = _mm("dw_in", [(dproj, u1)], "tn", 1280, D, tm, out_dtypes=(BF16,))
    parts_ffn, recv_ffn = reduce_end("ffn", rs_ffn, dwin_t)
    dwin_t = jnp.concatenate([dwin_t[:BA0 + 2 * HEADS], dwin_t[SX0:]], axis=0)
    rs_in, tok = reduce_start("in", [row_blocks(dwin_t)])

    big_w = {"w_in": (w_in, m_w_in, v_w_in), "w_out": (w_out, m_w_out, v_w_out), "w_gate": (w_gate, m_w_gate, v_w_gate),
             "w_up": (w_up, m_w_up, v_w_up), "w_down": (w_down, m_w_down, v_w_down)}

    def update(nm, parts, recv, after=None):
        w, m, v = big_w[nm]
        if nm in ("w_out", "w_down"):
            return _adamw_reduced("adamw_" + nm, parts, chip, recv, w[0], m[0], v[0], after)
        res = _adamw_reduced("adamw_" + nm, parts, chip, recv, w[0].T, m[0].T, v[0].T, after)
        return [t.T for t in res]

    big_out = {nm: update(nm, p, r, tok) for nm, p, r in zip(names_ffn, parts_ffn, recv_ffn)}
    ffn_done = sum(big_out[nm][1][:1, :1] for nm in names_ffn)
    rs_in, tok = reduce_mid("in", rs_in, ["w_in"], ffn_done)
    (du1,) = _mm("d_u1", [(dproj, win_t)], "nn", tm_big, D, 768, after=tok)
    dh0, d_g1, d_meta = _pre_norm_bwd(h0, du1, dh1, mix_pre_norm, bsz, lp, tr, padf)
    grad_x = dh0.reshape(bsz, lp, D)[:, first_real:]

    d_alog = d_scal[0:1, HEADS:2 * HEADS]
    d_dtb = d_scal[1:2, HEADS:2 * HEADS]
    slab = _pack_small([d_g1, d_g2, d_g3, d_g4], d_meta, d_convq, d_convs, d_alog, d_dtb, d_gn, loss_part)
    (grp_small,), _ = _split_start("small_grads_start", [[slab]], _gather_pattern, lambda s: (N_DEV,) + s)
    _, (slabs,) = _split_wait("small_grads_wait", grp_small, _gather_pattern, grad_x)
    off_meta = dev * DH
    off_cq = dev * (3 * GW // N_DEV)
    off_cs = dev * (SW // N_DEV)

    def shard_of(s):
        gains, meta, cq, cs, al, db, gn = _unpack_small(s)
        return gains, lax.dynamic_slice_in_dim(meta, off_meta, DH, 1), \
            lax.dynamic_slice_in_dim(cq, off_cq, 3 * GW // N_DEV, 1), \
            lax.dynamic_slice_in_dim(cs, off_cs, SW // N_DEV, 1), al, db, gn, s[27:28, 3 * DH:4 * DH]

    slabs_local = jax.vmap(lambda s: _pack_small(*shard_of(s)))(slabs)
    w_small = _pack_small([mix_pre_norm, mix_post_norm, ffn_pre_norm, ffn_post_norm], meta_tokens, conv_qkv[0],
                          conv_sc[0], a_log, dt_bias, gdn_norm)
    m_small = _pack_small([m_mix_pre_norm, m_mix_post_norm, m_ffn_pre_norm, m_ffn_post_norm], m_meta_tokens,
                          m_conv_qkv[0], m_conv_sc[0], m_a_log, m_dt_bias, m_gdn_norm)
    v_small = _pack_small([v_mix_pre_norm, v_mix_post_norm, v_ffn_pre_norm, v_ffn_post_norm], v_meta_tokens,
                          v_conv_qkv[0], v_conv_sc[0], v_a_log, v_dt_bias, v_gdn_norm)
    small_out, loss = _adamw_small(slabs_local, w_small, m_small, v_small)
    parts_in, recv_in = reduce_end("in", rs_in, small_out[1]["gdn_norm"])
    big_out["w_in"] = update("w_in", parts_in[0], recv_in[0])
    names = ["w_in", "w_out", "w_gate", "w_up", "w_down"]

    order = ["meta_tokens", "mix_pre_norm", "mix_post_norm", "ffn_pre_norm", "ffn_post_norm", "w_in", "conv_qkv",
             "a_log", "dt_bias", "gdn_norm", "conv_sc", "w_out", "w_gate", "w_up", "w_down"]
    outs = [loss[0, 0], grad_x]
    for kind in range(4):
        leaves = dict(small_out[kind])
        for nm in names:
            leaves[nm] = big_out[nm][kind][None]
        outs += [leaves[nm] for nm in order]
    return tuple(outs)
```

```python
import functools

import jax
import jax.numpy as jnp
from jax import lax
from jax.experimental import pallas as pl
from jax.experimental.pallas import tpu as pltpu

F32 = jnp.float32
BF16 = jnp.bfloat16
HIGHEST = lax.Precision.HIGHEST
MESH = pl.DeviceIdType.MESH

D = 1024
N_META = 16
HEADS = 4
DH = 128
GW = HEADS * DH
SW = D - GW
FF = 2816
CH = 64
KQ = 4
KS = 3
EPS = 1e-6
IN_W = 3 * GW + GW + 2 * HEADS + 3 * SW
Q0, K0, V0, Z0, BA0, SX0, SB0, SC0, PW = 0, 512, 1024, 1536, 2048, 2304, 2816, 3328, 3840
BA_W = SX0 - BA0
N_DEV = 8
VMEM_LIMIT = 56 * 1024 * 1024

ADAM_LR, ADAM_B1, ADAM_B2, ADAM_EPS, ADAM_WD, ADAM_STEP = 0.001, 0.9, 0.999, 1e-08, 0.01, 10


def _cparams(sem):
    return pltpu.CompilerParams(dimension_semantics=sem, vmem_limit_bytes=VMEM_LIMIT)


def _div_tile(n, target, mult=16):
    best = None
    for t in range(mult, min(n, target) + 1, mult):
        if n % t == 0:
            best = t
    assert best is not None, (n, target)
    return best


def _silu(x):
    return x * jax.nn.sigmoid(x)


def _dsilu(x):
    s = jax.nn.sigmoid(x)
    return s * (1.0 + x * (1.0 - s))


def _bdot(a, b, dims):
    return lax.dot_general(a.astype(BF16), b.astype(BF16), (dims, ((), ())), preferred_element_type=F32)


NN = ((1,), (0,))
NT = ((1,), (1,))
TN = ((0,), (0,))


def _fdot(a, b, dims=NN):
    return lax.dot_general(a, b, (dims, ((), ())), precision=HIGHEST, preferred_element_type=F32)


def _mm(name, pairs, mode, tm, tn, tk, out_dtypes=(F32,), epilogue=None, extras=(), after=None):
    a0, b0 = pairs[0]
    if mode == "nn":
        (m, k), n = a0.shape, b0.shape[1]
    elif mode == "nt":
        (m, k), n = a0.shape, b0.shape[0]
    else:
        (k, m), n = a0.shape, b0.shape[1]
    assert m % tm == 0 and n % tn == 0 and k % tk == 0, (name, m, n, k, tm, tn, tk)
    nk = k // tk
    npair = len(pairs)
    nex = len(extras)
    nout = len(out_dtypes)
    ntok = 0 if after is None else 1
    dims = {"nn": NN, "nt": NT, "tn": TN}[mode]

    def body(*refs):
        ab = refs[:2 * npair]
        ex = refs[2 * npair:2 * npair + nex]
        outs = refs[2 * npair + nex + ntok:2 * npair + nex + ntok + nout]
        acc_ref = refs[-1]
        kk = pl.program_id(2)

        part = None
        for p in range(npair):
            d = lax.dot_general(ab[2 * p][...], ab[2 * p + 1][...], (dims, ((), ())), preferred_element_type=F32)
            part = d if part is None else part + d

        def finish(acc):
            res = epilogue(acc, *[e[...] for e in ex]) if epilogue is not None else (acc,)
            for o, r in zip(outs, res):
                o[...] = r.astype(o.dtype)

        if nk == 1:
            finish(part)
        else:
            @pl.when(kk == 0)
            def _():
                acc_ref[...] = part

            @pl.when(kk > 0)
            def _():
                acc_ref[...] += part

            @pl.when(kk == nk - 1)
            def _():
                finish(acc_ref[...])

    if mode == "nn":
        a_spec = pl.BlockSpec((tm, tk), lambda i, j, q: (i, q))
        b_spec = pl.BlockSpec((tk, tn), lambda i, j, q: (q, j))
    elif mode == "nt":
        a_spec = pl.BlockSpec((tm, tk), lambda i, j, q: (i, q))
        b_spec = pl.BlockSpec((tn, tk), lambda i, j, q: (j, q))
    else:
        a_spec = pl.BlockSpec((tk, tm), lambda i, j, q: (q, i))
        b_spec = pl.BlockSpec((tk, tn), lambda i, j, q: (q, j))
    o_spec = pl.BlockSpec((tm, tn), lambda i, j, q: (i, j))
    flat = [t for pr in pairs for t in pr]
    res = pl.pallas_call(
        body, name=name,
        grid=(m // tm, n // tn, nk),
        in_specs=[a_spec, b_spec] * npair + [o_spec] * nex + [pl.BlockSpec((8, DH), lambda i, j, q: (0, 0))] * ntok,
        out_specs=[o_spec] * nout,
        out_shape=[jax.ShapeDtypeStruct((m, n), dt) for dt in out_dtypes],
        scratch_shapes=[pltpu.VMEM((tm, tn) if nk > 1 else (8, DH), F32)],
        compiler_params=_cparams(("parallel", "parallel", "arbitrary")),
    )(*flat, *extras, *([] if after is None else [after]))
    return res


def _ffn_up(u2, wg, wu, tm, tn):
    m, k = u2.shape
    n = wg.shape[0]

    def body(a_ref, g_ref, u_ref, gate_ref, up_ref, act_ref):
        a = a_ref[...]
        g = lax.dot_general(a, g_ref[...], (NT, ((), ())), preferred_element_type=F32)
        u = lax.dot_general(a, u_ref[...], (NT, ((), ())), preferred_element_type=F32)
        gate_ref[...] = g.astype(BF16)
        up_ref[...] = u.astype(BF16)
        act_ref[...] = (_silu(g) * u).astype(BF16)

    o_spec = pl.BlockSpec((tm, tn), lambda j, i: (i, j))
    w_spec = pl.BlockSpec((tn, k), lambda j, i: (j, 0))
    return pl.pallas_call(
        body, name="ffn_up", grid=(n // tn, m // tm),
        in_specs=[pl.BlockSpec((tm, k), lambda j, i: (i, 0)), w_spec, w_spec],
        out_specs=[o_spec, o_spec, o_spec],
        out_shape=[jax.ShapeDtypeStruct((m, n), BF16)] * 3,
        compiler_params=_cparams(("parallel", "parallel")),
    )(u2, wg, wu)


def _rms(x, gain):
    r = lax.rsqrt(jnp.mean(x * x, axis=-1, keepdims=True) + EPS)
    return x * r * gain, r


def _rms_bwd(x, gain, dy):
    r = lax.rsqrt(jnp.mean(x * x, axis=-1, keepdims=True) + EPS)
    dyw = dy * gain
    dx = r * dyw - x * (r * r * r) * jnp.mean(dyw * x, axis=-1, keepdims=True)
    return dx, dy * x * r


def _row_specs(tr, width, nb):
    return pl.BlockSpec((tr, width), lambda b, j: (b * nb + j, 0))


def _vec_spec(width):
    return pl.BlockSpec((1, width), lambda b, j: (0, 0))


def _first_step(b, j):
    return jnp.logical_and(b == 0, j == 0)


def _pre_norm(x, head_rows, gain, bsz, lp, tr):
    nb = lp // tr
    first_real = head_rows.shape[0]
    assert first_real % 8 == 0 and first_real < tr
    x_spec = pl.BlockSpec((pl.Squeezed(), pl.Element(tr), pl.Element(D)),
                          lambda b, j: (b, pl.multiple_of(jnp.maximum(j * tr - first_real, 0), 8), 0))

    def body(x_ref, hd_ref, g_ref, h_ref, u_ref):
        j = pl.program_id(1)
        xt = x_ref[...]
        h = jnp.where(j == 0, jnp.concatenate([hd_ref[...], xt[:tr - first_real]], axis=0), xt)
        h_ref[...] = h
        u_ref[...] = _rms(h, g_ref[...])[0].astype(BF16)

    rs = _row_specs(tr, D, nb)
    return pl.pallas_call(
        body, name="mix_pre_norm", grid=(bsz, nb),
        in_specs=[x_spec, pl.BlockSpec((first_real, D), lambda b, j: (0, 0)), _vec_spec(D)],
        out_specs=[rs, rs],
        out_shape=[jax.ShapeDtypeStruct((bsz * lp, D), F32), jax.ShapeDtypeStruct((bsz * lp, D), BF16)],
        compiler_params=_cparams(("parallel", "parallel")),
    )(x, head_rows, gain)


def _mid_norms(h0, mix, g_post, g_pre, bsz, lp, tr):
    nb = lp // tr

    def body(h_ref, m_ref, gp_ref, gq_ref, h1_ref, u2_ref):
        h1 = h_ref[...] + _rms(m_ref[...], gp_ref[...])[0]
        h1_ref[...] = h1
        u2_ref[...] = _rms(h1, gq_ref[...])[0].astype(BF16)

    rs = _row_specs(tr, D, nb)
    return pl.pallas_call(
        body, name="mid_norms", grid=(bsz, nb),
        in_specs=[rs, rs, _vec_spec(D), _vec_spec(D)],
        out_specs=[rs, rs],
        out_shape=[jax.ShapeDtypeStruct(h0.shape, F32), jax.ShapeDtypeStruct(h0.shape, BF16)],
        compiler_params=_cparams(("parallel", "parallel")),
    )(h0, mix, g_post, g_pre)


def _loss_head(h1, ffn, tgt, g_post, bsz, lp, tr, first_real):
    nb = lp // tr
    assert first_real % 8 == 0 and first_real < tr
    tgt_spec = pl.BlockSpec((pl.Squeezed(), pl.Element(tr), pl.Element(D)),
                            lambda b, j: (b, pl.multiple_of(jnp.maximum(j * tr - first_real, 0), 8), 0))

    def body(h_ref, f_ref, t_ref, g_ref, dy_ref, dffn_ref, dg_ref, loss_ref):
        b, j = pl.program_id(0), pl.program_id(1)
        f = f_ref[...]
        g = g_ref[...]
        y = h_ref[...] + _rms(f, g)[0]
        t = j * tr + lax.broadcasted_iota(jnp.int32, (tr, 1), 0)
        tg = t_ref[...]
        tg = jnp.where(j == 0, jnp.concatenate([tg[tr - first_real:], tg[:tr - first_real]], axis=0), tg)
        err = jnp.where(t >= first_real, y - tg, 0.0)
        dy = err * (1.0 / D)
        dy_ref[...] = dy
        dx, dgc = _rms_bwd(f, g, dy)
        dffn_ref[...] = dx.astype(BF16)
        lsum = jnp.sum(jnp.sum(err * err, axis=0, keepdims=True), axis=1, keepdims=True) * (0.5 / D)

        @pl.when(_first_step(b, j))
        def _():
            dg_ref[...] = jnp.zeros_like(dg_ref)
            loss_ref[...] = jnp.zeros_like(loss_ref)

        dg_ref[...] += jnp.sum(dgc, axis=0, keepdims=True)
        loss_ref[...] += jnp.broadcast_to(lsum, loss_ref.shape)

    rs = _row_specs(tr, D, nb)
    return pl.pallas_call(
        body, name="loss_head", grid=(bsz, nb),
        in_specs=[rs, rs, tgt_spec, _vec_spec(D)],
        out_specs=[rs, rs, _vec_spec(D), _vec_spec(DH)],
        out_shape=[jax.ShapeDtypeStruct(h1.shape, F32), jax.ShapeDtypeStruct(h1.shape, BF16),
                   jax.ShapeDtypeStruct((1, D), F32), jax.ShapeDtypeStruct((1, DH), F32)],
        compiler_params=_cparams(("arbitrary", "arbitrary")),
    )(h1, ffn, tgt, g_post)


def _mid_norms_bwd(h1, mix, du2, dy, g_pre, g_post, bsz, lp, tr):
    nb = lp // tr

    def body(h_ref, m_ref, du_ref, dy_ref, gq_ref, gp_ref, dh1_ref, dmix_ref, dgq_ref, dgp_ref):
        b, j = pl.program_id(0), pl.program_id(1)
        dx, dgq = _rms_bwd(h_ref[...], gq_ref[...], du_ref[...])
        dh1 = dy_ref[...] + dx
        dh1_ref[...] = dh1
        dm, dgp = _rms_bwd(m_ref[...], gp_ref[...], dh1)
        dmix_ref[...] = dm.astype(BF16)

        @pl.when(_first_step(b, j))
        def _():
            dgq_ref[...] = jnp.zeros_like(dgq_ref)
            dgp_ref[...] = jnp.zeros_like(dgp_ref)

        dgq_ref[...] += jnp.sum(dgq, axis=0, keepdims=True)
        dgp_ref[...] += jnp.sum(dgp, axis=0, keepdims=True)

    rs = _row_specs(tr, D, nb)
    return pl.pallas_call(
        body, name="mid_norms_bwd", grid=(bsz, nb),
        in_specs=[rs, rs, rs, rs, _vec_spec(D), _vec_spec(D)],
        out_specs=[rs, rs, _vec_spec(D), _vec_spec(D)],
        out_shape=[jax.ShapeDtypeStruct(h1.shape, F32), jax.ShapeDtypeStruct(h1.shape, BF16),
                   jax.ShapeDtypeStruct((1, D), F32), jax.ShapeDtypeStruct((1, D), F32)],
        compiler_params=_cparams(("arbitrary", "arbitrary")),
    )(h1, mix, du2, dy, g_pre, g_post)


def _pre_norm_bwd(h0, du1, dh1, gain, bsz, lp, tr, padf):
    nb = lp // tr
    assert padf % 8 == 0 and padf + N_META <= tr

    def body(h_ref, du_ref, dh1_ref, g_ref, dh0_ref, dg_ref, dmeta_ref):
        b, j = pl.program_id(0), pl.program_id(1)
        dx, dgc = _rms_bwd(h_ref[...], g_ref[...], du_ref[...])
        dh0 = dh1_ref[...] + dx
        dh0_ref[...] = dh0

        @pl.when(_first_step(b, j))
        def _():
            dg_ref[...] = jnp.zeros_like(dg_ref)
            dmeta_ref[...] = jnp.zeros_like(dmeta_ref)

        dg_ref[...] += jnp.sum(dgc, axis=0, keepdims=True)

        @pl.when(j == 0)
        def _():
            dmeta_ref[...] += dh0[padf:padf + N_META, :]

    rs = _row_specs(tr, D, nb)
    return pl.pallas_call(
        body, name="pre_norm_bwd", grid=(bsz, nb),
        in_specs=[rs, rs, rs, _vec_spec(D)],
        out_specs=[rs, _vec_spec(D), pl.BlockSpec((N_META, D), lambda b, j: (0, 0))],
        out_shape=[jax.ShapeDtypeStruct(h0.shape, F32), jax.ShapeDtypeStruct((1, D), F32),
                   jax.ShapeDtypeStruct((N_META, D), F32)],
        compiler_params=_cparams(("arbitrary", "arbitrary")),
    )(h0, du1, dh1, gain)


def _shift_down(x, s, lp):
    return x if s == 0 else pltpu.roll(x, s, axis=0)


def _shift_up(x, s, lp):
    return x if s == 0 else pltpu.roll(x, lp - s, axis=0)


def _conv_fwd(x, w, taps, lp):
    y = None
    for i in range(taps):
        term = _shift_down(x, taps - 1 - i, lp) * w[i:i + 1, :]
        y = term if y is None else y + term
    return y


def _conv_bwd(x, w, dy, taps, lp):
    dx = None
    dws = []
    for i in range(taps):
        s = taps - 1 - i
        term = _shift_up(dy, s, lp) * w[i:i + 1, :]
        dx = term if dx is None else dx + term
        dws.append(jnp.sum(dy * _shift_down(x, s, lp), axis=0, keepdims=True))
    return dx, jnp.concatenate(dws, axis=0)


def _slab(lp, col0):
    return pl.BlockSpec((lp, DH), lambda b, s: (b, col0 // DH + s))


def _qkv_act(c, s):
    y = _silu(c)
    r = lax.rsqrt(jnp.sum(y * y, axis=-1, keepdims=True) + EPS)
    scale = jnp.where(s < HEADS, DH ** -0.5, 1.0)
    return y, r, scale


def _gdn_prep(proj, conv_qkv, bsz, lp, padf):
    def body(p_ref, w_ref, o_ref):
        s = pl.program_id(1)
        c = _conv_fwd(p_ref[...].astype(F32), w_ref[...], KQ, lp)
        y, r, scale = _qkv_act(c, s)
        out = jnp.where(s < 2 * HEADS, y * (r * scale), y)
        t = lax.broadcasted_iota(jnp.int32, (lp, 1), 0)
        o_ref[...] = jnp.where(t >= padf, out, 0.0)

    return pl.pallas_call(
        body, name="gdn_prep", grid=(bsz, 3 * HEADS),
        in_specs=[_slab(lp, Q0), pl.BlockSpec((KQ, DH), lambda b, s: (0, s))],
        out_specs=_slab(lp, 0),
        out_shape=jax.ShapeDtypeStruct((bsz * lp, 3 * GW), F32),
        compiler_params=_cparams(("parallel", "parallel")),
    )(proj, conv_qkv)


def _gdn_prep_bwd(proj, conv_qkv, dq, dk, dv, bsz, lp, padf):
    def body(p_ref, w_ref, dq_ref, dk_ref, dv_ref, dp_ref, dw_ref):
        b, s = pl.program_id(0), pl.program_id(1)
        x = p_ref[...].astype(F32)
        w = w_ref[...]
        c = _conv_fwd(x, w, KQ, lp)
        y, r, scale = _qkv_act(c, s)
        d = jnp.where(s < HEADS, dq_ref[...], jnp.where(s < 2 * HEADS, dk_ref[...], dv_ref[...]))
        dn = d * scale
        n = y * r
        dy_norm = r * (dn - n * jnp.sum(dn * n, axis=-1, keepdims=True))
        dyy = jnp.where(s < 2 * HEADS, dy_norm, dn)
        t = lax.broadcasted_iota(jnp.int32, (lp, 1), 0)
        dc = jnp.where(t >= padf, dyy * _dsilu(c), 0.0)
        dx, dw = _conv_bwd(x, w, dc, KQ, lp)
        dp_ref[...] = jnp.where(t >= padf, dx, 0.0).astype(BF16)

        @pl.when(jnp.logical_and(b == 0, s == 0))
        def _():
            dw_ref[...] = jnp.zeros_like(dw_ref)

        col_block = jnp.right_shift(lax.broadcasted_iota(jnp.int32, (KQ, 3 * GW), 1), DH.bit_length() - 1)
        dw_ref[...] += jnp.where(col_block == s, jnp.tile(dw, (1, 3 * HEADS)), 0.0)

    sl = lambda col0: pl.BlockSpec((lp, DH), lambda b, s: (b, col0 // DH + s))
    part = lambda k: pl.BlockSpec((lp, DH), lambda b, s: (b, jnp.clip(s - k * HEADS, 0, HEADS - 1)))
    return pl.pallas_call(
        body, name="gdn_prep_bwd", grid=(bsz, 3 * HEADS),
        in_specs=[sl(Q0), pl.BlockSpec((KQ, DH), lambda b, s: (0, s)), part(0), part(1), part(2)],
        out_specs=[sl(0), pl.BlockSpec((KQ, 3 * GW), lambda b, s: (0, 0))],
        out_shape=[jax.ShapeDtypeStruct((bsz * lp, 3 * GW), BF16), jax.ShapeDtypeStruct((KQ, 3 * GW), F32)],
        compiler_params=_cparams(("arbitrary", "arbitrary")),
    )(proj, conv_qkv, dq, dk, dv)


def _softplus(x):
    e = jnp.exp(-jnp.abs(x))
    one_e = 1.0 + e
    l1p = jnp.where(one_e == 1.0, e, jnp.log(one_e) * (e / (one_e - 1.0)))
    return jnp.maximum(x, 0.0) + l1p


def _tri(n, kind):
    i = lax.broadcasted_iota(jnp.int32, (n, n), 0)
    j = lax.broadcasted_iota(jnp.int32, (n, n), 1)
    return {"incl": i >= j, "strict": i > j, "upper": i <= j}[kind]


def _gates(proj, alog_v, dtb_v, bsz, lp, padf):
    nchunk = lp // CH

    def body(p_ref, al_ref, dt_ref, gb_ref, cs_ref):
        x = p_ref[...].astype(F32)
        lane = lax.broadcasted_iota(jnp.int32, (lp, DH), 1)
        t = lax.broadcasted_iota(jnp.int32, (lp, DH), 0)
        beta = jax.nn.sigmoid(x)
        g = -jnp.exp(al_ref[...]) * _softplus(x + dt_ref[...])
        gb = jnp.where(lane < HEADS, beta, jnp.where(lane < 2 * HEADS, g, 0.0))
        gb_ref[...] = jnp.where(t >= padf, gb, 0.0)
        ltri = _tri(CH, "incl").astype(F32)

        def step(n, carry):
            r0 = pl.multiple_of(n * CH, CH)
            cs_ref[pl.ds(r0, CH), :] = _fdot(ltri, gb_ref[pl.ds(r0, CH), :])
            return carry

        lax.fori_loop(0, nchunk, step, 0)

    blk = pl.BlockSpec((lp, DH), lambda b: (b, 0))
    vec = pl.BlockSpec((1, DH), lambda b: (0, 0))
    out = pl.BlockSpec((lp, DH), lambda b: (b, 0))
    return pl.pallas_call(
        body, name="gates", grid=(bsz,),
        in_specs=[blk, vec, vec], out_specs=[out, out],
        out_shape=[jax.ShapeDtypeStruct((bsz * lp, DH), F32)] * 2,
        compiler_params=_cparams(("parallel",)),
    )(proj, alog_v, dtb_v)


def _gates_bwd(proj, alog_v, dtb_v, dbeta_b, dg_b, gb, bsz, lp, padf):
    def body(p_ref, al_ref, dt_ref, db_ref, dg_ref, gb_ref, dl_ref, dv_ref):
        b = pl.program_id(0)
        x = p_ref[...].astype(F32)
        lane = lax.broadcasted_iota(jnp.int32, (lp, DH), 1)
        t = lax.broadcasted_iota(jnp.int32, (lp, DH), 0)
        dbeta = jnp.zeros((lp, DH), F32)
        dg = jnp.zeros((lp, DH), F32)
        for h in range(HEADS):
            dbeta = jnp.where(lane == h, db_ref[h], dbeta)
            dg = jnp.where(lane == HEADS + h, dg_ref[h], dg)
        real = t >= padf
        dbeta = jnp.where(real, dbeta, 0.0)
        dg = jnp.where(real, dg, 0.0)
        sb = jax.nn.sigmoid(x)
        neg_ea = -jnp.exp(al_ref[...])
        dsp = dg * neg_ea * jax.nn.sigmoid(x + dt_ref[...])
        dl = dbeta * sb * (1.0 - sb) + dsp
        dl_ref[...] = jnp.concatenate([dl, jnp.zeros((lp, BA_W - DH), F32)], axis=1).astype(BF16)
        dalog = jnp.sum(dg * gb_ref[...], axis=0, keepdims=True)
        ddt = jnp.sum(dsp, axis=0, keepdims=True)

        @pl.when(b == 0)
        def _():
            dv_ref[...] = jnp.zeros_like(dv_ref)

        dv_ref[...] += jnp.concatenate([dalog, ddt], axis=0)

    blk = pl.BlockSpec((lp, DH), lambda b: (b, 0))
    vec = pl.BlockSpec((1, DH), lambda b: (0, 0))
    hb = pl.BlockSpec((HEADS, lp, DH), lambda b: (b, 0, 0))
    return pl.pallas_call(
        body, name="gates_bwd", grid=(bsz,),
        in_specs=[blk, vec, vec, hb, hb, pl.BlockSpec((lp, DH), lambda b: (b, 0))],
        out_specs=[pl.BlockSpec((lp, BA_W), lambda b: (b, 0)), pl.BlockSpec((2, DH), lambda b: (0, 0))],
        out_shape=[jax.ShapeDtypeStruct((bsz * lp, BA_W), BF16), jax.ShapeDtypeStruct((2, DH), F32)],
        compiler_params=_cparams(("arbitrary",)),
    )(proj, alog_v, dtb_v, dbeta_b, dg_b, gb)


def _mix_heads(proj, o, gdn_norm, conv_sc, bsz, lp, padf):
    def body(o_ref, z_ref, gn_ref, sx_ref, sb_ref, sc_ref, w_ref, og_ref, os_ref):
        og_ref[...] = (_rms(o_ref[...], gn_ref[...])[0] * _silu(z_ref[...].astype(F32))).astype(BF16)
        conv = _conv_fwd(sc_ref[...].astype(F32) * sx_ref[...].astype(F32), w_ref[...], KS, lp)
        t = lax.broadcasted_iota(jnp.int32, (lp, 1), 0)
        os_ref[...] = jnp.where(t >= padf, sb_ref[...].astype(F32) * conv, 0.0).astype(BF16)

    half = _slab(lp, 0)
    return pl.pallas_call(
        body, name="mix_heads", grid=(bsz, HEADS),
        in_specs=[half, _slab(lp, Z0), pl.BlockSpec((1, DH), lambda b, s: (0, 0)),
                  _slab(lp, SX0), _slab(lp, SB0), _slab(lp, SC0), pl.BlockSpec((KS, DH), lambda b, s: (0, s))],
        out_specs=[half, half],
        out_shape=[jax.ShapeDtypeStruct((bsz * lp, GW), BF16)] * 2,
        compiler_params=_cparams(("parallel", "parallel")),
    )(o, proj, gdn_norm, proj, proj, proj, conv_sc)


def _mix_heads_bwd(proj, o, gdn_norm, conv_sc, dmixin, bsz, lp, padf):
    def body(o_ref, z_ref, gn_ref, sx_ref, sb_ref, sc_ref, w_ref, dg_ref, ds_ref,
             do_ref, dz_ref, dsx_ref, dsb_ref, dsc_ref, dgn_ref, dw_ref):
        s, b = pl.program_id(0), pl.program_id(1)
        t = lax.broadcasted_iota(jnp.int32, (lp, 1), 0)
        real = t >= padf
        o, z, gn, d = o_ref[...], z_ref[...].astype(F32), gn_ref[...], dg_ref[...]
        sz = _silu(z)
        on = _rms(o, gn)[0]
        dz_ref[...] = (d * on * _dsilu(z)).astype(BF16)
        dox, dgn = _rms_bwd(o, gn, d * sz)
        do_ref[...] = dox
        sx, sb, sc = sx_ref[...].astype(F32), sb_ref[...].astype(F32), sc_ref[...].astype(F32)
        w, e = w_ref[...], ds_ref[...]
        e = jnp.where(real, e, 0.0)
        xin = sc * sx
        conv = _conv_fwd(xin, w, KS, lp)
        dsb_ref[...] = (e * conv).astype(BF16)
        dxin, dw = _conv_bwd(xin, w, e * sb, KS, lp)
        dxin = jnp.where(real, dxin, 0.0)
        dsx_ref[...] = (dxin * sc).astype(BF16)
        dsc_ref[...] = (dxin * sx).astype(BF16)

        @pl.when(jnp.logical_and(s == 0, b == 0))
        def _():
            dgn_ref[...] = jnp.zeros_like(dgn_ref)

        @pl.when(b == 0)
        def _():
            dw_ref[...] = jnp.zeros_like(dw_ref)

        dgn_ref[...] += jnp.sum(dgn, axis=0, keepdims=True)
        dw_ref[...] += dw

    sl = lambda col0: pl.BlockSpec((lp, DH), lambda s, b: (b, col0 // DH + s))
    half = sl(0)
    return pl.pallas_call(
        body, name="mix_heads_bwd", grid=(HEADS, bsz),
        in_specs=[half, sl(Z0), pl.BlockSpec((1, DH), lambda s, b: (0, 0)), sl(SX0), sl(SB0), sl(SC0),
                  pl.BlockSpec((KS, DH), lambda s, b: (0, s)), sl(0), sl(GW)],
        out_specs=[half] * 5 + [pl.BlockSpec((1, DH), lambda s, b: (0, 0)), pl.BlockSpec((KS, DH), lambda s, b: (0, s))],
        out_shape=[jax.ShapeDtypeStruct((bsz * lp, GW), F32)] + [jax.ShapeDtypeStruct((bsz * lp, GW), BF16)] * 4
        + [jax.ShapeDtypeStruct((1, DH), F32), jax.ShapeDtypeStruct((KS, SW), F32)],
        compiler_params=_cparams(("arbitrary", "arbitrary")),
    )(o, proj, gdn_norm, proj, proj, proj, conv_sc, dmixin, dmixin)


def _split_dot(a, b, dims=NN):
    ah = a.astype(BF16)
    al = (a - ah.astype(F32)).astype(BF16)
    bh = b.astype(BF16)
    bl = (b - bh.astype(F32)).astype(BF16)

    def d(x, y):
        return lax.dot_general(x, y, (dims, ((), ())), preferred_element_type=F32)

    return d(ah, bh) + (d(ah, bl) + d(al, bh))


def _eye(n):
    return (lax.broadcasted_iota(jnp.int32, (n, n), 0) == lax.broadcasted_iota(jnp.int32, (n, n), 1)).astype(F32)


def _each(f, *lists):
    return [f(*xs) for xs in zip(*lists)]


def _inv_unit_lower(a_list):
    eye = _eye(CH)
    p = _each(lambda a: -a, a_list)
    t = _each(lambda x: eye + x, p)
    p = _each(lambda x: _split_dot(x, x), p)
    for level in range(5):
        if level < 4:
            prod = _each(lambda tt, pp: _split_dot(jnp.concatenate([tt, pp], axis=0), pp), t, p)
            t = _each(lambda tt, pr: tt + pr[:CH], t, prod)
            p = _each(lambda pr: pr[CH:], prod)
        else:
            t = _each(lambda tt, pp: tt + _split_dot(tt, pp), t, p)
    return t


def _chunk_gates(gc_ref, be_ref, gt_ref, g):
    rows = pl.ds(g * CH, CH)
    gc = gc_ref[0, rows, :]
    be = be_ref[0, rows, :]
    gcl = gc_ref[0, pl.ds(g * CH + CH - 1, 1), :]
    gt = gt_ref[0, g]
    incl, upper = _tri(CH, "incl"), _tri(CH, "upper")
    dec = jnp.where(incl, jnp.exp(jnp.where(incl, gc[:, :CH] - gt, 0.0)), 0.0)
    dec_t = jnp.where(upper, jnp.exp(jnp.where(upper, gt - gc[:, :CH], 0.0)), 0.0)
    return rows, be, dec, dec_t, jnp.exp(gc), jnp.exp(gcl - gc), jnp.exp(gcl)


def _chunk_group(nchunk):
    return 3 if nchunk % 3 == 0 else 1


def _local_group(nchunk):
    return 11 if nchunk % 11 == 0 else _chunk_group(nchunk)


def _gdn_local_specs(lp, grp):
    ngrp = lp // (grp * CH)
    head = lambda col0: pl.BlockSpec((grp * CH, DH), lambda i, j: ((i // HEADS) * ngrp + j, col0 // DH + i % HEADS))
    lane_b = pl.BlockSpec((1, grp * CH, DH), lambda i, j: (i, j, 0))
    per_chunk = lambda r, c: pl.BlockSpec((1, grp, r, c), lambda i, j: (i, j, 0, 0))
    return ngrp, head, lane_b, per_chunk


def _gdn_local_fwd(qkv, gcb, bb, gct, bsz, lp):
    nchunk = lp // CH
    grp = _local_group(nchunk)
    ngrp, head, lane_b, per_chunk = _gdn_local_specs(lp, grp)
    bh = bsz * HEADS

    def body(q_ref, k_ref, v_ref, gc_ref, be_ref, gt_ref,
             u_ref, l1_ref, l2_ref, l3_ref, wt_ref, kd_ref, ti_ref, gl_ref):
        incl, strict, upper = _tri(CH, "incl"), _tri(CH, "strict"), _tri(CH, "upper")
        eye = _eye(CH)
        rows, be, dec, dec_t, eg, ekd, gl = zip(*[_chunk_gates(gc_ref, be_ref, gt_ref, g) for g in range(grp)])
        q = [q_ref[r, :] for r in rows]
        k = [k_ref[r, :] for r in rows]
        v = [v_ref[r, :] for r in rows]
        kb = _each(lambda x, y: x * y, k, be)
        a = _each(lambda x, y, d: jnp.where(strict, _bdot(x, y, NT) * d, 0.0), kb, k, dec)
        tinv = _inv_unit_lower(a)
        u = _each(lambda t, x, y: _bdot(t, x * y, NN), tinv, v, be)
        w = _each(lambda t, x, y: _bdot(t, x * y, NN), tinv, kb, eg)
        qk = _each(lambda x, y, d: jnp.where(incl, _bdot(x, y, NT) * d, 0.0), q, k, dec)
        qk_t = _each(lambda x, y, d: jnp.where(upper, _bdot(y, x, NT) * d, 0.0), q, k, dec_t)
        qd = _each(lambda x, y: x * y, q, eg)
        kd = _each(lambda x, y: x * y, k, ekd)
        kd_t = _each(lambda x: _bdot(x, eye, TN), kd)
        qd_t = _each(lambda x: _bdot(x, eye, TN), qd)
        w_t = _each(lambda x: _bdot(x, eye, TN), w)
        for g in range(grp):
            u_ref[rows[g], :] = u[g]
            l1_ref[0, g] = jnp.concatenate([w[g], qd[g]], axis=0).astype(BF16)
            l2_ref[0, g] = jnp.concatenate([qk[g], kd_t[g]], axis=0).astype(BF16)
            l3_ref[0, g] = jnp.concatenate([qk_t[g], qd_t[g]], axis=0).astype(BF16)
            wt_ref[0, g] = w_t[g].astype(BF16)
            kd_ref[0, g] = kd[g].astype(BF16)
            ti_ref[0, g] = tinv[g]
            gl_ref[0, g] = gl[g]

    r = bsz * lp
    shapes = [((r, GW), F32), ((bh, nchunk, 2 * CH, DH), BF16), ((bh, nchunk, 3 * CH, CH), BF16),
              ((bh, nchunk, 3 * CH, CH), BF16), ((bh, nchunk, DH, CH), BF16), ((bh, nchunk, CH, DH), BF16),
              ((bh, nchunk, CH, CH), F32), ((bh, nchunk, 1, DH), F32)]
    return pl.pallas_call(
        body, name="gdn_local_fwd", grid=(bh, ngrp),
        in_specs=[head(0), head(GW), head(2 * GW), lane_b, lane_b, per_chunk(1, CH)],
        out_specs=[head(0), per_chunk(2 * CH, DH), per_chunk(3 * CH, CH), per_chunk(3 * CH, CH), per_chunk(DH, CH),
                   per_chunk(CH, DH), per_chunk(CH, CH), per_chunk(1, DH)],
        out_shape=[jax.ShapeDtypeStruct(s, dt) for s, dt in shapes],
        compiler_params=_cparams(("parallel", "parallel")),
    )(qkv, qkv, qkv, gcb, bb, gct)


def _scan_specs(bsz, lp, grp, order):
    ngrp = lp // (grp * CH)
    bh = bsz * HEADS
    seq = lambda: pl.BlockSpec((bsz, grp * CH, GW), lambda s: (0, order(s, ngrp), 0))
    per_chunk = lambda r, c: pl.BlockSpec((bh, grp, r, c), lambda s: (0, order(s, ngrp), 0, 0))
    return ngrp, bh, seq, per_chunk


def _gdn_scan_fwd(u, l1, l2, gl, bsz, lp):
    nchunk = lp // CH
    grp = _chunk_group(nchunk)
    ngrp, bh, seq, per_chunk = _scan_specs(bsz, lp, grp, lambda s, n: s)

    def body(u_ref, l1_ref, l2_ref, gl_ref, o_ref, vn_ref, st_ref, s_scr):
        @pl.when(pl.program_id(0) == 0)
        def _():
            s_scr[...] = jnp.zeros_like(s_scr)

        chains = list(range(bh))
        where = [(i // HEADS, slice((i % HEADS) * DH, (i % HEADS + 1) * DH)) for i in chains]

        def dot(a, b):
            return jnp.dot(a, b.astype(BF16), preferred_element_type=F32)

        for c in range(grp):
            rows = slice(c * CH, (c + 1) * CH)
            s = [s_scr[i] for i in chains]
            for i in chains:
                st_ref[i, c] = s[i]
            x = _each(lambda i, si: dot(l1_ref[i, c], si), chains, s)
            vnew = _each(lambda bc, xi: u_ref[bc[0], rows, bc[1]] - xi[:CH], where, x)
            y = _each(lambda i, vi: dot(l2_ref[i, c], vi), chains, vnew)
            for i, (b, cols) in zip(chains, where):
                o_ref[b, rows, cols] = x[i][CH:] + y[i][:CH]
                vn_ref[b, rows, cols] = vnew[i].astype(BF16)
                s_scr[i] = s[i] * gl_ref[i, c] + y[i][CH:]

    return pl.pallas_call(
        body, name="gdn_scan_fwd", grid=(ngrp,),
        in_specs=[seq(), per_chunk(2 * CH, DH), per_chunk(3 * CH, CH), per_chunk(1, DH)],
        out_specs=[seq(), seq(), per_chunk(DH, DH)],
        out_shape=[jax.ShapeDtypeStruct((bsz, lp, GW), F32), jax.ShapeDtypeStruct((bsz, lp, GW), BF16),
                   jax.ShapeDtypeStruct((bh, nchunk, DH, DH), F32)],
        scratch_shapes=[pltpu.VMEM((bh, DH, DH), F32)],
        compiler_params=_cparams(("arbitrary",)),
    )(u.reshape(bsz, lp, GW), l1, l2, gl)


def _gdn_scan_bwd(d_o, l3, wt, kd, gl, bsz, lp):
    nchunk = lp // CH
    grp = _chunk_group(nchunk)
    ngrp, bh, seq, per_chunk = _scan_specs(bsz, lp, grp, lambda s, n: n - 1 - s)

    def body(do_ref, l3_ref, wt_ref, kd_ref, gl_ref, dvn_ref, dst_ref, ds_scr):
        @pl.when(pl.program_id(0) == 0)
        def _():
            ds_scr[...] = jnp.zeros_like(ds_scr)

        chains = list(range(bh))
        where = [(i // HEADS, slice((i % HEADS) * DH, (i % HEADS + 1) * DH)) for i in chains]

        def dot(a, b):
            return jnp.dot(a, b.astype(BF16), preferred_element_type=F32)

        for c in reversed(range(grp)):
            rows = slice(c * CH, (c + 1) * CH)
            ds = [ds_scr[i] for i in chains]
            for i in chains:
                dst_ref[i, c] = ds[i]
            p = _each(lambda i, bc: dot(l3_ref[i, c], do_ref[bc[0], rows, bc[1]]), chains, where)
            dvn = _each(lambda i, pi, di: pi[:CH] + dot(kd_ref[i, c], di), chains, p, ds)
            wd = _each(lambda i, vi: dot(wt_ref[i, c], vi), chains, dvn)
            for i, (b, cols) in zip(chains, where):
                dvn_ref[b, rows, cols] = dvn[i].astype(BF16)
                ds_scr[i] = ds[i] * gl_ref[i, c] + p[i][CH:] - wd[i]

    return pl.pallas_call(
        body, name="gdn_scan_bwd", grid=(ngrp,),
        in_specs=[seq(), per_chunk(3 * CH, CH), per_chunk(DH, CH), per_chunk(CH, DH), per_chunk(1, DH)],
        out_specs=[seq(), per_chunk(DH, DH)],
        out_shape=[jax.ShapeDtypeStruct((bsz, lp, GW), BF16), jax.ShapeDtypeStruct((bh, nchunk, DH, DH), F32)],
        scratch_shapes=[pltpu.VMEM((bh, DH, DH), F32)],
        compiler_params=_cparams(("arbitrary",)),
    )(d_o.reshape(bsz, lp, GW), l3, wt, kd, gl)


def _gdn_local_bwd(qkv, gcb, bb, gct, tinv_all, states, dstates, vnew, dvnew, d_o, bsz, lp):
    nchunk = lp // CH
    grp = _local_group(nchunk)
    ngrp, head, lane_b, per_chunk = _gdn_local_specs(lp, grp)
    bh = bsz * HEADS

    def body(q_ref, k_ref, v_ref, gc_ref, be_ref, gt_ref, ti_ref, st_ref, dst_ref, vn_ref, dvn_ref, do_ref,
             dq_ref, dk_ref, dv_ref, dg_ref, db_ref):
        incl, strict = _tri(CH, "incl"), _tri(CH, "strict")
        upper = _tri(CH, "upper").astype(F32)
        ones = jnp.ones((CH, DH), F32)
        last = lax.broadcasted_iota(jnp.int32, (CH, 1), 0) == CH - 1

        def rsum(x):
            return jnp.sum(x, axis=-1, keepdims=True)

        def mul(xs, ys):
            return _each(lambda x, y: x * y, xs, ys)

        rows, be, dec, _, eg, ekd, gl = zip(*[_chunk_gates(gc_ref, be_ref, gt_ref, g) for g in range(grp)])
        q = [q_ref[r, :] for r in rows]
        k = [k_ref[r, :] for r in rows]
        v = [v_ref[r, :] for r in rows]
        vnew = [vn_ref[r, :] for r in rows]
        dvn = [dvn_ref[r, :] for r in rows]
        do = [do_ref[r, :] for r in rows]
        tinv = [ti_ref[0, g] for g in range(grp)]
        s = [st_ref[0, g] for g in range(grp)]
        ds = [dst_ref[0, g] for g in range(grp)]
        kb, vb, qd, kd = mul(k, be), mul(v, be), mul(q, eg), mul(k, ekd)
        kbg = mul(kb, eg)
        a = _each(lambda x, y, d: jnp.where(strict, _bdot(x, y, NT) * d, 0.0), kb, k, dec)
        qk = _each(lambda x, y, d: jnp.where(incl, _bdot(x, y, NT) * d, 0.0), q, k, dec)
        dqk = _each(lambda x, y: jnp.where(incl, _bdot(x, y, NT), 0.0), do, vnew)
        dqd = _each(lambda x, y: _bdot(x, y, NT), do, s)
        dkd = _each(lambda x, y: _bdot(x, y, NT), vnew, ds)
        dgl = _each(lambda x, y: jnp.sum(rsum(x * y), axis=0, keepdims=True), s, ds)
        dw = _each(lambda x, y: -_bdot(x, y, NT), dvn, s)
        dvb = _each(lambda t, x: _bdot(t, x, TN), tinv, dvn)
        dkbg = _each(lambda t, x: _bdot(t, x, TN), tinv, dw)
        dt = _each(lambda x, y, z, w: _bdot(x, y, NT) + _bdot(z, w, NT), dvn, vb, dw, kbg)
        tdt = _each(lambda t, x: _split_dot(t, x, TN), tinv, dt)
        da = _each(lambda x, t: jnp.where(strict, -_split_dot(x, t, NT), 0.0), tdt, tinv)
        dp, dr = mul(da, dec), mul(dqk, dec)
        dkb = _each(lambda x, y, z, e: _bdot(x, y, NN) + z * e, dp, k, dkbg, eg)
        dk = _each(lambda p_, kb_, r_, q_, dkd_, ekd_, dkb_, be_:
                   _bdot(p_, kb_, TN) + _bdot(r_, q_, TN) + dkd_ * ekd_ + dkb_ * be_,
                   dp, kb, dr, q, dkd, ekd, dkb, be)
        dq = _each(lambda r_, k_, dqd_, eg_: _bdot(r_, k_, NN) + dqd_ * eg_, dr, k, dqd, eg)
        m = _each(lambda da_, a_, dqk_, qk_: da_ * a_ + dqk_ * qk_, da, a, dqk, qk)
        m_cols = _each(lambda x: _split_dot(x, ones, TN), m)
        kd_term = _each(lambda x, y: rsum(x * y), dkd, kd)
        dgc = _each(lambda m_, mc, dqd_, qd_, kt, dkbg_, kbg_, dgl_, gl_:
                    rsum(m_) - mc + rsum(dqd_ * qd_) - kt + rsum(dkbg_ * kbg_)
                    + jnp.where(last, jnp.sum(kt, axis=0, keepdims=True) + dgl_ * gl_, 0.0),
                    m, m_cols, dqd, qd, kd_term, dkbg, kbg, dgl, gl)
        dg = _each(lambda x: _split_dot(upper, x, NN), dgc)
        for g in range(grp):
            dq_ref[rows[g], :] = dq[g]
            dk_ref[rows[g], :] = dk[g]
            dv_ref[rows[g], :] = dvb[g] * be[g]
            dg_ref[0, rows[g], :] = dg[g]
            db_ref[0, rows[g], :] = jnp.broadcast_to(rsum(dkb[g] * k[g]) + rsum(dvb[g] * v[g]), (CH, DH))

    r = bsz * lp
    return pl.pallas_call(
        body, name="gdn_local_bwd", grid=(bh, ngrp),
        in_specs=[head(0), head(GW), head(2 * GW), lane_b, lane_b, per_chunk(1, CH), per_chunk(CH, CH),
                  per_chunk(DH, DH), per_chunk(DH, DH), head(0), head(0), head(0)],
        out_specs=[head(0), head(0), head(0), lane_b, lane_b],
        out_shape=[jax.ShapeDtypeStruct((r, GW), F32)] * 3 + [jax.ShapeDtypeStruct((bh, lp, DH), F32)] * 2,
        compiler_params=_cparams(("parallel", "parallel")),
    )(qkv, qkv, qkv, gcb, bb, gct, tinv_all, states, dstates, vnew, dvnew, d_o)


def _place():
    return lax.axis_index("x"), lax.axis_index("y"), lax.axis_index("c")


def _any_specs(n):
    return [pl.BlockSpec(memory_space=pl.ANY)] * n


def _all_gather(name, shards):
    n = len(shards)

    def body(*refs):
        ins, outs = refs[:n], refs[n:2 * n]
        send_sems, recv_sems, local_sems = refs[2 * n:2 * n + 3]
        stage = refs[2 * n + 3:]
        x, y, c = _place()
        me, sibling = (x, y, c), (x, y, 1 - c)
        chips = [(1 - x, y), (x, 1 - y), (1 - x, 1 - y)]

        def slot(i, p):
            return outs[i].at[4 * p[0] + 2 * p[1] + p[2]]

        def copy(i, k, block, to, src=None):
            return pltpu.make_async_remote_copy(
                src_ref=slot(i, block) if src is None else src, dst_ref=slot(i, block),
                send_sem=send_sems.at[i, k], recv_sem=recv_sems.at[i, k], device_id=to, device_id_type=MESH)

        load = [pltpu.make_async_copy(ins[i], stage[i], local_sems.at[i]) for i in range(n)]
        for cp in load:
            cp.start()
        first = []
        for i in range(n):
            first.append(copy(i, 0, me, sibling, src=ins[i]))
            first += [copy(i, 1 + j, me, (*chip, c), src=ins[i]) for j, chip in enumerate(chips)]
        for cp in first:
            cp.start()
        for cp in load:
            cp.wait()
        mine = [pltpu.make_async_copy(stage[i], slot(i, me), local_sems.at[i]) for i in range(n)]
        for cp in mine:
            cp.start()
        passed = []
        for j, chip in enumerate(chips):
            for i in range(n):
                copy(i, 1 + j, (*chip, c), me).wait_recv()
                fwd = copy(i, 4 + j, (*chip, c), sibling)
                fwd.start()
                passed.append(fwd)
        for i in range(n):
            copy(i, 0, sibling, me).wait_recv()
            for j, chip in enumerate(chips):
                copy(i, 4 + j, (*chip, 1 - c), me).wait_recv()
        for cp in first + passed:
            cp.wait_send()
        for cp in mine:
            cp.wait()

    return pl.pallas_call(
        body, name=name,
        in_specs=_any_specs(n), out_specs=_any_specs(n),
        out_shape=[jax.ShapeDtypeStruct((N_DEV,) + s.shape, s.dtype) for s in shards],
        scratch_shapes=[pltpu.SemaphoreType.DMA((n, 7)), pltpu.SemaphoreType.DMA((n, 7)),
                        pltpu.SemaphoreType.DMA((n,))] + [pltpu.VMEM(s.shape, s.dtype) for s in shards],
    )(*shards)


HBM_SPEC = pl.BlockSpec(memory_space=pltpu.HBM)
SEM_SPEC = pl.BlockSpec(memory_space=pltpu.SEMAPHORE)
EFFECT = pltpu.SideEffectType.DATAFLOW_SIDE_EFFECTING
N_PEER = N_DEV - 1


def _peer(r):
    x, y, c = _place()
    return ((1 - x) if r & 4 else x, (1 - y) if r & 2 else y, (1 - c) if r & 1 else c)


def _slot_of(p):
    return 4 * p[0] + 2 * p[1] + p[2]


def _hbm(a):
    return pltpu.with_memory_space_constraint(a, pltpu.HBM)


def _gather_pattern(src, land):
    me = _place()
    return [(src, land.at[_slot_of(me)], _peer(r), land.at[_slot_of(_peer(r))]) for r in range(1, N_DEV)]


def _chipcast_pattern(src, land):
    x, y, c = _place()
    me = (x, y, c)
    peers = [(x, y, 1 - c), (1 - x, y, c), (x, 1 - y, c), (1 - x, 1 - y, c)]
    return [(src, land.at[_slot_of(me)], p, land.at[_slot_of(p)]) for p in peers]


def _forward_pattern(land, _):
    x, y, c = _place()
    chips = [(1 - x, y), (x, 1 - y), (1 - x, 1 - y)]
    return [(land.at[_slot_of((*ch, c))], land.at[_slot_of((*ch, c))], (x, y, 1 - c), land.at[_slot_of((*ch, 1 - c))])
            for ch in chips]


def _sibling_pattern(src, land):
    x, y, c = _place()
    return [(src.at[ch, 1 - c], land.at[ch], (x, y, 1 - c), land.at[ch]) for ch in range(4)]


def _chip_pattern(src, land):
    x, y, c = _place()
    chips = [(1 - x, y), (x, 1 - y), (1 - x, 1 - y)]
    return [(src.at[2 * ch[0] + ch[1]], land.at[j], (*ch, c), land.at[j]) for j, ch in enumerate(chips)]


def _split_start(name, groups, pattern, land_shape, after=None):
    sizes = [len(g) for g in groups]
    arrays = [s for g in groups for s in g]
    n = len(arrays)
    inplace = land_shape is None
    fresh = [] if inplace else [lax.empty(land_shape(s.shape), s.dtype) for s in arrays]
    bufs = arrays + fresh
    order = [] if after is None else [after]

    def body(*refs):
        srcs, lands = refs[:n], (refs[:n] if inplace else refs[n:2 * n])
        first_out = len(bufs) + len(order)
        sems = refs[first_out:first_out + 2 * len(groups)]
        token = refs[-1]
        k = 0
        for gi, size in enumerate(sizes):
            for i in range(size):
                copies = pattern(srcs[k], lands[k])
                for j, (s, d, peer, _) in enumerate(copies):
                    pltpu.make_async_remote_copy(
                        src_ref=s, dst_ref=d, send_sem=sems[2 * gi].at[i * len(copies) + j],
                        recv_sem=sems[2 * gi + 1].at[i * len(copies) + j], device_id=peer, device_id_type=MESH).start()
                k += 1
        token[...] = jnp.zeros_like(token)

    per_array = {_gather_pattern: N_PEER, _chipcast_pattern: 4, _forward_pattern: 3, _sibling_pattern: 4,
                 _chip_pattern: 3}[pattern]
    sem_shapes = [pltpu.SemaphoreType.DMA((size * per_array,)) for size in sizes for _ in range(2)]
    res = pl.pallas_call(
        body, name=name,
        in_specs=[HBM_SPEC] * len(bufs) + [pl.BlockSpec(memory_space=pl.ANY)] * len(order),
        out_specs=[SEM_SPEC] * len(sem_shapes) + [HBM_SPEC] * len(bufs) + [pl.BlockSpec(memory_space=pltpu.VMEM)],
        out_shape=sem_shapes + [pltpu.HBM(s.shape, s.dtype) for s in bufs] + [jax.ShapeDtypeStruct((8, DH), F32)],
        input_output_aliases={i: len(sem_shapes) + i for i in range(len(bufs))},
        compiler_params=pltpu.CompilerParams(has_side_effects=EFFECT),
    )(*[_hbm(s) for s in bufs], *order)
    nsem = len(sem_shapes)
    thru = list(res[nsem:nsem + len(bufs)])
    out, k = [], 0
    for gi, size in enumerate(sizes):
        srcs_out = [] if inplace else thru[k:k + size]
        lands_out = thru[k:k + size] if inplace else thru[n + k:n + k + size]
        out.append((res[2 * gi], res[2 * gi + 1], srcs_out, lands_out))
        k += size
    return out, res[-1]


def _split_wait(name, group, pattern, after):
    send_sems, recv_sems, arrays, lands = group
    n = len(lands)
    inplace = not arrays
    bufs = list(arrays) + list(lands)
    fill_own = pattern in (_gather_pattern, _chipcast_pattern)

    def body(*refs):
        srcs, lands_ = refs[:n], (refs[:n] if inplace else refs[n:2 * n])
        s_sems, r_sems = refs[len(bufs)], refs[len(bufs) + 1]
        own = []
        if fill_own:
            stage, local_sems = refs[-n - 1:-1], refs[-1]
            load = [pltpu.make_async_copy(srcs[i], stage[i], local_sems.at[i]) for i in range(n)]
            for cp in load:
                cp.start()
            for cp in load:
                cp.wait()
            own = [pltpu.make_async_copy(stage[i], lands_[i].at[_slot_of(_place())], local_sems.at[i])
                   for i in range(n)]
            for cp in own:
                cp.start()
        for i in range(n):
            copies = pattern(srcs[i], lands_[i])
            for j, (s, _, peer, filled) in enumerate(copies):
                cp = pltpu.make_async_remote_copy(
                    src_ref=s, dst_ref=filled, send_sem=s_sems.at[i * len(copies) + j],
                    recv_sem=r_sems.at[i * len(copies) + j], device_id=peer, device_id_type=MESH)
                cp.wait_send()
                cp.wait_recv()
        for cp in own:
            cp.wait()

    res = pl.pallas_call(
        body, name=name,
        in_specs=[HBM_SPEC] * len(bufs) + [SEM_SPEC, SEM_SPEC, pl.BlockSpec(memory_space=pl.ANY)],
        out_specs=[HBM_SPEC] * len(bufs),
        out_shape=[pltpu.HBM(s.shape, s.dtype) for s in bufs],
        scratch_shapes=([pltpu.VMEM(a.shape, a.dtype) for a in arrays] + [pltpu.SemaphoreType.DMA((n,))])
        if fill_own else [],
        input_output_aliases={i: i for i in range(len(bufs))},
        compiler_params=pltpu.CompilerParams(has_side_effects=EFFECT),
    )(*bufs, send_sems, recv_sems, after)
    return ([], list(res)) if inplace else (list(res[:n]), list(res[n:]))


def _pair_sum(name, grads, recv, core):
    _, _, r, c = grads.shape
    tr = _div_tile(r, 256, 16) if r % 16 == 0 else r

    def body(core_ref, g_ref, r_ref, o_ref):
        o_ref[...] = (g_ref[0].astype(F32) + r_ref[...].astype(F32)).astype(o_ref.dtype)

    return pl.pallas_call(
        body, name=name,
        grid_spec=pltpu.PrefetchScalarGridSpec(
            num_scalar_prefetch=1, grid=(4, r // tr),
            in_specs=[pl.BlockSpec((1, 1, tr, c), lambda s, i, cr: (s, cr[0], i, 0)),
                      pl.BlockSpec((1, tr, c), lambda s, i, cr: (s, i, 0))],
            out_specs=pl.BlockSpec((1, tr, c), lambda s, i, cr: (s, i, 0))),
        out_shape=jax.ShapeDtypeStruct(recv.shape, recv.dtype),
        compiler_params=_cparams(("parallel", "parallel")),
    )(core, grads, recv)


def _adamw_math(w, g, m, v):
    m = ADAM_B1 * m + (1.0 - ADAM_B1) * g
    v = ADAM_B2 * v + (1.0 - ADAM_B2) * (g * g)
    m_hat = m / (1.0 - ADAM_B1 ** ADAM_STEP)
    v_hat = v / (1.0 - ADAM_B2 ** ADAM_STEP)
    delta = -ADAM_LR * (m_hat / (jnp.sqrt(v_hat) + ADAM_EPS) + ADAM_WD * w)
    return delta, m, v


SMALL_LEAVES = (
    ("mix_pre_norm", 0, 1, 0, D, (1, D)), ("mix_post_norm", 1, 1, 0, D, (1, D)),
    ("ffn_pre_norm", 2, 1, 0, D, (1, D)), ("ffn_post_norm", 3, 1, 0, D, (1, D)),
    ("meta_tokens", 4, N_META, 0, DH, (N_META, DH)),
    ("conv_qkv", 20, KQ, 0, 3 * GW // N_DEV, (1, KQ, 3 * GW // N_DEV)),
    ("conv_sc", 24, KS, 0, SW // N_DEV, (1, KS, SW // N_DEV)),
    ("a_log", 27, 1, 0, HEADS, (1, HEADS)), ("dt_bias", 27, 1, DH, HEADS, (1, HEADS)),
    ("gdn_norm", 27, 1, 2 * DH, DH, (1, DH)),
)


def _adamw_small(slabs, w, m, v):
    nslot = slabs.shape[0]
    nleaf = len(SMALL_LEAVES)

    def body(p_ref, w_ref, m_ref, v_ref, *refs):
        outs, loss_ref, scr = refs[:4 * nleaf], refs[4 * nleaf], refs[4 * nleaf + 1]
        g = p_ref[0]
        for s in range(1, nslot):
            g = g + p_ref[s]
        d, mn, vn = _adamw_math(w_ref[...], g, m_ref[...], v_ref[...])
        for kind, val in enumerate((g, d, mn, vn)):
            scr[kind] = val
        for kind in range(4):
            for li, (_, r0, nr, c0, nc, shape) in enumerate(SMALL_LEAVES):
                val = scr[kind, pl.ds(r0, nr), pl.ds(c0, nc)]
                o = outs[kind * nleaf + li]
                if len(shape) == 3:
                    o[0] = val
                else:
                    o[...] = val
        loss_ref[...] = scr[0, pl.ds(27, 1), pl.ds(3 * DH, 1)]

    whole = lambda a: pl.BlockSpec(a.shape, lambda: (0,) * a.ndim)
    out_shapes = [jax.ShapeDtypeStruct(shape, F32) for _ in range(4) for (*_, shape) in SMALL_LEAVES]
    out_shapes.append(jax.ShapeDtypeStruct((1, 1), F32))
    res = pl.pallas_call(
        body, name="adamw_small",
        in_specs=[whole(slabs), whole(w), whole(m), whole(v)],
        out_specs=[pl.BlockSpec(s.shape, lambda n=len(s.shape): (0,) * n) for s in out_shapes],
        out_shape=out_shapes,
        scratch_shapes=[pltpu.VMEM((4,) + w.shape, F32)],
        compiler_params=pltpu.CompilerParams(vmem_limit_bytes=VMEM_LIMIT),
    )(slabs, w, m, v)
    kinds = [{nm: res[kind * nleaf + li] for li, (nm, *_) in enumerate(SMALL_LEAVES)} for kind in range(4)]
    return kinds, res[-1]


def _adamw_reduced(name, parts, chip, recv, w, m, v, after=None):
    _, r, c = parts.shape
    tr = _div_tile(r, 256, 16) if r % 16 == 0 else r
    order = [] if after is None else [after]

    def body(chip_ref, p_ref, r_ref, w_ref, m_ref, v_ref, *rest):
        g_out, d_out, m_out, v_out = rest[len(order):]
        g = p_ref[0].astype(F32)
        for s in range(3):
            g = g + r_ref[s].astype(F32)
        d, mn, vn = _adamw_math(w_ref[...], g, m_ref[...], v_ref[...])
        g_out[...] = g
        d_out[...] = d
        m_out[...] = mn
        v_out[...] = vn

    blk = pl.BlockSpec((tr, c), lambda i, ch: (i, 0))
    return pl.pallas_call(
        body, name=name,
        grid_spec=pltpu.PrefetchScalarGridSpec(
            num_scalar_prefetch=1, grid=(r // tr,),
            in_specs=[pl.BlockSpec((1, tr, c), lambda i, ch: (ch[0], i, 0)),
                      pl.BlockSpec((3, tr, c), lambda i, ch: (0, i, 0)), blk, blk, blk]
            + [pl.BlockSpec((8, DH), lambda i, ch: (0, 0))] * len(order),
            out_specs=[blk] * 4),
        out_shape=[jax.ShapeDtypeStruct((r, c), F32)] * 4,
        compiler_params=_cparams(("parallel",)),
    )(chip, parts, recv, w, m, v, *order)


SMALL_ROWS, SMALL_COLS = 32, 3 * GW


def _pack_small(gains4, meta, conv_qkv, conv_sc, a_log, dt_bias, gdn_norm, extra=None):
    def padc(a):
        return jnp.pad(a, ((0, 0), (0, SMALL_COLS - a.shape[1])))
    scal = jnp.concatenate([jnp.pad(a_log, ((0, 0), (0, DH - HEADS))), jnp.pad(dt_bias, ((0, 0), (0, DH - HEADS))),
                            gdn_norm] + ([] if extra is None else [extra]), axis=1)
    rows = [padc(g) for g in gains4] + [padc(meta), padc(conv_qkv), padc(conv_sc), padc(scal)]
    slab = jnp.concatenate(rows, axis=0)
    return jnp.pad(slab, ((0, SMALL_ROWS - slab.shape[0]), (0, 0)))


def _unpack_small(slab):
    gains = [slab[i:i + 1, :D] for i in range(4)]
    meta = slab[4:20, :D]
    conv_qkv = slab[20:24, :]
    conv_sc = slab[24:27, :SW]
    a_log = slab[27:28, 0:HEADS]
    dt_bias = slab[27:28, DH:DH + HEADS]
    gdn_norm = slab[27:28, 2 * DH:3 * DH]
    return gains, meta, conv_qkv, conv_sc, a_log, dt_bias, gdn_norm


def kernel(x, meta_tokens, mix_pre_norm, mix_post_norm, ffn_pre_norm, ffn_post_norm, w_in, conv_qkv, a_log, dt_bias, gdn_norm, conv_sc, w_out, w_gate, w_up, w_down, loss_target, m_meta_tokens, m_mix_pre_norm, m_mix_post_norm, m_ffn_pre_norm, m_ffn_post_norm, m_w_in, m_conv_qkv, m_a_log, m_dt_bias, m_gdn_norm, m_conv_sc, m_w_out, m_w_gate, m_w_up, m_w_down, v_meta_tokens, v_mix_pre_norm, v_mix_post_norm, v_ffn_pre_norm, v_ffn_post_norm, v_w_in, v_conv_qkv, v_a_log, v_dt_bias, v_gdn_norm, v_conv_sc, v_w_out, v_w_gate, v_w_up, v_w_down):
    bsz, seq, _ = x.shape
    lp = -(-(N_META + seq) // CH) * CH
    padf = lp - N_META - seq
    first_real = padf + N_META
    rows = bsz * lp
    nchunk = lp // CH
    tr = _div_tile(lp, 528)
    tm = _div_tile(rows, 1056)
    tm_big = _div_tile(rows, 2112)
    cx, cy, cc = _place()
    dev = 4 * cx + 2 * cy + cc

    small_shard = jnp.concatenate([
        jnp.pad(meta_tokens, ((0, 0), (0, 256 - DH))),
        jnp.pad(conv_qkv[0], ((0, 0), (0, 256 - 3 * GW // N_DEV))),
        jnp.pad(conv_sc[0], ((0, 5), (0, 256 - SW // N_DEV)))], axis=0)
    g_in, g_small = _all_gather("gather_w_in", [w_in[0].T.astype(BF16), small_shard])
    my_rest = [w_out[0].astype(BF16), w_gate[0].T.astype(BF16), w_up[0].T.astype(BF16), w_down[0].astype(BF16)]
    (grp_rest,), token = _split_start("gather_rest_start", [my_rest], _chipcast_pattern, lambda s: (N_DEV,) + s,
                                      after=g_in)
    after_start = token[0:1, 0:1]

    def cols_full(g):
        return jnp.transpose(g, (1, 0, 2)).reshape(g.shape[1], -1)

    win_t = g_in.reshape(IN_W, D)
    win_t = jnp.concatenate([win_t[:BA0 + 2 * HEADS], jnp.zeros((BA_W - 2 * HEADS, D), BF16),
                             win_t[BA0 + 2 * HEADS:]], axis=0)
    meta_full = cols_full(g_small[:, 0:16, :DH])
    convq_full = cols_full(g_small[:, 16:20, :3 * GW // N_DEV])
    convs_full = cols_full(g_small[:, 20:23, :SW // N_DEV])
    alog_v = jnp.pad(a_log, ((0, 0), (HEADS, DH - 2 * HEADS)))
    dtb_v = jnp.pad(dt_bias, ((0, 0), (HEADS, DH - 2 * HEADS)))

    head_rows = jnp.concatenate([jnp.zeros((padf, D), F32), meta_full], axis=0)
    h0, u1 = _pre_norm(x, head_rows, mix_pre_norm + after_start, bsz, lp, tr)
    (proj,) = _mm("proj_in", [(u1, win_t)], "nt", tm_big, 768, D, out_dtypes=(BF16,))
    (ba,) = _mm("proj_gate_logits", [(u1, win_t[BA0:SX0])], "nt", tm, BA_W, D)
    qkv = _gdn_prep(proj, convq_full, bsz, lp, padf)
    gb, gcs = _gates(ba, alog_v, dtb_v, bsz, lp, padf)

    def lane_bcast(cols):
        t = jnp.transpose(cols.reshape(bsz, lp, HEADS), (0, 2, 1)).reshape(bsz * HEADS, lp, 1)
        return jnp.broadcast_to(t, (bsz * HEADS, lp, DH))

    bb = lane_bcast(gb[:, 0:HEADS])
    gcb = lane_bcast(gcs[:, HEADS:2 * HEADS])
    gct = jnp.transpose(gcs[:, HEADS:2 * HEADS].reshape(bsz, nchunk, CH, HEADS), (0, 3, 1, 2)).reshape(
        bsz * HEADS, nchunk, 1, CH)
    u_loc, l1, l2, l3, w_t, k_dec, tinv_all, g_last = _gdn_local_fwd(qkv, gcb, bb, gct, bsz, lp)
    o, v_new, states = _gdn_scan_fwd(u_loc, l1, l2, g_last, bsz, lp)
    o, v_new = o.reshape(rows, GW), v_new.reshape(rows, GW)
    _, l_rest = _split_wait("gather_rest_arrive", grp_rest, _chipcast_pattern, o)
    (grp_out, grp_ffn, grp_down), token = _split_start(
        "gather_rest_forward", [l_rest[0:1], l_rest[1:3], l_rest[3:4]], _forward_pattern, None)
    mix_g, mix_s = _mix_heads(proj, o, gdn_norm + token[0:1, :], convs_full, bsz, lp, padf)
    mixin = jnp.concatenate([mix_g, mix_s], axis=1)
    _, (l_out,) = _split_wait("gather_w_out_wait", grp_out, _forward_pattern, mixin)
    wout = l_out.reshape(D, D)
    (mix,) = _mm("proj_out", [(mixin, wout)], "nn", tm, 512, D)
    h1, u2 = _mid_norms(h0, mix, mix_post_norm, ffn_pre_norm, bsz, lp, tr)
    _, (l_gate, l_up) = _split_wait("gather_ffn_wait", grp_ffn, _forward_pattern, u2)
    wgate_t, wup_t = l_gate.reshape(FF, D), l_up.reshape(FF, D)
    gate, up, act = _ffn_up(u2, wgate_t, wup_t, _div_tile(rows, 528), 1408)
    _, (l_down,) = _split_wait("gather_down_wait", grp_down, _forward_pattern, act)
    wdown = l_down.reshape(FF, D)
    (ffn,) = _mm("ffn_down", [(act, wdown)], "nn", tm, 512, 1408)

    dy, dffn, d_g4, loss_part = _loss_head(h1, ffn, loss_target, ffn_post_norm, bsz, lp, tr, first_real)
    (dwdown,) = _mm("dw_down", [(act, dffn)], "tn", 1408, 512, tm, out_dtypes=(BF16,))

    def swiglu_bwd(da, g, u):
        g, u = g.astype(F32), u.astype(F32)
        return da * u * _dsilu(g), da * _silu(g)

    dgate, dup = _mm("d_act", [(dffn, wdown)], "nt", tm, 1408, D, out_dtypes=(BF16, BF16),
                     epilogue=swiglu_bwd, extras=(gate, up))
    (dwgate_t,) = _mm("dw_gate", [(dgate, u2)], "tn", 1408, D, tm, out_dtypes=(BF16,))
    (dwup_t,) = _mm("dw_up", [(dup, u2)], "tn", 1408, D, tm, out_dtypes=(BF16,))
    (du2,) = _mm("d_u2", [(dgate, wgate_t), (dup, wup_t)], "nn", tm, D, 1408)
    dh1, dmix, d_g3, d_g2 = _mid_norms_bwd(h1, mix, du2, dy, ffn_pre_norm, mix_post_norm, bsz, lp, tr)
    (dwout,) = _mm("dw_out", [(mixin, dmix)], "tn", D, 512, tm, out_dtypes=(BF16,))

    def row_blocks(g):
        return g.reshape(4, 2, -1, g.shape[1])

    core = jnp.reshape(cc, (1,)).astype(jnp.int32)
    chip = jnp.reshape(2 * cx + cy, (1,)).astype(jnp.int32)

    def reduce_start(tag, grads):
        (grp,), tok = _split_start("rs_sibling_start_" + tag, [grads], _sibling_pattern, lambda s: s[:1] + s[2:])
        return grp, tok

    def reduce_mid(tag, grp, nms, after):
        mine, lands = _split_wait("rs_sibling_wait_" + tag, grp, _sibling_pattern, after)
        parts = [_pair_sum("pair_sum_" + nm, g, r, core) for nm, g, r in zip(nms, mine, lands)]
        (grp2,), tok = _split_start("rs_chip_start_" + tag, [parts], _chip_pattern, lambda s: (3,) + s[1:])
        return grp2, tok

    def reduce_end(tag, grp2, after):
        return _split_wait("rs_chip_wait_" + tag, grp2, _chip_pattern, after)

    names_ffn = ["w_down", "w_gate", "w_up", "w_out"]
    rs_ffn, tok = reduce_start("ffn", [row_blocks(dwdown), row_blocks(dwgate_t), row_blocks(dwup_t), row_blocks(dwout)])
    (dmixin,) = _mm("d_mixin", [(dmix, wout)], "nt", tm, 512, D, after=tok)
    d_o, dz, dsx, dsb, dsc, d_gn, d_convs = _mix_heads_bwd(proj, o, gdn_norm, convs_full, dmixin, bsz, lp, padf)
    rs_ffn, tok = reduce_mid("ffn", rs_ffn, names_ffn, d_o)
    dv_new, dstates = _gdn_scan_bwd(d_o, l3, w_t, k_dec, g_last + tok[0, 0], bsz, lp)
    dq, dk, dv, dg_b, dbeta_b = _gdn_local_bwd(qkv, gcb, bb, gct, tinv_all, states, dstates, v_new,
                                               dv_new.reshape(rows, GW), d_o, bsz, lp)
    dpqkv, d_convq = _gdn_prep_bwd(proj, convq_full, dq, dk, dv, bsz, lp, padf)
    dba, d_scal = _gates_bwd(ba, alog_v, dtb_v, dbeta_b, dg_b, gb, bsz, lp, padf)
    dproj = jnp.concatenate([dpqkv, dz, dba, dsx, dsb, dsc], axis=1)
    (dwin_t,) = _mm("dw_in", [(dproj, u1)], "tn", 1280, D, tm, out_dtypes=(BF16,))
    parts_ffn, recv_ffn = reduce_end("ffn", rs_ffn, dwin_t)
    dwin_t = jnp.concatenate([dwin_t[:BA0 + 2 * HEADS], dwin_t[SX0:]], axis=0)
    rs_in, tok = reduce_start("in", [row_blocks(dwin_t)])

    big_w = {"w_in": (w_in, m_w_in, v_w_in), "w_out": (w_out, m_w_out, v_w_out), "w_gate": (w_gate, m_w_gate, v_w_gate),
             "w_up": (w_up, m_w_up, v_w_up), "w_down": (w_down, m_w_down, v_w_down)}

    def update(nm, parts, recv, after=None):
        w, m, v = big_w[nm]
        if nm in ("w_out", "w_down"):
            return _adamw_reduced("adamw_" + nm, parts, chip, recv, w[0], m[0], v[0], after)
        res = _adamw_reduced("adamw_" + nm, parts, chip, recv, w[0].T, m[0].T, v[0].T, after)
        return [t.T for t in res]

    big_out = {nm: update(nm, p, r, tok) for nm, p, r in zip(names_ffn, parts_ffn, recv_ffn)}
    ffn_done = sum(big_out[nm][1][:1, :1] for nm in names_ffn)
    rs_in, tok = reduce_mid("in", rs_in, ["w_in"], ffn_done)
    (du1,) = _mm("d_u1", [(dproj, win_t)], "nn", tm_big, D, 768, after=tok)
    dh0, d_g1, d_meta = _pre_norm_bwd(h0, du1, dh1, mix_pre_norm, bsz, lp, tr, padf)
    grad_x = dh0.reshape(bsz, lp, D)[:, first_real:]

    d_alog = d_scal[0:1, HEADS:2 * HEADS]
    d_dtb = d_scal[1:2, HEADS:2 * HEADS]
    slab = _pack_small([d_g1, d_g2, d_g3, d_g4], d_meta, d_convq, d_convs, d_alog, d_dtb, d_gn, loss_part)
    (grp_small,), _ = _split_start("small_grads_start", [[slab]], _gather_pattern, lambda s: (N_DEV,) + s)
    _, (slabs,) = _split_wait("small_grads_wait", grp_small, _gather_pattern, grad_x)
    off_meta = dev * DH
    off_cq = dev * (3 * GW // N_DEV)
    off_cs = dev * (SW // N_DEV)

    def shard_of(s):
        gains, meta, cq, cs, al, db, gn = _unpack_small(s)
        return gains, lax.dynamic_slice_in_dim(meta, off_meta, DH, 1), \
            lax.dynamic_slice_in_dim(cq, off_cq, 3 * GW // N_DEV, 1), \
            lax.dynamic_slice_in_dim(cs, off_cs, SW // N_DEV, 1), al, db, gn, s[27:28, 3 * DH:4 * DH]

    slabs_local = jax.vmap(lambda s: _pack_small(*shard_of(s)))(slabs)
    w_small = _pack_small([mix_pre_norm, mix_post_norm, ffn_pre_norm, ffn_post_norm], meta_tokens, conv_qkv[0],
                          conv_sc[0], a_log, dt_bias, gdn_norm)
    m_small = _pack_small([m_mix_pre_norm, m_mix_post_norm, m_ffn_pre_norm, m_ffn_post_norm], m_meta_tokens,
                          m_conv_qkv[0], m_conv_sc[0], m_a_log, m_dt_bias, m_gdn_norm)
    v_small = _pack_small([v_mix_pre_norm, v_mix_post_norm, v_ffn_pre_norm, v_ffn_post_norm], v_meta_tokens,
                          v_conv_qkv[0], v_conv_sc[0], v_a_log, v_dt_bias, v_gdn_norm)
    small_out, loss = _adamw_small(slabs_local, w_small, m_small, v_small)
    parts_in, recv_in = reduce_end("in", rs_in, small_out[1]["gdn_norm"])
    big_out["w_in"] = update("w_in", parts_in[0], recv_in[0])
    names = ["w_in", "w_out", "w_gate", "w_up", "w_down"]

    order = ["meta_tokens", "mix_pre_norm", "mix_post_norm", "ffn_pre_norm", "ffn_post_norm", "w_in", "conv_qkv",
             "a_log", "dt_bias", "gdn_norm", "conv_sc", "w_out", "w_gate", "w_up", "w_down"]
    outs = [loss[0, 0], grad_x]
    for kind in range(4):
        leaves = dict(small_out[kind])
        for nm in names:
            leaves[nm] = big_out[nm][kind][None]
        outs += [leaves[nm] for nm in order]
    return tuple(outs)
```

```python
import functools

import jax
import jax.numpy as jnp
from jax import lax
from jax.experimental import pallas as pl
from jax.experimental.pallas import tpu as pltpu

F32 = jnp.float32
BF16 = jnp.bfloat16
HIGHEST = lax.Precision.HIGHEST
MESH = pl.DeviceIdType.MESH

D = 1024
N_META = 16
HEADS = 4
DH = 128
GW = HEADS * DH
SW = D - GW
FF = 2816
CH = 64
KQ = 4
KS = 3
EPS = 1e-6
IN_W = 3 * GW + GW + 2 * HEADS + 3 * SW
Q0, K0, V0, Z0, BA0, SX0, SB0, SC0, PW = 0, 512, 1024, 1536, 2048, 2304, 2816, 3328, 3840
BA_W = SX0 - BA0
N_DEV = 8
VMEM_LIMIT = 56 * 1024 * 1024

ADAM_LR, ADAM_B1, ADAM_B2, ADAM_EPS, ADAM_WD, ADAM_STEP = 0.001, 0.9, 0.999, 1e-08, 0.01, 10


def _cparams(sem):
    return pltpu.CompilerParams(dimension_semantics=sem, vmem_limit_bytes=VMEM_LIMIT)


def _div_tile(n, target, mult=16):
    best = None
    for t in range(mult, min(n, target) + 1, mult):
        if n % t == 0:
            best = t
    assert best is not None, (n, target)
    return best


def _silu(x):
    return x * jax.nn.sigmoid(x)


def _dsilu(x):
    s = jax.nn.sigmoid(x)
    return s * (1.0 + x * (1.0 - s))


def _bdot(a, b, dims):
    return lax.dot_general(a.astype(BF16), b.astype(BF16), (dims, ((), ())), preferred_element_type=F32)


NN = ((1,), (0,))
NT = ((1,), (1,))
TN = ((0,), (0,))


def _fdot(a, b, dims=NN):
    return lax.dot_general(a, b, (dims, ((), ())), precision=HIGHEST, preferred_element_type=F32)


def _mm(name, pairs, mode, tm, tn, tk, out_dtypes=(F32,), epilogue=None, extras=(), after=None):
    a0, b0 = pairs[0]
    if mode == "nn":
        (m, k), n = a0.shape, b0.shape[1]
    elif mode == "nt":
        (m, k), n = a0.shape, b0.shape[0]
    else:
        (k, m), n = a0.shape, b0.shape[1]
    assert m % tm == 0 and n % tn == 0 and k % tk == 0, (name, m, n, k, tm, tn, tk)
    nk = k // tk
    npair = len(pairs)
    nex = len(extras)
    nout = len(out_dtypes)
    ntok = 0 if after is None else 1
    dims = {"nn": NN, "nt": NT, "tn": TN}[mode]

    def body(*refs):
        ab = refs[:2 * npair]
        ex = refs[2 * npair:2 * npair + nex]
        outs = refs[2 * npair + nex + ntok:2 * npair + nex + ntok + nout]
        acc_ref = refs[-1]
        kk = pl.program_id(2)

        part = None
        for p in range(npair):
            d = lax.dot_general(ab[2 * p][...], ab[2 * p + 1][...], (dims, ((), ())), preferred_element_type=F32)
            part = d if part is None else part + d

        def finish(acc):
            res = epilogue(acc, *[e[...] for e in ex]) if epilogue is not None else (acc,)
            for o, r in zip(outs, res):
                o[...] = r.astype(o.dtype)

        if nk == 1:
            finish(part)
        else:
            @pl.when(kk == 0)
            def _():
                acc_ref[...] = part

            @pl.when(kk > 0)
            def _():
                acc_ref[...] += part

            @pl.when(kk == nk - 1)
            def _():
                finish(acc_ref[...])

    if mode == "nn":
        a_spec = pl.BlockSpec((tm, tk), lambda i, j, q: (i, q))
        b_spec = pl.BlockSpec((tk, tn), lambda i, j, q: (q, j))
    elif mode == "nt":
        a_spec = pl.BlockSpec((tm, tk), lambda i, j, q: (i, q))
        b_spec = pl.BlockSpec((tn, tk), lambda i, j, q: (j, q))
    else:
        a_spec = pl.BlockSpec((tk, tm), lambda i, j, q: (q, i))
        b_spec = pl.BlockSpec((tk, tn), lambda i, j, q: (q, j))
    o_spec = pl.BlockSpec((tm, tn), lambda i, j, q: (i, j))
    flat = [t for pr in pairs for t in pr]
    res = pl.pallas_call(
        body, name=name,
        grid=(m // tm, n // tn, nk),
        in_specs=[a_spec, b_spec] * npair + [o_spec] * nex + [pl.BlockSpec((8, DH), lambda i, j, q: (0, 0))] * ntok,
        out_specs=[o_spec] * nout,
        out_shape=[jax.ShapeDtypeStruct((m, n), dt) for dt in out_dtypes],
        scratch_shapes=[pltpu.VMEM((tm, tn) if nk > 1 else (8, DH), F32)],
        compiler_params=_cparams(("parallel", "parallel", "arbitrary")),
    )(*flat, *extras, *([] if after is None else [after]))
    return res


def _ffn_up(u2, wg, wu, tm, tn):
    m, k = u2.shape
    n = wg.shape[0]

    def body(a_ref, g_ref, u_ref, gate_ref, up_ref, act_ref):
        a = a_ref[...]
        g = lax.dot_general(a, g_ref[...], (NT, ((), ())), preferred_element_type=F32)
        u = lax.dot_general(a, u_ref[...], (NT, ((), ())), preferred_element_type=F32)
        gate_ref[...] = g.astype(BF16)
        up_ref[...] = u.astype(BF16)
        act_ref[...] = (_silu(g) * u).astype(BF16)

    o_spec = pl.BlockSpec((tm, tn), lambda j, i: (i, j))
    w_spec = pl.BlockSpec((tn, k), lambda j, i: (j, 0))
    return pl.pallas_call(
        body, name="ffn_up", grid=(n // tn, m // tm),
        in_specs=[pl.BlockSpec((tm, k), lambda j, i: (i, 0)), w_spec, w_spec],
        out_specs=[o_spec, o_spec, o_spec],
        out_shape=[jax.ShapeDtypeStruct((m, n), BF16)] * 3,
        compiler_params=_cparams(("parallel", "parallel")),
    )(u2, wg, wu)


def _rms(x, gain):
    r = lax.rsqrt(jnp.mean(x * x, axis=-1, keepdims=True) + EPS)
    return x * r * gain, r


def _rms_bwd(x, gain, dy):
    r = lax.rsqrt(jnp.mean(x * x, axis=-1, keepdims=True) + EPS)
    dyw = dy * gain
    dx = r * dyw - x * (r * r * r) * jnp.mean(dyw * x, axis=-1, keepdims=True)
    return dx, dy * x * r


def _row_specs(tr, width, nb):
    return pl.BlockSpec((tr, width), lambda b, j: (b * nb + j, 0))


def _vec_spec(width):
    return pl.BlockSpec((1, width), lambda b, j: (0, 0))


def _first_step(b, j):
    return jnp.logical_and(b == 0, j == 0)


def _pre_norm(x, head_rows, gain, bsz, lp, tr):
    nb = lp // tr
    first_real = head_rows.shape[0]
    assert first_real % 8 == 0 and first_real < tr
    x_spec = pl.BlockSpec((pl.Squeezed(), pl.Element(tr), pl.Element(D)),
                          lambda b, j: (b, pl.multiple_of(jnp.maximum(j * tr - first_real, 0), 8), 0))

    def body(x_ref, hd_ref, g_ref, h_ref, u_ref):
        j = pl.program_id(1)
        xt = x_ref[...]
        h = jnp.where(j == 0, jnp.concatenate([hd_ref[...], xt[:tr - first_real]], axis=0), xt)
        h_ref[...] = h
        u_ref[...] = _rms(h, g_ref[...])[0].astype(BF16)

    rs = _row_specs(tr, D, nb)
    return pl.pallas_call(
        body, name="mix_pre_norm", grid=(bsz, nb),
        in_specs=[x_spec, pl.BlockSpec((first_real, D), lambda b, j: (0, 0)), _vec_spec(D)],
        out_specs=[rs, rs],
        out_shape=[jax.ShapeDtypeStruct((bsz * lp, D), F32), jax.ShapeDtypeStruct((bsz * lp, D), BF16)],
        compiler_params=_cparams(("parallel", "parallel")),
    )(x, head_rows, gain)


def _mid_norms(h0, mix, g_post, g_pre, bsz, lp, tr):
    nb = lp // tr

    def body(h_ref, m_ref, gp_ref, gq_ref, h1_ref, u2_ref):
        h1 = h_ref[...] + _rms(m_ref[...], gp_ref[...])[0]
        h1_ref[...] = h1
        u2_ref[...] = _rms(h1, gq_ref[...])[0].astype(BF16)

    rs = _row_specs(tr, D, nb)
    return pl.pallas_call(
        body, name="mid_norms", grid=(bsz, nb),
        in_specs=[rs, rs, _vec_spec(D), _vec_spec(D)],
        out_specs=[rs, rs],
        out_shape=[jax.ShapeDtypeStruct(h0.shape, F32), jax.ShapeDtypeStruct(h0.shape, BF16)],
        compiler_params=_cparams(("parallel", "parallel")),
    )(h0, mix, g_post, g_pre)


def _loss_head(h1, ffn, tgt, g_post, bsz, lp, tr, first_real):
    nb = lp // tr
    assert first_real % 8 == 0 and first_real < tr
    tgt_spec = pl.BlockSpec((pl.Squeezed(), pl.Element(tr), pl.Element(D)),
                            lambda b, j: (b, pl.multiple_of(jnp.maximum(j * tr - first_real, 0), 8), 0))

    def body(h_ref, f_ref, t_ref, g_ref, dy_ref, dffn_ref, dg_ref, loss_ref):
        b, j = pl.program_id(0), pl.program_id(1)
        f = f_ref[...]
        g = g_ref[...]
        y = h_ref[...] + _rms(f, g)[0]
        t = j * tr + lax.broadcasted_iota(jnp.int32, (tr, 1), 0)
        tg = t_ref[...]
        tg = jnp.where(j == 0, jnp.concatenate([tg[tr - first_real:], tg[:tr - first_real]], axis=0), tg)
        err = jnp.where(t >= first_real, y - tg, 0.0)
        dy = err * (1.0 / D)
        dy_ref[...] = dy
        dx, dgc = _rms_bwd(f, g, dy)
        dffn_ref[...] = dx.astype(BF16)
        lsum = jnp.sum(jnp.sum(err * err, axis=0, keepdims=True), axis=1, keepdims=True) * (0.5 / D)

        @pl.when(_first_step(b, j))
        def _():
            dg_ref[...] = jnp.zeros_like(dg_ref)
            loss_ref[...] = jnp.zeros_like(loss_ref)

        dg_ref[...] += jnp.sum(dgc, axis=0, keepdims=True)
        loss_ref[...] += jnp.broadcast_to(lsum, loss_ref.shape)

    rs = _row_specs(tr, D, nb)
    return pl.pallas_call(
        body, name="loss_head", grid=(bsz, nb),
        in_specs=[rs, rs, tgt_spec, _vec_spec(D)],
        out_specs=[rs, rs, _vec_spec(D), _vec_spec(DH)],
        out_shape=[jax.ShapeDtypeStruct(h1.shape, F32), jax.ShapeDtypeStruct(h1.shape, BF16),
                   jax.ShapeDtypeStruct((1, D), F32), jax.ShapeDtypeStruct((1, DH), F32)],
        compiler_params=_cparams(("arbitrary", "arbitrary")),
    )(h1, ffn, tgt, g_post)


def _mid_norms_bwd(h1, mix, du2, dy, g_pre, g_post, bsz, lp, tr):
    nb = lp // tr

    def body(h_ref, m_ref, du_ref, dy_ref, gq_ref, gp_ref, dh1_ref, dmix_ref, dgq_ref, dgp_ref):
        b, j = pl.program_id(0), pl.program_id(1)
        dx, dgq = _rms_bwd(h_ref[...], gq_ref[...], du_ref[...])
        dh1 = dy_ref[...] + dx
        dh1_ref[...] = dh1
        dm, dgp = _rms_bwd(m_ref[...], gp_ref[...], dh1)
        dmix_ref[...] = dm.astype(BF16)

        @pl.when(_first_step(b, j))
        def _():
            dgq_ref[...] = jnp.zeros_like(dgq_ref)
            dgp_ref[...] = jnp.zeros_like(dgp_ref)

        dgq_ref[...] += jnp.sum(dgq, axis=0, keepdims=True)
        dgp_ref[...] += jnp.sum(dgp, axis=0, keepdims=True)

    rs = _row_specs(tr, D, nb)
    return pl.pallas_call(
        body, name="mid_norms_bwd", grid=(bsz, nb),
        in_specs=[rs, rs, rs, rs, _vec_spec(D), _vec_spec(D)],
        out_specs=[rs, rs, _vec_spec(D), _vec_spec(D)],
        out_shape=[jax.ShapeDtypeStruct(h1.shape, F32), jax.ShapeDtypeStruct(h1.shape, BF16),
                   jax.ShapeDtypeStruct((1, D), F32), jax.ShapeDtypeStruct((1, D), F32)],
        compiler_params=_cparams(("arbitrary", "arbitrary")),
    )(h1, mix, du2, dy, g_pre, g_post)


def _pre_norm_bwd(h0, du1, dh1, gain, bsz, lp, seq, padf):
    first_real = lp - seq
    tx = _div_tile(seq, 512)
    assert first_real % 8 == 0 and lp % first_real == 0

    def tokens(h_ref, du_ref, dh1_ref, g_ref, gx_ref, dg_ref):
        b, j = pl.program_id(0), pl.program_id(1)
        dx, dgc = _rms_bwd(h_ref[...], g_ref[...], du_ref[...])
        gx_ref[...] = dh1_ref[...] + dx

        @pl.when(_first_step(b, j))
        def _():
            dg_ref[...] = jnp.zeros_like(dg_ref)

        dg_ref[...] += jnp.sum(dgc, axis=0, keepdims=True)

    window = pl.BlockSpec((pl.Element(tx), pl.Element(D)),
                          lambda b, j: (pl.multiple_of(b * lp + first_real + j * tx, 8), 0))
    grad_x, dg_tokens = pl.pallas_call(
        tokens, name="pre_norm_bwd", grid=(bsz, seq // tx),
        in_specs=[window, window, window, _vec_spec(D)],
        out_specs=[pl.BlockSpec((pl.Squeezed(), tx, D), lambda b, j: (b, j, 0)), _vec_spec(D)],
        out_shape=[jax.ShapeDtypeStruct((bsz, seq, D), F32), jax.ShapeDtypeStruct((1, D), F32)],
        compiler_params=_cparams(("arbitrary", "arbitrary")),
    )(h0, du1, dh1, gain)

    def head(h_ref, du_ref, dh1_ref, g_ref, dgt_ref, dg_ref, dmeta_ref):
        b = pl.program_id(0)
        dx, dgc = _rms_bwd(h_ref[...], g_ref[...], du_ref[...])
        dh0 = dh1_ref[...] + dx

        @pl.when(b == 0)
        def _():
            dg_ref[...] = dgt_ref[...]
            dmeta_ref[...] = jnp.zeros_like(dmeta_ref)

        dg_ref[...] += jnp.sum(dgc, axis=0, keepdims=True)
        dmeta_ref[...] += dh0[padf:padf + N_META, :]

    rows = pl.BlockSpec((first_real, D), lambda b: (b * (lp // first_real), 0))
    vec = pl.BlockSpec((1, D), lambda b: (0, 0))
    dg, dmeta = pl.pallas_call(
        head, name="pre_norm_bwd_head", grid=(bsz,),
        in_specs=[rows, rows, rows, vec, vec],
        out_specs=[vec, pl.BlockSpec((N_META, D), lambda b: (0, 0))],
        out_shape=[jax.ShapeDtypeStruct((1, D), F32), jax.ShapeDtypeStruct((N_META, D), F32)],
        compiler_params=_cparams(("arbitrary",)),
    )(h0, du1, dh1, gain, dg_tokens)
    return grad_x, dg, dmeta


def _shift_down(x, s, lp):
    return x if s == 0 else pltpu.roll(x, s, axis=0)


def _shift_up(x, s, lp):
    return x if s == 0 else pltpu.roll(x, lp - s, axis=0)


def _conv_fwd(x, w, taps, lp):
    y = None
    for i in range(taps):
        term = _shift_down(x, taps - 1 - i, lp) * w[i:i + 1, :]
        y = term if y is None else y + term
    return y


def _conv_bwd(x, w, dy, taps, lp):
    dx = None
    dws = []
    for i in range(taps):
        s = taps - 1 - i
        term = _shift_up(dy, s, lp) * w[i:i + 1, :]
        dx = term if dx is None else dx + term
        dws.append(jnp.sum(dy * _shift_down(x, s, lp), axis=0, keepdims=True))
    return dx, jnp.concatenate(dws, axis=0)


def _slab(lp, col0):
    return pl.BlockSpec((lp, DH), lambda b, s: (b, col0 // DH + s))


def _qkv_act(c, s):
    y = _silu(c)
    r = lax.rsqrt(jnp.sum(y * y, axis=-1, keepdims=True) + EPS)
    scale = jnp.where(s < HEADS, DH ** -0.5, 1.0)
    return y, r, scale


def _gdn_prep(proj, conv_qkv, bsz, lp, padf):
    def body(p_ref, w_ref, o_ref):
        s = pl.program_id(1)
        c = _conv_fwd(p_ref[...].astype(F32), w_ref[...], KQ, lp)
        y, r, scale = _qkv_act(c, s)
        out = jnp.where(s < 2 * HEADS, y * (r * scale), y)
        t = lax.broadcasted_iota(jnp.int32, (lp, 1), 0)
        o_ref[...] = jnp.where(t >= padf, out, 0.0)

    return pl.pallas_call(
        body, name="gdn_prep", grid=(bsz, 3 * HEADS),
        in_specs=[_slab(lp, Q0), pl.BlockSpec((KQ, DH), lambda b, s: (0, s))],
        out_specs=_slab(lp, 0),
        out_shape=jax.ShapeDtypeStruct((bsz * lp, 3 * GW), F32),
        compiler_params=_cparams(("parallel", "parallel")),
    )(proj, conv_qkv)


def _gdn_prep_bwd(proj, conv_qkv, dq, dk, dv, bsz, lp, padf):
    def body(p_ref, w_ref, dq_ref, dk_ref, dv_ref, dp_ref, dw_ref):
        b, s = pl.program_id(0), pl.program_id(1)
        x = p_ref[...].astype(F32)
        w = w_ref[...]
        c = _conv_fwd(x, w, KQ, lp)
        y, r, scale = _qkv_act(c, s)
        d = jnp.where(s < HEADS, dq_ref[...], jnp.where(s < 2 * HEADS, dk_ref[...], dv_ref[...]))
        dn = d * scale
        n = y * r
        dy_norm = r * (dn - n * jnp.sum(dn * n, axis=-1, keepdims=True))
        dyy = jnp.where(s < 2 * HEADS, dy_norm, dn)
        t = lax.broadcasted_iota(jnp.int32, (lp, 1), 0)
        dc = jnp.where(t >= padf, dyy * _dsilu(c), 0.0)
        dx, dw = _conv_bwd(x, w, dc, KQ, lp)
        dp_ref[...] = jnp.where(t >= padf, dx, 0.0).astype(BF16)

        @pl.when(jnp.logical_and(b == 0, s == 0))
        def _():
            dw_ref[...] = jnp.zeros_like(dw_ref)

        col_block = jnp.right_shift(lax.broadcasted_iota(jnp.int32, (KQ, 3 * GW), 1), DH.bit_length() - 1)
        dw_ref[...] += jnp.where(col_block == s, jnp.tile(dw, (1, 3 * HEADS)), 0.0)

    sl = lambda col0: pl.BlockSpec((lp, DH), lambda b, s: (b, col0 // DH + s))
    part = lambda k: pl.BlockSpec((lp, DH), lambda b, s: (b, jnp.clip(s - k * HEADS, 0, HEADS - 1)))
    return pl.pallas_call(
        body, name="gdn_prep_bwd", grid=(bsz, 3 * HEADS),
        in_specs=[sl(Q0), pl.BlockSpec((KQ, DH), lambda b, s: (0, s)), part(0), part(1), part(2)],
        out_specs=[sl(0), pl.BlockSpec((KQ, 3 * GW), lambda b, s: (0, 0))],
        out_shape=[jax.ShapeDtypeStruct((bsz * lp, 3 * GW), BF16), jax.ShapeDtypeStruct((KQ, 3 * GW), F32)],
        compiler_params=_cparams(("arbitrary", "arbitrary")),
    )(proj, conv_qkv, dq, dk, dv)


def _softplus(x):
    e = jnp.exp(-jnp.abs(x))
    one_e = 1.0 + e
    l1p = jnp.where(one_e == 1.0, e, jnp.log(one_e) * (e / (one_e - 1.0)))
    return jnp.maximum(x, 0.0) + l1p


def _tri(n, kind):
    i = lax.broadcasted_iota(jnp.int32, (n, n), 0)
    j = lax.broadcasted_iota(jnp.int32, (n, n), 1)
    return {"incl": i >= j, "strict": i > j, "upper": i <= j}[kind]


def _gates(proj, alog_v, dtb_v, bsz, lp, padf):
    nchunk = lp // CH

    def body(p_ref, al_ref, dt_ref, gb_ref, cs_ref):
        x = p_ref[...].astype(F32)
        lane = lax.broadcasted_iota(jnp.int32, (lp, DH), 1)
        t = lax.broadcasted_iota(jnp.int32, (lp, DH), 0)
        beta = jax.nn.sigmoid(x)
        g = -jnp.exp(al_ref[...]) * _softplus(x + dt_ref[...])
        gb = jnp.where(lane < HEADS, beta, jnp.where(lane < 2 * HEADS, g, 0.0))
        gb_ref[...] = jnp.where(t >= padf, gb, 0.0)
        ltri = _tri(CH, "incl").astype(BF16)
        rest = [gb_ref[pl.ds(n * CH, CH), :] for n in range(nchunk)]
        sums = None
        for _ in range(3):
            piece = [v.astype(BF16) for v in rest]
            rest = [v - p.astype(F32) for v, p in zip(rest, piece)]
            part = [jnp.dot(ltri, p, preferred_element_type=F32) for p in piece]
            sums = part if sums is None else [a + b for a, b in zip(sums, part)]
        for n in range(nchunk):
            cs_ref[pl.ds(n * CH, CH), :] = sums[n]

    blk = pl.BlockSpec((lp, DH), lambda b: (b, 0))
    vec = pl.BlockSpec((1, DH), lambda b: (0, 0))
    out = pl.BlockSpec((lp, DH), lambda b: (b, 0))
    return pl.pallas_call(
        body, name="gates", grid=(bsz,),
        in_specs=[blk, vec, vec], out_specs=[out, out],
        out_shape=[jax.ShapeDtypeStruct((bsz * lp, DH), F32)] * 2,
        compiler_params=_cparams(("parallel",)),
    )(proj, alog_v, dtb_v)


def _gates_bwd(proj, alog_v, dtb_v, dbeta_b, dg_b, gb, bsz, lp, padf):
    def body(p_ref, al_ref, dt_ref, db_ref, dg_ref, gb_ref, dl_ref, dv_ref):
        b = pl.program_id(0)
        x = p_ref[...].astype(F32)
        lane = lax.broadcasted_iota(jnp.int32, (lp, DH), 1)
        t = lax.broadcasted_iota(jnp.int32, (lp, DH), 0)
        dbeta = jnp.zeros((lp, DH), F32)
        dg = jnp.zeros((lp, DH), F32)
        for h in range(HEADS):
            dbeta = jnp.where(lane == h, db_ref[h], dbeta)
            dg = jnp.where(lane == HEADS + h, dg_ref[h], dg)
        real = t >= padf
        dbeta = jnp.where(real, dbeta, 0.0)
        dg = jnp.where(real, dg, 0.0)
        sb = jax.nn.sigmoid(x)
        neg_ea = -jnp.exp(al_ref[...])
        dsp = dg * neg_ea * jax.nn.sigmoid(x + dt_ref[...])
        dl = dbeta * sb * (1.0 - sb) + dsp
        dl_ref[...] = jnp.concatenate([dl, jnp.zeros((lp, BA_W - DH), F32)], axis=1).astype(BF16)
        dalog = jnp.sum(dg * gb_ref[...], axis=0, keepdims=True)
        ddt = jnp.sum(dsp, axis=0, keepdims=True)

        @pl.when(b == 0)
        def _():
            dv_ref[...] = jnp.zeros_like(dv_ref)

        dv_ref[...] += jnp.concatenate([dalog, ddt], axis=0)

    blk = pl.BlockSpec((lp, DH), lambda b: (b, 0))
    vec = pl.BlockSpec((1, DH), lambda b: (0, 0))
    hb = pl.BlockSpec((HEADS, lp, DH), lambda b: (b, 0, 0))
    return pl.pallas_call(
        body, name="gates_bwd", grid=(bsz,),
        in_specs=[blk, vec, vec, hb, hb, pl.BlockSpec((lp, DH), lambda b: (b, 0))],
        out_specs=[pl.BlockSpec((lp, BA_W), lambda b: (b, 0)), pl.BlockSpec((2, DH), lambda b: (0, 0))],
        out_shape=[jax.ShapeDtypeStruct((bsz * lp, BA_W), BF16), jax.ShapeDtypeStruct((2, DH), F32)],
        compiler_params=_cparams(("arbitrary",)),
    )(proj, alog_v, dtb_v, dbeta_b, dg_b, gb)


def _mix_heads(proj, o, gdn_norm, conv_sc, bsz, lp, padf):
    def body(o_ref, z_ref, gn_ref, sx_ref, sb_ref, sc_ref, w_ref, og_ref, os_ref):
        og_ref[...] = (_rms(o_ref[...], gn_ref[...])[0] * _silu(z_ref[...].astype(F32))).astype(BF16)
        conv = _conv_fwd(sc_ref[...].astype(F32) * sx_ref[...].astype(F32), w_ref[...], KS, lp)
        t = lax.broadcasted_iota(jnp.int32, (lp, 1), 0)
        os_ref[...] = jnp.where(t >= padf, sb_ref[...].astype(F32) * conv, 0.0).astype(BF16)

    half = _slab(lp, 0)
    return pl.pallas_call(
        body, name="mix_heads", grid=(bsz, HEADS),
        in_specs=[half, _slab(lp, Z0), pl.BlockSpec((1, DH), lambda b, s: (0, 0)),
                  _slab(lp, SX0), _slab(lp, SB0), _slab(lp, SC0), pl.BlockSpec((KS, DH), lambda b, s: (0, s))],
        out_specs=[half, half],
        out_shape=[jax.ShapeDtypeStruct((bsz * lp, GW), BF16)] * 2,
        compiler_params=_cparams(("parallel", "parallel")),
    )(o, proj, gdn_norm, proj, proj, proj, conv_sc)


def _mix_heads_bwd(proj, o, gdn_norm, conv_sc, dmixin, bsz, lp, padf):
    def body(o_ref, z_ref, gn_ref, sx_ref, sb_ref, sc_ref, w_ref, dg_ref, ds_ref,
             do_ref, dz_ref, dsx_ref, dsb_ref, dsc_ref, dgn_ref, dw_ref):
        s, b = pl.program_id(0), pl.program_id(1)
        t = lax.broadcasted_iota(jnp.int32, (lp, 1), 0)
        real = t >= padf
        o, z, gn, d = o_ref[...], z_ref[...].astype(F32), gn_ref[...], dg_ref[...]
        sz = _silu(z)
        on = _rms(o, gn)[0]
        dz_ref[...] = (d * on * _dsilu(z)).astype(BF16)
        dox, dgn = _rms_bwd(o, gn, d * sz)
        do_ref[...] = dox
        sx, sb, sc = sx_ref[...].astype(F32), sb_ref[...].astype(F32), sc_ref[...].astype(F32)
        w, e = w_ref[...], ds_ref[...]
        e = jnp.where(real, e, 0.0)
        xin = sc * sx
        conv = _conv_fwd(xin, w, KS, lp)
        dsb_ref[...] = (e * conv).astype(BF16)
        dxin, dw = _conv_bwd(xin, w, e * sb, KS, lp)
        dxin = jnp.where(real, dxin, 0.0)
        dsx_ref[...] = (dxin * sc).astype(BF16)
        dsc_ref[...] = (dxin * sx).astype(BF16)

        @pl.when(jnp.logical_and(s == 0, b == 0))
        def _():
            dgn_ref[...] = jnp.zeros_like(dgn_ref)

        @pl.when(b == 0)
        def _():
            dw_ref[...] = jnp.zeros_like(dw_ref)

        dgn_ref[...] += jnp.sum(dgn, axis=0, keepdims=True)
        dw_ref[...] += dw

    sl = lambda col0: pl.BlockSpec((lp, DH), lambda s, b: (b, col0 // DH + s))
    half = sl(0)
    return pl.pallas_call(
        body, name="mix_heads_bwd", grid=(HEADS, bsz),
        in_specs=[half, sl(Z0), pl.BlockSpec((1, DH), lambda s, b: (0, 0)), sl(SX0), sl(SB0), sl(SC0),
                  pl.BlockSpec((KS, DH), lambda s, b: (0, s)), sl(0), sl(GW)],
        out_specs=[half] * 5 + [pl.BlockSpec((1, DH), lambda s, b: (0, 0)), pl.BlockSpec((KS, DH), lambda s, b: (0, s))],
        out_shape=[jax.ShapeDtypeStruct((bsz * lp, GW), F32)] + [jax.ShapeDtypeStruct((bsz * lp, GW), BF16)] * 4
        + [jax.ShapeDtypeStruct((1, DH), F32), jax.ShapeDtypeStruct((KS, SW), F32)],
        compiler_params=_cparams(("arbitrary", "arbitrary")),
    )(o, proj, gdn_norm, proj, proj, proj, conv_sc, dmixin, dmixin)


def _split_dot(a, b, dims=NN):
    ah = a.astype(BF16)
    al = (a - ah.astype(F32)).astype(BF16)
    bh = b.astype(BF16)
    bl = (b - bh.astype(F32)).astype(BF16)

    def d(x, y):
        return lax.dot_general(x, y, (dims, ((), ())), preferred_element_type=F32)

    return d(ah, bh) + (d(ah, bl) + d(al, bh))


def _eye(n):
    return (lax.broadcasted_iota(jnp.int32, (n, n), 0) == lax.broadcasted_iota(jnp.int32, (n, n), 1)).astype(F32)


def _each(f, *lists):
    return [f(*xs) for xs in zip(*lists)]


def _inv_unit_lower(a_list):
    eye = _eye(CH)
    p = _each(lambda a: -a, a_list)
    t = _each(lambda x: eye + x, p)
    p = _each(lambda x: _split_dot(x, x), p)
    for level in range(5):
        if level < 4:
            prod = _each(lambda tt, pp: _split_dot(jnp.concatenate([tt, pp], axis=0), pp), t, p)
            t = _each(lambda tt, pr: tt + pr[:CH], t, prod)
            p = _each(lambda pr: pr[CH:], prod)
        else:
            t = _each(lambda tt, pp: tt + _split_dot(tt, pp), t, p)
    return t


def _chunk_gates(gc_ref, be_ref, gt_ref, g):
    rows = pl.ds(g * CH, CH)
    gc = gc_ref[0, rows, :]
    be = be_ref[0, rows, :]
    gcl = gc_ref[0, pl.ds(g * CH + CH - 1, 1), :]
    gt = gt_ref[0, g]
    incl, upper = _tri(CH, "incl"), _tri(CH, "upper")
    dec = jnp.where(incl, jnp.exp(jnp.where(incl, gc[:, :CH] - gt, 0.0)), 0.0)
    dec_t = jnp.where(upper, jnp.exp(jnp.where(upper, gt - gc[:, :CH], 0.0)), 0.0)
    return rows, be, dec, dec_t, jnp.exp(gc), jnp.exp(gcl - gc), jnp.exp(gcl)


def _chunk_group(nchunk):
    return 3 if nchunk % 3 == 0 else 1


def _local_group(nchunk):
    return 11 if nchunk % 11 == 0 else _chunk_group(nchunk)


def _gdn_local_specs(lp, grp):
    ngrp = lp // (grp * CH)
    head = lambda col0: pl.BlockSpec((grp * CH, DH), lambda i, j: ((i // HEADS) * ngrp + j, col0 // DH + i % HEADS))
    lane_b = pl.BlockSpec((1, grp * CH, DH), lambda i, j: (i, j, 0))
    per_chunk = lambda r, c: pl.BlockSpec((1, grp, r, c), lambda i, j: (i, j, 0, 0))
    return ngrp, head, lane_b, per_chunk


def _gdn_local_fwd(qkv, gcb, bb, gct, bsz, lp):
    nchunk = lp // CH
    grp = _local_group(nchunk)
    ngrp, head, lane_b, per_chunk = _gdn_local_specs(lp, grp)
    bh = bsz * HEADS

    def body(q_ref, k_ref, v_ref, gc_ref, be_ref, gt_ref,
             u_ref, l1_ref, l2_ref, l3_ref, wt_ref, kd_ref, ti_ref, gl_ref):
        incl, strict, upper = _tri(CH, "incl"), _tri(CH, "strict"), _tri(CH, "upper")
        eye = _eye(CH)
        rows, be, dec, dec_t, eg, ekd, gl = zip(*[_chunk_gates(gc_ref, be_ref, gt_ref, g) for g in range(grp)])
        q = [q_ref[r, :] for r in rows]
        k = [k_ref[r, :] for r in rows]
        v = [v_ref[r, :] for r in rows]
        kb = _each(lambda x, y: x * y, k, be)
        a = _each(lambda x, y, d: jnp.where(strict, _bdot(x, y, NT) * d, 0.0), kb, k, dec)
        tinv = _inv_unit_lower(a)
        u = _each(lambda t, x, y: _bdot(t, x * y, NN), tinv, v, be)
        w = _each(lambda t, x, y: _bdot(t, x * y, NN), tinv, kb, eg)
        qk = _each(lambda x, y, d: jnp.where(incl, _bdot(x, y, NT) * d, 0.0), q, k, dec)
        qk_t = _each(lambda x, y, d: jnp.where(upper, _bdot(y, x, NT) * d, 0.0), q, k, dec_t)
        qd = _each(lambda x, y: x * y, q, eg)
        kd = _each(lambda x, y: x * y, k, ekd)
        kd_t = _each(lambda x: _bdot(x, eye, TN), kd)
        qd_t = _each(lambda x: _bdot(x, eye, TN), qd)
        w_t = _each(lambda x: _bdot(x, eye, TN), w)
        for g in range(grp):
            u_ref[rows[g], :] = u[g]
            l1_ref[0, g] = jnp.concatenate([w[g], qd[g]], axis=0).astype(BF16)
            l2_ref[0, g] = jnp.concatenate([qk[g], kd_t[g]], axis=0).astype(BF16)
            l3_ref[0, g] = jnp.concatenate([qk_t[g], qd_t[g]], axis=0).astype(BF16)
            wt_ref[0, g] = w_t[g].astype(BF16)
            kd_ref[0, g] = kd[g].astype(BF16)
            ti_ref[0, g] = tinv[g]
            gl_ref[0, g] = gl[g]

    r = bsz * lp
    shapes = [((r, GW), F32), ((bh, nchunk, 2 * CH, DH), BF16), ((bh, nchunk, 3 * CH, CH), BF16),
              ((bh, nchunk, 3 * CH, CH), BF16), ((bh, nchunk, DH, CH), BF16), ((bh, nchunk, CH, DH), BF16),
              ((bh, nchunk, CH, CH), F32), ((bh, nchunk, 1, DH), F32)]
    return pl.pallas_call(
        body, name="gdn_local_fwd", grid=(bh, ngrp),
        in_specs=[head(0), head(GW), head(2 * GW), lane_b, lane_b, per_chunk(1, CH)],
        out_specs=[head(0), per_chunk(2 * CH, DH), per_chunk(3 * CH, CH), per_chunk(3 * CH, CH), per_chunk(DH, CH),
                   per_chunk(CH, DH), per_chunk(CH, CH), per_chunk(1, DH)],
        out_shape=[jax.ShapeDtypeStruct(s, dt) for s, dt in shapes],
        compiler_params=_cparams(("parallel", "parallel")),
    )(qkv, qkv, qkv, gcb, bb, gct)


def _scan_specs(bsz, lp, grp, order):
    ngrp = lp // (grp * CH)
    bh = bsz * HEADS
    seq = lambda: pl.BlockSpec((bsz, grp * CH, GW), lambda s: (0, order(s, ngrp), 0))
    per_chunk = lambda r, c: pl.BlockSpec((bh, grp, r, c), lambda s: (0, order(s, ngrp), 0, 0))
    return ngrp, bh, seq, per_chunk


def _gdn_scan_fwd(u, l1, l2, gl, bsz, lp):
    nchunk = lp // CH
    grp = _chunk_group(nchunk)
    ngrp, bh, seq, per_chunk = _scan_specs(bsz, lp, grp, lambda s, n: s)

    def body(u_ref, l1_ref, l2_ref, gl_ref, o_ref, vn_ref, st_ref, s_scr):
        @pl.when(pl.program_id(0) == 0)
        def _():
            s_scr[...] = jnp.zeros_like(s_scr)

        chains = list(range(bh))
        where = [(i // HEADS, slice((i % HEADS) * DH, (i % HEADS + 1) * DH)) for i in chains]

        def dot(a, b):
            return jnp.dot(a, b.astype(BF16), preferred_element_type=F32)

        for c in range(grp):
            rows = slice(c * CH, (c + 1) * CH)
            s = [s_scr[i] for i in chains]
            for i in chains:
                st_ref[i, c] = s[i]
            x = _each(lambda i, si: dot(l1_ref[i, c], si), chains, s)
            vnew = _each(lambda bc, xi: u_ref[bc[0], rows, bc[1]] - xi[:CH], where, x)
            y = _each(lambda i, vi: dot(l2_ref[i, c], vi), chains, vnew)
            for i, (b, cols) in zip(chains, where):
                o_ref[b, rows, cols] = x[i][CH:] + y[i][:CH]
                vn_ref[b, rows, cols] = vnew[i].astype(BF16)
                s_scr[i] = s[i] * gl_ref[i, c] + y[i][CH:]

    return pl.pallas_call(
        body, name="gdn_scan_fwd", grid=(ngrp,),
        in_specs=[seq(), per_chunk(2 * CH, DH), per_chunk(3 * CH, CH), per_chunk(1, DH)],
        out_specs=[seq(), seq(), per_chunk(DH, DH)],
        out_shape=[jax.ShapeDtypeStruct((bsz, lp, GW), F32), jax.ShapeDtypeStruct((bsz, lp, GW), BF16),
                   jax.ShapeDtypeStruct((bh, nchunk, DH, DH), F32)],
        scratch_shapes=[pltpu.VMEM((bh, DH, DH), F32)],
        compiler_params=_cparams(("arbitrary",)),
    )(u.reshape(bsz, lp, GW), l1, l2, gl)


def _gdn_scan_bwd(d_o, l3, wt, kd, gl, bsz, lp):
    nchunk = lp // CH
    grp = _chunk_group(nchunk)
    ngrp, bh, seq, per_chunk = _scan_specs(bsz, lp, grp, lambda s, n: n - 1 - s)

    def body(do_ref, l3_ref, wt_ref, kd_ref, gl_ref, dvn_ref, dst_ref, ds_scr):
        @pl.when(pl.program_id(0) == 0)
        def _():
            ds_scr[...] = jnp.zeros_like(ds_scr)

        chains = list(range(bh))
        where = [(i // HEADS, slice((i % HEADS) * DH, (i % HEADS + 1) * DH)) for i in chains]

        def dot(a, b):
            return jnp.dot(a, b.astype(BF16), preferred_element_type=F32)

        for c in reversed(range(grp)):
            rows = slice(c * CH, (c + 1) * CH)
            ds = [ds_scr[i] for i in chains]
            for i in chains:
                dst_ref[i, c] = ds[i]
            p = _each(lambda i, bc: dot(l3_ref[i, c], do_ref[bc[0], rows, bc[1]]), chains, where)
            dvn = _each(lambda i, pi, di: pi[:CH] + dot(kd_ref[i, c], di), chains, p, ds)
            wd = _each(lambda i, vi: dot(wt_ref[i, c], vi), chains, dvn)
            for i, (b, cols) in zip(chains, where):
                dvn_ref[b, rows, cols] = dvn[i].astype(BF16)
                ds_scr[i] = ds[i] * gl_ref[i, c] + p[i][CH:] - wd[i]

    return pl.pallas_call(
        body, name="gdn_scan_bwd", grid=(ngrp,),
        in_specs=[seq(), per_chunk(3 * CH, CH), per_chunk(DH, CH), per_chunk(CH, DH), per_chunk(1, DH)],
        out_specs=[seq(), per_chunk(DH, DH)],
        out_shape=[jax.ShapeDtypeStruct((bsz, lp, GW), BF16), jax.ShapeDtypeStruct((bh, nchunk, DH, DH), F32)],
        scratch_shapes=[pltpu.VMEM((bh, DH, DH), F32)],
        compiler_params=_cparams(("arbitrary",)),
    )(d_o.reshape(bsz, lp, GW), l3, wt, kd, gl)


def _gdn_local_bwd(qkv, gcb, bb, gct, tinv_all, states, dstates, vnew, dvnew, d_o, bsz, lp):
    nchunk = lp // CH
    grp = _local_group(nchunk)
    ngrp, head, lane_b, per_chunk = _gdn_local_specs(lp, grp)
    bh = bsz * HEADS

    def body(q_ref, k_ref, v_ref, gc_ref, be_ref, gt_ref, ti_ref, st_ref, dst_ref, vn_ref, dvn_ref, do_ref,
             dq_ref, dk_ref, dv_ref, dg_ref, db_ref):
        incl, strict = _tri(CH, "incl"), _tri(CH, "strict")
        upper = _tri(CH, "upper").astype(F32)
        ones = jnp.ones((CH, DH), F32)
        last = lax.broadcasted_iota(jnp.int32, (CH, 1), 0) == CH - 1

        def rsum(x):
            return jnp.sum(x, axis=-1, keepdims=True)

        def mul(xs, ys):
            return _each(lambda x, y: x * y, xs, ys)

        rows, be, dec, _, eg, ekd, gl = zip(*[_chunk_gates(gc_ref, be_ref, gt_ref, g) for g in range(grp)])
        q = [q_ref[r, :] for r in rows]
        k = [k_ref[r, :] for r in rows]
        v = [v_ref[r, :] for r in rows]
        vnew = [vn_ref[r, :] for r in rows]
        dvn = [dvn_ref[r, :] for r in rows]
        do = [do_ref[r, :] for r in rows]
        tinv = [ti_ref[0, g] for g in range(grp)]
        s = [st_ref[0, g] for g in range(grp)]
        ds = [dst_ref[0, g] for g in range(grp)]
        kb, vb, qd, kd = mul(k, be), mul(v, be), mul(q, eg), mul(k, ekd)
        kbg = mul(kb, eg)
        a = _each(lambda x, y, d: jnp.where(strict, _bdot(x, y, NT) * d, 0.0), kb, k, dec)
        qk = _each(lambda x, y, d: jnp.where(incl, _bdot(x, y, NT) * d, 0.0), q, k, dec)
        dqk = _each(lambda x, y: jnp.where(incl, _bdot(x, y, NT), 0.0), do, vnew)
        dqd = _each(lambda x, y: _bdot(x, y, NT), do, s)
        dkd = _each(lambda x, y: _bdot(x, y, NT), vnew, ds)
        dgl = _each(lambda x, y: jnp.sum(rsum(x * y), axis=0, keepdims=True), s, ds)
        dw = _each(lambda x, y: -_bdot(x, y, NT), dvn, s)
        dvb = _each(lambda t, x: _bdot(t, x, TN), tinv, dvn)
        dkbg = _each(lambda t, x: _bdot(t, x, TN), tinv, dw)
        dt = _each(lambda x, y, z, w: _bdot(x, y, NT) + _bdot(z, w, NT), dvn, vb, dw, kbg)
        tdt = _each(lambda t, x: _split_dot(t, x, TN), tinv, dt)
        da = _each(lambda x, t: jnp.where(strict, -_split_dot(x, t, NT), 0.0), tdt, tinv)
        dp, dr = mul(da, dec), mul(dqk, dec)
        dkb = _each(lambda x, y, z, e: _bdot(x, y, NN) + z * e, dp, k, dkbg, eg)
        dk = _each(lambda p_, kb_, r_, q_, dkd_, ekd_, dkb_, be_:
                   _bdot(p_, kb_, TN) + _bdot(r_, q_, TN) + dkd_ * ekd_ + dkb_ * be_,
                   dp, kb, dr, q, dkd, ekd, dkb, be)
        dq = _each(lambda r_, k_, dqd_, eg_: _bdot(r_, k_, NN) + dqd_ * eg_, dr, k, dqd, eg)
        m = _each(lambda da_, a_, dqk_, qk_: da_ * a_ + dqk_ * qk_, da, a, dqk, qk)
        m_cols = _each(lambda x: _split_dot(x, ones, TN), m)
        kd_term = _each(lambda x, y: rsum(x * y), dkd, kd)
        dgc = _each(lambda m_, mc, dqd_, qd_, kt, dkbg_, kbg_, dgl_, gl_:
                    rsum(m_) - mc + rsum(dqd_ * qd_) - kt + rsum(dkbg_ * kbg_)
                    + jnp.where(last, jnp.sum(kt, axis=0, keepdims=True) + dgl_ * gl_, 0.0),
                    m, m_cols, dqd, qd, kd_term, dkbg, kbg, dgl, gl)
        dg = _each(lambda x: _split_dot(upper, x, NN), dgc)
        for g in range(grp):
            dq_ref[rows[g], :] = dq[g]
            dk_ref[rows[g], :] = dk[g]
            dv_ref[rows[g], :] = dvb[g] * be[g]
            dg_ref[0, rows[g], :] = dg[g]
            db_ref[0, rows[g], :] = jnp.broadcast_to(rsum(dkb[g] * k[g]) + rsum(dvb[g] * v[g]), (CH, DH))

    r = bsz * lp
    return pl.pallas_call(
        body, name="gdn_local_bwd", grid=(bh, ngrp),
        in_specs=[head(0), head(GW), head(2 * GW), lane_b, lane_b, per_chunk(1, CH), per_chunk(CH, CH),
                  per_chunk(DH, DH), per_chunk(DH, DH), head(0), head(0), head(0)],
        out_specs=[head(0), head(0), head(0), lane_b, lane_b],
        out_shape=[jax.ShapeDtypeStruct((r, GW), F32)] * 3 + [jax.ShapeDtypeStruct((bh, lp, DH), F32)] * 2,
        compiler_params=_cparams(("parallel", "parallel")),
    )(qkv, qkv, qkv, gcb, bb, gct, tinv_all, states, dstates, vnew, dvnew, d_o)


def _place():
    return lax.axis_index("x"), lax.axis_index("y"), lax.axis_index("c")


def _any_specs(n):
    return [pl.BlockSpec(memory_space=pl.ANY)] * n


def _all_gather(name, shards):
    n = len(shards)

    def body(*refs):
        ins, outs = refs[:n], refs[n:2 * n]
        send_sems, recv_sems, local_sems = refs[2 * n:2 * n + 3]
        stage = refs[2 * n + 3:]
        x, y, c = _place()
        me, sibling = (x, y, c), (x, y, 1 - c)
        chips = [(1 - x, y), (x, 1 - y), (1 - x, 1 - y)]

        def slot(i, p):
            return outs[i].at[4 * p[0] + 2 * p[1] + p[2]]

        def copy(i, k, block, to, src=None):
            return pltpu.make_async_remote_copy(
                src_ref=slot(i, block) if src is None else src, dst_ref=slot(i, block),
                send_sem=send_sems.at[i, k], recv_sem=recv_sems.at[i, k], device_id=to, device_id_type=MESH)

        load = [pltpu.make_async_copy(ins[i], stage[i], local_sems.at[i]) for i in range(n)]
        for cp in load:
            cp.start()
        first = []
        for i in range(n):
            first.append(copy(i, 0, me, sibling, src=ins[i]))
            first += [copy(i, 1 + j, me, (*chip, c), src=ins[i]) for j, chip in enumerate(chips)]
        for cp in first:
            cp.start()
        for cp in load:
            cp.wait()
        mine = [pltpu.make_async_copy(stage[i], slot(i, me), local_sems.at[i]) for i in range(n)]
        for cp in mine:
            cp.start()
        passed = []
        for j, chip in enumerate(chips):
            for i in range(n):
                copy(i, 1 + j, (*chip, c), me).wait_recv()
                fwd = copy(i, 4 + j, (*chip, c), sibling)
                fwd.start()
                passed.append(fwd)
        for i in range(n):
            copy(i, 0, sibling, me).wait_recv()
            for j, chip in enumerate(chips):
                copy(i, 4 + j, (*chip, 1 - c), me).wait_recv()
        for cp in first + passed:
            cp.wait_send()
        for cp in mine:
            cp.wait()

    return pl.pallas_call(
        body, name=name,
        in_specs=_any_specs(n), out_specs=_any_specs(n),
        out_shape=[jax.ShapeDtypeStruct((N_DEV,) + s.shape, s.dtype) for s in shards],
        scratch_shapes=[pltpu.SemaphoreType.DMA((n, 7)), pltpu.SemaphoreType.DMA((n, 7)),
                        pltpu.SemaphoreType.DMA((n,))] + [pltpu.VMEM(s.shape, s.dtype) for s in shards],
    )(*shards)


HBM_SPEC = pl.BlockSpec(memory_space=pltpu.HBM)
SEM_SPEC = pl.BlockSpec(memory_space=pltpu.SEMAPHORE)
EFFECT = pltpu.SideEffectType.DATAFLOW_SIDE_EFFECTING
N_PEER = N_DEV - 1


def _peer(r):
    x, y, c = _place()
    return ((1 - x) if r & 4 else x, (1 - y) if r & 2 else y, (1 - c) if r & 1 else c)


def _slot_of(p):
    return 4 * p[0] + 2 * p[1] + p[2]


def _hbm(a):
    return pltpu.with_memory_space_constraint(a, pltpu.HBM)


def _gather_pattern(src, land):
    me = _place()
    return [(src, land.at[_slot_of(me)], _peer(r), land.at[_slot_of(_peer(r))]) for r in range(1, N_DEV)]


def _chipcast_pattern(src, land):
    x, y, c = _place()
    me = (x, y, c)
    peers = [(x, y, 1 - c), (1 - x, y, c), (x, 1 - y, c), (1 - x, 1 - y, c)]
    return [(src, land.at[_slot_of(me)], p, land.at[_slot_of(p)]) for p in peers]


def _forward_pattern(land, _):
    x, y, c = _place()
    chips = [(1 - x, y), (x, 1 - y), (1 - x, 1 - y)]
    return [(land.at[_slot_of((*ch, c))], land.at[_slot_of((*ch, c))], (x, y, 1 - c), land.at[_slot_of((*ch, 1 - c))])
            for ch in chips]


def _sibling_pattern(src, land):
    x, y, c = _place()
    return [(src.at[ch, 1 - c], land.at[ch], (x, y, 1 - c), land.at[ch]) for ch in range(4)]


def _chip_pattern(src, land):
    x, y, c = _place()
    chips = [(1 - x, y), (x, 1 - y), (1 - x, 1 - y)]
    return [(src.at[2 * ch[0] + ch[1]], land.at[j], (*ch, c), land.at[j]) for j, ch in enumerate(chips)]


def _split_start(name, groups, pattern, land_shape, after=None):
    sizes = [len(g) for g in groups]
    arrays = [s for g in groups for s in g]
    n = len(arrays)
    inplace = land_shape is None
    fresh = [] if inplace else [lax.empty(land_shape(s.shape), s.dtype) for s in arrays]
    bufs = arrays + fresh
    order = [] if after is None else [after]

    def body(*refs):
        srcs, lands = refs[:n], (refs[:n] if inplace else refs[n:2 * n])
        first_out = len(bufs) + len(order)
        sems = refs[first_out:first_out + 2 * len(groups)]
        token = refs[-1]
        k = 0
        for gi, size in enumerate(sizes):
            for i in range(size):
                copies = pattern(srcs[k], lands[k])
                for j, (s, d, peer, _) in enumerate(copies):
                    pltpu.make_async_remote_copy(
                        src_ref=s, dst_ref=d, send_sem=sems[2 * gi].at[i * len(copies) + j],
                        recv_sem=sems[2 * gi + 1].at[i * len(copies) + j], device_id=peer, device_id_type=MESH).start()
                k += 1
        token[...] = jnp.zeros_like(token)

    per_array = {_gather_pattern: N_PEER, _chipcast_pattern: 4, _forward_pattern: 3, _sibling_pattern: 4,
                 _chip_pattern: 3}[pattern]
    sem_shapes = [pltpu.SemaphoreType.DMA((size * per_array,)) for size in sizes for _ in range(2)]
    res = pl.pallas_call(
        body, name=name,
        in_specs=[HBM_SPEC] * len(bufs) + [pl.BlockSpec(memory_space=pl.ANY)] * len(order),
        out_specs=[SEM_SPEC] * len(sem_shapes) + [HBM_SPEC] * len(bufs) + [pl.BlockSpec(memory_space=pltpu.VMEM)],
        out_shape=sem_shapes + [pltpu.HBM(s.shape, s.dtype) for s in bufs] + [jax.ShapeDtypeStruct((8, DH), F32)],
        input_output_aliases={i: len(sem_shapes) + i for i in range(len(bufs))},
        compiler_params=pltpu.CompilerParams(has_side_effects=EFFECT),
    )(*[_hbm(s) for s in bufs], *order)
    nsem = len(sem_shapes)
    thru = list(res[nsem:nsem + len(bufs)])
    out, k = [], 0
    for gi, size in enumerate(sizes):
        srcs_out = [] if inplace else thru[k:k + size]
        lands_out = thru[k:k + size] if inplace else thru[n + k:n + k + size]
        out.append((res[2 * gi], res[2 * gi + 1], srcs_out, lands_out))
        k += size
    return out, res[-1]


def _split_wait(name, group, pattern, after):
    send_sems, recv_sems, arrays, lands = group
    n = len(lands)
    inplace = not arrays
    bufs = list(arrays) + list(lands)
    fill_own = pattern in (_gather_pattern, _chipcast_pattern)

    def body(*refs):
        srcs, lands_ = refs[:n], (refs[:n] if inplace else refs[n:2 * n])
        s_sems, r_sems = refs[len(bufs)], refs[len(bufs) + 1]
        own = []
        if fill_own:
            stage, local_sems = refs[-n - 1:-1], refs[-1]
            load = [pltpu.make_async_copy(srcs[i], stage[i], local_sems.at[i]) for i in range(n)]
            for cp in load:
                cp.start()
            for cp in load:
                cp.wait()
            own = [pltpu.make_async_copy(stage[i], lands_[i].at[_slot_of(_place())], local_sems.at[i])
                   for i in range(n)]
            for cp in own:
                cp.start()
        for i in range(n):
            copies = pattern(srcs[i], lands_[i])
            for j, (s, _, peer, filled) in enumerate(copies):
                cp = pltpu.make_async_remote_copy(
                    src_ref=s, dst_ref=filled, send_sem=s_sems.at[i * len(copies) + j],
                    recv_sem=r_sems.at[i * len(copies) + j], device_id=peer, device_id_type=MESH)
                cp.wait_send()
                cp.wait_recv()
        for cp in own:
            cp.wait()

    res = pl.pallas_call(
        body, name=name,
        in_specs=[HBM_SPEC] * len(bufs) + [SEM_SPEC, SEM_SPEC, pl.BlockSpec(memory_space=pl.ANY)],
        out_specs=[HBM_SPEC] * len(bufs),
        out_shape=[pltpu.HBM(s.shape, s.dtype) for s in bufs],
        scratch_shapes=([pltpu.VMEM(a.shape, a.dtype) for a in arrays] + [pltpu.SemaphoreType.DMA((n,))])
        if fill_own else [],
        input_output_aliases={i: i for i in range(len(bufs))},
        compiler_params=pltpu.CompilerParams(has_side_effects=EFFECT),
    )(*bufs, send_sems, recv_sems, after)
    return ([], list(res)) if inplace else (list(res[:n]), list(res[n:]))


def _pair_sum(name, grads, recv, core):
    _, _, r, c = grads.shape
    tr = r if r * c <= (1 << 20) or r % 16 else _div_tile(r, 512, 16)

    def body(core_ref, g_ref, r_ref, o_ref):
        o_ref[...] = (g_ref[0].astype(F32) + r_ref[...].astype(F32)).astype(o_ref.dtype)

    return pl.pallas_call(
        body, name=name,
        grid_spec=pltpu.PrefetchScalarGridSpec(
            num_scalar_prefetch=1, grid=(4, r // tr),
            in_specs=[pl.BlockSpec((1, 1, tr, c), lambda s, i, cr: (s, cr[0], i, 0)),
                      pl.BlockSpec((1, tr, c), lambda s, i, cr: (s, i, 0))],
            out_specs=pl.BlockSpec((1, tr, c), lambda s, i, cr: (s, i, 0))),
        out_shape=jax.ShapeDtypeStruct(recv.shape, recv.dtype),
        compiler_params=_cparams(("parallel", "parallel")),
    )(core, grads, recv)


def _adamw_math(w, g, m, v):
    m = ADAM_B1 * m + (1.0 - ADAM_B1) * g
    v = ADAM_B2 * v + (1.0 - ADAM_B2) * (g * g)
    m_hat = m / (1.0 - ADAM_B1 ** ADAM_STEP)
    v_hat = v / (1.0 - ADAM_B2 ** ADAM_STEP)
    delta = -ADAM_LR * (m_hat / (jnp.sqrt(v_hat) + ADAM_EPS) + ADAM_WD * w)
    return delta, m, v


SMALL_LEAVES = (
    ("mix_pre_norm", 0, 1, 0, D, (1, D)), ("mix_post_norm", 1, 1, 0, D, (1, D)),
    ("ffn_pre_norm", 2, 1, 0, D, (1, D)), ("ffn_post_norm", 3, 1, 0, D, (1, D)),
    ("meta_tokens", 4, N_META, 0, DH, (N_META, DH)),
    ("conv_qkv", 20, KQ, 0, 3 * GW // N_DEV, (1, KQ, 3 * GW // N_DEV)),
    ("conv_sc", 24, KS, 0, SW // N_DEV, (1, KS, SW // N_DEV)),
    ("a_log", 27, 1, 0, HEADS, (1, HEADS)), ("dt_bias", 27, 1, DH, HEADS, (1, HEADS)),
    ("gdn_norm", 27, 1, 2 * DH, DH, (1, DH)),
)


def _adamw_small(slabs, w, m, v):
    nslot = slabs.shape[0]
    nleaf = len(SMALL_LEAVES)

    def body(p_ref, w_ref, m_ref, v_ref, *refs):
        outs, loss_ref, scr = refs[:4 * nleaf], refs[4 * nleaf], refs[4 * nleaf + 1]
        g = p_ref[0]
        for s in range(1, nslot):
            g = g + p_ref[s]
        d, mn, vn = _adamw_math(w_ref[...], g, m_ref[...], v_ref[...])
        for kind, val in enumerate((g, d, mn, vn)):
            scr[kind] = val
        for kind in range(4):
            for li, (_, r0, nr, c0, nc, shape) in enumerate(SMALL_LEAVES):
                val = scr[kind, pl.ds(r0, nr), pl.ds(c0, nc)]
                o = outs[kind * nleaf + li]
                if len(shape) == 3:
                    o[0] = val
                else:
                    o[...] = val
        loss_ref[...] = scr[0, pl.ds(27, 1), pl.ds(3 * DH, 1)]

    whole = lambda a: pl.BlockSpec(a.shape, lambda: (0,) * a.ndim)
    out_shapes = [jax.ShapeDtypeStruct(shape, F32) for _ in range(4) for (*_, shape) in SMALL_LEAVES]
    out_shapes.append(jax.ShapeDtypeStruct((1, 1), F32))
    res = pl.pallas_call(
        body, name="adamw_small",
        in_specs=[whole(slabs), whole(w), whole(m), whole(v)],
        out_specs=[pl.BlockSpec(s.shape, lambda n=len(s.shape): (0,) * n) for s in out_shapes],
        out_shape=out_shapes,
        scratch_shapes=[pltpu.VMEM((4,) + w.shape, F32)],
        compiler_params=pltpu.CompilerParams(vmem_limit_bytes=VMEM_LIMIT),
    )(slabs, w, m, v)
    kinds = [{nm: res[kind * nleaf + li] for li, (nm, *_) in enumerate(SMALL_LEAVES)} for kind in range(4)]
    return kinds, res[-1]


def _adamw_reduced(name, parts, chip, recv, w, m, v, after=None):
    _, r, c = parts.shape
    tr = r if r * c <= (1 << 20) or r % 16 else _div_tile(r, 512, 16)
    order = [] if after is None else [after]

    def body(chip_ref, p_ref, r_ref, w_ref, m_ref, v_ref, *rest):
        g_out, d_out, m_out, v_out = rest[len(order):]
        g = p_ref[0].astype(F32)
        for s in range(3):
            g = g + r_ref[s].astype(F32)
        d, mn, vn = _adamw_math(w_ref[...], g, m_ref[...], v_ref[...])
        g_out[...] = g
        d_out[...] = d
        m_out[...] = mn
        v_out[...] = vn

    blk = pl.BlockSpec((tr, c), lambda i, ch: (i, 0))
    return pl.pallas_call(
        body, name=name,
        grid_spec=pltpu.PrefetchScalarGridSpec(
            num_scalar_prefetch=1, grid=(r // tr,),
            in_specs=[pl.BlockSpec((1, tr, c), lambda i, ch: (ch[0], i, 0)),
                      pl.BlockSpec((3, tr, c), lambda i, ch: (0, i, 0)), blk, blk, blk]
            + [pl.BlockSpec((8, DH), lambda i, ch: (0, 0))] * len(order),
            out_specs=[blk] * 4),
        out_shape=[jax.ShapeDtypeStruct((r, c), F32)] * 4,
        compiler_params=_cparams(("parallel",)),
    )(chip, parts, recv, w, m, v, *order)


SMALL_ROWS, SMALL_COLS = 32, 3 * GW


def _pack_small(gains4, meta, conv_qkv, conv_sc, a_log, dt_bias, gdn_norm, extra=None):
    def padc(a):
        return jnp.pad(a, ((0, 0), (0, SMALL_COLS - a.shape[1])))
    scal = jnp.concatenate([jnp.pad(a_log, ((0, 0), (0, DH - HEADS))), jnp.pad(dt_bias, ((0, 0), (0, DH - HEADS))),
                            gdn_norm] + ([] if extra is None else [extra]), axis=1)
    rows = [padc(g) for g in gains4] + [padc(meta), padc(conv_qkv), padc(conv_sc), padc(scal)]
    slab = jnp.concatenate(rows, axis=0)
    return jnp.pad(slab, ((0, SMALL_ROWS - slab.shape[0]), (0, 0)))


def _unpack_small(slab):
    gains = [slab[i:i + 1, :D] for i in range(4)]
    meta = slab[4:20, :D]
    conv_qkv = slab[20:24, :]
    conv_sc = slab[24:27, :SW]
    a_log = slab[27:28, 0:HEADS]
    dt_bias = slab[27:28, DH:DH + HEADS]
    gdn_norm = slab[27:28, 2 * DH:3 * DH]
    return gains, meta, conv_qkv, conv_sc, a_log, dt_bias, gdn_norm


def kernel(x, meta_tokens, mix_pre_norm, mix_post_norm, ffn_pre_norm, ffn_post_norm, w_in, conv_qkv, a_log, dt_bias, gdn_norm, conv_sc, w_out, w_gate, w_up, w_down, loss_target, m_meta_tokens, m_mix_pre_norm, m_mix_post_norm, m_ffn_pre_norm, m_ffn_post_norm, m_w_in, m_conv_qkv, m_a_log, m_dt_bias, m_gdn_norm, m_conv_sc, m_w_out, m_w_gate, m_w_up, m_w_down, v_meta_tokens, v_mix_pre_norm, v_mix_post_norm, v_ffn_pre_norm, v_ffn_post_norm, v_w_in, v_conv_qkv, v_a_log, v_dt_bias, v_gdn_norm, v_conv_sc, v_w_out, v_w_gate, v_w_up, v_w_down):
    bsz, seq, _ = x.shape
    lp = -(-(N_META + seq) // CH) * CH
    padf = lp - N_META - seq
    first_real = padf + N_META
    rows = bsz * lp
    nchunk = lp // CH
    tr = _div_tile(lp, 528)
    tm = _div_tile(rows, 1056)
    tm_big = _div_tile(rows, 2112)
    cx, cy, cc = _place()
    dev = 4 * cx + 2 * cy + cc

    small_shard = jnp.concatenate([
        jnp.pad(meta_tokens, ((0, 0), (0, 256 - DH))),
        jnp.pad(conv_qkv[0], ((0, 0), (0, 256 - 3 * GW // N_DEV))),
        jnp.pad(conv_sc[0], ((0, 5), (0, 256 - SW // N_DEV)))], axis=0)
    g_in, g_small = _all_gather("gather_w_in", [w_in[0].T.astype(BF16), small_shard])
    my_rest = [w_out[0].astype(BF16), w_gate[0].T.astype(BF16), w_up[0].T.astype(BF16), w_down[0].astype(BF16)]
    (grp_rest,), token = _split_start("gather_rest_start", [my_rest], _chipcast_pattern, lambda s: (N_DEV,) + s,
                                      after=g_in)
    after_start = token[0:1, 0:1]

    def cols_full(g):
        return jnp.transpose(g, (1, 0, 2)).reshape(g.shape[1], -1)

    win_t = g_in.reshape(IN_W, D)
    win_t = jnp.concatenate([win_t[:BA0 + 2 * HEADS], jnp.zeros((BA_W - 2 * HEADS, D), BF16),
                             win_t[BA0 + 2 * HEADS:]], axis=0)
    meta_full = cols_full(g_small[:, 0:16, :DH])
    convq_full = cols_full(g_small[:, 16:20, :3 * GW // N_DEV])
    convs_full = cols_full(g_small[:, 20:23, :SW // N_DEV])
    alog_v = jnp.pad(a_log, ((0, 0), (HEADS, DH - 2 * HEADS)))
    dtb_v = jnp.pad(dt_bias, ((0, 0), (HEADS, DH - 2 * HEADS)))

    head_rows = jnp.concatenate([jnp.zeros((padf, D), F32), meta_full], axis=0)
    h0, u1 = _pre_norm(x, head_rows, mix_pre_norm + after_start, bsz, lp, tr)
    (proj,) = _mm("proj_in", [(u1, win_t)], "nt", tm_big, 768, D, out_dtypes=(BF16,))
    (ba,) = _mm("proj_gate_logits", [(u1, win_t[BA0:SX0])], "nt", tm, BA_W, D)
    qkv = _gdn_prep(proj, convq_full, bsz, lp, padf)
    gb, gcs = _gates(ba, alog_v, dtb_v, bsz, lp, padf)

    def lane_bcast(cols):
        t = jnp.transpose(cols.reshape(bsz, lp, HEADS), (0, 2, 1)).reshape(bsz * HEADS, lp, 1)
        return jnp.broadcast_to(t, (bsz * HEADS, lp, DH))

    bb = lane_bcast(gb[:, 0:HEADS])
    gcb = lane_bcast(gcs[:, HEADS:2 * HEADS])
    gct = jnp.transpose(gcs[:, HEADS:2 * HEADS].reshape(bsz, nchunk, CH, HEADS), (0, 3, 1, 2)).reshape(
        bsz * HEADS, nchunk, 1, CH)
    u_loc, l1, l2, l3, w_t, k_dec, tinv_all, g_last = _gdn_local_fwd(qkv, gcb, bb, gct, bsz, lp)
    o, v_new, states = _gdn_scan_fwd(u_loc, l1, l2, g_last, bsz, lp)
    o, v_new = o.reshape(rows, GW), v_new.reshape(rows, GW)
    _, l_rest = _split_wait("gather_rest_arrive", grp_rest, _chipcast_pattern, o)
    (grp_out, grp_ffn, grp_down), token = _split_start(
        "gather_rest_forward", [l_rest[0:1], l_rest[1:3], l_rest[3:4]], _forward_pattern, None)
    mix_g, mix_s = _mix_heads(proj, o, gdn_norm + token[0:1, :], convs_full, bsz, lp, padf)
    mixin = jnp.concatenate([mix_g, mix_s], axis=1)
    _, (l_out,) = _split_wait("gather_w_out_wait", grp_out, _forward_pattern, mixin)
    wout = l_out.reshape(D, D)
    (mix,) = _mm("proj_out", [(mixin, wout)], "nn", tm, 512, D)
    h1, u2 = _mid_norms(h0, mix, mix_post_norm, ffn_pre_norm, bsz, lp, tr)
    _, (l_gate, l_up) = _split_wait("gather_ffn_wait", grp_ffn, _forward_pattern, u2)
    wgate_t, wup_t = l_gate.reshape(FF, D), l_up.reshape(FF, D)
    gate, up, act = _ffn_up(u2, wgate_t, wup_t, _div_tile(rows, 528), 1408)
    _, (l_down,) = _split_wait("gather_down_wait", grp_down, _forward_pattern, act)
    wdown = l_down.reshape(FF, D)
    (ffn,) = _mm("ffn_down", [(act, wdown)], "nn", tm, 512, 1408)

    dy, dffn, d_g4, loss_part = _loss_head(h1, ffn, loss_target, ffn_post_norm, bsz, lp, tr, first_real)
    (dwdown,) = _mm("dw_down", [(act, dffn)], "tn", 1408, 512, tm, out_dtypes=(BF16,))

    def swiglu_bwd(da, g, u):
        g, u = g.astype(F32), u.astype(F32)
        return da * u * _dsilu(g), da * _silu(g)

    dgate, dup = _mm("d_act", [(dffn, wdown)], "nt", tm, 1408, D, out_dtypes=(BF16, BF16),
                     epilogue=swiglu_bwd, extras=(gate, up))
    (dwgate_t,) = _mm("dw_gate", [(dgate, u2)], "tn", 1408, D, tm, out_dtypes=(BF16,))
    (dwup_t,) = _mm("dw_up", [(dup, u2)], "tn", 1408, D, tm, out_dtypes=(BF16,))
    (du2,) = _mm("d_u2", [(dgate, wgate_t), (dup, wup_t)], "nn", tm, D, 1408)
    dh1, dmix, d_g3, d_g2 = _mid_norms_bwd(h1, mix, du2, dy, ffn_pre_norm, mix_post_norm, bsz, lp, tr)
    (dwout,) = _mm("dw_out", [(mixin, dmix)], "tn", D, 512, tm, out_dtypes=(BF16,))

    def row_blocks(g):
        return g.reshape(4, 2, -1, g.shape[1])

    core = jnp.reshape(cc, (1,)).astype(jnp.int32)
    chip = jnp.reshape(2 * cx + cy, (1,)).astype(jnp.int32)

    def reduce_start(tag, grads):
        (grp,), tok = _split_start("rs_sibling_start_" + tag, [grads], _sibling_pattern, lambda s: s[:1] + s[2:])
        return grp, tok

    def reduce_mid(tag, grp, nms, after):
        mine, lands = _split_wait("rs_sibling_wait_" + tag, grp, _sibling_pattern, after)
        parts = [_pair_sum("pair_sum_" + nm, g, r, core) for nm, g, r in zip(nms, mine, lands)]
        (grp2,), tok = _split_start("rs_chip_start_" + tag, [parts], _chip_pattern, lambda s: (3,) + s[1:])
        return grp2, tok

    def reduce_end(tag, grp2, after):
        return _split_wait("rs_chip_wait_" + tag, grp2, _chip_pattern, after)

    names_ffn = ["w_down", "w_gate", "w_up", "w_out"]
    rs_ffn, tok = reduce_start("ffn", [row_blocks(dwdown), row_blocks(dwgate_t), row_blocks(dwup_t), row_blocks(dwout)])
    (dmixin,) = _mm("d_mixin", [(dmix, wout)], "nt", tm, 512, D, after=tok)
    d_o, dz, dsx, dsb, dsc, d_gn, d_convs = _mix_heads_bwd(proj, o, gdn_norm, convs_full, dmixin, bsz, lp, padf)
    rs_ffn, tok = reduce_mid("ffn", rs_ffn, names_ffn, d_o)
    dv_new, dstates = _gdn_scan_bwd(d_o, l3, w_t, k_dec, g_last + tok[0, 0], bsz, lp)
    dq, dk, dv, dg_b, dbeta_b = _gdn_local_bwd(qkv, gcb, bb, gct, tinv_all, states, dstates, v_new,
                                               dv_new.reshape(rows, GW), d_o, bsz, lp)
    dpqkv, d_convq = _gdn_prep_bwd(proj, convq_full, dq, dk, dv, bsz, lp, padf)
    dba, d_scal = _gates_bwd(ba, alog_v, dtb_v, dbeta_b, dg_b, gb, bsz, lp, padf)
    dproj = jnp.concatenate([dpqkv, dz, dba, dsx, dsb, dsc], axis=1)
    (dwin_t,) = _mm("dw_in", [(dproj, u1)], "tn", 1280, D, tm, out_dtypes=(BF16,))
    parts_ffn, recv_ffn = reduce_end("ffn", rs_ffn, dwin_t)
    dwin_t = jnp.concatenate([dwin_t[:BA0 + 2 * HEADS], dwin_t[SX0:]], axis=0)
    rs_in, tok = reduce_start("in", [row_blocks(dwin_t)])

    big_w = {"w_in": (w_in, m_w_in, v_w_in), "w_out": (w_out, m_w_out, v_w_out), "w_gate": (w_gate, m_w_gate, v_w_gate),
             "w_up": (w_up, m_w_up, v_w_up), "w_down": (w_down, m_w_down, v_w_down)}

    def update(nm, parts, recv, after=None):
        w, m, v = big_w[nm]
        if nm in ("w_out", "w_down"):
            return _adamw_reduced("adamw_" + nm, parts, chip, recv, w[0], m[0], v[0], after)
        res = _adamw_reduced("adamw_" + nm, parts, chip, recv, w[0].T, m[0].T, v[0].T, after)
        return [t.T for t in res]

    big_out = {nm: update(nm, p, r, tok) for nm, p, r in zip(names_ffn, parts_ffn, recv_ffn)}
    ffn_done = sum(big_out[nm][1][:1, :1] for nm in names_ffn)
    rs_in, tok = reduce_mid("in", rs_in, ["w_in"], ffn_done)
    (du1,) = _mm("d_u1", [(dproj, win_t)], "nn", tm_big, D, 768, after=tok)
    grad_x, d_g1, d_meta = _pre_norm_bwd(h0, du1, dh1, mix_pre_norm, bsz, lp, seq, padf)

    d_alog = d_scal[0:1, HEADS:2 * HEADS]
    d_dtb = d_scal[1:2, HEADS:2 * HEADS]
    slab = _pack_small([d_g1, d_g2, d_g3, d_g4], d_meta, d_convq, d_convs, d_alog, d_dtb, d_gn, loss_part)
    (grp_small,), _ = _split_start("small_grads_start", [[slab]], _gather_pattern, lambda s: (N_DEV,) + s)
    _, (slabs,) = _split_wait("small_grads_wait", grp_small, _gather_pattern, grad_x)
    off_meta = dev * DH
    off_cq = dev * (3 * GW // N_DEV)
    off_cs = dev * (SW // N_DEV)

    def shard_of(s):
        gains, meta, cq, cs, al, db, gn = _unpack_small(s)
        return gains, lax.dynamic_slice_in_dim(meta, off_meta, DH, 1), \
            lax.dynamic_slice_in_dim(cq, off_cq, 3 * GW // N_DEV, 1), \
            lax.dynamic_slice_in_dim(cs, off_cs, SW // N_DEV, 1), al, db, gn, s[27:28, 3 * DH:4 * DH]

    slabs_local = jax.vmap(lambda s: _pack_small(*shard_of(s)))(slabs)
    w_small = _pack_small([mix_pre_norm, mix_post_norm, ffn_pre_norm, ffn_post_norm], meta_tokens, conv_qkv[0],
                          conv_sc[0], a_log, dt_bias, gdn_norm)
    m_small = _pack_small([m_mix_pre_norm, m_mix_post_norm, m_ffn_pre_norm, m_ffn_post_norm], m_meta_tokens,
                          m_conv_qkv[0], m_conv_sc[0], m_a_log, m_dt_bias, m_gdn_norm)
    v_small = _pack_small([v_mix_pre_norm, v_mix_post_norm, v_ffn_pre_norm, v_ffn_post_norm], v_meta_tokens,
                          v_conv_qkv[0], v_conv_sc[0], v_a_log, v_dt_bias, v_gdn_norm)
    small_out, loss = _adamw_small(slabs_local, w_small, m_small, v_small)
    parts_in, recv_in = reduce_end("in", rs_in, small_out[1]["gdn_norm"])
    big_out["w_in"] = update("w_in", parts_in[0], recv_in[0])
    names = ["w_in", "w_out", "w_gate", "w_up", "w_down"]

    order = ["meta_tokens", "mix_pre_norm", "mix_post_norm", "ffn_pre_norm", "ffn_post_norm", "w_in", "conv_qkv",
             "a_log", "dt_bias", "gdn_norm", "conv_sc", "w_out", "w_gate", "w_up", "w_down"]
    outs = [loss[0, 0], grad_x]
    for kind in range(4):
        leaves = dict(small_out[kind])
        for nm in names:
            leaves[nm] = big_out[nm][kind][None]
        outs += [leaves[nm] for nm in order]
    return tuple(outs)
```

```python
import functools

import jax
import jax.numpy as jnp
from jax import lax
from jax.experimental import pallas as pl
from jax.experimental.pallas import tpu as pltpu

F32 = jnp.float32
BF16 = jnp.bfloat16
HIGHEST = lax.Precision.HIGHEST
MESH = pl.DeviceIdType.MESH

D = 1024
N_META = 16
HEADS = 4
DH = 128
GW = HEADS * DH
SW = D - GW
FF = 2816
CH = 64
KQ = 4
KS = 3
EPS = 1e-6
IN_W = 3 * GW + GW + 2 * HEADS + 3 * SW
Q0, K0, V0, Z0, BA0, SX0, SB0, SC0, PW = 0, 512, 1024, 1536, 2048, 2304, 2816, 3328, 3840
BA_W = SX0 - BA0
N_DEV = 8
VMEM_LIMIT = 56 * 1024 * 1024

ADAM_LR, ADAM_B1, ADAM_B2, ADAM_EPS, ADAM_WD, ADAM_STEP = 0.001, 0.9, 0.999, 1e-08, 0.01, 10


def _cparams(sem):
    return pltpu.CompilerParams(dimension_semantics=sem, vmem_limit_bytes=VMEM_LIMIT)


def _div_tile(n, target, mult=16):
    best = None
    for t in range(mult, min(n, target) + 1, mult):
        if n % t == 0:
            best = t
    assert best is not None, (n, target)
    return best


def _silu(x):
    return x * jax.nn.sigmoid(x)


def _dsilu(x):
    s = jax.nn.sigmoid(x)
    return s * (1.0 + x * (1.0 - s))


def _bdot(a, b, dims):
    return lax.dot_general(a.astype(BF16), b.astype(BF16), (dims, ((), ())), preferred_element_type=F32)


NN = ((1,), (0,))
NT = ((1,), (1,))
TN = ((0,), (0,))


def _fdot(a, b, dims=NN):
    return lax.dot_general(a, b, (dims, ((), ())), precision=HIGHEST, preferred_element_type=F32)


def _mm(name, pairs, mode, tm, tn, tk, out_dtypes=(F32,), epilogue=None, extras=(), after=None):
    a0, b0 = pairs[0]
    if mode == "nn":
        (m, k), n = a0.shape, b0.shape[1]
    elif mode == "nt":
        (m, k), n = a0.shape, b0.shape[0]
    else:
        (k, m), n = a0.shape, b0.shape[1]
    assert m % tm == 0 and n % tn == 0 and k % tk == 0, (name, m, n, k, tm, tn, tk)
    nk = k // tk
    npair = len(pairs)
    nex = len(extras)
    nout = len(out_dtypes)
    ntok = 0 if after is None else 1
    dims = {"nn": NN, "nt": NT, "tn": TN}[mode]

    def body(*refs):
        ab = refs[:2 * npair]
        ex = refs[2 * npair:2 * npair + nex]
        outs = refs[2 * npair + nex + ntok:2 * npair + nex + ntok + nout]
        acc_ref = refs[-1]
        kk = pl.program_id(2)

        part = None
        for p in range(npair):
            d = lax.dot_general(ab[2 * p][...], ab[2 * p + 1][...], (dims, ((), ())), preferred_element_type=F32)
            part = d if part is None else part + d

        def finish(acc):
            res = epilogue(acc, *[e[...] for e in ex]) if epilogue is not None else (acc,)
            for o, r in zip(outs, res):
                o[...] = r.astype(o.dtype)

        if nk == 1:
            finish(part)
        else:
            @pl.when(kk == 0)
            def _():
                acc_ref[...] = part

            @pl.when(kk > 0)
            def _():
                acc_ref[...] += part

            @pl.when(kk == nk - 1)
            def _():
                finish(acc_ref[...])

    if mode == "nn":
        a_spec = pl.BlockSpec((tm, tk), lambda i, j, q: (i, q))
        b_spec = pl.BlockSpec((tk, tn), lambda i, j, q: (q, j))
    elif mode == "nt":
        a_spec = pl.BlockSpec((tm, tk), lambda i, j, q: (i, q))
        b_spec = pl.BlockSpec((tn, tk), lambda i, j, q: (j, q))
    else:
        a_spec = pl.BlockSpec((tk, tm), lambda i, j, q: (q, i))
        b_spec = pl.BlockSpec((tk, tn), lambda i, j, q: (q, j))
    o_spec = pl.BlockSpec((tm, tn), lambda i, j, q: (i, j))
    flat = [t for pr in pairs for t in pr]
    res = pl.pallas_call(
        body, name=name,
        grid=(m // tm, n // tn, nk),
        in_specs=[a_spec, b_spec] * npair + [o_spec] * nex + [pl.BlockSpec((8, DH), lambda i, j, q: (0, 0))] * ntok,
        out_specs=[o_spec] * nout,
        out_shape=[jax.ShapeDtypeStruct((m, n), dt) for dt in out_dtypes],
        scratch_shapes=[pltpu.VMEM((tm, tn) if nk > 1 else (8, DH), F32)],
        compiler_params=_cparams(("parallel", "parallel", "arbitrary")),
    )(*flat, *extras, *([] if after is None else [after]))
    return res


def _ffn_up(u2, wg, wu, tm, tn):
    m, k = u2.shape
    n = wg.shape[0]

    def body(a_ref, g_ref, u_ref, gate_ref, up_ref, act_ref):
        a = a_ref[...]
        g = lax.dot_general(a, g_ref[...], (NT, ((), ())), preferred_element_type=F32)
        u = lax.dot_general(a, u_ref[...], (NT, ((), ())), preferred_element_type=F32)
        gate_ref[...] = g.astype(BF16)
        up_ref[...] = u.astype(BF16)
        act_ref[...] = (_silu(g) * u).astype(BF16)

    o_spec = pl.BlockSpec((tm, tn), lambda j, i: (i, j))
    w_spec = pl.BlockSpec((tn, k), lambda j, i: (j, 0))
    return pl.pallas_call(
        body, name="ffn_up", grid=(n // tn, m // tm),
        in_specs=[pl.BlockSpec((tm, k), lambda j, i: (i, 0)), w_spec, w_spec],
        out_specs=[o_spec, o_spec, o_spec],
        out_shape=[jax.ShapeDtypeStruct((m, n), BF16)] * 3,
        compiler_params=_cparams(("parallel", "parallel")),
    )(u2, wg, wu)


def _rms(x, gain):
    r = lax.rsqrt(jnp.mean(x * x, axis=-1, keepdims=True) + EPS)
    return x * r * gain, r


def _rms_bwd(x, gain, dy):
    r = lax.rsqrt(jnp.mean(x * x, axis=-1, keepdims=True) + EPS)
    dyw = dy * gain
    dx = r * dyw - x * (r * r * r) * jnp.mean(dyw * x, axis=-1, keepdims=True)
    return dx, dy * x * r


def _row_specs(tr, width, nb):
    return pl.BlockSpec((tr, width), lambda b, j: (b * nb + j, 0))


def _vec_spec(width):
    return pl.BlockSpec((1, width), lambda b, j: (0, 0))


def _first_step(b, j):
    return jnp.logical_and(b == 0, j == 0)


def _pre_norm(x, head_rows, gain, bsz, lp, tr):
    nb = lp // tr
    first_real = head_rows.shape[0]
    assert first_real % 8 == 0 and first_real < tr
    x_spec = pl.BlockSpec((pl.Squeezed(), pl.Element(tr), pl.Element(D)),
                          lambda b, j: (b, pl.multiple_of(jnp.maximum(j * tr - first_real, 0), 8), 0))

    def body(x_ref, hd_ref, g_ref, h_ref, u_ref):
        j = pl.program_id(1)
        xt = x_ref[...]
        h = jnp.where(j == 0, jnp.concatenate([hd_ref[...], xt[:tr - first_real]], axis=0), xt)
        h_ref[...] = h
        u_ref[...] = _rms(h, g_ref[...])[0].astype(BF16)

    rs = _row_specs(tr, D, nb)
    return pl.pallas_call(
        body, name="mix_pre_norm", grid=(bsz, nb),
        in_specs=[x_spec, pl.BlockSpec((first_real, D), lambda b, j: (0, 0)), _vec_spec(D)],
        out_specs=[rs, rs],
        out_shape=[jax.ShapeDtypeStruct((bsz * lp, D), F32), jax.ShapeDtypeStruct((bsz * lp, D), BF16)],
        compiler_params=_cparams(("parallel", "parallel")),
    )(x, head_rows, gain)


def _mid_norms(h0, mix, g_post, g_pre, bsz, lp, tr):
    nb = lp // tr

    def body(h_ref, m_ref, gp_ref, gq_ref, h1_ref, u2_ref):
        h1 = h_ref[...] + _rms(m_ref[...], gp_ref[...])[0]
        h1_ref[...] = h1
        u2_ref[...] = _rms(h1, gq_ref[...])[0].astype(BF16)

    rs = _row_specs(tr, D, nb)
    return pl.pallas_call(
        body, name="mid_norms", grid=(bsz, nb),
        in_specs=[rs, rs, _vec_spec(D), _vec_spec(D)],
        out_specs=[rs, rs],
        out_shape=[jax.ShapeDtypeStruct(h0.shape, F32), jax.ShapeDtypeStruct(h0.shape, BF16)],
        compiler_params=_cparams(("parallel", "parallel")),
    )(h0, mix, g_post, g_pre)


def _loss_head(h1, ffn, tgt, g_post, bsz, lp, tr, first_real):
    nb = lp // tr
    assert first_real % 8 == 0 and first_real < tr
    tgt_spec = pl.BlockSpec((pl.Squeezed(), pl.Element(tr), pl.Element(D)),
                            lambda b, j: (b, pl.multiple_of(jnp.maximum(j * tr - first_real, 0), 8), 0))

    def body(h_ref, f_ref, t_ref, g_ref, dy_ref, dffn_ref, dg_ref, loss_ref):
        b, j = pl.program_id(0), pl.program_id(1)
        f = f_ref[...]
        g = g_ref[...]
        y = h_ref[...] + _rms(f, g)[0]
        t = j * tr + lax.broadcasted_iota(jnp.int32, (tr, 1), 0)
        tg = t_ref[...]
        tg = jnp.where(j == 0, jnp.concatenate([tg[tr - first_real:], tg[:tr - first_real]], axis=0), tg)
        err = jnp.where(t >= first_real, y - tg, 0.0)
        dy = err * (1.0 / D)
        dy_ref[...] = dy
        dx, dgc = _rms_bwd(f, g, dy)
        dffn_ref[...] = dx.astype(BF16)
        lsum = jnp.sum(jnp.sum(err * err, axis=0, keepdims=True), axis=1, keepdims=True) * (0.5 / D)

        @pl.when(_first_step(b, j))
        def _():
            dg_ref[...] = jnp.zeros_like(dg_ref)
            loss_ref[...] = jnp.zeros_like(loss_ref)

        dg_ref[...] += jnp.sum(dgc, axis=0, keepdims=True)
        loss_ref[...] += jnp.broadcast_to(lsum, loss_ref.shape)

    rs = _row_specs(tr, D, nb)
    return pl.pallas_call(
        body, name="loss_head", grid=(bsz, nb),
        in_specs=[rs, rs, tgt_spec, _vec_spec(D)],
        out_specs=[rs, rs, _vec_spec(D), _vec_spec(DH)],
        out_shape=[jax.ShapeDtypeStruct(h1.shape, F32), jax.ShapeDtypeStruct(h1.shape, BF16),
                   jax.ShapeDtypeStruct((1, D), F32), jax.ShapeDtypeStruct((1, DH), F32)],
        compiler_params=_cparams(("arbitrary", "arbitrary")),
    )(h1, ffn, tgt, g_post)


def _mid_norms_bwd(h1, mix, du2, dy, g_pre, g_post, bsz, lp, tr):
    nb = lp // tr

    def body(h_ref, m_ref, du_ref, dy_ref, gq_ref, gp_ref, dh1_ref, dmix_ref, dgq_ref, dgp_ref):
        b, j = pl.program_id(0), pl.program_id(1)
        dx, dgq = _rms_bwd(h_ref[...], gq_ref[...], du_ref[...])
        dh1 = dy_ref[...] + dx
        dh1_ref[...] = dh1
        dm, dgp = _rms_bwd(m_ref[...], gp_ref[...], dh1)
        dmix_ref[...] = dm.astype(BF16)

        @pl.when(_first_step(b, j))
        def _():
            dgq_ref[...] = jnp.zeros_like(dgq_ref)
            dgp_ref[...] = jnp.zeros_like(dgp_ref)

        dgq_ref[...] += jnp.sum(dgq, axis=0, keepdims=True)
        dgp_ref[...] += jnp.sum(dgp, axis=0, keepdims=True)

    rs = _row_specs(tr, D, nb)
    return pl.pallas_call(
        body, name="mid_norms_bwd", grid=(bsz, nb),
        in_specs=[rs, rs, rs, rs, _vec_spec(D), _vec_spec(D)],
        out_specs=[rs, rs, _vec_spec(D), _vec_spec(D)],
        out_shape=[jax.ShapeDtypeStruct(h1.shape, F32), jax.ShapeDtypeStruct(h1.shape, BF16),
                   jax.ShapeDtypeStruct((1, D), F32), jax.ShapeDtypeStruct((1, D), F32)],
        compiler_params=_cparams(("arbitrary", "arbitrary")),
    )(h1, mix, du2, dy, g_pre, g_post)


def _pre_norm_bwd(h0, du1, dh1, gain, bsz, lp, seq, padf):
    first_real = lp - seq
    tx = _div_tile(seq, 512)
    assert first_real % 8 == 0 and lp % first_real == 0

    def tokens(h_ref, du_ref, dh1_ref, g_ref, gx_ref, dg_ref):
        b, j = pl.program_id(0), pl.program_id(1)
        dx, dgc = _rms_bwd(h_ref[...], g_ref[...], du_ref[...])
        gx_ref[...] = dh1_ref[...] + dx

        @pl.when(_first_step(b, j))
        def _():
            dg_ref[...] = jnp.zeros_like(dg_ref)

        dg_ref[...] += jnp.sum(dgc, axis=0, keepdims=True)

    window = pl.BlockSpec((pl.Element(tx), pl.Element(D)),
                          lambda b, j: (pl.multiple_of(b * lp + first_real + j * tx, 8), 0))
    grad_x, dg_tokens = pl.pallas_call(
        tokens, name="pre_norm_bwd", grid=(bsz, seq // tx),
        in_specs=[window, window, window, _vec_spec(D)],
        out_specs=[pl.BlockSpec((pl.Squeezed(), tx, D), lambda b, j: (b, j, 0)), _vec_spec(D)],
        out_shape=[jax.ShapeDtypeStruct((bsz, seq, D), F32), jax.ShapeDtypeStruct((1, D), F32)],
        compiler_params=_cparams(("arbitrary", "arbitrary")),
    )(h0, du1, dh1, gain)

    def head(h_ref, du_ref, dh1_ref, g_ref, dgt_ref, dg_ref, dmeta_ref):
        b = pl.program_id(0)
        dx, dgc = _rms_bwd(h_ref[...], g_ref[...], du_ref[...])
        dh0 = dh1_ref[...] + dx

        @pl.when(b == 0)
        def _():
            dg_ref[...] = dgt_ref[...]
            dmeta_ref[...] = jnp.zeros_like(dmeta_ref)

        dg_ref[...] += jnp.sum(dgc, axis=0, keepdims=True)
        dmeta_ref[...] += dh0[padf:padf + N_META, :]

    rows = pl.BlockSpec((first_real, D), lambda b: (b * (lp // first_real), 0))
    vec = pl.BlockSpec((1, D), lambda b: (0, 0))
    dg, dmeta = pl.pallas_call(
        head, name="pre_norm_bwd_head", grid=(bsz,),
        in_specs=[rows, rows, rows, vec, vec],
        out_specs=[vec, pl.BlockSpec((N_META, D), lambda b: (0, 0))],
        out_shape=[jax.ShapeDtypeStruct((1, D), F32), jax.ShapeDtypeStruct((N_META, D), F32)],
        compiler_params=_cparams(("arbitrary",)),
    )(h0, du1, dh1, gain, dg_tokens)
    return grad_x, dg, dmeta


def _shift_down(x, s, lp):
    return x if s == 0 else pltpu.roll(x, s, axis=0)


def _shift_up(x, s, lp):
    return x if s == 0 else pltpu.roll(x, lp - s, axis=0)


def _conv_fwd(x, w, taps, lp):
    y = None
    for i in range(taps):
        term = _shift_down(x, taps - 1 - i, lp) * w[i:i + 1, :]
        y = term if y is None else y + term
    return y


def _conv_bwd(x, w, dy, taps, lp):
    dx = None
    dws = []
    for i in range(taps):
        s = taps - 1 - i
        term = _shift_up(dy, s, lp) * w[i:i + 1, :]
        dx = term if dx is None else dx + term
        dws.append(jnp.sum(dy * _shift_down(x, s, lp), axis=0, keepdims=True))
    return dx, jnp.concatenate(dws, axis=0)


def _slab(lp, col0):
    return pl.BlockSpec((lp, DH), lambda b, s: (b, col0 // DH + s))


def _qkv_act(c, s):
    y = _silu(c)
    r = lax.rsqrt(jnp.sum(y * y, axis=-1, keepdims=True) + EPS)
    scale = jnp.where(s < HEADS, DH ** -0.5, 1.0)
    return y, r, scale


def _gdn_prep(proj, conv_qkv, bsz, lp, padf):
    def body(p_ref, w_ref, o_ref):
        s = pl.program_id(1)
        c = _conv_fwd(p_ref[...].astype(F32), w_ref[...], KQ, lp)
        y, r, scale = _qkv_act(c, s)
        out = jnp.where(s < 2 * HEADS, y * (r * scale), y)
        t = lax.broadcasted_iota(jnp.int32, (lp, 1), 0)
        o_ref[...] = jnp.where(t >= padf, out, 0.0)

    return pl.pallas_call(
        body, name="gdn_prep", grid=(bsz, 3 * HEADS),
        in_specs=[_slab(lp, Q0), pl.BlockSpec((KQ, DH), lambda b, s: (0, s))],
        out_specs=_slab(lp, 0),
        out_shape=jax.ShapeDtypeStruct((bsz * lp, 3 * GW), F32),
        compiler_params=_cparams(("parallel", "parallel")),
    )(proj, conv_qkv)


def _gdn_prep_bwd(proj, conv_qkv, dq, dk, dv, bsz, lp, padf):
    def body(p_ref, w_ref, dq_ref, dk_ref, dv_ref, dp_ref, dw_ref):
        b, s = pl.program_id(0), pl.program_id(1)
        x = p_ref[...].astype(F32)
        w = w_ref[...]
        c = _conv_fwd(x, w, KQ, lp)
        y, r, scale = _qkv_act(c, s)
        d = jnp.where(s < HEADS, dq_ref[...], jnp.where(s < 2 * HEADS, dk_ref[...], dv_ref[...]))
        dn = d * scale
        n = y * r
        dy_norm = r * (dn - n * jnp.sum(dn * n, axis=-1, keepdims=True))
        dyy = jnp.where(s < 2 * HEADS, dy_norm, dn)
        t = lax.broadcasted_iota(jnp.int32, (lp, 1), 0)
        dc = jnp.where(t >= padf, dyy * _dsilu(c), 0.0)
        dx, dw = _conv_bwd(x, w, dc, KQ, lp)
        dp_ref[...] = jnp.where(t >= padf, dx, 0.0).astype(BF16)

        @pl.when(jnp.logical_and(b == 0, s == 0))
        def _():
            dw_ref[...] = jnp.zeros_like(dw_ref)

        col_block = jnp.right_shift(lax.broadcasted_iota(jnp.int32, (KQ, 3 * GW), 1), DH.bit_length() - 1)
        dw_ref[...] += jnp.where(col_block == s, jnp.tile(dw, (1, 3 * HEADS)), 0.0)

    sl = lambda col0: pl.BlockSpec((lp, DH), lambda b, s: (b, col0 // DH + s))
    part = lambda k: pl.BlockSpec((lp, DH), lambda b, s: (b, jnp.clip(s - k * HEADS, 0, HEADS - 1)))
    return pl.pallas_call(
        body, name="gdn_prep_bwd", grid=(bsz, 3 * HEADS),
        in_specs=[sl(Q0), pl.BlockSpec((KQ, DH), lambda b, s: (0, s)), part(0), part(1), part(2)],
        out_specs=[sl(0), pl.BlockSpec((KQ, 3 * GW), lambda b, s: (0, 0))],
        out_shape=[jax.ShapeDtypeStruct((bsz * lp, 3 * GW), BF16), jax.ShapeDtypeStruct((KQ, 3 * GW), F32)],
        compiler_params=_cparams(("arbitrary", "arbitrary")),
    )(proj, conv_qkv, dq, dk, dv)


def _softplus(x):
    e = jnp.exp(-jnp.abs(x))
    one_e = 1.0 + e
    l1p = jnp.where(one_e == 1.0, e, jnp.log(one_e) * (e / (one_e - 1.0)))
    return jnp.maximum(x, 0.0) + l1p


def _tri(n, kind):
    i = lax.broadcasted_iota(jnp.int32, (n, n), 0)
    j = lax.broadcasted_iota(jnp.int32, (n, n), 1)
    return {"incl": i >= j, "strict": i > j, "upper": i <= j}[kind]


def _gates(proj, alog_v, dtb_v, bsz, lp, padf):
    nchunk = lp // CH

    def body(p_ref, al_ref, dt_ref, gb_ref, cs_ref):
        x = p_ref[...].astype(F32)
        lane = lax.broadcasted_iota(jnp.int32, (lp, DH), 1)
        t = lax.broadcasted_iota(jnp.int32, (lp, DH), 0)
        beta = jax.nn.sigmoid(x)
        g = -jnp.exp(al_ref[...]) * _softplus(x + dt_ref[...])
        gb = jnp.where(lane < HEADS, beta, jnp.where(lane < 2 * HEADS, g, 0.0))
        gb_ref[...] = jnp.where(t >= padf, gb, 0.0)
        ltri = _tri(CH, "incl").astype(BF16)
        rest = [gb_ref[pl.ds(n * CH, CH), :] for n in range(nchunk)]
        sums = None
        for _ in range(3):
            piece = [v.astype(BF16) for v in rest]
            rest = [v - p.astype(F32) for v, p in zip(rest, piece)]
            part = [jnp.dot(ltri, p, preferred_element_type=F32) for p in piece]
            sums = part if sums is None else [a + b for a, b in zip(sums, part)]
        for n in range(nchunk):
            cs_ref[pl.ds(n * CH, CH), :] = sums[n]

    blk = pl.BlockSpec((lp, DH), lambda b: (b, 0))
    vec = pl.BlockSpec((1, DH), lambda b: (0, 0))
    out = pl.BlockSpec((lp, DH), lambda b: (b, 0))
    return pl.pallas_call(
        body, name="gates", grid=(bsz,),
        in_specs=[blk, vec, vec], out_specs=[out, out],
        out_shape=[jax.ShapeDtypeStruct((bsz * lp, DH), F32)] * 2,
        compiler_params=_cparams(("parallel",)),
    )(proj, alog_v, dtb_v)


def _gates_bwd(proj, alog_v, dtb_v, dbeta_b, dg_b, gb, bsz, lp, padf):
    def body(p_ref, al_ref, dt_ref, db_ref, dg_ref, gb_ref, dl_ref, dv_ref):
        b = pl.program_id(0)
        x = p_ref[...].astype(F32)
        lane = lax.broadcasted_iota(jnp.int32, (lp, DH), 1)
        t = lax.broadcasted_iota(jnp.int32, (lp, DH), 0)
        dbeta = jnp.zeros((lp, DH), F32)
        dg = jnp.zeros((lp, DH), F32)
        for h in range(HEADS):
            dbeta = jnp.where(lane == h, db_ref[h], dbeta)
            dg = jnp.where(lane == HEADS + h, dg_ref[h], dg)
        real = t >= padf
        dbeta = jnp.where(real, dbeta, 0.0)
        dg = jnp.where(real, dg, 0.0)
        sb = jax.nn.sigmoid(x)
        neg_ea = -jnp.exp(al_ref[...])
        dsp = dg * neg_ea * jax.nn.sigmoid(x + dt_ref[...])
        dl = dbeta * sb * (1.0 - sb) + dsp
        dl_ref[...] = jnp.concatenate([dl, jnp.zeros((lp, BA_W - DH), F32)], axis=1).astype(BF16)
        dalog = jnp.sum(dg * gb_ref[...], axis=0, keepdims=True)
        ddt = jnp.sum(dsp, axis=0, keepdims=True)

        @pl.when(b == 0)
        def _():
            dv_ref[...] = jnp.zeros_like(dv_ref)

        dv_ref[...] += jnp.concatenate([dalog, ddt], axis=0)

    blk = pl.BlockSpec((lp, DH), lambda b: (b, 0))
    vec = pl.BlockSpec((1, DH), lambda b: (0, 0))
    hb = pl.BlockSpec((HEADS, lp, DH), lambda b: (b, 0, 0))
    return pl.pallas_call(
        body, name="gates_bwd", grid=(bsz,),
        in_specs=[blk, vec, vec, hb, hb, pl.BlockSpec((lp, DH), lambda b: (b, 0))],
        out_specs=[pl.BlockSpec((lp, BA_W), lambda b: (b, 0)), pl.BlockSpec((2, DH), lambda b: (0, 0))],
        out_shape=[jax.ShapeDtypeStruct((bsz * lp, BA_W), BF16), jax.ShapeDtypeStruct((2, DH), F32)],
        compiler_params=_cparams(("arbitrary",)),
    )(proj, alog_v, dtb_v, dbeta_b, dg_b, gb)


def _mix_heads(proj, o, gdn_norm, conv_sc, bsz, lp, padf):
    def body(o_ref, z_ref, gn_ref, sx_ref, sb_ref, sc_ref, w_ref, og_ref, os_ref):
        og_ref[...] = (_rms(o_ref[...], gn_ref[...])[0] * _silu(z_ref[...].astype(F32))).astype(BF16)
        conv = _conv_fwd(sc_ref[...].astype(F32) * sx_ref[...].astype(F32), w_ref[...], KS, lp)
        t = lax.broadcasted_iota(jnp.int32, (lp, 1), 0)
        os_ref[...] = jnp.where(t >= padf, sb_ref[...].astype(F32) * conv, 0.0).astype(BF16)

    half = _slab(lp, 0)
    return pl.pallas_call(
        body, name="mix_heads", grid=(bsz, HEADS),
        in_specs=[half, _slab(lp, Z0), pl.BlockSpec((1, DH), lambda b, s: (0, 0)),
                  _slab(lp, SX0), _slab(lp, SB0), _slab(lp, SC0), pl.BlockSpec((KS, DH), lambda b, s: (0, s))],
        out_specs=[half, half],
        out_shape=[jax.ShapeDtypeStruct((bsz * lp, GW), BF16)] * 2,
        compiler_params=_cparams(("parallel", "parallel")),
    )(o, proj, gdn_norm, proj, proj, proj, conv_sc)


def _mix_heads_bwd(proj, o, gdn_norm, conv_sc, dmixin, bsz, lp, padf):
    def body(o_ref, z_ref, gn_ref, sx_ref, sb_ref, sc_ref, w_ref, dg_ref, ds_ref,
             do_ref, dz_ref, dsx_ref, dsb_ref, dsc_ref, dgn_ref, dw_ref):
        s, b = pl.program_id(0), pl.program_id(1)
        t = lax.broadcasted_iota(jnp.int32, (lp, 1), 0)
        real = t >= padf
        o, z, gn, d = o_ref[...], z_ref[...].astype(F32), gn_ref[...], dg_ref[...]
        sz = _silu(z)
        on = _rms(o, gn)[0]
        dz_ref[...] = (d * on * _dsilu(z)).astype(BF16)
        dox, dgn = _rms_bwd(o, gn, d * sz)
        do_ref[...] = dox
        sx, sb, sc = sx_ref[...].astype(F32), sb_ref[...].astype(F32), sc_ref[...].astype(F32)
        w, e = w_ref[...], ds_ref[...]
        e = jnp.where(real, e, 0.0)
        xin = sc * sx
        conv = _conv_fwd(xin, w, KS, lp)
        dsb_ref[...] = (e * conv).astype(BF16)
        dxin, dw = _conv_bwd(xin, w, e * sb, KS, lp)
        dxin = jnp.where(real, dxin, 0.0)
        dsx_ref[...] = (dxin * sc).astype(BF16)
        dsc_ref[...] = (dxin * sx).astype(BF16)

        @pl.when(jnp.logical_and(s == 0, b == 0))
        def _():
            dgn_ref[...] = jnp.zeros_like(dgn_ref)

        @pl.when(b == 0)
        def _():
            dw_ref[...] = jnp.zeros_like(dw_ref)

        dgn_ref[...] += jnp.sum(dgn, axis=0, keepdims=True)
        dw_ref[...] += dw

    sl = lambda col0: pl.BlockSpec((lp, DH), lambda s, b: (b, col0 // DH + s))
    half = sl(0)
    return pl.pallas_call(
        body, name="mix_heads_bwd", grid=(HEADS, bsz),
        in_specs=[half, sl(Z0), pl.BlockSpec((1, DH), lambda s, b: (0, 0)), sl(SX0), sl(SB0), sl(SC0),
                  pl.BlockSpec((KS, DH), lambda s, b: (0, s)), sl(0), sl(GW)],
        out_specs=[half] * 5 + [pl.BlockSpec((1, DH), lambda s, b: (0, 0)), pl.BlockSpec((KS, DH), lambda s, b: (0, s))],
        out_shape=[jax.ShapeDtypeStruct((bsz * lp, GW), F32)] + [jax.ShapeDtypeStruct((bsz * lp, GW), BF16)] * 4
        + [jax.ShapeDtypeStruct((1, DH), F32), jax.ShapeDtypeStruct((KS, SW), F32)],
        compiler_params=_cparams(("arbitrary", "arbitrary")),
    )(o, proj, gdn_norm, proj, proj, proj, conv_sc, dmixin, dmixin)


def _split_dot(a, b, dims=NN):
    ah = a.astype(BF16)
    al = (a - ah.astype(F32)).astype(BF16)
    bh = b.astype(BF16)
    bl = (b - bh.astype(F32)).astype(BF16)

    def d(x, y):
        return lax.dot_general(x, y, (dims, ((), ())), preferred_element_type=F32)

    return d(ah, bh) + (d(ah, bl) + d(al, bh))


def _eye(n):
    return (lax.broadcasted_iota(jnp.int32, (n, n), 0) == lax.broadcasted_iota(jnp.int32, (n, n), 1)).astype(F32)


def _each(f, *lists):
    return [f(*xs) for xs in zip(*lists)]


def _inv_unit_lower(a_list):
    eye = _eye(CH)
    p = _each(lambda a: -a, a_list)
    t = _each(lambda x: eye + x, p)
    p = _each(lambda x: _split_dot(x, x), p)
    for level in range(5):
        if level < 4:
            prod = _each(lambda tt, pp: _split_dot(jnp.concatenate([tt, pp], axis=0), pp), t, p)
            t = _each(lambda tt, pr: tt + pr[:CH], t, prod)
            p = _each(lambda pr: pr[CH:], prod)
        else:
            t = _each(lambda tt, pp: tt + _split_dot(tt, pp), t, p)
    return t


def _chunk_gates(gc_ref, be_ref, gt_ref, g):
    rows = pl.ds(g * CH, CH)
    gc = gc_ref[0, rows, :]
    be = be_ref[0, rows, :]
    gcl = gc_ref[0, pl.ds(g * CH + CH - 1, 1), :]
    gt = gt_ref[0, g]
    incl, upper = _tri(CH, "incl"), _tri(CH, "upper")
    dec = jnp.where(incl, jnp.exp(jnp.where(incl, gc[:, :CH] - gt, 0.0)), 0.0)
    dec_t = jnp.where(upper, jnp.exp(jnp.where(upper, gt - gc[:, :CH], 0.0)), 0.0)
    return rows, be, dec, dec_t, jnp.exp(gc), jnp.exp(gcl - gc), jnp.exp(gcl)


def _chunk_group(nchunk):
    return 3 if nchunk % 3 == 0 else 1


def _local_group(nchunk):
    return 11 if nchunk % 11 == 0 else _chunk_group(nchunk)


def _gdn_local_specs(lp, grp):
    ngrp = lp // (grp * CH)
    head = lambda col0: pl.BlockSpec((grp * CH, DH), lambda i, j: ((i // HEADS) * ngrp + j, col0 // DH + i % HEADS))
    lane_b = pl.BlockSpec((1, grp * CH, DH), lambda i, j: (i, j, 0))
    per_chunk = lambda r, c: pl.BlockSpec((1, grp, r, c), lambda i, j: (i, j, 0, 0))
    return ngrp, head, lane_b, per_chunk


def _gdn_local_fwd(qkv, gcb, bb, gct, bsz, lp):
    nchunk = lp // CH
    grp = _local_group(nchunk)
    ngrp, head, lane_b, per_chunk = _gdn_local_specs(lp, grp)
    bh = bsz * HEADS

    def body(q_ref, k_ref, v_ref, gc_ref, be_ref, gt_ref,
             u_ref, l1_ref, l2_ref, l3_ref, wt_ref, kd_ref, ti_ref, gl_ref):
        incl, strict, upper = _tri(CH, "incl"), _tri(CH, "strict"), _tri(CH, "upper")
        eye = _eye(CH)
        rows, be, dec, dec_t, eg, ekd, gl = zip(*[_chunk_gates(gc_ref, be_ref, gt_ref, g) for g in range(grp)])
        q = [q_ref[r, :] for r in rows]
        k = [k_ref[r, :] for r in rows]
        v = [v_ref[r, :] for r in rows]
        kb = _each(lambda x, y: x * y, k, be)
        a = _each(lambda x, y, d: jnp.where(strict, _bdot(x, y, NT) * d, 0.0), kb, k, dec)
        tinv = _inv_unit_lower(a)
        u = _each(lambda t, x, y: _bdot(t, x * y, NN), tinv, v, be)
        w = _each(lambda t, x, y: _bdot(t, x * y, NN), tinv, kb, eg)
        qk = _each(lambda x, y, d: jnp.where(incl, _bdot(x, y, NT) * d, 0.0), q, k, dec)
        qk_t = _each(lambda x, y, d: jnp.where(upper, _bdot(y, x, NT) * d, 0.0), q, k, dec_t)
        qd = _each(lambda x, y: x * y, q, eg)
        kd = _each(lambda x, y: x * y, k, ekd)
        kd_t = _each(lambda x: _bdot(x, eye, TN), kd)
        qd_t = _each(lambda x: _bdot(x, eye, TN), qd)
        w_t = _each(lambda x: _bdot(x, eye, TN), w)
        for g in range(grp):
            u_ref[rows[g], :] = u[g]
            l1_ref[0, g] = jnp.concatenate([w[g], qd[g]], axis=0).astype(BF16)
            l2_ref[0, g] = jnp.concatenate([qk[g], kd_t[g]], axis=0).astype(BF16)
            l3_ref[0, g] = jnp.concatenate([qk_t[g], qd_t[g]], axis=0).astype(BF16)
            wt_ref[0, g] = w_t[g].astype(BF16)
            kd_ref[0, g] = kd[g].astype(BF16)
            ti_ref[0, g] = tinv[g]
            gl_ref[0, g] = gl[g]

    r = bsz * lp
    shapes = [((r, GW), F32), ((bh, nchunk, 2 * CH, DH), BF16), ((bh, nchunk, 3 * CH, CH), BF16),
              ((bh, nchunk, 3 * CH, CH), BF16), ((bh, nchunk, DH, CH), BF16), ((bh, nchunk, CH, DH), BF16),
              ((bh, nchunk, CH, CH), F32), ((bh, nchunk, 1, DH), F32)]
    return pl.pallas_call(
        body, name="gdn_local_fwd", grid=(bh, ngrp),
        in_specs=[head(0), head(GW), head(2 * GW), lane_b, lane_b, per_chunk(1, CH)],
        out_specs=[head(0), per_chunk(2 * CH, DH), per_chunk(3 * CH, CH), per_chunk(3 * CH, CH), per_chunk(DH, CH),
                   per_chunk(CH, DH), per_chunk(CH, CH), per_chunk(1, DH)],
        out_shape=[jax.ShapeDtypeStruct(s, dt) for s, dt in shapes],
        compiler_params=_cparams(("parallel", "parallel")),
    )(qkv, qkv, qkv, gcb, bb, gct)


def _scan_specs(bsz, lp, grp, order):
    ngrp = lp // (grp * CH)
    bh = bsz * HEADS
    seq = lambda: pl.BlockSpec((bsz, grp * CH, GW), lambda s: (0, order(s, ngrp), 0))
    per_chunk = lambda r, c: pl.BlockSpec((bh, grp, r, c), lambda s: (0, order(s, ngrp), 0, 0))
    return ngrp, bh, seq, per_chunk


def _gdn_scan_fwd(u, l1, l2, gl, bsz, lp):
    nchunk = lp // CH
    grp = _chunk_group(nchunk)
    ngrp, bh, seq, per_chunk = _scan_specs(bsz, lp, grp, lambda s, n: s)

    def body(u_ref, l1_ref, l2_ref, gl_ref, o_ref, vn_ref, st_ref, s_scr):
        @pl.when(pl.program_id(0) == 0)
        def _():
            s_scr[...] = jnp.zeros_like(s_scr)

        chains = list(range(bh))
        where = [(i // HEADS, slice((i % HEADS) * DH, (i % HEADS + 1) * DH)) for i in chains]

        def dot(a, b):
            return jnp.dot(a, b.astype(BF16), preferred_element_type=F32)

        for c in range(grp):
            rows = slice(c * CH, (c + 1) * CH)
            s = [s_scr[i] for i in chains]
            for i in chains:
                st_ref[i, c] = s[i]
            x = _each(lambda i, si: dot(l1_ref[i, c], si), chains, s)
            vnew = _each(lambda bc, xi: u_ref[bc[0], rows, bc[1]] - xi[:CH], where, x)
            y = _each(lambda i, vi: dot(l2_ref[i, c], vi), chains, vnew)
            for i, (b, cols) in zip(chains, where):
                o_ref[b, rows, cols] = x[i][CH:] + y[i][:CH]
                vn_ref[b, rows, cols] = vnew[i].astype(BF16)
                s_scr[i] = s[i] * gl_ref[i, c] + y[i][CH:]

    return pl.pallas_call(
        body, name="gdn_scan_fwd", grid=(ngrp,),
        in_specs=[seq(), per_chunk(2 * CH, DH), per_chunk(3 * CH, CH), per_chunk(1, DH)],
        out_specs=[seq(), seq(), per_chunk(DH, DH)],
        out_shape=[jax.ShapeDtypeStruct((bsz, lp, GW), F32), jax.ShapeDtypeStruct((bsz, lp, GW), BF16),
                   jax.ShapeDtypeStruct((bh, nchunk, DH, DH), F32)],
        scratch_shapes=[pltpu.VMEM((bh, DH, DH), F32)],
        compiler_params=_cparams(("arbitrary",)),
    )(u.reshape(bsz, lp, GW), l1, l2, gl)


def _gdn_scan_bwd(d_o, l3, wt, kd, gl, bsz, lp):
    nchunk = lp // CH
    grp = _chunk_group(nchunk)
    ngrp, bh, seq, per_chunk = _scan_specs(bsz, lp, grp, lambda s, n: n - 1 - s)

    def body(do_ref, l3_ref, wt_ref, kd_ref, gl_ref, dvn_ref, dst_ref, ds_scr):
        @pl.when(pl.program_id(0) == 0)
        def _():
            ds_scr[...] = jnp.zeros_like(ds_scr)

        chains = list(range(bh))
        where = [(i // HEADS, slice((i % HEADS) * DH, (i % HEADS + 1) * DH)) for i in chains]

        def dot(a, b):
            return jnp.dot(a, b.astype(BF16), preferred_element_type=F32)

        for c in reversed(range(grp)):
            rows = slice(c * CH, (c + 1) * CH)
            ds = [ds_scr[i] for i in chains]
            for i in chains:
                dst_ref[i, c] = ds[i]
            p = _each(lambda i, bc: dot(l3_ref[i, c], do_ref[bc[0], rows, bc[1]]), chains, where)
            dvn = _each(lambda i, pi, di: pi[:CH] + dot(kd_ref[i, c], di), chains, p, ds)
            wd = _each(lambda i, vi: dot(wt_ref[i, c], vi), chains, dvn)
            for i, (b, cols) in zip(chains, where):
                dvn_ref[b, rows, cols] = dvn[i].astype(BF16)
                ds_scr[i] = ds[i] * gl_ref[i, c] + p[i][CH:] - wd[i]

    return pl.pallas_call(
        body, name="gdn_scan_bwd", grid=(ngrp,),
        in_specs=[seq(), per_chunk(3 * CH, CH), per_chunk(DH, CH), per_chunk(CH, DH), per_chunk(1, DH)],
        out_specs=[seq(), per_chunk(DH, DH)],
        out_shape=[jax.ShapeDtypeStruct((bsz, lp, GW), BF16), jax.ShapeDtypeStruct((bh, nchunk, DH, DH), F32)],
        scratch_shapes=[pltpu.VMEM((bh, DH, DH), F32)],
        compiler_params=_cparams(("arbitrary",)),
    )(d_o.reshape(bsz, lp, GW), l3, wt, kd, gl)


def _gdn_local_bwd(qkv, gcb, bb, gct, tinv_all, states, dstates, vnew, dvnew, d_o, bsz, lp):
    nchunk = lp // CH
    grp = _local_group(nchunk)
    ngrp, head, lane_b, per_chunk = _gdn_local_specs(lp, grp)
    bh = bsz * HEADS

    def body(q_ref, k_ref, v_ref, gc_ref, be_ref, gt_ref, ti_ref, st_ref, dst_ref, vn_ref, dvn_ref, do_ref,
             dq_ref, dk_ref, dv_ref, dg_ref, db_ref):
        incl, strict = _tri(CH, "incl"), _tri(CH, "strict")
        upper = _tri(CH, "upper").astype(F32)
        ones = jnp.ones((CH, DH), F32)
        last = lax.broadcasted_iota(jnp.int32, (CH, 1), 0) == CH - 1

        def rsum(x):
            return jnp.sum(x, axis=-1, keepdims=True)

        def mul(xs, ys):
            return _each(lambda x, y: x * y, xs, ys)

        rows, be, dec, _, eg, ekd, gl = zip(*[_chunk_gates(gc_ref, be_ref, gt_ref, g) for g in range(grp)])
        q = [q_ref[r, :] for r in rows]
        k = [k_ref[r, :] for r in rows]
        v = [v_ref[r, :] for r in rows]
        vnew = [vn_ref[r, :] for r in rows]
        dvn = [dvn_ref[r, :] for r in rows]
        do = [do_ref[r, :] for r in rows]
        tinv = [ti_ref[0, g] for g in range(grp)]
        s = [st_ref[0, g] for g in range(grp)]
        ds = [dst_ref[0, g] for g in range(grp)]
        kb, vb, qd, kd = mul(k, be), mul(v, be), mul(q, eg), mul(k, ekd)
        kbg = mul(kb, eg)
        a = _each(lambda x, y, d: jnp.where(strict, _bdot(x, y, NT) * d, 0.0), kb, k, dec)
        qk = _each(lambda x, y, d: jnp.where(incl, _bdot(x, y, NT) * d, 0.0), q, k, dec)
        dqk = _each(lambda x, y: jnp.where(incl, _bdot(x, y, NT), 0.0), do, vnew)
        dqd = _each(lambda x, y: _bdot(x, y, NT), do, s)
        dkd = _each(lambda x, y: _bdot(x, y, NT), vnew, ds)
        dgl = _each(lambda x, y: jnp.sum(rsum(x * y), axis=0, keepdims=True), s, ds)
        dw = _each(lambda x, y: -_bdot(x, y, NT), dvn, s)
        dvb = _each(lambda t, x: _bdot(t, x, TN), tinv, dvn)
        dkbg = _each(lambda t, x: _bdot(t, x, TN), tinv, dw)
        dt = _each(lambda x, y, z, w: _bdot(x, y, NT) + _bdot(z, w, NT), dvn, vb, dw, kbg)
        tdt = _each(lambda t, x: _split_dot(t, x, TN), tinv, dt)
        da = _each(lambda x, t: jnp.where(strict, -_split_dot(x, t, NT), 0.0), tdt, tinv)
        dp, dr = mul(da, dec), mul(dqk, dec)
        dkb = _each(lambda x, y, z, e: _bdot(x, y, NN) + z * e, dp, k, dkbg, eg)
        dk = _each(lambda p_, kb_, r_, q_, dkd_, ekd_, dkb_, be_:
                   _bdot(p_, kb_, TN) + _bdot(r_, q_, TN) + dkd_ * ekd_ + dkb_ * be_,
                   dp, kb, dr, q, dkd, ekd, dkb, be)
        dq = _each(lambda r_, k_, dqd_, eg_: _bdot(r_, k_, NN) + dqd_ * eg_, dr, k, dqd, eg)
        m = _each(lambda da_, a_, dqk_, qk_: da_ * a_ + dqk_ * qk_, da, a, dqk, qk)
        m_cols = _each(lambda x: _split_dot(x, ones, TN), m)
        kd_term = _each(lambda x, y: rsum(x * y), dkd, kd)
        dgc = _each(lambda m_, mc, dqd_, qd_, kt, dkbg_, kbg_, dgl_, gl_:
                    rsum(m_) - mc + rsum(dqd_ * qd_) - kt + rsum(dkbg_ * kbg_)
                    + jnp.where(last, jnp.sum(kt, axis=0, keepdims=True) + dgl_ * gl_, 0.0),
                    m, m_cols, dqd, qd, kd_term, dkbg, kbg, dgl, gl)
        dg = _each(lambda x: _split_dot(upper, x, NN), dgc)
        for g in range(grp):
            dq_ref[rows[g], :] = dq[g]
            dk_ref[rows[g], :] = dk[g]
            dv_ref[rows[g], :] = dvb[g] * be[g]
            dg_ref[0, rows[g], :] = dg[g]
            db_ref[0, rows[g], :] = jnp.broadcast_to(rsum(dkb[g] * k[g]) + rsum(dvb[g] * v[g]), (CH, DH))

    r = bsz * lp
    return pl.pallas_call(
        body, name="gdn_local_bwd", grid=(bh, ngrp),
        in_specs=[head(0), head(GW), head(2 * GW), lane_b, lane_b, per_chunk(1, CH), per_chunk(CH, CH),
                  per_chunk(DH, DH), per_chunk(DH, DH), head(0), head(0), head(0)],
        out_specs=[head(0), head(0), head(0), lane_b, lane_b],
        out_shape=[jax.ShapeDtypeStruct((r, GW), F32)] * 3 + [jax.ShapeDtypeStruct((bh, lp, DH), F32)] * 2,
        compiler_params=_cparams(("parallel", "parallel")),
    )(qkv, qkv, qkv, gcb, bb, gct, tinv_all, states, dstates, vnew, dvnew, d_o)


def _place():
    return lax.axis_index("x"), lax.axis_index("y"), lax.axis_index("c")


def _any_specs(n):
    return [pl.BlockSpec(memory_space=pl.ANY)] * n


def _all_gather(name, shards):
    n = len(shards)

    def body(*refs):
        ins, outs = refs[:n], refs[n:2 * n]
        send_sems, recv_sems, local_sems = refs[2 * n:2 * n + 3]
        stage = refs[2 * n + 3:]
        x, y, c = _place()
        me, sibling = (x, y, c), (x, y, 1 - c)
        chips = [(1 - x, y), (x, 1 - y), (1 - x, 1 - y)]

        def slot(i, p):
            return outs[i].at[4 * p[0] + 2 * p[1] + p[2]]

        def copy(i, k, block, to, src=None):
            return pltpu.make_async_remote_copy(
                src_ref=slot(i, block) if src is None else src, dst_ref=slot(i, block),
                send_sem=send_sems.at[i, k], recv_sem=recv_sems.at[i, k], device_id=to, device_id_type=MESH)

        load = [pltpu.make_async_copy(ins[i], stage[i], local_sems.at[i]) for i in range(n)]
        for cp in load:
            cp.start()
        first = []
        for i in range(n):
            first.append(copy(i, 0, me, sibling, src=ins[i]))
            first += [copy(i, 1 + j, me, (*chip, c), src=ins[i]) for j, chip in enumerate(chips)]
        for cp in first:
            cp.start()
        for cp in load:
            cp.wait()
        mine = [pltpu.make_async_copy(stage[i], slot(i, me), local_sems.at[i]) for i in range(n)]
        for cp in mine:
            cp.start()
        passed = []
        for j, chip in enumerate(chips):
            for i in range(n):
                copy(i, 1 + j, (*chip, c), me).wait_recv()
                fwd = copy(i, 4 + j, (*chip, c), sibling)
                fwd.start()
                passed.append(fwd)
        for i in range(n):
            copy(i, 0, sibling, me).wait_recv()
            for j, chip in enumerate(chips):
                copy(i, 4 + j, (*chip, 1 - c), me).wait_recv()
        for cp in first + passed:
            cp.wait_send()
        for cp in mine:
            cp.wait()

    return pl.pallas_call(
        body, name=name,
        in_specs=_any_specs(n), out_specs=_any_specs(n),
        out_shape=[jax.ShapeDtypeStruct((N_DEV,) + s.shape, s.dtype) for s in shards],
        scratch_shapes=[pltpu.SemaphoreType.DMA((n, 7)), pltpu.SemaphoreType.DMA((n, 7)),
                        pltpu.SemaphoreType.DMA((n,))] + [pltpu.VMEM(s.shape, s.dtype) for s in shards],
    )(*shards)


HBM_SPEC = pl.BlockSpec(memory_space=pltpu.HBM)
SEM_SPEC = pl.BlockSpec(memory_space=pltpu.SEMAPHORE)
EFFECT = pltpu.SideEffectType.DATAFLOW_SIDE_EFFECTING
N_PEER = N_DEV - 1


def _peer(r):
    x, y, c = _place()
    return ((1 - x) if r & 4 else x, (1 - y) if r & 2 else y, (1 - c) if r & 1 else c)


def _slot_of(p):
    return 4 * p[0] + 2 * p[1] + p[2]


def _hbm(a):
    return pltpu.with_memory_space_constraint(a, pltpu.HBM)


def _gather_pattern(src, land):
    me = _place()
    return [(src, land.at[_slot_of(me)], _peer(r), land.at[_slot_of(_peer(r))]) for r in range(1, N_DEV)]


def _alltoall_pattern(src, land):
    me = _place()
    return [(src.at[_slot_of(_peer(r))], land.at[_slot_of(me)], _peer(r), land.at[_slot_of(_peer(r))])
            for r in range(1, N_DEV)]


def _chipcast_pattern(src, land):
    x, y, c = _place()
    me = (x, y, c)
    peers = [(x, y, 1 - c), (1 - x, y, c), (x, 1 - y, c), (1 - x, 1 - y, c)]
    return [(src, land.at[_slot_of(me)], p, land.at[_slot_of(p)]) for p in peers]


def _forward_pattern(land, _):
    x, y, c = _place()
    chips = [(1 - x, y), (x, 1 - y), (1 - x, 1 - y)]
    return [(land.at[_slot_of((*ch, c))], land.at[_slot_of((*ch, c))], (x, y, 1 - c), land.at[_slot_of((*ch, 1 - c))])
            for ch in chips]


def _sibling_pattern(src, land):
    x, y, c = _place()
    return [(src.at[ch, 1 - c], land.at[ch], (x, y, 1 - c), land.at[ch]) for ch in range(4)]


def _chip_pattern(src, land):
    x, y, c = _place()
    chips = [(1 - x, y), (x, 1 - y), (1 - x, 1 - y)]
    return [(src.at[2 * ch[0] + ch[1]], land.at[j], (*ch, c), land.at[j]) for j, ch in enumerate(chips)]


def _split_start(name, groups, pattern, land_shape, after=None):
    sizes = [len(g) for g in groups]
    arrays = [s for g in groups for s in g]
    n = len(arrays)
    inplace = land_shape is None
    fresh = [] if inplace else [lax.empty(land_shape(s.shape), s.dtype) for s in arrays]
    bufs = arrays + fresh
    order = [] if after is None else [after]

    def body(*refs):
        srcs, lands = refs[:n], (refs[:n] if inplace else refs[n:2 * n])
        first_out = len(bufs) + len(order)
        sems = refs[first_out:first_out + 2 * len(groups)]
        token = refs[-1]
        k = 0
        for gi, size in enumerate(sizes):
            for i in range(size):
                copies = pattern(srcs[k], lands[k])
                for j, (s, d, peer, _) in enumerate(copies):
                    pltpu.make_async_remote_copy(
                        src_ref=s, dst_ref=d, send_sem=sems[2 * gi].at[i * len(copies) + j],
                        recv_sem=sems[2 * gi + 1].at[i * len(copies) + j], device_id=peer, device_id_type=MESH).start()
                k += 1
        token[...] = jnp.zeros_like(token)

    per_array = {_gather_pattern: N_PEER, _alltoall_pattern: N_PEER, _chipcast_pattern: 4, _forward_pattern: 3,
                 _sibling_pattern: 4, _chip_pattern: 3}[pattern]
    sem_shapes = [pltpu.SemaphoreType.DMA((size * per_array,)) for size in sizes for _ in range(2)]
    res = pl.pallas_call(
        body, name=name,
        in_specs=[HBM_SPEC] * len(bufs) + [pl.BlockSpec(memory_space=pl.ANY)] * len(order),
        out_specs=[SEM_SPEC] * len(sem_shapes) + [HBM_SPEC] * len(bufs) + [pl.BlockSpec(memory_space=pltpu.VMEM)],
        out_shape=sem_shapes + [pltpu.HBM(s.shape, s.dtype) for s in bufs] + [jax.ShapeDtypeStruct((8, DH), F32)],
        input_output_aliases={i: len(sem_shapes) + i for i in range(len(bufs))},
        compiler_params=pltpu.CompilerParams(has_side_effects=EFFECT),
    )(*[_hbm(s) for s in bufs], *order)
    nsem = len(sem_shapes)
    thru = list(res[nsem:nsem + len(bufs)])
    out, k = [], 0
    for gi, size in enumerate(sizes):
        srcs_out = [] if inplace else thru[k:k + size]
        lands_out = thru[k:k + size] if inplace else thru[n + k:n + k + size]
        out.append((res[2 * gi], res[2 * gi + 1], srcs_out, lands_out))
        k += size
    return out, res[-1]


def _split_wait(name, group, pattern, after):
    send_sems, recv_sems, arrays, lands = group
    n = len(lands)
    inplace = not arrays
    bufs = list(arrays) + list(lands)
    fill_own = pattern in (_gather_pattern, _chipcast_pattern, _alltoall_pattern)
    own_of = (lambda a: a.at[_slot_of(_place())]) if pattern is _alltoall_pattern else (lambda a: a)
    own_shape = (lambda a: a.shape[1:]) if pattern is _alltoall_pattern else (lambda a: a.shape)

    def body(*refs):
        srcs, lands_ = refs[:n], (refs[:n] if inplace else refs[n:2 * n])
        s_sems, r_sems = refs[len(bufs)], refs[len(bufs) + 1]
        own = []
        if fill_own:
            stage, local_sems = refs[-n - 1:-1], refs[-1]
            load = [pltpu.make_async_copy(own_of(srcs[i]), stage[i], local_sems.at[i]) for i in range(n)]
            for cp in load:
                cp.start()
            for cp in load:
                cp.wait()
            own = [pltpu.make_async_copy(stage[i], lands_[i].at[_slot_of(_place())], local_sems.at[i])
                   for i in range(n)]
            for cp in own:
                cp.start()
        for i in range(n):
            copies = pattern(srcs[i], lands_[i])
            for j, (s, _, peer, filled) in enumerate(copies):
                cp = pltpu.make_async_remote_copy(
                    src_ref=s, dst_ref=filled, send_sem=s_sems.at[i * len(copies) + j],
                    recv_sem=r_sems.at[i * len(copies) + j], device_id=peer, device_id_type=MESH)
                cp.wait_send()
                cp.wait_recv()
        for cp in own:
            cp.wait()

    res = pl.pallas_call(
        body, name=name,
        in_specs=[HBM_SPEC] * len(bufs) + [SEM_SPEC, SEM_SPEC, pl.BlockSpec(memory_space=pl.ANY)],
        out_specs=[HBM_SPEC] * len(bufs),
        out_shape=[pltpu.HBM(s.shape, s.dtype) for s in bufs],
        scratch_shapes=([pltpu.VMEM(own_shape(a), a.dtype) for a in arrays] + [pltpu.SemaphoreType.DMA((n,))])
        if fill_own else [],
        input_output_aliases={i: i for i in range(len(bufs))},
        compiler_params=pltpu.CompilerParams(has_side_effects=EFFECT),
    )(*bufs, send_sems, recv_sems, after)
    return ([], list(res)) if inplace else (list(res[:n]), list(res[n:]))


def _pair_sum(name, grads, recv, core):
    _, _, r, c = grads.shape
    tr = r if r * c <= (1 << 20) or r % 16 else _div_tile(r, 512, 16)

    def body(core_ref, g_ref, r_ref, o_ref):
        o_ref[...] = (g_ref[0].astype(F32) + r_ref[...].astype(F32)).astype(o_ref.dtype)

    return pl.pallas_call(
        body, name=name,
        grid_spec=pltpu.PrefetchScalarGridSpec(
            num_scalar_prefetch=1, grid=(4, r // tr),
            in_specs=[pl.BlockSpec((1, 1, tr, c), lambda s, i, cr: (s, cr[0], i, 0)),
                      pl.BlockSpec((1, tr, c), lambda s, i, cr: (s, i, 0))],
            out_specs=pl.BlockSpec((1, tr, c), lambda s, i, cr: (s, i, 0))),
        out_shape=jax.ShapeDtypeStruct(recv.shape, recv.dtype),
        compiler_params=_cparams(("parallel", "parallel")),
    )(core, grads, recv)


def _adamw_math(w, g, m, v):
    m = ADAM_B1 * m + (1.0 - ADAM_B1) * g
    v = ADAM_B2 * v + (1.0 - ADAM_B2) * (g * g)
    m_hat = m / (1.0 - ADAM_B1 ** ADAM_STEP)
    v_hat = v / (1.0 - ADAM_B2 ** ADAM_STEP)
    delta = -ADAM_LR * (m_hat / (jnp.sqrt(v_hat) + ADAM_EPS) + ADAM_WD * w)
    return delta, m, v


SMALL_LEAVES = (
    ("mix_pre_norm", 0, 1, 0, D, (1, D)), ("mix_post_norm", 1, 1, 0, D, (1, D)),
    ("ffn_pre_norm", 2, 1, 0, D, (1, D)), ("ffn_post_norm", 3, 1, 0, D, (1, D)),
    ("meta_tokens", 8, N_META, 0, DH, (N_META, DH)),
    ("conv_qkv", 24, KQ, 0, 3 * GW // N_DEV, (1, KQ, 3 * GW // N_DEV)),
    ("conv_sc", 32, KS, 0, SW // N_DEV, (1, KS, SW // N_DEV)),
    ("a_log", 40, 1, 0, HEADS, (1, HEADS)), ("dt_bias", 40, 1, DH, HEADS, (1, HEADS)),
    ("gdn_norm", 40, 1, 2 * DH, DH, (1, DH)),
)
SMALL_LOSS_AT = (40, 3 * DH)


def _adamw_small(slabs, w, m, v):
    nslot = slabs.shape[0]
    nleaf = len(SMALL_LEAVES)

    def body(p_ref, w_ref, m_ref, v_ref, *refs):
        outs, loss_ref, scr = refs[:4 * nleaf], refs[4 * nleaf], refs[4 * nleaf + 1]
        g = p_ref[0]
        for s in range(1, nslot):
            g = g + p_ref[s]
        d, mn, vn = _adamw_math(w_ref[...], g, m_ref[...], v_ref[...])
        for kind, val in enumerate((g, d, mn, vn)):
            scr[kind] = val
        for kind in range(4):
            for li, (_, r0, nr, c0, nc, shape) in enumerate(SMALL_LEAVES):
                val = scr[kind, pl.ds(r0, nr), pl.ds(c0, nc)]
                o = outs[kind * nleaf + li]
                if len(shape) == 3:
                    o[0] = val
                else:
                    o[...] = val
        loss_ref[...] = scr[0, pl.ds(SMALL_LOSS_AT[0], 1), pl.ds(SMALL_LOSS_AT[1], 1)]

    whole = lambda a: pl.BlockSpec(a.shape, lambda: (0,) * a.ndim)
    out_shapes = [jax.ShapeDtypeStruct(shape, F32) for _ in range(4) for (*_, shape) in SMALL_LEAVES]
    out_shapes.append(jax.ShapeDtypeStruct((1, 1), F32))
    res = pl.pallas_call(
        body, name="adamw_small",
        in_specs=[whole(slabs), whole(w), whole(m), whole(v)],
        out_specs=[pl.BlockSpec(s.shape, lambda n=len(s.shape): (0,) * n) for s in out_shapes],
        out_shape=out_shapes,
        scratch_shapes=[pltpu.VMEM((4,) + w.shape, F32)],
        compiler_params=pltpu.CompilerParams(vmem_limit_bytes=VMEM_LIMIT),
    )(slabs, w, m, v)
    kinds = [{nm: res[kind * nleaf + li] for li, (nm, *_) in enumerate(SMALL_LEAVES)} for kind in range(4)]
    return kinds, res[-1]


def _adamw_reduced(name, parts, chip, recv, w, m, v, after=None):
    _, r, c = parts.shape
    tr = r if r * c <= (1 << 20) or r % 16 else _div_tile(r, 512, 16)
    order = [] if after is None else [after]

    def body(chip_ref, p_ref, r_ref, w_ref, m_ref, v_ref, *rest):
        g_out, d_out, m_out, v_out = rest[len(order):]
        g = p_ref[0].astype(F32)
        for s in range(3):
            g = g + r_ref[s].astype(F32)
        d, mn, vn = _adamw_math(w_ref[...], g, m_ref[...], v_ref[...])
        g_out[...] = g
        d_out[...] = d
        m_out[...] = mn
        v_out[...] = vn

    blk = pl.BlockSpec((tr, c), lambda i, ch: (i, 0))
    return pl.pallas_call(
        body, name=name,
        grid_spec=pltpu.PrefetchScalarGridSpec(
            num_scalar_prefetch=1, grid=(r // tr,),
            in_specs=[pl.BlockSpec((1, tr, c), lambda i, ch: (ch[0], i, 0)),
                      pl.BlockSpec((3, tr, c), lambda i, ch: (0, i, 0)), blk, blk, blk]
            + [pl.BlockSpec((8, DH), lambda i, ch: (0, 0))] * len(order),
            out_specs=[blk] * 4),
        out_shape=[jax.ShapeDtypeStruct((r, c), F32)] * 4,
        compiler_params=_cparams(("parallel",)),
    )(chip, parts, recv, w, m, v, *order)


SMALL_ROWS, SMALL_COLS = 48, 3 * GW


def _pack_small(gains4, meta, conv_qkv, conv_sc, a_log, dt_bias, gdn_norm, extra=None):
    def tile(a):
        return jnp.pad(a, ((0, -a.shape[0] % 8), (0, SMALL_COLS - a.shape[1])))
    scal = jnp.concatenate([jnp.pad(a_log, ((0, 0), (0, DH - HEADS))), jnp.pad(dt_bias, ((0, 0), (0, DH - HEADS))),
                            gdn_norm] + ([] if extra is None else [extra]), axis=1)
    gains = sum(jnp.pad(g, ((i, 7 - i), (0, SMALL_COLS - g.shape[1]))) for i, g in enumerate(gains4))
    parts = [gains, tile(meta), tile(conv_qkv), tile(conv_sc), tile(scal)]
    slab = jnp.concatenate(parts, axis=0)
    assert slab.shape == (SMALL_ROWS, SMALL_COLS)
    return slab


def kernel(x, meta_tokens, mix_pre_norm, mix_post_norm, ffn_pre_norm, ffn_post_norm, w_in, conv_qkv, a_log, dt_bias, gdn_norm, conv_sc, w_out, w_gate, w_up, w_down, loss_target, m_meta_tokens, m_mix_pre_norm, m_mix_post_norm, m_ffn_pre_norm, m_ffn_post_norm, m_w_in, m_conv_qkv, m_a_log, m_dt_bias, m_gdn_norm, m_conv_sc, m_w_out, m_w_gate, m_w_up, m_w_down, v_meta_tokens, v_mix_pre_norm, v_mix_post_norm, v_ffn_pre_norm, v_ffn_post_norm, v_w_in, v_conv_qkv, v_a_log, v_dt_bias, v_gdn_norm, v_conv_sc, v_w_out, v_w_gate, v_w_up, v_w_down):
    bsz, seq, _ = x.shape
    lp = -(-(N_META + seq) // CH) * CH
    padf = lp - N_META - seq
    first_real = padf + N_META
    rows = bsz * lp
    nchunk = lp // CH
    tr = _div_tile(lp, 528)
    tm = _div_tile(rows, 1056)
    tm_big = _div_tile(rows, 2112)
    cx, cy, cc = _place()

    small_shard = jnp.concatenate([
        jnp.pad(meta_tokens, ((0, 0), (0, 256 - DH))),
        jnp.pad(conv_qkv[0], ((0, 0), (0, 256 - 3 * GW // N_DEV))),
        jnp.pad(conv_sc[0], ((0, 5), (0, 256 - SW // N_DEV)))], axis=0)
    g_in, g_small = _all_gather("gather_w_in", [w_in[0].T.astype(BF16), small_shard])
    my_rest = [w_out[0].astype(BF16), w_gate[0].T.astype(BF16), w_up[0].T.astype(BF16), w_down[0].astype(BF16)]
    (grp_rest,), token = _split_start("gather_rest_start", [my_rest], _chipcast_pattern, lambda s: (N_DEV,) + s,
                                      after=g_in)
    after_start = token[0:1, 0:1]

    def cols_full(g):
        return jnp.transpose(g, (1, 0, 2)).reshape(g.shape[1], -1)

    win_t = g_in.reshape(IN_W, D)
    win_t = jnp.concatenate([win_t[:BA0 + 2 * HEADS], jnp.zeros((BA_W - 2 * HEADS, D), BF16),
                             win_t[BA0 + 2 * HEADS:]], axis=0)
    meta_full = cols_full(g_small[:, 0:16, :DH])
    convq_full = cols_full(g_small[:, 16:20, :3 * GW // N_DEV])
    convs_full = cols_full(g_small[:, 20:23, :SW // N_DEV])
    alog_v = jnp.pad(a_log, ((0, 0), (HEADS, DH - 2 * HEADS)))
    dtb_v = jnp.pad(dt_bias, ((0, 0), (HEADS, DH - 2 * HEADS)))

    head_rows = jnp.concatenate([jnp.zeros((padf, D), F32), meta_full], axis=0)
    h0, u1 = _pre_norm(x, head_rows, mix_pre_norm + after_start, bsz, lp, tr)
    (proj,) = _mm("proj_in", [(u1, win_t)], "nt", tm_big, 768, D, out_dtypes=(BF16,))
    (ba,) = _mm("proj_gate_logits", [(u1, win_t[BA0:SX0])], "nt", tm, BA_W, D)
    qkv = _gdn_prep(proj, convq_full, bsz, lp, padf)
    gb, gcs = _gates(ba, alog_v, dtb_v, bsz, lp, padf)

    def lane_bcast(cols):
        t = jnp.transpose(cols.reshape(bsz, lp, HEADS), (0, 2, 1)).reshape(bsz * HEADS, lp, 1)
        return jnp.broadcast_to(t, (bsz * HEADS, lp, DH))

    bb = lane_bcast(gb[:, 0:HEADS])
    gcb = lane_bcast(gcs[:, HEADS:2 * HEADS])
    gct = jnp.transpose(gcs[:, HEADS:2 * HEADS].reshape(bsz, nchunk, CH, HEADS), (0, 3, 1, 2)).reshape(
        bsz * HEADS, nchunk, 1, CH)
    u_loc, l1, l2, l3, w_t, k_dec, tinv_all, g_last = _gdn_local_fwd(qkv, gcb, bb, gct, bsz, lp)
    o, v_new, states = _gdn_scan_fwd(u_loc, l1, l2, g_last, bsz, lp)
    o, v_new = o.reshape(rows, GW), v_new.reshape(rows, GW)
    _, l_rest = _split_wait("gather_rest_arrive", grp_rest, _chipcast_pattern, o)
    (grp_out, grp_ffn, grp_down), token = _split_start(
        "gather_rest_forward", [l_rest[0:1], l_rest[1:3], l_rest[3:4]], _forward_pattern, None)
    mix_g, mix_s = _mix_heads(proj, o, gdn_norm + token[0:1, :], convs_full, bsz, lp, padf)
    mixin = jnp.concatenate([mix_g, mix_s], axis=1)
    _, (l_out,) = _split_wait("gather_w_out_wait", grp_out, _forward_pattern, mixin)
    wout = l_out.reshape(D, D)
    (mix,) = _mm("proj_out", [(mixin, wout)], "nn", tm, 512, D)
    h1, u2 = _mid_norms(h0, mix, mix_post_norm, ffn_pre_norm, bsz, lp, tr)
    _, (l_gate, l_up) = _split_wait("gather_ffn_wait", grp_ffn, _forward_pattern, u2)
    wgate_t, wup_t = l_gate.reshape(FF, D), l_up.reshape(FF, D)
    gate, up, act = _ffn_up(u2, wgate_t, wup_t, _div_tile(rows, 528), 1408)
    _, (l_down,) = _split_wait("gather_down_wait", grp_down, _forward_pattern, act)
    wdown = l_down.reshape(FF, D)
    (ffn,) = _mm("ffn_down", [(act, wdown)], "nn", tm, 512, 1408)

    dy, dffn, d_g4, loss_part = _loss_head(h1, ffn, loss_target, ffn_post_norm, bsz, lp, tr, first_real)
    (dwdown,) = _mm("dw_down", [(act, dffn)], "tn", 1408, 512, tm, out_dtypes=(BF16,))

    def swiglu_bwd(da, g, u):
        g, u = g.astype(F32), u.astype(F32)
        return da * u * _dsilu(g), da * _silu(g)

    dgate, dup = _mm("d_act", [(dffn, wdown)], "nt", tm, 1408, D, out_dtypes=(BF16, BF16),
                     epilogue=swiglu_bwd, extras=(gate, up))
    (dwgate_t,) = _mm("dw_gate", [(dgate, u2)], "tn", 1408, D, tm, out_dtypes=(BF16,))
    (dwup_t,) = _mm("dw_up", [(dup, u2)], "tn", 1408, D, tm, out_dtypes=(BF16,))
    (du2,) = _mm("d_u2", [(dgate, wgate_t), (dup, wup_t)], "nn", tm, D, 1408)
    dh1, dmix, d_g3, d_g2 = _mid_norms_bwd(h1, mix, du2, dy, ffn_pre_norm, mix_post_norm, bsz, lp, tr)
    (dwout,) = _mm("dw_out", [(mixin, dmix)], "tn", D, 512, tm, out_dtypes=(BF16,))

    def row_blocks(g):
        return g.reshape(4, 2, -1, g.shape[1])

    core = jnp.reshape(cc, (1,)).astype(jnp.int32)
    chip = jnp.reshape(2 * cx + cy, (1,)).astype(jnp.int32)

    def reduce_start(tag, grads):
        (grp,), tok = _split_start("rs_sibling_start_" + tag, [grads], _sibling_pattern, lambda s: s[:1] + s[2:])
        return grp, tok

    def reduce_mid(tag, grp, nms, after):
        mine, lands = _split_wait("rs_sibling_wait_" + tag, grp, _sibling_pattern, after)
        parts = [_pair_sum("pair_sum_" + nm, g, r, core) for nm, g, r in zip(nms, mine, lands)]
        (grp2,), tok = _split_start("rs_chip_start_" + tag, [parts], _chip_pattern, lambda s: (3,) + s[1:])
        return grp2, tok

    def reduce_end(tag, grp2, after):
        return _split_wait("rs_chip_wait_" + tag, grp2, _chip_pattern, after)

    names_ffn = ["w_down", "w_gate", "w_up", "w_out"]
    rs_ffn, tok = reduce_start("ffn", [row_blocks(dwdown), row_blocks(dwgate_t), row_blocks(dwup_t), row_blocks(dwout)])
    (dmixin,) = _mm("d_mixin", [(dmix, wout)], "nt", tm, 512, D, after=tok)
    d_o, dz, dsx, dsb, dsc, d_gn, d_convs = _mix_heads_bwd(proj, o, gdn_norm, convs_full, dmixin, bsz, lp, padf)
    rs_ffn, tok = reduce_mid("ffn", rs_ffn, names_ffn, d_o)
    dv_new, dstates = _gdn_scan_bwd(d_o, l3, w_t, k_dec, g_last + tok[0, 0], bsz, lp)
    dq, dk, dv, dg_b, dbeta_b = _gdn_local_bwd(qkv, gcb, bb, gct, tinv_all, states, dstates, v_new,
                                               dv_new.reshape(rows, GW), d_o, bsz, lp)
    dpqkv, d_convq = _gdn_prep_bwd(proj, convq_full, dq, dk, dv, bsz, lp, padf)
    dba, d_scal = _gates_bwd(ba, alog_v, dtb_v, dbeta_b, dg_b, gb, bsz, lp, padf)
    dproj = jnp.concatenate([dpqkv, dz, dba, dsx, dsb, dsc], axis=1)
    (dwin_t,) = _mm("dw_in", [(dproj, u1)], "tn", 1280, D, tm, out_dtypes=(BF16,))
    parts_ffn, recv_ffn = reduce_end("ffn", rs_ffn, dwin_t)
    dwin_t = jnp.concatenate([dwin_t[:BA0 + 2 * HEADS], dwin_t[SX0:]], axis=0)
    rs_in, tok = reduce_start("in", [row_blocks(dwin_t)])

    big_w = {"w_in": (w_in, m_w_in, v_w_in), "w_out": (w_out, m_w_out, v_w_out), "w_gate": (w_gate, m_w_gate, v_w_gate),
             "w_up": (w_up, m_w_up, v_w_up), "w_down": (w_down, m_w_down, v_w_down)}

    def update(nm, parts, recv, after=None):
        w, m, v = big_w[nm]
        if nm in ("w_out", "w_down"):
            return _adamw_reduced("adamw_" + nm, parts, chip, recv, w[0], m[0], v[0], after)
        res = _adamw_reduced("adamw_" + nm, parts, chip, recv, w[0].T, m[0].T, v[0].T, after)
        return [t.T for t in res]

    big_out = {nm: update(nm, p, r, tok) for nm, p, r in zip(names_ffn, parts_ffn, recv_ffn)}
    ffn_done = sum(big_out[nm][1][:1, :1] for nm in names_ffn)
    rs_in, tok = reduce_mid("in", rs_in, ["w_in"], ffn_done)
    (du1,) = _mm("d_u1", [(dproj, win_t)], "nn", tm_big, D, 768, after=tok)
    grad_x, d_g1, d_meta = _pre_norm_bwd(h0, du1, dh1, mix_pre_norm, bsz, lp, seq, padf)

    d_alog = d_scal[0:1, HEADS:2 * HEADS]
    d_dtb = d_scal[1:2, HEADS:2 * HEADS]
    cq_w, cs_w = 3 * GW // N_DEV, SW // N_DEV
    slab = jnp.stack([
        _pack_small([d_g1, d_g2, d_g3, d_g4], d_meta[:, p * DH:(p + 1) * DH], d_convq[:, p * cq_w:(p + 1) * cq_w],
                    d_convs[:, p * cs_w:(p + 1) * cs_w], d_alog, d_dtb, d_gn, loss_part) for p in range(N_DEV)])
    (grp_small,), _ = _split_start("small_grads_start", [[slab]], _alltoall_pattern, lambda s: s)
    _, (slabs_local,) = _split_wait("small_grads_wait", grp_small, _alltoall_pattern, grad_x)
    w_small = _pack_small([mix_pre_norm, mix_post_norm, ffn_pre_norm, ffn_post_norm], meta_tokens, conv_qkv[0],
                          conv_sc[0], a_log, dt_bias, gdn_norm)
    m_small = _pack_small([m_mix_pre_norm, m_mix_post_norm, m_ffn_pre_norm, m_ffn_post_norm], m_meta_tokens,
                          m_conv_qkv[0], m_conv_sc[0], m_a_log, m_dt_bias, m_gdn_norm)
    v_small = _pack_small([v_mix_pre_norm, v_mix_post_norm, v_ffn_pre_norm, v_ffn_post_norm], v_meta_tokens,
                          v_conv_qkv[0], v_conv_sc[0], v_a_log, v_dt_bias, v_gdn_norm)
    small_out, loss = _adamw_small(slabs_local, w_small, m_small, v_small)
    parts_in, recv_in = reduce_end("in", rs_in, small_out[1]["gdn_norm"])
    big_out["w_in"] = update("w_in", parts_in[0], recv_in[0])
    names = ["w_in", "w_out", "w_gate", "w_up", "w_down"]

    order = ["meta_tokens", "mix_pre_norm", "mix_post_norm", "ffn_pre_norm", "ffn_post_norm", "w_in", "conv_qkv",
             "a_log", "dt_bias", "gdn_norm", "conv_sc", "w_out", "w_gate", "w_up", "w_down"]
    outs = [loss[0, 0], grad_x]
    for kind in range(4):
        leaves = dict(small_out[kind])
        for nm in names:
            leaves[nm] = big_out[nm][kind][None]
        outs += [leaves[nm] for nm in order]
    return tuple(outs)
```

```python
import functools

import jax
import jax.numpy as jnp
from jax import lax
from jax.experimental import pallas as pl
from jax.experimental.pallas import tpu as pltpu

F32 = jnp.float32
BF16 = jnp.bfloat16
HIGHEST = lax.Precision.HIGHEST
MESH = pl.DeviceIdType.MESH

D = 1024
N_META = 16
HEADS = 4
DH = 128
GW = HEADS * DH
SW = D - GW
FF = 2816
CH = 64
KQ = 4
KS = 3
EPS = 1e-6
IN_W = 3 * GW + GW + 2 * HEADS + 3 * SW
Q0, K0, V0, Z0, BA0, SX0, SB0, SC0, PW = 0, 512, 1024, 1536, 2048, 2304, 2816, 3328, 3840
BA_W = SX0 - BA0
N_DEV = 8
VMEM_LIMIT = 56 * 1024 * 1024

ADAM_LR, ADAM_B1, ADAM_B2, ADAM_EPS, ADAM_WD, ADAM_STEP = 0.001, 0.9, 0.999, 1e-08, 0.01, 10


def _cparams(sem):
    return pltpu.CompilerParams(dimension_semantics=sem, vmem_limit_bytes=VMEM_LIMIT)


def _div_tile(n, target, mult=16):
    best = None
    for t in range(mult, min(n, target) + 1, mult):
        if n % t == 0:
            best = t
    assert best is not None, (n, target)
    return best


def _silu(x):
    return x * jax.nn.sigmoid(x)


def _dsilu(x):
    s = jax.nn.sigmoid(x)
    return s * (1.0 + x * (1.0 - s))


def _bdot(a, b, dims):
    return lax.dot_general(a.astype(BF16), b.astype(BF16), (dims, ((), ())), preferred_element_type=F32)


NN = ((1,), (0,))
NT = ((1,), (1,))
TN = ((0,), (0,))


def _fdot(a, b, dims=NN):
    return lax.dot_general(a, b, (dims, ((), ())), precision=HIGHEST, preferred_element_type=F32)


def _mm(name, pairs, mode, tm, tn, tk, out_dtypes=(F32,), epilogue=None, extras=(), after=None):
    a0, b0 = pairs[0]
    if mode == "nn":
        (m, k), n = a0.shape, b0.shape[1]
    elif mode == "nt":
        (m, k), n = a0.shape, b0.shape[0]
    else:
        (k, m), n = a0.shape, b0.shape[1]
    assert m % tm == 0 and n % tn == 0 and k % tk == 0, (name, m, n, k, tm, tn, tk)
    nk = k // tk
    npair = len(pairs)
    nex = len(extras)
    nout = len(out_dtypes)
    ntok = 0 if after is None else 1
    dims = {"nn": NN, "nt": NT, "tn": TN}[mode]

    def body(*refs):
        ab = refs[:2 * npair]
        ex = refs[2 * npair:2 * npair + nex]
        outs = refs[2 * npair + nex + ntok:2 * npair + nex + ntok + nout]
        acc_ref = refs[-1]
        kk = pl.program_id(2)

        part = None
        for p in range(npair):
            d = lax.dot_general(ab[2 * p][...], ab[2 * p + 1][...], (dims, ((), ())), preferred_element_type=F32)
            part = d if part is None else part + d

        def finish(acc):
            res = epilogue(acc, *[e[...] for e in ex]) if epilogue is not None else (acc,)
            for o, r in zip(outs, res):
                o[...] = r.astype(o.dtype)

        if nk == 1:
            finish(part)
        else:
            @pl.when(kk == 0)
            def _():
                acc_ref[...] = part

            @pl.when(kk > 0)
            def _():
                acc_ref[...] += part

            @pl.when(kk == nk - 1)
            def _():
                finish(acc_ref[...])

    if mode == "nn":
        a_spec = pl.BlockSpec((tm, tk), lambda i, j, q: (i, q))
        b_spec = pl.BlockSpec((tk, tn), lambda i, j, q: (q, j))
    elif mode == "nt":
        a_spec = pl.BlockSpec((tm, tk), lambda i, j, q: (i, q))
        b_spec = pl.BlockSpec((tn, tk), lambda i, j, q: (j, q))
    else:
        a_spec = pl.BlockSpec((tk, tm), lambda i, j, q: (q, i))
        b_spec = pl.BlockSpec((tk, tn), lambda i, j, q: (q, j))
    o_spec = pl.BlockSpec((tm, tn), lambda i, j, q: (i, j))
    flat = [t for pr in pairs for t in pr]
    res = pl.pallas_call(
        body, name=name,
        grid=(m // tm, n // tn, nk),
        in_specs=[a_spec, b_spec] * npair + [o_spec] * nex + [pl.BlockSpec((8, DH), lambda i, j, q: (0, 0))] * ntok,
        out_specs=[o_spec] * nout,
        out_shape=[jax.ShapeDtypeStruct((m, n), dt) for dt in out_dtypes],
        scratch_shapes=[pltpu.VMEM((tm, tn) if nk > 1 else (8, DH), F32)],
        compiler_params=_cparams(("parallel", "parallel", "arbitrary")),
    )(*flat, *extras, *([] if after is None else [after]))
    return res


def _ffn_up(u2, wg, wu, tm, tn):
    m, k = u2.shape
    n = wg.shape[0]

    def body(a_ref, g_ref, u_ref, gate_ref, up_ref, act_ref):
        a = a_ref[...]
        g = lax.dot_general(a, g_ref[...], (NT, ((), ())), preferred_element_type=F32)
        u = lax.dot_general(a, u_ref[...], (NT, ((), ())), preferred_element_type=F32)
        gate_ref[...] = g.astype(BF16)
        up_ref[...] = u.astype(BF16)
        act_ref[...] = (_silu(g) * u).astype(BF16)

    o_spec = pl.BlockSpec((tm, tn), lambda j, i: (i, j))
    w_spec = pl.BlockSpec((tn, k), lambda j, i: (j, 0))
    return pl.pallas_call(
        body, name="ffn_up", grid=(n // tn, m // tm),
        in_specs=[pl.BlockSpec((tm, k), lambda j, i: (i, 0)), w_spec, w_spec],
        out_specs=[o_spec, o_spec, o_spec],
        out_shape=[jax.ShapeDtypeStruct((m, n), BF16)] * 3,
        compiler_params=_cparams(("parallel", "parallel")),
    )(u2, wg, wu)


def _rms(x, gain):
    r = lax.rsqrt(jnp.mean(x * x, axis=-1, keepdims=True) + EPS)
    return x * r * gain, r


def _rms_bwd(x, gain, dy):
    r = lax.rsqrt(jnp.mean(x * x, axis=-1, keepdims=True) + EPS)
    dyw = dy * gain
    dx = r * dyw - x * (r * r * r) * jnp.mean(dyw * x, axis=-1, keepdims=True)
    return dx, dy * x * r


def _row_specs(tr, width, nb):
    return pl.BlockSpec((tr, width), lambda b, j: (b * nb + j, 0))


def _vec_spec(width):
    return pl.BlockSpec((1, width), lambda b, j: (0, 0))


def _first_step(b, j):
    return jnp.logical_and(b == 0, j == 0)


def _pre_norm(x, head_rows, gain, bsz, lp, tr):
    nb = lp // tr
    first_real = head_rows.shape[0]
    assert first_real % 8 == 0 and first_real < tr
    x_spec = pl.BlockSpec((pl.Squeezed(), pl.Element(tr), pl.Element(D)),
                          lambda b, j: (b, pl.multiple_of(jnp.maximum(j * tr - first_real, 0), 8), 0))

    def body(x_ref, hd_ref, g_ref, h_ref, u_ref):
        j = pl.program_id(1)
        xt = x_ref[...]
        h = jnp.where(j == 0, jnp.concatenate([hd_ref[...], xt[:tr - first_real]], axis=0), xt)
        h_ref[...] = h
        u_ref[...] = _rms(h, g_ref[...])[0].astype(BF16)

    rs = _row_specs(tr, D, nb)
    return pl.pallas_call(
        body, name="mix_pre_norm", grid=(bsz, nb),
        in_specs=[x_spec, pl.BlockSpec((first_real, D), lambda b, j: (0, 0)), _vec_spec(D)],
        out_specs=[rs, rs],
        out_shape=[jax.ShapeDtypeStruct((bsz * lp, D), F32), jax.ShapeDtypeStruct((bsz * lp, D), BF16)],
        compiler_params=_cparams(("parallel", "parallel")),
    )(x, head_rows, gain)


def _mid_norms(h0, mix, g_post, g_pre, bsz, lp, tr):
    nb = lp // tr

    def body(h_ref, m_ref, gp_ref, gq_ref, h1_ref, u2_ref):
        h1 = h_ref[...] + _rms(m_ref[...], gp_ref[...])[0]
        h1_ref[...] = h1
        u2_ref[...] = _rms(h1, gq_ref[...])[0].astype(BF16)

    rs = _row_specs(tr, D, nb)
    return pl.pallas_call(
        body, name="mid_norms", grid=(bsz, nb),
        in_specs=[rs, rs, _vec_spec(D), _vec_spec(D)],
        out_specs=[rs, rs],
        out_shape=[jax.ShapeDtypeStruct(h0.shape, F32), jax.ShapeDtypeStruct(h0.shape, BF16)],
        compiler_params=_cparams(("parallel", "parallel")),
    )(h0, mix, g_post, g_pre)


def _loss_head(h1, ffn, tgt, g_post, bsz, lp, tr, first_real):
    nb = lp // tr
    assert first_real % 8 == 0 and first_real < tr
    tgt_spec = pl.BlockSpec((pl.Squeezed(), pl.Element(tr), pl.Element(D)),
                            lambda b, j: (b, pl.multiple_of(jnp.maximum(j * tr - first_real, 0), 8), 0))

    def body(h_ref, f_ref, t_ref, g_ref, dy_ref, dffn_ref, dg_ref, loss_ref):
        b, j = pl.program_id(0), pl.program_id(1)
        f = f_ref[...]
        g = g_ref[...]
        y = h_ref[...] + _rms(f, g)[0]
        t = j * tr + lax.broadcasted_iota(jnp.int32, (tr, 1), 0)
        tg = t_ref[...]
        tg = jnp.where(j == 0, jnp.concatenate([tg[tr - first_real:], tg[:tr - first_real]], axis=0), tg)
        err = jnp.where(t >= first_real, y - tg, 0.0)
        dy = err * (1.0 / D)
        dy_ref[...] = dy
        dx, dgc = _rms_bwd(f, g, dy)
        dffn_ref[...] = dx.astype(BF16)
        lsum = jnp.sum(jnp.sum(err * err, axis=0, keepdims=True), axis=1, keepdims=True) * (0.5 / D)

        @pl.when(_first_step(b, j))
        def _():
            dg_ref[...] = jnp.zeros_like(dg_ref)
            loss_ref[...] = jnp.zeros_like(loss_ref)

        dg_ref[...] += jnp.sum(dgc, axis=0, keepdims=True)
        loss_ref[...] += jnp.broadcast_to(lsum, loss_ref.shape)

    rs = _row_specs(tr, D, nb)
    return pl.pallas_call(
        body, name="loss_head", grid=(bsz, nb),
        in_specs=[rs, rs, tgt_spec, _vec_spec(D)],
        out_specs=[rs, rs, _vec_spec(D), _vec_spec(DH)],
        out_shape=[jax.ShapeDtypeStruct(h1.shape, F32), jax.ShapeDtypeStruct(h1.shape, BF16),
                   jax.ShapeDtypeStruct((1, D), F32), jax.ShapeDtypeStruct((1, DH), F32)],
        compiler_params=_cparams(("arbitrary", "arbitrary")),
    )(h1, ffn, tgt, g_post)


def _mid_norms_bwd(h1, mix, du2, dy, g_pre, g_post, bsz, lp, tr):
    nb = lp // tr

    def body(h_ref, m_ref, du_ref, dy_ref, gq_ref, gp_ref, dh1_ref, dmix_ref, dgq_ref, dgp_ref):
        b, j = pl.program_id(0), pl.program_id(1)
        dx, dgq = _rms_bwd(h_ref[...], gq_ref[...], du_ref[...])
        dh1 = dy_ref[...] + dx
        dh1_ref[...] = dh1
        dm, dgp = _rms_bwd(m_ref[...], gp_ref[...], dh1)
        dmix_ref[...] = dm.astype(BF16)

        @pl.when(_first_step(b, j))
        def _():
            dgq_ref[...] = jnp.zeros_like(dgq_ref)
            dgp_ref[...] = jnp.zeros_like(dgp_ref)

        dgq_ref[...] += jnp.sum(dgq, axis=0, keepdims=True)
        dgp_ref[...] += jnp.sum(dgp, axis=0, keepdims=True)

    rs = _row_specs(tr, D, nb)
    return pl.pallas_call(
        body, name="mid_norms_bwd", grid=(bsz, nb),
        in_specs=[rs, rs, rs, rs, _vec_spec(D), _vec_spec(D)],
        out_specs=[rs, rs, _vec_spec(D), _vec_spec(D)],
        out_shape=[jax.ShapeDtypeStruct(h1.shape, F32), jax.ShapeDtypeStruct(h1.shape, BF16),
                   jax.ShapeDtypeStruct((1, D), F32), jax.ShapeDtypeStruct((1, D), F32)],
        compiler_params=_cparams(("arbitrary", "arbitrary")),
    )(h1, mix, du2, dy, g_pre, g_post)


def _pre_norm_bwd(h0, du1, dh1, gain, bsz, lp, seq, padf):
    first_real = lp - seq
    tx = _div_tile(seq, 512)
    assert first_real % 8 == 0 and lp % first_real == 0

    def tokens(h_ref, du_ref, dh1_ref, g_ref, gx_ref, dg_ref):
        b, j = pl.program_id(0), pl.program_id(1)
        dx, dgc = _rms_bwd(h_ref[...], g_ref[...], du_ref[...])
        gx_ref[...] = dh1_ref[...] + dx

        @pl.when(_first_step(b, j))
        def _():
            dg_ref[...] = jnp.zeros_like(dg_ref)

        dg_ref[...] += jnp.sum(dgc, axis=0, keepdims=True)

    window = pl.BlockSpec((pl.Element(tx), pl.Element(D)),
                          lambda b, j: (pl.multiple_of(b * lp + first_real + j * tx, 8), 0))
    grad_x, dg_tokens = pl.pallas_call(
        tokens, name="pre_norm_bwd", grid=(bsz, seq // tx),
        in_specs=[window, window, window, _vec_spec(D)],
        out_specs=[pl.BlockSpec((pl.Squeezed(), tx, D), lambda b, j: (b, j, 0)), _vec_spec(D)],
        out_shape=[jax.ShapeDtypeStruct((bsz, seq, D), F32), jax.ShapeDtypeStruct((1, D), F32)],
        compiler_params=_cparams(("arbitrary", "arbitrary")),
    )(h0, du1, dh1, gain)

    def head(h_ref, du_ref, dh1_ref, g_ref, dgt_ref, dg_ref, dmeta_ref):
        b = pl.program_id(0)
        dx, dgc = _rms_bwd(h_ref[...], g_ref[...], du_ref[...])
        dh0 = dh1_ref[...] + dx

        @pl.when(b == 0)
        def _():
            dg_ref[...] = dgt_ref[...]
            dmeta_ref[...] = jnp.zeros_like(dmeta_ref)

        dg_ref[...] += jnp.sum(dgc, axis=0, keepdims=True)
        dmeta_ref[...] += dh0[padf:padf + N_META, :]

    rows = pl.BlockSpec((first_real, D), lambda b: (b * (lp // first_real), 0))
    vec = pl.BlockSpec((1, D), lambda b: (0, 0))
    dg, dmeta = pl.pallas_call(
        head, name="pre_norm_bwd_head", grid=(bsz,),
        in_specs=[rows, rows, rows, vec, vec],
        out_specs=[vec, pl.BlockSpec((N_META, D), lambda b: (0, 0))],
        out_shape=[jax.ShapeDtypeStruct((1, D), F32), jax.ShapeDtypeStruct((N_META, D), F32)],
        compiler_params=_cparams(("arbitrary",)),
    )(h0, du1, dh1, gain, dg_tokens)
    return grad_x, dg, dmeta


def _shift_down(x, s, lp):
    return x if s == 0 else pltpu.roll(x, s, axis=0)


def _shift_up(x, s, lp):
    return x if s == 0 else pltpu.roll(x, lp - s, axis=0)


def _conv_fwd(x, w, taps, lp):
    y = None
    for i in range(taps):
        term = _shift_down(x, taps - 1 - i, lp) * w[i:i + 1, :]
        y = term if y is None else y + term
    return y


def _conv_bwd(x, w, dy, taps, lp):
    dx = None
    dws = []
    for i in range(taps):
        s = taps - 1 - i
        term = _shift_up(dy, s, lp) * w[i:i + 1, :]
        dx = term if dx is None else dx + term
        dws.append(jnp.sum(dy * _shift_down(x, s, lp), axis=0, keepdims=True))
    return dx, jnp.concatenate(dws, axis=0)


def _slab(lp, col0):
    return pl.BlockSpec((lp, DH), lambda b, s: (b, col0 // DH + s))


def _qkv_act(c, s):
    y = _silu(c)
    r = lax.rsqrt(jnp.sum(y * y, axis=-1, keepdims=True) + EPS)
    scale = jnp.where(s < HEADS, DH ** -0.5, 1.0)
    return y, r, scale


def _gdn_prep(proj, conv_qkv, bsz, lp, padf):
    def body(p_ref, w_ref, o_ref):
        s = pl.program_id(1)
        c = _conv_fwd(p_ref[...].astype(F32), w_ref[...], KQ, lp)
        y, r, scale = _qkv_act(c, s)
        out = jnp.where(s < 2 * HEADS, y * (r * scale), y)
        t = lax.broadcasted_iota(jnp.int32, (lp, 1), 0)
        o_ref[...] = jnp.where(t >= padf, out, 0.0)

    return pl.pallas_call(
        body, name="gdn_prep", grid=(bsz, 3 * HEADS),
        in_specs=[_slab(lp, Q0), pl.BlockSpec((KQ, DH), lambda b, s: (0, s))],
        out_specs=_slab(lp, 0),
        out_shape=jax.ShapeDtypeStruct((bsz * lp, 3 * GW), F32),
        compiler_params=_cparams(("parallel", "parallel")),
    )(proj, conv_qkv)


def _gdn_prep_bwd(proj, conv_qkv, dq, dk, dv, bsz, lp, padf):
    def body(p_ref, w_ref, dq_ref, dk_ref, dv_ref, dp_ref, dw_ref):
        b, s = pl.program_id(0), pl.program_id(1)
        x = p_ref[...].astype(F32)
        w = w_ref[...]
        c = _conv_fwd(x, w, KQ, lp)
        y, r, scale = _qkv_act(c, s)
        d = jnp.where(s < HEADS, dq_ref[...], jnp.where(s < 2 * HEADS, dk_ref[...], dv_ref[...]))
        dn = d * scale
        n = y * r
        dy_norm = r * (dn - n * jnp.sum(dn * n, axis=-1, keepdims=True))
        dyy = jnp.where(s < 2 * HEADS, dy_norm, dn)
        t = lax.broadcasted_iota(jnp.int32, (lp, 1), 0)
        dc = jnp.where(t >= padf, dyy * _dsilu(c), 0.0)
        dx, dw = _conv_bwd(x, w, dc, KQ, lp)
        dp_ref[...] = jnp.where(t >= padf, dx, 0.0).astype(BF16)

        @pl.when(jnp.logical_and(b == 0, s == 0))
        def _():
            dw_ref[...] = jnp.zeros_like(dw_ref)

        col_block = jnp.right_shift(lax.broadcasted_iota(jnp.int32, (KQ, 3 * GW), 1), DH.bit_length() - 1)
        dw_ref[...] += jnp.where(col_block == s, jnp.tile(dw, (1, 3 * HEADS)), 0.0)

    sl = lambda col0: pl.BlockSpec((lp, DH), lambda b, s: (b, col0 // DH + s))
    part = lambda k: pl.BlockSpec((lp, DH), lambda b, s: (b, jnp.clip(s - k * HEADS, 0, HEADS - 1)))
    return pl.pallas_call(
        body, name="gdn_prep_bwd", grid=(bsz, 3 * HEADS),
        in_specs=[sl(Q0), pl.BlockSpec((KQ, DH), lambda b, s: (0, s)), part(0), part(1), part(2)],
        out_specs=[sl(0), pl.BlockSpec((KQ, 3 * GW), lambda b, s: (0, 0))],
        out_shape=[jax.ShapeDtypeStruct((bsz * lp, 3 * GW), BF16), jax.ShapeDtypeStruct((KQ, 3 * GW), F32)],
        compiler_params=_cparams(("arbitrary", "arbitrary")),
    )(proj, conv_qkv, dq, dk, dv)


def _softplus(x):
    e = jnp.exp(-jnp.abs(x))
    one_e = 1.0 + e
    l1p = jnp.where(one_e == 1.0, e, jnp.log(one_e) * (e / (one_e - 1.0)))
    return jnp.maximum(x, 0.0) + l1p


def _tri(n, kind):
    i = lax.broadcasted_iota(jnp.int32, (n, n), 0)
    j = lax.broadcasted_iota(jnp.int32, (n, n), 1)
    return {"incl": i >= j, "strict": i > j, "upper": i <= j}[kind]


def _gates(proj, alog_v, dtb_v, bsz, lp, padf):
    nchunk = lp // CH

    def body(p_ref, al_ref, dt_ref, gb_ref, cs_ref):
        x = p_ref[...].astype(F32)
        lane = lax.broadcasted_iota(jnp.int32, (lp, DH), 1)
        t = lax.broadcasted_iota(jnp.int32, (lp, DH), 0)
        beta = jax.nn.sigmoid(x)
        g = -jnp.exp(al_ref[...]) * _softplus(x + dt_ref[...])
        gb = jnp.where(lane < HEADS, beta, jnp.where(lane < 2 * HEADS, g, 0.0))
        gb_ref[...] = jnp.where(t >= padf, gb, 0.0)
        ltri = _tri(CH, "incl").astype(BF16)
        rest = [gb_ref[pl.ds(n * CH, CH), :] for n in range(nchunk)]
        sums = None
        for _ in range(3):
            piece = [v.astype(BF16) for v in rest]
            rest = [v - p.astype(F32) for v, p in zip(rest, piece)]
            part = [jnp.dot(ltri, p, preferred_element_type=F32) for p in piece]
            sums = part if sums is None else [a + b for a, b in zip(sums, part)]
        for n in range(nchunk):
            cs_ref[pl.ds(n * CH, CH), :] = sums[n]

    blk = pl.BlockSpec((lp, DH), lambda b: (b, 0))
    vec = pl.BlockSpec((1, DH), lambda b: (0, 0))
    out = pl.BlockSpec((lp, DH), lambda b: (b, 0))
    return pl.pallas_call(
        body, name="gates", grid=(bsz,),
        in_specs=[blk, vec, vec], out_specs=[out, out],
        out_shape=[jax.ShapeDtypeStruct((bsz * lp, DH), F32)] * 2,
        compiler_params=_cparams(("parallel",)),
    )(proj, alog_v, dtb_v)


def _gates_bwd(proj, alog_v, dtb_v, dbeta_b, dg_b, gb, bsz, lp, padf):
    def body(p_ref, al_ref, dt_ref, db_ref, dg_ref, gb_ref, dl_ref, dv_ref):
        b = pl.program_id(0)
        x = p_ref[...].astype(F32)
        lane = lax.broadcasted_iota(jnp.int32, (lp, DH), 1)
        t = lax.broadcasted_iota(jnp.int32, (lp, DH), 0)
        dbeta = jnp.zeros((lp, DH), F32)
        dg = jnp.zeros((lp, DH), F32)
        for h in range(HEADS):
            dbeta = jnp.where(lane == h, db_ref[h], dbeta)
            dg = jnp.where(lane == HEADS + h, dg_ref[h], dg)
        real = t >= padf
        dbeta = jnp.where(real, dbeta, 0.0)
        dg = jnp.where(real, dg, 0.0)
        sb = jax.nn.sigmoid(x)
        neg_ea = -jnp.exp(al_ref[...])
        dsp = dg * neg_ea * jax.nn.sigmoid(x + dt_ref[...])
        dl = dbeta * sb * (1.0 - sb) + dsp
        dl_ref[...] = jnp.concatenate([dl, jnp.zeros((lp, BA_W - DH), F32)], axis=1).astype(BF16)
        dalog = jnp.sum(dg * gb_ref[...], axis=0, keepdims=True)
        ddt = jnp.sum(dsp, axis=0, keepdims=True)

        @pl.when(b == 0)
        def _():
            dv_ref[...] = jnp.zeros_like(dv_ref)

        dv_ref[...] += jnp.concatenate([dalog, ddt], axis=0)

    blk = pl.BlockSpec((lp, DH), lambda b: (b, 0))
    vec = pl.BlockSpec((1, DH), lambda b: (0, 0))
    hb = pl.BlockSpec((HEADS, lp, DH), lambda b: (b, 0, 0))
    return pl.pallas_call(
        body, name="gates_bwd", grid=(bsz,),
        in_specs=[blk, vec, vec, hb, hb, pl.BlockSpec((lp, DH), lambda b: (b, 0))],
        out_specs=[pl.BlockSpec((lp, BA_W), lambda b: (b, 0)), pl.BlockSpec((2, DH), lambda b: (0, 0))],
        out_shape=[jax.ShapeDtypeStruct((bsz * lp, BA_W), BF16), jax.ShapeDtypeStruct((2, DH), F32)],
        compiler_params=_cparams(("arbitrary",)),
    )(proj, alog_v, dtb_v, dbeta_b, dg_b, gb)


def _mix_heads(proj, o, gdn_norm, conv_sc, bsz, lp, padf):
    def body(o_ref, z_ref, gn_ref, sx_ref, sb_ref, sc_ref, w_ref, og_ref, os_ref):
        og_ref[...] = (_rms(o_ref[...], gn_ref[...])[0] * _silu(z_ref[...].astype(F32))).astype(BF16)
        conv = _conv_fwd(sc_ref[...].astype(F32) * sx_ref[...].astype(F32), w_ref[...], KS, lp)
        t = lax.broadcasted_iota(jnp.int32, (lp, 1), 0)
        os_ref[...] = jnp.where(t >= padf, sb_ref[...].astype(F32) * conv, 0.0).astype(BF16)

    half = _slab(lp, 0)
    return pl.pallas_call(
        body, name="mix_heads", grid=(bsz, HEADS),
        in_specs=[half, _slab(lp, Z0), pl.BlockSpec((1, DH), lambda b, s: (0, 0)),
                  _slab(lp, SX0), _slab(lp, SB0), _slab(lp, SC0), pl.BlockSpec((KS, DH), lambda b, s: (0, s))],
        out_specs=[half, half],
        out_shape=[jax.ShapeDtypeStruct((bsz * lp, GW), BF16)] * 2,
        compiler_params=_cparams(("parallel", "parallel")),
    )(o, proj, gdn_norm, proj, proj, proj, conv_sc)


def _mix_heads_bwd(proj, o, gdn_norm, conv_sc, dmixin, bsz, lp, padf):
    def body(o_ref, z_ref, gn_ref, sx_ref, sb_ref, sc_ref, w_ref, dg_ref, ds_ref,
             do_ref, dz_ref, dsx_ref, dsb_ref, dsc_ref, dgn_ref, dw_ref):
        s, b = pl.program_id(0), pl.program_id(1)
        t = lax.broadcasted_iota(jnp.int32, (lp, 1), 0)
        real = t >= padf
        o, z, gn, d = o_ref[...], z_ref[...].astype(F32), gn_ref[...], dg_ref[...]
        sz = _silu(z)
        on = _rms(o, gn)[0]
        dz_ref[...] = (d * on * _dsilu(z)).astype(BF16)
        dox, dgn = _rms_bwd(o, gn, d * sz)
        do_ref[...] = dox
        sx, sb, sc = sx_ref[...].astype(F32), sb_ref[...].astype(F32), sc_ref[...].astype(F32)
        w, e = w_ref[...], ds_ref[...]
        e = jnp.where(real, e, 0.0)
        xin = sc * sx
        conv = _conv_fwd(xin, w, KS, lp)
        dsb_ref[...] = (e * conv).astype(BF16)
        dxin, dw = _conv_bwd(xin, w, e * sb, KS, lp)
        dxin = jnp.where(real, dxin, 0.0)
        dsx_ref[...] = (dxin * sc).astype(BF16)
        dsc_ref[...] = (dxin * sx).astype(BF16)

        @pl.when(jnp.logical_and(s == 0, b == 0))
        def _():
            dgn_ref[...] = jnp.zeros_like(dgn_ref)

        @pl.when(b == 0)
        def _():
            dw_ref[...] = jnp.zeros_like(dw_ref)

        dgn_ref[...] += jnp.sum(dgn, axis=0, keepdims=True)
        dw_ref[...] += dw

    sl = lambda col0: pl.BlockSpec((lp, DH), lambda s, b: (b, col0 // DH + s))
    half = sl(0)
    return pl.pallas_call(
        body, name="mix_heads_bwd", grid=(HEADS, bsz),
        in_specs=[half, sl(Z0), pl.BlockSpec((1, DH), lambda s, b: (0, 0)), sl(SX0), sl(SB0), sl(SC0),
                  pl.BlockSpec((KS, DH), lambda s, b: (0, s)), sl(0), sl(GW)],
        out_specs=[half] * 5 + [pl.BlockSpec((1, DH), lambda s, b: (0, 0)), pl.BlockSpec((KS, DH), lambda s, b: (0, s))],
        out_shape=[jax.ShapeDtypeStruct((bsz * lp, GW), F32)] + [jax.ShapeDtypeStruct((bsz * lp, GW), BF16)] * 4
        + [jax.ShapeDtypeStruct((1, DH), F32), jax.ShapeDtypeStruct((KS, SW), F32)],
        compiler_params=_cparams(("arbitrary", "arbitrary")),
    )(o, proj, gdn_norm, proj, proj, proj, conv_sc, dmixin, dmixin)


def _split_dot(a, b, dims=NN):
    ah = a.astype(BF16)
    al = (a - ah.astype(F32)).astype(BF16)
    bh = b.astype(BF16)
    bl = (b - bh.astype(F32)).astype(BF16)

    def d(x, y):
        return lax.dot_general(x, y, (dims, ((), ())), preferred_element_type=F32)

    return d(ah, bh) + (d(ah, bl) + d(al, bh))


def _eye(n):
    return (lax.broadcasted_iota(jnp.int32, (n, n), 0) == lax.broadcasted_iota(jnp.int32, (n, n), 1)).astype(F32)


def _each(f, *lists):
    return [f(*xs) for xs in zip(*lists)]


def _inv_unit_lower(a_list):
    eye = _eye(CH)
    p = _each(lambda a: -a, a_list)
    t = _each(lambda x: eye + x, p)
    p = _each(lambda x: _split_dot(x, x), p)
    for level in range(5):
        if level < 4:
            prod = _each(lambda tt, pp: _split_dot(jnp.concatenate([tt, pp], axis=0), pp), t, p)
            t = _each(lambda tt, pr: tt + pr[:CH], t, prod)
            p = _each(lambda pr: pr[CH:], prod)
        else:
            t = _each(lambda tt, pp: tt + _split_dot(tt, pp), t, p)
    return t


def _chunk_gates(gc_ref, be_ref, gt_ref, g):
    rows = pl.ds(g * CH, CH)
    gc = gc_ref[0, rows, :]
    be = be_ref[0, rows, :]
    gcl = gc_ref[0, pl.ds(g * CH + CH - 1, 1), :]
    gt = gt_ref[0, g]
    incl, upper = _tri(CH, "incl"), _tri(CH, "upper")
    dec = jnp.where(incl, jnp.exp(jnp.where(incl, gc[:, :CH] - gt, 0.0)), 0.0)
    dec_t = jnp.where(upper, jnp.exp(jnp.where(upper, gt - gc[:, :CH], 0.0)), 0.0)
    return rows, be, dec, dec_t, jnp.exp(gc), jnp.exp(gcl - gc), jnp.exp(gcl)


def _chunk_group(nchunk):
    return 3 if nchunk % 3 == 0 else 1


def _local_group(nchunk):
    return 11 if nchunk % 11 == 0 else _chunk_group(nchunk)


def _gdn_local_specs(lp, grp):
    ngrp = lp // (grp * CH)
    head = lambda col0: pl.BlockSpec((grp * CH, DH), lambda i, j: ((i // HEADS) * ngrp + j, col0 // DH + i % HEADS))
    lane_b = pl.BlockSpec((1, grp * CH, DH), lambda i, j: (i, j, 0))
    per_chunk = lambda r, c: pl.BlockSpec((1, grp, r, c), lambda i, j: (i, j, 0, 0))
    return ngrp, head, lane_b, per_chunk


def _gdn_local_fwd(qkv, gcb, bb, gct, bsz, lp):
    nchunk = lp // CH
    grp = _local_group(nchunk)
    ngrp, head, lane_b, per_chunk = _gdn_local_specs(lp, grp)
    bh = bsz * HEADS

    def body(q_ref, k_ref, v_ref, gc_ref, be_ref, gt_ref,
             u_ref, l1_ref, l2_ref, l3_ref, wt_ref, kd_ref, ti_ref, gl_ref):
        incl, strict, upper = _tri(CH, "incl"), _tri(CH, "strict"), _tri(CH, "upper")
        eye = _eye(CH)
        rows, be, dec, dec_t, eg, ekd, gl = zip(*[_chunk_gates(gc_ref, be_ref, gt_ref, g) for g in range(grp)])
        q = [q_ref[r, :] for r in rows]
        k = [k_ref[r, :] for r in rows]
        v = [v_ref[r, :] for r in rows]
        kb = _each(lambda x, y: x * y, k, be)
        a = _each(lambda x, y, d: jnp.where(strict, _bdot(x, y, NT) * d, 0.0), kb, k, dec)
        tinv = _inv_unit_lower(a)
        u = _each(lambda t, x, y: _bdot(t, x * y, NN), tinv, v, be)
        w = _each(lambda t, x, y: _bdot(t, x * y, NN), tinv, kb, eg)
        qk = _each(lambda x, y, d: jnp.where(incl, _bdot(x, y, NT) * d, 0.0), q, k, dec)
        qk_t = _each(lambda x, y, d: jnp.where(upper, _bdot(y, x, NT) * d, 0.0), q, k, dec_t)
        qd = _each(lambda x, y: x * y, q, eg)
        kd = _each(lambda x, y: x * y, k, ekd)
        kd_t = _each(lambda x: _bdot(x, eye, TN), kd)
        qd_t = _each(lambda x: _bdot(x, eye, TN), qd)
        w_t = _each(lambda x: _bdot(x, eye, TN), w)
        for g in range(grp):
            u_ref[rows[g], :] = u[g]
            l1_ref[0, g] = jnp.concatenate([w[g], qd[g]], axis=0).astype(BF16)
            l2_ref[0, g] = jnp.concatenate([qk[g], kd_t[g]], axis=0).astype(BF16)
            l3_ref[0, g] = jnp.concatenate([qk_t[g], qd_t[g]], axis=0).astype(BF16)
            wt_ref[0, g] = w_t[g].astype(BF16)
            kd_ref[0, g] = kd[g].astype(BF16)
            ti_ref[0, g] = tinv[g]
            gl_ref[0, g] = gl[g]

    r = bsz * lp
    shapes = [((r, GW), F32), ((bh, nchunk, 2 * CH, DH), BF16), ((bh, nchunk, 3 * CH, CH), BF16),
              ((bh, nchunk, 3 * CH, CH), BF16), ((bh, nchunk, DH, CH), BF16), ((bh, nchunk, CH, DH), BF16),
              ((bh, nchunk, CH, CH), F32), ((bh, nchunk, 1, DH), F32)]
    return pl.pallas_call(
        body, name="gdn_local_fwd", grid=(bh, ngrp),
        in_specs=[head(0), head(GW), head(2 * GW), lane_b, lane_b, per_chunk(1, CH)],
        out_specs=[head(0), per_chunk(2 * CH, DH), per_chunk(3 * CH, CH), per_chunk(3 * CH, CH), per_chunk(DH, CH),
                   per_chunk(CH, DH), per_chunk(CH, CH), per_chunk(1, DH)],
        out_shape=[jax.ShapeDtypeStruct(s, dt) for s, dt in shapes],
        compiler_params=_cparams(("parallel", "parallel")),
    )(qkv, qkv, qkv, gcb, bb, gct)


def _scan_specs(bsz, lp, grp, order):
    ngrp = lp // (grp * CH)
    bh = bsz * HEADS
    seq = lambda: pl.BlockSpec((bsz, grp * CH, GW), lambda s: (0, order(s, ngrp), 0))
    per_chunk = lambda r, c: pl.BlockSpec((bh, grp, r, c), lambda s: (0, order(s, ngrp), 0, 0))
    return ngrp, bh, seq, per_chunk


def _gdn_scan_fwd(u, l1, l2, gl, bsz, lp):
    nchunk = lp // CH
    grp = _chunk_group(nchunk)
    ngrp, bh, seq, per_chunk = _scan_specs(bsz, lp, grp, lambda s, n: s)

    def body(u_ref, l1_ref, l2_ref, gl_ref, o_ref, vn_ref, st_ref, s_scr):
        @pl.when(pl.program_id(0) == 0)
        def _():
            s_scr[...] = jnp.zeros_like(s_scr)

        chains = list(range(bh))
        where = [(i // HEADS, slice((i % HEADS) * DH, (i % HEADS + 1) * DH)) for i in chains]

        def dot(a, b):
            return jnp.dot(a, b.astype(BF16), preferred_element_type=F32)

        for c in range(grp):
            rows = slice(c * CH, (c + 1) * CH)
            s = [s_scr[i] for i in chains]
            for i in chains:
                st_ref[i, c] = s[i]
            x = _each(lambda i, si: dot(l1_ref[i, c], si), chains, s)
            vnew = _each(lambda bc, xi: u_ref[bc[0], rows, bc[1]] - xi[:CH], where, x)
            y = _each(lambda i, vi: dot(l2_ref[i, c], vi), chains, vnew)
            for i, (b, cols) in zip(chains, where):
                o_ref[b, rows, cols] = x[i][CH:] + y[i][:CH]
                vn_ref[b, rows, cols] = vnew[i].astype(BF16)
                s_scr[i] = s[i] * gl_ref[i, c] + y[i][CH:]

    return pl.pallas_call(
        body, name="gdn_scan_fwd", grid=(ngrp,),
        in_specs=[seq(), per_chunk(2 * CH, DH), per_chunk(3 * CH, CH), per_chunk(1, DH)],
        out_specs=[seq(), seq(), per_chunk(DH, DH)],
        out_shape=[jax.ShapeDtypeStruct((bsz, lp, GW), F32), jax.ShapeDtypeStruct((bsz, lp, GW), BF16),
                   jax.ShapeDtypeStruct((bh, nchunk, DH, DH), F32)],
        scratch_shapes=[pltpu.VMEM((bh, DH, DH), F32)],
        compiler_params=_cparams(("arbitrary",)),
    )(u.reshape(bsz, lp, GW), l1, l2, gl)


def _gdn_scan_bwd(d_o, l3, wt, kd, gl, bsz, lp):
    nchunk = lp // CH
    grp = _chunk_group(nchunk)
    ngrp, bh, seq, per_chunk = _scan_specs(bsz, lp, grp, lambda s, n: n - 1 - s)

    def body(do_ref, l3_ref, wt_ref, kd_ref, gl_ref, dvn_ref, dst_ref, ds_scr):
        @pl.when(pl.program_id(0) == 0)
        def _():
            ds_scr[...] = jnp.zeros_like(ds_scr)

        chains = list(range(bh))
        where = [(i // HEADS, slice((i % HEADS) * DH, (i % HEADS + 1) * DH)) for i in chains]

        def dot(a, b):
            return jnp.dot(a, b.astype(BF16), preferred_element_type=F32)

        for c in reversed(range(grp)):
            rows = slice(c * CH, (c + 1) * CH)
            ds = [ds_scr[i] for i in chains]
            for i in chains:
                dst_ref[i, c] = ds[i]
            p = _each(lambda i, bc: dot(l3_ref[i, c], do_ref[bc[0], rows, bc[1]]), chains, where)
            dvn = _each(lambda i, pi, di: pi[:CH] + dot(kd_ref[i, c], di), chains, p, ds)
            wd = _each(lambda i, vi: dot(wt_ref[i, c], vi), chains, dvn)
            for i, (b, cols) in zip(chains, where):
                dvn_ref[b, rows, cols] = dvn[i].astype(BF16)
                ds_scr[i] = ds[i] * gl_ref[i, c] + p[i][CH:] - wd[i]

    return pl.pallas_call(
        body, name="gdn_scan_bwd", grid=(ngrp,),
        in_specs=[seq(), per_chunk(3 * CH, CH), per_chunk(DH, CH), per_chunk(CH, DH), per_chunk(1, DH)],
        out_specs=[seq(), per_chunk(DH, DH)],
        out_shape=[jax.ShapeDtypeStruct((bsz, lp, GW), BF16), jax.ShapeDtypeStruct((bh, nchunk, DH, DH), F32)],
        scratch_shapes=[pltpu.VMEM((bh, DH, DH), F32)],
        compiler_params=_cparams(("arbitrary",)),
    )(d_o.reshape(bsz, lp, GW), l3, wt, kd, gl)


def _gdn_local_bwd(qkv, gcb, bb, gct, tinv_all, states, dstates, vnew, dvnew, d_o, bsz, lp):
    nchunk = lp // CH
    grp = _local_group(nchunk)
    ngrp, head, lane_b, per_chunk = _gdn_local_specs(lp, grp)
    bh = bsz * HEADS

    def body(q_ref, k_ref, v_ref, gc_ref, be_ref, gt_ref, ti_ref, st_ref, dst_ref, vn_ref, dvn_ref, do_ref,
             dq_ref, dk_ref, dv_ref, dg_ref, db_ref):
        incl, strict = _tri(CH, "incl"), _tri(CH, "strict")
        upper = _tri(CH, "upper").astype(F32)
        ones = jnp.ones((CH, DH), F32)
        last = lax.broadcasted_iota(jnp.int32, (CH, 1), 0) == CH - 1

        def rsum(x):
            return jnp.sum(x, axis=-1, keepdims=True)

        def mul(xs, ys):
            return _each(lambda x, y: x * y, xs, ys)

        rows, be, dec, _, eg, ekd, gl = zip(*[_chunk_gates(gc_ref, be_ref, gt_ref, g) for g in range(grp)])
        q = [q_ref[r, :] for r in rows]
        k = [k_ref[r, :] for r in rows]
        v = [v_ref[r, :] for r in rows]
        vnew = [vn_ref[r, :] for r in rows]
        dvn = [dvn_ref[r, :] for r in rows]
        do = [do_ref[r, :] for r in rows]
        tinv = [ti_ref[0, g] for g in range(grp)]
        s = [st_ref[0, g] for g in range(grp)]
        ds = [dst_ref[0, g] for g in range(grp)]
        kb, vb, qd, kd = mul(k, be), mul(v, be), mul(q, eg), mul(k, ekd)
        kbg = mul(kb, eg)
        a = _each(lambda x, y, d: jnp.where(strict, _bdot(x, y, NT) * d, 0.0), kb, k, dec)
        qk = _each(lambda x, y, d: jnp.where(incl, _bdot(x, y, NT) * d, 0.0), q, k, dec)
        dqk = _each(lambda x, y: jnp.where(incl, _bdot(x, y, NT), 0.0), do, vnew)
        dqd = _each(lambda x, y: _bdot(x, y, NT), do, s)
        dkd = _each(lambda x, y: _bdot(x, y, NT), vnew, ds)
        dgl = _each(lambda x, y: jnp.sum(rsum(x * y), axis=0, keepdims=True), s, ds)
        dw = _each(lambda x, y: -_bdot(x, y, NT), dvn, s)
        dvb = _each(lambda t, x: _bdot(t, x, TN), tinv, dvn)
        dkbg = _each(lambda t, x: _bdot(t, x, TN), tinv, dw)
        dt = _each(lambda x, y, z, w: _bdot(x, y, NT) + _bdot(z, w, NT), dvn, vb, dw, kbg)
        tdt = _each(lambda t, x: _split_dot(t, x, TN), tinv, dt)
        da = _each(lambda x, t: jnp.where(strict, -_split_dot(x, t, NT), 0.0), tdt, tinv)
        dp, dr = mul(da, dec), mul(dqk, dec)
        dkb = _each(lambda x, y, z, e: _bdot(x, y, NN) + z * e, dp, k, dkbg, eg)
        dk = _each(lambda p_, kb_, r_, q_, dkd_, ekd_, dkb_, be_:
                   _bdot(p_, kb_, TN) + _bdot(r_, q_, TN) + dkd_ * ekd_ + dkb_ * be_,
                   dp, kb, dr, q, dkd, ekd, dkb, be)
        dq = _each(lambda r_, k_, dqd_, eg_: _bdot(r_, k_, NN) + dqd_ * eg_, dr, k, dqd, eg)
        m = _each(lambda da_, a_, dqk_, qk_: da_ * a_ + dqk_ * qk_, da, a, dqk, qk)
        m_cols = _each(lambda x: _split_dot(x, ones, TN), m)
        kd_term = _each(lambda x, y: rsum(x * y), dkd, kd)
        dgc = _each(lambda m_, mc, dqd_, qd_, kt, dkbg_, kbg_, dgl_, gl_:
                    rsum(m_) - mc + rsum(dqd_ * qd_) - kt + rsum(dkbg_ * kbg_)
                    + jnp.where(last, jnp.sum(kt, axis=0, keepdims=True) + dgl_ * gl_, 0.0),
                    m, m_cols, dqd, qd, kd_term, dkbg, kbg, dgl, gl)
        dg = _each(lambda x: _split_dot(upper, x, NN), dgc)
        for g in range(grp):
            dq_ref[rows[g], :] = dq[g]
            dk_ref[rows[g], :] = dk[g]
            dv_ref[rows[g], :] = dvb[g] * be[g]
            dg_ref[0, rows[g], :] = dg[g]
            db_ref[0, rows[g], :] = jnp.broadcast_to(rsum(dkb[g] * k[g]) + rsum(dvb[g] * v[g]), (CH, DH))

    r = bsz * lp
    return pl.pallas_call(
        body, name="gdn_local_bwd", grid=(bh, ngrp),
        in_specs=[head(0), head(GW), head(2 * GW), lane_b, lane_b, per_chunk(1, CH), per_chunk(CH, CH),
                  per_chunk(DH, DH), per_chunk(DH, DH), head(0), head(0), head(0)],
        out_specs=[head(0), head(0), head(0), lane_b, lane_b],
        out_shape=[jax.ShapeDtypeStruct((r, GW), F32)] * 3 + [jax.ShapeDtypeStruct((bh, lp, DH), F32)] * 2,
        compiler_params=_cparams(("parallel", "parallel")),
    )(qkv, qkv, qkv, gcb, bb, gct, tinv_all, states, dstates, vnew, dvnew, d_o)


def _place():
    return lax.axis_index("x"), lax.axis_index("y"), lax.axis_index("c")


def _any_specs(n):
    return [pl.BlockSpec(memory_space=pl.ANY)] * n


def _all_gather(name, shards):
    n = len(shards)

    def body(*refs):
        ins, outs = refs[:n], refs[n:2 * n]
        send_sems, recv_sems, local_sems = refs[2 * n:2 * n + 3]
        stage = refs[2 * n + 3:]
        x, y, c = _place()
        me, sibling = (x, y, c), (x, y, 1 - c)
        chips = [(1 - x, y), (x, 1 - y), (1 - x, 1 - y)]

        def slot(i, p):
            return outs[i].at[4 * p[0] + 2 * p[1] + p[2]]

        def copy(i, k, block, to, src=None):
            return pltpu.make_async_remote_copy(
                src_ref=slot(i, block) if src is None else src, dst_ref=slot(i, block),
                send_sem=send_sems.at[i, k], recv_sem=recv_sems.at[i, k], device_id=to, device_id_type=MESH)

        load = [pltpu.make_async_copy(ins[i], stage[i], local_sems.at[i]) for i in range(n)]
        for cp in load:
            cp.start()
        first = []
        for i in range(n):
            first.append(copy(i, 0, me, sibling, src=ins[i]))
            first += [copy(i, 1 + j, me, (*chip, c), src=ins[i]) for j, chip in enumerate(chips)]
        for cp in first:
            cp.start()
        for cp in load:
            cp.wait()
        mine = [pltpu.make_async_copy(stage[i], slot(i, me), local_sems.at[i]) for i in range(n)]
        for cp in mine:
            cp.start()
        passed = []
        for j, chip in enumerate(chips):
            for i in range(n):
                copy(i, 1 + j, (*chip, c), me).wait_recv()
                fwd = copy(i, 4 + j, (*chip, c), sibling)
                fwd.start()
                passed.append(fwd)
        for i in range(n):
            copy(i, 0, sibling, me).wait_recv()
            for j, chip in enumerate(chips):
                copy(i, 4 + j, (*chip, 1 - c), me).wait_recv()
        for cp in first + passed:
            cp.wait_send()
        for cp in mine:
            cp.wait()

    return pl.pallas_call(
        body, name=name,
        in_specs=_any_specs(n), out_specs=_any_specs(n),
        out_shape=[jax.ShapeDtypeStruct((N_DEV,) + s.shape, s.dtype) for s in shards],
        scratch_shapes=[pltpu.SemaphoreType.DMA((n, 7)), pltpu.SemaphoreType.DMA((n, 7)),
                        pltpu.SemaphoreType.DMA((n,))] + [pltpu.VMEM(s.shape, s.dtype) for s in shards],
    )(*shards)


HBM_SPEC = pl.BlockSpec(memory_space=pltpu.HBM)
SEM_SPEC = pl.BlockSpec(memory_space=pltpu.SEMAPHORE)
EFFECT = pltpu.SideEffectType.DATAFLOW_SIDE_EFFECTING
N_PEER = N_DEV - 1


def _peer(r):
    x, y, c = _place()
    return ((1 - x) if r & 4 else x, (1 - y) if r & 2 else y, (1 - c) if r & 1 else c)


def _slot_of(p):
    return 4 * p[0] + 2 * p[1] + p[2]


def _hbm(a):
    return pltpu.with_memory_space_constraint(a, pltpu.HBM)


def _gather_pattern(src, land):
    me = _place()
    return [(src, land.at[_slot_of(me)], _peer(r), land.at[_slot_of(_peer(r))]) for r in range(1, N_DEV)]


def _alltoall_pattern(src, land):
    me = _place()
    return [(src.at[_slot_of(_peer(r))], land.at[_slot_of(me)], _peer(r), land.at[_slot_of(_peer(r))])
            for r in range(1, N_DEV)]


def _chipcast_pattern(src, land):
    x, y, c = _place()
    me = (x, y, c)
    peers = [(x, y, 1 - c), (1 - x, y, c), (x, 1 - y, c), (1 - x, 1 - y, c)]
    return [(src, land.at[_slot_of(me)], p, land.at[_slot_of(p)]) for p in peers]


def _forward_pattern(land, _):
    x, y, c = _place()
    chips = [(1 - x, y), (x, 1 - y), (1 - x, 1 - y)]
    return [(land.at[_slot_of((*ch, c))], land.at[_slot_of((*ch, c))], (x, y, 1 - c), land.at[_slot_of((*ch, 1 - c))])
            for ch in chips]


def _sibling_pattern(src, land):
    x, y, c = _place()
    return [(src.at[ch, 1 - c], land.at[ch], (x, y, 1 - c), land.at[ch]) for ch in range(4)]


def _chip_pattern(src, land):
    x, y, c = _place()
    chips = [(1 - x, y), (x, 1 - y), (1 - x, 1 - y)]
    return [(src.at[2 * ch[0] + ch[1]], land.at[j], (*ch, c), land.at[j]) for j, ch in enumerate(chips)]


def _split_start(name, groups, pattern, land_shape, after=None):
    sizes = [len(g) for g in groups]
    arrays = [s for g in groups for s in g]
    n = len(arrays)
    inplace = land_shape is None
    fresh = [] if inplace else [lax.empty(land_shape(s.shape), s.dtype) for s in arrays]
    bufs = arrays + fresh
    order = [] if after is None else [after]

    def body(*refs):
        srcs, lands = refs[:n], (refs[:n] if inplace else refs[n:2 * n])
        first_out = len(bufs) + len(order)
        sems = refs[first_out:first_out + 2 * len(groups)]
        token = refs[-1]
        k = 0
        for gi, size in enumerate(sizes):
            for i in range(size):
                copies = pattern(srcs[k], lands[k])
                for j, (s, d, peer, _) in enumerate(copies):
                    pltpu.make_async_remote_copy(
                        src_ref=s, dst_ref=d, send_sem=sems[2 * gi].at[i * len(copies) + j],
                        recv_sem=sems[2 * gi + 1].at[i * len(copies) + j], device_id=peer, device_id_type=MESH).start()
                k += 1
        token[...] = jnp.zeros_like(token)

    per_array = {_gather_pattern: N_PEER, _alltoall_pattern: N_PEER, _chipcast_pattern: 4, _forward_pattern: 3,
                 _sibling_pattern: 4, _chip_pattern: 3}[pattern]
    sem_shapes = [pltpu.SemaphoreType.DMA((size * per_array,)) for size in sizes for _ in range(2)]
    res = pl.pallas_call(
        body, name=name,
        in_specs=[HBM_SPEC] * len(bufs) + [pl.BlockSpec(memory_space=pl.ANY)] * len(order),
        out_specs=[SEM_SPEC] * len(sem_shapes) + [HBM_SPEC] * len(bufs) + [pl.BlockSpec(memory_space=pltpu.VMEM)],
        out_shape=sem_shapes + [pltpu.HBM(s.shape, s.dtype) for s in bufs] + [jax.ShapeDtypeStruct((8, DH), F32)],
        input_output_aliases={i: len(sem_shapes) + i for i in range(len(bufs))},
        compiler_params=pltpu.CompilerParams(has_side_effects=EFFECT),
    )(*[_hbm(s) for s in bufs], *order)
    nsem = len(sem_shapes)
    thru = list(res[nsem:nsem + len(bufs)])
    out, k = [], 0
    for gi, size in enumerate(sizes):
        srcs_out = [] if inplace else thru[k:k + size]
        lands_out = thru[k:k + size] if inplace else thru[n + k:n + k + size]
        out.append((res[2 * gi], res[2 * gi + 1], srcs_out, lands_out))
        k += size
    return out, res[-1]


def _split_wait(name, group, pattern, after):
    send_sems, recv_sems, arrays, lands = group
    n = len(lands)
    inplace = not arrays
    bufs = list(arrays) + list(lands)
    fill_own = pattern in (_gather_pattern, _chipcast_pattern, _alltoall_pattern)
    own_of = (lambda a: a.at[_slot_of(_place())]) if pattern is _alltoall_pattern else (lambda a: a)
    own_shape = (lambda a: a.shape[1:]) if pattern is _alltoall_pattern else (lambda a: a.shape)

    def body(*refs):
        srcs, lands_ = refs[:n], (refs[:n] if inplace else refs[n:2 * n])
        s_sems, r_sems = refs[len(bufs)], refs[len(bufs) + 1]
        own = []
        if fill_own:
            stage, local_sems = refs[-n - 1:-1], refs[-1]
            load = [pltpu.make_async_copy(own_of(srcs[i]), stage[i], local_sems.at[i]) for i in range(n)]
            for cp in load:
                cp.start()
            for cp in load:
                cp.wait()
            own = [pltpu.make_async_copy(stage[i], lands_[i].at[_slot_of(_place())], local_sems.at[i])
                   for i in range(n)]
            for cp in own:
                cp.start()
        for i in range(n):
            copies = pattern(srcs[i], lands_[i])
            for j, (s, _, peer, filled) in enumerate(copies):
                cp = pltpu.make_async_remote_copy(
                    src_ref=s, dst_ref=filled, send_sem=s_sems.at[i * len(copies) + j],
                    recv_sem=r_sems.at[i * len(copies) + j], device_id=peer, device_id_type=MESH)
                cp.wait_send()
                cp.wait_recv()
        for cp in own:
            cp.wait()

    res = pl.pallas_call(
        body, name=name,
        in_specs=[HBM_SPEC] * len(bufs) + [SEM_SPEC, SEM_SPEC, pl.BlockSpec(memory_space=pl.ANY)],
        out_specs=[HBM_SPEC] * len(bufs),
        out_shape=[pltpu.HBM(s.shape, s.dtype) for s in bufs],
        scratch_shapes=([pltpu.VMEM(own_shape(a), a.dtype) for a in arrays] + [pltpu.SemaphoreType.DMA((n,))])
        if fill_own else [],
        input_output_aliases={i: i for i in range(len(bufs))},
        compiler_params=pltpu.CompilerParams(has_side_effects=EFFECT),
    )(*bufs, send_sems, recv_sems, after)
    return ([], list(res)) if inplace else (list(res[:n]), list(res[n:]))


def _pair_sum(name, grads, recv, core):
    _, _, r, c = grads.shape
    tr = r if r * c <= (1 << 20) or r % 16 else _div_tile(r, 512, 16)

    def body(core_ref, g_ref, r_ref, o_ref):
        o_ref[...] = (g_ref[0].astype(F32) + r_ref[...].astype(F32)).astype(o_ref.dtype)

    return pl.pallas_call(
        body, name=name,
        grid_spec=pltpu.PrefetchScalarGridSpec(
            num_scalar_prefetch=1, grid=(4, r // tr),
            in_specs=[pl.BlockSpec((1, 1, tr, c), lambda s, i, cr: (s, cr[0], i, 0)),
                      pl.BlockSpec((1, tr, c), lambda s, i, cr: (s, i, 0))],
            out_specs=pl.BlockSpec((1, tr, c), lambda s, i, cr: (s, i, 0))),
        out_shape=jax.ShapeDtypeStruct(recv.shape, recv.dtype),
        compiler_params=_cparams(("parallel", "parallel")),
    )(core, grads, recv)


def _adamw_math(w, g, m, v):
    m = ADAM_B1 * m + (1.0 - ADAM_B1) * g
    v = ADAM_B2 * v + (1.0 - ADAM_B2) * (g * g)
    m_hat = m / (1.0 - ADAM_B1 ** ADAM_STEP)
    v_hat = v / (1.0 - ADAM_B2 ** ADAM_STEP)
    delta = -ADAM_LR * (m_hat / (jnp.sqrt(v_hat) + ADAM_EPS) + ADAM_WD * w)
    return delta, m, v


SMALL_LEAVES = (
    ("mix_pre_norm", 0, 1, 0, D, (1, D)), ("mix_post_norm", 1, 1, 0, D, (1, D)),
    ("ffn_pre_norm", 2, 1, 0, D, (1, D)), ("ffn_post_norm", 3, 1, 0, D, (1, D)),
    ("meta_tokens", 8, N_META, 0, DH, (N_META, DH)),
    ("conv_qkv", 24, KQ, 0, 3 * GW // N_DEV, (1, KQ, 3 * GW // N_DEV)),
    ("conv_sc", 32, KS, 0, SW // N_DEV, (1, KS, SW // N_DEV)),
    ("a_log", 40, 1, 0, HEADS, (1, HEADS)), ("dt_bias", 40, 1, DH, HEADS, (1, HEADS)),
    ("gdn_norm", 40, 1, 2 * DH, DH, (1, DH)),
)
SMALL_LOSS_AT = (40, 3 * DH)


LATE_ROWS = 8 + N_META


def _adamw_small(slabs, late, w, m, v):
    nslot = slabs.shape[0]
    nleaf = len(SMALL_LEAVES)

    def body(p_ref, l_ref, w_ref, m_ref, v_ref, *refs):
        outs, loss_ref, scr = refs[:4 * nleaf], refs[4 * nleaf], refs[4 * nleaf + 1]
        g = p_ref[0]
        gain = l_ref[0, pl.ds(0, 1), :]
        meta = l_ref[0, pl.ds(8, N_META), pl.ds(0, DH)]
        for s in range(1, nslot):
            g = g + p_ref[s]
            gain = gain + l_ref[s, pl.ds(0, 1), :]
            meta = meta + l_ref[s, pl.ds(8, N_META), pl.ds(0, DH)]
        scr[0] = g
        scr[0, pl.ds(0, 1), pl.ds(0, D)] += gain
        scr[0, pl.ds(8, N_META), pl.ds(0, DH)] += meta
        g = scr[0]
        d, mn, vn = _adamw_math(w_ref[...], g, m_ref[...], v_ref[...])
        for kind, val in enumerate((g, d, mn, vn)):
            scr[kind] = val
        for kind in range(4):
            for li, (_, r0, nr, c0, nc, shape) in enumerate(SMALL_LEAVES):
                val = scr[kind, pl.ds(r0, nr), pl.ds(c0, nc)]
                o = outs[kind * nleaf + li]
                if len(shape) == 3:
                    o[0] = val
                else:
                    o[...] = val
        loss_ref[...] = scr[0, pl.ds(SMALL_LOSS_AT[0], 1), pl.ds(SMALL_LOSS_AT[1], 1)]

    whole = lambda a: pl.BlockSpec(a.shape, lambda: (0,) * a.ndim)
    out_shapes = [jax.ShapeDtypeStruct(shape, F32) for _ in range(4) for (*_, shape) in SMALL_LEAVES]
    out_shapes.append(jax.ShapeDtypeStruct((1, 1), F32))
    res = pl.pallas_call(
        body, name="adamw_small",
        in_specs=[whole(slabs), whole(late), whole(w), whole(m), whole(v)],
        out_specs=[pl.BlockSpec(s.shape, lambda n=len(s.shape): (0,) * n) for s in out_shapes],
        out_shape=out_shapes,
        scratch_shapes=[pltpu.VMEM((4,) + w.shape, F32)],
        compiler_params=pltpu.CompilerParams(vmem_limit_bytes=VMEM_LIMIT),
    )(slabs, late, w, m, v)
    kinds = [{nm: res[kind * nleaf + li] for li, (nm, *_) in enumerate(SMALL_LEAVES)} for kind in range(4)]
    return kinds, res[-1]


def _adamw_reduced(name, parts, chip, recv, w, m, v, after=None):
    _, r, c = parts.shape
    tr = r if r * c <= (1 << 20) or r % 16 else _div_tile(r, 512, 16)
    order = [] if after is None else [after]

    def body(chip_ref, p_ref, r_ref, w_ref, m_ref, v_ref, *rest):
        g_out, d_out, m_out, v_out = rest[len(order):]
        g = p_ref[0].astype(F32)
        for s in range(3):
            g = g + r_ref[s].astype(F32)
        d, mn, vn = _adamw_math(w_ref[...], g, m_ref[...], v_ref[...])
        g_out[...] = g
        d_out[...] = d
        m_out[...] = mn
        v_out[...] = vn

    blk = pl.BlockSpec((tr, c), lambda i, ch: (i, 0))
    return pl.pallas_call(
        body, name=name,
        grid_spec=pltpu.PrefetchScalarGridSpec(
            num_scalar_prefetch=1, grid=(r // tr,),
            in_specs=[pl.BlockSpec((1, tr, c), lambda i, ch: (ch[0], i, 0)),
                      pl.BlockSpec((3, tr, c), lambda i, ch: (0, i, 0)), blk, blk, blk]
            + [pl.BlockSpec((8, DH), lambda i, ch: (0, 0))] * len(order),
            out_specs=[blk] * 4),
        out_shape=[jax.ShapeDtypeStruct((r, c), F32)] * 4,
        compiler_params=_cparams(("parallel",)),
    )(chip, parts, recv, w, m, v, *order)


SMALL_ROWS, SMALL_COLS = 48, 3 * GW


def _pack_small(gains4, meta, conv_qkv, conv_sc, a_log, dt_bias, gdn_norm, extra=None):
    def tile(a):
        return jnp.pad(a, ((0, -a.shape[0] % 8), (0, SMALL_COLS - a.shape[1])))
    scal = jnp.concatenate([jnp.pad(a_log, ((0, 0), (0, DH - HEADS))), jnp.pad(dt_bias, ((0, 0), (0, DH - HEADS))),
                            gdn_norm] + ([] if extra is None else [extra]), axis=1)
    gains = sum(jnp.pad(g, ((i, 7 - i), (0, SMALL_COLS - g.shape[1]))) for i, g in enumerate(gains4))
    parts = [gains, tile(meta), tile(conv_qkv), tile(conv_sc), tile(scal)]
    slab = jnp.concatenate(parts, axis=0)
    assert slab.shape == (SMALL_ROWS, SMALL_COLS)
    return slab


def kernel(x, meta_tokens, mix_pre_norm, mix_post_norm, ffn_pre_norm, ffn_post_norm, w_in, conv_qkv, a_log, dt_bias, gdn_norm, conv_sc, w_out, w_gate, w_up, w_down, loss_target, m_meta_tokens, m_mix_pre_norm, m_mix_post_norm, m_ffn_pre_norm, m_ffn_post_norm, m_w_in, m_conv_qkv, m_a_log, m_dt_bias, m_gdn_norm, m_conv_sc, m_w_out, m_w_gate, m_w_up, m_w_down, v_meta_tokens, v_mix_pre_norm, v_mix_post_norm, v_ffn_pre_norm, v_ffn_post_norm, v_w_in, v_conv_qkv, v_a_log, v_dt_bias, v_gdn_norm, v_conv_sc, v_w_out, v_w_gate, v_w_up, v_w_down):
    bsz, seq, _ = x.shape
    lp = -(-(N_META + seq) // CH) * CH
    padf = lp - N_META - seq
    first_real = padf + N_META
    rows = bsz * lp
    nchunk = lp // CH
    tr = _div_tile(lp, 528)
    tm = _div_tile(rows, 1056)
    tm_big = _div_tile(rows, 2112)
    cx, cy, cc = _place()

    small_shard = jnp.concatenate([
        jnp.pad(meta_tokens, ((0, 0), (0, 256 - DH))),
        jnp.pad(conv_qkv[0], ((0, 0), (0, 256 - 3 * GW // N_DEV))),
        jnp.pad(conv_sc[0], ((0, 5), (0, 256 - SW // N_DEV)))], axis=0)
    g_in, g_small = _all_gather("gather_w_in", [w_in[0].T.astype(BF16), small_shard])
    my_rest = [w_out[0].astype(BF16), w_gate[0].T.astype(BF16), w_up[0].T.astype(BF16), w_down[0].astype(BF16)]
    (grp_rest,), token = _split_start("gather_rest_start", [my_rest], _chipcast_pattern, lambda s: (N_DEV,) + s,
                                      after=g_in)
    after_start = token[0:1, 0:1]

    def cols_full(g):
        return jnp.transpose(g, (1, 0, 2)).reshape(g.shape[1], -1)

    win_t = g_in.reshape(IN_W, D)
    win_t = jnp.concatenate([win_t[:BA0 + 2 * HEADS], jnp.zeros((BA_W - 2 * HEADS, D), BF16),
                             win_t[BA0 + 2 * HEADS:]], axis=0)
    meta_full = cols_full(g_small[:, 0:16, :DH])
    convq_full = cols_full(g_small[:, 16:20, :3 * GW // N_DEV])
    convs_full = cols_full(g_small[:, 20:23, :SW // N_DEV])
    alog_v = jnp.pad(a_log, ((0, 0), (HEADS, DH - 2 * HEADS)))
    dtb_v = jnp.pad(dt_bias, ((0, 0), (HEADS, DH - 2 * HEADS)))

    head_rows = jnp.concatenate([jnp.zeros((padf, D), F32), meta_full], axis=0)
    h0, u1 = _pre_norm(x, head_rows, mix_pre_norm + after_start, bsz, lp, tr)
    (proj,) = _mm("proj_in", [(u1, win_t)], "nt", tm_big, 768, D, out_dtypes=(BF16,))
    (ba,) = _mm("proj_gate_logits", [(u1, win_t[BA0:SX0])], "nt", tm, BA_W, D)
    qkv = _gdn_prep(proj, convq_full, bsz, lp, padf)
    gb, gcs = _gates(ba, alog_v, dtb_v, bsz, lp, padf)

    def lane_bcast(cols):
        t = jnp.transpose(cols.reshape(bsz, lp, HEADS), (0, 2, 1)).reshape(bsz * HEADS, lp, 1)
        return jnp.broadcast_to(t, (bsz * HEADS, lp, DH))

    bb = lane_bcast(gb[:, 0:HEADS])
    gcb = lane_bcast(gcs[:, HEADS:2 * HEADS])
    gct = jnp.transpose(gcs[:, HEADS:2 * HEADS].reshape(bsz, nchunk, CH, HEADS), (0, 3, 1, 2)).reshape(
        bsz * HEADS, nchunk, 1, CH)
    u_loc, l1, l2, l3, w_t, k_dec, tinv_all, g_last = _gdn_local_fwd(qkv, gcb, bb, gct, bsz, lp)
    o, v_new, states = _gdn_scan_fwd(u_loc, l1, l2, g_last, bsz, lp)
    o, v_new = o.reshape(rows, GW), v_new.reshape(rows, GW)
    _, l_rest = _split_wait("gather_rest_arrive", grp_rest, _chipcast_pattern, o)
    (grp_out, grp_ffn, grp_down), token = _split_start(
        "gather_rest_forward", [l_rest[0:1], l_rest[1:3], l_rest[3:4]], _forward_pattern, None)
    mix_g, mix_s = _mix_heads(proj, o, gdn_norm + token[0:1, :], convs_full, bsz, lp, padf)
    mixin = jnp.concatenate([mix_g, mix_s], axis=1)
    _, (l_out,) = _split_wait("gather_w_out_wait", grp_out, _forward_pattern, mixin)
    wout = l_out.reshape(D, D)
    (mix,) = _mm("proj_out", [(mixin, wout)], "nn", tm, 512, D)
    h1, u2 = _mid_norms(h0, mix, mix_post_norm, ffn_pre_norm, bsz, lp, tr)
    _, (l_gate, l_up) = _split_wait("gather_ffn_wait", grp_ffn, _forward_pattern, u2)
    wgate_t, wup_t = l_gate.reshape(FF, D), l_up.reshape(FF, D)
    gate, up, act = _ffn_up(u2, wgate_t, wup_t, _div_tile(rows, 528), 1408)
    _, (l_down,) = _split_wait("gather_down_wait", grp_down, _forward_pattern, act)
    wdown = l_down.reshape(FF, D)
    (ffn,) = _mm("ffn_down", [(act, wdown)], "nn", tm, 512, 1408)

    dy, dffn, d_g4, loss_part = _loss_head(h1, ffn, loss_target, ffn_post_norm, bsz, lp, tr, first_real)
    (dwdown,) = _mm("dw_down", [(act, dffn)], "tn", 1408, 512, tm, out_dtypes=(BF16,))

    def swiglu_bwd(da, g, u):
        g, u = g.astype(F32), u.astype(F32)
        return da * u * _dsilu(g), da * _silu(g)

    dgate, dup = _mm("d_act", [(dffn, wdown)], "nt", tm, 1408, D, out_dtypes=(BF16, BF16),
                     epilogue=swiglu_bwd, extras=(gate, up))
    (dwgate_t,) = _mm("dw_gate", [(dgate, u2)], "tn", 1408, D, tm, out_dtypes=(BF16,))
    (dwup_t,) = _mm("dw_up", [(dup, u2)], "tn", 1408, D, tm, out_dtypes=(BF16,))
    (du2,) = _mm("d_u2", [(dgate, wgate_t), (dup, wup_t)], "nn", tm, D, 1408)
    dh1, dmix, d_g3, d_g2 = _mid_norms_bwd(h1, mix, du2, dy, ffn_pre_norm, mix_post_norm, bsz, lp, tr)
    (dwout,) = _mm("dw_out", [(mixin, dmix)], "tn", D, 512, tm, out_dtypes=(BF16,))

    def row_blocks(g):
        return g.reshape(4, 2, -1, g.shape[1])

    core = jnp.reshape(cc, (1,)).astype(jnp.int32)
    chip = jnp.reshape(2 * cx + cy, (1,)).astype(jnp.int32)

    def reduce_start(tag, grads):
        (grp,), tok = _split_start("rs_sibling_start_" + tag, [grads], _sibling_pattern, lambda s: s[:1] + s[2:])
        return grp, tok

    def reduce_mid(tag, grp, nms, after):
        mine, lands = _split_wait("rs_sibling_wait_" + tag, grp, _sibling_pattern, after)
        parts = [_pair_sum("pair_sum_" + nm, g, r, core) for nm, g, r in zip(nms, mine, lands)]
        (grp2,), tok = _split_start("rs_chip_start_" + tag, [parts], _chip_pattern, lambda s: (3,) + s[1:])
        return grp2, tok

    def reduce_end(tag, grp2, after):
        return _split_wait("rs_chip_wait_" + tag, grp2, _chip_pattern, after)

    names_ffn = ["w_down", "w_gate", "w_up", "w_out"]
    rs_ffn, tok = reduce_start("ffn", [row_blocks(dwdown), row_blocks(dwgate_t), row_blocks(dwup_t), row_blocks(dwout)])
    (dmixin,) = _mm("d_mixin", [(dmix, wout)], "nt", tm, 512, D, after=tok)
    d_o, dz, dsx, dsb, dsc, d_gn, d_convs = _mix_heads_bwd(proj, o, gdn_norm, convs_full, dmixin, bsz, lp, padf)
    rs_ffn, tok = reduce_mid("ffn", rs_ffn, names_ffn, d_o)
    dv_new, dstates = _gdn_scan_bwd(d_o, l3, w_t, k_dec, g_last + tok[0, 0], bsz, lp)
    dq, dk, dv, dg_b, dbeta_b = _gdn_local_bwd(qkv, gcb, bb, gct, tinv_all, states, dstates, v_new,
                                               dv_new.reshape(rows, GW), d_o, bsz, lp)
    dpqkv, d_convq = _gdn_prep_bwd(proj, convq_full, dq, dk, dv, bsz, lp, padf)
    dba, d_scal = _gates_bwd(ba, alog_v, dtb_v, dbeta_b, dg_b, gb, bsz, lp, padf)
    dproj = jnp.concatenate([dpqkv, dz, dba, dsx, dsb, dsc], axis=1)

    d_alog = d_scal[0:1, HEADS:2 * HEADS]
    d_dtb = d_scal[1:2, HEADS:2 * HEADS]
    cq_w, cs_w = 3 * GW // N_DEV, SW // N_DEV
    slab = jnp.stack([
        _pack_small([jnp.zeros_like(d_g2), d_g2, d_g3, d_g4], jnp.zeros((N_META, DH), F32),
                    d_convq[:, p * cq_w:(p + 1) * cq_w], d_convs[:, p * cs_w:(p + 1) * cs_w], d_alog, d_dtb, d_gn,
                    loss_part) for p in range(N_DEV)])
    (grp_small,), tok = _split_start("small_grads_start", [[slab]], _alltoall_pattern, lambda s: s)
    (dwin_t,) = _mm("dw_in", [(dproj, u1)], "tn", 1280, D, tm, out_dtypes=(BF16,), after=tok)
    parts_ffn, recv_ffn = reduce_end("ffn", rs_ffn, dwin_t)
    dwin_t = jnp.concatenate([dwin_t[:BA0 + 2 * HEADS], dwin_t[SX0:]], axis=0)
    rs_in, tok = reduce_start("in", [row_blocks(dwin_t)])

    big_w = {"w_in": (w_in, m_w_in, v_w_in), "w_out": (w_out, m_w_out, v_w_out), "w_gate": (w_gate, m_w_gate, v_w_gate),
             "w_up": (w_up, m_w_up, v_w_up), "w_down": (w_down, m_w_down, v_w_down)}

    def update(nm, parts, recv, after=None):
        w, m, v = big_w[nm]
        if nm in ("w_out", "w_down"):
            return _adamw_reduced("adamw_" + nm, parts, chip, recv, w[0], m[0], v[0], after)
        res = _adamw_reduced("adamw_" + nm, parts, chip, recv, w[0].T, m[0].T, v[0].T, after)
        return [t.T for t in res]

    big_out = {nm: update(nm, p, r, tok) for nm, p, r in zip(names_ffn, parts_ffn, recv_ffn)}
    ffn_done = sum(big_out[nm][1][:1, :1] for nm in names_ffn)
    rs_in, tok = reduce_mid("in", rs_in, ["w_in"], ffn_done)
    (du1,) = _mm("d_u1", [(dproj, win_t)], "nn", tm_big, D, 768, after=tok)
    grad_x, d_g1, d_meta = _pre_norm_bwd(h0, du1, dh1, mix_pre_norm, bsz, lp, seq, padf)

    late = jnp.stack([
        jnp.concatenate([jnp.pad(d_g1, ((0, 7), (0, 0))), jnp.pad(d_meta[:, p * DH:(p + 1) * DH], ((0, 0), (0, D - DH)))],
                        axis=0) for p in range(N_DEV)])
    (grp_late,), _ = _split_start("late_grads_start", [[late]], _alltoall_pattern, lambda s: s)
    _, (slabs_local,) = _split_wait("small_grads_wait", grp_small, _alltoall_pattern, grad_x)
    _, (late_local,) = _split_wait("late_grads_wait", grp_late, _alltoall_pattern, slabs_local)
    w_small = _pack_small([mix_pre_norm, mix_post_norm, ffn_pre_norm, ffn_post_norm], meta_tokens, conv_qkv[0],
                          conv_sc[0], a_log, dt_bias, gdn_norm)
    m_small = _pack_small([m_mix_pre_norm, m_mix_post_norm, m_ffn_pre_norm, m_ffn_post_norm], m_meta_tokens,
                          m_conv_qkv[0], m_conv_sc[0], m_a_log, m_dt_bias, m_gdn_norm)
    v_small = _pack_small([v_mix_pre_norm, v_mix_post_norm, v_ffn_pre_norm, v_ffn_post_norm], v_meta_tokens,
                          v_conv_qkv[0], v_conv_sc[0], v_a_log, v_dt_bias, v_gdn_norm)
    small_out, loss = _adamw_small(slabs_local, late_local, w_small, m_small, v_small)
    parts_in, recv_in = reduce_end("in", rs_in, small_out[1]["gdn_norm"])
    big_out["w_in"] = update("w_in", parts_in[0], recv_in[0])
    names = ["w_in", "w_out", "w_gate", "w_up", "w_down"]

    order = ["meta_tokens", "mix_pre_norm", "mix_post_norm", "ffn_pre_norm", "ffn_post_norm", "w_in", "conv_qkv",
             "a_log", "dt_bias", "gdn_norm", "conv_sc", "w_out", "w_gate", "w_up", "w_down"]
    outs = [loss[0, 0], grad_x]
    for kind in range(4):
        leaves = dict(small_out[kind])
        for nm in names:
            leaves[nm] = big_out[nm][kind][None]
        outs += [leaves[nm] for nm in order]
    return tuple(outs)
```

```python
import jax
import jax.numpy as jnp
from jax import lax
from jax.experimental import pallas as pl
from jax.experimental.pallas import tpu as pltpu

F32 = jnp.float32
BF16 = jnp.bfloat16
MESH = pl.DeviceIdType.MESH

D = 1024
N_META = 16
HEADS = 4
DH = 128
GW = HEADS * DH
SW = D - GW
FF = 2816
CH = 64
KQ = 4
KS = 3
EPS = 1e-6
IN_W = 3 * GW + GW + 2 * HEADS + 3 * SW
Q0, Z0, BA0, SX0, SB0, SC0 = 0, 1536, 2048, 2304, 2816, 3328
BA_W = SX0 - BA0
N_DEV = 8
VMEM_LIMIT = 56 * 1024 * 1024

ADAM_LR, ADAM_B1, ADAM_B2, ADAM_EPS, ADAM_WD, ADAM_STEP = 0.001, 0.9, 0.999, 1e-08, 0.01, 10


def _cparams(sem):
    return pltpu.CompilerParams(dimension_semantics=sem, vmem_limit_bytes=VMEM_LIMIT)


def _div_tile(n, target, mult=16):
    best = None
    for t in range(mult, min(n, target) + 1, mult):
        if n % t == 0:
            best = t
    assert best is not None, (n, target)
    return best


def _silu(x):
    return x * jax.nn.sigmoid(x)


def _dsilu(x):
    s = jax.nn.sigmoid(x)
    return s * (1.0 + x * (1.0 - s))


def _bdot(a, b, dims):
    return lax.dot_general(a.astype(BF16), b.astype(BF16), (dims, ((), ())), preferred_element_type=F32)


NN = ((1,), (0,))
NT = ((1,), (1,))
TN = ((0,), (0,))


def _mm(name, pairs, mode, tm, tn, tk, out_dtypes=(F32,), epilogue=None, extras=(), after=None):
    a0, b0 = pairs[0]
    if mode == "nn":
        (m, k), n = a0.shape, b0.shape[1]
    elif mode == "nt":
        (m, k), n = a0.shape, b0.shape[0]
    else:
        (k, m), n = a0.shape, b0.shape[1]
    assert m % tm == 0 and n % tn == 0 and k % tk == 0, (name, m, n, k, tm, tn, tk)
    nk = k // tk
    npair = len(pairs)
    nex = len(extras)
    nout = len(out_dtypes)
    ntok = 0 if after is None else 1
    dims = {"nn": NN, "nt": NT, "tn": TN}[mode]

    def body(*refs):
        ab = refs[:2 * npair]
        ex = refs[2 * npair:2 * npair + nex]
        outs = refs[2 * npair + nex + ntok:2 * npair + nex + ntok + nout]
        acc_ref = refs[-1]
        kk = pl.program_id(2)

        part = None
        for p in range(npair):
            d = lax.dot_general(ab[2 * p][...], ab[2 * p + 1][...], (dims, ((), ())), preferred_element_type=F32)
            part = d if part is None else part + d

        def finish(acc):
            res = epilogue(acc, *[e[...] for e in ex]) if epilogue is not None else (acc,)
            for o, r in zip(outs, res):
                o[...] = r.astype(o.dtype)

        if nk == 1:
            finish(part)
        else:
            @pl.when(kk == 0)
            def _():
                acc_ref[...] = part

            @pl.when(kk > 0)
            def _():
                acc_ref[...] += part

            @pl.when(kk == nk - 1)
            def _():
                finish(acc_ref[...])

    if mode == "nn":
        a_spec = pl.BlockSpec((tm, tk), lambda i, j, q: (i, q))
        b_spec = pl.BlockSpec((tk, tn), lambda i, j, q: (q, j))
    elif mode == "nt":
        a_spec = pl.BlockSpec((tm, tk), lambda i, j, q: (i, q))
        b_spec = pl.BlockSpec((tn, tk), lambda i, j, q: (j, q))
    else:
        a_spec = pl.BlockSpec((tk, tm), lambda i, j, q: (q, i))
        b_spec = pl.BlockSpec((tk, tn), lambda i, j, q: (q, j))
    o_spec = pl.BlockSpec((tm, tn), lambda i, j, q: (i, j))
    flat = [t for pr in pairs for t in pr]
    res = pl.pallas_call(
        body, name=name,
        grid=(m // tm, n // tn, nk),
        in_specs=[a_spec, b_spec] * npair + [o_spec] * nex + [pl.BlockSpec((8, DH), lambda i, j, q: (0, 0))] * ntok,
        out_specs=[o_spec] * nout,
        out_shape=[jax.ShapeDtypeStruct((m, n), dt) for dt in out_dtypes],
        scratch_shapes=[pltpu.VMEM((tm, tn) if nk > 1 else (8, DH), F32)],
        compiler_params=_cparams(("parallel", "parallel", "arbitrary")),
    )(*flat, *extras, *([] if after is None else [after]))
    return res


def _ffn_up(u2, wg, wu, tm, tn):
    m, k = u2.shape
    n = wg.shape[0]

    def body(a_ref, g_ref, u_ref, gate_ref, up_ref, act_ref):
        a = a_ref[...]
        g = lax.dot_general(a, g_ref[...], (NT, ((), ())), preferred_element_type=F32)
        u = lax.dot_general(a, u_ref[...], (NT, ((), ())), preferred_element_type=F32)
        gate_ref[...] = g.astype(BF16)
        up_ref[...] = u.astype(BF16)
        act_ref[...] = (_silu(g) * u).astype(BF16)

    o_spec = pl.BlockSpec((tm, tn), lambda j, i: (i, j))
    w_spec = pl.BlockSpec((tn, k), lambda j, i: (j, 0))
    return pl.pallas_call(
        body, name="ffn_up", grid=(n // tn, m // tm),
        in_specs=[pl.BlockSpec((tm, k), lambda j, i: (i, 0)), w_spec, w_spec],
        out_specs=[o_spec, o_spec, o_spec],
        out_shape=[jax.ShapeDtypeStruct((m, n), BF16)] * 3,
        compiler_params=_cparams(("parallel", "parallel")),
    )(u2, wg, wu)


def _rms(x, gain):
    r = lax.rsqrt(jnp.mean(x * x, axis=-1, keepdims=True) + EPS)
    return x * r * gain, r


def _rms_bwd(x, gain, dy):
    r = lax.rsqrt(jnp.mean(x * x, axis=-1, keepdims=True) + EPS)
    dyw = dy * gain
    dx = r * dyw - x * (r * r * r) * jnp.mean(dyw * x, axis=-1, keepdims=True)
    return dx, dy * x * r


def _row_specs(tr, width, nb):
    return pl.BlockSpec((tr, width), lambda b, j: (b * nb + j, 0))


def _vec_spec(width):
    return pl.BlockSpec((1, width), lambda b, j: (0, 0))


def _first_step(b, j):
    return jnp.logical_and(b == 0, j == 0)


def _pre_norm(x, head_rows, gain, bsz, lp, tr):
    nb = lp // tr
    first_real = head_rows.shape[0]
    assert first_real % 8 == 0 and first_real < tr
    x_spec = pl.BlockSpec((pl.Squeezed(), pl.Element(tr), pl.Element(D)),
                          lambda b, j: (b, pl.multiple_of(jnp.maximum(j * tr - first_real, 0), 8), 0))

    def body(x_ref, hd_ref, g_ref, h_ref, u_ref):
        j = pl.program_id(1)
        xt = x_ref[...]
        h = jnp.where(j == 0, jnp.concatenate([hd_ref[...], xt[:tr - first_real]], axis=0), xt)
        h_ref[...] = h
        u_ref[...] = _rms(h, g_ref[...])[0].astype(BF16)

    rs = _row_specs(tr, D, nb)
    return pl.pallas_call(
        body, name="mix_pre_norm", grid=(bsz, nb),
        in_specs=[x_spec, pl.BlockSpec((first_real, D), lambda b, j: (0, 0)), _vec_spec(D)],
        out_specs=[rs, rs],
        out_shape=[jax.ShapeDtypeStruct((bsz * lp, D), F32), jax.ShapeDtypeStruct((bsz * lp, D), BF16)],
        compiler_params=_cparams(("parallel", "parallel")),
    )(x, head_rows, gain)


def _mid_norms(h0, mix, g_post, g_pre, bsz, lp, tr):
    nb = lp // tr

    def body(h_ref, m_ref, gp_ref, gq_ref, h1_ref, u2_ref):
        h1 = h_ref[...] + _rms(m_ref[...], gp_ref[...])[0]
        h1_ref[...] = h1
        u2_ref[...] = _rms(h1, gq_ref[...])[0].astype(BF16)

    rs = _row_specs(tr, D, nb)
    return pl.pallas_call(
        body, name="mid_norms", grid=(bsz, nb),
        in_specs=[rs, rs, _vec_spec(D), _vec_spec(D)],
        out_specs=[rs, rs],
        out_shape=[jax.ShapeDtypeStruct(h0.shape, F32), jax.ShapeDtypeStruct(h0.shape, BF16)],
        compiler_params=_cparams(("parallel", "parallel")),
    )(h0, mix, g_post, g_pre)


def _loss_head(h1, ffn, tgt, g_post, bsz, lp, tr, first_real):
    nb = lp // tr
    assert first_real % 8 == 0 and first_real < tr
    tgt_spec = pl.BlockSpec((pl.Squeezed(), pl.Element(tr), pl.Element(D)),
                            lambda b, j: (b, pl.multiple_of(jnp.maximum(j * tr - first_real, 0), 8), 0))

    def body(h_ref, f_ref, t_ref, g_ref, dy_ref, dffn_ref, dg_ref, loss_ref):
        b, j = pl.program_id(0), pl.program_id(1)
        f = f_ref[...]
        g = g_ref[...]
        y = h_ref[...] + _rms(f, g)[0]
        t = j * tr + lax.broadcasted_iota(jnp.int32, (tr, 1), 0)
        tg = t_ref[...]
        tg = jnp.where(j == 0, jnp.concatenate([tg[tr - first_real:], tg[:tr - first_real]], axis=0), tg)
        err = jnp.where(t >= first_real, y - tg, 0.0)
        dy = err * (1.0 / D)
        dy_ref[...] = dy
        dx, dgc = _rms_bwd(f, g, dy)
        dffn_ref[...] = dx.astype(BF16)
        lsum = jnp.sum(jnp.sum(err * err, axis=0, keepdims=True), axis=1, keepdims=True) * (0.5 / D)

        @pl.when(_first_step(b, j))
        def _():
            dg_ref[...] = jnp.zeros_like(dg_ref)
            loss_ref[...] = jnp.zeros_like(loss_ref)

        dg_ref[...] += jnp.sum(dgc, axis=0, keepdims=True)
        loss_ref[...] += jnp.broadcast_to(lsum, loss_ref.shape)

    rs = _row_specs(tr, D, nb)
    return pl.pallas_call(
        body, name="loss_head", grid=(bsz, nb),
        in_specs=[rs, rs, tgt_spec, _vec_spec(D)],
        out_specs=[rs, rs, _vec_spec(D), _vec_spec(DH)],
        out_shape=[jax.ShapeDtypeStruct(h1.shape, F32), jax.ShapeDtypeStruct(h1.shape, BF16),
                   jax.ShapeDtypeStruct((1, D), F32), jax.ShapeDtypeStruct((1, DH), F32)],
        compiler_params=_cparams(("arbitrary", "arbitrary")),
    )(h1, ffn, tgt, g_post)


def _mid_norms_bwd(h1, mix, du2, dy, g_pre, g_post, bsz, lp, tr):
    nb = lp // tr

    def body(h_ref, m_ref, du_ref, dy_ref, gq_ref, gp_ref, dh1_ref, dmix_ref, dgq_ref, dgp_ref):
        b, j = pl.program_id(0), pl.program_id(1)
        dx, dgq = _rms_bwd(h_ref[...], gq_ref[...], du_ref[...])
        dh1 = dy_ref[...] + dx
        dh1_ref[...] = dh1
        dm, dgp = _rms_bwd(m_ref[...], gp_ref[...], dh1)
        dmix_ref[...] = dm.astype(BF16)

        @pl.when(_first_step(b, j))
        def _():
            dgq_ref[...] = jnp.zeros_like(dgq_ref)
            dgp_ref[...] = jnp.zeros_like(dgp_ref)

        dgq_ref[...] += jnp.sum(dgq, axis=0, keepdims=True)
        dgp_ref[...] += jnp.sum(dgp, axis=0, keepdims=True)

    rs = _row_specs(tr, D, nb)
    return pl.pallas_call(
        body, name="mid_norms_bwd", grid=(bsz, nb),
        in_specs=[rs, rs, rs, rs, _vec_spec(D), _vec_spec(D)],
        out_specs=[rs, rs, _vec_spec(D), _vec_spec(D)],
        out_shape=[jax.ShapeDtypeStruct(h1.shape, F32), jax.ShapeDtypeStruct(h1.shape, BF16),
                   jax.ShapeDtypeStruct((1, D), F32), jax.ShapeDtypeStruct((1, D), F32)],
        compiler_params=_cparams(("arbitrary", "arbitrary")),
    )(h1, mix, du2, dy, g_pre, g_post)


def _pre_norm_bwd(h0, du1, dh1, gain, bsz, lp, seq, padf):
    first_real = lp - seq
    tx = _div_tile(seq, 512)
    assert first_real % 8 == 0 and lp % first_real == 0

    def tokens(h_ref, du_ref, dh1_ref, g_ref, gx_ref, dg_ref):
        b, j = pl.program_id(0), pl.program_id(1)
        dx, dgc = _rms_bwd(h_ref[...], g_ref[...], du_ref[...])
        gx_ref[...] = dh1_ref[...] + dx

        @pl.when(_first_step(b, j))
        def _():
            dg_ref[...] = jnp.zeros_like(dg_ref)

        dg_ref[...] += jnp.sum(dgc, axis=0, keepdims=True)

    window = pl.BlockSpec((pl.Element(tx), pl.Element(D)),
                          lambda b, j: (pl.multiple_of(b * lp + first_real + j * tx, 8), 0))
    grad_x, dg_tokens = pl.pallas_call(
        tokens, name="pre_norm_bwd", grid=(bsz, seq // tx),
        in_specs=[window, window, window, _vec_spec(D)],
        out_specs=[pl.BlockSpec((pl.Squeezed(), tx, D), lambda b, j: (b, j, 0)), _vec_spec(D)],
        out_shape=[jax.ShapeDtypeStruct((bsz, seq, D), F32), jax.ShapeDtypeStruct((1, D), F32)],
        compiler_params=_cparams(("arbitrary", "arbitrary")),
    )(h0, du1, dh1, gain)

    def head(h_ref, du_ref, dh1_ref, g_ref, dgt_ref, dg_ref, dmeta_ref):
        b = pl.program_id(0)
        dx, dgc = _rms_bwd(h_ref[...], g_ref[...], du_ref[...])
        dh0 = dh1_ref[...] + dx

        @pl.when(b == 0)
        def _():
            dg_ref[...] = dgt_ref[...]
            dmeta_ref[...] = jnp.zeros_like(dmeta_ref)

        dg_ref[...] += jnp.sum(dgc, axis=0, keepdims=True)
        dmeta_ref[...] += dh0[padf:padf + N_META, :]

    rows = pl.BlockSpec((first_real, D), lambda b: (b * (lp // first_real), 0))
    vec = pl.BlockSpec((1, D), lambda b: (0, 0))
    dg, dmeta = pl.pallas_call(
        head, name="pre_norm_bwd_head", grid=(bsz,),
        in_specs=[rows, rows, rows, vec, vec],
        out_specs=[vec, pl.BlockSpec((N_META, D), lambda b: (0, 0))],
        out_shape=[jax.ShapeDtypeStruct((1, D), F32), jax.ShapeDtypeStruct((N_META, D), F32)],
        compiler_params=_cparams(("arbitrary",)),
    )(h0, du1, dh1, gain, dg_tokens)
    return grad_x, dg, dmeta


def _shift_down(x, s, lp):
    return x if s == 0 else pltpu.roll(x, s, axis=0)


def _shift_up(x, s, lp):
    return x if s == 0 else pltpu.roll(x, lp - s, axis=0)


def _conv_fwd(x, w, taps, lp):
    y = None
    for i in range(taps):
        term = _shift_down(x, taps - 1 - i, lp) * w[i:i + 1, :]
        y = term if y is None else y + term
    return y


def _conv_bwd(x, w, dy, taps, lp):
    dx = None
    dws = []
    for i in range(taps):
        s = taps - 1 - i
        term = _shift_up(dy, s, lp) * w[i:i + 1, :]
        dx = term if dx is None else dx + term
        dws.append(jnp.sum(dy * _shift_down(x, s, lp), axis=0, keepdims=True))
    return dx, jnp.concatenate(dws, axis=0)


def _slab(lp, col0):
    return pl.BlockSpec((lp, DH), lambda b, s: (b, col0 // DH + s))


def _qkv_act(c, s):
    y = _silu(c)
    r = lax.rsqrt(jnp.sum(y * y, axis=-1, keepdims=True) + EPS)
    scale = jnp.where(s < HEADS, DH ** -0.5, 1.0)
    return y, r, scale


def _gdn_prep(proj, conv_qkv, bsz, lp, padf):
    def body(p_ref, w_ref, o_ref):
        s = pl.program_id(1)
        c = _conv_fwd(p_ref[...].astype(F32), w_ref[...], KQ, lp)
        y, r, scale = _qkv_act(c, s)
        out = jnp.where(s < 2 * HEADS, y * (r * scale), y)
        t = lax.broadcasted_iota(jnp.int32, (lp, 1), 0)
        o_ref[...] = jnp.where(t >= padf, out, 0.0)

    return pl.pallas_call(
        body, name="gdn_prep", grid=(bsz, 3 * HEADS),
        in_specs=[_slab(lp, Q0), pl.BlockSpec((KQ, DH), lambda b, s: (0, s))],
        out_specs=_slab(lp, 0),
        out_shape=jax.ShapeDtypeStruct((bsz * lp, 3 * GW), F32),
        compiler_params=_cparams(("parallel", "parallel")),
    )(proj, conv_qkv)


def _gdn_prep_bwd(proj, conv_qkv, dq, dk, dv, bsz, lp, padf):
    def body(p_ref, w_ref, dq_ref, dk_ref, dv_ref, dp_ref, dw_ref):
        b, s = pl.program_id(0), pl.program_id(1)
        x = p_ref[...].astype(F32)
        w = w_ref[...]
        c = _conv_fwd(x, w, KQ, lp)
        y, r, scale = _qkv_act(c, s)
        d = jnp.where(s < HEADS, dq_ref[...], jnp.where(s < 2 * HEADS, dk_ref[...], dv_ref[...]))
        dn = d * scale
        n = y * r
        dy_norm = r * (dn - n * jnp.sum(dn * n, axis=-1, keepdims=True))
        dyy = jnp.where(s < 2 * HEADS, dy_norm, dn)
        t = lax.broadcasted_iota(jnp.int32, (lp, 1), 0)
        dc = jnp.where(t >= padf, dyy * _dsilu(c), 0.0)
        dx, dw = _conv_bwd(x, w, dc, KQ, lp)
        dp_ref[...] = jnp.where(t >= padf, dx, 0.0).astype(BF16)

        @pl.when(jnp.logical_and(b == 0, s == 0))
        def _():
            dw_ref[...] = jnp.zeros_like(dw_ref)

        col_block = jnp.right_shift(lax.broadcasted_iota(jnp.int32, (KQ, 3 * GW), 1), DH.bit_length() - 1)
        dw_ref[...] += jnp.where(col_block == s, jnp.tile(dw, (1, 3 * HEADS)), 0.0)

    sl = lambda col0: pl.BlockSpec((lp, DH), lambda b, s: (b, col0 // DH + s))
    part = lambda k: pl.BlockSpec((lp, DH), lambda b, s: (b, jnp.clip(s - k * HEADS, 0, HEADS - 1)))
    return pl.pallas_call(
        body, name="gdn_prep_bwd", grid=(bsz, 3 * HEADS),
        in_specs=[sl(Q0), pl.BlockSpec((KQ, DH), lambda b, s: (0, s)), part(0), part(1), part(2)],
        out_specs=[sl(0), pl.BlockSpec((KQ, 3 * GW), lambda b, s: (0, 0))],
        out_shape=[jax.ShapeDtypeStruct((bsz * lp, 3 * GW), BF16), jax.ShapeDtypeStruct((KQ, 3 * GW), F32)],
        compiler_params=_cparams(("arbitrary", "arbitrary")),
    )(proj, conv_qkv, dq, dk, dv)


def _softplus(x):
    e = jnp.exp(-jnp.abs(x))
    one_e = 1.0 + e
    l1p = jnp.where(one_e == 1.0, e, jnp.log(one_e) * (e / (one_e - 1.0)))
    return jnp.maximum(x, 0.0) + l1p


def _tri(n, kind):
    i = lax.broadcasted_iota(jnp.int32, (n, n), 0)
    j = lax.broadcasted_iota(jnp.int32, (n, n), 1)
    return {"incl": i >= j, "strict": i > j, "upper": i <= j}[kind]


def _gates(proj, alog_v, dtb_v, bsz, lp, padf):
    nchunk = lp // CH

    def body(p_ref, al_ref, dt_ref, gb_ref, cs_ref):
        x = p_ref[...].astype(F32)
        lane = lax.broadcasted_iota(jnp.int32, (lp, DH), 1)
        t = lax.broadcasted_iota(jnp.int32, (lp, DH), 0)
        beta = jax.nn.sigmoid(x)
        g = -jnp.exp(al_ref[...]) * _softplus(x + dt_ref[...])
        gb = jnp.where(lane < HEADS, beta, jnp.where(lane < 2 * HEADS, g, 0.0))
        gb_ref[...] = jnp.where(t >= padf, gb, 0.0)
        ltri = _tri(CH, "incl").astype(BF16)
        rest = [gb_ref[pl.ds(n * CH, CH), :] for n in range(nchunk)]
        sums = None
        for _ in range(3):
            piece = [v.astype(BF16) for v in rest]
            rest = [v - p.astype(F32) for v, p in zip(rest, piece)]
            part = [jnp.dot(ltri, p, preferred_element_type=F32) for p in piece]
            sums = part if sums is None else [a + b for a, b in zip(sums, part)]
        for n in range(nchunk):
            cs_ref[pl.ds(n * CH, CH), :] = sums[n]

    blk = pl.BlockSpec((lp, DH), lambda b: (b, 0))
    vec = pl.BlockSpec((1, DH), lambda b: (0, 0))
    out = pl.BlockSpec((lp, DH), lambda b: (b, 0))
    return pl.pallas_call(
        body, name="gates", grid=(bsz,),
        in_specs=[blk, vec, vec], out_specs=[out, out],
        out_shape=[jax.ShapeDtypeStruct((bsz * lp, DH), F32)] * 2,
        compiler_params=_cparams(("parallel",)),
    )(proj, alog_v, dtb_v)


def _gates_bwd(proj, alog_v, dtb_v, dbeta_b, dg_b, gb, bsz, lp, padf):
    def body(p_ref, al_ref, dt_ref, db_ref, dg_ref, gb_ref, dl_ref, dv_ref):
        b = pl.program_id(0)
        x = p_ref[...].astype(F32)
        lane = lax.broadcasted_iota(jnp.int32, (lp, DH), 1)
        t = lax.broadcasted_iota(jnp.int32, (lp, DH), 0)
        dbeta = jnp.zeros((lp, DH), F32)
        dg = jnp.zeros((lp, DH), F32)
        for h in range(HEADS):
            dbeta = jnp.where(lane == h, db_ref[h], dbeta)
            dg = jnp.where(lane == HEADS + h, dg_ref[h], dg)
        real = t >= padf
        dbeta = jnp.where(real, dbeta, 0.0)
        dg = jnp.where(real, dg, 0.0)
        sb = jax.nn.sigmoid(x)
        neg_ea = -jnp.exp(al_ref[...])
        dsp = dg * neg_ea * jax.nn.sigmoid(x + dt_ref[...])
        dl = dbeta * sb * (1.0 - sb) + dsp
        dl_ref[...] = jnp.concatenate([dl, jnp.zeros((lp, BA_W - DH), F32)], axis=1).astype(BF16)
        dalog = jnp.sum(dg * gb_ref[...], axis=0, keepdims=True)
        ddt = jnp.sum(dsp, axis=0, keepdims=True)

        @pl.when(b == 0)
        def _():
            dv_ref[...] = jnp.zeros_like(dv_ref)

        dv_ref[...] += jnp.concatenate([dalog, ddt], axis=0)

    blk = pl.BlockSpec((lp, DH), lambda b: (b, 0))
    vec = pl.BlockSpec((1, DH), lambda b: (0, 0))
    hb = pl.BlockSpec((HEADS, lp, DH), lambda b: (b, 0, 0))
    return pl.pallas_call(
        body, name="gates_bwd", grid=(bsz,),
        in_specs=[blk, vec, vec, hb, hb, pl.BlockSpec((lp, DH), lambda b: (b, 0))],
        out_specs=[pl.BlockSpec((lp, BA_W), lambda b: (b, 0)), pl.BlockSpec((2, DH), lambda b: (0, 0))],
        out_shape=[jax.ShapeDtypeStruct((bsz * lp, BA_W), BF16), jax.ShapeDtypeStruct((2, DH), F32)],
        compiler_params=_cparams(("arbitrary",)),
    )(proj, alog_v, dtb_v, dbeta_b, dg_b, gb)


def _mix_heads(proj, o, gdn_norm, conv_sc, bsz, lp, padf):
    def body(o_ref, z_ref, gn_ref, sx_ref, sb_ref, sc_ref, w_ref, og_ref, os_ref):
        og_ref[...] = (_rms(o_ref[...], gn_ref[...])[0] * _silu(z_ref[...].astype(F32))).astype(BF16)
        conv = _conv_fwd(sc_ref[...].astype(F32) * sx_ref[...].astype(F32), w_ref[...], KS, lp)
        t = lax.broadcasted_iota(jnp.int32, (lp, 1), 0)
        os_ref[...] = jnp.where(t >= padf, sb_ref[...].astype(F32) * conv, 0.0).astype(BF16)

    half = _slab(lp, 0)
    return pl.pallas_call(
        body, name="mix_heads", grid=(bsz, HEADS),
        in_specs=[half, _slab(lp, Z0), pl.BlockSpec((1, DH), lambda b, s: (0, 0)),
                  _slab(lp, SX0), _slab(lp, SB0), _slab(lp, SC0), pl.BlockSpec((KS, DH), lambda b, s: (0, s))],
        out_specs=[half, half],
        out_shape=[jax.ShapeDtypeStruct((bsz * lp, GW), BF16)] * 2,
        compiler_params=_cparams(("parallel", "parallel")),
    )(o, proj, gdn_norm, proj, proj, proj, conv_sc)


def _mix_heads_bwd(proj, o, gdn_norm, conv_sc, dmixin, bsz, lp, padf):
    def body(o_ref, z_ref, gn_ref, sx_ref, sb_ref, sc_ref, w_ref, dg_ref, ds_ref,
             do_ref, dz_ref, dsx_ref, dsb_ref, dsc_ref, dgn_ref, dw_ref):
        s, b = pl.program_id(0), pl.program_id(1)
        t = lax.broadcasted_iota(jnp.int32, (lp, 1), 0)
        real = t >= padf
        o, z, gn, d = o_ref[...], z_ref[...].astype(F32), gn_ref[...], dg_ref[...]
        sz = _silu(z)
        on = _rms(o, gn)[0]
        dz_ref[...] = (d * on * _dsilu(z)).astype(BF16)
        dox, dgn = _rms_bwd(o, gn, d * sz)
        do_ref[...] = dox
        sx, sb, sc = sx_ref[...].astype(F32), sb_ref[...].astype(F32), sc_ref[...].astype(F32)
        w, e = w_ref[...], ds_ref[...]
        e = jnp.where(real, e, 0.0)
        xin = sc * sx
        conv = _conv_fwd(xin, w, KS, lp)
        dsb_ref[...] = (e * conv).astype(BF16)
        dxin, dw = _conv_bwd(xin, w, e * sb, KS, lp)
        dxin = jnp.where(real, dxin, 0.0)
        dsx_ref[...] = (dxin * sc).astype(BF16)
        dsc_ref[...] = (dxin * sx).astype(BF16)

        @pl.when(jnp.logical_and(s == 0, b == 0))
        def _():
            dgn_ref[...] = jnp.zeros_like(dgn_ref)

        @pl.when(b == 0)
        def _():
            dw_ref[...] = jnp.zeros_like(dw_ref)

        dgn_ref[...] += jnp.sum(dgn, axis=0, keepdims=True)
        dw_ref[...] += dw

    sl = lambda col0: pl.BlockSpec((lp, DH), lambda s, b: (b, col0 // DH + s))
    half = sl(0)
    return pl.pallas_call(
        body, name="mix_heads_bwd", grid=(HEADS, bsz),
        in_specs=[half, sl(Z0), pl.BlockSpec((1, DH), lambda s, b: (0, 0)), sl(SX0), sl(SB0), sl(SC0),
                  pl.BlockSpec((KS, DH), lambda s, b: (0, s)), sl(0), sl(GW)],
        out_specs=[half] * 5 + [pl.BlockSpec((1, DH), lambda s, b: (0, 0)), pl.BlockSpec((KS, DH), lambda s, b: (0, s))],
        out_shape=[jax.ShapeDtypeStruct((bsz * lp, GW), F32)] + [jax.ShapeDtypeStruct((bsz * lp, GW), BF16)] * 4
        + [jax.ShapeDtypeStruct((1, DH), F32), jax.ShapeDtypeStruct((KS, SW), F32)],
        compiler_params=_cparams(("arbitrary", "arbitrary")),
    )(o, proj, gdn_norm, proj, proj, proj, conv_sc, dmixin, dmixin)


def _split_dot(a, b, dims=NN):
    ah = a.astype(BF16)
    al = (a - ah.astype(F32)).astype(BF16)
    bh = b.astype(BF16)
    bl = (b - bh.astype(F32)).astype(BF16)

    def d(x, y):
        return lax.dot_general(x, y, (dims, ((), ())), preferred_element_type=F32)

    return d(ah, bh) + (d(ah, bl) + d(al, bh))


def _eye(n):
    return (lax.broadcasted_iota(jnp.int32, (n, n), 0) == lax.broadcasted_iota(jnp.int32, (n, n), 1)).astype(F32)


def _each(f, *lists):
    return [f(*xs) for xs in zip(*lists)]


def _inv_unit_lower(a_list):
    eye = _eye(CH)
    p = _each(lambda a: -a, a_list)
    t = _each(lambda x: eye + x, p)
    p = _each(lambda x: _split_dot(x, x), p)
    for level in range(5):
        if level < 4:
            prod = _each(lambda tt, pp: _split_dot(jnp.concatenate([tt, pp], axis=0), pp), t, p)
            t = _each(lambda tt, pr: tt + pr[:CH], t, prod)
            p = _each(lambda pr: pr[CH:], prod)
        else:
            t = _each(lambda tt, pp: tt + _split_dot(tt, pp), t, p)
    return t


def _chunk_gates(gc_ref, be_ref, gt_ref, g):
    rows = pl.ds(g * CH, CH)
    gc = gc_ref[0, rows, :]
    be = be_ref[0, rows, :]
    gcl = gc_ref[0, pl.ds(g * CH + CH - 1, 1), :]
    gt = gt_ref[0, g]
    incl, upper = _tri(CH, "incl"), _tri(CH, "upper")
    dec = jnp.where(incl, jnp.exp(jnp.where(incl, gc[:, :CH] - gt, 0.0)), 0.0)
    dec_t = jnp.where(upper, jnp.exp(jnp.where(upper, gt - gc[:, :CH], 0.0)), 0.0)
    return rows, be, dec, dec_t, jnp.exp(gc), jnp.exp(gcl - gc), jnp.exp(gcl)


def _chunk_group(nchunk):
    return 3 if nchunk % 3 == 0 else 1


def _local_group(nchunk):
    return 11 if nchunk % 11 == 0 else _chunk_group(nchunk)


def _gdn_local_specs(lp, grp):
    ngrp = lp // (grp * CH)
    head = lambda col0: pl.BlockSpec((grp * CH, DH), lambda i, j: ((i // HEADS) * ngrp + j, col0 // DH + i % HEADS))
    lane_b = pl.BlockSpec((1, grp * CH, DH), lambda i, j: (i, j, 0))
    per_chunk = lambda r, c: pl.BlockSpec((1, grp, r, c), lambda i, j: (i, j, 0, 0))
    return ngrp, head, lane_b, per_chunk


def _gdn_local_fwd(qkv, gcb, bb, gct, bsz, lp):
    nchunk = lp // CH
    grp = _local_group(nchunk)
    ngrp, head, lane_b, per_chunk = _gdn_local_specs(lp, grp)
    bh = bsz * HEADS

    def body(q_ref, k_ref, v_ref, gc_ref, be_ref, gt_ref,
             u_ref, l1_ref, l2_ref, l3_ref, wt_ref, kd_ref, ti_ref, gl_ref):
        incl, strict, upper = _tri(CH, "incl"), _tri(CH, "strict"), _tri(CH, "upper")
        eye = _eye(CH)
        rows, be, dec, dec_t, eg, ekd, gl = zip(*[_chunk_gates(gc_ref, be_ref, gt_ref, g) for g in range(grp)])
        q = [q_ref[r, :] for r in rows]
        k = [k_ref[r, :] for r in rows]
        v = [v_ref[r, :] for r in rows]
        kb = _each(lambda x, y: x * y, k, be)
        a = _each(lambda x, y, d: jnp.where(strict, _bdot(x, y, NT) * d, 0.0), kb, k, dec)
        tinv = _inv_unit_lower(a)
        u = _each(lambda t, x, y: _bdot(t, x * y, NN), tinv, v, be)
        w = _each(lambda t, x, y: _bdot(t, x * y, NN), tinv, kb, eg)
        qk = _each(lambda x, y, d: jnp.where(incl, _bdot(x, y, NT) * d, 0.0), q, k, dec)
        qk_t = _each(lambda x, y, d: jnp.where(upper, _bdot(y, x, NT) * d, 0.0), q, k, dec_t)
        qd = _each(lambda x, y: x * y, q, eg)
        kd = _each(lambda x, y: x * y, k, ekd)
        kd_t = _each(lambda x: _bdot(x, eye, TN), kd)
        qd_t = _each(lambda x: _bdot(x, eye, TN), qd)
        w_t = _each(lambda x: _bdot(x, eye, TN), w)
        for g in range(grp):
            u_ref[rows[g], :] = u[g]
            l1_ref[0, g] = jnp.concatenate([w[g], qd[g]], axis=0).astype(BF16)
            l2_ref[0, g] = jnp.concatenate([qk[g], kd_t[g]], axis=0).astype(BF16)
            l3_ref[0, g] = jnp.concatenate([qk_t[g], qd_t[g]], axis=0).astype(BF16)
            wt_ref[0, g] = w_t[g].astype(BF16)
            kd_ref[0, g] = kd[g].astype(BF16)
            ti_ref[0, g] = tinv[g]
            gl_ref[0, g] = gl[g]

    r = bsz * lp
    shapes = [((r, GW), F32), ((bh, nchunk, 2 * CH, DH), BF16), ((bh, nchunk, 3 * CH, CH), BF16),
              ((bh, nchunk, 3 * CH, CH), BF16), ((bh, nchunk, DH, CH), BF16), ((bh, nchunk, CH, DH), BF16),
              ((bh, nchunk, CH, CH), F32), ((bh, nchunk, 1, DH), F32)]
    return pl.pallas_call(
        body, name="gdn_local_fwd", grid=(bh, ngrp),
        in_specs=[head(0), head(GW), head(2 * GW), lane_b, lane_b, per_chunk(1, CH)],
        out_specs=[head(0), per_chunk(2 * CH, DH), per_chunk(3 * CH, CH), per_chunk(3 * CH, CH), per_chunk(DH, CH),
                   per_chunk(CH, DH), per_chunk(CH, CH), per_chunk(1, DH)],
        out_shape=[jax.ShapeDtypeStruct(s, dt) for s, dt in shapes],
        compiler_params=_cparams(("parallel", "parallel")),
    )(qkv, qkv, qkv, gcb, bb, gct)


def _scan_specs(bsz, lp, grp, order):
    ngrp = lp // (grp * CH)
    bh = bsz * HEADS
    seq = lambda: pl.BlockSpec((bsz, grp * CH, GW), lambda s: (0, order(s, ngrp), 0))
    per_chunk = lambda r, c: pl.BlockSpec((bh, grp, r, c), lambda s: (0, order(s, ngrp), 0, 0))
    return ngrp, bh, seq, per_chunk


def _gdn_scan_fwd(u, l1, l2, gl, bsz, lp):
    nchunk = lp // CH
    grp = _chunk_group(nchunk)
    ngrp, bh, seq, per_chunk = _scan_specs(bsz, lp, grp, lambda s, n: s)

    def body(u_ref, l1_ref, l2_ref, gl_ref, o_ref, vn_ref, st_ref, s_scr):
        @pl.when(pl.program_id(0) == 0)
        def _():
            s_scr[...] = jnp.zeros_like(s_scr)

        chains = list(range(bh))
        where = [(i // HEADS, slice((i % HEADS) * DH, (i % HEADS + 1) * DH)) for i in chains]

        def dot(a, b):
            return jnp.dot(a, b.astype(BF16), preferred_element_type=F32)

        for c in range(grp):
            rows = slice(c * CH, (c + 1) * CH)
            s = [s_scr[i] for i in chains]
            for i in chains:
                st_ref[i, c] = s[i]
            x = _each(lambda i, si: dot(l1_ref[i, c], si), chains, s)
            vnew = _each(lambda bc, xi: u_ref[bc[0], rows, bc[1]] - xi[:CH], where, x)
            y = _each(lambda i, vi: dot(l2_ref[i, c], vi), chains, vnew)
            for i, (b, cols) in zip(chains, where):
                o_ref[b, rows, cols] = x[i][CH:] + y[i][:CH]
                vn_ref[b, rows, cols] = vnew[i].astype(BF16)
                s_scr[i] = s[i] * gl_ref[i, c] + y[i][CH:]

    return pl.pallas_call(
        body, name="gdn_scan_fwd", grid=(ngrp,),
        in_specs=[seq(), per_chunk(2 * CH, DH), per_chunk(3 * CH, CH), per_chunk(1, DH)],
        out_specs=[seq(), seq(), per_chunk(DH, DH)],
        out_shape=[jax.ShapeDtypeStruct((bsz, lp, GW), F32), jax.ShapeDtypeStruct((bsz, lp, GW), BF16),
                   jax.ShapeDtypeStruct((bh, nchunk, DH, DH), F32)],
        scratch_shapes=[pltpu.VMEM((bh, DH, DH), F32)],
        compiler_params=_cparams(("arbitrary",)),
    )(u.reshape(bsz, lp, GW), l1, l2, gl)


def _gdn_scan_bwd(d_o, l3, wt, kd, gl, bsz, lp):
    nchunk = lp // CH
    grp = _chunk_group(nchunk)
    ngrp, bh, seq, per_chunk = _scan_specs(bsz, lp, grp, lambda s, n: n - 1 - s)

    def body(do_ref, l3_ref, wt_ref, kd_ref, gl_ref, dvn_ref, dst_ref, ds_scr):
        @pl.when(pl.program_id(0) == 0)
        def _():
            ds_scr[...] = jnp.zeros_like(ds_scr)

        chains = list(range(bh))
        where = [(i // HEADS, slice((i % HEADS) * DH, (i % HEADS + 1) * DH)) for i in chains]

        def dot(a, b):
            return jnp.dot(a, b.astype(BF16), preferred_element_type=F32)

        for c in reversed(range(grp)):
            rows = slice(c * CH, (c + 1) * CH)
            ds = [ds_scr[i] for i in chains]
            for i in chains:
                dst_ref[i, c] = ds[i]
            p = _each(lambda i, bc: dot(l3_ref[i, c], do_ref[bc[0], rows, bc[1]]), chains, where)
            dvn = _each(lambda i, pi, di: pi[:CH] + dot(kd_ref[i, c], di), chains, p, ds)
            wd = _each(lambda i, vi: dot(wt_ref[i, c], vi), chains, dvn)
            for i, (b, cols) in zip(chains, where):
                dvn_ref[b, rows, cols] = dvn[i].astype(BF16)
                ds_scr[i] = ds[i] * gl_ref[i, c] + p[i][CH:] - wd[i]

    return pl.pallas_call(
        body, name="gdn_scan_bwd", grid=(ngrp,),
        in_specs=[seq(), per_chunk(3 * CH, CH), per_chunk(DH, CH), per_chunk(CH, DH), per_chunk(1, DH)],
        out_specs=[seq(), per_chunk(DH, DH)],
        out_shape=[jax.ShapeDtypeStruct((bsz, lp, GW), BF16), jax.ShapeDtypeStruct((bh, nchunk, DH, DH), F32)],
        scratch_shapes=[pltpu.VMEM((bh, DH, DH), F32)],
        compiler_params=_cparams(("arbitrary",)),
    )(d_o.reshape(bsz, lp, GW), l3, wt, kd, gl)


def _gdn_local_bwd(qkv, gcb, bb, gct, tinv_all, states, dstates, vnew, dvnew, d_o, bsz, lp):
    nchunk = lp // CH
    grp = _local_group(nchunk)
    ngrp, head, lane_b, per_chunk = _gdn_local_specs(lp, grp)
    bh = bsz * HEADS

    def body(q_ref, k_ref, v_ref, gc_ref, be_ref, gt_ref, ti_ref, st_ref, dst_ref, vn_ref, dvn_ref, do_ref,
             dq_ref, dk_ref, dv_ref, dg_ref, db_ref):
        incl, strict = _tri(CH, "incl"), _tri(CH, "strict")
        upper = _tri(CH, "upper").astype(F32)
        ones = jnp.ones((CH, DH), F32)
        last = lax.broadcasted_iota(jnp.int32, (CH, 1), 0) == CH - 1

        def rsum(x):
            return jnp.sum(x, axis=-1, keepdims=True)

        def mul(xs, ys):
            return _each(lambda x, y: x * y, xs, ys)

        rows, be, dec, _, eg, ekd, gl = zip(*[_chunk_gates(gc_ref, be_ref, gt_ref, g) for g in range(grp)])
        q = [q_ref[r, :] for r in rows]
        k = [k_ref[r, :] for r in rows]
        v = [v_ref[r, :] for r in rows]
        vnew = [vn_ref[r, :] for r in rows]
        dvn = [dvn_ref[r, :] for r in rows]
        do = [do_ref[r, :] for r in rows]
        tinv = [ti_ref[0, g] for g in range(grp)]
        s = [st_ref[0, g] for g in range(grp)]
        ds = [dst_ref[0, g] for g in range(grp)]
        kb, vb, qd, kd = mul(k, be), mul(v, be), mul(q, eg), mul(k, ekd)
        kbg = mul(kb, eg)
        a = _each(lambda x, y, d: jnp.where(strict, _bdot(x, y, NT) * d, 0.0), kb, k, dec)
        qk = _each(lambda x, y, d: jnp.where(incl, _bdot(x, y, NT) * d, 0.0), q, k, dec)
        dqk = _each(lambda x, y: jnp.where(incl, _bdot(x, y, NT), 0.0), do, vnew)
        dqd = _each(lambda x, y: _bdot(x, y, NT), do, s)
        dkd = _each(lambda x, y: _bdot(x, y, NT), vnew, ds)
        dgl = _each(lambda x, y: jnp.sum(rsum(x * y), axis=0, keepdims=True), s, ds)
        dw = _each(lambda x, y: -_bdot(x, y, NT), dvn, s)
        dvb = _each(lambda t, x: _bdot(t, x, TN), tinv, dvn)
        dkbg = _each(lambda t, x: _bdot(t, x, TN), tinv, dw)
        dt = _each(lambda x, y, z, w: _bdot(x, y, NT) + _bdot(z, w, NT), dvn, vb, dw, kbg)
        tdt = _each(lambda t, x: _split_dot(t, x, TN), tinv, dt)
        da = _each(lambda x, t: jnp.where(strict, -_split_dot(x, t, NT), 0.0), tdt, tinv)
        dp, dr = mul(da, dec), mul(dqk, dec)
        dkb = _each(lambda x, y, z, e: _bdot(x, y, NN) + z * e, dp, k, dkbg, eg)
        dk = _each(lambda p_, kb_, r_, q_, dkd_, ekd_, dkb_, be_:
                   _bdot(p_, kb_, TN) + _bdot(r_, q_, TN) + dkd_ * ekd_ + dkb_ * be_,
                   dp, kb, dr, q, dkd, ekd, dkb, be)
        dq = _each(lambda r_, k_, dqd_, eg_: _bdot(r_, k_, NN) + dqd_ * eg_, dr, k, dqd, eg)
        m = _each(lambda da_, a_, dqk_, qk_: da_ * a_ + dqk_ * qk_, da, a, dqk, qk)
        m_cols = _each(lambda x: _split_dot(x, ones, TN), m)
        kd_term = _each(lambda x, y: rsum(x * y), dkd, kd)
        dgc = _each(lambda m_, mc, dqd_, qd_, kt, dkbg_, kbg_, dgl_, gl_:
                    rsum(m_) - mc + rsum(dqd_ * qd_) - kt + rsum(dkbg_ * kbg_)
                    + jnp.where(last, jnp.sum(kt, axis=0, keepdims=True) + dgl_ * gl_, 0.0),
                    m, m_cols, dqd, qd, kd_term, dkbg, kbg, dgl, gl)
        dg = _each(lambda x: _split_dot(upper, x, NN), dgc)
        for g in range(grp):
            dq_ref[rows[g], :] = dq[g]
            dk_ref[rows[g], :] = dk[g]
            dv_ref[rows[g], :] = dvb[g] * be[g]
            dg_ref[0, rows[g], :] = dg[g]
            db_ref[0, rows[g], :] = jnp.broadcast_to(rsum(dkb[g] * k[g]) + rsum(dvb[g] * v[g]), (CH, DH))

    r = bsz * lp
    return pl.pallas_call(
        body, name="gdn_local_bwd", grid=(bh, ngrp),
        in_specs=[head(0), head(GW), head(2 * GW), lane_b, lane_b, per_chunk(1, CH), per_chunk(CH, CH),
                  per_chunk(DH, DH), per_chunk(DH, DH), head(0), head(0), head(0)],
        out_specs=[head(0), head(0), head(0), lane_b, lane_b],
        out_shape=[jax.ShapeDtypeStruct((r, GW), F32)] * 3 + [jax.ShapeDtypeStruct((bh, lp, DH), F32)] * 2,
        compiler_params=_cparams(("parallel", "parallel")),
    )(qkv, qkv, qkv, gcb, bb, gct, tinv_all, states, dstates, vnew, dvnew, d_o)


def _place():
    return lax.axis_index("x"), lax.axis_index("y"), lax.axis_index("c")


def _any_specs(n):
    return [pl.BlockSpec(memory_space=pl.ANY)] * n


def _all_gather(name, shards):
    n = len(shards)

    def body(*refs):
        ins, outs = refs[:n], refs[n:2 * n]
        send_sems, recv_sems, local_sems = refs[2 * n:2 * n + 3]
        stage = refs[2 * n + 3:]
        x, y, c = _place()
        me, sibling = (x, y, c), (x, y, 1 - c)
        chips = [(1 - x, y), (x, 1 - y), (1 - x, 1 - y)]

        def slot(i, p):
            return outs[i].at[4 * p[0] + 2 * p[1] + p[2]]

        def copy(i, k, block, to, src=None):
            return pltpu.make_async_remote_copy(
                src_ref=slot(i, block) if src is None else src, dst_ref=slot(i, block),
                send_sem=send_sems.at[i, k], recv_sem=recv_sems.at[i, k], device_id=to, device_id_type=MESH)

        load = [pltpu.make_async_copy(ins[i], stage[i], local_sems.at[i]) for i in range(n)]
        for cp in load:
            cp.start()
        first = []
        for i in range(n):
            first.append(copy(i, 0, me, sibling, src=ins[i]))
            first += [copy(i, 1 + j, me, (*chip, c), src=ins[i]) for j, chip in enumerate(chips)]
        for cp in first:
            cp.start()
        for cp in load:
            cp.wait()
        mine = [pltpu.make_async_copy(stage[i], slot(i, me), local_sems.at[i]) for i in range(n)]
        for cp in mine:
            cp.start()
        passed = []
        for j, chip in enumerate(chips):
            for i in range(n):
                copy(i, 1 + j, (*chip, c), me).wait_recv()
                fwd = copy(i, 4 + j, (*chip, c), sibling)
                fwd.start()
                passed.append(fwd)
        for i in range(n):
            copy(i, 0, sibling, me).wait_recv()
            for j, chip in enumerate(chips):
                copy(i, 4 + j, (*chip, 1 - c), me).wait_recv()
        for cp in first + passed:
            cp.wait_send()
        for cp in mine:
            cp.wait()

    return pl.pallas_call(
        body, name=name,
        in_specs=_any_specs(n), out_specs=_any_specs(n),
        out_shape=[jax.ShapeDtypeStruct((N_DEV,) + s.shape, s.dtype) for s in shards],
        scratch_shapes=[pltpu.SemaphoreType.DMA((n, 7)), pltpu.SemaphoreType.DMA((n, 7)),
                        pltpu.SemaphoreType.DMA((n,))] + [pltpu.VMEM(s.shape, s.dtype) for s in shards],
    )(*shards)


HBM_SPEC = pl.BlockSpec(memory_space=pltpu.HBM)
SEM_SPEC = pl.BlockSpec(memory_space=pltpu.SEMAPHORE)
EFFECT = pltpu.SideEffectType.DATAFLOW_SIDE_EFFECTING
N_PEER = N_DEV - 1


def _peer(r):
    x, y, c = _place()
    return ((1 - x) if r & 4 else x, (1 - y) if r & 2 else y, (1 - c) if r & 1 else c)


def _slot_of(p):
    return 4 * p[0] + 2 * p[1] + p[2]


def _hbm(a):
    return pltpu.with_memory_space_constraint(a, pltpu.HBM)


def _gather_pattern(src, land):
    me = _place()
    return [(src, land.at[_slot_of(me)], _peer(r), land.at[_slot_of(_peer(r))]) for r in range(1, N_DEV)]


def _alltoall_pattern(src, land):
    me = _place()
    return [(src.at[_slot_of(_peer(r))], land.at[_slot_of(me)], _peer(r), land.at[_slot_of(_peer(r))])
            for r in range(1, N_DEV)]


def _chipcast_pattern(src, land):
    x, y, c = _place()
    me = (x, y, c)
    peers = [(x, y, 1 - c), (1 - x, y, c), (x, 1 - y, c), (1 - x, 1 - y, c)]
    return [(src, land.at[_slot_of(me)], p, land.at[_slot_of(p)]) for p in peers]


def _forward_pattern(land, _):
    x, y, c = _place()
    chips = [(1 - x, y), (x, 1 - y), (1 - x, 1 - y)]
    return [(land.at[_slot_of((*ch, c))], land.at[_slot_of((*ch, c))], (x, y, 1 - c), land.at[_slot_of((*ch, 1 - c))])
            for ch in chips]


def _sibling_pattern(src, land):
    x, y, c = _place()
    return [(src.at[ch, 1 - c], land.at[ch], (x, y, 1 - c), land.at[ch]) for ch in range(4)]


def _chip_pattern(src, land):
    x, y, c = _place()
    chips = [(1 - x, y), (x, 1 - y), (1 - x, 1 - y)]
    return [(src.at[2 * ch[0] + ch[1]], land.at[j], (*ch, c), land.at[j]) for j, ch in enumerate(chips)]


def _split_start(name, groups, pattern, land_shape, after=None):
    sizes = [len(g) for g in groups]
    arrays = [s for g in groups for s in g]
    n = len(arrays)
    inplace = land_shape is None
    fresh = [] if inplace else [lax.empty(land_shape(s.shape), s.dtype) for s in arrays]
    bufs = arrays + fresh
    order = [] if after is None else [after]

    def body(*refs):
        srcs, lands = refs[:n], (refs[:n] if inplace else refs[n:2 * n])
        first_out = len(bufs) + len(order)
        sems = refs[first_out:first_out + 2 * len(groups)]
        token = refs[-1]
        k = 0
        for gi, size in enumerate(sizes):
            for i in range(size):
                copies = pattern(srcs[k], lands[k])
                for j, (s, d, peer, _) in enumerate(copies):
                    pltpu.make_async_remote_copy(
                        src_ref=s, dst_ref=d, send_sem=sems[2 * gi].at[i * len(copies) + j],
                        recv_sem=sems[2 * gi + 1].at[i * len(copies) + j], device_id=peer, device_id_type=MESH).start()
                k += 1
        token[...] = jnp.zeros_like(token)

    per_array = {_gather_pattern: N_PEER, _alltoall_pattern: N_PEER, _chipcast_pattern: 4, _forward_pattern: 3,
                 _sibling_pattern: 4, _chip_pattern: 3}[pattern]
    sem_shapes = [pltpu.SemaphoreType.DMA((size * per_array,)) for size in sizes for _ in range(2)]
    res = pl.pallas_call(
        body, name=name,
        in_specs=[HBM_SPEC] * len(bufs) + [pl.BlockSpec(memory_space=pl.ANY)] * len(order),
        out_specs=[SEM_SPEC] * len(sem_shapes) + [HBM_SPEC] * len(bufs) + [pl.BlockSpec(memory_space=pltpu.VMEM)],
        out_shape=sem_shapes + [pltpu.HBM(s.shape, s.dtype) for s in bufs] + [jax.ShapeDtypeStruct((8, DH), F32)],
        input_output_aliases={i: len(sem_shapes) + i for i in range(len(bufs))},
        compiler_params=pltpu.CompilerParams(has_side_effects=EFFECT),
    )(*[_hbm(s) for s in bufs], *order)
    nsem = len(sem_shapes)
    thru = list(res[nsem:nsem + len(bufs)])
    out, k = [], 0
    for gi, size in enumerate(sizes):
        srcs_out = [] if inplace else thru[k:k + size]
        lands_out = thru[k:k + size] if inplace else thru[n + k:n + k + size]
        out.append((res[2 * gi], res[2 * gi + 1], srcs_out, lands_out))
        k += size
    return out, res[-1]


def _split_wait(name, group, pattern, after):
    send_sems, recv_sems, arrays, lands = group
    n = len(lands)
    inplace = not arrays
    bufs = list(arrays) + list(lands)
    fill_own = pattern in (_gather_pattern, _chipcast_pattern, _alltoall_pattern)
    own_of = (lambda a: a.at[_slot_of(_place())]) if pattern is _alltoall_pattern else (lambda a: a)
    own_shape = (lambda a: a.shape[1:]) if pattern is _alltoall_pattern else (lambda a: a.shape)

    def body(*refs):
        srcs, lands_ = refs[:n], (refs[:n] if inplace else refs[n:2 * n])
        s_sems, r_sems = refs[len(bufs)], refs[len(bufs) + 1]
        own = []
        if fill_own:
            stage, local_sems = refs[-n - 1:-1], refs[-1]
            load = [pltpu.make_async_copy(own_of(srcs[i]), stage[i], local_sems.at[i]) for i in range(n)]
            for cp in load:
                cp.start()
            for cp in load:
                cp.wait()
            own = [pltpu.make_async_copy(stage[i], lands_[i].at[_slot_of(_place())], local_sems.at[i])
                   for i in range(n)]
            for cp in own:
                cp.start()
        for i in range(n):
            copies = pattern(srcs[i], lands_[i])
            for j, (s, _, peer, filled) in enumerate(copies):
                cp = pltpu.make_async_remote_copy(
                    src_ref=s, dst_ref=filled, send_sem=s_sems.at[i * len(copies) + j],
                    recv_sem=r_sems.at[i * len(copies) + j], device_id=peer, device_id_type=MESH)
                cp.wait_send()
                cp.wait_recv()
        for cp in own:
            cp.wait()

    res = pl.pallas_call(
        body, name=name,
        in_specs=[HBM_SPEC] * len(bufs) + [SEM_SPEC, SEM_SPEC, pl.BlockSpec(memory_space=pl.ANY)],
        out_specs=[HBM_SPEC] * len(bufs),
        out_shape=[pltpu.HBM(s.shape, s.dtype) for s in bufs],
        scratch_shapes=([pltpu.VMEM(own_shape(a), a.dtype) for a in arrays] + [pltpu.SemaphoreType.DMA((n,))])
        if fill_own else [],
        input_output_aliases={i: i for i in range(len(bufs))},
        compiler_params=pltpu.CompilerParams(has_side_effects=EFFECT),
    )(*bufs, send_sems, recv_sems, after)
    return ([], list(res)) if inplace else (list(res[:n]), list(res[n:]))


def _pair_sum(name, grads, recv, core):
    _, _, r, c = grads.shape
    tr = r if r * c <= (1 << 20) or r % 16 else _div_tile(r, 512, 16)

    def body(core_ref, g_ref, r_ref, o_ref):
        o_ref[...] = (g_ref[0].astype(F32) + r_ref[...].astype(F32)).astype(o_ref.dtype)

    return pl.pallas_call(
        body, name=name,
        grid_spec=pltpu.PrefetchScalarGridSpec(
            num_scalar_prefetch=1, grid=(4, r // tr),
            in_specs=[pl.BlockSpec((1, 1, tr, c), lambda s, i, cr: (s, cr[0], i, 0)),
                      pl.BlockSpec((1, tr, c), lambda s, i, cr: (s, i, 0))],
            out_specs=pl.BlockSpec((1, tr, c), lambda s, i, cr: (s, i, 0))),
        out_shape=jax.ShapeDtypeStruct(recv.shape, recv.dtype),
        compiler_params=_cparams(("parallel", "parallel")),
    )(core, grads, recv)


def _adamw_math(w, g, m, v):
    m = ADAM_B1 * m + (1.0 - ADAM_B1) * g
    v = ADAM_B2 * v + (1.0 - ADAM_B2) * (g * g)
    m_hat = m / (1.0 - ADAM_B1 ** ADAM_STEP)
    v_hat = v / (1.0 - ADAM_B2 ** ADAM_STEP)
    delta = -ADAM_LR * (m_hat / (jnp.sqrt(v_hat) + ADAM_EPS) + ADAM_WD * w)
    return delta, m, v


SMALL_LEAVES = (
    ("mix_pre_norm", 0, 1, 0, D, (1, D)), ("mix_post_norm", 1, 1, 0, D, (1, D)),
    ("ffn_pre_norm", 2, 1, 0, D, (1, D)), ("ffn_post_norm", 3, 1, 0, D, (1, D)),
    ("meta_tokens", 8, N_META, 0, DH, (N_META, DH)),
    ("conv_qkv", 24, KQ, 0, 3 * GW // N_DEV, (1, KQ, 3 * GW // N_DEV)),
    ("conv_sc", 32, KS, 0, SW // N_DEV, (1, KS, SW // N_DEV)),
    ("a_log", 40, 1, 0, HEADS, (1, HEADS)), ("dt_bias", 40, 1, DH, HEADS, (1, HEADS)),
    ("gdn_norm", 40, 1, 2 * DH, DH, (1, DH)),
)
SMALL_LOSS_AT = (40, 3 * DH)


def _adamw_small(slabs, late, w, m, v):
    nslot = slabs.shape[0]
    nleaf = len(SMALL_LEAVES)

    def body(p_ref, l_ref, w_ref, m_ref, v_ref, *refs):
        outs, loss_ref, scr = refs[:4 * nleaf], refs[4 * nleaf], refs[4 * nleaf + 1]
        g = p_ref[0]
        gain = l_ref[0, pl.ds(0, 1), :]
        meta = l_ref[0, pl.ds(8, N_META), pl.ds(0, DH)]
        for s in range(1, nslot):
            g = g + p_ref[s]
            gain = gain + l_ref[s, pl.ds(0, 1), :]
            meta = meta + l_ref[s, pl.ds(8, N_META), pl.ds(0, DH)]
        scr[0] = g
        scr[0, pl.ds(0, 1), pl.ds(0, D)] += gain
        scr[0, pl.ds(8, N_META), pl.ds(0, DH)] += meta
        g = scr[0]
        d, mn, vn = _adamw_math(w_ref[...], g, m_ref[...], v_ref[...])
        for kind, val in enumerate((g, d, mn, vn)):
            scr[kind] = val
        for kind in range(4):
            for li, (_, r0, nr, c0, nc, shape) in enumerate(SMALL_LEAVES):
                val = scr[kind, pl.ds(r0, nr), pl.ds(c0, nc)]
                o = outs[kind * nleaf + li]
                if len(shape) == 3:
                    o[0] = val
                else:
                    o[...] = val
        loss_ref[...] = scr[0, pl.ds(SMALL_LOSS_AT[0], 1), pl.ds(SMALL_LOSS_AT[1], 1)]

    whole = lambda a: pl.BlockSpec(a.shape, lambda: (0,) * a.ndim)
    out_shapes = [jax.ShapeDtypeStruct(shape, F32) for _ in range(4) for (*_, shape) in SMALL_LEAVES]
    out_shapes.append(jax.ShapeDtypeStruct((1, 1), F32))
    res = pl.pallas_call(
        body, name="adamw_small",
        in_specs=[whole(slabs), whole(late), whole(w), whole(m), whole(v)],
        out_specs=[pl.BlockSpec(s.shape, lambda n=len(s.shape): (0,) * n) for s in out_shapes],
        out_shape=out_shapes,
        scratch_shapes=[pltpu.VMEM((4,) + w.shape, F32)],
        compiler_params=pltpu.CompilerParams(vmem_limit_bytes=VMEM_LIMIT),
    )(slabs, late, w, m, v)
    kinds = [{nm: res[kind * nleaf + li] for li, (nm, *_) in enumerate(SMALL_LEAVES)} for kind in range(4)]
    return kinds, res[-1]


def _adamw_reduced(name, parts, chip, recv, w, m, v, after=None):
    _, r, c = parts.shape
    tr = r if r * c <= (1 << 20) or r % 16 else _div_tile(r, 512, 16)
    order = [] if after is None else [after]

    def body(chip_ref, p_ref, r_ref, w_ref, m_ref, v_ref, *rest):
        g_out, d_out, m_out, v_out = rest[len(order):]
        g = p_ref[0].astype(F32)
        for s in range(3):
            g = g + r_ref[s].astype(F32)
        d, mn, vn = _adamw_math(w_ref[...], g, m_ref[...], v_ref[...])
        g_out[...] = g
        d_out[...] = d
        m_out[...] = mn
        v_out[...] = vn

    blk = pl.BlockSpec((tr, c), lambda i, ch: (i, 0))
    return pl.pallas_call(
        body, name=name,
        grid_spec=pltpu.PrefetchScalarGridSpec(
            num_scalar_prefetch=1, grid=(r // tr,),
            in_specs=[pl.BlockSpec((1, tr, c), lambda i, ch: (ch[0], i, 0)),
                      pl.BlockSpec((3, tr, c), lambda i, ch: (0, i, 0)), blk, blk, blk]
            + [pl.BlockSpec((8, DH), lambda i, ch: (0, 0))] * len(order),
            out_specs=[blk] * 4),
        out_shape=[jax.ShapeDtypeStruct((r, c), F32)] * 4,
        compiler_params=_cparams(("parallel",)),
    )(chip, parts, recv, w, m, v, *order)


SMALL_ROWS, SMALL_COLS = 48, 3 * GW


def _pack_small(gains4, meta, conv_qkv, conv_sc, a_log, dt_bias, gdn_norm, extra=None):
    def tile(a):
        return jnp.pad(a, ((0, -a.shape[0] % 8), (0, SMALL_COLS - a.shape[1])))
    scal = jnp.concatenate([jnp.pad(a_log, ((0, 0), (0, DH - HEADS))), jnp.pad(dt_bias, ((0, 0), (0, DH - HEADS))),
                            gdn_norm] + ([] if extra is None else [extra]), axis=1)
    gains = sum(jnp.pad(g, ((i, 7 - i), (0, SMALL_COLS - g.shape[1]))) for i, g in enumerate(gains4))
    parts = [gains, tile(meta), tile(conv_qkv), tile(conv_sc), tile(scal)]
    slab = jnp.concatenate(parts, axis=0)
    assert slab.shape == (SMALL_ROWS, SMALL_COLS)
    return slab


def kernel(x, meta_tokens, mix_pre_norm, mix_post_norm, ffn_pre_norm, ffn_post_norm, w_in, conv_qkv, a_log, dt_bias, gdn_norm, conv_sc, w_out, w_gate, w_up, w_down, loss_target, m_meta_tokens, m_mix_pre_norm, m_mix_post_norm, m_ffn_pre_norm, m_ffn_post_norm, m_w_in, m_conv_qkv, m_a_log, m_dt_bias, m_gdn_norm, m_conv_sc, m_w_out, m_w_gate, m_w_up, m_w_down, v_meta_tokens, v_mix_pre_norm, v_mix_post_norm, v_ffn_pre_norm, v_ffn_post_norm, v_w_in, v_conv_qkv, v_a_log, v_dt_bias, v_gdn_norm, v_conv_sc, v_w_out, v_w_gate, v_w_up, v_w_down):
    bsz, seq, _ = x.shape
    lp = -(-(N_META + seq) // CH) * CH
    padf = lp - N_META - seq
    first_real = padf + N_META
    rows = bsz * lp
    nchunk = lp // CH
    tr = _div_tile(lp, 528)
    tm = _div_tile(rows, 1056)
    tm_big = _div_tile(rows, 2112)
    cx, cy, cc = _place()

    small_shard = jnp.concatenate([
        jnp.pad(meta_tokens, ((0, 0), (0, 256 - DH))),
        jnp.pad(conv_qkv[0], ((0, 0), (0, 256 - 3 * GW // N_DEV))),
        jnp.pad(conv_sc[0], ((0, 5), (0, 256 - SW // N_DEV)))], axis=0)
    g_in, g_small = _all_gather("gather_w_in", [w_in[0].T.astype(BF16), small_shard])
    my_rest = [w_out[0].astype(BF16), w_gate[0].T.astype(BF16), w_up[0].T.astype(BF16), w_down[0].astype(BF16)]
    (grp_rest,), token = _split_start("gather_rest_start", [my_rest], _chipcast_pattern, lambda s: (N_DEV,) + s,
                                      after=g_in)
    after_start = token[0:1, 0:1]

    def cols_full(g):
        return jnp.transpose(g, (1, 0, 2)).reshape(g.shape[1], -1)

    win_t = g_in.reshape(IN_W, D)
    win_t = jnp.concatenate([win_t[:BA0 + 2 * HEADS], jnp.zeros((BA_W - 2 * HEADS, D), BF16),
                             win_t[BA0 + 2 * HEADS:]], axis=0)
    meta_full = cols_full(g_small[:, 0:16, :DH])
    convq_full = cols_full(g_small[:, 16:20, :3 * GW // N_DEV])
    convs_full = cols_full(g_small[:, 20:23, :SW // N_DEV])
    alog_v = jnp.pad(a_log, ((0, 0), (HEADS, DH - 2 * HEADS)))
    dtb_v = jnp.pad(dt_bias, ((0, 0), (HEADS, DH - 2 * HEADS)))

    head_rows = jnp.concatenate([jnp.zeros((padf, D), F32), meta_full], axis=0)
    h0, u1 = _pre_norm(x, head_rows, mix_pre_norm + after_start, bsz, lp, tr)
    (proj,) = _mm("proj_in", [(u1, win_t)], "nt", tm_big, 768, D, out_dtypes=(BF16,))
    (ba,) = _mm("proj_gate_logits", [(u1, win_t[BA0:SX0])], "nt", tm, BA_W, D)
    qkv = _gdn_prep(proj, convq_full, bsz, lp, padf)
    gb, gcs = _gates(ba, alog_v, dtb_v, bsz, lp, padf)

    def lane_bcast(cols):
        t = jnp.transpose(cols.reshape(bsz, lp, HEADS), (0, 2, 1)).reshape(bsz * HEADS, lp, 1)
        return jnp.broadcast_to(t, (bsz * HEADS, lp, DH))

    bb = lane_bcast(gb[:, 0:HEADS])
    gcb = lane_bcast(gcs[:, HEADS:2 * HEADS])
    gct = jnp.transpose(gcs[:, HEADS:2 * HEADS].reshape(bsz, nchunk, CH, HEADS), (0, 3, 1, 2)).reshape(
        bsz * HEADS, nchunk, 1, CH)
    u_loc, l1, l2, l3, w_t, k_dec, tinv_all, g_last = _gdn_local_fwd(qkv, gcb, bb, gct, bsz, lp)
    o, v_new, states = _gdn_scan_fwd(u_loc, l1, l2, g_last, bsz, lp)
    o, v_new = o.reshape(rows, GW), v_new.reshape(rows, GW)
    _, l_rest = _split_wait("gather_rest_arrive", grp_rest, _chipcast_pattern, o)
    (grp_out, grp_ffn, grp_down), token = _split_start(
        "gather_rest_forward", [l_rest[0:1], l_rest[1:3], l_rest[3:4]], _forward_pattern, None)
    mix_g, mix_s = _mix_heads(proj, o, gdn_norm + token[0:1, :], convs_full, bsz, lp, padf)
    mixin = jnp.concatenate([mix_g, mix_s], axis=1)
    _, (l_out,) = _split_wait("gather_w_out_wait", grp_out, _forward_pattern, mixin)
    wout = l_out.reshape(D, D)
    (mix,) = _mm("proj_out", [(mixin, wout)], "nn", tm, D, D)
    h1, u2 = _mid_norms(h0, mix, mix_post_norm, ffn_pre_norm, bsz, lp, tr)
    _, (l_gate, l_up) = _split_wait("gather_ffn_wait", grp_ffn, _forward_pattern, u2)
    wgate_t, wup_t = l_gate.reshape(FF, D), l_up.reshape(FF, D)
    gate, up, act = _ffn_up(u2, wgate_t, wup_t, _div_tile(rows, 528), 1408)
    _, (l_down,) = _split_wait("gather_down_wait", grp_down, _forward_pattern, act)
    wdown = l_down.reshape(FF, D)
    (ffn,) = _mm("ffn_down", [(act, wdown)], "nn", tm, D, 1408)

    dy, dffn, d_g4, loss_part = _loss_head(h1, ffn, loss_target, ffn_post_norm, bsz, lp, tr, first_real)
    (dwdown,) = _mm("dw_down", [(act, dffn)], "tn", 1408, D, tm, out_dtypes=(BF16,))

    def swiglu_bwd(da, g, u):
        g, u = g.astype(F32), u.astype(F32)
        return da * u * _dsilu(g), da * _silu(g)

    dgate, dup = _mm("d_act", [(dffn, wdown)], "nt", tm, 1408, D, out_dtypes=(BF16, BF16),
                     epilogue=swiglu_bwd, extras=(gate, up))
    (dwgate_t,) = _mm("dw_gate", [(dgate, u2)], "tn", 1408, D, tm, out_dtypes=(BF16,))
    (dwup_t,) = _mm("dw_up", [(dup, u2)], "tn", 1408, D, tm, out_dtypes=(BF16,))
    (du2,) = _mm("d_u2", [(dgate, wgate_t), (dup, wup_t)], "nn", tm, D, 1408)
    dh1, dmix, d_g3, d_g2 = _mid_norms_bwd(h1, mix, du2, dy, ffn_pre_norm, mix_post_norm, bsz, lp, tr)
    (dwout,) = _mm("dw_out", [(mixin, dmix)], "tn", D, D, tm, out_dtypes=(BF16,))

    def row_blocks(g):
        return g.reshape(4, 2, -1, g.shape[1])

    core = jnp.reshape(cc, (1,)).astype(jnp.int32)
    chip = jnp.reshape(2 * cx + cy, (1,)).astype(jnp.int32)

    def reduce_start(tag, grads):
        (grp,), tok = _split_start("rs_sibling_start_" + tag, [grads], _sibling_pattern, lambda s: s[:1] + s[2:])
        return grp, tok

    def reduce_mid(tag, grp, nms, after):
        mine, lands = _split_wait("rs_sibling_wait_" + tag, grp, _sibling_pattern, after)
        parts = [_pair_sum("pair_sum_" + nm, g, r, core) for nm, g, r in zip(nms, mine, lands)]
        (grp2,), tok = _split_start("rs_chip_start_" + tag, [parts], _chip_pattern, lambda s: (3,) + s[1:])
        return grp2, tok

    def reduce_end(tag, grp2, after):
        return _split_wait("rs_chip_wait_" + tag, grp2, _chip_pattern, after)

    names_ffn = ["w_down", "w_gate", "w_up", "w_out"]
    rs_ffn, tok = reduce_start("ffn", [row_blocks(dwdown), row_blocks(dwgate_t), row_blocks(dwup_t), row_blocks(dwout)])
    (dmixin,) = _mm("d_mixin", [(dmix, wout)], "nt", tm, D, D, after=tok)
    d_o, dz, dsx, dsb, dsc, d_gn, d_convs = _mix_heads_bwd(proj, o, gdn_norm, convs_full, dmixin, bsz, lp, padf)
    rs_ffn, tok = reduce_mid("ffn", rs_ffn, names_ffn, d_o)
    dv_new, dstates = _gdn_scan_bwd(d_o, l3, w_t, k_dec, g_last + tok[0, 0], bsz, lp)
    dq, dk, dv, dg_b, dbeta_b = _gdn_local_bwd(qkv, gcb, bb, gct, tinv_all, states, dstates, v_new,
                                               dv_new.reshape(rows, GW), d_o, bsz, lp)
    dpqkv, d_convq = _gdn_prep_bwd(proj, convq_full, dq, dk, dv, bsz, lp, padf)
    dba, d_scal = _gates_bwd(ba, alog_v, dtb_v, dbeta_b, dg_b, gb, bsz, lp, padf)
    dproj = jnp.concatenate([dpqkv, dz, dba, dsx, dsb, dsc], axis=1)

    d_alog = d_scal[0:1, HEADS:2 * HEADS]
    d_dtb = d_scal[1:2, HEADS:2 * HEADS]
    cq_w, cs_w = 3 * GW // N_DEV, SW // N_DEV
    slab = jnp.stack([
        _pack_small([jnp.zeros_like(d_g2), d_g2, d_g3, d_g4], jnp.zeros((N_META, DH), F32),
                    d_convq[:, p * cq_w:(p + 1) * cq_w], d_convs[:, p * cs_w:(p + 1) * cs_w], d_alog, d_dtb, d_gn,
                    loss_part) for p in range(N_DEV)])
    (grp_small,), tok = _split_start("small_grads_start", [[slab]], _alltoall_pattern, lambda s: s)
    (dwin_t,) = _mm("dw_in", [(dproj, u1)], "tn", 1280, D, tm, out_dtypes=(BF16,), after=tok)
    parts_ffn, recv_ffn = reduce_end("ffn", rs_ffn, dwin_t)
    dwin_t = jnp.concatenate([dwin_t[:BA0 + 2 * HEADS], dwin_t[SX0:]], axis=0)
    rs_in, tok = reduce_start("in", [row_blocks(dwin_t)])

    big_w = {"w_in": (w_in, m_w_in, v_w_in), "w_out": (w_out, m_w_out, v_w_out), "w_gate": (w_gate, m_w_gate, v_w_gate),
             "w_up": (w_up, m_w_up, v_w_up), "w_down": (w_down, m_w_down, v_w_down)}

    def update(nm, parts, recv, after=None):
        w, m, v = big_w[nm]
        if nm in ("w_out", "w_down"):
            return _adamw_reduced("adamw_" + nm, parts, chip, recv, w[0], m[0], v[0], after)
        res = _adamw_reduced("adamw_" + nm, parts, chip, recv, w[0].T, m[0].T, v[0].T, after)
        return [t.T for t in res]

    big_out = {nm: update(nm, p, r, tok) for nm, p, r in zip(names_ffn, parts_ffn, recv_ffn)}
    ffn_done = sum(big_out[nm][1][:1, :1] for nm in names_ffn)
    rs_in, tok = reduce_mid("in", rs_in, ["w_in"], ffn_done)
    (du1,) = _mm("d_u1", [(dproj, win_t)], "nn", tm_big, D, 768, after=tok)
    grad_x, d_g1, d_meta = _pre_norm_bwd(h0, du1, dh1, mix_pre_norm, bsz, lp, seq, padf)

    late = jnp.stack([
        jnp.concatenate([jnp.pad(d_g1, ((0, 7), (0, 0))), jnp.pad(d_meta[:, p * DH:(p + 1) * DH], ((0, 0), (0, D - DH)))],
                        axis=0) for p in range(N_DEV)])
    (grp_late,), _ = _split_start("late_grads_start", [[late]], _alltoall_pattern, lambda s: s)
    _, (slabs_local,) = _split_wait("small_grads_wait", grp_small, _alltoall_pattern, grad_x)
    _, (late_local,) = _split_wait("late_grads_wait", grp_late, _alltoall_pattern, slabs_local)
    w_small = _pack_small([mix_pre_norm, mix_post_norm, ffn_pre_norm, ffn_post_norm], meta_tokens, conv_qkv[0],
                          conv_sc[0], a_log, dt_bias, gdn_norm)
    m_small = _pack_small([m_mix_pre_norm, m_mix_post_norm, m_ffn_pre_norm, m_ffn_post_norm], m_meta_tokens,
                          m_conv_qkv[0], m_conv_sc[0], m_a_log, m_dt_bias, m_gdn_norm)
    v_small = _pack_small([v_mix_pre_norm, v_mix_post_norm, v_ffn_pre_norm, v_ffn_post_norm], v_meta_tokens,
                          v_conv_qkv[0], v_conv_sc[0], v_a_log, v_dt_bias, v_gdn_norm)
    small_out, loss = _adamw_small(slabs_local, late_local, w_small, m_small, v_small)
    parts_in, recv_in = reduce_end("in", rs_in, small_out[1]["gdn_norm"])
    big_out["w_in"] = update("w_in", parts_in[0], recv_in[0])
    names = ["w_in", "w_out", "w_gate", "w_up", "w_down"]

    order = ["meta_tokens", "mix_pre_norm", "mix_post_norm", "ffn_pre_norm", "ffn_post_norm", "w_in", "conv_qkv",
             "a_log", "dt_bias", "gdn_norm", "conv_sc", "w_out", "w_gate", "w_up", "w_down"]
    outs = [loss[0, 0], grad_x]
    for kind in range(4):
        leaves = dict(small_out[kind])
        for nm in names:
            leaves[nm] = big_out[nm][kind][None]
        outs += [leaves[nm] for nm in order]
    return tuple(outs)
```

```python
import jax
import jax.numpy as jnp
from jax import lax
from jax.experimental import pallas as pl
from jax.experimental.pallas import tpu as pltpu

F32 = jnp.float32
BF16 = jnp.bfloat16
MESH = pl.DeviceIdType.MESH

D = 1024
N_META = 16
HEADS = 4
DH = 128
GW = HEADS * DH
SW = D - GW
FF = 2816
CH = 64
KQ = 4
KS = 3
EPS = 1e-6
IN_W = 3 * GW + GW + 2 * HEADS + 3 * SW
Q0, Z0, BA0, SX0, SB0, SC0 = 0, 1536, 2048, 2304, 2816, 3328
BA_W = SX0 - BA0
N_DEV = 8
VMEM_LIMIT = 56 * 1024 * 1024

ADAM_LR, ADAM_B1, ADAM_B2, ADAM_EPS, ADAM_WD, ADAM_STEP = 0.001, 0.9, 0.999, 1e-08, 0.01, 10


def _cparams(sem):
    return pltpu.CompilerParams(dimension_semantics=sem, vmem_limit_bytes=VMEM_LIMIT)


def _div_tile(n, target, mult=16):
    best = None
    for t in range(mult, min(n, target) + 1, mult):
        if n % t == 0:
            best = t
    assert best is not None, (n, target)
    return best


def _silu(x):
    return x * jax.nn.sigmoid(x)


def _dsilu(x):
    s = jax.nn.sigmoid(x)
    return s * (1.0 + x * (1.0 - s))


def _bdot(a, b, dims):
    return lax.dot_general(a.astype(BF16), b.astype(BF16), (dims, ((), ())), preferred_element_type=F32)


NN = ((1,), (0,))
NT = ((1,), (1,))
TN = ((0,), (0,))


def _mm(name, pairs, mode, tm, tn, tk, out_dtypes=(F32,), epilogue=None, extras=(), after=None):
    a0, b0 = pairs[0]
    if mode == "nn":
        (m, k), n = a0.shape, b0.shape[1]
    elif mode == "nt":
        (m, k), n = a0.shape, b0.shape[0]
    else:
        (k, m), n = a0.shape, b0.shape[1]
    assert m % tm == 0 and n % tn == 0 and k % tk == 0, (name, m, n, k, tm, tn, tk)
    nk = k // tk
    npair = len(pairs)
    nex = len(extras)
    nout = len(out_dtypes)
    ntok = 0 if after is None else 1
    dims = {"nn": NN, "nt": NT, "tn": TN}[mode]

    def body(*refs):
        ab = refs[:2 * npair]
        ex = refs[2 * npair:2 * npair + nex]
        outs = refs[2 * npair + nex + ntok:2 * npair + nex + ntok + nout]
        acc_ref = refs[-1]
        kk = pl.program_id(2)

        part = None
        for p in range(npair):
            d = lax.dot_general(ab[2 * p][...], ab[2 * p + 1][...], (dims, ((), ())), preferred_element_type=F32)
            part = d if part is None else part + d

        def finish(acc):
            res = epilogue(acc, *[e[...] for e in ex]) if epilogue is not None else (acc,)
            for o, r in zip(outs, res):
                o[...] = r.astype(o.dtype)

        if nk == 1:
            finish(part)
        else:
            @pl.when(kk == 0)
            def _():
                acc_ref[...] = part

            @pl.when(kk > 0)
            def _():
                acc_ref[...] += part

            @pl.when(kk == nk - 1)
            def _():
                finish(acc_ref[...])

    if mode == "nn":
        a_spec = pl.BlockSpec((tm, tk), lambda i, j, q: (i, q))
        b_spec = pl.BlockSpec((tk, tn), lambda i, j, q: (q, j))
    elif mode == "nt":
        a_spec = pl.BlockSpec((tm, tk), lambda i, j, q: (i, q))
        b_spec = pl.BlockSpec((tn, tk), lambda i, j, q: (j, q))
    else:
        a_spec = pl.BlockSpec((tk, tm), lambda i, j, q: (q, i))
        b_spec = pl.BlockSpec((tk, tn), lambda i, j, q: (q, j))
    o_spec = pl.BlockSpec((tm, tn), lambda i, j, q: (i, j))
    flat = [t for pr in pairs for t in pr]
    res = pl.pallas_call(
        body, name=name,
        grid=(m // tm, n // tn, nk),
        in_specs=[a_spec, b_spec] * npair + [o_spec] * nex + [pl.BlockSpec((8, DH), lambda i, j, q: (0, 0))] * ntok,
        out_specs=[o_spec] * nout,
        out_shape=[jax.ShapeDtypeStruct((m, n), dt) for dt in out_dtypes],
        scratch_shapes=[pltpu.VMEM((tm, tn) if nk > 1 else (8, DH), F32)],
        compiler_params=_cparams(("parallel", "parallel", "arbitrary")),
    )(*flat, *extras, *([] if after is None else [after]))
    return res


def _ffn_up(u2, wg, wu, tm, tn):
    m, k = u2.shape
    n = wg.shape[0]

    def body(a_ref, g_ref, u_ref, gate_ref, up_ref, act_ref):
        a = a_ref[...]
        g = lax.dot_general(a, g_ref[...], (NT, ((), ())), preferred_element_type=F32)
        u = lax.dot_general(a, u_ref[...], (NT, ((), ())), preferred_element_type=F32)
        gate_ref[...] = g.astype(BF16)
        up_ref[...] = u.astype(BF16)
        act_ref[...] = (_silu(g) * u).astype(BF16)

    o_spec = pl.BlockSpec((tm, tn), lambda j, i: (i, j))
    w_spec = pl.BlockSpec((tn, k), lambda j, i: (j, 0))
    return pl.pallas_call(
        body, name="ffn_up", grid=(n // tn, m // tm),
        in_specs=[pl.BlockSpec((tm, k), lambda j, i: (i, 0)), w_spec, w_spec],
        out_specs=[o_spec, o_spec, o_spec],
        out_shape=[jax.ShapeDtypeStruct((m, n), BF16)] * 3,
        compiler_params=_cparams(("parallel", "parallel")),
    )(u2, wg, wu)


def _rms(x, gain):
    r = lax.rsqrt(jnp.mean(x * x, axis=-1, keepdims=True) + EPS)
    return x * r * gain, r


def _rms_bwd(x, gain, dy):
    r = lax.rsqrt(jnp.mean(x * x, axis=-1, keepdims=True) + EPS)
    dyw = dy * gain
    dx = r * dyw - x * (r * r * r) * jnp.mean(dyw * x, axis=-1, keepdims=True)
    return dx, dy * x * r


def _row_specs(tr, width, nb):
    return pl.BlockSpec((tr, width), lambda b, j: (b * nb + j, 0))


def _vec_spec(width):
    return pl.BlockSpec((1, width), lambda b, j: (0, 0))


def _first_step(b, j):
    return jnp.logical_and(b == 0, j == 0)


def _pre_norm(x, head_rows, gain, bsz, lp, tr):
    nb = lp // tr
    first_real = head_rows.shape[0]
    assert first_real % 8 == 0 and first_real < tr
    x_spec = pl.BlockSpec((pl.Squeezed(), pl.Element(tr), pl.Element(D)),
                          lambda b, j: (b, pl.multiple_of(jnp.maximum(j * tr - first_real, 0), 8), 0))

    def body(x_ref, hd_ref, g_ref, h_ref, u_ref):
        j = pl.program_id(1)
        xt = x_ref[...]
        h = jnp.where(j == 0, jnp.concatenate([hd_ref[...], xt[:tr - first_real]], axis=0), xt)
        h_ref[...] = h
        u_ref[...] = _rms(h, g_ref[...])[0].astype(BF16)

    rs = _row_specs(tr, D, nb)
    return pl.pallas_call(
        body, name="mix_pre_norm", grid=(bsz, nb),
        in_specs=[x_spec, pl.BlockSpec((first_real, D), lambda b, j: (0, 0)), _vec_spec(D)],
        out_specs=[rs, rs],
        out_shape=[jax.ShapeDtypeStruct((bsz * lp, D), F32), jax.ShapeDtypeStruct((bsz * lp, D), BF16)],
        compiler_params=_cparams(("parallel", "parallel")),
    )(x, head_rows, gain)


def _mid_norms(h0, mix, g_post, g_pre, bsz, lp, tr):
    nb = lp // tr

    def body(h_ref, m_ref, gp_ref, gq_ref, h1_ref, u2_ref):
        h1 = h_ref[...] + _rms(m_ref[...], gp_ref[...])[0]
        h1_ref[...] = h1
        u2_ref[...] = _rms(h1, gq_ref[...])[0].astype(BF16)

    rs = _row_specs(tr, D, nb)
    return pl.pallas_call(
        body, name="mid_norms", grid=(bsz, nb),
        in_specs=[rs, rs, _vec_spec(D), _vec_spec(D)],
        out_specs=[rs, rs],
        out_shape=[jax.ShapeDtypeStruct(h0.shape, F32), jax.ShapeDtypeStruct(h0.shape, BF16)],
        compiler_params=_cparams(("parallel", "parallel")),
    )(h0, mix, g_post, g_pre)


def _loss_head(h1, ffn, tgt, g_post, bsz, lp, tr, first_real):
    nb = lp // tr
    assert first_real % 8 == 0 and first_real < tr
    tgt_spec = pl.BlockSpec((pl.Squeezed(), pl.Element(tr), pl.Element(D)),
                            lambda b, j: (b, pl.multiple_of(jnp.maximum(j * tr - first_real, 0), 8), 0))

    def body(h_ref, f_ref, t_ref, g_ref, dy_ref, dffn_ref, dg_ref, loss_ref):
        b, j = pl.program_id(0), pl.program_id(1)
        f = f_ref[...]
        g = g_ref[...]
        y = h_ref[...] + _rms(f, g)[0]
        t = j * tr + lax.broadcasted_iota(jnp.int32, (tr, 1), 0)
        tg = t_ref[...]
        tg = jnp.where(j == 0, jnp.concatenate([tg[tr - first_real:], tg[:tr - first_real]], axis=0), tg)
        err = jnp.where(t >= first_real, y - tg, 0.0)
        dy = err * (1.0 / D)
        dy_ref[...] = dy
        dx, dgc = _rms_bwd(f, g, dy)
        dffn_ref[...] = dx.astype(BF16)
        lsum = jnp.sum(jnp.sum(err * err, axis=0, keepdims=True), axis=1, keepdims=True) * (0.5 / D)

        @pl.when(_first_step(b, j))
        def _():
            dg_ref[...] = jnp.zeros_like(dg_ref)
            loss_ref[...] = jnp.zeros_like(loss_ref)

        dg_ref[...] += jnp.sum(dgc, axis=0, keepdims=True)
        loss_ref[...] += jnp.broadcast_to(lsum, loss_ref.shape)

    rs = _row_specs(tr, D, nb)
    return pl.pallas_call(
        body, name="loss_head", grid=(bsz, nb),
        in_specs=[rs, rs, tgt_spec, _vec_spec(D)],
        out_specs=[rs, rs, _vec_spec(D), _vec_spec(DH)],
        out_shape=[jax.ShapeDtypeStruct(h1.shape, F32), jax.ShapeDtypeStruct(h1.shape, BF16),
                   jax.ShapeDtypeStruct((1, D), F32), jax.ShapeDtypeStruct((1, DH), F32)],
        compiler_params=_cparams(("arbitrary", "arbitrary")),
    )(h1, ffn, tgt, g_post)


def _mid_norms_bwd(h1, mix, du2, dy, g_pre, g_post, bsz, lp, tr):
    nb = lp // tr

    def body(h_ref, m_ref, du_ref, dy_ref, gq_ref, gp_ref, dh1_ref, dmix_ref, dgq_ref, dgp_ref):
        b, j = pl.program_id(0), pl.program_id(1)
        dx, dgq = _rms_bwd(h_ref[...], gq_ref[...], du_ref[...])
        dh1 = dy_ref[...] + dx
        dh1_ref[...] = dh1
        dm, dgp = _rms_bwd(m_ref[...], gp_ref[...], dh1)
        dmix_ref[...] = dm.astype(BF16)

        @pl.when(_first_step(b, j))
        def _():
            dgq_ref[...] = jnp.zeros_like(dgq_ref)
            dgp_ref[...] = jnp.zeros_like(dgp_ref)

        dgq_ref[...] += jnp.sum(dgq, axis=0, keepdims=True)
        dgp_ref[...] += jnp.sum(dgp, axis=0, keepdims=True)

    rs = _row_specs(tr, D, nb)
    return pl.pallas_call(
        body, name="mid_norms_bwd", grid=(bsz, nb),
        in_specs=[rs, rs, rs, rs, _vec_spec(D), _vec_spec(D)],
        out_specs=[rs, rs, _vec_spec(D), _vec_spec(D)],
        out_shape=[jax.ShapeDtypeStruct(h1.shape, F32), jax.ShapeDtypeStruct(h1.shape, BF16),
                   jax.ShapeDtypeStruct((1, D), F32), jax.ShapeDtypeStruct((1, D), F32)],
        compiler_params=_cparams(("arbitrary", "arbitrary")),
    )(h1, mix, du2, dy, g_pre, g_post)


def _pre_norm_bwd(h0, du1, dh1, gain, bsz, lp, seq, padf):
    first_real = lp - seq
    tx = _div_tile(seq, 512)
    assert first_real % 8 == 0 and lp % first_real == 0

    def tokens(h_ref, du_ref, dh1_ref, g_ref, gx_ref, dg_ref):
        b, j = pl.program_id(0), pl.program_id(1)
        dx, dgc = _rms_bwd(h_ref[...], g_ref[...], du_ref[...])
        gx_ref[...] = dh1_ref[...] + dx

        @pl.when(_first_step(b, j))
        def _():
            dg_ref[...] = jnp.zeros_like(dg_ref)

        dg_ref[...] += jnp.sum(dgc, axis=0, keepdims=True)

    window = pl.BlockSpec((pl.Element(tx), pl.Element(D)),
                          lambda b, j: (pl.multiple_of(b * lp + first_real + j * tx, 8), 0))
    grad_x, dg_tokens = pl.pallas_call(
        tokens, name="pre_norm_bwd", grid=(bsz, seq // tx),
        in_specs=[window, window, window, _vec_spec(D)],
        out_specs=[pl.BlockSpec((pl.Squeezed(), tx, D), lambda b, j: (b, j, 0)), _vec_spec(D)],
        out_shape=[jax.ShapeDtypeStruct((bsz, seq, D), F32), jax.ShapeDtypeStruct((1, D), F32)],
        compiler_params=_cparams(("arbitrary", "arbitrary")),
    )(h0, du1, dh1, gain)

    def head(h_ref, du_ref, dh1_ref, g_ref, dgt_ref, dg_ref, dmeta_ref):
        b = pl.program_id(0)
        dx, dgc = _rms_bwd(h_ref[...], g_ref[...], du_ref[...])
        dh0 = dh1_ref[...] + dx

        @pl.when(b == 0)
        def _():
            dg_ref[...] = dgt_ref[...]
            dmeta_ref[...] = jnp.zeros_like(dmeta_ref)

        dg_ref[...] += jnp.sum(dgc, axis=0, keepdims=True)
        dmeta_ref[...] += dh0[padf:padf + N_META, :]

    rows = pl.BlockSpec((first_real, D), lambda b: (b * (lp // first_real), 0))
    vec = pl.BlockSpec((1, D), lambda b: (0, 0))
    dg, dmeta = pl.pallas_call(
        head, name="pre_norm_bwd_head", grid=(bsz,),
        in_specs=[rows, rows, rows, vec, vec],
        out_specs=[vec, pl.BlockSpec((N_META, D), lambda b: (0, 0))],
        out_shape=[jax.ShapeDtypeStruct((1, D), F32), jax.ShapeDtypeStruct((N_META, D), F32)],
        compiler_params=_cparams(("arbitrary",)),
    )(h0, du1, dh1, gain, dg_tokens)
    return grad_x, dg, dmeta


def _shift_down(x, s, lp):
    return x if s == 0 else pltpu.roll(x, s, axis=0)


def _shift_up(x, s, lp):
    return x if s == 0 else pltpu.roll(x, lp - s, axis=0)


def _conv_fwd(x, w, taps, lp):
    y = None
    for i in range(taps):
        term = _shift_down(x, taps - 1 - i, lp) * w[i:i + 1, :]
        y = term if y is None else y + term
    return y


def _conv_bwd(x, w, dy, taps, lp):
    dx = None
    dws = []
    for i in range(taps):
        s = taps - 1 - i
        term = _shift_up(dy, s, lp) * w[i:i + 1, :]
        dx = term if dx is None else dx + term
        dws.append(jnp.sum(dy * _shift_down(x, s, lp), axis=0, keepdims=True))
    return dx, jnp.concatenate(dws, axis=0)


def _slab(lp, col0):
    return pl.BlockSpec((lp, DH), lambda b, s: (b, col0 // DH + s))


def _qkv_act(c, s):
    y = _silu(c)
    r = lax.rsqrt(jnp.sum(y * y, axis=-1, keepdims=True) + EPS)
    scale = jnp.where(s < HEADS, DH ** -0.5, 1.0)
    return y, r, scale


def _gdn_prep(proj, conv_qkv, bsz, lp, padf):
    def body(p_ref, w_ref, o_ref):
        s = pl.program_id(1)
        c = _conv_fwd(p_ref[...].astype(F32), w_ref[...], KQ, lp)
        y, r, scale = _qkv_act(c, s)
        out = jnp.where(s < 2 * HEADS, y * (r * scale), y)
        t = lax.broadcasted_iota(jnp.int32, (lp, 1), 0)
        o_ref[...] = jnp.where(t >= padf, out, 0.0)

    return pl.pallas_call(
        body, name="gdn_prep", grid=(bsz, 3 * HEADS),
        in_specs=[_slab(lp, Q0), pl.BlockSpec((KQ, DH), lambda b, s: (0, s))],
        out_specs=_slab(lp, 0),
        out_shape=jax.ShapeDtypeStruct((bsz * lp, 3 * GW), F32),
        compiler_params=_cparams(("parallel", "parallel")),
    )(proj, conv_qkv)


def _gdn_prep_bwd(proj, conv_qkv, dq, dk, dv, bsz, lp, padf):
    def body(p_ref, w_ref, dq_ref, dk_ref, dv_ref, dp_ref, dw_ref):
        b, s = pl.program_id(0), pl.program_id(1)
        x = p_ref[...].astype(F32)
        w = w_ref[...]
        c = _conv_fwd(x, w, KQ, lp)
        y, r, scale = _qkv_act(c, s)
        d = jnp.where(s < HEADS, dq_ref[...], jnp.where(s < 2 * HEADS, dk_ref[...], dv_ref[...]))
        dn = d * scale
        n = y * r
        dy_norm = r * (dn - n * jnp.sum(dn * n, axis=-1, keepdims=True))
        dyy = jnp.where(s < 2 * HEADS, dy_norm, dn)
        t = lax.broadcasted_iota(jnp.int32, (lp, 1), 0)
        dc = jnp.where(t >= padf, dyy * _dsilu(c), 0.0)
        dx, dw = _conv_bwd(x, w, dc, KQ, lp)
        dp_ref[...] = jnp.where(t >= padf, dx, 0.0).astype(BF16)

        @pl.when(jnp.logical_and(b == 0, s == 0))
        def _():
            dw_ref[...] = jnp.zeros_like(dw_ref)

        col_block = jnp.right_shift(lax.broadcasted_iota(jnp.int32, (KQ, 3 * GW), 1), DH.bit_length() - 1)
        dw_ref[...] += jnp.where(col_block == s, jnp.tile(dw, (1, 3 * HEADS)), 0.0)

    sl = lambda col0: pl.BlockSpec((lp, DH), lambda b, s: (b, col0 // DH + s))
    part = lambda k: pl.BlockSpec((lp, DH), lambda b, s: (b, jnp.clip(s - k * HEADS, 0, HEADS - 1)))
    return pl.pallas_call(
        body, name="gdn_prep_bwd", grid=(bsz, 3 * HEADS),
        in_specs=[sl(Q0), pl.BlockSpec((KQ, DH), lambda b, s: (0, s)), part(0), part(1), part(2)],
        out_specs=[sl(0), pl.BlockSpec((KQ, 3 * GW), lambda b, s: (0, 0))],
        out_shape=[jax.ShapeDtypeStruct((bsz * lp, 3 * GW), BF16), jax.ShapeDtypeStruct((KQ, 3 * GW), F32)],
        compiler_params=_cparams(("arbitrary", "arbitrary")),
    )(proj, conv_qkv, dq, dk, dv)


def _softplus(x):
    e = jnp.exp(-jnp.abs(x))
    one_e = 1.0 + e
    l1p = jnp.where(one_e == 1.0, e, jnp.log(one_e) * (e / (one_e - 1.0)))
    return jnp.maximum(x, 0.0) + l1p


def _tri(n, kind):
    i = lax.broadcasted_iota(jnp.int32, (n, n), 0)
    j = lax.broadcasted_iota(jnp.int32, (n, n), 1)
    return {"incl": i >= j, "strict": i > j, "upper": i <= j}[kind]


def _gates(proj, alog_v, dtb_v, bsz, lp, padf):
    nchunk = lp // CH

    def body(p_ref, al_ref, dt_ref, gb_ref, cs_ref):
        x = p_ref[...].astype(F32)
        lane = lax.broadcasted_iota(jnp.int32, (lp, DH), 1)
        t = lax.broadcasted_iota(jnp.int32, (lp, DH), 0)
        beta = jax.nn.sigmoid(x)
        g = -jnp.exp(al_ref[...]) * _softplus(x + dt_ref[...])
        gb = jnp.where(lane < HEADS, beta, jnp.where(lane < 2 * HEADS, g, 0.0))
        gb_ref[...] = jnp.where(t >= padf, gb, 0.0)
        ltri = _tri(CH, "incl").astype(BF16)
        rest = [gb_ref[pl.ds(n * CH, CH), :] for n in range(nchunk)]
        sums = None
        for _ in range(3):
            piece = [v.astype(BF16) for v in rest]
            rest = [v - p.astype(F32) for v, p in zip(rest, piece)]
            part = [jnp.dot(ltri, p, preferred_element_type=F32) for p in piece]
            sums = part if sums is None else [a + b for a, b in zip(sums, part)]
        for n in range(nchunk):
            cs_ref[pl.ds(n * CH, CH), :] = sums[n]

    blk = pl.BlockSpec((lp, DH), lambda b: (b, 0))
    vec = pl.BlockSpec((1, DH), lambda b: (0, 0))
    out = pl.BlockSpec((lp, DH), lambda b: (b, 0))
    return pl.pallas_call(
        body, name="gates", grid=(bsz,),
        in_specs=[blk, vec, vec], out_specs=[out, out],
        out_shape=[jax.ShapeDtypeStruct((bsz * lp, DH), F32)] * 2,
        compiler_params=_cparams(("parallel",)),
    )(proj, alog_v, dtb_v)


def _gates_bwd(proj, alog_v, dtb_v, dbeta_b, dg_b, gb, bsz, lp, padf):
    def body(p_ref, al_ref, dt_ref, db_ref, dg_ref, gb_ref, dl_ref, dv_ref):
        b = pl.program_id(0)
        x = p_ref[...].astype(F32)
        lane = lax.broadcasted_iota(jnp.int32, (lp, DH), 1)
        t = lax.broadcasted_iota(jnp.int32, (lp, DH), 0)
        dbeta = jnp.zeros((lp, DH), F32)
        dg = jnp.zeros((lp, DH), F32)
        for h in range(HEADS):
            dbeta = jnp.where(lane == h, db_ref[h], dbeta)
            dg = jnp.where(lane == HEADS + h, dg_ref[h], dg)
        real = t >= padf
        dbeta = jnp.where(real, dbeta, 0.0)
        dg = jnp.where(real, dg, 0.0)
        sb = jax.nn.sigmoid(x)
        neg_ea = -jnp.exp(al_ref[...])
        dsp = dg * neg_ea * jax.nn.sigmoid(x + dt_ref[...])
        dl = dbeta * sb * (1.0 - sb) + dsp
        dl_ref[...] = jnp.concatenate([dl, jnp.zeros((lp, BA_W - DH), F32)], axis=1).astype(BF16)
        dalog = jnp.sum(dg * gb_ref[...], axis=0, keepdims=True)
        ddt = jnp.sum(dsp, axis=0, keepdims=True)

        @pl.when(b == 0)
        def _():
            dv_ref[...] = jnp.zeros_like(dv_ref)

        dv_ref[...] += jnp.concatenate([dalog, ddt], axis=0)

    blk = pl.BlockSpec((lp, DH), lambda b: (b, 0))
    vec = pl.BlockSpec((1, DH), lambda b: (0, 0))
    hb = pl.BlockSpec((HEADS, lp, DH), lambda b: (b, 0, 0))
    return pl.pallas_call(
        body, name="gates_bwd", grid=(bsz,),
        in_specs=[blk, vec, vec, hb, hb, pl.BlockSpec((lp, DH), lambda b: (b, 0))],
        out_specs=[pl.BlockSpec((lp, BA_W), lambda b: (b, 0)), pl.BlockSpec((2, DH), lambda b: (0, 0))],
        out_shape=[jax.ShapeDtypeStruct((bsz * lp, BA_W), BF16), jax.ShapeDtypeStruct((2, DH), F32)],
        compiler_params=_cparams(("arbitrary",)),
    )(proj, alog_v, dtb_v, dbeta_b, dg_b, gb)


def _mix_heads(proj, o, gdn_norm, conv_sc, bsz, lp, padf):
    def body(o_ref, z_ref, gn_ref, sx_ref, sb_ref, sc_ref, w_ref, og_ref, os_ref):
        og_ref[...] = (_rms(o_ref[...], gn_ref[...])[0] * _silu(z_ref[...].astype(F32))).astype(BF16)
        conv = _conv_fwd(sc_ref[...].astype(F32) * sx_ref[...].astype(F32), w_ref[...], KS, lp)
        t = lax.broadcasted_iota(jnp.int32, (lp, 1), 0)
        os_ref[...] = jnp.where(t >= padf, sb_ref[...].astype(F32) * conv, 0.0).astype(BF16)

    half = _slab(lp, 0)
    return pl.pallas_call(
        body, name="mix_heads", grid=(bsz, HEADS),
        in_specs=[half, _slab(lp, Z0), pl.BlockSpec((1, DH), lambda b, s: (0, 0)),
                  _slab(lp, SX0), _slab(lp, SB0), _slab(lp, SC0), pl.BlockSpec((KS, DH), lambda b, s: (0, s))],
        out_specs=[half, half],
        out_shape=[jax.ShapeDtypeStruct((bsz * lp, GW), BF16)] * 2,
        compiler_params=_cparams(("parallel", "parallel")),
    )(o, proj, gdn_norm, proj, proj, proj, conv_sc)


def _mix_heads_bwd(proj, o, gdn_norm, conv_sc, dmixin, bsz, lp, padf):
    def body(o_ref, z_ref, gn_ref, sx_ref, sb_ref, sc_ref, w_ref, dg_ref, ds_ref,
             do_ref, dz_ref, dsx_ref, dsb_ref, dsc_ref, dgn_ref, dw_ref):
        s, b = pl.program_id(0), pl.program_id(1)
        t = lax.broadcasted_iota(jnp.int32, (lp, 1), 0)
        real = t >= padf
        o, z, gn, d = o_ref[...], z_ref[...].astype(F32), gn_ref[...], dg_ref[...]
        sz = _silu(z)
        on = _rms(o, gn)[0]
        dz_ref[...] = (d * on * _dsilu(z)).astype(BF16)
        dox, dgn = _rms_bwd(o, gn, d * sz)
        do_ref[...] = dox
        sx, sb, sc = sx_ref[...].astype(F32), sb_ref[...].astype(F32), sc_ref[...].astype(F32)
        w, e = w_ref[...], ds_ref[...]
        e = jnp.where(real, e, 0.0)
        xin = sc * sx
        conv = _conv_fwd(xin, w, KS, lp)
        dsb_ref[...] = (e * conv).astype(BF16)
        dxin, dw = _conv_bwd(xin, w, e * sb, KS, lp)
        dxin = jnp.where(real, dxin, 0.0)
        dsx_ref[...] = (dxin * sc).astype(BF16)
        dsc_ref[...] = (dxin * sx).astype(BF16)

        @pl.when(jnp.logical_and(s == 0, b == 0))
        def _():
            dgn_ref[...] = jnp.zeros_like(dgn_ref)

        @pl.when(b == 0)
        def _():
            dw_ref[...] = jnp.zeros_like(dw_ref)

        dgn_ref[...] += jnp.sum(dgn, axis=0, keepdims=True)
        dw_ref[...] += dw

    sl = lambda col0: pl.BlockSpec((lp, DH), lambda s, b: (b, col0 // DH + s))
    half = sl(0)
    return pl.pallas_call(
        body, name="mix_heads_bwd", grid=(HEADS, bsz),
        in_specs=[half, sl(Z0), pl.BlockSpec((1, DH), lambda s, b: (0, 0)), sl(SX0), sl(SB0), sl(SC0),
                  pl.BlockSpec((KS, DH), lambda s, b: (0, s)), sl(0), sl(GW)],
        out_specs=[half] * 5 + [pl.BlockSpec((1, DH), lambda s, b: (0, 0)), pl.BlockSpec((KS, DH), lambda s, b: (0, s))],
        out_shape=[jax.ShapeDtypeStruct((bsz * lp, GW), F32)] + [jax.ShapeDtypeStruct((bsz * lp, GW), BF16)] * 4
        + [jax.ShapeDtypeStruct((1, DH), F32), jax.ShapeDtypeStruct((KS, SW), F32)],
        compiler_params=_cparams(("arbitrary", "arbitrary")),
    )(o, proj, gdn_norm, proj, proj, proj, conv_sc, dmixin, dmixin)


def _split_dot(a, b, dims=NN):
    ah = a.astype(BF16)
    al = (a - ah.astype(F32)).astype(BF16)
    bh = b.astype(BF16)
    bl = (b - bh.astype(F32)).astype(BF16)

    def d(x, y):
        return lax.dot_general(x, y, (dims, ((), ())), preferred_element_type=F32)

    return d(ah, bh) + (d(ah, bl) + d(al, bh))


def _eye(n):
    return (lax.broadcasted_iota(jnp.int32, (n, n), 0) == lax.broadcasted_iota(jnp.int32, (n, n), 1)).astype(F32)


def _each(f, *lists):
    return [f(*xs) for xs in zip(*lists)]


def _inv_unit_lower(a_list):
    eye = _eye(CH)
    p = _each(lambda a: -a, a_list)
    t = _each(lambda x: eye + x, p)
    p = _each(lambda x: _split_dot(x, x), p)
    for level in range(5):
        if level < 4:
            prod = _each(lambda tt, pp: _split_dot(jnp.concatenate([tt, pp], axis=0), pp), t, p)
            t = _each(lambda tt, pr: tt + pr[:CH], t, prod)
            p = _each(lambda pr: pr[CH:], prod)
        else:
            t = _each(lambda tt, pp: tt + _split_dot(tt, pp), t, p)
    return t


def _chunk_gates(gc_ref, be_ref, gt_ref, g):
    rows = pl.ds(g * CH, CH)
    gc = gc_ref[0, rows, :]
    be = be_ref[0, rows, :]
    gcl = gc_ref[0, pl.ds(g * CH + CH - 1, 1), :]
    gt = gt_ref[0, g]
    incl = _tri(CH, "incl")
    dec = jnp.where(incl, jnp.exp(jnp.where(incl, gc[:, :CH] - gt, 0.0)), 0.0)
    return rows, be, dec, jnp.exp(gc), jnp.exp(gcl - gc), jnp.exp(gcl)


def _chunk_group(nchunk):
    return 3 if nchunk % 3 == 0 else 1


def _local_group(nchunk):
    return 11 if nchunk % 11 == 0 else _chunk_group(nchunk)


def _gdn_local_specs(lp, grp):
    ngrp = lp // (grp * CH)
    head = lambda col0: pl.BlockSpec((grp * CH, DH), lambda i, j: ((i // HEADS) * ngrp + j, col0 // DH + i % HEADS))
    lane_b = pl.BlockSpec((1, grp * CH, DH), lambda i, j: (i, j, 0))
    per_chunk = lambda r, c: pl.BlockSpec((1, grp, r, c), lambda i, j: (i, j, 0, 0))
    return ngrp, head, lane_b, per_chunk


def _gdn_local_fwd(qkv, gcb, bb, gct, bsz, lp):
    nchunk = lp // CH
    grp = _local_group(nchunk)
    ngrp, head, lane_b, per_chunk = _gdn_local_specs(lp, grp)
    bh = bsz * HEADS

    def body(q_ref, k_ref, v_ref, gc_ref, be_ref, gt_ref,
             u_ref, l1_ref, l2_ref, l3_ref, wt_ref, kd_ref, ti_ref, gl_ref):
        incl, strict = _tri(CH, "incl"), _tri(CH, "strict")
        rows, be, dec, eg, ekd, gl = zip(*[_chunk_gates(gc_ref, be_ref, gt_ref, g) for g in range(grp)])
        q = [q_ref[r, :] for r in rows]
        k = [k_ref[r, :] for r in rows]
        v = [v_ref[r, :] for r in rows]
        kb = _each(lambda x, y: x * y, k, be)
        a = _each(lambda x, y, d: jnp.where(strict, _bdot(x, y, NT) * d, 0.0), kb, k, dec)
        tinv = _inv_unit_lower(a)
        u = _each(lambda t, x, y: _bdot(t, x * y, NN), tinv, v, be)
        w = _each(lambda t, x, y: _bdot(t, x * y, NN), tinv, kb, eg)
        qk = _each(lambda x, y, d: jnp.where(incl, _bdot(x, y, NT) * d, 0.0), q, k, dec)
        qk_t = _each(lambda x: x.T, qk)
        qd = _each(lambda x, y: x * y, q, eg)
        kd = _each(lambda x, y: x * y, k, ekd)
        kd_t = _each(lambda x: x.T, kd)
        qd_t = _each(lambda x: x.T, qd)
        w_t = _each(lambda x: x.T, w)
        for g in range(grp):
            u_ref[rows[g], :] = u[g]
            l1_ref[0, g] = jnp.concatenate([w[g], qd[g]], axis=0).astype(BF16)
            l2_ref[0, g] = jnp.concatenate([qk[g], kd_t[g]], axis=0).astype(BF16)
            l3_ref[0, g] = jnp.concatenate([qk_t[g], qd_t[g]], axis=0).astype(BF16)
            wt_ref[0, g] = w_t[g].astype(BF16)
            kd_ref[0, g] = kd[g].astype(BF16)
            ti_ref[0, g] = tinv[g]
            gl_ref[0, g] = gl[g]

    r = bsz * lp
    shapes = [((r, GW), F32), ((bh, nchunk, 2 * CH, DH), BF16), ((bh, nchunk, 3 * CH, CH), BF16),
              ((bh, nchunk, 3 * CH, CH), BF16), ((bh, nchunk, DH, CH), BF16), ((bh, nchunk, CH, DH), BF16),
              ((bh, nchunk, CH, CH), F32), ((bh, nchunk, 1, DH), F32)]
    return pl.pallas_call(
        body, name="gdn_local_fwd", grid=(bh, ngrp),
        in_specs=[head(0), head(GW), head(2 * GW), lane_b, lane_b, per_chunk(1, CH)],
        out_specs=[head(0), per_chunk(2 * CH, DH), per_chunk(3 * CH, CH), per_chunk(3 * CH, CH), per_chunk(DH, CH),
                   per_chunk(CH, DH), per_chunk(CH, CH), per_chunk(1, DH)],
        out_shape=[jax.ShapeDtypeStruct(s, dt) for s, dt in shapes],
        compiler_params=_cparams(("parallel", "parallel")),
    )(qkv, qkv, qkv, gcb, bb, gct)


def _scan_specs(bsz, lp, grp, order):
    ngrp = lp // (grp * CH)
    bh = bsz * HEADS
    seq = lambda: pl.BlockSpec((bsz, grp * CH, GW), lambda s: (0, order(s, ngrp), 0))
    per_chunk = lambda r, c: pl.BlockSpec((bh, grp, r, c), lambda s: (0, order(s, ngrp), 0, 0))
    return ngrp, bh, seq, per_chunk


def _gdn_scan_fwd(u, l1, l2, gl, bsz, lp):
    nchunk = lp // CH
    grp = _chunk_group(nchunk)
    ngrp, bh, seq, per_chunk = _scan_specs(bsz, lp, grp, lambda s, n: s)

    def body(u_ref, l1_ref, l2_ref, gl_ref, o_ref, vn_ref, st_ref, s_scr):
        @pl.when(pl.program_id(0) == 0)
        def _():
            s_scr[...] = jnp.zeros_like(s_scr)

        chains = list(range(bh))
        where = [(i // HEADS, slice((i % HEADS) * DH, (i % HEADS + 1) * DH)) for i in chains]

        def dot(a, b):
            return jnp.dot(a, b.astype(BF16), preferred_element_type=F32)

        for c in range(grp):
            rows = slice(c * CH, (c + 1) * CH)
            s = [s_scr[i] for i in chains]
            for i in chains:
                st_ref[i, c] = s[i]
            x = _each(lambda i, si: dot(l1_ref[i, c], si), chains, s)
            vnew = _each(lambda bc, xi: u_ref[bc[0], rows, bc[1]] - xi[:CH], where, x)
            y = _each(lambda i, vi: dot(l2_ref[i, c], vi), chains, vnew)
            for i, (b, cols) in zip(chains, where):
                o_ref[b, rows, cols] = x[i][CH:] + y[i][:CH]
                vn_ref[b, rows, cols] = vnew[i].astype(BF16)
                s_scr[i] = s[i] * gl_ref[i, c] + y[i][CH:]

    return pl.pallas_call(
        body, name="gdn_scan_fwd", grid=(ngrp,),
        in_specs=[seq(), per_chunk(2 * CH, DH), per_chunk(3 * CH, CH), per_chunk(1, DH)],
        out_specs=[seq(), seq(), per_chunk(DH, DH)],
        out_shape=[jax.ShapeDtypeStruct((bsz, lp, GW), F32), jax.ShapeDtypeStruct((bsz, lp, GW), BF16),
                   jax.ShapeDtypeStruct((bh, nchunk, DH, DH), F32)],
        scratch_shapes=[pltpu.VMEM((bh, DH, DH), F32)],
        compiler_params=_cparams(("arbitrary",)),
    )(u.reshape(bsz, lp, GW), l1, l2, gl)


def _gdn_scan_bwd(d_o, l3, wt, kd, gl, bsz, lp):
    nchunk = lp // CH
    grp = _chunk_group(nchunk)
    ngrp, bh, seq, per_chunk = _scan_specs(bsz, lp, grp, lambda s, n: n - 1 - s)

    def body(do_ref, l3_ref, wt_ref, kd_ref, gl_ref, dvn_ref, dst_ref, ds_scr):
        @pl.when(pl.program_id(0) == 0)
        def _():
            ds_scr[...] = jnp.zeros_like(ds_scr)

        chains = list(range(bh))
        where = [(i // HEADS, slice((i % HEADS) * DH, (i % HEADS + 1) * DH)) for i in chains]

        def dot(a, b):
            return jnp.dot(a, b.astype(BF16), preferred_element_type=F32)

        for c in reversed(range(grp)):
            rows = slice(c * CH, (c + 1) * CH)
            ds = [ds_scr[i] for i in chains]
            for i in chains:
                dst_ref[i, c] = ds[i]
            p = _each(lambda i, bc: dot(l3_ref[i, c], do_ref[bc[0], rows, bc[1]]), chains, where)
            dvn = _each(lambda i, pi, di: pi[:CH] + dot(kd_ref[i, c], di), chains, p, ds)
            wd = _each(lambda i, vi: dot(wt_ref[i, c], vi), chains, dvn)
            for i, (b, cols) in zip(chains, where):
                dvn_ref[b, rows, cols] = dvn[i].astype(BF16)
                ds_scr[i] = ds[i] * gl_ref[i, c] + p[i][CH:] - wd[i]

    return pl.pallas_call(
        body, name="gdn_scan_bwd", grid=(ngrp,),
        in_specs=[seq(), per_chunk(3 * CH, CH), per_chunk(DH, CH), per_chunk(CH, DH), per_chunk(1, DH)],
        out_specs=[seq(), per_chunk(DH, DH)],
        out_shape=[jax.ShapeDtypeStruct((bsz, lp, GW), BF16), jax.ShapeDtypeStruct((bh, nchunk, DH, DH), F32)],
        scratch_shapes=[pltpu.VMEM((bh, DH, DH), F32)],
        compiler_params=_cparams(("arbitrary",)),
    )(d_o.reshape(bsz, lp, GW), l3, wt, kd, gl)


def _gdn_local_bwd(qkv, gcb, bb, gct, tinv_all, states, dstates, vnew, dvnew, d_o, bsz, lp):
    nchunk = lp // CH
    grp = _local_group(nchunk)
    ngrp, head, lane_b, per_chunk = _gdn_local_specs(lp, grp)
    bh = bsz * HEADS

    def body(q_ref, k_ref, v_ref, gc_ref, be_ref, gt_ref, ti_ref, st_ref, dst_ref, vn_ref, dvn_ref, do_ref,
             dq_ref, dk_ref, dv_ref, dg_ref, db_ref):
        incl, strict = _tri(CH, "incl"), _tri(CH, "strict")
        upper = _tri(CH, "upper").astype(F32)
        ones = jnp.ones((CH, DH), F32)
        last = lax.broadcasted_iota(jnp.int32, (CH, 1), 0) == CH - 1

        def rsum(x):
            return jnp.sum(x, axis=-1, keepdims=True)

        def mul(xs, ys):
            return _each(lambda x, y: x * y, xs, ys)

        rows, be, dec, eg, ekd, gl = zip(*[_chunk_gates(gc_ref, be_ref, gt_ref, g) for g in range(grp)])
        q = [q_ref[r, :] for r in rows]
        k = [k_ref[r, :] for r in rows]
        v = [v_ref[r, :] for r in rows]
        vnew = [vn_ref[r, :] for r in rows]
        dvn = [dvn_ref[r, :] for r in rows]
        do = [do_ref[r, :] for r in rows]
        tinv = [ti_ref[0, g] for g in range(grp)]
        s = [st_ref[0, g] for g in range(grp)]
        ds = [dst_ref[0, g] for g in range(grp)]
        kb, vb, qd, kd = mul(k, be), mul(v, be), mul(q, eg), mul(k, ekd)
        kbg = mul(kb, eg)
        a = _each(lambda x, y, d: jnp.where(strict, _bdot(x, y, NT) * d, 0.0), kb, k, dec)
        qk = _each(lambda x, y, d: jnp.where(incl, _bdot(x, y, NT) * d, 0.0), q, k, dec)
        dqk = _each(lambda x, y: jnp.where(incl, _bdot(x, y, NT), 0.0), do, vnew)
        dqd = _each(lambda x, y: _bdot(x, y, NT), do, s)
        dkd = _each(lambda x, y: _bdot(x, y, NT), vnew, ds)
        dgl = _each(lambda x, y: jnp.sum(rsum(x * y), axis=0, keepdims=True), s, ds)
        dw = _each(lambda x, y: -_bdot(x, y, NT), dvn, s)
        dvb = _each(lambda t, x: _bdot(t, x, TN), tinv, dvn)
        dkbg = _each(lambda t, x: _bdot(t, x, TN), tinv, dw)
        dt = _each(lambda x, y, z, w: _bdot(x, y, NT) + _bdot(z, w, NT), dvn, vb, dw, kbg)
        tdt = _each(lambda t, x: _split_dot(t, x, TN), tinv, dt)
        da = _each(lambda x, t: jnp.where(strict, -_split_dot(x, t, NT), 0.0), tdt, tinv)
        dp, dr = mul(da, dec), mul(dqk, dec)
        dkb = _each(lambda x, y, z, e: _bdot(x, y, NN) + z * e, dp, k, dkbg, eg)
        dk = _each(lambda p_, kb_, r_, q_, dkd_, ekd_, dkb_, be_:
                   _bdot(p_, kb_, TN) + _bdot(r_, q_, TN) + dkd_ * ekd_ + dkb_ * be_,
                   dp, kb, dr, q, dkd, ekd, dkb, be)
        dq = _each(lambda r_, k_, dqd_, eg_: _bdot(r_, k_, NN) + dqd_ * eg_, dr, k, dqd, eg)
        m = _each(lambda da_, a_, dqk_, qk_: da_ * a_ + dqk_ * qk_, da, a, dqk, qk)
        m_cols = _each(lambda x: _split_dot(x, ones, TN), m)
        kd_term = _each(lambda x, y: rsum(x * y), dkd, kd)
        dgc = _each(lambda m_, mc, dqd_, qd_, kt, dkbg_, kbg_, dgl_, gl_:
                    rsum(m_) - mc + rsum(dqd_ * qd_) - kt + rsum(dkbg_ * kbg_)
                    + jnp.where(last, jnp.sum(kt, axis=0, keepdims=True) + dgl_ * gl_, 0.0),
                    m, m_cols, dqd, qd, kd_term, dkbg, kbg, dgl, gl)
        dg = _each(lambda x: _split_dot(upper, x, NN), dgc)
        for g in range(grp):
            dq_ref[rows[g], :] = dq[g]
            dk_ref[rows[g], :] = dk[g]
            dv_ref[rows[g], :] = dvb[g] * be[g]
            dg_ref[0, rows[g], :] = dg[g]
            db_ref[0, rows[g], :] = jnp.broadcast_to(rsum(dkb[g] * k[g]) + rsum(dvb[g] * v[g]), (CH, DH))

    r = bsz * lp
    return pl.pallas_call(
        body, name="gdn_local_bwd", grid=(bh, ngrp),
        in_specs=[head(0), head(GW), head(2 * GW), lane_b, lane_b, per_chunk(1, CH), per_chunk(CH, CH),
                  per_chunk(DH, DH), per_chunk(DH, DH), head(0), head(0), head(0)],
        out_specs=[head(0), head(0), head(0), lane_b, lane_b],
        out_shape=[jax.ShapeDtypeStruct((r, GW), F32)] * 3 + [jax.ShapeDtypeStruct((bh, lp, DH), F32)] * 2,
        compiler_params=_cparams(("parallel", "parallel")),
    )(qkv, qkv, qkv, gcb, bb, gct, tinv_all, states, dstates, vnew, dvnew, d_o)


def _place():
    return lax.axis_index("x"), lax.axis_index("y"), lax.axis_index("c")


def _any_specs(n):
    return [pl.BlockSpec(memory_space=pl.ANY)] * n


def _all_gather(name, shards):
    n = len(shards)

    def body(*refs):
        ins, outs = refs[:n], refs[n:2 * n]
        send_sems, recv_sems, local_sems = refs[2 * n:2 * n + 3]
        stage = refs[2 * n + 3:]
        x, y, c = _place()
        me, sibling = (x, y, c), (x, y, 1 - c)
        chips = [(1 - x, y), (x, 1 - y), (1 - x, 1 - y)]

        def slot(i, p):
            return outs[i].at[4 * p[0] + 2 * p[1] + p[2]]

        def copy(i, k, block, to, src=None):
            return pltpu.make_async_remote_copy(
                src_ref=slot(i, block) if src is None else src, dst_ref=slot(i, block),
                send_sem=send_sems.at[i, k], recv_sem=recv_sems.at[i, k], device_id=to, device_id_type=MESH)

        load = [pltpu.make_async_copy(ins[i], stage[i], local_sems.at[i]) for i in range(n)]
        for cp in load:
            cp.start()
        first = []
        for i in range(n):
            first.append(copy(i, 0, me, sibling, src=ins[i]))
            first += [copy(i, 1 + j, me, (*chip, c), src=ins[i]) for j, chip in enumerate(chips)]
        for cp in first:
            cp.start()
        for cp in load:
            cp.wait()
        mine = [pltpu.make_async_copy(stage[i], slot(i, me), local_sems.at[i]) for i in range(n)]
        for cp in mine:
            cp.start()
        passed = []
        for j, chip in enumerate(chips):
            for i in range(n):
                copy(i, 1 + j, (*chip, c), me).wait_recv()
                fwd = copy(i, 4 + j, (*chip, c), sibling)
                fwd.start()
                passed.append(fwd)
        for i in range(n):
            copy(i, 0, sibling, me).wait_recv()
            for j, chip in enumerate(chips):
                copy(i, 4 + j, (*chip, 1 - c), me).wait_recv()
        for cp in first + passed:
            cp.wait_send()
        for cp in mine:
            cp.wait()

    return pl.pallas_call(
        body, name=name,
        in_specs=_any_specs(n), out_specs=_any_specs(n),
        out_shape=[jax.ShapeDtypeStruct((N_DEV,) + s.shape, s.dtype) for s in shards],
        scratch_shapes=[pltpu.SemaphoreType.DMA((n, 7)), pltpu.SemaphoreType.DMA((n, 7)),
                        pltpu.SemaphoreType.DMA((n,))] + [pltpu.VMEM(s.shape, s.dtype) for s in shards],
    )(*shards)


HBM_SPEC = pl.BlockSpec(memory_space=pltpu.HBM)
SEM_SPEC = pl.BlockSpec(memory_space=pltpu.SEMAPHORE)
EFFECT = pltpu.SideEffectType.DATAFLOW_SIDE_EFFECTING
N_PEER = N_DEV - 1


def _peer(r):
    x, y, c = _place()
    return ((1 - x) if r & 4 else x, (1 - y) if r & 2 else y, (1 - c) if r & 1 else c)


def _slot_of(p):
    return 4 * p[0] + 2 * p[1] + p[2]


def _hbm(a):
    return pltpu.with_memory_space_constraint(a, pltpu.HBM)


def _gather_pattern(src, land):
    me = _place()
    return [(src, land.at[_slot_of(me)], _peer(r), land.at[_slot_of(_peer(r))]) for r in range(1, N_DEV)]


def _alltoall_pattern(src, land):
    me = _place()
    return [(src.at[_slot_of(_peer(r))], land.at[_slot_of(me)], _peer(r), land.at[_slot_of(_peer(r))])
            for r in range(1, N_DEV)]


def _chipcast_pattern(src, land):
    x, y, c = _place()
    me = (x, y, c)
    peers = [(x, y, 1 - c), (1 - x, y, c), (x, 1 - y, c), (1 - x, 1 - y, c)]
    return [(src, land.at[_slot_of(me)], p, land.at[_slot_of(p)]) for p in peers]


def _forward_pattern(land, _):
    x, y, c = _place()
    chips = [(1 - x, y), (x, 1 - y), (1 - x, 1 - y)]
    return [(land.at[_slot_of((*ch, c))], land.at[_slot_of((*ch, c))], (x, y, 1 - c), land.at[_slot_of((*ch, 1 - c))])
            for ch in chips]


def _sibling_pattern(src, land):
    x, y, c = _place()
    return [(src.at[ch, 1 - c], land.at[ch], (x, y, 1 - c), land.at[ch]) for ch in range(4)]


def _chip_pattern(src, land):
    x, y, c = _place()
    chips = [(1 - x, y), (x, 1 - y), (1 - x, 1 - y)]
    return [(src.at[2 * ch[0] + ch[1]], land.at[j], (*ch, c), land.at[j]) for j, ch in enumerate(chips)]


def _split_start(name, groups, pattern, land_shape, after=None):
    sizes = [len(g) for g in groups]
    arrays = [s for g in groups for s in g]
    n = len(arrays)
    inplace = land_shape is None
    fresh = [] if inplace else [lax.empty(land_shape(s.shape), s.dtype) for s in arrays]
    bufs = arrays + fresh
    order = [] if after is None else [after]

    def body(*refs):
        srcs, lands = refs[:n], (refs[:n] if inplace else refs[n:2 * n])
        first_out = len(bufs) + len(order)
        sems = refs[first_out:first_out + 2 * len(groups)]
        token = refs[-1]
        k = 0
        for gi, size in enumerate(sizes):
            for i in range(size):
                copies = pattern(srcs[k], lands[k])
                for j, (s, d, peer, _) in enumerate(copies):
                    pltpu.make_async_remote_copy(
                        src_ref=s, dst_ref=d, send_sem=sems[2 * gi].at[i * len(copies) + j],
                        recv_sem=sems[2 * gi + 1].at[i * len(copies) + j], device_id=peer, device_id_type=MESH).start()
                k += 1
        token[...] = jnp.zeros_like(token)

    per_array = {_gather_pattern: N_PEER, _alltoall_pattern: N_PEER, _chipcast_pattern: 4, _forward_pattern: 3,
                 _sibling_pattern: 4, _chip_pattern: 3}[pattern]
    sem_shapes = [pltpu.SemaphoreType.DMA((size * per_array,)) for size in sizes for _ in range(2)]
    res = pl.pallas_call(
        body, name=name,
        in_specs=[HBM_SPEC] * len(bufs) + [pl.BlockSpec(memory_space=pl.ANY)] * len(order),
        out_specs=[SEM_SPEC] * len(sem_shapes) + [HBM_SPEC] * len(bufs) + [pl.BlockSpec(memory_space=pltpu.VMEM)],
        out_shape=sem_shapes + [pltpu.HBM(s.shape, s.dtype) for s in bufs] + [jax.ShapeDtypeStruct((8, DH), F32)],
        input_output_aliases={i: len(sem_shapes) + i for i in range(len(bufs))},
        compiler_params=pltpu.CompilerParams(has_side_effects=EFFECT),
    )(*[_hbm(s) for s in bufs], *order)
    nsem = len(sem_shapes)
    thru = list(res[nsem:nsem + len(bufs)])
    out, k = [], 0
    for gi, size in enumerate(sizes):
        srcs_out = [] if inplace else thru[k:k + size]
        lands_out = thru[k:k + size] if inplace else thru[n + k:n + k + size]
        out.append((res[2 * gi], res[2 * gi + 1], srcs_out, lands_out))
        k += size
    return out, res[-1]


def _split_wait(name, group, pattern, after):
    send_sems, recv_sems, arrays, lands = group
    n = len(lands)
    inplace = not arrays
    bufs = list(arrays) + list(lands)
    fill_own = pattern in (_gather_pattern, _chipcast_pattern, _alltoall_pattern)
    own_of = (lambda a: a.at[_slot_of(_place())]) if pattern is _alltoall_pattern else (lambda a: a)
    own_shape = (lambda a: a.shape[1:]) if pattern is _alltoall_pattern else (lambda a: a.shape)

    def body(*refs):
        srcs, lands_ = refs[:n], (refs[:n] if inplace else refs[n:2 * n])
        s_sems, r_sems = refs[len(bufs)], refs[len(bufs) + 1]
        own = []
        if fill_own:
            stage, local_sems = refs[-n - 1:-1], refs[-1]
            load = [pltpu.make_async_copy(own_of(srcs[i]), stage[i], local_sems.at[i]) for i in range(n)]
            for cp in load:
                cp.start()
            for cp in load:
                cp.wait()
            own = [pltpu.make_async_copy(stage[i], lands_[i].at[_slot_of(_place())], local_sems.at[i])
                   for i in range(n)]
            for cp in own:
                cp.start()
        for i in range(n):
            copies = pattern(srcs[i], lands_[i])
            for j, (s, _, peer, filled) in enumerate(copies):
                cp = pltpu.make_async_remote_copy(
                    src_ref=s, dst_ref=filled, send_sem=s_sems.at[i * len(copies) + j],
                    recv_sem=r_sems.at[i * len(copies) + j], device_id=peer, device_id_type=MESH)
                cp.wait_send()
                cp.wait_recv()
        for cp in own:
            cp.wait()

    res = pl.pallas_call(
        body, name=name,
        in_specs=[HBM_SPEC] * len(bufs) + [SEM_SPEC, SEM_SPEC, pl.BlockSpec(memory_space=pl.ANY)],
        out_specs=[HBM_SPEC] * len(bufs),
        out_shape=[pltpu.HBM(s.shape, s.dtype) for s in bufs],
        scratch_shapes=([pltpu.VMEM(own_shape(a), a.dtype) for a in arrays] + [pltpu.SemaphoreType.DMA((n,))])
        if fill_own else [],
        input_output_aliases={i: i for i in range(len(bufs))},
        compiler_params=pltpu.CompilerParams(has_side_effects=EFFECT),
    )(*bufs, send_sems, recv_sems, after)
    return ([], list(res)) if inplace else (list(res[:n]), list(res[n:]))


def _pair_sum(name, grads, recv, core):
    _, _, r, c = grads.shape
    tr = r if r * c <= (1 << 20) or r % 16 else _div_tile(r, 512, 16)

    def body(core_ref, g_ref, r_ref, o_ref):
        o_ref[...] = (g_ref[0].astype(F32) + r_ref[...].astype(F32)).astype(o_ref.dtype)

    return pl.pallas_call(
        body, name=name,
        grid_spec=pltpu.PrefetchScalarGridSpec(
            num_scalar_prefetch=1, grid=(4, r // tr),
            in_specs=[pl.BlockSpec((1, 1, tr, c), lambda s, i, cr: (s, cr[0], i, 0)),
                      pl.BlockSpec((1, tr, c), lambda s, i, cr: (s, i, 0))],
            out_specs=pl.BlockSpec((1, tr, c), lambda s, i, cr: (s, i, 0))),
        out_shape=jax.ShapeDtypeStruct(recv.shape, recv.dtype),
        compiler_params=_cparams(("parallel", "parallel")),
    )(core, grads, recv)


def _adamw_math(w, g, m, v):
    m = ADAM_B1 * m + (1.0 - ADAM_B1) * g
    v = ADAM_B2 * v + (1.0 - ADAM_B2) * (g * g)
    m_hat = m / (1.0 - ADAM_B1 ** ADAM_STEP)
    v_hat = v / (1.0 - ADAM_B2 ** ADAM_STEP)
    delta = -ADAM_LR * (m_hat / (jnp.sqrt(v_hat) + ADAM_EPS) + ADAM_WD * w)
    return delta, m, v


SMALL_LEAVES = (
    ("mix_pre_norm", 0, 1, 0, D, (1, D)), ("mix_post_norm", 1, 1, 0, D, (1, D)),
    ("ffn_pre_norm", 2, 1, 0, D, (1, D)), ("ffn_post_norm", 3, 1, 0, D, (1, D)),
    ("meta_tokens", 8, N_META, 0, DH, (N_META, DH)),
    ("conv_qkv", 24, KQ, 0, 3 * GW // N_DEV, (1, KQ, 3 * GW // N_DEV)),
    ("conv_sc", 32, KS, 0, SW // N_DEV, (1, KS, SW // N_DEV)),
    ("a_log", 40, 1, 0, HEADS, (1, HEADS)), ("dt_bias", 40, 1, DH, HEADS, (1, HEADS)),
    ("gdn_norm", 40, 1, 2 * DH, DH, (1, DH)),
)
SMALL_LOSS_AT = (40, 3 * DH)


def _adamw_small(slabs, late, w, m, v):
    nslot = slabs.shape[0]
    nleaf = len(SMALL_LEAVES)

    def body(p_ref, l_ref, w_ref, m_ref, v_ref, *refs):
        outs, loss_ref, scr = refs[:4 * nleaf], refs[4 * nleaf], refs[4 * nleaf + 1]
        g = p_ref[0]
        gain = l_ref[0, pl.ds(0, 1), :]
        meta = l_ref[0, pl.ds(8, N_META), pl.ds(0, DH)]
        for s in range(1, nslot):
            g = g + p_ref[s]
            gain = gain + l_ref[s, pl.ds(0, 1), :]
            meta = meta + l_ref[s, pl.ds(8, N_META), pl.ds(0, DH)]
        scr[0] = g
        scr[0, pl.ds(0, 1), pl.ds(0, D)] += gain
        scr[0, pl.ds(8, N_META), pl.ds(0, DH)] += meta
        g = scr[0]
        d, mn, vn = _adamw_math(w_ref[...], g, m_ref[...], v_ref[...])
        for kind, val in enumerate((g, d, mn, vn)):
            scr[kind] = val
        for kind in range(4):
            for li, (_, r0, nr, c0, nc, shape) in enumerate(SMALL_LEAVES):
                val = scr[kind, pl.ds(r0, nr), pl.ds(c0, nc)]
                o = outs[kind * nleaf + li]
                if len(shape) == 3:
                    o[0] = val
                else:
                    o[...] = val
        loss_ref[...] = scr[0, pl.ds(SMALL_LOSS_AT[0], 1), pl.ds(SMALL_LOSS_AT[1], 1)]

    whole = lambda a: pl.BlockSpec(a.shape, lambda: (0,) * a.ndim)
    out_shapes = [jax.ShapeDtypeStruct(shape, F32) for _ in range(4) for (*_, shape) in SMALL_LEAVES]
    out_shapes.append(jax.ShapeDtypeStruct((1, 1), F32))
    res = pl.pallas_call(
        body, name="adamw_small",
        in_specs=[whole(slabs), whole(late), whole(w), whole(m), whole(v)],
        out_specs=[pl.BlockSpec(s.shape, lambda n=len(s.shape): (0,) * n) for s in out_shapes],
        out_shape=out_shapes,
        scratch_shapes=[pltpu.VMEM((4,) + w.shape, F32)],
        compiler_params=pltpu.CompilerParams(vmem_limit_bytes=VMEM_LIMIT),
    )(slabs, late, w, m, v)
    kinds = [{nm: res[kind * nleaf + li] for li, (nm, *_) in enumerate(SMALL_LEAVES)} for kind in range(4)]
    return kinds, res[-1]


def _adamw_reduced(name, parts, chip, recv, w, m, v, after=None):
    _, r, c = parts.shape
    tr = r if r * c <= (1 << 20) or r % 16 else _div_tile(r, 512, 16)
    order = [] if after is None else [after]

    def body(chip_ref, p_ref, r_ref, w_ref, m_ref, v_ref, *rest):
        g_out, d_out, m_out, v_out = rest[len(order):]
        g = p_ref[0].astype(F32)
        for s in range(3):
            g = g + r_ref[s].astype(F32)
        d, mn, vn = _adamw_math(w_ref[...], g, m_ref[...], v_ref[...])
        g_out[...] = g
        d_out[...] = d
        m_out[...] = mn
        v_out[...] = vn

    blk = pl.BlockSpec((tr, c), lambda i, ch: (i, 0))
    return pl.pallas_call(
        body, name=name,
        grid_spec=pltpu.PrefetchScalarGridSpec(
            num_scalar_prefetch=1, grid=(r // tr,),
            in_specs=[pl.BlockSpec((1, tr, c), lambda i, ch: (ch[0], i, 0)),
                      pl.BlockSpec((3, tr, c), lambda i, ch: (0, i, 0)), blk, blk, blk]
            + [pl.BlockSpec((8, DH), lambda i, ch: (0, 0))] * len(order),
            out_specs=[blk] * 4),
        out_shape=[jax.ShapeDtypeStruct((r, c), F32)] * 4,
        compiler_params=_cparams(("parallel",)),
    )(chip, parts, recv, w, m, v, *order)


SMALL_ROWS, SMALL_COLS = 48, 3 * GW


def _pack_small(gains4, meta, conv_qkv, conv_sc, a_log, dt_bias, gdn_norm, extra=None):
    def tile(a):
        return jnp.pad(a, ((0, -a.shape[0] % 8), (0, SMALL_COLS - a.shape[1])))
    scal = jnp.concatenate([jnp.pad(a_log, ((0, 0), (0, DH - HEADS))), jnp.pad(dt_bias, ((0, 0), (0, DH - HEADS))),
                            gdn_norm] + ([] if extra is None else [extra]), axis=1)
    gains = sum(jnp.pad(g, ((i, 7 - i), (0, SMALL_COLS - g.shape[1]))) for i, g in enumerate(gains4))
    parts = [gains, tile(meta), tile(conv_qkv), tile(conv_sc), tile(scal)]
    slab = jnp.concatenate(parts, axis=0)
    assert slab.shape == (SMALL_ROWS, SMALL_COLS)
    return slab


def kernel(x, meta_tokens, mix_pre_norm, mix_post_norm, ffn_pre_norm, ffn_post_norm, w_in, conv_qkv, a_log, dt_bias, gdn_norm, conv_sc, w_out, w_gate, w_up, w_down, loss_target, m_meta_tokens, m_mix_pre_norm, m_mix_post_norm, m_ffn_pre_norm, m_ffn_post_norm, m_w_in, m_conv_qkv, m_a_log, m_dt_bias, m_gdn_norm, m_conv_sc, m_w_out, m_w_gate, m_w_up, m_w_down, v_meta_tokens, v_mix_pre_norm, v_mix_post_norm, v_ffn_pre_norm, v_ffn_post_norm, v_w_in, v_conv_qkv, v_a_log, v_dt_bias, v_gdn_norm, v_conv_sc, v_w_out, v_w_gate, v_w_up, v_w_down):
    bsz, seq, _ = x.shape
    lp = -(-(N_META + seq) // CH) * CH
    padf = lp - N_META - seq
    first_real = padf + N_META
    rows = bsz * lp
    nchunk = lp // CH
    tr = _div_tile(lp, 528)
    tm = _div_tile(rows, 1056)
    tm_big = _div_tile(rows, 2112)
    cx, cy, cc = _place()

    small_shard = jnp.concatenate([
        jnp.pad(meta_tokens, ((0, 0), (0, 256 - DH))),
        jnp.pad(conv_qkv[0], ((0, 0), (0, 256 - 3 * GW // N_DEV))),
        jnp.pad(conv_sc[0], ((0, 5), (0, 256 - SW // N_DEV)))], axis=0)
    g_in, g_small = _all_gather("gather_w_in", [w_in[0].T.astype(BF16), small_shard])
    my_rest = [w_out[0].astype(BF16), w_gate[0].T.astype(BF16), w_up[0].T.astype(BF16), w_down[0].astype(BF16)]
    (grp_rest,), token = _split_start("gather_rest_start", [my_rest], _chipcast_pattern, lambda s: (N_DEV,) + s,
                                      after=g_in)
    after_start = token[0:1, 0:1]

    def cols_full(g):
        return jnp.transpose(g, (1, 0, 2)).reshape(g.shape[1], -1)

    win_t = g_in.reshape(IN_W, D)
    win_t = jnp.concatenate([win_t[:BA0 + 2 * HEADS], jnp.zeros((BA_W - 2 * HEADS, D), BF16),
                             win_t[BA0 + 2 * HEADS:]], axis=0)
    meta_full = cols_full(g_small[:, 0:16, :DH])
    convq_full = cols_full(g_small[:, 16:20, :3 * GW // N_DEV])
    convs_full = cols_full(g_small[:, 20:23, :SW // N_DEV])
    alog_v = jnp.pad(a_log, ((0, 0), (HEADS, DH - 2 * HEADS)))
    dtb_v = jnp.pad(dt_bias, ((0, 0), (HEADS, DH - 2 * HEADS)))

    head_rows = jnp.concatenate([jnp.zeros((padf, D), F32), meta_full], axis=0)
    h0, u1 = _pre_norm(x, head_rows, mix_pre_norm + after_start, bsz, lp, tr)
    (proj,) = _mm("proj_in", [(u1, win_t)], "nt", tm_big, 768, D, out_dtypes=(BF16,))
    (ba,) = _mm("proj_gate_logits", [(u1, win_t[BA0:SX0])], "nt", tm, BA_W, D)
    qkv = _gdn_prep(proj, convq_full, bsz, lp, padf)
    gb, gcs = _gates(ba, alog_v, dtb_v, bsz, lp, padf)

    def lane_bcast(cols):
        t = jnp.transpose(cols.reshape(bsz, lp, HEADS), (0, 2, 1)).reshape(bsz * HEADS, lp, 1)
        return jnp.broadcast_to(t, (bsz * HEADS, lp, DH))

    bb = lane_bcast(gb[:, 0:HEADS])
    gcb = lane_bcast(gcs[:, HEADS:2 * HEADS])
    gct = jnp.transpose(gcs[:, HEADS:2 * HEADS].reshape(bsz, nchunk, CH, HEADS), (0, 3, 1, 2)).reshape(
        bsz * HEADS, nchunk, 1, CH)
    u_loc, l1, l2, l3, w_t, k_dec, tinv_all, g_last = _gdn_local_fwd(qkv, gcb, bb, gct, bsz, lp)
    o, v_new, states = _gdn_scan_fwd(u_loc, l1, l2, g_last, bsz, lp)
    o, v_new = o.reshape(rows, GW), v_new.reshape(rows, GW)
    _, l_rest = _split_wait("gather_rest_arrive", grp_rest, _chipcast_pattern, o)
    (grp_out, grp_ffn, grp_down), token = _split_start(
        "gather_rest_forward", [l_rest[0:1], l_rest[1:3], l_rest[3:4]], _forward_pattern, None)
    mix_g, mix_s = _mix_heads(proj, o, gdn_norm + token[0:1, :], convs_full, bsz, lp, padf)
    mixin = jnp.concatenate([mix_g, mix_s], axis=1)
    _, (l_out,) = _split_wait("gather_w_out_wait", grp_out, _forward_pattern, mixin)
    wout = l_out.reshape(D, D)
    (mix,) = _mm("proj_out", [(mixin, wout)], "nn", tm, D, D)
    h1, u2 = _mid_norms(h0, mix, mix_post_norm, ffn_pre_norm, bsz, lp, tr)
    _, (l_gate, l_up) = _split_wait("gather_ffn_wait", grp_ffn, _forward_pattern, u2)
    wgate_t, wup_t = l_gate.reshape(FF, D), l_up.reshape(FF, D)
    gate, up, act = _ffn_up(u2, wgate_t, wup_t, _div_tile(rows, 528), 1408)
    _, (l_down,) = _split_wait("gather_down_wait", grp_down, _forward_pattern, act)
    wdown = l_down.reshape(FF, D)
    (ffn,) = _mm("ffn_down", [(act, wdown)], "nn", tm, D, 1408)

    dy, dffn, d_g4, loss_part = _loss_head(h1, ffn, loss_target, ffn_post_norm, bsz, lp, tr, first_real)
    (dwdown,) = _mm("dw_down", [(act, dffn)], "tn", 1408, D, tm, out_dtypes=(BF16,))

    def swiglu_bwd(da, g, u):
        g, u = g.astype(F32), u.astype(F32)
        return da * u * _dsilu(g), da * _silu(g)

    dgate, dup = _mm("d_act", [(dffn, wdown)], "nt", tm, 1408, D, out_dtypes=(BF16, BF16),
                     epilogue=swiglu_bwd, extras=(gate, up))
    (dwgate_t,) = _mm("dw_gate", [(dgate, u2)], "tn", 1408, D, tm, out_dtypes=(BF16,))
    (dwup_t,) = _mm("dw_up", [(dup, u2)], "tn", 1408, D, tm, out_dtypes=(BF16,))
    (du2,) = _mm("d_u2", [(dgate, wgate_t), (dup, wup_t)], "nn", tm, D, 1408)
    dh1, dmix, d_g3, d_g2 = _mid_norms_bwd(h1, mix, du2, dy, ffn_pre_norm, mix_post_norm, bsz, lp, tr)
    (dwout,) = _mm("dw_out", [(mixin, dmix)], "tn", D, D, tm, out_dtypes=(BF16,))

    def row_blocks(g):
        return g.reshape(4, 2, -1, g.shape[1])

    core = jnp.reshape(cc, (1,)).astype(jnp.int32)
    chip = jnp.reshape(2 * cx + cy, (1,)).astype(jnp.int32)

    def reduce_start(tag, grads):
        (grp,), tok = _split_start("rs_sibling_start_" + tag, [grads], _sibling_pattern, lambda s: s[:1] + s[2:])
        return grp, tok

    def reduce_mid(tag, grp, nms, after):
        mine, lands = _split_wait("rs_sibling_wait_" + tag, grp, _sibling_pattern, after)
        parts = [_pair_sum("pair_sum_" + nm, g, r, core) for nm, g, r in zip(nms, mine, lands)]
        (grp2,), tok = _split_start("rs_chip_start_" + tag, [parts], _chip_pattern, lambda s: (3,) + s[1:])
        return grp2, tok

    def reduce_end(tag, grp2, after):
        return _split_wait("rs_chip_wait_" + tag, grp2, _chip_pattern, after)

    names_ffn = ["w_down", "w_gate", "w_up", "w_out"]
    rs_ffn, tok = reduce_start("ffn", [row_blocks(dwdown), row_blocks(dwgate_t), row_blocks(dwup_t), row_blocks(dwout)])
    (dmixin,) = _mm("d_mixin", [(dmix, wout)], "nt", tm, D, D, after=tok)
    d_o, dz, dsx, dsb, dsc, d_gn, d_convs = _mix_heads_bwd(proj, o, gdn_norm, convs_full, dmixin, bsz, lp, padf)
    rs_ffn, tok = reduce_mid("ffn", rs_ffn, names_ffn, d_o)
    dv_new, dstates = _gdn_scan_bwd(d_o, l3, w_t, k_dec, g_last + tok[0, 0], bsz, lp)
    dq, dk, dv, dg_b, dbeta_b = _gdn_local_bwd(qkv, gcb, bb, gct, tinv_all, states, dstates, v_new,
                                               dv_new.reshape(rows, GW), d_o, bsz, lp)
    dpqkv, d_convq = _gdn_prep_bwd(proj, convq_full, dq, dk, dv, bsz, lp, padf)
    dba, d_scal = _gates_bwd(ba, alog_v, dtb_v, dbeta_b, dg_b, gb, bsz, lp, padf)
    dproj = jnp.concatenate([dpqkv, dz, dba, dsx, dsb, dsc], axis=1)

    d_alog = d_scal[0:1, HEADS:2 * HEADS]
    d_dtb = d_scal[1:2, HEADS:2 * HEADS]
    cq_w, cs_w = 3 * GW // N_DEV, SW // N_DEV
    slab = jnp.stack([
        _pack_small([jnp.zeros_like(d_g2), d_g2, d_g3, d_g4], jnp.zeros((N_META, DH), F32),
                    d_convq[:, p * cq_w:(p + 1) * cq_w], d_convs[:, p * cs_w:(p + 1) * cs_w], d_alog, d_dtb, d_gn,
                    loss_part) for p in range(N_DEV)])
    (grp_small,), tok = _split_start("small_grads_start", [[slab]], _alltoall_pattern, lambda s: s)
    (dwin_t,) = _mm("dw_in", [(dproj, u1)], "tn", 1280, D, tm, out_dtypes=(BF16,), after=tok)
    parts_ffn, recv_ffn = reduce_end("ffn", rs_ffn, dwin_t)
    dwin_t = jnp.concatenate([dwin_t[:BA0 + 2 * HEADS], dwin_t[SX0:]], axis=0)
    rs_in, tok = reduce_start("in", [row_blocks(dwin_t)])

    big_w = {"w_in": (w_in, m_w_in, v_w_in), "w_out": (w_out, m_w_out, v_w_out), "w_gate": (w_gate, m_w_gate, v_w_gate),
             "w_up": (w_up, m_w_up, v_w_up), "w_down": (w_down, m_w_down, v_w_down)}

    def update(nm, parts, recv, after=None):
        w, m, v = big_w[nm]
        if nm in ("w_out", "w_down"):
            return _adamw_reduced("adamw_" + nm, parts, chip, recv, w[0], m[0], v[0], after)
        res = _adamw_reduced("adamw_" + nm, parts, chip, recv, w[0].T, m[0].T, v[0].T, after)
        return [t.T for t in res]

    big_out = {nm: update(nm, p, r, tok) for nm, p, r in zip(names_ffn, parts_ffn, recv_ffn)}
    ffn_done = sum(big_out[nm][1][:1, :1] for nm in names_ffn)
    rs_in, tok = reduce_mid("in", rs_in, ["w_in"], ffn_done)
    (du1,) = _mm("d_u1", [(dproj, win_t)], "nn", tm_big, D, 768, after=tok)
    grad_x, d_g1, d_meta = _pre_norm_bwd(h0, du1, dh1, mix_pre_norm, bsz, lp, seq, padf)

    late = jnp.stack([
        jnp.concatenate([jnp.pad(d_g1, ((0, 7), (0, 0))), jnp.pad(d_meta[:, p * DH:(p + 1) * DH], ((0, 0), (0, D - DH)))],
                        axis=0) for p in range(N_DEV)])
    (grp_late,), _ = _split_start("late_grads_start", [[late]], _alltoall_pattern, lambda s: s)
    _, (slabs_local,) = _split_wait("small_grads_wait", grp_small, _alltoall_pattern, grad_x)
    _, (late_local,) = _split_wait("late_grads_wait", grp_late, _alltoall_pattern, slabs_local)
    w_small = _pack_small([mix_pre_norm, mix_post_norm, ffn_pre_norm, ffn_post_norm], meta_tokens, conv_qkv[0],
                          conv_sc[0], a_log, dt_bias, gdn_norm)
    m_small = _pack_small([m_mix_pre_norm, m_mix_post_norm, m_ffn_pre_norm, m_ffn_post_norm], m_meta_tokens,
                          m_conv_qkv[0], m_conv_sc[0], m_a_log, m_dt_bias, m_gdn_norm)
    v_small = _pack_small([v_mix_pre_norm, v_mix_post_norm, v_ffn_pre_norm, v_ffn_post_norm], v_meta_tokens,
                          v_conv_qkv[0], v_conv_sc[0], v_a_log, v_dt_bias, v_gdn_norm)
    small_out, loss = _adamw_small(slabs_local, late_local, w_small, m_small, v_small)
    parts_in, recv_in = reduce_end("in", rs_in, small_out[1]["gdn_norm"])
    big_out["w_in"] = update("w_in", parts_in[0], recv_in[0])
    names = ["w_in", "w_out", "w_gate", "w_up", "w_down"]

    order = ["meta_tokens", "mix_pre_norm", "mix_post_norm", "ffn_pre_norm", "ffn_post_norm", "w_in", "conv_qkv",
             "a_log", "dt_bias", "gdn_norm", "conv_sc", "w_out", "w_gate", "w_up", "w_down"]
    outs = [loss[0, 0], grad_x]
    for kind in range(4):
        leaves = dict(small_out[kind])
        for nm in names:
            leaves[nm] = big_out[nm][kind][None]
        outs += [leaves[nm] for nm in order]
    return tuple(outs)
```

```python
import jax
import jax.numpy as jnp
from jax import lax
from jax.experimental import pallas as pl
from jax.experimental.pallas import tpu as pltpu

F32 = jnp.float32
BF16 = jnp.bfloat16
MESH = pl.DeviceIdType.MESH

D = 1024
N_META = 16
HEADS = 4
DH = 128
GW = HEADS * DH
SW = D - GW
FF = 2816
CH = 64
KQ = 4
KS = 3
EPS = 1e-6
IN_W = 3 * GW + GW + 2 * HEADS + 3 * SW
Q0, Z0, BA0, SX0, SB0, SC0 = 0, 1536, 2048, 2304, 2816, 3328
BA_W = SX0 - BA0
N_DEV = 8
VMEM_LIMIT = 56 * 1024 * 1024

ADAM_LR, ADAM_B1, ADAM_B2, ADAM_EPS, ADAM_WD, ADAM_STEP = 0.001, 0.9, 0.999, 1e-08, 0.01, 10


def _cparams(sem):
    return pltpu.CompilerParams(dimension_semantics=sem, vmem_limit_bytes=VMEM_LIMIT)


def _div_tile(n, target, mult=16):
    best = None
    for t in range(mult, min(n, target) + 1, mult):
        if n % t == 0:
            best = t
    assert best is not None, (n, target)
    return best


def _silu(x):
    return x * jax.nn.sigmoid(x)


def _dsilu(x):
    s = jax.nn.sigmoid(x)
    return s * (1.0 + x * (1.0 - s))


def _bdot(a, b, dims):
    return lax.dot_general(a.astype(BF16), b.astype(BF16), (dims, ((), ())), preferred_element_type=F32)


NN = ((1,), (0,))
NT = ((1,), (1,))
TN = ((0,), (0,))


def _mm(name, pairs, mode, tm, tn, tk, out_dtypes=(F32,), epilogue=None, extras=(), after=None):
    a0, b0 = pairs[0]
    if mode == "nn":
        (m, k), n = a0.shape, b0.shape[1]
    elif mode == "nt":
        (m, k), n = a0.shape, b0.shape[0]
    else:
        (k, m), n = a0.shape, b0.shape[1]
    assert m % tm == 0 and n % tn == 0 and k % tk == 0, (name, m, n, k, tm, tn, tk)
    nk = k // tk
    npair = len(pairs)
    nex = len(extras)
    nout = len(out_dtypes)
    ntok = 0 if after is None else 1
    dims = {"nn": NN, "nt": NT, "tn": TN}[mode]

    def body(*refs):
        ab = refs[:2 * npair]
        ex = refs[2 * npair:2 * npair + nex]
        outs = refs[2 * npair + nex + ntok:2 * npair + nex + ntok + nout]
        acc_ref = refs[-1]
        kk = pl.program_id(2)

        part = None
        for p in range(npair):
            d = lax.dot_general(ab[2 * p][...], ab[2 * p + 1][...], (dims, ((), ())), preferred_element_type=F32)
            part = d if part is None else part + d

        def finish(acc):
            res = epilogue(acc, *[e[...] for e in ex]) if epilogue is not None else (acc,)
            for o, r in zip(outs, res):
                o[...] = r.astype(o.dtype)

        if nk == 1:
            finish(part)
        else:
            @pl.when(kk == 0)
            def _():
                acc_ref[...] = part

            @pl.when(kk > 0)
            def _():
                acc_ref[...] += part

            @pl.when(kk == nk - 1)
            def _():
                finish(acc_ref[...])

    if mode == "nn":
        a_spec = pl.BlockSpec((tm, tk), lambda i, j, q: (i, q))
        b_spec = pl.BlockSpec((tk, tn), lambda i, j, q: (q, j))
    elif mode == "nt":
        a_spec = pl.BlockSpec((tm, tk), lambda i, j, q: (i, q))
        b_spec = pl.BlockSpec((tn, tk), lambda i, j, q: (j, q))
    else:
        a_spec = pl.BlockSpec((tk, tm), lambda i, j, q: (q, i))
        b_spec = pl.BlockSpec((tk, tn), lambda i, j, q: (q, j))
    o_spec = pl.BlockSpec((tm, tn), lambda i, j, q: (i, j))
    flat = [t for pr in pairs for t in pr]
    res = pl.pallas_call(
        body, name=name,
        grid=(m // tm, n // tn, nk),
        in_specs=[a_spec, b_spec] * npair + [o_spec] * nex + [pl.BlockSpec((8, DH), lambda i, j, q: (0, 0))] * ntok,
        out_specs=[o_spec] * nout,
        out_shape=[jax.ShapeDtypeStruct((m, n), dt) for dt in out_dtypes],
        scratch_shapes=[pltpu.VMEM((tm, tn) if nk > 1 else (8, DH), F32)],
        compiler_params=_cparams(("parallel", "parallel", "arbitrary")),
    )(*flat, *extras, *([] if after is None else [after]))
    return res


def _ffn_up(u2, wg, wu, tm, tn):
    m, k = u2.shape
    n = wg.shape[0]

    def body(a_ref, g_ref, u_ref, gate_ref, up_ref, act_ref):
        a = a_ref[...]
        g = lax.dot_general(a, g_ref[...], (NT, ((), ())), preferred_element_type=F32)
        u = lax.dot_general(a, u_ref[...], (NT, ((), ())), preferred_element_type=F32)
        gate_ref[...] = g.astype(BF16)
        up_ref[...] = u.astype(BF16)
        act_ref[...] = (_silu(g) * u).astype(BF16)

    o_spec = pl.BlockSpec((tm, tn), lambda j, i: (i, j))
    w_spec = pl.BlockSpec((tn, k), lambda j, i: (j, 0))
    return pl.pallas_call(
        body, name="ffn_up", grid=(n // tn, m // tm),
        in_specs=[pl.BlockSpec((tm, k), lambda j, i: (i, 0)), w_spec, w_spec],
        out_specs=[o_spec, o_spec, o_spec],
        out_shape=[jax.ShapeDtypeStruct((m, n), BF16)] * 3,
        compiler_params=_cparams(("parallel", "parallel")),
    )(u2, wg, wu)


def _rms(x, gain):
    r = lax.rsqrt(jnp.mean(x * x, axis=-1, keepdims=True) + EPS)
    return x * r * gain, r


def _rms_bwd(x, gain, dy):
    r = lax.rsqrt(jnp.mean(x * x, axis=-1, keepdims=True) + EPS)
    dyw = dy * gain
    dx = r * dyw - x * (r * r * r) * jnp.mean(dyw * x, axis=-1, keepdims=True)
    return dx, dy * x * r


def _row_specs(tr, width, nb):
    return pl.BlockSpec((tr, width), lambda b, j: (b * nb + j, 0))


def _vec_spec(width):
    return pl.BlockSpec((1, width), lambda b, j: (0, 0))


def _first_step(b, j):
    return jnp.logical_and(b == 0, j == 0)


def _pre_norm(x, head_rows, gain, bsz, lp, tr):
    nb = lp // tr
    first_real = head_rows.shape[0]
    assert first_real % 8 == 0 and first_real < tr
    x_spec = pl.BlockSpec((pl.Squeezed(), pl.Element(tr), pl.Element(D)),
                          lambda b, j: (b, pl.multiple_of(jnp.maximum(j * tr - first_real, 0), 8), 0))

    def body(x_ref, hd_ref, g_ref, h_ref, u_ref):
        j = pl.program_id(1)
        xt = x_ref[...]
        h = jnp.where(j == 0, jnp.concatenate([hd_ref[...], xt[:tr - first_real]], axis=0), xt)
        h_ref[...] = h
        u_ref[...] = _rms(h, g_ref[...])[0].astype(BF16)

    rs = _row_specs(tr, D, nb)
    return pl.pallas_call(
        body, name="mix_pre_norm", grid=(bsz, nb),
        in_specs=[x_spec, pl.BlockSpec((first_real, D), lambda b, j: (0, 0)), _vec_spec(D)],
        out_specs=[rs, rs],
        out_shape=[jax.ShapeDtypeStruct((bsz * lp, D), F32), jax.ShapeDtypeStruct((bsz * lp, D), BF16)],
        compiler_params=_cparams(("parallel", "parallel")),
    )(x, head_rows, gain)


def _mid_norms(h0, mix, g_post, g_pre, bsz, lp, tr):
    nb = lp // tr

    def body(h_ref, m_ref, gp_ref, gq_ref, h1_ref, u2_ref):
        h1 = h_ref[...] + _rms(m_ref[...], gp_ref[...])[0]
        h1_ref[...] = h1
        u2_ref[...] = _rms(h1, gq_ref[...])[0].astype(BF16)

    rs = _row_specs(tr, D, nb)
    return pl.pallas_call(
        body, name="mid_norms", grid=(bsz, nb),
        in_specs=[rs, rs, _vec_spec(D), _vec_spec(D)],
        out_specs=[rs, rs],
        out_shape=[jax.ShapeDtypeStruct(h0.shape, F32), jax.ShapeDtypeStruct(h0.shape, BF16)],
        compiler_params=_cparams(("parallel", "parallel")),
    )(h0, mix, g_post, g_pre)


def _loss_head(h1, ffn, tgt, g_post, bsz, lp, tr, first_real):
    nb = lp // tr
    assert first_real % 8 == 0 and first_real < tr
    tgt_spec = pl.BlockSpec((pl.Squeezed(), pl.Element(tr), pl.Element(D)),
                            lambda b, j: (b, pl.multiple_of(jnp.maximum(j * tr - first_real, 0), 8), 0))

    def body(h_ref, f_ref, t_ref, g_ref, dy_ref, dffn_ref, dg_ref, loss_ref):
        b, j = pl.program_id(0), pl.program_id(1)
        f = f_ref[...]
        g = g_ref[...]
        y = h_ref[...] + _rms(f, g)[0]
        t = j * tr + lax.broadcasted_iota(jnp.int32, (tr, 1), 0)
        tg = t_ref[...]
        tg = jnp.where(j == 0, jnp.concatenate([tg[tr - first_real:], tg[:tr - first_real]], axis=0), tg)
        err = jnp.where(t >= first_real, y - tg, 0.0)
        dy = err * (1.0 / D)
        dy_ref[...] = dy
        dx, dgc = _rms_bwd(f, g, dy)
        dffn_ref[...] = dx.astype(BF16)
        lsum = jnp.sum(jnp.sum(err * err, axis=0, keepdims=True), axis=1, keepdims=True) * (0.5 / D)

        @pl.when(_first_step(b, j))
        def _():
            dg_ref[...] = jnp.zeros_like(dg_ref)
            loss_ref[...] = jnp.zeros_like(loss_ref)

        dg_ref[...] += jnp.sum(dgc, axis=0, keepdims=True)
        loss_ref[...] += jnp.broadcast_to(lsum, loss_ref.shape)

    rs = _row_specs(tr, D, nb)
    return pl.pallas_call(
        body, name="loss_head", grid=(bsz, nb),
        in_specs=[rs, rs, tgt_spec, _vec_spec(D)],
        out_specs=[rs, rs, _vec_spec(D), _vec_spec(DH)],
        out_shape=[jax.ShapeDtypeStruct(h1.shape, F32), jax.ShapeDtypeStruct(h1.shape, BF16),
                   jax.ShapeDtypeStruct((1, D), F32), jax.ShapeDtypeStruct((1, DH), F32)],
        compiler_params=_cparams(("arbitrary", "arbitrary")),
    )(h1, ffn, tgt, g_post)


def _mid_norms_bwd(h1, mix, du2, dy, g_pre, g_post, bsz, lp, tr):
    nb = lp // tr

    def body(h_ref, m_ref, du_ref, dy_ref, gq_ref, gp_ref, dh1_ref, dmix_ref, dgq_ref, dgp_ref):
        b, j = pl.program_id(0), pl.program_id(1)
        dx, dgq = _rms_bwd(h_ref[...], gq_ref[...], du_ref[...].astype(F32))
        dh1 = dy_ref[...] + dx
        dh1_ref[...] = dh1
        dm, dgp = _rms_bwd(m_ref[...], gp_ref[...], dh1)
        dmix_ref[...] = dm.astype(BF16)

        @pl.when(_first_step(b, j))
        def _():
            dgq_ref[...] = jnp.zeros_like(dgq_ref)
            dgp_ref[...] = jnp.zeros_like(dgp_ref)

        dgq_ref[...] += jnp.sum(dgq, axis=0, keepdims=True)
        dgp_ref[...] += jnp.sum(dgp, axis=0, keepdims=True)

    rs = _row_specs(tr, D, nb)
    return pl.pallas_call(
        body, name="mid_norms_bwd", grid=(bsz, nb),
        in_specs=[rs, rs, rs, rs, _vec_spec(D), _vec_spec(D)],
        out_specs=[rs, rs, _vec_spec(D), _vec_spec(D)],
        out_shape=[jax.ShapeDtypeStruct(h1.shape, F32), jax.ShapeDtypeStruct(h1.shape, BF16),
                   jax.ShapeDtypeStruct((1, D), F32), jax.ShapeDtypeStruct((1, D), F32)],
        compiler_params=_cparams(("arbitrary", "arbitrary")),
    )(h1, mix, du2, dy, g_pre, g_post)


def _pre_norm_bwd(h0, du1, dh1, gain, bsz, lp, seq, padf):
    first_real = lp - seq
    tx = _div_tile(seq, 512)
    assert first_real % 8 == 0 and lp % first_real == 0

    def tokens(h_ref, du_ref, dh1_ref, g_ref, gx_ref, dg_ref):
        b, j = pl.program_id(0), pl.program_id(1)
        dx, dgc = _rms_bwd(h_ref[...], g_ref[...], du_ref[...].astype(F32))
        gx_ref[...] = dh1_ref[...] + dx

        @pl.when(_first_step(b, j))
        def _():
            dg_ref[...] = jnp.zeros_like(dg_ref)

        dg_ref[...] += jnp.sum(dgc, axis=0, keepdims=True)

    window = pl.BlockSpec((pl.Element(tx), pl.Element(D)),
                          lambda b, j: (pl.multiple_of(b * lp + first_real + j * tx, 16), 0))
    assert first_real % 16 == 0 and tx % 16 == 0 and lp % 16 == 0
    grad_x, dg_tokens = pl.pallas_call(
        tokens, name="pre_norm_bwd", grid=(bsz, seq // tx),
        in_specs=[window, window, window, _vec_spec(D)],
        out_specs=[pl.BlockSpec((pl.Squeezed(), tx, D), lambda b, j: (b, j, 0)), _vec_spec(D)],
        out_shape=[jax.ShapeDtypeStruct((bsz, seq, D), F32), jax.ShapeDtypeStruct((1, D), F32)],
        compiler_params=_cparams(("arbitrary", "arbitrary")),
    )(h0, du1, dh1, gain)

    def head(h_ref, du_ref, dh1_ref, g_ref, dgt_ref, dg_ref, dmeta_ref):
        b = pl.program_id(0)
        dx, dgc = _rms_bwd(h_ref[...], g_ref[...], du_ref[...].astype(F32))
        dh0 = dh1_ref[...] + dx

        @pl.when(b == 0)
        def _():
            dg_ref[...] = dgt_ref[...]
            dmeta_ref[...] = jnp.zeros_like(dmeta_ref)

        dg_ref[...] += jnp.sum(dgc, axis=0, keepdims=True)
        dmeta_ref[...] += dh0[padf:padf + N_META, :]

    rows = pl.BlockSpec((first_real, D), lambda b: (b * (lp // first_real), 0))
    vec = pl.BlockSpec((1, D), lambda b: (0, 0))
    dg, dmeta = pl.pallas_call(
        head, name="pre_norm_bwd_head", grid=(bsz,),
        in_specs=[rows, rows, rows, vec, vec],
        out_specs=[vec, pl.BlockSpec((N_META, D), lambda b: (0, 0))],
        out_shape=[jax.ShapeDtypeStruct((1, D), F32), jax.ShapeDtypeStruct((N_META, D), F32)],
        compiler_params=_cparams(("arbitrary",)),
    )(h0, du1, dh1, gain, dg_tokens)
    return grad_x, dg, dmeta


def _shift_down(x, s, lp):
    return x if s == 0 else pltpu.roll(x, s, axis=0)


def _shift_up(x, s, lp):
    return x if s == 0 else pltpu.roll(x, lp - s, axis=0)


def _conv_fwd(x, w, taps, lp):
    y = None
    for i in range(taps):
        term = _shift_down(x, taps - 1 - i, lp) * w[i:i + 1, :]
        y = term if y is None else y + term
    return y


def _conv_bwd(x, w, dy, taps, lp):
    dx = None
    dws = []
    for i in range(taps):
        s = taps - 1 - i
        term = _shift_up(dy, s, lp) * w[i:i + 1, :]
        dx = term if dx is None else dx + term
        dws.append(jnp.sum(dy * _shift_down(x, s, lp), axis=0, keepdims=True))
    return dx, jnp.concatenate(dws, axis=0)


def _slab(lp, col0):
    return pl.BlockSpec((lp, DH), lambda b, s: (b, col0 // DH + s))


def _qkv_act(c, s):
    y = _silu(c)
    r = lax.rsqrt(jnp.sum(y * y, axis=-1, keepdims=True) + EPS)
    scale = jnp.where(s < HEADS, DH ** -0.5, 1.0)
    return y, r, scale


def _gdn_prep(proj, conv_qkv, bsz, lp, padf):
    def body(p_ref, w_ref, o_ref):
        s = pl.program_id(1)
        c = _conv_fwd(p_ref[...].astype(F32), w_ref[...], KQ, lp)
        y, r, scale = _qkv_act(c, s)
        out = jnp.where(s < 2 * HEADS, y * (r * scale), y)
        t = lax.broadcasted_iota(jnp.int32, (lp, 1), 0)
        o_ref[...] = jnp.where(t >= padf, out, 0.0)

    return pl.pallas_call(
        body, name="gdn_prep", grid=(bsz, 3 * HEADS),
        in_specs=[_slab(lp, Q0), pl.BlockSpec((KQ, DH), lambda b, s: (0, s))],
        out_specs=_slab(lp, 0),
        out_shape=jax.ShapeDtypeStruct((bsz * lp, 3 * GW), F32),
        compiler_params=_cparams(("parallel", "parallel")),
    )(proj, conv_qkv)


def _gdn_prep_bwd(proj, conv_qkv, dq, dk, dv, bsz, lp, padf):
    def body(p_ref, w_ref, dq_ref, dk_ref, dv_ref, dp_ref, dw_ref):
        b, s = pl.program_id(0), pl.program_id(1)
        x = p_ref[...].astype(F32)
        w = w_ref[...]
        c = _conv_fwd(x, w, KQ, lp)
        y, r, scale = _qkv_act(c, s)
        d = jnp.where(s < HEADS, dq_ref[...], jnp.where(s < 2 * HEADS, dk_ref[...], dv_ref[...]))
        dn = d * scale
        n = y * r
        dy_norm = r * (dn - n * jnp.sum(dn * n, axis=-1, keepdims=True))
        dyy = jnp.where(s < 2 * HEADS, dy_norm, dn)
        t = lax.broadcasted_iota(jnp.int32, (lp, 1), 0)
        dc = jnp.where(t >= padf, dyy * _dsilu(c), 0.0)
        dx, dw = _conv_bwd(x, w, dc, KQ, lp)
        dp_ref[...] = jnp.where(t >= padf, dx, 0.0).astype(BF16)

        @pl.when(jnp.logical_and(b == 0, s == 0))
        def _():
            dw_ref[...] = jnp.zeros_like(dw_ref)

        col_block = jnp.right_shift(lax.broadcasted_iota(jnp.int32, (KQ, 3 * GW), 1), DH.bit_length() - 1)
        dw_ref[...] += jnp.where(col_block == s, jnp.tile(dw, (1, 3 * HEADS)), 0.0)

    sl = lambda col0: pl.BlockSpec((lp, DH), lambda b, s: (b, col0 // DH + s))
    part = lambda k: pl.BlockSpec((lp, DH), lambda b, s: (b, jnp.clip(s - k * HEADS, 0, HEADS - 1)))
    return pl.pallas_call(
        body, name="gdn_prep_bwd", grid=(bsz, 3 * HEADS),
        in_specs=[sl(Q0), pl.BlockSpec((KQ, DH), lambda b, s: (0, s)), part(0), part(1), part(2)],
        out_specs=[sl(0), pl.BlockSpec((KQ, 3 * GW), lambda b, s: (0, 0))],
        out_shape=[jax.ShapeDtypeStruct((bsz * lp, 3 * GW), BF16), jax.ShapeDtypeStruct((KQ, 3 * GW), F32)],
        compiler_params=_cparams(("arbitrary", "arbitrary")),
    )(proj, conv_qkv, dq, dk, dv)


def _softplus(x):
    e = jnp.exp(-jnp.abs(x))
    one_e = 1.0 + e
    l1p = jnp.where(one_e == 1.0, e, jnp.log(one_e) * (e / (one_e - 1.0)))
    return jnp.maximum(x, 0.0) + l1p


def _tri(n, kind):
    i = lax.broadcasted_iota(jnp.int32, (n, n), 0)
    j = lax.broadcasted_iota(jnp.int32, (n, n), 1)
    return {"incl": i >= j, "strict": i > j, "upper": i <= j}[kind]


def _gates(proj, alog_v, dtb_v, bsz, lp, padf):
    nchunk = lp // CH

    def body(p_ref, al_ref, dt_ref, gb_ref, cs_ref):
        x = p_ref[...].astype(F32)
        lane = lax.broadcasted_iota(jnp.int32, (lp, DH), 1)
        t = lax.broadcasted_iota(jnp.int32, (lp, DH), 0)
        beta = jax.nn.sigmoid(x)
        g = -jnp.exp(al_ref[...]) * _softplus(x + dt_ref[...])
        gb = jnp.where(lane < HEADS, beta, jnp.where(lane < 2 * HEADS, g, 0.0))
        gb_ref[...] = jnp.where(t >= padf, gb, 0.0)
        ltri = _tri(CH, "incl").astype(BF16)
        rest = [gb_ref[pl.ds(n * CH, CH), :] for n in range(nchunk)]
        sums = None
        for _ in range(3):
            piece = [v.astype(BF16) for v in rest]
            rest = [v - p.astype(F32) for v, p in zip(rest, piece)]
            part = [jnp.dot(ltri, p, preferred_element_type=F32) for p in piece]
            sums = part if sums is None else [a + b for a, b in zip(sums, part)]
        for n in range(nchunk):
            cs_ref[pl.ds(n * CH, CH), :] = sums[n]

    blk = pl.BlockSpec((lp, DH), lambda b: (b, 0))
    vec = pl.BlockSpec((1, DH), lambda b: (0, 0))
    out = pl.BlockSpec((lp, DH), lambda b: (b, 0))
    return pl.pallas_call(
        body, name="gates", grid=(bsz,),
        in_specs=[blk, vec, vec], out_specs=[out, out],
        out_shape=[jax.ShapeDtypeStruct((bsz * lp, DH), F32)] * 2,
        compiler_params=_cparams(("parallel",)),
    )(proj, alog_v, dtb_v)


def _gates_bwd(proj, alog_v, dtb_v, dbeta_b, dg_b, gb, bsz, lp, padf):
    def body(p_ref, al_ref, dt_ref, db_ref, dg_ref, gb_ref, dl_ref, dv_ref):
        b = pl.program_id(0)
        x = p_ref[...].astype(F32)
        lane = lax.broadcasted_iota(jnp.int32, (lp, DH), 1)
        t = lax.broadcasted_iota(jnp.int32, (lp, DH), 0)
        dbeta = jnp.zeros((lp, DH), F32)
        dg = jnp.zeros((lp, DH), F32)
        for h in range(HEADS):
            dbeta = jnp.where(lane == h, db_ref[h], dbeta)
            dg = jnp.where(lane == HEADS + h, dg_ref[h], dg)
        real = t >= padf
        dbeta = jnp.where(real, dbeta, 0.0)
        dg = jnp.where(real, dg, 0.0)
        sb = jax.nn.sigmoid(x)
        neg_ea = -jnp.exp(al_ref[...])
        dsp = dg * neg_ea * jax.nn.sigmoid(x + dt_ref[...])
        dl = dbeta * sb * (1.0 - sb) + dsp
        dl_ref[...] = jnp.concatenate([dl, jnp.zeros((lp, BA_W - DH), F32)], axis=1).astype(BF16)
        dalog = jnp.sum(dg * gb_ref[...], axis=0, keepdims=True)
        ddt = jnp.sum(dsp, axis=0, keepdims=True)

        @pl.when(b == 0)
        def _():
            dv_ref[...] = jnp.zeros_like(dv_ref)

        dv_ref[...] += jnp.concatenate([dalog, ddt], axis=0)

    blk = pl.BlockSpec((lp, DH), lambda b: (b, 0))
    vec = pl.BlockSpec((1, DH), lambda b: (0, 0))
    hb = pl.BlockSpec((HEADS, lp, DH), lambda b: (b, 0, 0))
    return pl.pallas_call(
        body, name="gates_bwd", grid=(bsz,),
        in_specs=[blk, vec, vec, hb, hb, pl.BlockSpec((lp, DH), lambda b: (b, 0))],
        out_specs=[pl.BlockSpec((lp, BA_W), lambda b: (b, 0)), pl.BlockSpec((2, DH), lambda b: (0, 0))],
        out_shape=[jax.ShapeDtypeStruct((bsz * lp, BA_W), BF16), jax.ShapeDtypeStruct((2, DH), F32)],
        compiler_params=_cparams(("arbitrary",)),
    )(proj, alog_v, dtb_v, dbeta_b, dg_b, gb)


def _mix_heads(proj, o, gdn_norm, conv_sc, bsz, lp, padf):
    def body(o_ref, z_ref, gn_ref, sx_ref, sb_ref, sc_ref, w_ref, og_ref, os_ref):
        og_ref[...] = (_rms(o_ref[...], gn_ref[...])[0] * _silu(z_ref[...].astype(F32))).astype(BF16)
        conv = _conv_fwd(sc_ref[...].astype(F32) * sx_ref[...].astype(F32), w_ref[...], KS, lp)
        t = lax.broadcasted_iota(jnp.int32, (lp, 1), 0)
        os_ref[...] = jnp.where(t >= padf, sb_ref[...].astype(F32) * conv, 0.0).astype(BF16)

    half = _slab(lp, 0)
    return pl.pallas_call(
        body, name="mix_heads", grid=(bsz, HEADS),
        in_specs=[half, _slab(lp, Z0), pl.BlockSpec((1, DH), lambda b, s: (0, 0)),
                  _slab(lp, SX0), _slab(lp, SB0), _slab(lp, SC0), pl.BlockSpec((KS, DH), lambda b, s: (0, s))],
        out_specs=[half, half],
        out_shape=[jax.ShapeDtypeStruct((bsz * lp, GW), BF16)] * 2,
        compiler_params=_cparams(("parallel", "parallel")),
    )(o, proj, gdn_norm, proj, proj, proj, conv_sc)


def _mix_heads_bwd(proj, o, gdn_norm, conv_sc, dmixin, bsz, lp, padf):
    def body(o_ref, z_ref, gn_ref, sx_ref, sb_ref, sc_ref, w_ref, dg_ref, ds_ref,
             do_ref, dz_ref, dsx_ref, dsb_ref, dsc_ref, dgn_ref, dw_ref):
        s, b = pl.program_id(0), pl.program_id(1)
        t = lax.broadcasted_iota(jnp.int32, (lp, 1), 0)
        real = t >= padf
        o, z, gn, d = o_ref[...], z_ref[...].astype(F32), gn_ref[...], dg_ref[...].astype(F32)
        sz = _silu(z)
        on = _rms(o, gn)[0]
        dz_ref[...] = (d * on * _dsilu(z)).astype(BF16)
        dox, dgn = _rms_bwd(o, gn, d * sz)
        do_ref[...] = dox
        sx, sb, sc = sx_ref[...].astype(F32), sb_ref[...].astype(F32), sc_ref[...].astype(F32)
        w, e = w_ref[...], ds_ref[...].astype(F32)
        e = jnp.where(real, e, 0.0)
        xin = sc * sx
        conv = _conv_fwd(xin, w, KS, lp)
        dsb_ref[...] = (e * conv).astype(BF16)
        dxin, dw = _conv_bwd(xin, w, e * sb, KS, lp)
        dxin = jnp.where(real, dxin, 0.0)
        dsx_ref[...] = (dxin * sc).astype(BF16)
        dsc_ref[...] = (dxin * sx).astype(BF16)

        @pl.when(jnp.logical_and(s == 0, b == 0))
        def _():
            dgn_ref[...] = jnp.zeros_like(dgn_ref)

        @pl.when(b == 0)
        def _():
            dw_ref[...] = jnp.zeros_like(dw_ref)

        dgn_ref[...] += jnp.sum(dgn, axis=0, keepdims=True)
        dw_ref[...] += dw

    sl = lambda col0: pl.BlockSpec((lp, DH), lambda s, b: (b, col0 // DH + s))
    half = sl(0)
    return pl.pallas_call(
        body, name="mix_heads_bwd", grid=(HEADS, bsz),
        in_specs=[half, sl(Z0), pl.BlockSpec((1, DH), lambda s, b: (0, 0)), sl(SX0), sl(SB0), sl(SC0),
                  pl.BlockSpec((KS, DH), lambda s, b: (0, s)), sl(0), sl(GW)],
        out_specs=[half] * 5 + [pl.BlockSpec((1, DH), lambda s, b: (0, 0)), pl.BlockSpec((KS, DH), lambda s, b: (0, s))],
        out_shape=[jax.ShapeDtypeStruct((bsz * lp, GW), F32)] + [jax.ShapeDtypeStruct((bsz * lp, GW), BF16)] * 4
        + [jax.ShapeDtypeStruct((1, DH), F32), jax.ShapeDtypeStruct((KS, SW), F32)],
        compiler_params=_cparams(("arbitrary", "arbitrary")),
    )(o, proj, gdn_norm, proj, proj, proj, conv_sc, dmixin, dmixin)


def _split_dot(a, b, dims=NN):
    ah = a.astype(BF16)
    al = (a - ah.astype(F32)).astype(BF16)
    bh = b.astype(BF16)
    bl = (b - bh.astype(F32)).astype(BF16)

    def d(x, y):
        return lax.dot_general(x, y, (dims, ((), ())), preferred_element_type=F32)

    return d(ah, bh) + (d(ah, bl) + d(al, bh))


def _eye(n):
    return (lax.broadcasted_iota(jnp.int32, (n, n), 0) == lax.broadcasted_iota(jnp.int32, (n, n), 1)).astype(F32)


def _each(f, *lists):
    return [f(*xs) for xs in zip(*lists)]


def _inv_unit_lower(a_list):
    eye = _eye(CH)
    p = _each(lambda a: -a, a_list)
    t = _each(lambda x: eye + x, p)
    p = _each(lambda x: _split_dot(x, x), p)
    for level in range(5):
        if level < 4:
            prod = _each(lambda tt, pp: _split_dot(jnp.concatenate([tt, pp], axis=0), pp), t, p)
            t = _each(lambda tt, pr: tt + pr[:CH], t, prod)
            p = _each(lambda pr: pr[CH:], prod)
        else:
            t = _each(lambda tt, pp: tt + _split_dot(tt, pp), t, p)
    return t


def _chunk_gates(gc_ref, be_ref, gt_ref, g):
    rows = pl.ds(g * CH, CH)
    gc = gc_ref[0, rows, :]
    be = be_ref[0, rows, :]
    gcl = gc_ref[0, pl.ds(g * CH + CH - 1, 1), :]
    gt = gt_ref[0, g]
    incl = _tri(CH, "incl")
    dec = jnp.where(incl, jnp.exp(jnp.where(incl, gc[:, :CH] - gt, 0.0)), 0.0)
    return rows, be, dec, jnp.exp(gc), jnp.exp(gcl - gc), jnp.exp(gcl)


def _chunk_group(nchunk):
    return 3 if nchunk % 3 == 0 else 1


def _local_group(nchunk):
    return 11 if nchunk % 11 == 0 else _chunk_group(nchunk)


def _gdn_local_specs(lp, grp):
    ngrp = lp // (grp * CH)
    head = lambda col0: pl.BlockSpec((grp * CH, DH), lambda i, j: ((i // HEADS) * ngrp + j, col0 // DH + i % HEADS))
    lane_b = pl.BlockSpec((1, grp * CH, DH), lambda i, j: (i, j, 0))
    per_chunk = lambda r, c: pl.BlockSpec((1, grp, r, c), lambda i, j: (i, j, 0, 0))
    return ngrp, head, lane_b, per_chunk


def _gdn_local_fwd(qkv, gcb, bb, gct, bsz, lp):
    nchunk = lp // CH
    grp = _local_group(nchunk)
    ngrp, head, lane_b, per_chunk = _gdn_local_specs(lp, grp)
    bh = bsz * HEADS

    def body(q_ref, k_ref, v_ref, gc_ref, be_ref, gt_ref,
             u_ref, l1_ref, l2_ref, l3_ref, wt_ref, kd_ref, ti_ref, gl_ref):
        incl, strict = _tri(CH, "incl"), _tri(CH, "strict")
        rows, be, dec, eg, ekd, gl = zip(*[_chunk_gates(gc_ref, be_ref, gt_ref, g) for g in range(grp)])
        q = [q_ref[r, :] for r in rows]
        k = [k_ref[r, :] for r in rows]
        v = [v_ref[r, :] for r in rows]
        kb = _each(lambda x, y: x * y, k, be)
        a = _each(lambda x, y, d: jnp.where(strict, _bdot(x, y, NT) * d, 0.0), kb, k, dec)
        tinv = _inv_unit_lower(a)
        u = _each(lambda t, x, y: _bdot(t, x * y, NN), tinv, v, be)
        w = _each(lambda t, x, y: _bdot(t, x * y, NN), tinv, kb, eg)
        qk = _each(lambda x, y, d: jnp.where(incl, _bdot(x, y, NT) * d, 0.0), q, k, dec)
        qk_t = _each(lambda x: x.T, qk)
        qd = _each(lambda x, y: x * y, q, eg)
        kd = _each(lambda x, y: x * y, k, ekd)
        kd_t = _each(lambda x: x.T, kd)
        qd_t = _each(lambda x: x.T, qd)
        w_t = _each(lambda x: x.T, w)
        for g in range(grp):
            u_ref[rows[g], :] = u[g]
            l1_ref[0, g] = jnp.concatenate([w[g], qd[g]], axis=0).astype(BF16)
            l2_ref[0, g] = jnp.concatenate([qk[g], kd_t[g]], axis=0).astype(BF16)
            l3_ref[0, g] = jnp.concatenate([qk_t[g], qd_t[g]], axis=0).astype(BF16)
            wt_ref[0, g] = w_t[g].astype(BF16)
            kd_ref[0, g] = kd[g].astype(BF16)
            ti_ref[0, g] = tinv[g]
            gl_ref[0, g] = gl[g]

    r = bsz * lp
    shapes = [((r, GW), F32), ((bh, nchunk, 2 * CH, DH), BF16), ((bh, nchunk, 3 * CH, CH), BF16),
              ((bh, nchunk, 3 * CH, CH), BF16), ((bh, nchunk, DH, CH), BF16), ((bh, nchunk, CH, DH), BF16),
              ((bh, nchunk, CH, CH), F32), ((bh, nchunk, 1, DH), F32)]
    return pl.pallas_call(
        body, name="gdn_local_fwd", grid=(bh, ngrp),
        in_specs=[head(0), head(GW), head(2 * GW), lane_b, lane_b, per_chunk(1, CH)],
        out_specs=[head(0), per_chunk(2 * CH, DH), per_chunk(3 * CH, CH), per_chunk(3 * CH, CH), per_chunk(DH, CH),
                   per_chunk(CH, DH), per_chunk(CH, CH), per_chunk(1, DH)],
        out_shape=[jax.ShapeDtypeStruct(s, dt) for s, dt in shapes],
        compiler_params=_cparams(("parallel", "parallel")),
    )(qkv, qkv, qkv, gcb, bb, gct)


def _scan_specs(bsz, lp, grp, order):
    ngrp = lp // (grp * CH)
    bh = bsz * HEADS
    seq = lambda: pl.BlockSpec((bsz, grp * CH, GW), lambda s: (0, order(s, ngrp), 0))
    per_chunk = lambda r, c: pl.BlockSpec((bh, grp, r, c), lambda s: (0, order(s, ngrp), 0, 0))
    return ngrp, bh, seq, per_chunk


def _gdn_scan_fwd(u, l1, l2, gl, bsz, lp):
    nchunk = lp // CH
    grp = _chunk_group(nchunk)
    ngrp, bh, seq, per_chunk = _scan_specs(bsz, lp, grp, lambda s, n: s)

    def body(u_ref, l1_ref, l2_ref, gl_ref, o_ref, vn_ref, st_ref, s_scr):
        @pl.when(pl.program_id(0) == 0)
        def _():
            s_scr[...] = jnp.zeros_like(s_scr)

        chains = list(range(bh))
        where = [(i // HEADS, slice((i % HEADS) * DH, (i % HEADS + 1) * DH)) for i in chains]

        def dot(a, b):
            return jnp.dot(a, b.astype(BF16), preferred_element_type=F32)

        for c in range(grp):
            rows = slice(c * CH, (c + 1) * CH)
            s = [s_scr[i] for i in chains]
            for i in chains:
                st_ref[i, c] = s[i]
            x = _each(lambda i, si: dot(l1_ref[i, c], si), chains, s)
            vnew = _each(lambda bc, xi: u_ref[bc[0], rows, bc[1]] - xi[:CH], where, x)
            y = _each(lambda i, vi: dot(l2_ref[i, c], vi), chains, vnew)
            for i, (b, cols) in zip(chains, where):
                o_ref[b, rows, cols] = x[i][CH:] + y[i][:CH]
                vn_ref[b, rows, cols] = vnew[i].astype(BF16)
                s_scr[i] = s[i] * gl_ref[i, c] + y[i][CH:]

    return pl.pallas_call(
        body, name="gdn_scan_fwd", grid=(ngrp,),
        in_specs=[seq(), per_chunk(2 * CH, DH), per_chunk(3 * CH, CH), per_chunk(1, DH)],
        out_specs=[seq(), seq(), per_chunk(DH, DH)],
        out_shape=[jax.ShapeDtypeStruct((bsz, lp, GW), F32), jax.ShapeDtypeStruct((bsz, lp, GW), BF16),
                   jax.ShapeDtypeStruct((bh, nchunk, DH, DH), F32)],
        scratch_shapes=[pltpu.VMEM((bh, DH, DH), F32)],
        compiler_params=_cparams(("arbitrary",)),
    )(u.reshape(bsz, lp, GW), l1, l2, gl)


def _gdn_scan_bwd(d_o, l3, wt, kd, gl, bsz, lp):
    nchunk = lp // CH
    grp = _chunk_group(nchunk)
    ngrp, bh, seq, per_chunk = _scan_specs(bsz, lp, grp, lambda s, n: n - 1 - s)

    def body(do_ref, l3_ref, wt_ref, kd_ref, gl_ref, dvn_ref, dst_ref, ds_scr):
        @pl.when(pl.program_id(0) == 0)
        def _():
            ds_scr[...] = jnp.zeros_like(ds_scr)

        chains = list(range(bh))
        where = [(i // HEADS, slice((i % HEADS) * DH, (i % HEADS + 1) * DH)) for i in chains]

        def dot(a, b):
            return jnp.dot(a, b.astype(BF16), preferred_element_type=F32)

        for c in reversed(range(grp)):
            rows = slice(c * CH, (c + 1) * CH)
            ds = [ds_scr[i] for i in chains]
            for i in chains:
                dst_ref[i, c] = ds[i]
            p = _each(lambda i, bc: dot(l3_ref[i, c], do_ref[bc[0], rows, bc[1]]), chains, where)
            dvn = _each(lambda i, pi, di: pi[:CH] + dot(kd_ref[i, c], di), chains, p, ds)
            wd = _each(lambda i, vi: dot(wt_ref[i, c], vi), chains, dvn)
            for i, (b, cols) in zip(chains, where):
                dvn_ref[b, rows, cols] = dvn[i].astype(BF16)
                ds_scr[i] = ds[i] * gl_ref[i, c] + p[i][CH:] - wd[i]

    return pl.pallas_call(
        body, name="gdn_scan_bwd", grid=(ngrp,),
        in_specs=[seq(), per_chunk(3 * CH, CH), per_chunk(DH, CH), per_chunk(CH, DH), per_chunk(1, DH)],
        out_specs=[seq(), per_chunk(DH, DH)],
        out_shape=[jax.ShapeDtypeStruct((bsz, lp, GW), BF16), jax.ShapeDtypeStruct((bh, nchunk, DH, DH), F32)],
        scratch_shapes=[pltpu.VMEM((bh, DH, DH), F32)],
        compiler_params=_cparams(("arbitrary",)),
    )(d_o.reshape(bsz, lp, GW), l3, wt, kd, gl)


def _gdn_local_bwd(qkv, gcb, bb, gct, tinv_all, states, dstates, vnew, dvnew, d_o, bsz, lp):
    nchunk = lp // CH
    grp = _local_group(nchunk)
    ngrp, head, lane_b, per_chunk = _gdn_local_specs(lp, grp)
    bh = bsz * HEADS

    def body(q_ref, k_ref, v_ref, gc_ref, be_ref, gt_ref, ti_ref, st_ref, dst_ref, vn_ref, dvn_ref, do_ref,
             dq_ref, dk_ref, dv_ref, dg_ref, db_ref):
        incl, strict = _tri(CH, "incl"), _tri(CH, "strict")
        upper = _tri(CH, "upper").astype(F32)
        ones = jnp.ones((CH, DH), F32)
        last = lax.broadcasted_iota(jnp.int32, (CH, 1), 0) == CH - 1

        def rsum(x):
            return jnp.sum(x, axis=-1, keepdims=True)

        def mul(xs, ys):
            return _each(lambda x, y: x * y, xs, ys)

        rows, be, dec, eg, ekd, gl = zip(*[_chunk_gates(gc_ref, be_ref, gt_ref, g) for g in range(grp)])
        q = [q_ref[r, :] for r in rows]
        k = [k_ref[r, :] for r in rows]
        v = [v_ref[r, :] for r in rows]
        vnew = [vn_ref[r, :] for r in rows]
        dvn = [dvn_ref[r, :] for r in rows]
        do = [do_ref[r, :] for r in rows]
        tinv = [ti_ref[0, g] for g in range(grp)]
        s = [st_ref[0, g] for g in range(grp)]
        ds = [dst_ref[0, g] for g in range(grp)]
        kb, vb, qd, kd = mul(k, be), mul(v, be), mul(q, eg), mul(k, ekd)
        kbg = mul(kb, eg)
        a = _each(lambda x, y, d: jnp.where(strict, _bdot(x, y, NT) * d, 0.0), kb, k, dec)
        qk = _each(lambda x, y, d: jnp.where(incl, _bdot(x, y, NT) * d, 0.0), q, k, dec)
        dqk = _each(lambda x, y: jnp.where(incl, _bdot(x, y, NT), 0.0), do, vnew)
        dqd = _each(lambda x, y: _bdot(x, y, NT), do, s)
        dkd = _each(lambda x, y: _bdot(x, y, NT), vnew, ds)
        dgl = _each(lambda x, y: jnp.sum(rsum(x * y), axis=0, keepdims=True), s, ds)
        dw = _each(lambda x, y: -_bdot(x, y, NT), dvn, s)
        dvb = _each(lambda t, x: _bdot(t, x, TN), tinv, dvn)
        dkbg = _each(lambda t, x: _bdot(t, x, TN), tinv, dw)
        dt = _each(lambda x, y, z, w: _bdot(x, y, NT) + _bdot(z, w, NT), dvn, vb, dw, kbg)
        tdt = _each(lambda t, x: _split_dot(t, x, TN), tinv, dt)
        da = _each(lambda x, t: jnp.where(strict, -_split_dot(x, t, NT), 0.0), tdt, tinv)
        dp, dr = mul(da, dec), mul(dqk, dec)
        dkb = _each(lambda x, y, z, e: _bdot(x, y, NN) + z * e, dp, k, dkbg, eg)
        dk = _each(lambda p_, kb_, r_, q_, dkd_, ekd_, dkb_, be_:
                   _bdot(p_, kb_, TN) + _bdot(r_, q_, TN) + dkd_ * ekd_ + dkb_ * be_,
                   dp, kb, dr, q, dkd, ekd, dkb, be)
        dq = _each(lambda r_, k_, dqd_, eg_: _bdot(r_, k_, NN) + dqd_ * eg_, dr, k, dqd, eg)
        m = _each(lambda da_, a_, dqk_, qk_: da_ * a_ + dqk_ * qk_, da, a, dqk, qk)
        m_cols = _each(lambda x: _split_dot(x, ones, TN), m)
        kd_term = _each(lambda x, y: rsum(x * y), dkd, kd)
        dgc = _each(lambda m_, mc, dqd_, qd_, kt, dkbg_, kbg_, dgl_, gl_:
                    rsum(m_) - mc + rsum(dqd_ * qd_) - kt + rsum(dkbg_ * kbg_)
                    + jnp.where(last, jnp.sum(kt, axis=0, keepdims=True) + dgl_ * gl_, 0.0),
                    m, m_cols, dqd, qd, kd_term, dkbg, kbg, dgl, gl)
        dg = _each(lambda x: _split_dot(upper, x, NN), dgc)
        for g in range(grp):
            dq_ref[rows[g], :] = dq[g]
            dk_ref[rows[g], :] = dk[g]
            dv_ref[rows[g], :] = dvb[g] * be[g]
            dg_ref[0, rows[g], :] = dg[g]
            db_ref[0, rows[g], :] = jnp.broadcast_to(rsum(dkb[g] * k[g]) + rsum(dvb[g] * v[g]), (CH, DH))

    r = bsz * lp
    return pl.pallas_call(
        body, name="gdn_local_bwd", grid=(bh, ngrp),
        in_specs=[head(0), head(GW), head(2 * GW), lane_b, lane_b, per_chunk(1, CH), per_chunk(CH, CH),
                  per_chunk(DH, DH), per_chunk(DH, DH), head(0), head(0), head(0)],
        out_specs=[head(0), head(0), head(0), lane_b, lane_b],
        out_shape=[jax.ShapeDtypeStruct((r, GW), F32)] * 3 + [jax.ShapeDtypeStruct((bh, lp, DH), F32)] * 2,
        compiler_params=_cparams(("parallel", "parallel")),
    )(qkv, qkv, qkv, gcb, bb, gct, tinv_all, states, dstates, vnew, dvnew, d_o)


def _place():
    return lax.axis_index("x"), lax.axis_index("y"), lax.axis_index("c")


def _any_specs(n):
    return [pl.BlockSpec(memory_space=pl.ANY)] * n


def _all_gather(name, shards):
    n = len(shards)

    def body(*refs):
        ins, outs = refs[:n], refs[n:2 * n]
        send_sems, recv_sems, local_sems = refs[2 * n:2 * n + 3]
        stage = refs[2 * n + 3:]
        x, y, c = _place()
        me, sibling = (x, y, c), (x, y, 1 - c)
        chips = [(1 - x, y), (x, 1 - y), (1 - x, 1 - y)]

        def slot(i, p):
            return outs[i].at[4 * p[0] + 2 * p[1] + p[2]]

        def copy(i, k, block, to, src=None):
            return pltpu.make_async_remote_copy(
                src_ref=slot(i, block) if src is None else src, dst_ref=slot(i, block),
                send_sem=send_sems.at[i, k], recv_sem=recv_sems.at[i, k], device_id=to, device_id_type=MESH)

        load = [pltpu.make_async_copy(ins[i], stage[i], local_sems.at[i]) for i in range(n)]
        for cp in load:
            cp.start()
        first = []
        for i in range(n):
            first.append(copy(i, 0, me, sibling, src=ins[i]))
            first += [copy(i, 1 + j, me, (*chip, c), src=ins[i]) for j, chip in enumerate(chips)]
        for cp in first:
            cp.start()
        for cp in load:
            cp.wait()
        mine = [pltpu.make_async_copy(stage[i], slot(i, me), local_sems.at[i]) for i in range(n)]
        for cp in mine:
            cp.start()
        passed = []
        for j, chip in enumerate(chips):
            for i in range(n):
                copy(i, 1 + j, (*chip, c), me).wait_recv()
                fwd = copy(i, 4 + j, (*chip, c), sibling)
                fwd.start()
                passed.append(fwd)
        for i in range(n):
            copy(i, 0, sibling, me).wait_recv()
            for j, chip in enumerate(chips):
                copy(i, 4 + j, (*chip, 1 - c), me).wait_recv()
        for cp in first + passed:
            cp.wait_send()
        for cp in mine:
            cp.wait()

    return pl.pallas_call(
        body, name=name,
        in_specs=_any_specs(n), out_specs=_any_specs(n),
        out_shape=[jax.ShapeDtypeStruct((N_DEV,) + s.shape, s.dtype) for s in shards],
        scratch_shapes=[pltpu.SemaphoreType.DMA((n, 7)), pltpu.SemaphoreType.DMA((n, 7)),
                        pltpu.SemaphoreType.DMA((n,))] + [pltpu.VMEM(s.shape, s.dtype) for s in shards],
    )(*shards)


HBM_SPEC = pl.BlockSpec(memory_space=pltpu.HBM)
SEM_SPEC = pl.BlockSpec(memory_space=pltpu.SEMAPHORE)
EFFECT = pltpu.SideEffectType.DATAFLOW_SIDE_EFFECTING
N_PEER = N_DEV - 1


def _peer(r):
    x, y, c = _place()
    return ((1 - x) if r & 4 else x, (1 - y) if r & 2 else y, (1 - c) if r & 1 else c)


def _slot_of(p):
    return 4 * p[0] + 2 * p[1] + p[2]


def _hbm(a):
    return pltpu.with_memory_space_constraint(a, pltpu.HBM)


def _gather_pattern(src, land):
    me = _place()
    return [(src, land.at[_slot_of(me)], _peer(r), land.at[_slot_of(_peer(r))]) for r in range(1, N_DEV)]


def _alltoall_pattern(src, land):
    me = _place()
    return [(src.at[_slot_of(_peer(r))], land.at[_slot_of(me)], _peer(r), land.at[_slot_of(_peer(r))])
            for r in range(1, N_DEV)]


def _chipcast_pattern(src, land):
    x, y, c = _place()
    me = (x, y, c)
    peers = [(x, y, 1 - c), (1 - x, y, c), (x, 1 - y, c), (1 - x, 1 - y, c)]
    return [(src, land.at[_slot_of(me)], p, land.at[_slot_of(p)]) for p in peers]


def _forward_pattern(land, _):
    x, y, c = _place()
    chips = [(1 - x, y), (x, 1 - y), (1 - x, 1 - y)]
    return [(land.at[_slot_of((*ch, c))], land.at[_slot_of((*ch, c))], (x, y, 1 - c), land.at[_slot_of((*ch, 1 - c))])
            for ch in chips]


def _sibling_pattern(src, land):
    x, y, c = _place()
    return [(src.at[ch, 1 - c], land.at[ch], (x, y, 1 - c), land.at[ch]) for ch in range(4)]


def _chip_pattern(src, land):
    x, y, c = _place()
    chips = [(1 - x, y), (x, 1 - y), (1 - x, 1 - y)]
    return [(src.at[2 * ch[0] + ch[1]], land.at[j], (*ch, c), land.at[j]) for j, ch in enumerate(chips)]


def _split_start(name, groups, pattern, land_shape, after=None):
    sizes = [len(g) for g in groups]
    arrays = [s for g in groups for s in g]
    n = len(arrays)
    inplace = land_shape is None
    fresh = [] if inplace else [lax.empty(land_shape(s.shape), s.dtype) for s in arrays]
    bufs = arrays + fresh
    order = [] if after is None else [after]

    def body(*refs):
        srcs, lands = refs[:n], (refs[:n] if inplace else refs[n:2 * n])
        first_out = len(bufs) + len(order)
        sems = refs[first_out:first_out + 2 * len(groups)]
        token = refs[-1]
        k = 0
        for gi, size in enumerate(sizes):
            for i in range(size):
                copies = pattern(srcs[k], lands[k])
                for j, (s, d, peer, _) in enumerate(copies):
                    pltpu.make_async_remote_copy(
                        src_ref=s, dst_ref=d, send_sem=sems[2 * gi].at[i * len(copies) + j],
                        recv_sem=sems[2 * gi + 1].at[i * len(copies) + j], device_id=peer, device_id_type=MESH).start()
                k += 1
        token[...] = jnp.zeros_like(token)

    per_array = {_gather_pattern: N_PEER, _alltoall_pattern: N_PEER, _chipcast_pattern: 4, _forward_pattern: 3,
                 _sibling_pattern: 4, _chip_pattern: 3}[pattern]
    sem_shapes = [pltpu.SemaphoreType.DMA((size * per_array,)) for size in sizes for _ in range(2)]
    res = pl.pallas_call(
        body, name=name,
        in_specs=[HBM_SPEC] * len(bufs) + [pl.BlockSpec(memory_space=pl.ANY)] * len(order),
        out_specs=[SEM_SPEC] * len(sem_shapes) + [HBM_SPEC] * len(bufs) + [pl.BlockSpec(memory_space=pltpu.VMEM)],
        out_shape=sem_shapes + [pltpu.HBM(s.shape, s.dtype) for s in bufs] + [jax.ShapeDtypeStruct((8, DH), F32)],
        input_output_aliases={i: len(sem_shapes) + i for i in range(len(bufs))},
        compiler_params=pltpu.CompilerParams(has_side_effects=EFFECT),
    )(*[_hbm(s) for s in bufs], *order)
    nsem = len(sem_shapes)
    thru = list(res[nsem:nsem + len(bufs)])
    out, k = [], 0
    for gi, size in enumerate(sizes):
        srcs_out = [] if inplace else thru[k:k + size]
        lands_out = thru[k:k + size] if inplace else thru[n + k:n + k + size]
        out.append((res[2 * gi], res[2 * gi + 1], srcs_out, lands_out))
        k += size
    return out, res[-1]


def _split_wait(name, group, pattern, after):
    send_sems, recv_sems, arrays, lands = group
    n = len(lands)
    inplace = not arrays
    bufs = list(arrays) + list(lands)
    fill_own = pattern in (_gather_pattern, _chipcast_pattern, _alltoall_pattern)
    own_of = (lambda a: a.at[_slot_of(_place())]) if pattern is _alltoall_pattern else (lambda a: a)
    own_shape = (lambda a: a.shape[1:]) if pattern is _alltoall_pattern else (lambda a: a.shape)

    def body(*refs):
        srcs, lands_ = refs[:n], (refs[:n] if inplace else refs[n:2 * n])
        s_sems, r_sems = refs[len(bufs)], refs[len(bufs) + 1]
        own = []
        if fill_own:
            stage, local_sems = refs[-n - 1:-1], refs[-1]
            load = [pltpu.make_async_copy(own_of(srcs[i]), stage[i], local_sems.at[i]) for i in range(n)]
            for cp in load:
                cp.start()
            for cp in load:
                cp.wait()
            own = [pltpu.make_async_copy(stage[i], lands_[i].at[_slot_of(_place())], local_sems.at[i])
                   for i in range(n)]
            for cp in own:
                cp.start()
        for i in range(n):
            copies = pattern(srcs[i], lands_[i])
            for j, (s, _, peer, filled) in enumerate(copies):
                cp = pltpu.make_async_remote_copy(
                    src_ref=s, dst_ref=filled, send_sem=s_sems.at[i * len(copies) + j],
                    recv_sem=r_sems.at[i * len(copies) + j], device_id=peer, device_id_type=MESH)
                cp.wait_send()
                cp.wait_recv()
        for cp in own:
            cp.wait()

    res = pl.pallas_call(
        body, name=name,
        in_specs=[HBM_SPEC] * len(bufs) + [SEM_SPEC, SEM_SPEC, pl.BlockSpec(memory_space=pl.ANY)],
        out_specs=[HBM_SPEC] * len(bufs),
        out_shape=[pltpu.HBM(s.shape, s.dtype) for s in bufs],
        scratch_shapes=([pltpu.VMEM(own_shape(a), a.dtype) for a in arrays] + [pltpu.SemaphoreType.DMA((n,))])
        if fill_own else [],
        input_output_aliases={i: i for i in range(len(bufs))},
        compiler_params=pltpu.CompilerParams(has_side_effects=EFFECT),
    )(*bufs, send_sems, recv_sems, after)
    return ([], list(res)) if inplace else (list(res[:n]), list(res[n:]))


def _pair_sum(name, grads, recv, core):
    _, _, r, c = grads.shape
    tr = r if r * c <= (1 << 20) or r % 16 else _div_tile(r, 512, 16)

    def body(core_ref, g_ref, r_ref, o_ref):
        o_ref[...] = (g_ref[0].astype(F32) + r_ref[...].astype(F32)).astype(o_ref.dtype)

    return pl.pallas_call(
        body, name=name,
        grid_spec=pltpu.PrefetchScalarGridSpec(
            num_scalar_prefetch=1, grid=(4, r // tr),
            in_specs=[pl.BlockSpec((1, 1, tr, c), lambda s, i, cr: (s, cr[0], i, 0)),
                      pl.BlockSpec((1, tr, c), lambda s, i, cr: (s, i, 0))],
            out_specs=pl.BlockSpec((1, tr, c), lambda s, i, cr: (s, i, 0))),
        out_shape=jax.ShapeDtypeStruct(recv.shape, recv.dtype),
        compiler_params=_cparams(("parallel", "parallel")),
    )(core, grads, recv)


def _adamw_math(w, g, m, v):
    m = ADAM_B1 * m + (1.0 - ADAM_B1) * g
    v = ADAM_B2 * v + (1.0 - ADAM_B2) * (g * g)
    m_hat = m / (1.0 - ADAM_B1 ** ADAM_STEP)
    v_hat = v / (1.0 - ADAM_B2 ** ADAM_STEP)
    delta = -ADAM_LR * (m_hat / (jnp.sqrt(v_hat) + ADAM_EPS) + ADAM_WD * w)
    return delta, m, v


SMALL_LEAVES = (
    ("mix_pre_norm", 0, 1, 0, D, (1, D)), ("mix_post_norm", 1, 1, 0, D, (1, D)),
    ("ffn_pre_norm", 2, 1, 0, D, (1, D)), ("ffn_post_norm", 3, 1, 0, D, (1, D)),
    ("meta_tokens", 8, N_META, 0, DH, (N_META, DH)),
    ("conv_qkv", 24, KQ, 0, 3 * GW // N_DEV, (1, KQ, 3 * GW // N_DEV)),
    ("conv_sc", 32, KS, 0, SW // N_DEV, (1, KS, SW // N_DEV)),
    ("a_log", 40, 1, 0, HEADS, (1, HEADS)), ("dt_bias", 40, 1, DH, HEADS, (1, HEADS)),
    ("gdn_norm", 40, 1, 2 * DH, DH, (1, DH)),
)
SMALL_LOSS_AT = (40, 3 * DH)


def _adamw_small(slabs, late, w, m, v):
    nslot = slabs.shape[0]
    nleaf = len(SMALL_LEAVES)

    def body(p_ref, l_ref, w_ref, m_ref, v_ref, *refs):
        outs, loss_ref, scr = refs[:4 * nleaf], refs[4 * nleaf], refs[4 * nleaf + 1]
        g = p_ref[0]
        gain = l_ref[0, pl.ds(0, 1), :]
        meta = l_ref[0, pl.ds(8, N_META), pl.ds(0, DH)]
        for s in range(1, nslot):
            g = g + p_ref[s]
            gain = gain + l_ref[s, pl.ds(0, 1), :]
            meta = meta + l_ref[s, pl.ds(8, N_META), pl.ds(0, DH)]
        scr[0] = g
        scr[0, pl.ds(0, 1), pl.ds(0, D)] += gain
        scr[0, pl.ds(8, N_META), pl.ds(0, DH)] += meta
        g = scr[0]
        d, mn, vn = _adamw_math(w_ref[...], g, m_ref[...], v_ref[...])
        for kind, val in enumerate((g, d, mn, vn)):
            scr[kind] = val
        for kind in range(4):
            for li, (_, r0, nr, c0, nc, shape) in enumerate(SMALL_LEAVES):
                val = scr[kind, pl.ds(r0, nr), pl.ds(c0, nc)]
                o = outs[kind * nleaf + li]
                if len(shape) == 3:
                    o[0] = val
                else:
                    o[...] = val
        loss_ref[...] = scr[0, pl.ds(SMALL_LOSS_AT[0], 1), pl.ds(SMALL_LOSS_AT[1], 1)]

    whole = lambda a: pl.BlockSpec(a.shape, lambda: (0,) * a.ndim)
    out_shapes = [jax.ShapeDtypeStruct(shape, F32) for _ in range(4) for (*_, shape) in SMALL_LEAVES]
    out_shapes.append(jax.ShapeDtypeStruct((1, 1), F32))
    res = pl.pallas_call(
        body, name="adamw_small",
        in_specs=[whole(slabs), whole(late), whole(w), whole(m), whole(v)],
        out_specs=[pl.BlockSpec(s.shape, lambda n=len(s.shape): (0,) * n) for s in out_shapes],
        out_shape=out_shapes,
        scratch_shapes=[pltpu.VMEM((4,) + w.shape, F32)],
        compiler_params=pltpu.CompilerParams(vmem_limit_bytes=VMEM_LIMIT),
    )(slabs, late, w, m, v)
    kinds = [{nm: res[kind * nleaf + li] for li, (nm, *_) in enumerate(SMALL_LEAVES)} for kind in range(4)]
    return kinds, res[-1]


def _adamw_reduced(name, parts, chip, recv, w, m, v, after=None):
    _, r, c = parts.shape
    tr = r if r * c <= (1 << 20) or r % 16 else _div_tile(r, 512, 16)
    order = [] if after is None else [after]

    def body(chip_ref, p_ref, r_ref, w_ref, m_ref, v_ref, *rest):
        g_out, d_out, m_out, v_out = rest[len(order):]
        g = p_ref[0].astype(F32)
        for s in range(3):
            g = g + r_ref[s].astype(F32)
        d, mn, vn = _adamw_math(w_ref[...], g, m_ref[...], v_ref[...])
        g_out[...] = g
        d_out[...] = d
        m_out[...] = mn
        v_out[...] = vn

    blk = pl.BlockSpec((tr, c), lambda i, ch: (i, 0))
    return pl.pallas_call(
        body, name=name,
        grid_spec=pltpu.PrefetchScalarGridSpec(
            num_scalar_prefetch=1, grid=(r // tr,),
            in_specs=[pl.BlockSpec((1, tr, c), lambda i, ch: (ch[0], i, 0)),
                      pl.BlockSpec((3, tr, c), lambda i, ch: (0, i, 0)), blk, blk, blk]
            + [pl.BlockSpec((8, DH), lambda i, ch: (0, 0))] * len(order),
            out_specs=[blk] * 4),
        out_shape=[jax.ShapeDtypeStruct((r, c), F32)] * 4,
        compiler_params=_cparams(("parallel",)),
    )(chip, parts, recv, w, m, v, *order)


SMALL_ROWS, SMALL_COLS = 48, 3 * GW


def _pack_small(gains4, meta, conv_qkv, conv_sc, a_log, dt_bias, gdn_norm, extra=None):
    def tile(a):
        return jnp.pad(a, ((0, -a.shape[0] % 8), (0, SMALL_COLS - a.shape[1])))
    scal = jnp.concatenate([jnp.pad(a_log, ((0, 0), (0, DH - HEADS))), jnp.pad(dt_bias, ((0, 0), (0, DH - HEADS))),
                            gdn_norm] + ([] if extra is None else [extra]), axis=1)
    gains = sum(jnp.pad(g, ((i, 7 - i), (0, SMALL_COLS - g.shape[1]))) for i, g in enumerate(gains4))
    parts = [gains, tile(meta), tile(conv_qkv), tile(conv_sc), tile(scal)]
    slab = jnp.concatenate(parts, axis=0)
    assert slab.shape == (SMALL_ROWS, SMALL_COLS)
    return slab


def kernel(x, meta_tokens, mix_pre_norm, mix_post_norm, ffn_pre_norm, ffn_post_norm, w_in, conv_qkv, a_log, dt_bias, gdn_norm, conv_sc, w_out, w_gate, w_up, w_down, loss_target, m_meta_tokens, m_mix_pre_norm, m_mix_post_norm, m_ffn_pre_norm, m_ffn_post_norm, m_w_in, m_conv_qkv, m_a_log, m_dt_bias, m_gdn_norm, m_conv_sc, m_w_out, m_w_gate, m_w_up, m_w_down, v_meta_tokens, v_mix_pre_norm, v_mix_post_norm, v_ffn_pre_norm, v_ffn_post_norm, v_w_in, v_conv_qkv, v_a_log, v_dt_bias, v_gdn_norm, v_conv_sc, v_w_out, v_w_gate, v_w_up, v_w_down):
    bsz, seq, _ = x.shape
    lp = -(-(N_META + seq) // CH) * CH
    padf = lp - N_META - seq
    first_real = padf + N_META
    rows = bsz * lp
    nchunk = lp // CH
    tr = _div_tile(lp, 528)
    tm = _div_tile(rows, 1056)
    tm_big = _div_tile(rows, 2112)
    cx, cy, cc = _place()

    small_shard = jnp.concatenate([
        jnp.pad(meta_tokens, ((0, 0), (0, 256 - DH))),
        jnp.pad(conv_qkv[0], ((0, 0), (0, 256 - 3 * GW // N_DEV))),
        jnp.pad(conv_sc[0], ((0, 5), (0, 256 - SW // N_DEV)))], axis=0)
    g_in, g_small = _all_gather("gather_w_in", [w_in[0].T.astype(BF16), small_shard])
    my_rest = [w_out[0].astype(BF16), w_gate[0].T.astype(BF16), w_up[0].T.astype(BF16), w_down[0].astype(BF16)]
    (grp_rest,), token = _split_start("gather_rest_start", [my_rest], _chipcast_pattern, lambda s: (N_DEV,) + s,
                                      after=g_in)
    after_start = token[0:1, 0:1]

    def cols_full(g):
        return jnp.transpose(g, (1, 0, 2)).reshape(g.shape[1], -1)

    win_t = g_in.reshape(IN_W, D)
    win_t = jnp.concatenate([win_t[:BA0 + 2 * HEADS], jnp.zeros((BA_W - 2 * HEADS, D), BF16),
                             win_t[BA0 + 2 * HEADS:]], axis=0)
    meta_full = cols_full(g_small[:, 0:16, :DH])
    convq_full = cols_full(g_small[:, 16:20, :3 * GW // N_DEV])
    convs_full = cols_full(g_small[:, 20:23, :SW // N_DEV])
    alog_v = jnp.pad(a_log, ((0, 0), (HEADS, DH - 2 * HEADS)))
    dtb_v = jnp.pad(dt_bias, ((0, 0), (HEADS, DH - 2 * HEADS)))

    head_rows = jnp.concatenate([jnp.zeros((padf, D), F32), meta_full], axis=0)
    h0, u1 = _pre_norm(x, head_rows, mix_pre_norm + after_start, bsz, lp, tr)
    (proj,) = _mm("proj_in", [(u1, win_t)], "nt", tm_big, 768, D, out_dtypes=(BF16,))
    (ba,) = _mm("proj_gate_logits", [(u1, win_t[BA0:SX0])], "nt", tm, BA_W, D)
    qkv = _gdn_prep(proj, convq_full, bsz, lp, padf)
    gb, gcs = _gates(ba, alog_v, dtb_v, bsz, lp, padf)

    def lane_bcast(cols):
        t = jnp.transpose(cols.reshape(bsz, lp, HEADS), (0, 2, 1)).reshape(bsz * HEADS, lp, 1)
        return jnp.broadcast_to(t, (bsz * HEADS, lp, DH))

    bb = lane_bcast(gb[:, 0:HEADS])
    gcb = lane_bcast(gcs[:, HEADS:2 * HEADS])
    gct = jnp.transpose(gcs[:, HEADS:2 * HEADS].reshape(bsz, nchunk, CH, HEADS), (0, 3, 1, 2)).reshape(
        bsz * HEADS, nchunk, 1, CH)
    u_loc, l1, l2, l3, w_t, k_dec, tinv_all, g_last = _gdn_local_fwd(qkv, gcb, bb, gct, bsz, lp)
    o, v_new, states = _gdn_scan_fwd(u_loc, l1, l2, g_last, bsz, lp)
    o, v_new = o.reshape(rows, GW), v_new.reshape(rows, GW)
    _, l_rest = _split_wait("gather_rest_arrive", grp_rest, _chipcast_pattern, o)
    (grp_out, grp_ffn, grp_down), token = _split_start(
        "gather_rest_forward", [l_rest[0:1], l_rest[1:3], l_rest[3:4]], _forward_pattern, None)
    mix_g, mix_s = _mix_heads(proj, o, gdn_norm + token[0:1, :], convs_full, bsz, lp, padf)
    mixin = jnp.concatenate([mix_g, mix_s], axis=1)
    _, (l_out,) = _split_wait("gather_w_out_wait", grp_out, _forward_pattern, mixin)
    wout = l_out.reshape(D, D)
    (mix,) = _mm("proj_out", [(mixin, wout)], "nn", tm, D, D)
    h1, u2 = _mid_norms(h0, mix, mix_post_norm, ffn_pre_norm, bsz, lp, tr)
    _, (l_gate, l_up) = _split_wait("gather_ffn_wait", grp_ffn, _forward_pattern, u2)
    wgate_t, wup_t = l_gate.reshape(FF, D), l_up.reshape(FF, D)
    gate, up, act = _ffn_up(u2, wgate_t, wup_t, _div_tile(rows, 528), 1408)
    _, (l_down,) = _split_wait("gather_down_wait", grp_down, _forward_pattern, act)
    wdown = l_down.reshape(FF, D)
    (ffn,) = _mm("ffn_down", [(act, wdown)], "nn", tm, D, 1408)

    dy, dffn, d_g4, loss_part = _loss_head(h1, ffn, loss_target, ffn_post_norm, bsz, lp, tr, first_real)
    (dwdown,) = _mm("dw_down", [(act, dffn)], "tn", 1408, D, tm, out_dtypes=(BF16,))

    def swiglu_bwd(da, g, u):
        g, u = g.astype(F32), u.astype(F32)
        return da * u * _dsilu(g), da * _silu(g)

    dgate, dup = _mm("d_act", [(dffn, wdown)], "nt", tm, 1408, D, out_dtypes=(BF16, BF16),
                     epilogue=swiglu_bwd, extras=(gate, up))
    (dwgate_t,) = _mm("dw_gate", [(dgate, u2)], "tn", 1408, D, tm, out_dtypes=(BF16,))
    (dwup_t,) = _mm("dw_up", [(dup, u2)], "tn", 1408, D, tm, out_dtypes=(BF16,))
    (du2,) = _mm("d_u2", [(dgate, wgate_t), (dup, wup_t)], "nn", tm, D, 1408, out_dtypes=(BF16,))
    dh1, dmix, d_g3, d_g2 = _mid_norms_bwd(h1, mix, du2, dy, ffn_pre_norm, mix_post_norm, bsz, lp, tr)
    (dwout,) = _mm("dw_out", [(mixin, dmix)], "tn", D, D, tm, out_dtypes=(BF16,))

    def row_blocks(g):
        return g.reshape(4, 2, -1, g.shape[1])

    core = jnp.reshape(cc, (1,)).astype(jnp.int32)
    chip = jnp.reshape(2 * cx + cy, (1,)).astype(jnp.int32)

    def reduce_start(tag, grads):
        (grp,), tok = _split_start("rs_sibling_start_" + tag, [grads], _sibling_pattern, lambda s: s[:1] + s[2:])
        return grp, tok

    def reduce_mid(tag, grp, nms, after):
        mine, lands = _split_wait("rs_sibling_wait_" + tag, grp, _sibling_pattern, after)
        parts = [_pair_sum("pair_sum_" + nm, g, r, core) for nm, g, r in zip(nms, mine, lands)]
        (grp2,), tok = _split_start("rs_chip_start_" + tag, [parts], _chip_pattern, lambda s: (3,) + s[1:])
        return grp2, tok

    def reduce_end(tag, grp2, after):
        return _split_wait("rs_chip_wait_" + tag, grp2, _chip_pattern, after)

    names_ffn = ["w_down", "w_gate", "w_up", "w_out"]
    rs_ffn, tok = reduce_start("ffn", [row_blocks(dwdown), row_blocks(dwgate_t), row_blocks(dwup_t), row_blocks(dwout)])
    (dmixin,) = _mm("d_mixin", [(dmix, wout)], "nt", tm, D, D, out_dtypes=(BF16,), after=tok)
    d_o, dz, dsx, dsb, dsc, d_gn, d_convs = _mix_heads_bwd(proj, o, gdn_norm, convs_full, dmixin, bsz, lp, padf)
    rs_ffn, tok = reduce_mid("ffn", rs_ffn, names_ffn, d_o)
    dv_new, dstates = _gdn_scan_bwd(d_o, l3, w_t, k_dec, g_last + tok[0, 0], bsz, lp)
    dq, dk, dv, dg_b, dbeta_b = _gdn_local_bwd(qkv, gcb, bb, gct, tinv_all, states, dstates, v_new,
                                               dv_new.reshape(rows, GW), d_o, bsz, lp)
    dpqkv, d_convq = _gdn_prep_bwd(proj, convq_full, dq, dk, dv, bsz, lp, padf)
    dba, d_scal = _gates_bwd(ba, alog_v, dtb_v, dbeta_b, dg_b, gb, bsz, lp, padf)
    dproj = jnp.concatenate([dpqkv, dz, dba, dsx, dsb, dsc], axis=1)

    d_alog = d_scal[0:1, HEADS:2 * HEADS]
    d_dtb = d_scal[1:2, HEADS:2 * HEADS]
    cq_w, cs_w = 3 * GW // N_DEV, SW // N_DEV
    slab = jnp.stack([
        _pack_small([jnp.zeros_like(d_g2), d_g2, d_g3, d_g4], jnp.zeros((N_META, DH), F32),
                    d_convq[:, p * cq_w:(p + 1) * cq_w], d_convs[:, p * cs_w:(p + 1) * cs_w], d_alog, d_dtb, d_gn,
                    loss_part) for p in range(N_DEV)])
    (grp_small,), tok = _split_start("small_grads_start", [[slab]], _alltoall_pattern, lambda s: s)
    (dwin_t,) = _mm("dw_in", [(dproj, u1)], "tn", 1280, D, tm, out_dtypes=(BF16,), after=tok)
    parts_ffn, recv_ffn = reduce_end("ffn", rs_ffn, dwin_t)
    dwin_t = jnp.concatenate([dwin_t[:BA0 + 2 * HEADS], dwin_t[SX0:]], axis=0)
    rs_in, tok = reduce_start("in", [row_blocks(dwin_t)])

    big_w = {"w_in": (w_in, m_w_in, v_w_in), "w_out": (w_out, m_w_out, v_w_out), "w_gate": (w_gate, m_w_gate, v_w_gate),
             "w_up": (w_up, m_w_up, v_w_up), "w_down": (w_down, m_w_down, v_w_down)}

    def update(nm, parts, recv, after=None):
        w, m, v = big_w[nm]
        if nm in ("w_out", "w_down"):
            return _adamw_reduced("adamw_" + nm, parts, chip, recv, w[0], m[0], v[0], after)
        res = _adamw_reduced("adamw_" + nm, parts, chip, recv, w[0].T, m[0].T, v[0].T, after)
        return [t.T for t in res]

    big_out = {nm: update(nm, p, r, tok) for nm, p, r in zip(names_ffn, parts_ffn, recv_ffn)}
    ffn_done = sum(big_out[nm][1][:1, :1] for nm in names_ffn)
    rs_in, tok = reduce_mid("in", rs_in, ["w_in"], ffn_done)
    (du1,) = _mm("d_u1", [(dproj, win_t)], "nn", tm_big, D, 768, out_dtypes=(BF16,), after=tok)
    grad_x, d_g1, d_meta = _pre_norm_bwd(h0, du1, dh1, mix_pre_norm, bsz, lp, seq, padf)

    late = jnp.stack([
        jnp.concatenate([jnp.pad(d_g1, ((0, 7), (0, 0))), jnp.pad(d_meta[:, p * DH:(p + 1) * DH], ((0, 0), (0, D - DH)))],
                        axis=0) for p in range(N_DEV)])
    (grp_late,), _ = _split_start("late_grads_start", [[late]], _alltoall_pattern, lambda s: s)
    _, (slabs_local,) = _split_wait("small_grads_wait", grp_small, _alltoall_pattern, grad_x)
    _, (late_local,) = _split_wait("late_grads_wait", grp_late, _alltoall_pattern, slabs_local)
    w_small = _pack_small([mix_pre_norm, mix_post_norm, ffn_pre_norm, ffn_post_norm], meta_tokens, conv_qkv[0],
                          conv_sc[0], a_log, dt_bias, gdn_norm)
    m_small = _pack_small([m_mix_pre_norm, m_mix_post_norm, m_ffn_pre_norm, m_ffn_post_norm], m_meta_tokens,
                          m_conv_qkv[0], m_conv_sc[0], m_a_log, m_dt_bias, m_gdn_norm)
    v_small = _pack_small([v_mix_pre_norm, v_mix_post_norm, v_ffn_pre_norm, v_ffn_post_norm], v_meta_tokens,
                          v_conv_qkv[0], v_conv_sc[0], v_a_log, v_dt_bias, v_gdn_norm)
    small_out, loss = _adamw_small(slabs_local, late_local, w_small, m_small, v_small)
    parts_in, recv_in = reduce_end("in", rs_in, small_out[1]["gdn_norm"])
    big_out["w_in"] = update("w_in", parts_in[0], recv_in[0])
    names = ["w_in", "w_out", "w_gate", "w_up", "w_down"]

    order = ["meta_tokens", "mix_pre_norm", "mix_post_norm", "ffn_pre_norm", "ffn_post_norm", "w_in", "conv_qkv",
             "a_log", "dt_bias", "gdn_norm", "conv_sc", "w_out", "w_gate", "w_up", "w_down"]
    outs = [loss[0, 0], grad_x]
    for kind in range(4):
        leaves = dict(small_out[kind])
        for nm in names:
            leaves[nm] = big_out[nm][kind][None]
        outs += [leaves[nm] for nm in order]
    return tuple(outs)
```

```python
import jax
import jax.numpy as jnp
from jax import lax
from jax.experimental import pallas as pl
from jax.experimental.pallas import tpu as pltpu

F32 = jnp.float32
BF16 = jnp.bfloat16
MESH = pl.DeviceIdType.MESH

D = 1024
N_META = 16
HEADS = 4
DH = 128
GW = HEADS * DH
SW = D - GW
FF = 2816
CH = 64
KQ = 4
KS = 3
EPS = 1e-6
IN_W = 3 * GW + GW + 2 * HEADS + 3 * SW
Q0, Z0, BA0, SX0, SB0, SC0 = 0, 1536, 2048, 2304, 2816, 3328
BA_W = SX0 - BA0
N_DEV = 8
VMEM_LIMIT = 56 * 1024 * 1024

ADAM_LR, ADAM_B1, ADAM_B2, ADAM_EPS, ADAM_WD, ADAM_STEP = 0.001, 0.9, 0.999, 1e-08, 0.01, 10


def _cparams(sem):
    return pltpu.CompilerParams(dimension_semantics=sem, vmem_limit_bytes=VMEM_LIMIT)


def _div_tile(n, target, mult=16):
    best = None
    for t in range(mult, min(n, target) + 1, mult):
        if n % t == 0:
            best = t
    assert best is not None, (n, target)
    return best


def _silu(x):
    return x * jax.nn.sigmoid(x)


def _dsilu(x):
    s = jax.nn.sigmoid(x)
    return s * (1.0 + x * (1.0 - s))


def _bdot(a, b, dims):
    return lax.dot_general(a.astype(BF16), b.astype(BF16), (dims, ((), ())), preferred_element_type=F32)


NN = ((1,), (0,))
NT = ((1,), (1,))
TN = ((0,), (0,))


def _mm(name, pairs, mode, tm, tn, tk, out_dtypes=(F32,), epilogue=None, extras=(), after=None):
    a0, b0 = pairs[0]
    if mode == "nn":
        (m, k), n = a0.shape, b0.shape[1]
    elif mode == "nt":
        (m, k), n = a0.shape, b0.shape[0]
    else:
        (k, m), n = a0.shape, b0.shape[1]
    assert m % tm == 0 and n % tn == 0 and k % tk == 0, (name, m, n, k, tm, tn, tk)
    nk = k // tk
    npair = len(pairs)
    nex = len(extras)
    nout = len(out_dtypes)
    ntok = 0 if after is None else 1
    dims = {"nn": NN, "nt": NT, "tn": TN}[mode]

    def body(*refs):
        ab = refs[:2 * npair]
        ex = refs[2 * npair:2 * npair + nex]
        outs = refs[2 * npair + nex + ntok:2 * npair + nex + ntok + nout]
        acc_ref = refs[-1]
        kk = pl.program_id(2)

        part = None
        for p in range(npair):
            d = lax.dot_general(ab[2 * p][...], ab[2 * p + 1][...], (dims, ((), ())), preferred_element_type=F32)
            part = d if part is None else part + d

        def finish(acc):
            res = epilogue(acc, *[e[...] for e in ex]) if epilogue is not None else (acc,)
            for o, r in zip(outs, res):
                o[...] = r.astype(o.dtype)

        if nk == 1:
            finish(part)
        else:
            @pl.when(kk == 0)
            def _():
                acc_ref[...] = part

            @pl.when(kk > 0)
            def _():
                acc_ref[...] += part

            @pl.when(kk == nk - 1)
            def _():
                finish(acc_ref[...])

    if mode == "nn":
        a_spec = pl.BlockSpec((tm, tk), lambda i, j, q: (i, q))
        b_spec = pl.BlockSpec((tk, tn), lambda i, j, q: (q, j))
    elif mode == "nt":
        a_spec = pl.BlockSpec((tm, tk), lambda i, j, q: (i, q))
        b_spec = pl.BlockSpec((tn, tk), lambda i, j, q: (j, q))
    else:
        a_spec = pl.BlockSpec((tk, tm), lambda i, j, q: (q, i))
        b_spec = pl.BlockSpec((tk, tn), lambda i, j, q: (q, j))
    o_spec = pl.BlockSpec((tm, tn), lambda i, j, q: (i, j))
    flat = [t for pr in pairs for t in pr]
    res = pl.pallas_call(
        body, name=name,
        grid=(m // tm, n // tn, nk),
        in_specs=[a_spec, b_spec] * npair + [o_spec] * nex + [pl.BlockSpec((8, DH), lambda i, j, q: (0, 0))] * ntok,
        out_specs=[o_spec] * nout,
        out_shape=[jax.ShapeDtypeStruct((m, n), dt) for dt in out_dtypes],
        scratch_shapes=[pltpu.VMEM((tm, tn) if nk > 1 else (8, DH), F32)],
        compiler_params=_cparams(("parallel", "parallel", "arbitrary")),
    )(*flat, *extras, *([] if after is None else [after]))
    return res


def _ffn_up(u2, wg, wu, tm, tn):
    m, k = u2.shape
    n = wg.shape[0]

    def body(a_ref, g_ref, u_ref, gate_ref, up_ref, act_ref):
        a = a_ref[...]
        g = lax.dot_general(a, g_ref[...], (NT, ((), ())), preferred_element_type=F32)
        u = lax.dot_general(a, u_ref[...], (NT, ((), ())), preferred_element_type=F32)
        gate_ref[...] = g.astype(BF16)
        up_ref[...] = u.astype(BF16)
        act_ref[...] = (_silu(g) * u).astype(BF16)

    o_spec = pl.BlockSpec((tm, tn), lambda j, i: (i, j))
    w_spec = pl.BlockSpec((tn, k), lambda j, i: (j, 0))
    return pl.pallas_call(
        body, name="ffn_up", grid=(n // tn, m // tm),
        in_specs=[pl.BlockSpec((tm, k), lambda j, i: (i, 0)), w_spec, w_spec],
        out_specs=[o_spec, o_spec, o_spec],
        out_shape=[jax.ShapeDtypeStruct((m, n), BF16)] * 3,
        compiler_params=_cparams(("parallel", "parallel")),
    )(u2, wg, wu)


def _rms(x, gain):
    r = lax.rsqrt(jnp.mean(x * x, axis=-1, keepdims=True) + EPS)
    return x * r * gain, r


def _rms_bwd(x, gain, dy):
    r = lax.rsqrt(jnp.mean(x * x, axis=-1, keepdims=True) + EPS)
    dyw = dy * gain
    dx = r * dyw - x * (r * r * r) * jnp.mean(dyw * x, axis=-1, keepdims=True)
    return dx, dy * x * r


def _row_specs(tr, width, nb):
    return pl.BlockSpec((tr, width), lambda b, j: (b * nb + j, 0))


def _vec_spec(width):
    return pl.BlockSpec((1, width), lambda b, j: (0, 0))


def _first_step(b, j):
    return jnp.logical_and(b == 0, j == 0)


def _pre_norm(x, head_rows, gain, bsz, lp, tr):
    nb = lp // tr
    first_real = head_rows.shape[0]
    assert first_real % 8 == 0 and first_real < tr
    x_spec = pl.BlockSpec((pl.Squeezed(), pl.Element(tr), pl.Element(D)),
                          lambda b, j: (b, pl.multiple_of(jnp.maximum(j * tr - first_real, 0), 8), 0))

    def body(x_ref, hd_ref, g_ref, h_ref, u_ref):
        j = pl.program_id(1)
        xt = x_ref[...]
        h = jnp.where(j == 0, jnp.concatenate([hd_ref[...], xt[:tr - first_real]], axis=0), xt)
        h_ref[...] = h
        u_ref[...] = _rms(h, g_ref[...])[0].astype(BF16)

    rs = _row_specs(tr, D, nb)
    return pl.pallas_call(
        body, name="mix_pre_norm", grid=(bsz, nb),
        in_specs=[x_spec, pl.BlockSpec((first_real, D), lambda b, j: (0, 0)), _vec_spec(D)],
        out_specs=[rs, rs],
        out_shape=[jax.ShapeDtypeStruct((bsz * lp, D), F32), jax.ShapeDtypeStruct((bsz * lp, D), BF16)],
        compiler_params=_cparams(("parallel", "parallel")),
    )(x, head_rows, gain)


def _mid_norms(h0, mix, g_post, g_pre, bsz, lp, tr):
    nb = lp // tr

    def body(h_ref, m_ref, gp_ref, gq_ref, h1_ref, u2_ref):
        h1 = h_ref[...] + _rms(m_ref[...].astype(F32), gp_ref[...])[0]
        h1_ref[...] = h1
        u2_ref[...] = _rms(h1, gq_ref[...])[0].astype(BF16)

    rs = _row_specs(tr, D, nb)
    return pl.pallas_call(
        body, name="mid_norms", grid=(bsz, nb),
        in_specs=[rs, rs, _vec_spec(D), _vec_spec(D)],
        out_specs=[rs, rs],
        out_shape=[jax.ShapeDtypeStruct(h0.shape, F32), jax.ShapeDtypeStruct(h0.shape, BF16)],
        compiler_params=_cparams(("parallel", "parallel")),
    )(h0, mix, g_post, g_pre)


def _loss_head(h1, ffn, tgt, g_post, bsz, lp, tr, first_real):
    nb = lp // tr
    assert first_real % 8 == 0 and first_real < tr
    tgt_spec = pl.BlockSpec((pl.Squeezed(), pl.Element(tr), pl.Element(D)),
                            lambda b, j: (b, pl.multiple_of(jnp.maximum(j * tr - first_real, 0), 8), 0))

    def body(h_ref, f_ref, t_ref, g_ref, dy_ref, dffn_ref, dg_ref, loss_ref):
        b, j = pl.program_id(0), pl.program_id(1)
        f = f_ref[...].astype(F32)
        g = g_ref[...]
        y = h_ref[...] + _rms(f, g)[0]
        t = j * tr + lax.broadcasted_iota(jnp.int32, (tr, 1), 0)
        tg = t_ref[...]
        tg = jnp.where(j == 0, jnp.concatenate([tg[tr - first_real:], tg[:tr - first_real]], axis=0), tg)
        err = jnp.where(t >= first_real, y - tg, 0.0)
        dy = err * (1.0 / D)
        dy_ref[...] = dy
        dx, dgc = _rms_bwd(f, g, dy)
        dffn_ref[...] = dx.astype(BF16)
        lsum = jnp.sum(jnp.sum(err * err, axis=0, keepdims=True), axis=1, keepdims=True) * (0.5 / D)

        @pl.when(_first_step(b, j))
        def _():
            dg_ref[...] = jnp.zeros_like(dg_ref)
            loss_ref[...] = jnp.zeros_like(loss_ref)

        dg_ref[...] += jnp.sum(dgc, axis=0, keepdims=True)
        loss_ref[...] += jnp.broadcast_to(lsum, loss_ref.shape)

    rs = _row_specs(tr, D, nb)
    return pl.pallas_call(
        body, name="loss_head", grid=(bsz, nb),
        in_specs=[rs, rs, tgt_spec, _vec_spec(D)],
        out_specs=[rs, rs, _vec_spec(D), _vec_spec(DH)],
        out_shape=[jax.ShapeDtypeStruct(h1.shape, F32), jax.ShapeDtypeStruct(h1.shape, BF16),
                   jax.ShapeDtypeStruct((1, D), F32), jax.ShapeDtypeStruct((1, DH), F32)],
        compiler_params=_cparams(("arbitrary", "arbitrary")),
    )(h1, ffn, tgt, g_post)


def _mid_norms_bwd(h1, mix, du2, dy, g_pre, g_post, bsz, lp, tr):
    nb = lp // tr

    def body(h_ref, m_ref, du_ref, dy_ref, gq_ref, gp_ref, dh1_ref, dmix_ref, dgq_ref, dgp_ref):
        b, j = pl.program_id(0), pl.program_id(1)
        dx, dgq = _rms_bwd(h_ref[...], gq_ref[...], du_ref[...].astype(F32))
        dh1 = dy_ref[...] + dx
        dh1_ref[...] = dh1
        dm, dgp = _rms_bwd(m_ref[...].astype(F32), gp_ref[...], dh1)
        dmix_ref[...] = dm.astype(BF16)

        @pl.when(_first_step(b, j))
        def _():
            dgq_ref[...] = jnp.zeros_like(dgq_ref)
            dgp_ref[...] = jnp.zeros_like(dgp_ref)

        dgq_ref[...] += jnp.sum(dgq, axis=0, keepdims=True)
        dgp_ref[...] += jnp.sum(dgp, axis=0, keepdims=True)

    rs = _row_specs(tr, D, nb)
    return pl.pallas_call(
        body, name="mid_norms_bwd", grid=(bsz, nb),
        in_specs=[rs, rs, rs, rs, _vec_spec(D), _vec_spec(D)],
        out_specs=[rs, rs, _vec_spec(D), _vec_spec(D)],
        out_shape=[jax.ShapeDtypeStruct(h1.shape, F32), jax.ShapeDtypeStruct(h1.shape, BF16),
                   jax.ShapeDtypeStruct((1, D), F32), jax.ShapeDtypeStruct((1, D), F32)],
        compiler_params=_cparams(("arbitrary", "arbitrary")),
    )(h1, mix, du2, dy, g_pre, g_post)


def _pre_norm_bwd(h0, du1, dh1, gain, bsz, lp, seq, padf):
    first_real = lp - seq
    tx = _div_tile(seq, 512)
    assert first_real % 8 == 0 and lp % first_real == 0

    def tokens(h_ref, du_ref, dh1_ref, g_ref, gx_ref, dg_ref):
        b, j = pl.program_id(0), pl.program_id(1)
        dx, dgc = _rms_bwd(h_ref[...], g_ref[...], du_ref[...].astype(F32))
        gx_ref[...] = dh1_ref[...] + dx

        @pl.when(_first_step(b, j))
        def _():
            dg_ref[...] = jnp.zeros_like(dg_ref)

        dg_ref[...] += jnp.sum(dgc, axis=0, keepdims=True)

    window = pl.BlockSpec((pl.Element(tx), pl.Element(D)),
                          lambda b, j: (pl.multiple_of(b * lp + first_real + j * tx, 16), 0))
    assert first_real % 16 == 0 and tx % 16 == 0 and lp % 16 == 0
    grad_x, dg_tokens = pl.pallas_call(
        tokens, name="pre_norm_bwd", grid=(bsz, seq // tx),
        in_specs=[window, window, window, _vec_spec(D)],
        out_specs=[pl.BlockSpec((pl.Squeezed(), tx, D), lambda b, j: (b, j, 0)), _vec_spec(D)],
        out_shape=[jax.ShapeDtypeStruct((bsz, seq, D), F32), jax.ShapeDtypeStruct((1, D), F32)],
        compiler_params=_cparams(("arbitrary", "arbitrary")),
    )(h0, du1, dh1, gain)

    def head(h_ref, du_ref, dh1_ref, g_ref, dgt_ref, dg_ref, dmeta_ref):
        b = pl.program_id(0)
        dx, dgc = _rms_bwd(h_ref[...], g_ref[...], du_ref[...].astype(F32))
        dh0 = dh1_ref[...] + dx

        @pl.when(b == 0)
        def _():
            dg_ref[...] = dgt_ref[...]
            dmeta_ref[...] = jnp.zeros_like(dmeta_ref)

        dg_ref[...] += jnp.sum(dgc, axis=0, keepdims=True)
        dmeta_ref[...] += dh0[padf:padf + N_META, :]

    rows = pl.BlockSpec((first_real, D), lambda b: (b * (lp // first_real), 0))
    vec = pl.BlockSpec((1, D), lambda b: (0, 0))
    dg, dmeta = pl.pallas_call(
        head, name="pre_norm_bwd_head", grid=(bsz,),
        in_specs=[rows, rows, rows, vec, vec],
        out_specs=[vec, pl.BlockSpec((N_META, D), lambda b: (0, 0))],
        out_shape=[jax.ShapeDtypeStruct((1, D), F32), jax.ShapeDtypeStruct((N_META, D), F32)],
        compiler_params=_cparams(("arbitrary",)),
    )(h0, du1, dh1, gain, dg_tokens)
    return grad_x, dg, dmeta


def _shift_down(x, s, lp):
    return x if s == 0 else pltpu.roll(x, s, axis=0)


def _shift_up(x, s, lp):
    return x if s == 0 else pltpu.roll(x, lp - s, axis=0)


def _conv_fwd(x, w, taps, lp):
    y = None
    for i in range(taps):
        term = _shift_down(x, taps - 1 - i, lp) * w[i:i + 1, :]
        y = term if y is None else y + term
    return y


def _conv_bwd(x, w, dy, taps, lp):
    dx = None
    dws = []
    for i in range(taps):
        s = taps - 1 - i
        term = _shift_up(dy, s, lp) * w[i:i + 1, :]
        dx = term if dx is None else dx + term
        dws.append(jnp.sum(dy * _shift_down(x, s, lp), axis=0, keepdims=True))
    return dx, jnp.concatenate(dws, axis=0)


def _slab(lp, col0):
    return pl.BlockSpec((lp, DH), lambda b, s: (b, col0 // DH + s))


def _qkv_act(c, s):
    y = _silu(c)
    r = lax.rsqrt(jnp.sum(y * y, axis=-1, keepdims=True) + EPS)
    scale = jnp.where(s < HEADS, DH ** -0.5, 1.0)
    return y, r, scale


def _gdn_prep(proj, conv_qkv, bsz, lp, padf):
    def body(p_ref, w_ref, o_ref):
        s = pl.program_id(1)
        c = _conv_fwd(p_ref[...].astype(F32), w_ref[...], KQ, lp)
        y, r, scale = _qkv_act(c, s)
        out = jnp.where(s < 2 * HEADS, y * (r * scale), y)
        t = lax.broadcasted_iota(jnp.int32, (lp, 1), 0)
        o_ref[...] = jnp.where(t >= padf, out, 0.0)

    return pl.pallas_call(
        body, name="gdn_prep", grid=(bsz, 3 * HEADS),
        in_specs=[_slab(lp, Q0), pl.BlockSpec((KQ, DH), lambda b, s: (0, s))],
        out_specs=_slab(lp, 0),
        out_shape=jax.ShapeDtypeStruct((bsz * lp, 3 * GW), F32),
        compiler_params=_cparams(("parallel", "parallel")),
    )(proj, conv_qkv)


def _gdn_prep_bwd(proj, conv_qkv, dq, dk, dv, bsz, lp, padf):
    def body(p_ref, w_ref, dq_ref, dk_ref, dv_ref, dp_ref, dw_ref):
        b, s = pl.program_id(0), pl.program_id(1)
        x = p_ref[...].astype(F32)
        w = w_ref[...]
        c = _conv_fwd(x, w, KQ, lp)
        y, r, scale = _qkv_act(c, s)
        d = jnp.where(s < HEADS, dq_ref[...], jnp.where(s < 2 * HEADS, dk_ref[...], dv_ref[...]))
        dn = d * scale
        n = y * r
        dy_norm = r * (dn - n * jnp.sum(dn * n, axis=-1, keepdims=True))
        dyy = jnp.where(s < 2 * HEADS, dy_norm, dn)
        t = lax.broadcasted_iota(jnp.int32, (lp, 1), 0)
        dc = jnp.where(t >= padf, dyy * _dsilu(c), 0.0)
        dx, dw = _conv_bwd(x, w, dc, KQ, lp)
        dp_ref[...] = jnp.where(t >= padf, dx, 0.0).astype(BF16)

        @pl.when(jnp.logical_and(b == 0, s == 0))
        def _():
            dw_ref[...] = jnp.zeros_like(dw_ref)

        col_block = jnp.right_shift(lax.broadcasted_iota(jnp.int32, (KQ, 3 * GW), 1), DH.bit_length() - 1)
        dw_ref[...] += jnp.where(col_block == s, jnp.tile(dw, (1, 3 * HEADS)), 0.0)

    sl = lambda col0: pl.BlockSpec((lp, DH), lambda b, s: (b, col0 // DH + s))
    part = lambda k: pl.BlockSpec((lp, DH), lambda b, s: (b, jnp.clip(s - k * HEADS, 0, HEADS - 1)))
    return pl.pallas_call(
        body, name="gdn_prep_bwd", grid=(bsz, 3 * HEADS),
        in_specs=[sl(Q0), pl.BlockSpec((KQ, DH), lambda b, s: (0, s)), part(0), part(1), part(2)],
        out_specs=[sl(0), pl.BlockSpec((KQ, 3 * GW), lambda b, s: (0, 0))],
        out_shape=[jax.ShapeDtypeStruct((bsz * lp, 3 * GW), BF16), jax.ShapeDtypeStruct((KQ, 3 * GW), F32)],
        compiler_params=_cparams(("arbitrary", "arbitrary")),
    )(proj, conv_qkv, dq, dk, dv)


def _softplus(x):
    e = jnp.exp(-jnp.abs(x))
    one_e = 1.0 + e
    l1p = jnp.where(one_e == 1.0, e, jnp.log(one_e) * (e / (one_e - 1.0)))
    return jnp.maximum(x, 0.0) + l1p


def _tri(n, kind):
    i = lax.broadcasted_iota(jnp.int32, (n, n), 0)
    j = lax.broadcasted_iota(jnp.int32, (n, n), 1)
    return {"incl": i >= j, "strict": i > j, "upper": i <= j}[kind]


def _gates(proj, alog_v, dtb_v, bsz, lp, padf):
    nchunk = lp // CH

    def body(p_ref, al_ref, dt_ref, gb_ref, cs_ref):
        x = p_ref[...].astype(F32)
        lane = lax.broadcasted_iota(jnp.int32, (lp, DH), 1)
        t = lax.broadcasted_iota(jnp.int32, (lp, DH), 0)
        beta = jax.nn.sigmoid(x)
        g = -jnp.exp(al_ref[...]) * _softplus(x + dt_ref[...])
        gb = jnp.where(lane < HEADS, beta, jnp.where(lane < 2 * HEADS, g, 0.0))
        gb_ref[...] = jnp.where(t >= padf, gb, 0.0)
        ltri = _tri(CH, "incl").astype(BF16)
        rest = [gb_ref[pl.ds(n * CH, CH), :] for n in range(nchunk)]
        sums = None
        for _ in range(3):
            piece = [v.astype(BF16) for v in rest]
            rest = [v - p.astype(F32) for v, p in zip(rest, piece)]
            part = [jnp.dot(ltri, p, preferred_element_type=F32) for p in piece]
            sums = part if sums is None else [a + b for a, b in zip(sums, part)]
        for n in range(nchunk):
            cs_ref[pl.ds(n * CH, CH), :] = sums[n]

    blk = pl.BlockSpec((lp, DH), lambda b: (b, 0))
    vec = pl.BlockSpec((1, DH), lambda b: (0, 0))
    out = pl.BlockSpec((lp, DH), lambda b: (b, 0))
    return pl.pallas_call(
        body, name="gates", grid=(bsz,),
        in_specs=[blk, vec, vec], out_specs=[out, out],
        out_shape=[jax.ShapeDtypeStruct((bsz * lp, DH), F32)] * 2,
        compiler_params=_cparams(("parallel",)),
    )(proj, alog_v, dtb_v)


def _gates_bwd(proj, alog_v, dtb_v, dbeta_b, dg_b, gb, bsz, lp, padf):
    def body(p_ref, al_ref, dt_ref, db_ref, dg_ref, gb_ref, dl_ref, dv_ref):
        b = pl.program_id(0)
        x = p_ref[...].astype(F32)
        lane = lax.broadcasted_iota(jnp.int32, (lp, DH), 1)
        t = lax.broadcasted_iota(jnp.int32, (lp, DH), 0)
        dbeta = jnp.zeros((lp, DH), F32)
        dg = jnp.zeros((lp, DH), F32)
        for h in range(HEADS):
            dbeta = jnp.where(lane == h, db_ref[h], dbeta)
            dg = jnp.where(lane == HEADS + h, dg_ref[h], dg)
        real = t >= padf
        dbeta = jnp.where(real, dbeta, 0.0)
        dg = jnp.where(real, dg, 0.0)
        sb = jax.nn.sigmoid(x)
        neg_ea = -jnp.exp(al_ref[...])
        dsp = dg * neg_ea * jax.nn.sigmoid(x + dt_ref[...])
        dl = dbeta * sb * (1.0 - sb) + dsp
        dl_ref[...] = jnp.concatenate([dl, jnp.zeros((lp, BA_W - DH), F32)], axis=1).astype(BF16)
        dalog = jnp.sum(dg * gb_ref[...], axis=0, keepdims=True)
        ddt = jnp.sum(dsp, axis=0, keepdims=True)

        @pl.when(b == 0)
        def _():
            dv_ref[...] = jnp.zeros_like(dv_ref)

        dv_ref[...] += jnp.concatenate([dalog, ddt], axis=0)

    blk = pl.BlockSpec((lp, DH), lambda b: (b, 0))
    vec = pl.BlockSpec((1, DH), lambda b: (0, 0))
    hb = pl.BlockSpec((HEADS, lp, DH), lambda b: (b, 0, 0))
    return pl.pallas_call(
        body, name="gates_bwd", grid=(bsz,),
        in_specs=[blk, vec, vec, hb, hb, pl.BlockSpec((lp, DH), lambda b: (b, 0))],
        out_specs=[pl.BlockSpec((lp, BA_W), lambda b: (b, 0)), pl.BlockSpec((2, DH), lambda b: (0, 0))],
        out_shape=[jax.ShapeDtypeStruct((bsz * lp, BA_W), BF16), jax.ShapeDtypeStruct((2, DH), F32)],
        compiler_params=_cparams(("arbitrary",)),
    )(proj, alog_v, dtb_v, dbeta_b, dg_b, gb)


def _mix_heads(proj, o, gdn_norm, conv_sc, bsz, lp, padf):
    def body(o_ref, z_ref, gn_ref, sx_ref, sb_ref, sc_ref, w_ref, og_ref, os_ref):
        og_ref[...] = (_rms(o_ref[...], gn_ref[...])[0] * _silu(z_ref[...].astype(F32))).astype(BF16)
        conv = _conv_fwd(sc_ref[...].astype(F32) * sx_ref[...].astype(F32), w_ref[...], KS, lp)
        t = lax.broadcasted_iota(jnp.int32, (lp, 1), 0)
        os_ref[...] = jnp.where(t >= padf, sb_ref[...].astype(F32) * conv, 0.0).astype(BF16)

    half = _slab(lp, 0)
    return pl.pallas_call(
        body, name="mix_heads", grid=(bsz, HEADS),
        in_specs=[half, _slab(lp, Z0), pl.BlockSpec((1, DH), lambda b, s: (0, 0)),
                  _slab(lp, SX0), _slab(lp, SB0), _slab(lp, SC0), pl.BlockSpec((KS, DH), lambda b, s: (0, s))],
        out_specs=[half, half],
        out_shape=[jax.ShapeDtypeStruct((bsz * lp, GW), BF16)] * 2,
        compiler_params=_cparams(("parallel", "parallel")),
    )(o, proj, gdn_norm, proj, proj, proj, conv_sc)


def _mix_heads_bwd(proj, o, gdn_norm, conv_sc, dmixin, bsz, lp, padf):
    def body(o_ref, z_ref, gn_ref, sx_ref, sb_ref, sc_ref, w_ref, dg_ref, ds_ref,
             do_ref, dz_ref, dsx_ref, dsb_ref, dsc_ref, dgn_ref, dw_ref):
        s, b = pl.program_id(0), pl.program_id(1)
        t = lax.broadcasted_iota(jnp.int32, (lp, 1), 0)
        real = t >= padf
        o, z, gn, d = o_ref[...], z_ref[...].astype(F32), gn_ref[...], dg_ref[...].astype(F32)
        sz = _silu(z)
        on = _rms(o, gn)[0]
        dz_ref[...] = (d * on * _dsilu(z)).astype(BF16)
        dox, dgn = _rms_bwd(o, gn, d * sz)
        do_ref[...] = dox
        sx, sb, sc = sx_ref[...].astype(F32), sb_ref[...].astype(F32), sc_ref[...].astype(F32)
        w, e = w_ref[...], ds_ref[...].astype(F32)
        e = jnp.where(real, e, 0.0)
        xin = sc * sx
        conv = _conv_fwd(xin, w, KS, lp)
        dsb_ref[...] = (e * conv).astype(BF16)
        dxin, dw = _conv_bwd(xin, w, e * sb, KS, lp)
        dxin = jnp.where(real, dxin, 0.0)
        dsx_ref[...] = (dxin * sc).astype(BF16)
        dsc_ref[...] = (dxin * sx).astype(BF16)

        @pl.when(jnp.logical_and(s == 0, b == 0))
        def _():
            dgn_ref[...] = jnp.zeros_like(dgn_ref)

        @pl.when(b == 0)
        def _():
            dw_ref[...] = jnp.zeros_like(dw_ref)

        dgn_ref[...] += jnp.sum(dgn, axis=0, keepdims=True)
        dw_ref[...] += dw

    sl = lambda col0: pl.BlockSpec((lp, DH), lambda s, b: (b, col0 // DH + s))
    half = sl(0)
    return pl.pallas_call(
        body, name="mix_heads_bwd", grid=(HEADS, bsz),
        in_specs=[half, sl(Z0), pl.BlockSpec((1, DH), lambda s, b: (0, 0)), sl(SX0), sl(SB0), sl(SC0),
                  pl.BlockSpec((KS, DH), lambda s, b: (0, s)), sl(0), sl(GW)],
        out_specs=[half] * 5 + [pl.BlockSpec((1, DH), lambda s, b: (0, 0)), pl.BlockSpec((KS, DH), lambda s, b: (0, s))],
        out_shape=[jax.ShapeDtypeStruct((bsz * lp, GW), F32)] + [jax.ShapeDtypeStruct((bsz * lp, GW), BF16)] * 4
        + [jax.ShapeDtypeStruct((1, DH), F32), jax.ShapeDtypeStruct((KS, SW), F32)],
        compiler_params=_cparams(("arbitrary", "arbitrary")),
    )(o, proj, gdn_norm, proj, proj, proj, conv_sc, dmixin, dmixin)


def _split_dot(a, b, dims=NN):
    ah = a.astype(BF16)
    al = (a - ah.astype(F32)).astype(BF16)
    bh = b.astype(BF16)
    bl = (b - bh.astype(F32)).astype(BF16)

    def d(x, y):
        return lax.dot_general(x, y, (dims, ((), ())), preferred_element_type=F32)

    return d(ah, bh) + (d(ah, bl) + d(al, bh))


def _eye(n):
    return (lax.broadcasted_iota(jnp.int32, (n, n), 0) == lax.broadcasted_iota(jnp.int32, (n, n), 1)).astype(F32)


def _each(f, *lists):
    return [f(*xs) for xs in zip(*lists)]


def _inv_unit_lower(a_list):
    eye = _eye(CH)
    p = _each(lambda a: -a, a_list)
    t = _each(lambda x: eye + x, p)
    p = _each(lambda x: _split_dot(x, x), p)
    for level in range(5):
        if level < 4:
            prod = _each(lambda tt, pp: _split_dot(jnp.concatenate([tt, pp], axis=0), pp), t, p)
            t = _each(lambda tt, pr: tt + pr[:CH], t, prod)
            p = _each(lambda pr: pr[CH:], prod)
        else:
            t = _each(lambda tt, pp: tt + _split_dot(tt, pp), t, p)
    return t


def _chunk_gates(gc_ref, be_ref, gt_ref, g):
    rows = pl.ds(g * CH, CH)
    gc = gc_ref[0, rows, :]
    be = be_ref[0, rows, :]
    gcl = gc_ref[0, pl.ds(g * CH + CH - 1, 1), :]
    gt = gt_ref[0, g]
    incl = _tri(CH, "incl")
    dec = jnp.where(incl, jnp.exp(jnp.where(incl, gc[:, :CH] - gt, 0.0)), 0.0)
    return rows, be, dec, jnp.exp(gc), jnp.exp(gcl - gc), jnp.exp(gcl)


def _chunk_group(nchunk):
    return 3 if nchunk % 3 == 0 else 1


def _local_group(nchunk):
    return 11 if nchunk % 11 == 0 else _chunk_group(nchunk)


def _gdn_local_specs(lp, grp):
    ngrp = lp // (grp * CH)
    head = lambda col0: pl.BlockSpec((grp * CH, DH), lambda i, j: ((i // HEADS) * ngrp + j, col0 // DH + i % HEADS))
    lane_b = pl.BlockSpec((1, grp * CH, DH), lambda i, j: (i, j, 0))
    per_chunk = lambda r, c: pl.BlockSpec((1, grp, r, c), lambda i, j: (i, j, 0, 0))
    return ngrp, head, lane_b, per_chunk


def _gdn_local_fwd(qkv, gcb, bb, gct, bsz, lp):
    nchunk = lp // CH
    grp = _local_group(nchunk)
    ngrp, head, lane_b, per_chunk = _gdn_local_specs(lp, grp)
    bh = bsz * HEADS

    def body(q_ref, k_ref, v_ref, gc_ref, be_ref, gt_ref,
             u_ref, l1_ref, l2_ref, l3_ref, wt_ref, kd_ref, ti_ref, gl_ref):
        incl, strict = _tri(CH, "incl"), _tri(CH, "strict")
        rows, be, dec, eg, ekd, gl = zip(*[_chunk_gates(gc_ref, be_ref, gt_ref, g) for g in range(grp)])
        q = [q_ref[r, :] for r in rows]
        k = [k_ref[r, :] for r in rows]
        v = [v_ref[r, :] for r in rows]
        kb = _each(lambda x, y: x * y, k, be)
        a = _each(lambda x, y, d: jnp.where(strict, _bdot(x, y, NT) * d, 0.0), kb, k, dec)
        tinv = _inv_unit_lower(a)
        u = _each(lambda t, x, y: _bdot(t, x * y, NN), tinv, v, be)
        w = _each(lambda t, x, y: _bdot(t, x * y, NN), tinv, kb, eg)
        qk = _each(lambda x, y, d: jnp.where(incl, _bdot(x, y, NT) * d, 0.0), q, k, dec)
        qk_t = _each(lambda x: x.T, qk)
        qd = _each(lambda x, y: x * y, q, eg)
        kd = _each(lambda x, y: x * y, k, ekd)
        kd_t = _each(lambda x: x.T, kd)
        qd_t = _each(lambda x: x.T, qd)
        w_t = _each(lambda x: x.T, w)
        for g in range(grp):
            u_ref[rows[g], :] = u[g]
            l1_ref[0, g] = jnp.concatenate([w[g], qd[g]], axis=0).astype(BF16)
            l2_ref[0, g] = jnp.concatenate([qk[g], kd_t[g]], axis=0).astype(BF16)
            l3_ref[0, g] = jnp.concatenate([qk_t[g], qd_t[g]], axis=0).astype(BF16)
            wt_ref[0, g] = w_t[g].astype(BF16)
            kd_ref[0, g] = kd[g].astype(BF16)
            ti_ref[0, g] = tinv[g]
            gl_ref[0, g] = gl[g]

    r = bsz * lp
    shapes = [((r, GW), F32), ((bh, nchunk, 2 * CH, DH), BF16), ((bh, nchunk, 3 * CH, CH), BF16),
              ((bh, nchunk, 3 * CH, CH), BF16), ((bh, nchunk, DH, CH), BF16), ((bh, nchunk, CH, DH), BF16),
              ((bh, nchunk, CH, CH), F32), ((bh, nchunk, 1, DH), F32)]
    return pl.pallas_call(
        body, name="gdn_local_fwd", grid=(bh, ngrp),
        in_specs=[head(0), head(GW), head(2 * GW), lane_b, lane_b, per_chunk(1, CH)],
        out_specs=[head(0), per_chunk(2 * CH, DH), per_chunk(3 * CH, CH), per_chunk(3 * CH, CH), per_chunk(DH, CH),
                   per_chunk(CH, DH), per_chunk(CH, CH), per_chunk(1, DH)],
        out_shape=[jax.ShapeDtypeStruct(s, dt) for s, dt in shapes],
        compiler_params=_cparams(("parallel", "parallel")),
    )(qkv, qkv, qkv, gcb, bb, gct)


def _scan_specs(bsz, lp, grp, order):
    ngrp = lp // (grp * CH)
    bh = bsz * HEADS
    seq = lambda: pl.BlockSpec((bsz, grp * CH, GW), lambda s: (0, order(s, ngrp), 0))
    per_chunk = lambda r, c: pl.BlockSpec((bh, grp, r, c), lambda s: (0, order(s, ngrp), 0, 0))
    return ngrp, bh, seq, per_chunk


def _gdn_scan_fwd(u, l1, l2, gl, bsz, lp):
    nchunk = lp // CH
    grp = _chunk_group(nchunk)
    ngrp, bh, seq, per_chunk = _scan_specs(bsz, lp, grp, lambda s, n: s)

    def body(u_ref, l1_ref, l2_ref, gl_ref, o_ref, vn_ref, st_ref, s_scr):
        @pl.when(pl.program_id(0) == 0)
        def _():
            s_scr[...] = jnp.zeros_like(s_scr)

        chains = list(range(bh))
        where = [(i // HEADS, slice((i % HEADS) * DH, (i % HEADS + 1) * DH)) for i in chains]

        def dot(a, b):
            return jnp.dot(a, b.astype(BF16), preferred_element_type=F32)

        for c in range(grp):
            rows = slice(c * CH, (c + 1) * CH)
            s = [s_scr[i] for i in chains]
            for i in chains:
                st_ref[i, c] = s[i]
            x = _each(lambda i, si: dot(l1_ref[i, c], si), chains, s)
            vnew = _each(lambda bc, xi: u_ref[bc[0], rows, bc[1]] - xi[:CH], where, x)
            y = _each(lambda i, vi: dot(l2_ref[i, c], vi), chains, vnew)
            for i, (b, cols) in zip(chains, where):
                o_ref[b, rows, cols] = x[i][CH:] + y[i][:CH]
                vn_ref[b, rows, cols] = vnew[i].astype(BF16)
                s_scr[i] = s[i] * gl_ref[i, c] + y[i][CH:]

    return pl.pallas_call(
        body, name="gdn_scan_fwd", grid=(ngrp,),
        in_specs=[seq(), per_chunk(2 * CH, DH), per_chunk(3 * CH, CH), per_chunk(1, DH)],
        out_specs=[seq(), seq(), per_chunk(DH, DH)],
        out_shape=[jax.ShapeDtypeStruct((bsz, lp, GW), F32), jax.ShapeDtypeStruct((bsz, lp, GW), BF16),
                   jax.ShapeDtypeStruct((bh, nchunk, DH, DH), F32)],
        scratch_shapes=[pltpu.VMEM((bh, DH, DH), F32)],
        compiler_params=_cparams(("arbitrary",)),
    )(u.reshape(bsz, lp, GW), l1, l2, gl)


def _gdn_scan_bwd(d_o, l3, wt, kd, gl, bsz, lp):
    nchunk = lp // CH
    grp = _chunk_group(nchunk)
    ngrp, bh, seq, per_chunk = _scan_specs(bsz, lp, grp, lambda s, n: n - 1 - s)

    def body(do_ref, l3_ref, wt_ref, kd_ref, gl_ref, dvn_ref, dst_ref, ds_scr):
        @pl.when(pl.program_id(0) == 0)
        def _():
            ds_scr[...] = jnp.zeros_like(ds_scr)

        chains = list(range(bh))
        where = [(i // HEADS, slice((i % HEADS) * DH, (i % HEADS + 1) * DH)) for i in chains]

        def dot(a, b):
            return jnp.dot(a, b.astype(BF16), preferred_element_type=F32)

        for c in reversed(range(grp)):
            rows = slice(c * CH, (c + 1) * CH)
            ds = [ds_scr[i] for i in chains]
            for i in chains:
                dst_ref[i, c] = ds[i]
            p = _each(lambda i, bc: dot(l3_ref[i, c], do_ref[bc[0], rows, bc[1]]), chains, where)
            dvn = _each(lambda i, pi, di: pi[:CH] + dot(kd_ref[i, c], di), chains, p, ds)
            wd = _each(lambda i, vi: dot(wt_ref[i, c], vi), chains, dvn)
            for i, (b, cols) in zip(chains, where):
                dvn_ref[b, rows, cols] = dvn[i].astype(BF16)
                ds_scr[i] = ds[i] * gl_ref[i, c] + p[i][CH:] - wd[i]

    return pl.pallas_call(
        body, name="gdn_scan_bwd", grid=(ngrp,),
        in_specs=[seq(), per_chunk(3 * CH, CH), per_chunk(DH, CH), per_chunk(CH, DH), per_chunk(1, DH)],
        out_specs=[seq(), per_chunk(DH, DH)],
        out_shape=[jax.ShapeDtypeStruct((bsz, lp, GW), BF16), jax.ShapeDtypeStruct((bh, nchunk, DH, DH), F32)],
        scratch_shapes=[pltpu.VMEM((bh, DH, DH), F32)],
        compiler_params=_cparams(("arbitrary",)),
    )(d_o.reshape(bsz, lp, GW), l3, wt, kd, gl)


def _gdn_local_bwd(qkv, gcb, bb, gct, tinv_all, states, dstates, vnew, dvnew, d_o, bsz, lp):
    nchunk = lp // CH
    grp = _local_group(nchunk)
    ngrp, head, lane_b, per_chunk = _gdn_local_specs(lp, grp)
    bh = bsz * HEADS

    def body(q_ref, k_ref, v_ref, gc_ref, be_ref, gt_ref, ti_ref, st_ref, dst_ref, vn_ref, dvn_ref, do_ref,
             dq_ref, dk_ref, dv_ref, dg_ref, db_ref):
        incl, strict = _tri(CH, "incl"), _tri(CH, "strict")
        upper = _tri(CH, "upper").astype(F32)
        ones = jnp.ones((CH, DH), F32)
        last = lax.broadcasted_iota(jnp.int32, (CH, 1), 0) == CH - 1

        def rsum(x):
            return jnp.sum(x, axis=-1, keepdims=True)

        def mul(xs, ys):
            return _each(lambda x, y: x * y, xs, ys)

        rows, be, dec, eg, ekd, gl = zip(*[_chunk_gates(gc_ref, be_ref, gt_ref, g) for g in range(grp)])
        q = [q_ref[r, :] for r in rows]
        k = [k_ref[r, :] for r in rows]
        v = [v_ref[r, :] for r in rows]
        vnew = [vn_ref[r, :] for r in rows]
        dvn = [dvn_ref[r, :] for r in rows]
        do = [do_ref[r, :] for r in rows]
        tinv = [ti_ref[0, g] for g in range(grp)]
        s = [st_ref[0, g] for g in range(grp)]
        ds = [dst_ref[0, g] for g in range(grp)]
        kb, vb, qd, kd = mul(k, be), mul(v, be), mul(q, eg), mul(k, ekd)
        kbg = mul(kb, eg)
        a = _each(lambda x, y, d: jnp.where(strict, _bdot(x, y, NT) * d, 0.0), kb, k, dec)
        qk = _each(lambda x, y, d: jnp.where(incl, _bdot(x, y, NT) * d, 0.0), q, k, dec)
        dqk = _each(lambda x, y: jnp.where(incl, _bdot(x, y, NT), 0.0), do, vnew)
        dqd = _each(lambda x, y: _bdot(x, y, NT), do, s)
        dkd = _each(lambda x, y: _bdot(x, y, NT), vnew, ds)
        dgl = _each(lambda x, y: jnp.sum(rsum(x * y), axis=0, keepdims=True), s, ds)
        dw = _each(lambda x, y: -_bdot(x, y, NT), dvn, s)
        dvb = _each(lambda t, x: _bdot(t, x, TN), tinv, dvn)
        dkbg = _each(lambda t, x: _bdot(t, x, TN), tinv, dw)
        dt = _each(lambda x, y, z, w: _bdot(x, y, NT) + _bdot(z, w, NT), dvn, vb, dw, kbg)
        tdt = _each(lambda t, x: _split_dot(t, x, TN), tinv, dt)
        da = _each(lambda x, t: jnp.where(strict, -_split_dot(x, t, NT), 0.0), tdt, tinv)
        dp, dr = mul(da, dec), mul(dqk, dec)
        dkb = _each(lambda x, y, z, e: _bdot(x, y, NN) + z * e, dp, k, dkbg, eg)
        dk = _each(lambda p_, kb_, r_, q_, dkd_, ekd_, dkb_, be_:
                   _bdot(p_, kb_, TN) + _bdot(r_, q_, TN) + dkd_ * ekd_ + dkb_ * be_,
                   dp, kb, dr, q, dkd, ekd, dkb, be)
        dq = _each(lambda r_, k_, dqd_, eg_: _bdot(r_, k_, NN) + dqd_ * eg_, dr, k, dqd, eg)
        m = _each(lambda da_, a_, dqk_, qk_: da_ * a_ + dqk_ * qk_, da, a, dqk, qk)
        m_cols = _each(lambda x: _split_dot(x, ones, TN), m)
        kd_term = _each(lambda x, y: rsum(x * y), dkd, kd)
        dgc = _each(lambda m_, mc, dqd_, qd_, kt, dkbg_, kbg_, dgl_, gl_:
                    rsum(m_) - mc + rsum(dqd_ * qd_) - kt + rsum(dkbg_ * kbg_)
                    + jnp.where(last, jnp.sum(kt, axis=0, keepdims=True) + dgl_ * gl_, 0.0),
                    m, m_cols, dqd, qd, kd_term, dkbg, kbg, dgl, gl)
        dg = _each(lambda x: _split_dot(upper, x, NN), dgc)
        for g in range(grp):
            dq_ref[rows[g], :] = dq[g]
            dk_ref[rows[g], :] = dk[g]
            dv_ref[rows[g], :] = dvb[g] * be[g]
            dg_ref[0, rows[g], :] = dg[g]
            db_ref[0, rows[g], :] = jnp.broadcast_to(rsum(dkb[g] * k[g]) + rsum(dvb[g] * v[g]), (CH, DH))

    r = bsz * lp
    return pl.pallas_call(
        body, name="gdn_local_bwd", grid=(bh, ngrp),
        in_specs=[head(0), head(GW), head(2 * GW), lane_b, lane_b, per_chunk(1, CH), per_chunk(CH, CH),
                  per_chunk(DH, DH), per_chunk(DH, DH), head(0), head(0), head(0)],
        out_specs=[head(0), head(0), head(0), lane_b, lane_b],
        out_shape=[jax.ShapeDtypeStruct((r, GW), F32)] * 3 + [jax.ShapeDtypeStruct((bh, lp, DH), F32)] * 2,
        compiler_params=_cparams(("parallel", "parallel")),
    )(qkv, qkv, qkv, gcb, bb, gct, tinv_all, states, dstates, vnew, dvnew, d_o)


def _place():
    return lax.axis_index("x"), lax.axis_index("y"), lax.axis_index("c")


def _any_specs(n):
    return [pl.BlockSpec(memory_space=pl.ANY)] * n


def _all_gather(name, shards):
    n = len(shards)

    def body(*refs):
        ins, outs = refs[:n], refs[n:2 * n]
        send_sems, recv_sems, local_sems = refs[2 * n:2 * n + 3]
        stage = refs[2 * n + 3:]
        x, y, c = _place()
        me, sibling = (x, y, c), (x, y, 1 - c)
        chips = [(1 - x, y), (x, 1 - y), (1 - x, 1 - y)]

        def slot(i, p):
            return outs[i].at[4 * p[0] + 2 * p[1] + p[2]]

        def copy(i, k, block, to, src=None):
            return pltpu.make_async_remote_copy(
                src_ref=slot(i, block) if src is None else src, dst_ref=slot(i, block),
                send_sem=send_sems.at[i, k], recv_sem=recv_sems.at[i, k], device_id=to, device_id_type=MESH)

        load = [pltpu.make_async_copy(ins[i], stage[i], local_sems.at[i]) for i in range(n)]
        for cp in load:
            cp.start()
        first = []
        for i in range(n):
            first.append(copy(i, 0, me, sibling, src=ins[i]))
            first += [copy(i, 1 + j, me, (*chip, c), src=ins[i]) for j, chip in enumerate(chips)]
        for cp in first:
            cp.start()
        for cp in load:
            cp.wait()
        mine = [pltpu.make_async_copy(stage[i], slot(i, me), local_sems.at[i]) for i in range(n)]
        for cp in mine:
            cp.start()
        passed = []
        for j, chip in enumerate(chips):
            for i in range(n):
                copy(i, 1 + j, (*chip, c), me).wait_recv()
                fwd = copy(i, 4 + j, (*chip, c), sibling)
                fwd.start()
                passed.append(fwd)
        for i in range(n):
            copy(i, 0, sibling, me).wait_recv()
            for j, chip in enumerate(chips):
                copy(i, 4 + j, (*chip, 1 - c), me).wait_recv()
        for cp in first + passed:
            cp.wait_send()
        for cp in mine:
            cp.wait()

    return pl.pallas_call(
        body, name=name,
        in_specs=_any_specs(n), out_specs=_any_specs(n),
        out_shape=[jax.ShapeDtypeStruct((N_DEV,) + s.shape, s.dtype) for s in shards],
        scratch_shapes=[pltpu.SemaphoreType.DMA((n, 7)), pltpu.SemaphoreType.DMA((n, 7)),
                        pltpu.SemaphoreType.DMA((n,))] + [pltpu.VMEM(s.shape, s.dtype) for s in shards],
    )(*shards)


HBM_SPEC = pl.BlockSpec(memory_space=pltpu.HBM)
SEM_SPEC = pl.BlockSpec(memory_space=pltpu.SEMAPHORE)
EFFECT = pltpu.SideEffectType.DATAFLOW_SIDE_EFFECTING
N_PEER = N_DEV - 1


def _peer(r):
    x, y, c = _place()
    return ((1 - x) if r & 4 else x, (1 - y) if r & 2 else y, (1 - c) if r & 1 else c)


def _slot_of(p):
    return 4 * p[0] + 2 * p[1] + p[2]


def _hbm(a):
    return pltpu.with_memory_space_constraint(a, pltpu.HBM)


def _gather_pattern(src, land):
    me = _place()
    return [(src, land.at[_slot_of(me)], _peer(r), land.at[_slot_of(_peer(r))]) for r in range(1, N_DEV)]


def _alltoall_pattern(src, land):
    me = _place()
    return [(src.at[_slot_of(_peer(r))], land.at[_slot_of(me)], _peer(r), land.at[_slot_of(_peer(r))])
            for r in range(1, N_DEV)]


def _chipcast_pattern(src, land):
    x, y, c = _place()
    me = (x, y, c)
    peers = [(x, y, 1 - c), (1 - x, y, c), (x, 1 - y, c), (1 - x, 1 - y, c)]
    return [(src, land.at[_slot_of(me)], p, land.at[_slot_of(p)]) for p in peers]


def _forward_pattern(land, _):
    x, y, c = _place()
    chips = [(1 - x, y), (x, 1 - y), (1 - x, 1 - y)]
    return [(land.at[_slot_of((*ch, c))], land.at[_slot_of((*ch, c))], (x, y, 1 - c), land.at[_slot_of((*ch, 1 - c))])
            for ch in chips]


def _sibling_pattern(src, land):
    x, y, c = _place()
    return [(src.at[ch, 1 - c], land.at[ch], (x, y, 1 - c), land.at[ch]) for ch in range(4)]


def _chip_pattern(src, land):
    x, y, c = _place()
    chips = [(1 - x, y), (x, 1 - y), (1 - x, 1 - y)]
    return [(src.at[2 * ch[0] + ch[1]], land.at[j], (*ch, c), land.at[j]) for j, ch in enumerate(chips)]


def _split_start(name, groups, pattern, land_shape, after=None):
    sizes = [len(g) for g in groups]
    arrays = [s for g in groups for s in g]
    n = len(arrays)
    inplace = land_shape is None
    fresh = [] if inplace else [lax.empty(land_shape(s.shape), s.dtype) for s in arrays]
    bufs = arrays + fresh
    order = [] if after is None else [after]

    def body(*refs):
        srcs, lands = refs[:n], (refs[:n] if inplace else refs[n:2 * n])
        first_out = len(bufs) + len(order)
        sems = refs[first_out:first_out + 2 * len(groups)]
        token = refs[-1]
        k = 0
        for gi, size in enumerate(sizes):
            for i in range(size):
                copies = pattern(srcs[k], lands[k])
                for j, (s, d, peer, _) in enumerate(copies):
                    pltpu.make_async_remote_copy(
                        src_ref=s, dst_ref=d, send_sem=sems[2 * gi].at[i * len(copies) + j],
                        recv_sem=sems[2 * gi + 1].at[i * len(copies) + j], device_id=peer, device_id_type=MESH).start()
                k += 1
        token[...] = jnp.zeros_like(token)

    per_array = {_gather_pattern: N_PEER, _alltoall_pattern: N_PEER, _chipcast_pattern: 4, _forward_pattern: 3,
                 _sibling_pattern: 4, _chip_pattern: 3}[pattern]
    sem_shapes = [pltpu.SemaphoreType.DMA((size * per_array,)) for size in sizes for _ in range(2)]
    res = pl.pallas_call(
        body, name=name,
        in_specs=[HBM_SPEC] * len(bufs) + [pl.BlockSpec(memory_space=pl.ANY)] * len(order),
        out_specs=[SEM_SPEC] * len(sem_shapes) + [HBM_SPEC] * len(bufs) + [pl.BlockSpec(memory_space=pltpu.VMEM)],
        out_shape=sem_shapes + [pltpu.HBM(s.shape, s.dtype) for s in bufs] + [jax.ShapeDtypeStruct((8, DH), F32)],
        input_output_aliases={i: len(sem_shapes) + i for i in range(len(bufs))},
        compiler_params=pltpu.CompilerParams(has_side_effects=EFFECT),
    )(*[_hbm(s) for s in bufs], *order)
    nsem = len(sem_shapes)
    thru = list(res[nsem:nsem + len(bufs)])
    out, k = [], 0
    for gi, size in enumerate(sizes):
        srcs_out = [] if inplace else thru[k:k + size]
        lands_out = thru[k:k + size] if inplace else thru[n + k:n + k + size]
        out.append((res[2 * gi], res[2 * gi + 1], srcs_out, lands_out))
        k += size
    return out, res[-1]


def _split_wait(name, group, pattern, after):
    send_sems, recv_sems, arrays, lands = group
    n = len(lands)
    inplace = not arrays
    bufs = list(arrays) + list(lands)
    fill_own = pattern in (_gather_pattern, _chipcast_pattern, _alltoall_pattern)
    own_of = (lambda a: a.at[_slot_of(_place())]) if pattern is _alltoall_pattern else (lambda a: a)
    own_shape = (lambda a: a.shape[1:]) if pattern is _alltoall_pattern else (lambda a: a.shape)

    def body(*refs):
        srcs, lands_ = refs[:n], (refs[:n] if inplace else refs[n:2 * n])
        s_sems, r_sems = refs[len(bufs)], refs[len(bufs) + 1]
        own = []
        if fill_own:
            stage, local_sems = refs[-n - 1:-1], refs[-1]
            load = [pltpu.make_async_copy(own_of(srcs[i]), stage[i], local_sems.at[i]) for i in range(n)]
            for cp in load:
                cp.start()
            for cp in load:
                cp.wait()
            own = [pltpu.make_async_copy(stage[i], lands_[i].at[_slot_of(_place())], local_sems.at[i])
                   for i in range(n)]
            for cp in own:
                cp.start()
        for i in range(n):
            copies = pattern(srcs[i], lands_[i])
            for j, (s, _, peer, filled) in enumerate(copies):
                cp = pltpu.make_async_remote_copy(
                    src_ref=s, dst_ref=filled, send_sem=s_sems.at[i * len(copies) + j],
                    recv_sem=r_sems.at[i * len(copies) + j], device_id=peer, device_id_type=MESH)
                cp.wait_send()
                cp.wait_recv()
        for cp in own:
            cp.wait()

    res = pl.pallas_call(
        body, name=name,
        in_specs=[HBM_SPEC] * len(bufs) + [SEM_SPEC, SEM_SPEC, pl.BlockSpec(memory_space=pl.ANY)],
        out_specs=[HBM_SPEC] * len(bufs),
        out_shape=[pltpu.HBM(s.shape, s.dtype) for s in bufs],
        scratch_shapes=([pltpu.VMEM(own_shape(a), a.dtype) for a in arrays] + [pltpu.SemaphoreType.DMA((n,))])
        if fill_own else [],
        input_output_aliases={i: i for i in range(len(bufs))},
        compiler_params=pltpu.CompilerParams(has_side_effects=EFFECT),
    )(*bufs, send_sems, recv_sems, after)
    return ([], list(res)) if inplace else (list(res[:n]), list(res[n:]))


def _pair_sum(name, grads, recv, core):
    _, _, r, c = grads.shape
    tr = r if r * c <= (1 << 20) or r % 16 else _div_tile(r, 512, 16)

    def body(core_ref, g_ref, r_ref, o_ref):
        o_ref[...] = (g_ref[0].astype(F32) + r_ref[...].astype(F32)).astype(o_ref.dtype)

    return pl.pallas_call(
        body, name=name,
        grid_spec=pltpu.PrefetchScalarGridSpec(
            num_scalar_prefetch=1, grid=(4, r // tr),
            in_specs=[pl.BlockSpec((1, 1, tr, c), lambda s, i, cr: (s, cr[0], i, 0)),
                      pl.BlockSpec((1, tr, c), lambda s, i, cr: (s, i, 0))],
            out_specs=pl.BlockSpec((1, tr, c), lambda s, i, cr: (s, i, 0))),
        out_shape=jax.ShapeDtypeStruct(recv.shape, recv.dtype),
        compiler_params=_cparams(("parallel", "parallel")),
    )(core, grads, recv)


def _adamw_math(w, g, m, v):
    m = ADAM_B1 * m + (1.0 - ADAM_B1) * g
    v = ADAM_B2 * v + (1.0 - ADAM_B2) * (g * g)
    m_hat = m / (1.0 - ADAM_B1 ** ADAM_STEP)
    v_hat = v / (1.0 - ADAM_B2 ** ADAM_STEP)
    delta = -ADAM_LR * (m_hat / (jnp.sqrt(v_hat) + ADAM_EPS) + ADAM_WD * w)
    return delta, m, v


SMALL_LEAVES = (
    ("mix_pre_norm", 0, 1, 0, D, (1, D)), ("mix_post_norm", 1, 1, 0, D, (1, D)),
    ("ffn_pre_norm", 2, 1, 0, D, (1, D)), ("ffn_post_norm", 3, 1, 0, D, (1, D)),
    ("meta_tokens", 8, N_META, 0, DH, (N_META, DH)),
    ("conv_qkv", 24, KQ, 0, 3 * GW // N_DEV, (1, KQ, 3 * GW // N_DEV)),
    ("conv_sc", 32, KS, 0, SW // N_DEV, (1, KS, SW // N_DEV)),
    ("a_log", 40, 1, 0, HEADS, (1, HEADS)), ("dt_bias", 40, 1, DH, HEADS, (1, HEADS)),
    ("gdn_norm", 40, 1, 2 * DH, DH, (1, DH)),
)
SMALL_LOSS_AT = (40, 3 * DH)


def _adamw_small(slabs, late, w, m, v):
    nslot = slabs.shape[0]
    nleaf = len(SMALL_LEAVES)

    def body(p_ref, l_ref, w_ref, m_ref, v_ref, *refs):
        outs, loss_ref, scr = refs[:4 * nleaf], refs[4 * nleaf], refs[4 * nleaf + 1]
        g = p_ref[0]
        gain = l_ref[0, pl.ds(0, 1), :]
        meta = l_ref[0, pl.ds(8, N_META), pl.ds(0, DH)]
        for s in range(1, nslot):
            g = g + p_ref[s]
            gain = gain + l_ref[s, pl.ds(0, 1), :]
            meta = meta + l_ref[s, pl.ds(8, N_META), pl.ds(0, DH)]
        scr[0] = g
        scr[0, pl.ds(0, 1), pl.ds(0, D)] += gain
        scr[0, pl.ds(8, N_META), pl.ds(0, DH)] += meta
        g = scr[0]
        d, mn, vn = _adamw_math(w_ref[...], g, m_ref[...], v_ref[...])
        for kind, val in enumerate((g, d, mn, vn)):
            scr[kind] = val
        for kind in range(4):
            for li, (_, r0, nr, c0, nc, shape) in enumerate(SMALL_LEAVES):
                val = scr[kind, pl.ds(r0, nr), pl.ds(c0, nc)]
                o = outs[kind * nleaf + li]
                if len(shape) == 3:
                    o[0] = val
                else:
                    o[...] = val
        loss_ref[...] = scr[0, pl.ds(SMALL_LOSS_AT[0], 1), pl.ds(SMALL_LOSS_AT[1], 1)]

    whole = lambda a: pl.BlockSpec(a.shape, lambda: (0,) * a.ndim)
    out_shapes = [jax.ShapeDtypeStruct(shape, F32) for _ in range(4) for (*_, shape) in SMALL_LEAVES]
    out_shapes.append(jax.ShapeDtypeStruct((1, 1), F32))
    res = pl.pallas_call(
        body, name="adamw_small",
        in_specs=[whole(slabs), whole(late), whole(w), whole(m), whole(v)],
        out_specs=[pl.BlockSpec(s.shape, lambda n=len(s.shape): (0,) * n) for s in out_shapes],
        out_shape=out_shapes,
        scratch_shapes=[pltpu.VMEM((4,) + w.shape, F32)],
        compiler_params=pltpu.CompilerParams(vmem_limit_bytes=VMEM_LIMIT),
    )(slabs, late, w, m, v)
    kinds = [{nm: res[kind * nleaf + li] for li, (nm, *_) in enumerate(SMALL_LEAVES)} for kind in range(4)]
    return kinds, res[-1]


def _adamw_reduced(name, parts, chip, recv, w, m, v, after=None):
    _, r, c = parts.shape
    tr = r if r * c <= (1 << 20) or r % 16 else _div_tile(r, 512, 16)
    order = [] if after is None else [after]

    def body(chip_ref, p_ref, r_ref, w_ref, m_ref, v_ref, *rest):
        g_out, d_out, m_out, v_out = rest[len(order):]
        g = p_ref[0].astype(F32)
        for s in range(3):
            g = g + r_ref[s].astype(F32)
        d, mn, vn = _adamw_math(w_ref[...], g, m_ref[...], v_ref[...])
        g_out[...] = g
        d_out[...] = d
        m_out[...] = mn
        v_out[...] = vn

    blk = pl.BlockSpec((tr, c), lambda i, ch: (i, 0))
    return pl.pallas_call(
        body, name=name,
        grid_spec=pltpu.PrefetchScalarGridSpec(
            num_scalar_prefetch=1, grid=(r // tr,),
            in_specs=[pl.BlockSpec((1, tr, c), lambda i, ch: (ch[0], i, 0)),
                      pl.BlockSpec((3, tr, c), lambda i, ch: (0, i, 0)), blk, blk, blk]
            + [pl.BlockSpec((8, DH), lambda i, ch: (0, 0))] * len(order),
            out_specs=[blk] * 4),
        out_shape=[jax.ShapeDtypeStruct((r, c), F32)] * 4,
        compiler_params=_cparams(("parallel",)),
    )(chip, parts, recv, w, m, v, *order)


SMALL_ROWS, SMALL_COLS = 48, 3 * GW


def _pack_small(gains4, meta, conv_qkv, conv_sc, a_log, dt_bias, gdn_norm, extra=None):
    def tile(a):
        return jnp.pad(a, ((0, -a.shape[0] % 8), (0, SMALL_COLS - a.shape[1])))
    scal = jnp.concatenate([jnp.pad(a_log, ((0, 0), (0, DH - HEADS))), jnp.pad(dt_bias, ((0, 0), (0, DH - HEADS))),
                            gdn_norm] + ([] if extra is None else [extra]), axis=1)
    gains = sum(jnp.pad(g, ((i, 7 - i), (0, SMALL_COLS - g.shape[1]))) for i, g in enumerate(gains4))
    parts = [gains, tile(meta), tile(conv_qkv), tile(conv_sc), tile(scal)]
    slab = jnp.concatenate(parts, axis=0)
    assert slab.shape == (SMALL_ROWS, SMALL_COLS)
    return slab


def kernel(x, meta_tokens, mix_pre_norm, mix_post_norm, ffn_pre_norm, ffn_post_norm, w_in, conv_qkv, a_log, dt_bias, gdn_norm, conv_sc, w_out, w_gate, w_up, w_down, loss_target, m_meta_tokens, m_mix_pre_norm, m_mix_post_norm, m_ffn_pre_norm, m_ffn_post_norm, m_w_in, m_conv_qkv, m_a_log, m_dt_bias, m_gdn_norm, m_conv_sc, m_w_out, m_w_gate, m_w_up, m_w_down, v_meta_tokens, v_mix_pre_norm, v_mix_post_norm, v_ffn_pre_norm, v_ffn_post_norm, v_w_in, v_conv_qkv, v_a_log, v_dt_bias, v_gdn_norm, v_conv_sc, v_w_out, v_w_gate, v_w_up, v_w_down):
    bsz, seq, _ = x.shape
    lp = -(-(N_META + seq) // CH) * CH
    padf = lp - N_META - seq
    first_real = padf + N_META
    rows = bsz * lp
    nchunk = lp // CH
    tr = _div_tile(lp, 528)
    tm = _div_tile(rows, 1056)
    tm_big = _div_tile(rows, 2112)
    cx, cy, cc = _place()

    small_shard = jnp.concatenate([
        jnp.pad(meta_tokens, ((0, 0), (0, 256 - DH))),
        jnp.pad(conv_qkv[0], ((0, 0), (0, 256 - 3 * GW // N_DEV))),
        jnp.pad(conv_sc[0], ((0, 5), (0, 256 - SW // N_DEV)))], axis=0)
    g_in, g_small = _all_gather("gather_w_in", [w_in[0].T.astype(BF16), small_shard])
    my_rest = [w_out[0].astype(BF16), w_gate[0].T.astype(BF16), w_up[0].T.astype(BF16), w_down[0].astype(BF16)]
    (grp_rest,), token = _split_start("gather_rest_start", [my_rest], _chipcast_pattern, lambda s: (N_DEV,) + s,
                                      after=g_in)
    after_start = token[0:1, 0:1]

    def cols_full(g):
        return jnp.transpose(g, (1, 0, 2)).reshape(g.shape[1], -1)

    win_t = g_in.reshape(IN_W, D)
    win_t = jnp.concatenate([win_t[:BA0 + 2 * HEADS], jnp.zeros((BA_W - 2 * HEADS, D), BF16),
                             win_t[BA0 + 2 * HEADS:]], axis=0)
    meta_full = cols_full(g_small[:, 0:16, :DH])
    convq_full = cols_full(g_small[:, 16:20, :3 * GW // N_DEV])
    convs_full = cols_full(g_small[:, 20:23, :SW // N_DEV])
    alog_v = jnp.pad(a_log, ((0, 0), (HEADS, DH - 2 * HEADS)))
    dtb_v = jnp.pad(dt_bias, ((0, 0), (HEADS, DH - 2 * HEADS)))

    head_rows = jnp.concatenate([jnp.zeros((padf, D), F32), meta_full], axis=0)
    h0, u1 = _pre_norm(x, head_rows, mix_pre_norm + after_start, bsz, lp, tr)
    (proj,) = _mm("proj_in", [(u1, win_t)], "nt", tm_big, 768, D, out_dtypes=(BF16,))
    (ba,) = _mm("proj_gate_logits", [(u1, win_t[BA0:SX0])], "nt", tm, BA_W, D)
    qkv = _gdn_prep(proj, convq_full, bsz, lp, padf)
    gb, gcs = _gates(ba, alog_v, dtb_v, bsz, lp, padf)

    def lane_bcast(cols):
        t = jnp.transpose(cols.reshape(bsz, lp, HEADS), (0, 2, 1)).reshape(bsz * HEADS, lp, 1)
        return jnp.broadcast_to(t, (bsz * HEADS, lp, DH))

    bb = lane_bcast(gb[:, 0:HEADS])
    gcb = lane_bcast(gcs[:, HEADS:2 * HEADS])
    gct = jnp.transpose(gcs[:, HEADS:2 * HEADS].reshape(bsz, nchunk, CH, HEADS), (0, 3, 1, 2)).reshape(
        bsz * HEADS, nchunk, 1, CH)
    u_loc, l1, l2, l3, w_t, k_dec, tinv_all, g_last = _gdn_local_fwd(qkv, gcb, bb, gct, bsz, lp)
    o, v_new, states = _gdn_scan_fwd(u_loc, l1, l2, g_last, bsz, lp)
    o, v_new = o.reshape(rows, GW), v_new.reshape(rows, GW)
    _, l_rest = _split_wait("gather_rest_arrive", grp_rest, _chipcast_pattern, o)
    (grp_out, grp_ffn, grp_down), token = _split_start(
        "gather_rest_forward", [l_rest[0:1], l_rest[1:3], l_rest[3:4]], _forward_pattern, None)
    mix_g, mix_s = _mix_heads(proj, o, gdn_norm + token[0:1, :], convs_full, bsz, lp, padf)
    mixin = jnp.concatenate([mix_g, mix_s], axis=1)
    _, (l_out,) = _split_wait("gather_w_out_wait", grp_out, _forward_pattern, mixin)
    wout = l_out.reshape(D, D)
    (mix,) = _mm("proj_out", [(mixin, wout)], "nn", tm, D, D, out_dtypes=(BF16,))
    h1, u2 = _mid_norms(h0, mix, mix_post_norm, ffn_pre_norm, bsz, lp, tr)
    _, (l_gate, l_up) = _split_wait("gather_ffn_wait", grp_ffn, _forward_pattern, u2)
    wgate_t, wup_t = l_gate.reshape(FF, D), l_up.reshape(FF, D)
    gate, up, act = _ffn_up(u2, wgate_t, wup_t, _div_tile(rows, 528), 1408)
    _, (l_down,) = _split_wait("gather_down_wait", grp_down, _forward_pattern, act)
    wdown = l_down.reshape(FF, D)
    (ffn,) = _mm("ffn_down", [(act, wdown)], "nn", tm, D, 1408, out_dtypes=(BF16,))

    dy, dffn, d_g4, loss_part = _loss_head(h1, ffn, loss_target, ffn_post_norm, bsz, lp, tr, first_real)
    (dwdown,) = _mm("dw_down", [(act, dffn)], "tn", 1408, D, tm, out_dtypes=(BF16,))

    def swiglu_bwd(da, g, u):
        g, u = g.astype(F32), u.astype(F32)
        return da * u * _dsilu(g), da * _silu(g)

    dgate, dup = _mm("d_act", [(dffn, wdown)], "nt", tm, 1408, D, out_dtypes=(BF16, BF16),
                     epilogue=swiglu_bwd, extras=(gate, up))
    (dwgate_t,) = _mm("dw_gate", [(dgate, u2)], "tn", 1408, D, tm, out_dtypes=(BF16,))
    (dwup_t,) = _mm("dw_up", [(dup, u2)], "tn", 1408, D, tm, out_dtypes=(BF16,))
    (du2,) = _mm("d_u2", [(dgate, wgate_t), (dup, wup_t)], "nn", tm, D, 1408, out_dtypes=(BF16,))
    dh1, dmix, d_g3, d_g2 = _mid_norms_bwd(h1, mix, du2, dy, ffn_pre_norm, mix_post_norm, bsz, lp, tr)
    (dwout,) = _mm("dw_out", [(mixin, dmix)], "tn", D, D, tm, out_dtypes=(BF16,))

    def row_blocks(g):
        return g.reshape(4, 2, -1, g.shape[1])

    core = jnp.reshape(cc, (1,)).astype(jnp.int32)
    chip = jnp.reshape(2 * cx + cy, (1,)).astype(jnp.int32)

    def reduce_start(tag, grads):
        (grp,), tok = _split_start("rs_sibling_start_" + tag, [grads], _sibling_pattern, lambda s: s[:1] + s[2:])
        return grp, tok

    def reduce_mid(tag, grp, nms, after):
        mine, lands = _split_wait("rs_sibling_wait_" + tag, grp, _sibling_pattern, after)
        parts = [_pair_sum("pair_sum_" + nm, g, r, core) for nm, g, r in zip(nms, mine, lands)]
        (grp2,), tok = _split_start("rs_chip_start_" + tag, [parts], _chip_pattern, lambda s: (3,) + s[1:])
        return grp2, tok

    def reduce_end(tag, grp2, after):
        return _split_wait("rs_chip_wait_" + tag, grp2, _chip_pattern, after)

    names_ffn = ["w_down", "w_gate", "w_up", "w_out"]
    rs_ffn, tok = reduce_start("ffn", [row_blocks(dwdown), row_blocks(dwgate_t), row_blocks(dwup_t), row_blocks(dwout)])
    (dmixin,) = _mm("d_mixin", [(dmix, wout)], "nt", tm, D, D, out_dtypes=(BF16,), after=tok)
    d_o, dz, dsx, dsb, dsc, d_gn, d_convs = _mix_heads_bwd(proj, o, gdn_norm, convs_full, dmixin, bsz, lp, padf)
    rs_ffn, tok = reduce_mid("ffn", rs_ffn, names_ffn, d_o)
    dv_new, dstates = _gdn_scan_bwd(d_o, l3, w_t, k_dec, g_last + tok[0, 0], bsz, lp)
    dq, dk, dv, dg_b, dbeta_b = _gdn_local_bwd(qkv, gcb, bb, gct, tinv_all, states, dstates, v_new,
                                               dv_new.reshape(rows, GW), d_o, bsz, lp)
    dpqkv, d_convq = _gdn_prep_bwd(proj, convq_full, dq, dk, dv, bsz, lp, padf)
    dba, d_scal = _gates_bwd(ba, alog_v, dtb_v, dbeta_b, dg_b, gb, bsz, lp, padf)
    dproj = jnp.concatenate([dpqkv, dz, dba, dsx, dsb, dsc], axis=1)

    d_alog = d_scal[0:1, HEADS:2 * HEADS]
    d_dtb = d_scal[1:2, HEADS:2 * HEADS]
    cq_w, cs_w = 3 * GW // N_DEV, SW // N_DEV
    slab = jnp.stack([
        _pack_small([jnp.zeros_like(d_g2), d_g2, d_g3, d_g4], jnp.zeros((N_META, DH), F32),
                    d_convq[:, p * cq_w:(p + 1) * cq_w], d_convs[:, p * cs_w:(p + 1) * cs_w], d_alog, d_dtb, d_gn,
                    loss_part) for p in range(N_DEV)])
    (grp_small,), tok = _split_start("small_grads_start", [[slab]], _alltoall_pattern, lambda s: s)
    (dwin_t,) = _mm("dw_in", [(dproj, u1)], "tn", 1280, D, tm, out_dtypes=(BF16,), after=tok)
    parts_ffn, recv_ffn = reduce_end("ffn", rs_ffn, dwin_t)
    dwin_t = jnp.concatenate([dwin_t[:BA0 + 2 * HEADS], dwin_t[SX0:]], axis=0)
    rs_in, tok = reduce_start("in", [row_blocks(dwin_t)])

    big_w = {"w_in": (w_in, m_w_in, v_w_in), "w_out": (w_out, m_w_out, v_w_out), "w_gate": (w_gate, m_w_gate, v_w_gate),
             "w_up": (w_up, m_w_up, v_w_up), "w_down": (w_down, m_w_down, v_w_down)}

    def update(nm, parts, recv, after=None):
        w, m, v = big_w[nm]
        if nm in ("w_out", "w_down"):
            return _adamw_reduced("adamw_" + nm, parts, chip, recv, w[0], m[0], v[0], after)
        res = _adamw_reduced("adamw_" + nm, parts, chip, recv, w[0].T, m[0].T, v[0].T, after)
        return [t.T for t in res]

    big_out = {nm: update(nm, p, r, tok) for nm, p, r in zip(names_ffn, parts_ffn, recv_ffn)}
    ffn_done = sum(big_out[nm][1][:1, :1] for nm in names_ffn)
    rs_in, tok = reduce_mid("in", rs_in, ["w_in"], ffn_done)
    (du1,) = _mm("d_u1", [(dproj, win_t)], "nn", tm_big, D, 768, out_dtypes=(BF16,), after=tok)
    grad_x, d_g1, d_meta = _pre_norm_bwd(h0, du1, dh1, mix_pre_norm, bsz, lp, seq, padf)

    late = jnp.stack([
        jnp.concatenate([jnp.pad(d_g1, ((0, 7), (0, 0))), jnp.pad(d_meta[:, p * DH:(p + 1) * DH], ((0, 0), (0, D - DH)))],
                        axis=0) for p in range(N_DEV)])
    (grp_late,), _ = _split_start("late_grads_start", [[late]], _alltoall_pattern, lambda s: s)
    _, (slabs_local,) = _split_wait("small_grads_wait", grp_small, _alltoall_pattern, grad_x)
    _, (late_local,) = _split_wait("late_grads_wait", grp_late, _alltoall_pattern, slabs_local)
    w_small = _pack_small([mix_pre_norm, mix_post_norm, ffn_pre_norm, ffn_post_norm], meta_tokens, conv_qkv[0],
                          conv_sc[0], a_log, dt_bias, gdn_norm)
    m_small = _pack_small([m_mix_pre_norm, m_mix_post_norm, m_ffn_pre_norm, m_ffn_post_norm], m_meta_tokens,
                          m_conv_qkv[0], m_conv_sc[0], m_a_log, m_dt_bias, m_gdn_norm)
    v_small = _pack_small([v_mix_pre_norm, v_mix_post_norm, v_ffn_pre_norm, v_ffn_post_norm], v_meta_tokens,
                          v_conv_qkv[0], v_conv_sc[0], v_a_log, v_dt_bias, v_gdn_norm)
    small_out, loss = _adamw_small(slabs_local, late_local, w_small, m_small, v_small)
    parts_in, recv_in = reduce_end("in", rs_in, small_out[1]["gdn_norm"])
    big_out["w_in"] = update("w_in", parts_in[0], recv_in[0])
    names = ["w_in", "w_out", "w_gate", "w_up", "w_down"]

    order = ["meta_tokens", "mix_pre_norm", "mix_post_norm", "ffn_pre_norm", "ffn_post_norm", "w_in", "conv_qkv",
             "a_log", "dt_bias", "gdn_norm", "conv_sc", "w_out", "w_gate", "w_up", "w_down"]
    outs = [loss[0, 0], grad_x]
    for kind in range(4):
        leaves = dict(small_out[kind])
        for nm in names:
            leaves[nm] = big_out[nm][kind][None]
        outs += [leaves[nm] for nm in order]
    return tuple(outs)
```

```python
import jax
import jax.numpy as jnp
from jax import lax
from jax.experimental import pallas as pl
from jax.experimental.pallas import tpu as pltpu

F32 = jnp.float32
BF16 = jnp.bfloat16
MESH = pl.DeviceIdType.MESH

D = 1024
N_META = 16
HEADS = 4
DH = 128
GW = HEADS * DH
SW = D - GW
FF = 2816
CH = 64
KQ = 4
KS = 3
EPS = 1e-6
IN_W = 3 * GW + GW + 2 * HEADS + 3 * SW
Q0, Z0, BA0, SX0, SB0, SC0 = 0, 1536, 2048, 2304, 2816, 3328
BA_W = SX0 - BA0
N_DEV = 8
VMEM_LIMIT = 56 * 1024 * 1024

ADAM_LR, ADAM_B1, ADAM_B2, ADAM_EPS, ADAM_WD, ADAM_STEP = 0.001, 0.9, 0.999, 1e-08, 0.01, 10


def _cparams(sem):
    return pltpu.CompilerParams(dimension_semantics=sem, vmem_limit_bytes=VMEM_LIMIT)


def _div_tile(n, target, mult=16):
    best = None
    for t in range(mult, min(n, target) + 1, mult):
        if n % t == 0:
            best = t
    assert best is not None, (n, target)
    return best


def _silu(x):
    return x * jax.nn.sigmoid(x)


def _dsilu(x):
    s = jax.nn.sigmoid(x)
    return s * (1.0 + x * (1.0 - s))


def _bdot(a, b, dims):
    return lax.dot_general(a.astype(BF16), b.astype(BF16), (dims, ((), ())), preferred_element_type=F32)


NN = ((1,), (0,))
NT = ((1,), (1,))
TN = ((0,), (0,))


def _mm(name, pairs, mode, tm, tn, tk, out_dtypes=(F32,), epilogue=None, extras=(), after=None):
    a0, b0 = pairs[0]
    if mode == "nn":
        (m, k), n = a0.shape, b0.shape[1]
    elif mode == "nt":
        (m, k), n = a0.shape, b0.shape[0]
    else:
        (k, m), n = a0.shape, b0.shape[1]
    assert m % tm == 0 and n % tn == 0 and k % tk == 0, (name, m, n, k, tm, tn, tk)
    nk = k // tk
    npair = len(pairs)
    nex = len(extras)
    nout = len(out_dtypes)
    ntok = 0 if after is None else 1
    dims = {"nn": NN, "nt": NT, "tn": TN}[mode]

    def body(*refs):
        ab = refs[:2 * npair]
        ex = refs[2 * npair:2 * npair + nex]
        outs = refs[2 * npair + nex + ntok:2 * npair + nex + ntok + nout]
        acc_ref = refs[-1]
        kk = pl.program_id(2)

        part = None
        for p in range(npair):
            d = lax.dot_general(ab[2 * p][...], ab[2 * p + 1][...], (dims, ((), ())), preferred_element_type=F32)
            part = d if part is None else part + d

        def finish(acc):
            res = epilogue(acc, *[e[...] for e in ex]) if epilogue is not None else (acc,)
            for o, r in zip(outs, res):
                o[...] = r.astype(o.dtype)

        if nk == 1:
            finish(part)
        else:
            @pl.when(kk == 0)
            def _():
                acc_ref[...] = part

            @pl.when(kk > 0)
            def _():
                acc_ref[...] += part

            @pl.when(kk == nk - 1)
            def _():
                finish(acc_ref[...])

    if mode == "nn":
        a_spec = pl.BlockSpec((tm, tk), lambda i, j, q: (i, q))
        b_spec = pl.BlockSpec((tk, tn), lambda i, j, q: (q, j))
    elif mode == "nt":
        a_spec = pl.BlockSpec((tm, tk), lambda i, j, q: (i, q))
        b_spec = pl.BlockSpec((tn, tk), lambda i, j, q: (j, q))
    else:
        a_spec = pl.BlockSpec((tk, tm), lambda i, j, q: (q, i))
        b_spec = pl.BlockSpec((tk, tn), lambda i, j, q: (q, j))
    o_spec = pl.BlockSpec((tm, tn), lambda i, j, q: (i, j))
    flat = [t for pr in pairs for t in pr]
    res = pl.pallas_call(
        body, name=name,
        grid=(m // tm, n // tn, nk),
        in_specs=[a_spec, b_spec] * npair + [o_spec] * nex + [pl.BlockSpec((8, DH), lambda i, j, q: (0, 0))] * ntok,
        out_specs=[o_spec] * nout,
        out_shape=[jax.ShapeDtypeStruct((m, n), dt) for dt in out_dtypes],
        scratch_shapes=[pltpu.VMEM((tm, tn) if nk > 1 else (8, DH), F32)],
        compiler_params=_cparams(("parallel", "parallel", "arbitrary")),
    )(*flat, *extras, *([] if after is None else [after]))
    return res


def _ffn_up(u2, wg, wu, tm, tn):
    m, k = u2.shape
    n = wg.shape[0]

    def body(a_ref, g_ref, u_ref, gate_ref, up_ref, act_ref):
        a = a_ref[...]
        g = lax.dot_general(a, g_ref[...], (NT, ((), ())), preferred_element_type=F32)
        u = lax.dot_general(a, u_ref[...], (NT, ((), ())), preferred_element_type=F32)
        gate_ref[...] = g.astype(BF16)
        up_ref[...] = u.astype(BF16)
        act_ref[...] = (_silu(g) * u).astype(BF16)

    o_spec = pl.BlockSpec((tm, tn), lambda j, i: (i, j))
    w_spec = pl.BlockSpec((tn, k), lambda j, i: (j, 0))
    return pl.pallas_call(
        body, name="ffn_up", grid=(n // tn, m // tm),
        in_specs=[pl.BlockSpec((tm, k), lambda j, i: (i, 0)), w_spec, w_spec],
        out_specs=[o_spec, o_spec, o_spec],
        out_shape=[jax.ShapeDtypeStruct((m, n), BF16)] * 3,
        compiler_params=_cparams(("parallel", "parallel")),
    )(u2, wg, wu)


def _rms(x, gain):
    r = lax.rsqrt(jnp.mean(x * x, axis=-1, keepdims=True) + EPS)
    return x * r * gain, r


def _rms_bwd(x, gain, dy):
    r = lax.rsqrt(jnp.mean(x * x, axis=-1, keepdims=True) + EPS)
    dyw = dy * gain
    dx = r * dyw - x * (r * r * r) * jnp.mean(dyw * x, axis=-1, keepdims=True)
    return dx, dy * x * r


def _row_specs(tr, width, nb):
    return pl.BlockSpec((tr, width), lambda b, j: (b * nb + j, 0))


def _vec_spec(width):
    return pl.BlockSpec((1, width), lambda b, j: (0, 0))


def _first_step(b, j):
    return jnp.logical_and(b == 0, j == 0)


def _pre_norm(x, head_rows, gain, bsz, lp, tr):
    nb = lp // tr
    first_real = head_rows.shape[0]
    assert first_real % 8 == 0 and first_real < tr
    x_spec = pl.BlockSpec((pl.Squeezed(), pl.Element(tr), pl.Element(D)),
                          lambda b, j: (b, pl.multiple_of(jnp.maximum(j * tr - first_real, 0), 8), 0))

    def body(x_ref, hd_ref, g_ref, h_ref, u_ref):
        j = pl.program_id(1)
        xt = x_ref[...]
        h = jnp.where(j == 0, jnp.concatenate([hd_ref[...], xt[:tr - first_real]], axis=0), xt)
        h_ref[...] = h
        u_ref[...] = _rms(h, g_ref[...])[0].astype(BF16)

    rs = _row_specs(tr, D, nb)
    return pl.pallas_call(
        body, name="mix_pre_norm", grid=(bsz, nb),
        in_specs=[x_spec, pl.BlockSpec((first_real, D), lambda b, j: (0, 0)), _vec_spec(D)],
        out_specs=[rs, rs],
        out_shape=[jax.ShapeDtypeStruct((bsz * lp, D), F32), jax.ShapeDtypeStruct((bsz * lp, D), BF16)],
        compiler_params=_cparams(("parallel", "parallel")),
    )(x, head_rows, gain)


def _mid_norms(h0, mix, g_post, g_pre, bsz, lp, tr):
    nb = lp // tr

    def body(h_ref, m_ref, gp_ref, gq_ref, h1_ref, u2_ref):
        h1 = h_ref[...] + _rms(m_ref[...].astype(F32), gp_ref[...])[0]
        h1_ref[...] = h1
        u2_ref[...] = _rms(h1, gq_ref[...])[0].astype(BF16)

    rs = _row_specs(tr, D, nb)
    return pl.pallas_call(
        body, name="mid_norms", grid=(bsz, nb),
        in_specs=[rs, rs, _vec_spec(D), _vec_spec(D)],
        out_specs=[rs, rs],
        out_shape=[jax.ShapeDtypeStruct(h0.shape, F32), jax.ShapeDtypeStruct(h0.shape, BF16)],
        compiler_params=_cparams(("parallel", "parallel")),
    )(h0, mix, g_post, g_pre)


def _loss_head(h1, ffn, tgt, g_post, bsz, lp, tr, first_real):
    nb = lp // tr
    assert first_real % 8 == 0 and first_real < tr
    tgt_spec = pl.BlockSpec((pl.Squeezed(), pl.Element(tr), pl.Element(D)),
                            lambda b, j: (b, pl.multiple_of(jnp.maximum(j * tr - first_real, 0), 8), 0))

    def body(h_ref, f_ref, t_ref, g_ref, dy_ref, dffn_ref, dg_ref, loss_ref):
        b, j = pl.program_id(0), pl.program_id(1)
        f = f_ref[...].astype(F32)
        g = g_ref[...]
        y = h_ref[...] + _rms(f, g)[0]
        t = j * tr + lax.broadcasted_iota(jnp.int32, (tr, 1), 0)
        tg = t_ref[...]
        tg = jnp.where(j == 0, jnp.concatenate([tg[tr - first_real:], tg[:tr - first_real]], axis=0), tg)
        err = jnp.where(t >= first_real, y - tg, 0.0)
        dy = err * (1.0 / D)
        dy_ref[...] = dy
        dx, dgc = _rms_bwd(f, g, dy)
        dffn_ref[...] = dx.astype(BF16)
        lsum = jnp.sum(jnp.sum(err * err, axis=0, keepdims=True), axis=1, keepdims=True) * (0.5 / D)

        @pl.when(_first_step(b, j))
        def _():
            dg_ref[...] = jnp.zeros_like(dg_ref)
            loss_ref[...] = jnp.zeros_like(loss_ref)

        dg_ref[...] += jnp.sum(dgc, axis=0, keepdims=True)
        loss_ref[...] += jnp.broadcast_to(lsum, loss_ref.shape)

    rs = _row_specs(tr, D, nb)
    return pl.pallas_call(
        body, name="loss_head", grid=(bsz, nb),
        in_specs=[rs, rs, tgt_spec, _vec_spec(D)],
        out_specs=[rs, rs, _vec_spec(D), _vec_spec(DH)],
        out_shape=[jax.ShapeDtypeStruct(h1.shape, F32), jax.ShapeDtypeStruct(h1.shape, BF16),
                   jax.ShapeDtypeStruct((1, D), F32), jax.ShapeDtypeStruct((1, DH), F32)],
        compiler_params=_cparams(("arbitrary", "arbitrary")),
    )(h1, ffn, tgt, g_post)


def _mid_norms_bwd(h1, mix, du2, dy, g_pre, g_post, bsz, lp, tr):
    nb = lp // tr

    def body(h_ref, m_ref, du_ref, dy_ref, gq_ref, gp_ref, dh1_ref, dmix_ref, dgq_ref, dgp_ref):
        b, j = pl.program_id(0), pl.program_id(1)
        dx, dgq = _rms_bwd(h_ref[...], gq_ref[...], du_ref[...].astype(F32))
        dh1 = dy_ref[...] + dx
        dh1_ref[...] = dh1
        dm, dgp = _rms_bwd(m_ref[...].astype(F32), gp_ref[...], dh1)
        dmix_ref[...] = dm.astype(BF16)

        @pl.when(_first_step(b, j))
        def _():
            dgq_ref[...] = jnp.zeros_like(dgq_ref)
            dgp_ref[...] = jnp.zeros_like(dgp_ref)

        dgq_ref[...] += jnp.sum(dgq, axis=0, keepdims=True)
        dgp_ref[...] += jnp.sum(dgp, axis=0, keepdims=True)

    rs = _row_specs(tr, D, nb)
    return pl.pallas_call(
        body, name="mid_norms_bwd", grid=(bsz, nb),
        in_specs=[rs, rs, rs, rs, _vec_spec(D), _vec_spec(D)],
        out_specs=[rs, rs, _vec_spec(D), _vec_spec(D)],
        out_shape=[jax.ShapeDtypeStruct(h1.shape, F32), jax.ShapeDtypeStruct(h1.shape, BF16),
                   jax.ShapeDtypeStruct((1, D), F32), jax.ShapeDtypeStruct((1, D), F32)],
        compiler_params=_cparams(("arbitrary", "arbitrary")),
    )(h1, mix, du2, dy, g_pre, g_post)


def _pre_norm_bwd(h0, du1, dh1, gain, bsz, lp, seq, padf):
    first_real = lp - seq
    tx = _div_tile(seq, 512)
    assert first_real % 8 == 0 and lp % first_real == 0

    def tokens(h_ref, du_ref, dh1_ref, g_ref, gx_ref, dg_ref):
        b, j = pl.program_id(0), pl.program_id(1)
        dx, dgc = _rms_bwd(h_ref[...], g_ref[...], du_ref[...].astype(F32))
        gx_ref[...] = dh1_ref[...] + dx

        @pl.when(_first_step(b, j))
        def _():
            dg_ref[...] = jnp.zeros_like(dg_ref)

        dg_ref[...] += jnp.sum(dgc, axis=0, keepdims=True)

    window = pl.BlockSpec((pl.Element(tx), pl.Element(D)),
                          lambda b, j: (pl.multiple_of(b * lp + first_real + j * tx, 16), 0))
    assert first_real % 16 == 0 and tx % 16 == 0 and lp % 16 == 0
    grad_x, dg_tokens = pl.pallas_call(
        tokens, name="pre_norm_bwd", grid=(bsz, seq // tx),
        in_specs=[window, window, window, _vec_spec(D)],
        out_specs=[pl.BlockSpec((pl.Squeezed(), tx, D), lambda b, j: (b, j, 0)), _vec_spec(D)],
        out_shape=[jax.ShapeDtypeStruct((bsz, seq, D), F32), jax.ShapeDtypeStruct((1, D), F32)],
        compiler_params=_cparams(("arbitrary", "arbitrary")),
    )(h0, du1, dh1, gain)

    def head(h_ref, du_ref, dh1_ref, g_ref, dgt_ref, dg_ref, dmeta_ref):
        b = pl.program_id(0)
        dx, dgc = _rms_bwd(h_ref[...], g_ref[...], du_ref[...].astype(F32))
        dh0 = dh1_ref[...] + dx

        @pl.when(b == 0)
        def _():
            dg_ref[...] = dgt_ref[...]
            dmeta_ref[...] = jnp.zeros_like(dmeta_ref)

        dg_ref[...] += jnp.sum(dgc, axis=0, keepdims=True)
        dmeta_ref[...] += dh0[padf:padf + N_META, :]

    rows = pl.BlockSpec((first_real, D), lambda b: (b * (lp // first_real), 0))
    vec = pl.BlockSpec((1, D), lambda b: (0, 0))
    dg, dmeta = pl.pallas_call(
        head, name="pre_norm_bwd_head", grid=(bsz,),
        in_specs=[rows, rows, rows, vec, vec],
        out_specs=[vec, pl.BlockSpec((N_META, D), lambda b: (0, 0))],
        out_shape=[jax.ShapeDtypeStruct((1, D), F32), jax.ShapeDtypeStruct((N_META, D), F32)],
        compiler_params=_cparams(("arbitrary",)),
    )(h0, du1, dh1, gain, dg_tokens)
    return grad_x, dg, dmeta


def _shift_down(x, s, lp):
    return x if s == 0 else pltpu.roll(x, s, axis=0)


def _shift_up(x, s, lp):
    return x if s == 0 else pltpu.roll(x, lp - s, axis=0)


def _conv_fwd(x, w, taps, lp):
    y = None
    for i in range(taps):
        term = _shift_down(x, taps - 1 - i, lp) * w[i:i + 1, :]
        y = term if y is None else y + term
    return y


def _conv_bwd(x, w, dy, taps, lp):
    dx = None
    dws = []
    for i in range(taps):
        s = taps - 1 - i
        term = _shift_up(dy, s, lp) * w[i:i + 1, :]
        dx = term if dx is None else dx + term
        dws.append(jnp.sum(dy * _shift_down(x, s, lp), axis=0, keepdims=True))
    return dx, jnp.concatenate(dws, axis=0)


def _slab(lp, col0):
    return pl.BlockSpec((lp, DH), lambda b, s: (b, col0 // DH + s))


def _qkv_act(c, s):
    y = _silu(c)
    r = lax.rsqrt(jnp.sum(y * y, axis=-1, keepdims=True) + EPS)
    scale = jnp.where(s < HEADS, DH ** -0.5, 1.0)
    return y, r, scale


def _gdn_prep(proj, conv_qkv, bsz, lp, padf):
    def body(p_ref, w_ref, o_ref):
        s = pl.program_id(1)
        c = _conv_fwd(p_ref[...].astype(F32), w_ref[...], KQ, lp)
        y, r, scale = _qkv_act(c, s)
        out = jnp.where(s < 2 * HEADS, y * (r * scale), y)
        t = lax.broadcasted_iota(jnp.int32, (lp, 1), 0)
        o_ref[...] = jnp.where(t >= padf, out, 0.0).astype(BF16)

    return pl.pallas_call(
        body, name="gdn_prep", grid=(bsz, 3 * HEADS),
        in_specs=[_slab(lp, Q0), pl.BlockSpec((KQ, DH), lambda b, s: (0, s))],
        out_specs=_slab(lp, 0),
        out_shape=jax.ShapeDtypeStruct((bsz * lp, 3 * GW), BF16),
        compiler_params=_cparams(("parallel", "parallel")),
    )(proj, conv_qkv)


def _gdn_prep_bwd(proj, conv_qkv, dq, dk, dv, bsz, lp, padf):
    def body(p_ref, w_ref, dq_ref, dk_ref, dv_ref, dp_ref, dw_ref):
        b, s = pl.program_id(0), pl.program_id(1)
        x = p_ref[...].astype(F32)
        w = w_ref[...]
        c = _conv_fwd(x, w, KQ, lp)
        y, r, scale = _qkv_act(c, s)
        d = jnp.where(s < HEADS, dq_ref[...], jnp.where(s < 2 * HEADS, dk_ref[...], dv_ref[...])).astype(F32)
        dn = d * scale
        n = y * r
        dy_norm = r * (dn - n * jnp.sum(dn * n, axis=-1, keepdims=True))
        dyy = jnp.where(s < 2 * HEADS, dy_norm, dn)
        t = lax.broadcasted_iota(jnp.int32, (lp, 1), 0)
        dc = jnp.where(t >= padf, dyy * _dsilu(c), 0.0)
        dx, dw = _conv_bwd(x, w, dc, KQ, lp)
        dp_ref[...] = jnp.where(t >= padf, dx, 0.0).astype(BF16)

        @pl.when(jnp.logical_and(b == 0, s == 0))
        def _():
            dw_ref[...] = jnp.zeros_like(dw_ref)

        col_block = jnp.right_shift(lax.broadcasted_iota(jnp.int32, (KQ, 3 * GW), 1), DH.bit_length() - 1)
        dw_ref[...] += jnp.where(col_block == s, jnp.tile(dw, (1, 3 * HEADS)), 0.0)

    sl = lambda col0: pl.BlockSpec((lp, DH), lambda b, s: (b, col0 // DH + s))
    part = lambda k: pl.BlockSpec((lp, DH), lambda b, s: (b, jnp.clip(s - k * HEADS, 0, HEADS - 1)))
    return pl.pallas_call(
        body, name="gdn_prep_bwd", grid=(bsz, 3 * HEADS),
        in_specs=[sl(Q0), pl.BlockSpec((KQ, DH), lambda b, s: (0, s)), part(0), part(1), part(2)],
        out_specs=[sl(0), pl.BlockSpec((KQ, 3 * GW), lambda b, s: (0, 0))],
        out_shape=[jax.ShapeDtypeStruct((bsz * lp, 3 * GW), BF16), jax.ShapeDtypeStruct((KQ, 3 * GW), F32)],
        compiler_params=_cparams(("arbitrary", "arbitrary")),
    )(proj, conv_qkv, dq, dk, dv)


def _softplus(x):
    e = jnp.exp(-jnp.abs(x))
    one_e = 1.0 + e
    l1p = jnp.where(one_e == 1.0, e, jnp.log(one_e) * (e / (one_e - 1.0)))
    return jnp.maximum(x, 0.0) + l1p


def _tri(n, kind):
    i = lax.broadcasted_iota(jnp.int32, (n, n), 0)
    j = lax.broadcasted_iota(jnp.int32, (n, n), 1)
    return {"incl": i >= j, "strict": i > j, "upper": i <= j}[kind]


def _gates(proj, alog_v, dtb_v, bsz, lp, padf):
    nchunk = lp // CH

    def body(p_ref, al_ref, dt_ref, gb_ref, cs_ref):
        x = p_ref[...].astype(F32)
        lane = lax.broadcasted_iota(jnp.int32, (lp, DH), 1)
        t = lax.broadcasted_iota(jnp.int32, (lp, DH), 0)
        beta = jax.nn.sigmoid(x)
        g = -jnp.exp(al_ref[...]) * _softplus(x + dt_ref[...])
        gb = jnp.where(lane < HEADS, beta, jnp.where(lane < 2 * HEADS, g, 0.0))
        gb_ref[...] = jnp.where(t >= padf, gb, 0.0)
        ltri = _tri(CH, "incl").astype(BF16)
        rest = [gb_ref[pl.ds(n * CH, CH), :] for n in range(nchunk)]
        sums = None
        for _ in range(3):
            piece = [v.astype(BF16) for v in rest]
            rest = [v - p.astype(F32) for v, p in zip(rest, piece)]
            part = [jnp.dot(ltri, p, preferred_element_type=F32) for p in piece]
            sums = part if sums is None else [a + b for a, b in zip(sums, part)]
        for n in range(nchunk):
            cs_ref[pl.ds(n * CH, CH), :] = sums[n]

    blk = pl.BlockSpec((lp, DH), lambda b: (b, 0))
    vec = pl.BlockSpec((1, DH), lambda b: (0, 0))
    out = pl.BlockSpec((lp, DH), lambda b: (b, 0))
    return pl.pallas_call(
        body, name="gates", grid=(bsz,),
        in_specs=[blk, vec, vec], out_specs=[out, out],
        out_shape=[jax.ShapeDtypeStruct((bsz * lp, DH), F32)] * 2,
        compiler_params=_cparams(("parallel",)),
    )(proj, alog_v, dtb_v)


def _gates_bwd(proj, alog_v, dtb_v, dbeta_b, dg_b, gb, bsz, lp, padf):
    def body(p_ref, al_ref, dt_ref, db_ref, dg_ref, gb_ref, dl_ref, dv_ref):
        b = pl.program_id(0)
        x = p_ref[...].astype(F32)
        lane = lax.broadcasted_iota(jnp.int32, (lp, DH), 1)
        t = lax.broadcasted_iota(jnp.int32, (lp, DH), 0)
        dbeta = jnp.zeros((lp, DH), F32)
        dg = jnp.zeros((lp, DH), F32)
        for h in range(HEADS):
            dbeta = jnp.where(lane == h, db_ref[h], dbeta)
            dg = jnp.where(lane == HEADS + h, dg_ref[h], dg)
        real = t >= padf
        dbeta = jnp.where(real, dbeta, 0.0)
        dg = jnp.where(real, dg, 0.0)
        sb = jax.nn.sigmoid(x)
        neg_ea = -jnp.exp(al_ref[...])
        dsp = dg * neg_ea * jax.nn.sigmoid(x + dt_ref[...])
        dl = dbeta * sb * (1.0 - sb) + dsp
        dl_ref[...] = jnp.concatenate([dl, jnp.zeros((lp, BA_W - DH), F32)], axis=1).astype(BF16)
        dalog = jnp.sum(dg * gb_ref[...], axis=0, keepdims=True)
        ddt = jnp.sum(dsp, axis=0, keepdims=True)

        @pl.when(b == 0)
        def _():
            dv_ref[...] = jnp.zeros_like(dv_ref)

        dv_ref[...] += jnp.concatenate([dalog, ddt], axis=0)

    blk = pl.BlockSpec((lp, DH), lambda b: (b, 0))
    vec = pl.BlockSpec((1, DH), lambda b: (0, 0))
    hb = pl.BlockSpec((HEADS, lp, DH), lambda b: (b, 0, 0))
    return pl.pallas_call(
        body, name="gates_bwd", grid=(bsz,),
        in_specs=[blk, vec, vec, hb, hb, pl.BlockSpec((lp, DH), lambda b: (b, 0))],
        out_specs=[pl.BlockSpec((lp, BA_W), lambda b: (b, 0)), pl.BlockSpec((2, DH), lambda b: (0, 0))],
        out_shape=[jax.ShapeDtypeStruct((bsz * lp, BA_W), BF16), jax.ShapeDtypeStruct((2, DH), F32)],
        compiler_params=_cparams(("arbitrary",)),
    )(proj, alog_v, dtb_v, dbeta_b, dg_b, gb)


def _mix_heads(proj, o, gdn_norm, conv_sc, bsz, lp, padf):
    def body(o_ref, z_ref, gn_ref, sx_ref, sb_ref, sc_ref, w_ref, og_ref, os_ref):
        og_ref[...] = (_rms(o_ref[...], gn_ref[...])[0] * _silu(z_ref[...].astype(F32))).astype(BF16)
        conv = _conv_fwd(sc_ref[...].astype(F32) * sx_ref[...].astype(F32), w_ref[...], KS, lp)
        t = lax.broadcasted_iota(jnp.int32, (lp, 1), 0)
        os_ref[...] = jnp.where(t >= padf, sb_ref[...].astype(F32) * conv, 0.0).astype(BF16)

    half = _slab(lp, 0)
    return pl.pallas_call(
        body, name="mix_heads", grid=(bsz, HEADS),
        in_specs=[half, _slab(lp, Z0), pl.BlockSpec((1, DH), lambda b, s: (0, 0)),
                  _slab(lp, SX0), _slab(lp, SB0), _slab(lp, SC0), pl.BlockSpec((KS, DH), lambda b, s: (0, s))],
        out_specs=[half, half],
        out_shape=[jax.ShapeDtypeStruct((bsz * lp, GW), BF16)] * 2,
        compiler_params=_cparams(("parallel", "parallel")),
    )(o, proj, gdn_norm, proj, proj, proj, conv_sc)


def _mix_heads_bwd(proj, o, gdn_norm, conv_sc, dmixin, bsz, lp, padf):
    def body(o_ref, z_ref, gn_ref, sx_ref, sb_ref, sc_ref, w_ref, dg_ref, ds_ref,
             do_ref, dz_ref, dsx_ref, dsb_ref, dsc_ref, dgn_ref, dw_ref):
        s, b = pl.program_id(0), pl.program_id(1)
        t = lax.broadcasted_iota(jnp.int32, (lp, 1), 0)
        real = t >= padf
        o, z, gn, d = o_ref[...], z_ref[...].astype(F32), gn_ref[...], dg_ref[...].astype(F32)
        sz = _silu(z)
        on = _rms(o, gn)[0]
        dz_ref[...] = (d * on * _dsilu(z)).astype(BF16)
        dox, dgn = _rms_bwd(o, gn, d * sz)
        do_ref[...] = dox
        sx, sb, sc = sx_ref[...].astype(F32), sb_ref[...].astype(F32), sc_ref[...].astype(F32)
        w, e = w_ref[...], ds_ref[...].astype(F32)
        e = jnp.where(real, e, 0.0)
        xin = sc * sx
        conv = _conv_fwd(xin, w, KS, lp)
        dsb_ref[...] = (e * conv).astype(BF16)
        dxin, dw = _conv_bwd(xin, w, e * sb, KS, lp)
        dxin = jnp.where(real, dxin, 0.0)
        dsx_ref[...] = (dxin * sc).astype(BF16)
        dsc_ref[...] = (dxin * sx).astype(BF16)

        @pl.when(jnp.logical_and(s == 0, b == 0))
        def _():
            dgn_ref[...] = jnp.zeros_like(dgn_ref)

        @pl.when(b == 0)
        def _():
            dw_ref[...] = jnp.zeros_like(dw_ref)

        dgn_ref[...] += jnp.sum(dgn, axis=0, keepdims=True)
        dw_ref[...] += dw

    sl = lambda col0: pl.BlockSpec((lp, DH), lambda s, b: (b, col0 // DH + s))
    half = sl(0)
    return pl.pallas_call(
        body, name="mix_heads_bwd", grid=(HEADS, bsz),
        in_specs=[half, sl(Z0), pl.BlockSpec((1, DH), lambda s, b: (0, 0)), sl(SX0), sl(SB0), sl(SC0),
                  pl.BlockSpec((KS, DH), lambda s, b: (0, s)), sl(0), sl(GW)],
        out_specs=[half] * 5 + [pl.BlockSpec((1, DH), lambda s, b: (0, 0)), pl.BlockSpec((KS, DH), lambda s, b: (0, s))],
        out_shape=[jax.ShapeDtypeStruct((bsz * lp, GW), F32)] + [jax.ShapeDtypeStruct((bsz * lp, GW), BF16)] * 4
        + [jax.ShapeDtypeStruct((1, DH), F32), jax.ShapeDtypeStruct((KS, SW), F32)],
        compiler_params=_cparams(("arbitrary", "arbitrary")),
    )(o, proj, gdn_norm, proj, proj, proj, conv_sc, dmixin, dmixin)


def _split_dot(a, b, dims=NN):
    ah = a.astype(BF16)
    al = (a - ah.astype(F32)).astype(BF16)
    bh = b.astype(BF16)
    bl = (b - bh.astype(F32)).astype(BF16)

    def d(x, y):
        return lax.dot_general(x, y, (dims, ((), ())), preferred_element_type=F32)

    return d(ah, bh) + (d(ah, bl) + d(al, bh))


def _eye(n):
    return (lax.broadcasted_iota(jnp.int32, (n, n), 0) == lax.broadcasted_iota(jnp.int32, (n, n), 1)).astype(F32)


def _each(f, *lists):
    return [f(*xs) for xs in zip(*lists)]


def _inv_unit_lower(a_list):
    eye = _eye(CH)
    p = _each(lambda a: -a, a_list)
    t = _each(lambda x: eye + x, p)
    p = _each(lambda x: _split_dot(x, x), p)
    for level in range(5):
        if level < 4:
            prod = _each(lambda tt, pp: _split_dot(jnp.concatenate([tt, pp], axis=0), pp), t, p)
            t = _each(lambda tt, pr: tt + pr[:CH], t, prod)
            p = _each(lambda pr: pr[CH:], prod)
        else:
            t = _each(lambda tt, pp: tt + _split_dot(tt, pp), t, p)
    return t


def _chunk_gates(gc_ref, be_ref, gt_ref, g):
    rows = pl.ds(g * CH, CH)
    gc = gc_ref[0, rows, :]
    be = be_ref[0, rows, :]
    gcl = gc_ref[0, pl.ds(g * CH + CH - 1, 1), :]
    gt = gt_ref[0, g]
    incl = _tri(CH, "incl")
    dec = jnp.where(incl, jnp.exp(jnp.where(incl, gc[:, :CH] - gt, 0.0)), 0.0)
    return rows, be, dec, jnp.exp(gc), jnp.exp(gcl - gc), jnp.exp(gcl)


def _chunk_group(nchunk):
    return 3 if nchunk % 3 == 0 else 1


def _local_group(nchunk):
    return 11 if nchunk % 11 == 0 else _chunk_group(nchunk)


def _gdn_local_specs(lp, grp):
    ngrp = lp // (grp * CH)
    head = lambda col0: pl.BlockSpec((grp * CH, DH), lambda i, j: ((i // HEADS) * ngrp + j, col0 // DH + i % HEADS))
    lane_b = pl.BlockSpec((1, grp * CH, DH), lambda i, j: (i, j, 0))
    per_chunk = lambda r, c: pl.BlockSpec((1, grp, r, c), lambda i, j: (i, j, 0, 0))
    return ngrp, head, lane_b, per_chunk


def _gdn_local_fwd(qkv, gcb, bb, gct, bsz, lp):
    nchunk = lp // CH
    grp = _local_group(nchunk)
    ngrp, head, lane_b, per_chunk = _gdn_local_specs(lp, grp)
    bh = bsz * HEADS

    def body(q_ref, k_ref, v_ref, gc_ref, be_ref, gt_ref,
             u_ref, l1_ref, l2_ref, l3_ref, wt_ref, kd_ref, ti_ref, gl_ref):
        incl, strict = _tri(CH, "incl"), _tri(CH, "strict")
        rows, be, dec, eg, ekd, gl = zip(*[_chunk_gates(gc_ref, be_ref, gt_ref, g) for g in range(grp)])
        q = [q_ref[r, :].astype(F32) for r in rows]
        k = [k_ref[r, :].astype(F32) for r in rows]
        v = [v_ref[r, :].astype(F32) for r in rows]
        kb = _each(lambda x, y: x * y, k, be)
        a = _each(lambda x, y, d: jnp.where(strict, _bdot(x, y, NT) * d, 0.0), kb, k, dec)
        tinv = _inv_unit_lower(a)
        u = _each(lambda t, x, y: _bdot(t, x * y, NN), tinv, v, be)
        w = _each(lambda t, x, y: _bdot(t, x * y, NN), tinv, kb, eg)
        qk = _each(lambda x, y, d: jnp.where(incl, _bdot(x, y, NT) * d, 0.0), q, k, dec)
        qk_t = _each(lambda x: x.T, qk)
        qd = _each(lambda x, y: x * y, q, eg)
        kd = _each(lambda x, y: x * y, k, ekd)
        kd_t = _each(lambda x: x.T, kd)
        qd_t = _each(lambda x: x.T, qd)
        w_t = _each(lambda x: x.T, w)
        for g in range(grp):
            u_ref[rows[g], :] = u[g]
            l1_ref[0, g] = jnp.concatenate([w[g], qd[g]], axis=0).astype(BF16)
            l2_ref[0, g] = jnp.concatenate([qk[g], kd_t[g]], axis=0).astype(BF16)
            l3_ref[0, g] = jnp.concatenate([qk_t[g], qd_t[g]], axis=0).astype(BF16)
            wt_ref[0, g] = w_t[g].astype(BF16)
            kd_ref[0, g] = kd[g].astype(BF16)
            ti_ref[0, g] = tinv[g]
            gl_ref[0, g] = gl[g]

    r = bsz * lp
    shapes = [((r, GW), F32), ((bh, nchunk, 2 * CH, DH), BF16), ((bh, nchunk, 3 * CH, CH), BF16),
              ((bh, nchunk, 3 * CH, CH), BF16), ((bh, nchunk, DH, CH), BF16), ((bh, nchunk, CH, DH), BF16),
              ((bh, nchunk, CH, CH), F32), ((bh, nchunk, 1, DH), F32)]
    return pl.pallas_call(
        body, name="gdn_local_fwd", grid=(bh, ngrp),
        in_specs=[head(0), head(GW), head(2 * GW), lane_b, lane_b, per_chunk(1, CH)],
        out_specs=[head(0), per_chunk(2 * CH, DH), per_chunk(3 * CH, CH), per_chunk(3 * CH, CH), per_chunk(DH, CH),
                   per_chunk(CH, DH), per_chunk(CH, CH), per_chunk(1, DH)],
        out_shape=[jax.ShapeDtypeStruct(s, dt) for s, dt in shapes],
        compiler_params=_cparams(("parallel", "parallel")),
    )(qkv, qkv, qkv, gcb, bb, gct)


def _scan_specs(bsz, lp, grp, order):
    ngrp = lp // (grp * CH)
    bh = bsz * HEADS
    seq = lambda: pl.BlockSpec((bsz, grp * CH, GW), lambda s: (0, order(s, ngrp), 0))
    per_chunk = lambda r, c: pl.BlockSpec((bh, grp, r, c), lambda s: (0, order(s, ngrp), 0, 0))
    return ngrp, bh, seq, per_chunk


def _gdn_scan_fwd(u, l1, l2, gl, bsz, lp):
    nchunk = lp // CH
    grp = _chunk_group(nchunk)
    ngrp, bh, seq, per_chunk = _scan_specs(bsz, lp, grp, lambda s, n: s)

    def body(u_ref, l1_ref, l2_ref, gl_ref, o_ref, vn_ref, st_ref, s_scr):
        @pl.when(pl.program_id(0) == 0)
        def _():
            s_scr[...] = jnp.zeros_like(s_scr)

        chains = list(range(bh))
        where = [(i // HEADS, slice((i % HEADS) * DH, (i % HEADS + 1) * DH)) for i in chains]

        def dot(a, b):
            return jnp.dot(a, b.astype(BF16), preferred_element_type=F32)

        for c in range(grp):
            rows = slice(c * CH, (c + 1) * CH)
            s = [s_scr[i] for i in chains]
            for i in chains:
                st_ref[i, c] = s[i]
            x = _each(lambda i, si: dot(l1_ref[i, c], si), chains, s)
            vnew = _each(lambda bc, xi: u_ref[bc[0], rows, bc[1]] - xi[:CH], where, x)
            y = _each(lambda i, vi: dot(l2_ref[i, c], vi), chains, vnew)
            for i, (b, cols) in zip(chains, where):
                o_ref[b, rows, cols] = x[i][CH:] + y[i][:CH]
                vn_ref[b, rows, cols] = vnew[i].astype(BF16)
                s_scr[i] = s[i] * gl_ref[i, c] + y[i][CH:]

    return pl.pallas_call(
        body, name="gdn_scan_fwd", grid=(ngrp,),
        in_specs=[seq(), per_chunk(2 * CH, DH), per_chunk(3 * CH, CH), per_chunk(1, DH)],
        out_specs=[seq(), seq(), per_chunk(DH, DH)],
        out_shape=[jax.ShapeDtypeStruct((bsz, lp, GW), F32), jax.ShapeDtypeStruct((bsz, lp, GW), BF16),
                   jax.ShapeDtypeStruct((bh, nchunk, DH, DH), F32)],
        scratch_shapes=[pltpu.VMEM((bh, DH, DH), F32)],
        compiler_params=_cparams(("arbitrary",)),
    )(u.reshape(bsz, lp, GW), l1, l2, gl)


def _gdn_scan_bwd(d_o, l3, wt, kd, gl, bsz, lp):
    nchunk = lp // CH
    grp = _chunk_group(nchunk)
    ngrp, bh, seq, per_chunk = _scan_specs(bsz, lp, grp, lambda s, n: n - 1 - s)

    def body(do_ref, l3_ref, wt_ref, kd_ref, gl_ref, dvn_ref, dst_ref, ds_scr):
        @pl.when(pl.program_id(0) == 0)
        def _():
            ds_scr[...] = jnp.zeros_like(ds_scr)

        chains = list(range(bh))
        where = [(i // HEADS, slice((i % HEADS) * DH, (i % HEADS + 1) * DH)) for i in chains]

        def dot(a, b):
            return jnp.dot(a, b.astype(BF16), preferred_element_type=F32)

        for c in reversed(range(grp)):
            rows = slice(c * CH, (c + 1) * CH)
            ds = [ds_scr[i] for i in chains]
            for i in chains:
                dst_ref[i, c] = ds[i]
            p = _each(lambda i, bc: dot(l3_ref[i, c], do_ref[bc[0], rows, bc[1]]), chains, where)
            dvn = _each(lambda i, pi, di: pi[:CH] + dot(kd_ref[i, c], di), chains, p, ds)
            wd = _each(lambda i, vi: dot(wt_ref[i, c], vi), chains, dvn)
            for i, (b, cols) in zip(chains, where):
                dvn_ref[b, rows, cols] = dvn[i].astype(BF16)
                ds_scr[i] = ds[i] * gl_ref[i, c] + p[i][CH:] - wd[i]

    return pl.pallas_call(
        body, name="gdn_scan_bwd", grid=(ngrp,),
        in_specs=[seq(), per_chunk(3 * CH, CH), per_chunk(DH, CH), per_chunk(CH, DH), per_chunk(1, DH)],
        out_specs=[seq(), per_chunk(DH, DH)],
        out_shape=[jax.ShapeDtypeStruct((bsz, lp, GW), BF16), jax.ShapeDtypeStruct((bh, nchunk, DH, DH), F32)],
        scratch_shapes=[pltpu.VMEM((bh, DH, DH), F32)],
        compiler_params=_cparams(("arbitrary",)),
    )(d_o.reshape(bsz, lp, GW), l3, wt, kd, gl)


def _gdn_local_bwd(qkv, gcb, bb, gct, tinv_all, states, dstates, vnew, dvnew, d_o, bsz, lp):
    nchunk = lp // CH
    grp = _local_group(nchunk)
    ngrp, head, lane_b, per_chunk = _gdn_local_specs(lp, grp)
    bh = bsz * HEADS

    def body(q_ref, k_ref, v_ref, gc_ref, be_ref, gt_ref, ti_ref, st_ref, dst_ref, vn_ref, dvn_ref, do_ref,
             dq_ref, dk_ref, dv_ref, dg_ref, db_ref):
        incl, strict = _tri(CH, "incl"), _tri(CH, "strict")
        upper = _tri(CH, "upper").astype(F32)
        ones = jnp.ones((CH, DH), F32)
        last = lax.broadcasted_iota(jnp.int32, (CH, 1), 0) == CH - 1

        def rsum(x):
            return jnp.sum(x, axis=-1, keepdims=True)

        def mul(xs, ys):
            return _each(lambda x, y: x * y, xs, ys)

        rows, be, dec, eg, ekd, gl = zip(*[_chunk_gates(gc_ref, be_ref, gt_ref, g) for g in range(grp)])
        q = [q_ref[r, :].astype(F32) for r in rows]
        k = [k_ref[r, :].astype(F32) for r in rows]
        v = [v_ref[r, :].astype(F32) for r in rows]
        vnew = [vn_ref[r, :] for r in rows]
        dvn = [dvn_ref[r, :] for r in rows]
        do = [do_ref[r, :] for r in rows]
        tinv = [ti_ref[0, g] for g in range(grp)]
        s = [st_ref[0, g] for g in range(grp)]
        ds = [dst_ref[0, g] for g in range(grp)]
        kb, vb, qd, kd = mul(k, be), mul(v, be), mul(q, eg), mul(k, ekd)
        kbg = mul(kb, eg)
        a = _each(lambda x, y, d: jnp.where(strict, _bdot(x, y, NT) * d, 0.0), kb, k, dec)
        qk = _each(lambda x, y, d: jnp.where(incl, _bdot(x, y, NT) * d, 0.0), q, k, dec)
        dqk = _each(lambda x, y: jnp.where(incl, _bdot(x, y, NT), 0.0), do, vnew)
        dqd = _each(lambda x, y: _bdot(x, y, NT), do, s)
        dkd = _each(lambda x, y: _bdot(x, y, NT), vnew, ds)
        dgl = _each(lambda x, y: jnp.sum(rsum(x * y), axis=0, keepdims=True), s, ds)
        dw = _each(lambda x, y: -_bdot(x, y, NT), dvn, s)
        dvb = _each(lambda t, x: _bdot(t, x, TN), tinv, dvn)
        dkbg = _each(lambda t, x: _bdot(t, x, TN), tinv, dw)
        dt = _each(lambda x, y, z, w: _bdot(x, y, NT) + _bdot(z, w, NT), dvn, vb, dw, kbg)
        tdt = _each(lambda t, x: _split_dot(t, x, TN), tinv, dt)
        da = _each(lambda x, t: jnp.where(strict, -_split_dot(x, t, NT), 0.0), tdt, tinv)
        dp, dr = mul(da, dec), mul(dqk, dec)
        dkb = _each(lambda x, y, z, e: _bdot(x, y, NN) + z * e, dp, k, dkbg, eg)
        dk = _each(lambda p_, kb_, r_, q_, dkd_, ekd_, dkb_, be_:
                   _bdot(p_, kb_, TN) + _bdot(r_, q_, TN) + dkd_ * ekd_ + dkb_ * be_,
                   dp, kb, dr, q, dkd, ekd, dkb, be)
        dq = _each(lambda r_, k_, dqd_, eg_: _bdot(r_, k_, NN) + dqd_ * eg_, dr, k, dqd, eg)
        m = _each(lambda da_, a_, dqk_, qk_: da_ * a_ + dqk_ * qk_, da, a, dqk, qk)
        m_cols = _each(lambda x: _split_dot(x, ones, TN), m)
        kd_term = _each(lambda x, y: rsum(x * y), dkd, kd)
        dgc = _each(lambda m_, mc, dqd_, qd_, kt, dkbg_, kbg_, dgl_, gl_:
                    rsum(m_) - mc + rsum(dqd_ * qd_) - kt + rsum(dkbg_ * kbg_)
                    + jnp.where(last, jnp.sum(kt, axis=0, keepdims=True) + dgl_ * gl_, 0.0),
                    m, m_cols, dqd, qd, kd_term, dkbg, kbg, dgl, gl)
        dg = _each(lambda x: _split_dot(upper, x, NN), dgc)
        for g in range(grp):
            dq_ref[rows[g], :] = dq[g].astype(BF16)
            dk_ref[rows[g], :] = dk[g].astype(BF16)
            dv_ref[rows[g], :] = (dvb[g] * be[g]).astype(BF16)
            dg_ref[0, rows[g], :] = dg[g]
            db_ref[0, rows[g], :] = jnp.broadcast_to(rsum(dkb[g] * k[g]) + rsum(dvb[g] * v[g]), (CH, DH))

    r = bsz * lp
    return pl.pallas_call(
        body, name="gdn_local_bwd", grid=(bh, ngrp),
        in_specs=[head(0), head(GW), head(2 * GW), lane_b, lane_b, per_chunk(1, CH), per_chunk(CH, CH),
                  per_chunk(DH, DH), per_chunk(DH, DH), head(0), head(0), head(0)],
        out_specs=[head(0), head(0), head(0), lane_b, lane_b],
        out_shape=[jax.ShapeDtypeStruct((r, GW), BF16)] * 3 + [jax.ShapeDtypeStruct((bh, lp, DH), F32)] * 2,
        compiler_params=_cparams(("parallel", "parallel")),
    )(qkv, qkv, qkv, gcb, bb, gct, tinv_all, states, dstates, vnew, dvnew, d_o)


def _place():
    return lax.axis_index("x"), lax.axis_index("y"), lax.axis_index("c")


def _any_specs(n):
    return [pl.BlockSpec(memory_space=pl.ANY)] * n


def _all_gather(name, shards):
    n = len(shards)

    def body(*refs):
        ins, outs = refs[:n], refs[n:2 * n]
        send_sems, recv_sems, local_sems = refs[2 * n:2 * n + 3]
        stage = refs[2 * n + 3:]
        x, y, c = _place()
        me, sibling = (x, y, c), (x, y, 1 - c)
        chips = [(1 - x, y), (x, 1 - y), (1 - x, 1 - y)]

        def slot(i, p):
            return outs[i].at[4 * p[0] + 2 * p[1] + p[2]]

        def copy(i, k, block, to, src=None):
            return pltpu.make_async_remote_copy(
                src_ref=slot(i, block) if src is None else src, dst_ref=slot(i, block),
                send_sem=send_sems.at[i, k], recv_sem=recv_sems.at[i, k], device_id=to, device_id_type=MESH)

        load = [pltpu.make_async_copy(ins[i], stage[i], local_sems.at[i]) for i in range(n)]
        for cp in load:
            cp.start()
        first = []
        for i in range(n):
            first.append(copy(i, 0, me, sibling, src=ins[i]))
            first += [copy(i, 1 + j, me, (*chip, c), src=ins[i]) for j, chip in enumerate(chips)]
        for cp in first:
            cp.start()
        for cp in load:
            cp.wait()
        mine = [pltpu.make_async_copy(stage[i], slot(i, me), local_sems.at[i]) for i in range(n)]
        for cp in mine:
            cp.start()
        passed = []
        for j, chip in enumerate(chips):
            for i in range(n):
                copy(i, 1 + j, (*chip, c), me).wait_recv()
                fwd = copy(i, 4 + j, (*chip, c), sibling)
                fwd.start()
                passed.append(fwd)
        for i in range(n):
            copy(i, 0, sibling, me).wait_recv()
            for j, chip in enumerate(chips):
                copy(i, 4 + j, (*chip, 1 - c), me).wait_recv()
        for cp in first + passed:
            cp.wait_send()
        for cp in mine:
            cp.wait()

    return pl.pallas_call(
        body, name=name,
        in_specs=_any_specs(n), out_specs=_any_specs(n),
        out_shape=[jax.ShapeDtypeStruct((N_DEV,) + s.shape, s.dtype) for s in shards],
        scratch_shapes=[pltpu.SemaphoreType.DMA((n, 7)), pltpu.SemaphoreType.DMA((n, 7)),
                        pltpu.SemaphoreType.DMA((n,))] + [pltpu.VMEM(s.shape, s.dtype) for s in shards],
    )(*shards)


HBM_SPEC = pl.BlockSpec(memory_space=pltpu.HBM)
SEM_SPEC = pl.BlockSpec(memory_space=pltpu.SEMAPHORE)
EFFECT = pltpu.SideEffectType.DATAFLOW_SIDE_EFFECTING
N_PEER = N_DEV - 1


def _peer(r):
    x, y, c = _place()
    return ((1 - x) if r & 4 else x, (1 - y) if r & 2 else y, (1 - c) if r & 1 else c)


def _slot_of(p):
    return 4 * p[0] + 2 * p[1] + p[2]


def _hbm(a):
    return pltpu.with_memory_space_constraint(a, pltpu.HBM)


def _gather_pattern(src, land):
    me = _place()
    return [(src, land.at[_slot_of(me)], _peer(r), land.at[_slot_of(_peer(r))]) for r in range(1, N_DEV)]


def _alltoall_pattern(src, land):
    me = _place()
    return [(src.at[_slot_of(_peer(r))], land.at[_slot_of(me)], _peer(r), land.at[_slot_of(_peer(r))])
            for r in range(1, N_DEV)]


def _chipcast_pattern(src, land):
    x, y, c = _place()
    me = (x, y, c)
    peers = [(x, y, 1 - c), (1 - x, y, c), (x, 1 - y, c), (1 - x, 1 - y, c)]
    return [(src, land.at[_slot_of(me)], p, land.at[_slot_of(p)]) for p in peers]


def _forward_pattern(land, _):
    x, y, c = _place()
    chips = [(1 - x, y), (x, 1 - y), (1 - x, 1 - y)]
    return [(land.at[_slot_of((*ch, c))], land.at[_slot_of((*ch, c))], (x, y, 1 - c), land.at[_slot_of((*ch, 1 - c))])
            for ch in chips]


def _sibling_pattern(src, land):
    x, y, c = _place()
    return [(src.at[ch, 1 - c], land.at[ch], (x, y, 1 - c), land.at[ch]) for ch in range(4)]


def _chip_pattern(src, land):
    x, y, c = _place()
    chips = [(1 - x, y), (x, 1 - y), (1 - x, 1 - y)]
    return [(src.at[2 * ch[0] + ch[1]], land.at[j], (*ch, c), land.at[j]) for j, ch in enumerate(chips)]


def _split_start(name, groups, pattern, land_shape, after=None):
    sizes = [len(g) for g in groups]
    arrays = [s for g in groups for s in g]
    n = len(arrays)
    inplace = land_shape is None
    fresh = [] if inplace else [lax.empty(land_shape(s.shape), s.dtype) for s in arrays]
    bufs = arrays + fresh
    order = [] if after is None else [after]

    def body(*refs):
        srcs, lands = refs[:n], (refs[:n] if inplace else refs[n:2 * n])
        first_out = len(bufs) + len(order)
        sems = refs[first_out:first_out + 2 * len(groups)]
        token = refs[-1]
        k = 0
        for gi, size in enumerate(sizes):
            for i in range(size):
                copies = pattern(srcs[k], lands[k])
                for j, (s, d, peer, _) in enumerate(copies):
                    pltpu.make_async_remote_copy(
                        src_ref=s, dst_ref=d, send_sem=sems[2 * gi].at[i * len(copies) + j],
                        recv_sem=sems[2 * gi + 1].at[i * len(copies) + j], device_id=peer, device_id_type=MESH).start()
                k += 1
        token[...] = jnp.zeros_like(token)

    per_array = {_gather_pattern: N_PEER, _alltoall_pattern: N_PEER, _chipcast_pattern: 4, _forward_pattern: 3,
                 _sibling_pattern: 4, _chip_pattern: 3}[pattern]
    sem_shapes = [pltpu.SemaphoreType.DMA((size * per_array,)) for size in sizes for _ in range(2)]
    res = pl.pallas_call(
        body, name=name,
        in_specs=[HBM_SPEC] * len(bufs) + [pl.BlockSpec(memory_space=pl.ANY)] * len(order),
        out_specs=[SEM_SPEC] * len(sem_shapes) + [HBM_SPEC] * len(bufs) + [pl.BlockSpec(memory_space=pltpu.VMEM)],
        out_shape=sem_shapes + [pltpu.HBM(s.shape, s.dtype) for s in bufs] + [jax.ShapeDtypeStruct((8, DH), F32)],
        input_output_aliases={i: len(sem_shapes) + i for i in range(len(bufs))},
        compiler_params=pltpu.CompilerParams(has_side_effects=EFFECT),
    )(*[_hbm(s) for s in bufs], *order)
    nsem = len(sem_shapes)
    thru = list(res[nsem:nsem + len(bufs)])
    out, k = [], 0
    for gi, size in enumerate(sizes):
        srcs_out = [] if inplace else thru[k:k + size]
        lands_out = thru[k:k + size] if inplace else thru[n + k:n + k + size]
        out.append((res[2 * gi], res[2 * gi + 1], srcs_out, lands_out))
        k += size
    return out, res[-1]


def _split_wait(name, group, pattern, after):
    send_sems, recv_sems, arrays, lands = group
    n = len(lands)
    inplace = not arrays
    bufs = list(arrays) + list(lands)
    fill_own = pattern in (_gather_pattern, _chipcast_pattern, _alltoall_pattern)
    own_of = (lambda a: a.at[_slot_of(_place())]) if pattern is _alltoall_pattern else (lambda a: a)
    own_shape = (lambda a: a.shape[1:]) if pattern is _alltoall_pattern else (lambda a: a.shape)

    def body(*refs):
        srcs, lands_ = refs[:n], (refs[:n] if inplace else refs[n:2 * n])
        s_sems, r_sems = refs[len(bufs)], refs[len(bufs) + 1]
        own = []
        if fill_own:
            stage, local_sems = refs[-n - 1:-1], refs[-1]
            load = [pltpu.make_async_copy(own_of(srcs[i]), stage[i], local_sems.at[i]) for i in range(n)]
            for cp in load:
                cp.start()
            for cp in load:
                cp.wait()
            own = [pltpu.make_async_copy(stage[i], lands_[i].at[_slot_of(_place())], local_sems.at[i])
                   for i in range(n)]
            for cp in own:
                cp.start()
        for i in range(n):
            copies = pattern(srcs[i], lands_[i])
            for j, (s, _, peer, filled) in enumerate(copies):
                cp = pltpu.make_async_remote_copy(
                    src_ref=s, dst_ref=filled, send_sem=s_sems.at[i * len(copies) + j],
                    recv_sem=r_sems.at[i * len(copies) + j], device_id=peer, device_id_type=MESH)
                cp.wait_send()
                cp.wait_recv()
        for cp in own:
            cp.wait()

    res = pl.pallas_call(
        body, name=name,
        in_specs=[HBM_SPEC] * len(bufs) + [SEM_SPEC, SEM_SPEC, pl.BlockSpec(memory_space=pl.ANY)],
        out_specs=[HBM_SPEC] * len(bufs),
        out_shape=[pltpu.HBM(s.shape, s.dtype) for s in bufs],
        scratch_shapes=([pltpu.VMEM(own_shape(a), a.dtype) for a in arrays] + [pltpu.SemaphoreType.DMA((n,))])
        if fill_own else [],
        input_output_aliases={i: i for i in range(len(bufs))},
        compiler_params=pltpu.CompilerParams(has_side_effects=EFFECT),
    )(*bufs, send_sems, recv_sems, after)
    return ([], list(res)) if inplace else (list(res[:n]), list(res[n:]))


def _pair_sum(name, grads, recv, core):
    _, _, r, c = grads.shape
    tr = r if r * c <= (1 << 20) or r % 16 else _div_tile(r, 512, 16)

    def body(core_ref, g_ref, r_ref, o_ref):
        o_ref[...] = (g_ref[0].astype(F32) + r_ref[...].astype(F32)).astype(o_ref.dtype)

    return pl.pallas_call(
        body, name=name,
        grid_spec=pltpu.PrefetchScalarGridSpec(
            num_scalar_prefetch=1, grid=(4, r // tr),
            in_specs=[pl.BlockSpec((1, 1, tr, c), lambda s, i, cr: (s, cr[0], i, 0)),
                      pl.BlockSpec((1, tr, c), lambda s, i, cr: (s, i, 0))],
            out_specs=pl.BlockSpec((1, tr, c), lambda s, i, cr: (s, i, 0))),
        out_shape=jax.ShapeDtypeStruct(recv.shape, recv.dtype),
        compiler_params=_cparams(("parallel", "parallel")),
    )(core, grads, recv)


def _adamw_math(w, g, m, v):
    m = ADAM_B1 * m + (1.0 - ADAM_B1) * g
    v = ADAM_B2 * v + (1.0 - ADAM_B2) * (g * g)
    m_hat = m / (1.0 - ADAM_B1 ** ADAM_STEP)
    v_hat = v / (1.0 - ADAM_B2 ** ADAM_STEP)
    delta = -ADAM_LR * (m_hat / (jnp.sqrt(v_hat) + ADAM_EPS) + ADAM_WD * w)
    return delta, m, v


SMALL_LEAVES = (
    ("mix_pre_norm", 0, 1, 0, D, (1, D)), ("mix_post_norm", 1, 1, 0, D, (1, D)),
    ("ffn_pre_norm", 2, 1, 0, D, (1, D)), ("ffn_post_norm", 3, 1, 0, D, (1, D)),
    ("meta_tokens", 8, N_META, 0, DH, (N_META, DH)),
    ("conv_qkv", 24, KQ, 0, 3 * GW // N_DEV, (1, KQ, 3 * GW // N_DEV)),
    ("conv_sc", 32, KS, 0, SW // N_DEV, (1, KS, SW // N_DEV)),
    ("a_log", 40, 1, 0, HEADS, (1, HEADS)), ("dt_bias", 40, 1, DH, HEADS, (1, HEADS)),
    ("gdn_norm", 40, 1, 2 * DH, DH, (1, DH)),
)
SMALL_LOSS_AT = (40, 3 * DH)


def _adamw_small(slabs, late, w, m, v):
    nslot = slabs.shape[0]
    nleaf = len(SMALL_LEAVES)

    def body(p_ref, l_ref, w_ref, m_ref, v_ref, *refs):
        outs, loss_ref, scr = refs[:4 * nleaf], refs[4 * nleaf], refs[4 * nleaf + 1]
        g = p_ref[0]
        gain = l_ref[0, pl.ds(0, 1), :]
        meta = l_ref[0, pl.ds(8, N_META), pl.ds(0, DH)]
        for s in range(1, nslot):
            g = g + p_ref[s]
            gain = gain + l_ref[s, pl.ds(0, 1), :]
            meta = meta + l_ref[s, pl.ds(8, N_META), pl.ds(0, DH)]
        scr[0] = g
        scr[0, pl.ds(0, 1), pl.ds(0, D)] += gain
        scr[0, pl.ds(8, N_META), pl.ds(0, DH)] += meta
        g = scr[0]
        d, mn, vn = _adamw_math(w_ref[...], g, m_ref[...], v_ref[...])
        for kind, val in enumerate((g, d, mn, vn)):
            scr[kind] = val
        for kind in range(4):
            for li, (_, r0, nr, c0, nc, shape) in enumerate(SMALL_LEAVES):
                val = scr[kind, pl.ds(r0, nr), pl.ds(c0, nc)]
                o = outs[kind * nleaf + li]
                if len(shape) == 3:
                    o[0] = val
                else:
                    o[...] = val
        loss_ref[...] = scr[0, pl.ds(SMALL_LOSS_AT[0], 1), pl.ds(SMALL_LOSS_AT[1], 1)]

    whole = lambda a: pl.BlockSpec(a.shape, lambda: (0,) * a.ndim)
    out_shapes = [jax.ShapeDtypeStruct(shape, F32) for _ in range(4) for (*_, shape) in SMALL_LEAVES]
    out_shapes.append(jax.ShapeDtypeStruct((1, 1), F32))
    res = pl.pallas_call(
        body, name="adamw_small",
        in_specs=[whole(slabs), whole(late), whole(w), whole(m), whole(v)],
        out_specs=[pl.BlockSpec(s.shape, lambda n=len(s.shape): (0,) * n) for s in out_shapes],
        out_shape=out_shapes,
        scratch_shapes=[pltpu.VMEM((4,) + w.shape, F32)],
        compiler_params=pltpu.CompilerParams(vmem_limit_bytes=VMEM_LIMIT),
    )(slabs, late, w, m, v)
    kinds = [{nm: res[kind * nleaf + li] for li, (nm, *_) in enumerate(SMALL_LEAVES)} for kind in range(4)]
    return kinds, res[-1]


def _adamw_reduced(name, parts, chip, recv, w, m, v, after=None):
    _, r, c = parts.shape
    tr = r if r * c <= (1 << 20) or r % 16 else _div_tile(r, 512, 16)
    order = [] if after is None else [after]

    def body(chip_ref, p_ref, r_ref, w_ref, m_ref, v_ref, *rest):
        g_out, d_out, m_out, v_out = rest[len(order):]
        g = p_ref[0].astype(F32)
        for s in range(3):
            g = g + r_ref[s].astype(F32)
        d, mn, vn = _adamw_math(w_ref[...], g, m_ref[...], v_ref[...])
        g_out[...] = g
        d_out[...] = d
        m_out[...] = mn
        v_out[...] = vn

    blk = pl.BlockSpec((tr, c), lambda i, ch: (i, 0))
    return pl.pallas_call(
        body, name=name,
        grid_spec=pltpu.PrefetchScalarGridSpec(
            num_scalar_prefetch=1, grid=(r // tr,),
            in_specs=[pl.BlockSpec((1, tr, c), lambda i, ch: (ch[0], i, 0)),
                      pl.BlockSpec((3, tr, c), lambda i, ch: (0, i, 0)), blk, blk, blk]
            + [pl.BlockSpec((8, DH), lambda i, ch: (0, 0))] * len(order),
            out_specs=[blk] * 4),
        out_shape=[jax.ShapeDtypeStruct((r, c), F32)] * 4,
        compiler_params=_cparams(("parallel",)),
    )(chip, parts, recv, w, m, v, *order)


SMALL_ROWS, SMALL_COLS = 48, 3 * GW


def _pack_small(gains4, meta, conv_qkv, conv_sc, a_log, dt_bias, gdn_norm, extra=None):
    def tile(a):
        return jnp.pad(a, ((0, -a.shape[0] % 8), (0, SMALL_COLS - a.shape[1])))
    scal = jnp.concatenate([jnp.pad(a_log, ((0, 0), (0, DH - HEADS))), jnp.pad(dt_bias, ((0, 0), (0, DH - HEADS))),
                            gdn_norm] + ([] if extra is None else [extra]), axis=1)
    gains = sum(jnp.pad(g, ((i, 7 - i), (0, SMALL_COLS - g.shape[1]))) for i, g in enumerate(gains4))
    parts = [gains, tile(meta), tile(conv_qkv), tile(conv_sc), tile(scal)]
    slab = jnp.concatenate(parts, axis=0)
    assert slab.shape == (SMALL_ROWS, SMALL_COLS)
    return slab


def kernel(x, meta_tokens, mix_pre_norm, mix_post_norm, ffn_pre_norm, ffn_post_norm, w_in, conv_qkv, a_log, dt_bias, gdn_norm, conv_sc, w_out, w_gate, w_up, w_down, loss_target, m_meta_tokens, m_mix_pre_norm, m_mix_post_norm, m_ffn_pre_norm, m_ffn_post_norm, m_w_in, m_conv_qkv, m_a_log, m_dt_bias, m_gdn_norm, m_conv_sc, m_w_out, m_w_gate, m_w_up, m_w_down, v_meta_tokens, v_mix_pre_norm, v_mix_post_norm, v_ffn_pre_norm, v_ffn_post_norm, v_w_in, v_conv_qkv, v_a_log, v_dt_bias, v_gdn_norm, v_conv_sc, v_w_out, v_w_gate, v_w_up, v_w_down):
    bsz, seq, _ = x.shape
    lp = -(-(N_META + seq) // CH) * CH
    padf = lp - N_META - seq
    first_real = padf + N_META
    rows = bsz * lp
    nchunk = lp // CH
    tr = _div_tile(lp, 528)
    tm = _div_tile(rows, 1056)
    tm_big = _div_tile(rows, 2112)
    cx, cy, cc = _place()

    small_shard = jnp.concatenate([
        jnp.pad(meta_tokens, ((0, 0), (0, 256 - DH))),
        jnp.pad(conv_qkv[0], ((0, 0), (0, 256 - 3 * GW // N_DEV))),
        jnp.pad(conv_sc[0], ((0, 5), (0, 256 - SW // N_DEV)))], axis=0)
    g_in, g_small = _all_gather("gather_w_in", [w_in[0].T.astype(BF16), small_shard])
    my_rest = [w_out[0].astype(BF16), w_gate[0].T.astype(BF16), w_up[0].T.astype(BF16), w_down[0].astype(BF16)]
    (grp_rest,), token = _split_start("gather_rest_start", [my_rest], _chipcast_pattern, lambda s: (N_DEV,) + s,
                                      after=g_in)
    after_start = token[0:1, 0:1]

    def cols_full(g):
        return jnp.transpose(g, (1, 0, 2)).reshape(g.shape[1], -1)

    win_t = g_in.reshape(IN_W, D)
    win_t = jnp.concatenate([win_t[:BA0 + 2 * HEADS], jnp.zeros((BA_W - 2 * HEADS, D), BF16),
                             win_t[BA0 + 2 * HEADS:]], axis=0)
    meta_full = cols_full(g_small[:, 0:16, :DH])
    convq_full = cols_full(g_small[:, 16:20, :3 * GW // N_DEV])
    convs_full = cols_full(g_small[:, 20:23, :SW // N_DEV])
    alog_v = jnp.pad(a_log, ((0, 0), (HEADS, DH - 2 * HEADS)))
    dtb_v = jnp.pad(dt_bias, ((0, 0), (HEADS, DH - 2 * HEADS)))

    head_rows = jnp.concatenate([jnp.zeros((padf, D), F32), meta_full], axis=0)
    h0, u1 = _pre_norm(x, head_rows, mix_pre_norm + after_start, bsz, lp, tr)
    (proj,) = _mm("proj_in", [(u1, win_t)], "nt", tm_big, 768, D, out_dtypes=(BF16,))
    (ba,) = _mm("proj_gate_logits", [(u1, win_t[BA0:SX0])], "nt", tm, BA_W, D)
    qkv = _gdn_prep(proj, convq_full, bsz, lp, padf)
    gb, gcs = _gates(ba, alog_v, dtb_v, bsz, lp, padf)

    def lane_bcast(cols):
        t = jnp.transpose(cols.reshape(bsz, lp, HEADS), (0, 2, 1)).reshape(bsz * HEADS, lp, 1)
        return jnp.broadcast_to(t, (bsz * HEADS, lp, DH))

    bb = lane_bcast(gb[:, 0:HEADS])
    gcb = lane_bcast(gcs[:, HEADS:2 * HEADS])
    gct = jnp.transpose(gcs[:, HEADS:2 * HEADS].reshape(bsz, nchunk, CH, HEADS), (0, 3, 1, 2)).reshape(
        bsz * HEADS, nchunk, 1, CH)
    u_loc, l1, l2, l3, w_t, k_dec, tinv_all, g_last = _gdn_local_fwd(qkv, gcb, bb, gct, bsz, lp)
    o, v_new, states = _gdn_scan_fwd(u_loc, l1, l2, g_last, bsz, lp)
    o, v_new = o.reshape(rows, GW), v_new.reshape(rows, GW)
    _, l_rest = _split_wait("gather_rest_arrive", grp_rest, _chipcast_pattern, o)
    (grp_out, grp_ffn, grp_down), token = _split_start(
        "gather_rest_forward", [l_rest[0:1], l_rest[1:3], l_rest[3:4]], _forward_pattern, None)
    mix_g, mix_s = _mix_heads(proj, o, gdn_norm + token[0:1, :], convs_full, bsz, lp, padf)
    mixin = jnp.concatenate([mix_g, mix_s], axis=1)
    _, (l_out,) = _split_wait("gather_w_out_wait", grp_out, _forward_pattern, mixin)
    wout = l_out.reshape(D, D)
    (mix,) = _mm("proj_out", [(mixin, wout)], "nn", tm, D, D, out_dtypes=(BF16,))
    h1, u2 = _mid_norms(h0, mix, mix_post_norm, ffn_pre_norm, bsz, lp, tr)
    _, (l_gate, l_up) = _split_wait("gather_ffn_wait", grp_ffn, _forward_pattern, u2)
    wgate_t, wup_t = l_gate.reshape(FF, D), l_up.reshape(FF, D)
    gate, up, act = _ffn_up(u2, wgate_t, wup_t, _div_tile(rows, 528), 1408)
    _, (l_down,) = _split_wait("gather_down_wait", grp_down, _forward_pattern, act)
    wdown = l_down.reshape(FF, D)
    (ffn,) = _mm("ffn_down", [(act, wdown)], "nn", tm, D, 1408, out_dtypes=(BF16,))

    dy, dffn, d_g4, loss_part = _loss_head(h1, ffn, loss_target, ffn_post_norm, bsz, lp, tr, first_real)
    (dwdown,) = _mm("dw_down", [(act, dffn)], "tn", 1408, D, tm, out_dtypes=(BF16,))

    def swiglu_bwd(da, g, u):
        g, u = g.astype(F32), u.astype(F32)
        return da * u * _dsilu(g), da * _silu(g)

    dgate, dup = _mm("d_act", [(dffn, wdown)], "nt", tm, 1408, D, out_dtypes=(BF16, BF16),
                     epilogue=swiglu_bwd, extras=(gate, up))
    (dwgate_t,) = _mm("dw_gate", [(dgate, u2)], "tn", 1408, D, tm, out_dtypes=(BF16,))
    (dwup_t,) = _mm("dw_up", [(dup, u2)], "tn", 1408, D, tm, out_dtypes=(BF16,))
    (du2,) = _mm("d_u2", [(dgate, wgate_t), (dup, wup_t)], "nn", tm, D, 1408, out_dtypes=(BF16,))
    dh1, dmix, d_g3, d_g2 = _mid_norms_bwd(h1, mix, du2, dy, ffn_pre_norm, mix_post_norm, bsz, lp, tr)
    (dwout,) = _mm("dw_out", [(mixin, dmix)], "tn", D, D, tm, out_dtypes=(BF16,))

    def row_blocks(g):
        return g.reshape(4, 2, -1, g.shape[1])

    core = jnp.reshape(cc, (1,)).astype(jnp.int32)
    chip = jnp.reshape(2 * cx + cy, (1,)).astype(jnp.int32)

    def reduce_start(tag, grads):
        (grp,), tok = _split_start("rs_sibling_start_" + tag, [grads], _sibling_pattern, lambda s: s[:1] + s[2:])
        return grp, tok

    def reduce_mid(tag, grp, nms, after):
        mine, lands = _split_wait("rs_sibling_wait_" + tag, grp, _sibling_pattern, after)
        parts = [_pair_sum("pair_sum_" + nm, g, r, core) for nm, g, r in zip(nms, mine, lands)]
        (grp2,), tok = _split_start("rs_chip_start_" + tag, [parts], _chip_pattern, lambda s: (3,) + s[1:])
        return grp2, tok

    def reduce_end(tag, grp2, after):
        return _split_wait("rs_chip_wait_" + tag, grp2, _chip_pattern, after)

    names_ffn = ["w_down", "w_gate", "w_up", "w_out"]
    rs_ffn, tok = reduce_start("ffn", [row_blocks(dwdown), row_blocks(dwgate_t), row_blocks(dwup_t), row_blocks(dwout)])
    (dmixin,) = _mm("d_mixin", [(dmix, wout)], "nt", tm, D, D, out_dtypes=(BF16,), after=tok)
    d_o, dz, dsx, dsb, dsc, d_gn, d_convs = _mix_heads_bwd(proj, o, gdn_norm, convs_full, dmixin, bsz, lp, padf)
    rs_ffn, tok = reduce_mid("ffn", rs_ffn, names_ffn, d_o)
    dv_new, dstates = _gdn_scan_bwd(d_o, l3, w_t, k_dec, g_last + tok[0, 0], bsz, lp)
    dq, dk, dv, dg_b, dbeta_b = _gdn_local_bwd(qkv, gcb, bb, gct, tinv_all, states, dstates, v_new,
                                               dv_new.reshape(rows, GW), d_o, bsz, lp)
    dpqkv, d_convq = _gdn_prep_bwd(proj, convq_full, dq, dk, dv, bsz, lp, padf)
    dba, d_scal = _gates_bwd(ba, alog_v, dtb_v, dbeta_b, dg_b, gb, bsz, lp, padf)
    dproj = jnp.concatenate([dpqkv, dz, dba, dsx, dsb, dsc], axis=1)

    d_alog = d_scal[0:1, HEADS:2 * HEADS]
    d_dtb = d_scal[1:2, HEADS:2 * HEADS]
    cq_w, cs_w = 3 * GW // N_DEV, SW // N_DEV
    slab = jnp.stack([
        _pack_small([jnp.zeros_like(d_g2), d_g2, d_g3, d_g4], jnp.zeros((N_META, DH), F32),
                    d_convq[:, p * cq_w:(p + 1) * cq_w], d_convs[:, p * cs_w:(p + 1) * cs_w], d_alog, d_dtb, d_gn,
                    loss_part) for p in range(N_DEV)])
    (grp_small,), tok = _split_start("small_grads_start", [[slab]], _alltoall_pattern, lambda s: s)
    (dwin_t,) = _mm("dw_in", [(dproj, u1)], "tn", 1280, D, tm, out_dtypes=(BF16,), after=tok)
    parts_ffn, recv_ffn = reduce_end("ffn", rs_ffn, dwin_t)
    dwin_t = jnp.concatenate([dwin_t[:BA0 + 2 * HEADS], dwin_t[SX0:]], axis=0)
    rs_in, tok = reduce_start("in", [row_blocks(dwin_t)])

    big_w = {"w_in": (w_in, m_w_in, v_w_in), "w_out": (w_out, m_w_out, v_w_out), "w_gate": (w_gate, m_w_gate, v_w_gate),
             "w_up": (w_up, m_w_up, v_w_up), "w_down": (w_down, m_w_down, v_w_down)}

    def update(nm, parts, recv, after=None):
        w, m, v = big_w[nm]
        if nm in ("w_out", "w_down"):
            return _adamw_reduced("adamw_" + nm, parts, chip, recv, w[0], m[0], v[0], after)
        res = _adamw_reduced("adamw_" + nm, parts, chip, recv, w[0].T, m[0].T, v[0].T, after)
        return [t.T for t in res]

    big_out = {nm: update(nm, p, r, tok) for nm, p, r in zip(names_ffn, parts_ffn, recv_ffn)}
    ffn_done = sum(big_out[nm][1][:1, :1] for nm in names_ffn)
    rs_in, tok = reduce_mid("in", rs_in, ["w_in"], ffn_done)
    (du1,) = _mm("d_u1", [(dproj, win_t)], "nn", tm_big, D, 768, out_dtypes=(BF16,), after=tok)
    grad_x, d_g1, d_meta = _pre_norm_bwd(h0, du1, dh1, mix_pre_norm, bsz, lp, seq, padf)

    late = jnp.stack([
        jnp.concatenate([jnp.pad(d_g1, ((0, 7), (0, 0))), jnp.pad(d_meta[:, p * DH:(p + 1) * DH], ((0, 0), (0, D - DH)))],
                        axis=0) for p in range(N_DEV)])
    (grp_late,), _ = _split_start("late_grads_start", [[late]], _alltoall_pattern, lambda s: s)
    _, (slabs_local,) = _split_wait("small_grads_wait", grp_small, _alltoall_pattern, grad_x)
    _, (late_local,) = _split_wait("late_grads_wait", grp_late, _alltoall_pattern, slabs_local)
    w_small = _pack_small([mix_pre_norm, mix_post_norm, ffn_pre_norm, ffn_post_norm], meta_tokens, conv_qkv[0],
                          conv_sc[0], a_log, dt_bias, gdn_norm)
    m_small = _pack_small([m_mix_pre_norm, m_mix_post_norm, m_ffn_pre_norm, m_ffn_post_norm], m_meta_tokens,
                          m_conv_qkv[0], m_conv_sc[0], m_a_log, m_dt_bias, m_gdn_norm)
    v_small = _pack_small([v_mix_pre_norm, v_mix_post_norm, v_ffn_pre_norm, v_ffn_post_norm], v_meta_tokens,
                          v_conv_qkv[0], v_conv_sc[0], v_a_log, v_dt_bias, v_gdn_norm)
    small_out, loss = _adamw_small(slabs_local, late_local, w_small, m_small, v_small)
    parts_in, recv_in = reduce_end("in", rs_in, small_out[1]["gdn_norm"])
    big_out["w_in"] = update("w_in", parts_in[0], recv_in[0])
    names = ["w_in", "w_out", "w_gate", "w_up", "w_down"]

    order = ["meta_tokens", "mix_pre_norm", "mix_post_norm", "ffn_pre_norm", "ffn_post_norm", "w_in", "conv_qkv",
             "a_log", "dt_bias", "gdn_norm", "conv_sc", "w_out", "w_gate", "w_up", "w_down"]
    outs = [loss[0, 0], grad_x]
    for kind in range(4):
        leaves = dict(small_out[kind])
        for nm in names:
            leaves[nm] = big_out[nm][kind][None]
        outs += [leaves[nm] for nm in order]
    return tuple(outs)
```

```python
import jax
import jax.numpy as jnp
from jax import lax
from jax.experimental import pallas as pl
from jax.experimental.pallas import tpu as pltpu

F32 = jnp.float32
BF16 = jnp.bfloat16
MESH = pl.DeviceIdType.MESH

D = 1024
N_META = 16
HEADS = 4
DH = 128
GW = HEADS * DH
SW = D - GW
FF = 2816
CH = 64
KQ = 4
KS = 3
EPS = 1e-6
IN_W = 3 * GW + GW + 2 * HEADS + 3 * SW
Q0, Z0, BA0, SX0, SB0, SC0 = 0, 1536, 2048, 2304, 2816, 3328
BA_W = SX0 - BA0
N_DEV = 8
VMEM_LIMIT = 56 * 1024 * 1024

ADAM_LR, ADAM_B1, ADAM_B2, ADAM_EPS, ADAM_WD, ADAM_STEP = 0.001, 0.9, 0.999, 1e-08, 0.01, 10


def _cparams(sem):
    return pltpu.CompilerParams(dimension_semantics=sem, vmem_limit_bytes=VMEM_LIMIT)


def _div_tile(n, target, mult=16):
    best = None
    for t in range(mult, min(n, target) + 1, mult):
        if n % t == 0:
            best = t
    assert best is not None, (n, target)
    return best


def _silu(x):
    return x * jax.nn.sigmoid(x)


def _dsilu(x):
    s = jax.nn.sigmoid(x)
    return s * (1.0 + x * (1.0 - s))


def _bdot(a, b, dims):
    return lax.dot_general(a.astype(BF16), b.astype(BF16), (dims, ((), ())), preferred_element_type=F32)


NN = ((1,), (0,))
NT = ((1,), (1,))
TN = ((0,), (0,))


def _mm(name, pairs, mode, tm, tn, tk, out_dtypes=(F32,), epilogue=None, extras=(), after=None):
    a0, b0 = pairs[0]
    if mode == "nn":
        (m, k), n = a0.shape, b0.shape[1]
    elif mode == "nt":
        (m, k), n = a0.shape, b0.shape[0]
    else:
        (k, m), n = a0.shape, b0.shape[1]
    assert m % tm == 0 and n % tn == 0 and k % tk == 0, (name, m, n, k, tm, tn, tk)
    nk = k // tk
    npair = len(pairs)
    nex = len(extras)
    nout = len(out_dtypes)
    ntok = 0 if after is None else 1
    dims = {"nn": NN, "nt": NT, "tn": TN}[mode]

    def body(*refs):
        ab = refs[:2 * npair]
        ex = refs[2 * npair:2 * npair + nex]
        outs = refs[2 * npair + nex + ntok:2 * npair + nex + ntok + nout]
        acc_ref = refs[-1]
        kk = pl.program_id(2)

        part = None
        for p in range(npair):
            d = lax.dot_general(ab[2 * p][...], ab[2 * p + 1][...], (dims, ((), ())), preferred_element_type=F32)
            part = d if part is None else part + d

        def finish(acc):
            res = epilogue(acc, *[e[...] for e in ex]) if epilogue is not None else (acc,)
            for o, r in zip(outs, res):
                o[...] = r.astype(o.dtype)

        if nk == 1:
            finish(part)
        else:
            @pl.when(kk == 0)
            def _():
                acc_ref[...] = part

            @pl.when(kk > 0)
            def _():
                acc_ref[...] += part

            @pl.when(kk == nk - 1)
            def _():
                finish(acc_ref[...])

    if mode == "nn":
        a_spec = pl.BlockSpec((tm, tk), lambda i, j, q: (i, q))
        b_spec = pl.BlockSpec((tk, tn), lambda i, j, q: (q, j))
    elif mode == "nt":
        a_spec = pl.BlockSpec((tm, tk), lambda i, j, q: (i, q))
        b_spec = pl.BlockSpec((tn, tk), lambda i, j, q: (j, q))
    else:
        a_spec = pl.BlockSpec((tk, tm), lambda i, j, q: (q, i))
        b_spec = pl.BlockSpec((tk, tn), lambda i, j, q: (q, j))
    o_spec = pl.BlockSpec((tm, tn), lambda i, j, q: (i, j))
    flat = [t for pr in pairs for t in pr]
    res = pl.pallas_call(
        body, name=name,
        grid=(m // tm, n // tn, nk),
        in_specs=[a_spec, b_spec] * npair + [o_spec] * nex + [pl.BlockSpec((8, DH), lambda i, j, q: (0, 0))] * ntok,
        out_specs=[o_spec] * nout,
        out_shape=[jax.ShapeDtypeStruct((m, n), dt) for dt in out_dtypes],
        scratch_shapes=[pltpu.VMEM((tm, tn) if nk > 1 else (8, DH), F32)],
        compiler_params=_cparams(("parallel", "parallel", "arbitrary")),
    )(*flat, *extras, *([] if after is None else [after]))
    return res


def _ffn_up(u2, wg, wu, tm, tn):
    m, k = u2.shape
    n = wg.shape[0]

    def body(a_ref, g_ref, u_ref, gate_ref, up_ref, act_ref):
        a = a_ref[...]
        g = lax.dot_general(a, g_ref[...], (NT, ((), ())), preferred_element_type=F32)
        u = lax.dot_general(a, u_ref[...], (NT, ((), ())), preferred_element_type=F32)
        gate_ref[...] = g.astype(BF16)
        up_ref[...] = u.astype(BF16)
        act_ref[...] = (_silu(g) * u).astype(BF16)

    o_spec = pl.BlockSpec((tm, tn), lambda j, i: (i, j))
    w_spec = pl.BlockSpec((tn, k), lambda j, i: (j, 0))
    return pl.pallas_call(
        body, name="ffn_up", grid=(n // tn, m // tm),
        in_specs=[pl.BlockSpec((tm, k), lambda j, i: (i, 0)), w_spec, w_spec],
        out_specs=[o_spec, o_spec, o_spec],
        out_shape=[jax.ShapeDtypeStruct((m, n), BF16)] * 3,
        compiler_params=_cparams(("parallel", "parallel")),
    )(u2, wg, wu)


def _rms(x, gain):
    r = lax.rsqrt(jnp.mean(x * x, axis=-1, keepdims=True) + EPS)
    return x * r * gain, r


def _rms_bwd(x, gain, dy):
    r = lax.rsqrt(jnp.mean(x * x, axis=-1, keepdims=True) + EPS)
    dyw = dy * gain
    dx = r * dyw - x * (r * r * r) * jnp.mean(dyw * x, axis=-1, keepdims=True)
    return dx, dy * x * r


def _row_specs(tr, width, nb):
    return pl.BlockSpec((tr, width), lambda b, j: (b * nb + j, 0))


def _vec_spec(width):
    return pl.BlockSpec((1, width), lambda b, j: (0, 0))


def _first_step(b, j):
    return jnp.logical_and(b == 0, j == 0)


def _pre_norm(x, head_rows, gain, bsz, lp, tr):
    nb = lp // tr
    first_real = head_rows.shape[0]
    assert first_real % 8 == 0 and first_real < tr
    x_spec = pl.BlockSpec((pl.Squeezed(), pl.Element(tr), pl.Element(D)),
                          lambda b, j: (b, pl.multiple_of(jnp.maximum(j * tr - first_real, 0), 8), 0))

    def body(x_ref, hd_ref, g_ref, h_ref, u_ref):
        j = pl.program_id(1)
        xt = x_ref[...]
        h = jnp.where(j == 0, jnp.concatenate([hd_ref[...], xt[:tr - first_real]], axis=0), xt)
        h_ref[...] = h
        u_ref[...] = _rms(h, g_ref[...])[0].astype(BF16)

    rs = _row_specs(tr, D, nb)
    return pl.pallas_call(
        body, name="mix_pre_norm", grid=(bsz, nb),
        in_specs=[x_spec, pl.BlockSpec((first_real, D), lambda b, j: (0, 0)), _vec_spec(D)],
        out_specs=[rs, rs],
        out_shape=[jax.ShapeDtypeStruct((bsz * lp, D), F32), jax.ShapeDtypeStruct((bsz * lp, D), BF16)],
        compiler_params=_cparams(("parallel", "parallel")),
    )(x, head_rows, gain)


def _mid_norms(h0, mix, g_post, g_pre, bsz, lp, tr):
    nb = lp // tr

    def body(h_ref, m_ref, gp_ref, gq_ref, h1_ref, u2_ref):
        h1 = h_ref[...] + _rms(m_ref[...].astype(F32), gp_ref[...])[0]
        h1_ref[...] = h1
        u2_ref[...] = _rms(h1, gq_ref[...])[0].astype(BF16)

    rs = _row_specs(tr, D, nb)
    return pl.pallas_call(
        body, name="mid_norms", grid=(bsz, nb),
        in_specs=[rs, rs, _vec_spec(D), _vec_spec(D)],
        out_specs=[rs, rs],
        out_shape=[jax.ShapeDtypeStruct(h0.shape, F32), jax.ShapeDtypeStruct(h0.shape, BF16)],
        compiler_params=_cparams(("parallel", "parallel")),
    )(h0, mix, g_post, g_pre)


def _loss_head(h1, ffn, tgt, g_post, bsz, lp, tr, first_real):
    nb = lp // tr
    assert first_real % 8 == 0 and first_real < tr
    tgt_spec = pl.BlockSpec((pl.Squeezed(), pl.Element(tr), pl.Element(D)),
                            lambda b, j: (b, pl.multiple_of(jnp.maximum(j * tr - first_real, 0), 8), 0))

    def body(h_ref, f_ref, t_ref, g_ref, dy_ref, dffn_ref, dg_ref, loss_ref):
        b, j = pl.program_id(0), pl.program_id(1)
        f = f_ref[...].astype(F32)
        g = g_ref[...]
        y = h_ref[...] + _rms(f, g)[0]
        t = j * tr + lax.broadcasted_iota(jnp.int32, (tr, 1), 0)
        tg = t_ref[...]
        tg = jnp.where(j == 0, jnp.concatenate([tg[tr - first_real:], tg[:tr - first_real]], axis=0), tg)
        err = jnp.where(t >= first_real, y - tg, 0.0)
        dy = err * (1.0 / D)
        dy_ref[...] = dy
        dx, dgc = _rms_bwd(f, g, dy)
        dffn_ref[...] = dx.astype(BF16)
        lsum = jnp.sum(jnp.sum(err * err, axis=0, keepdims=True), axis=1, keepdims=True) * (0.5 / D)

        @pl.when(_first_step(b, j))
        def _():
            dg_ref[...] = jnp.zeros_like(dg_ref)
            loss_ref[...] = jnp.zeros_like(loss_ref)

        dg_ref[...] += jnp.sum(dgc, axis=0, keepdims=True)
        loss_ref[...] += jnp.broadcast_to(lsum, loss_ref.shape)

    rs = _row_specs(tr, D, nb)
    return pl.pallas_call(
        body, name="loss_head", grid=(bsz, nb),
        in_specs=[rs, rs, tgt_spec, _vec_spec(D)],
        out_specs=[rs, rs, _vec_spec(D), _vec_spec(DH)],
        out_shape=[jax.ShapeDtypeStruct(h1.shape, F32), jax.ShapeDtypeStruct(h1.shape, BF16),
                   jax.ShapeDtypeStruct((1, D), F32), jax.ShapeDtypeStruct((1, DH), F32)],
        compiler_params=_cparams(("arbitrary", "arbitrary")),
    )(h1, ffn, tgt, g_post)


def _mid_norms_bwd(h1, mix, du2, dy, g_pre, g_post, bsz, lp, tr):
    nb = lp // tr

    def body(h_ref, m_ref, du_ref, dy_ref, gq_ref, gp_ref, dh1_ref, dmix_ref, dgq_ref, dgp_ref):
        b, j = pl.program_id(0), pl.program_id(1)
        dx, dgq = _rms_bwd(h_ref[...], gq_ref[...], du_ref[...].astype(F32))
        dh1 = dy_ref[...] + dx
        dh1_ref[...] = dh1
        dm, dgp = _rms_bwd(m_ref[...].astype(F32), gp_ref[...], dh1)
        dmix_ref[...] = dm.astype(BF16)

        @pl.when(_first_step(b, j))
        def _():
            dgq_ref[...] = jnp.zeros_like(dgq_ref)
            dgp_ref[...] = jnp.zeros_like(dgp_ref)

        dgq_ref[...] += jnp.sum(dgq, axis=0, keepdims=True)
        dgp_ref[...] += jnp.sum(dgp, axis=0, keepdims=True)

    rs = _row_specs(tr, D, nb)
    return pl.pallas_call(
        body, name="mid_norms_bwd", grid=(bsz, nb),
        in_specs=[rs, rs, rs, rs, _vec_spec(D), _vec_spec(D)],
        out_specs=[rs, rs, _vec_spec(D), _vec_spec(D)],
        out_shape=[jax.ShapeDtypeStruct(h1.shape, F32), jax.ShapeDtypeStruct(h1.shape, BF16),
                   jax.ShapeDtypeStruct((1, D), F32), jax.ShapeDtypeStruct((1, D), F32)],
        compiler_params=_cparams(("arbitrary", "arbitrary")),
    )(h1, mix, du2, dy, g_pre, g_post)


def _pre_norm_bwd(h0, du1, dh1, gain, bsz, lp, seq, padf):
    first_real = lp - seq
    tx = _div_tile(seq, 512)
    assert first_real % 8 == 0 and lp % first_real == 0

    def tokens(h_ref, du_ref, dh1_ref, g_ref, gx_ref, dg_ref):
        b, j = pl.program_id(0), pl.program_id(1)
        dx, dgc = _rms_bwd(h_ref[...], g_ref[...], du_ref[...].astype(F32))
        gx_ref[...] = dh1_ref[...] + dx

        @pl.when(_first_step(b, j))
        def _():
            dg_ref[...] = jnp.zeros_like(dg_ref)

        dg_ref[...] += jnp.sum(dgc, axis=0, keepdims=True)

    window = pl.BlockSpec((pl.Element(tx), pl.Element(D)),
                          lambda b, j: (pl.multiple_of(b * lp + first_real + j * tx, 16), 0))
    assert first_real % 16 == 0 and tx % 16 == 0 and lp % 16 == 0
    grad_x, dg_tokens = pl.pallas_call(
        tokens, name="pre_norm_bwd", grid=(bsz, seq // tx),
        in_specs=[window, window, window, _vec_spec(D)],
        out_specs=[pl.BlockSpec((pl.Squeezed(), tx, D), lambda b, j: (b, j, 0)), _vec_spec(D)],
        out_shape=[jax.ShapeDtypeStruct((bsz, seq, D), F32), jax.ShapeDtypeStruct((1, D), F32)],
        compiler_params=_cparams(("arbitrary", "arbitrary")),
    )(h0, du1, dh1, gain)

    def head(h_ref, du_ref, dh1_ref, g_ref, dgt_ref, dg_ref, dmeta_ref):
        b = pl.program_id(0)
        dx, dgc = _rms_bwd(h_ref[...], g_ref[...], du_ref[...].astype(F32))
        dh0 = dh1_ref[...] + dx

        @pl.when(b == 0)
        def _():
            dg_ref[...] = dgt_ref[...]
            dmeta_ref[...] = jnp.zeros_like(dmeta_ref)

        dg_ref[...] += jnp.sum(dgc, axis=0, keepdims=True)
        dmeta_ref[...] += dh0[padf:padf + N_META, :]

    rows = pl.BlockSpec((first_real, D), lambda b: (b * (lp // first_real), 0))
    vec = pl.BlockSpec((1, D), lambda b: (0, 0))
    dg, dmeta = pl.pallas_call(
        head, name="pre_norm_bwd_head", grid=(bsz,),
        in_specs=[rows, rows, rows, vec, vec],
        out_specs=[vec, pl.BlockSpec((N_META, D), lambda b: (0, 0))],
        out_shape=[jax.ShapeDtypeStruct((1, D), F32), jax.ShapeDtypeStruct((N_META, D), F32)],
        compiler_params=_cparams(("arbitrary",)),
    )(h0, du1, dh1, gain, dg_tokens)
    return grad_x, dg, dmeta


def _shift_down(x, s, lp):
    return x if s == 0 else pltpu.roll(x, s, axis=0)


def _shift_up(x, s, lp):
    return x if s == 0 else pltpu.roll(x, lp - s, axis=0)


def _conv_fwd(x, w, taps, lp):
    y = None
    for i in range(taps):
        term = _shift_down(x, taps - 1 - i, lp) * w[i:i + 1, :]
        y = term if y is None else y + term
    return y


def _conv_bwd(x, w, dy, taps, lp):
    dx = None
    dws = []
    for i in range(taps):
        s = taps - 1 - i
        term = _shift_up(dy, s, lp) * w[i:i + 1, :]
        dx = term if dx is None else dx + term
        dws.append(jnp.sum(dy * _shift_down(x, s, lp), axis=0, keepdims=True))
    return dx, jnp.concatenate(dws, axis=0)


def _slab(lp, col0):
    return pl.BlockSpec((lp, DH), lambda b, s: (b, col0 // DH + s))


def _qkv_act(c, s):
    y = _silu(c)
    r = lax.rsqrt(jnp.sum(y * y, axis=-1, keepdims=True) + EPS)
    scale = jnp.where(s < HEADS, DH ** -0.5, 1.0)
    return y, r, scale


def _gdn_prep(proj, conv_qkv, bsz, lp, padf):
    def body(p_ref, w_ref, o_ref):
        s = pl.program_id(1)
        c = _conv_fwd(p_ref[...].astype(F32), w_ref[...], KQ, lp)
        y, r, scale = _qkv_act(c, s)
        out = jnp.where(s < 2 * HEADS, y * (r * scale), y)
        t = lax.broadcasted_iota(jnp.int32, (lp, 1), 0)
        o_ref[...] = jnp.where(t >= padf, out, 0.0).astype(BF16)

    return pl.pallas_call(
        body, name="gdn_prep", grid=(bsz, 3 * HEADS),
        in_specs=[_slab(lp, Q0), pl.BlockSpec((KQ, DH), lambda b, s: (0, s))],
        out_specs=_slab(lp, 0),
        out_shape=jax.ShapeDtypeStruct((bsz * lp, 3 * GW), BF16),
        compiler_params=_cparams(("parallel", "parallel")),
    )(proj, conv_qkv)


def _gdn_prep_bwd(proj, conv_qkv, dq, dk, dv, bsz, lp, padf):
    def body(p_ref, w_ref, dq_ref, dk_ref, dv_ref, dp_ref, dw_ref):
        b, s = pl.program_id(0), pl.program_id(1)
        x = p_ref[...].astype(F32)
        w = w_ref[...]
        c = _conv_fwd(x, w, KQ, lp)
        y, r, scale = _qkv_act(c, s)
        d = jnp.where(s < HEADS, dq_ref[...], jnp.where(s < 2 * HEADS, dk_ref[...], dv_ref[...])).astype(F32)
        dn = d * scale
        n = y * r
        dy_norm = r * (dn - n * jnp.sum(dn * n, axis=-1, keepdims=True))
        dyy = jnp.where(s < 2 * HEADS, dy_norm, dn)
        t = lax.broadcasted_iota(jnp.int32, (lp, 1), 0)
        dc = jnp.where(t >= padf, dyy * _dsilu(c), 0.0)
        dx, dw = _conv_bwd(x, w, dc, KQ, lp)
        dp_ref[...] = jnp.where(t >= padf, dx, 0.0).astype(BF16)

        @pl.when(jnp.logical_and(b == 0, s == 0))
        def _():
            dw_ref[...] = jnp.zeros_like(dw_ref)

        col_block = jnp.right_shift(lax.broadcasted_iota(jnp.int32, (KQ, 3 * GW), 1), DH.bit_length() - 1)
        dw_ref[...] += jnp.where(col_block == s, jnp.tile(dw, (1, 3 * HEADS)), 0.0)

    sl = lambda col0: pl.BlockSpec((lp, DH), lambda b, s: (b, col0 // DH + s))
    part = lambda k: pl.BlockSpec((lp, DH), lambda b, s: (b, jnp.clip(s - k * HEADS, 0, HEADS - 1)))
    return pl.pallas_call(
        body, name="gdn_prep_bwd", grid=(bsz, 3 * HEADS),
        in_specs=[sl(Q0), pl.BlockSpec((KQ, DH), lambda b, s: (0, s)), part(0), part(1), part(2)],
        out_specs=[sl(0), pl.BlockSpec((KQ, 3 * GW), lambda b, s: (0, 0))],
        out_shape=[jax.ShapeDtypeStruct((bsz * lp, 3 * GW), BF16), jax.ShapeDtypeStruct((KQ, 3 * GW), F32)],
        compiler_params=_cparams(("arbitrary", "arbitrary")),
    )(proj, conv_qkv, dq, dk, dv)


def _softplus(x):
    e = jnp.exp(-jnp.abs(x))
    one_e = 1.0 + e
    l1p = jnp.where(one_e == 1.0, e, jnp.log(one_e) * (e / (one_e - 1.0)))
    return jnp.maximum(x, 0.0) + l1p


def _tri(n, kind):
    i = lax.broadcasted_iota(jnp.int32, (n, n), 0)
    j = lax.broadcasted_iota(jnp.int32, (n, n), 1)
    return {"incl": i >= j, "strict": i > j, "upper": i <= j}[kind]


def _gates(proj, alog_v, dtb_v, bsz, lp, padf):
    nchunk = lp // CH

    def body(p_ref, al_ref, dt_ref, gb_ref, cs_ref):
        x = p_ref[...].astype(F32)
        lane = lax.broadcasted_iota(jnp.int32, (lp, DH), 1)
        t = lax.broadcasted_iota(jnp.int32, (lp, DH), 0)
        beta = jax.nn.sigmoid(x)
        g = -jnp.exp(al_ref[...]) * _softplus(x + dt_ref[...])
        gb = jnp.where(lane < HEADS, beta, jnp.where(lane < 2 * HEADS, g, 0.0))
        gb_ref[...] = jnp.where(t >= padf, gb, 0.0)
        ltri = _tri(CH, "incl").astype(BF16)
        rest = [gb_ref[pl.ds(n * CH, CH), :] for n in range(nchunk)]
        sums = None
        for _ in range(3):
            piece = [v.astype(BF16) for v in rest]
            rest = [v - p.astype(F32) for v, p in zip(rest, piece)]
            part = [jnp.dot(ltri, p, preferred_element_type=F32) for p in piece]
            sums = part if sums is None else [a + b for a, b in zip(sums, part)]
        for n in range(nchunk):
            cs_ref[pl.ds(n * CH, CH), :] = sums[n]

    blk = pl.BlockSpec((lp, DH), lambda b: (b, 0))
    vec = pl.BlockSpec((1, DH), lambda b: (0, 0))
    out = pl.BlockSpec((lp, DH), lambda b: (b, 0))
    return pl.pallas_call(
        body, name="gates", grid=(bsz,),
        in_specs=[blk, vec, vec], out_specs=[out, out],
        out_shape=[jax.ShapeDtypeStruct((bsz * lp, DH), F32)] * 2,
        compiler_params=_cparams(("parallel",)),
    )(proj, alog_v, dtb_v)


def _gates_bwd(proj, alog_v, dtb_v, dbeta_b, dg_b, gb, bsz, lp, padf):
    def body(p_ref, al_ref, dt_ref, db_ref, dg_ref, gb_ref, dl_ref, dv_ref):
        b = pl.program_id(0)
        x = p_ref[...].astype(F32)
        lane = lax.broadcasted_iota(jnp.int32, (lp, DH), 1)
        t = lax.broadcasted_iota(jnp.int32, (lp, DH), 0)
        dbeta = jnp.zeros((lp, DH), F32)
        dg = jnp.zeros((lp, DH), F32)
        for h in range(HEADS):
            dbeta = jnp.where(lane == h, db_ref[h], dbeta)
            dg = jnp.where(lane == HEADS + h, dg_ref[h], dg)
        real = t >= padf
        dbeta = jnp.where(real, dbeta, 0.0)
        dg = jnp.where(real, dg, 0.0)
        sb = jax.nn.sigmoid(x)
        neg_ea = -jnp.exp(al_ref[...])
        dsp = dg * neg_ea * jax.nn.sigmoid(x + dt_ref[...])
        dl = dbeta * sb * (1.0 - sb) + dsp
        dl_ref[...] = jnp.concatenate([dl, jnp.zeros((lp, BA_W - DH), F32)], axis=1).astype(BF16)
        dalog = jnp.sum(dg * gb_ref[...], axis=0, keepdims=True)
        ddt = jnp.sum(dsp, axis=0, keepdims=True)

        @pl.when(b == 0)
        def _():
            dv_ref[...] = jnp.zeros_like(dv_ref)

        dv_ref[...] += jnp.concatenate([dalog, ddt], axis=0)

    blk = pl.BlockSpec((lp, DH), lambda b: (b, 0))
    vec = pl.BlockSpec((1, DH), lambda b: (0, 0))
    hb = pl.BlockSpec((HEADS, lp, DH), lambda b: (b, 0, 0))
    return pl.pallas_call(
        body, name="gates_bwd", grid=(bsz,),
        in_specs=[blk, vec, vec, hb, hb, pl.BlockSpec((lp, DH), lambda b: (b, 0))],
        out_specs=[pl.BlockSpec((lp, BA_W), lambda b: (b, 0)), pl.BlockSpec((2, DH), lambda b: (0, 0))],
        out_shape=[jax.ShapeDtypeStruct((bsz * lp, BA_W), BF16), jax.ShapeDtypeStruct((2, DH), F32)],
        compiler_params=_cparams(("arbitrary",)),
    )(proj, alog_v, dtb_v, dbeta_b, dg_b, gb)


def _mix_heads(proj, o, gdn_norm, conv_sc, bsz, lp, padf):
    def body(o_ref, z_ref, gn_ref, sx_ref, sb_ref, sc_ref, w_ref, og_ref, os_ref):
        og_ref[...] = (_rms(o_ref[...].astype(F32), gn_ref[...])[0] * _silu(z_ref[...].astype(F32))).astype(BF16)
        conv = _conv_fwd(sc_ref[...].astype(F32) * sx_ref[...].astype(F32), w_ref[...], KS, lp)
        t = lax.broadcasted_iota(jnp.int32, (lp, 1), 0)
        os_ref[...] = jnp.where(t >= padf, sb_ref[...].astype(F32) * conv, 0.0).astype(BF16)

    half = _slab(lp, 0)
    return pl.pallas_call(
        body, name="mix_heads", grid=(bsz, HEADS),
        in_specs=[half, _slab(lp, Z0), pl.BlockSpec((1, DH), lambda b, s: (0, 0)),
                  _slab(lp, SX0), _slab(lp, SB0), _slab(lp, SC0), pl.BlockSpec((KS, DH), lambda b, s: (0, s))],
        out_specs=[half, half],
        out_shape=[jax.ShapeDtypeStruct((bsz * lp, GW), BF16)] * 2,
        compiler_params=_cparams(("parallel", "parallel")),
    )(o, proj, gdn_norm, proj, proj, proj, conv_sc)


def _mix_heads_bwd(proj, o, gdn_norm, conv_sc, dmixin, bsz, lp, padf):
    def body(o_ref, z_ref, gn_ref, sx_ref, sb_ref, sc_ref, w_ref, dg_ref, ds_ref,
             do_ref, dz_ref, dsx_ref, dsb_ref, dsc_ref, dgn_ref, dw_ref):
        s, b = pl.program_id(0), pl.program_id(1)
        t = lax.broadcasted_iota(jnp.int32, (lp, 1), 0)
        real = t >= padf
        o, z, gn, d = o_ref[...].astype(F32), z_ref[...].astype(F32), gn_ref[...], dg_ref[...].astype(F32)
        sz = _silu(z)
        on = _rms(o, gn)[0]
        dz_ref[...] = (d * on * _dsilu(z)).astype(BF16)
        dox, dgn = _rms_bwd(o, gn, d * sz)
        do_ref[...] = dox.astype(BF16)
        sx, sb, sc = sx_ref[...].astype(F32), sb_ref[...].astype(F32), sc_ref[...].astype(F32)
        w, e = w_ref[...], ds_ref[...].astype(F32)
        e = jnp.where(real, e, 0.0)
        xin = sc * sx
        conv = _conv_fwd(xin, w, KS, lp)
        dsb_ref[...] = (e * conv).astype(BF16)
        dxin, dw = _conv_bwd(xin, w, e * sb, KS, lp)
        dxin = jnp.where(real, dxin, 0.0)
        dsx_ref[...] = (dxin * sc).astype(BF16)
        dsc_ref[...] = (dxin * sx).astype(BF16)

        @pl.when(jnp.logical_and(s == 0, b == 0))
        def _():
            dgn_ref[...] = jnp.zeros_like(dgn_ref)

        @pl.when(b == 0)
        def _():
            dw_ref[...] = jnp.zeros_like(dw_ref)

        dgn_ref[...] += jnp.sum(dgn, axis=0, keepdims=True)
        dw_ref[...] += dw

    sl = lambda col0: pl.BlockSpec((lp, DH), lambda s, b: (b, col0 // DH + s))
    half = sl(0)
    return pl.pallas_call(
        body, name="mix_heads_bwd", grid=(HEADS, bsz),
        in_specs=[half, sl(Z0), pl.BlockSpec((1, DH), lambda s, b: (0, 0)), sl(SX0), sl(SB0), sl(SC0),
                  pl.BlockSpec((KS, DH), lambda s, b: (0, s)), sl(0), sl(GW)],
        out_specs=[half] * 5 + [pl.BlockSpec((1, DH), lambda s, b: (0, 0)), pl.BlockSpec((KS, DH), lambda s, b: (0, s))],
        out_shape=[jax.ShapeDtypeStruct((bsz * lp, GW), BF16)] * 5
        + [jax.ShapeDtypeStruct((1, DH), F32), jax.ShapeDtypeStruct((KS, SW), F32)],
        compiler_params=_cparams(("arbitrary", "arbitrary")),
    )(o, proj, gdn_norm, proj, proj, proj, conv_sc, dmixin, dmixin)


def _split_dot(a, b, dims=NN):
    ah = a.astype(BF16)
    al = (a - ah.astype(F32)).astype(BF16)
    bh = b.astype(BF16)
    bl = (b - bh.astype(F32)).astype(BF16)

    def d(x, y):
        return lax.dot_general(x, y, (dims, ((), ())), preferred_element_type=F32)

    return d(ah, bh) + (d(ah, bl) + d(al, bh))


def _eye(n):
    return (lax.broadcasted_iota(jnp.int32, (n, n), 0) == lax.broadcasted_iota(jnp.int32, (n, n), 1)).astype(F32)


def _each(f, *lists):
    return [f(*xs) for xs in zip(*lists)]


def _inv_unit_lower(a_list):
    eye = _eye(CH)
    p = _each(lambda a: -a, a_list)
    t = _each(lambda x: eye + x, p)
    p = _each(lambda x: _split_dot(x, x), p)
    for level in range(5):
        if level < 4:
            prod = _each(lambda tt, pp: _split_dot(jnp.concatenate([tt, pp], axis=0), pp), t, p)
            t = _each(lambda tt, pr: tt + pr[:CH], t, prod)
            p = _each(lambda pr: pr[CH:], prod)
        else:
            t = _each(lambda tt, pp: tt + _split_dot(tt, pp), t, p)
    return t


def _chunk_gates(gc_ref, be_ref, gt_ref, g):
    rows = pl.ds(g * CH, CH)
    gc = gc_ref[0, rows, :]
    be = be_ref[0, rows, :]
    gcl = gc_ref[0, pl.ds(g * CH + CH - 1, 1), :]
    gt = gt_ref[0, g]
    incl = _tri(CH, "incl")
    dec = jnp.where(incl, jnp.exp(jnp.where(incl, gc[:, :CH] - gt, 0.0)), 0.0)
    return rows, be, dec, jnp.exp(gc), jnp.exp(gcl - gc), jnp.exp(gcl)


def _chunk_group(nchunk):
    return 3 if nchunk % 3 == 0 else 1


def _local_group(nchunk):
    return 11 if nchunk % 11 == 0 else _chunk_group(nchunk)


def _gdn_local_specs(lp, grp):
    ngrp = lp // (grp * CH)
    head = lambda col0: pl.BlockSpec((grp * CH, DH), lambda i, j: ((i // HEADS) * ngrp + j, col0 // DH + i % HEADS))
    lane_b = pl.BlockSpec((1, grp * CH, DH), lambda i, j: (i, j, 0))
    per_chunk = lambda r, c: pl.BlockSpec((1, grp, r, c), lambda i, j: (i, j, 0, 0))
    return ngrp, head, lane_b, per_chunk


def _gdn_local_fwd(qkv, gcb, bb, gct, bsz, lp):
    nchunk = lp // CH
    grp = _local_group(nchunk)
    ngrp, head, lane_b, per_chunk = _gdn_local_specs(lp, grp)
    bh = bsz * HEADS

    def body(q_ref, k_ref, v_ref, gc_ref, be_ref, gt_ref,
             u_ref, l1_ref, l2_ref, l3_ref, wt_ref, kd_ref, ti_ref, gl_ref):
        incl, strict = _tri(CH, "incl"), _tri(CH, "strict")
        rows, be, dec, eg, ekd, gl = zip(*[_chunk_gates(gc_ref, be_ref, gt_ref, g) for g in range(grp)])
        q = [q_ref[r, :].astype(F32) for r in rows]
        k = [k_ref[r, :].astype(F32) for r in rows]
        v = [v_ref[r, :].astype(F32) for r in rows]
        kb = _each(lambda x, y: x * y, k, be)
        a = _each(lambda x, y, d: jnp.where(strict, _bdot(x, y, NT) * d, 0.0), kb, k, dec)
        tinv = _inv_unit_lower(a)
        u = _each(lambda t, x, y: _bdot(t, x * y, NN), tinv, v, be)
        w = _each(lambda t, x, y: _bdot(t, x * y, NN), tinv, kb, eg)
        qk = _each(lambda x, y, d: jnp.where(incl, _bdot(x, y, NT) * d, 0.0), q, k, dec)
        qk_t = _each(lambda x: x.T, qk)
        qd = _each(lambda x, y: x * y, q, eg)
        kd = _each(lambda x, y: x * y, k, ekd)
        kd_t = _each(lambda x: x.T, kd)
        qd_t = _each(lambda x: x.T, qd)
        w_t = _each(lambda x: x.T, w)
        for g in range(grp):
            u_ref[rows[g], :] = u[g]
            l1_ref[0, g] = jnp.concatenate([w[g], qd[g]], axis=0).astype(BF16)
            l2_ref[0, g] = jnp.concatenate([qk[g], kd_t[g]], axis=0).astype(BF16)
            l3_ref[0, g] = jnp.concatenate([qk_t[g], qd_t[g]], axis=0).astype(BF16)
            wt_ref[0, g] = w_t[g].astype(BF16)
            kd_ref[0, g] = kd[g].astype(BF16)
            ti_ref[0, g] = tinv[g]
            gl_ref[0, g] = gl[g]

    r = bsz * lp
    shapes = [((r, GW), F32), ((bh, nchunk, 2 * CH, DH), BF16), ((bh, nchunk, 3 * CH, CH), BF16),
              ((bh, nchunk, 3 * CH, CH), BF16), ((bh, nchunk, DH, CH), BF16), ((bh, nchunk, CH, DH), BF16),
              ((bh, nchunk, CH, CH), F32), ((bh, nchunk, 1, DH), F32)]
    return pl.pallas_call(
        body, name="gdn_local_fwd", grid=(bh, ngrp),
        in_specs=[head(0), head(GW), head(2 * GW), lane_b, lane_b, per_chunk(1, CH)],
        out_specs=[head(0), per_chunk(2 * CH, DH), per_chunk(3 * CH, CH), per_chunk(3 * CH, CH), per_chunk(DH, CH),
                   per_chunk(CH, DH), per_chunk(CH, CH), per_chunk(1, DH)],
        out_shape=[jax.ShapeDtypeStruct(s, dt) for s, dt in shapes],
        compiler_params=_cparams(("parallel", "parallel")),
    )(qkv, qkv, qkv, gcb, bb, gct)


def _scan_specs(bsz, lp, grp, order):
    ngrp = lp // (grp * CH)
    bh = bsz * HEADS
    seq = lambda: pl.BlockSpec((bsz, grp * CH, GW), lambda s: (0, order(s, ngrp), 0))
    per_chunk = lambda r, c: pl.BlockSpec((bh, grp, r, c), lambda s: (0, order(s, ngrp), 0, 0))
    return ngrp, bh, seq, per_chunk


def _gdn_scan_fwd(u, l1, l2, gl, bsz, lp):
    nchunk = lp // CH
    grp = _chunk_group(nchunk)
    ngrp, bh, seq, per_chunk = _scan_specs(bsz, lp, grp, lambda s, n: s)

    def body(u_ref, l1_ref, l2_ref, gl_ref, o_ref, vn_ref, st_ref, s_scr):
        @pl.when(pl.program_id(0) == 0)
        def _():
            s_scr[...] = jnp.zeros_like(s_scr)

        chains = list(range(bh))
        where = [(i // HEADS, slice((i % HEADS) * DH, (i % HEADS + 1) * DH)) for i in chains]

        def dot(a, b):
            return jnp.dot(a, b.astype(BF16), preferred_element_type=F32)

        for c in range(grp):
            rows = slice(c * CH, (c + 1) * CH)
            s = [s_scr[i] for i in chains]
            for i in chains:
                st_ref[i, c] = s[i]
            x = _each(lambda i, si: dot(l1_ref[i, c], si), chains, s)
            vnew = _each(lambda bc, xi: u_ref[bc[0], rows, bc[1]] - xi[:CH], where, x)
            y = _each(lambda i, vi: dot(l2_ref[i, c], vi), chains, vnew)
            for i, (b, cols) in zip(chains, where):
                o_ref[b, rows, cols] = (x[i][CH:] + y[i][:CH]).astype(BF16)
                vn_ref[b, rows, cols] = vnew[i].astype(BF16)
                s_scr[i] = s[i] * gl_ref[i, c] + y[i][CH:]

    return pl.pallas_call(
        body, name="gdn_scan_fwd", grid=(ngrp,),
        in_specs=[seq(), per_chunk(2 * CH, DH), per_chunk(3 * CH, CH), per_chunk(1, DH)],
        out_specs=[seq(), seq(), per_chunk(DH, DH)],
        out_shape=[jax.ShapeDtypeStruct((bsz, lp, GW), BF16), jax.ShapeDtypeStruct((bsz, lp, GW), BF16),
                   jax.ShapeDtypeStruct((bh, nchunk, DH, DH), F32)],
        scratch_shapes=[pltpu.VMEM((bh, DH, DH), F32)],
        compiler_params=_cparams(("arbitrary",)),
    )(u.reshape(bsz, lp, GW), l1, l2, gl)


def _gdn_scan_bwd(d_o, l3, wt, kd, gl, bsz, lp):
    nchunk = lp // CH
    grp = _chunk_group(nchunk)
    ngrp, bh, seq, per_chunk = _scan_specs(bsz, lp, grp, lambda s, n: n - 1 - s)

    def body(do_ref, l3_ref, wt_ref, kd_ref, gl_ref, dvn_ref, dst_ref, ds_scr):
        @pl.when(pl.program_id(0) == 0)
        def _():
            ds_scr[...] = jnp.zeros_like(ds_scr)

        chains = list(range(bh))
        where = [(i // HEADS, slice((i % HEADS) * DH, (i % HEADS + 1) * DH)) for i in chains]

        def dot(a, b):
            return jnp.dot(a, b.astype(BF16), preferred_element_type=F32)

        for c in reversed(range(grp)):
            rows = slice(c * CH, (c + 1) * CH)
            ds = [ds_scr[i] for i in chains]
            for i in chains:
                dst_ref[i, c] = ds[i]
            p = _each(lambda i, bc: dot(l3_ref[i, c], do_ref[bc[0], rows, bc[1]]), chains, where)
            dvn = _each(lambda i, pi, di: pi[:CH] + dot(kd_ref[i, c], di), chains, p, ds)
            wd = _each(lambda i, vi: dot(wt_ref[i, c], vi), chains, dvn)
            for i, (b, cols) in zip(chains, where):
                dvn_ref[b, rows, cols] = dvn[i].astype(BF16)
                ds_scr[i] = ds[i] * gl_ref[i, c] + p[i][CH:] - wd[i]

    return pl.pallas_call(
        body, name="gdn_scan_bwd", grid=(ngrp,),
        in_specs=[seq(), per_chunk(3 * CH, CH), per_chunk(DH, CH), per_chunk(CH, DH), per_chunk(1, DH)],
        out_specs=[seq(), per_chunk(DH, DH)],
        out_shape=[jax.ShapeDtypeStruct((bsz, lp, GW), BF16), jax.ShapeDtypeStruct((bh, nchunk, DH, DH), F32)],
        scratch_shapes=[pltpu.VMEM((bh, DH, DH), F32)],
        compiler_params=_cparams(("arbitrary",)),
    )(d_o.reshape(bsz, lp, GW), l3, wt, kd, gl)


def _gdn_local_bwd(qkv, gcb, bb, gct, tinv_all, states, dstates, vnew, dvnew, d_o, bsz, lp):
    nchunk = lp // CH
    grp = _local_group(nchunk)
    ngrp, head, lane_b, per_chunk = _gdn_local_specs(lp, grp)
    bh = bsz * HEADS

    def body(q_ref, k_ref, v_ref, gc_ref, be_ref, gt_ref, ti_ref, st_ref, dst_ref, vn_ref, dvn_ref, do_ref,
             dq_ref, dk_ref, dv_ref, dg_ref, db_ref):
        incl, strict = _tri(CH, "incl"), _tri(CH, "strict")
        upper = _tri(CH, "upper").astype(F32)
        ones = jnp.ones((CH, DH), F32)
        last = lax.broadcasted_iota(jnp.int32, (CH, 1), 0) == CH - 1

        def rsum(x):
            return jnp.sum(x, axis=-1, keepdims=True)

        def mul(xs, ys):
            return _each(lambda x, y: x * y, xs, ys)

        rows, be, dec, eg, ekd, gl = zip(*[_chunk_gates(gc_ref, be_ref, gt_ref, g) for g in range(grp)])
        q = [q_ref[r, :].astype(F32) for r in rows]
        k = [k_ref[r, :].astype(F32) for r in rows]
        v = [v_ref[r, :].astype(F32) for r in rows]
        vnew = [vn_ref[r, :] for r in rows]
        dvn = [dvn_ref[r, :] for r in rows]
        do = [do_ref[r, :] for r in rows]
        tinv = [ti_ref[0, g] for g in range(grp)]
        s = [st_ref[0, g] for g in range(grp)]
        ds = [dst_ref[0, g] for g in range(grp)]
        kb, vb, qd, kd = mul(k, be), mul(v, be), mul(q, eg), mul(k, ekd)
        kbg = mul(kb, eg)
        a = _each(lambda x, y, d: jnp.where(strict, _bdot(x, y, NT) * d, 0.0), kb, k, dec)
        qk = _each(lambda x, y, d: jnp.where(incl, _bdot(x, y, NT) * d, 0.0), q, k, dec)
        dqk = _each(lambda x, y: jnp.where(incl, _bdot(x, y, NT), 0.0), do, vnew)
        dqd = _each(lambda x, y: _bdot(x, y, NT), do, s)
        dkd = _each(lambda x, y: _bdot(x, y, NT), vnew, ds)
        dgl = _each(lambda x, y: jnp.sum(rsum(x * y), axis=0, keepdims=True), s, ds)
        dw = _each(lambda x, y: -_bdot(x, y, NT), dvn, s)
        dvb = _each(lambda t, x: _bdot(t, x, TN), tinv, dvn)
        dkbg = _each(lambda t, x: _bdot(t, x, TN), tinv, dw)
        dt = _each(lambda x, y, z, w: _bdot(x, y, NT) + _bdot(z, w, NT), dvn, vb, dw, kbg)
        tdt = _each(lambda t, x: _split_dot(t, x, TN), tinv, dt)
        da = _each(lambda x, t: jnp.where(strict, -_split_dot(x, t, NT), 0.0), tdt, tinv)
        dp, dr = mul(da, dec), mul(dqk, dec)
        dkb = _each(lambda x, y, z, e: _bdot(x, y, NN) + z * e, dp, k, dkbg, eg)
        dk = _each(lambda p_, kb_, r_, q_, dkd_, ekd_, dkb_, be_:
                   _bdot(p_, kb_, TN) + _bdot(r_, q_, TN) + dkd_ * ekd_ + dkb_ * be_,
                   dp, kb, dr, q, dkd, ekd, dkb, be)
        dq = _each(lambda r_, k_, dqd_, eg_: _bdot(r_, k_, NN) + dqd_ * eg_, dr, k, dqd, eg)
        m = _each(lambda da_, a_, dqk_, qk_: da_ * a_ + dqk_ * qk_, da, a, dqk, qk)
        m_cols = _each(lambda x: _split_dot(x, ones, TN), m)
        kd_term = _each(lambda x, y: rsum(x * y), dkd, kd)
        dgc = _each(lambda m_, mc, dqd_, qd_, kt, dkbg_, kbg_, dgl_, gl_:
                    rsum(m_) - mc + rsum(dqd_ * qd_) - kt + rsum(dkbg_ * kbg_)
                    + jnp.where(last, jnp.sum(kt, axis=0, keepdims=True) + dgl_ * gl_, 0.0),
                    m, m_cols, dqd, qd, kd_term, dkbg, kbg, dgl, gl)
        dg = _each(lambda x: _split_dot(upper, x, NN), dgc)
        for g in range(grp):
            dq_ref[rows[g], :] = dq[g].astype(BF16)
            dk_ref[rows[g], :] = dk[g].astype(BF16)
            dv_ref[rows[g], :] = (dvb[g] * be[g]).astype(BF16)
            dg_ref[0, rows[g], :] = dg[g]
            db_ref[0, rows[g], :] = jnp.broadcast_to(rsum(dkb[g] * k[g]) + rsum(dvb[g] * v[g]), (CH, DH))

    r = bsz * lp
    return pl.pallas_call(
        body, name="gdn_local_bwd", grid=(bh, ngrp),
        in_specs=[head(0), head(GW), head(2 * GW), lane_b, lane_b, per_chunk(1, CH), per_chunk(CH, CH),
                  per_chunk(DH, DH), per_chunk(DH, DH), head(0), head(0), head(0)],
        out_specs=[head(0), head(0), head(0), lane_b, lane_b],
        out_shape=[jax.ShapeDtypeStruct((r, GW), BF16)] * 3 + [jax.ShapeDtypeStruct((bh, lp, DH), F32)] * 2,
        compiler_params=_cparams(("parallel", "parallel")),
    )(qkv, qkv, qkv, gcb, bb, gct, tinv_all, states, dstates, vnew, dvnew, d_o)


def _place():
    return lax.axis_index("x"), lax.axis_index("y"), lax.axis_index("c")


def _any_specs(n):
    return [pl.BlockSpec(memory_space=pl.ANY)] * n


def _all_gather(name, shards):
    n = len(shards)

    def body(*refs):
        ins, outs = refs[:n], refs[n:2 * n]
        send_sems, recv_sems, local_sems = refs[2 * n:2 * n + 3]
        stage = refs[2 * n + 3:]
        x, y, c = _place()
        me, sibling = (x, y, c), (x, y, 1 - c)
        chips = [(1 - x, y), (x, 1 - y), (1 - x, 1 - y)]

        def slot(i, p):
            return outs[i].at[4 * p[0] + 2 * p[1] + p[2]]

        def copy(i, k, block, to, src=None):
            return pltpu.make_async_remote_copy(
                src_ref=slot(i, block) if src is None else src, dst_ref=slot(i, block),
                send_sem=send_sems.at[i, k], recv_sem=recv_sems.at[i, k], device_id=to, device_id_type=MESH)

        load = [pltpu.make_async_copy(ins[i], stage[i], local_sems.at[i]) for i in range(n)]
        for cp in load:
            cp.start()
        first = []
        for i in range(n):
            first.append(copy(i, 0, me, sibling, src=ins[i]))
            first += [copy(i, 1 + j, me, (*chip, c), src=ins[i]) for j, chip in enumerate(chips)]
        for cp in first:
            cp.start()
        for cp in load:
            cp.wait()
        mine = [pltpu.make_async_copy(stage[i], slot(i, me), local_sems.at[i]) for i in range(n)]
        for cp in mine:
            cp.start()
        passed = []
        for j, chip in enumerate(chips):
            for i in range(n):
                copy(i, 1 + j, (*chip, c), me).wait_recv()
                fwd = copy(i, 4 + j, (*chip, c), sibling)
                fwd.start()
                passed.append(fwd)
        for i in range(n):
            copy(i, 0, sibling, me).wait_recv()
            for j, chip in enumerate(chips):
                copy(i, 4 + j, (*chip, 1 - c), me).wait_recv()
        for cp in first + passed:
            cp.wait_send()
        for cp in mine:
            cp.wait()

    return pl.pallas_call(
        body, name=name,
        in_specs=_any_specs(n), out_specs=_any_specs(n),
        out_shape=[jax.ShapeDtypeStruct((N_DEV,) + s.shape, s.dtype) for s in shards],
        scratch_shapes=[pltpu.SemaphoreType.DMA((n, 7)), pltpu.SemaphoreType.DMA((n, 7)),
                        pltpu.SemaphoreType.DMA((n,))] + [pltpu.VMEM(s.shape, s.dtype) for s in shards],
    )(*shards)


HBM_SPEC = pl.BlockSpec(memory_space=pltpu.HBM)
SEM_SPEC = pl.BlockSpec(memory_space=pltpu.SEMAPHORE)
EFFECT = pltpu.SideEffectType.DATAFLOW_SIDE_EFFECTING
N_PEER = N_DEV - 1


def _peer(r):
    x, y, c = _place()
    return ((1 - x) if r & 4 else x, (1 - y) if r & 2 else y, (1 - c) if r & 1 else c)


def _slot_of(p):
    return 4 * p[0] + 2 * p[1] + p[2]


def _hbm(a):
    return pltpu.with_memory_space_constraint(a, pltpu.HBM)


def _gather_pattern(src, land):
    me = _place()
    return [(src, land.at[_slot_of(me)], _peer(r), land.at[_slot_of(_peer(r))]) for r in range(1, N_DEV)]


def _alltoall_pattern(src, land):
    me = _place()
    return [(src.at[_slot_of(_peer(r))], land.at[_slot_of(me)], _peer(r), land.at[_slot_of(_peer(r))])
            for r in range(1, N_DEV)]


def _chipcast_pattern(src, land):
    x, y, c = _place()
    me = (x, y, c)
    peers = [(x, y, 1 - c), (1 - x, y, c), (x, 1 - y, c), (1 - x, 1 - y, c)]
    return [(src, land.at[_slot_of(me)], p, land.at[_slot_of(p)]) for p in peers]


def _forward_pattern(land, _):
    x, y, c = _place()
    chips = [(1 - x, y), (x, 1 - y), (1 - x, 1 - y)]
    return [(land.at[_slot_of((*ch, c))], land.at[_slot_of((*ch, c))], (x, y, 1 - c), land.at[_slot_of((*ch, 1 - c))])
            for ch in chips]


def _sibling_pattern(src, land):
    x, y, c = _place()
    return [(src.at[ch, 1 - c], land.at[ch], (x, y, 1 - c), land.at[ch]) for ch in range(4)]


def _chip_pattern(src, land):
    x, y, c = _place()
    chips = [(1 - x, y), (x, 1 - y), (1 - x, 1 - y)]
    return [(src.at[2 * ch[0] + ch[1]], land.at[j], (*ch, c), land.at[j]) for j, ch in enumerate(chips)]


def _split_start(name, groups, pattern, land_shape, after=None):
    sizes = [len(g) for g in groups]
    arrays = [s for g in groups for s in g]
    n = len(arrays)
    inplace = land_shape is None
    fresh = [] if inplace else [lax.empty(land_shape(s.shape), s.dtype) for s in arrays]
    bufs = arrays + fresh
    order = [] if after is None else [after]

    def body(*refs):
        srcs, lands = refs[:n], (refs[:n] if inplace else refs[n:2 * n])
        first_out = len(bufs) + len(order)
        sems = refs[first_out:first_out + 2 * len(groups)]
        token = refs[-1]
        k = 0
        for gi, size in enumerate(sizes):
            for i in range(size):
                copies = pattern(srcs[k], lands[k])
                for j, (s, d, peer, _) in enumerate(copies):
                    pltpu.make_async_remote_copy(
                        src_ref=s, dst_ref=d, send_sem=sems[2 * gi].at[i * len(copies) + j],
                        recv_sem=sems[2 * gi + 1].at[i * len(copies) + j], device_id=peer, device_id_type=MESH).start()
                k += 1
        token[...] = jnp.zeros_like(token)

    per_array = {_gather_pattern: N_PEER, _alltoall_pattern: N_PEER, _chipcast_pattern: 4, _forward_pattern: 3,
                 _sibling_pattern: 4, _chip_pattern: 3}[pattern]
    sem_shapes = [pltpu.SemaphoreType.DMA((size * per_array,)) for size in sizes for _ in range(2)]
    res = pl.pallas_call(
        body, name=name,
        in_specs=[HBM_SPEC] * len(bufs) + [pl.BlockSpec(memory_space=pl.ANY)] * len(order),
        out_specs=[SEM_SPEC] * len(sem_shapes) + [HBM_SPEC] * len(bufs) + [pl.BlockSpec(memory_space=pltpu.VMEM)],
        out_shape=sem_shapes + [pltpu.HBM(s.shape, s.dtype) for s in bufs] + [jax.ShapeDtypeStruct((8, DH), F32)],
        input_output_aliases={i: len(sem_shapes) + i for i in range(len(bufs))},
        compiler_params=pltpu.CompilerParams(has_side_effects=EFFECT),
    )(*[_hbm(s) for s in bufs], *order)
    nsem = len(sem_shapes)
    thru = list(res[nsem:nsem + len(bufs)])
    out, k = [], 0
    for gi, size in enumerate(sizes):
        srcs_out = [] if inplace else thru[k:k + size]
        lands_out = thru[k:k + size] if inplace else thru[n + k:n + k + size]
        out.append((res[2 * gi], res[2 * gi + 1], srcs_out, lands_out))
        k += size
    return out, res[-1]


def _split_wait(name, group, pattern, after):
    send_sems, recv_sems, arrays, lands = group
    n = len(lands)
    inplace = not arrays
    bufs = list(arrays) + list(lands)
    fill_own = pattern in (_gather_pattern, _chipcast_pattern, _alltoall_pattern)
    own_of = (lambda a: a.at[_slot_of(_place())]) if pattern is _alltoall_pattern else (lambda a: a)
    own_shape = (lambda a: a.shape[1:]) if pattern is _alltoall_pattern else (lambda a: a.shape)

    def body(*refs):
        srcs, lands_ = refs[:n], (refs[:n] if inplace else refs[n:2 * n])
        s_sems, r_sems = refs[len(bufs)], refs[len(bufs) + 1]
        own = []
        if fill_own:
            stage, local_sems = refs[-n - 1:-1], refs[-1]
            load = [pltpu.make_async_copy(own_of(srcs[i]), stage[i], local_sems.at[i]) for i in range(n)]
            for cp in load:
                cp.start()
            for cp in load:
                cp.wait()
            own = [pltpu.make_async_copy(stage[i], lands_[i].at[_slot_of(_place())], local_sems.at[i])
                   for i in range(n)]
            for cp in own:
                cp.start()
        for i in range(n):
            copies = pattern(srcs[i], lands_[i])
            for j, (s, _, peer, filled) in enumerate(copies):
                cp = pltpu.make_async_remote_copy(
                    src_ref=s, dst_ref=filled, send_sem=s_sems.at[i * len(copies) + j],
                    recv_sem=r_sems.at[i * len(copies) + j], device_id=peer, device_id_type=MESH)
                cp.wait_send()
                cp.wait_recv()
        for cp in own:
            cp.wait()

    res = pl.pallas_call(
        body, name=name,
        in_specs=[HBM_SPEC] * len(bufs) + [SEM_SPEC, SEM_SPEC, pl.BlockSpec(memory_space=pl.ANY)],
        out_specs=[HBM_SPEC] * len(bufs),
        out_shape=[pltpu.HBM(s.shape, s.dtype) for s in bufs],
        scratch_shapes=([pltpu.VMEM(own_shape(a), a.dtype) for a in arrays] + [pltpu.SemaphoreType.DMA((n,))])
        if fill_own else [],
        input_output_aliases={i: i for i in range(len(bufs))},
        compiler_params=pltpu.CompilerParams(has_side_effects=EFFECT),
    )(*bufs, send_sems, recv_sems, after)
    return ([], list(res)) if inplace else (list(res[:n]), list(res[n:]))


def _pair_sum(name, grads, recv, core):
    _, _, r, c = grads.shape
    tr = r if r * c <= (1 << 20) or r % 16 else _div_tile(r, 512, 16)

    def body(core_ref, g_ref, r_ref, o_ref):
        o_ref[...] = (g_ref[0].astype(F32) + r_ref[...].astype(F32)).astype(o_ref.dtype)

    return pl.pallas_call(
        body, name=name,
        grid_spec=pltpu.PrefetchScalarGridSpec(
            num_scalar_prefetch=1, grid=(4, r // tr),
            in_specs=[pl.BlockSpec((1, 1, tr, c), lambda s, i, cr: (s, cr[0], i, 0)),
                      pl.BlockSpec((1, tr, c), lambda s, i, cr: (s, i, 0))],
            out_specs=pl.BlockSpec((1, tr, c), lambda s, i, cr: (s, i, 0))),
        out_shape=jax.ShapeDtypeStruct(recv.shape, recv.dtype),
        compiler_params=_cparams(("parallel", "parallel")),
    )(core, grads, recv)


def _adamw_math(w, g, m, v):
    m = ADAM_B1 * m + (1.0 - ADAM_B1) * g
    v = ADAM_B2 * v + (1.0 - ADAM_B2) * (g * g)
    m_hat = m / (1.0 - ADAM_B1 ** ADAM_STEP)
    v_hat = v / (1.0 - ADAM_B2 ** ADAM_STEP)
    delta = -ADAM_LR * (m_hat / (jnp.sqrt(v_hat) + ADAM_EPS) + ADAM_WD * w)
    return delta, m, v


SMALL_LEAVES = (
    ("mix_pre_norm", 0, 1, 0, D, (1, D)), ("mix_post_norm", 1, 1, 0, D, (1, D)),
    ("ffn_pre_norm", 2, 1, 0, D, (1, D)), ("ffn_post_norm", 3, 1, 0, D, (1, D)),
    ("meta_tokens", 8, N_META, 0, DH, (N_META, DH)),
    ("conv_qkv", 24, KQ, 0, 3 * GW // N_DEV, (1, KQ, 3 * GW // N_DEV)),
    ("conv_sc", 32, KS, 0, SW // N_DEV, (1, KS, SW // N_DEV)),
    ("a_log", 40, 1, 0, HEADS, (1, HEADS)), ("dt_bias", 40, 1, DH, HEADS, (1, HEADS)),
    ("gdn_norm", 40, 1, 2 * DH, DH, (1, DH)),
)
SMALL_LOSS_AT = (40, 3 * DH)


def _adamw_small(slabs, late, w, m, v):
    nslot = slabs.shape[0]
    nleaf = len(SMALL_LEAVES)

    def body(p_ref, l_ref, w_ref, m_ref, v_ref, *refs):
        outs, loss_ref, scr = refs[:4 * nleaf], refs[4 * nleaf], refs[4 * nleaf + 1]
        g = p_ref[0]
        gain = l_ref[0, pl.ds(0, 1), :]
        meta = l_ref[0, pl.ds(8, N_META), pl.ds(0, DH)]
        for s in range(1, nslot):
            g = g + p_ref[s]
            gain = gain + l_ref[s, pl.ds(0, 1), :]
            meta = meta + l_ref[s, pl.ds(8, N_META), pl.ds(0, DH)]
        scr[0] = g
        scr[0, pl.ds(0, 1), pl.ds(0, D)] += gain
        scr[0, pl.ds(8, N_META), pl.ds(0, DH)] += meta
        g = scr[0]
        d, mn, vn = _adamw_math(w_ref[...], g, m_ref[...], v_ref[...])
        for kind, val in enumerate((g, d, mn, vn)):
            scr[kind] = val
        for kind in range(4):
            for li, (_, r0, nr, c0, nc, shape) in enumerate(SMALL_LEAVES):
                val = scr[kind, pl.ds(r0, nr), pl.ds(c0, nc)]
                o = outs[kind * nleaf + li]
                if len(shape) == 3:
                    o[0] = val
                else:
                    o[...] = val
        loss_ref[...] = scr[0, pl.ds(SMALL_LOSS_AT[0], 1), pl.ds(SMALL_LOSS_AT[1], 1)]

    whole = lambda a: pl.BlockSpec(a.shape, lambda: (0,) * a.ndim)
    out_shapes = [jax.ShapeDtypeStruct(shape, F32) for _ in range(4) for (*_, shape) in SMALL_LEAVES]
    out_shapes.append(jax.ShapeDtypeStruct((1, 1), F32))
    res = pl.pallas_call(
        body, name="adamw_small",
        in_specs=[whole(slabs), whole(late), whole(w), whole(m), whole(v)],
        out_specs=[pl.BlockSpec(s.shape, lambda n=len(s.shape): (0,) * n) for s in out_shapes],
        out_shape=out_shapes,
        scratch_shapes=[pltpu.VMEM((4,) + w.shape, F32)],
        compiler_params=pltpu.CompilerParams(vmem_limit_bytes=VMEM_LIMIT),
    )(slabs, late, w, m, v)
    kinds = [{nm: res[kind * nleaf + li] for li, (nm, *_) in enumerate(SMALL_LEAVES)} for kind in range(4)]
    return kinds, res[-1]


def _adamw_reduced(name, parts, chip, recv, w, m, v, after=None):
    _, r, c = parts.shape
    tr = r if r * c <= (1 << 20) or r % 16 else _div_tile(r, 512, 16)
    order = [] if after is None else [after]

    def body(chip_ref, p_ref, r_ref, w_ref, m_ref, v_ref, *rest):
        g_out, d_out, m_out, v_out = rest[len(order):]
        g = p_ref[0].astype(F32)
        for s in range(3):
            g = g + r_ref[s].astype(F32)
        d, mn, vn = _adamw_math(w_ref[...], g, m_ref[...], v_ref[...])
        g_out[...] = g
        d_out[...] = d
        m_out[...] = mn
        v_out[...] = vn

    blk = pl.BlockSpec((tr, c), lambda i, ch: (i, 0))
    return pl.pallas_call(
        body, name=name,
        grid_spec=pltpu.PrefetchScalarGridSpec(
            num_scalar_prefetch=1, grid=(r // tr,),
            in_specs=[pl.BlockSpec((1, tr, c), lambda i, ch: (ch[0], i, 0)),
                      pl.BlockSpec((3, tr, c), lambda i, ch: (0, i, 0)), blk, blk, blk]
            + [pl.BlockSpec((8, DH), lambda i, ch: (0, 0))] * len(order),
            out_specs=[blk] * 4),
        out_shape=[jax.ShapeDtypeStruct((r, c), F32)] * 4,
        compiler_params=_cparams(("parallel",)),
    )(chip, parts, recv, w, m, v, *order)


SMALL_ROWS, SMALL_COLS = 48, 3 * GW


def _pack_small(gains4, meta, conv_qkv, conv_sc, a_log, dt_bias, gdn_norm, extra=None):
    def tile(a):
        return jnp.pad(a, ((0, -a.shape[0] % 8), (0, SMALL_COLS - a.shape[1])))
    scal = jnp.concatenate([jnp.pad(a_log, ((0, 0), (0, DH - HEADS))), jnp.pad(dt_bias, ((0, 0), (0, DH - HEADS))),
                            gdn_norm] + ([] if extra is None else [extra]), axis=1)
    gains = sum(jnp.pad(g, ((i, 7 - i), (0, SMALL_COLS - g.shape[1]))) for i, g in enumerate(gains4))
    parts = [gains, tile(meta), tile(conv_qkv), tile(conv_sc), tile(scal)]
    slab = jnp.concatenate(parts, axis=0)
    assert slab.shape == (SMALL_ROWS, SMALL_COLS)
    return slab


def kernel(x, meta_tokens, mix_pre_norm, mix_post_norm, ffn_pre_norm, ffn_post_norm, w_in, conv_qkv, a_log, dt_bias, gdn_norm, conv_sc, w_out, w_gate, w_up, w_down, loss_target, m_meta_tokens, m_mix_pre_norm, m_mix_post_norm, m_ffn_pre_norm, m_ffn_post_norm, m_w_in, m_conv_qkv, m_a_log, m_dt_bias, m_gdn_norm, m_conv_sc, m_w_out, m_w_gate, m_w_up, m_w_down, v_meta_tokens, v_mix_pre_norm, v_mix_post_norm, v_ffn_pre_norm, v_ffn_post_norm, v_w_in, v_conv_qkv, v_a_log, v_dt_bias, v_gdn_norm, v_conv_sc, v_w_out, v_w_gate, v_w_up, v_w_down):
    bsz, seq, _ = x.shape
    lp = -(-(N_META + seq) // CH) * CH
    padf = lp - N_META - seq
    first_real = padf + N_META
    rows = bsz * lp
    nchunk = lp // CH
    tr = _div_tile(lp, 528)
    tm = _div_tile(rows, 1056)
    tm_big = _div_tile(rows, 2112)
    cx, cy, cc = _place()

    small_shard = jnp.concatenate([
        jnp.pad(meta_tokens, ((0, 0), (0, 256 - DH))),
        jnp.pad(conv_qkv[0], ((0, 0), (0, 256 - 3 * GW // N_DEV))),
        jnp.pad(conv_sc[0], ((0, 5), (0, 256 - SW // N_DEV)))], axis=0)
    g_in, g_small = _all_gather("gather_w_in", [w_in[0].T.astype(BF16), small_shard])
    my_rest = [w_out[0].astype(BF16), w_gate[0].T.astype(BF16), w_up[0].T.astype(BF16), w_down[0].astype(BF16)]
    (grp_rest,), token = _split_start("gather_rest_start", [my_rest], _chipcast_pattern, lambda s: (N_DEV,) + s,
                                      after=g_in)
    after_start = token[0:1, 0:1]

    def cols_full(g):
        return jnp.transpose(g, (1, 0, 2)).reshape(g.shape[1], -1)

    win_t = g_in.reshape(IN_W, D)
    win_t = jnp.concatenate([win_t[:BA0 + 2 * HEADS], jnp.zeros((BA_W - 2 * HEADS, D), BF16),
                             win_t[BA0 + 2 * HEADS:]], axis=0)
    meta_full = cols_full(g_small[:, 0:16, :DH])
    convq_full = cols_full(g_small[:, 16:20, :3 * GW // N_DEV])
    convs_full = cols_full(g_small[:, 20:23, :SW // N_DEV])
    alog_v = jnp.pad(a_log, ((0, 0), (HEADS, DH - 2 * HEADS)))
    dtb_v = jnp.pad(dt_bias, ((0, 0), (HEADS, DH - 2 * HEADS)))

    head_rows = jnp.concatenate([jnp.zeros((padf, D), F32), meta_full], axis=0)
    h0, u1 = _pre_norm(x, head_rows, mix_pre_norm + after_start, bsz, lp, tr)
    (proj,) = _mm("proj_in", [(u1, win_t)], "nt", tm_big, 768, D, out_dtypes=(BF16,))
    (ba,) = _mm("proj_gate_logits", [(u1, win_t[BA0:SX0])], "nt", tm, BA_W, D)
    qkv = _gdn_prep(proj, convq_full, bsz, lp, padf)
    gb, gcs = _gates(ba, alog_v, dtb_v, bsz, lp, padf)

    def lane_bcast(cols):
        t = jnp.transpose(cols.reshape(bsz, lp, HEADS), (0, 2, 1)).reshape(bsz * HEADS, lp, 1)
        return jnp.broadcast_to(t, (bsz * HEADS, lp, DH))

    bb = lane_bcast(gb[:, 0:HEADS])
    gcb = lane_bcast(gcs[:, HEADS:2 * HEADS])
    gct = jnp.transpose(gcs[:, HEADS:2 * HEADS].reshape(bsz, nchunk, CH, HEADS), (0, 3, 1, 2)).reshape(
        bsz * HEADS, nchunk, 1, CH)
    u_loc, l1, l2, l3, w_t, k_dec, tinv_all, g_last = _gdn_local_fwd(qkv, gcb, bb, gct, bsz, lp)
    o, v_new, states = _gdn_scan_fwd(u_loc, l1, l2, g_last, bsz, lp)
    o, v_new = o.reshape(rows, GW), v_new.reshape(rows, GW)
    _, l_rest = _split_wait("gather_rest_arrive", grp_rest, _chipcast_pattern, o)
    (grp_out, grp_ffn, grp_down), token = _split_start(
        "gather_rest_forward", [l_rest[0:1], l_rest[1:3], l_rest[3:4]], _forward_pattern, None)
    mix_g, mix_s = _mix_heads(proj, o, gdn_norm + token[0:1, :], convs_full, bsz, lp, padf)
    mixin = jnp.concatenate([mix_g, mix_s], axis=1)
    _, (l_out,) = _split_wait("gather_w_out_wait", grp_out, _forward_pattern, mixin)
    wout = l_out.reshape(D, D)
    (mix,) = _mm("proj_out", [(mixin, wout)], "nn", tm, D, D, out_dtypes=(BF16,))
    h1, u2 = _mid_norms(h0, mix, mix_post_norm, ffn_pre_norm, bsz, lp, tr)
    _, (l_gate, l_up) = _split_wait("gather_ffn_wait", grp_ffn, _forward_pattern, u2)
    wgate_t, wup_t = l_gate.reshape(FF, D), l_up.reshape(FF, D)
    gate, up, act = _ffn_up(u2, wgate_t, wup_t, _div_tile(rows, 528), 1408)
    _, (l_down,) = _split_wait("gather_down_wait", grp_down, _forward_pattern, act)
    wdown = l_down.reshape(FF, D)
    (ffn,) = _mm("ffn_down", [(act, wdown)], "nn", tm, D, 1408, out_dtypes=(BF16,))

    dy, dffn, d_g4, loss_part = _loss_head(h1, ffn, loss_target, ffn_post_norm, bsz, lp, tr, first_real)
    (dwdown,) = _mm("dw_down", [(act, dffn)], "tn", 1408, D, tm, out_dtypes=(BF16,))

    def swiglu_bwd(da, g, u):
        g, u = g.astype(F32), u.astype(F32)
        return da * u * _dsilu(g), da * _silu(g)

    dgate, dup = _mm("d_act", [(dffn, wdown)], "nt", tm, 1408, D, out_dtypes=(BF16, BF16),
                     epilogue=swiglu_bwd, extras=(gate, up))
    (dwgate_t,) = _mm("dw_gate", [(dgate, u2)], "tn", 1408, D, tm, out_dtypes=(BF16,))
    (dwup_t,) = _mm("dw_up", [(dup, u2)], "tn", 1408, D, tm, out_dtypes=(BF16,))
    (du2,) = _mm("d_u2", [(dgate, wgate_t), (dup, wup_t)], "nn", tm, D, 1408, out_dtypes=(BF16,))
    dh1, dmix, d_g3, d_g2 = _mid_norms_bwd(h1, mix, du2, dy, ffn_pre_norm, mix_post_norm, bsz, lp, tr)
    (dwout,) = _mm("dw_out", [(mixin, dmix)], "tn", D, D, tm, out_dtypes=(BF16,))

    def row_blocks(g):
        return g.reshape(4, 2, -1, g.shape[1])

    core = jnp.reshape(cc, (1,)).astype(jnp.int32)
    chip = jnp.reshape(2 * cx + cy, (1,)).astype(jnp.int32)

    def reduce_start(tag, grads):
        (grp,), tok = _split_start("rs_sibling_start_" + tag, [grads], _sibling_pattern, lambda s: s[:1] + s[2:])
        return grp, tok

    def reduce_mid(tag, grp, nms, after):
        mine, lands = _split_wait("rs_sibling_wait_" + tag, grp, _sibling_pattern, after)
        parts = [_pair_sum("pair_sum_" + nm, g, r, core) for nm, g, r in zip(nms, mine, lands)]
        (grp2,), tok = _split_start("rs_chip_start_" + tag, [parts], _chip_pattern, lambda s: (3,) + s[1:])
        return grp2, tok

    def reduce_end(tag, grp2, after):
        return _split_wait("rs_chip_wait_" + tag, grp2, _chip_pattern, after)

    names_ffn = ["w_down", "w_gate", "w_up", "w_out"]
    rs_ffn, tok = reduce_start("ffn", [row_blocks(dwdown), row_blocks(dwgate_t), row_blocks(dwup_t), row_blocks(dwout)])
    (dmixin,) = _mm("d_mixin", [(dmix, wout)], "nt", tm, D, D, out_dtypes=(BF16,), after=tok)
    d_o, dz, dsx, dsb, dsc, d_gn, d_convs = _mix_heads_bwd(proj, o, gdn_norm, convs_full, dmixin, bsz, lp, padf)
    rs_ffn, tok = reduce_mid("ffn", rs_ffn, names_ffn, d_o)
    dv_new, dstates = _gdn_scan_bwd(d_o, l3, w_t, k_dec, g_last + tok[0, 0], bsz, lp)
    dq, dk, dv, dg_b, dbeta_b = _gdn_local_bwd(qkv, gcb, bb, gct, tinv_all, states, dstates, v_new,
                                               dv_new.reshape(rows, GW), d_o, bsz, lp)
    dpqkv, d_convq = _gdn_prep_bwd(proj, convq_full, dq, dk, dv, bsz, lp, padf)
    dba, d_scal = _gates_bwd(ba, alog_v, dtb_v, dbeta_b, dg_b, gb, bsz, lp, padf)
    dproj = jnp.concatenate([dpqkv, dz, dba, dsx, dsb, dsc], axis=1)

    d_alog = d_scal[0:1, HEADS:2 * HEADS]
    d_dtb = d_scal[1:2, HEADS:2 * HEADS]
    cq_w, cs_w = 3 * GW // N_DEV, SW // N_DEV
    slab = jnp.stack([
        _pack_small([jnp.zeros_like(d_g2), d_g2, d_g3, d_g4], jnp.zeros((N_META, DH), F32),
                    d_convq[:, p * cq_w:(p + 1) * cq_w], d_convs[:, p * cs_w:(p + 1) * cs_w], d_alog, d_dtb, d_gn,
                    loss_part) for p in range(N_DEV)])
    (grp_small,), tok = _split_start("small_grads_start", [[slab]], _alltoall_pattern, lambda s: s)
    (dwin_t,) = _mm("dw_in", [(dproj, u1)], "tn", 1280, D, tm, out_dtypes=(BF16,), after=tok)
    parts_ffn, recv_ffn = reduce_end("ffn", rs_ffn, dwin_t)
    dwin_t = jnp.concatenate([dwin_t[:BA0 + 2 * HEADS], dwin_t[SX0:]], axis=0)
    rs_in, tok = reduce_start("in", [row_blocks(dwin_t)])

    big_w = {"w_in": (w_in, m_w_in, v_w_in), "w_out": (w_out, m_w_out, v_w_out), "w_gate": (w_gate, m_w_gate, v_w_gate),
             "w_up": (w_up, m_w_up, v_w_up), "w_down": (w_down, m_w_down, v_w_down)}

    def update(nm, parts, recv, after=None):
        w, m, v = big_w[nm]
        if nm in ("w_out", "w_down"):
            return _adamw_reduced("adamw_" + nm, parts, chip, recv, w[0], m[0], v[0], after)
        res = _adamw_reduced("adamw_" + nm, parts, chip, recv, w[0].T, m[0].T, v[0].T, after)
        return [t.T for t in res]

    big_out = {nm: update(nm, p, r, tok) for nm, p, r in zip(names_ffn, parts_ffn, recv_ffn)}
    ffn_done = sum(big_out[nm][1][:1, :1] for nm in names_ffn)
    rs_in, tok = reduce_mid("in", rs_in, ["w_in"], ffn_done)
    (du1,) = _mm("d_u1", [(dproj, win_t)], "nn", tm_big, D, 768, out_dtypes=(BF16,), after=tok)
    grad_x, d_g1, d_meta = _pre_norm_bwd(h0, du1, dh1, mix_pre_norm, bsz, lp, seq, padf)

    late = jnp.stack([
        jnp.concatenate([jnp.pad(d_g1, ((0, 7), (0, 0))), jnp.pad(d_meta[:, p * DH:(p + 1) * DH], ((0, 0), (0, D - DH)))],
                        axis=0) for p in range(N_DEV)])
    (grp_late,), _ = _split_start("late_grads_start", [[late]], _alltoall_pattern, lambda s: s)
    _, (slabs_local,) = _split_wait("small_grads_wait", grp_small, _alltoall_pattern, grad_x)
    _, (late_local,) = _split_wait("late_grads_wait", grp_late, _alltoall_pattern, slabs_local)
    w_small = _pack_small([mix_pre_norm, mix_post_norm, ffn_pre_norm, ffn_post_norm], meta_tokens, conv_qkv[0],
                          conv_sc[0], a_log, dt_bias, gdn_norm)
    m_small = _pack_small([m_mix_pre_norm, m_mix_post_norm, m_ffn_pre_norm, m_ffn_post_norm], m_meta_tokens,
                          m_conv_qkv[0], m_conv_sc[0], m_a_log, m_dt_bias, m_gdn_norm)
    v_small = _pack_small([v_mix_pre_norm, v_mix_post_norm, v_ffn_pre_norm, v_ffn_post_norm], v_meta_tokens,
                          v_conv_qkv[0], v_conv_sc[0], v_a_log, v_dt_bias, v_gdn_norm)
    small_out, loss = _adamw_small(slabs_local, late_local, w_small, m_small, v_small)
    parts_in, recv_in = reduce_end("in", rs_in, small_out[1]["gdn_norm"])
    big_out["w_in"] = update("w_in", parts_in[0], recv_in[0])
    names = ["w_in", "w_out", "w_gate", "w_up", "w_down"]

    order = ["meta_tokens", "mix_pre_norm", "mix_post_norm", "ffn_pre_norm", "ffn_post_norm", "w_in", "conv_qkv",
             "a_log", "dt_bias", "gdn_norm", "conv_sc", "w_out", "w_gate", "w_up", "w_down"]
    outs = [loss[0, 0], grad_x]
    for kind in range(4):
        leaves = dict(small_out[kind])
        for nm in names:
            leaves[nm] = big_out[nm][kind][None]
        outs += [leaves[nm] for nm in order]
    return tuple(outs)
```

```python
import jax
import jax.numpy as jnp
from jax import lax
from jax.experimental import pallas as pl
from jax.experimental.pallas import tpu as pltpu

F32 = jnp.float32
BF16 = jnp.bfloat16
MESH = pl.DeviceIdType.MESH

D = 1024
N_META = 16
HEADS = 4
DH = 128
GW = HEADS * DH
SW = D - GW
FF = 2816
CH = 64
KQ = 4
KS = 3
EPS = 1e-6
IN_W = 3 * GW + GW + 2 * HEADS + 3 * SW
Q0, Z0, BA0, SX0, SB0, SC0 = 0, 1536, 2048, 2304, 2816, 3328
BA_W = SX0 - BA0
N_DEV = 8
VMEM_LIMIT = 56 * 1024 * 1024

ADAM_LR, ADAM_B1, ADAM_B2, ADAM_EPS, ADAM_WD, ADAM_STEP = 0.001, 0.9, 0.999, 1e-08, 0.01, 10


def _cparams(sem):
    return pltpu.CompilerParams(dimension_semantics=sem, vmem_limit_bytes=VMEM_LIMIT)


def _div_tile(n, target, mult=16):
    best = None
    for t in range(mult, min(n, target) + 1, mult):
        if n % t == 0:
            best = t
    assert best is not None, (n, target)
    return best


def _silu(x):
    return x * jax.nn.sigmoid(x)


def _dsilu(x):
    s = jax.nn.sigmoid(x)
    return s * (1.0 + x * (1.0 - s))


def _bdot(a, b, dims):
    return lax.dot_general(a.astype(BF16), b.astype(BF16), (dims, ((), ())), preferred_element_type=F32)


NN = ((1,), (0,))
NT = ((1,), (1,))
TN = ((0,), (0,))


def _mm(name, pairs, mode, tm, tn, tk, out_dtypes=(F32,), epilogue=None, extras=(), after=None):
    a0, b0 = pairs[0]
    if mode == "nn":
        (m, k), n = a0.shape, b0.shape[1]
    elif mode == "nt":
        (m, k), n = a0.shape, b0.shape[0]
    else:
        (k, m), n = a0.shape, b0.shape[1]
    assert m % tm == 0 and n % tn == 0 and k % tk == 0, (name, m, n, k, tm, tn, tk)
    nk = k // tk
    npair = len(pairs)
    nex = len(extras)
    nout = len(out_dtypes)
    ntok = 0 if after is None else 1
    dims = {"nn": NN, "nt": NT, "tn": TN}[mode]

    def body(*refs):
        ab = refs[:2 * npair]
        ex = refs[2 * npair:2 * npair + nex]
        outs = refs[2 * npair + nex + ntok:2 * npair + nex + ntok + nout]
        acc_ref = refs[-1]
        kk = pl.program_id(2)

        part = None
        for p in range(npair):
            d = lax.dot_general(ab[2 * p][...], ab[2 * p + 1][...], (dims, ((), ())), preferred_element_type=F32)
            part = d if part is None else part + d

        def finish(acc):
            res = epilogue(acc, *[e[...] for e in ex]) if epilogue is not None else (acc,)
            for o, r in zip(outs, res):
                o[...] = r.astype(o.dtype)

        if nk == 1:
            finish(part)
        else:
            @pl.when(kk == 0)
            def _():
                acc_ref[...] = part

            @pl.when(kk > 0)
            def _():
                acc_ref[...] += part

            @pl.when(kk == nk - 1)
            def _():
                finish(acc_ref[...])

    if mode == "nn":
        a_spec = pl.BlockSpec((tm, tk), lambda i, j, q: (i, q))
        b_spec = pl.BlockSpec((tk, tn), lambda i, j, q: (q, j))
    elif mode == "nt":
        a_spec = pl.BlockSpec((tm, tk), lambda i, j, q: (i, q))
        b_spec = pl.BlockSpec((tn, tk), lambda i, j, q: (j, q))
    else:
        a_spec = pl.BlockSpec((tk, tm), lambda i, j, q: (q, i))
        b_spec = pl.BlockSpec((tk, tn), lambda i, j, q: (q, j))
    o_spec = pl.BlockSpec((tm, tn), lambda i, j, q: (i, j))
    flat = [t for pr in pairs for t in pr]
    res = pl.pallas_call(
        body, name=name,
        grid=(m // tm, n // tn, nk),
        in_specs=[a_spec, b_spec] * npair + [o_spec] * nex + [pl.BlockSpec((8, DH), lambda i, j, q: (0, 0))] * ntok,
        out_specs=[o_spec] * nout,
        out_shape=[jax.ShapeDtypeStruct((m, n), dt) for dt in out_dtypes],
        scratch_shapes=[pltpu.VMEM((tm, tn) if nk > 1 else (8, DH), F32)],
        compiler_params=_cparams(("parallel", "parallel", "arbitrary")),
    )(*flat, *extras, *([] if after is None else [after]))
    return res


def _ffn_up(u2, wg, wu, tm, tn):
    m, k = u2.shape
    n = wg.shape[0]

    def body(a_ref, g_ref, u_ref, gate_ref, up_ref, act_ref):
        a = a_ref[...]
        g = lax.dot_general(a, g_ref[...], (NT, ((), ())), preferred_element_type=F32)
        u = lax.dot_general(a, u_ref[...], (NT, ((), ())), preferred_element_type=F32)
        gate_ref[...] = g.astype(BF16)
        up_ref[...] = u.astype(BF16)
        act_ref[...] = (_silu(g) * u).astype(BF16)

    o_spec = pl.BlockSpec((tm, tn), lambda j, i: (i, j))
    w_spec = pl.BlockSpec((tn, k), lambda j, i: (j, 0))
    return pl.pallas_call(
        body, name="ffn_up", grid=(n // tn, m // tm),
        in_specs=[pl.BlockSpec((tm, k), lambda j, i: (i, 0)), w_spec, w_spec],
        out_specs=[o_spec, o_spec, o_spec],
        out_shape=[jax.ShapeDtypeStruct((m, n), BF16)] * 3,
        compiler_params=_cparams(("parallel", "parallel")),
    )(u2, wg, wu)


def _rms(x, gain):
    r = lax.rsqrt(jnp.mean(x * x, axis=-1, keepdims=True) + EPS)
    return x * r * gain, r


def _rms_bwd(x, gain, dy):
    r = lax.rsqrt(jnp.mean(x * x, axis=-1, keepdims=True) + EPS)
    dyw = dy * gain
    dx = r * dyw - x * (r * r * r) * jnp.mean(dyw * x, axis=-1, keepdims=True)
    return dx, dy * x * r


def _row_specs(tr, width, nb):
    return pl.BlockSpec((tr, width), lambda b, j: (b * nb + j, 0))


def _vec_spec(width):
    return pl.BlockSpec((1, width), lambda b, j: (0, 0))


def _first_step(b, j):
    return jnp.logical_and(b == 0, j == 0)


def _pre_norm(x, head_rows, gain, bsz, lp, tr):
    nb = lp // tr
    first_real = head_rows.shape[0]
    assert first_real % 8 == 0 and first_real < tr
    x_spec = pl.BlockSpec((pl.Squeezed(), pl.Element(tr), pl.Element(D)),
                          lambda b, j: (b, pl.multiple_of(jnp.maximum(j * tr - first_real, 0), 8), 0))

    def body(x_ref, hd_ref, g_ref, h_ref, u_ref):
        j = pl.program_id(1)
        xt = x_ref[...]
        h = jnp.where(j == 0, jnp.concatenate([hd_ref[...], xt[:tr - first_real]], axis=0), xt)
        h_ref[...] = h
        u_ref[...] = _rms(h, g_ref[...])[0].astype(BF16)

    rs = _row_specs(tr, D, nb)
    return pl.pallas_call(
        body, name="mix_pre_norm", grid=(bsz, nb),
        in_specs=[x_spec, pl.BlockSpec((first_real, D), lambda b, j: (0, 0)), _vec_spec(D)],
        out_specs=[rs, rs],
        out_shape=[jax.ShapeDtypeStruct((bsz * lp, D), F32), jax.ShapeDtypeStruct((bsz * lp, D), BF16)],
        compiler_params=_cparams(("parallel", "parallel")),
    )(x, head_rows, gain)


def _mid_norms(h0, mix, g_post, g_pre, bsz, lp, tr):
    nb = lp // tr

    def body(h_ref, m_ref, gp_ref, gq_ref, h1_ref, u2_ref):
        h1 = h_ref[...] + _rms(m_ref[...].astype(F32), gp_ref[...])[0]
        h1_ref[...] = h1
        u2_ref[...] = _rms(h1, gq_ref[...])[0].astype(BF16)

    rs = _row_specs(tr, D, nb)
    return pl.pallas_call(
        body, name="mid_norms", grid=(bsz, nb),
        in_specs=[rs, rs, _vec_spec(D), _vec_spec(D)],
        out_specs=[rs, rs],
        out_shape=[jax.ShapeDtypeStruct(h0.shape, F32), jax.ShapeDtypeStruct(h0.shape, BF16)],
        compiler_params=_cparams(("parallel", "parallel")),
    )(h0, mix, g_post, g_pre)


def _loss_head(h1, ffn, tgt, g_post, bsz, lp, tr, first_real):
    nb = lp // tr
    assert first_real % 8 == 0 and first_real < tr
    tgt_spec = pl.BlockSpec((pl.Squeezed(), pl.Element(tr), pl.Element(D)),
                            lambda b, j: (b, pl.multiple_of(jnp.maximum(j * tr - first_real, 0), 8), 0))

    def body(h_ref, f_ref, t_ref, g_ref, dy_ref, dffn_ref, dg_ref, loss_ref):
        b, j = pl.program_id(0), pl.program_id(1)
        f = f_ref[...].astype(F32)
        g = g_ref[...]
        y = h_ref[...] + _rms(f, g)[0]
        t = j * tr + lax.broadcasted_iota(jnp.int32, (tr, 1), 0)
        tg = t_ref[...]
        tg = jnp.where(j == 0, jnp.concatenate([tg[tr - first_real:], tg[:tr - first_real]], axis=0), tg)
        err = jnp.where(t >= first_real, y - tg, 0.0)
        dy = err * (1.0 / D)
        dy_ref[...] = dy
        dx, dgc = _rms_bwd(f, g, dy)
        dffn_ref[...] = dx.astype(BF16)
        lsum = jnp.sum(jnp.sum(err * err, axis=0, keepdims=True), axis=1, keepdims=True) * (0.5 / D)

        @pl.when(_first_step(b, j))
        def _():
            dg_ref[...] = jnp.zeros_like(dg_ref)
            loss_ref[...] = jnp.zeros_like(loss_ref)

        dg_ref[...] += jnp.sum(dgc, axis=0, keepdims=True)
        loss_ref[...] += jnp.broadcast_to(lsum, loss_ref.shape)

    rs = _row_specs(tr, D, nb)
    return pl.pallas_call(
        body, name="loss_head", grid=(bsz, nb),
        in_specs=[rs, rs, tgt_spec, _vec_spec(D)],
        out_specs=[rs, rs, _vec_spec(D), _vec_spec(DH)],
        out_shape=[jax.ShapeDtypeStruct(h1.shape, F32), jax.ShapeDtypeStruct(h1.shape, BF16),
                   jax.ShapeDtypeStruct((1, D), F32), jax.ShapeDtypeStruct((1, DH), F32)],
        compiler_params=_cparams(("arbitrary", "arbitrary")),
    )(h1, ffn, tgt, g_post)


def _mid_norms_bwd(h1, mix, du2, dy, g_pre, g_post, bsz, lp, tr):
    nb = lp // tr

    def body(h_ref, m_ref, du_ref, dy_ref, gq_ref, gp_ref, dh1_ref, dmix_ref, dgq_ref, dgp_ref):
        b, j = pl.program_id(0), pl.program_id(1)
        dx, dgq = _rms_bwd(h_ref[...], gq_ref[...], du_ref[...].astype(F32))
        dh1 = dy_ref[...] + dx
        dh1_ref[...] = dh1
        dm, dgp = _rms_bwd(m_ref[...].astype(F32), gp_ref[...], dh1)
        dmix_ref[...] = dm.astype(BF16)

        @pl.when(_first_step(b, j))
        def _():
            dgq_ref[...] = jnp.zeros_like(dgq_ref)
            dgp_ref[...] = jnp.zeros_like(dgp_ref)

        dgq_ref[...] += jnp.sum(dgq, axis=0, keepdims=True)
        dgp_ref[...] += jnp.sum(dgp, axis=0, keepdims=True)

    rs = _row_specs(tr, D, nb)
    return pl.pallas_call(
        body, name="mid_norms_bwd", grid=(bsz, nb),
        in_specs=[rs, rs, rs, rs, _vec_spec(D), _vec_spec(D)],
        out_specs=[rs, rs, _vec_spec(D), _vec_spec(D)],
        out_shape=[jax.ShapeDtypeStruct(h1.shape, F32), jax.ShapeDtypeStruct(h1.shape, BF16),
                   jax.ShapeDtypeStruct((1, D), F32), jax.ShapeDtypeStruct((1, D), F32)],
        compiler_params=_cparams(("arbitrary", "arbitrary")),
    )(h1, mix, du2, dy, g_pre, g_post)


def _pre_norm_bwd(h0, du1, dh1, gain, bsz, lp, seq, padf):
    first_real = lp - seq
    tx = _div_tile(seq, 512)
    assert first_real % 8 == 0 and lp % first_real == 0

    def tokens(h_ref, du_ref, dh1_ref, g_ref, gx_ref, dg_ref):
        b, j = pl.program_id(0), pl.program_id(1)
        dx, dgc = _rms_bwd(h_ref[...], g_ref[...], du_ref[...].astype(F32))
        gx_ref[...] = dh1_ref[...] + dx

        @pl.when(_first_step(b, j))
        def _():
            dg_ref[...] = jnp.zeros_like(dg_ref)

        dg_ref[...] += jnp.sum(dgc, axis=0, keepdims=True)

    window = pl.BlockSpec((pl.Element(tx), pl.Element(D)),
                          lambda b, j: (pl.multiple_of(b * lp + first_real + j * tx, 16), 0))
    assert first_real % 16 == 0 and tx % 16 == 0 and lp % 16 == 0
    grad_x, dg_tokens = pl.pallas_call(
        tokens, name="pre_norm_bwd", grid=(bsz, seq // tx),
        in_specs=[window, window, window, _vec_spec(D)],
        out_specs=[pl.BlockSpec((pl.Squeezed(), tx, D), lambda b, j: (b, j, 0)), _vec_spec(D)],
        out_shape=[jax.ShapeDtypeStruct((bsz, seq, D), F32), jax.ShapeDtypeStruct((1, D), F32)],
        compiler_params=_cparams(("arbitrary", "arbitrary")),
    )(h0, du1, dh1, gain)

    def head(h_ref, du_ref, dh1_ref, g_ref, dgt_ref, dg_ref, dmeta_ref):
        b = pl.program_id(0)
        dx, dgc = _rms_bwd(h_ref[...], g_ref[...], du_ref[...].astype(F32))
        dh0 = dh1_ref[...] + dx

        @pl.when(b == 0)
        def _():
            dg_ref[...] = dgt_ref[...]
            dmeta_ref[...] = jnp.zeros_like(dmeta_ref)

        dg_ref[...] += jnp.sum(dgc, axis=0, keepdims=True)
        dmeta_ref[...] += dh0[padf:padf + N_META, :]

    rows = pl.BlockSpec((first_real, D), lambda b: (b * (lp // first_real), 0))
    vec = pl.BlockSpec((1, D), lambda b: (0, 0))
    dg, dmeta = pl.pallas_call(
        head, name="pre_norm_bwd_head", grid=(bsz,),
        in_specs=[rows, rows, rows, vec, vec],
        out_specs=[vec, pl.BlockSpec((N_META, D), lambda b: (0, 0))],
        out_shape=[jax.ShapeDtypeStruct((1, D), F32), jax.ShapeDtypeStruct((N_META, D), F32)],
        compiler_params=_cparams(("arbitrary",)),
    )(h0, du1, dh1, gain, dg_tokens)
    return grad_x, dg, dmeta


def _shift_down(x, s, lp):
    return x if s == 0 else pltpu.roll(x, s, axis=0)


def _shift_up(x, s, lp):
    return x if s == 0 else pltpu.roll(x, lp - s, axis=0)


def _conv_fwd(x, w, taps, lp):
    y = None
    for i in range(taps):
        term = _shift_down(x, taps - 1 - i, lp) * w[i:i + 1, :]
        y = term if y is None else y + term
    return y


def _conv_bwd(x, w, dy, taps, lp):
    dx = None
    dws = []
    for i in range(taps):
        s = taps - 1 - i
        term = _shift_up(dy, s, lp) * w[i:i + 1, :]
        dx = term if dx is None else dx + term
        dws.append(jnp.sum(dy * _shift_down(x, s, lp), axis=0, keepdims=True))
    return dx, jnp.concatenate(dws, axis=0)


def _slab(lp, col0):
    return pl.BlockSpec((lp, DH), lambda b, s: (b, col0 // DH + s))


def _qkv_act(c, s):
    y = _silu(c)
    r = lax.rsqrt(jnp.sum(y * y, axis=-1, keepdims=True) + EPS)
    scale = jnp.where(s < HEADS, DH ** -0.5, 1.0)
    return y, r, scale


def _gdn_prep(proj, conv_qkv, bsz, lp, padf):
    def body(p_ref, w_ref, o_ref):
        s = pl.program_id(1)
        c = _conv_fwd(p_ref[...].astype(F32), w_ref[...], KQ, lp)
        y, r, scale = _qkv_act(c, s)
        out = jnp.where(s < 2 * HEADS, y * (r * scale), y)
        t = lax.broadcasted_iota(jnp.int32, (lp, 1), 0)
        o_ref[...] = jnp.where(t >= padf, out, 0.0).astype(BF16)

    return pl.pallas_call(
        body, name="gdn_prep", grid=(bsz, 3 * HEADS),
        in_specs=[_slab(lp, Q0), pl.BlockSpec((KQ, DH), lambda b, s: (0, s))],
        out_specs=_slab(lp, 0),
        out_shape=jax.ShapeDtypeStruct((bsz * lp, 3 * GW), BF16),
        compiler_params=_cparams(("parallel", "parallel")),
    )(proj, conv_qkv)


def _gdn_prep_bwd(proj, conv_qkv, dq, dk, dv, bsz, lp, padf):
    def body(p_ref, w_ref, dq_ref, dk_ref, dv_ref, dp_ref, dw_ref):
        b, s = pl.program_id(0), pl.program_id(1)
        x = p_ref[...].astype(F32)
        w = w_ref[...]
        c = _conv_fwd(x, w, KQ, lp)
        y, r, scale = _qkv_act(c, s)
        d = jnp.where(s < HEADS, dq_ref[...], jnp.where(s < 2 * HEADS, dk_ref[...], dv_ref[...])).astype(F32)
        dn = d * scale
        n = y * r
        dy_norm = r * (dn - n * jnp.sum(dn * n, axis=-1, keepdims=True))
        dyy = jnp.where(s < 2 * HEADS, dy_norm, dn)
        t = lax.broadcasted_iota(jnp.int32, (lp, 1), 0)
        dc = jnp.where(t >= padf, dyy * _dsilu(c), 0.0)
        dx, dw = _conv_bwd(x, w, dc, KQ, lp)
        dp_ref[...] = jnp.where(t >= padf, dx, 0.0).astype(BF16)

        @pl.when(jnp.logical_and(b == 0, s == 0))
        def _():
            dw_ref[...] = jnp.zeros_like(dw_ref)

        col_block = jnp.right_shift(lax.broadcasted_iota(jnp.int32, (KQ, 3 * GW), 1), DH.bit_length() - 1)
        dw_ref[...] += jnp.where(col_block == s, jnp.tile(dw, (1, 3 * HEADS)), 0.0)

    sl = lambda col0: pl.BlockSpec((lp, DH), lambda b, s: (b, col0 // DH + s))
    part = lambda k: pl.BlockSpec((lp, DH), lambda b, s: (b, jnp.clip(s - k * HEADS, 0, HEADS - 1)))
    return pl.pallas_call(
        body, name="gdn_prep_bwd", grid=(bsz, 3 * HEADS),
        in_specs=[sl(Q0), pl.BlockSpec((KQ, DH), lambda b, s: (0, s)), part(0), part(1), part(2)],
        out_specs=[sl(0), pl.BlockSpec((KQ, 3 * GW), lambda b, s: (0, 0))],
        out_shape=[jax.ShapeDtypeStruct((bsz * lp, 3 * GW), BF16), jax.ShapeDtypeStruct((KQ, 3 * GW), F32)],
        compiler_params=_cparams(("arbitrary", "arbitrary")),
    )(proj, conv_qkv, dq, dk, dv)


def _softplus(x):
    e = jnp.exp(-jnp.abs(x))
    one_e = 1.0 + e
    l1p = jnp.where(one_e == 1.0, e, jnp.log(one_e) * (e / (one_e - 1.0)))
    return jnp.maximum(x, 0.0) + l1p


def _tri(n, kind):
    i = lax.broadcasted_iota(jnp.int32, (n, n), 0)
    j = lax.broadcasted_iota(jnp.int32, (n, n), 1)
    return {"incl": i >= j, "strict": i > j, "upper": i <= j}[kind]


def _gates(proj, alog_v, dtb_v, bsz, lp, padf):
    nchunk = lp // CH

    def body(p_ref, al_ref, dt_ref, gb_ref, cs_ref):
        x = p_ref[...].astype(F32)
        lane = lax.broadcasted_iota(jnp.int32, (lp, DH), 1)
        t = lax.broadcasted_iota(jnp.int32, (lp, DH), 0)
        beta = jax.nn.sigmoid(x)
        g = -jnp.exp(al_ref[...]) * _softplus(x + dt_ref[...])
        gb = jnp.where(lane < HEADS, beta, jnp.where(lane < 2 * HEADS, g, 0.0))
        gb_ref[...] = jnp.where(t >= padf, gb, 0.0)
        ltri = _tri(CH, "incl").astype(BF16)
        rest = [gb_ref[pl.ds(n * CH, CH), :] for n in range(nchunk)]
        sums = None
        for _ in range(3):
            piece = [v.astype(BF16) for v in rest]
            rest = [v - p.astype(F32) for v, p in zip(rest, piece)]
            part = [jnp.dot(ltri, p, preferred_element_type=F32) for p in piece]
            sums = part if sums is None else [a + b for a, b in zip(sums, part)]
        for n in range(nchunk):
            cs_ref[pl.ds(n * CH, CH), :] = sums[n]

    blk = pl.BlockSpec((lp, DH), lambda b: (b, 0))
    vec = pl.BlockSpec((1, DH), lambda b: (0, 0))
    out = pl.BlockSpec((lp, DH), lambda b: (b, 0))
    return pl.pallas_call(
        body, name="gates", grid=(bsz,),
        in_specs=[blk, vec, vec], out_specs=[out, out],
        out_shape=[jax.ShapeDtypeStruct((bsz * lp, DH), F32)] * 2,
        compiler_params=_cparams(("parallel",)),
    )(proj, alog_v, dtb_v)


def _gates_bwd(proj, alog_v, dtb_v, dbeta_b, dg_b, gb, bsz, lp, padf):
    def body(p_ref, al_ref, dt_ref, db_ref, dg_ref, gb_ref, dl_ref, dv_ref):
        b = pl.program_id(0)
        x = p_ref[...].astype(F32)
        lane = lax.broadcasted_iota(jnp.int32, (lp, DH), 1)
        t = lax.broadcasted_iota(jnp.int32, (lp, DH), 0)
        dbeta = jnp.zeros((lp, DH), F32)
        dg = jnp.zeros((lp, DH), F32)
        for h in range(HEADS):
            dbeta = jnp.where(lane == h, db_ref[h], dbeta)
            dg = jnp.where(lane == HEADS + h, dg_ref[h], dg)
        real = t >= padf
        dbeta = jnp.where(real, dbeta, 0.0)
        dg = jnp.where(real, dg, 0.0)
        sb = jax.nn.sigmoid(x)
        neg_ea = -jnp.exp(al_ref[...])
        dsp = dg * neg_ea * jax.nn.sigmoid(x + dt_ref[...])
        dl = dbeta * sb * (1.0 - sb) + dsp
        dl_ref[...] = jnp.concatenate([dl, jnp.zeros((lp, BA_W - DH), F32)], axis=1).astype(BF16)
        dalog = jnp.sum(dg * gb_ref[...], axis=0, keepdims=True)
        ddt = jnp.sum(dsp, axis=0, keepdims=True)

        @pl.when(b == 0)
        def _():
            dv_ref[...] = jnp.zeros_like(dv_ref)

        dv_ref[...] += jnp.concatenate([dalog, ddt], axis=0)

    blk = pl.BlockSpec((lp, DH), lambda b: (b, 0))
    vec = pl.BlockSpec((1, DH), lambda b: (0, 0))
    hb = pl.BlockSpec((HEADS, lp, DH), lambda b: (b, 0, 0))
    return pl.pallas_call(
        body, name="gates_bwd", grid=(bsz,),
        in_specs=[blk, vec, vec, hb, hb, pl.BlockSpec((lp, DH), lambda b: (b, 0))],
        out_specs=[pl.BlockSpec((lp, BA_W), lambda b: (b, 0)), pl.BlockSpec((2, DH), lambda b: (0, 0))],
        out_shape=[jax.ShapeDtypeStruct((bsz * lp, BA_W), BF16), jax.ShapeDtypeStruct((2, DH), F32)],
        compiler_params=_cparams(("arbitrary",)),
    )(proj, alog_v, dtb_v, dbeta_b, dg_b, gb)


def _mix_heads(proj, o, gdn_norm, conv_sc, bsz, lp, padf):
    def body(o_ref, z_ref, gn_ref, sx_ref, sb_ref, sc_ref, w_ref, og_ref, os_ref):
        og_ref[...] = (_rms(o_ref[...].astype(F32), gn_ref[...])[0] * _silu(z_ref[...].astype(F32))).astype(BF16)
        conv = _conv_fwd(sc_ref[...].astype(F32) * sx_ref[...].astype(F32), w_ref[...], KS, lp)
        t = lax.broadcasted_iota(jnp.int32, (lp, 1), 0)
        os_ref[...] = jnp.where(t >= padf, sb_ref[...].astype(F32) * conv, 0.0).astype(BF16)

    half = _slab(lp, 0)
    return pl.pallas_call(
        body, name="mix_heads", grid=(bsz, HEADS),
        in_specs=[half, _slab(lp, Z0), pl.BlockSpec((1, DH), lambda b, s: (0, 0)),
                  _slab(lp, SX0), _slab(lp, SB0), _slab(lp, SC0), pl.BlockSpec((KS, DH), lambda b, s: (0, s))],
        out_specs=[half, half],
        out_shape=[jax.ShapeDtypeStruct((bsz * lp, GW), BF16)] * 2,
        compiler_params=_cparams(("parallel", "parallel")),
    )(o, proj, gdn_norm, proj, proj, proj, conv_sc)


def _mix_heads_bwd(proj, o, gdn_norm, conv_sc, dmixin, bsz, lp, padf):
    def body(o_ref, z_ref, gn_ref, sx_ref, sb_ref, sc_ref, w_ref, dg_ref, ds_ref,
             do_ref, dz_ref, dsx_ref, dsb_ref, dsc_ref, dgn_ref, dw_ref):
        s, b = pl.program_id(0), pl.program_id(1)
        t = lax.broadcasted_iota(jnp.int32, (lp, 1), 0)
        real = t >= padf
        o, z, gn, d = o_ref[...].astype(F32), z_ref[...].astype(F32), gn_ref[...], dg_ref[...].astype(F32)
        sz = _silu(z)
        on = _rms(o, gn)[0]
        dz_ref[...] = (d * on * _dsilu(z)).astype(BF16)
        dox, dgn = _rms_bwd(o, gn, d * sz)
        do_ref[...] = dox.astype(BF16)
        sx, sb, sc = sx_ref[...].astype(F32), sb_ref[...].astype(F32), sc_ref[...].astype(F32)
        w, e = w_ref[...], ds_ref[...].astype(F32)
        e = jnp.where(real, e, 0.0)
        xin = sc * sx
        conv = _conv_fwd(xin, w, KS, lp)
        dsb_ref[...] = (e * conv).astype(BF16)
        dxin, dw = _conv_bwd(xin, w, e * sb, KS, lp)
        dxin = jnp.where(real, dxin, 0.0)
        dsx_ref[...] = (dxin * sc).astype(BF16)
        dsc_ref[...] = (dxin * sx).astype(BF16)

        @pl.when(jnp.logical_and(s == 0, b == 0))
        def _():
            dgn_ref[...] = jnp.zeros_like(dgn_ref)

        @pl.when(b == 0)
        def _():
            dw_ref[...] = jnp.zeros_like(dw_ref)

        dgn_ref[...] += jnp.sum(dgn, axis=0, keepdims=True)
        dw_ref[...] += dw

    sl = lambda col0: pl.BlockSpec((lp, DH), lambda s, b: (b, col0 // DH + s))
    half = sl(0)
    return pl.pallas_call(
        body, name="mix_heads_bwd", grid=(HEADS, bsz),
        in_specs=[half, sl(Z0), pl.BlockSpec((1, DH), lambda s, b: (0, 0)), sl(SX0), sl(SB0), sl(SC0),
                  pl.BlockSpec((KS, DH), lambda s, b: (0, s)), sl(0), sl(GW)],
        out_specs=[half] * 5 + [pl.BlockSpec((1, DH), lambda s, b: (0, 0)), pl.BlockSpec((KS, DH), lambda s, b: (0, s))],
        out_shape=[jax.ShapeDtypeStruct((bsz * lp, GW), BF16)] * 5
        + [jax.ShapeDtypeStruct((1, DH), F32), jax.ShapeDtypeStruct((KS, SW), F32)],
        compiler_params=_cparams(("arbitrary", "arbitrary")),
    )(o, proj, gdn_norm, proj, proj, proj, conv_sc, dmixin, dmixin)


def _split_dot(a, b, dims=NN):
    ah = a.astype(BF16)
    al = (a - ah.astype(F32)).astype(BF16)
    bh = b.astype(BF16)
    bl = (b - bh.astype(F32)).astype(BF16)

    def d(x, y):
        return lax.dot_general(x, y, (dims, ((), ())), preferred_element_type=F32)

    return d(ah, bh) + (d(ah, bl) + d(al, bh))


def _eye(n):
    return (lax.broadcasted_iota(jnp.int32, (n, n), 0) == lax.broadcasted_iota(jnp.int32, (n, n), 1)).astype(F32)


def _each(f, *lists):
    return [f(*xs) for xs in zip(*lists)]


def _inv_unit_lower(a_list):
    eye = _eye(CH)
    p = _each(lambda a: -a, a_list)
    t = _each(lambda x: eye + x, p)
    p = _each(lambda x: _split_dot(x, x), p)
    for level in range(5):
        if level < 4:
            prod = _each(lambda tt, pp: _split_dot(jnp.concatenate([tt, pp], axis=0), pp), t, p)
            t = _each(lambda tt, pr: tt + pr[:CH], t, prod)
            p = _each(lambda pr: pr[CH:], prod)
        else:
            t = _each(lambda tt, pp: tt + _split_dot(tt, pp), t, p)
    return t


def _chunk_gates(gc_ref, be_ref, gt_ref, g):
    rows = pl.ds(g * CH, CH)
    gc = gc_ref[0, rows, :]
    be = be_ref[0, rows, :]
    gcl = gc_ref[0, pl.ds(g * CH + CH - 1, 1), :]
    gt = gt_ref[0, g]
    incl = _tri(CH, "incl")
    dec = jnp.where(incl, jnp.exp(jnp.where(incl, gc[:, :CH] - gt, 0.0)), 0.0)
    return rows, be, dec, jnp.exp(gc), jnp.exp(gcl - gc), jnp.exp(gcl)


def _chunk_group(nchunk):
    return 3 if nchunk % 3 == 0 else 1


def _local_group(nchunk):
    return 11 if nchunk % 11 == 0 else _chunk_group(nchunk)


def _gdn_local_specs(lp, grp):
    ngrp = lp // (grp * CH)
    head = lambda col0: pl.BlockSpec((grp * CH, DH), lambda i, j: ((i // HEADS) * ngrp + j, col0 // DH + i % HEADS))
    lane_b = pl.BlockSpec((1, grp * CH, DH), lambda i, j: (i, j, 0))
    per_chunk = lambda r, c: pl.BlockSpec((1, grp, r, c), lambda i, j: (i, j, 0, 0))
    return ngrp, head, lane_b, per_chunk


def _gdn_local_fwd(qkv, gcb, bb, gct, bsz, lp):
    nchunk = lp // CH
    grp = _local_group(nchunk)
    ngrp, head, lane_b, per_chunk = _gdn_local_specs(lp, grp)
    bh = bsz * HEADS

    def body(q_ref, k_ref, v_ref, gc_ref, be_ref, gt_ref,
             u_ref, l1_ref, l2_ref, l3_ref, wt_ref, kd_ref, ti_ref, gl_ref):
        incl, strict = _tri(CH, "incl"), _tri(CH, "strict")
        rows, be, dec, eg, ekd, gl = zip(*[_chunk_gates(gc_ref, be_ref, gt_ref, g) for g in range(grp)])
        q = [q_ref[r, :].astype(F32) for r in rows]
        k = [k_ref[r, :].astype(F32) for r in rows]
        v = [v_ref[r, :].astype(F32) for r in rows]
        kb = _each(lambda x, y: x * y, k, be)
        a = _each(lambda x, y, d: jnp.where(strict, _bdot(x, y, NT) * d, 0.0), kb, k, dec)
        tinv = _inv_unit_lower(a)
        u = _each(lambda t, x, y: _bdot(t, x * y, NN), tinv, v, be)
        w = _each(lambda t, x, y: _bdot(t, x * y, NN), tinv, kb, eg)
        qk = _each(lambda x, y, d: jnp.where(incl, _bdot(x, y, NT) * d, 0.0), q, k, dec)
        qk_t = _each(lambda x: x.T, qk)
        qd = _each(lambda x, y: x * y, q, eg)
        kd = _each(lambda x, y: x * y, k, ekd)
        kd_t = _each(lambda x: x.T, kd)
        qd_t = _each(lambda x: x.T, qd)
        w_t = _each(lambda x: x.T, w)
        for g in range(grp):
            u_ref[rows[g], :] = u[g]
            l1_ref[0, g] = jnp.concatenate([w[g], qd[g]], axis=0).astype(BF16)
            l2_ref[0, g] = jnp.concatenate([qk[g], kd_t[g]], axis=0).astype(BF16)
            l3_ref[0, g] = jnp.concatenate([qk_t[g], qd_t[g]], axis=0).astype(BF16)
            wt_ref[0, g] = w_t[g].astype(BF16)
            kd_ref[0, g] = kd[g].astype(BF16)
            ti_ref[0, g] = tinv[g]
            gl_ref[0, g] = gl[g]

    r = bsz * lp
    shapes = [((r, GW), F32), ((bh, nchunk, 2 * CH, DH), BF16), ((bh, nchunk, 3 * CH, CH), BF16),
              ((bh, nchunk, 3 * CH, CH), BF16), ((bh, nchunk, DH, CH), BF16), ((bh, nchunk, CH, DH), BF16),
              ((bh, nchunk, CH, CH), F32), ((bh, nchunk, 1, DH), F32)]
    return pl.pallas_call(
        body, name="gdn_local_fwd", grid=(bh, ngrp),
        in_specs=[head(0), head(GW), head(2 * GW), lane_b, lane_b, per_chunk(1, CH)],
        out_specs=[head(0), per_chunk(2 * CH, DH), per_chunk(3 * CH, CH), per_chunk(3 * CH, CH), per_chunk(DH, CH),
                   per_chunk(CH, DH), per_chunk(CH, CH), per_chunk(1, DH)],
        out_shape=[jax.ShapeDtypeStruct(s, dt) for s, dt in shapes],
        compiler_params=_cparams(("parallel", "parallel")),
    )(qkv, qkv, qkv, gcb, bb, gct)


def _scan_specs(bsz, lp, grp, order):
    ngrp = lp // (grp * CH)
    bh = bsz * HEADS
    seq = lambda: pl.BlockSpec((bsz, grp * CH, GW), lambda s: (0, order(s, ngrp), 0))
    per_chunk = lambda r, c: pl.BlockSpec((bh, grp, r, c), lambda s: (0, order(s, ngrp), 0, 0))
    return ngrp, bh, seq, per_chunk


def _gdn_scan_fwd(u, l1, l2, gl, bsz, lp):
    nchunk = lp // CH
    grp = _chunk_group(nchunk)
    ngrp, bh, seq, per_chunk = _scan_specs(bsz, lp, grp, lambda s, n: s)

    def body(u_ref, l1_ref, l2_ref, gl_ref, o_ref, vn_ref, st_ref, s_scr):
        @pl.when(pl.program_id(0) == 0)
        def _():
            s_scr[...] = jnp.zeros_like(s_scr)

        chains = list(range(bh))
        where = [(i // HEADS, slice((i % HEADS) * DH, (i % HEADS + 1) * DH)) for i in chains]

        def dot(a, b):
            return jnp.dot(a, b.astype(BF16), preferred_element_type=F32)

        for c in range(grp):
            rows = slice(c * CH, (c + 1) * CH)
            s = [s_scr[i] for i in chains]
            for i in chains:
                st_ref[i, c] = s[i]
            x = _each(lambda i, si: dot(l1_ref[i, c], si), chains, s)
            vnew = _each(lambda bc, xi: u_ref[bc[0], rows, bc[1]] - xi[:CH], where, x)
            y = _each(lambda i, vi: dot(l2_ref[i, c], vi), chains, vnew)
            for i, (b, cols) in zip(chains, where):
                o_ref[b, rows, cols] = (x[i][CH:] + y[i][:CH]).astype(BF16)
                vn_ref[b, rows, cols] = vnew[i].astype(BF16)
                s_scr[i] = s[i] * gl_ref[i, c] + y[i][CH:]

    return pl.pallas_call(
        body, name="gdn_scan_fwd", grid=(ngrp,),
        in_specs=[seq(), per_chunk(2 * CH, DH), per_chunk(3 * CH, CH), per_chunk(1, DH)],
        out_specs=[seq(), seq(), per_chunk(DH, DH)],
        out_shape=[jax.ShapeDtypeStruct((bsz, lp, GW), BF16), jax.ShapeDtypeStruct((bsz, lp, GW), BF16),
                   jax.ShapeDtypeStruct((bh, nchunk, DH, DH), F32)],
        scratch_shapes=[pltpu.VMEM((bh, DH, DH), F32)],
        compiler_params=_cparams(("arbitrary",)),
    )(u.reshape(bsz, lp, GW), l1, l2, gl)


def _gdn_scan_bwd(d_o, l3, wt, kd, gl, bsz, lp):
    nchunk = lp // CH
    grp = _chunk_group(nchunk)
    ngrp, bh, seq, per_chunk = _scan_specs(bsz, lp, grp, lambda s, n: n - 1 - s)

    def body(do_ref, l3_ref, wt_ref, kd_ref, gl_ref, dvn_ref, dst_ref, ds_scr):
        @pl.when(pl.program_id(0) == 0)
        def _():
            ds_scr[...] = jnp.zeros_like(ds_scr)

        chains = list(range(bh))
        where = [(i // HEADS, slice((i % HEADS) * DH, (i % HEADS + 1) * DH)) for i in chains]

        def dot(a, b):
            return jnp.dot(a, b.astype(BF16), preferred_element_type=F32)

        for c in reversed(range(grp)):
            rows = slice(c * CH, (c + 1) * CH)
            ds = [ds_scr[i] for i in chains]
            for i in chains:
                dst_ref[i, c] = ds[i]
            p = _each(lambda i, bc: dot(l3_ref[i, c], do_ref[bc[0], rows, bc[1]]), chains, where)
            dvn = _each(lambda i, pi, di: pi[:CH] + dot(kd_ref[i, c], di), chains, p, ds)
            wd = _each(lambda i, vi: dot(wt_ref[i, c], vi), chains, dvn)
            for i, (b, cols) in zip(chains, where):
                dvn_ref[b, rows, cols] = dvn[i].astype(BF16)
                ds_scr[i] = ds[i] * gl_ref[i, c] + p[i][CH:] - wd[i]

    return pl.pallas_call(
        body, name="gdn_scan_bwd", grid=(ngrp,),
        in_specs=[seq(), per_chunk(3 * CH, CH), per_chunk(DH, CH), per_chunk(CH, DH), per_chunk(1, DH)],
        out_specs=[seq(), per_chunk(DH, DH)],
        out_shape=[jax.ShapeDtypeStruct((bsz, lp, GW), BF16), jax.ShapeDtypeStruct((bh, nchunk, DH, DH), F32)],
        scratch_shapes=[pltpu.VMEM((bh, DH, DH), F32)],
        compiler_params=_cparams(("arbitrary",)),
    )(d_o.reshape(bsz, lp, GW), l3, wt, kd, gl)


def _gdn_local_bwd(qkv, gcb, bb, gct, tinv_all, states, dstates, vnew, dvnew, d_o, bsz, lp):
    nchunk = lp // CH
    grp = _local_group(nchunk)
    ngrp, head, lane_b, per_chunk = _gdn_local_specs(lp, grp)
    bh = bsz * HEADS

    def body(q_ref, k_ref, v_ref, gc_ref, be_ref, gt_ref, ti_ref, st_ref, dst_ref, vn_ref, dvn_ref, do_ref,
             dq_ref, dk_ref, dv_ref, dg_ref, db_ref):
        incl, strict = _tri(CH, "incl"), _tri(CH, "strict")
        upper = _tri(CH, "upper").astype(F32)
        ones = jnp.ones((CH, DH), F32)
        last = lax.broadcasted_iota(jnp.int32, (CH, 1), 0) == CH - 1

        def rsum(x):
            return jnp.sum(x, axis=-1, keepdims=True)

        def mul(xs, ys):
            return _each(lambda x, y: x * y, xs, ys)

        rows, be, dec, eg, ekd, gl = zip(*[_chunk_gates(gc_ref, be_ref, gt_ref, g) for g in range(grp)])
        q = [q_ref[r, :].astype(F32) for r in rows]
        k = [k_ref[r, :].astype(F32) for r in rows]
        v = [v_ref[r, :].astype(F32) for r in rows]
        vnew = [vn_ref[r, :] for r in rows]
        dvn = [dvn_ref[r, :] for r in rows]
        do = [do_ref[r, :] for r in rows]
        tinv = [ti_ref[0, g] for g in range(grp)]
        s = [st_ref[0, g] for g in range(grp)]
        ds = [dst_ref[0, g] for g in range(grp)]
        kb, vb, qd, kd = mul(k, be), mul(v, be), mul(q, eg), mul(k, ekd)
        kbg = mul(kb, eg)
        a = _each(lambda x, y, d: jnp.where(strict, _bdot(x, y, NT) * d, 0.0), kb, k, dec)
        qk = _each(lambda x, y, d: jnp.where(incl, _bdot(x, y, NT) * d, 0.0), q, k, dec)
        dqk = _each(lambda x, y: jnp.where(incl, _bdot(x, y, NT), 0.0), do, vnew)
        dqd = _each(lambda x, y: _bdot(x, y, NT), do, s)
        dkd = _each(lambda x, y: _bdot(x, y, NT), vnew, ds)
        dgl = _each(lambda x, y: jnp.sum(rsum(x * y), axis=0, keepdims=True), s, ds)
        dw = _each(lambda x, y: -_bdot(x, y, NT), dvn, s)
        dvb = _each(lambda t, x: _bdot(t, x, TN), tinv, dvn)
        dkbg = _each(lambda t, x: _bdot(t, x, TN), tinv, dw)
        dt = _each(lambda x, y, z, w: _bdot(x, y, NT) + _bdot(z, w, NT), dvn, vb, dw, kbg)
        tdt = _each(lambda t, x: _split_dot(t, x, TN), tinv, dt)
        da = _each(lambda x, t: jnp.where(strict, -_split_dot(x, t, NT), 0.0), tdt, tinv)
        dp, dr = mul(da, dec), mul(dqk, dec)
        dkb = _each(lambda x, y, z, e: _bdot(x, y, NN) + z * e, dp, k, dkbg, eg)
        dk = _each(lambda p_, kb_, r_, q_, dkd_, ekd_, dkb_, be_:
                   _bdot(p_, kb_, TN) + _bdot(r_, q_, TN) + dkd_ * ekd_ + dkb_ * be_,
                   dp, kb, dr, q, dkd, ekd, dkb, be)
        dq = _each(lambda r_, k_, dqd_, eg_: _bdot(r_, k_, NN) + dqd_ * eg_, dr, k, dqd, eg)
        m = _each(lambda da_, a_, dqk_, qk_: da_ * a_ + dqk_ * qk_, da, a, dqk, qk)
        m_cols = _each(lambda x: _split_dot(x, ones, TN), m)
        kd_term = _each(lambda x, y: rsum(x * y), dkd, kd)
        dgc = _each(lambda m_, mc, dqd_, qd_, kt, dkbg_, kbg_, dgl_, gl_:
                    rsum(m_) - mc + rsum(dqd_ * qd_) - kt + rsum(dkbg_ * kbg_)
                    + jnp.where(last, jnp.sum(kt, axis=0, keepdims=True) + dgl_ * gl_, 0.0),
                    m, m_cols, dqd, qd, kd_term, dkbg, kbg, dgl, gl)
        dg = _each(lambda x: _split_dot(upper, x, NN), dgc)
        for g in range(grp):
            dq_ref[rows[g], :] = dq[g].astype(BF16)
            dk_ref[rows[g], :] = dk[g].astype(BF16)
            dv_ref[rows[g], :] = (dvb[g] * be[g]).astype(BF16)
            dg_ref[0, rows[g], :] = dg[g]
            db_ref[0, rows[g], :] = jnp.broadcast_to(rsum(dkb[g] * k[g]) + rsum(dvb[g] * v[g]), (CH, DH))

    r = bsz * lp
    return pl.pallas_call(
        body, name="gdn_local_bwd", grid=(bh, ngrp),
        in_specs=[head(0), head(GW), head(2 * GW), lane_b, lane_b, per_chunk(1, CH), per_chunk(CH, CH),
                  per_chunk(DH, DH), per_chunk(DH, DH), head(0), head(0), head(0)],
        out_specs=[head(0), head(0), head(0), lane_b, lane_b],
        out_shape=[jax.ShapeDtypeStruct((r, GW), BF16)] * 3 + [jax.ShapeDtypeStruct((bh, lp, DH), F32)] * 2,
        compiler_params=_cparams(("parallel", "parallel")),
    )(qkv, qkv, qkv, gcb, bb, gct, tinv_all, states, dstates, vnew, dvnew, d_o)


def _place():
    return lax.axis_index("x"), lax.axis_index("y"), lax.axis_index("c")


def _any_specs(n):
    return [pl.BlockSpec(memory_space=pl.ANY)] * n


def _all_gather(name, shards):
    n = len(shards)

    def body(*refs):
        ins, outs = refs[:n], refs[n:2 * n]
        send_sems, recv_sems, local_sems = refs[2 * n:2 * n + 3]
        stage = refs[2 * n + 3:]
        x, y, c = _place()
        me, sibling = (x, y, c), (x, y, 1 - c)
        chips = [(1 - x, y), (x, 1 - y), (1 - x, 1 - y)]

        def slot(i, p):
            return outs[i].at[4 * p[0] + 2 * p[1] + p[2]]

        def copy(i, k, block, to, src=None):
            return pltpu.make_async_remote_copy(
                src_ref=slot(i, block) if src is None else src, dst_ref=slot(i, block),
                send_sem=send_sems.at[i, k], recv_sem=recv_sems.at[i, k], device_id=to, device_id_type=MESH)

        load = [pltpu.make_async_copy(ins[i], stage[i], local_sems.at[i]) for i in range(n)]
        for cp in load:
            cp.start()
        first = []
        for i in range(n):
            first.append(copy(i, 0, me, sibling, src=ins[i]))
            first += [copy(i, 1 + j, me, (*chip, c), src=ins[i]) for j, chip in enumerate(chips)]
        for cp in first:
            cp.start()
        for cp in load:
            cp.wait()
        mine = [pltpu.make_async_copy(stage[i], slot(i, me), local_sems.at[i]) for i in range(n)]
        for cp in mine:
            cp.start()
        passed = []
        for j, chip in enumerate(chips):
            for i in range(n):
                copy(i, 1 + j, (*chip, c), me).wait_recv()
                fwd = copy(i, 4 + j, (*chip, c), sibling)
                fwd.start()
                passed.append(fwd)
        for i in range(n):
            copy(i, 0, sibling, me).wait_recv()
            for j, chip in enumerate(chips):
                copy(i, 4 + j, (*chip, 1 - c), me).wait_recv()
        for cp in first + passed:
            cp.wait_send()
        for cp in mine:
            cp.wait()

    return pl.pallas_call(
        body, name=name,
        in_specs=_any_specs(n), out_specs=_any_specs(n),
        out_shape=[jax.ShapeDtypeStruct((N_DEV,) + s.shape, s.dtype) for s in shards],
        scratch_shapes=[pltpu.SemaphoreType.DMA((n, 7)), pltpu.SemaphoreType.DMA((n, 7)),
                        pltpu.SemaphoreType.DMA((n,))] + [pltpu.VMEM(s.shape, s.dtype) for s in shards],
    )(*shards)


HBM_SPEC = pl.BlockSpec(memory_space=pltpu.HBM)
SEM_SPEC = pl.BlockSpec(memory_space=pltpu.SEMAPHORE)
EFFECT = pltpu.SideEffectType.DATAFLOW_SIDE_EFFECTING
N_PEER = N_DEV - 1


def _peer(r):
    x, y, c = _place()
    return ((1 - x) if r & 4 else x, (1 - y) if r & 2 else y, (1 - c) if r & 1 else c)


def _slot_of(p):
    return 4 * p[0] + 2 * p[1] + p[2]


def _hbm(a):
    return pltpu.with_memory_space_constraint(a, pltpu.HBM)


def _gather_pattern(src, land):
    me = _place()
    return [(src, land.at[_slot_of(me)], _peer(r), land.at[_slot_of(_peer(r))]) for r in range(1, N_DEV)]


def _alltoall_pattern(src, land):
    me = _place()
    return [(src.at[_slot_of(_peer(r))], land.at[_slot_of(me)], _peer(r), land.at[_slot_of(_peer(r))])
            for r in range(1, N_DEV)]


def _chipcast_pattern(src, land):
    x, y, c = _place()
    me = (x, y, c)
    peers = [(x, y, 1 - c), (1 - x, y, c), (x, 1 - y, c), (1 - x, 1 - y, c)]
    return [(src, land.at[_slot_of(me)], p, land.at[_slot_of(p)]) for p in peers]


def _forward_pattern(land, _):
    x, y, c = _place()
    chips = [(1 - x, y), (x, 1 - y), (1 - x, 1 - y)]
    return [(land.at[_slot_of((*ch, c))], land.at[_slot_of((*ch, c))], (x, y, 1 - c), land.at[_slot_of((*ch, 1 - c))])
            for ch in chips]


def _sibling_pattern(src, land):
    x, y, c = _place()
    return [(src.at[ch, 1 - c], land.at[ch], (x, y, 1 - c), land.at[ch]) for ch in range(4)]


def _chip_pattern(src, land):
    x, y, c = _place()
    chips = [(1 - x, y), (x, 1 - y), (1 - x, 1 - y)]
    return [(src.at[2 * ch[0] + ch[1]], land.at[j], (*ch, c), land.at[j]) for j, ch in enumerate(chips)]


def _split_start(name, groups, pattern, land_shape, after=None):
    sizes = [len(g) for g in groups]
    arrays = [s for g in groups for s in g]
    n = len(arrays)
    inplace = land_shape is None
    fresh = [] if inplace else [lax.empty(land_shape(s.shape), s.dtype) for s in arrays]
    bufs = arrays + fresh
    order = [] if after is None else [after]

    def body(*refs):
        srcs, lands = refs[:n], (refs[:n] if inplace else refs[n:2 * n])
        first_out = len(bufs) + len(order)
        sems = refs[first_out:first_out + 2 * len(groups)]
        token = refs[-1]
        k = 0
        for gi, size in enumerate(sizes):
            for i in range(size):
                copies = pattern(srcs[k], lands[k])
                for j, (s, d, peer, _) in enumerate(copies):
                    pltpu.make_async_remote_copy(
                        src_ref=s, dst_ref=d, send_sem=sems[2 * gi].at[i * len(copies) + j],
                        recv_sem=sems[2 * gi + 1].at[i * len(copies) + j], device_id=peer, device_id_type=MESH).start()
                k += 1
        token[...] = jnp.zeros_like(token)

    per_array = {_gather_pattern: N_PEER, _alltoall_pattern: N_PEER, _chipcast_pattern: 4, _forward_pattern: 3,
                 _sibling_pattern: 4, _chip_pattern: 3}[pattern]
    sem_shapes = [pltpu.SemaphoreType.DMA((size * per_array,)) for size in sizes for _ in range(2)]
    res = pl.pallas_call(
        body, name=name,
        in_specs=[HBM_SPEC] * len(bufs) + [pl.BlockSpec(memory_space=pl.ANY)] * len(order),
        out_specs=[SEM_SPEC] * len(sem_shapes) + [HBM_SPEC] * len(bufs) + [pl.BlockSpec(memory_space=pltpu.VMEM)],
        out_shape=sem_shapes + [pltpu.HBM(s.shape, s.dtype) for s in bufs] + [jax.ShapeDtypeStruct((8, DH), F32)],
        input_output_aliases={i: len(sem_shapes) + i for i in range(len(bufs))},
        compiler_params=pltpu.CompilerParams(has_side_effects=EFFECT),
    )(*[_hbm(s) for s in bufs], *order)
    nsem = len(sem_shapes)
    thru = list(res[nsem:nsem + len(bufs)])
    out, k = [], 0
    for gi, size in enumerate(sizes):
        srcs_out = [] if inplace else thru[k:k + size]
        lands_out = thru[k:k + size] if inplace else thru[n + k:n + k + size]
        out.append((res[2 * gi], res[2 * gi + 1], srcs_out, lands_out))
        k += size
    return out, res[-1]


def _split_wait(name, group, pattern, after):
    send_sems, recv_sems, arrays, lands = group
    n = len(lands)
    inplace = not arrays
    bufs = list(arrays) + list(lands)
    fill_own = pattern in (_gather_pattern, _chipcast_pattern, _alltoall_pattern)
    own_of = (lambda a: a.at[_slot_of(_place())]) if pattern is _alltoall_pattern else (lambda a: a)
    own_shape = (lambda a: a.shape[1:]) if pattern is _alltoall_pattern else (lambda a: a.shape)

    def body(*refs):
        srcs, lands_ = refs[:n], (refs[:n] if inplace else refs[n:2 * n])
        s_sems, r_sems = refs[len(bufs)], refs[len(bufs) + 1]
        own = []
        if fill_own:
            stage, local_sems = refs[-n - 1:-1], refs[-1]
            load = [pltpu.make_async_copy(own_of(srcs[i]), stage[i], local_sems.at[i]) for i in range(n)]
            for cp in load:
                cp.start()
            for cp in load:
                cp.wait()
            own = [pltpu.make_async_copy(stage[i], lands_[i].at[_slot_of(_place())], local_sems.at[i])
                   for i in range(n)]
            for cp in own:
                cp.start()
        for i in range(n):
            copies = pattern(srcs[i], lands_[i])
            for j, (s, _, peer, filled) in enumerate(copies):
                cp = pltpu.make_async_remote_copy(
                    src_ref=s, dst_ref=filled, send_sem=s_sems.at[i * len(copies) + j],
                    recv_sem=r_sems.at[i * len(copies) + j], device_id=peer, device_id_type=MESH)
                cp.wait_send()
                cp.wait_recv()
        for cp in own:
            cp.wait()

    res = pl.pallas_call(
        body, name=name,
        in_specs=[HBM_SPEC] * len(bufs) + [SEM_SPEC, SEM_SPEC, pl.BlockSpec(memory_space=pl.ANY)],
        out_specs=[HBM_SPEC] * len(bufs),
        out_shape=[pltpu.HBM(s.shape, s.dtype) for s in bufs],
        scratch_shapes=([pltpu.VMEM(own_shape(a), a.dtype) for a in arrays] + [pltpu.SemaphoreType.DMA((n,))])
        if fill_own else [],
        input_output_aliases={i: i for i in range(len(bufs))},
        compiler_params=pltpu.CompilerParams(has_side_effects=EFFECT),
    )(*bufs, send_sems, recv_sems, after)
    return ([], list(res)) if inplace else (list(res[:n]), list(res[n:]))


def _pair_sum(name, grads, recv, core):
    _, _, r, c = grads.shape
    tr = r if r * c <= (1 << 20) or r % 16 else _div_tile(r, 512, 16)

    def body(core_ref, g_ref, r_ref, o_ref):
        o_ref[...] = (g_ref[0].astype(F32) + r_ref[...].astype(F32)).astype(o_ref.dtype)

    return pl.pallas_call(
        body, name=name,
        grid_spec=pltpu.PrefetchScalarGridSpec(
            num_scalar_prefetch=1, grid=(4, r // tr),
            in_specs=[pl.BlockSpec((1, 1, tr, c), lambda s, i, cr: (s, cr[0], i, 0)),
                      pl.BlockSpec((1, tr, c), lambda s, i, cr: (s, i, 0))],
            out_specs=pl.BlockSpec((1, tr, c), lambda s, i, cr: (s, i, 0))),
        out_shape=jax.ShapeDtypeStruct(recv.shape, recv.dtype),
        compiler_params=_cparams(("parallel", "parallel")),
    )(core, grads, recv)


def _adamw_math(w, g, m, v):
    m = ADAM_B1 * m + (1.0 - ADAM_B1) * g
    v = ADAM_B2 * v + (1.0 - ADAM_B2) * (g * g)
    m_hat = m / (1.0 - ADAM_B1 ** ADAM_STEP)
    v_hat = v / (1.0 - ADAM_B2 ** ADAM_STEP)
    delta = -ADAM_LR * (m_hat / (jnp.sqrt(v_hat) + ADAM_EPS) + ADAM_WD * w)
    return delta, m, v


SMALL_LEAVES = (
    ("mix_pre_norm", 0, 1, 0, D, (1, D)), ("mix_post_norm", 1, 1, 0, D, (1, D)),
    ("ffn_pre_norm", 2, 1, 0, D, (1, D)), ("ffn_post_norm", 3, 1, 0, D, (1, D)),
    ("meta_tokens", 8, N_META, 0, DH, (N_META, DH)),
    ("conv_qkv", 24, KQ, 0, 3 * GW // N_DEV, (1, KQ, 3 * GW // N_DEV)),
    ("conv_sc", 32, KS, 0, SW // N_DEV, (1, KS, SW // N_DEV)),
    ("a_log", 40, 1, 0, HEADS, (1, HEADS)), ("dt_bias", 40, 1, DH, HEADS, (1, HEADS)),
    ("gdn_norm", 40, 1, 2 * DH, DH, (1, DH)),
)
SMALL_LOSS_AT = (40, 3 * DH)


def _adamw_small(slabs, late, w, m, v):
    nslot = slabs.shape[0]
    nleaf = len(SMALL_LEAVES)

    def body(p_ref, l_ref, w_ref, m_ref, v_ref, *refs):
        outs, loss_ref, scr = refs[:4 * nleaf], refs[4 * nleaf], refs[4 * nleaf + 1]
        g = p_ref[0]
        gain = l_ref[0, pl.ds(0, 1), :]
        meta = l_ref[0, pl.ds(8, N_META), pl.ds(0, DH)]
        for s in range(1, nslot):
            g = g + p_ref[s]
            gain = gain + l_ref[s, pl.ds(0, 1), :]
            meta = meta + l_ref[s, pl.ds(8, N_META), pl.ds(0, DH)]
        scr[0] = g
        scr[0, pl.ds(0, 1), pl.ds(0, D)] += gain
        scr[0, pl.ds(8, N_META), pl.ds(0, DH)] += meta
        g = scr[0]
        d, mn, vn = _adamw_math(w_ref[...], g, m_ref[...], v_ref[...])
        for kind, val in enumerate((g, d, mn, vn)):
            scr[kind] = val
        for kind in range(4):
            for li, (_, r0, nr, c0, nc, shape) in enumerate(SMALL_LEAVES):
                val = scr[kind, pl.ds(r0, nr), pl.ds(c0, nc)]
                o = outs[kind * nleaf + li]
                if len(shape) == 3:
                    o[0] = val
                else:
                    o[...] = val
        loss_ref[...] = scr[0, pl.ds(SMALL_LOSS_AT[0], 1), pl.ds(SMALL_LOSS_AT[1], 1)]

    whole = lambda a: pl.BlockSpec(a.shape, lambda: (0,) * a.ndim)
    out_shapes = [jax.ShapeDtypeStruct(shape, F32) for _ in range(4) for (*_, shape) in SMALL_LEAVES]
    out_shapes.append(jax.ShapeDtypeStruct((1, 1), F32))
    res = pl.pallas_call(
        body, name="adamw_small",
        in_specs=[whole(slabs), whole(late), whole(w), whole(m), whole(v)],
        out_specs=[pl.BlockSpec(s.shape, lambda n=len(s.shape): (0,) * n) for s in out_shapes],
        out_shape=out_shapes,
        scratch_shapes=[pltpu.VMEM((4,) + w.shape, F32)],
        compiler_params=pltpu.CompilerParams(vmem_limit_bytes=VMEM_LIMIT),
    )(slabs, late, w, m, v)
    kinds = [{nm: res[kind * nleaf + li] for li, (nm, *_) in enumerate(SMALL_LEAVES)} for kind in range(4)]
    return kinds, res[-1]


def _adamw_reduced(name, parts, chip, recv, w, m, v, after=None):
    _, r, c = parts.shape
    tr = r if r * c <= (1 << 20) or r % 16 else _div_tile(r, 512, 16)
    order = [] if after is None else [after]

    def body(chip_ref, p_ref, r_ref, w_ref, m_ref, v_ref, *rest):
        g_out, d_out, m_out, v_out = rest[len(order):]
        g = p_ref[0].astype(F32)
        for s in range(3):
            g = g + r_ref[s].astype(F32)
        d, mn, vn = _adamw_math(w_ref[...], g, m_ref[...], v_ref[...])
        g_out[...] = g
        d_out[...] = d
        m_out[...] = mn
        v_out[...] = vn

    blk = pl.BlockSpec((tr, c), lambda i, ch: (i, 0))
    return pl.pallas_call(
        body, name=name,
        grid_spec=pltpu.PrefetchScalarGridSpec(
            num_scalar_prefetch=1, grid=(r // tr,),
            in_specs=[pl.BlockSpec((1, tr, c), lambda i, ch: (ch[0], i, 0)),
                      pl.BlockSpec((3, tr, c), lambda i, ch: (0, i, 0)), blk, blk, blk]
            + [pl.BlockSpec((8, DH), lambda i, ch: (0, 0))] * len(order),
            out_specs=[blk] * 4),
        out_shape=[jax.ShapeDtypeStruct((r, c), F32)] * 4,
        compiler_params=_cparams(("parallel",)),
    )(chip, parts, recv, w, m, v, *order)


SMALL_ROWS, SMALL_COLS = 48, 3 * GW


def _pack_small(gains4, meta, conv_qkv, conv_sc, a_log, dt_bias, gdn_norm, extra=None):
    def tile(a):
        return jnp.pad(a, ((0, -a.shape[0] % 8), (0, SMALL_COLS - a.shape[1])))
    scal = jnp.concatenate([jnp.pad(a_log, ((0, 0), (0, DH - HEADS))), jnp.pad(dt_bias, ((0, 0), (0, DH - HEADS))),
                            gdn_norm] + ([] if extra is None else [extra]), axis=1)
    gains = sum(jnp.pad(g, ((i, 7 - i), (0, SMALL_COLS - g.shape[1]))) for i, g in enumerate(gains4))
    parts = [gains, tile(meta), tile(conv_qkv), tile(conv_sc), tile(scal)]
    slab = jnp.concatenate(parts, axis=0)
    assert slab.shape == (SMALL_ROWS, SMALL_COLS)
    return slab


def kernel(x, meta_tokens, mix_pre_norm, mix_post_norm, ffn_pre_norm, ffn_post_norm, w_in, conv_qkv, a_log, dt_bias, gdn_norm, conv_sc, w_out, w_gate, w_up, w_down, loss_target, m_meta_tokens, m_mix_pre_norm, m_mix_post_norm, m_ffn_pre_norm, m_ffn_post_norm, m_w_in, m_conv_qkv, m_a_log, m_dt_bias, m_gdn_norm, m_conv_sc, m_w_out, m_w_gate, m_w_up, m_w_down, v_meta_tokens, v_mix_pre_norm, v_mix_post_norm, v_ffn_pre_norm, v_ffn_post_norm, v_w_in, v_conv_qkv, v_a_log, v_dt_bias, v_gdn_norm, v_conv_sc, v_w_out, v_w_gate, v_w_up, v_w_down):
    bsz, seq, _ = x.shape
    lp = -(-(N_META + seq) // CH) * CH
    padf = lp - N_META - seq
    first_real = padf + N_META
    rows = bsz * lp
    nchunk = lp // CH
    tr = _div_tile(lp, 528)
    tm = _div_tile(rows, 1056)
    tm_big = _div_tile(rows, 2112)
    cx, cy, cc = _place()

    small_shard = jnp.concatenate([
        jnp.pad(meta_tokens, ((0, 0), (0, 256 - DH))),
        jnp.pad(conv_qkv[0], ((0, 0), (0, 256 - 3 * GW // N_DEV))),
        jnp.pad(conv_sc[0], ((0, 5), (0, 256 - SW // N_DEV)))], axis=0)
    g_in, g_small = _all_gather("gather_w_in", [w_in[0].T.astype(BF16), small_shard])
    my_rest = [w_out[0].astype(BF16), w_gate[0].T.astype(BF16), w_up[0].T.astype(BF16), w_down[0].astype(BF16)]
    (grp_rest,), token = _split_start("gather_rest_start", [my_rest], _chipcast_pattern, lambda s: (N_DEV,) + s,
                                      after=g_in)
    after_start = token[0:1, 0:1]

    def cols_full(g):
        return jnp.transpose(g, (1, 0, 2)).reshape(g.shape[1], -1)

    win_t = g_in.reshape(IN_W, D)
    win_t = jnp.concatenate([win_t[:BA0 + 2 * HEADS], jnp.zeros((BA_W - 2 * HEADS, D), BF16),
                             win_t[BA0 + 2 * HEADS:]], axis=0)
    meta_full = cols_full(g_small[:, 0:16, :DH])
    convq_full = cols_full(g_small[:, 16:20, :3 * GW // N_DEV])
    convs_full = cols_full(g_small[:, 20:23, :SW // N_DEV])
    alog_v = jnp.pad(a_log, ((0, 0), (HEADS, DH - 2 * HEADS)))
    dtb_v = jnp.pad(dt_bias, ((0, 0), (HEADS, DH - 2 * HEADS)))

    head_rows = jnp.concatenate([jnp.zeros((padf, D), F32), meta_full], axis=0)
    h0, u1 = _pre_norm(x, head_rows, mix_pre_norm + after_start, bsz, lp, tr)
    (proj,) = _mm("proj_in", [(u1, win_t)], "nt", tm_big, 768, D, out_dtypes=(BF16,))
    (ba,) = _mm("proj_gate_logits", [(u1, win_t[BA0:SX0])], "nt", tm, BA_W, D)
    qkv = _gdn_prep(proj, convq_full, bsz, lp, padf)
    gb, gcs = _gates(ba, alog_v, dtb_v, bsz, lp, padf)

    def lane_bcast(cols):
        t = jnp.transpose(cols.reshape(bsz, lp, HEADS), (0, 2, 1)).reshape(bsz * HEADS, lp, 1)
        return jnp.broadcast_to(t, (bsz * HEADS, lp, DH))

    bb = lane_bcast(gb[:, 0:HEADS])
    gcb = lane_bcast(gcs[:, HEADS:2 * HEADS])
    gct = jnp.transpose(gcs[:, HEADS:2 * HEADS].reshape(bsz, nchunk, CH, HEADS), (0, 3, 1, 2)).reshape(
        bsz * HEADS, nchunk, 1, CH)
    u_loc, l1, l2, l3, w_t, k_dec, tinv_all, g_last = _gdn_local_fwd(qkv, gcb, bb, gct, bsz, lp)
    o, v_new, states = _gdn_scan_fwd(u_loc, l1, l2, g_last, bsz, lp)
    o, v_new = o.reshape(rows, GW), v_new.reshape(rows, GW)
    _, l_rest = _split_wait("gather_rest_arrive", grp_rest, _chipcast_pattern, o)
    (grp_out, grp_ffn, grp_down), token = _split_start(
        "gather_rest_forward", [l_rest[0:1], l_rest[1:3], l_rest[3:4]], _forward_pattern, None)
    mix_g, mix_s = _mix_heads(proj, o, gdn_norm + token[0:1, :], convs_full, bsz, lp, padf)
    mixin = jnp.concatenate([mix_g, mix_s], axis=1)
    _, (l_out,) = _split_wait("gather_w_out_wait", grp_out, _forward_pattern, mixin)
    wout = l_out.reshape(D, D)
    (mix,) = _mm("proj_out", [(mixin, wout)], "nn", tm, D, D, out_dtypes=(BF16,))
    h1, u2 = _mid_norms(h0, mix, mix_post_norm, ffn_pre_norm, bsz, lp, tr)
    _, (l_gate, l_up) = _split_wait("gather_ffn_wait", grp_ffn, _forward_pattern, u2)
    wgate_t, wup_t = l_gate.reshape(FF, D), l_up.reshape(FF, D)
    gate, up, act = _ffn_up(u2, wgate_t, wup_t, tm, 1408)
    _, (l_down,) = _split_wait("gather_down_wait", grp_down, _forward_pattern, act)
    wdown = l_down.reshape(FF, D)
    (ffn,) = _mm("ffn_down", [(act, wdown)], "nn", tm, D, 1408, out_dtypes=(BF16,))

    dy, dffn, d_g4, loss_part = _loss_head(h1, ffn, loss_target, ffn_post_norm, bsz, lp, tr, first_real)
    (dwdown,) = _mm("dw_down", [(act, dffn)], "tn", 1408, D, tm, out_dtypes=(BF16,))

    def swiglu_bwd(da, g, u):
        g, u = g.astype(F32), u.astype(F32)
        return da * u * _dsilu(g), da * _silu(g)

    dgate, dup = _mm("d_act", [(dffn, wdown)], "nt", tm, 1408, D, out_dtypes=(BF16, BF16),
                     epilogue=swiglu_bwd, extras=(gate, up))
    (dwgate_t,) = _mm("dw_gate", [(dgate, u2)], "tn", 1408, D, tm, out_dtypes=(BF16,))
    (dwup_t,) = _mm("dw_up", [(dup, u2)], "tn", 1408, D, tm, out_dtypes=(BF16,))
    (du2,) = _mm("d_u2", [(dgate, wgate_t), (dup, wup_t)], "nn", tm, D, 1408, out_dtypes=(BF16,))
    dh1, dmix, d_g3, d_g2 = _mid_norms_bwd(h1, mix, du2, dy, ffn_pre_norm, mix_post_norm, bsz, lp, tr)
    (dwout,) = _mm("dw_out", [(mixin, dmix)], "tn", D, D, tm, out_dtypes=(BF16,))

    def row_blocks(g):
        return g.reshape(4, 2, -1, g.shape[1])

    core = jnp.reshape(cc, (1,)).astype(jnp.int32)
    chip = jnp.reshape(2 * cx + cy, (1,)).astype(jnp.int32)

    def reduce_start(tag, grads):
        (grp,), tok = _split_start("rs_sibling_start_" + tag, [grads], _sibling_pattern, lambda s: s[:1] + s[2:])
        return grp, tok

    def reduce_mid(tag, grp, nms, after):
        mine, lands = _split_wait("rs_sibling_wait_" + tag, grp, _sibling_pattern, after)
        parts = [_pair_sum("pair_sum_" + nm, g, r, core) for nm, g, r in zip(nms, mine, lands)]
        (grp2,), tok = _split_start("rs_chip_start_" + tag, [parts], _chip_pattern, lambda s: (3,) + s[1:])
        return grp2, tok

    def reduce_end(tag, grp2, after):
        return _split_wait("rs_chip_wait_" + tag, grp2, _chip_pattern, after)

    names_ffn = ["w_down", "w_gate", "w_up", "w_out"]
    rs_ffn, tok = reduce_start("ffn", [row_blocks(dwdown), row_blocks(dwgate_t), row_blocks(dwup_t), row_blocks(dwout)])
    (dmixin,) = _mm("d_mixin", [(dmix, wout)], "nt", tm, D, D, out_dtypes=(BF16,), after=tok)
    d_o, dz, dsx, dsb, dsc, d_gn, d_convs = _mix_heads_bwd(proj, o, gdn_norm, convs_full, dmixin, bsz, lp, padf)
    rs_ffn, tok = reduce_mid("ffn", rs_ffn, names_ffn, d_o)
    dv_new, dstates = _gdn_scan_bwd(d_o, l3, w_t, k_dec, g_last + tok[0, 0], bsz, lp)
    dq, dk, dv, dg_b, dbeta_b = _gdn_local_bwd(qkv, gcb, bb, gct, tinv_all, states, dstates, v_new,
                                               dv_new.reshape(rows, GW), d_o, bsz, lp)
    dpqkv, d_convq = _gdn_prep_bwd(proj, convq_full, dq, dk, dv, bsz, lp, padf)
    dba, d_scal = _gates_bwd(ba, alog_v, dtb_v, dbeta_b, dg_b, gb, bsz, lp, padf)
    dproj = jnp.concatenate([dpqkv, dz, dba, dsx, dsb, dsc], axis=1)

    d_alog = d_scal[0:1, HEADS:2 * HEADS]
    d_dtb = d_scal[1:2, HEADS:2 * HEADS]
    cq_w, cs_w = 3 * GW // N_DEV, SW // N_DEV
    slab = jnp.stack([
        _pack_small([jnp.zeros_like(d_g2), d_g2, d_g3, d_g4], jnp.zeros((N_META, DH), F32),
                    d_convq[:, p * cq_w:(p + 1) * cq_w], d_convs[:, p * cs_w:(p + 1) * cs_w], d_alog, d_dtb, d_gn,
                    loss_part) for p in range(N_DEV)])
    (grp_small,), tok = _split_start("small_grads_start", [[slab]], _alltoall_pattern, lambda s: s)
    (dwin_t,) = _mm("dw_in", [(dproj, u1)], "tn", 1280, D, tm, out_dtypes=(BF16,), after=tok)
    parts_ffn, recv_ffn = reduce_end("ffn", rs_ffn, dwin_t)
    dwin_t = jnp.concatenate([dwin_t[:BA0 + 2 * HEADS], dwin_t[SX0:]], axis=0)
    rs_in, tok = reduce_start("in", [row_blocks(dwin_t)])

    big_w = {"w_in": (w_in, m_w_in, v_w_in), "w_out": (w_out, m_w_out, v_w_out), "w_gate": (w_gate, m_w_gate, v_w_gate),
             "w_up": (w_up, m_w_up, v_w_up), "w_down": (w_down, m_w_down, v_w_down)}

    def update(nm, parts, recv, after=None):
        w, m, v = big_w[nm]
        if nm in ("w_out", "w_down"):
            return _adamw_reduced("adamw_" + nm, parts, chip, recv, w[0], m[0], v[0], after)
        res = _adamw_reduced("adamw_" + nm, parts, chip, recv, w[0].T, m[0].T, v[0].T, after)
        return [t.T for t in res]

    big_out = {nm: update(nm, p, r, tok) for nm, p, r in zip(names_ffn, parts_ffn, recv_ffn)}
    ffn_done = sum(big_out[nm][1][:1, :1] for nm in names_ffn)
    rs_in, tok = reduce_mid("in", rs_in, ["w_in"], ffn_done)
    (du1,) = _mm("d_u1", [(dproj, win_t)], "nn", tm_big, D, 768, out_dtypes=(BF16,), after=tok)
    grad_x, d_g1, d_meta = _pre_norm_bwd(h0, du1, dh1, mix_pre_norm, bsz, lp, seq, padf)

    late = jnp.stack([
        jnp.concatenate([jnp.pad(d_g1, ((0, 7), (0, 0))), jnp.pad(d_meta[:, p * DH:(p + 1) * DH], ((0, 0), (0, D - DH)))],
                        axis=0) for p in range(N_DEV)])
    (grp_late,), _ = _split_start("late_grads_start", [[late]], _alltoall_pattern, lambda s: s)
    _, (slabs_local,) = _split_wait("small_grads_wait", grp_small, _alltoall_pattern, grad_x)
    _, (late_local,) = _split_wait("late_grads_wait", grp_late, _alltoall_pattern, slabs_local)
    w_small = _pack_small([mix_pre_norm, mix_post_norm, ffn_pre_norm, ffn_post_norm], meta_tokens, conv_qkv[0],
                          conv_sc[0], a_log, dt_bias, gdn_norm)
    m_small = _pack_small([m_mix_pre_norm, m_mix_post_norm, m_ffn_pre_norm, m_ffn_post_norm], m_meta_tokens,
                          m_conv_qkv[0], m_conv_sc[0], m_a_log, m_dt_bias, m_gdn_norm)
    v_small = _pack_small([v_mix_pre_norm, v_mix_post_norm, v_ffn_pre_norm, v_ffn_post_norm], v_meta_tokens,
                          v_conv_qkv[0], v_conv_sc[0], v_a_log, v_dt_bias, v_gdn_norm)
    small_out, loss = _adamw_small(slabs_local, late_local, w_small, m_small, v_small)
    parts_in, recv_in = reduce_end("in", rs_in, small_out[1]["gdn_norm"])
    big_out["w_in"] = update("w_in", parts_in[0], recv_in[0])
    names = ["w_in", "w_out", "w_gate", "w_up", "w_down"]

    order = ["meta_tokens", "mix_pre_norm", "mix_post_norm", "ffn_pre_norm", "ffn_post_norm", "w_in", "conv_qkv",
             "a_log", "dt_bias", "gdn_norm", "conv_sc", "w_out", "w_gate", "w_up", "w_down"]
    outs = [loss[0, 0], grad_x]
    for kind in range(4):
        leaves = dict(small_out[kind])
        for nm in names:
            leaves[nm] = big_out[nm][kind][None]
        outs += [leaves[nm] for nm in order]
    return tuple(outs)
```

```python
import jax
import jax.numpy as jnp
from jax import lax
from jax.experimental import pallas as pl
from jax.experimental.pallas import tpu as pltpu

F32 = jnp.float32
BF16 = jnp.bfloat16
MESH = pl.DeviceIdType.MESH

D = 1024
N_META = 16
HEADS = 4
DH = 128
GW = HEADS * DH
SW = D - GW
FF = 2816
CH = 64
KQ = 4
KS = 3
EPS = 1e-6
IN_W = 3 * GW + GW + 2 * HEADS + 3 * SW
Q0, Z0, BA0, SX0, SB0, SC0 = 0, 1536, 2048, 2304, 2816, 3328
BA_W = SX0 - BA0
N_DEV = 8
VMEM_LIMIT = 56 * 1024 * 1024

ADAM_LR, ADAM_B1, ADAM_B2, ADAM_EPS, ADAM_WD, ADAM_STEP = 0.001, 0.9, 0.999, 1e-08, 0.01, 10


def _cparams(sem):
    return pltpu.CompilerParams(dimension_semantics=sem, vmem_limit_bytes=VMEM_LIMIT)


def _div_tile(n, target, mult=16):
    best = None
    for t in range(mult, min(n, target) + 1, mult):
        if n % t == 0:
            best = t
    assert best is not None, (n, target)
    return best


def _silu(x):
    return x * jax.nn.sigmoid(x)


def _dsilu(x):
    s = jax.nn.sigmoid(x)
    return s * (1.0 + x * (1.0 - s))


def _bdot(a, b, dims):
    return lax.dot_general(a.astype(BF16), b.astype(BF16), (dims, ((), ())), preferred_element_type=F32)


NN = ((1,), (0,))
NT = ((1,), (1,))
TN = ((0,), (0,))


def _mm(name, pairs, mode, tm, tn, tk, out_dtypes=(F32,), epilogue=None, extras=(), after=None):
    a0, b0 = pairs[0]
    if mode == "nn":
        (m, k), n = a0.shape, b0.shape[1]
    elif mode == "nt":
        (m, k), n = a0.shape, b0.shape[0]
    else:
        (k, m), n = a0.shape, b0.shape[1]
    assert m % tm == 0 and n % tn == 0 and k % tk == 0, (name, m, n, k, tm, tn, tk)
    nk = k // tk
    npair = len(pairs)
    nex = len(extras)
    nout = len(out_dtypes)
    ntok = 0 if after is None else 1
    dims = {"nn": NN, "nt": NT, "tn": TN}[mode]

    def body(*refs):
        ab = refs[:2 * npair]
        ex = refs[2 * npair:2 * npair + nex]
        outs = refs[2 * npair + nex + ntok:2 * npair + nex + ntok + nout]
        acc_ref = refs[-1]
        kk = pl.program_id(2)

        part = None
        for p in range(npair):
            d = lax.dot_general(ab[2 * p][...], ab[2 * p + 1][...], (dims, ((), ())), preferred_element_type=F32)
            part = d if part is None else part + d

        def finish(acc):
            res = epilogue(acc, *[e[...] for e in ex]) if epilogue is not None else (acc,)
            for o, r in zip(outs, res):
                o[...] = r.astype(o.dtype)

        if nk == 1:
            finish(part)
        else:
            @pl.when(kk == 0)
            def _():
                acc_ref[...] = part

            @pl.when(kk > 0)
            def _():
                acc_ref[...] += part

            @pl.when(kk == nk - 1)
            def _():
                finish(acc_ref[...])

    if mode == "nn":
        a_spec = pl.BlockSpec((tm, tk), lambda i, j, q: (i, q))
        b_spec = pl.BlockSpec((tk, tn), lambda i, j, q: (q, j))
    elif mode == "nt":
        a_spec = pl.BlockSpec((tm, tk), lambda i, j, q: (i, q))
        b_spec = pl.BlockSpec((tn, tk), lambda i, j, q: (j, q))
    else:
        a_spec = pl.BlockSpec((tk, tm), lambda i, j, q: (q, i))
        b_spec = pl.BlockSpec((tk, tn), lambda i, j, q: (q, j))
    o_spec = pl.BlockSpec((tm, tn), lambda i, j, q: (i, j))
    flat = [t for pr in pairs for t in pr]
    res = pl.pallas_call(
        body, name=name,
        grid=(m // tm, n // tn, nk),
        in_specs=[a_spec, b_spec] * npair + [o_spec] * nex + [pl.BlockSpec((8, DH), lambda i, j, q: (0, 0))] * ntok,
        out_specs=[o_spec] * nout,
        out_shape=[jax.ShapeDtypeStruct((m, n), dt) for dt in out_dtypes],
        scratch_shapes=[pltpu.VMEM((tm, tn) if nk > 1 else (8, DH), F32)],
        compiler_params=_cparams(("parallel", "parallel", "arbitrary")),
    )(*flat, *extras, *([] if after is None else [after]))
    return res


def _ffn_up(u2, wg, wu, tm, tn):
    m, k = u2.shape
    n = wg.shape[0]

    def body(a_ref, g_ref, u_ref, gate_ref, up_ref, act_ref):
        a = a_ref[...]
        g = lax.dot_general(a, g_ref[...], (NT, ((), ())), preferred_element_type=F32)
        u = lax.dot_general(a, u_ref[...], (NT, ((), ())), preferred_element_type=F32)
        gate_ref[...] = g.astype(BF16)
        up_ref[...] = u.astype(BF16)
        act_ref[...] = (_silu(g) * u).astype(BF16)

    o_spec = pl.BlockSpec((tm, tn), lambda j, i: (i, j))
    w_spec = pl.BlockSpec((tn, k), lambda j, i: (j, 0))
    return pl.pallas_call(
        body, name="ffn_up", grid=(n // tn, m // tm),
        in_specs=[pl.BlockSpec((tm, k), lambda j, i: (i, 0)), w_spec, w_spec],
        out_specs=[o_spec, o_spec, o_spec],
        out_shape=[jax.ShapeDtypeStruct((m, n), BF16)] * 3,
        compiler_params=_cparams(("parallel", "parallel")),
    )(u2, wg, wu)


def _rms(x, gain):
    r = lax.rsqrt(jnp.mean(x * x, axis=-1, keepdims=True) + EPS)
    return x * r * gain, r


def _rms_bwd(x, gain, dy):
    r = lax.rsqrt(jnp.mean(x * x, axis=-1, keepdims=True) + EPS)
    dyw = dy * gain
    dx = r * dyw - x * (r * r * r) * jnp.mean(dyw * x, axis=-1, keepdims=True)
    return dx, dy * x * r


def _row_specs(tr, width, nb):
    return pl.BlockSpec((tr, width), lambda b, j: (b * nb + j, 0))


def _vec_spec(width):
    return pl.BlockSpec((1, width), lambda b, j: (0, 0))


def _first_step(b, j):
    return jnp.logical_and(b == 0, j == 0)


def _pre_norm(x, head_rows, gain, bsz, lp, tr):
    nb = lp // tr
    first_real = head_rows.shape[0]
    assert first_real % 8 == 0 and first_real < tr
    x_spec = pl.BlockSpec((pl.Squeezed(), pl.Element(tr), pl.Element(D)),
                          lambda b, j: (b, pl.multiple_of(jnp.maximum(j * tr - first_real, 0), 8), 0))

    def body(x_ref, hd_ref, g_ref, h_ref, u_ref):
        j = pl.program_id(1)
        xt = x_ref[...]
        h = jnp.where(j == 0, jnp.concatenate([hd_ref[...], xt[:tr - first_real]], axis=0), xt)
        h_ref[...] = h
        u_ref[...] = _rms(h, g_ref[...])[0].astype(BF16)

    rs = _row_specs(tr, D, nb)
    return pl.pallas_call(
        body, name="mix_pre_norm", grid=(bsz, nb),
        in_specs=[x_spec, pl.BlockSpec((first_real, D), lambda b, j: (0, 0)), _vec_spec(D)],
        out_specs=[rs, rs],
        out_shape=[jax.ShapeDtypeStruct((bsz * lp, D), F32), jax.ShapeDtypeStruct((bsz * lp, D), BF16)],
        compiler_params=_cparams(("parallel", "parallel")),
    )(x, head_rows, gain)


def _mid_norms(h0, mix, g_post, g_pre, bsz, lp, tr):
    nb = lp // tr

    def body(h_ref, m_ref, gp_ref, gq_ref, h1_ref, u2_ref):
        h1 = h_ref[...] + _rms(m_ref[...].astype(F32), gp_ref[...])[0]
        h1_ref[...] = h1
        u2_ref[...] = _rms(h1, gq_ref[...])[0].astype(BF16)

    rs = _row_specs(tr, D, nb)
    return pl.pallas_call(
        body, name="mid_norms", grid=(bsz, nb),
        in_specs=[rs, rs, _vec_spec(D), _vec_spec(D)],
        out_specs=[rs, rs],
        out_shape=[jax.ShapeDtypeStruct(h0.shape, F32), jax.ShapeDtypeStruct(h0.shape, BF16)],
        compiler_params=_cparams(("parallel", "parallel")),
    )(h0, mix, g_post, g_pre)


def _loss_head(h1, ffn, tgt, g_post, bsz, lp, tr, first_real):
    nb = lp // tr
    assert first_real % 8 == 0 and first_real < tr
    tgt_spec = pl.BlockSpec((pl.Squeezed(), pl.Element(tr), pl.Element(D)),
                            lambda b, j: (b, pl.multiple_of(jnp.maximum(j * tr - first_real, 0), 8), 0))

    def body(h_ref, f_ref, t_ref, g_ref, dy_ref, dffn_ref, dg_ref, loss_ref):
        b, j = pl.program_id(0), pl.program_id(1)
        f = f_ref[...].astype(F32)
        g = g_ref[...]
        y = h_ref[...] + _rms(f, g)[0]
        t = j * tr + lax.broadcasted_iota(jnp.int32, (tr, 1), 0)
        tg = t_ref[...]
        tg = jnp.where(j == 0, jnp.concatenate([tg[tr - first_real:], tg[:tr - first_real]], axis=0), tg)
        err = jnp.where(t >= first_real, y - tg, 0.0)
        dy = err * (1.0 / D)
        dy_ref[...] = dy
        dx, dgc = _rms_bwd(f, g, dy)
        dffn_ref[...] = dx.astype(BF16)
        lsum = jnp.sum(jnp.sum(err * err, axis=0, keepdims=True), axis=1, keepdims=True) * (0.5 / D)

        @pl.when(_first_step(b, j))
        def _():
            dg_ref[...] = jnp.zeros_like(dg_ref)
            loss_ref[...] = jnp.zeros_like(loss_ref)

        dg_ref[...] += jnp.sum(dgc, axis=0, keepdims=True)
        loss_ref[...] += jnp.broadcast_to(lsum, loss_ref.shape)

    rs = _row_specs(tr, D, nb)
    return pl.pallas_call(
        body, name="loss_head", grid=(bsz, nb),
        in_specs=[rs, rs, tgt_spec, _vec_spec(D)],
        out_specs=[rs, rs, _vec_spec(D), _vec_spec(DH)],
        out_shape=[jax.ShapeDtypeStruct(h1.shape, F32), jax.ShapeDtypeStruct(h1.shape, BF16),
                   jax.ShapeDtypeStruct((1, D), F32), jax.ShapeDtypeStruct((1, DH), F32)],
        compiler_params=_cparams(("arbitrary", "arbitrary")),
    )(h1, ffn, tgt, g_post)


def _mid_norms_bwd(h1, mix, du2, dy, g_pre, g_post, bsz, lp, tr):
    nb = lp // tr

    def body(h_ref, m_ref, du_ref, dy_ref, gq_ref, gp_ref, dh1_ref, dmix_ref, dgq_ref, dgp_ref):
        b, j = pl.program_id(0), pl.program_id(1)
        dx, dgq = _rms_bwd(h_ref[...], gq_ref[...], du_ref[...].astype(F32))
        dh1 = dy_ref[...] + dx
        dh1_ref[...] = dh1
        dm, dgp = _rms_bwd(m_ref[...].astype(F32), gp_ref[...], dh1)
        dmix_ref[...] = dm.astype(BF16)

        @pl.when(_first_step(b, j))
        def _():
            dgq_ref[...] = jnp.zeros_like(dgq_ref)
            dgp_ref[...] = jnp.zeros_like(dgp_ref)

        dgq_ref[...] += jnp.sum(dgq, axis=0, keepdims=True)
        dgp_ref[...] += jnp.sum(dgp, axis=0, keepdims=True)

    rs = _row_specs(tr, D, nb)
    return pl.pallas_call(
        body, name="mid_norms_bwd", grid=(bsz, nb),
        in_specs=[rs, rs, rs, rs, _vec_spec(D), _vec_spec(D)],
        out_specs=[rs, rs, _vec_spec(D), _vec_spec(D)],
        out_shape=[jax.ShapeDtypeStruct(h1.shape, F32), jax.ShapeDtypeStruct(h1.shape, BF16),
                   jax.ShapeDtypeStruct((1, D), F32), jax.ShapeDtypeStruct((1, D), F32)],
        compiler_params=_cparams(("arbitrary", "arbitrary")),
    )(h1, mix, du2, dy, g_pre, g_post)


def _pre_norm_bwd(h0, du1, dh1, gain, bsz, lp, seq, padf):
    first_real = lp - seq
    tx = _div_tile(seq, 512)
    assert first_real % 8 == 0 and lp % first_real == 0

    def tokens(h_ref, du_ref, dh1_ref, g_ref, gx_ref, dg_ref):
        b, j = pl.program_id(0), pl.program_id(1)
        dx, dgc = _rms_bwd(h_ref[...], g_ref[...], du_ref[...].astype(F32))
        gx_ref[...] = dh1_ref[...] + dx

        @pl.when(_first_step(b, j))
        def _():
            dg_ref[...] = jnp.zeros_like(dg_ref)

        dg_ref[...] += jnp.sum(dgc, axis=0, keepdims=True)

    window = pl.BlockSpec((pl.Element(tx), pl.Element(D)),
                          lambda b, j: (pl.multiple_of(b * lp + first_real + j * tx, 16), 0))
    assert first_real % 16 == 0 and tx % 16 == 0 and lp % 16 == 0
    grad_x, dg_tokens = pl.pallas_call(
        tokens, name="pre_norm_bwd", grid=(bsz, seq // tx),
        in_specs=[window, window, window, _vec_spec(D)],
        out_specs=[pl.BlockSpec((pl.Squeezed(), tx, D), lambda b, j: (b, j, 0)), _vec_spec(D)],
        out_shape=[jax.ShapeDtypeStruct((bsz, seq, D), F32), jax.ShapeDtypeStruct((1, D), F32)],
        compiler_params=_cparams(("arbitrary", "arbitrary")),
    )(h0, du1, dh1, gain)

    def head(h_ref, du_ref, dh1_ref, g_ref, dgt_ref, dg_ref, dmeta_ref):
        b = pl.program_id(0)
        dx, dgc = _rms_bwd(h_ref[...], g_ref[...], du_ref[...].astype(F32))
        dh0 = dh1_ref[...] + dx

        @pl.when(b == 0)
        def _():
            dg_ref[...] = dgt_ref[...]
            dmeta_ref[...] = jnp.zeros_like(dmeta_ref)

        dg_ref[...] += jnp.sum(dgc, axis=0, keepdims=True)
        dmeta_ref[...] += dh0[padf:padf + N_META, :]

    rows = pl.BlockSpec((first_real, D), lambda b: (b * (lp // first_real), 0))
    vec = pl.BlockSpec((1, D), lambda b: (0, 0))
    dg, dmeta = pl.pallas_call(
        head, name="pre_norm_bwd_head", grid=(bsz,),
        in_specs=[rows, rows, rows, vec, vec],
        out_specs=[vec, pl.BlockSpec((N_META, D), lambda b: (0, 0))],
        out_shape=[jax.ShapeDtypeStruct((1, D), F32), jax.ShapeDtypeStruct((N_META, D), F32)],
        compiler_params=_cparams(("arbitrary",)),
    )(h0, du1, dh1, gain, dg_tokens)
    return grad_x, dg, dmeta


def _shift_down(x, s, lp):
    return x if s == 0 else pltpu.roll(x, s, axis=0)


def _shift_up(x, s, lp):
    return x if s == 0 else pltpu.roll(x, lp - s, axis=0)


def _conv_fwd(x, w, taps, lp):
    y = None
    for i in range(taps):
        term = _shift_down(x, taps - 1 - i, lp) * w[i:i + 1, :]
        y = term if y is None else y + term
    return y


def _conv_bwd(x, w, dy, taps, lp):
    dx = None
    dws = []
    for i in range(taps):
        s = taps - 1 - i
        term = _shift_up(dy, s, lp) * w[i:i + 1, :]
        dx = term if dx is None else dx + term
        dws.append(jnp.sum(dy * _shift_down(x, s, lp), axis=0, keepdims=True))
    return dx, jnp.concatenate(dws, axis=0)


def _slab(lp, col0):
    return pl.BlockSpec((lp, DH), lambda b, s: (b, col0 // DH + s))


def _qkv_act(c, s):
    y = _silu(c)
    r = lax.rsqrt(jnp.sum(y * y, axis=-1, keepdims=True) + EPS)
    scale = jnp.where(s < HEADS, DH ** -0.5, 1.0)
    return y, r, scale


def _gdn_prep(proj, conv_qkv, bsz, lp, padf):
    def body(p_ref, w_ref, o_ref):
        s = pl.program_id(1)
        c = _conv_fwd(p_ref[...].astype(F32), w_ref[...], KQ, lp)
        y, r, scale = _qkv_act(c, s)
        out = jnp.where(s < 2 * HEADS, y * (r * scale), y)
        t = lax.broadcasted_iota(jnp.int32, (lp, 1), 0)
        o_ref[...] = jnp.where(t >= padf, out, 0.0).astype(BF16)

    return pl.pallas_call(
        body, name="gdn_prep", grid=(bsz, 3 * HEADS),
        in_specs=[_slab(lp, Q0), pl.BlockSpec((KQ, DH), lambda b, s: (0, s))],
        out_specs=_slab(lp, 0),
        out_shape=jax.ShapeDtypeStruct((bsz * lp, 3 * GW), BF16),
        compiler_params=_cparams(("parallel", "parallel")),
    )(proj, conv_qkv)


def _gdn_prep_bwd(proj, conv_qkv, dq, dk, dv, bsz, lp, padf):
    def body(p_ref, w_ref, dq_ref, dk_ref, dv_ref, dp_ref, dw_ref):
        b, s = pl.program_id(0), pl.program_id(1)
        x = p_ref[...].astype(F32)
        w = w_ref[...]
        c = _conv_fwd(x, w, KQ, lp)
        y, r, scale = _qkv_act(c, s)
        d = jnp.where(s < HEADS, dq_ref[...], jnp.where(s < 2 * HEADS, dk_ref[...], dv_ref[...])).astype(F32)
        dn = d * scale
        n = y * r
        dy_norm = r * (dn - n * jnp.sum(dn * n, axis=-1, keepdims=True))
        dyy = jnp.where(s < 2 * HEADS, dy_norm, dn)
        t = lax.broadcasted_iota(jnp.int32, (lp, 1), 0)
        dc = jnp.where(t >= padf, dyy * _dsilu(c), 0.0)
        dx, dw = _conv_bwd(x, w, dc, KQ, lp)
        dp_ref[...] = jnp.where(t >= padf, dx, 0.0).astype(BF16)

        @pl.when(jnp.logical_and(b == 0, s == 0))
        def _():
            dw_ref[...] = jnp.zeros_like(dw_ref)

        col_block = jnp.right_shift(lax.broadcasted_iota(jnp.int32, (KQ, 3 * GW), 1), DH.bit_length() - 1)
        dw_ref[...] += jnp.where(col_block == s, jnp.tile(dw, (1, 3 * HEADS)), 0.0)

    sl = lambda col0: pl.BlockSpec((lp, DH), lambda b, s: (b, col0 // DH + s))
    part = lambda k: pl.BlockSpec((lp, DH), lambda b, s: (b, jnp.clip(s - k * HEADS, 0, HEADS - 1)))
    return pl.pallas_call(
        body, name="gdn_prep_bwd", grid=(bsz, 3 * HEADS),
        in_specs=[sl(Q0), pl.BlockSpec((KQ, DH), lambda b, s: (0, s)), part(0), part(1), part(2)],
        out_specs=[sl(0), pl.BlockSpec((KQ, 3 * GW), lambda b, s: (0, 0))],
        out_shape=[jax.ShapeDtypeStruct((bsz * lp, 3 * GW), BF16), jax.ShapeDtypeStruct((KQ, 3 * GW), F32)],
        compiler_params=_cparams(("arbitrary", "arbitrary")),
    )(proj, conv_qkv, dq, dk, dv)


def _softplus(x):
    e = jnp.exp(-jnp.abs(x))
    one_e = 1.0 + e
    l1p = jnp.where(one_e == 1.0, e, jnp.log(one_e) * (e / (one_e - 1.0)))
    return jnp.maximum(x, 0.0) + l1p


def _tri(n, kind):
    i = lax.broadcasted_iota(jnp.int32, (n, n), 0)
    j = lax.broadcasted_iota(jnp.int32, (n, n), 1)
    return {"incl": i >= j, "strict": i > j, "upper": i <= j}[kind]


def _gates(proj, alog_v, dtb_v, bsz, lp, padf):
    nchunk = lp // CH

    def body(p_ref, al_ref, dt_ref, gb_ref, cs_ref):
        x = p_ref[...].astype(F32)
        lane = lax.broadcasted_iota(jnp.int32, (lp, DH), 1)
        t = lax.broadcasted_iota(jnp.int32, (lp, DH), 0)
        beta = jax.nn.sigmoid(x)
        g = -jnp.exp(al_ref[...]) * _softplus(x + dt_ref[...])
        gb = jnp.where(lane < HEADS, beta, jnp.where(lane < 2 * HEADS, g, 0.0))
        gb_ref[...] = jnp.where(t >= padf, gb, 0.0)
        ltri = _tri(CH, "incl").astype(BF16)
        rest = [gb_ref[pl.ds(n * CH, CH), :] for n in range(nchunk)]
        sums = None
        for _ in range(3):
            piece = [v.astype(BF16) for v in rest]
            rest = [v - p.astype(F32) for v, p in zip(rest, piece)]
            part = [jnp.dot(ltri, p, preferred_element_type=F32) for p in piece]
            sums = part if sums is None else [a + b for a, b in zip(sums, part)]
        for n in range(nchunk):
            cs_ref[pl.ds(n * CH, CH), :] = sums[n]

    blk = pl.BlockSpec((lp, DH), lambda b: (b, 0))
    vec = pl.BlockSpec((1, DH), lambda b: (0, 0))
    out = pl.BlockSpec((lp, DH), lambda b: (b, 0))
    return pl.pallas_call(
        body, name="gates", grid=(bsz,),
        in_specs=[blk, vec, vec], out_specs=[out, out],
        out_shape=[jax.ShapeDtypeStruct((bsz * lp, DH), F32)] * 2,
        compiler_params=_cparams(("parallel",)),
    )(proj, alog_v, dtb_v)


def _gates_bwd(proj, alog_v, dtb_v, dbeta_b, dg_b, gb, bsz, lp, padf):
    def body(p_ref, al_ref, dt_ref, db_ref, dg_ref, gb_ref, dl_ref, dv_ref):
        b = pl.program_id(0)
        x = p_ref[...].astype(F32)
        lane = lax.broadcasted_iota(jnp.int32, (lp, DH), 1)
        t = lax.broadcasted_iota(jnp.int32, (lp, DH), 0)
        dbeta = jnp.zeros((lp, DH), F32)
        dg = jnp.zeros((lp, DH), F32)
        for h in range(HEADS):
            dbeta = jnp.where(lane == h, db_ref[h], dbeta)
            dg = jnp.where(lane == HEADS + h, dg_ref[h], dg)
        real = t >= padf
        dbeta = jnp.where(real, dbeta, 0.0)
        dg = jnp.where(real, dg, 0.0)
        sb = jax.nn.sigmoid(x)
        neg_ea = -jnp.exp(al_ref[...])
        dsp = dg * neg_ea * jax.nn.sigmoid(x + dt_ref[...])
        dl = dbeta * sb * (1.0 - sb) + dsp
        dl_ref[...] = jnp.concatenate([dl, jnp.zeros((lp, BA_W - DH), F32)], axis=1).astype(BF16)
        dalog = jnp.sum(dg * gb_ref[...], axis=0, keepdims=True)
        ddt = jnp.sum(dsp, axis=0, keepdims=True)

        @pl.when(b == 0)
        def _():
            dv_ref[...] = jnp.zeros_like(dv_ref)

        dv_ref[...] += jnp.concatenate([dalog, ddt], axis=0)

    blk = pl.BlockSpec((lp, DH), lambda b: (b, 0))
    vec = pl.BlockSpec((1, DH), lambda b: (0, 0))
    hb = pl.BlockSpec((HEADS, lp, DH), lambda b: (b, 0, 0))
    return pl.pallas_call(
        body, name="gates_bwd", grid=(bsz,),
        in_specs=[blk, vec, vec, hb, hb, pl.BlockSpec((lp, DH), lambda b: (b, 0))],
        out_specs=[pl.BlockSpec((lp, BA_W), lambda b: (b, 0)), pl.BlockSpec((2, DH), lambda b: (0, 0))],
        out_shape=[jax.ShapeDtypeStruct((bsz * lp, BA_W), BF16), jax.ShapeDtypeStruct((2, DH), F32)],
        compiler_params=_cparams(("arbitrary",)),
    )(proj, alog_v, dtb_v, dbeta_b, dg_b, gb)


def _mix_heads(proj, o, gdn_norm, conv_sc, bsz, lp, padf):
    def body(o_ref, z_ref, gn_ref, sx_ref, sb_ref, sc_ref, w_ref, og_ref, os_ref):
        og_ref[...] = (_rms(o_ref[...].astype(F32), gn_ref[...])[0] * _silu(z_ref[...].astype(F32))).astype(BF16)
        conv = _conv_fwd(sc_ref[...].astype(F32) * sx_ref[...].astype(F32), w_ref[...], KS, lp)
        t = lax.broadcasted_iota(jnp.int32, (lp, 1), 0)
        os_ref[...] = jnp.where(t >= padf, sb_ref[...].astype(F32) * conv, 0.0).astype(BF16)

    half = _slab(lp, 0)
    return pl.pallas_call(
        body, name="mix_heads", grid=(bsz, HEADS),
        in_specs=[half, _slab(lp, Z0), pl.BlockSpec((1, DH), lambda b, s: (0, 0)),
                  _slab(lp, SX0), _slab(lp, SB0), _slab(lp, SC0), pl.BlockSpec((KS, DH), lambda b, s: (0, s))],
        out_specs=[half, half],
        out_shape=[jax.ShapeDtypeStruct((bsz * lp, GW), BF16)] * 2,
        compiler_params=_cparams(("parallel", "parallel")),
    )(o, proj, gdn_norm, proj, proj, proj, conv_sc)


def _mix_heads_bwd(proj, o, gdn_norm, conv_sc, dmixin, bsz, lp, padf):
    def body(o_ref, z_ref, gn_ref, sx_ref, sb_ref, sc_ref, w_ref, dg_ref, ds_ref,
             do_ref, dz_ref, dsx_ref, dsb_ref, dsc_ref, dgn_ref, dw_ref):
        s, b = pl.program_id(0), pl.program_id(1)
        t = lax.broadcasted_iota(jnp.int32, (lp, 1), 0)
        real = t >= padf
        o, z, gn, d = o_ref[...].astype(F32), z_ref[...].astype(F32), gn_ref[...], dg_ref[...].astype(F32)
        sz = _silu(z)
        on = _rms(o, gn)[0]
        dz_ref[...] = (d * on * _dsilu(z)).astype(BF16)
        dox, dgn = _rms_bwd(o, gn, d * sz)
        do_ref[...] = dox.astype(BF16)
        sx, sb, sc = sx_ref[...].astype(F32), sb_ref[...].astype(F32), sc_ref[...].astype(F32)
        w, e = w_ref[...], ds_ref[...].astype(F32)
        e = jnp.where(real, e, 0.0)
        xin = sc * sx
        conv = _conv_fwd(xin, w, KS, lp)
        dsb_ref[...] = (e * conv).astype(BF16)
        dxin, dw = _conv_bwd(xin, w, e * sb, KS, lp)
        dxin = jnp.where(real, dxin, 0.0)
        dsx_ref[...] = (dxin * sc).astype(BF16)
        dsc_ref[...] = (dxin * sx).astype(BF16)

        @pl.when(jnp.logical_and(s == 0, b == 0))
        def _():
            dgn_ref[...] = jnp.zeros_like(dgn_ref)

        @pl.when(b == 0)
        def _():
            dw_ref[...] = jnp.zeros_like(dw_ref)

        dgn_ref[...] += jnp.sum(dgn, axis=0, keepdims=True)
        dw_ref[...] += dw

    sl = lambda col0: pl.BlockSpec((lp, DH), lambda s, b: (b, col0 // DH + s))
    half = sl(0)
    return pl.pallas_call(
        body, name="mix_heads_bwd", grid=(HEADS, bsz),
        in_specs=[half, sl(Z0), pl.BlockSpec((1, DH), lambda s, b: (0, 0)), sl(SX0), sl(SB0), sl(SC0),
                  pl.BlockSpec((KS, DH), lambda s, b: (0, s)), sl(0), sl(GW)],
        out_specs=[half] * 5 + [pl.BlockSpec((1, DH), lambda s, b: (0, 0)), pl.BlockSpec((KS, DH), lambda s, b: (0, s))],
        out_shape=[jax.ShapeDtypeStruct((bsz * lp, GW), BF16)] * 5
        + [jax.ShapeDtypeStruct((1, DH), F32), jax.ShapeDtypeStruct((KS, SW), F32)],
        compiler_params=_cparams(("arbitrary", "arbitrary")),
    )(o, proj, gdn_norm, proj, proj, proj, conv_sc, dmixin, dmixin)


def _split_dot(a, b, dims=NN):
    ah = a.astype(BF16)
    al = (a - ah.astype(F32)).astype(BF16)
    bh = b.astype(BF16)
    bl = (b - bh.astype(F32)).astype(BF16)

    def d(x, y):
        return lax.dot_general(x, y, (dims, ((), ())), preferred_element_type=F32)

    return d(ah, bh) + (d(ah, bl) + d(al, bh))


def _eye(n):
    return (lax.broadcasted_iota(jnp.int32, (n, n), 0) == lax.broadcasted_iota(jnp.int32, (n, n), 1)).astype(F32)


def _each(f, *lists):
    return [f(*xs) for xs in zip(*lists)]


def _inv_unit_lower(a_list):
    eye = _eye(CH)
    p = _each(lambda a: -a, a_list)
    t = _each(lambda x: eye + x, p)
    p = _each(lambda x: _split_dot(x, x), p)
    for level in range(5):
        if level < 4:
            prod = _each(lambda tt, pp: _split_dot(jnp.concatenate([tt, pp], axis=0), pp), t, p)
            t = _each(lambda tt, pr: tt + pr[:CH], t, prod)
            p = _each(lambda pr: pr[CH:], prod)
        else:
            t = _each(lambda tt, pp: tt + _split_dot(tt, pp), t, p)
    return t


def _chunk_gates(gc_ref, be_ref, gt_ref, g):
    rows = pl.ds(g * CH, CH)
    gc = gc_ref[0, rows, :]
    be = be_ref[0, rows, :]
    gcl = gc_ref[0, pl.ds(g * CH + CH - 1, 1), :]
    gt = gt_ref[0, g]
    incl = _tri(CH, "incl")
    dec = jnp.where(incl, jnp.exp(jnp.where(incl, gc[:, :CH] - gt, 0.0)), 0.0)
    return rows, be, dec, jnp.exp(gc), jnp.exp(gcl - gc), jnp.exp(gcl)


def _chunk_group(nchunk):
    return 3 if nchunk % 3 == 0 else 1


def _local_group(nchunk):
    return 11 if nchunk % 11 == 0 else _chunk_group(nchunk)


def _gdn_local_specs(lp, grp):
    ngrp = lp // (grp * CH)
    head = lambda col0: pl.BlockSpec((grp * CH, DH), lambda i, j: ((i // HEADS) * ngrp + j, col0 // DH + i % HEADS))
    lane_b = pl.BlockSpec((1, grp * CH, DH), lambda i, j: (i, j, 0))
    per_chunk = lambda r, c: pl.BlockSpec((1, grp, r, c), lambda i, j: (i, j, 0, 0))
    return ngrp, head, lane_b, per_chunk


def _gdn_local_fwd(qkv, gcb, bb, gct, bsz, lp):
    nchunk = lp // CH
    grp = _local_group(nchunk)
    ngrp, head, lane_b, per_chunk = _gdn_local_specs(lp, grp)
    bh = bsz * HEADS

    def body(q_ref, k_ref, v_ref, gc_ref, be_ref, gt_ref,
             u_ref, l1_ref, l2_ref, l3_ref, wt_ref, kd_ref, ti_ref, gl_ref):
        incl, strict = _tri(CH, "incl"), _tri(CH, "strict")
        rows, be, dec, eg, ekd, gl = zip(*[_chunk_gates(gc_ref, be_ref, gt_ref, g) for g in range(grp)])
        q = [q_ref[r, :].astype(F32) for r in rows]
        k = [k_ref[r, :].astype(F32) for r in rows]
        v = [v_ref[r, :].astype(F32) for r in rows]
        kb = _each(lambda x, y: x * y, k, be)
        a = _each(lambda x, y, d: jnp.where(strict, _bdot(x, y, NT) * d, 0.0), kb, k, dec)
        tinv = _inv_unit_lower(a)
        u = _each(lambda t, x, y: _bdot(t, x * y, NN), tinv, v, be)
        w = _each(lambda t, x, y: _bdot(t, x * y, NN), tinv, kb, eg)
        qk = _each(lambda x, y, d: jnp.where(incl, _bdot(x, y, NT) * d, 0.0), q, k, dec)
        qk_t = _each(lambda x: x.T, qk)
        qd = _each(lambda x, y: x * y, q, eg)
        kd = _each(lambda x, y: x * y, k, ekd)
        kd_t = _each(lambda x: x.T, kd)
        qd_t = _each(lambda x: x.T, qd)
        w_t = _each(lambda x: x.T, w)
        for g in range(grp):
            u_ref[rows[g], :] = u[g]
            l1_ref[0, g] = jnp.concatenate([w[g], qd[g]], axis=0).astype(BF16)
            l2_ref[0, g] = jnp.concatenate([qk[g], kd_t[g]], axis=0).astype(BF16)
            l3_ref[0, g] = jnp.concatenate([qk_t[g], qd_t[g]], axis=0).astype(BF16)
            wt_ref[0, g] = w_t[g].astype(BF16)
            kd_ref[0, g] = kd[g].astype(BF16)
            ti_ref[0, g] = tinv[g]
            gl_ref[0, g] = gl[g]

    r = bsz * lp
    shapes = [((r, GW), F32), ((bh, nchunk, 2 * CH, DH), BF16), ((bh, nchunk, 3 * CH, CH), BF16),
              ((bh, nchunk, 3 * CH, CH), BF16), ((bh, nchunk, DH, CH), BF16), ((bh, nchunk, CH, DH), BF16),
              ((bh, nchunk, CH, CH), F32), ((bh, nchunk, 1, DH), F32)]
    return pl.pallas_call(
        body, name="gdn_local_fwd", grid=(bh, ngrp),
        in_specs=[head(0), head(GW), head(2 * GW), lane_b, lane_b, per_chunk(1, CH)],
        out_specs=[head(0), per_chunk(2 * CH, DH), per_chunk(3 * CH, CH), per_chunk(3 * CH, CH), per_chunk(DH, CH),
                   per_chunk(CH, DH), per_chunk(CH, CH), per_chunk(1, DH)],
        out_shape=[jax.ShapeDtypeStruct(s, dt) for s, dt in shapes],
        compiler_params=_cparams(("parallel", "parallel")),
    )(qkv, qkv, qkv, gcb, bb, gct)


def _scan_specs(bsz, lp, grp, order):
    ngrp = lp // (grp * CH)
    bh = bsz * HEADS
    seq = lambda: pl.BlockSpec((bsz, grp * CH, GW), lambda s: (0, order(s, ngrp), 0))
    per_chunk = lambda r, c: pl.BlockSpec((bh, grp, r, c), lambda s: (0, order(s, ngrp), 0, 0))
    return ngrp, bh, seq, per_chunk


def _gdn_scan_fwd(u, l1, l2, gl, bsz, lp):
    nchunk = lp // CH
    grp = _chunk_group(nchunk)
    ngrp, bh, seq, per_chunk = _scan_specs(bsz, lp, grp, lambda s, n: s)

    def body(u_ref, l1_ref, l2_ref, gl_ref, o_ref, vn_ref, st_ref, s_scr):
        @pl.when(pl.program_id(0) == 0)
        def _():
            s_scr[...] = jnp.zeros_like(s_scr)

        chains = list(range(bh))
        where = [(i // HEADS, slice((i % HEADS) * DH, (i % HEADS + 1) * DH)) for i in chains]

        def dot(a, b):
            return jnp.dot(a, b.astype(BF16), preferred_element_type=F32)

        for c in range(grp):
            rows = slice(c * CH, (c + 1) * CH)
            s = [s_scr[i] for i in chains]
            for i in chains:
                st_ref[i, c] = s[i]
            x = _each(lambda i, si: dot(l1_ref[i, c], si), chains, s)
            vnew = _each(lambda bc, xi: u_ref[bc[0], rows, bc[1]] - xi[:CH], where, x)
            y = _each(lambda i, vi: dot(l2_ref[i, c], vi), chains, vnew)
            for i, (b, cols) in zip(chains, where):
                o_ref[b, rows, cols] = (x[i][CH:] + y[i][:CH]).astype(BF16)
                vn_ref[b, rows, cols] = vnew[i].astype(BF16)
                s_scr[i] = s[i] * gl_ref[i, c] + y[i][CH:]

    return pl.pallas_call(
        body, name="gdn_scan_fwd", grid=(ngrp,),
        in_specs=[seq(), per_chunk(2 * CH, DH), per_chunk(3 * CH, CH), per_chunk(1, DH)],
        out_specs=[seq(), seq(), per_chunk(DH, DH)],
        out_shape=[jax.ShapeDtypeStruct((bsz, lp, GW), BF16), jax.ShapeDtypeStruct((bsz, lp, GW), BF16),
                   jax.ShapeDtypeStruct((bh, nchunk, DH, DH), F32)],
        scratch_shapes=[pltpu.VMEM((bh, DH, DH), F32)],
        compiler_params=_cparams(("arbitrary",)),
    )(u.reshape(bsz, lp, GW), l1, l2, gl)


def _gdn_scan_bwd(d_o, l3, wt, kd, gl, bsz, lp):
    nchunk = lp // CH
    grp = _chunk_group(nchunk)
    ngrp, bh, seq, per_chunk = _scan_specs(bsz, lp, grp, lambda s, n: n - 1 - s)

    def body(do_ref, l3_ref, wt_ref, kd_ref, gl_ref, dvn_ref, dst_ref, ds_scr):
        @pl.when(pl.program_id(0) == 0)
        def _():
            ds_scr[...] = jnp.zeros_like(ds_scr)

        chains = list(range(bh))
        where = [(i // HEADS, slice((i % HEADS) * DH, (i % HEADS + 1) * DH)) for i in chains]

        def dot(a, b):
            return jnp.dot(a, b.astype(BF16), preferred_element_type=F32)

        for c in reversed(range(grp)):
            rows = slice(c * CH, (c + 1) * CH)
            ds = [ds_scr[i] for i in chains]
            for i in chains:
                dst_ref[i, c] = ds[i]
            p = _each(lambda i, bc: dot(l3_ref[i, c], do_ref[bc[0], rows, bc[1]]), chains, where)
            dvn = _each(lambda i, pi, di: pi[:CH] + dot(kd_ref[i, c], di), chains, p, ds)
            wd = _each(lambda i, vi: dot(wt_ref[i, c], vi), chains, dvn)
            for i, (b, cols) in zip(chains, where):
                dvn_ref[b, rows, cols] = dvn[i].astype(BF16)
                ds_scr[i] = ds[i] * gl_ref[i, c] + p[i][CH:] - wd[i]

    return pl.pallas_call(
        body, name="gdn_scan_bwd", grid=(ngrp,),
        in_specs=[seq(), per_chunk(3 * CH, CH), per_chunk(DH, CH), per_chunk(CH, DH), per_chunk(1, DH)],
        out_specs=[seq(), per_chunk(DH, DH)],
        out_shape=[jax.ShapeDtypeStruct((bsz, lp, GW), BF16), jax.ShapeDtypeStruct((bh, nchunk, DH, DH), F32)],
        scratch_shapes=[pltpu.VMEM((bh, DH, DH), F32)],
        compiler_params=_cparams(("arbitrary",)),
    )(d_o.reshape(bsz, lp, GW), l3, wt, kd, gl)


def _gdn_local_bwd(qkv, gcb, bb, gct, tinv_all, states, dstates, vnew, dvnew, d_o, bsz, lp):
    nchunk = lp // CH
    grp = _local_group(nchunk)
    ngrp, head, lane_b, per_chunk = _gdn_local_specs(lp, grp)
    bh = bsz * HEADS

    def body(q_ref, k_ref, v_ref, gc_ref, be_ref, gt_ref, ti_ref, st_ref, dst_ref, vn_ref, dvn_ref, do_ref,
             dq_ref, dk_ref, dv_ref, dg_ref, db_ref):
        incl, strict = _tri(CH, "incl"), _tri(CH, "strict")
        upper = _tri(CH, "upper").astype(F32)
        ones = jnp.ones((CH, DH), F32)
        last = lax.broadcasted_iota(jnp.int32, (CH, 1), 0) == CH - 1

        def rsum(x):
            return jnp.sum(x, axis=-1, keepdims=True)

        def mul(xs, ys):
            return _each(lambda x, y: x * y, xs, ys)

        rows, be, dec, eg, ekd, gl = zip(*[_chunk_gates(gc_ref, be_ref, gt_ref, g) for g in range(grp)])
        q = [q_ref[r, :].astype(F32) for r in rows]
        k = [k_ref[r, :].astype(F32) for r in rows]
        v = [v_ref[r, :].astype(F32) for r in rows]
        vnew = [vn_ref[r, :] for r in rows]
        dvn = [dvn_ref[r, :] for r in rows]
        do = [do_ref[r, :] for r in rows]
        tinv = [ti_ref[0, g] for g in range(grp)]
        s = [st_ref[0, g] for g in range(grp)]
        ds = [dst_ref[0, g] for g in range(grp)]
        kb, vb, qd, kd = mul(k, be), mul(v, be), mul(q, eg), mul(k, ekd)
        kbg = mul(kb, eg)
        a = _each(lambda x, y, d: jnp.where(strict, _bdot(x, y, NT) * d, 0.0), kb, k, dec)
        qk = _each(lambda x, y, d: jnp.where(incl, _bdot(x, y, NT) * d, 0.0), q, k, dec)
        dqk = _each(lambda x, y: jnp.where(incl, _bdot(x, y, NT), 0.0), do, vnew)
        dqd = _each(lambda x, y: _bdot(x, y, NT), do, s)
        dkd = _each(lambda x, y: _bdot(x, y, NT), vnew, ds)
        dgl = _each(lambda x, y: jnp.sum(rsum(x * y), axis=0, keepdims=True), s, ds)
        dw = _each(lambda x, y: -_bdot(x, y, NT), dvn, s)
        dvb = _each(lambda t, x: _bdot(t, x, TN), tinv, dvn)
        dkbg = _each(lambda t, x: _bdot(t, x, TN), tinv, dw)
        dt = _each(lambda x, y, z, w: _bdot(x, y, NT) + _bdot(z, w, NT), dvn, vb, dw, kbg)
        tdt = _each(lambda t, x: _split_dot(t, x, TN), tinv, dt)
        da = _each(lambda x, t: jnp.where(strict, -_split_dot(x, t, NT), 0.0), tdt, tinv)
        dp, dr = mul(da, dec), mul(dqk, dec)
        dkb = _each(lambda x, y, z, e: _bdot(x, y, NN) + z * e, dp, k, dkbg, eg)
        dk = _each(lambda p_, kb_, r_, q_, dkd_, ekd_, dkb_, be_:
                   _bdot(p_, kb_, TN) + _bdot(r_, q_, TN) + dkd_ * ekd_ + dkb_ * be_,
                   dp, kb, dr, q, dkd, ekd, dkb, be)
        dq = _each(lambda r_, k_, dqd_, eg_: _bdot(r_, k_, NN) + dqd_ * eg_, dr, k, dqd, eg)
        m = _each(lambda da_, a_, dqk_, qk_: da_ * a_ + dqk_ * qk_, da, a, dqk, qk)
        m_cols = _each(lambda x: _split_dot(x, ones, TN), m)
        kd_term = _each(lambda x, y: rsum(x * y), dkd, kd)
        dgc = _each(lambda m_, mc, dqd_, qd_, kt, dkbg_, kbg_, dgl_, gl_:
                    rsum(m_) - mc + rsum(dqd_ * qd_) - kt + rsum(dkbg_ * kbg_)
                    + jnp.where(last, jnp.sum(kt, axis=0, keepdims=True) + dgl_ * gl_, 0.0),
                    m, m_cols, dqd, qd, kd_term, dkbg, kbg, dgl, gl)
        dg = _each(lambda x: _split_dot(upper, x, NN), dgc)
        for g in range(grp):
            dq_ref[rows[g], :] = dq[g].astype(BF16)
            dk_ref[rows[g], :] = dk[g].astype(BF16)
            dv_ref[rows[g], :] = (dvb[g] * be[g]).astype(BF16)
            dg_ref[0, rows[g], :] = dg[g]
            db_ref[0, rows[g], :] = jnp.broadcast_to(rsum(dkb[g] * k[g]) + rsum(dvb[g] * v[g]), (CH, DH))

    r = bsz * lp
    return pl.pallas_call(
        body, name="gdn_local_bwd", grid=(bh, ngrp),
        in_specs=[head(0), head(GW), head(2 * GW), lane_b, lane_b, per_chunk(1, CH), per_chunk(CH, CH),
                  per_chunk(DH, DH), per_chunk(DH, DH), head(0), head(0), head(0)],
        out_specs=[head(0), head(0), head(0), lane_b, lane_b],
        out_shape=[jax.ShapeDtypeStruct((r, GW), BF16)] * 3 + [jax.ShapeDtypeStruct((bh, lp, DH), F32)] * 2,
        compiler_params=_cparams(("parallel", "parallel")),
    )(qkv, qkv, qkv, gcb, bb, gct, tinv_all, states, dstates, vnew, dvnew, d_o)


def _place():
    return lax.axis_index("x"), lax.axis_index("y"), lax.axis_index("c")


def _any_specs(n):
    return [pl.BlockSpec(memory_space=pl.ANY)] * n


def _all_gather(name, shards):
    n = len(shards)

    def body(*refs):
        ins, outs = refs[:n], refs[n:2 * n]
        send_sems, recv_sems, local_sems = refs[2 * n:2 * n + 3]
        stage = refs[2 * n + 3:]
        x, y, c = _place()
        me, sibling = (x, y, c), (x, y, 1 - c)
        chips = [(1 - x, y), (x, 1 - y), (1 - x, 1 - y)]

        def slot(i, p):
            return outs[i].at[4 * p[0] + 2 * p[1] + p[2]]

        def copy(i, k, block, to, src=None):
            return pltpu.make_async_remote_copy(
                src_ref=slot(i, block) if src is None else src, dst_ref=slot(i, block),
                send_sem=send_sems.at[i, k], recv_sem=recv_sems.at[i, k], device_id=to, device_id_type=MESH)

        load = [pltpu.make_async_copy(ins[i], stage[i], local_sems.at[i]) for i in range(n)]
        for cp in load:
            cp.start()
        first = []
        for i in range(n):
            first.append(copy(i, 0, me, sibling, src=ins[i]))
            first += [copy(i, 1 + j, me, (*chip, c), src=ins[i]) for j, chip in enumerate(chips)]
        for cp in first:
            cp.start()
        for cp in load:
            cp.wait()
        mine = [pltpu.make_async_copy(stage[i], slot(i, me), local_sems.at[i]) for i in range(n)]
        for cp in mine:
            cp.start()
        passed = []
        for j, chip in enumerate(chips):
            for i in range(n):
                copy(i, 1 + j, (*chip, c), me).wait_recv()
                fwd = copy(i, 4 + j, (*chip, c), sibling)
                fwd.start()
                passed.append(fwd)
        for i in range(n):
            copy(i, 0, sibling, me).wait_recv()
            for j, chip in enumerate(chips):
                copy(i, 4 + j, (*chip, 1 - c), me).wait_recv()
        for cp in first + passed:
            cp.wait_send()
        for cp in mine:
            cp.wait()

    return pl.pallas_call(
        body, name=name,
        in_specs=_any_specs(n), out_specs=_any_specs(n),
        out_shape=[jax.ShapeDtypeStruct((N_DEV,) + s.shape, s.dtype) for s in shards],
        scratch_shapes=[pltpu.SemaphoreType.DMA((n, 7)), pltpu.SemaphoreType.DMA((n, 7)),
                        pltpu.SemaphoreType.DMA((n,))] + [pltpu.VMEM(s.shape, s.dtype) for s in shards],
    )(*shards)


HBM_SPEC = pl.BlockSpec(memory_space=pltpu.HBM)
SEM_SPEC = pl.BlockSpec(memory_space=pltpu.SEMAPHORE)
EFFECT = pltpu.SideEffectType.DATAFLOW_SIDE_EFFECTING
N_PEER = N_DEV - 1


def _peer(r):
    x, y, c = _place()
    return ((1 - x) if r & 4 else x, (1 - y) if r & 2 else y, (1 - c) if r & 1 else c)


def _slot_of(p):
    return 4 * p[0] + 2 * p[1] + p[2]


def _hbm(a):
    return pltpu.with_memory_space_constraint(a, pltpu.HBM)


def _gather_pattern(src, land):
    me = _place()
    return [(src, land.at[_slot_of(me)], _peer(r), land.at[_slot_of(_peer(r))]) for r in range(1, N_DEV)]


def _alltoall_pattern(src, land):
    me = _place()
    return [(src.at[_slot_of(_peer(r))], land.at[_slot_of(me)], _peer(r), land.at[_slot_of(_peer(r))])
            for r in range(1, N_DEV)]


def _chipcast_pattern(src, land):
    x, y, c = _place()
    me = (x, y, c)
    peers = [(x, y, 1 - c), (1 - x, y, c), (x, 1 - y, c), (1 - x, 1 - y, c)]
    return [(src, land.at[_slot_of(me)], p, land.at[_slot_of(p)]) for p in peers]


def _forward_pattern(land, _):
    x, y, c = _place()
    chips = [(1 - x, y), (x, 1 - y), (1 - x, 1 - y)]
    return [(land.at[_slot_of((*ch, c))], land.at[_slot_of((*ch, c))], (x, y, 1 - c), land.at[_slot_of((*ch, 1 - c))])
            for ch in chips]


def _sibling_pattern(src, land):
    x, y, c = _place()
    return [(src.at[ch, 1 - c], land.at[ch], (x, y, 1 - c), land.at[ch]) for ch in range(4)]


def _chip_pattern(src, land):
    x, y, c = _place()
    chips = [(1 - x, y), (x, 1 - y), (1 - x, 1 - y)]
    return [(src.at[2 * ch[0] + ch[1]], land.at[j], (*ch, c), land.at[j]) for j, ch in enumerate(chips)]


def _split_start(name, groups, pattern, land_shape, after=None):
    sizes = [len(g) for g in groups]
    arrays = [s for g in groups for s in g]
    n = len(arrays)
    inplace = land_shape is None
    fresh = [] if inplace else [lax.empty(land_shape(s.shape), s.dtype) for s in arrays]
    bufs = arrays + fresh
    order = [] if after is None else [after]

    def body(*refs):
        srcs, lands = refs[:n], (refs[:n] if inplace else refs[n:2 * n])
        first_out = len(bufs) + len(order)
        sems = refs[first_out:first_out + 2 * len(groups)]
        token = refs[-1]
        k = 0
        for gi, size in enumerate(sizes):
            for i in range(size):
                copies = pattern(srcs[k], lands[k])
                for j, (s, d, peer, _) in enumerate(copies):
                    pltpu.make_async_remote_copy(
                        src_ref=s, dst_ref=d, send_sem=sems[2 * gi].at[i * len(copies) + j],
                        recv_sem=sems[2 * gi + 1].at[i * len(copies) + j], device_id=peer, device_id_type=MESH).start()
                k += 1
        token[...] = jnp.zeros_like(token)

    per_array = {_gather_pattern: N_PEER, _alltoall_pattern: N_PEER, _chipcast_pattern: 4, _forward_pattern: 3,
                 _sibling_pattern: 4, _chip_pattern: 3}[pattern]
    sem_shapes = [pltpu.SemaphoreType.DMA((size * per_array,)) for size in sizes for _ in range(2)]
    res = pl.pallas_call(
        body, name=name,
        in_specs=[HBM_SPEC] * len(bufs) + [pl.BlockSpec(memory_space=pl.ANY)] * len(order),
        out_specs=[SEM_SPEC] * len(sem_shapes) + [HBM_SPEC] * len(bufs) + [pl.BlockSpec(memory_space=pltpu.VMEM)],
        out_shape=sem_shapes + [pltpu.HBM(s.shape, s.dtype) for s in bufs] + [jax.ShapeDtypeStruct((8, DH), F32)],
        input_output_aliases={i: len(sem_shapes) + i for i in range(len(bufs))},
        compiler_params=pltpu.CompilerParams(has_side_effects=EFFECT),
    )(*[_hbm(s) for s in bufs], *order)
    nsem = len(sem_shapes)
    thru = list(res[nsem:nsem + len(bufs)])
    out, k = [], 0
    for gi, size in enumerate(sizes):
        srcs_out = [] if inplace else thru[k:k + size]
        lands_out = thru[k:k + size] if inplace else thru[n + k:n + k + size]
        out.append((res[2 * gi], res[2 * gi + 1], srcs_out, lands_out))
        k += size
    return out, res[-1]


def _split_wait(name, group, pattern, after):
    send_sems, recv_sems, arrays, lands = group
    n = len(lands)
    inplace = not arrays
    bufs = list(arrays) + list(lands)
    fill_own = pattern in (_gather_pattern, _chipcast_pattern, _alltoall_pattern)
    own_of = (lambda a: a.at[_slot_of(_place())]) if pattern is _alltoall_pattern else (lambda a: a)
    own_shape = (lambda a: a.shape[1:]) if pattern is _alltoall_pattern else (lambda a: a.shape)

    def body(*refs):
        srcs, lands_ = refs[:n], (refs[:n] if inplace else refs[n:2 * n])
        s_sems, r_sems = refs[len(bufs)], refs[len(bufs) + 1]
        own = []
        if fill_own:
            stage, local_sems = refs[-n - 1:-1], refs[-1]
            load = [pltpu.make_async_copy(own_of(srcs[i]), stage[i], local_sems.at[i]) for i in range(n)]
            for cp in load:
                cp.start()
            for cp in load:
                cp.wait()
            own = [pltpu.make_async_copy(stage[i], lands_[i].at[_slot_of(_place())], local_sems.at[i])
                   for i in range(n)]
            for cp in own:
                cp.start()
        for i in range(n):
            copies = pattern(srcs[i], lands_[i])
            for j, (s, _, peer, filled) in enumerate(copies):
                cp = pltpu.make_async_remote_copy(
                    src_ref=s, dst_ref=filled, send_sem=s_sems.at[i * len(copies) + j],
                    recv_sem=r_sems.at[i * len(copies) + j], device_id=peer, device_id_type=MESH)
                cp.wait_send()
                cp.wait_recv()
        for cp in own:
            cp.wait()

    res = pl.pallas_call(
        body, name=name,
        in_specs=[HBM_SPEC] * len(bufs) + [SEM_SPEC, SEM_SPEC, pl.BlockSpec(memory_space=pl.ANY)],
        out_specs=[HBM_SPEC] * len(bufs),
        out_shape=[pltpu.HBM(s.shape, s.dtype) for s in bufs],
        scratch_shapes=([pltpu.VMEM(own_shape(a), a.dtype) for a in arrays] + [pltpu.SemaphoreType.DMA((n,))])
        if fill_own else [],
        input_output_aliases={i: i for i in range(len(bufs))},
        compiler_params=pltpu.CompilerParams(has_side_effects=EFFECT),
    )(*bufs, send_sems, recv_sems, after)
    return ([], list(res)) if inplace else (list(res[:n]), list(res[n:]))


def _pair_sum(name, grads, recv, core):
    _, _, r, c = grads.shape
    tr = r if r * c <= (1 << 20) or r % 16 else _div_tile(r, 512, 16)

    def body(core_ref, g_ref, r_ref, o_ref):
        o_ref[...] = (g_ref[0].astype(F32) + r_ref[...].astype(F32)).astype(o_ref.dtype)

    return pl.pallas_call(
        body, name=name,
        grid_spec=pltpu.PrefetchScalarGridSpec(
            num_scalar_prefetch=1, grid=(4, r // tr),
            in_specs=[pl.BlockSpec((1, 1, tr, c), lambda s, i, cr: (s, cr[0], i, 0)),
                      pl.BlockSpec((1, tr, c), lambda s, i, cr: (s, i, 0))],
            out_specs=pl.BlockSpec((1, tr, c), lambda s, i, cr: (s, i, 0))),
        out_shape=jax.ShapeDtypeStruct(recv.shape, recv.dtype),
        compiler_params=_cparams(("parallel", "parallel")),
    )(core, grads, recv)


def _adamw_math(w, g, m, v):
    m = ADAM_B1 * m + (1.0 - ADAM_B1) * g
    v = ADAM_B2 * v + (1.0 - ADAM_B2) * (g * g)
    m_hat = m / (1.0 - ADAM_B1 ** ADAM_STEP)
    v_hat = v / (1.0 - ADAM_B2 ** ADAM_STEP)
    delta = -ADAM_LR * (m_hat / (jnp.sqrt(v_hat) + ADAM_EPS) + ADAM_WD * w)
    return delta, m, v


SMALL_LEAVES = (
    ("mix_pre_norm", 0, 1, 0, D, (1, D)), ("mix_post_norm", 1, 1, 0, D, (1, D)),
    ("ffn_pre_norm", 2, 1, 0, D, (1, D)), ("ffn_post_norm", 3, 1, 0, D, (1, D)),
    ("meta_tokens", 8, N_META, 0, DH, (N_META, DH)),
    ("conv_qkv", 24, KQ, 0, 3 * GW // N_DEV, (1, KQ, 3 * GW // N_DEV)),
    ("conv_sc", 32, KS, 0, SW // N_DEV, (1, KS, SW // N_DEV)),
    ("a_log", 40, 1, 0, HEADS, (1, HEADS)), ("dt_bias", 40, 1, DH, HEADS, (1, HEADS)),
    ("gdn_norm", 40, 1, 2 * DH, DH, (1, DH)),
)
SMALL_LOSS_AT = (40, 3 * DH)


def _adamw_small(slabs, late, w, m, v):
    nslot = slabs.shape[0]
    nleaf = len(SMALL_LEAVES)

    def body(p_ref, l_ref, w_ref, m_ref, v_ref, *refs):
        outs, loss_ref, scr = refs[:4 * nleaf], refs[4 * nleaf], refs[4 * nleaf + 1]
        g = p_ref[0]
        gain = l_ref[0, pl.ds(0, 1), :]
        meta = l_ref[0, pl.ds(8, N_META), pl.ds(0, DH)]
        for s in range(1, nslot):
            g = g + p_ref[s]
            gain = gain + l_ref[s, pl.ds(0, 1), :]
            meta = meta + l_ref[s, pl.ds(8, N_META), pl.ds(0, DH)]
        scr[0] = g
        scr[0, pl.ds(0, 1), pl.ds(0, D)] += gain
        scr[0, pl.ds(8, N_META), pl.ds(0, DH)] += meta
        g = scr[0]
        d, mn, vn = _adamw_math(w_ref[...], g, m_ref[...], v_ref[...])
        for kind, val in enumerate((g, d, mn, vn)):
            scr[kind] = val
        for kind in range(4):
            for li, (_, r0, nr, c0, nc, shape) in enumerate(SMALL_LEAVES):
                val = scr[kind, pl.ds(r0, nr), pl.ds(c0, nc)]
                o = outs[kind * nleaf + li]
                if len(shape) == 3:
                    o[0] = val
                else:
                    o[...] = val
        loss_ref[...] = scr[0, pl.ds(SMALL_LOSS_AT[0], 1), pl.ds(SMALL_LOSS_AT[1], 1)]

    whole = lambda a: pl.BlockSpec(a.shape, lambda: (0,) * a.ndim)
    out_shapes = [jax.ShapeDtypeStruct(shape, F32) for _ in range(4) for (*_, shape) in SMALL_LEAVES]
    out_shapes.append(jax.ShapeDtypeStruct((1, 1), F32))
    res = pl.pallas_call(
        body, name="adamw_small",
        in_specs=[whole(slabs), whole(late), whole(w), whole(m), whole(v)],
        out_specs=[pl.BlockSpec(s.shape, lambda n=len(s.shape): (0,) * n) for s in out_shapes],
        out_shape=out_shapes,
        scratch_shapes=[pltpu.VMEM((4,) + w.shape, F32)],
        compiler_params=pltpu.CompilerParams(vmem_limit_bytes=VMEM_LIMIT),
    )(slabs, late, w, m, v)
    kinds = [{nm: res[kind * nleaf + li] for li, (nm, *_) in enumerate(SMALL_LEAVES)} for kind in range(4)]
    return kinds, res[-1]


def _adamw_reduced(name, parts, chip, recv, w, m, v, after=None):
    _, r, c = parts.shape
    tr = r if r * c <= (1 << 20) or r % 16 else _div_tile(r, 512, 16)
    order = [] if after is None else [after]

    def body(chip_ref, p_ref, r_ref, w_ref, m_ref, v_ref, *rest):
        g_out, d_out, m_out, v_out = rest[len(order):]
        g = p_ref[0].astype(F32)
        for s in range(3):
            g = g + r_ref[s].astype(F32)
        d, mn, vn = _adamw_math(w_ref[...], g, m_ref[...], v_ref[...])
        g_out[...] = g
        d_out[...] = d
        m_out[...] = mn
        v_out[...] = vn

    blk = pl.BlockSpec((tr, c), lambda i, ch: (i, 0))
    return pl.pallas_call(
        body, name=name,
        grid_spec=pltpu.PrefetchScalarGridSpec(
            num_scalar_prefetch=1, grid=(r // tr,),
            in_specs=[pl.BlockSpec((1, tr, c), lambda i, ch: (ch[0], i, 0)),
                      pl.BlockSpec((3, tr, c), lambda i, ch: (0, i, 0)), blk, blk, blk]
            + [pl.BlockSpec((8, DH), lambda i, ch: (0, 0))] * len(order),
            out_specs=[blk] * 4),
        out_shape=[jax.ShapeDtypeStruct((r, c), F32)] * 4,
        compiler_params=_cparams(("parallel",)),
    )(chip, parts, recv, w, m, v, *order)


SMALL_ROWS, SMALL_COLS = 48, 3 * GW


def _pack_small(gains4, meta, conv_qkv, conv_sc, a_log, dt_bias, gdn_norm, extra=None):
    def tile(a):
        return jnp.pad(a, ((0, -a.shape[0] % 8), (0, SMALL_COLS - a.shape[1])))
    scal = jnp.concatenate([jnp.pad(a_log, ((0, 0), (0, DH - HEADS))), jnp.pad(dt_bias, ((0, 0), (0, DH - HEADS))),
                            gdn_norm] + ([] if extra is None else [extra]), axis=1)
    gains = sum(jnp.pad(g, ((i, 7 - i), (0, SMALL_COLS - g.shape[1]))) for i, g in enumerate(gains4))
    parts = [gains, tile(meta), tile(conv_qkv), tile(conv_sc), tile(scal)]
    slab = jnp.concatenate(parts, axis=0)
    assert slab.shape == (SMALL_ROWS, SMALL_COLS)
    return slab


def kernel(x, meta_tokens, mix_pre_norm, mix_post_norm, ffn_pre_norm, ffn_post_norm, w_in, conv_qkv, a_log, dt_bias, gdn_norm, conv_sc, w_out, w_gate, w_up, w_down, loss_target, m_meta_tokens, m_mix_pre_norm, m_mix_post_norm, m_ffn_pre_norm, m_ffn_post_norm, m_w_in, m_conv_qkv, m_a_log, m_dt_bias, m_gdn_norm, m_conv_sc, m_w_out, m_w_gate, m_w_up, m_w_down, v_meta_tokens, v_mix_pre_norm, v_mix_post_norm, v_ffn_pre_norm, v_ffn_post_norm, v_w_in, v_conv_qkv, v_a_log, v_dt_bias, v_gdn_norm, v_conv_sc, v_w_out, v_w_gate, v_w_up, v_w_down):
    bsz, seq, _ = x.shape
    lp = -(-(N_META + seq) // CH) * CH
    padf = lp - N_META - seq
    first_real = padf + N_META
    rows = bsz * lp
    nchunk = lp // CH
    tr = _div_tile(lp, 1056)
    tm = _div_tile(rows, 1056)
    tm_big = _div_tile(rows, 2112)
    cx, cy, cc = _place()

    small_shard = jnp.concatenate([
        jnp.pad(meta_tokens, ((0, 0), (0, 256 - DH))),
        jnp.pad(conv_qkv[0], ((0, 0), (0, 256 - 3 * GW // N_DEV))),
        jnp.pad(conv_sc[0], ((0, 5), (0, 256 - SW // N_DEV)))], axis=0)
    g_in, g_small = _all_gather("gather_w_in", [w_in[0].T.astype(BF16), small_shard])
    my_rest = [w_out[0].astype(BF16), w_gate[0].T.astype(BF16), w_up[0].T.astype(BF16), w_down[0].astype(BF16)]
    (grp_rest,), token = _split_start("gather_rest_start", [my_rest], _chipcast_pattern, lambda s: (N_DEV,) + s,
                                      after=g_in)
    after_start = token[0:1, 0:1]

    def cols_full(g):
        return jnp.transpose(g, (1, 0, 2)).reshape(g.shape[1], -1)

    win_t = g_in.reshape(IN_W, D)
    win_t = jnp.concatenate([win_t[:BA0 + 2 * HEADS], jnp.zeros((BA_W - 2 * HEADS, D), BF16),
                             win_t[BA0 + 2 * HEADS:]], axis=0)
    meta_full = cols_full(g_small[:, 0:16, :DH])
    convq_full = cols_full(g_small[:, 16:20, :3 * GW // N_DEV])
    convs_full = cols_full(g_small[:, 20:23, :SW // N_DEV])
    alog_v = jnp.pad(a_log, ((0, 0), (HEADS, DH - 2 * HEADS)))
    dtb_v = jnp.pad(dt_bias, ((0, 0), (HEADS, DH - 2 * HEADS)))

    head_rows = jnp.concatenate([jnp.zeros((padf, D), F32), meta_full], axis=0)
    h0, u1 = _pre_norm(x, head_rows, mix_pre_norm + after_start, bsz, lp, tr)
    (proj,) = _mm("proj_in", [(u1, win_t)], "nt", tm_big, 768, D, out_dtypes=(BF16,))
    (ba,) = _mm("proj_gate_logits", [(u1, win_t[BA0:SX0])], "nt", tm, BA_W, D)
    qkv = _gdn_prep(proj, convq_full, bsz, lp, padf)
    gb, gcs = _gates(ba, alog_v, dtb_v, bsz, lp, padf)

    def lane_bcast(cols):
        t = jnp.transpose(cols.reshape(bsz, lp, HEADS), (0, 2, 1)).reshape(bsz * HEADS, lp, 1)
        return jnp.broadcast_to(t, (bsz * HEADS, lp, DH))

    bb = lane_bcast(gb[:, 0:HEADS])
    gcb = lane_bcast(gcs[:, HEADS:2 * HEADS])
    gct = jnp.transpose(gcs[:, HEADS:2 * HEADS].reshape(bsz, nchunk, CH, HEADS), (0, 3, 1, 2)).reshape(
        bsz * HEADS, nchunk, 1, CH)
    u_loc, l1, l2, l3, w_t, k_dec, tinv_all, g_last = _gdn_local_fwd(qkv, gcb, bb, gct, bsz, lp)
    o, v_new, states = _gdn_scan_fwd(u_loc, l1, l2, g_last, bsz, lp)
    o, v_new = o.reshape(rows, GW), v_new.reshape(rows, GW)
    _, l_rest = _split_wait("gather_rest_arrive", grp_rest, _chipcast_pattern, o)
    (grp_out, grp_ffn, grp_down), token = _split_start(
        "gather_rest_forward", [l_rest[0:1], l_rest[1:3], l_rest[3:4]], _forward_pattern, None)
    mix_g, mix_s = _mix_heads(proj, o, gdn_norm + token[0:1, :], convs_full, bsz, lp, padf)
    mixin = jnp.concatenate([mix_g, mix_s], axis=1)
    _, (l_out,) = _split_wait("gather_w_out_wait", grp_out, _forward_pattern, mixin)
    wout = l_out.reshape(D, D)
    (mix,) = _mm("proj_out", [(mixin, wout)], "nn", tm, D, D, out_dtypes=(BF16,))
    h1, u2 = _mid_norms(h0, mix, mix_post_norm, ffn_pre_norm, bsz, lp, tr)
    _, (l_gate, l_up) = _split_wait("gather_ffn_wait", grp_ffn, _forward_pattern, u2)
    wgate_t, wup_t = l_gate.reshape(FF, D), l_up.reshape(FF, D)
    gate, up, act = _ffn_up(u2, wgate_t, wup_t, tm, 1408)
    _, (l_down,) = _split_wait("gather_down_wait", grp_down, _forward_pattern, act)
    wdown = l_down.reshape(FF, D)
    (ffn,) = _mm("ffn_down", [(act, wdown)], "nn", tm, D, 1408, out_dtypes=(BF16,))

    dy, dffn, d_g4, loss_part = _loss_head(h1, ffn, loss_target, ffn_post_norm, bsz, lp, tr, first_real)
    (dwdown,) = _mm("dw_down", [(act, dffn)], "tn", 1408, D, tm, out_dtypes=(BF16,))

    def swiglu_bwd(da, g, u):
        g, u = g.astype(F32), u.astype(F32)
        return da * u * _dsilu(g), da * _silu(g)

    dgate, dup = _mm("d_act", [(dffn, wdown)], "nt", tm, 1408, D, out_dtypes=(BF16, BF16),
                     epilogue=swiglu_bwd, extras=(gate, up))
    (dwgate_t,) = _mm("dw_gate", [(dgate, u2)], "tn", 1408, D, tm, out_dtypes=(BF16,))
    (dwup_t,) = _mm("dw_up", [(dup, u2)], "tn", 1408, D, tm, out_dtypes=(BF16,))
    (du2,) = _mm("d_u2", [(dgate, wgate_t), (dup, wup_t)], "nn", tm, D, 1408, out_dtypes=(BF16,))
    dh1, dmix, d_g3, d_g2 = _mid_norms_bwd(h1, mix, du2, dy, ffn_pre_norm, mix_post_norm, bsz, lp, tr)
    (dwout,) = _mm("dw_out", [(mixin, dmix)], "tn", D, D, tm, out_dtypes=(BF16,))

    def row_blocks(g):
        return g.reshape(4, 2, -1, g.shape[1])

    core = jnp.reshape(cc, (1,)).astype(jnp.int32)
    chip = jnp.reshape(2 * cx + cy, (1,)).astype(jnp.int32)

    def reduce_start(tag, grads):
        (grp,), tok = _split_start("rs_sibling_start_" + tag, [grads], _sibling_pattern, lambda s: s[:1] + s[2:])
        return grp, tok

    def reduce_mid(tag, grp, nms, after):
        mine, lands = _split_wait("rs_sibling_wait_" + tag, grp, _sibling_pattern, after)
        parts = [_pair_sum("pair_sum_" + nm, g, r, core) for nm, g, r in zip(nms, mine, lands)]
        (grp2,), tok = _split_start("rs_chip_start_" + tag, [parts], _chip_pattern, lambda s: (3,) + s[1:])
        return grp2, tok

    def reduce_end(tag, grp2, after):
        return _split_wait("rs_chip_wait_" + tag, grp2, _chip_pattern, after)

    names_ffn = ["w_down", "w_gate", "w_up", "w_out"]
    rs_ffn, tok = reduce_start("ffn", [row_blocks(dwdown), row_blocks(dwgate_t), row_blocks(dwup_t), row_blocks(dwout)])
    (dmixin,) = _mm("d_mixin", [(dmix, wout)], "nt", tm, D, D, out_dtypes=(BF16,), after=tok)
    d_o, dz, dsx, dsb, dsc, d_gn, d_convs = _mix_heads_bwd(proj, o, gdn_norm, convs_full, dmixin, bsz, lp, padf)
    rs_ffn, tok = reduce_mid("ffn", rs_ffn, names_ffn, d_o)
    dv_new, dstates = _gdn_scan_bwd(d_o, l3, w_t, k_dec, g_last + tok[0, 0], bsz, lp)
    dq, dk, dv, dg_b, dbeta_b = _gdn_local_bwd(qkv, gcb, bb, gct, tinv_all, states, dstates, v_new,
                                               dv_new.reshape(rows, GW), d_o, bsz, lp)
    dpqkv, d_convq = _gdn_prep_bwd(proj, convq_full, dq, dk, dv, bsz, lp, padf)
    dba, d_scal = _gates_bwd(ba, alog_v, dtb_v, dbeta_b, dg_b, gb, bsz, lp, padf)
    dproj = jnp.concatenate([dpqkv, dz, dba, dsx, dsb, dsc], axis=1)

    d_alog = d_scal[0:1, HEADS:2 * HEADS]
    d_dtb = d_scal[1:2, HEADS:2 * HEADS]
    cq_w, cs_w = 3 * GW // N_DEV, SW // N_DEV
    slab = jnp.stack([
        _pack_small([jnp.zeros_like(d_g2), d_g2, d_g3, d_g4], jnp.zeros((N_META, DH), F32),
                    d_convq[:, p * cq_w:(p + 1) * cq_w], d_convs[:, p * cs_w:(p + 1) * cs_w], d_alog, d_dtb, d_gn,
                    loss_part) for p in range(N_DEV)])
    (grp_small,), tok = _split_start("small_grads_start", [[slab]], _alltoall_pattern, lambda s: s)
    (dwin_t,) = _mm("dw_in", [(dproj, u1)], "tn", 1280, D, tm, out_dtypes=(BF16,), after=tok)
    parts_ffn, recv_ffn = reduce_end("ffn", rs_ffn, dwin_t)
    dwin_t = jnp.concatenate([dwin_t[:BA0 + 2 * HEADS], dwin_t[SX0:]], axis=0)
    rs_in, tok = reduce_start("in", [row_blocks(dwin_t)])

    big_w = {"w_in": (w_in, m_w_in, v_w_in), "w_out": (w_out, m_w_out, v_w_out), "w_gate": (w_gate, m_w_gate, v_w_gate),
             "w_up": (w_up, m_w_up, v_w_up), "w_down": (w_down, m_w_down, v_w_down)}

    def update(nm, parts, recv, after=None):
        w, m, v = big_w[nm]
        if nm in ("w_out", "w_down"):
            return _adamw_reduced("adamw_" + nm, parts, chip, recv, w[0], m[0], v[0], after)
        res = _adamw_reduced("adamw_" + nm, parts, chip, recv, w[0].T, m[0].T, v[0].T, after)
        return [t.T for t in res]

    big_out = {nm: update(nm, p, r, tok) for nm, p, r in zip(names_ffn, parts_ffn, recv_ffn)}
    ffn_done = sum(big_out[nm][1][:1, :1] for nm in names_ffn)
    rs_in, tok = reduce_mid("in", rs_in, ["w_in"], ffn_done)
    (du1,) = _mm("d_u1", [(dproj, win_t)], "nn", tm_big, D, 768, out_dtypes=(BF16,), after=tok)
    grad_x, d_g1, d_meta = _pre_norm_bwd(h0, du1, dh1, mix_pre_norm, bsz, lp, seq, padf)

    late = jnp.stack([
        jnp.concatenate([jnp.pad(d_g1, ((0, 7), (0, 0))), jnp.pad(d_meta[:, p * DH:(p + 1) * DH], ((0, 0), (0, D - DH)))],
                        axis=0) for p in range(N_DEV)])
    (grp_late,), _ = _split_start("late_grads_start", [[late]], _alltoall_pattern, lambda s: s)
    _, (slabs_local,) = _split_wait("small_grads_wait", grp_small, _alltoall_pattern, grad_x)
    _, (late_local,) = _split_wait("late_grads_wait", grp_late, _alltoall_pattern, slabs_local)
    w_small = _pack_small([mix_pre_norm, mix_post_norm, ffn_pre_norm, ffn_post_norm], meta_tokens, conv_qkv[0],
                          conv_sc[0], a_log, dt_bias, gdn_norm)
    m_small = _pack_small([m_mix_pre_norm, m_mix_post_norm, m_ffn_pre_norm, m_ffn_post_norm], m_meta_tokens,
                          m_conv_qkv[0], m_conv_sc[0], m_a_log, m_dt_bias, m_gdn_norm)
    v_small = _pack_small([v_mix_pre_norm, v_mix_post_norm, v_ffn_pre_norm, v_ffn_post_norm], v_meta_tokens,
                          v_conv_qkv[0], v_conv_sc[0], v_a_log, v_dt_bias, v_gdn_norm)
    small_out, loss = _adamw_small(slabs_local, late_local, w_small, m_small, v_small)
    parts_in, recv_in = reduce_end("in", rs_in, small_out[1]["gdn_norm"])
    big_out["w_in"] = update("w_in", parts_in[0], recv_in[0])
    names = ["w_in", "w_out", "w_gate", "w_up", "w_down"]

    order = ["meta_tokens", "mix_pre_norm", "mix_post_norm", "ffn_pre_norm", "ffn_post_norm", "w_in", "conv_qkv",
             "a_log", "dt_bias", "gdn_norm", "conv_sc", "w_out", "w_gate", "w_up", "w_down"]
    outs = [loss[0, 0], grad_x]
    for kind in range(4):
        leaves = dict(small_out[kind])
        for nm in names:
            leaves[nm] = big_out[nm][kind][None]
        outs += [leaves[nm] for nm in order]
    return tuple(outs)
```
